```python
import jax, jax.numpy as jnp
from jax import lax
import numpy as np

D_MODEL = 1024
BATCH = 8
SEQ = 8192
DEPTH = 1

HEAD_DIM = 64
ATTN_HEADS_PER_GROUP = 8
DILATED_GROUPS = ((128, 1), (512, 4), (2048, 16))
N_DIL = len(DILATED_GROUPS)
ATTN_WIDTH = ATTN_HEADS_PER_GROUP * HEAD_DIM
ROPE_DIM = HEAD_DIM // 4
ROPE_THETA = 500000.0
BLK = 128
SGU_CHUNK = 128
SGU_GROUPS = 8
SGU_WIDTH = D_MODEL // 2
SGU_GROUP_DIM = SGU_WIDTH // SGU_GROUPS
D_FF = -(-8 * D_MODEL // (3 * 256)) * 256
QKV_COLS = 3 * N_DIL * ATTN_WIDTH
IN_COLS = QKV_COLS + 2 * SGU_WIDTH + 2 * D_MODEL
EPS = 1e-6

kernel_name = "hybrid_dilated_attn_gmlp_gated_block"


def rmsnorm(x, g):
    xf = x.astype(jnp.float32)
    y = xf * lax.rsqrt(jnp.mean(xf * xf, axis=-1, keepdims=True) + EPS)
    return (y * g.astype(jnp.float32)).astype(x.dtype)


def layernorm(x, g, b):
    xf = x.astype(jnp.float32)
    mu = jnp.mean(xf, axis=-1, keepdims=True)
    xc = xf - mu
    y = xc * lax.rsqrt(jnp.mean(xc * xc, axis=-1, keepdims=True) + EPS)
    return (y * g.astype(jnp.float32) + b.astype(jnp.float32)).astype(x.dtype)


def partial_rope(t, positions):
    half = ROPE_DIM // 2
    inv_freq = ROPE_THETA ** (-jnp.arange(0, ROPE_DIM, 2, dtype=jnp.float32) / ROPE_DIM)
    ang = positions.astype(jnp.float32)[..., None] * inv_freq
    cos = jnp.cos(ang)[:, :, None, :]
    sin = jnp.sin(ang)[:, :, None, :]
    tf = t.astype(jnp.float32)
    x1, x2 = tf[..., :half], tf[..., half:ROPE_DIM]
    rot = jnp.concatenate([x1 * cos - x2 * sin, x2 * cos + x1 * sin, tf[..., ROPE_DIM:]], axis=-1)
    return rot.astype(t.dtype)


def dilated_attention(q, k, v, window, dilation):
    B, S, H, Dh = q.shape
    span = window // dilation
    L = -(-S // dilation)
    L_pad = -(-L // BLK) * BLK
    S_pad = L_pad * dilation
    nb = L_pad // BLK
    pad = ((0, 0), (0, S_pad - S), (0, 0), (0, 0))

    def strided(t):
        t = jnp.pad(t, pad).reshape(B, L_pad, dilation, H, Dh).transpose(0, 2, 1, 3, 4)
        return t.reshape(B, dilation, nb, BLK, H, Dh)

    def with_prev(t):
        prev = jnp.pad(t, ((0, 0), (0, 0), (1, 0), (0, 0), (0, 0), (0, 0)))[:, :, :-1]
        return jnp.concatenate([prev, t], axis=3)

    qs = strided(q * (Dh ** -0.5))
    kb = with_prev(strided(k))
    vb = with_prev(strided(v))
    s = jnp.einsum('brnqhd,brnkhd->brnhqk', qs, kb, preferred_element_type=jnp.float32)

    i = jnp.arange(BLK)[:, None]
    j = jnp.arange(2 * BLK)[None, :]
    diff = BLK + i - j
    band = (diff >= 0) & (diff <= span)
    key_exists = (jnp.arange(nb)[:, None, None] > 0) | (j >= BLK)[None]
    mask = band[None] & key_exists
    s = jnp.where(mask[None, None, :, None], s, -jnp.inf)

    m = jnp.max(s, axis=-1, keepdims=True)
    p = jnp.exp(s - m)
    den = jnp.sum(p, axis=-1)
    lse = m[..., 0] + jnp.log(den)
    o = jnp.einsum('brnhqk,brnkhd->brnqhd', p, vb.astype(jnp.float32))
    o = o / jnp.swapaxes(den, -1, -2)[..., None]

    o = o.reshape(B, dilation, L_pad, H, Dh).transpose(0, 2, 1, 3, 4).reshape(B, S_pad, H, Dh)[:, :S]
    lse = jnp.swapaxes(lse, -1, -2).reshape(B, dilation, L_pad, H).transpose(0, 2, 1, 3)
    lse = lse.reshape(B, S_pad, H)[:, :S]
    return o.astype(q.dtype), lse


def spatial_gating(uv, ln_g, ln_b, w_s, b_s):
    B, S, _ = uv.shape
    z = jax.nn.gelu(uv, approximate=False)
    u, v = z[..., :SGU_WIDTH], z[..., SGU_WIDTH:]
    v = layernorm(v, ln_g, ln_b)
    vc = v.reshape(B, S // SGU_CHUNK, SGU_CHUNK, SGU_GROUPS, SGU_GROUP_DIM)
    causal = jnp.tril(jnp.ones((SGU_CHUNK, SGU_CHUNK), dtype=bool))
    w_causal = jnp.where(causal[None], w_s, jnp.zeros_like(w_s))
    mixed = jnp.einsum('gts,bnsgc->bntgc', w_causal, vc)
    mixed = mixed + jnp.transpose(b_s)[None, None, :, :, None]
    return u * mixed.reshape(B, S, SGU_WIDTH)


def _fwd_setup_inputs(seed: int = 0) -> dict:
    key = jax.random.key(seed)
    ks = jax.random.split(key, 18)
    f32 = jnp.float32
    x = jax.random.normal(ks[0], (BATCH, SEQ, D_MODEL), f32)
    offset = jax.random.randint(ks[1], (BATCH, 1), 0, 4096, dtype=jnp.int32)
    positions = offset + jnp.arange(SEQ, dtype=jnp.int32)[None, :]
    nrm = lambda k, shape, fan_in: jax.random.normal(k, shape, f32) * (fan_in ** -0.5)
    return {
        "x": x,
        "positions": positions,
        "norm1_g": 1.0 + 0.02 * jax.random.normal(ks[2], (DEPTH, D_MODEL), f32),
        "w_in": nrm(ks[3], (DEPTH, D_MODEL, IN_COLS), D_MODEL),
        "sgu_ln_g": 1.0 + 0.02 * jax.random.normal(ks[4], (DEPTH, SGU_WIDTH), f32),
        "sgu_ln_b": 0.02 * jax.random.normal(ks[5], (DEPTH, SGU_WIDTH), f32),
        "w_spatial": nrm(ks[6], (DEPTH, SGU_GROUPS, SGU_CHUNK, SGU_CHUNK), SGU_CHUNK),
        "b_spatial": 1.0 + 0.1 * jax.random.normal(ks[7], (DEPTH, SGU_GROUPS, SGU_CHUNK), f32),
        "w_proj_attn": nrm(ks[8], (DEPTH, ATTN_WIDTH, D_MODEL), ATTN_WIDTH),
        "w_proj_sgu": nrm(ks[9], (DEPTH, SGU_WIDTH, D_MODEL), SGU_WIDTH),
        "w_out": nrm(ks[10], (DEPTH, D_MODEL, D_MODEL), D_MODEL),
        "norm2_g": 1.0 + 0.02 * jax.random.normal(ks[11], (DEPTH, D_MODEL), f32),
        "w_ffn_gate": nrm(ks[12], (DEPTH, D_MODEL, D_FF), D_MODEL),
        "w_ffn_up": nrm(ks[13], (DEPTH, D_MODEL, D_FF), D_MODEL),
        "w_ffn_down": nrm(ks[14], (DEPTH, D_FF, D_MODEL), D_FF),
        "final_g": 1.0 + 0.02 * jax.random.normal(ks[15], (D_MODEL,), f32),
    }


def _fwd_reference(x, positions, norm1_g, w_in, sgu_ln_g, sgu_ln_b, w_spatial, b_spatial,
              w_proj_attn, w_proj_sgu, w_out, norm2_g, w_ffn_gate, w_ffn_up, w_ffn_down,
              final_g):
    B, S, _ = x.shape
    for l in range(DEPTH):
        h = rmsnorm(x, norm1_g[l])
        proj = h @ w_in[l]
        qkv = proj[..., :QKV_COLS].reshape(B, S, 3, N_DIL, ATTN_HEADS_PER_GROUP, HEAD_DIM)
        uv = proj[..., QKV_COLS:QKV_COLS + 2 * SGU_WIDTH]
        gate_a = jax.nn.sigmoid(proj[..., QKV_COLS + 2 * SGU_WIDTH:QKV_COLS + 2 * SGU_WIDTH + D_MODEL])
        gate_b = jax.nn.sigmoid(proj[..., QKV_COLS + 2 * SGU_WIDTH + D_MODEL:])

        outs, lses = [], []
        for g, (window, dilation) in enumerate(DILATED_GROUPS):
            q = partial_rope(qkv[:, :, 0, g], positions)
            k = partial_rope(qkv[:, :, 1, g], positions)
            o, lse = dilated_attention(q, k, qkv[:, :, 2, g], window, dilation)
            outs.append(o)
            lses.append(lse)
        alpha = jax.nn.softmax(jnp.stack(lses, axis=0), axis=0)
        attn = jnp.sum(alpha[..., None].astype(x.dtype) * jnp.stack(outs, axis=0), axis=0)
        attn = attn.reshape(B, S, ATTN_WIDTH)

        sgu = spatial_gating(uv, sgu_ln_g[l], sgu_ln_b[l], w_spatial[l], b_spatial[l])

        merged = gate_a * (attn @ w_proj_attn[l]) + gate_b * (sgu @ w_proj_sgu[l])
        x = x + merged @ w_out[l]

        h2 = rmsnorm(x, norm2_g[l])
        ff = jax.nn.silu(h2 @ w_ffn_gate[l]) * (h2 @ w_ffn_up[l])
        x = x + ff @ w_ffn_down[l]
    return rmsnorm(x, final_g)


import jax as _jax
import jax.numpy as _jnp

TWIN_FORMAT = 'train_step'
FWD_PARAMS = ['x', 'positions', 'norm1_g', 'w_in', 'sgu_ln_g', 'sgu_ln_b', 'w_spatial', 'b_spatial', 'w_proj_attn', 'w_proj_sgu', 'w_out', 'norm2_g', 'w_ffn_gate', 'w_ffn_up', 'w_ffn_down', 'final_g']
TWIN_WEIGHTS = ['norm1_g', 'w_in', 'sgu_ln_g', 'sgu_ln_b', 'w_spatial', 'b_spatial', 'w_proj_attn', 'w_proj_sgu', 'w_out', 'norm2_g', 'w_ffn_gate', 'w_ffn_up', 'w_ffn_down', 'final_g']
TWIN_DIFF_INPUT = 'x'
TWIN_INPUTS = ['x', 'positions', 'norm1_g', 'w_in', 'sgu_ln_g', 'sgu_ln_b', 'w_spatial', 'b_spatial', 'w_proj_attn', 'w_proj_sgu', 'w_out', 'norm2_g', 'w_ffn_gate', 'w_ffn_up', 'w_ffn_down', 'final_g', 'loss_target', 'm_norm1_g', 'm_w_in', 'm_sgu_ln_g', 'm_sgu_ln_b', 'm_w_spatial', 'm_b_spatial', 'm_w_proj_attn', 'm_w_proj_sgu', 'm_w_out', 'm_norm2_g', 'm_w_ffn_gate', 'm_w_ffn_up', 'm_w_ffn_down', 'm_final_g', 'v_norm1_g', 'v_w_in', 'v_sgu_ln_g', 'v_sgu_ln_b', 'v_w_spatial', 'v_b_spatial', 'v_w_proj_attn', 'v_w_proj_sgu', 'v_w_out', 'v_norm2_g', 'v_w_ffn_gate', 'v_w_ffn_up', 'v_w_ffn_down', 'v_final_g']
TWIN_OUTPUTS = ['loss', 'grad_x', 'grad_norm1_g', 'grad_w_in', 'grad_sgu_ln_g', 'grad_sgu_ln_b', 'grad_w_spatial', 'grad_b_spatial', 'grad_w_proj_attn', 'grad_w_proj_sgu', 'grad_w_out', 'grad_norm2_g', 'grad_w_ffn_gate', 'grad_w_ffn_up', 'grad_w_ffn_down', 'grad_final_g', 'delta_norm1_g', 'delta_w_in', 'delta_sgu_ln_g', 'delta_sgu_ln_b', 'delta_w_spatial', 'delta_b_spatial', 'delta_w_proj_attn', 'delta_w_proj_sgu', 'delta_w_out', 'delta_norm2_g', 'delta_w_ffn_gate', 'delta_w_ffn_up', 'delta_w_ffn_down', 'delta_final_g', 'new_m_norm1_g', 'new_m_w_in', 'new_m_sgu_ln_g', 'new_m_sgu_ln_b', 'new_m_w_spatial', 'new_m_b_spatial', 'new_m_w_proj_attn', 'new_m_w_proj_sgu', 'new_m_w_out', 'new_m_norm2_g', 'new_m_w_ffn_gate', 'new_m_w_ffn_up', 'new_m_w_ffn_down', 'new_m_final_g', 'new_v_norm1_g', 'new_v_w_in', 'new_v_sgu_ln_g', 'new_v_sgu_ln_b', 'new_v_w_spatial', 'new_v_b_spatial', 'new_v_w_proj_attn', 'new_v_w_proj_sgu', 'new_v_w_out', 'new_v_norm2_g', 'new_v_w_ffn_gate', 'new_v_w_ffn_up', 'new_v_w_ffn_down', 'new_v_final_g']
TWIN_LEAF_KINDS = {'loss': 'loss', 'grad_x': 'grad_x', 'grad_norm1_g': 'grad_w', 'grad_w_in': 'grad_w', 'grad_sgu_ln_g': 'grad_w', 'grad_sgu_ln_b': 'grad_w', 'grad_w_spatial': 'grad_w', 'grad_b_spatial': 'grad_w', 'grad_w_proj_attn': 'grad_w', 'grad_w_proj_sgu': 'grad_w', 'grad_w_out': 'grad_w', 'grad_norm2_g': 'grad_w', 'grad_w_ffn_gate': 'grad_w', 'grad_w_ffn_up': 'grad_w', 'grad_w_ffn_down': 'grad_w', 'grad_final_g': 'grad_w', 'delta_norm1_g': 'delta_w', 'delta_w_in': 'delta_w', 'delta_sgu_ln_g': 'delta_w', 'delta_sgu_ln_b': 'delta_w', 'delta_w_spatial': 'delta_w', 'delta_b_spatial': 'delta_w', 'delta_w_proj_attn': 'delta_w', 'delta_w_proj_sgu': 'delta_w', 'delta_w_out': 'delta_w', 'delta_norm2_g': 'delta_w', 'delta_w_ffn_gate': 'delta_w', 'delta_w_ffn_up': 'delta_w', 'delta_w_ffn_down': 'delta_w', 'delta_final_g': 'delta_w', 'new_m_norm1_g': 'new_m', 'new_m_w_in': 'new_m', 'new_m_sgu_ln_g': 'new_m', 'new_m_sgu_ln_b': 'new_m', 'new_m_w_spatial': 'new_m', 'new_m_b_spatial': 'new_m', 'new_m_w_proj_attn': 'new_m', 'new_m_w_proj_sgu': 'new_m', 'new_m_w_out': 'new_m', 'new_m_norm2_g': 'new_m', 'new_m_w_ffn_gate': 'new_m', 'new_m_w_ffn_up': 'new_m', 'new_m_w_ffn_down': 'new_m', 'new_m_final_g': 'new_m', 'new_v_norm1_g': 'new_v', 'new_v_w_in': 'new_v', 'new_v_sgu_ln_g': 'new_v', 'new_v_sgu_ln_b': 'new_v', 'new_v_w_spatial': 'new_v', 'new_v_b_spatial': 'new_v', 'new_v_w_proj_attn': 'new_v', 'new_v_w_proj_sgu': 'new_v', 'new_v_w_out': 'new_v', 'new_v_norm2_g': 'new_v', 'new_v_w_ffn_gate': 'new_v', 'new_v_w_ffn_up': 'new_v', 'new_v_w_ffn_down': 'new_v', 'new_v_final_g': 'new_v'}


def _forward(args):
    return _fwd_reference(*[args[k] for k in FWD_PARAMS])


def _output_shape():
    def fwd():
        inp = _fwd_setup_inputs(0)
        return _fwd_reference(*[inp[k] for k in FWD_PARAMS])
    out = _jax.eval_shape(fwd)
    return out.shape, out.dtype

N_MICROBATCH = 1
ADAM_LR = 0.001
ADAM_B1 = 0.9
ADAM_B2 = 0.999
ADAM_EPS = 1e-08
ADAM_WD = 0.01
ADAM_STEP = 10
PER_EXAMPLE_BATCH_AXIS = {'x': 0, 'positions': 0, 'loss_target': 0}
SHARED_INPUTS = []
_WEIGHT_DTYPES = {'norm1_g': _jnp.float32, 'w_in': _jnp.float32, 'sgu_ln_g': _jnp.float32, 'sgu_ln_b': _jnp.float32, 'w_spatial': _jnp.float32, 'b_spatial': _jnp.float32, 'w_proj_attn': _jnp.float32, 'w_proj_sgu': _jnp.float32, 'w_out': _jnp.float32, 'norm2_g': _jnp.float32, 'w_ffn_gate': _jnp.float32, 'w_ffn_up': _jnp.float32, 'w_ffn_down': _jnp.float32, 'final_g': _jnp.float32}
MOMENT_SCALE = {'norm1_g': 1.491610e-01, 'w_in': 5.406257e-02, 'sgu_ln_g': 9.506598e-02, 'sgu_ln_b': 8.869512e-02, 'w_spatial': 6.335175e-02, 'b_spatial': 9.725221e-02, 'w_proj_attn': 2.615643e-02, 'w_proj_sgu': 1.134839e-01, 'w_out': 1.128862e-01, 'norm2_g': 1.884111e-01, 'w_ffn_gate': 7.507889e-02, 'w_ffn_up': 7.280616e-02, 'w_ffn_down': 1.208215e-01, 'final_g': 6.400416e+01}


def _to_microbatches(a, axis):
    t = _jnp.moveaxis(a, axis, 0)
    t = t.reshape((N_MICROBATCH, t.shape[0] // N_MICROBATCH) + t.shape[1:])
    return _jnp.moveaxis(t, 1, axis + 1)


def setup_inputs(seed: int = 0) -> dict:
    inp = _fwd_setup_inputs(seed)
    key = _jax.random.fold_in(_jax.random.key(seed), 7919)
    shape, _ = _output_shape()
    out = dict(inp)
    out["loss_target"] = _jax.random.normal(_jax.random.fold_in(key, 0), shape, _jnp.float32)
    for i, name in enumerate(TWIN_WEIGHTS):
        w = inp[name].astype(_jnp.float32)
        if MOMENT_SCALE is None:
            s = _jnp.sqrt(_jnp.mean(_jnp.square(w)) + 1e-30)
        else:
            s = MOMENT_SCALE[name]
        km, kv = _jax.random.split(_jax.random.fold_in(key, i + 1))
        out[name] = w
        out["m_" + name] = s * _jax.random.normal(km, w.shape, _jnp.float32)
        out["v_" + name] = (s * s) * _jax.random.uniform(kv, w.shape, _jnp.float32, 0.5, 1.5)
    if N_MICROBATCH > 1:
        for name, axis in PER_EXAMPLE_BATCH_AXIS.items():
            out[name] = _to_microbatches(out[name], axis)
    return {'x': out['x'], 'positions': out['positions'], 'norm1_g': out['norm1_g'], 'w_in': out['w_in'], 'sgu_ln_g': out['sgu_ln_g'], 'sgu_ln_b': out['sgu_ln_b'], 'w_spatial': out['w_spatial'], 'b_spatial': out['b_spatial'], 'w_proj_attn': out['w_proj_attn'], 'w_proj_sgu': out['w_proj_sgu'], 'w_out': out['w_out'], 'norm2_g': out['norm2_g'], 'w_ffn_gate': out['w_ffn_gate'], 'w_ffn_up': out['w_ffn_up'], 'w_ffn_down': out['w_ffn_down'], 'final_g': out['final_g'], 'loss_target': out['loss_target'], 'm_norm1_g': out['m_norm1_g'], 'm_w_in': out['m_w_in'], 'm_sgu_ln_g': out['m_sgu_ln_g'], 'm_sgu_ln_b': out['m_sgu_ln_b'], 'm_w_spatial': out['m_w_spatial'], 'm_b_spatial': out['m_b_spatial'], 'm_w_proj_attn': out['m_w_proj_attn'], 'm_w_proj_sgu': out['m_w_proj_sgu'], 'm_w_out': out['m_w_out'], 'm_norm2_g': out['m_norm2_g'], 'm_w_ffn_gate': out['m_w_ffn_gate'], 'm_w_ffn_up': out['m_w_ffn_up'], 'm_w_ffn_down': out['m_w_ffn_down'], 'm_final_g': out['m_final_g'], 'v_norm1_g': out['v_norm1_g'], 'v_w_in': out['v_w_in'], 'v_sgu_ln_g': out['v_sgu_ln_g'], 'v_sgu_ln_b': out['v_sgu_ln_b'], 'v_w_spatial': out['v_w_spatial'], 'v_b_spatial': out['v_b_spatial'], 'v_w_proj_attn': out['v_w_proj_attn'], 'v_w_proj_sgu': out['v_w_proj_sgu'], 'v_w_out': out['v_w_out'], 'v_norm2_g': out['v_norm2_g'], 'v_w_ffn_gate': out['v_w_ffn_gate'], 'v_w_ffn_up': out['v_w_ffn_up'], 'v_w_ffn_down': out['v_w_ffn_down'], 'v_final_g': out['v_final_g']}


def _loss(weights, diff, rest, loss_target):
    with _jax.named_scope("forward"):
        args = {**rest, TWIN_DIFF_INPUT: diff, **{k: w.astype(_WEIGHT_DTYPES[k]) for k, w in weights.items()}}
        y = _forward(args)
    with _jax.named_scope("loss_head"):
        err = _jnp.square(y.astype(_jnp.float32) - loss_target)
        return 0.5 * _jnp.sum(_jnp.mean(err, axis=-1)) if err.ndim else 0.5 * err


def _adamw(w, g, m, v):
    m = ADAM_B1 * m + (1.0 - ADAM_B1) * g
    v = ADAM_B2 * v + (1.0 - ADAM_B2) * _jnp.square(g)
    m_hat = m / (1.0 - ADAM_B1 ** ADAM_STEP)
    v_hat = v / (1.0 - ADAM_B2 ** ADAM_STEP)
    delta = -ADAM_LR * (m_hat / (_jnp.sqrt(v_hat) + ADAM_EPS) + ADAM_WD * w)
    return delta, m, v


def reference(x, positions, norm1_g, w_in, sgu_ln_g, sgu_ln_b, w_spatial, b_spatial, w_proj_attn, w_proj_sgu, w_out, norm2_g, w_ffn_gate, w_ffn_up, w_ffn_down, final_g, loss_target, m_norm1_g, m_w_in, m_sgu_ln_g, m_sgu_ln_b, m_w_spatial, m_b_spatial, m_w_proj_attn, m_w_proj_sgu, m_w_out, m_norm2_g, m_w_ffn_gate, m_w_ffn_up, m_w_ffn_down, m_final_g, v_norm1_g, v_w_in, v_sgu_ln_g, v_sgu_ln_b, v_w_spatial, v_b_spatial, v_w_proj_attn, v_w_proj_sgu, v_w_out, v_norm2_g, v_w_ffn_gate, v_w_ffn_up, v_w_ffn_down, v_final_g):
    given = dict(x=x, positions=positions, norm1_g=norm1_g, w_in=w_in, sgu_ln_g=sgu_ln_g, sgu_ln_b=sgu_ln_b, w_spatial=w_spatial, b_spatial=b_spatial, w_proj_attn=w_proj_attn, w_proj_sgu=w_proj_sgu, w_out=w_out, norm2_g=norm2_g, w_ffn_gate=w_ffn_gate, w_ffn_up=w_ffn_up, w_ffn_down=w_ffn_down, final_g=final_g, loss_target=loss_target, m_norm1_g=m_norm1_g, m_w_in=m_w_in, m_sgu_ln_g=m_sgu_ln_g, m_sgu_ln_b=m_sgu_ln_b, m_w_spatial=m_w_spatial, m_b_spatial=m_b_spatial, m_w_proj_attn=m_w_proj_attn, m_w_proj_sgu=m_w_proj_sgu, m_w_out=m_w_out, m_norm2_g=m_norm2_g, m_w_ffn_gate=m_w_ffn_gate, m_w_ffn_up=m_w_ffn_up, m_w_ffn_down=m_w_ffn_down, m_final_g=m_final_g, v_norm1_g=v_norm1_g, v_w_in=v_w_in, v_sgu_ln_g=v_sgu_ln_g, v_sgu_ln_b=v_sgu_ln_b, v_w_spatial=v_w_spatial, v_b_spatial=v_b_spatial, v_w_proj_attn=v_w_proj_attn, v_w_proj_sgu=v_w_proj_sgu, v_w_out=v_w_out, v_norm2_g=v_norm2_g, v_w_ffn_gate=v_w_ffn_gate, v_w_ffn_up=v_w_ffn_up, v_w_ffn_down=v_w_ffn_down, v_final_g=v_final_g)
    weights = {n: given[n] for n in TWIN_WEIGHTS}
    shared = {n: given[n] for n in SHARED_INPUTS}
    per_example = {n: given[n] for n in ['x', 'positions']}
    grad_fn = _jax.value_and_grad(_loss, argnums=(0, 1))

    def one_microbatch(ex, loss_target):
        ex = dict(ex)
        diff = ex.pop(TWIN_DIFF_INPUT)
        return grad_fn(weights, diff, {**shared, **ex}, loss_target)

    if N_MICROBATCH == 1:
        loss, (grad_w, grad_x) = one_microbatch(per_example, given["loss_target"])
    else:
        def body(carry, xs):
            loss_sum, grad_sum = carry
            l_k, (gw_k, gx_k) = one_microbatch(xs[0], xs[1])
            with _jax.named_scope("update"):
                return (loss_sum + l_k, _jax.tree.map(_jnp.add, grad_sum, gw_k)), gx_k

        init = (_jnp.zeros((), _jnp.float32), _jax.tree.map(_jnp.zeros_like, weights))
        (loss, grad_w), grad_x = _jax.lax.scan(body, init, (per_example, given["loss_target"]))
    with _jax.named_scope("update"):
        delta_w, new_m, new_v = {}, {}, {}
        for n in TWIN_WEIGHTS:
            delta_w[n], new_m[n], new_v[n] = _adamw(weights[n], grad_w[n], given["m_" + n], given["v_" + n])
    return (loss, grad_x, *[grad_w[n] for n in TWIN_WEIGHTS], *[delta_w[n] for n in TWIN_WEIGHTS],
            *[new_m[n] for n in TWIN_WEIGHTS], *[new_v[n] for n in TWIN_WEIGHTS])
```

```python
import functools

import numpy as np
import jax
import jax.numpy as jnp
from jax import lax
from jax.experimental import pallas as pl
from jax.experimental.pallas import tpu as pltpu

F32, BF16 = jnp.float32, jnp.bfloat16
MESH = pl.DeviceIdType.MESH

D_MODEL = 1024
HEAD_DIM = 64
ATTN_W = 512
DILATIONS = (1, 4, 16)
BLK = 128
ROPE_DIM = 16
ROPE_THETA = 500000.0
SGU_W = 512
SGU_CHUNK = 128
SGU_GROUPS = 8
D_FF = 2816
N_CHIPS = 4
FF_SHARD = D_FF // N_CHIPS
IN_COLS = 7680
QKV_COLS = 4608
EPS = 1e-6
NEG = -1e30
LANES = 128
VMEM_LIMIT = 52 * 1024 * 1024

ADAM_LR, ADAM_B1, ADAM_B2, ADAM_EPS, ADAM_WD, ADAM_STEP = 0.001, 0.9, 0.999, 1e-08, 0.01, 10

PERM = (11, 12, 13, 14, 9, 10, 0, 3, 6, 1, 4, 7, 2, 5, 8)
INV_PERM = tuple(int(i) for i in np.argsort(np.array(PERM)))


def _cparams(ngrid):
    return pltpu.CompilerParams(dimension_semantics=("arbitrary",) * ngrid, vmem_limit_bytes=VMEM_LIMIT)


def _full(shape):
    return pl.BlockSpec(shape, lambda *_: (0,) * len(shape))


NN = ((1,), (0,))
NT = ((1,), (1,))
TN = ((0,), (0,))


def _mm(name, grid, pairs, dims, acc_shape, epi, *, extras=(), outs=(), reds=(), aliases=None):
    nk = grid[-1]
    npair, nex, nout, nred = len(pairs), len(extras), len(outs), len(reds)

    def body(*refs):
        a_refs = refs[:npair]
        b_refs = refs[npair:2 * npair]
        p0 = 2 * npair
        e_refs = refs[p0:p0 + nex]
        o_refs = refs[p0 + nex:p0 + nex + nout]
        r_refs = refs[p0 + nex + nout:p0 + nex + nout + nred]
        ids = [pl.program_id(a) for a in range(len(grid))]
        k = ids[-1]
        if nred:
            first = ids[0] == 0
            for v in ids[1:]:
                first = first & (v == 0)

            @pl.when(first)
            def _():
                for r in r_refs:
                    r[...] = jnp.zeros(r.shape, r.dtype)

        part = None
        for a_ref, b_ref in zip(a_refs, b_refs):
            d = lax.dot_general(a_ref[...], b_ref[...], (dims, ((), ())), preferred_element_type=F32)
            part = d if part is None else part + d
        if nk == 1:
            epi(part, e_refs, o_refs, r_refs, ids)
        else:
            acc_ref = refs[-1]

            @pl.when(k == 0)
            def _():
                acc_ref[...] = part

            @pl.when(k > 0)
            def _():
                acc_ref[...] += part

            @pl.when(k == nk - 1)
            def _():
                epi(acc_ref[...], e_refs, o_refs, r_refs, ids)

    in_specs = [p[1] for p in pairs] + [p[3] for p in pairs] + [e[1] for e in extras]
    args = [p[0] for p in pairs] + [p[2] for p in pairs] + [e[0] for e in extras]
    out_shape = [jax.ShapeDtypeStruct(o[0], o[1]) for o in outs] + [jax.ShapeDtypeStruct(r, F32) for r in reds]
    out_specs = [o[2] for o in outs] + [_full(r) for r in reds]
    scratch = [pltpu.VMEM(acc_shape, F32)] if nk > 1 else []
    return pl.pallas_call(
        body, grid=grid, in_specs=in_specs, out_specs=out_specs, out_shape=out_shape, scratch_shapes=scratch,
        input_output_aliases=aliases or {}, compiler_params=_cparams(len(grid)), name=name)(*args)


def _lane_tile(t, width):
    n = width // LANES
    return t if n == 1 else jnp.concatenate([t] * n, axis=1)


def _rope(v, cos_w, sin_w):
    w = v.shape[1]
    lane = lax.broadcasted_iota(jnp.int32, v.shape, 1)
    partner = jnp.where((lane % HEAD_DIM) < ROPE_DIM // 2, pltpu.roll(v, w - ROPE_DIM // 2, axis=1),
                        pltpu.roll(v, ROPE_DIM // 2, axis=1))
    return v * cos_w + partner * sin_w


def _sigmoid(v):
    return 1.0 / (1.0 + jnp.exp(-v))


def _rms_stats(v):
    r = lax.rsqrt(jnp.mean(v * v, axis=-1, keepdims=True) + EPS)
    return v * r, r


def _rms_bwd(dy, xhat, r, g):
    dxh = dy * g
    return r * (dxh - xhat * jnp.mean(dxh * xhat, axis=-1, keepdims=True))


def _head_sum_matrix():
    idx = np.arange(ATTN_W) // HEAD_DIM
    return jnp.asarray((idx[:, None] == idx[None, :]).astype(np.float32), dtype=BF16)


def _group_sum(v, e):
    hi = v.astype(BF16)
    lo = (v - hi.astype(F32)).astype(BF16)
    return jnp.dot(hi, e, preferred_element_type=F32) + jnp.dot(lo, e, preferred_element_type=F32)


def _rope_consts():
    lane = np.arange(LANES) % HEAD_DIM
    fi = lane % (ROPE_DIM // 2)
    invf = np.where(lane < ROPE_DIM, ROPE_THETA ** (-(2.0 * fi) / ROPE_DIM), 0.0)
    sgn = np.where(lane < ROPE_DIM // 2, -1.0, np.where(lane < ROPE_DIM, 1.0, 0.0))
    return (jnp.asarray(invf.astype(np.float32)).reshape(1, LANES), jnp.asarray(sgn.astype(np.float32)).reshape(1, LANES))


def _rope_table(pos_col):
    t = pos_col.shape[0]
    tm = min(t, 1024)
    invf, sgn = _rope_consts()

    def body(p_ref, f_ref, s_ref, c_out, s_out):
        ang = p_ref[...].astype(F32) * f_ref[...]
        c_out[...] = jnp.cos(ang)
        s_out[...] = jnp.sin(ang) * s_ref[...]

    return pl.pallas_call(
        body, grid=(t // tm,),
        in_specs=[pl.BlockSpec((tm, 1), lambda i: (i, 0)), _full((1, LANES)), _full((1, LANES))],
        out_specs=[pl.BlockSpec((tm, LANES), lambda i: (i, 0))] * 2,
        out_shape=[jax.ShapeDtypeStruct((t, LANES), F32)] * 2,
        compiler_params=_cparams(1), name="rope_table")(pos_col, invf, sgn)


def _norm_fwd(x, g):
    t = x.shape[0]
    tm = min(t, 512)

    def body(x_ref, g_ref, h_ref):
        xhat, _ = _rms_stats(x_ref[...])
        h_ref[...] = (xhat * g_ref[...]).astype(BF16)

    return pl.pallas_call(
        body, grid=(t // tm,),
        in_specs=[pl.BlockSpec((tm, D_MODEL), lambda i: (i, 0)), _full((1, D_MODEL))],
        out_specs=pl.BlockSpec((tm, D_MODEL), lambda i: (i, 0)),
        out_shape=jax.ShapeDtypeStruct((t, D_MODEL), BF16),
        compiler_params=_cparams(1), name="norm1_fwd")(x, g)


def _in_proj(h, w_p, cos_t, sin_t):
    t = h.shape[0]
    tm, tn = min(t, 512), 512

    def epi_plain(acc, e, o, r, ids):
        o[0][...] = acc

    gu = _mm("in_proj_gates_uv", (t // tm, 3072 // tn, 1),
             [(h, pl.BlockSpec((tm, D_MODEL), lambda i, j, k: (i, 0)), w_p, pl.BlockSpec((D_MODEL, tn), lambda i, j, k: (0, j)))],
             NN, (tm, tn), epi_plain,
             outs=[((t, 3072), F32, pl.BlockSpec((tm, tn), lambda i, j, k: (i, j)))])[0]

    def epi_qkv(acc, e, o, r, ids):
        role = ids[1] % 3
        cos_w = _lane_tile(e[0][...], tn)
        sin_w = _lane_tile(e[1][...], tn)
        roped = _rope(acc, cos_w, sin_w) * jnp.where(role == 0, HEAD_DIM ** -0.5, 1.0)
        o[0][...] = jnp.where(role == 2, acc, roped).astype(BF16)

    qkv = _mm("in_proj_qkv", (t // tm, QKV_COLS // tn, 1),
              [(h, pl.BlockSpec((tm, D_MODEL), lambda i, j, k: (i, 0)), w_p,
                pl.BlockSpec((D_MODEL, tn), lambda i, j, k: (0, j + 3072 // tn)))],
              NN, (tm, tn), epi_qkv,
              extras=[(cos_t, pl.BlockSpec((tm, LANES), lambda i, j, k: (i, 0))),
                      (sin_t, pl.BlockSpec((tm, LANES), lambda i, j, k: (i, 0)))],
              outs=[((t, QKV_COLS), BF16, pl.BlockSpec((tm, tn), lambda i, j, k: (i, j)))])[0]
    return gu, qkv


def _attn_masks(n):
    row = lax.broadcasted_iota(jnp.int32, (BLK, 2 * BLK), 0)
    col = lax.broadcasted_iota(jnp.int32, (BLK, 2 * BLK), 1)
    diff = BLK + row - col
    valid = (diff >= 0) & (diff <= BLK) & ((col >= BLK) | (n > 0))
    upper = lax.broadcasted_iota(jnp.int32, (BLK, LANES), 1) >= HEAD_DIM
    return valid, upper


def _attn_fwd(qkv, g, dil):
    t = qkv.shape[0]
    length = t // dil
    nb = length // BLK
    view = qkv.reshape(length, dil * QKV_COLS)

    def cb(r, part):
        return (r * 3 + g) * 3 + part

    def body(q_ref, kc_ref, kp_ref, vc_ref, vp_ref, o_ref, l_ref):
        n = pl.program_id(1)
        valid, upper = _attn_masks(n)
        for p in range(ATTN_W // LANES):
            sl = slice(p * LANES, (p + 1) * LANES)
            q2 = q_ref[:, sl]
            k2 = jnp.concatenate([kp_ref[:, sl], kc_ref[:, sl]], axis=0)
            v2 = jnp.concatenate([vp_ref[:, sl], vc_ref[:, sl]], axis=0)
            outs, lses = [], []
            for hh in (0, 1):
                sel = upper if hh else jnp.logical_not(upper)
                qm = jnp.where(sel, q2, jnp.zeros_like(q2))
                s = lax.dot_general(qm, k2, (NT, ((), ())), preferred_element_type=F32)
                s = jnp.where(valid, s, NEG)
                m = jnp.max(s, axis=1, keepdims=True)
                pe = jnp.exp(s - m)
                den = jnp.sum(pe, axis=1, keepdims=True)
                outs.append(jnp.dot(pe.astype(BF16), v2, preferred_element_type=F32) / den)
                lses.append(jnp.broadcast_to(m + jnp.log(den), (BLK, LANES)))
            o_ref[:, sl] = jnp.where(upper, outs[1], outs[0])
            l_ref[:, sl] = jnp.where(upper, lses[1], lses[0])

    cur = lambda part: pl.BlockSpec((BLK, ATTN_W), lambda r, n: (n, cb(r, part)))
    prev = lambda part: pl.BlockSpec((BLK, ATTN_W), lambda r, n: (jnp.maximum(n - 1, 0), cb(r, part)))
    out_spec = pl.BlockSpec((BLK, ATTN_W), lambda r, n: (n, r))
    o, lse = pl.pallas_call(
        body, grid=(dil, nb),
        in_specs=[cur(0), cur(1), prev(1), cur(2), prev(2)],
        out_specs=[out_spec, out_spec],
        out_shape=[jax.ShapeDtypeStruct((length, dil * ATTN_W), F32)] * 2,
        compiler_params=_cparams(2), name=f"attn_fwd_g{g}")(view, view, view, view, view)
    return o.reshape(t, ATTN_W), lse.reshape(t, ATTN_W)


def _alphas(l_refs):
    l0, l1, l2 = (r[...] for r in l_refs)
    m = jnp.maximum(jnp.maximum(l0, l1), l2)
    e0, e1, e2 = jnp.exp(l0 - m), jnp.exp(l1 - m), jnp.exp(l2 - m)
    inv = 1.0 / (e0 + e1 + e2)
    return e0 * inv, e1 * inv, e2 * inv


def _combine_fwd(os_, ls_):
    t = os_[0].shape[0]
    tm = min(t, 512)

    def body(o0, o1, o2, l0, l1, l2, a_ref):
        a0, a1, a2 = _alphas((l0, l1, l2))
        a_ref[...] = (a0 * o0[...] + a1 * o1[...] + a2 * o2[...]).astype(BF16)

    spec = pl.BlockSpec((tm, ATTN_W), lambda i: (i, 0))
    return pl.pallas_call(
        body, grid=(t // tm,), in_specs=[spec] * 6, out_specs=spec,
        out_shape=jax.ShapeDtypeStruct((t, ATTN_W), BF16),
        compiler_params=_cparams(1), name="combine_fwd")(*os_, *ls_)


def _combine_bwd(dattn, os_, ls_):
    t = dattn.shape[0]
    tm = min(t, 512)
    e = _head_sum_matrix()

    def body(d_ref, o0, o1, o2, l0, l1, l2, e_ref, do0, do1, do2, c0, c1, c2):
        alphas = _alphas((l0, l1, l2))
        d = d_ref[...]
        attn = alphas[0] * o0[...] + alphas[1] * o1[...] + alphas[2] * o2[...]
        s = _group_sum(d * attn, e_ref[...])
        for a, do_ref, c_ref in zip(alphas, (do0, do1, do2), (c0, c1, c2)):
            do_ref[...] = (a * d).astype(BF16)
            c_ref[...] = -a * s

    spec = pl.BlockSpec((tm, ATTN_W), lambda i: (i, 0))
    outs = pl.pallas_call(
        body, grid=(t // tm,), in_specs=[spec] * 7 + [_full((ATTN_W, ATTN_W))], out_specs=[spec] * 6,
        out_shape=[jax.ShapeDtypeStruct((t, ATTN_W), BF16)] * 3 + [jax.ShapeDtypeStruct((t, ATTN_W), F32)] * 3,
        compiler_params=_cparams(1), name="combine_bwd")(dattn, *os_, *ls_, e)
    return outs[:3], outs[3:]


def _attn_bwd(dproj, qkv, do, cc, lse, cos_t, sin_t, g, dil):
    t = qkv.shape[0]
    length = t // dil
    nb = length // BLK
    qkv_v = qkv.reshape(length, dil * QKV_COLS)
    dproj_v = dproj.reshape(length, dil * IN_COLS)
    do_v, cc_v, lse_v = (a.reshape(length, dil * ATTN_W) for a in (do, cc, lse))
    cos_v, sin_v = (a.reshape(length, dil * LANES) for a in (cos_t, sin_t))
    scale = HEAD_DIM ** -0.5

    def cb(r, part):
        return (r * 3 + g) * 3 + part

    def body(dp_in, q_ref, kc_ref, kp_ref, vc_ref, vp_ref, do_ref, c_ref, l_ref, cosc, sinc, cosp, sinp,
             out_ref, dq_s, dk_s, dv_s):
        n = pl.program_id(1)
        valid, upper = _attn_masks(n)
        lower = jnp.logical_not(upper)

        @pl.when(n < nb)
        def _():
            cos_c, sin_c = _lane_tile(cosc[...], ATTN_W), _lane_tile(sinc[...], ATTN_W)
            cos_p, sin_p = _lane_tile(cosp[...], ATTN_W), _lane_tile(sinp[...], ATTN_W)
            dq_parts, dkp_parts, dkc_parts, dvp_parts, dvc_parts = [], [], [], [], []
            for p in range(ATTN_W // LANES):
                sl = slice(p * LANES, (p + 1) * LANES)
                q2 = q_ref[:, sl]
                k2 = jnp.concatenate([kp_ref[:, sl], kc_ref[:, sl]], axis=0)
                v2 = jnp.concatenate([vp_ref[:, sl], vc_ref[:, sl]], axis=0)
                do2 = do_ref[:, sl]
                l2 = l_ref[:, sl]
                c2 = c_ref[:, sl]
                dq2 = jnp.zeros((BLK, LANES), F32)
                dk2 = jnp.zeros((2 * BLK, LANES), F32)
                dv2 = jnp.zeros((2 * BLK, LANES), F32)
                for hh in (0, 1):
                    sel = upper if hh else lower
                    qm = jnp.where(sel, q2, jnp.zeros_like(q2))
                    dom = jnp.where(sel, do2, jnp.zeros_like(do2))
                    l_col = l2[:, hh * HEAD_DIM:hh * HEAD_DIM + 1]
                    c_col = c2[:, hh * HEAD_DIM:hh * HEAD_DIM + 1]
                    s = lax.dot_general(qm, k2, (NT, ((), ())), preferred_element_type=F32)
                    pe = jnp.where(valid, jnp.exp(jnp.where(valid, s, NEG) - l_col), 0.0)
                    dpv = lax.dot_general(dom, v2, (NT, ((), ())), preferred_element_type=F32)
                    ds = (pe * (dpv + c_col)).astype(BF16)
                    pb = pe.astype(BF16)
                    dq2 = dq2 + jnp.where(sel, jnp.dot(ds, k2, preferred_element_type=F32), 0.0)
                    dk2 = dk2 + lax.dot_general(ds, qm, (TN, ((), ())), preferred_element_type=F32)
                    dv2 = dv2 + lax.dot_general(pb, dom, (TN, ((), ())), preferred_element_type=F32)
                dq_parts.append(dq2)
                dkp_parts.append(dk2[:BLK])
                dkc_parts.append(dk2[BLK:])
                dvp_parts.append(dv2[:BLK])
                dvc_parts.append(dv2[BLK:])
            dq = _rope(jnp.concatenate(dq_parts, axis=1) * scale, cos_c, -sin_c)
            dkc = _rope(jnp.concatenate(dkc_parts, axis=1), cos_c, -sin_c)
            dkp = _rope(jnp.concatenate(dkp_parts, axis=1), cos_p, -sin_p)
            dvp = jnp.concatenate(dvp_parts, axis=1)
            dvc = jnp.concatenate(dvc_parts, axis=1)

            @pl.when(n > 0)
            def _():
                out_ref[:, 0:ATTN_W] = dq_s[...].astype(BF16)
                out_ref[:, ATTN_W:2 * ATTN_W] = (dk_s[...] + dkp).astype(BF16)
                out_ref[:, 2 * ATTN_W:3 * ATTN_W] = (dv_s[...] + dvp).astype(BF16)

            dq_s[...] = dq
            dk_s[...] = dkc
            dv_s[...] = dvc

        @pl.when(n == nb)
        def _():
            out_ref[:, 0:ATTN_W] = dq_s[...].astype(BF16)
            out_ref[:, ATTN_W:2 * ATTN_W] = dk_s[...].astype(BF16)
            out_ref[:, 2 * ATTN_W:3 * ATTN_W] = dv_s[...].astype(BF16)

    nc = lambda n: jnp.minimum(n, nb - 1)
    npv = lambda n: jnp.maximum(jnp.minimum(n, nb - 1) - 1, 0)
    cur = lambda part: pl.BlockSpec((BLK, ATTN_W), lambda r, n: (nc(n), cb(r, part)))
    prev = lambda part: pl.BlockSpec((BLK, ATTN_W), lambda r, n: (npv(n), cb(r, part)))
    row = pl.BlockSpec((BLK, ATTN_W), lambda r, n: (nc(n), r))
    tab_c = pl.BlockSpec((BLK, LANES), lambda r, n: (nc(n), r))
    tab_p = pl.BlockSpec((BLK, LANES), lambda r, n: (npv(n), r))
    out_spec = pl.BlockSpec((BLK, 3 * ATTN_W), lambda r, n: (jnp.maximum(n - 1, 0), r * 5 + 2 + g))
    out = pl.pallas_call(
        body, grid=(dil, nb + 1),
        in_specs=[pl.BlockSpec(memory_space=pl.ANY), cur(0), cur(1), prev(1), cur(2), prev(2), row, row, row,
                  tab_c, tab_c, tab_p, tab_p],
        out_specs=out_spec,
        out_shape=jax.ShapeDtypeStruct(dproj_v.shape, BF16),
        scratch_shapes=[pltpu.VMEM((BLK, ATTN_W), F32)] * 3,
        input_output_aliases={0: 0},
        compiler_params=_cparams(2), name=f"attn_bwd_g{g}")(
            dproj_v, qkv_v, qkv_v, qkv_v, qkv_v, qkv_v, do_v, cc_v, lse_v, cos_v, sin_v, cos_v, sin_v)
    return out.reshape(t, IN_COLS)


SQRT_HALF = 0.7071067811865476
INV_SQRT_2PI = 0.3989422804014327


def _sgu_core(uv, g, b, w_ref, bias):
    cdf = 0.5 * (1.0 + lax.erf(uv * SQRT_HALF))
    z = uv * cdf
    u, v = z[:, :SGU_W], z[:, SGU_W:]
    mu = jnp.mean(v, axis=1, keepdims=True)
    xc = v - mu
    rs = lax.rsqrt(jnp.mean(xc * xc, axis=1, keepdims=True) + EPS)
    xhat = xc * rs
    vn = xhat * g + b
    row = lax.broadcasted_iota(jnp.int32, (SGU_CHUNK, SGU_CHUNK), 0)
    col = lax.broadcasted_iota(jnp.int32, (SGU_CHUNK, SGU_CHUNK), 1)
    tril = row >= col
    upper = lax.broadcasted_iota(jnp.int32, (SGU_CHUNK, LANES), 1) >= SGU_W // SGU_GROUPS
    ws, vlo, vhi, mixed = [], [], [], []
    for pr in range(SGU_W // LANES):
        sl = slice(pr * LANES, (pr + 1) * LANES)
        w0 = jnp.where(tril, w_ref[2 * pr], 0.0).astype(BF16)
        w1 = jnp.where(tril, w_ref[2 * pr + 1], 0.0).astype(BF16)
        vn2 = vn[:, sl]
        lo = jnp.where(upper, 0.0, vn2).astype(BF16)
        hi = jnp.where(upper, vn2, 0.0).astype(BF16)
        mixed.append(jnp.dot(w0, lo, preferred_element_type=F32) + jnp.dot(w1, hi, preferred_element_type=F32)
                     + bias[:, sl])
        ws.append((w0, w1))
        vlo.append(lo)
        vhi.append(hi)
    return cdf, u, xhat, rs, jnp.concatenate(mixed, axis=1), ws, vlo, vhi, tril, upper


def _sgu_fwd(gu, ln_g, ln_b, w_s, bias_exp):
    t = gu.shape[0]

    def body(uv_ref, g_ref, b_ref, w_ref, bias_ref, o_ref):
        _, u, _, _, mixed, *_ = _sgu_core(uv_ref[...], g_ref[...], b_ref[...], w_ref, bias_ref[...])
        o_ref[...] = (u * mixed).astype(BF16)

    return pl.pallas_call(
        body, grid=(t // SGU_CHUNK,),
        in_specs=[pl.BlockSpec((SGU_CHUNK, 2 * SGU_W), lambda n: (n, 2)), _full((1, SGU_W)), _full((1, SGU_W)),
                  _full((SGU_GROUPS, SGU_CHUNK, SGU_CHUNK)), _full((SGU_CHUNK, SGU_W))],
        out_specs=pl.BlockSpec((SGU_CHUNK, SGU_W), lambda n: (n, 0)),
        out_shape=jax.ShapeDtypeStruct((t, SGU_W), BF16),
        compiler_params=_cparams(1), name="sgu_fwd")(gu, ln_g, ln_b, w_s, bias_exp)


def _sgu_bwd(dproj, gu, dsgu, ln_g, ln_b, w_s, bias_exp):
    t = gu.shape[0]
    nchunks = t // SGU_CHUNK
    e = _head_sum_matrix()

    def body(dp_in, uv_ref, ds_ref, g_ref, b_ref, w_ref, bias_ref, e_ref, out_ref, dw_ref, dbias_ref, dg_ref, db_ref):
        n = pl.program_id(0)

        @pl.when(n == 0)
        def _():
            dw_ref[...] = jnp.zeros(dw_ref.shape, F32)
            dbias_ref[...] = jnp.zeros(dbias_ref.shape, F32)
            dg_ref[...] = jnp.zeros(dg_ref.shape, F32)
            db_ref[...] = jnp.zeros(db_ref.shape, F32)

        uv = uv_ref[...]
        g = g_ref[...]
        cdf, u, xhat, rs, mixed, ws, vlo, vhi, tril, upper = _sgu_core(uv, g, b_ref[...], w_ref, bias_ref[...])
        dsg = ds_ref[...]
        du = dsg * mixed
        dmixed = dsg * u
        dbias_ref[...] += dmixed
        dvn = []
        for pr in range(SGU_W // LANES):
            sl = slice(pr * LANES, (pr + 1) * LANES)
            dm2 = dmixed[:, sl]
            dlo = jnp.where(upper, 0.0, dm2).astype(BF16)
            dhi = jnp.where(upper, dm2, 0.0).astype(BF16)
            w0, w1 = ws[pr]
            dvn.append(lax.dot_general(w0, dlo, (TN, ((), ())), preferred_element_type=F32)
                       + lax.dot_general(w1, dhi, (TN, ((), ())), preferred_element_type=F32))
            dw0 = lax.dot_general(dlo, vlo[pr], (NT, ((), ())), preferred_element_type=F32)
            dw1 = lax.dot_general(dhi, vhi[pr], (NT, ((), ())), preferred_element_type=F32)
            dw_ref[2 * pr] += jnp.where(tril, dw0, 0.0)
            dw_ref[2 * pr + 1] += jnp.where(tril, dw1, 0.0)
        dvn = jnp.concatenate(dvn, axis=1)
        dg_ref[...] += jnp.sum(dvn * xhat, axis=0, keepdims=True)
        db_ref[...] += jnp.sum(dvn, axis=0, keepdims=True)
        dxh = dvn * g
        dv = rs * (dxh - jnp.mean(dxh, axis=1, keepdims=True) - xhat * jnp.mean(dxh * xhat, axis=1, keepdims=True))
        dz = jnp.concatenate([du, dv], axis=1)
        dgelu = cdf + uv * (INV_SQRT_2PI * jnp.exp(-0.5 * uv * uv))
        out_ref[...] = (dz * dgelu).astype(BF16)

        @pl.when(n == nchunks - 1)
        def _():
            dbias_ref[...] = _group_sum(dbias_ref[...], e_ref[...])

    outs = pl.pallas_call(
        body, grid=(nchunks,),
        in_specs=[pl.BlockSpec(memory_space=pl.ANY), pl.BlockSpec((SGU_CHUNK, 2 * SGU_W), lambda n: (n, 2)),
                  pl.BlockSpec((SGU_CHUNK, SGU_W), lambda n: (n, 0)), _full((1, SGU_W)), _full((1, SGU_W)),
                  _full((SGU_GROUPS, SGU_CHUNK, SGU_CHUNK)), _full((SGU_CHUNK, SGU_W)), _full((ATTN_W, ATTN_W))],
        out_specs=[pl.BlockSpec((SGU_CHUNK, 2 * SGU_W), lambda n: (n, 2)), _full((SGU_GROUPS, SGU_CHUNK, SGU_CHUNK)),
                   _full((SGU_CHUNK, SGU_W)), _full((1, SGU_W)), _full((1, SGU_W))],
        out_shape=[jax.ShapeDtypeStruct(dproj.shape, BF16), jax.ShapeDtypeStruct((SGU_GROUPS, SGU_CHUNK, SGU_CHUNK), F32),
                   jax.ShapeDtypeStruct((SGU_CHUNK, SGU_W), F32), jax.ShapeDtypeStruct((1, SGU_W), F32),
                   jax.ShapeDtypeStruct((1, SGU_W), F32)],
        input_output_aliases={0: 0},
        compiler_params=_cparams(1), name="sgu_bwd")(dproj, gu, dsgu, ln_g, ln_b, w_s, bias_exp, e)
    return outs


def _merge_fwd(attn, sgu, gu, x, w_pa, w_ps, w_out, g2):
    t = x.shape[0]
    tm = min(t, 256)

    def body(a_ref, s_ref, ga_ref, gb_ref, x_ref, wpa, wps, wo, g_ref, pa_ref, ps_ref, m_ref, x1_ref, h2_ref):
        pa = jnp.dot(a_ref[...], wpa[...], preferred_element_type=F32)
        ps = jnp.dot(s_ref[...], wps[...], preferred_element_type=F32)
        merged = (_sigmoid(ga_ref[...]) * pa + _sigmoid(gb_ref[...]) * ps).astype(BF16)
        x1 = x_ref[...] + jnp.dot(merged, wo[...], preferred_element_type=F32)
        xhat, _ = _rms_stats(x1)
        pa_ref[...] = pa
        ps_ref[...] = ps
        m_ref[...] = merged
        x1_ref[...] = x1
        h2_ref[...] = (xhat * g_ref[...]).astype(BF16)

    half = pl.BlockSpec((tm, ATTN_W), lambda i: (i, 0))
    full = pl.BlockSpec((tm, D_MODEL), lambda i: (i, 0))
    return pl.pallas_call(
        body, grid=(t // tm,),
        in_specs=[half, half, pl.BlockSpec((tm, D_MODEL), lambda i: (i, 0)), pl.BlockSpec((tm, D_MODEL), lambda i: (i, 1)),
                  full, _full((ATTN_W, D_MODEL)), _full((SGU_W, D_MODEL)), _full((D_MODEL, D_MODEL)), _full((1, D_MODEL))],
        out_specs=[full] * 5,
        out_shape=[jax.ShapeDtypeStruct((t, D_MODEL), F32), jax.ShapeDtypeStruct((t, D_MODEL), F32),
                   jax.ShapeDtypeStruct((t, D_MODEL), BF16), jax.ShapeDtypeStruct((t, D_MODEL), F32),
                   jax.ShapeDtypeStruct((t, D_MODEL), BF16)],
        compiler_params=_cparams(1), name="merge_fwd")(attn, sgu, gu, gu, x, w_pa, w_ps, w_out, g2)


def _merge_bwd(dproj, dx1b, gu, pa, ps, w_pa, w_ps, w_out):
    t = dx1b.shape[0]
    tm = min(t, 256)

    def body(dp_in, d_ref, ga_ref, gb_ref, pa_ref, ps_ref, wpa, wps, wo, out_ref, dpa_ref, dps_ref, da_ref, dsg_ref):
        dm = lax.dot_general(d_ref[...], wo[...], (NT, ((), ())), preferred_element_type=F32)
        sa, sb = _sigmoid(ga_ref[...]), _sigmoid(gb_ref[...])
        dpa = (dm * sa).astype(BF16)
        dps = (dm * sb).astype(BF16)
        out_ref[:, 0:D_MODEL] = (dm * pa_ref[...] * sa * (1.0 - sa)).astype(BF16)
        out_ref[:, D_MODEL:2 * D_MODEL] = (dm * ps_ref[...] * sb * (1.0 - sb)).astype(BF16)
        dpa_ref[...] = dpa
        dps_ref[...] = dps
        da_ref[...] = lax.dot_general(dpa, wpa[...], (NT, ((), ())), preferred_element_type=F32)
        dsg_ref[...] = lax.dot_general(dps, wps[...], (NT, ((), ())), preferred_element_type=F32)

    half = pl.BlockSpec((tm, ATTN_W), lambda i: (i, 0))
    full = pl.BlockSpec((tm, D_MODEL), lambda i: (i, 0))
    return pl.pallas_call(
        body, grid=(t // tm,),
        in_specs=[pl.BlockSpec(memory_space=pl.ANY), full, pl.BlockSpec((tm, D_MODEL), lambda i: (i, 0)),
                  pl.BlockSpec((tm, D_MODEL), lambda i: (i, 1)), full, full,
                  _full((ATTN_W, D_MODEL)), _full((SGU_W, D_MODEL)), _full((D_MODEL, D_MODEL))],
        out_specs=[pl.BlockSpec((tm, 2 * D_MODEL), lambda i: (i, 0)), full, full, half, half],
        out_shape=[jax.ShapeDtypeStruct(dproj.shape, BF16), jax.ShapeDtypeStruct((t, D_MODEL), BF16),
                   jax.ShapeDtypeStruct((t, D_MODEL), BF16), jax.ShapeDtypeStruct((t, ATTN_W), F32),
                   jax.ShapeDtypeStruct((t, SGU_W), F32)],
        input_output_aliases={0: 0},
        compiler_params=_cparams(1), name="merge_bwd")(dproj, dx1b, gu, gu, pa, ps, w_pa, w_ps, w_out)


def _ffn_fwd(h2, w_g, w_u):
    t = h2.shape[0]
    tm = min(t, 512)
    a_spec = pl.BlockSpec((tm, D_MODEL), lambda i, j, k: (i, 0))
    w_spec = pl.BlockSpec((None, D_MODEL, FF_SHARD), lambda i, j, k: (j, 0, 0))
    o_spec = pl.BlockSpec((None, tm, FF_SHARD), lambda i, j, k: (j, i, 0))
    shp = (N_CHIPS, t, FF_SHARD)

    def epi_a(acc, e, o, r, ids):
        o[0][...] = acc

    a = _mm("ffn_gate", (t // tm, N_CHIPS, 1), [(h2, a_spec, w_g, w_spec)], NN, (tm, FF_SHARD), epi_a,
            outs=[(shp, F32, o_spec)])[0]

    def epi_b(acc, e, o, r, ids):
        av = e[0][...]
        o[0][...] = acc
        o[1][...] = (av * _sigmoid(av) * acc).astype(BF16)

    b, ff = _mm("ffn_up", (t // tm, N_CHIPS, 1), [(h2, a_spec, w_u, w_spec)], NN, (tm, FF_SHARD), epi_b,
                extras=[(a, o_spec)], outs=[(shp, F32, o_spec), (shp, BF16, o_spec)])
    return a, b, ff


def _ffn_down_loss(ff, w_d, x1, tgt, gf):
    t = x1.shape[0]
    tm = min(t, 512)

    def epi(acc, e, o, r, ids):
        x2 = e[0][...] + acc
        g = e[2][...]
        xhat, rr = _rms_stats(x2)
        diff = xhat * g - e[1][...]
        rows = jnp.sum(diff * diff, axis=1, keepdims=True)
        r[0][...] += jnp.broadcast_to(jnp.sum(rows, axis=0, keepdims=True) * (0.5 / D_MODEL), (1, LANES))
        dy = diff * (1.0 / D_MODEL)
        r[1][...] += jnp.sum(dy * xhat, axis=0, keepdims=True)
        dx2 = _rms_bwd(dy, xhat, rr, g)
        o[0][...] = dx2
        o[1][...] = dx2.astype(BF16)

    row = pl.BlockSpec((tm, D_MODEL), lambda i, j, k: (i, 0))
    return _mm("ffn_down_loss", (t // tm, 1, N_CHIPS),
               [(ff, pl.BlockSpec((None, tm, FF_SHARD), lambda i, j, k: (k, i, 0)),
                 w_d, pl.BlockSpec((None, FF_SHARD, D_MODEL), lambda i, j, k: (k, 0, 0)))],
               NN, (tm, D_MODEL), epi,
               extras=[(x1, row), (tgt, row), (gf, pl.BlockSpec((1, D_MODEL), lambda i, j, k: (0, 0)))],
               outs=[((t, D_MODEL), F32, row), ((t, D_MODEL), BF16, row)],
               reds=[(1, LANES), (1, D_MODEL)])


def _ffn_bwd_act(dx2b, w_d, a, b):
    t = dx2b.shape[0]
    tm = min(t, 512)
    o_spec = pl.BlockSpec((None, tm, FF_SHARD), lambda i, j, k: (j, i, 0))
    shp = (N_CHIPS, t, FF_SHARD)

    def epi(acc, e, o, r, ids):
        av, bv = e[0][...], e[1][...]
        sg = _sigmoid(av)
        o[0][...] = (acc * bv * (sg * (1.0 + av * (1.0 - sg)))).astype(BF16)
        o[1][...] = (acc * (av * sg)).astype(BF16)

    return _mm("ffn_bwd_act", (t // tm, N_CHIPS, 1),
               [(dx2b, pl.BlockSpec((tm, D_MODEL), lambda i, j, k: (i, 0)),
                 w_d, pl.BlockSpec((None, FF_SHARD, D_MODEL), lambda i, j, k: (j, 0, 0)))],
               NT, (tm, FF_SHARD), epi, extras=[(a, o_spec), (b, o_spec)],
               outs=[(shp, BF16, o_spec), (shp, BF16, o_spec)])


def _ffn_bwd_in(da, db, w_g, w_u, x1, dx2, g2):
    t = x1.shape[0]
    tm = min(t, 512)

    def epi(acc, e, o, r, ids):
        xhat, rr = _rms_stats(e[0][...])
        r[0][...] += jnp.sum(acc * xhat, axis=0, keepdims=True)
        dx1 = e[1][...] + _rms_bwd(acc, xhat, rr, e[2][...])
        o[0][...] = dx1
        o[1][...] = dx1.astype(BF16)

    a_spec = pl.BlockSpec((None, tm, FF_SHARD), lambda i, j, k: (k, i, 0))
    w_spec = pl.BlockSpec((None, D_MODEL, FF_SHARD), lambda i, j, k: (k, 0, 0))
    row = pl.BlockSpec((tm, D_MODEL), lambda i, j, k: (i, 0))
    return _mm("ffn_bwd_in", (t // tm, 1, N_CHIPS), [(da, a_spec, w_g, w_spec), (db, a_spec, w_u, w_spec)],
               NT, (tm, D_MODEL), epi,
               extras=[(x1, row), (dx2, row), (g2, pl.BlockSpec((1, D_MODEL), lambda i, j, k: (0, 0)))],
               outs=[((t, D_MODEL), F32, row), ((t, D_MODEL), BF16, row)], reds=[(1, D_MODEL)])


def _in_proj_bwd(dproj, w_p, x, dx1, g1):
    t = x.shape[0]
    tm, tk = min(t, 512), 1920

    def epi(acc, e, o, r, ids):
        xhat, rr = _rms_stats(e[0][...])
        r[0][...] += jnp.sum(acc * xhat, axis=0, keepdims=True)
        o[0][...] = e[1][...] + _rms_bwd(acc, xhat, rr, e[2][...])

    row = pl.BlockSpec((tm, D_MODEL), lambda i, j, k: (i, 0))
    return _mm("in_proj_bwd", (t // tm, 1, IN_COLS // tk),
               [(dproj, pl.BlockSpec((tm, tk), lambda i, j, k: (i, k)), w_p, pl.BlockSpec((D_MODEL, tk), lambda i, j, k: (0, k)))],
               NT, (tm, D_MODEL), epi,
               extras=[(x, row), (dx1, row), (g1, pl.BlockSpec((1, D_MODEL), lambda i, j, k: (0, 0)))],
               outs=[((t, D_MODEL), F32, row)], reds=[(1, D_MODEL)])


def _epi_bf16(acc, e, o, r, ids):
    o[0][...] = acc.astype(BF16)


def _wgrad_2d(name, a, b, tm, tn, tk=512):
    t, k1 = a.shape
    n = b.shape[1]
    tk = min(t, tk)
    return _mm(name, (k1 // tm, n // tn, t // tk),
               [(a, pl.BlockSpec((tk, tm), lambda i, j, k: (k, i)), b, pl.BlockSpec((tk, tn), lambda i, j, k: (k, j)))],
               TN, (tm, tn), _epi_bf16, outs=[((k1, n), BF16, pl.BlockSpec((tm, tn), lambda i, j, k: (i, j)))])[0]


def _wgrad_ff_in(name, h2, da, tk=512):
    t = h2.shape[0]
    tk = min(t, tk)
    return _mm(name, (N_CHIPS, 1, t // tk),
               [(h2, pl.BlockSpec((tk, D_MODEL), lambda i, j, k: (k, 0)),
                 da, pl.BlockSpec((None, tk, FF_SHARD), lambda i, j, k: (i, k, 0)))],
               TN, (D_MODEL, FF_SHARD), _epi_bf16,
               outs=[((N_CHIPS, D_MODEL, FF_SHARD), BF16, pl.BlockSpec((None, D_MODEL, FF_SHARD), lambda i, j, k: (i, 0, 0)))])[0]


def _wgrad_ff_down(ff, dx2b, tk=512):
    t = dx2b.shape[0]
    tk = min(t, tk)
    return _mm("wgrad_ffn_down", (N_CHIPS, 1, t // tk),
               [(ff, pl.BlockSpec((None, tk, FF_SHARD), lambda i, j, k: (i, k, 0)),
                 dx2b, pl.BlockSpec((tk, D_MODEL), lambda i, j, k: (k, 0)))],
               TN, (FF_SHARD, D_MODEL), _epi_bf16,
               outs=[((N_CHIPS, FF_SHARD, D_MODEL), BF16, pl.BlockSpec((None, FF_SHARD, D_MODEL), lambda i, j, k: (i, 0, 0)))])[0]


def _local_step(x, pos_col, tgt, g1, ln_g, ln_b, w_s, b_s, g2, gf, w_p, w_pa, w_ps, w_out, w_g, w_u, w_d):
    cos_t, sin_t = _rope_table(pos_col)
    bias_exp = jnp.repeat(jnp.transpose(b_s), SGU_W // SGU_GROUPS, axis=1)

    h = _norm_fwd(x, g1)
    gu, qkv = _in_proj(h, w_p, cos_t, sin_t)
    os_, ls_ = [], []
    for g, dil in enumerate(DILATIONS):
        o, lse = _attn_fwd(qkv, g, dil)
        os_.append(o)
        ls_.append(lse)
    attn = _combine_fwd(os_, ls_)
    sgu = _sgu_fwd(gu, ln_g, ln_b, w_s, bias_exp)
    pa, ps, merged, x1, h2 = _merge_fwd(attn, sgu, gu, x, w_pa, w_ps, w_out, g2)
    a, b, ff = _ffn_fwd(h2, w_g, w_u)
    dx2, dx2b, loss, dgf = _ffn_down_loss(ff, w_d, x1, tgt, gf)

    da, db = _ffn_bwd_act(dx2b, w_d, a, b)
    dw_d = _wgrad_ff_down(ff, dx2b)
    dx1, dx1b, dg2 = _ffn_bwd_in(da, db, w_g, w_u, x1, dx2, g2)
    dw_g = _wgrad_ff_in("wgrad_ffn_gate", h2, da)
    dw_u = _wgrad_ff_in("wgrad_ffn_up", h2, db)

    dproj = lax.empty((x.shape[0], IN_COLS), BF16)
    dproj, dpa, dps, dattn, dsgu = _merge_bwd(dproj, dx1b, gu, pa, ps, w_pa, w_ps, w_out)
    dw_out = _wgrad_2d("wgrad_out", merged, dx1b, D_MODEL, D_MODEL)
    dw_pa = _wgrad_2d("wgrad_proj_attn", attn, dpa, ATTN_W, D_MODEL)
    dw_ps = _wgrad_2d("wgrad_proj_sgu", sgu, dps, SGU_W, D_MODEL)
    dproj, dw_s, dbias, dln_g, dln_b = _sgu_bwd(dproj, gu, dsgu, ln_g, ln_b, w_s, bias_exp)
    dos, ccs = _combine_bwd(dattn, os_, ls_)
    for g, dil in enumerate(DILATIONS):
        dproj = _attn_bwd(dproj, qkv, dos[g], ccs[g], ls_[g], cos_t, sin_t, g, dil)
    dx, dg1 = _in_proj_bwd(dproj, w_p, x, dx1, g1)
    dw_p = _wgrad_2d("wgrad_in", h, dproj, D_MODEL, 1536)

    db_s = jnp.transpose(dbias[:, ::SGU_W // SGU_GROUPS])
    small = dict(loss=loss, norm1_g=dg1, sgu_ln_g=dln_g, sgu_ln_b=dln_b, w_spatial=dw_s, b_spatial=db_s,
                 norm2_g=dg2, final_g=dgf)
    big = dict(w_in=dw_p, w_proj_attn=dw_pa, w_proj_sgu=dw_ps, w_out=dw_out, w_ffn_gate=dw_g, w_ffn_up=dw_u,
               w_ffn_down=dw_d)
    return dx, big, small


def _ew(name, fn, ins, out_dtypes, out_shape=None, lead_all=False):
    shp = ins[0].shape
    rows, cols = shp[-2], shp[-1]
    tr = rows
    for cand in (256, 352, 128):
        if rows % cand == 0 and rows > cand:
            tr = cand
            break
    nout = len(out_dtypes)

    def body(*refs):
        res = fn(*[r[...] for r in refs[:len(ins)]])
        for o_ref, v in zip(refs[len(ins):], res):
            o_ref[...] = v.astype(o_ref.dtype)

    if len(shp) == 2:
        grid = (rows // tr,)
        spec = pl.BlockSpec((tr, cols), lambda i: (i, 0))
        in_specs, out_specs, oshape = [spec] * len(ins), [spec] * nout, shp
    elif lead_all:
        grid = (rows // tr,)
        in_specs = [pl.BlockSpec((shp[0], tr, cols), lambda i: (0, i, 0))] * len(ins)
        out_specs = [pl.BlockSpec((tr, cols), lambda i: (i, 0))] * nout
        oshape = (rows, cols)
    else:
        grid = (shp[0], rows // tr)
        spec = pl.BlockSpec((None, tr, cols), lambda s, i: (s, i, 0))
        in_specs, out_specs, oshape = [spec] * len(ins), [spec] * nout, shp
    return pl.pallas_call(
        body, grid=grid, in_specs=in_specs, out_specs=out_specs,
        out_shape=[jax.ShapeDtypeStruct(out_shape or oshape, d) for d in out_dtypes],
        compiler_params=_cparams(len(grid)), name=name)(*ins)


def _adamw_math(g, w, m, v):
    m = ADAM_B1 * m + (1.0 - ADAM_B1) * g
    v = ADAM_B2 * v + (1.0 - ADAM_B2) * (g * g)
    m_hat = m / (1.0 - ADAM_B1 ** ADAM_STEP)
    v_hat = v / (1.0 - ADAM_B2 ** ADAM_STEP)
    delta = -ADAM_LR * (m_hat / (jnp.sqrt(v_hat) + ADAM_EPS) + ADAM_WD * w)
    return delta, m, v


def _adamw(name, g, w, m, v):
    return _ew(name, _adamw_math, [g, w, m, v], [F32, F32, F32])


HBM_SPEC = pl.BlockSpec(memory_space=pltpu.HBM)


def _place():
    x, y, c = lax.axis_index("x"), lax.axis_index("y"), lax.axis_index("c")
    chips = [(1 - x, y), (x, 1 - y), (1 - x, 1 - y)]
    return x, y, c, 2 * x + y, chips


def _rows(ref, start, size):
    if len(ref.shape) == 2:
        return ref.at[pl.ds(start, size), :]
    return ref.at[:, pl.ds(start, size), :]


def _comm_call(name, body, ins, out_shapes, n_remote, n_local):
    return pl.pallas_call(
        body, in_specs=[HBM_SPEC] * len(ins), out_specs=[HBM_SPEC] * len(out_shapes),
        out_shape=out_shapes,
        scratch_shapes=[pltpu.SemaphoreType.DMA((n_remote,)), pltpu.SemaphoreType.DMA((n_remote,)),
                        pltpu.SemaphoreType.DMA((max(n_local, 1),))],
        name=name)(*ins)


def _gather_weights(shards):
    nt = len(shards)

    def body(*refs):
        ins, outs = refs[:nt], refs[nt:2 * nt]
        send, recv, loc = refs[2 * nt:]
        x, y, c, me, chips = _place()
        sibling = (x, y, 1 - c)
        locals_, firsts, passed, expects = [], [], [], []
        for t in range(nt):
            kh = ins[t].shape[0] // 2
            cp = pltpu.make_async_copy(ins[t], outs[t].at[me], loc.at[t])
            cp.start()
            locals_.append(cp)
            for j, chip in enumerate(chips):
                k = t * 3 + j
                theirs = 2 * chip[0] + chip[1]
                firsts.append(pltpu.make_async_remote_copy(
                    src_ref=_rows(ins[t], c * kh, kh), dst_ref=_rows(outs[t].at[me], c * kh, kh),
                    send_sem=send.at[k], recv_sem=recv.at[k], device_id=(*chip, c), device_id_type=MESH))
                landed = _rows(outs[t].at[theirs], c * kh, kh)
                expects.append(pltpu.make_async_remote_copy(
                    src_ref=landed, dst_ref=landed, send_sem=send.at[k], recv_sem=recv.at[k],
                    device_id=(*chip, c), device_id_type=MESH))
                passed.append(pltpu.make_async_remote_copy(
                    src_ref=landed, dst_ref=landed, send_sem=send.at[3 * nt + k], recv_sem=recv.at[3 * nt + k],
                    device_id=sibling, device_id_type=MESH))
        for cp in firsts:
            cp.start()
        for k in range(3 * nt):
            expects[k].wait_recv()
            passed[k].start()
        for t in range(nt):
            kh = ins[t].shape[0] // 2
            for j, chip in enumerate(chips):
                k = t * 3 + j
                theirs = 2 * chip[0] + chip[1]
                other = _rows(outs[t].at[theirs], (1 - c) * kh, kh)
                pltpu.make_async_remote_copy(
                    src_ref=other, dst_ref=other, send_sem=send.at[3 * nt + k], recv_sem=recv.at[3 * nt + k],
                    device_id=sibling, device_id_type=MESH).wait_recv()
        for cp in firsts + passed:
            cp.wait_send()
        for cp in locals_:
            cp.wait()

    out_shapes = [jax.ShapeDtypeStruct((N_CHIPS,) + s.shape, s.dtype) for s in shards]
    return _comm_call("gather_weights", body, shards, out_shapes, 6 * nt, nt)


def _pair_exchange(grads):
    nt = len(grads)

    def body(*refs):
        ins, own, got = refs[:nt], refs[nt:2 * nt], refs[2 * nt:3 * nt]
        send, recv, loc = refs[3 * nt:]
        x, y, c, me, chips = _place()
        sibling = (x, y, 1 - c)
        copies = []
        for t in range(nt):
            kh = ins[t].shape[-2] // 2
            lc = pltpu.make_async_copy(_rows(ins[t], c * kh, kh), own[t], loc.at[t])
            lc.start()
            rc = pltpu.make_async_remote_copy(
                src_ref=_rows(ins[t], (1 - c) * kh, kh), dst_ref=got[t], send_sem=send.at[t], recv_sem=recv.at[t],
                device_id=sibling, device_id_type=MESH)
            rc.start()
            copies.append((lc, rc))
        for lc, rc in copies:
            rc.wait_recv()
        for lc, rc in copies:
            rc.wait_send()
            lc.wait()

    def half(s):
        shp = list(s.shape)
        shp[-2] //= 2
        return jax.ShapeDtypeStruct(tuple(shp), s.dtype)

    outs = _comm_call("rs_pair_exchange", body, grads, [half(g) for g in grads] * 2, nt, nt)
    return outs[:nt], outs[nt:]


def _chip_exchange(sums):
    nt = len(sums)

    def piece(ref, j):
        if len(ref.shape) == 3:
            return ref.at[j]
        n4 = ref.shape[1] // N_CHIPS
        return ref.at[:, pl.ds(j * n4, n4)]

    def body(*refs):
        ins, outs = refs[:nt], refs[nt:2 * nt]
        send, recv, loc = refs[2 * nt:]
        x, y, c, me, chips = _place()
        copies, locals_ = [], []
        for t in range(nt):
            lc = pltpu.make_async_copy(piece(ins[t], me), outs[t].at[me], loc.at[t])
            lc.start()
            locals_.append(lc)
            for j, chip in enumerate(chips):
                k = t * 3 + j
                theirs = 2 * chip[0] + chip[1]
                rc = pltpu.make_async_remote_copy(
                    src_ref=piece(ins[t], theirs), dst_ref=outs[t].at[me], send_sem=send.at[k], recv_sem=recv.at[k],
                    device_id=(*chip, c), device_id_type=MESH)
                rc.start()
                arrive = pltpu.make_async_remote_copy(
                    src_ref=outs[t].at[theirs], dst_ref=outs[t].at[theirs], send_sem=send.at[k], recv_sem=recv.at[k],
                    device_id=(*chip, c), device_id_type=MESH)
                copies.append((rc, arrive))
        for rc, arrive in copies:
            arrive.wait_recv()
        for rc, arrive in copies:
            rc.wait_send()
        for lc in locals_:
            lc.wait()

    def slots(s):
        if len(s.shape) == 3:
            return jax.ShapeDtypeStruct(s.shape, s.dtype)
        return jax.ShapeDtypeStruct((N_CHIPS, s.shape[0], s.shape[1] // N_CHIPS), s.dtype)

    return _comm_call("rs_chip_exchange", body, sums, [slots(s) for s in sums], 3 * nt, nt)


def _pair_gather(halves):
    nt = len(halves)

    def body(*refs):
        ins, outs = refs[:nt], refs[nt:2 * nt]
        send, recv, loc = refs[2 * nt:]
        x, y, c, me, chips = _place()
        sibling = (x, y, 1 - c)
        copies = []
        for t in range(nt):
            kh = ins[t].shape[0]
            lc = pltpu.make_async_copy(ins[t], _rows(outs[t], c * kh, kh), loc.at[t])
            lc.start()
            rc = pltpu.make_async_remote_copy(
                src_ref=ins[t], dst_ref=_rows(outs[t], c * kh, kh), send_sem=send.at[t], recv_sem=recv.at[t],
                device_id=sibling, device_id_type=MESH)
            rc.start()
            arrive = pltpu.make_async_remote_copy(
                src_ref=ins[t], dst_ref=_rows(outs[t], (1 - c) * kh, kh), send_sem=send.at[t], recv_sem=recv.at[t],
                device_id=sibling, device_id_type=MESH)
            copies.append((lc, rc, arrive))
        for lc, rc, arrive in copies:
            arrive.wait_recv()
        for lc, rc, arrive in copies:
            rc.wait_send()
            lc.wait()

    out_shapes = [jax.ShapeDtypeStruct((2 * h.shape[0], h.shape[1]), h.dtype) for h in halves]
    return _comm_call("rs_pair_gather", body, halves, out_shapes, nt, nt)


def _allreduce_small(buf):
    shp = buf.shape

    def body(in_ref, out_ref, pair_ref, slot_ref, send, recv):
        x, y, c, me, chips = _place()
        sibling = (x, y, 1 - c)
        first = pltpu.make_async_remote_copy(src_ref=in_ref, dst_ref=pair_ref, send_sem=send.at[0], recv_sem=recv.at[0],
                                             device_id=sibling, device_id_type=MESH)
        first.start()
        first.wait_recv()
        slot_ref[me] = in_ref[...] + pair_ref[...]
        copies = []
        for j, chip in enumerate(chips):
            theirs = 2 * chip[0] + chip[1]
            rc = pltpu.make_async_remote_copy(src_ref=slot_ref.at[me], dst_ref=slot_ref.at[me], send_sem=send.at[1 + j],
                                              recv_sem=recv.at[1 + j], device_id=(*chip, c), device_id_type=MESH)
            rc.start()
            arrive = pltpu.make_async_remote_copy(src_ref=slot_ref.at[theirs], dst_ref=slot_ref.at[theirs],
                                                  send_sem=send.at[1 + j], recv_sem=recv.at[1 + j],
                                                  device_id=(*chip, c), device_id_type=MESH)
            copies.append((rc, arrive))
        for rc, arrive in copies:
            arrive.wait_recv()
        out_ref[...] = ((slot_ref[0] + slot_ref[1]) + slot_ref[2]) + slot_ref[3]
        first.wait_send()
        for rc, arrive in copies:
            rc.wait_send()

    vm = pl.BlockSpec(memory_space=pltpu.VMEM)
    return pl.pallas_call(
        body, in_specs=[vm], out_specs=vm, out_shape=jax.ShapeDtypeStruct(shp, F32),
        scratch_shapes=[pltpu.VMEM(shp, F32), pltpu.VMEM((N_CHIPS,) + shp, F32),
                        pltpu.SemaphoreType.DMA((4,)), pltpu.SemaphoreType.DMA((4,))],
        compiler_params=pltpu.CompilerParams(vmem_limit_bytes=VMEM_LIMIT),
        name="allreduce_small")(buf)


SMALL_ORDER = ("norm1_g", "norm2_g", "final_g", "sgu_ln_g", "sgu_ln_b", "b_spatial", "loss", "w_spatial")
SMALL_SIZES = dict(norm1_g=1024, norm2_g=1024, final_g=1024, sgu_ln_g=512, sgu_ln_b=512, b_spatial=1024, loss=LANES,
                   w_spatial=SGU_GROUPS * SGU_CHUNK * SGU_CHUNK)
SMALL_ROWS = 1072


def _pack_small(parts):
    flat = [jnp.reshape(parts[n].astype(F32), (-1,)) for n in SMALL_ORDER]
    used = sum(SMALL_SIZES[n] for n in SMALL_ORDER)
    flat.append(jnp.zeros((SMALL_ROWS * LANES - used,), F32))
    return jnp.concatenate(flat).reshape(SMALL_ROWS, LANES)


def _unpack_small(buf, shapes):
    flat = buf.reshape(-1)
    out, off = {}, 0
    for n in SMALL_ORDER:
        sz = SMALL_SIZES[n]
        if n in shapes:
            out[n] = flat[off:off + sz].reshape(shapes[n])
        off += sz
    return out


BIG = ("w_in", "w_proj_attn", "w_proj_sgu", "w_out", "w_ffn_gate", "w_ffn_up", "w_ffn_down")
WEIGHTS = ("norm1_g", "w_in", "sgu_ln_g", "sgu_ln_b", "w_spatial", "b_spatial", "w_proj_attn", "w_proj_sgu", "w_out",
           "norm2_g", "w_ffn_gate", "w_ffn_up", "w_ffn_down", "final_g")


def _cols_from_chips(g):
    return jnp.transpose(g, (1, 0, 2)).reshape(g.shape[1], N_CHIPS * g.shape[2])


def _permute_cols(w, perm):
    return jnp.concatenate([w[:, 512 * b:512 * (b + 1)] for b in perm], axis=1)


def kernel(x, positions, norm1_g, w_in, sgu_ln_g, sgu_ln_b, w_spatial, b_spatial, w_proj_attn, w_proj_sgu, w_out, norm2_g, w_ffn_gate, w_ffn_up, w_ffn_down, final_g, loss_target, m_norm1_g, m_w_in, m_sgu_ln_g, m_sgu_ln_b, m_w_spatial, m_b_spatial, m_w_proj_attn, m_w_proj_sgu, m_w_out, m_norm2_g, m_w_ffn_gate, m_w_ffn_up, m_w_ffn_down, m_final_g, v_norm1_g, v_w_in, v_sgu_ln_g, v_sgu_ln_b, v_w_spatial, v_b_spatial, v_w_proj_attn, v_w_proj_sgu, v_w_out, v_norm2_g, v_w_ffn_gate, v_w_ffn_up, v_w_ffn_down, v_final_g):
    w = dict(norm1_g=norm1_g, w_in=w_in, sgu_ln_g=sgu_ln_g, sgu_ln_b=sgu_ln_b, w_spatial=w_spatial, b_spatial=b_spatial,
             w_proj_attn=w_proj_attn, w_proj_sgu=w_proj_sgu, w_out=w_out, norm2_g=norm2_g, w_ffn_gate=w_ffn_gate,
             w_ffn_up=w_ffn_up, w_ffn_down=w_ffn_down, final_g=final_g)
    m = dict(norm1_g=m_norm1_g, w_in=m_w_in, sgu_ln_g=m_sgu_ln_g, sgu_ln_b=m_sgu_ln_b, w_spatial=m_w_spatial,
             b_spatial=m_b_spatial, w_proj_attn=m_w_proj_attn, w_proj_sgu=m_w_proj_sgu, w_out=m_w_out, norm2_g=m_norm2_g,
             w_ffn_gate=m_w_ffn_gate, w_ffn_up=m_w_ffn_up, w_ffn_down=m_w_ffn_down, final_g=m_final_g)
    v = dict(norm1_g=v_norm1_g, w_in=v_w_in, sgu_ln_g=v_sgu_ln_g, sgu_ln_b=v_sgu_ln_b, w_spatial=v_w_spatial,
             b_spatial=v_b_spatial, w_proj_attn=v_w_proj_attn, w_proj_sgu=v_w_proj_sgu, w_out=v_w_out, norm2_g=v_norm2_g,
             w_ffn_gate=v_w_ffn_gate, w_ffn_up=v_w_ffn_up, w_ffn_down=v_w_ffn_down, final_g=v_final_g)
    t = x.shape[1]

    shards = [_ew(f"cast_{n}", lambda a: (a,), [w[n][0]], [BF16])[0] for n in BIG]
    gath = dict(zip(BIG, _gather_weights(shards)))
    w_p = _permute_cols(_cols_from_chips(gath["w_in"]), PERM)
    w_pa = _cols_from_chips(gath["w_proj_attn"])
    w_ps = _cols_from_chips(gath["w_proj_sgu"])
    w_o = gath["w_out"].reshape(D_MODEL, D_MODEL)

    dx, big, small = _local_step(
        x[0], positions.reshape(t, 1), loss_target[0], norm1_g, sgu_ln_g, sgu_ln_b, w_spatial[0], b_spatial[0], norm2_g,
        final_g.reshape(1, D_MODEL), w_p, w_pa, w_ps, w_o, gath["w_ffn_gate"], gath["w_ffn_up"], gath["w_ffn_down"])

    big["w_in"] = _permute_cols(big["w_in"], INV_PERM)
    big["w_out"] = big["w_out"].reshape(N_CHIPS, D_MODEL // N_CHIPS, D_MODEL)
    own, got = _pair_exchange([big[n] for n in BIG])
    add2 = lambda a, b: (a.astype(F32) + b.astype(F32),)
    sums = [_ew(f"pair_sum_{n}", add2, [o, r], [BF16])[0] for n, o, r in zip(BIG, own, got)]
    slots = _chip_exchange(sums)
    add4 = lambda s: (((s[0].astype(F32) + s[1].astype(F32)) + s[2].astype(F32)) + s[3].astype(F32),)
    halves = [_ew(f"chip_sum_{n}", add4, [s], [F32], lead_all=True)[0] for n, s in zip(BIG, slots)]
    grads = dict(zip(BIG, _pair_gather(halves)))

    small_shapes = {n: w[n].shape for n in SMALL_ORDER if n != "loss"}
    reduced = _allreduce_small(_pack_small(small))
    grads.update(_unpack_small(reduced, small_shapes))
    loss = reduced.reshape(-1)[sum(SMALL_SIZES[n] for n in SMALL_ORDER[:SMALL_ORDER.index("loss")])]

    delta, new_m, new_v = {}, {}, {}
    for n in BIG:
        shp = w[n].shape
        grads[n] = grads[n].reshape(shp)
        d_, m_, v_ = _adamw(f"adamw_{n}", grads[n][0], w[n][0], m[n][0], v[n][0])
        delta[n], new_m[n], new_v[n] = d_.reshape(shp), m_.reshape(shp), v_.reshape(shp)
    zero_loss = dict(loss=jnp.zeros((LANES,), F32))
    d_, m_, v_ = _adamw("adamw_small", reduced, _pack_small({**w, **zero_loss}), _pack_small({**m, **zero_loss}),
                        _pack_small({**v, **zero_loss}))
    delta.update(_unpack_small(d_, small_shapes))
    new_m.update(_unpack_small(m_, small_shapes))
    new_v.update(_unpack_small(v_, small_shapes))

    return (loss, dx.reshape(x.shape), *[grads[n] for n in WEIGHTS], *[delta[n] for n in WEIGHTS],
            *[new_m[n] for n in WEIGHTS], *[new_v[n] for n in WEIGHTS])
```

```python
import functools

import numpy as np
import jax
import jax.numpy as jnp
from jax import lax
from jax.experimental import pallas as pl
from jax.experimental.pallas import tpu as pltpu

F32, BF16 = jnp.float32, jnp.bfloat16
MESH = pl.DeviceIdType.MESH

D_MODEL = 1024
HEAD_DIM = 64
ATTN_W = 512
DILATIONS = (1, 4, 16)
BLK = 128
ROPE_DIM = 16
ROPE_THETA = 500000.0
SGU_W = 512
SGU_CHUNK = 128
SGU_GROUPS = 8
D_FF = 2816
N_CHIPS = 4
FF_SHARD = D_FF // N_CHIPS
IN_COLS = 7680
QKV_COLS = 4608
EPS = 1e-6
NEG = -1e30
LANES = 128
VMEM_LIMIT = 52 * 1024 * 1024

ADAM_LR, ADAM_B1, ADAM_B2, ADAM_EPS, ADAM_WD, ADAM_STEP = 0.001, 0.9, 0.999, 1e-08, 0.01, 10

PERM = (11, 12, 13, 14, 9, 10, 0, 3, 6, 1, 4, 7, 2, 5, 8)
INV_PERM = tuple(int(i) for i in np.argsort(np.array(PERM)))


def _cparams(ngrid):
    return pltpu.CompilerParams(dimension_semantics=("arbitrary",) * ngrid, vmem_limit_bytes=VMEM_LIMIT)


def _full(shape):
    return pl.BlockSpec(shape, lambda *_: (0,) * len(shape))


NN = ((1,), (0,))
NT = ((1,), (1,))
TN = ((0,), (0,))


def _mm(name, grid, pairs, dims, acc_shape, epi, *, extras=(), outs=(), reds=(), aliases=None):
    nk = grid[-1]
    npair, nex, nout, nred = len(pairs), len(extras), len(outs), len(reds)

    def body(*refs):
        a_refs = refs[:npair]
        b_refs = refs[npair:2 * npair]
        p0 = 2 * npair
        e_refs = refs[p0:p0 + nex]
        o_refs = refs[p0 + nex:p0 + nex + nout]
        r_refs = refs[p0 + nex + nout:p0 + nex + nout + nred]
        ids = [pl.program_id(a) for a in range(len(grid))]
        k = ids[-1]
        if nred:
            first = ids[0] == 0
            for v in ids[1:]:
                first = first & (v == 0)

            @pl.when(first)
            def _():
                for r in r_refs:
                    r[...] = jnp.zeros(r.shape, r.dtype)

        part = None
        for a_ref, b_ref in zip(a_refs, b_refs):
            d = lax.dot_general(a_ref[...], b_ref[...], (dims, ((), ())), preferred_element_type=F32)
            part = d if part is None else part + d
        if nk == 1:
            epi(part, e_refs, o_refs, r_refs, ids)
        else:
            acc_ref = refs[-1]

            @pl.when(k == 0)
            def _():
                acc_ref[...] = part

            @pl.when(k > 0)
            def _():
                acc_ref[...] += part

            @pl.when(k == nk - 1)
            def _():
                epi(acc_ref[...], e_refs, o_refs, r_refs, ids)

    in_specs = [p[1] for p in pairs] + [p[3] for p in pairs] + [e[1] for e in extras]
    args = [p[0] for p in pairs] + [p[2] for p in pairs] + [e[0] for e in extras]
    out_shape = [jax.ShapeDtypeStruct(o[0], o[1]) for o in outs] + [jax.ShapeDtypeStruct(r, F32) for r in reds]
    out_specs = [o[2] for o in outs] + [_full(r) for r in reds]
    scratch = [pltpu.VMEM(acc_shape, F32)] if nk > 1 else []
    return pl.pallas_call(
        body, grid=grid, in_specs=in_specs, out_specs=out_specs, out_shape=out_shape, scratch_shapes=scratch,
        input_output_aliases=aliases or {}, compiler_params=_cparams(len(grid)), name=name)(*args)


def _lane_tile(t, width):
    n = width // LANES
    return t if n == 1 else jnp.concatenate([t] * n, axis=1)


def _rope(v, cos_w, sin_w):
    w = v.shape[1]
    lane = lax.broadcasted_iota(jnp.int32, v.shape, 1)
    partner = jnp.where((lane % HEAD_DIM) < ROPE_DIM // 2, pltpu.roll(v, w - ROPE_DIM // 2, axis=1),
                        pltpu.roll(v, ROPE_DIM // 2, axis=1))
    return v * cos_w + partner * sin_w


def _sigmoid(v):
    return 1.0 / (1.0 + jnp.exp(-v))


def _rms_stats(v):
    r = lax.rsqrt(jnp.mean(v * v, axis=-1, keepdims=True) + EPS)
    return v * r, r


def _rms_bwd(dy, xhat, r, g):
    dxh = dy * g
    return r * (dxh - xhat * jnp.mean(dxh * xhat, axis=-1, keepdims=True))


def _head_sum_matrix():
    idx = np.arange(ATTN_W) // HEAD_DIM
    return jnp.asarray((idx[:, None] == idx[None, :]).astype(np.float32), dtype=BF16)


def _group_sum(v, e):
    hi = v.astype(BF16)
    lo = (v - hi.astype(F32)).astype(BF16)
    return jnp.dot(hi, e, preferred_element_type=F32) + jnp.dot(lo, e, preferred_element_type=F32)


def _rope_consts():
    lane = np.arange(LANES) % HEAD_DIM
    fi = lane % (ROPE_DIM // 2)
    invf = np.where(lane < ROPE_DIM, ROPE_THETA ** (-(2.0 * fi) / ROPE_DIM), 0.0)
    sgn = np.where(lane < ROPE_DIM // 2, -1.0, np.where(lane < ROPE_DIM, 1.0, 0.0))
    return (jnp.asarray(invf.astype(np.float32)).reshape(1, LANES), jnp.asarray(sgn.astype(np.float32)).reshape(1, LANES))


def _rope_table(pos_col):
    t = pos_col.shape[0]
    tm = min(t, 1024)
    invf, sgn = _rope_consts()

    def body(p_ref, f_ref, s_ref, c_out, s_out):
        ang = p_ref[...].astype(F32) * f_ref[...]
        c_out[...] = jnp.cos(ang)
        s_out[...] = jnp.sin(ang) * s_ref[...]

    return pl.pallas_call(
        body, grid=(t // tm,),
        in_specs=[pl.BlockSpec((tm, 1), lambda i: (i, 0)), _full((1, LANES)), _full((1, LANES))],
        out_specs=[pl.BlockSpec((tm, LANES), lambda i: (i, 0))] * 2,
        out_shape=[jax.ShapeDtypeStruct((t, LANES), F32)] * 2,
        compiler_params=_cparams(1), name="rope_table")(pos_col, invf, sgn)


def _norm_fwd(x, g):
    t = x.shape[0]
    tm = min(t, 512)

    def body(x_ref, g_ref, h_ref):
        xhat, _ = _rms_stats(x_ref[...])
        h_ref[...] = (xhat * g_ref[...]).astype(BF16)

    return pl.pallas_call(
        body, grid=(t // tm,),
        in_specs=[pl.BlockSpec((tm, D_MODEL), lambda i: (i, 0)), _full((1, D_MODEL))],
        out_specs=pl.BlockSpec((tm, D_MODEL), lambda i: (i, 0)),
        out_shape=jax.ShapeDtypeStruct((t, D_MODEL), BF16),
        compiler_params=_cparams(1), name="norm1_fwd")(x, g)


def _in_proj(h, w_p, cos_t, sin_t):
    t = h.shape[0]
    tm, tn = min(t, 512), 512

    def epi_plain(acc, e, o, r, ids):
        o[0][...] = acc

    gu = _mm("in_proj_gates_uv", (t // tm, 3072 // tn, 1),
             [(h, pl.BlockSpec((tm, D_MODEL), lambda i, j, k: (i, 0)), w_p, pl.BlockSpec((D_MODEL, tn), lambda i, j, k: (0, j)))],
             NN, (tm, tn), epi_plain,
             outs=[((t, 3072), F32, pl.BlockSpec((tm, tn), lambda i, j, k: (i, j)))])[0]

    def epi_qkv(acc, e, o, r, ids):
        role = ids[1] % 3
        cos_w = _lane_tile(e[0][...], tn)
        sin_w = _lane_tile(e[1][...], tn)
        roped = _rope(acc, cos_w, sin_w) * jnp.where(role == 0, HEAD_DIM ** -0.5, 1.0)
        o[0][...] = jnp.where(role == 2, acc, roped).astype(BF16)

    qkv = _mm("in_proj_qkv", (t // tm, QKV_COLS // tn, 1),
              [(h, pl.BlockSpec((tm, D_MODEL), lambda i, j, k: (i, 0)), w_p,
                pl.BlockSpec((D_MODEL, tn), lambda i, j, k: (0, j + 3072 // tn)))],
              NN, (tm, tn), epi_qkv,
              extras=[(cos_t, pl.BlockSpec((tm, LANES), lambda i, j, k: (i, 0))),
                      (sin_t, pl.BlockSpec((tm, LANES), lambda i, j, k: (i, 0)))],
              outs=[((t, QKV_COLS), BF16, pl.BlockSpec((tm, tn), lambda i, j, k: (i, j)))])[0]
    return gu, qkv


def _attn_masks(n):
    row = lax.broadcasted_iota(jnp.int32, (BLK, 2 * BLK), 0)
    col = lax.broadcasted_iota(jnp.int32, (BLK, 2 * BLK), 1)
    diff = BLK + row - col
    valid = (diff >= 0) & (diff <= BLK) & ((col >= BLK) | (n > 0))
    upper = lax.broadcasted_iota(jnp.int32, (BLK, LANES), 1) >= HEAD_DIM
    return valid, upper


def _attn_fwd(qkv, g, dil):
    t = qkv.shape[0]
    length = t // dil
    nb = length // BLK
    view = qkv.reshape(length, dil * QKV_COLS)

    def cb(r, part):
        return (r * 3 + g) * 3 + part

    def body(q_ref, kc_ref, kp_ref, vc_ref, vp_ref, o_ref, l_ref):
        n = pl.program_id(1)
        valid, upper = _attn_masks(n)
        for p in range(ATTN_W // LANES):
            sl = slice(p * LANES, (p + 1) * LANES)
            q2 = q_ref[:, sl]
            k2 = jnp.concatenate([kp_ref[:, sl], kc_ref[:, sl]], axis=0)
            v2 = jnp.concatenate([vp_ref[:, sl], vc_ref[:, sl]], axis=0)
            outs, lses = [], []
            for hh in (0, 1):
                sel = upper if hh else jnp.logical_not(upper)
                qm = jnp.where(sel, q2, jnp.zeros_like(q2))
                s = lax.dot_general(qm, k2, (NT, ((), ())), preferred_element_type=F32)
                s = jnp.where(valid, s, NEG)
                m = jnp.max(s, axis=1, keepdims=True)
                pe = jnp.exp(s - m)
                den = jnp.sum(pe, axis=1, keepdims=True)
                outs.append(jnp.dot(pe.astype(BF16), v2, preferred_element_type=F32) / den)
                lses.append(jnp.broadcast_to(m + jnp.log(den), (BLK, LANES)))
            o_ref[:, sl] = jnp.where(upper, outs[1], outs[0])
            l_ref[:, sl] = jnp.where(upper, lses[1], lses[0])

    cur = lambda part: pl.BlockSpec((BLK, ATTN_W), lambda r, n: (n, cb(r, part)))
    prev = lambda part: pl.BlockSpec((BLK, ATTN_W), lambda r, n: (jnp.maximum(n - 1, 0), cb(r, part)))
    out_spec = pl.BlockSpec((BLK, ATTN_W), lambda r, n: (n, r))
    o, lse = pl.pallas_call(
        body, grid=(dil, nb),
        in_specs=[cur(0), cur(1), prev(1), cur(2), prev(2)],
        out_specs=[out_spec, out_spec],
        out_shape=[jax.ShapeDtypeStruct((length, dil * ATTN_W), F32)] * 2,
        compiler_params=_cparams(2), name=f"attn_fwd_g{g}")(view, view, view, view, view)
    return o.reshape(t, ATTN_W), lse.reshape(t, ATTN_W)


def _alphas(l_refs):
    l0, l1, l2 = (r[...] for r in l_refs)
    m = jnp.maximum(jnp.maximum(l0, l1), l2)
    e0, e1, e2 = jnp.exp(l0 - m), jnp.exp(l1 - m), jnp.exp(l2 - m)
    inv = 1.0 / (e0 + e1 + e2)
    return e0 * inv, e1 * inv, e2 * inv


def _combine_fwd(os_, ls_):
    t = os_[0].shape[0]
    tm = min(t, 512)

    def body(o0, o1, o2, l0, l1, l2, a_ref):
        a0, a1, a2 = _alphas((l0, l1, l2))
        a_ref[...] = (a0 * o0[...] + a1 * o1[...] + a2 * o2[...]).astype(BF16)

    spec = pl.BlockSpec((tm, ATTN_W), lambda i: (i, 0))
    return pl.pallas_call(
        body, grid=(t // tm,), in_specs=[spec] * 6, out_specs=spec,
        out_shape=jax.ShapeDtypeStruct((t, ATTN_W), BF16),
        compiler_params=_cparams(1), name="combine_fwd")(*os_, *ls_)


def _combine_bwd(dattn, os_, ls_):
    t = dattn.shape[0]
    tm = min(t, 512)
    e = _head_sum_matrix()

    def body(d_ref, o0, o1, o2, l0, l1, l2, e_ref, do0, do1, do2, c0, c1, c2):
        alphas = _alphas((l0, l1, l2))
        d = d_ref[...]
        attn = alphas[0] * o0[...] + alphas[1] * o1[...] + alphas[2] * o2[...]
        s = _group_sum(d * attn, e_ref[...])
        for a, do_ref, c_ref in zip(alphas, (do0, do1, do2), (c0, c1, c2)):
            do_ref[...] = (a * d).astype(BF16)
            c_ref[...] = -a * s

    spec = pl.BlockSpec((tm, ATTN_W), lambda i: (i, 0))
    outs = pl.pallas_call(
        body, grid=(t // tm,), in_specs=[spec] * 7 + [_full((ATTN_W, ATTN_W))], out_specs=[spec] * 6,
        out_shape=[jax.ShapeDtypeStruct((t, ATTN_W), BF16)] * 3 + [jax.ShapeDtypeStruct((t, ATTN_W), F32)] * 3,
        compiler_params=_cparams(1), name="combine_bwd")(dattn, *os_, *ls_, e)
    return outs[:3], outs[3:]


def _attn_bwd(dproj, qkv, do, cc, lse, cos_t, sin_t, g, dil):
    t = qkv.shape[0]
    length = t // dil
    nb = length // BLK
    qkv_v = qkv.reshape(length, dil * QKV_COLS)
    dproj_v = dproj.reshape(length, dil * IN_COLS)
    do_v, cc_v, lse_v = (a.reshape(length, dil * ATTN_W) for a in (do, cc, lse))
    cos_v, sin_v = (a.reshape(length, dil * LANES) for a in (cos_t, sin_t))
    scale = HEAD_DIM ** -0.5

    def cb(r, part):
        return (r * 3 + g) * 3 + part

    def body(dp_in, q_ref, kc_ref, kp_ref, vc_ref, vp_ref, do_ref, c_ref, l_ref, cosc, sinc, cosp, sinp,
             out_ref, dq_s, dk_s, dv_s):
        n = pl.program_id(1)
        valid, upper = _attn_masks(n)
        lower = jnp.logical_not(upper)

        @pl.when(n < nb)
        def _():
            cos_c, sin_c = _lane_tile(cosc[...], ATTN_W), _lane_tile(sinc[...], ATTN_W)
            cos_p, sin_p = _lane_tile(cosp[...], ATTN_W), _lane_tile(sinp[...], ATTN_W)
            dq_parts, dkp_parts, dkc_parts, dvp_parts, dvc_parts = [], [], [], [], []
            for p in range(ATTN_W // LANES):
                sl = slice(p * LANES, (p + 1) * LANES)
                q2 = q_ref[:, sl]
                k2 = jnp.concatenate([kp_ref[:, sl], kc_ref[:, sl]], axis=0)
                v2 = jnp.concatenate([vp_ref[:, sl], vc_ref[:, sl]], axis=0)
                do2 = do_ref[:, sl]
                l2 = l_ref[:, sl]
                c2 = c_ref[:, sl]
                dq2 = jnp.zeros((BLK, LANES), F32)
                dk2 = jnp.zeros((2 * BLK, LANES), F32)
                dv2 = jnp.zeros((2 * BLK, LANES), F32)
                for hh in (0, 1):
                    sel = upper if hh else lower
                    qm = jnp.where(sel, q2, jnp.zeros_like(q2))
                    dom = jnp.where(sel, do2, jnp.zeros_like(do2))
                    l_col = l2[:, hh * HEAD_DIM:hh * HEAD_DIM + 1]
                    c_col = c2[:, hh * HEAD_DIM:hh * HEAD_DIM + 1]
                    s = lax.dot_general(qm, k2, (NT, ((), ())), preferred_element_type=F32)
                    pe = jnp.where(valid, jnp.exp(jnp.where(valid, s, NEG) - l_col), 0.0)
                    dpv = lax.dot_general(dom, v2, (NT, ((), ())), preferred_element_type=F32)
                    ds = (pe * (dpv + c_col)).astype(BF16)
                    pb = pe.astype(BF16)
                    dq2 = dq2 + jnp.where(sel, jnp.dot(ds, k2, preferred_element_type=F32), 0.0)
                    dk2 = dk2 + lax.dot_general(ds, qm, (TN, ((), ())), preferred_element_type=F32)
                    dv2 = dv2 + lax.dot_general(pb, dom, (TN, ((), ())), preferred_element_type=F32)
                dq_parts.append(dq2)
                dkp_parts.append(dk2[:BLK])
                dkc_parts.append(dk2[BLK:])
                dvp_parts.append(dv2[:BLK])
                dvc_parts.append(dv2[BLK:])
            dq = _rope(jnp.concatenate(dq_parts, axis=1) * scale, cos_c, -sin_c)
            dkc = _rope(jnp.concatenate(dkc_parts, axis=1), cos_c, -sin_c)
            dkp = _rope(jnp.concatenate(dkp_parts, axis=1), cos_p, -sin_p)
            dvp = jnp.concatenate(dvp_parts, axis=1)
            dvc = jnp.concatenate(dvc_parts, axis=1)

            @pl.when(n > 0)
            def _():
                out_ref[:, 0:ATTN_W] = dq_s[...].astype(BF16)
                out_ref[:, ATTN_W:2 * ATTN_W] = (dk_s[...] + dkp).astype(BF16)
                out_ref[:, 2 * ATTN_W:3 * ATTN_W] = (dv_s[...] + dvp).astype(BF16)

            dq_s[...] = dq
            dk_s[...] = dkc
            dv_s[...] = dvc

        @pl.when(n == nb)
        def _():
            out_ref[:, 0:ATTN_W] = dq_s[...].astype(BF16)
            out_ref[:, ATTN_W:2 * ATTN_W] = dk_s[...].astype(BF16)
            out_ref[:, 2 * ATTN_W:3 * ATTN_W] = dv_s[...].astype(BF16)

    nc = lambda n: jnp.minimum(n, nb - 1)
    npv = lambda n: jnp.maximum(jnp.minimum(n, nb - 1) - 1, 0)
    cur = lambda part: pl.BlockSpec((BLK, ATTN_W), lambda r, n: (nc(n), cb(r, part)))
    prev = lambda part: pl.BlockSpec((BLK, ATTN_W), lambda r, n: (npv(n), cb(r, part)))
    row = pl.BlockSpec((BLK, ATTN_W), lambda r, n: (nc(n), r))
    tab_c = pl.BlockSpec((BLK, LANES), lambda r, n: (nc(n), r))
    tab_p = pl.BlockSpec((BLK, LANES), lambda r, n: (npv(n), r))
    out_spec = pl.BlockSpec((BLK, 3 * ATTN_W), lambda r, n: (jnp.maximum(n - 1, 0), r * 5 + 2 + g))
    out = pl.pallas_call(
        body, grid=(dil, nb + 1),
        in_specs=[pl.BlockSpec(memory_space=pl.ANY), cur(0), cur(1), prev(1), cur(2), prev(2), row, row, row,
                  tab_c, tab_c, tab_p, tab_p],
        out_specs=out_spec,
        out_shape=jax.ShapeDtypeStruct(dproj_v.shape, BF16),
        scratch_shapes=[pltpu.VMEM((BLK, ATTN_W), F32)] * 3,
        input_output_aliases={0: 0},
        compiler_params=_cparams(2), name=f"attn_bwd_g{g}")(
            dproj_v, qkv_v, qkv_v, qkv_v, qkv_v, qkv_v, do_v, cc_v, lse_v, cos_v, sin_v, cos_v, sin_v)
    return out.reshape(t, IN_COLS)


SQRT_HALF = 0.7071067811865476
INV_SQRT_2PI = 0.3989422804014327


def _sgu_core(uv, g, b, w_ref, bias):
    cdf = 0.5 * (1.0 + lax.erf(uv * SQRT_HALF))
    z = uv * cdf
    u, v = z[:, :SGU_W], z[:, SGU_W:]
    mu = jnp.mean(v, axis=1, keepdims=True)
    xc = v - mu
    rs = lax.rsqrt(jnp.mean(xc * xc, axis=1, keepdims=True) + EPS)
    xhat = xc * rs
    vn = xhat * g + b
    row = lax.broadcasted_iota(jnp.int32, (SGU_CHUNK, SGU_CHUNK), 0)
    col = lax.broadcasted_iota(jnp.int32, (SGU_CHUNK, SGU_CHUNK), 1)
    tril = row >= col
    upper = lax.broadcasted_iota(jnp.int32, (SGU_CHUNK, LANES), 1) >= SGU_W // SGU_GROUPS
    ws, vlo, vhi, mixed = [], [], [], []
    for pr in range(SGU_W // LANES):
        sl = slice(pr * LANES, (pr + 1) * LANES)
        w0 = jnp.where(tril, w_ref[2 * pr], 0.0).astype(BF16)
        w1 = jnp.where(tril, w_ref[2 * pr + 1], 0.0).astype(BF16)
        vn2 = vn[:, sl]
        lo = jnp.where(upper, 0.0, vn2).astype(BF16)
        hi = jnp.where(upper, vn2, 0.0).astype(BF16)
        mixed.append(jnp.dot(w0, lo, preferred_element_type=F32) + jnp.dot(w1, hi, preferred_element_type=F32)
                     + bias[:, sl])
        ws.append((w0, w1))
        vlo.append(lo)
        vhi.append(hi)
    return cdf, u, xhat, rs, jnp.concatenate(mixed, axis=1), ws, vlo, vhi, tril, upper


def _sgu_fwd(gu, ln_g, ln_b, w_s, bias_exp):
    t = gu.shape[0]

    def body(uv_ref, g_ref, b_ref, w_ref, bias_ref, o_ref):
        _, u, _, _, mixed, *_ = _sgu_core(uv_ref[...], g_ref[...], b_ref[...], w_ref, bias_ref[...])
        o_ref[...] = (u * mixed).astype(BF16)

    return pl.pallas_call(
        body, grid=(t // SGU_CHUNK,),
        in_specs=[pl.BlockSpec((SGU_CHUNK, 2 * SGU_W), lambda n: (n, 2)), _full((1, SGU_W)), _full((1, SGU_W)),
                  _full((SGU_GROUPS, SGU_CHUNK, SGU_CHUNK)), _full((SGU_CHUNK, SGU_W))],
        out_specs=pl.BlockSpec((SGU_CHUNK, SGU_W), lambda n: (n, 0)),
        out_shape=jax.ShapeDtypeStruct((t, SGU_W), BF16),
        compiler_params=_cparams(1), name="sgu_fwd")(gu, ln_g, ln_b, w_s, bias_exp)


def _sgu_bwd(dproj, gu, dsgu, ln_g, ln_b, w_s, bias_exp):
    t = gu.shape[0]
    nchunks = t // SGU_CHUNK
    e = _head_sum_matrix()

    def body(dp_in, uv_ref, ds_ref, g_ref, b_ref, w_ref, bias_ref, e_ref, out_ref, dw_ref, dbias_ref, dg_ref, db_ref):
        n = pl.program_id(0)

        @pl.when(n == 0)
        def _():
            dw_ref[...] = jnp.zeros(dw_ref.shape, F32)
            dbias_ref[...] = jnp.zeros(dbias_ref.shape, F32)
            dg_ref[...] = jnp.zeros(dg_ref.shape, F32)
            db_ref[...] = jnp.zeros(db_ref.shape, F32)

        uv = uv_ref[...]
        g = g_ref[...]
        cdf, u, xhat, rs, mixed, ws, vlo, vhi, tril, upper = _sgu_core(uv, g, b_ref[...], w_ref, bias_ref[...])
        dsg = ds_ref[...]
        du = dsg * mixed
        dmixed = dsg * u
        dbias_ref[...] += dmixed
        dvn = []
        for pr in range(SGU_W // LANES):
            sl = slice(pr * LANES, (pr + 1) * LANES)
            dm2 = dmixed[:, sl]
            dlo = jnp.where(upper, 0.0, dm2).astype(BF16)
            dhi = jnp.where(upper, dm2, 0.0).astype(BF16)
            w0, w1 = ws[pr]
            dvn.append(lax.dot_general(w0, dlo, (TN, ((), ())), preferred_element_type=F32)
                       + lax.dot_general(w1, dhi, (TN, ((), ())), preferred_element_type=F32))
            dw0 = lax.dot_general(dlo, vlo[pr], (NT, ((), ())), preferred_element_type=F32)
            dw1 = lax.dot_general(dhi, vhi[pr], (NT, ((), ())), preferred_element_type=F32)
            dw_ref[2 * pr] += jnp.where(tril, dw0, 0.0)
            dw_ref[2 * pr + 1] += jnp.where(tril, dw1, 0.0)
        dvn = jnp.concatenate(dvn, axis=1)
        dg_ref[...] += jnp.sum(dvn * xhat, axis=0, keepdims=True)
        db_ref[...] += jnp.sum(dvn, axis=0, keepdims=True)
        dxh = dvn * g
        dv = rs * (dxh - jnp.mean(dxh, axis=1, keepdims=True) - xhat * jnp.mean(dxh * xhat, axis=1, keepdims=True))
        dz = jnp.concatenate([du, dv], axis=1)
        dgelu = cdf + uv * (INV_SQRT_2PI * jnp.exp(-0.5 * uv * uv))
        out_ref[...] = (dz * dgelu).astype(BF16)

        @pl.when(n == nchunks - 1)
        def _():
            dbias_ref[...] = _group_sum(dbias_ref[...], e_ref[...])

    outs = pl.pallas_call(
        body, grid=(nchunks,),
        in_specs=[pl.BlockSpec(memory_space=pl.ANY), pl.BlockSpec((SGU_CHUNK, 2 * SGU_W), lambda n: (n, 2)),
                  pl.BlockSpec((SGU_CHUNK, SGU_W), lambda n: (n, 0)), _full((1, SGU_W)), _full((1, SGU_W)),
                  _full((SGU_GROUPS, SGU_CHUNK, SGU_CHUNK)), _full((SGU_CHUNK, SGU_W)), _full((ATTN_W, ATTN_W))],
        out_specs=[pl.BlockSpec((SGU_CHUNK, 2 * SGU_W), lambda n: (n, 2)), _full((SGU_GROUPS, SGU_CHUNK, SGU_CHUNK)),
                   _full((SGU_CHUNK, SGU_W)), _full((1, SGU_W)), _full((1, SGU_W))],
        out_shape=[jax.ShapeDtypeStruct(dproj.shape, BF16), jax.ShapeDtypeStruct((SGU_GROUPS, SGU_CHUNK, SGU_CHUNK), F32),
                   jax.ShapeDtypeStruct((SGU_CHUNK, SGU_W), F32), jax.ShapeDtypeStruct((1, SGU_W), F32),
                   jax.ShapeDtypeStruct((1, SGU_W), F32)],
        input_output_aliases={0: 0},
        compiler_params=_cparams(1), name="sgu_bwd")(dproj, gu, dsgu, ln_g, ln_b, w_s, bias_exp, e)
    return outs


def _merge_fwd(attn, sgu, gu, x, w_pa, w_ps, w_out, g2):
    t = x.shape[0]
    tm = min(t, 256)

    def body(a_ref, s_ref, ga_ref, gb_ref, x_ref, wpa, wps, wo, g_ref, pa_ref, ps_ref, m_ref, x1_ref, h2_ref):
        pa = jnp.dot(a_ref[...], wpa[...], preferred_element_type=F32)
        ps = jnp.dot(s_ref[...], wps[...], preferred_element_type=F32)
        merged = (_sigmoid(ga_ref[...]) * pa + _sigmoid(gb_ref[...]) * ps).astype(BF16)
        x1 = x_ref[...] + jnp.dot(merged, wo[...], preferred_element_type=F32)
        xhat, _ = _rms_stats(x1)
        pa_ref[...] = pa
        ps_ref[...] = ps
        m_ref[...] = merged
        x1_ref[...] = x1
        h2_ref[...] = (xhat * g_ref[...]).astype(BF16)

    half = pl.BlockSpec((tm, ATTN_W), lambda i: (i, 0))
    full = pl.BlockSpec((tm, D_MODEL), lambda i: (i, 0))
    return pl.pallas_call(
        body, grid=(t // tm,),
        in_specs=[half, half, pl.BlockSpec((tm, D_MODEL), lambda i: (i, 0)), pl.BlockSpec((tm, D_MODEL), lambda i: (i, 1)),
                  full, _full((ATTN_W, D_MODEL)), _full((SGU_W, D_MODEL)), _full((D_MODEL, D_MODEL)), _full((1, D_MODEL))],
        out_specs=[full] * 5,
        out_shape=[jax.ShapeDtypeStruct((t, D_MODEL), F32), jax.ShapeDtypeStruct((t, D_MODEL), F32),
                   jax.ShapeDtypeStruct((t, D_MODEL), BF16), jax.ShapeDtypeStruct((t, D_MODEL), F32),
                   jax.ShapeDtypeStruct((t, D_MODEL), BF16)],
        compiler_params=_cparams(1), name="merge_fwd")(attn, sgu, gu, gu, x, w_pa, w_ps, w_out, g2)


def _merge_bwd(dproj, dx1b, gu, pa, ps, w_pa, w_ps, w_out):
    t = dx1b.shape[0]
    tm = min(t, 256)

    def body(dp_in, d_ref, ga_ref, gb_ref, pa_ref, ps_ref, wpa, wps, wo, out_ref, dpa_ref, dps_ref, da_ref, dsg_ref):
        dm = lax.dot_general(d_ref[...], wo[...], (NT, ((), ())), preferred_element_type=F32)
        sa, sb = _sigmoid(ga_ref[...]), _sigmoid(gb_ref[...])
        dpa = (dm * sa).astype(BF16)
        dps = (dm * sb).astype(BF16)
        out_ref[:, 0:D_MODEL] = (dm * pa_ref[...] * sa * (1.0 - sa)).astype(BF16)
        out_ref[:, D_MODEL:2 * D_MODEL] = (dm * ps_ref[...] * sb * (1.0 - sb)).astype(BF16)
        dpa_ref[...] = dpa
        dps_ref[...] = dps
        da_ref[...] = lax.dot_general(dpa, wpa[...], (NT, ((), ())), preferred_element_type=F32)
        dsg_ref[...] = lax.dot_general(dps, wps[...], (NT, ((), ())), preferred_element_type=F32)

    half = pl.BlockSpec((tm, ATTN_W), lambda i: (i, 0))
    full = pl.BlockSpec((tm, D_MODEL), lambda i: (i, 0))
    return pl.pallas_call(
        body, grid=(t // tm,),
        in_specs=[pl.BlockSpec(memory_space=pl.ANY), full, pl.BlockSpec((tm, D_MODEL), lambda i: (i, 0)),
                  pl.BlockSpec((tm, D_MODEL), lambda i: (i, 1)), full, full,
                  _full((ATTN_W, D_MODEL)), _full((SGU_W, D_MODEL)), _full((D_MODEL, D_MODEL))],
        out_specs=[pl.BlockSpec((tm, 2 * D_MODEL), lambda i: (i, 0)), full, full, half, half],
        out_shape=[jax.ShapeDtypeStruct(dproj.shape, BF16), jax.ShapeDtypeStruct((t, D_MODEL), BF16),
                   jax.ShapeDtypeStruct((t, D_MODEL), BF16), jax.ShapeDtypeStruct((t, ATTN_W), F32),
                   jax.ShapeDtypeStruct((t, SGU_W), F32)],
        input_output_aliases={0: 0},
        compiler_params=_cparams(1), name="merge_bwd")(dproj, dx1b, gu, gu, pa, ps, w_pa, w_ps, w_out)


def _ffn_fwd(h2, w_g, w_u):
    t = h2.shape[0]
    tm = min(t, 512)
    a_spec = pl.BlockSpec((tm, D_MODEL), lambda i, j, k: (i, 0))
    w_spec = pl.BlockSpec((None, D_MODEL, FF_SHARD), lambda i, j, k: (j, 0, 0))
    o_spec = pl.BlockSpec((None, tm, FF_SHARD), lambda i, j, k: (j, i, 0))
    shp = (N_CHIPS, t, FF_SHARD)

    def epi_a(acc, e, o, r, ids):
        o[0][...] = acc

    a = _mm("ffn_gate", (t // tm, N_CHIPS, 1), [(h2, a_spec, w_g, w_spec)], NN, (tm, FF_SHARD), epi_a,
            outs=[(shp, F32, o_spec)])[0]

    def epi_b(acc, e, o, r, ids):
        av = e[0][...]
        o[0][...] = acc
        o[1][...] = (av * _sigmoid(av) * acc).astype(BF16)

    b, ff = _mm("ffn_up", (t // tm, N_CHIPS, 1), [(h2, a_spec, w_u, w_spec)], NN, (tm, FF_SHARD), epi_b,
                extras=[(a, o_spec)], outs=[(shp, F32, o_spec), (shp, BF16, o_spec)])
    return a, b, ff


def _ffn_down_loss(ff, w_d, x1, tgt, gf):
    t = x1.shape[0]
    tm = min(t, 512)

    def epi(acc, e, o, r, ids):
        x2 = e[0][...] + acc
        g = e[2][...]
        xhat, rr = _rms_stats(x2)
        diff = xhat * g - e[1][...]
        rows = jnp.sum(diff * diff, axis=1, keepdims=True)
        r[0][...] += jnp.broadcast_to(jnp.sum(rows, axis=0, keepdims=True) * (0.5 / D_MODEL), (1, LANES))
        dy = diff * (1.0 / D_MODEL)
        r[1][...] += jnp.sum(dy * xhat, axis=0, keepdims=True)
        dx2 = _rms_bwd(dy, xhat, rr, g)
        o[0][...] = dx2
        o[1][...] = dx2.astype(BF16)

    row = pl.BlockSpec((tm, D_MODEL), lambda i, j, k: (i, 0))
    return _mm("ffn_down_loss", (t // tm, 1, N_CHIPS),
               [(ff, pl.BlockSpec((None, tm, FF_SHARD), lambda i, j, k: (k, i, 0)),
                 w_d, pl.BlockSpec((None, FF_SHARD, D_MODEL), lambda i, j, k: (k, 0, 0)))],
               NN, (tm, D_MODEL), epi,
               extras=[(x1, row), (tgt, row), (gf, pl.BlockSpec((1, D_MODEL), lambda i, j, k: (0, 0)))],
               outs=[((t, D_MODEL), F32, row), ((t, D_MODEL), BF16, row)],
               reds=[(1, LANES), (1, D_MODEL)])


def _ffn_bwd_act(dx2b, w_d, a, b):
    t = dx2b.shape[0]
    tm = min(t, 512)
    o_spec = pl.BlockSpec((None, tm, FF_SHARD), lambda i, j, k: (j, i, 0))
    shp = (N_CHIPS, t, FF_SHARD)

    def epi(acc, e, o, r, ids):
        av, bv = e[0][...], e[1][...]
        sg = _sigmoid(av)
        o[0][...] = (acc * bv * (sg * (1.0 + av * (1.0 - sg)))).astype(BF16)
        o[1][...] = (acc * (av * sg)).astype(BF16)

    return _mm("ffn_bwd_act", (t // tm, N_CHIPS, 1),
               [(dx2b, pl.BlockSpec((tm, D_MODEL), lambda i, j, k: (i, 0)),
                 w_d, pl.BlockSpec((None, FF_SHARD, D_MODEL), lambda i, j, k: (j, 0, 0)))],
               NT, (tm, FF_SHARD), epi, extras=[(a, o_spec), (b, o_spec)],
               outs=[(shp, BF16, o_spec), (shp, BF16, o_spec)])


def _ffn_bwd_in(da, db, w_g, w_u, x1, dx2, g2):
    t = x1.shape[0]
    tm = min(t, 512)

    def epi(acc, e, o, r, ids):
        xhat, rr = _rms_stats(e[0][...])
        r[0][...] += jnp.sum(acc * xhat, axis=0, keepdims=True)
        dx1 = e[1][...] + _rms_bwd(acc, xhat, rr, e[2][...])
        o[0][...] = dx1
        o[1][...] = dx1.astype(BF16)

    a_spec = pl.BlockSpec((None, tm, FF_SHARD), lambda i, j, k: (k, i, 0))
    w_spec = pl.BlockSpec((None, D_MODEL, FF_SHARD), lambda i, j, k: (k, 0, 0))
    row = pl.BlockSpec((tm, D_MODEL), lambda i, j, k: (i, 0))
    return _mm("ffn_bwd_in", (t // tm, 1, N_CHIPS), [(da, a_spec, w_g, w_spec), (db, a_spec, w_u, w_spec)],
               NT, (tm, D_MODEL), epi,
               extras=[(x1, row), (dx2, row), (g2, pl.BlockSpec((1, D_MODEL), lambda i, j, k: (0, 0)))],
               outs=[((t, D_MODEL), F32, row), ((t, D_MODEL), BF16, row)], reds=[(1, D_MODEL)])


def _in_proj_bwd(dproj, w_p, x, dx1, g1):
    t = x.shape[0]
    tm, tk = min(t, 512), 1920

    def epi(acc, e, o, r, ids):
        xhat, rr = _rms_stats(e[0][...])
        r[0][...] += jnp.sum(acc * xhat, axis=0, keepdims=True)
        o[0][...] = e[1][...] + _rms_bwd(acc, xhat, rr, e[2][...])

    row = pl.BlockSpec((tm, D_MODEL), lambda i, j, k: (i, 0))
    return _mm("in_proj_bwd", (t // tm, 1, IN_COLS // tk),
               [(dproj, pl.BlockSpec((tm, tk), lambda i, j, k: (i, k)), w_p, pl.BlockSpec((D_MODEL, tk), lambda i, j, k: (0, k)))],
               NT, (tm, D_MODEL), epi,
               extras=[(x, row), (dx1, row), (g1, pl.BlockSpec((1, D_MODEL), lambda i, j, k: (0, 0)))],
               outs=[((t, D_MODEL), F32, row)], reds=[(1, D_MODEL)])


def _epi_bf16(acc, e, o, r, ids):
    o[0][...] = acc.astype(BF16)


def _wgrad_2d(name, a, b, tm, tn, tk=512):
    t, k1 = a.shape
    n = b.shape[1]
    tk = min(t, tk)
    return _mm(name, (k1 // tm, n // tn, t // tk),
               [(a, pl.BlockSpec((tk, tm), lambda i, j, k: (k, i)), b, pl.BlockSpec((tk, tn), lambda i, j, k: (k, j)))],
               TN, (tm, tn), _epi_bf16, outs=[((k1, n), BF16, pl.BlockSpec((tm, tn), lambda i, j, k: (i, j)))])[0]


def _wgrad_ff_in(name, h2, da, tk=512):
    t = h2.shape[0]
    tk = min(t, tk)
    return _mm(name, (N_CHIPS, 1, t // tk),
               [(h2, pl.BlockSpec((tk, D_MODEL), lambda i, j, k: (k, 0)),
                 da, pl.BlockSpec((None, tk, FF_SHARD), lambda i, j, k: (i, k, 0)))],
               TN, (D_MODEL, FF_SHARD), _epi_bf16,
               outs=[((N_CHIPS, D_MODEL, FF_SHARD), BF16, pl.BlockSpec((None, D_MODEL, FF_SHARD), lambda i, j, k: (i, 0, 0)))])[0]


def _wgrad_ff_down(ff, dx2b, tk=512):
    t = dx2b.shape[0]
    tk = min(t, tk)
    return _mm("wgrad_ffn_down", (N_CHIPS, 1, t // tk),
               [(ff, pl.BlockSpec((None, tk, FF_SHARD), lambda i, j, k: (i, k, 0)),
                 dx2b, pl.BlockSpec((tk, D_MODEL), lambda i, j, k: (k, 0)))],
               TN, (FF_SHARD, D_MODEL), _epi_bf16,
               outs=[((N_CHIPS, FF_SHARD, D_MODEL), BF16, pl.BlockSpec((None, FF_SHARD, D_MODEL), lambda i, j, k: (i, 0, 0)))])[0]


def _local_step(x, pos_col, tgt, g1, ln_g, ln_b, w_s, b_s, g2, gf, w_p, w_pa, w_ps, w_out, w_g, w_u, w_d):
    cos_t, sin_t = _rope_table(pos_col)
    bias_exp = jnp.repeat(jnp.transpose(b_s), SGU_W // SGU_GROUPS, axis=1)

    h = _norm_fwd(x, g1)
    gu, qkv = _in_proj(h, w_p, cos_t, sin_t)
    os_, ls_ = [], []
    for g, dil in enumerate(DILATIONS):
        o, lse = _attn_fwd(qkv, g, dil)
        os_.append(o)
        ls_.append(lse)
    attn = _combine_fwd(os_, ls_)
    sgu = _sgu_fwd(gu, ln_g, ln_b, w_s, bias_exp)
    pa, ps, merged, x1, h2 = _merge_fwd(attn, sgu, gu, x, w_pa, w_ps, w_out, g2)
    a, b, ff = _ffn_fwd(h2, w_g, w_u)
    dx2, dx2b, loss, dgf = _ffn_down_loss(ff, w_d, x1, tgt, gf)

    da, db = _ffn_bwd_act(dx2b, w_d, a, b)
    dw_d = _wgrad_ff_down(ff, dx2b)
    dx1, dx1b, dg2 = _ffn_bwd_in(da, db, w_g, w_u, x1, dx2, g2)
    dw_g = _wgrad_ff_in("wgrad_ffn_gate", h2, da)
    dw_u = _wgrad_ff_in("wgrad_ffn_up", h2, db)

    dproj = lax.empty((x.shape[0], IN_COLS), BF16)
    dproj, dpa, dps, dattn, dsgu = _merge_bwd(dproj, dx1b, gu, pa, ps, w_pa, w_ps, w_out)
    dw_out = _wgrad_2d("wgrad_out", merged, dx1b, D_MODEL, D_MODEL)
    dw_pa = _wgrad_2d("wgrad_proj_attn", attn, dpa, ATTN_W, D_MODEL)
    dw_ps = _wgrad_2d("wgrad_proj_sgu", sgu, dps, SGU_W, D_MODEL)
    dproj, dw_s, dbias, dln_g, dln_b = _sgu_bwd(dproj, gu, dsgu, ln_g, ln_b, w_s, bias_exp)
    dos, ccs = _combine_bwd(dattn, os_, ls_)
    for g, dil in enumerate(DILATIONS):
        dproj = _attn_bwd(dproj, qkv, dos[g], ccs[g], ls_[g], cos_t, sin_t, g, dil)
    dx, dg1 = _in_proj_bwd(dproj, w_p, x, dx1, g1)
    dw_p = _wgrad_2d("wgrad_in", h, dproj, D_MODEL, 1536)

    db_s = jnp.transpose(dbias[:, ::SGU_W // SGU_GROUPS])
    small = dict(loss=loss, norm1_g=dg1, sgu_ln_g=dln_g, sgu_ln_b=dln_b, w_spatial=dw_s, b_spatial=db_s,
                 norm2_g=dg2, final_g=dgf)
    big = dict(w_in=dw_p, w_proj_attn=dw_pa, w_proj_sgu=dw_ps, w_out=dw_out, w_ffn_gate=dw_g, w_ffn_up=dw_u,
               w_ffn_down=dw_d)
    return dx, big, small


def _ew(name, fn, ins, out_dtypes):
    shp = ins[0].shape
    rows, cols = shp
    tr = next((cand for cand in (256, 352, 128) if rows % cand == 0 and rows > cand), rows)

    def body(*refs):
        res = fn(*[r[...] for r in refs[:len(ins)]])
        for o_ref, v in zip(refs[len(ins):], res):
            o_ref[...] = v.astype(o_ref.dtype)

    spec = pl.BlockSpec((tr, cols), lambda i: (i, 0))
    return pl.pallas_call(
        body, grid=(rows // tr,), in_specs=[spec] * len(ins), out_specs=[spec] * len(out_dtypes),
        out_shape=[jax.ShapeDtypeStruct(shp, d) for d in out_dtypes],
        compiler_params=_cparams(1), name=name)(*ins)


def _adamw_math(g, w, m, v):
    m = ADAM_B1 * m + (1.0 - ADAM_B1) * g
    v = ADAM_B2 * v + (1.0 - ADAM_B2) * (g * g)
    m_hat = m / (1.0 - ADAM_B1 ** ADAM_STEP)
    v_hat = v / (1.0 - ADAM_B2 ** ADAM_STEP)
    delta = -ADAM_LR * (m_hat / (jnp.sqrt(v_hat) + ADAM_EPS) + ADAM_WD * w)
    return delta, m, v


def _adamw(name, g, w, m, v):
    return _ew(name, _adamw_math, [g, w, m, v], [F32, F32, F32])


VMEM_SPEC = pl.BlockSpec(memory_space=pltpu.VMEM)


def _for_row_chunks(rows, fn):
    ck = next(c for c in (64, 32, 16) if rows % c == 0)

    def step(i, carry):
        fn(pl.multiple_of(i * ck, ck), ck)
        return carry

    lax.fori_loop(0, rows // ck, step, 0)


def _place():
    x, y, c = lax.axis_index("x"), lax.axis_index("y"), lax.axis_index("c")
    chips = [(1 - x, y), (x, 1 - y), (1 - x, 1 - y)]
    return x, y, c, 2 * x + y, chips


def _rows(ref, start, size):
    if len(ref.shape) == 2:
        return ref.at[pl.ds(start, size), :]
    return ref.at[:, pl.ds(start, size), :]


def _comm_call(name, body, ins, out_shapes, scratch, n_remote):
    return pl.pallas_call(
        body, in_specs=[VMEM_SPEC] * len(ins), out_specs=[VMEM_SPEC] * len(out_shapes),
        out_shape=out_shapes,
        scratch_shapes=list(scratch) + [pltpu.SemaphoreType.DMA((n_remote,)), pltpu.SemaphoreType.DMA((n_remote,))],
        compiler_params=pltpu.CompilerParams(vmem_limit_bytes=VMEM_LIMIT), name=name)(*ins)


def _gather_weights(name, shards):
    nt = len(shards)

    def body(*refs):
        ins, outs = refs[:nt], refs[nt:2 * nt]
        send, recv = refs[2 * nt:]
        x, y, c, me, chips = _place()
        sibling = (x, y, 1 - c)
        firsts, passed, expects = [], [], []
        for t in range(nt):
            kh = ins[t].shape[0] // 2
            for j, chip in enumerate(chips):
                k = t * 3 + j
                theirs = 2 * chip[0] + chip[1]
                firsts.append(pltpu.make_async_remote_copy(
                    src_ref=_rows(ins[t], c * kh, kh), dst_ref=_rows(outs[t].at[me], c * kh, kh),
                    send_sem=send.at[k], recv_sem=recv.at[k], device_id=(*chip, c), device_id_type=MESH))
                landed = _rows(outs[t].at[theirs], c * kh, kh)
                expects.append(pltpu.make_async_remote_copy(
                    src_ref=landed, dst_ref=landed, send_sem=send.at[k], recv_sem=recv.at[k],
                    device_id=(*chip, c), device_id_type=MESH))
                passed.append(pltpu.make_async_remote_copy(
                    src_ref=landed, dst_ref=landed, send_sem=send.at[3 * nt + k], recv_sem=recv.at[3 * nt + k],
                    device_id=sibling, device_id_type=MESH))
        for cp in firsts:
            cp.start()
        for t in range(nt):
            mine = outs[t].at[me]

            def put(r0, ck, src=ins[t], dst=mine):
                dst[pl.ds(r0, ck), :] = src[pl.ds(r0, ck), :]

            _for_row_chunks(ins[t].shape[0], put)
        for k in range(3 * nt):
            expects[k].wait_recv()
            passed[k].start()
        for t in range(nt):
            kh = ins[t].shape[0] // 2
            for j, chip in enumerate(chips):
                k = t * 3 + j
                theirs = 2 * chip[0] + chip[1]
                other = _rows(outs[t].at[theirs], (1 - c) * kh, kh)
                pltpu.make_async_remote_copy(
                    src_ref=other, dst_ref=other, send_sem=send.at[3 * nt + k], recv_sem=recv.at[3 * nt + k],
                    device_id=sibling, device_id_type=MESH).wait_recv()
        for cp in firsts + passed:
            cp.wait_send()

    out_shapes = [jax.ShapeDtypeStruct((N_CHIPS,) + s.shape, s.dtype) for s in shards]
    return _comm_call(name, body, shards, out_shapes, [], 6 * nt)


def _pair_reduce(name, grads):
    nt = len(grads)

    def half(s):
        shp = list(s.shape)
        shp[-2] //= 2
        return tuple(shp)

    def body(*refs):
        ins, outs, got = refs[:nt], refs[nt:2 * nt], refs[2 * nt:3 * nt]
        send, recv = refs[3 * nt:]
        x, y, c, me, chips = _place()
        copies = []
        for t in range(nt):
            kh = ins[t].shape[-2] // 2
            rc = pltpu.make_async_remote_copy(
                src_ref=_rows(ins[t], (1 - c) * kh, kh), dst_ref=got[t], send_sem=send.at[t], recv_sem=recv.at[t],
                device_id=(x, y, 1 - c), device_id_type=MESH)
            rc.start()
            copies.append(rc)
        for t in range(nt):
            kh = ins[t].shape[-2] // 2
            copies[t].wait_recv()
            for lead in ([()] if len(ins[t].shape) == 2 else [(s,) for s in range(ins[t].shape[0])]):

                def add(r0, ck, lead=lead, src=ins[t], oth=got[t], dst=outs[t], kh=kh):
                    own = src[lead + (pl.ds(pl.multiple_of(c * kh + r0, ck), ck), slice(None))]
                    rows = lead + (pl.ds(r0, ck), slice(None))
                    dst[rows] = (own.astype(F32) + oth[rows].astype(F32)).astype(BF16)

                _for_row_chunks(kh, add)
        for rc in copies:
            rc.wait_send()

    shapes = [half(g) for g in grads]
    return _comm_call(name, body, grads, [jax.ShapeDtypeStruct(s, BF16) for s in shapes],
                      [pltpu.VMEM(s, BF16) for s in shapes], nt)


def _chip_reduce(name, sums):
    nt = len(sums)

    def cols(s):
        return s[2] if len(s) == 3 else s[1] // N_CHIPS

    def piece(ref, j):
        if len(ref.shape) == 3:
            return ref.at[j]
        n4 = ref.shape[1] // N_CHIPS
        return ref.at[:, pl.ds(j * n4, n4)]

    def body(*refs):
        ins, outs, slots = refs[:nt], refs[nt:2 * nt], refs[2 * nt:3 * nt]
        send, recv = refs[3 * nt:]
        x, y, c, me, chips = _place()
        sibling = (x, y, 1 - c)
        copies = []
        for t in range(nt):
            for j, chip in enumerate(chips):
                k = t * 3 + j
                rc = pltpu.make_async_remote_copy(
                    src_ref=piece(ins[t], 2 * chip[0] + chip[1]), dst_ref=slots[t].at[j], send_sem=send.at[k],
                    recv_sem=recv.at[k], device_id=(*chip, c), device_id_type=MESH)
                rc.start()
                copies.append(rc)
        handed = []
        for t in range(nt):
            kh, n4 = ins[t].shape[-2], outs[t].shape[1]
            for j in range(3):
                copies[t * 3 + j].wait_recv()
            for jj in range(N_CHIPS):

                @pl.when(me == jj)
                def _(jj=jj, src=ins[t], slot=slots[t], dst=outs[t], kh=kh, n4=n4):
                    def add(r0, ck):
                        rows = pl.ds(r0, ck)
                        own = src[jj, rows, :] if len(src.shape) == 3 else src[rows, jj * n4:(jj + 1) * n4]
                        acc = ((own.astype(F32) + slot[0, rows, :].astype(F32)) + slot[1, rows, :].astype(F32)) \
                            + slot[2, rows, :].astype(F32)
                        dst[pl.ds(pl.multiple_of(c * kh + r0, ck), ck), :] = acc

                    _for_row_chunks(kh, add)

            rc = pltpu.make_async_remote_copy(
                src_ref=_rows(outs[t], c * kh, kh), dst_ref=_rows(outs[t], c * kh, kh), send_sem=send.at[3 * nt + t],
                recv_sem=recv.at[3 * nt + t], device_id=sibling, device_id_type=MESH)
            rc.start()
            handed.append(rc)
        for t in range(nt):
            kh = ins[t].shape[-2]
            other = _rows(outs[t], (1 - c) * kh, kh)
            pltpu.make_async_remote_copy(
                src_ref=other, dst_ref=other, send_sem=send.at[3 * nt + t], recv_sem=recv.at[3 * nt + t],
                device_id=sibling, device_id_type=MESH).wait_recv()
        for rc in copies + handed:
            rc.wait_send()

    out_shapes = [jax.ShapeDtypeStruct((2 * s.shape[-2], cols(s.shape)), F32) for s in sums]
    scratch = [pltpu.VMEM((3, s.shape[-2], cols(s.shape)), BF16) for s in sums]
    return _comm_call(name, body, sums, out_shapes, scratch, 4 * nt)


def _allreduce_small(buf):
    shp = buf.shape

    def body(in_ref, out_ref, pair_ref, slot_ref, send, recv):
        x, y, c, me, chips = _place()
        sibling = (x, y, 1 - c)
        first = pltpu.make_async_remote_copy(src_ref=in_ref, dst_ref=pair_ref, send_sem=send.at[0], recv_sem=recv.at[0],
                                             device_id=sibling, device_id_type=MESH)
        first.start()
        first.wait_recv()
        slot_ref[me] = in_ref[...] + pair_ref[...]
        copies = []
        for j, chip in enumerate(chips):
            theirs = 2 * chip[0] + chip[1]
            rc = pltpu.make_async_remote_copy(src_ref=slot_ref.at[me], dst_ref=slot_ref.at[me], send_sem=send.at[1 + j],
                                              recv_sem=recv.at[1 + j], device_id=(*chip, c), device_id_type=MESH)
            rc.start()
            arrive = pltpu.make_async_remote_copy(src_ref=slot_ref.at[theirs], dst_ref=slot_ref.at[theirs],
                                                  send_sem=send.at[1 + j], recv_sem=recv.at[1 + j],
                                                  device_id=(*chip, c), device_id_type=MESH)
            copies.append((rc, arrive))
        for rc, arrive in copies:
            arrive.wait_recv()
        out_ref[...] = ((slot_ref[0] + slot_ref[1]) + slot_ref[2]) + slot_ref[3]
        first.wait_send()
        for rc, arrive in copies:
            rc.wait_send()

    vm = pl.BlockSpec(memory_space=pltpu.VMEM)
    return pl.pallas_call(
        body, in_specs=[vm], out_specs=vm, out_shape=jax.ShapeDtypeStruct(shp, F32),
        scratch_shapes=[pltpu.VMEM(shp, F32), pltpu.VMEM((N_CHIPS,) + shp, F32),
                        pltpu.SemaphoreType.DMA((4,)), pltpu.SemaphoreType.DMA((4,))],
        compiler_params=pltpu.CompilerParams(vmem_limit_bytes=VMEM_LIMIT),
        name="allreduce_small")(buf)


SMALL_ORDER = ("norm1_g", "norm2_g", "final_g", "sgu_ln_g", "sgu_ln_b", "b_spatial", "loss", "w_spatial")
SMALL_SIZES = dict(norm1_g=1024, norm2_g=1024, final_g=1024, sgu_ln_g=512, sgu_ln_b=512, b_spatial=1024, loss=LANES,
                   w_spatial=SGU_GROUPS * SGU_CHUNK * SGU_CHUNK)
SMALL_ROWS = 1072


def _pack_small(parts):
    flat = [jnp.reshape(parts[n].astype(F32), (-1,)) for n in SMALL_ORDER]
    used = sum(SMALL_SIZES[n] for n in SMALL_ORDER)
    flat.append(jnp.zeros((SMALL_ROWS * LANES - used,), F32))
    return jnp.concatenate(flat).reshape(SMALL_ROWS, LANES)


def _unpack_small(buf, shapes):
    flat = buf.reshape(-1)
    out, off = {}, 0
    for n in SMALL_ORDER:
        sz = SMALL_SIZES[n]
        if n in shapes:
            out[n] = flat[off:off + sz].reshape(shapes[n])
        off += sz
    return out


BIG = ("w_in", "w_proj_attn", "w_proj_sgu", "w_out", "w_ffn_gate", "w_ffn_up", "w_ffn_down")
COMM_GROUPS = (("w_in",), ("w_proj_attn", "w_proj_sgu", "w_out", "w_ffn_gate", "w_ffn_up", "w_ffn_down"))
WEIGHTS = ("norm1_g", "w_in", "sgu_ln_g", "sgu_ln_b", "w_spatial", "b_spatial", "w_proj_attn", "w_proj_sgu", "w_out",
           "norm2_g", "w_ffn_gate", "w_ffn_up", "w_ffn_down", "final_g")


def _cols_from_chips(g):
    return jnp.transpose(g, (1, 0, 2)).reshape(g.shape[1], N_CHIPS * g.shape[2])


def _permute_cols(w, perm):
    return jnp.concatenate([w[:, 512 * b:512 * (b + 1)] for b in perm], axis=1)


def kernel(x, positions, norm1_g, w_in, sgu_ln_g, sgu_ln_b, w_spatial, b_spatial, w_proj_attn, w_proj_sgu, w_out, norm2_g, w_ffn_gate, w_ffn_up, w_ffn_down, final_g, loss_target, m_norm1_g, m_w_in, m_sgu_ln_g, m_sgu_ln_b, m_w_spatial, m_b_spatial, m_w_proj_attn, m_w_proj_sgu, m_w_out, m_norm2_g, m_w_ffn_gate, m_w_ffn_up, m_w_ffn_down, m_final_g, v_norm1_g, v_w_in, v_sgu_ln_g, v_sgu_ln_b, v_w_spatial, v_b_spatial, v_w_proj_attn, v_w_proj_sgu, v_w_out, v_norm2_g, v_w_ffn_gate, v_w_ffn_up, v_w_ffn_down, v_final_g):
    w = dict(norm1_g=norm1_g, w_in=w_in, sgu_ln_g=sgu_ln_g, sgu_ln_b=sgu_ln_b, w_spatial=w_spatial, b_spatial=b_spatial,
             w_proj_attn=w_proj_attn, w_proj_sgu=w_proj_sgu, w_out=w_out, norm2_g=norm2_g, w_ffn_gate=w_ffn_gate,
             w_ffn_up=w_ffn_up, w_ffn_down=w_ffn_down, final_g=final_g)
    m = dict(norm1_g=m_norm1_g, w_in=m_w_in, sgu_ln_g=m_sgu_ln_g, sgu_ln_b=m_sgu_ln_b, w_spatial=m_w_spatial,
             b_spatial=m_b_spatial, w_proj_attn=m_w_proj_attn, w_proj_sgu=m_w_proj_sgu, w_out=m_w_out, norm2_g=m_norm2_g,
             w_ffn_gate=m_w_ffn_gate, w_ffn_up=m_w_ffn_up, w_ffn_down=m_w_ffn_down, final_g=m_final_g)
    v = dict(norm1_g=v_norm1_g, w_in=v_w_in, sgu_ln_g=v_sgu_ln_g, sgu_ln_b=v_sgu_ln_b, w_spatial=v_w_spatial,
             b_spatial=v_b_spatial, w_proj_attn=v_w_proj_attn, w_proj_sgu=v_w_proj_sgu, w_out=v_w_out, norm2_g=v_norm2_g,
             w_ffn_gate=v_w_ffn_gate, w_ffn_up=v_w_ffn_up, w_ffn_down=v_w_ffn_down, final_g=v_final_g)
    t = x.shape[1]

    shards = {n: _ew(f"cast_{n}", lambda a: (a,), [w[n][0]], [BF16])[0] for n in BIG}
    gath = {}
    for i, grp in enumerate(COMM_GROUPS):
        gath.update(zip(grp, _gather_weights(f"gather_weights_{i}", [shards[n] for n in grp])))
    w_p = _permute_cols(_cols_from_chips(gath["w_in"]), PERM)
    w_pa = _cols_from_chips(gath["w_proj_attn"])
    w_ps = _cols_from_chips(gath["w_proj_sgu"])
    w_o = gath["w_out"].reshape(D_MODEL, D_MODEL)

    dx, big, small = _local_step(
        x[0], positions.reshape(t, 1), loss_target[0], norm1_g, sgu_ln_g, sgu_ln_b, w_spatial[0], b_spatial[0], norm2_g,
        final_g.reshape(1, D_MODEL), w_p, w_pa, w_ps, w_o, gath["w_ffn_gate"], gath["w_ffn_up"], gath["w_ffn_down"])

    big["w_in"] = _permute_cols(big["w_in"], INV_PERM)
    big["w_out"] = big["w_out"].reshape(N_CHIPS, D_MODEL // N_CHIPS, D_MODEL)
    grads = {}
    for i, grp in enumerate(COMM_GROUPS):
        sums = _pair_reduce(f"rs_pair_reduce_{i}", [big[n] for n in grp])
        grads.update(zip(grp, _chip_reduce(f"rs_chip_reduce_{i}", sums)))

    small_shapes = {n: w[n].shape for n in SMALL_ORDER if n != "loss"}
    reduced = _allreduce_small(_pack_small(small))
    grads.update(_unpack_small(reduced, small_shapes))
    loss = reduced.reshape(-1)[sum(SMALL_SIZES[n] for n in SMALL_ORDER[:SMALL_ORDER.index("loss")])]

    delta, new_m, new_v = {}, {}, {}
    for n in BIG:
        shp = w[n].shape
        grads[n] = grads[n].reshape(shp)
        d_, m_, v_ = _adamw(f"adamw_{n}", grads[n][0], w[n][0], m[n][0], v[n][0])
        delta[n], new_m[n], new_v[n] = d_.reshape(shp), m_.reshape(shp), v_.reshape(shp)
    zero_loss = dict(loss=jnp.zeros((LANES,), F32))
    d_, m_, v_ = _adamw("adamw_small", reduced, _pack_small({**w, **zero_loss}), _pack_small({**m, **zero_loss}),
                        _pack_small({**v, **zero_loss}))
    delta.update(_unpack_small(d_, small_shapes))
    new_m.update(_unpack_small(m_, small_shapes))
    new_v.update(_unpack_small(v_, small_shapes))

    return (loss, dx.reshape(x.shape), *[grads[n] for n in WEIGHTS], *[delta[n] for n in WEIGHTS],
            *[new_m[n] for n in WEIGHTS], *[new_v[n] for n in WEIGHTS])
```

```python
import functools

import numpy as np
import jax
import jax.numpy as jnp
from jax import lax
from jax.experimental import pallas as pl
from jax.experimental.pallas import tpu as pltpu

F32, BF16 = jnp.float32, jnp.bfloat16
MESH = pl.DeviceIdType.MESH

D_MODEL = 1024
HEAD_DIM = 64
ATTN_W = 512
DILATIONS = (1, 4, 16)
BLK = 128
ROPE_DIM = 16
ROPE_THETA = 500000.0
SGU_W = 512
SGU_CHUNK = 128
SGU_GROUPS = 8
D_FF = 2816
N_CHIPS = 4
FF_SHARD = D_FF // N_CHIPS
IN_COLS = 7680
EPS = 1e-6
NEG = -1e30
LANES = 128
VMEM_LIMIT = 52 * 1024 * 1024

ADAM_LR, ADAM_B1, ADAM_B2, ADAM_EPS, ADAM_WD, ADAM_STEP = 0.001, 0.9, 0.999, 1e-08, 0.01, 10

PERM = (11, 12, 13, 14, 9, 10, 0, 3, 6, 1, 4, 7, 2, 5, 8)
INV_PERM = tuple(int(i) for i in np.argsort(np.array(PERM)))


def _cparams(ngrid):
    return pltpu.CompilerParams(dimension_semantics=("arbitrary",) * ngrid, vmem_limit_bytes=VMEM_LIMIT)


def _full(shape):
    return pl.BlockSpec(shape, lambda *_: (0,) * len(shape))


NN = ((1,), (0,))
NT = ((1,), (1,))
TN = ((0,), (0,))


def _mm(name, grid, pairs, dims, acc_shape, epi, *, extras=(), outs=(), reds=(), aliases=None, k_ranges=None,
        scratch=()):
    nk = grid[-1]
    npair, nex, nout, nred = len(pairs), len(extras), len(outs), len(reds)

    def body(*refs):
        a_refs = refs[:npair]
        b_refs = refs[npair:2 * npair]
        p0 = 2 * npair
        o_refs = refs[p0 + nex:p0 + nex + nout]
        r_refs = refs[p0 + nex + nout:p0 + nex + nout + nred]
        e_refs = tuple(refs[p0:p0 + nex]) + tuple(refs[p0 + nex + nout + nred:p0 + nex + nout + nred + len(scratch)])
        ids = [pl.program_id(a) for a in range(len(grid))]
        k = ids[-1]
        if nred:
            first = ids[0] == 0
            for v in ids[1:]:
                first = first & (v == 0)

            @pl.when(first)
            def _():
                for r in r_refs:
                    r[...] = jnp.zeros(r.shape, r.dtype)

        def dot(a_ref, b_ref):
            return lax.dot_general(a_ref[...], b_ref[...], (dims, ((), ())), preferred_element_type=F32)

        if k_ranges is not None:
            acc_ref = refs[-1]

            @pl.when(k == 0)
            def _():
                acc_ref[...] = jnp.zeros(acc_ref.shape, F32)

            for a_ref, b_ref, (k0, kn) in zip(a_refs, b_refs, k_ranges):

                @pl.when((k >= k0) & (k < k0 + kn))
                def _(a_ref=a_ref, b_ref=b_ref):
                    acc_ref[...] += dot(a_ref, b_ref)

            @pl.when(k == nk - 1)
            def _():
                epi(acc_ref[...], e_refs, o_refs, r_refs, ids)

            return

        part = None
        for a_ref, b_ref in zip(a_refs, b_refs):
            d = dot(a_ref, b_ref)
            part = d if part is None else part + d
        if nk == 1:
            epi(part, e_refs, o_refs, r_refs, ids)
        else:
            acc_ref = refs[-1]

            @pl.when(k == 0)
            def _():
                acc_ref[...] = part

            @pl.when(k > 0)
            def _():
                acc_ref[...] += part

            @pl.when(k == nk - 1)
            def _():
                epi(acc_ref[...], e_refs, o_refs, r_refs, ids)

    in_specs = [p[1] for p in pairs] + [p[3] for p in pairs] + [e[1] for e in extras]
    args = [p[0] for p in pairs] + [p[2] for p in pairs] + [e[0] for e in extras]
    out_shape = [jax.ShapeDtypeStruct(o[0], o[1]) for o in outs] + [jax.ShapeDtypeStruct(r, F32) for r in reds]
    out_specs = [o[2] for o in outs] + [_full(r) for r in reds]
    scratch_shapes = list(scratch) + ([pltpu.VMEM(acc_shape, F32)] if nk > 1 else [])
    return pl.pallas_call(
        body, grid=grid, in_specs=in_specs, out_specs=out_specs, out_shape=out_shape, scratch_shapes=scratch_shapes,
        input_output_aliases=aliases or {}, compiler_params=_cparams(len(grid)), name=name)(*args)


def _lane_tile(t, width):
    n = width // LANES
    return t if n == 1 else jnp.concatenate([t] * n, axis=1)


def _rope(v, cos_w, sin_w):
    w = v.shape[1]
    lane = lax.broadcasted_iota(jnp.int32, v.shape, 1)
    partner = jnp.where((lane % HEAD_DIM) < ROPE_DIM // 2, pltpu.roll(v, w - ROPE_DIM // 2, axis=1),
                        pltpu.roll(v, ROPE_DIM // 2, axis=1))
    return v * cos_w + partner * sin_w


def _sigmoid(v):
    return 1.0 / (1.0 + jnp.exp(-v))


def _rms_stats(v):
    r = lax.rsqrt(jnp.mean(v * v, axis=-1, keepdims=True) + EPS)
    return v * r, r


def _rms_bwd(dy, xhat, r, g):
    dxh = dy * g
    return r * (dxh - xhat * jnp.mean(dxh * xhat, axis=-1, keepdims=True))


def _head_sum_matrix():
    idx = np.arange(ATTN_W) // HEAD_DIM
    return jnp.asarray((idx[:, None] == idx[None, :]).astype(np.float32), dtype=BF16)


def _group_sum(v, e):
    hi = v.astype(BF16)
    lo = (v - hi.astype(F32)).astype(BF16)
    return jnp.dot(hi, e, preferred_element_type=F32) + jnp.dot(lo, e, preferred_element_type=F32)


TILE = 512


def _to_slabs(slab_ref, v):
    for cs in range(slab_ref.shape[0]):
        slab_ref[cs] = v[:, cs * LANES:(cs + 1) * LANES]


def _from_slabs(slab_ref):
    return jnp.concatenate([slab_ref[cs] for cs in range(slab_ref.shape[0])], axis=1)


def _class_rows(slab_ref, r, dil):
    n = slab_ref.shape[1] // dil
    return jnp.concatenate([slab_ref.at[cs][pl.ds(r, n, stride=dil), :] for cs in range(slab_ref.shape[0])], axis=1)


def _put_class_rows(slab_ref, r, dil, v):
    n = slab_ref.shape[1] // dil
    for cs in range(slab_ref.shape[0]):
        slab_ref.at[cs][pl.ds(r, n, stride=dil), :] = v[:, cs * LANES:(cs + 1) * LANES]


def _natural_from_group(slab_ref, grp_ref):
    dil = grp_ref.shape[0]
    for r in range(dil):
        _put_class_rows(slab_ref, r, dil, grp_ref[r].astype(F32))
    return _from_slabs(slab_ref)


def _group_from_natural(slab_ref, grp_ref, v):
    dil = grp_ref.shape[0]
    _to_slabs(slab_ref, v)
    for r in range(dil):
        grp_ref[r] = _class_rows(slab_ref, r, dil).astype(grp_ref.dtype)


def _group_spec(dil, tile, width):
    return pl.BlockSpec((dil, tile // dil, width), lambda i, *_: (0, i, 0))


def _slabs(tile, width):
    return pltpu.VMEM((width // LANES, tile, LANES), F32)


def _rope_consts():
    lane = np.arange(LANES) % HEAD_DIM
    fi = lane % (ROPE_DIM // 2)
    invf = np.where(lane < ROPE_DIM, ROPE_THETA ** (-(2.0 * fi) / ROPE_DIM), 0.0)
    sgn = np.where(lane < ROPE_DIM // 2, -1.0, np.where(lane < ROPE_DIM, 1.0, 0.0))
    return (jnp.asarray(invf.astype(np.float32)).reshape(1, LANES), jnp.asarray(sgn.astype(np.float32)).reshape(1, LANES))


def _rope_table(name, pos_col):
    t = pos_col.shape[0]
    tm = min(t, 1024)
    invf, sgn = _rope_consts()

    def body(p_ref, f_ref, s_ref, c_out, s_out):
        ang = p_ref[...].astype(F32) * f_ref[...]
        c_out[...] = jnp.cos(ang)
        s_out[...] = jnp.sin(ang) * s_ref[...]

    return pl.pallas_call(
        body, grid=(t // tm,),
        in_specs=[pl.BlockSpec((tm, 1), lambda i: (i, 0)), _full((1, LANES)), _full((1, LANES))],
        out_specs=[pl.BlockSpec((tm, LANES), lambda i: (i, 0))] * 2,
        out_shape=[jax.ShapeDtypeStruct((t, LANES), F32)] * 2,
        compiler_params=_cparams(1), name=name)(pos_col, invf, sgn)


def _norm_fwd(x, g):
    t = x.shape[0]
    tile = min(t, TILE)

    def body(x_ref, g_ref, h0_ref, h1_ref, h2_ref, slab):
        xhat, _ = _rms_stats(x_ref[...])
        hn = xhat * g_ref[...]
        h0_ref[...] = hn.astype(BF16)
        _group_from_natural(slab, h1_ref, hn)
        for r in range(DILATIONS[2]):
            h2_ref[r] = _class_rows(slab, r, DILATIONS[2]).astype(BF16)

    nat = pl.BlockSpec((tile, D_MODEL), lambda i: (i, 0))
    return pl.pallas_call(
        body, grid=(t // tile,),
        in_specs=[nat, _full((1, D_MODEL))],
        out_specs=[nat] + [_group_spec(d, tile, D_MODEL) for d in DILATIONS[1:]],
        out_shape=[jax.ShapeDtypeStruct((t, D_MODEL), BF16)]
        + [jax.ShapeDtypeStruct((d, t // d, D_MODEL), BF16) for d in DILATIONS[1:]],
        scratch_shapes=[_slabs(tile, D_MODEL)],
        compiler_params=_cparams(1), name="norm1_fwd")(x, g)


GU_COLS = 3072
GROUP_COLS = 1536


def _in_proj(hs, w_p, tables):
    t = hs[0].shape[0]
    tm, tn = min(t, 512), 512

    def epi_plain(acc, e, o, r, ids):
        o[0][...] = acc

    gu = _mm("in_proj_gates_uv", (t // tm, GU_COLS // tn, 1),
             [(hs[0], pl.BlockSpec((tm, D_MODEL), lambda i, j, k: (i, 0)), w_p,
               pl.BlockSpec((D_MODEL, tn), lambda i, j, k: (0, j)))],
             NN, (tm, tn), epi_plain,
             outs=[((t, GU_COLS), F32, pl.BlockSpec((tm, tn), lambda i, j, k: (i, j)))])[0]

    def epi_qkv(acc, e, o, r, ids):
        role = ids[1]
        cos_w = _lane_tile(e[0][...], tn)
        sin_w = _lane_tile(e[1][...], tn)
        roped = _rope(acc, cos_w, sin_w) * jnp.where(role == 0, HEAD_DIM ** -0.5, 1.0)
        o[0][...] = jnp.where(role == 2, acc, roped).astype(BF16)

    qkvs = []
    for g in range(len(DILATIONS)):
        col0 = (GU_COLS + g * GROUP_COLS) // tn
        cos_t, sin_t = tables[g]
        qkvs.append(_mm(
            f"in_proj_qkv_g{g}", (t // tm, GROUP_COLS // tn, 1),
            [(hs[g].reshape(t, D_MODEL), pl.BlockSpec((tm, D_MODEL), lambda i, j, k: (i, 0)), w_p,
              pl.BlockSpec((D_MODEL, tn), lambda i, j, k, col0=col0: (0, j + col0)))],
            NN, (tm, tn), epi_qkv,
            extras=[(cos_t, pl.BlockSpec((tm, LANES), lambda i, j, k: (i, 0))),
                    (sin_t, pl.BlockSpec((tm, LANES), lambda i, j, k: (i, 0)))],
            outs=[((t, GROUP_COLS), BF16, pl.BlockSpec((tm, tn), lambda i, j, k: (i, j)))])[0])
    return gu, qkvs


def _attn_masks(n):
    row = lax.broadcasted_iota(jnp.int32, (BLK, 2 * BLK), 0)
    col = lax.broadcasted_iota(jnp.int32, (BLK, 2 * BLK), 1)
    diff = BLK + row - col
    valid = (diff >= 0) & (diff <= BLK) & ((col >= BLK) | (n > 0))
    upper = lax.broadcasted_iota(jnp.int32, (BLK, LANES), 1) >= HEAD_DIM
    return valid, upper


def _attn_fwd(qkv, g, dil):
    t = qkv.shape[0]
    length = t // dil
    nb = length // BLK
    view = qkv.reshape(dil, length, GROUP_COLS)

    def body(q_ref, kc_ref, kp_ref, vc_ref, vp_ref, o_ref, l_ref):
        n = pl.program_id(1)
        valid, upper = _attn_masks(n)
        for p in range(ATTN_W // LANES):
            sl = slice(p * LANES, (p + 1) * LANES)
            q2 = q_ref[:, sl]
            k2 = jnp.concatenate([kp_ref[:, sl], kc_ref[:, sl]], axis=0)
            v2 = jnp.concatenate([vp_ref[:, sl], vc_ref[:, sl]], axis=0)
            outs, lses = [], []
            for hh in (0, 1):
                sel = upper if hh else jnp.logical_not(upper)
                qm = jnp.where(sel, q2, jnp.zeros_like(q2))
                s = lax.dot_general(qm, k2, (NT, ((), ())), preferred_element_type=F32)
                s = jnp.where(valid, s, NEG)
                m = jnp.max(s, axis=1, keepdims=True)
                pe = jnp.exp(s - m)
                den = jnp.sum(pe, axis=1, keepdims=True)
                outs.append(jnp.dot(pe.astype(BF16), v2, preferred_element_type=F32) / den)
                lses.append(jnp.broadcast_to(m + jnp.log(den), (BLK, LANES)))
            o_ref[:, sl] = jnp.where(upper, outs[1], outs[0])
            l_ref[:, sl] = jnp.where(upper, lses[1], lses[0])

    cur = lambda part: pl.BlockSpec((None, BLK, ATTN_W), lambda r, n: (r, n, part))
    prev = lambda part: pl.BlockSpec((None, BLK, ATTN_W), lambda r, n: (r, jnp.maximum(n - 1, 0), part))
    out_spec = pl.BlockSpec((None, BLK, ATTN_W), lambda r, n: (r, n, 0))
    return pl.pallas_call(
        body, grid=(dil, nb),
        in_specs=[cur(0), cur(1), prev(1), cur(2), prev(2)],
        out_specs=[out_spec, out_spec],
        out_shape=[jax.ShapeDtypeStruct((dil, length, ATTN_W), F32)] * 2,
        compiler_params=_cparams(2), name=f"attn_fwd_g{g}")(view, view, view, view, view)


def _alphas(l0, l1, l2):
    m = jnp.maximum(jnp.maximum(l0, l1), l2)
    e0, e1, e2 = jnp.exp(l0 - m), jnp.exp(l1 - m), jnp.exp(l2 - m)
    inv = 1.0 / (e0 + e1 + e2)
    return e0 * inv, e1 * inv, e2 * inv


def _natural_group_values(o_refs, l_refs, slabs):
    os_ = [o_refs[0][0]] + [_natural_from_group(slabs[2 * g - 2], o_refs[g]) for g in (1, 2)]
    ls_ = [l_refs[0][0]] + [_natural_from_group(slabs[2 * g - 1], l_refs[g]) for g in (1, 2)]
    return os_, ls_


def _combine_fwd(os_, ls_):
    t = os_[0].shape[1]
    tile = min(t, TILE)

    def body(o0, o1, o2, l0, l1, l2, a_ref, *slabs):
        ov, lv = _natural_group_values((o0, o1, o2), (l0, l1, l2), slabs)
        a0, a1, a2 = _alphas(*lv)
        a_ref[...] = (a0 * ov[0] + a1 * ov[1] + a2 * ov[2]).astype(BF16)

    specs = [_group_spec(d, tile, ATTN_W) for d in DILATIONS]
    return pl.pallas_call(
        body, grid=(t // tile,), in_specs=specs * 2, out_specs=pl.BlockSpec((tile, ATTN_W), lambda i: (i, 0)),
        out_shape=jax.ShapeDtypeStruct((t, ATTN_W), BF16),
        scratch_shapes=[_slabs(tile, ATTN_W)] * 4,
        compiler_params=_cparams(1), name="combine_fwd")(*os_, *ls_)


def _combine_bwd(dattn, os_, ls_):
    t = dattn.shape[0]
    tile = min(t, TILE)
    e = _head_sum_matrix()

    def body(d_ref, o0, o1, o2, l0, l1, l2, e_ref, do0, do1, do2, c0, c1, c2, *slabs):
        ov, lv = _natural_group_values((o0, o1, o2), (l0, l1, l2), slabs)
        alphas = _alphas(*lv)
        d = d_ref[...]
        attn = alphas[0] * ov[0] + alphas[1] * ov[1] + alphas[2] * ov[2]
        s = _group_sum(d * attn, e_ref[...])
        do0[0] = (alphas[0] * d).astype(BF16)
        c0[0] = -alphas[0] * s
        for g, do_ref, c_ref in ((1, do1, c1), (2, do2, c2)):
            _group_from_natural(slabs[2 * g - 2], do_ref, alphas[g] * d)
            _group_from_natural(slabs[2 * g - 1], c_ref, -alphas[g] * s)

    specs = [_group_spec(d, tile, ATTN_W) for d in DILATIONS]
    shapes = [(d, t // d, ATTN_W) for d in DILATIONS]
    outs = pl.pallas_call(
        body, grid=(t // tile,),
        in_specs=[pl.BlockSpec((tile, ATTN_W), lambda i: (i, 0))] + specs * 2 + [_full((ATTN_W, ATTN_W))],
        out_specs=specs * 2,
        out_shape=[jax.ShapeDtypeStruct(s, BF16) for s in shapes] + [jax.ShapeDtypeStruct(s, F32) for s in shapes],
        scratch_shapes=[_slabs(tile, ATTN_W)] * 4,
        compiler_params=_cparams(1), name="combine_bwd")(dattn, *os_, *ls_, e)
    return outs[:3], outs[3:]


def _attn_bwd(qkv, do, cc, lse, cos_t, sin_t, g, dil):
    t = qkv.shape[0]
    length = t // dil
    nb = length // BLK
    qkv_v = qkv.reshape(dil, length, GROUP_COLS)
    cos_v, sin_v = (a.reshape(dil, length, LANES) for a in (cos_t, sin_t))
    scale = HEAD_DIM ** -0.5

    def body(q_ref, kc_ref, kp_ref, vc_ref, vp_ref, do_ref, c_ref, l_ref, cosc, sinc, cosp, sinp,
             out_ref, dq_s, dk_s, dv_s):
        n = pl.program_id(1)
        valid, upper = _attn_masks(n)
        lower = jnp.logical_not(upper)

        @pl.when(n < nb)
        def _():
            cos_c, sin_c = _lane_tile(cosc[...], ATTN_W), _lane_tile(sinc[...], ATTN_W)
            cos_p, sin_p = _lane_tile(cosp[...], ATTN_W), _lane_tile(sinp[...], ATTN_W)
            dq_parts, dkp_parts, dkc_parts, dvp_parts, dvc_parts = [], [], [], [], []
            for p in range(ATTN_W // LANES):
                sl = slice(p * LANES, (p + 1) * LANES)
                q2 = q_ref[:, sl]
                k2 = jnp.concatenate([kp_ref[:, sl], kc_ref[:, sl]], axis=0)
                v2 = jnp.concatenate([vp_ref[:, sl], vc_ref[:, sl]], axis=0)
                do2 = do_ref[:, sl]
                l2 = l_ref[:, sl]
                c2 = c_ref[:, sl]
                dq2 = jnp.zeros((BLK, LANES), F32)
                dk2 = jnp.zeros((2 * BLK, LANES), F32)
                dv2 = jnp.zeros((2 * BLK, LANES), F32)
                for hh in (0, 1):
                    sel = upper if hh else lower
                    qm = jnp.where(sel, q2, jnp.zeros_like(q2))
                    dom = jnp.where(sel, do2, jnp.zeros_like(do2))
                    l_col = l2[:, hh * HEAD_DIM:hh * HEAD_DIM + 1]
                    c_col = c2[:, hh * HEAD_DIM:hh * HEAD_DIM + 1]
                    s = lax.dot_general(qm, k2, (NT, ((), ())), preferred_element_type=F32)
                    pe = jnp.where(valid, jnp.exp(jnp.where(valid, s, NEG) - l_col), 0.0)
                    dpv = lax.dot_general(dom, v2, (NT, ((), ())), preferred_element_type=F32)
                    ds = (pe * (dpv + c_col)).astype(BF16)
                    pb = pe.astype(BF16)
                    dq2 = dq2 + jnp.where(sel, jnp.dot(ds, k2, preferred_element_type=F32), 0.0)
                    dk2 = dk2 + lax.dot_general(ds, qm, (TN, ((), ())), preferred_element_type=F32)
                    dv2 = dv2 + lax.dot_general(pb, dom, (TN, ((), ())), preferred_element_type=F32)
                dq_parts.append(dq2)
                dkp_parts.append(dk2[:BLK])
                dkc_parts.append(dk2[BLK:])
                dvp_parts.append(dv2[:BLK])
                dvc_parts.append(dv2[BLK:])
            dq = _rope(jnp.concatenate(dq_parts, axis=1) * scale, cos_c, -sin_c)
            dkc = _rope(jnp.concatenate(dkc_parts, axis=1), cos_c, -sin_c)
            dkp = _rope(jnp.concatenate(dkp_parts, axis=1), cos_p, -sin_p)
            dvp = jnp.concatenate(dvp_parts, axis=1)
            dvc = jnp.concatenate(dvc_parts, axis=1)

            @pl.when(n > 0)
            def _():
                out_ref[:, 0:ATTN_W] = dq_s[...].astype(BF16)
                out_ref[:, ATTN_W:2 * ATTN_W] = (dk_s[...] + dkp).astype(BF16)
                out_ref[:, 2 * ATTN_W:3 * ATTN_W] = (dv_s[...] + dvp).astype(BF16)

            dq_s[...] = dq
            dk_s[...] = dkc
            dv_s[...] = dvc

        @pl.when(n == nb)
        def _():
            out_ref[:, 0:ATTN_W] = dq_s[...].astype(BF16)
            out_ref[:, ATTN_W:2 * ATTN_W] = dk_s[...].astype(BF16)
            out_ref[:, 2 * ATTN_W:3 * ATTN_W] = dv_s[...].astype(BF16)

    nc = lambda n: jnp.minimum(n, nb - 1)
    npv = lambda n: jnp.maximum(jnp.minimum(n, nb - 1) - 1, 0)
    cur = lambda part: pl.BlockSpec((None, BLK, ATTN_W), lambda r, n: (r, nc(n), part))
    prev = lambda part: pl.BlockSpec((None, BLK, ATTN_W), lambda r, n: (r, npv(n), part))
    row = pl.BlockSpec((None, BLK, ATTN_W), lambda r, n: (r, nc(n), 0))
    tab_c = pl.BlockSpec((None, BLK, LANES), lambda r, n: (r, nc(n), 0))
    tab_p = pl.BlockSpec((None, BLK, LANES), lambda r, n: (r, npv(n), 0))
    out_spec = pl.BlockSpec((None, BLK, GROUP_COLS), lambda r, n: (r, jnp.maximum(n - 1, 0), 0))
    out = pl.pallas_call(
        body, grid=(dil, nb + 1),
        in_specs=[cur(0), cur(1), prev(1), cur(2), prev(2), row, row, row, tab_c, tab_c, tab_p, tab_p],
        out_specs=out_spec,
        out_shape=jax.ShapeDtypeStruct((dil, length, GROUP_COLS), BF16),
        scratch_shapes=[pltpu.VMEM((BLK, ATTN_W), F32)] * 3,
        compiler_params=_cparams(2), name=f"attn_bwd_g{g}")(
            qkv_v, qkv_v, qkv_v, qkv_v, qkv_v, do, cc, lse, cos_v, sin_v, cos_v, sin_v)
    return out.reshape(t, GROUP_COLS)


SQRT_HALF = 0.7071067811865476
INV_SQRT_2PI = 0.3989422804014327


def _sgu_core(uv, g, b, w_ref, bias):
    cdf = 0.5 * (1.0 + lax.erf(uv * SQRT_HALF))
    z = uv * cdf
    u, v = z[:, :SGU_W], z[:, SGU_W:]
    mu = jnp.mean(v, axis=1, keepdims=True)
    xc = v - mu
    rs = lax.rsqrt(jnp.mean(xc * xc, axis=1, keepdims=True) + EPS)
    xhat = xc * rs
    vn = xhat * g + b
    row = lax.broadcasted_iota(jnp.int32, (SGU_CHUNK, SGU_CHUNK), 0)
    col = lax.broadcasted_iota(jnp.int32, (SGU_CHUNK, SGU_CHUNK), 1)
    tril = row >= col
    upper = lax.broadcasted_iota(jnp.int32, (SGU_CHUNK, LANES), 1) >= SGU_W // SGU_GROUPS
    ws, vlo, vhi, mixed = [], [], [], []
    for pr in range(SGU_W // LANES):
        sl = slice(pr * LANES, (pr + 1) * LANES)
        w0 = jnp.where(tril, w_ref[2 * pr], 0.0).astype(BF16)
        w1 = jnp.where(tril, w_ref[2 * pr + 1], 0.0).astype(BF16)
        vn2 = vn[:, sl]
        lo = jnp.where(upper, 0.0, vn2).astype(BF16)
        hi = jnp.where(upper, vn2, 0.0).astype(BF16)
        mixed.append(jnp.dot(w0, lo, preferred_element_type=F32) + jnp.dot(w1, hi, preferred_element_type=F32)
                     + bias[:, sl])
        ws.append((w0, w1))
        vlo.append(lo)
        vhi.append(hi)
    return cdf, u, xhat, rs, jnp.concatenate(mixed, axis=1), ws, vlo, vhi, tril, upper


def _sgu_fwd(gu, ln_g, ln_b, w_s, bias_exp):
    t = gu.shape[0]

    def body(uv_ref, g_ref, b_ref, w_ref, bias_ref, o_ref):
        _, u, _, _, mixed, *_ = _sgu_core(uv_ref[...], g_ref[...], b_ref[...], w_ref, bias_ref[...])
        o_ref[...] = (u * mixed).astype(BF16)

    return pl.pallas_call(
        body, grid=(t // SGU_CHUNK,),
        in_specs=[pl.BlockSpec((SGU_CHUNK, 2 * SGU_W), lambda n: (n, 2)), _full((1, SGU_W)), _full((1, SGU_W)),
                  _full((SGU_GROUPS, SGU_CHUNK, SGU_CHUNK)), _full((SGU_CHUNK, SGU_W))],
        out_specs=pl.BlockSpec((SGU_CHUNK, SGU_W), lambda n: (n, 0)),
        out_shape=jax.ShapeDtypeStruct((t, SGU_W), BF16),
        compiler_params=_cparams(1), name="sgu_fwd")(gu, ln_g, ln_b, w_s, bias_exp)


def _sgu_bwd(dproj, gu, dsgu, ln_g, ln_b, w_s, bias_exp):
    t = gu.shape[0]
    nchunks = t // SGU_CHUNK
    e = _head_sum_matrix()

    def body(dp_in, uv_ref, ds_ref, g_ref, b_ref, w_ref, bias_ref, e_ref, out_ref, dw_ref, dbias_ref, dg_ref, db_ref):
        n = pl.program_id(0)

        @pl.when(n == 0)
        def _():
            dw_ref[...] = jnp.zeros(dw_ref.shape, F32)
            dbias_ref[...] = jnp.zeros(dbias_ref.shape, F32)
            dg_ref[...] = jnp.zeros(dg_ref.shape, F32)
            db_ref[...] = jnp.zeros(db_ref.shape, F32)

        uv = uv_ref[...]
        g = g_ref[...]
        cdf, u, xhat, rs, mixed, ws, vlo, vhi, tril, upper = _sgu_core(uv, g, b_ref[...], w_ref, bias_ref[...])
        dsg = ds_ref[...]
        du = dsg * mixed
        dmixed = dsg * u
        dbias_ref[...] += dmixed
        dvn = []
        for pr in range(SGU_W // LANES):
            sl = slice(pr * LANES, (pr + 1) * LANES)
            dm2 = dmixed[:, sl]
            dlo = jnp.where(upper, 0.0, dm2).astype(BF16)
            dhi = jnp.where(upper, dm2, 0.0).astype(BF16)
            w0, w1 = ws[pr]
            dvn.append(lax.dot_general(w0, dlo, (TN, ((), ())), preferred_element_type=F32)
                       + lax.dot_general(w1, dhi, (TN, ((), ())), preferred_element_type=F32))
            dw0 = lax.dot_general(dlo, vlo[pr], (NT, ((), ())), preferred_element_type=F32)
            dw1 = lax.dot_general(dhi, vhi[pr], (NT, ((), ())), preferred_element_type=F32)
            dw_ref[2 * pr] += jnp.where(tril, dw0, 0.0)
            dw_ref[2 * pr + 1] += jnp.where(tril, dw1, 0.0)
        dvn = jnp.concatenate(dvn, axis=1)
        dg_ref[...] += jnp.sum(dvn * xhat, axis=0, keepdims=True)
        db_ref[...] += jnp.sum(dvn, axis=0, keepdims=True)
        dxh = dvn * g
        dv = rs * (dxh - jnp.mean(dxh, axis=1, keepdims=True) - xhat * jnp.mean(dxh * xhat, axis=1, keepdims=True))
        dz = jnp.concatenate([du, dv], axis=1)
        dgelu = cdf + uv * (INV_SQRT_2PI * jnp.exp(-0.5 * uv * uv))
        out_ref[...] = (dz * dgelu).astype(BF16)

        @pl.when(n == nchunks - 1)
        def _():
            dbias_ref[...] = _group_sum(dbias_ref[...], e_ref[...])

    outs = pl.pallas_call(
        body, grid=(nchunks,),
        in_specs=[pl.BlockSpec(memory_space=pl.ANY), pl.BlockSpec((SGU_CHUNK, 2 * SGU_W), lambda n: (n, 2)),
                  pl.BlockSpec((SGU_CHUNK, SGU_W), lambda n: (n, 0)), _full((1, SGU_W)), _full((1, SGU_W)),
                  _full((SGU_GROUPS, SGU_CHUNK, SGU_CHUNK)), _full((SGU_CHUNK, SGU_W)), _full((ATTN_W, ATTN_W))],
        out_specs=[pl.BlockSpec((SGU_CHUNK, 2 * SGU_W), lambda n: (n, 2)), _full((SGU_GROUPS, SGU_CHUNK, SGU_CHUNK)),
                   _full((SGU_CHUNK, SGU_W)), _full((1, SGU_W)), _full((1, SGU_W))],
        out_shape=[jax.ShapeDtypeStruct(dproj.shape, BF16), jax.ShapeDtypeStruct((SGU_GROUPS, SGU_CHUNK, SGU_CHUNK), F32),
                   jax.ShapeDtypeStruct((SGU_CHUNK, SGU_W), F32), jax.ShapeDtypeStruct((1, SGU_W), F32),
                   jax.ShapeDtypeStruct((1, SGU_W), F32)],
        input_output_aliases={0: 0},
        compiler_params=_cparams(1), name="sgu_bwd")(dproj, gu, dsgu, ln_g, ln_b, w_s, bias_exp, e)
    return outs


def _merge_fwd(attn, sgu, gu, x, w_pa, w_ps, w_out, g2):
    t = x.shape[0]
    tm = min(t, 256)

    def body(a_ref, s_ref, ga_ref, gb_ref, x_ref, wpa, wps, wo, g_ref, pa_ref, ps_ref, m_ref, x1_ref, h2_ref):
        pa = jnp.dot(a_ref[...], wpa[...], preferred_element_type=F32)
        ps = jnp.dot(s_ref[...], wps[...], preferred_element_type=F32)
        merged = (_sigmoid(ga_ref[...]) * pa + _sigmoid(gb_ref[...]) * ps).astype(BF16)
        x1 = x_ref[...] + jnp.dot(merged, wo[...], preferred_element_type=F32)
        xhat, _ = _rms_stats(x1)
        pa_ref[...] = pa
        ps_ref[...] = ps
        m_ref[...] = merged
        x1_ref[...] = x1
        h2_ref[...] = (xhat * g_ref[...]).astype(BF16)

    half = pl.BlockSpec((tm, ATTN_W), lambda i: (i, 0))
    full = pl.BlockSpec((tm, D_MODEL), lambda i: (i, 0))
    return pl.pallas_call(
        body, grid=(t // tm,),
        in_specs=[half, half, pl.BlockSpec((tm, D_MODEL), lambda i: (i, 0)), pl.BlockSpec((tm, D_MODEL), lambda i: (i, 1)),
                  full, _full((ATTN_W, D_MODEL)), _full((SGU_W, D_MODEL)), _full((D_MODEL, D_MODEL)), _full((1, D_MODEL))],
        out_specs=[full] * 5,
        out_shape=[jax.ShapeDtypeStruct((t, D_MODEL), F32), jax.ShapeDtypeStruct((t, D_MODEL), F32),
                   jax.ShapeDtypeStruct((t, D_MODEL), BF16), jax.ShapeDtypeStruct((t, D_MODEL), F32),
                   jax.ShapeDtypeStruct((t, D_MODEL), BF16)],
        compiler_params=_cparams(1), name="merge_fwd")(attn, sgu, gu, gu, x, w_pa, w_ps, w_out, g2)


def _merge_bwd(dx1b, gu, pa, ps, w_pa, w_ps, w_out):
    t = dx1b.shape[0]
    tm = min(t, 256)

    def body(d_ref, ga_ref, gb_ref, pa_ref, ps_ref, wpa, wps, wo, out_ref, dpa_ref, dps_ref, da_ref, dsg_ref):
        dm = lax.dot_general(d_ref[...], wo[...], (NT, ((), ())), preferred_element_type=F32)
        sa, sb = _sigmoid(ga_ref[...]), _sigmoid(gb_ref[...])
        dpa = (dm * sa).astype(BF16)
        dps = (dm * sb).astype(BF16)
        out_ref[:, 0:D_MODEL] = (dm * pa_ref[...] * sa * (1.0 - sa)).astype(BF16)
        out_ref[:, D_MODEL:2 * D_MODEL] = (dm * ps_ref[...] * sb * (1.0 - sb)).astype(BF16)
        out_ref[:, 2 * D_MODEL:GU_COLS] = jnp.zeros((tm, GU_COLS - 2 * D_MODEL), BF16)
        dpa_ref[...] = dpa
        dps_ref[...] = dps
        da_ref[...] = lax.dot_general(dpa, wpa[...], (NT, ((), ())), preferred_element_type=F32)
        dsg_ref[...] = lax.dot_general(dps, wps[...], (NT, ((), ())), preferred_element_type=F32)

    half = pl.BlockSpec((tm, ATTN_W), lambda i: (i, 0))
    full = pl.BlockSpec((tm, D_MODEL), lambda i: (i, 0))
    return pl.pallas_call(
        body, grid=(t // tm,),
        in_specs=[full, pl.BlockSpec((tm, D_MODEL), lambda i: (i, 0)),
                  pl.BlockSpec((tm, D_MODEL), lambda i: (i, 1)), full, full,
                  _full((ATTN_W, D_MODEL)), _full((SGU_W, D_MODEL)), _full((D_MODEL, D_MODEL))],
        out_specs=[pl.BlockSpec((tm, GU_COLS), lambda i: (i, 0)), full, full, half, half],
        out_shape=[jax.ShapeDtypeStruct((t, GU_COLS), BF16), jax.ShapeDtypeStruct((t, D_MODEL), BF16),
                   jax.ShapeDtypeStruct((t, D_MODEL), BF16), jax.ShapeDtypeStruct((t, ATTN_W), F32),
                   jax.ShapeDtypeStruct((t, SGU_W), F32)],
        compiler_params=_cparams(1), name="merge_bwd")(dx1b, gu, gu, pa, ps, w_pa, w_ps, w_out)


def _ffn_fwd(h2, w_g, w_u):
    t = h2.shape[0]
    tm = min(t, 512)
    a_spec = pl.BlockSpec((tm, D_MODEL), lambda i, j, k: (i, 0))
    w_spec = pl.BlockSpec((None, D_MODEL, FF_SHARD), lambda i, j, k: (j, 0, 0))
    o_spec = pl.BlockSpec((None, tm, FF_SHARD), lambda i, j, k: (j, i, 0))
    shp = (N_CHIPS, t, FF_SHARD)

    def epi_a(acc, e, o, r, ids):
        o[0][...] = acc

    a = _mm("ffn_gate", (t // tm, N_CHIPS, 1), [(h2, a_spec, w_g, w_spec)], NN, (tm, FF_SHARD), epi_a,
            outs=[(shp, F32, o_spec)])[0]

    def epi_b(acc, e, o, r, ids):
        av = e[0][...]
        o[0][...] = acc
        o[1][...] = (av * _sigmoid(av) * acc).astype(BF16)

    b, ff = _mm("ffn_up", (t // tm, N_CHIPS, 1), [(h2, a_spec, w_u, w_spec)], NN, (tm, FF_SHARD), epi_b,
                extras=[(a, o_spec)], outs=[(shp, F32, o_spec), (shp, BF16, o_spec)])
    return a, b, ff


def _ffn_down_loss(ff, w_d, x1, tgt, gf):
    t = x1.shape[0]
    tm = min(t, 512)

    def epi(acc, e, o, r, ids):
        x2 = e[0][...] + acc
        g = e[2][...]
        xhat, rr = _rms_stats(x2)
        diff = xhat * g - e[1][...]
        rows = jnp.sum(diff * diff, axis=1, keepdims=True)
        r[0][...] += jnp.broadcast_to(jnp.sum(rows, axis=0, keepdims=True) * (0.5 / D_MODEL), (1, LANES))
        dy = diff * (1.0 / D_MODEL)
        r[1][...] += jnp.sum(dy * xhat, axis=0, keepdims=True)
        dx2 = _rms_bwd(dy, xhat, rr, g)
        o[0][...] = dx2
        o[1][...] = dx2.astype(BF16)

    row = pl.BlockSpec((tm, D_MODEL), lambda i, j, k: (i, 0))
    return _mm("ffn_down_loss", (t // tm, 1, N_CHIPS),
               [(ff, pl.BlockSpec((None, tm, FF_SHARD), lambda i, j, k: (k, i, 0)),
                 w_d, pl.BlockSpec((None, FF_SHARD, D_MODEL), lambda i, j, k: (k, 0, 0)))],
               NN, (tm, D_MODEL), epi,
               extras=[(x1, row), (tgt, row), (gf, pl.BlockSpec((1, D_MODEL), lambda i, j, k: (0, 0)))],
               outs=[((t, D_MODEL), F32, row), ((t, D_MODEL), BF16, row)],
               reds=[(1, LANES), (1, D_MODEL)])


def _ffn_bwd_act(dx2b, w_d, a, b):
    t = dx2b.shape[0]
    tm = min(t, 512)
    o_spec = pl.BlockSpec((None, tm, FF_SHARD), lambda i, j, k: (j, i, 0))
    shp = (N_CHIPS, t, FF_SHARD)

    def epi(acc, e, o, r, ids):
        av, bv = e[0][...], e[1][...]
        sg = _sigmoid(av)
        o[0][...] = (acc * bv * (sg * (1.0 + av * (1.0 - sg)))).astype(BF16)
        o[1][...] = (acc * (av * sg)).astype(BF16)

    return _mm("ffn_bwd_act", (t // tm, N_CHIPS, 1),
               [(dx2b, pl.BlockSpec((tm, D_MODEL), lambda i, j, k: (i, 0)),
                 w_d, pl.BlockSpec((None, FF_SHARD, D_MODEL), lambda i, j, k: (j, 0, 0)))],
               NT, (tm, FF_SHARD), epi, extras=[(a, o_spec), (b, o_spec)],
               outs=[(shp, BF16, o_spec), (shp, BF16, o_spec)])


def _ffn_bwd_in(da, db, w_g, w_u, x1, dx2, g2):
    t = x1.shape[0]
    tm = min(t, 512)

    def epi(acc, e, o, r, ids):
        xhat, rr = _rms_stats(e[0][...])
        r[0][...] += jnp.sum(acc * xhat, axis=0, keepdims=True)
        dx1 = e[1][...] + _rms_bwd(acc, xhat, rr, e[2][...])
        o[0][...] = dx1
        o[1][...] = dx1.astype(BF16)

    a_spec = pl.BlockSpec((None, tm, FF_SHARD), lambda i, j, k: (k, i, 0))
    w_spec = pl.BlockSpec((None, D_MODEL, FF_SHARD), lambda i, j, k: (k, 0, 0))
    row = pl.BlockSpec((tm, D_MODEL), lambda i, j, k: (i, 0))
    return _mm("ffn_bwd_in", (t // tm, 1, N_CHIPS), [(da, a_spec, w_g, w_spec), (db, a_spec, w_u, w_spec)],
               NT, (tm, D_MODEL), epi,
               extras=[(x1, row), (dx2, row), (g2, pl.BlockSpec((1, D_MODEL), lambda i, j, k: (0, 0)))],
               outs=[((t, D_MODEL), F32, row), ((t, D_MODEL), BF16, row)], reds=[(1, D_MODEL)])


def _in_proj_bwd(dgu, dqkvs, w_p, x, dx1, g1):
    t = x.shape[0]
    tile = min(t, TILE)
    tk = GROUP_COLS // 2
    row = pl.BlockSpec((tile, D_MODEL), lambda i, j, k: (i, 0))

    def epi_plain(acc, e, o, r, ids):
        o[0][...] = acc

    dhs = []
    for g in (1, 2):
        col = (GU_COLS + g * GROUP_COLS) // GROUP_COLS
        dh = _mm(f"in_proj_bwd_g{g}", (t // tile, 1, 1),
                 [(dqkvs[g], pl.BlockSpec((tile, GROUP_COLS), lambda i, j, k: (i, 0)), w_p,
                   pl.BlockSpec((D_MODEL, GROUP_COLS), lambda i, j, k, col=col: (0, col)))],
                 NT, (tile, D_MODEL), epi_plain, outs=[((t, D_MODEL), F32, row)])[0]
        dhs.append(dh.reshape(DILATIONS[g], t // DILATIONS[g], D_MODEL))

    def epi(acc, e, o, r, ids):
        x_ref, dx1_ref, g_ref, dh1_ref, dh2_ref, slab = e
        dh = acc + _natural_from_group(slab, dh1_ref)
        dh = dh + _natural_from_group(slab, dh2_ref)
        xhat, rr = _rms_stats(x_ref[...])
        r[0][...] += jnp.sum(dh * xhat, axis=0, keepdims=True)
        o[0][...] = dx1_ref[...] + _rms_bwd(dh, xhat, rr, g_ref[...])

    n_gu, n_g0 = GU_COLS // tk, GROUP_COLS // tk
    in_gu = lambda k: jnp.minimum(k, n_gu - 1)
    in_g0 = lambda k: jnp.clip(k - n_gu, 0, n_g0 - 1)
    return _mm("in_proj_bwd", (t // tile, 1, n_gu + n_g0),
               [(dgu, pl.BlockSpec((tile, tk), lambda i, j, k: (i, in_gu(k))),
                 w_p, pl.BlockSpec((D_MODEL, tk), lambda i, j, k: (0, in_gu(k)))),
                (dqkvs[0], pl.BlockSpec((tile, tk), lambda i, j, k: (i, in_g0(k))),
                 w_p, pl.BlockSpec((D_MODEL, tk), lambda i, j, k: (0, n_gu + in_g0(k))))],
               NT, (tile, D_MODEL), epi, k_ranges=[(0, n_gu), (n_gu, n_g0)],
               extras=[(x, row), (dx1, row), (g1, pl.BlockSpec((1, D_MODEL), lambda i, j, k: (0, 0))),
                       (dhs[0], _group_spec(DILATIONS[1], tile, D_MODEL)), (dhs[1], _group_spec(DILATIONS[2], tile, D_MODEL))],
               scratch=[_slabs(tile, D_MODEL)],
               outs=[((t, D_MODEL), F32, row)], reds=[(1, D_MODEL)])


def _epi_bf16(acc, e, o, r, ids):
    o[0][...] = acc.astype(BF16)


def _wgrad_2d(name, a, b, tm, tn, tk=512):
    t, k1 = a.shape
    n = b.shape[1]
    tk = min(t, tk)
    return _mm(name, (k1 // tm, n // tn, t // tk),
               [(a, pl.BlockSpec((tk, tm), lambda i, j, k: (k, i)), b, pl.BlockSpec((tk, tn), lambda i, j, k: (k, j)))],
               TN, (tm, tn), _epi_bf16, outs=[((k1, n), BF16, pl.BlockSpec((tm, tn), lambda i, j, k: (i, j)))])[0]


def _wgrad_in(hs, dgu, dqkvs, tk=512):
    t = dgu.shape[0]
    tk = min(t, tk)
    dst = None
    parts = [(hs[0], dgu, 0)] + [(hs[g].reshape(t, D_MODEL), dqkvs[g], GU_COLS // GROUP_COLS + g) for g in range(3)]
    for n, (a, b, col0) in enumerate(parts):
        dst = _mm(f"wgrad_in_{n}", (1, b.shape[1] // GROUP_COLS, t // tk),
                  [(a, pl.BlockSpec((tk, D_MODEL), lambda i, j, k: (k, 0)), b,
                    pl.BlockSpec((tk, GROUP_COLS), lambda i, j, k: (k, j)))],
                  TN, (D_MODEL, GROUP_COLS), _epi_bf16,
                  extras=[] if dst is None else [(dst, pl.BlockSpec(memory_space=pl.ANY))],
                  outs=[((D_MODEL, IN_COLS), BF16, pl.BlockSpec((D_MODEL, GROUP_COLS), lambda i, j, k, col0=col0: (0, j + col0)))],
                  aliases=None if dst is None else {2: 0})[0]
    return dst


def _wgrad_ff_in(name, h2, da, tk=512):
    t = h2.shape[0]
    tk = min(t, tk)
    return _mm(name, (N_CHIPS, 1, t // tk),
               [(h2, pl.BlockSpec((tk, D_MODEL), lambda i, j, k: (k, 0)),
                 da, pl.BlockSpec((None, tk, FF_SHARD), lambda i, j, k: (i, k, 0)))],
               TN, (D_MODEL, FF_SHARD), _epi_bf16,
               outs=[((N_CHIPS, D_MODEL, FF_SHARD), BF16, pl.BlockSpec((None, D_MODEL, FF_SHARD), lambda i, j, k: (i, 0, 0)))])[0]


def _wgrad_ff_down(ff, dx2b, tk=512):
    t = dx2b.shape[0]
    tk = min(t, tk)
    return _mm("wgrad_ffn_down", (N_CHIPS, 1, t // tk),
               [(ff, pl.BlockSpec((None, tk, FF_SHARD), lambda i, j, k: (i, k, 0)),
                 dx2b, pl.BlockSpec((tk, D_MODEL), lambda i, j, k: (k, 0)))],
               TN, (FF_SHARD, D_MODEL), _epi_bf16,
               outs=[((N_CHIPS, FF_SHARD, D_MODEL), BF16, pl.BlockSpec((None, FF_SHARD, D_MODEL), lambda i, j, k: (i, 0, 0)))])[0]


def _local_step(x, pos_col, tgt, g1, ln_g, ln_b, w_s, b_s, g2, gf, w_p, w_pa, w_ps, w_out, w_g, w_u, w_d):
    t = x.shape[0]
    tables = [_rope_table(f"rope_table_g{g}", jnp.transpose(pos_col.reshape(t // d, d)).reshape(t, 1))
              for g, d in enumerate(DILATIONS)]
    bias_exp = jnp.repeat(jnp.transpose(b_s), SGU_W // SGU_GROUPS, axis=1)

    hs = _norm_fwd(x, g1)
    gu, qkvs = _in_proj(hs, w_p, tables)
    os_, ls_ = [], []
    for g, dil in enumerate(DILATIONS):
        o, lse = _attn_fwd(qkvs[g], g, dil)
        os_.append(o)
        ls_.append(lse)
    attn = _combine_fwd(os_, ls_)
    sgu = _sgu_fwd(gu, ln_g, ln_b, w_s, bias_exp)
    pa, ps, merged, x1, h2 = _merge_fwd(attn, sgu, gu, x, w_pa, w_ps, w_out, g2)
    a, b, ff = _ffn_fwd(h2, w_g, w_u)
    dx2, dx2b, loss, dgf = _ffn_down_loss(ff, w_d, x1, tgt, gf)

    da, db = _ffn_bwd_act(dx2b, w_d, a, b)
    dw_d = _wgrad_ff_down(ff, dx2b)
    dx1, dx1b, dg2 = _ffn_bwd_in(da, db, w_g, w_u, x1, dx2, g2)
    dw_g = _wgrad_ff_in("wgrad_ffn_gate", h2, da)
    dw_u = _wgrad_ff_in("wgrad_ffn_up", h2, db)

    dgu, dpa, dps, dattn, dsgu = _merge_bwd(dx1b, gu, pa, ps, w_pa, w_ps, w_out)
    dw_out = _wgrad_2d("wgrad_out", merged, dx1b, D_MODEL, D_MODEL)
    dw_pa = _wgrad_2d("wgrad_proj_attn", attn, dpa, ATTN_W, D_MODEL)
    dw_ps = _wgrad_2d("wgrad_proj_sgu", sgu, dps, SGU_W, D_MODEL)
    dgu, dw_s, dbias, dln_g, dln_b = _sgu_bwd(dgu, gu, dsgu, ln_g, ln_b, w_s, bias_exp)
    dos, ccs = _combine_bwd(dattn, os_, ls_)
    dqkvs = [_attn_bwd(qkvs[g], dos[g], ccs[g], ls_[g], *tables[g], g, dil) for g, dil in enumerate(DILATIONS)]
    dx, dg1 = _in_proj_bwd(dgu, dqkvs, w_p, x, dx1, g1)
    dw_p = _wgrad_in(hs, dgu, dqkvs)

    db_s = jnp.transpose(dbias[:, ::SGU_W // SGU_GROUPS])
    small = dict(loss=loss, norm1_g=dg1, sgu_ln_g=dln_g, sgu_ln_b=dln_b, w_spatial=dw_s, b_spatial=db_s,
                 norm2_g=dg2, final_g=dgf)
    big = dict(w_in=dw_p, w_proj_attn=dw_pa, w_proj_sgu=dw_ps, w_out=dw_out, w_ffn_gate=dw_g, w_ffn_up=dw_u,
               w_ffn_down=dw_d)
    return dx, big, small


def _ew(name, fn, ins, out_dtypes):
    shp = ins[0].shape
    rows, cols = shp
    tr = next((cand for cand in (256, 352, 128) if rows % cand == 0 and rows > cand), rows)

    def body(*refs):
        res = fn(*[r[...] for r in refs[:len(ins)]])
        for o_ref, v in zip(refs[len(ins):], res):
            o_ref[...] = v.astype(o_ref.dtype)

    spec = pl.BlockSpec((tr, cols), lambda i: (i, 0))
    return pl.pallas_call(
        body, grid=(rows // tr,), in_specs=[spec] * len(ins), out_specs=[spec] * len(out_dtypes),
        out_shape=[jax.ShapeDtypeStruct(shp, d) for d in out_dtypes],
        compiler_params=_cparams(1), name=name)(*ins)


def _adamw_math(g, w, m, v):
    m = ADAM_B1 * m + (1.0 - ADAM_B1) * g
    v = ADAM_B2 * v + (1.0 - ADAM_B2) * (g * g)
    m_hat = m / (1.0 - ADAM_B1 ** ADAM_STEP)
    v_hat = v / (1.0 - ADAM_B2 ** ADAM_STEP)
    delta = -ADAM_LR * (m_hat / (jnp.sqrt(v_hat) + ADAM_EPS) + ADAM_WD * w)
    return delta, m, v


def _adamw(name, g, w, m, v):
    return _ew(name, _adamw_math, [g, w, m, v], [F32, F32, F32])


VMEM_SPEC = pl.BlockSpec(memory_space=pltpu.VMEM)


def _for_row_chunks(rows, fn):
    ck = next(c for c in (64, 32, 16) if rows % c == 0)

    def step(i, carry):
        fn(pl.multiple_of(i * ck, ck), ck)
        return carry

    lax.fori_loop(0, rows // ck, step, 0)


def _place():
    x, y, c = lax.axis_index("x"), lax.axis_index("y"), lax.axis_index("c")
    chips = [(1 - x, y), (x, 1 - y), (1 - x, 1 - y)]
    return x, y, c, 2 * x + y, chips


def _rows(ref, start, size):
    if len(ref.shape) == 2:
        return ref.at[pl.ds(start, size), :]
    return ref.at[:, pl.ds(start, size), :]


def _comm_call(name, body, ins, out_shapes, scratch, n_remote):
    return pl.pallas_call(
        body, in_specs=[VMEM_SPEC] * len(ins), out_specs=[VMEM_SPEC] * len(out_shapes),
        out_shape=out_shapes,
        scratch_shapes=list(scratch) + [pltpu.SemaphoreType.DMA((n_remote,)), pltpu.SemaphoreType.DMA((n_remote,))],
        compiler_params=pltpu.CompilerParams(vmem_limit_bytes=VMEM_LIMIT), name=name)(*ins)


def _gather_weights(name, shards):
    nt = len(shards)

    def body(*refs):
        ins, outs = refs[:nt], refs[nt:2 * nt]
        send, recv = refs[2 * nt:]
        x, y, c, me, chips = _place()
        sibling = (x, y, 1 - c)
        firsts, passed, expects = [], [], []
        for t in range(nt):
            kh = ins[t].shape[0] // 2
            for j, chip in enumerate(chips):
                k = t * 3 + j
                theirs = 2 * chip[0] + chip[1]
                firsts.append(pltpu.make_async_remote_copy(
                    src_ref=_rows(ins[t], c * kh, kh), dst_ref=_rows(outs[t].at[me], c * kh, kh),
                    send_sem=send.at[k], recv_sem=recv.at[k], device_id=(*chip, c), device_id_type=MESH))
                landed = _rows(outs[t].at[theirs], c * kh, kh)
                expects.append(pltpu.make_async_remote_copy(
                    src_ref=landed, dst_ref=landed, send_sem=send.at[k], recv_sem=recv.at[k],
                    device_id=(*chip, c), device_id_type=MESH))
                passed.append(pltpu.make_async_remote_copy(
                    src_ref=landed, dst_ref=landed, send_sem=send.at[3 * nt + k], recv_sem=recv.at[3 * nt + k],
                    device_id=sibling, device_id_type=MESH))
        for cp in firsts:
            cp.start()
        for t in range(nt):
            mine = outs[t].at[me]

            def put(r0, ck, src=ins[t], dst=mine):
                dst[pl.ds(r0, ck), :] = src[pl.ds(r0, ck), :]

            _for_row_chunks(ins[t].shape[0], put)
        for k in range(3 * nt):
            expects[k].wait_recv()
            passed[k].start()
        for t in range(nt):
            kh = ins[t].shape[0] // 2
            for j, chip in enumerate(chips):
                k = t * 3 + j
                theirs = 2 * chip[0] + chip[1]
                other = _rows(outs[t].at[theirs], (1 - c) * kh, kh)
                pltpu.make_async_remote_copy(
                    src_ref=other, dst_ref=other, send_sem=send.at[3 * nt + k], recv_sem=recv.at[3 * nt + k],
                    device_id=sibling, device_id_type=MESH).wait_recv()
        for cp in firsts + passed:
            cp.wait_send()

    out_shapes = [jax.ShapeDtypeStruct((N_CHIPS,) + s.shape, s.dtype) for s in shards]
    return _comm_call(name, body, shards, out_shapes, [], 6 * nt)


def _pair_reduce(name, grads):
    nt = len(grads)

    def half(s):
        shp = list(s.shape)
        shp[-2] //= 2
        return tuple(shp)

    def body(*refs):
        ins, outs, got = refs[:nt], refs[nt:2 * nt], refs[2 * nt:3 * nt]
        send, recv = refs[3 * nt:]
        x, y, c, me, chips = _place()
        copies = []
        for t in range(nt):
            kh = ins[t].shape[-2] // 2
            rc = pltpu.make_async_remote_copy(
                src_ref=_rows(ins[t], (1 - c) * kh, kh), dst_ref=got[t], send_sem=send.at[t], recv_sem=recv.at[t],
                device_id=(x, y, 1 - c), device_id_type=MESH)
            rc.start()
            copies.append(rc)
        for t in range(nt):
            kh = ins[t].shape[-2] // 2
            copies[t].wait_recv()
            for lead in ([()] if len(ins[t].shape) == 2 else [(s,) for s in range(ins[t].shape[0])]):

                def add(r0, ck, lead=lead, src=ins[t], oth=got[t], dst=outs[t], kh=kh):
                    own = src[lead + (pl.ds(pl.multiple_of(c * kh + r0, ck), ck), slice(None))]
                    rows = lead + (pl.ds(r0, ck), slice(None))
                    dst[rows] = (own.astype(F32) + oth[rows].astype(F32)).astype(BF16)

                _for_row_chunks(kh, add)
        for rc in copies:
            rc.wait_send()

    shapes = [half(g) for g in grads]
    return _comm_call(name, body, grads, [jax.ShapeDtypeStruct(s, BF16) for s in shapes],
                      [pltpu.VMEM(s, BF16) for s in shapes], nt)


def _chip_reduce(name, sums):
    nt = len(sums)

    def cols(s):
        return s[2] if len(s) == 3 else s[1] // N_CHIPS

    def piece(ref, j):
        if len(ref.shape) == 3:
            return ref.at[j]
        n4 = ref.shape[1] // N_CHIPS
        return ref.at[:, pl.ds(j * n4, n4)]

    def body(*refs):
        ins, outs, slots = refs[:nt], refs[nt:2 * nt], refs[2 * nt:3 * nt]
        send, recv = refs[3 * nt:]
        x, y, c, me, chips = _place()
        sibling = (x, y, 1 - c)
        copies = []
        for t in range(nt):
            for j, chip in enumerate(chips):
                k = t * 3 + j
                rc = pltpu.make_async_remote_copy(
                    src_ref=piece(ins[t], 2 * chip[0] + chip[1]), dst_ref=slots[t].at[j], send_sem=send.at[k],
                    recv_sem=recv.at[k], device_id=(*chip, c), device_id_type=MESH)
                rc.start()
                copies.append(rc)
        handed = []
        for t in range(nt):
            kh, n4 = ins[t].shape[-2], outs[t].shape[1]
            for j in range(3):
                copies[t * 3 + j].wait_recv()
            for jj in range(N_CHIPS):

                @pl.when(me == jj)
                def _(jj=jj, src=ins[t], slot=slots[t], dst=outs[t], kh=kh, n4=n4):
                    def add(r0, ck):
                        rows = pl.ds(r0, ck)
                        own = src[jj, rows, :] if len(src.shape) == 3 else src[rows, jj * n4:(jj + 1) * n4]
                        acc = ((own.astype(F32) + slot[0, rows, :].astype(F32)) + slot[1, rows, :].astype(F32)) \
                            + slot[2, rows, :].astype(F32)
                        dst[pl.ds(pl.multiple_of(c * kh + r0, ck), ck), :] = acc

                    _for_row_chunks(kh, add)

            rc = pltpu.make_async_remote_copy(
                src_ref=_rows(outs[t], c * kh, kh), dst_ref=_rows(outs[t], c * kh, kh), send_sem=send.at[3 * nt + t],
                recv_sem=recv.at[3 * nt + t], device_id=sibling, device_id_type=MESH)
            rc.start()
            handed.append(rc)
        for t in range(nt):
            kh = ins[t].shape[-2]
            other = _rows(outs[t], (1 - c) * kh, kh)
            pltpu.make_async_remote_copy(
                src_ref=other, dst_ref=other, send_sem=send.at[3 * nt + t], recv_sem=recv.at[3 * nt + t],
                device_id=sibling, device_id_type=MESH).wait_recv()
        for rc in copies + handed:
            rc.wait_send()

    out_shapes = [jax.ShapeDtypeStruct((2 * s.shape[-2], cols(s.shape)), F32) for s in sums]
    scratch = [pltpu.VMEM((3, s.shape[-2], cols(s.shape)), BF16) for s in sums]
    return _comm_call(name, body, sums, out_shapes, scratch, 4 * nt)


def _allreduce_small(buf):
    shp = buf.shape

    def body(in_ref, out_ref, pair_ref, slot_ref, send, recv):
        x, y, c, me, chips = _place()
        sibling = (x, y, 1 - c)
        first = pltpu.make_async_remote_copy(src_ref=in_ref, dst_ref=pair_ref, send_sem=send.at[0], recv_sem=recv.at[0],
                                             device_id=sibling, device_id_type=MESH)
        first.start()
        first.wait_recv()
        slot_ref[me] = in_ref[...] + pair_ref[...]
        copies = []
        for j, chip in enumerate(chips):
            theirs = 2 * chip[0] + chip[1]
            rc = pltpu.make_async_remote_copy(src_ref=slot_ref.at[me], dst_ref=slot_ref.at[me], send_sem=send.at[1 + j],
                                              recv_sem=recv.at[1 + j], device_id=(*chip, c), device_id_type=MESH)
            rc.start()
            arrive = pltpu.make_async_remote_copy(src_ref=slot_ref.at[theirs], dst_ref=slot_ref.at[theirs],
                                                  send_sem=send.at[1 + j], recv_sem=recv.at[1 + j],
                                                  device_id=(*chip, c), device_id_type=MESH)
            copies.append((rc, arrive))
        for rc, arrive in copies:
            arrive.wait_recv()
        out_ref[...] = ((slot_ref[0] + slot_ref[1]) + slot_ref[2]) + slot_ref[3]
        first.wait_send()
        for rc, arrive in copies:
            rc.wait_send()

    vm = pl.BlockSpec(memory_space=pltpu.VMEM)
    return pl.pallas_call(
        body, in_specs=[vm], out_specs=vm, out_shape=jax.ShapeDtypeStruct(shp, F32),
        scratch_shapes=[pltpu.VMEM(shp, F32), pltpu.VMEM((N_CHIPS,) + shp, F32),
                        pltpu.SemaphoreType.DMA((4,)), pltpu.SemaphoreType.DMA((4,))],
        compiler_params=pltpu.CompilerParams(vmem_limit_bytes=VMEM_LIMIT),
        name="allreduce_small")(buf)


SMALL_ORDER = ("norm1_g", "norm2_g", "final_g", "sgu_ln_g", "sgu_ln_b", "b_spatial", "loss", "w_spatial")
SMALL_SIZES = dict(norm1_g=1024, norm2_g=1024, final_g=1024, sgu_ln_g=512, sgu_ln_b=512, b_spatial=1024, loss=LANES,
                   w_spatial=SGU_GROUPS * SGU_CHUNK * SGU_CHUNK)
SMALL_ROWS = 1072


def _pack_small(parts):
    flat = [jnp.reshape(parts[n].astype(F32), (-1,)) for n in SMALL_ORDER]
    used = sum(SMALL_SIZES[n] for n in SMALL_ORDER)
    flat.append(jnp.zeros((SMALL_ROWS * LANES - used,), F32))
    return jnp.concatenate(flat).reshape(SMALL_ROWS, LANES)


def _unpack_small(buf, shapes):
    flat = buf.reshape(-1)
    out, off = {}, 0
    for n in SMALL_ORDER:
        sz = SMALL_SIZES[n]
        if n in shapes:
            out[n] = flat[off:off + sz].reshape(shapes[n])
        off += sz
    return out


BIG = ("w_in", "w_proj_attn", "w_proj_sgu", "w_out", "w_ffn_gate", "w_ffn_up", "w_ffn_down")
COMM_GROUPS = (("w_in",), ("w_proj_attn", "w_proj_sgu", "w_out", "w_ffn_gate", "w_ffn_up", "w_ffn_down"))
WEIGHTS = ("norm1_g", "w_in", "sgu_ln_g", "sgu_ln_b", "w_spatial", "b_spatial", "w_proj_attn", "w_proj_sgu", "w_out",
           "norm2_g", "w_ffn_gate", "w_ffn_up", "w_ffn_down", "final_g")


def _cols_from_chips(g):
    return jnp.transpose(g, (1, 0, 2)).reshape(g.shape[1], N_CHIPS * g.shape[2])


def _permute_cols(w, perm):
    return jnp.concatenate([w[:, 512 * b:512 * (b + 1)] for b in perm], axis=1)


def kernel(x, positions, norm1_g, w_in, sgu_ln_g, sgu_ln_b, w_spatial, b_spatial, w_proj_attn, w_proj_sgu, w_out, norm2_g, w_ffn_gate, w_ffn_up, w_ffn_down, final_g, loss_target, m_norm1_g, m_w_in, m_sgu_ln_g, m_sgu_ln_b, m_w_spatial, m_b_spatial, m_w_proj_attn, m_w_proj_sgu, m_w_out, m_norm2_g, m_w_ffn_gate, m_w_ffn_up, m_w_ffn_down, m_final_g, v_norm1_g, v_w_in, v_sgu_ln_g, v_sgu_ln_b, v_w_spatial, v_b_spatial, v_w_proj_attn, v_w_proj_sgu, v_w_out, v_norm2_g, v_w_ffn_gate, v_w_ffn_up, v_w_ffn_down, v_final_g):
    w = dict(norm1_g=norm1_g, w_in=w_in, sgu_ln_g=sgu_ln_g, sgu_ln_b=sgu_ln_b, w_spatial=w_spatial, b_spatial=b_spatial,
             w_proj_attn=w_proj_attn, w_proj_sgu=w_proj_sgu, w_out=w_out, norm2_g=norm2_g, w_ffn_gate=w_ffn_gate,
             w_ffn_up=w_ffn_up, w_ffn_down=w_ffn_down, final_g=final_g)
    m = dict(norm1_g=m_norm1_g, w_in=m_w_in, sgu_ln_g=m_sgu_ln_g, sgu_ln_b=m_sgu_ln_b, w_spatial=m_w_spatial,
             b_spatial=m_b_spatial, w_proj_attn=m_w_proj_attn, w_proj_sgu=m_w_proj_sgu, w_out=m_w_out, norm2_g=m_norm2_g,
             w_ffn_gate=m_w_ffn_gate, w_ffn_up=m_w_ffn_up, w_ffn_down=m_w_ffn_down, final_g=m_final_g)
    v = dict(norm1_g=v_norm1_g, w_in=v_w_in, sgu_ln_g=v_sgu_ln_g, sgu_ln_b=v_sgu_ln_b, w_spatial=v_w_spatial,
             b_spatial=v_b_spatial, w_proj_attn=v_w_proj_attn, w_proj_sgu=v_w_proj_sgu, w_out=v_w_out, norm2_g=v_norm2_g,
             w_ffn_gate=v_w_ffn_gate, w_ffn_up=v_w_ffn_up, w_ffn_down=v_w_ffn_down, final_g=v_final_g)
    t = x.shape[1]

    shards = {n: _ew(f"cast_{n}", lambda a: (a,), [w[n][0]], [BF16])[0] for n in BIG}
    gath = {}
    for i, grp in enumerate(COMM_GROUPS):
        gath.update(zip(grp, _gather_weights(f"gather_weights_{i}", [shards[n] for n in grp])))
    w_p = _permute_cols(_cols_from_chips(gath["w_in"]), PERM)
    w_pa = _cols_from_chips(gath["w_proj_attn"])
    w_ps = _cols_from_chips(gath["w_proj_sgu"])
    w_o = gath["w_out"].reshape(D_MODEL, D_MODEL)

    dx, big, small = _local_step(
        x[0], positions.reshape(t, 1), loss_target[0], norm1_g, sgu_ln_g, sgu_ln_b, w_spatial[0], b_spatial[0], norm2_g,
        final_g.reshape(1, D_MODEL), w_p, w_pa, w_ps, w_o, gath["w_ffn_gate"], gath["w_ffn_up"], gath["w_ffn_down"])

    big["w_in"] = _permute_cols(big["w_in"], INV_PERM)
    big["w_out"] = big["w_out"].reshape(N_CHIPS, D_MODEL // N_CHIPS, D_MODEL)
    grads = {}
    for i, grp in enumerate(COMM_GROUPS):
        sums = _pair_reduce(f"rs_pair_reduce_{i}", [big[n] for n in grp])
        grads.update(zip(grp, _chip_reduce(f"rs_chip_reduce_{i}", sums)))

    small_shapes = {n: w[n].shape for n in SMALL_ORDER if n != "loss"}
    reduced = _allreduce_small(_pack_small(small))
    grads.update(_unpack_small(reduced, small_shapes))
    loss = reduced.reshape(-1)[sum(SMALL_SIZES[n] for n in SMALL_ORDER[:SMALL_ORDER.index("loss")])]

    delta, new_m, new_v = {}, {}, {}
    for n in BIG:
        shp = w[n].shape
        grads[n] = grads[n].reshape(shp)
        d_, m_, v_ = _adamw(f"adamw_{n}", grads[n][0], w[n][0], m[n][0], v[n][0])
        delta[n], new_m[n], new_v[n] = d_.reshape(shp), m_.reshape(shp), v_.reshape(shp)
    zero_loss = dict(loss=jnp.zeros((LANES,), F32))
    d_, m_, v_ = _adamw("adamw_small", reduced, _pack_small({**w, **zero_loss}), _pack_small({**m, **zero_loss}),
                        _pack_small({**v, **zero_loss}))
    delta.update(_unpack_small(d_, small_shapes))
    new_m.update(_unpack_small(m_, small_shapes))
    new_v.update(_unpack_small(v_, small_shapes))

    return (loss, dx.reshape(x.shape), *[grads[n] for n in WEIGHTS], *[delta[n] for n in WEIGHTS],
            *[new_m[n] for n in WEIGHTS], *[new_v[n] for n in WEIGHTS])
```

```python
import functools

import numpy as np
import jax
import jax.numpy as jnp
from jax import lax
from jax.experimental import pallas as pl
from jax.experimental.pallas import tpu as pltpu

F32, BF16 = jnp.float32, jnp.bfloat16
MESH = pl.DeviceIdType.MESH

D_MODEL = 1024
HEAD_DIM = 64
ATTN_W = 512
DILATIONS = (1, 4, 16)
BLK = 128
ROPE_DIM = 16
ROPE_THETA = 500000.0
SGU_W = 512
SGU_CHUNK = 128
SGU_GROUPS = 8
D_FF = 2816
N_CHIPS = 4
FF_SHARD = D_FF // N_CHIPS
IN_COLS = 7680
EPS = 1e-6
NEG = -1e30
LANES = 128
VMEM_LIMIT = 52 * 1024 * 1024

ADAM_LR, ADAM_B1, ADAM_B2, ADAM_EPS, ADAM_WD, ADAM_STEP = 0.001, 0.9, 0.999, 1e-08, 0.01, 10

PERM = (11, 12, 13, 14, 9, 10, 0, 3, 6, 1, 4, 7, 2, 5, 8)
INV_PERM = tuple(int(i) for i in np.argsort(np.array(PERM)))


def _cparams(ngrid):
    return pltpu.CompilerParams(dimension_semantics=("arbitrary",) * ngrid, vmem_limit_bytes=VMEM_LIMIT)


def _full(shape):
    return pl.BlockSpec(shape, lambda *_: (0,) * len(shape))


def _resident(shape):
    return pl.BlockSpec(shape, lambda *_: (0,) * len(shape), pipeline_mode=pl.Buffered(1))


NN = ((1,), (0,))
NT = ((1,), (1,))
TN = ((0,), (0,))


def _mm(name, grid, pairs, dims, acc_shape, epi, *, extras=(), outs=(), reds=(), aliases=None):
    nk = grid[-1]
    npair, nex, nout, nred = len(pairs), len(extras), len(outs), len(reds)

    def body(*refs):
        a_refs = refs[:npair]
        b_refs = refs[npair:2 * npair]
        p0 = 2 * npair
        e_refs = refs[p0:p0 + nex]
        o_refs = refs[p0 + nex:p0 + nex + nout]
        r_refs = refs[p0 + nex + nout:p0 + nex + nout + nred]
        ids = [pl.program_id(a) for a in range(len(grid))]
        k = ids[-1]
        if nred:
            first = ids[0] == 0
            for v in ids[1:]:
                first = first & (v == 0)

            @pl.when(first)
            def _():
                for r in r_refs:
                    r[...] = jnp.zeros(r.shape, r.dtype)

        part = None
        for a_ref, b_ref in zip(a_refs, b_refs):
            d = lax.dot_general(a_ref[...], b_ref[...], (dims, ((), ())), preferred_element_type=F32)
            part = d if part is None else part + d
        if nk == 1:
            epi(part, e_refs, o_refs, r_refs, ids)
        else:
            acc_ref = refs[-1]

            @pl.when(k == 0)
            def _():
                acc_ref[...] = part

            @pl.when(k > 0)
            def _():
                acc_ref[...] += part

            @pl.when(k == nk - 1)
            def _():
                epi(acc_ref[...], e_refs, o_refs, r_refs, ids)

    in_specs = [p[1] for p in pairs] + [p[3] for p in pairs] + [e[1] for e in extras]
    args = [p[0] for p in pairs] + [p[2] for p in pairs] + [e[0] for e in extras]
    out_shape = [jax.ShapeDtypeStruct(o[0], o[1]) for o in outs] + [jax.ShapeDtypeStruct(r, F32) for r in reds]
    out_specs = [o[2] for o in outs] + [_full(r) for r in reds]
    scratch_shapes = [pltpu.VMEM(acc_shape, F32)] if nk > 1 else []
    return pl.pallas_call(
        body, grid=grid, in_specs=in_specs, out_specs=out_specs, out_shape=out_shape, scratch_shapes=scratch_shapes,
        input_output_aliases=aliases or {}, compiler_params=_cparams(len(grid)), name=name)(*args)


def _lane_tile(t, width):
    n = width // LANES
    return t if n == 1 else jnp.concatenate([t] * n, axis=1)


def _rope(v, cos_w, sin_w):
    w = v.shape[1]
    lane = lax.broadcasted_iota(jnp.int32, v.shape, 1)
    partner = jnp.where((lane % HEAD_DIM) < ROPE_DIM // 2, pltpu.roll(v, w - ROPE_DIM // 2, axis=1),
                        pltpu.roll(v, ROPE_DIM // 2, axis=1))
    return v * cos_w + partner * sin_w


def _sigmoid(v):
    return 1.0 / (1.0 + jnp.exp(-v))


def _rms_stats(v):
    r = lax.rsqrt(jnp.mean(v * v, axis=-1, keepdims=True) + EPS)
    return v * r, r


def _rms_bwd(dy, xhat, r, g):
    dxh = dy * g
    return r * (dxh - xhat * jnp.mean(dxh * xhat, axis=-1, keepdims=True))


def _head_sum_matrix():
    idx = np.arange(ATTN_W) // HEAD_DIM
    return jnp.asarray((idx[:, None] == idx[None, :]).astype(np.float32), dtype=BF16)


def _group_sum(v, e):
    hi = v.astype(BF16)
    lo = (v - hi.astype(F32)).astype(BF16)
    return jnp.dot(hi, e, preferred_element_type=F32) + jnp.dot(lo, e, preferred_element_type=F32)


TILE = 512


def _to_slabs(slab_ref, v):
    for cs in range(slab_ref.shape[0]):
        slab_ref[cs] = v[:, cs * LANES:(cs + 1) * LANES]


def _from_slabs(slab_ref):
    return jnp.concatenate([slab_ref[cs] for cs in range(slab_ref.shape[0])], axis=1)


def _class_rows(slab_ref, r, dil):
    n = slab_ref.shape[1] // dil
    return jnp.concatenate([slab_ref.at[cs][pl.ds(r, n, stride=dil), :] for cs in range(slab_ref.shape[0])], axis=1)


def _put_class_rows(slab_ref, r, dil, v):
    n = slab_ref.shape[1] // dil
    for cs in range(slab_ref.shape[0]):
        slab_ref.at[cs][pl.ds(r, n, stride=dil), :] = v[:, cs * LANES:(cs + 1) * LANES]


def _natural_from_group(slab_ref, grp_ref):
    dil = grp_ref.shape[0]
    for r in range(dil):
        _put_class_rows(slab_ref, r, dil, grp_ref[r].astype(F32))
    return _from_slabs(slab_ref)


def _group_from_natural(slab_ref, grp_ref, v):
    dil = grp_ref.shape[0]
    _to_slabs(slab_ref, v)
    for r in range(dil):
        grp_ref[r] = _class_rows(slab_ref, r, dil).astype(grp_ref.dtype)


def _group_spec(dil, tile, width):
    return pl.BlockSpec((dil, tile // dil, width), lambda i, *_: (0, i, 0))


def _slabs(tile, width):
    return pltpu.VMEM((width // LANES, tile, LANES), F32)


def _rope_consts():
    lane = np.arange(LANES) % HEAD_DIM
    fi = lane % (ROPE_DIM // 2)
    invf = np.where(lane < ROPE_DIM, ROPE_THETA ** (-(2.0 * fi) / ROPE_DIM), 0.0)
    sgn = np.where(lane < ROPE_DIM // 2, -1.0, np.where(lane < ROPE_DIM, 1.0, 0.0))
    return (jnp.asarray(invf.astype(np.float32)).reshape(1, LANES), jnp.asarray(sgn.astype(np.float32)).reshape(1, LANES))


def _rope_table(name, pos_col):
    t = pos_col.shape[0]
    tm = min(t, 1024)
    invf, sgn = _rope_consts()

    def body(p_ref, f_ref, s_ref, c_out, s_out):
        ang = p_ref[...].astype(F32) * f_ref[...]
        c_out[...] = jnp.cos(ang)
        s_out[...] = jnp.sin(ang) * s_ref[...]

    return pl.pallas_call(
        body, grid=(t // tm,),
        in_specs=[pl.BlockSpec((tm, 1), lambda i: (i, 0)), _full((1, LANES)), _full((1, LANES))],
        out_specs=[pl.BlockSpec((tm, LANES), lambda i: (i, 0))] * 2,
        out_shape=[jax.ShapeDtypeStruct((t, LANES), F32)] * 2,
        compiler_params=_cparams(1), name=name)(pos_col, invf, sgn)


def _norm_fwd(x, g):
    t = x.shape[0]
    tile = min(t, TILE)

    def body(x_ref, g_ref, h0_ref, h1_ref, h2_ref, slab):
        xhat, _ = _rms_stats(x_ref[...])
        hn = xhat * g_ref[...]
        h0_ref[...] = hn.astype(BF16)
        _group_from_natural(slab, h1_ref, hn)
        for r in range(DILATIONS[2]):
            h2_ref[r] = _class_rows(slab, r, DILATIONS[2]).astype(BF16)

    nat = pl.BlockSpec((tile, D_MODEL), lambda i: (i, 0))
    return pl.pallas_call(
        body, grid=(t // tile,),
        in_specs=[nat, _full((1, D_MODEL))],
        out_specs=[nat] + [_group_spec(d, tile, D_MODEL) for d in DILATIONS[1:]],
        out_shape=[jax.ShapeDtypeStruct((t, D_MODEL), BF16)]
        + [jax.ShapeDtypeStruct((d, t // d, D_MODEL), BF16) for d in DILATIONS[1:]],
        scratch_shapes=[_slabs(tile, D_MODEL)],
        compiler_params=_cparams(1), name="norm1_fwd")(x, g)


GU_COLS = 3072
GROUP_COLS = 1536


def _in_proj(hs, w_p, tables):
    t = hs[0].shape[0]
    tm = min(t, 1024)

    def body_gu(h_ref, w_ref, o_ref):
        o_ref[...] = jnp.dot(h_ref[...], w_ref[...], preferred_element_type=F32).astype(BF16)

    gu = _token_call("in_proj_gates_uv", body_gu, t, tm,
                     [(hs[0], _rows_spec(tm, D_MODEL)),
                      (w_p, pl.BlockSpec((D_MODEL, GU_COLS), lambda i: (0, 0), pipeline_mode=pl.Buffered(1)))],
                     [((t, GU_COLS), BF16, _rows_spec(tm, GU_COLS))])[0]

    qkvs = []
    for g in range(len(DILATIONS)):

        def body_qkv(h_ref, w_ref, cos_ref, sin_ref, o_ref):
            acc = jnp.dot(h_ref[...], w_ref[...], preferred_element_type=F32)
            cos_w = _lane_tile(cos_ref[...], ATTN_W)
            sin_w = _lane_tile(sin_ref[...], ATTN_W)
            o_ref[:, 0:ATTN_W] = (_rope(acc[:, 0:ATTN_W], cos_w, sin_w) * HEAD_DIM ** -0.5).astype(BF16)
            o_ref[:, ATTN_W:2 * ATTN_W] = _rope(acc[:, ATTN_W:2 * ATTN_W], cos_w, sin_w).astype(BF16)
            o_ref[:, 2 * ATTN_W:] = acc[:, 2 * ATTN_W:].astype(BF16)

        col = (GU_COLS + g * GROUP_COLS) // GROUP_COLS
        cos_t, sin_t = tables[g]
        qkvs.append(_token_call(
            f"in_proj_qkv_g{g}", body_qkv, t, tm,
            [(hs[g].reshape(t, D_MODEL), _rows_spec(tm, D_MODEL)),
             (w_p, pl.BlockSpec((D_MODEL, GROUP_COLS), lambda i, col=col: (0, col), pipeline_mode=pl.Buffered(1))),
             (cos_t, _rows_spec(tm, LANES)), (sin_t, _rows_spec(tm, LANES))],
            [((t, GROUP_COLS), BF16, _rows_spec(tm, GROUP_COLS))])[0])
    return gu, qkvs


def _attn_masks(n):
    row = lax.broadcasted_iota(jnp.int32, (BLK, 2 * BLK), 0)
    col = lax.broadcasted_iota(jnp.int32, (BLK, 2 * BLK), 1)
    diff = BLK + row - col
    valid = (diff >= 0) & (diff <= BLK) & ((col >= BLK) | (n > 0))
    upper = lax.broadcasted_iota(jnp.int32, (BLK, LANES), 1) >= HEAD_DIM
    return valid, upper


def _attn_fwd(qkv, g, dil):
    t = qkv.shape[0]
    length = t // dil
    nb = length // BLK
    view = qkv.reshape(dil, length, GROUP_COLS)

    def body(q_ref, kc_ref, kp_ref, vc_ref, vp_ref, o_ref, l_ref):
        n = pl.program_id(1)
        valid, upper = _attn_masks(n)
        for p in range(ATTN_W // LANES):
            sl = slice(p * LANES, (p + 1) * LANES)
            q2 = q_ref[:, sl]
            k2 = jnp.concatenate([kp_ref[:, sl], kc_ref[:, sl]], axis=0)
            v2 = jnp.concatenate([vp_ref[:, sl], vc_ref[:, sl]], axis=0)
            outs, lses = [], []
            for hh in (0, 1):
                sel = upper if hh else jnp.logical_not(upper)
                qm = jnp.where(sel, q2, jnp.zeros_like(q2))
                s = lax.dot_general(qm, k2, (NT, ((), ())), preferred_element_type=F32)
                s = jnp.where(valid, s, NEG)
                m = jnp.max(s, axis=1, keepdims=True)
                pe = jnp.exp(s - m)
                den = jnp.sum(pe, axis=1, keepdims=True)
                outs.append(jnp.dot(pe.astype(BF16), v2, preferred_element_type=F32) / den)
                lses.append(jnp.broadcast_to(m + jnp.log(den), (BLK, LANES)))
            o_ref[:, sl] = jnp.where(upper, outs[1], outs[0])
            l_ref[:, sl] = jnp.where(upper, lses[1], lses[0])

    cur = lambda part: pl.BlockSpec((None, BLK, ATTN_W), lambda r, n: (r, n, part))
    prev = lambda part: pl.BlockSpec((None, BLK, ATTN_W), lambda r, n: (r, jnp.maximum(n - 1, 0), part))
    out_spec = pl.BlockSpec((None, BLK, ATTN_W), lambda r, n: (r, n, 0))
    return pl.pallas_call(
        body, grid=(dil, nb),
        in_specs=[cur(0), cur(1), prev(1), cur(2), prev(2)],
        out_specs=[out_spec, out_spec],
        out_shape=[jax.ShapeDtypeStruct((dil, length, ATTN_W), F32)] * 2,
        compiler_params=_cparams(2), name=f"attn_fwd_g{g}")(view, view, view, view, view)


def _alphas(l0, l1, l2):
    m = jnp.maximum(jnp.maximum(l0, l1), l2)
    e0, e1, e2 = jnp.exp(l0 - m), jnp.exp(l1 - m), jnp.exp(l2 - m)
    inv = 1.0 / (e0 + e1 + e2)
    return e0 * inv, e1 * inv, e2 * inv


def _natural_group_values(o_refs, l_refs, slabs):
    os_ = [o_refs[0][0]] + [_natural_from_group(slabs[2 * g - 2], o_refs[g]) for g in (1, 2)]
    ls_ = [l_refs[0][0]] + [_natural_from_group(slabs[2 * g - 1], l_refs[g]) for g in (1, 2)]
    return os_, ls_


def _combine_fwd(os_, ls_):
    t = os_[0].shape[1]
    tile = min(t, TILE)

    def body(o0, o1, o2, l0, l1, l2, a_ref, *slabs):
        ov, lv = _natural_group_values((o0, o1, o2), (l0, l1, l2), slabs)
        a0, a1, a2 = _alphas(*lv)
        a_ref[...] = (a0 * ov[0] + a1 * ov[1] + a2 * ov[2]).astype(BF16)

    specs = [_group_spec(d, tile, ATTN_W) for d in DILATIONS]
    return pl.pallas_call(
        body, grid=(t // tile,), in_specs=specs * 2, out_specs=pl.BlockSpec((tile, ATTN_W), lambda i: (i, 0)),
        out_shape=jax.ShapeDtypeStruct((t, ATTN_W), BF16),
        scratch_shapes=[_slabs(tile, ATTN_W)] * 4,
        compiler_params=_cparams(1), name="combine_fwd")(*os_, *ls_)


def _combine_bwd(dattn, os_, ls_):
    t = dattn.shape[0]
    tile = min(t, TILE)
    e = _head_sum_matrix()

    def body(d_ref, o0, o1, o2, l0, l1, l2, e_ref, do0, do1, do2, c0, c1, c2, *slabs):
        ov, lv = _natural_group_values((o0, o1, o2), (l0, l1, l2), slabs)
        alphas = _alphas(*lv)
        d = d_ref[...]
        attn = alphas[0] * ov[0] + alphas[1] * ov[1] + alphas[2] * ov[2]
        s = _group_sum(d * attn, e_ref[...])
        do0[0] = (alphas[0] * d).astype(BF16)
        c0[0] = -alphas[0] * s
        for g, do_ref, c_ref in ((1, do1, c1), (2, do2, c2)):
            _group_from_natural(slabs[2 * g - 2], do_ref, alphas[g] * d)
            _group_from_natural(slabs[2 * g - 1], c_ref, -alphas[g] * s)

    specs = [_group_spec(d, tile, ATTN_W) for d in DILATIONS]
    shapes = [(d, t // d, ATTN_W) for d in DILATIONS]
    outs = pl.pallas_call(
        body, grid=(t // tile,),
        in_specs=[pl.BlockSpec((tile, ATTN_W), lambda i: (i, 0))] + specs * 2 + [_full((ATTN_W, ATTN_W))],
        out_specs=specs * 2,
        out_shape=[jax.ShapeDtypeStruct(s, BF16) for s in shapes] + [jax.ShapeDtypeStruct(s, F32) for s in shapes],
        scratch_shapes=[_slabs(tile, ATTN_W)] * 4,
        compiler_params=_cparams(1), name="combine_bwd")(dattn, *os_, *ls_, e)
    return outs[:3], outs[3:]


def _attn_bwd(qkv, do, cc, lse, cos_t, sin_t, g, dil):
    t = qkv.shape[0]
    length = t // dil
    nb = length // BLK
    qkv_v = qkv.reshape(dil, length, GROUP_COLS)
    cos_v, sin_v = (a.reshape(dil, length, LANES) for a in (cos_t, sin_t))
    scale = HEAD_DIM ** -0.5

    def body(q_ref, kc_ref, kp_ref, vc_ref, vp_ref, do_ref, c_ref, l_ref, cosc, sinc, cosp, sinp,
             out_ref, dq_s, dk_s, dv_s):
        n = pl.program_id(1)
        valid, upper = _attn_masks(n)
        lower = jnp.logical_not(upper)

        @pl.when(n < nb)
        def _():
            cos_c, sin_c = _lane_tile(cosc[...], ATTN_W), _lane_tile(sinc[...], ATTN_W)
            cos_p, sin_p = _lane_tile(cosp[...], ATTN_W), _lane_tile(sinp[...], ATTN_W)
            dq_parts, dkp_parts, dkc_parts, dvp_parts, dvc_parts = [], [], [], [], []
            for p in range(ATTN_W // LANES):
                sl = slice(p * LANES, (p + 1) * LANES)
                q2 = q_ref[:, sl]
                k2 = jnp.concatenate([kp_ref[:, sl], kc_ref[:, sl]], axis=0)
                v2 = jnp.concatenate([vp_ref[:, sl], vc_ref[:, sl]], axis=0)
                do2 = do_ref[:, sl]
                l2 = l_ref[:, sl]
                c2 = c_ref[:, sl]
                dq2 = jnp.zeros((BLK, LANES), F32)
                dk2 = jnp.zeros((2 * BLK, LANES), F32)
                dv2 = jnp.zeros((2 * BLK, LANES), F32)
                for hh in (0, 1):
                    sel = upper if hh else lower
                    qm = jnp.where(sel, q2, jnp.zeros_like(q2))
                    dom = jnp.where(sel, do2, jnp.zeros_like(do2))
                    l_col = l2[:, hh * HEAD_DIM:hh * HEAD_DIM + 1]
                    c_col = c2[:, hh * HEAD_DIM:hh * HEAD_DIM + 1]
                    s = lax.dot_general(qm, k2, (NT, ((), ())), preferred_element_type=F32)
                    pe = jnp.where(valid, jnp.exp(jnp.where(valid, s, NEG) - l_col), 0.0)
                    dpv = lax.dot_general(dom, v2, (NT, ((), ())), preferred_element_type=F32)
                    ds = (pe * (dpv + c_col)).astype(BF16)
                    pb = pe.astype(BF16)
                    dq2 = dq2 + jnp.where(sel, jnp.dot(ds, k2, preferred_element_type=F32), 0.0)
                    dk2 = dk2 + lax.dot_general(ds, qm, (TN, ((), ())), preferred_element_type=F32)
                    dv2 = dv2 + lax.dot_general(pb, dom, (TN, ((), ())), preferred_element_type=F32)
                dq_parts.append(dq2)
                dkp_parts.append(dk2[:BLK])
                dkc_parts.append(dk2[BLK:])
                dvp_parts.append(dv2[:BLK])
                dvc_parts.append(dv2[BLK:])
            dq = _rope(jnp.concatenate(dq_parts, axis=1) * scale, cos_c, -sin_c)
            dkc = _rope(jnp.concatenate(dkc_parts, axis=1), cos_c, -sin_c)
            dkp = _rope(jnp.concatenate(dkp_parts, axis=1), cos_p, -sin_p)
            dvp = jnp.concatenate(dvp_parts, axis=1)
            dvc = jnp.concatenate(dvc_parts, axis=1)

            @pl.when(n > 0)
            def _():
                out_ref[:, 0:ATTN_W] = dq_s[...].astype(BF16)
                out_ref[:, ATTN_W:2 * ATTN_W] = (dk_s[...] + dkp).astype(BF16)
                out_ref[:, 2 * ATTN_W:3 * ATTN_W] = (dv_s[...] + dvp).astype(BF16)

            dq_s[...] = dq
            dk_s[...] = dkc
            dv_s[...] = dvc

        @pl.when(n == nb)
        def _():
            out_ref[:, 0:ATTN_W] = dq_s[...].astype(BF16)
            out_ref[:, ATTN_W:2 * ATTN_W] = dk_s[...].astype(BF16)
            out_ref[:, 2 * ATTN_W:3 * ATTN_W] = dv_s[...].astype(BF16)

    nc = lambda n: jnp.minimum(n, nb - 1)
    npv = lambda n: jnp.maximum(jnp.minimum(n, nb - 1) - 1, 0)
    cur = lambda part: pl.BlockSpec((None, BLK, ATTN_W), lambda r, n: (r, nc(n), part))
    prev = lambda part: pl.BlockSpec((None, BLK, ATTN_W), lambda r, n: (r, npv(n), part))
    row = pl.BlockSpec((None, BLK, ATTN_W), lambda r, n: (r, nc(n), 0))
    tab_c = pl.BlockSpec((None, BLK, LANES), lambda r, n: (r, nc(n), 0))
    tab_p = pl.BlockSpec((None, BLK, LANES), lambda r, n: (r, npv(n), 0))
    out_spec = pl.BlockSpec((None, BLK, GROUP_COLS), lambda r, n: (r, jnp.maximum(n - 1, 0), 0))
    out = pl.pallas_call(
        body, grid=(dil, nb + 1),
        in_specs=[cur(0), cur(1), prev(1), cur(2), prev(2), row, row, row, tab_c, tab_c, tab_p, tab_p],
        out_specs=out_spec,
        out_shape=jax.ShapeDtypeStruct((dil, length, GROUP_COLS), BF16),
        scratch_shapes=[pltpu.VMEM((BLK, ATTN_W), F32)] * 3,
        compiler_params=_cparams(2), name=f"attn_bwd_g{g}")(
            qkv_v, qkv_v, qkv_v, qkv_v, qkv_v, do, cc, lse, cos_v, sin_v, cos_v, sin_v)
    return out.reshape(t, GROUP_COLS)


SQRT_HALF = 0.7071067811865476
INV_SQRT_2PI = 0.3989422804014327


def _sgu_core(uv, g, b, w_ref, bias):
    cdf = 0.5 * (1.0 + lax.erf(uv * SQRT_HALF))
    z = uv * cdf
    u, v = z[:, :SGU_W], z[:, SGU_W:]
    mu = jnp.mean(v, axis=1, keepdims=True)
    xc = v - mu
    rs = lax.rsqrt(jnp.mean(xc * xc, axis=1, keepdims=True) + EPS)
    xhat = xc * rs
    vn = xhat * g + b
    row = lax.broadcasted_iota(jnp.int32, (SGU_CHUNK, SGU_CHUNK), 0)
    col = lax.broadcasted_iota(jnp.int32, (SGU_CHUNK, SGU_CHUNK), 1)
    tril = row >= col
    upper = lax.broadcasted_iota(jnp.int32, (SGU_CHUNK, LANES), 1) >= SGU_W // SGU_GROUPS
    ws, vlo, vhi, mixed = [], [], [], []
    for pr in range(SGU_W // LANES):
        sl = slice(pr * LANES, (pr + 1) * LANES)
        w0 = jnp.where(tril, w_ref[2 * pr], 0.0).astype(BF16)
        w1 = jnp.where(tril, w_ref[2 * pr + 1], 0.0).astype(BF16)
        vn2 = vn[:, sl]
        lo = jnp.where(upper, 0.0, vn2).astype(BF16)
        hi = jnp.where(upper, vn2, 0.0).astype(BF16)
        mixed.append(jnp.dot(w0, lo, preferred_element_type=F32) + jnp.dot(w1, hi, preferred_element_type=F32)
                     + bias[:, sl])
        ws.append((w0, w1))
        vlo.append(lo)
        vhi.append(hi)
    return cdf, u, xhat, rs, jnp.concatenate(mixed, axis=1), ws, vlo, vhi, tril, upper


def _sgu_fwd(gu, ln_g, ln_b, w_s, bias_exp):
    t = gu.shape[0]

    def body(uv_ref, g_ref, b_ref, w_ref, bias_ref, o_ref):
        _, u, _, _, mixed, *_ = _sgu_core(uv_ref[...].astype(F32), g_ref[...], b_ref[...], w_ref, bias_ref[...])
        o_ref[...] = (u * mixed).astype(BF16)

    return pl.pallas_call(
        body, grid=(t // SGU_CHUNK,),
        in_specs=[pl.BlockSpec((SGU_CHUNK, 2 * SGU_W), lambda n: (n, 2)), _full((1, SGU_W)), _full((1, SGU_W)),
                  _full((SGU_GROUPS, SGU_CHUNK, SGU_CHUNK)), _full((SGU_CHUNK, SGU_W))],
        out_specs=pl.BlockSpec((SGU_CHUNK, SGU_W), lambda n: (n, 0)),
        out_shape=jax.ShapeDtypeStruct((t, SGU_W), BF16),
        compiler_params=_cparams(1), name="sgu_fwd")(gu, ln_g, ln_b, w_s, bias_exp)


def _sgu_bwd(dproj, gu, dsgu, ln_g, ln_b, w_s, bias_exp):
    t = gu.shape[0]
    nchunks = t // SGU_CHUNK
    e = _head_sum_matrix()

    def body(dp_in, uv_ref, ds_ref, g_ref, b_ref, w_ref, bias_ref, e_ref, out_ref, dw_ref, dbias_ref, dg_ref, db_ref):
        n = pl.program_id(0)

        @pl.when(n == 0)
        def _():
            dw_ref[...] = jnp.zeros(dw_ref.shape, F32)
            dbias_ref[...] = jnp.zeros(dbias_ref.shape, F32)
            dg_ref[...] = jnp.zeros(dg_ref.shape, F32)
            db_ref[...] = jnp.zeros(db_ref.shape, F32)

        uv = uv_ref[...].astype(F32)
        g = g_ref[...]
        cdf, u, xhat, rs, mixed, ws, vlo, vhi, tril, upper = _sgu_core(uv, g, b_ref[...], w_ref, bias_ref[...])
        dsg = ds_ref[...]
        du = dsg * mixed
        dmixed = dsg * u
        dbias_ref[...] += dmixed
        dvn = []
        for pr in range(SGU_W // LANES):
            sl = slice(pr * LANES, (pr + 1) * LANES)
            dm2 = dmixed[:, sl]
            dlo = jnp.where(upper, 0.0, dm2).astype(BF16)
            dhi = jnp.where(upper, dm2, 0.0).astype(BF16)
            w0, w1 = ws[pr]
            dvn.append(lax.dot_general(w0, dlo, (TN, ((), ())), preferred_element_type=F32)
                       + lax.dot_general(w1, dhi, (TN, ((), ())), preferred_element_type=F32))
            dw0 = lax.dot_general(dlo, vlo[pr], (NT, ((), ())), preferred_element_type=F32)
            dw1 = lax.dot_general(dhi, vhi[pr], (NT, ((), ())), preferred_element_type=F32)
            dw_ref[2 * pr] += jnp.where(tril, dw0, 0.0)
            dw_ref[2 * pr + 1] += jnp.where(tril, dw1, 0.0)
        dvn = jnp.concatenate(dvn, axis=1)
        dg_ref[...] += jnp.sum(dvn * xhat, axis=0, keepdims=True)
        db_ref[...] += jnp.sum(dvn, axis=0, keepdims=True)
        dxh = dvn * g
        dv = rs * (dxh - jnp.mean(dxh, axis=1, keepdims=True) - xhat * jnp.mean(dxh * xhat, axis=1, keepdims=True))
        dz = jnp.concatenate([du, dv], axis=1)
        dgelu = cdf + uv * (INV_SQRT_2PI * jnp.exp(-0.5 * uv * uv))
        out_ref[...] = (dz * dgelu).astype(BF16)

        @pl.when(n == nchunks - 1)
        def _():
            dbias_ref[...] = _group_sum(dbias_ref[...], e_ref[...])

    outs = pl.pallas_call(
        body, grid=(nchunks,),
        in_specs=[pl.BlockSpec(memory_space=pl.ANY), pl.BlockSpec((SGU_CHUNK, 2 * SGU_W), lambda n: (n, 2)),
                  pl.BlockSpec((SGU_CHUNK, SGU_W), lambda n: (n, 0)), _full((1, SGU_W)), _full((1, SGU_W)),
                  _full((SGU_GROUPS, SGU_CHUNK, SGU_CHUNK)), _full((SGU_CHUNK, SGU_W)), _full((ATTN_W, ATTN_W))],
        out_specs=[pl.BlockSpec((SGU_CHUNK, 2 * SGU_W), lambda n: (n, 2)), _full((SGU_GROUPS, SGU_CHUNK, SGU_CHUNK)),
                   _full((SGU_CHUNK, SGU_W)), _full((1, SGU_W)), _full((1, SGU_W))],
        out_shape=[jax.ShapeDtypeStruct(dproj.shape, BF16), jax.ShapeDtypeStruct((SGU_GROUPS, SGU_CHUNK, SGU_CHUNK), F32),
                   jax.ShapeDtypeStruct((SGU_CHUNK, SGU_W), F32), jax.ShapeDtypeStruct((1, SGU_W), F32),
                   jax.ShapeDtypeStruct((1, SGU_W), F32)],
        input_output_aliases={0: 0},
        compiler_params=_cparams(1), name="sgu_bwd")(dproj, gu, dsgu, ln_g, ln_b, w_s, bias_exp, e)
    return outs


def _merge_fwd(attn, sgu, gu, x, w_pa, w_ps, w_out, g2):
    t = x.shape[0]
    tm = min(t, 256)

    def body(a_ref, s_ref, ga_ref, gb_ref, x_ref, wpa, wps, wo, g_ref, pa_ref, ps_ref, m_ref, x1_ref, h2_ref):
        pa = jnp.dot(a_ref[...], wpa[...], preferred_element_type=F32)
        ps = jnp.dot(s_ref[...], wps[...], preferred_element_type=F32)
        merged = (_sigmoid(ga_ref[...].astype(F32)) * pa + _sigmoid(gb_ref[...].astype(F32)) * ps).astype(BF16)
        x1 = x_ref[...] + jnp.dot(merged, wo[...], preferred_element_type=F32)
        xhat, _ = _rms_stats(x1)
        pa_ref[...] = pa.astype(BF16)
        ps_ref[...] = ps.astype(BF16)
        m_ref[...] = merged
        x1_ref[...] = x1
        h2_ref[...] = (xhat * g_ref[...]).astype(BF16)

    half = pl.BlockSpec((tm, ATTN_W), lambda i: (i, 0))
    full = pl.BlockSpec((tm, D_MODEL), lambda i: (i, 0))
    return pl.pallas_call(
        body, grid=(t // tm,),
        in_specs=[half, half, pl.BlockSpec((tm, D_MODEL), lambda i: (i, 0)), pl.BlockSpec((tm, D_MODEL), lambda i: (i, 1)),
                  full, _full((ATTN_W, D_MODEL)), _full((SGU_W, D_MODEL)), _full((D_MODEL, D_MODEL)), _full((1, D_MODEL))],
        out_specs=[full] * 5,
        out_shape=[jax.ShapeDtypeStruct((t, D_MODEL), BF16), jax.ShapeDtypeStruct((t, D_MODEL), BF16),
                   jax.ShapeDtypeStruct((t, D_MODEL), BF16), jax.ShapeDtypeStruct((t, D_MODEL), F32),
                   jax.ShapeDtypeStruct((t, D_MODEL), BF16)],
        compiler_params=_cparams(1), name="merge_fwd")(attn, sgu, gu, gu, x, w_pa, w_ps, w_out, g2)


def _merge_bwd(dx1b, gu, pa, ps, w_pa, w_ps, w_out):
    t = dx1b.shape[0]
    tm = min(t, 256)

    def body(d_ref, ga_ref, gb_ref, pa_ref, ps_ref, wpa, wps, wo, out_ref, dpa_ref, dps_ref, da_ref, dsg_ref):
        dm = lax.dot_general(d_ref[...], wo[...], (NT, ((), ())), preferred_element_type=F32)
        sa, sb = _sigmoid(ga_ref[...].astype(F32)), _sigmoid(gb_ref[...].astype(F32))
        dpa = (dm * sa).astype(BF16)
        dps = (dm * sb).astype(BF16)
        out_ref[:, 0:D_MODEL] = (dm * pa_ref[...].astype(F32) * sa * (1.0 - sa)).astype(BF16)
        out_ref[:, D_MODEL:2 * D_MODEL] = (dm * ps_ref[...].astype(F32) * sb * (1.0 - sb)).astype(BF16)
        out_ref[:, 2 * D_MODEL:GU_COLS] = jnp.zeros((tm, GU_COLS - 2 * D_MODEL), BF16)
        dpa_ref[...] = dpa
        dps_ref[...] = dps
        da_ref[...] = lax.dot_general(dpa, wpa[...], (NT, ((), ())), preferred_element_type=F32)
        dsg_ref[...] = lax.dot_general(dps, wps[...], (NT, ((), ())), preferred_element_type=F32)

    half = pl.BlockSpec((tm, ATTN_W), lambda i: (i, 0))
    full = pl.BlockSpec((tm, D_MODEL), lambda i: (i, 0))
    return pl.pallas_call(
        body, grid=(t // tm,),
        in_specs=[full, pl.BlockSpec((tm, D_MODEL), lambda i: (i, 0)),
                  pl.BlockSpec((tm, D_MODEL), lambda i: (i, 1)), full, full,
                  _full((ATTN_W, D_MODEL)), _full((SGU_W, D_MODEL)), _full((D_MODEL, D_MODEL))],
        out_specs=[pl.BlockSpec((tm, GU_COLS), lambda i: (i, 0)), full, full, half, half],
        out_shape=[jax.ShapeDtypeStruct((t, GU_COLS), BF16), jax.ShapeDtypeStruct((t, D_MODEL), BF16),
                   jax.ShapeDtypeStruct((t, D_MODEL), BF16), jax.ShapeDtypeStruct((t, ATTN_W), F32),
                   jax.ShapeDtypeStruct((t, SGU_W), F32)],
        compiler_params=_cparams(1), name="merge_bwd")(dx1b, gu, gu, pa, ps, w_pa, w_ps, w_out)


def _token_call(name, body, t, tm, ins, outs, reds=(), scratch=()):
    return pl.pallas_call(
        body, grid=(t // tm,), in_specs=[s for _, s in ins],
        out_specs=[o[2] for o in outs] + [_full(r) for r in reds],
        out_shape=[jax.ShapeDtypeStruct(o[0], o[1]) for o in outs] + [jax.ShapeDtypeStruct(r, F32) for r in reds],
        scratch_shapes=list(scratch), compiler_params=_cparams(1), name=name)(*[a for a, _ in ins])


def _rows_spec(tm, width):
    return pl.BlockSpec((tm, width), lambda i: (i, 0))


def _chips_spec(tm):
    return pl.BlockSpec((N_CHIPS, tm, FF_SHARD), lambda i: (0, i, 0))


def _zero_at_start(*refs):
    @pl.when(pl.program_id(0) == 0)
    def _():
        for r in refs:
            r[...] = jnp.zeros(r.shape, r.dtype)


def _ffn_fwd(h2, w_g, w_u):
    t = h2.shape[0]
    tm = min(t, 512)

    def body(h_ref, wg_ref, wu_ref, a_ref, b_ref, ff_ref):
        h = h_ref[...]
        for s in range(N_CHIPS):
            a = jnp.dot(h, wg_ref[s], preferred_element_type=F32)
            b = jnp.dot(h, wu_ref[s], preferred_element_type=F32)
            a_ref[s] = a.astype(BF16)
            b_ref[s] = b.astype(BF16)
            ff_ref[s] = (a * _sigmoid(a) * b).astype(BF16)

    shp = (N_CHIPS, t, FF_SHARD)
    w_spec = _resident((N_CHIPS, D_MODEL, FF_SHARD))
    return _token_call("ffn_fwd", body, t, tm, [(h2, _rows_spec(tm, D_MODEL)), (w_g, w_spec), (w_u, w_spec)],
                       [(shp, BF16, _chips_spec(tm))] * 3)


def _ffn_down_loss(ff, w_d, x1, tgt, gf):
    t = x1.shape[0]
    tm = min(t, 512)

    def body(ff_ref, wd_ref, x1_ref, tgt_ref, g_ref, dx2_ref, dx2b_ref, loss_ref, dgf_ref):
        _zero_at_start(loss_ref, dgf_ref)
        acc = jnp.dot(ff_ref[0], wd_ref[0], preferred_element_type=F32)
        for s in range(1, N_CHIPS):
            acc = acc + jnp.dot(ff_ref[s], wd_ref[s], preferred_element_type=F32)
        x2 = x1_ref[...] + acc
        g = g_ref[...]
        xhat, rr = _rms_stats(x2)
        diff = xhat * g - tgt_ref[...]
        rows = jnp.sum(diff * diff, axis=1, keepdims=True)
        loss_ref[...] += jnp.broadcast_to(jnp.sum(rows, axis=0, keepdims=True) * (0.5 / D_MODEL), (1, LANES))
        dy = diff * (1.0 / D_MODEL)
        dgf_ref[...] += jnp.sum(dy * xhat, axis=0, keepdims=True)
        dx2 = _rms_bwd(dy, xhat, rr, g)
        dx2_ref[...] = dx2
        dx2b_ref[...] = dx2.astype(BF16)

    row = _rows_spec(tm, D_MODEL)
    return _token_call("ffn_down_loss", body, t, tm,
                       [(ff, _chips_spec(tm)), (w_d, _resident((N_CHIPS, FF_SHARD, D_MODEL))), (x1, row), (tgt, row),
                        (gf, _full((1, D_MODEL)))],
                       [((t, D_MODEL), F32, row), ((t, D_MODEL), BF16, row)], reds=[(1, LANES), (1, D_MODEL)])


def _ffn_bwd_act(dx2b, w_d, a, b):
    t = dx2b.shape[0]
    tm = min(t, 512)

    def body(d_ref, wd_ref, a_ref, b_ref, da_ref, db_ref):
        d = d_ref[...]
        for s in range(N_CHIPS):
            dff = lax.dot_general(d, wd_ref[s], (NT, ((), ())), preferred_element_type=F32)
            av, bv = a_ref[s].astype(F32), b_ref[s].astype(F32)
            sg = _sigmoid(av)
            da_ref[s] = (dff * bv * (sg * (1.0 + av * (1.0 - sg)))).astype(BF16)
            db_ref[s] = (dff * (av * sg)).astype(BF16)

    shp = (N_CHIPS, t, FF_SHARD)
    return _token_call("ffn_bwd_act", body, t, tm,
                       [(dx2b, _rows_spec(tm, D_MODEL)), (w_d, _resident((N_CHIPS, FF_SHARD, D_MODEL))),
                        (a, _chips_spec(tm)), (b, _chips_spec(tm))],
                       [(shp, BF16, _chips_spec(tm))] * 2)


def _ffn_bwd_in(da, db, w_g, w_u, x1, dx2, g2):
    t = x1.shape[0]
    tm = min(t, 512)

    def body(da_ref, db_ref, wg_ref, wu_ref, x1_ref, dx2_ref, g_ref, dx1_ref, dx1b_ref, dg_ref):
        _zero_at_start(dg_ref)
        acc = None
        for s in range(N_CHIPS):
            part = (lax.dot_general(da_ref[s], wg_ref[s], (NT, ((), ())), preferred_element_type=F32)
                    + lax.dot_general(db_ref[s], wu_ref[s], (NT, ((), ())), preferred_element_type=F32))
            acc = part if acc is None else acc + part
        xhat, rr = _rms_stats(x1_ref[...])
        dg_ref[...] += jnp.sum(acc * xhat, axis=0, keepdims=True)
        dx1 = dx2_ref[...] + _rms_bwd(acc, xhat, rr, g_ref[...])
        dx1_ref[...] = dx1
        dx1b_ref[...] = dx1.astype(BF16)

    row = _rows_spec(tm, D_MODEL)
    w_spec = _resident((N_CHIPS, D_MODEL, FF_SHARD))
    return _token_call("ffn_bwd_in", body, t, tm,
                       [(da, _chips_spec(tm)), (db, _chips_spec(tm)), (w_g, w_spec), (w_u, w_spec), (x1, row), (dx2, row),
                        (g2, _full((1, D_MODEL)))],
                       [((t, D_MODEL), F32, row), ((t, D_MODEL), BF16, row)], reds=[(1, D_MODEL)])


def _in_proj_bwd(dgu, dqkvs, w_p, x, dx1, g1):
    t = x.shape[0]
    tile = min(t, TILE)
    tm = min(t, 1024)
    nat_cols = GU_COLS + GROUP_COLS

    dhs = []
    for g in (1, 2):
        col = (GU_COLS + g * GROUP_COLS) // GROUP_COLS

        def body_g(d_ref, w_ref, o_ref):
            o_ref[...] = lax.dot_general(d_ref[...], w_ref[...], (NT, ((), ())), preferred_element_type=F32)

        dh = _token_call(
            f"in_proj_bwd_g{g}", body_g, t, tm,
            [(dqkvs[g], _rows_spec(tm, GROUP_COLS)),
             (w_p, pl.BlockSpec((D_MODEL, GROUP_COLS), lambda i, col=col: (0, col), pipeline_mode=pl.Buffered(1)))],
            [((t, D_MODEL), F32, _rows_spec(tm, D_MODEL))])[0]
        dhs.append(dh.reshape(DILATIONS[g], t // DILATIONS[g], D_MODEL))

    def body(dgu_ref, dq0_ref, w_ref, x_ref, dx1_ref, g_ref, dh1_ref, dh2_ref, dx_ref, dg_ref, slab):
        _zero_at_start(dg_ref)
        dh = lax.dot_general(dgu_ref[...], w_ref[:, 0:GU_COLS], (NT, ((), ())), preferred_element_type=F32)
        dh = dh + lax.dot_general(dq0_ref[...], w_ref[:, GU_COLS:nat_cols], (NT, ((), ())), preferred_element_type=F32)
        dh = dh + _natural_from_group(slab, dh1_ref)
        dh = dh + _natural_from_group(slab, dh2_ref)
        xhat, rr = _rms_stats(x_ref[...])
        dg_ref[...] += jnp.sum(dh * xhat, axis=0, keepdims=True)
        dx_ref[...] = dx1_ref[...] + _rms_bwd(dh, xhat, rr, g_ref[...])

    row = _rows_spec(tile, D_MODEL)
    return _token_call(
        "in_proj_bwd", body, t, tile,
        [(dgu, _rows_spec(tile, GU_COLS)), (dqkvs[0], _rows_spec(tile, GROUP_COLS)),
         (w_p, pl.BlockSpec((D_MODEL, nat_cols), lambda i: (0, 0), pipeline_mode=pl.Buffered(1))),
         (x, row), (dx1, row), (g1, _full((1, D_MODEL))),
         (dhs[0], _group_spec(DILATIONS[1], tile, D_MODEL)), (dhs[1], _group_spec(DILATIONS[2], tile, D_MODEL))],
        [((t, D_MODEL), F32, row)], reds=[(1, D_MODEL)], scratch=[_slabs(tile, D_MODEL)])


def _epi_bf16(acc, e, o, r, ids):
    o[0][...] = acc.astype(BF16)


def _wgrad_2d(name, a, b, tm, tn, tk=512):
    t, k1 = a.shape
    n = b.shape[1]
    tk = min(t, tk)
    return _mm(name, (k1 // tm, n // tn, t // tk),
               [(a, pl.BlockSpec((tk, tm), lambda i, j, k: (k, i)), b, pl.BlockSpec((tk, tn), lambda i, j, k: (k, j)))],
               TN, (tm, tn), _epi_bf16, outs=[((k1, n), BF16, pl.BlockSpec((tm, tn), lambda i, j, k: (i, j)))])[0]


def _wgrad_in(hs, dgu, dqkvs, tk=512):
    t = dgu.shape[0]
    tk = min(t, tk)
    dst = None
    parts = [(hs[0], dgu, 0)] + [(hs[g].reshape(t, D_MODEL), dqkvs[g], GU_COLS // GROUP_COLS + g) for g in range(3)]
    for n, (a, b, col0) in enumerate(parts):
        dst = _mm(f"wgrad_in_{n}", (1, b.shape[1] // GROUP_COLS, t // tk),
                  [(a, pl.BlockSpec((tk, D_MODEL), lambda i, j, k: (k, 0)), b,
                    pl.BlockSpec((tk, GROUP_COLS), lambda i, j, k: (k, j)))],
                  TN, (D_MODEL, GROUP_COLS), _epi_bf16,
                  extras=[] if dst is None else [(dst, pl.BlockSpec(memory_space=pl.ANY))],
                  outs=[((D_MODEL, IN_COLS), BF16, pl.BlockSpec((D_MODEL, GROUP_COLS), lambda i, j, k, col0=col0: (0, j + col0)))],
                  aliases=None if dst is None else {2: 0})[0]
    return dst


def _wgrad_ff_in(name, h2, da, tk=512):
    t = h2.shape[0]
    tk = min(t, tk)
    return _mm(name, (N_CHIPS, 1, t // tk),
               [(h2, pl.BlockSpec((tk, D_MODEL), lambda i, j, k: (k, 0)),
                 da, pl.BlockSpec((None, tk, FF_SHARD), lambda i, j, k: (i, k, 0)))],
               TN, (D_MODEL, FF_SHARD), _epi_bf16,
               outs=[((N_CHIPS, D_MODEL, FF_SHARD), BF16, pl.BlockSpec((None, D_MODEL, FF_SHARD), lambda i, j, k: (i, 0, 0)))])[0]


def _wgrad_ff_down(ff, dx2b, tk=512):
    t = dx2b.shape[0]
    tk = min(t, tk)
    return _mm("wgrad_ffn_down", (N_CHIPS, 1, t // tk),
               [(ff, pl.BlockSpec((None, tk, FF_SHARD), lambda i, j, k: (i, k, 0)),
                 dx2b, pl.BlockSpec((tk, D_MODEL), lambda i, j, k: (k, 0)))],
               TN, (FF_SHARD, D_MODEL), _epi_bf16,
               outs=[((N_CHIPS, FF_SHARD, D_MODEL), BF16, pl.BlockSpec((None, FF_SHARD, D_MODEL), lambda i, j, k: (i, 0, 0)))])[0]


def _local_step(x, pos_col, tgt, g1, ln_g, ln_b, w_s, b_s, g2, gf, w_p, w_pa, w_ps, w_out, w_g, w_u, w_d):
    t = x.shape[0]
    tables = [_rope_table(f"rope_table_g{g}", jnp.transpose(pos_col.reshape(t // d, d)).reshape(t, 1))
              for g, d in enumerate(DILATIONS)]
    bias_exp = jnp.repeat(jnp.transpose(b_s), SGU_W // SGU_GROUPS, axis=1)

    hs = _norm_fwd(x, g1)
    gu, qkvs = _in_proj(hs, w_p, tables)
    os_, ls_ = [], []
    for g, dil in enumerate(DILATIONS):
        o, lse = _attn_fwd(qkvs[g], g, dil)
        os_.append(o)
        ls_.append(lse)
    attn = _combine_fwd(os_, ls_)
    sgu = _sgu_fwd(gu, ln_g, ln_b, w_s, bias_exp)
    pa, ps, merged, x1, h2 = _merge_fwd(attn, sgu, gu, x, w_pa, w_ps, w_out, g2)
    a, b, ff = _ffn_fwd(h2, w_g, w_u)
    dx2, dx2b, loss, dgf = _ffn_down_loss(ff, w_d, x1, tgt, gf)

    da, db = _ffn_bwd_act(dx2b, w_d, a, b)
    dw_d = _wgrad_ff_down(ff, dx2b)
    dx1, dx1b, dg2 = _ffn_bwd_in(da, db, w_g, w_u, x1, dx2, g2)
    dw_g = _wgrad_ff_in("wgrad_ffn_gate", h2, da)
    dw_u = _wgrad_ff_in("wgrad_ffn_up", h2, db)

    dgu, dpa, dps, dattn, dsgu = _merge_bwd(dx1b, gu, pa, ps, w_pa, w_ps, w_out)
    dw_out = _wgrad_2d("wgrad_out", merged, dx1b, D_MODEL, D_MODEL)
    dw_pa = _wgrad_2d("wgrad_proj_attn", attn, dpa, ATTN_W, D_MODEL)
    dw_ps = _wgrad_2d("wgrad_proj_sgu", sgu, dps, SGU_W, D_MODEL)
    dgu, dw_s, dbias, dln_g, dln_b = _sgu_bwd(dgu, gu, dsgu, ln_g, ln_b, w_s, bias_exp)
    dos, ccs = _combine_bwd(dattn, os_, ls_)
    dqkvs = [_attn_bwd(qkvs[g], dos[g], ccs[g], ls_[g], *tables[g], g, dil) for g, dil in enumerate(DILATIONS)]
    dx, dg1 = _in_proj_bwd(dgu, dqkvs, w_p, x, dx1, g1)
    dw_p = _wgrad_in(hs, dgu, dqkvs)

    db_s = jnp.transpose(dbias[:, ::SGU_W // SGU_GROUPS])
    small = dict(loss=loss, norm1_g=dg1, sgu_ln_g=dln_g, sgu_ln_b=dln_b, w_spatial=dw_s, b_spatial=db_s,
                 norm2_g=dg2, final_g=dgf)
    big = dict(w_in=dw_p, w_proj_attn=dw_pa, w_proj_sgu=dw_ps, w_out=dw_out, w_ffn_gate=dw_g, w_ffn_up=dw_u,
               w_ffn_down=dw_d)
    return dx, big, small


def _ew(name, fn, ins, out_dtypes):
    shp = ins[0].shape
    rows, cols = shp
    tr = next((cand for cand in (256, 352, 128) if rows % cand == 0 and rows > cand), rows)

    def body(*refs):
        res = fn(*[r[...] for r in refs[:len(ins)]])
        for o_ref, v in zip(refs[len(ins):], res):
            o_ref[...] = v.astype(o_ref.dtype)

    spec = pl.BlockSpec((tr, cols), lambda i: (i, 0))
    return pl.pallas_call(
        body, grid=(rows // tr,), in_specs=[spec] * len(ins), out_specs=[spec] * len(out_dtypes),
        out_shape=[jax.ShapeDtypeStruct(shp, d) for d in out_dtypes],
        compiler_params=_cparams(1), name=name)(*ins)


def _adamw_math(g, w, m, v):
    m = ADAM_B1 * m + (1.0 - ADAM_B1) * g
    v = ADAM_B2 * v + (1.0 - ADAM_B2) * (g * g)
    m_hat = m / (1.0 - ADAM_B1 ** ADAM_STEP)
    v_hat = v / (1.0 - ADAM_B2 ** ADAM_STEP)
    delta = -ADAM_LR * (m_hat / (jnp.sqrt(v_hat) + ADAM_EPS) + ADAM_WD * w)
    return delta, m, v


def _adamw(name, g, w, m, v):
    return _ew(name, _adamw_math, [g, w, m, v], [F32, F32, F32])


VMEM_SPEC = pl.BlockSpec(memory_space=pltpu.VMEM)


def _for_row_chunks(rows, fn):
    ck = next(c for c in (64, 32, 16) if rows % c == 0)

    def step(i, carry):
        fn(pl.multiple_of(i * ck, ck), ck)
        return carry

    lax.fori_loop(0, rows // ck, step, 0)


def _place():
    x, y, c = lax.axis_index("x"), lax.axis_index("y"), lax.axis_index("c")
    chips = [(1 - x, y), (x, 1 - y), (1 - x, 1 - y)]
    return x, y, c, 2 * x + y, chips


def _rows(ref, start, size):
    if len(ref.shape) == 2:
        return ref.at[pl.ds(start, size), :]
    return ref.at[:, pl.ds(start, size), :]


def _comm_call(name, body, ins, out_shapes, scratch, n_remote):
    return pl.pallas_call(
        body, in_specs=[VMEM_SPEC] * len(ins), out_specs=[VMEM_SPEC] * len(out_shapes),
        out_shape=out_shapes,
        scratch_shapes=list(scratch) + [pltpu.SemaphoreType.DMA((n_remote,)), pltpu.SemaphoreType.DMA((n_remote,))],
        compiler_params=pltpu.CompilerParams(vmem_limit_bytes=VMEM_LIMIT), name=name)(*ins)


def _gather_weights(name, shards):
    nt = len(shards)

    def body(*refs):
        ins, outs = refs[:nt], refs[nt:2 * nt]
        send, recv = refs[2 * nt:]
        x, y, c, me, chips = _place()
        sibling = (x, y, 1 - c)
        firsts, passed, expects = [], [], []
        for t in range(nt):
            kh = ins[t].shape[0] // 2
            for j, chip in enumerate(chips):
                k = t * 3 + j
                theirs = 2 * chip[0] + chip[1]
                firsts.append(pltpu.make_async_remote_copy(
                    src_ref=_rows(ins[t], c * kh, kh), dst_ref=_rows(outs[t].at[me], c * kh, kh),
                    send_sem=send.at[k], recv_sem=recv.at[k], device_id=(*chip, c), device_id_type=MESH))
                landed = _rows(outs[t].at[theirs], c * kh, kh)
                expects.append(pltpu.make_async_remote_copy(
                    src_ref=landed, dst_ref=landed, send_sem=send.at[k], recv_sem=recv.at[k],
                    device_id=(*chip, c), device_id_type=MESH))
                passed.append(pltpu.make_async_remote_copy(
                    src_ref=landed, dst_ref=landed, send_sem=send.at[3 * nt + k], recv_sem=recv.at[3 * nt + k],
                    device_id=sibling, device_id_type=MESH))
        for cp in firsts:
            cp.start()
        for t in range(nt):
            mine = outs[t].at[me]

            def put(r0, ck, src=ins[t], dst=mine):
                dst[pl.ds(r0, ck), :] = src[pl.ds(r0, ck), :]

            _for_row_chunks(ins[t].shape[0], put)
        for k in range(3 * nt):
            expects[k].wait_recv()
            passed[k].start()
        for t in range(nt):
            kh = ins[t].shape[0] // 2
            for j, chip in enumerate(chips):
                k = t * 3 + j
                theirs = 2 * chip[0] + chip[1]
                other = _rows(outs[t].at[theirs], (1 - c) * kh, kh)
                pltpu.make_async_remote_copy(
                    src_ref=other, dst_ref=other, send_sem=send.at[3 * nt + k], recv_sem=recv.at[3 * nt + k],
                    device_id=sibling, device_id_type=MESH).wait_recv()
        for cp in firsts + passed:
            cp.wait_send()

    out_shapes = [jax.ShapeDtypeStruct((N_CHIPS,) + s.shape, s.dtype) for s in shards]
    return _comm_call(name, body, shards, out_shapes, [], 6 * nt)


def _pair_reduce(name, grads):
    nt = len(grads)

    def half(s):
        shp = list(s.shape)
        shp[-2] //= 2
        return tuple(shp)

    def body(*refs):
        ins, outs, got = refs[:nt], refs[nt:2 * nt], refs[2 * nt:3 * nt]
        send, recv = refs[3 * nt:]
        x, y, c, me, chips = _place()
        copies = []
        for t in range(nt):
            kh = ins[t].shape[-2] // 2
            rc = pltpu.make_async_remote_copy(
                src_ref=_rows(ins[t], (1 - c) * kh, kh), dst_ref=got[t], send_sem=send.at[t], recv_sem=recv.at[t],
                device_id=(x, y, 1 - c), device_id_type=MESH)
            rc.start()
            copies.append(rc)
        for t in range(nt):
            kh = ins[t].shape[-2] // 2
            copies[t].wait_recv()
            for lead in ([()] if len(ins[t].shape) == 2 else [(s,) for s in range(ins[t].shape[0])]):

                def add(r0, ck, lead=lead, src=ins[t], oth=got[t], dst=outs[t], kh=kh):
                    own = src[lead + (pl.ds(pl.multiple_of(c * kh + r0, ck), ck), slice(None))]
                    rows = lead + (pl.ds(r0, ck), slice(None))
                    dst[rows] = (own.astype(F32) + oth[rows].astype(F32)).astype(BF16)

                _for_row_chunks(kh, add)
        for rc in copies:
            rc.wait_send()

    shapes = [half(g) for g in grads]
    return _comm_call(name, body, grads, [jax.ShapeDtypeStruct(s, BF16) for s in shapes],
                      [pltpu.VMEM(s, BF16) for s in shapes], nt)


def _chip_reduce(name, sums):
    nt = len(sums)

    def cols(s):
        return s[2] if len(s) == 3 else s[1] // N_CHIPS

    def piece(ref, j):
        if len(ref.shape) == 3:
            return ref.at[j]
        n4 = ref.shape[1] // N_CHIPS
        return ref.at[:, pl.ds(j * n4, n4)]

    def body(*refs):
        ins, outs, slots = refs[:nt], refs[nt:2 * nt], refs[2 * nt:3 * nt]
        send, recv = refs[3 * nt:]
        x, y, c, me, chips = _place()
        sibling = (x, y, 1 - c)
        copies = []
        for t in range(nt):
            for j, chip in enumerate(chips):
                k = t * 3 + j
                rc = pltpu.make_async_remote_copy(
                    src_ref=piece(ins[t], 2 * chip[0] + chip[1]), dst_ref=slots[t].at[j], send_sem=send.at[k],
                    recv_sem=recv.at[k], device_id=(*chip, c), device_id_type=MESH)
                rc.start()
                copies.append(rc)
        handed = []
        for t in range(nt):
            kh, n4 = ins[t].shape[-2], outs[t].shape[1]
            for j in range(3):
                copies[t * 3 + j].wait_recv()
            for jj in range(N_CHIPS):

                @pl.when(me == jj)
                def _(jj=jj, src=ins[t], slot=slots[t], dst=outs[t], kh=kh, n4=n4):
                    def add(r0, ck):
                        rows = pl.ds(r0, ck)
                        own = src[jj, rows, :] if len(src.shape) == 3 else src[rows, jj * n4:(jj + 1) * n4]
                        acc = ((own.astype(F32) + slot[0, rows, :].astype(F32)) + slot[1, rows, :].astype(F32)) \
                            + slot[2, rows, :].astype(F32)
                        dst[pl.ds(pl.multiple_of(c * kh + r0, ck), ck), :] = acc

                    _for_row_chunks(kh, add)

            rc = pltpu.make_async_remote_copy(
                src_ref=_rows(outs[t], c * kh, kh), dst_ref=_rows(outs[t], c * kh, kh), send_sem=send.at[3 * nt + t],
                recv_sem=recv.at[3 * nt + t], device_id=sibling, device_id_type=MESH)
            rc.start()
            handed.append(rc)
        for t in range(nt):
            kh = ins[t].shape[-2]
            other = _rows(outs[t], (1 - c) * kh, kh)
            pltpu.make_async_remote_copy(
                src_ref=other, dst_ref=other, send_sem=send.at[3 * nt + t], recv_sem=recv.at[3 * nt + t],
                device_id=sibling, device_id_type=MESH).wait_recv()
        for rc in copies + handed:
            rc.wait_send()

    out_shapes = [jax.ShapeDtypeStruct((2 * s.shape[-2], cols(s.shape)), F32) for s in sums]
    scratch = [pltpu.VMEM((3, s.shape[-2], cols(s.shape)), BF16) for s in sums]
    return _comm_call(name, body, sums, out_shapes, scratch, 4 * nt)


def _allreduce_small(buf):
    shp = buf.shape

    def body(in_ref, out_ref, pair_ref, slot_ref, send, recv):
        x, y, c, me, chips = _place()
        sibling = (x, y, 1 - c)
        first = pltpu.make_async_remote_copy(src_ref=in_ref, dst_ref=pair_ref, send_sem=send.at[0], recv_sem=recv.at[0],
                                             device_id=sibling, device_id_type=MESH)
        first.start()
        first.wait_recv()
        slot_ref[me] = in_ref[...] + pair_ref[...]
        copies = []
        for j, chip in enumerate(chips):
            theirs = 2 * chip[0] + chip[1]
            rc = pltpu.make_async_remote_copy(src_ref=slot_ref.at[me], dst_ref=slot_ref.at[me], send_sem=send.at[1 + j],
                                              recv_sem=recv.at[1 + j], device_id=(*chip, c), device_id_type=MESH)
            rc.start()
            arrive = pltpu.make_async_remote_copy(src_ref=slot_ref.at[theirs], dst_ref=slot_ref.at[theirs],
                                                  send_sem=send.at[1 + j], recv_sem=recv.at[1 + j],
                                                  device_id=(*chip, c), device_id_type=MESH)
            copies.append((rc, arrive))
        for rc, arrive in copies:
            arrive.wait_recv()
        out_ref[...] = ((slot_ref[0] + slot_ref[1]) + slot_ref[2]) + slot_ref[3]
        first.wait_send()
        for rc, arrive in copies:
            rc.wait_send()

    vm = pl.BlockSpec(memory_space=pltpu.VMEM)
    return pl.pallas_call(
        body, in_specs=[vm], out_specs=vm, out_shape=jax.ShapeDtypeStruct(shp, F32),
        scratch_shapes=[pltpu.VMEM(shp, F32), pltpu.VMEM((N_CHIPS,) + shp, F32),
                        pltpu.SemaphoreType.DMA((4,)), pltpu.SemaphoreType.DMA((4,))],
        compiler_params=pltpu.CompilerParams(vmem_limit_bytes=VMEM_LIMIT),
        name="allreduce_small")(buf)


SMALL_ORDER = ("norm1_g", "norm2_g", "final_g", "sgu_ln_g", "sgu_ln_b", "b_spatial", "loss", "w_spatial")
SMALL_SIZES = dict(norm1_g=1024, norm2_g=1024, final_g=1024, sgu_ln_g=512, sgu_ln_b=512, b_spatial=1024, loss=LANES,
                   w_spatial=SGU_GROUPS * SGU_CHUNK * SGU_CHUNK)
SMALL_ROWS = 1072


def _pack_small(parts):
    flat = [jnp.reshape(parts[n].astype(F32), (-1,)) for n in SMALL_ORDER]
    used = sum(SMALL_SIZES[n] for n in SMALL_ORDER)
    flat.append(jnp.zeros((SMALL_ROWS * LANES - used,), F32))
    return jnp.concatenate(flat).reshape(SMALL_ROWS, LANES)


def _unpack_small(buf, shapes):
    flat = buf.reshape(-1)
    out, off = {}, 0
    for n in SMALL_ORDER:
        sz = SMALL_SIZES[n]
        if n in shapes:
            out[n] = flat[off:off + sz].reshape(shapes[n])
        off += sz
    return out


BIG = ("w_in", "w_proj_attn", "w_proj_sgu", "w_out", "w_ffn_gate", "w_ffn_up", "w_ffn_down")
COMM_GROUPS = (("w_in",), ("w_proj_attn", "w_proj_sgu", "w_out", "w_ffn_gate", "w_ffn_up", "w_ffn_down"))
WEIGHTS = ("norm1_g", "w_in", "sgu_ln_g", "sgu_ln_b", "w_spatial", "b_spatial", "w_proj_attn", "w_proj_sgu", "w_out",
           "norm2_g", "w_ffn_gate", "w_ffn_up", "w_ffn_down", "final_g")


def _cols_from_chips(g):
    return jnp.transpose(g, (1, 0, 2)).reshape(g.shape[1], N_CHIPS * g.shape[2])


def _permute_cols(w, perm):
    return jnp.concatenate([w[:, 512 * b:512 * (b + 1)] for b in perm], axis=1)


def kernel(x, positions, norm1_g, w_in, sgu_ln_g, sgu_ln_b, w_spatial, b_spatial, w_proj_attn, w_proj_sgu, w_out, norm2_g, w_ffn_gate, w_ffn_up, w_ffn_down, final_g, loss_target, m_norm1_g, m_w_in, m_sgu_ln_g, m_sgu_ln_b, m_w_spatial, m_b_spatial, m_w_proj_attn, m_w_proj_sgu, m_w_out, m_norm2_g, m_w_ffn_gate, m_w_ffn_up, m_w_ffn_down, m_final_g, v_norm1_g, v_w_in, v_sgu_ln_g, v_sgu_ln_b, v_w_spatial, v_b_spatial, v_w_proj_attn, v_w_proj_sgu, v_w_out, v_norm2_g, v_w_ffn_gate, v_w_ffn_up, v_w_ffn_down, v_final_g):
    w = dict(norm1_g=norm1_g, w_in=w_in, sgu_ln_g=sgu_ln_g, sgu_ln_b=sgu_ln_b, w_spatial=w_spatial, b_spatial=b_spatial,
             w_proj_attn=w_proj_attn, w_proj_sgu=w_proj_sgu, w_out=w_out, norm2_g=norm2_g, w_ffn_gate=w_ffn_gate,
             w_ffn_up=w_ffn_up, w_ffn_down=w_ffn_down, final_g=final_g)
    m = dict(norm1_g=m_norm1_g, w_in=m_w_in, sgu_ln_g=m_sgu_ln_g, sgu_ln_b=m_sgu_ln_b, w_spatial=m_w_spatial,
             b_spatial=m_b_spatial, w_proj_attn=m_w_proj_attn, w_proj_sgu=m_w_proj_sgu, w_out=m_w_out, norm2_g=m_norm2_g,
             w_ffn_gate=m_w_ffn_gate, w_ffn_up=m_w_ffn_up, w_ffn_down=m_w_ffn_down, final_g=m_final_g)
    v = dict(norm1_g=v_norm1_g, w_in=v_w_in, sgu_ln_g=v_sgu_ln_g, sgu_ln_b=v_sgu_ln_b, w_spatial=v_w_spatial,
             b_spatial=v_b_spatial, w_proj_attn=v_w_proj_attn, w_proj_sgu=v_w_proj_sgu, w_out=v_w_out, norm2_g=v_norm2_g,
             w_ffn_gate=v_w_ffn_gate, w_ffn_up=v_w_ffn_up, w_ffn_down=v_w_ffn_down, final_g=v_final_g)
    t = x.shape[1]

    shards = {n: _ew(f"cast_{n}", lambda a: (a,), [w[n][0]], [BF16])[0] for n in BIG}
    gath = {}
    for i, grp in enumerate(COMM_GROUPS):
        gath.update(zip(grp, _gather_weights(f"gather_weights_{i}", [shards[n] for n in grp])))
    w_p = _permute_cols(_cols_from_chips(gath["w_in"]), PERM)
    w_pa = _cols_from_chips(gath["w_proj_attn"])
    w_ps = _cols_from_chips(gath["w_proj_sgu"])
    w_o = gath["w_out"].reshape(D_MODEL, D_MODEL)

    dx, big, small = _local_step(
        x[0], positions.reshape(t, 1), loss_target[0], norm1_g, sgu_ln_g, sgu_ln_b, w_spatial[0], b_spatial[0], norm2_g,
        final_g.reshape(1, D_MODEL), w_p, w_pa, w_ps, w_o, gath["w_ffn_gate"], gath["w_ffn_up"], gath["w_ffn_down"])

    big["w_in"] = _permute_cols(big["w_in"], INV_PERM)
    big["w_out"] = big["w_out"].reshape(N_CHIPS, D_MODEL // N_CHIPS, D_MODEL)
    grads = {}
    for i, grp in enumerate(COMM_GROUPS):
        sums = _pair_reduce(f"rs_pair_reduce_{i}", [big[n] for n in grp])
        grads.update(zip(grp, _chip_reduce(f"rs_chip_reduce_{i}", sums)))

    small_shapes = {n: w[n].shape for n in SMALL_ORDER if n != "loss"}
    reduced = _allreduce_small(_pack_small(small))
    grads.update(_unpack_small(reduced, small_shapes))
    loss = reduced.reshape(-1)[sum(SMALL_SIZES[n] for n in SMALL_ORDER[:SMALL_ORDER.index("loss")])]

    delta, new_m, new_v = {}, {}, {}
    for n in BIG:
        shp = w[n].shape
        grads[n] = grads[n].reshape(shp)
        d_, m_, v_ = _adamw(f"adamw_{n}", grads[n][0], w[n][0], m[n][0], v[n][0])
        delta[n], new_m[n], new_v[n] = d_.reshape(shp), m_.reshape(shp), v_.reshape(shp)
    zero_loss = dict(loss=jnp.zeros((LANES,), F32))
    d_, m_, v_ = _adamw("adamw_small", reduced, _pack_small({**w, **zero_loss}), _pack_small({**m, **zero_loss}),
                        _pack_small({**v, **zero_loss}))
    delta.update(_unpack_small(d_, small_shapes))
    new_m.update(_unpack_small(m_, small_shapes))
    new_v.update(_unpack_small(v_, small_shapes))

    return (loss, dx.reshape(x.shape), *[grads[n] for n in WEIGHTS], *[delta[n] for n in WEIGHTS],
            *[new_m[n] for n in WEIGHTS], *[new_v[n] for n in WEIGHTS])
```

```python
import functools

import numpy as np
import jax
import jax.numpy as jnp
from jax import lax
from jax.experimental import pallas as pl
from jax.experimental.pallas import tpu as pltpu

F32, BF16 = jnp.float32, jnp.bfloat16
MESH = pl.DeviceIdType.MESH

D_MODEL = 1024
HEAD_DIM = 64
ATTN_W = 512
DILATIONS = (1, 4, 16)
BLK = 128
ROPE_DIM = 16
ROPE_THETA = 500000.0
SGU_W = 512
SGU_CHUNK = 128
SGU_GROUPS = 8
D_FF = 2816
N_CHIPS = 4
FF_SHARD = D_FF // N_CHIPS
IN_COLS = 7680
EPS = 1e-6
NEG = -1e30
LANES = 128
VMEM_LIMIT = 52 * 1024 * 1024

ADAM_LR, ADAM_B1, ADAM_B2, ADAM_EPS, ADAM_WD, ADAM_STEP = 0.001, 0.9, 0.999, 1e-08, 0.01, 10

PERM = (11, 12, 13, 14, 9, 10, 0, 3, 6, 1, 4, 7, 2, 5, 8)
INV_PERM = tuple(int(i) for i in np.argsort(np.array(PERM)))


def _cparams(ngrid):
    return pltpu.CompilerParams(dimension_semantics=("arbitrary",) * ngrid, vmem_limit_bytes=VMEM_LIMIT)


def _full(shape):
    return pl.BlockSpec(shape, lambda *_: (0,) * len(shape))


def _resident(shape):
    return pl.BlockSpec(shape, lambda *_: (0,) * len(shape), pipeline_mode=pl.Buffered(1))


NN = ((1,), (0,))
NT = ((1,), (1,))
TN = ((0,), (0,))


def _mm(name, grid, pairs, dims, acc_shape, epi, *, extras=(), outs=(), reds=(), aliases=None):
    nk = grid[-1]
    npair, nex, nout, nred = len(pairs), len(extras), len(outs), len(reds)

    def body(*refs):
        a_refs = refs[:npair]
        b_refs = refs[npair:2 * npair]
        p0 = 2 * npair
        e_refs = refs[p0:p0 + nex]
        o_refs = refs[p0 + nex:p0 + nex + nout]
        r_refs = refs[p0 + nex + nout:p0 + nex + nout + nred]
        ids = [pl.program_id(a) for a in range(len(grid))]
        k = ids[-1]
        if nred:
            first = ids[0] == 0
            for v in ids[1:]:
                first = first & (v == 0)

            @pl.when(first)
            def _():
                for r in r_refs:
                    r[...] = jnp.zeros(r.shape, r.dtype)

        part = None
        for a_ref, b_ref in zip(a_refs, b_refs):
            d = lax.dot_general(a_ref[...], b_ref[...], (dims, ((), ())), preferred_element_type=F32)
            part = d if part is None else part + d
        if nk == 1:
            epi(part, e_refs, o_refs, r_refs, ids)
        else:
            acc_ref = refs[-1]

            @pl.when(k == 0)
            def _():
                acc_ref[...] = part

            @pl.when(k > 0)
            def _():
                acc_ref[...] += part

            @pl.when(k == nk - 1)
            def _():
                epi(acc_ref[...], e_refs, o_refs, r_refs, ids)

    in_specs = [p[1] for p in pairs] + [p[3] for p in pairs] + [e[1] for e in extras]
    args = [p[0] for p in pairs] + [p[2] for p in pairs] + [e[0] for e in extras]
    out_shape = [jax.ShapeDtypeStruct(o[0], o[1]) for o in outs] + [jax.ShapeDtypeStruct(r, F32) for r in reds]
    out_specs = [o[2] for o in outs] + [_full(r) for r in reds]
    scratch_shapes = [pltpu.VMEM(acc_shape, F32)] if nk > 1 else []
    return pl.pallas_call(
        body, grid=grid, in_specs=in_specs, out_specs=out_specs, out_shape=out_shape, scratch_shapes=scratch_shapes,
        input_output_aliases=aliases or {}, compiler_params=_cparams(len(grid)), name=name)(*args)


def _lane_tile(t, width):
    n = width // LANES
    return t if n == 1 else jnp.concatenate([t] * n, axis=1)


def _rope(v, cos_w, sin_w):
    w = v.shape[1]
    lane = lax.broadcasted_iota(jnp.int32, v.shape, 1)
    partner = jnp.where((lane % HEAD_DIM) < ROPE_DIM // 2, pltpu.roll(v, w - ROPE_DIM // 2, axis=1),
                        pltpu.roll(v, ROPE_DIM // 2, axis=1))
    return v * cos_w + partner * sin_w


def _sigmoid(v):
    return 1.0 / (1.0 + jnp.exp(-v))


def _rms_stats(v):
    r = lax.rsqrt(jnp.mean(v * v, axis=-1, keepdims=True) + EPS)
    return v * r, r


def _rms_bwd(dy, xhat, r, g):
    dxh = dy * g
    return r * (dxh - xhat * jnp.mean(dxh * xhat, axis=-1, keepdims=True))


def _head_sum_matrix():
    idx = np.arange(ATTN_W) // HEAD_DIM
    return jnp.asarray((idx[:, None] == idx[None, :]).astype(np.float32), dtype=BF16)


def _group_sum(v, e):
    hi = v.astype(BF16)
    lo = (v - hi.astype(F32)).astype(BF16)
    return jnp.dot(hi, e, preferred_element_type=F32) + jnp.dot(lo, e, preferred_element_type=F32)


TILE = 512


def _to_slabs(slab_ref, v):
    for cs in range(slab_ref.shape[0]):
        slab_ref[cs] = v[:, cs * LANES:(cs + 1) * LANES]


def _from_slabs(slab_ref):
    return jnp.concatenate([slab_ref[cs] for cs in range(slab_ref.shape[0])], axis=1)


def _class_rows(slab_ref, r, dil):
    n = slab_ref.shape[1] // dil
    return jnp.concatenate([slab_ref.at[cs][pl.ds(r, n, stride=dil), :] for cs in range(slab_ref.shape[0])], axis=1)


def _put_class_rows(slab_ref, r, dil, v):
    n = slab_ref.shape[1] // dil
    for cs in range(slab_ref.shape[0]):
        slab_ref.at[cs][pl.ds(r, n, stride=dil), :] = v[:, cs * LANES:(cs + 1) * LANES]


def _natural_from_group(slab_ref, grp_ref):
    dil = grp_ref.shape[0]
    for r in range(dil):
        _put_class_rows(slab_ref, r, dil, grp_ref[r].astype(F32))
    return _from_slabs(slab_ref)


def _group_from_natural(slab_ref, grp_ref, v):
    dil = grp_ref.shape[0]
    _to_slabs(slab_ref, v)
    for r in range(dil):
        grp_ref[r] = _class_rows(slab_ref, r, dil).astype(grp_ref.dtype)


def _group_spec(dil, tile, width):
    return pl.BlockSpec((dil, tile // dil, width), lambda i, *_: (0, i, 0))


def _slabs(tile, width):
    return pltpu.VMEM((width // LANES, tile, LANES), F32)


def _rope_consts():
    lane = np.arange(LANES) % HEAD_DIM
    fi = lane % (ROPE_DIM // 2)
    invf = np.where(lane < ROPE_DIM, ROPE_THETA ** (-(2.0 * fi) / ROPE_DIM), 0.0)
    sgn = np.where(lane < ROPE_DIM // 2, -1.0, np.where(lane < ROPE_DIM, 1.0, 0.0))
    return (jnp.asarray(invf.astype(np.float32)).reshape(1, LANES), jnp.asarray(sgn.astype(np.float32)).reshape(1, LANES))


def _rope_tables(pos_col):
    t = pos_col.shape[0]
    tile = min(t, TILE)
    invf, sgn = _rope_consts()

    def body(p_ref, f_ref, s_ref, c0, s0, c1, s1, c2, s2, slab_c, slab_s):
        ang = p_ref[...].astype(F32) * f_ref[...]
        cos, sin = jnp.cos(ang), jnp.sin(ang) * s_ref[...]
        c0[...] = cos
        s0[...] = sin
        _group_from_natural(slab_c, c1, cos)
        _group_from_natural(slab_s, s1, sin)
        for r in range(DILATIONS[2]):
            c2[r] = _class_rows(slab_c, r, DILATIONS[2])
            s2[r] = _class_rows(slab_s, r, DILATIONS[2])

    nat = pl.BlockSpec((tile, LANES), lambda i: (i, 0))
    specs, shapes = [nat, nat], [(t, LANES)] * 2
    for d in DILATIONS[1:]:
        specs += [_group_spec(d, tile, LANES)] * 2
        shapes += [(d, t // d, LANES)] * 2
    outs = pl.pallas_call(
        body, grid=(t // tile,),
        in_specs=[pl.BlockSpec((tile, 1), lambda i: (i, 0)), _full((1, LANES)), _full((1, LANES))],
        out_specs=specs, out_shape=[jax.ShapeDtypeStruct(s, F32) for s in shapes],
        scratch_shapes=[_slabs(tile, LANES)] * 2,
        compiler_params=_cparams(1), name="rope_tables")(pos_col, invf, sgn)
    return [(outs[2 * g].reshape(t, LANES), outs[2 * g + 1].reshape(t, LANES)) for g in range(len(DILATIONS))]


def _norm_fwd(x, g):
    t = x.shape[0]
    tile = min(t, TILE)

    def body(x_ref, g_ref, h0_ref, h1_ref, h2_ref, slab):
        xhat, _ = _rms_stats(x_ref[...])
        hn = xhat * g_ref[...]
        h0_ref[...] = hn.astype(BF16)
        _group_from_natural(slab, h1_ref, hn)
        for r in range(DILATIONS[2]):
            h2_ref[r] = _class_rows(slab, r, DILATIONS[2]).astype(BF16)

    nat = pl.BlockSpec((tile, D_MODEL), lambda i: (i, 0))
    return pl.pallas_call(
        body, grid=(t // tile,),
        in_specs=[nat, _full((1, D_MODEL))],
        out_specs=[nat] + [_group_spec(d, tile, D_MODEL) for d in DILATIONS[1:]],
        out_shape=[jax.ShapeDtypeStruct((t, D_MODEL), BF16)]
        + [jax.ShapeDtypeStruct((d, t // d, D_MODEL), BF16) for d in DILATIONS[1:]],
        scratch_shapes=[_slabs(tile, D_MODEL)],
        compiler_params=_cparams(1), name="norm1_fwd")(x, g)


GU_COLS = 3072
GROUP_COLS = 1536


def _in_proj(hs, w_p, tables):
    t = hs[0].shape[0]
    tm = min(t, 1024)

    def body_gu(h_ref, w_ref, o_ref):
        o_ref[...] = jnp.dot(h_ref[...], w_ref[...], preferred_element_type=F32).astype(BF16)

    gu = _token_call("in_proj_gates_uv", body_gu, t, tm,
                     [(hs[0], _rows_spec(tm, D_MODEL)),
                      (w_p, pl.BlockSpec((D_MODEL, GU_COLS), lambda i: (0, 0), pipeline_mode=pl.Buffered(1)))],
                     [((t, GU_COLS), BF16, _rows_spec(tm, GU_COLS))])[0]

    qkvs = []
    for g in range(len(DILATIONS)):

        def body_qkv(h_ref, w_ref, cos_ref, sin_ref, o_ref):
            acc = jnp.dot(h_ref[...], w_ref[...], preferred_element_type=F32)
            cos_w = _lane_tile(cos_ref[...], ATTN_W)
            sin_w = _lane_tile(sin_ref[...], ATTN_W)
            o_ref[:, 0:ATTN_W] = (_rope(acc[:, 0:ATTN_W], cos_w, sin_w) * HEAD_DIM ** -0.5).astype(BF16)
            o_ref[:, ATTN_W:2 * ATTN_W] = _rope(acc[:, ATTN_W:2 * ATTN_W], cos_w, sin_w).astype(BF16)
            o_ref[:, 2 * ATTN_W:] = acc[:, 2 * ATTN_W:].astype(BF16)

        col = (GU_COLS + g * GROUP_COLS) // GROUP_COLS
        cos_t, sin_t = tables[g]
        qkvs.append(_token_call(
            f"in_proj_qkv_g{g}", body_qkv, t, tm,
            [(hs[g].reshape(t, D_MODEL), _rows_spec(tm, D_MODEL)),
             (w_p, pl.BlockSpec((D_MODEL, GROUP_COLS), lambda i, col=col: (0, col), pipeline_mode=pl.Buffered(1))),
             (cos_t, _rows_spec(tm, LANES)), (sin_t, _rows_spec(tm, LANES))],
            [((t, GROUP_COLS), BF16, _rows_spec(tm, GROUP_COLS))])[0])
    return gu, qkvs


def _attn_masks(n):
    row = lax.broadcasted_iota(jnp.int32, (BLK, 2 * BLK), 0)
    col = lax.broadcasted_iota(jnp.int32, (BLK, 2 * BLK), 1)
    diff = BLK + row - col
    valid = (diff >= 0) & (diff <= BLK) & ((col >= BLK) | (n > 0))
    upper = lax.broadcasted_iota(jnp.int32, (BLK, LANES), 1) >= HEAD_DIM
    return valid, upper


def _attn_fwd(qkv, g, dil):
    t = qkv.shape[0]
    length = t // dil
    nb = length // BLK
    view = qkv.reshape(dil, length, GROUP_COLS)

    def body(q_ref, kc_ref, kp_ref, vc_ref, vp_ref, o_ref, l_ref):
        n = pl.program_id(1)
        valid, upper = _attn_masks(n)
        for p in range(ATTN_W // LANES):
            sl = slice(p * LANES, (p + 1) * LANES)
            q2 = q_ref[:, sl]
            k2 = jnp.concatenate([kp_ref[:, sl], kc_ref[:, sl]], axis=0)
            v2 = jnp.concatenate([vp_ref[:, sl], vc_ref[:, sl]], axis=0)
            outs, lses = [], []
            for hh in (0, 1):
                sel = upper if hh else jnp.logical_not(upper)
                qm = jnp.where(sel, q2, jnp.zeros_like(q2))
                s = lax.dot_general(qm, k2, (NT, ((), ())), preferred_element_type=F32)
                s = jnp.where(valid, s, NEG)
                m = jnp.max(s, axis=1, keepdims=True)
                pe = jnp.exp(s - m)
                den = jnp.sum(pe, axis=1, keepdims=True)
                outs.append(jnp.dot(pe.astype(BF16), v2, preferred_element_type=F32) / den)
                lses.append(jnp.broadcast_to(m + jnp.log(den), (BLK, LANES)))
            o_ref[:, sl] = jnp.where(upper, outs[1], outs[0])
            l_ref[:, sl] = jnp.where(upper, lses[1], lses[0])

    cur = lambda part: pl.BlockSpec((None, BLK, ATTN_W), lambda r, n: (r, n, part))
    prev = lambda part: pl.BlockSpec((None, BLK, ATTN_W), lambda r, n: (r, jnp.maximum(n - 1, 0), part))
    out_spec = pl.BlockSpec((None, BLK, ATTN_W), lambda r, n: (r, n, 0))
    return pl.pallas_call(
        body, grid=(dil, nb),
        in_specs=[cur(0), cur(1), prev(1), cur(2), prev(2)],
        out_specs=[out_spec, out_spec],
        out_shape=[jax.ShapeDtypeStruct((dil, length, ATTN_W), F32)] * 2,
        compiler_params=_cparams(2), name=f"attn_fwd_g{g}")(view, view, view, view, view)


def _alphas(l0, l1, l2):
    m = jnp.maximum(jnp.maximum(l0, l1), l2)
    e0, e1, e2 = jnp.exp(l0 - m), jnp.exp(l1 - m), jnp.exp(l2 - m)
    inv = 1.0 / (e0 + e1 + e2)
    return e0 * inv, e1 * inv, e2 * inv


def _natural_group_values(o_refs, l_refs, slabs):
    os_ = [o_refs[0][0]] + [_natural_from_group(slabs[2 * g - 2], o_refs[g]) for g in (1, 2)]
    ls_ = [l_refs[0][0]] + [_natural_from_group(slabs[2 * g - 1], l_refs[g]) for g in (1, 2)]
    return os_, ls_


def _combine_fwd(os_, ls_):
    t = os_[0].shape[1]
    tile = min(t, TILE)

    def body(o0, o1, o2, l0, l1, l2, a_ref, *slabs):
        ov, lv = _natural_group_values((o0, o1, o2), (l0, l1, l2), slabs)
        a0, a1, a2 = _alphas(*lv)
        a_ref[...] = (a0 * ov[0] + a1 * ov[1] + a2 * ov[2]).astype(BF16)

    specs = [_group_spec(d, tile, ATTN_W) for d in DILATIONS]
    return pl.pallas_call(
        body, grid=(t // tile,), in_specs=specs * 2, out_specs=pl.BlockSpec((tile, ATTN_W), lambda i: (i, 0)),
        out_shape=jax.ShapeDtypeStruct((t, ATTN_W), BF16),
        scratch_shapes=[_slabs(tile, ATTN_W)] * 4,
        compiler_params=_cparams(1), name="combine_fwd")(*os_, *ls_)


def _combine_bwd(dattn, os_, ls_):
    t = dattn.shape[0]
    tile = min(t, TILE)
    e = _head_sum_matrix()

    def body(d_ref, o0, o1, o2, l0, l1, l2, e_ref, do0, do1, do2, c0, c1, c2, *slabs):
        ov, lv = _natural_group_values((o0, o1, o2), (l0, l1, l2), slabs)
        alphas = _alphas(*lv)
        d = d_ref[...]
        attn = alphas[0] * ov[0] + alphas[1] * ov[1] + alphas[2] * ov[2]
        s = _group_sum(d * attn, e_ref[...])
        do0[0] = (alphas[0] * d).astype(BF16)
        c0[0] = -alphas[0] * s
        for g, do_ref, c_ref in ((1, do1, c1), (2, do2, c2)):
            _group_from_natural(slabs[2 * g - 2], do_ref, alphas[g] * d)
            _group_from_natural(slabs[2 * g - 1], c_ref, -alphas[g] * s)

    specs = [_group_spec(d, tile, ATTN_W) for d in DILATIONS]
    shapes = [(d, t // d, ATTN_W) for d in DILATIONS]
    outs = pl.pallas_call(
        body, grid=(t // tile,),
        in_specs=[pl.BlockSpec((tile, ATTN_W), lambda i: (i, 0))] + specs * 2 + [_full((ATTN_W, ATTN_W))],
        out_specs=specs * 2,
        out_shape=[jax.ShapeDtypeStruct(s, BF16) for s in shapes] + [jax.ShapeDtypeStruct(s, F32) for s in shapes],
        scratch_shapes=[_slabs(tile, ATTN_W)] * 4,
        compiler_params=_cparams(1), name="combine_bwd")(dattn, *os_, *ls_, e)
    return outs[:3], outs[3:]


def _attn_bwd(qkv, do, cc, lse, cos_t, sin_t, g, dil):
    t = qkv.shape[0]
    length = t // dil
    nb = length // BLK
    qkv_v = qkv.reshape(dil, length, GROUP_COLS)
    cos_v, sin_v = (a.reshape(dil, length, LANES) for a in (cos_t, sin_t))
    scale = HEAD_DIM ** -0.5

    def body(q_ref, kc_ref, kp_ref, vc_ref, vp_ref, do_ref, c_ref, l_ref, cosc, sinc, cosp, sinp,
             out_ref, dq_s, dk_s, dv_s):
        n = pl.program_id(1)
        valid, upper = _attn_masks(n)
        lower = jnp.logical_not(upper)

        @pl.when(n < nb)
        def _():
            cos_c, sin_c = _lane_tile(cosc[...], ATTN_W), _lane_tile(sinc[...], ATTN_W)
            cos_p, sin_p = _lane_tile(cosp[...], ATTN_W), _lane_tile(sinp[...], ATTN_W)
            dq_parts, dkp_parts, dkc_parts, dvp_parts, dvc_parts = [], [], [], [], []
            for p in range(ATTN_W // LANES):
                sl = slice(p * LANES, (p + 1) * LANES)
                q2 = q_ref[:, sl]
                k2 = jnp.concatenate([kp_ref[:, sl], kc_ref[:, sl]], axis=0)
                v2 = jnp.concatenate([vp_ref[:, sl], vc_ref[:, sl]], axis=0)
                do2 = do_ref[:, sl]
                l2 = l_ref[:, sl]
                c2 = c_ref[:, sl]
                dq2 = jnp.zeros((BLK, LANES), F32)
                dk2 = jnp.zeros((2 * BLK, LANES), F32)
                dv2 = jnp.zeros((2 * BLK, LANES), F32)
                for hh in (0, 1):
                    sel = upper if hh else lower
                    qm = jnp.where(sel, q2, jnp.zeros_like(q2))
                    dom = jnp.where(sel, do2, jnp.zeros_like(do2))
                    l_col = l2[:, hh * HEAD_DIM:hh * HEAD_DIM + 1]
                    c_col = c2[:, hh * HEAD_DIM:hh * HEAD_DIM + 1]
                    s = lax.dot_general(qm, k2, (NT, ((), ())), preferred_element_type=F32)
                    pe = jnp.where(valid, jnp.exp(jnp.where(valid, s, NEG) - l_col), 0.0)
                    dpv = lax.dot_general(dom, v2, (NT, ((), ())), preferred_element_type=F32)
                    ds = (pe * (dpv + c_col)).astype(BF16)
                    pb = pe.astype(BF16)
                    dq2 = dq2 + jnp.where(sel, jnp.dot(ds, k2, preferred_element_type=F32), 0.0)
                    dk2 = dk2 + lax.dot_general(ds, qm, (TN, ((), ())), preferred_element_type=F32)
                    dv2 = dv2 + lax.dot_general(pb, dom, (TN, ((), ())), preferred_element_type=F32)
                dq_parts.append(dq2)
                dkp_parts.append(dk2[:BLK])
                dkc_parts.append(dk2[BLK:])
                dvp_parts.append(dv2[:BLK])
                dvc_parts.append(dv2[BLK:])
            dq = _rope(jnp.concatenate(dq_parts, axis=1) * scale, cos_c, -sin_c)
            dkc = _rope(jnp.concatenate(dkc_parts, axis=1), cos_c, -sin_c)
            dkp = _rope(jnp.concatenate(dkp_parts, axis=1), cos_p, -sin_p)
            dvp = jnp.concatenate(dvp_parts, axis=1)
            dvc = jnp.concatenate(dvc_parts, axis=1)

            @pl.when(n > 0)
            def _():
                out_ref[:, 0:ATTN_W] = dq_s[...].astype(BF16)
                out_ref[:, ATTN_W:2 * ATTN_W] = (dk_s[...] + dkp).astype(BF16)
                out_ref[:, 2 * ATTN_W:3 * ATTN_W] = (dv_s[...] + dvp).astype(BF16)

            dq_s[...] = dq
            dk_s[...] = dkc
            dv_s[...] = dvc

        @pl.when(n == nb)
        def _():
            out_ref[:, 0:ATTN_W] = dq_s[...].astype(BF16)
            out_ref[:, ATTN_W:2 * ATTN_W] = dk_s[...].astype(BF16)
            out_ref[:, 2 * ATTN_W:3 * ATTN_W] = dv_s[...].astype(BF16)

    nc = lambda n: jnp.minimum(n, nb - 1)
    npv = lambda n: jnp.maximum(jnp.minimum(n, nb - 1) - 1, 0)
    cur = lambda part: pl.BlockSpec((None, BLK, ATTN_W), lambda r, n: (r, nc(n), part))
    prev = lambda part: pl.BlockSpec((None, BLK, ATTN_W), lambda r, n: (r, npv(n), part))
    row = pl.BlockSpec((None, BLK, ATTN_W), lambda r, n: (r, nc(n), 0))
    tab_c = pl.BlockSpec((None, BLK, LANES), lambda r, n: (r, nc(n), 0))
    tab_p = pl.BlockSpec((None, BLK, LANES), lambda r, n: (r, npv(n), 0))
    out_spec = pl.BlockSpec((None, BLK, GROUP_COLS), lambda r, n: (r, jnp.maximum(n - 1, 0), 0))
    out = pl.pallas_call(
        body, grid=(dil, nb + 1),
        in_specs=[cur(0), cur(1), prev(1), cur(2), prev(2), row, row, row, tab_c, tab_c, tab_p, tab_p],
        out_specs=out_spec,
        out_shape=jax.ShapeDtypeStruct((dil, length, GROUP_COLS), BF16),
        scratch_shapes=[pltpu.VMEM((BLK, ATTN_W), F32)] * 3,
        compiler_params=_cparams(2), name=f"attn_bwd_g{g}")(
            qkv_v, qkv_v, qkv_v, qkv_v, qkv_v, do, cc, lse, cos_v, sin_v, cos_v, sin_v)
    return out.reshape(t, GROUP_COLS)


SQRT_HALF = 0.7071067811865476
INV_SQRT_2PI = 0.3989422804014327


def _sgu_core(uv, g, b, w_ref, bias):
    cdf = 0.5 * (1.0 + lax.erf(uv * SQRT_HALF))
    z = uv * cdf
    u, v = z[:, :SGU_W], z[:, SGU_W:]
    mu = jnp.mean(v, axis=1, keepdims=True)
    xc = v - mu
    rs = lax.rsqrt(jnp.mean(xc * xc, axis=1, keepdims=True) + EPS)
    xhat = xc * rs
    vn = xhat * g + b
    row = lax.broadcasted_iota(jnp.int32, (SGU_CHUNK, SGU_CHUNK), 0)
    col = lax.broadcasted_iota(jnp.int32, (SGU_CHUNK, SGU_CHUNK), 1)
    tril = row >= col
    upper = lax.broadcasted_iota(jnp.int32, (SGU_CHUNK, LANES), 1) >= SGU_W // SGU_GROUPS
    ws, vlo, vhi, mixed = [], [], [], []
    for pr in range(SGU_W // LANES):
        sl = slice(pr * LANES, (pr + 1) * LANES)
        w0 = jnp.where(tril, w_ref[2 * pr], 0.0).astype(BF16)
        w1 = jnp.where(tril, w_ref[2 * pr + 1], 0.0).astype(BF16)
        vn2 = vn[:, sl]
        lo = jnp.where(upper, 0.0, vn2).astype(BF16)
        hi = jnp.where(upper, vn2, 0.0).astype(BF16)
        mixed.append(jnp.dot(w0, lo, preferred_element_type=F32) + jnp.dot(w1, hi, preferred_element_type=F32)
                     + bias[:, sl])
        ws.append((w0, w1))
        vlo.append(lo)
        vhi.append(hi)
    return cdf, u, xhat, rs, jnp.concatenate(mixed, axis=1), ws, vlo, vhi, tril, upper


def _sgu_fwd(gu, ln_g, ln_b, w_s, bias_exp):
    t = gu.shape[0]

    def body(uv_ref, g_ref, b_ref, w_ref, bias_ref, o_ref):
        _, u, _, _, mixed, *_ = _sgu_core(uv_ref[...].astype(F32), g_ref[...], b_ref[...], w_ref, bias_ref[...])
        o_ref[...] = (u * mixed).astype(BF16)

    return pl.pallas_call(
        body, grid=(t // SGU_CHUNK,),
        in_specs=[pl.BlockSpec((SGU_CHUNK, 2 * SGU_W), lambda n: (n, 2)), _full((1, SGU_W)), _full((1, SGU_W)),
                  _full((SGU_GROUPS, SGU_CHUNK, SGU_CHUNK)), _full((SGU_CHUNK, SGU_W))],
        out_specs=pl.BlockSpec((SGU_CHUNK, SGU_W), lambda n: (n, 0)),
        out_shape=jax.ShapeDtypeStruct((t, SGU_W), BF16),
        compiler_params=_cparams(1), name="sgu_fwd")(gu, ln_g, ln_b, w_s, bias_exp)


def _sgu_bwd(dproj, gu, dsgu, ln_g, ln_b, w_s, bias_exp):
    t = gu.shape[0]
    nchunks = t // SGU_CHUNK
    e = _head_sum_matrix()

    def body(dp_in, uv_ref, ds_ref, g_ref, b_ref, w_ref, bias_ref, e_ref, out_ref, dw_ref, dbias_ref, dg_ref, db_ref):
        n = pl.program_id(0)

        @pl.when(n == 0)
        def _():
            dw_ref[...] = jnp.zeros(dw_ref.shape, F32)
            dbias_ref[...] = jnp.zeros(dbias_ref.shape, F32)
            dg_ref[...] = jnp.zeros(dg_ref.shape, F32)
            db_ref[...] = jnp.zeros(db_ref.shape, F32)

        uv = uv_ref[...].astype(F32)
        g = g_ref[...]
        cdf, u, xhat, rs, mixed, ws, vlo, vhi, tril, upper = _sgu_core(uv, g, b_ref[...], w_ref, bias_ref[...])
        dsg = ds_ref[...]
        du = dsg * mixed
        dmixed = dsg * u
        dbias_ref[...] += dmixed
        dvn = []
        for pr in range(SGU_W // LANES):
            sl = slice(pr * LANES, (pr + 1) * LANES)
            dm2 = dmixed[:, sl]
            dlo = jnp.where(upper, 0.0, dm2).astype(BF16)
            dhi = jnp.where(upper, dm2, 0.0).astype(BF16)
            w0, w1 = ws[pr]
            dvn.append(lax.dot_general(w0, dlo, (TN, ((), ())), preferred_element_type=F32)
                       + lax.dot_general(w1, dhi, (TN, ((), ())), preferred_element_type=F32))
            dw0 = lax.dot_general(dlo, vlo[pr], (NT, ((), ())), preferred_element_type=F32)
            dw1 = lax.dot_general(dhi, vhi[pr], (NT, ((), ())), preferred_element_type=F32)
            dw_ref[2 * pr] += jnp.where(tril, dw0, 0.0)
            dw_ref[2 * pr + 1] += jnp.where(tril, dw1, 0.0)
        dvn = jnp.concatenate(dvn, axis=1)
        dg_ref[...] += jnp.sum(dvn * xhat, axis=0, keepdims=True)
        db_ref[...] += jnp.sum(dvn, axis=0, keepdims=True)
        dxh = dvn * g
        dv = rs * (dxh - jnp.mean(dxh, axis=1, keepdims=True) - xhat * jnp.mean(dxh * xhat, axis=1, keepdims=True))
        dz = jnp.concatenate([du, dv], axis=1)
        dgelu = cdf + uv * (INV_SQRT_2PI * jnp.exp(-0.5 * uv * uv))
        out_ref[...] = (dz * dgelu).astype(BF16)

        @pl.when(n == nchunks - 1)
        def _():
            dbias_ref[...] = _group_sum(dbias_ref[...], e_ref[...])

    outs = pl.pallas_call(
        body, grid=(nchunks,),
        in_specs=[pl.BlockSpec(memory_space=pl.ANY), pl.BlockSpec((SGU_CHUNK, 2 * SGU_W), lambda n: (n, 2)),
                  pl.BlockSpec((SGU_CHUNK, SGU_W), lambda n: (n, 0)), _full((1, SGU_W)), _full((1, SGU_W)),
                  _full((SGU_GROUPS, SGU_CHUNK, SGU_CHUNK)), _full((SGU_CHUNK, SGU_W)), _full((ATTN_W, ATTN_W))],
        out_specs=[pl.BlockSpec((SGU_CHUNK, 2 * SGU_W), lambda n: (n, 2)), _full((SGU_GROUPS, SGU_CHUNK, SGU_CHUNK)),
                   _full((SGU_CHUNK, SGU_W)), _full((1, SGU_W)), _full((1, SGU_W))],
        out_shape=[jax.ShapeDtypeStruct(dproj.shape, BF16), jax.ShapeDtypeStruct((SGU_GROUPS, SGU_CHUNK, SGU_CHUNK), F32),
                   jax.ShapeDtypeStruct((SGU_CHUNK, SGU_W), F32), jax.ShapeDtypeStruct((1, SGU_W), F32),
                   jax.ShapeDtypeStruct((1, SGU_W), F32)],
        input_output_aliases={0: 0},
        compiler_params=_cparams(1), name="sgu_bwd")(dproj, gu, dsgu, ln_g, ln_b, w_s, bias_exp, e)
    return outs


def _merge_fwd(attn, sgu, gu, x, w_pa, w_ps, w_out, g2):
    t = x.shape[0]
    tm = min(t, 256)

    def body(a_ref, s_ref, ga_ref, gb_ref, x_ref, wpa, wps, wo, g_ref, pa_ref, ps_ref, m_ref, x1_ref, h2_ref):
        pa = jnp.dot(a_ref[...], wpa[...], preferred_element_type=F32)
        ps = jnp.dot(s_ref[...], wps[...], preferred_element_type=F32)
        merged = (_sigmoid(ga_ref[...].astype(F32)) * pa + _sigmoid(gb_ref[...].astype(F32)) * ps).astype(BF16)
        x1 = x_ref[...] + jnp.dot(merged, wo[...], preferred_element_type=F32)
        xhat, _ = _rms_stats(x1)
        pa_ref[...] = pa.astype(BF16)
        ps_ref[...] = ps.astype(BF16)
        m_ref[...] = merged
        x1_ref[...] = x1
        h2_ref[...] = (xhat * g_ref[...]).astype(BF16)

    half = pl.BlockSpec((tm, ATTN_W), lambda i: (i, 0))
    full = pl.BlockSpec((tm, D_MODEL), lambda i: (i, 0))
    return pl.pallas_call(
        body, grid=(t // tm,),
        in_specs=[half, half, pl.BlockSpec((tm, D_MODEL), lambda i: (i, 0)), pl.BlockSpec((tm, D_MODEL), lambda i: (i, 1)),
                  full, _full((ATTN_W, D_MODEL)), _full((SGU_W, D_MODEL)), _full((D_MODEL, D_MODEL)), _full((1, D_MODEL))],
        out_specs=[full] * 5,
        out_shape=[jax.ShapeDtypeStruct((t, D_MODEL), BF16), jax.ShapeDtypeStruct((t, D_MODEL), BF16),
                   jax.ShapeDtypeStruct((t, D_MODEL), BF16), jax.ShapeDtypeStruct((t, D_MODEL), F32),
                   jax.ShapeDtypeStruct((t, D_MODEL), BF16)],
        compiler_params=_cparams(1), name="merge_fwd")(attn, sgu, gu, gu, x, w_pa, w_ps, w_out, g2)


def _merge_bwd(dx1b, gu, pa, ps, w_pa, w_ps, w_out):
    t = dx1b.shape[0]
    tm = min(t, 256)

    def body(d_ref, ga_ref, gb_ref, pa_ref, ps_ref, wpa, wps, wo, out_ref, dpa_ref, dps_ref, da_ref, dsg_ref):
        dm = lax.dot_general(d_ref[...], wo[...], (NT, ((), ())), preferred_element_type=F32)
        sa, sb = _sigmoid(ga_ref[...].astype(F32)), _sigmoid(gb_ref[...].astype(F32))
        dpa = (dm * sa).astype(BF16)
        dps = (dm * sb).astype(BF16)
        out_ref[:, 0:D_MODEL] = (dm * pa_ref[...].astype(F32) * sa * (1.0 - sa)).astype(BF16)
        out_ref[:, D_MODEL:2 * D_MODEL] = (dm * ps_ref[...].astype(F32) * sb * (1.0 - sb)).astype(BF16)
        out_ref[:, 2 * D_MODEL:GU_COLS] = jnp.zeros((tm, GU_COLS - 2 * D_MODEL), BF16)
        dpa_ref[...] = dpa
        dps_ref[...] = dps
        da_ref[...] = lax.dot_general(dpa, wpa[...], (NT, ((), ())), preferred_element_type=F32)
        dsg_ref[...] = lax.dot_general(dps, wps[...], (NT, ((), ())), preferred_element_type=F32)

    half = pl.BlockSpec((tm, ATTN_W), lambda i: (i, 0))
    full = pl.BlockSpec((tm, D_MODEL), lambda i: (i, 0))
    return pl.pallas_call(
        body, grid=(t // tm,),
        in_specs=[full, pl.BlockSpec((tm, D_MODEL), lambda i: (i, 0)),
                  pl.BlockSpec((tm, D_MODEL), lambda i: (i, 1)), full, full,
                  _full((ATTN_W, D_MODEL)), _full((SGU_W, D_MODEL)), _full((D_MODEL, D_MODEL))],
        out_specs=[pl.BlockSpec((tm, GU_COLS), lambda i: (i, 0)), full, full, half, half],
        out_shape=[jax.ShapeDtypeStruct((t, GU_COLS), BF16), jax.ShapeDtypeStruct((t, D_MODEL), BF16),
                   jax.ShapeDtypeStruct((t, D_MODEL), BF16), jax.ShapeDtypeStruct((t, ATTN_W), F32),
                   jax.ShapeDtypeStruct((t, SGU_W), F32)],
        compiler_params=_cparams(1), name="merge_bwd")(dx1b, gu, gu, pa, ps, w_pa, w_ps, w_out)


def _token_call(name, body, t, tm, ins, outs, reds=(), scratch=()):
    return pl.pallas_call(
        body, grid=(t // tm,), in_specs=[s for _, s in ins],
        out_specs=[o[2] for o in outs] + [_full(r) for r in reds],
        out_shape=[jax.ShapeDtypeStruct(o[0], o[1]) for o in outs] + [jax.ShapeDtypeStruct(r, F32) for r in reds],
        scratch_shapes=list(scratch), compiler_params=_cparams(1), name=name)(*[a for a, _ in ins])


def _rows_spec(tm, width):
    return pl.BlockSpec((tm, width), lambda i: (i, 0))


def _chips_spec(tm):
    return pl.BlockSpec((N_CHIPS, tm, FF_SHARD), lambda i: (0, i, 0))


def _zero_at_start(*refs):
    @pl.when(pl.program_id(0) == 0)
    def _():
        for r in refs:
            r[...] = jnp.zeros(r.shape, r.dtype)


def _ffn_fwd(h2, w_g, w_u):
    t = h2.shape[0]
    tm = min(t, 512)

    def body(h_ref, wg_ref, wu_ref, a_ref, b_ref, ff_ref):
        h = h_ref[...]
        for s in range(N_CHIPS):
            a = jnp.dot(h, wg_ref[s], preferred_element_type=F32)
            b = jnp.dot(h, wu_ref[s], preferred_element_type=F32)
            a_ref[s] = a.astype(BF16)
            b_ref[s] = b.astype(BF16)
            ff_ref[s] = (a * _sigmoid(a) * b).astype(BF16)

    shp = (N_CHIPS, t, FF_SHARD)
    w_spec = _resident((N_CHIPS, D_MODEL, FF_SHARD))
    return _token_call("ffn_fwd", body, t, tm, [(h2, _rows_spec(tm, D_MODEL)), (w_g, w_spec), (w_u, w_spec)],
                       [(shp, BF16, _chips_spec(tm))] * 3)


def _ffn_down_loss(ff, w_d, x1, tgt, gf):
    t = x1.shape[0]
    tm = min(t, 512)

    def body(ff_ref, wd_ref, x1_ref, tgt_ref, g_ref, dx2_ref, dx2b_ref, loss_ref, dgf_ref):
        _zero_at_start(loss_ref, dgf_ref)
        acc = jnp.dot(ff_ref[0], wd_ref[0], preferred_element_type=F32)
        for s in range(1, N_CHIPS):
            acc = acc + jnp.dot(ff_ref[s], wd_ref[s], preferred_element_type=F32)
        x2 = x1_ref[...] + acc
        g = g_ref[...]
        xhat, rr = _rms_stats(x2)
        diff = xhat * g - tgt_ref[...]
        rows = jnp.sum(diff * diff, axis=1, keepdims=True)
        loss_ref[...] += jnp.broadcast_to(jnp.sum(rows, axis=0, keepdims=True) * (0.5 / D_MODEL), (1, LANES))
        dy = diff * (1.0 / D_MODEL)
        dgf_ref[...] += jnp.sum(dy * xhat, axis=0, keepdims=True)
        dx2 = _rms_bwd(dy, xhat, rr, g)
        dx2_ref[...] = dx2
        dx2b_ref[...] = dx2.astype(BF16)

    row = _rows_spec(tm, D_MODEL)
    return _token_call("ffn_down_loss", body, t, tm,
                       [(ff, _chips_spec(tm)), (w_d, _resident((N_CHIPS, FF_SHARD, D_MODEL))), (x1, row), (tgt, row),
                        (gf, _full((1, D_MODEL)))],
                       [((t, D_MODEL), F32, row), ((t, D_MODEL), BF16, row)], reds=[(1, LANES), (1, D_MODEL)])


def _ffn_bwd_act(dx2b, w_d, a, b):
    t = dx2b.shape[0]
    tm = min(t, 512)

    def body(d_ref, wd_ref, a_ref, b_ref, da_ref, db_ref):
        d = d_ref[...]
        for s in range(N_CHIPS):
            dff = lax.dot_general(d, wd_ref[s], (NT, ((), ())), preferred_element_type=F32)
            av, bv = a_ref[s].astype(F32), b_ref[s].astype(F32)
            sg = _sigmoid(av)
            da_ref[s] = (dff * bv * (sg * (1.0 + av * (1.0 - sg)))).astype(BF16)
            db_ref[s] = (dff * (av * sg)).astype(BF16)

    shp = (N_CHIPS, t, FF_SHARD)
    return _token_call("ffn_bwd_act", body, t, tm,
                       [(dx2b, _rows_spec(tm, D_MODEL)), (w_d, _resident((N_CHIPS, FF_SHARD, D_MODEL))),
                        (a, _chips_spec(tm)), (b, _chips_spec(tm))],
                       [(shp, BF16, _chips_spec(tm))] * 2)


def _ffn_bwd_in(da, db, w_g, w_u, x1, dx2, g2):
    t = x1.shape[0]
    tm = min(t, 512)

    def body(da_ref, db_ref, wg_ref, wu_ref, x1_ref, dx2_ref, g_ref, dx1_ref, dx1b_ref, dg_ref):
        _zero_at_start(dg_ref)
        acc = None
        for s in range(N_CHIPS):
            part = (lax.dot_general(da_ref[s], wg_ref[s], (NT, ((), ())), preferred_element_type=F32)
                    + lax.dot_general(db_ref[s], wu_ref[s], (NT, ((), ())), preferred_element_type=F32))
            acc = part if acc is None else acc + part
        xhat, rr = _rms_stats(x1_ref[...])
        dg_ref[...] += jnp.sum(acc * xhat, axis=0, keepdims=True)
        dx1 = dx2_ref[...] + _rms_bwd(acc, xhat, rr, g_ref[...])
        dx1_ref[...] = dx1
        dx1b_ref[...] = dx1.astype(BF16)

    row = _rows_spec(tm, D_MODEL)
    w_spec = _resident((N_CHIPS, D_MODEL, FF_SHARD))
    return _token_call("ffn_bwd_in", body, t, tm,
                       [(da, _chips_spec(tm)), (db, _chips_spec(tm)), (w_g, w_spec), (w_u, w_spec), (x1, row), (dx2, row),
                        (g2, _full((1, D_MODEL)))],
                       [((t, D_MODEL), F32, row), ((t, D_MODEL), BF16, row)], reds=[(1, D_MODEL)])


def _in_proj_bwd(dgu, dqkvs, w_p, x, dx1, g1):
    t = x.shape[0]
    tile = min(t, TILE)
    tm = min(t, 1024)
    nat_cols = GU_COLS + GROUP_COLS

    dhs = []
    for g in (1, 2):
        col = (GU_COLS + g * GROUP_COLS) // GROUP_COLS

        def body_g(d_ref, w_ref, o_ref):
            o_ref[...] = lax.dot_general(d_ref[...], w_ref[...], (NT, ((), ())), preferred_element_type=F32)

        dh = _token_call(
            f"in_proj_bwd_g{g}", body_g, t, tm,
            [(dqkvs[g], _rows_spec(tm, GROUP_COLS)),
             (w_p, pl.BlockSpec((D_MODEL, GROUP_COLS), lambda i, col=col: (0, col), pipeline_mode=pl.Buffered(1)))],
            [((t, D_MODEL), F32, _rows_spec(tm, D_MODEL))])[0]
        dhs.append(dh.reshape(DILATIONS[g], t // DILATIONS[g], D_MODEL))

    def body(dgu_ref, dq0_ref, w_ref, x_ref, dx1_ref, g_ref, dh1_ref, dh2_ref, dx_ref, dg_ref, slab):
        _zero_at_start(dg_ref)
        dh = lax.dot_general(dgu_ref[...], w_ref[:, 0:GU_COLS], (NT, ((), ())), preferred_element_type=F32)
        dh = dh + lax.dot_general(dq0_ref[...], w_ref[:, GU_COLS:nat_cols], (NT, ((), ())), preferred_element_type=F32)
        dh = dh + _natural_from_group(slab, dh1_ref)
        dh = dh + _natural_from_group(slab, dh2_ref)
        xhat, rr = _rms_stats(x_ref[...])
        dg_ref[...] += jnp.sum(dh * xhat, axis=0, keepdims=True)
        dx_ref[...] = dx1_ref[...] + _rms_bwd(dh, xhat, rr, g_ref[...])

    row = _rows_spec(tile, D_MODEL)
    return _token_call(
        "in_proj_bwd", body, t, tile,
        [(dgu, _rows_spec(tile, GU_COLS)), (dqkvs[0], _rows_spec(tile, GROUP_COLS)),
         (w_p, pl.BlockSpec((D_MODEL, nat_cols), lambda i: (0, 0), pipeline_mode=pl.Buffered(1))),
         (x, row), (dx1, row), (g1, _full((1, D_MODEL))),
         (dhs[0], _group_spec(DILATIONS[1], tile, D_MODEL)), (dhs[1], _group_spec(DILATIONS[2], tile, D_MODEL))],
        [((t, D_MODEL), F32, row)], reds=[(1, D_MODEL)], scratch=[_slabs(tile, D_MODEL)])


def _epi_bf16(acc, e, o, r, ids):
    o[0][...] = acc.astype(BF16)


WGRAD_TK = 2048


def _wgrad_2d(name, a, b, tm, tn):
    t, k1 = a.shape
    n = b.shape[1]
    tk = min(t, WGRAD_TK)
    return _mm(name, (k1 // tm, n // tn, t // tk),
               [(a, pl.BlockSpec((tk, tm), lambda i, j, k: (k, i)), b, pl.BlockSpec((tk, tn), lambda i, j, k: (k, j)))],
               TN, (tm, tn), _epi_bf16, outs=[((k1, n), BF16, pl.BlockSpec((tm, tn), lambda i, j, k: (i, j)))])[0]


def _wgrad_in(hs, dgu, dqkvs):
    t = dgu.shape[0]
    tk = min(t, WGRAD_TK)
    dst = None
    parts = [(hs[0], dgu, 0)] + [(hs[g].reshape(t, D_MODEL), dqkvs[g], GU_COLS // GROUP_COLS + g) for g in range(3)]
    for n, (a, b, col0) in enumerate(parts):
        dst = _mm(f"wgrad_in_{n}", (1, b.shape[1] // GROUP_COLS, t // tk),
                  [(a, pl.BlockSpec((tk, D_MODEL), lambda i, j, k: (k, 0)), b,
                    pl.BlockSpec((tk, GROUP_COLS), lambda i, j, k: (k, j)))],
                  TN, (D_MODEL, GROUP_COLS), _epi_bf16,
                  extras=[] if dst is None else [(dst, pl.BlockSpec(memory_space=pl.ANY))],
                  outs=[((D_MODEL, IN_COLS), BF16, pl.BlockSpec((D_MODEL, GROUP_COLS), lambda i, j, k, col0=col0: (0, j + col0)))],
                  aliases=None if dst is None else {2: 0})[0]
    return dst


def _wgrad_ff_in(name, h2, da):
    t = h2.shape[0]
    tk = min(t, WGRAD_TK)
    return _mm(name, (N_CHIPS, 1, t // tk),
               [(h2, pl.BlockSpec((tk, D_MODEL), lambda i, j, k: (k, 0)),
                 da, pl.BlockSpec((None, tk, FF_SHARD), lambda i, j, k: (i, k, 0)))],
               TN, (D_MODEL, FF_SHARD), _epi_bf16,
               outs=[((N_CHIPS, D_MODEL, FF_SHARD), BF16, pl.BlockSpec((None, D_MODEL, FF_SHARD), lambda i, j, k: (i, 0, 0)))])[0]


def _wgrad_ff_down(ff, dx2b):
    t = dx2b.shape[0]
    tk = min(t, WGRAD_TK)
    return _mm("wgrad_ffn_down", (N_CHIPS, 1, t // tk),
               [(ff, pl.BlockSpec((None, tk, FF_SHARD), lambda i, j, k: (i, k, 0)),
                 dx2b, pl.BlockSpec((tk, D_MODEL), lambda i, j, k: (k, 0)))],
               TN, (FF_SHARD, D_MODEL), _epi_bf16,
               outs=[((N_CHIPS, FF_SHARD, D_MODEL), BF16, pl.BlockSpec((None, FF_SHARD, D_MODEL), lambda i, j, k: (i, 0, 0)))])[0]


def _local_step(x, pos_col, tgt, g1, ln_g, ln_b, w_s, b_s, g2, gf, w_p, w_pa, w_ps, w_out, w_g, w_u, w_d):
    tables = _rope_tables(pos_col)
    bias_exp = jnp.repeat(jnp.transpose(b_s), SGU_W // SGU_GROUPS, axis=1)

    hs = _norm_fwd(x, g1)
    gu, qkvs = _in_proj(hs, w_p, tables)
    os_, ls_ = [], []
    for g, dil in enumerate(DILATIONS):
        o, lse = _attn_fwd(qkvs[g], g, dil)
        os_.append(o)
        ls_.append(lse)
    attn = _combine_fwd(os_, ls_)
    sgu = _sgu_fwd(gu, ln_g, ln_b, w_s, bias_exp)
    pa, ps, merged, x1, h2 = _merge_fwd(attn, sgu, gu, x, w_pa, w_ps, w_out, g2)
    a, b, ff = _ffn_fwd(h2, w_g, w_u)
    dx2, dx2b, loss, dgf = _ffn_down_loss(ff, w_d, x1, tgt, gf)

    da, db = _ffn_bwd_act(dx2b, w_d, a, b)
    dw_d = _wgrad_ff_down(ff, dx2b)
    dx1, dx1b, dg2 = _ffn_bwd_in(da, db, w_g, w_u, x1, dx2, g2)
    dw_g = _wgrad_ff_in("wgrad_ffn_gate", h2, da)
    dw_u = _wgrad_ff_in("wgrad_ffn_up", h2, db)

    dgu, dpa, dps, dattn, dsgu = _merge_bwd(dx1b, gu, pa, ps, w_pa, w_ps, w_out)
    dw_out = _wgrad_2d("wgrad_out", merged, dx1b, D_MODEL, D_MODEL)
    dw_pa = _wgrad_2d("wgrad_proj_attn", attn, dpa, ATTN_W, D_MODEL)
    dw_ps = _wgrad_2d("wgrad_proj_sgu", sgu, dps, SGU_W, D_MODEL)
    dgu, dw_s, dbias, dln_g, dln_b = _sgu_bwd(dgu, gu, dsgu, ln_g, ln_b, w_s, bias_exp)
    dos, ccs = _combine_bwd(dattn, os_, ls_)
    dqkvs = [_attn_bwd(qkvs[g], dos[g], ccs[g], ls_[g], *tables[g], g, dil) for g, dil in enumerate(DILATIONS)]
    dx, dg1 = _in_proj_bwd(dgu, dqkvs, w_p, x, dx1, g1)
    dw_p = _wgrad_in(hs, dgu, dqkvs)

    db_s = jnp.transpose(dbias[:, ::SGU_W // SGU_GROUPS])
    small = dict(loss=loss, norm1_g=dg1, sgu_ln_g=dln_g, sgu_ln_b=dln_b, w_spatial=dw_s, b_spatial=db_s,
                 norm2_g=dg2, final_g=dgf)
    big = dict(w_in=dw_p, w_proj_attn=dw_pa, w_proj_sgu=dw_ps, w_out=dw_out, w_ffn_gate=dw_g, w_ffn_up=dw_u,
               w_ffn_down=dw_d)
    return dx, big, small


def _ew(name, fn, ins, out_dtypes):
    shp = ins[0].shape
    rows, cols = shp
    tr = next((cand for cand in (256, 352, 128) if rows % cand == 0 and rows > cand), rows)

    def body(*refs):
        res = fn(*[r[...] for r in refs[:len(ins)]])
        for o_ref, v in zip(refs[len(ins):], res):
            o_ref[...] = v.astype(o_ref.dtype)

    spec = pl.BlockSpec((tr, cols), lambda i: (i, 0))
    return pl.pallas_call(
        body, grid=(rows // tr,), in_specs=[spec] * len(ins), out_specs=[spec] * len(out_dtypes),
        out_shape=[jax.ShapeDtypeStruct(shp, d) for d in out_dtypes],
        compiler_params=_cparams(1), name=name)(*ins)


def _adamw_math(g, w, m, v):
    m = ADAM_B1 * m + (1.0 - ADAM_B1) * g
    v = ADAM_B2 * v + (1.0 - ADAM_B2) * (g * g)
    m_hat = m / (1.0 - ADAM_B1 ** ADAM_STEP)
    v_hat = v / (1.0 - ADAM_B2 ** ADAM_STEP)
    delta = -ADAM_LR * (m_hat / (jnp.sqrt(v_hat) + ADAM_EPS) + ADAM_WD * w)
    return delta, m, v


def _adamw(name, g, w, m, v):
    return _ew(name, lambda g_, w_, m_, v_: (g_,) + _adamw_math(g_, w_, m_, v_), [g, w, m, v], [F32] * 4)


VMEM_SPEC = pl.BlockSpec(memory_space=pltpu.VMEM)


def _for_row_chunks(rows, fn):
    ck = next(c for c in (64, 32, 16) if rows % c == 0)

    def step(i, carry):
        fn(pl.multiple_of(i * ck, ck), ck)
        return carry

    lax.fori_loop(0, rows // ck, step, 0)


def _place():
    x, y, c = lax.axis_index("x"), lax.axis_index("y"), lax.axis_index("c")
    chips = [(1 - x, y), (x, 1 - y), (1 - x, 1 - y)]
    return x, y, c, 2 * x + y, chips


def _rows(ref, start, size):
    if len(ref.shape) == 2:
        return ref.at[pl.ds(start, size), :]
    return ref.at[:, pl.ds(start, size), :]


def _comm_call(name, body, ins, out_shapes, scratch, n_remote):
    return pl.pallas_call(
        body, in_specs=[VMEM_SPEC] * len(ins), out_specs=[VMEM_SPEC] * len(out_shapes),
        out_shape=out_shapes,
        scratch_shapes=list(scratch) + [pltpu.SemaphoreType.DMA((n_remote,)), pltpu.SemaphoreType.DMA((n_remote,))],
        compiler_params=pltpu.CompilerParams(vmem_limit_bytes=VMEM_LIMIT), name=name)(*ins)


def _gather_weights(name, shards):
    nt = len(shards)

    def body(*refs):
        ins, outs = refs[:nt], refs[nt:2 * nt]
        send, recv = refs[2 * nt:]
        x, y, c, me, chips = _place()
        sibling = (x, y, 1 - c)
        firsts, passed, expects = [], [], []
        for t in range(nt):
            kh = ins[t].shape[0] // 2
            for j, chip in enumerate(chips):
                k = t * 3 + j
                theirs = 2 * chip[0] + chip[1]
                firsts.append(pltpu.make_async_remote_copy(
                    src_ref=_rows(ins[t], c * kh, kh), dst_ref=_rows(outs[t].at[me], c * kh, kh),
                    send_sem=send.at[k], recv_sem=recv.at[k], device_id=(*chip, c), device_id_type=MESH))
                landed = _rows(outs[t].at[theirs], c * kh, kh)
                expects.append(pltpu.make_async_remote_copy(
                    src_ref=landed, dst_ref=landed, send_sem=send.at[k], recv_sem=recv.at[k],
                    device_id=(*chip, c), device_id_type=MESH))
                passed.append(pltpu.make_async_remote_copy(
                    src_ref=landed, dst_ref=landed, send_sem=send.at[3 * nt + k], recv_sem=recv.at[3 * nt + k],
                    device_id=sibling, device_id_type=MESH))
        for cp in firsts:
            cp.start()
        for t in range(nt):
            mine = outs[t].at[me]

            def put(r0, ck, src=ins[t], dst=mine):
                dst[pl.ds(r0, ck), :] = src[pl.ds(r0, ck), :]

            _for_row_chunks(ins[t].shape[0], put)
        for k in range(3 * nt):
            expects[k].wait_recv()
            passed[k].start()
        for t in range(nt):
            kh = ins[t].shape[0] // 2
            for j, chip in enumerate(chips):
                k = t * 3 + j
                theirs = 2 * chip[0] + chip[1]
                other = _rows(outs[t].at[theirs], (1 - c) * kh, kh)
                pltpu.make_async_remote_copy(
                    src_ref=other, dst_ref=other, send_sem=send.at[3 * nt + k], recv_sem=recv.at[3 * nt + k],
                    device_id=sibling, device_id_type=MESH).wait_recv()
        for cp in firsts + passed:
            cp.wait_send()

    out_shapes = [jax.ShapeDtypeStruct((N_CHIPS,) + s.shape, s.dtype) for s in shards]
    return _comm_call(name, body, shards, out_shapes, [], 6 * nt)


def _pair_reduce(name, grads):
    nt = len(grads)

    def half(s):
        shp = list(s.shape)
        shp[-2] //= 2
        return tuple(shp)

    def body(*refs):
        ins, outs, got = refs[:nt], refs[nt:2 * nt], refs[2 * nt:3 * nt]
        send, recv = refs[3 * nt:]
        x, y, c, me, chips = _place()
        copies = []
        for t in range(nt):
            kh = ins[t].shape[-2] // 2
            rc = pltpu.make_async_remote_copy(
                src_ref=_rows(ins[t], (1 - c) * kh, kh), dst_ref=got[t], send_sem=send.at[t], recv_sem=recv.at[t],
                device_id=(x, y, 1 - c), device_id_type=MESH)
            rc.start()
            copies.append(rc)
        for t in range(nt):
            kh = ins[t].shape[-2] // 2
            copies[t].wait_recv()
            for lead in ([()] if len(ins[t].shape) == 2 else [(s,) for s in range(ins[t].shape[0])]):

                def add(r0, ck, lead=lead, src=ins[t], oth=got[t], dst=outs[t], kh=kh):
                    own = src[lead + (pl.ds(pl.multiple_of(c * kh + r0, ck), ck), slice(None))]
                    rows = lead + (pl.ds(r0, ck), slice(None))
                    dst[rows] = (own.astype(F32) + oth[rows].astype(F32)).astype(BF16)

                _for_row_chunks(kh, add)
        for rc in copies:
            rc.wait_send()

    shapes = [half(g) for g in grads]
    return _comm_call(name, body, grads, [jax.ShapeDtypeStruct(s, BF16) for s in shapes],
                      [pltpu.VMEM(s, BF16) for s in shapes], nt)


def _chip_reduce(name, sums):
    nt = len(sums)

    def cols(s):
        return s[2] if len(s) == 3 else s[1] // N_CHIPS

    def piece(ref, j):
        if len(ref.shape) == 3:
            return ref.at[j]
        n4 = ref.shape[1] // N_CHIPS
        return ref.at[:, pl.ds(j * n4, n4)]

    def body(*refs):
        ins, outs, slots = refs[:nt], refs[nt:2 * nt], refs[2 * nt:3 * nt]
        send, recv = refs[3 * nt:]
        x, y, c, me, chips = _place()
        sibling = (x, y, 1 - c)
        copies = []
        for t in range(nt):
            for j, chip in enumerate(chips):
                k = t * 3 + j
                rc = pltpu.make_async_remote_copy(
                    src_ref=piece(ins[t], 2 * chip[0] + chip[1]), dst_ref=slots[t].at[j], send_sem=send.at[k],
                    recv_sem=recv.at[k], device_id=(*chip, c), device_id_type=MESH)
                rc.start()
                copies.append(rc)
        handed = []
        for t in range(nt):
            kh, n4 = ins[t].shape[-2], outs[t].shape[1]
            for j in range(3):
                copies[t * 3 + j].wait_recv()
            for jj in range(N_CHIPS):

                @pl.when(me == jj)
                def _(jj=jj, src=ins[t], slot=slots[t], dst=outs[t], kh=kh, n4=n4):
                    def add(r0, ck):
                        rows = pl.ds(r0, ck)
                        own = src[jj, rows, :] if len(src.shape) == 3 else src[rows, jj * n4:(jj + 1) * n4]
                        acc = ((own.astype(F32) + slot[0, rows, :].astype(F32)) + slot[1, rows, :].astype(F32)) \
                            + slot[2, rows, :].astype(F32)
                        dst[pl.ds(pl.multiple_of(c * kh + r0, ck), ck), :] = acc

                    _for_row_chunks(kh, add)

            rc = pltpu.make_async_remote_copy(
                src_ref=_rows(outs[t], c * kh, kh), dst_ref=_rows(outs[t], c * kh, kh), send_sem=send.at[3 * nt + t],
                recv_sem=recv.at[3 * nt + t], device_id=sibling, device_id_type=MESH)
            rc.start()
            handed.append(rc)
        for t in range(nt):
            kh = ins[t].shape[-2]
            other = _rows(outs[t], (1 - c) * kh, kh)
            pltpu.make_async_remote_copy(
                src_ref=other, dst_ref=other, send_sem=send.at[3 * nt + t], recv_sem=recv.at[3 * nt + t],
                device_id=sibling, device_id_type=MESH).wait_recv()
        for rc in copies + handed:
            rc.wait_send()

    out_shapes = [jax.ShapeDtypeStruct((2 * s.shape[-2], cols(s.shape)), F32) for s in sums]
    scratch = [pltpu.VMEM((3, s.shape[-2], cols(s.shape)), BF16) for s in sums]
    return _comm_call(name, body, sums, out_shapes, scratch, 4 * nt)


VEC_SHAPE = (8, D_MODEL + LANES)
VEC_SLOTS = dict(norm1_g=(slice(0, 1), slice(0, D_MODEL)), norm2_g=(slice(1, 2), slice(0, D_MODEL)),
                 final_g=(slice(2, 3), slice(0, D_MODEL)), sgu_ln_g=(slice(3, 4), slice(0, SGU_W)),
                 sgu_ln_b=(slice(3, 4), slice(SGU_W, 2 * SGU_W)), b_spatial=(slice(0, 8), slice(D_MODEL, D_MODEL + LANES)),
                 loss=(slice(4, 5), slice(0, LANES)))
VEC_PARAMS = ("norm1_g", "norm2_g", "final_g", "sgu_ln_g", "sgu_ln_b", "b_spatial")
SMALL_PARAMS = VEC_PARAMS + ("w_spatial",)
W_SPATIAL_2D = (SGU_GROUPS * SGU_CHUNK, SGU_CHUNK)


def _small_step(partials, w, m, v):
    def shape2d(name):
        if name == "w_spatial":
            return W_SPATIAL_2D
        rows, cols = VEC_SLOTS[name]
        return (rows.stop - rows.start, cols.stop - cols.start)

    g_names = VEC_PARAMS + ("loss", "w_spatial")
    ins = [partials[n].reshape(shape2d(n)) for n in g_names]
    for src in (w, m, v):
        ins += [src[n].reshape(shape2d(n)) for n in SMALL_PARAMS]
    ng, npar = len(g_names), len(SMALL_PARAMS)

    def body(*refs):
        g_in = dict(zip(g_names, refs[:ng]))
        w_in, m_in, v_in = (dict(zip(SMALL_PARAMS, refs[ng + k * npar:ng + (k + 1) * npar])) for k in range(3))
        o0 = ng + 3 * npar
        g_out = dict(zip(g_names, refs[o0:o0 + ng]))
        d_out, m_out, v_out = (dict(zip(SMALL_PARAMS, refs[o0 + ng + k * npar:o0 + ng + (k + 1) * npar])) for k in range(3))
        vec, vec_pair, vec_slot, ws_pair, ws_slot, vw, vm, vv, send, recv = refs[o0 + ng + 3 * npar:]
        x, y, c, me, chips = _place()
        sibling = (x, y, 1 - c)

        def pack(dst, parts):
            dst[...] = jnp.zeros(VEC_SHAPE, F32)
            for n, ref in parts.items():
                if n in VEC_SLOTS:
                    dst[VEC_SLOTS[n]] = ref[...]

        pack(vec, g_in)
        copies = []

        def allreduce(k0, src, pair, slot):
            first = pltpu.make_async_remote_copy(src_ref=src, dst_ref=pair, send_sem=send.at[k0], recv_sem=recv.at[k0],
                                                 device_id=sibling, device_id_type=MESH)
            first.start()
            first.wait_recv()
            slot[me] = src[...] + pair[...]
            arrivals = []
            for j, chip in enumerate(chips):
                theirs = 2 * chip[0] + chip[1]
                rc = pltpu.make_async_remote_copy(src_ref=slot.at[me], dst_ref=slot.at[me], send_sem=send.at[k0 + 1 + j],
                                                  recv_sem=recv.at[k0 + 1 + j], device_id=(*chip, c), device_id_type=MESH)
                rc.start()
                arrivals.append(pltpu.make_async_remote_copy(
                    src_ref=slot.at[theirs], dst_ref=slot.at[theirs], send_sem=send.at[k0 + 1 + j],
                    recv_sem=recv.at[k0 + 1 + j], device_id=(*chip, c), device_id_type=MESH))
                copies.append(rc)
            copies.append(first)
            return arrivals

        arrivals = allreduce(0, vec, vec_pair, vec_slot) + allreduce(4, g_in["w_spatial"], ws_pair, ws_slot)
        pack(vw, w_in)
        pack(vm, m_in)
        pack(vv, v_in)
        for a in arrivals:
            a.wait_recv()
        for rc in copies:
            rc.wait_send()

        g_vec = ((vec_slot[0] + vec_slot[1]) + vec_slot[2]) + vec_slot[3]
        d_vec, m_vec, v_vec = _adamw_math(g_vec, vw[...], vm[...], vv[...])
        vec[...] = g_vec
        vw[...] = d_vec
        vm[...] = m_vec
        vv[...] = v_vec
        for n in VEC_PARAMS + ("loss",):
            g_out[n][...] = vec[VEC_SLOTS[n]]
        for n in VEC_PARAMS:
            d_out[n][...] = vw[VEC_SLOTS[n]]
            m_out[n][...] = vm[VEC_SLOTS[n]]
            v_out[n][...] = vv[VEC_SLOTS[n]]

        def spatial(r0, ck):
            rows = pl.ds(r0, ck)
            g = ((ws_slot[0, rows, :] + ws_slot[1, rows, :]) + ws_slot[2, rows, :]) + ws_slot[3, rows, :]
            d_, m_, v_ = _adamw_math(g, w_in["w_spatial"][rows, :], m_in["w_spatial"][rows, :], v_in["w_spatial"][rows, :])
            g_out["w_spatial"][rows, :] = g
            d_out["w_spatial"][rows, :] = d_
            m_out["w_spatial"][rows, :] = m_
            v_out["w_spatial"][rows, :] = v_

        _for_row_chunks(W_SPATIAL_2D[0], spatial)

    out_shapes = [jax.ShapeDtypeStruct(shape2d(n), F32) for n in g_names + SMALL_PARAMS * 3]
    outs = pl.pallas_call(
        body, in_specs=[VMEM_SPEC] * len(ins), out_specs=[VMEM_SPEC] * len(out_shapes), out_shape=out_shapes,
        scratch_shapes=[pltpu.VMEM(VEC_SHAPE, F32), pltpu.VMEM(VEC_SHAPE, F32), pltpu.VMEM((N_CHIPS,) + VEC_SHAPE, F32),
                        pltpu.VMEM(W_SPATIAL_2D, F32), pltpu.VMEM((N_CHIPS,) + W_SPATIAL_2D, F32),
                        pltpu.VMEM(VEC_SHAPE, F32), pltpu.VMEM(VEC_SHAPE, F32), pltpu.VMEM(VEC_SHAPE, F32),
                        pltpu.SemaphoreType.DMA((8,)), pltpu.SemaphoreType.DMA((8,))],
        name="small_params_step")(*ins)
    grads = dict(zip(g_names, outs[:ng]))
    rest = [dict(zip(SMALL_PARAMS, outs[ng + k * npar:ng + (k + 1) * npar])) for k in range(3)]
    return grads, rest[0], rest[1], rest[2]


BIG = ("w_in", "w_proj_attn", "w_proj_sgu", "w_out", "w_ffn_gate", "w_ffn_up", "w_ffn_down")
COMM_GROUPS = (("w_in",), ("w_proj_attn", "w_proj_sgu", "w_out", "w_ffn_gate", "w_ffn_up", "w_ffn_down"))
WEIGHTS = ("norm1_g", "w_in", "sgu_ln_g", "sgu_ln_b", "w_spatial", "b_spatial", "w_proj_attn", "w_proj_sgu", "w_out",
           "norm2_g", "w_ffn_gate", "w_ffn_up", "w_ffn_down", "final_g")


def _cols_from_chips(g):
    return jnp.transpose(g, (1, 0, 2)).reshape(g.shape[1], N_CHIPS * g.shape[2])


def _permute_cols(w, perm):
    return jnp.concatenate([w[:, 512 * b:512 * (b + 1)] for b in perm], axis=1)


def kernel(x, positions, norm1_g, w_in, sgu_ln_g, sgu_ln_b, w_spatial, b_spatial, w_proj_attn, w_proj_sgu, w_out, norm2_g, w_ffn_gate, w_ffn_up, w_ffn_down, final_g, loss_target, m_norm1_g, m_w_in, m_sgu_ln_g, m_sgu_ln_b, m_w_spatial, m_b_spatial, m_w_proj_attn, m_w_proj_sgu, m_w_out, m_norm2_g, m_w_ffn_gate, m_w_ffn_up, m_w_ffn_down, m_final_g, v_norm1_g, v_w_in, v_sgu_ln_g, v_sgu_ln_b, v_w_spatial, v_b_spatial, v_w_proj_attn, v_w_proj_sgu, v_w_out, v_norm2_g, v_w_ffn_gate, v_w_ffn_up, v_w_ffn_down, v_final_g):
    w = dict(norm1_g=norm1_g, w_in=w_in, sgu_ln_g=sgu_ln_g, sgu_ln_b=sgu_ln_b, w_spatial=w_spatial, b_spatial=b_spatial,
             w_proj_attn=w_proj_attn, w_proj_sgu=w_proj_sgu, w_out=w_out, norm2_g=norm2_g, w_ffn_gate=w_ffn_gate,
             w_ffn_up=w_ffn_up, w_ffn_down=w_ffn_down, final_g=final_g)
    m = dict(norm1_g=m_norm1_g, w_in=m_w_in, sgu_ln_g=m_sgu_ln_g, sgu_ln_b=m_sgu_ln_b, w_spatial=m_w_spatial,
             b_spatial=m_b_spatial, w_proj_attn=m_w_proj_attn, w_proj_sgu=m_w_proj_sgu, w_out=m_w_out, norm2_g=m_norm2_g,
             w_ffn_gate=m_w_ffn_gate, w_ffn_up=m_w_ffn_up, w_ffn_down=m_w_ffn_down, final_g=m_final_g)
    v = dict(norm1_g=v_norm1_g, w_in=v_w_in, sgu_ln_g=v_sgu_ln_g, sgu_ln_b=v_sgu_ln_b, w_spatial=v_w_spatial,
             b_spatial=v_b_spatial, w_proj_attn=v_w_proj_attn, w_proj_sgu=v_w_proj_sgu, w_out=v_w_out, norm2_g=v_norm2_g,
             w_ffn_gate=v_w_ffn_gate, w_ffn_up=v_w_ffn_up, w_ffn_down=v_w_ffn_down, final_g=v_final_g)
    t = x.shape[1]

    shards = {n: _ew(f"cast_{n}", lambda a: (a,), [w[n][0]], [BF16])[0] for n in BIG}
    gath = {}
    for i, grp in enumerate(COMM_GROUPS):
        gath.update(zip(grp, _gather_weights(f"gather_weights_{i}", [shards[n] for n in grp])))
    w_p = _permute_cols(_cols_from_chips(gath["w_in"]), PERM)
    w_pa = _cols_from_chips(gath["w_proj_attn"])
    w_ps = _cols_from_chips(gath["w_proj_sgu"])
    w_o = gath["w_out"].reshape(D_MODEL, D_MODEL)

    dx, big, small = _local_step(
        x[0], positions.reshape(t, 1), loss_target[0], norm1_g, sgu_ln_g, sgu_ln_b, w_spatial[0], b_spatial[0], norm2_g,
        final_g.reshape(1, D_MODEL), w_p, w_pa, w_ps, w_o, gath["w_ffn_gate"], gath["w_ffn_up"], gath["w_ffn_down"])

    big["w_in"] = _permute_cols(big["w_in"], INV_PERM)
    big["w_out"] = big["w_out"].reshape(N_CHIPS, D_MODEL // N_CHIPS, D_MODEL)
    grads = {}
    for i, grp in enumerate(COMM_GROUPS):
        sums = _pair_reduce(f"rs_pair_reduce_{i}", [big[n] for n in grp])
        grads.update(zip(grp, _chip_reduce(f"rs_chip_reduce_{i}", sums)))

    delta, new_m, new_v = {}, {}, {}
    for n in BIG:
        shp = w[n].shape
        g_, d_, m_, v_ = _adamw(f"adamw_{n}", grads[n], w[n][0], m[n][0], v[n][0])
        grads[n], delta[n], new_m[n], new_v[n] = g_.reshape(shp), d_.reshape(shp), m_.reshape(shp), v_.reshape(shp)

    g_s, d_s, m_s, v_s = _small_step(small, w, m, v)
    loss = g_s["loss"][0, 0]
    for n in SMALL_PARAMS:
        shp = w[n].shape
        grads[n], delta[n], new_m[n], new_v[n] = (a[n].reshape(shp) for a in (g_s, d_s, m_s, v_s))

    return (loss, dx.reshape(x.shape), *[grads[n] for n in WEIGHTS], *[delta[n] for n in WEIGHTS],
            *[new_m[n] for n in WEIGHTS], *[new_v[n] for n in WEIGHTS])
```

```python
import functools

import numpy as np
import jax
import jax.numpy as jnp
from jax import lax
from jax.experimental import pallas as pl
from jax.experimental.pallas import tpu as pltpu

F32, BF16 = jnp.float32, jnp.bfloat16
MESH = pl.DeviceIdType.MESH

D_MODEL = 1024
HEAD_DIM = 64
ATTN_W = 512
DILATIONS = (1, 4, 16)
BLK = 128
ROPE_DIM = 16
ROPE_THETA = 500000.0
SGU_W = 512
SGU_CHUNK = 128
SGU_GROUPS = 8
D_FF = 2816
N_CHIPS = 4
FF_SHARD = D_FF // N_CHIPS
IN_COLS = 7680
EPS = 1e-6
NEG = -1e30
LANES = 128
VMEM_LIMIT = 52 * 1024 * 1024

ADAM_LR, ADAM_B1, ADAM_B2, ADAM_EPS, ADAM_WD, ADAM_STEP = 0.001, 0.9, 0.999, 1e-08, 0.01, 10

PERM = (11, 12, 13, 14, 9, 10, 0, 3, 6, 1, 4, 7, 2, 5, 8)
INV_PERM = tuple(int(i) for i in np.argsort(np.array(PERM)))


def _cparams(ngrid):
    return pltpu.CompilerParams(dimension_semantics=("arbitrary",) * ngrid, vmem_limit_bytes=VMEM_LIMIT)


def _full(shape):
    return pl.BlockSpec(shape, lambda *_: (0,) * len(shape))


def _resident(shape):
    return pl.BlockSpec(shape, lambda *_: (0,) * len(shape), pipeline_mode=pl.Buffered(1))


NN = ((1,), (0,))
NT = ((1,), (1,))
TN = ((0,), (0,))


def _mm(name, grid, pairs, dims, acc_shape, epi, *, extras=(), outs=(), reds=(), aliases=None):
    nk = grid[-1]
    npair, nex, nout, nred = len(pairs), len(extras), len(outs), len(reds)

    def body(*refs):
        a_refs = refs[:npair]
        b_refs = refs[npair:2 * npair]
        p0 = 2 * npair
        e_refs = refs[p0:p0 + nex]
        o_refs = refs[p0 + nex:p0 + nex + nout]
        r_refs = refs[p0 + nex + nout:p0 + nex + nout + nred]
        ids = [pl.program_id(a) for a in range(len(grid))]
        k = ids[-1]
        if nred:
            first = ids[0] == 0
            for v in ids[1:]:
                first = first & (v == 0)

            @pl.when(first)
            def _():
                for r in r_refs:
                    r[...] = jnp.zeros(r.shape, r.dtype)

        part = None
        for a_ref, b_ref in zip(a_refs, b_refs):
            d = lax.dot_general(a_ref[...], b_ref[...], (dims, ((), ())), preferred_element_type=F32)
            part = d if part is None else part + d
        if nk == 1:
            epi(part, e_refs, o_refs, r_refs, ids)
        else:
            acc_ref = refs[-1]

            @pl.when(k == 0)
            def _():
                acc_ref[...] = part

            @pl.when(k > 0)
            def _():
                acc_ref[...] += part

            @pl.when(k == nk - 1)
            def _():
                epi(acc_ref[...], e_refs, o_refs, r_refs, ids)

    in_specs = [p[1] for p in pairs] + [p[3] for p in pairs] + [e[1] for e in extras]
    args = [p[0] for p in pairs] + [p[2] for p in pairs] + [e[0] for e in extras]
    out_shape = [jax.ShapeDtypeStruct(o[0], o[1]) for o in outs] + [jax.ShapeDtypeStruct(r, F32) for r in reds]
    out_specs = [o[2] for o in outs] + [_full(r) for r in reds]
    scratch_shapes = [pltpu.VMEM(acc_shape, F32)] if nk > 1 else []
    return pl.pallas_call(
        body, grid=grid, in_specs=in_specs, out_specs=out_specs, out_shape=out_shape, scratch_shapes=scratch_shapes,
        input_output_aliases=aliases or {}, compiler_params=_cparams(len(grid)), name=name)(*args)


def _rope(v, cos_t, sin_t):
    half = ROPE_DIM // 2
    first = (lax.broadcasted_iota(jnp.int32, cos_t.shape, 1) % HEAD_DIM) < half
    outs = []
    for cs in range(v.shape[1] // LANES):
        x = v[:, cs * LANES:(cs + 1) * LANES]
        partner = jnp.where(first, pltpu.roll(x, LANES - half, axis=1), pltpu.roll(x, half, axis=1))
        outs.append(x * cos_t + partner * sin_t)
    return outs[0] if len(outs) == 1 else jnp.concatenate(outs, axis=1)


def _spread_heads(v2, upper):
    other = pltpu.roll(v2, HEAD_DIM, axis=1)
    h0 = jnp.where(upper, other, v2)
    h1 = jnp.where(upper, v2, other)
    return jnp.concatenate([jnp.concatenate([h0, h0], axis=1), jnp.concatenate([h1, h1], axis=1)], axis=0)


def _sigmoid(v):
    return 1.0 / (1.0 + jnp.exp(-v))


def _rms_stats(v):
    r = lax.rsqrt(jnp.mean(v * v, axis=-1, keepdims=True) + EPS)
    return v * r, r


def _rms_bwd(dy, xhat, r, g):
    dxh = dy * g
    return r * (dxh - xhat * jnp.mean(dxh * xhat, axis=-1, keepdims=True))


def _head_sum_matrix():
    idx = np.arange(ATTN_W) // HEAD_DIM
    return jnp.asarray((idx[:, None] == idx[None, :]).astype(np.float32), dtype=BF16)


def _group_sum(v, e):
    hi = v.astype(BF16)
    lo = (v - hi.astype(F32)).astype(BF16)
    return jnp.dot(hi, e, preferred_element_type=F32) + jnp.dot(lo, e, preferred_element_type=F32)


TILE = 512


def _to_slabs(slab_ref, v):
    for cs in range(slab_ref.shape[0]):
        slab_ref[cs] = v[:, cs * LANES:(cs + 1) * LANES]


def _from_slabs(slab_ref):
    return jnp.concatenate([slab_ref[cs] for cs in range(slab_ref.shape[0])], axis=1)


def _class_rows(slab_ref, r, dil):
    n = slab_ref.shape[1] // dil
    return jnp.concatenate([slab_ref.at[cs][pl.ds(r, n, stride=dil), :] for cs in range(slab_ref.shape[0])], axis=1)


def _put_class_rows(slab_ref, r, dil, v):
    n = slab_ref.shape[1] // dil
    for cs in range(slab_ref.shape[0]):
        slab_ref.at[cs][pl.ds(r, n, stride=dil), :] = v[:, cs * LANES:(cs + 1) * LANES]


def _natural_from_group(slab_ref, grp_ref):
    dil = grp_ref.shape[0]
    for r in range(dil):
        _put_class_rows(slab_ref, r, dil, grp_ref[r].astype(F32))
    return _from_slabs(slab_ref)


def _group_from_natural(slab_ref, grp_ref, v):
    dil = grp_ref.shape[0]
    _to_slabs(slab_ref, v)
    for r in range(dil):
        grp_ref[r] = _class_rows(slab_ref, r, dil).astype(grp_ref.dtype)


def _group_spec(dil, tile, width):
    return pl.BlockSpec((dil, tile // dil, width), lambda i, *_: (0, i, 0))


def _slabs(tile, width):
    return pltpu.VMEM((width // LANES, tile, LANES), F32)


def _rope_consts():
    lane = np.arange(LANES) % HEAD_DIM
    fi = lane % (ROPE_DIM // 2)
    invf = np.where(lane < ROPE_DIM, ROPE_THETA ** (-(2.0 * fi) / ROPE_DIM), 0.0)
    sgn = np.where(lane < ROPE_DIM // 2, -1.0, np.where(lane < ROPE_DIM, 1.0, 0.0))
    return (jnp.asarray(invf.astype(np.float32)).reshape(1, LANES), jnp.asarray(sgn.astype(np.float32)).reshape(1, LANES))


def _rope_tables(pos_col):
    t = pos_col.shape[0]
    tile = min(t, TILE)
    invf, sgn = _rope_consts()

    def body(p_ref, f_ref, s_ref, c0, s0, c1, s1, c2, s2, slab_c, slab_s):
        ang = p_ref[...].astype(F32) * f_ref[...]
        cos, sin = jnp.cos(ang), jnp.sin(ang) * s_ref[...]
        c0[...] = cos
        s0[...] = sin
        _group_from_natural(slab_c, c1, cos)
        _group_from_natural(slab_s, s1, sin)
        for r in range(DILATIONS[2]):
            c2[r] = _class_rows(slab_c, r, DILATIONS[2])
            s2[r] = _class_rows(slab_s, r, DILATIONS[2])

    nat = pl.BlockSpec((tile, LANES), lambda i: (i, 0))
    specs, shapes = [nat, nat], [(t, LANES)] * 2
    for d in DILATIONS[1:]:
        specs += [_group_spec(d, tile, LANES)] * 2
        shapes += [(d, t // d, LANES)] * 2
    outs = pl.pallas_call(
        body, grid=(t // tile,),
        in_specs=[pl.BlockSpec((tile, 1), lambda i: (i, 0)), _full((1, LANES)), _full((1, LANES))],
        out_specs=specs, out_shape=[jax.ShapeDtypeStruct(s, F32) for s in shapes],
        scratch_shapes=[_slabs(tile, LANES)] * 2,
        compiler_params=_cparams(1), name="rope_tables")(pos_col, invf, sgn)
    return [(outs[2 * g].reshape(t, LANES), outs[2 * g + 1].reshape(t, LANES)) for g in range(len(DILATIONS))]


def _norm_fwd(x, g):
    t = x.shape[0]
    tile = min(t, TILE)

    def body(x_ref, g_ref, h0_ref, h1_ref, h2_ref, slab):
        xhat, _ = _rms_stats(x_ref[...])
        hn = xhat * g_ref[...]
        h0_ref[...] = hn.astype(BF16)
        _group_from_natural(slab, h1_ref, hn)
        for r in range(DILATIONS[2]):
            h2_ref[r] = _class_rows(slab, r, DILATIONS[2]).astype(BF16)

    nat = pl.BlockSpec((tile, D_MODEL), lambda i: (i, 0))
    return pl.pallas_call(
        body, grid=(t // tile,),
        in_specs=[nat, _full((1, D_MODEL))],
        out_specs=[nat] + [_group_spec(d, tile, D_MODEL) for d in DILATIONS[1:]],
        out_shape=[jax.ShapeDtypeStruct((t, D_MODEL), BF16)]
        + [jax.ShapeDtypeStruct((d, t // d, D_MODEL), BF16) for d in DILATIONS[1:]],
        scratch_shapes=[_slabs(tile, D_MODEL)],
        compiler_params=_cparams(1), name="norm1_fwd")(x, g)


GU_COLS = 3072
GROUP_COLS = 1536


def _in_proj(hs, w_p, tables):
    t = hs[0].shape[0]
    tm = min(t, 1024)

    def body_gu(h_ref, w_ref, o_ref):
        o_ref[...] = jnp.dot(h_ref[...], w_ref[...], preferred_element_type=F32).astype(BF16)

    gu = _token_call("in_proj_gates_uv", body_gu, t, tm,
                     [(hs[0], _rows_spec(tm, D_MODEL)),
                      (w_p, pl.BlockSpec((D_MODEL, GU_COLS), lambda i: (0, 0), pipeline_mode=pl.Buffered(1)))],
                     [((t, GU_COLS), BF16, _rows_spec(tm, GU_COLS))])[0]

    qkvs = []
    for g in range(len(DILATIONS)):

        def body_qkv(h_ref, w_ref, cos_ref, sin_ref, o_ref):
            acc = jnp.dot(h_ref[...], w_ref[...], preferred_element_type=F32)
            cos_w, sin_w = cos_ref[...], sin_ref[...]
            o_ref[:, 0:ATTN_W] = (_rope(acc[:, 0:ATTN_W], cos_w, sin_w) * HEAD_DIM ** -0.5).astype(BF16)
            o_ref[:, ATTN_W:2 * ATTN_W] = _rope(acc[:, ATTN_W:2 * ATTN_W], cos_w, sin_w).astype(BF16)
            o_ref[:, 2 * ATTN_W:] = acc[:, 2 * ATTN_W:].astype(BF16)

        col = (GU_COLS + g * GROUP_COLS) // GROUP_COLS
        cos_t, sin_t = tables[g]
        qkvs.append(_token_call(
            f"in_proj_qkv_g{g}", body_qkv, t, tm,
            [(hs[g].reshape(t, D_MODEL), _rows_spec(tm, D_MODEL)),
             (w_p, pl.BlockSpec((D_MODEL, GROUP_COLS), lambda i, col=col: (0, col), pipeline_mode=pl.Buffered(1))),
             (cos_t, _rows_spec(tm, LANES)), (sin_t, _rows_spec(tm, LANES))],
            [((t, GROUP_COLS), BF16, _rows_spec(tm, GROUP_COLS))])[0])
    return gu, qkvs


def _attn_masks(n):
    row = lax.broadcasted_iota(jnp.int32, (2 * BLK, 2 * BLK), 0) % BLK
    col = lax.broadcasted_iota(jnp.int32, (2 * BLK, 2 * BLK), 1)
    diff = BLK + row - col
    valid = (diff >= 0) & (diff <= BLK) & ((col >= BLK) | (n > 0))
    upper = lax.broadcasted_iota(jnp.int32, (BLK, LANES), 1) >= HEAD_DIM
    return valid, upper


def _stack_heads(v2, upper):
    zero = jnp.zeros_like(v2)
    return jnp.concatenate([jnp.where(upper, zero, v2), jnp.where(upper, v2, zero)], axis=0)


def _unstack_heads(v, upper):
    return jnp.where(upper, v[BLK:], v[:BLK])


def _attn_fwd(qkv, g, dil):
    t = qkv.shape[0]
    length = t // dil
    nb = length // BLK
    view = qkv.reshape(dil, length, GROUP_COLS)

    def body(q_ref, kc_ref, kp_ref, vc_ref, vp_ref, o_ref, l_ref):
        n = pl.program_id(1)
        valid, upper = _attn_masks(n)
        for p in range(ATTN_W // LANES):
            sl = slice(p * LANES, (p + 1) * LANES)
            qs = _stack_heads(q_ref[:, sl], upper)
            k2 = jnp.concatenate([kp_ref[:, sl], kc_ref[:, sl]], axis=0)
            v2 = jnp.concatenate([vp_ref[:, sl], vc_ref[:, sl]], axis=0)
            s = lax.dot_general(qs, k2, (NT, ((), ())), preferred_element_type=F32)
            s = jnp.where(valid, s, NEG)
            m = jnp.max(s, axis=1, keepdims=True)
            pe = jnp.exp(s - m)
            den = jnp.sum(pe, axis=1, keepdims=True)
            o = jnp.dot(pe.astype(BF16), v2, preferred_element_type=F32) / den
            lse = jnp.broadcast_to(m + jnp.log(den), (2 * BLK, LANES))
            o_ref[:, sl] = _unstack_heads(o, upper)
            l_ref[:, sl] = _unstack_heads(lse, upper)

    cur = lambda part: pl.BlockSpec((None, BLK, ATTN_W), lambda r, n: (r, n, part))
    prev = lambda part: pl.BlockSpec((None, BLK, ATTN_W), lambda r, n: (r, jnp.maximum(n - 1, 0), part))
    out_spec = pl.BlockSpec((None, BLK, ATTN_W), lambda r, n: (r, n, 0))
    return pl.pallas_call(
        body, grid=(dil, nb),
        in_specs=[cur(0), cur(1), prev(1), cur(2), prev(2)],
        out_specs=[out_spec, out_spec],
        out_shape=[jax.ShapeDtypeStruct((dil, length, ATTN_W), F32)] * 2,
        compiler_params=_cparams(2), name=f"attn_fwd_g{g}")(view, view, view, view, view)


def _alphas(l0, l1, l2):
    m = jnp.maximum(jnp.maximum(l0, l1), l2)
    e0, e1, e2 = jnp.exp(l0 - m), jnp.exp(l1 - m), jnp.exp(l2 - m)
    inv = 1.0 / (e0 + e1 + e2)
    return e0 * inv, e1 * inv, e2 * inv


def _natural_group_values(o_refs, l_refs, slabs):
    os_ = [o_refs[0][0]] + [_natural_from_group(slabs[2 * g - 2], o_refs[g]) for g in (1, 2)]
    ls_ = [l_refs[0][0]] + [_natural_from_group(slabs[2 * g - 1], l_refs[g]) for g in (1, 2)]
    return os_, ls_


def _combine_fwd(os_, ls_):
    t = os_[0].shape[1]
    tile = min(t, TILE)

    def body(o0, o1, o2, l0, l1, l2, a_ref, *slabs):
        ov, lv = _natural_group_values((o0, o1, o2), (l0, l1, l2), slabs)
        a0, a1, a2 = _alphas(*lv)
        a_ref[...] = (a0 * ov[0] + a1 * ov[1] + a2 * ov[2]).astype(BF16)

    specs = [_group_spec(d, tile, ATTN_W) for d in DILATIONS]
    return pl.pallas_call(
        body, grid=(t // tile,), in_specs=specs * 2, out_specs=pl.BlockSpec((tile, ATTN_W), lambda i: (i, 0)),
        out_shape=jax.ShapeDtypeStruct((t, ATTN_W), BF16),
        scratch_shapes=[_slabs(tile, ATTN_W)] * 4,
        compiler_params=_cparams(1), name="combine_fwd")(*os_, *ls_)


def _combine_bwd(dattn, os_, ls_):
    t = dattn.shape[0]
    tile = min(t, TILE)
    e = _head_sum_matrix()

    def body(d_ref, o0, o1, o2, l0, l1, l2, e_ref, do0, do1, do2, c0, c1, c2, *slabs):
        ov, lv = _natural_group_values((o0, o1, o2), (l0, l1, l2), slabs)
        alphas = _alphas(*lv)
        d = d_ref[...]
        attn = alphas[0] * ov[0] + alphas[1] * ov[1] + alphas[2] * ov[2]
        s = _group_sum(d * attn, e_ref[...])
        do0[0] = (alphas[0] * d).astype(BF16)
        c0[0] = -alphas[0] * s
        for g, do_ref, c_ref in ((1, do1, c1), (2, do2, c2)):
            _group_from_natural(slabs[2 * g - 2], do_ref, alphas[g] * d)
            _group_from_natural(slabs[2 * g - 1], c_ref, -alphas[g] * s)

    specs = [_group_spec(d, tile, ATTN_W) for d in DILATIONS]
    shapes = [(d, t // d, ATTN_W) for d in DILATIONS]
    outs = pl.pallas_call(
        body, grid=(t // tile,),
        in_specs=[pl.BlockSpec((tile, ATTN_W), lambda i: (i, 0))] + specs * 2 + [_full((ATTN_W, ATTN_W))],
        out_specs=specs * 2,
        out_shape=[jax.ShapeDtypeStruct(s, BF16) for s in shapes] + [jax.ShapeDtypeStruct(s, F32) for s in shapes],
        scratch_shapes=[_slabs(tile, ATTN_W)] * 4,
        compiler_params=_cparams(1), name="combine_bwd")(dattn, *os_, *ls_, e)
    return outs[:3], outs[3:]


def _attn_bwd(qkv, do, cc, lse, cos_t, sin_t, g, dil):
    t = qkv.shape[0]
    length = t // dil
    nb = length // BLK
    qkv_v = qkv.reshape(dil, length, GROUP_COLS)
    cos_v, sin_v = (a.reshape(dil, length, LANES) for a in (cos_t, sin_t))
    scale = HEAD_DIM ** -0.5

    def body(q_ref, kc_ref, kp_ref, vc_ref, vp_ref, do_ref, c_ref, l_ref, cosc, sinc, cosp, sinp,
             out_ref, dq_s, dk_s, dv_s):
        n = pl.program_id(1)
        valid, upper = _attn_masks(n)

        @pl.when(n < nb)
        def _():
            cos_c, sin_c = cosc[...], sinc[...]
            cos_p, sin_p = cosp[...], sinp[...]
            dq_parts, dkp_parts, dkc_parts, dvp_parts, dvc_parts = [], [], [], [], []
            for p in range(ATTN_W // LANES):
                sl = slice(p * LANES, (p + 1) * LANES)
                qs = _stack_heads(q_ref[:, sl], upper)
                dos = _stack_heads(do_ref[:, sl], upper)
                k2 = jnp.concatenate([kp_ref[:, sl], kc_ref[:, sl]], axis=0)
                v2 = jnp.concatenate([vp_ref[:, sl], vc_ref[:, sl]], axis=0)
                l_col = _spread_heads(l_ref[:, sl], upper)
                c_col = _spread_heads(c_ref[:, sl], upper)
                s = lax.dot_general(qs, k2, (NT, ((), ())), preferred_element_type=F32)
                pe = jnp.exp(jnp.where(valid, s, NEG) - l_col)
                dpv = lax.dot_general(dos, v2, (NT, ((), ())), preferred_element_type=F32)
                ds = (pe * (dpv + c_col)).astype(BF16)
                dq2 = _unstack_heads(jnp.dot(ds, k2, preferred_element_type=F32), upper)
                dk2 = lax.dot_general(ds, qs, (TN, ((), ())), preferred_element_type=F32)
                dv2 = lax.dot_general(pe.astype(BF16), dos, (TN, ((), ())), preferred_element_type=F32)
                dq_parts.append(dq2)
                dkp_parts.append(dk2[:BLK])
                dkc_parts.append(dk2[BLK:])
                dvp_parts.append(dv2[:BLK])
                dvc_parts.append(dv2[BLK:])
            dq = _rope(jnp.concatenate(dq_parts, axis=1) * scale, cos_c, -sin_c)
            dkc = _rope(jnp.concatenate(dkc_parts, axis=1), cos_c, -sin_c)
            dkp = _rope(jnp.concatenate(dkp_parts, axis=1), cos_p, -sin_p)
            dvp = jnp.concatenate(dvp_parts, axis=1)
            dvc = jnp.concatenate(dvc_parts, axis=1)

            @pl.when(n > 0)
            def _():
                out_ref[:, 0:ATTN_W] = dq_s[...].astype(BF16)
                out_ref[:, ATTN_W:2 * ATTN_W] = (dk_s[...] + dkp).astype(BF16)
                out_ref[:, 2 * ATTN_W:3 * ATTN_W] = (dv_s[...] + dvp).astype(BF16)

            dq_s[...] = dq
            dk_s[...] = dkc
            dv_s[...] = dvc

        @pl.when(n == nb)
        def _():
            out_ref[:, 0:ATTN_W] = dq_s[...].astype(BF16)
            out_ref[:, ATTN_W:2 * ATTN_W] = dk_s[...].astype(BF16)
            out_ref[:, 2 * ATTN_W:3 * ATTN_W] = dv_s[...].astype(BF16)

    nc = lambda n: jnp.minimum(n, nb - 1)
    npv = lambda n: jnp.maximum(jnp.minimum(n, nb - 1) - 1, 0)
    cur = lambda part: pl.BlockSpec((None, BLK, ATTN_W), lambda r, n: (r, nc(n), part))
    prev = lambda part: pl.BlockSpec((None, BLK, ATTN_W), lambda r, n: (r, npv(n), part))
    row = pl.BlockSpec((None, BLK, ATTN_W), lambda r, n: (r, nc(n), 0))
    tab_c = pl.BlockSpec((None, BLK, LANES), lambda r, n: (r, nc(n), 0))
    tab_p = pl.BlockSpec((None, BLK, LANES), lambda r, n: (r, npv(n), 0))
    out_spec = pl.BlockSpec((None, BLK, GROUP_COLS), lambda r, n: (r, jnp.maximum(n - 1, 0), 0))
    out = pl.pallas_call(
        body, grid=(dil, nb + 1),
        in_specs=[cur(0), cur(1), prev(1), cur(2), prev(2), row, row, row, tab_c, tab_c, tab_p, tab_p],
        out_specs=out_spec,
        out_shape=jax.ShapeDtypeStruct((dil, length, GROUP_COLS), BF16),
        scratch_shapes=[pltpu.VMEM((BLK, ATTN_W), F32)] * 3,
        compiler_params=_cparams(2), name=f"attn_bwd_g{g}")(
            qkv_v, qkv_v, qkv_v, qkv_v, qkv_v, do, cc, lse, cos_v, sin_v, cos_v, sin_v)
    return out.reshape(t, GROUP_COLS)


SQRT_HALF = 0.7071067811865476
INV_SQRT_2PI = 0.3989422804014327


def _sgu_core(uv, g, b, w_ref, bias):
    cdf = 0.5 * (1.0 + lax.erf(uv * SQRT_HALF))
    z = uv * cdf
    u, v = z[:, :SGU_W], z[:, SGU_W:]
    mu = jnp.mean(v, axis=1, keepdims=True)
    xc = v - mu
    rs = lax.rsqrt(jnp.mean(xc * xc, axis=1, keepdims=True) + EPS)
    xhat = xc * rs
    vn = xhat * g + b
    row = lax.broadcasted_iota(jnp.int32, (SGU_CHUNK, SGU_CHUNK), 0)
    col = lax.broadcasted_iota(jnp.int32, (SGU_CHUNK, SGU_CHUNK), 1)
    tril = row >= col
    upper = lax.broadcasted_iota(jnp.int32, (SGU_CHUNK, LANES), 1) >= SGU_W // SGU_GROUPS
    ws, vlo, vhi, mixed = [], [], [], []
    for pr in range(SGU_W // LANES):
        sl = slice(pr * LANES, (pr + 1) * LANES)
        w0 = jnp.where(tril, w_ref[2 * pr], 0.0).astype(BF16)
        w1 = jnp.where(tril, w_ref[2 * pr + 1], 0.0).astype(BF16)
        vn2 = vn[:, sl]
        lo = jnp.where(upper, 0.0, vn2).astype(BF16)
        hi = jnp.where(upper, vn2, 0.0).astype(BF16)
        mixed.append(jnp.dot(w0, lo, preferred_element_type=F32) + jnp.dot(w1, hi, preferred_element_type=F32)
                     + bias[:, sl])
        ws.append((w0, w1))
        vlo.append(lo)
        vhi.append(hi)
    return cdf, u, xhat, rs, jnp.concatenate(mixed, axis=1), ws, vlo, vhi, tril, upper


def _sgu_fwd(gu, ln_g, ln_b, w_s, bias_exp):
    t = gu.shape[0]

    def body(uv_ref, g_ref, b_ref, w_ref, bias_ref, o_ref):
        _, u, _, _, mixed, *_ = _sgu_core(uv_ref[...].astype(F32), g_ref[...], b_ref[...], w_ref, bias_ref[...])
        o_ref[...] = (u * mixed).astype(BF16)

    return pl.pallas_call(
        body, grid=(t // SGU_CHUNK,),
        in_specs=[pl.BlockSpec((SGU_CHUNK, 2 * SGU_W), lambda n: (n, 2)), _full((1, SGU_W)), _full((1, SGU_W)),
                  _full((SGU_GROUPS, SGU_CHUNK, SGU_CHUNK)), _full((SGU_CHUNK, SGU_W))],
        out_specs=pl.BlockSpec((SGU_CHUNK, SGU_W), lambda n: (n, 0)),
        out_shape=jax.ShapeDtypeStruct((t, SGU_W), BF16),
        compiler_params=_cparams(1), name="sgu_fwd")(gu, ln_g, ln_b, w_s, bias_exp)


def _sgu_bwd(dproj, gu, dsgu, ln_g, ln_b, w_s, bias_exp):
    t = gu.shape[0]
    nchunks = t // SGU_CHUNK
    e = _head_sum_matrix()

    def body(dp_in, uv_ref, ds_ref, g_ref, b_ref, w_ref, bias_ref, e_ref, out_ref, dw_ref, dbias_ref, dg_ref, db_ref):
        n = pl.program_id(0)

        @pl.when(n == 0)
        def _():
            dw_ref[...] = jnp.zeros(dw_ref.shape, F32)
            dbias_ref[...] = jnp.zeros(dbias_ref.shape, F32)
            dg_ref[...] = jnp.zeros(dg_ref.shape, F32)
            db_ref[...] = jnp.zeros(db_ref.shape, F32)

        uv = uv_ref[...].astype(F32)
        g = g_ref[...]
        cdf, u, xhat, rs, mixed, ws, vlo, vhi, tril, upper = _sgu_core(uv, g, b_ref[...], w_ref, bias_ref[...])
        dsg = ds_ref[...]
        du = dsg * mixed
        dmixed = dsg * u
        dbias_ref[...] += dmixed
        dvn = []
        for pr in range(SGU_W // LANES):
            sl = slice(pr * LANES, (pr + 1) * LANES)
            dm2 = dmixed[:, sl]
            dlo = jnp.where(upper, 0.0, dm2).astype(BF16)
            dhi = jnp.where(upper, dm2, 0.0).astype(BF16)
            w0, w1 = ws[pr]
            dvn.append(lax.dot_general(w0, dlo, (TN, ((), ())), preferred_element_type=F32)
                       + lax.dot_general(w1, dhi, (TN, ((), ())), preferred_element_type=F32))
            dw0 = lax.dot_general(dlo, vlo[pr], (NT, ((), ())), preferred_element_type=F32)
            dw1 = lax.dot_general(dhi, vhi[pr], (NT, ((), ())), preferred_element_type=F32)
            dw_ref[2 * pr] += jnp.where(tril, dw0, 0.0)
            dw_ref[2 * pr + 1] += jnp.where(tril, dw1, 0.0)
        dvn = jnp.concatenate(dvn, axis=1)
        dg_ref[...] += jnp.sum(dvn * xhat, axis=0, keepdims=True)
        db_ref[...] += jnp.sum(dvn, axis=0, keepdims=True)
        dxh = dvn * g
        dv = rs * (dxh - jnp.mean(dxh, axis=1, keepdims=True) - xhat * jnp.mean(dxh * xhat, axis=1, keepdims=True))
        dz = jnp.concatenate([du, dv], axis=1)
        dgelu = cdf + uv * (INV_SQRT_2PI * jnp.exp(-0.5 * uv * uv))
        out_ref[...] = (dz * dgelu).astype(BF16)

        @pl.when(n == nchunks - 1)
        def _():
            dbias_ref[...] = _group_sum(dbias_ref[...], e_ref[...])

    outs = pl.pallas_call(
        body, grid=(nchunks,),
        in_specs=[pl.BlockSpec(memory_space=pl.ANY), pl.BlockSpec((SGU_CHUNK, 2 * SGU_W), lambda n: (n, 2)),
                  pl.BlockSpec((SGU_CHUNK, SGU_W), lambda n: (n, 0)), _full((1, SGU_W)), _full((1, SGU_W)),
                  _full((SGU_GROUPS, SGU_CHUNK, SGU_CHUNK)), _full((SGU_CHUNK, SGU_W)), _full((ATTN_W, ATTN_W))],
        out_specs=[pl.BlockSpec((SGU_CHUNK, 2 * SGU_W), lambda n: (n, 2)), _full((SGU_GROUPS, SGU_CHUNK, SGU_CHUNK)),
                   _full((SGU_CHUNK, SGU_W)), _full((1, SGU_W)), _full((1, SGU_W))],
        out_shape=[jax.ShapeDtypeStruct(dproj.shape, BF16), jax.ShapeDtypeStruct((SGU_GROUPS, SGU_CHUNK, SGU_CHUNK), F32),
                   jax.ShapeDtypeStruct((SGU_CHUNK, SGU_W), F32), jax.ShapeDtypeStruct((1, SGU_W), F32),
                   jax.ShapeDtypeStruct((1, SGU_W), F32)],
        input_output_aliases={0: 0},
        compiler_params=_cparams(1), name="sgu_bwd")(dproj, gu, dsgu, ln_g, ln_b, w_s, bias_exp, e)
    return outs


def _merge_fwd(attn, sgu, gu, x, w_pa, w_ps, w_out, g2):
    t = x.shape[0]
    tm = min(t, 256)

    def body(a_ref, s_ref, ga_ref, gb_ref, x_ref, wpa, wps, wo, g_ref, pa_ref, ps_ref, m_ref, x1_ref, h2_ref):
        pa = jnp.dot(a_ref[...], wpa[...], preferred_element_type=F32)
        ps = jnp.dot(s_ref[...], wps[...], preferred_element_type=F32)
        merged = (_sigmoid(ga_ref[...].astype(F32)) * pa + _sigmoid(gb_ref[...].astype(F32)) * ps).astype(BF16)
        x1 = x_ref[...] + jnp.dot(merged, wo[...], preferred_element_type=F32)
        xhat, _ = _rms_stats(x1)
        pa_ref[...] = pa.astype(BF16)
        ps_ref[...] = ps.astype(BF16)
        m_ref[...] = merged
        x1_ref[...] = x1
        h2_ref[...] = (xhat * g_ref[...]).astype(BF16)

    half = pl.BlockSpec((tm, ATTN_W), lambda i: (i, 0))
    full = pl.BlockSpec((tm, D_MODEL), lambda i: (i, 0))
    return pl.pallas_call(
        body, grid=(t // tm,),
        in_specs=[half, half, pl.BlockSpec((tm, D_MODEL), lambda i: (i, 0)), pl.BlockSpec((tm, D_MODEL), lambda i: (i, 1)),
                  full, _full((ATTN_W, D_MODEL)), _full((SGU_W, D_MODEL)), _full((D_MODEL, D_MODEL)), _full((1, D_MODEL))],
        out_specs=[full] * 5,
        out_shape=[jax.ShapeDtypeStruct((t, D_MODEL), BF16), jax.ShapeDtypeStruct((t, D_MODEL), BF16),
                   jax.ShapeDtypeStruct((t, D_MODEL), BF16), jax.ShapeDtypeStruct((t, D_MODEL), F32),
                   jax.ShapeDtypeStruct((t, D_MODEL), BF16)],
        compiler_params=_cparams(1), name="merge_fwd")(attn, sgu, gu, gu, x, w_pa, w_ps, w_out, g2)


def _merge_bwd(dx1b, gu, pa, ps, w_pa, w_ps, w_out):
    t = dx1b.shape[0]
    tm = min(t, 256)

    def body(d_ref, ga_ref, gb_ref, pa_ref, ps_ref, wpa, wps, wo, out_ref, dpa_ref, dps_ref, da_ref, dsg_ref):
        dm = lax.dot_general(d_ref[...], wo[...], (NT, ((), ())), preferred_element_type=F32)
        sa, sb = _sigmoid(ga_ref[...].astype(F32)), _sigmoid(gb_ref[...].astype(F32))
        dpa = (dm * sa).astype(BF16)
        dps = (dm * sb).astype(BF16)
        out_ref[:, 0:D_MODEL] = (dm * pa_ref[...].astype(F32) * sa * (1.0 - sa)).astype(BF16)
        out_ref[:, D_MODEL:2 * D_MODEL] = (dm * ps_ref[...].astype(F32) * sb * (1.0 - sb)).astype(BF16)
        out_ref[:, 2 * D_MODEL:GU_COLS] = jnp.zeros((tm, GU_COLS - 2 * D_MODEL), BF16)
        dpa_ref[...] = dpa
        dps_ref[...] = dps
        da_ref[...] = lax.dot_general(dpa, wpa[...], (NT, ((), ())), preferred_element_type=F32)
        dsg_ref[...] = lax.dot_general(dps, wps[...], (NT, ((), ())), preferred_element_type=F32)

    half = pl.BlockSpec((tm, ATTN_W), lambda i: (i, 0))
    full = pl.BlockSpec((tm, D_MODEL), lambda i: (i, 0))
    return pl.pallas_call(
        body, grid=(t // tm,),
        in_specs=[full, pl.BlockSpec((tm, D_MODEL), lambda i: (i, 0)),
                  pl.BlockSpec((tm, D_MODEL), lambda i: (i, 1)), full, full,
                  _full((ATTN_W, D_MODEL)), _full((SGU_W, D_MODEL)), _full((D_MODEL, D_MODEL))],
        out_specs=[pl.BlockSpec((tm, GU_COLS), lambda i: (i, 0)), full, full, half, half],
        out_shape=[jax.ShapeDtypeStruct((t, GU_COLS), BF16), jax.ShapeDtypeStruct((t, D_MODEL), BF16),
                   jax.ShapeDtypeStruct((t, D_MODEL), BF16), jax.ShapeDtypeStruct((t, ATTN_W), F32),
                   jax.ShapeDtypeStruct((t, SGU_W), F32)],
        compiler_params=_cparams(1), name="merge_bwd")(dx1b, gu, gu, pa, ps, w_pa, w_ps, w_out)


def _token_call(name, body, t, tm, ins, outs, reds=(), scratch=()):
    return pl.pallas_call(
        body, grid=(t // tm,), in_specs=[s for _, s in ins],
        out_specs=[o[2] for o in outs] + [_full(r) for r in reds],
        out_shape=[jax.ShapeDtypeStruct(o[0], o[1]) for o in outs] + [jax.ShapeDtypeStruct(r, F32) for r in reds],
        scratch_shapes=list(scratch), compiler_params=_cparams(1), name=name)(*[a for a, _ in ins])


def _rows_spec(tm, width):
    return pl.BlockSpec((tm, width), lambda i: (i, 0))


def _chips_spec(tm):
    return pl.BlockSpec((N_CHIPS, tm, FF_SHARD), lambda i: (0, i, 0))


def _zero_at_start(*refs):
    @pl.when(pl.program_id(0) == 0)
    def _():
        for r in refs:
            r[...] = jnp.zeros(r.shape, r.dtype)


def _ffn_fwd(h2, w_g, w_u):
    t = h2.shape[0]
    tm = min(t, 512)

    def body(h_ref, wg_ref, wu_ref, a_ref, b_ref, ff_ref):
        h = h_ref[...]
        for s in range(N_CHIPS):
            a = jnp.dot(h, wg_ref[s], preferred_element_type=F32)
            b = jnp.dot(h, wu_ref[s], preferred_element_type=F32)
            a_ref[s] = a.astype(BF16)
            b_ref[s] = b.astype(BF16)
            ff_ref[s] = (a * _sigmoid(a) * b).astype(BF16)

    shp = (N_CHIPS, t, FF_SHARD)
    w_spec = _resident((N_CHIPS, D_MODEL, FF_SHARD))
    return _token_call("ffn_fwd", body, t, tm, [(h2, _rows_spec(tm, D_MODEL)), (w_g, w_spec), (w_u, w_spec)],
                       [(shp, BF16, _chips_spec(tm))] * 3)


def _ffn_down_loss(ff, w_d, x1, tgt, gf):
    t = x1.shape[0]
    tm = min(t, 512)

    def body(ff_ref, wd_ref, x1_ref, tgt_ref, g_ref, dx2_ref, dx2b_ref, loss_ref, dgf_ref):
        _zero_at_start(loss_ref, dgf_ref)
        acc = jnp.dot(ff_ref[0], wd_ref[0], preferred_element_type=F32)
        for s in range(1, N_CHIPS):
            acc = acc + jnp.dot(ff_ref[s], wd_ref[s], preferred_element_type=F32)
        x2 = x1_ref[...] + acc
        g = g_ref[...]
        xhat, rr = _rms_stats(x2)
        diff = xhat * g - tgt_ref[...]
        rows = jnp.sum(diff * diff, axis=1, keepdims=True)
        loss_ref[...] += jnp.broadcast_to(jnp.sum(rows, axis=0, keepdims=True) * (0.5 / D_MODEL), (1, LANES))
        dy = diff * (1.0 / D_MODEL)
        dgf_ref[...] += jnp.sum(dy * xhat, axis=0, keepdims=True)
        dx2 = _rms_bwd(dy, xhat, rr, g)
        dx2_ref[...] = dx2
        dx2b_ref[...] = dx2.astype(BF16)

    row = _rows_spec(tm, D_MODEL)
    return _token_call("ffn_down_loss", body, t, tm,
                       [(ff, _chips_spec(tm)), (w_d, _resident((N_CHIPS, FF_SHARD, D_MODEL))), (x1, row), (tgt, row),
                        (gf, _full((1, D_MODEL)))],
                       [((t, D_MODEL), F32, row), ((t, D_MODEL), BF16, row)], reds=[(1, LANES), (1, D_MODEL)])


def _ffn_bwd_act(dx2b, w_d, a, b):
    t = dx2b.shape[0]
    tm = min(t, 512)

    def body(d_ref, wd_ref, a_ref, b_ref, da_ref, db_ref):
        d = d_ref[...]
        for s in range(N_CHIPS):
            dff = lax.dot_general(d, wd_ref[s], (NT, ((), ())), preferred_element_type=F32)
            av, bv = a_ref[s].astype(F32), b_ref[s].astype(F32)
            sg = _sigmoid(av)
            da_ref[s] = (dff * bv * (sg * (1.0 + av * (1.0 - sg)))).astype(BF16)
            db_ref[s] = (dff * (av * sg)).astype(BF16)

    shp = (N_CHIPS, t, FF_SHARD)
    return _token_call("ffn_bwd_act", body, t, tm,
                       [(dx2b, _rows_spec(tm, D_MODEL)), (w_d, _resident((N_CHIPS, FF_SHARD, D_MODEL))),
                        (a, _chips_spec(tm)), (b, _chips_spec(tm))],
                       [(shp, BF16, _chips_spec(tm))] * 2)


def _ffn_bwd_in(da, db, w_g, w_u, x1, dx2, g2):
    t = x1.shape[0]
    tm = min(t, 512)

    def body(da_ref, db_ref, wg_ref, wu_ref, x1_ref, dx2_ref, g_ref, dx1_ref, dx1b_ref, dg_ref):
        _zero_at_start(dg_ref)
        acc = None
        for s in range(N_CHIPS):
            part = (lax.dot_general(da_ref[s], wg_ref[s], (NT, ((), ())), preferred_element_type=F32)
                    + lax.dot_general(db_ref[s], wu_ref[s], (NT, ((), ())), preferred_element_type=F32))
            acc = part if acc is None else acc + part
        xhat, rr = _rms_stats(x1_ref[...])
        dg_ref[...] += jnp.sum(acc * xhat, axis=0, keepdims=True)
        dx1 = dx2_ref[...] + _rms_bwd(acc, xhat, rr, g_ref[...])
        dx1_ref[...] = dx1
        dx1b_ref[...] = dx1.astype(BF16)

    row = _rows_spec(tm, D_MODEL)
    w_spec = _resident((N_CHIPS, D_MODEL, FF_SHARD))
    return _token_call("ffn_bwd_in", body, t, tm,
                       [(da, _chips_spec(tm)), (db, _chips_spec(tm)), (w_g, w_spec), (w_u, w_spec), (x1, row), (dx2, row),
                        (g2, _full((1, D_MODEL)))],
                       [((t, D_MODEL), F32, row), ((t, D_MODEL), BF16, row)], reds=[(1, D_MODEL)])


def _in_proj_bwd(dgu, dqkvs, w_p, x, dx1, g1):
    t = x.shape[0]
    tile = min(t, TILE)
    tm = min(t, 1024)
    nat_cols = GU_COLS + GROUP_COLS

    dhs = []
    for g in (1, 2):
        col = (GU_COLS + g * GROUP_COLS) // GROUP_COLS

        def body_g(d_ref, w_ref, o_ref):
            o_ref[...] = lax.dot_general(d_ref[...], w_ref[...], (NT, ((), ())), preferred_element_type=F32)

        dh = _token_call(
            f"in_proj_bwd_g{g}", body_g, t, tm,
            [(dqkvs[g], _rows_spec(tm, GROUP_COLS)),
             (w_p, pl.BlockSpec((D_MODEL, GROUP_COLS), lambda i, col=col: (0, col), pipeline_mode=pl.Buffered(1)))],
            [((t, D_MODEL), F32, _rows_spec(tm, D_MODEL))])[0]
        dhs.append(dh.reshape(DILATIONS[g], t // DILATIONS[g], D_MODEL))

    def body(dgu_ref, dq0_ref, w_ref, x_ref, dx1_ref, g_ref, dh1_ref, dh2_ref, dx_ref, dg_ref, slab):
        _zero_at_start(dg_ref)
        dh = lax.dot_general(dgu_ref[...], w_ref[:, 0:GU_COLS], (NT, ((), ())), preferred_element_type=F32)
        dh = dh + lax.dot_general(dq0_ref[...], w_ref[:, GU_COLS:nat_cols], (NT, ((), ())), preferred_element_type=F32)
        dh = dh + _natural_from_group(slab, dh1_ref)
        dh = dh + _natural_from_group(slab, dh2_ref)
        xhat, rr = _rms_stats(x_ref[...])
        dg_ref[...] += jnp.sum(dh * xhat, axis=0, keepdims=True)
        dx_ref[...] = dx1_ref[...] + _rms_bwd(dh, xhat, rr, g_ref[...])

    row = _rows_spec(tile, D_MODEL)
    return _token_call(
        "in_proj_bwd", body, t, tile,
        [(dgu, _rows_spec(tile, GU_COLS)), (dqkvs[0], _rows_spec(tile, GROUP_COLS)),
         (w_p, pl.BlockSpec((D_MODEL, nat_cols), lambda i: (0, 0), pipeline_mode=pl.Buffered(1))),
         (x, row), (dx1, row), (g1, _full((1, D_MODEL))),
         (dhs[0], _group_spec(DILATIONS[1], tile, D_MODEL)), (dhs[1], _group_spec(DILATIONS[2], tile, D_MODEL))],
        [((t, D_MODEL), F32, row)], reds=[(1, D_MODEL)], scratch=[_slabs(tile, D_MODEL)])


def _epi_bf16(acc, e, o, r, ids):
    o[0][...] = acc.astype(BF16)


WGRAD_TK = 2048


def _wgrad_2d(name, a, b, tm, tn):
    t, k1 = a.shape
    n = b.shape[1]
    tk = min(t, WGRAD_TK)
    return _mm(name, (k1 // tm, n // tn, t // tk),
               [(a, pl.BlockSpec((tk, tm), lambda i, j, k: (k, i)), b, pl.BlockSpec((tk, tn), lambda i, j, k: (k, j)))],
               TN, (tm, tn), _epi_bf16, outs=[((k1, n), BF16, pl.BlockSpec((tm, tn), lambda i, j, k: (i, j)))])[0]


def _wgrad_in(hs, dgu, dqkvs):
    t = dgu.shape[0]
    tk = min(t, WGRAD_TK)
    dst = None
    parts = [(hs[0], dgu, 0)] + [(hs[g].reshape(t, D_MODEL), dqkvs[g], GU_COLS // GROUP_COLS + g) for g in range(3)]
    for n, (a, b, col0) in enumerate(parts):
        dst = _mm(f"wgrad_in_{n}", (1, b.shape[1] // GROUP_COLS, t // tk),
                  [(a, pl.BlockSpec((tk, D_MODEL), lambda i, j, k: (k, 0)), b,
                    pl.BlockSpec((tk, GROUP_COLS), lambda i, j, k: (k, j)))],
                  TN, (D_MODEL, GROUP_COLS), _epi_bf16,
                  extras=[] if dst is None else [(dst, pl.BlockSpec(memory_space=pl.ANY))],
                  outs=[((D_MODEL, IN_COLS), BF16, pl.BlockSpec((D_MODEL, GROUP_COLS), lambda i, j, k, col0=col0: (0, j + col0)))],
                  aliases=None if dst is None else {2: 0})[0]
    return dst


def _wgrad_ff_in(name, h2, da):
    t = h2.shape[0]
    tk = min(t, WGRAD_TK)
    return _mm(name, (N_CHIPS, 1, t // tk),
               [(h2, pl.BlockSpec((tk, D_MODEL), lambda i, j, k: (k, 0)),
                 da, pl.BlockSpec((None, tk, FF_SHARD), lambda i, j, k: (i, k, 0)))],
               TN, (D_MODEL, FF_SHARD), _epi_bf16,
               outs=[((N_CHIPS, D_MODEL, FF_SHARD), BF16, pl.BlockSpec((None, D_MODEL, FF_SHARD), lambda i, j, k: (i, 0, 0)))])[0]


def _wgrad_ff_down(ff, dx2b):
    t = dx2b.shape[0]
    tk = min(t, WGRAD_TK)
    return _mm("wgrad_ffn_down", (N_CHIPS, 1, t // tk),
               [(ff, pl.BlockSpec((None, tk, FF_SHARD), lambda i, j, k: (i, k, 0)),
                 dx2b, pl.BlockSpec((tk, D_MODEL), lambda i, j, k: (k, 0)))],
               TN, (FF_SHARD, D_MODEL), _epi_bf16,
               outs=[((N_CHIPS, FF_SHARD, D_MODEL), BF16, pl.BlockSpec((None, FF_SHARD, D_MODEL), lambda i, j, k: (i, 0, 0)))])[0]


def _local_step(x, pos_col, tgt, g1, ln_g, ln_b, w_s, b_s, g2, gf, w_p, w_pa, w_ps, w_out, w_g, w_u, w_d):
    tables = _rope_tables(pos_col)
    bias_exp = jnp.repeat(jnp.transpose(b_s), SGU_W // SGU_GROUPS, axis=1)

    hs = _norm_fwd(x, g1)
    gu, qkvs = _in_proj(hs, w_p, tables)
    os_, ls_ = [], []
    for g, dil in enumerate(DILATIONS):
        o, lse = _attn_fwd(qkvs[g], g, dil)
        os_.append(o)
        ls_.append(lse)
    attn = _combine_fwd(os_, ls_)
    sgu = _sgu_fwd(gu, ln_g, ln_b, w_s, bias_exp)
    pa, ps, merged, x1, h2 = _merge_fwd(attn, sgu, gu, x, w_pa, w_ps, w_out, g2)
    a, b, ff = _ffn_fwd(h2, w_g, w_u)
    dx2, dx2b, loss, dgf = _ffn_down_loss(ff, w_d, x1, tgt, gf)

    da, db = _ffn_bwd_act(dx2b, w_d, a, b)
    dw_d = _wgrad_ff_down(ff, dx2b)
    dx1, dx1b, dg2 = _ffn_bwd_in(da, db, w_g, w_u, x1, dx2, g2)
    dw_g = _wgrad_ff_in("wgrad_ffn_gate", h2, da)
    dw_u = _wgrad_ff_in("wgrad_ffn_up", h2, db)

    dgu, dpa, dps, dattn, dsgu = _merge_bwd(dx1b, gu, pa, ps, w_pa, w_ps, w_out)
    dw_out = _wgrad_2d("wgrad_out", merged, dx1b, D_MODEL, D_MODEL)
    dw_pa = _wgrad_2d("wgrad_proj_attn", attn, dpa, ATTN_W, D_MODEL)
    dw_ps = _wgrad_2d("wgrad_proj_sgu", sgu, dps, SGU_W, D_MODEL)
    dgu, dw_s, dbias, dln_g, dln_b = _sgu_bwd(dgu, gu, dsgu, ln_g, ln_b, w_s, bias_exp)
    dos, ccs = _combine_bwd(dattn, os_, ls_)
    dqkvs = [_attn_bwd(qkvs[g], dos[g], ccs[g], ls_[g], *tables[g], g, dil) for g, dil in enumerate(DILATIONS)]
    dx, dg1 = _in_proj_bwd(dgu, dqkvs, w_p, x, dx1, g1)
    dw_p = _wgrad_in(hs, dgu, dqkvs)

    db_s = jnp.transpose(dbias[:, ::SGU_W // SGU_GROUPS])
    small = dict(loss=loss, norm1_g=dg1, sgu_ln_g=dln_g, sgu_ln_b=dln_b, w_spatial=dw_s, b_spatial=db_s,
                 norm2_g=dg2, final_g=dgf)
    big = dict(w_in=dw_p, w_proj_attn=dw_pa, w_proj_sgu=dw_ps, w_out=dw_out, w_ffn_gate=dw_g, w_ffn_up=dw_u,
               w_ffn_down=dw_d)
    return dx, big, small


def _ew(name, fn, ins, out_dtypes):
    shp = ins[0].shape
    rows, cols = shp
    tr = next((cand for cand in (256, 352, 128) if rows % cand == 0 and rows > cand), rows)

    def body(*refs):
        res = fn(*[r[...] for r in refs[:len(ins)]])
        for o_ref, v in zip(refs[len(ins):], res):
            o_ref[...] = v.astype(o_ref.dtype)

    spec = pl.BlockSpec((tr, cols), lambda i: (i, 0))
    return pl.pallas_call(
        body, grid=(rows // tr,), in_specs=[spec] * len(ins), out_specs=[spec] * len(out_dtypes),
        out_shape=[jax.ShapeDtypeStruct(shp, d) for d in out_dtypes],
        compiler_params=_cparams(1), name=name)(*ins)


def _adamw_math(g, w, m, v):
    m = ADAM_B1 * m + (1.0 - ADAM_B1) * g
    v = ADAM_B2 * v + (1.0 - ADAM_B2) * (g * g)
    m_hat = m / (1.0 - ADAM_B1 ** ADAM_STEP)
    v_hat = v / (1.0 - ADAM_B2 ** ADAM_STEP)
    delta = -ADAM_LR * (m_hat / (jnp.sqrt(v_hat) + ADAM_EPS) + ADAM_WD * w)
    return delta, m, v


def _adamw(name, g, w, m, v):
    return _ew(name, lambda g_, w_, m_, v_: (g_,) + _adamw_math(g_, w_, m_, v_), [g, w, m, v], [F32] * 4)


VMEM_SPEC = pl.BlockSpec(memory_space=pltpu.VMEM)


def _for_row_chunks(rows, fn):
    ck = next(c for c in (64, 32, 16) if rows % c == 0)

    def step(i, carry):
        fn(pl.multiple_of(i * ck, ck), ck)
        return carry

    lax.fori_loop(0, rows // ck, step, 0)


def _place():
    x, y, c = lax.axis_index("x"), lax.axis_index("y"), lax.axis_index("c")
    chips = [(1 - x, y), (x, 1 - y), (1 - x, 1 - y)]
    return x, y, c, 2 * x + y, chips


def _rows(ref, start, size):
    if len(ref.shape) == 2:
        return ref.at[pl.ds(start, size), :]
    return ref.at[:, pl.ds(start, size), :]


def _comm_call(name, body, ins, out_shapes, scratch, n_remote):
    return pl.pallas_call(
        body, in_specs=[VMEM_SPEC] * len(ins), out_specs=[VMEM_SPEC] * len(out_shapes),
        out_shape=out_shapes,
        scratch_shapes=list(scratch) + [pltpu.SemaphoreType.DMA((n_remote,)), pltpu.SemaphoreType.DMA((n_remote,))],
        compiler_params=pltpu.CompilerParams(vmem_limit_bytes=VMEM_LIMIT), name=name)(*ins)


def _gather_weights(name, shards):
    nt = len(shards)

    def body(*refs):
        ins, outs = refs[:nt], refs[nt:2 * nt]
        send, recv = refs[2 * nt:]
        x, y, c, me, chips = _place()
        sibling = (x, y, 1 - c)
        firsts, passed, expects = [], [], []
        for t in range(nt):
            kh = ins[t].shape[0] // 2
            for j, chip in enumerate(chips):
                k = t * 3 + j
                theirs = 2 * chip[0] + chip[1]
                firsts.append(pltpu.make_async_remote_copy(
                    src_ref=_rows(ins[t], c * kh, kh), dst_ref=_rows(outs[t].at[me], c * kh, kh),
                    send_sem=send.at[k], recv_sem=recv.at[k], device_id=(*chip, c), device_id_type=MESH))
                landed = _rows(outs[t].at[theirs], c * kh, kh)
                expects.append(pltpu.make_async_remote_copy(
                    src_ref=landed, dst_ref=landed, send_sem=send.at[k], recv_sem=recv.at[k],
                    device_id=(*chip, c), device_id_type=MESH))
                passed.append(pltpu.make_async_remote_copy(
                    src_ref=landed, dst_ref=landed, send_sem=send.at[3 * nt + k], recv_sem=recv.at[3 * nt + k],
                    device_id=sibling, device_id_type=MESH))
        for cp in firsts:
            cp.start()
        for t in range(nt):
            mine = outs[t].at[me]

            def put(r0, ck, src=ins[t], dst=mine):
                dst[pl.ds(r0, ck), :] = src[pl.ds(r0, ck), :]

            _for_row_chunks(ins[t].shape[0], put)
        for k in range(3 * nt):
            expects[k].wait_recv()
            passed[k].start()
        for t in range(nt):
            kh = ins[t].shape[0] // 2
            for j, chip in enumerate(chips):
                k = t * 3 + j
                theirs = 2 * chip[0] + chip[1]
                other = _rows(outs[t].at[theirs], (1 - c) * kh, kh)
                pltpu.make_async_remote_copy(
                    src_ref=other, dst_ref=other, send_sem=send.at[3 * nt + k], recv_sem=recv.at[3 * nt + k],
                    device_id=sibling, device_id_type=MESH).wait_recv()
        for cp in firsts + passed:
            cp.wait_send()

    out_shapes = [jax.ShapeDtypeStruct((N_CHIPS,) + s.shape, s.dtype) for s in shards]
    return _comm_call(name, body, shards, out_shapes, [], 6 * nt)


def _pair_reduce(name, grads):
    nt = len(grads)

    def half(s):
        shp = list(s.shape)
        shp[-2] //= 2
        return tuple(shp)

    def body(*refs):
        ins, outs, got = refs[:nt], refs[nt:2 * nt], refs[2 * nt:3 * nt]
        send, recv = refs[3 * nt:]
        x, y, c, me, chips = _place()
        copies = []
        for t in range(nt):
            kh = ins[t].shape[-2] // 2
            rc = pltpu.make_async_remote_copy(
                src_ref=_rows(ins[t], (1 - c) * kh, kh), dst_ref=got[t], send_sem=send.at[t], recv_sem=recv.at[t],
                device_id=(x, y, 1 - c), device_id_type=MESH)
            rc.start()
            copies.append(rc)
        for t in range(nt):
            kh = ins[t].shape[-2] // 2
            copies[t].wait_recv()
            for lead in ([()] if len(ins[t].shape) == 2 else [(s,) for s in range(ins[t].shape[0])]):

                def add(r0, ck, lead=lead, src=ins[t], oth=got[t], dst=outs[t], kh=kh):
                    own = src[lead + (pl.ds(pl.multiple_of(c * kh + r0, ck), ck), slice(None))]
                    rows = lead + (pl.ds(r0, ck), slice(None))
                    dst[rows] = (own.astype(F32) + oth[rows].astype(F32)).astype(BF16)

                _for_row_chunks(kh, add)
        for rc in copies:
            rc.wait_send()

    shapes = [half(g) for g in grads]
    return _comm_call(name, body, grads, [jax.ShapeDtypeStruct(s, BF16) for s in shapes],
                      [pltpu.VMEM(s, BF16) for s in shapes], nt)


def _chip_reduce(name, sums):
    nt = len(sums)

    def cols(s):
        return s[2] if len(s) == 3 else s[1] // N_CHIPS

    def piece(ref, j):
        if len(ref.shape) == 3:
            return ref.at[j]
        n4 = ref.shape[1] // N_CHIPS
        return ref.at[:, pl.ds(j * n4, n4)]

    def body(*refs):
        ins, outs, slots = refs[:nt], refs[nt:2 * nt], refs[2 * nt:3 * nt]
        send, recv = refs[3 * nt:]
        x, y, c, me, chips = _place()
        sibling = (x, y, 1 - c)
        copies = []
        for t in range(nt):
            for j, chip in enumerate(chips):
                k = t * 3 + j
                rc = pltpu.make_async_remote_copy(
                    src_ref=piece(ins[t], 2 * chip[0] + chip[1]), dst_ref=slots[t].at[j], send_sem=send.at[k],
                    recv_sem=recv.at[k], device_id=(*chip, c), device_id_type=MESH)
                rc.start()
                copies.append(rc)
        handed = []
        for t in range(nt):
            kh, n4 = ins[t].shape[-2], outs[t].shape[1]
            for j in range(3):
                copies[t * 3 + j].wait_recv()
            for jj in range(N_CHIPS):

                @pl.when(me == jj)
                def _(jj=jj, src=ins[t], slot=slots[t], dst=outs[t], kh=kh, n4=n4):
                    def add(r0, ck):
                        rows = pl.ds(r0, ck)
                        own = src[jj, rows, :] if len(src.shape) == 3 else src[rows, jj * n4:(jj + 1) * n4]
                        acc = ((own.astype(F32) + slot[0, rows, :].astype(F32)) + slot[1, rows, :].astype(F32)) \
                            + slot[2, rows, :].astype(F32)
                        dst[pl.ds(pl.multiple_of(c * kh + r0, ck), ck), :] = acc

                    _for_row_chunks(kh, add)

            rc = pltpu.make_async_remote_copy(
                src_ref=_rows(outs[t], c * kh, kh), dst_ref=_rows(outs[t], c * kh, kh), send_sem=send.at[3 * nt + t],
                recv_sem=recv.at[3 * nt + t], device_id=sibling, device_id_type=MESH)
            rc.start()
            handed.append(rc)
        for t in range(nt):
            kh = ins[t].shape[-2]
            other = _rows(outs[t], (1 - c) * kh, kh)
            pltpu.make_async_remote_copy(
                src_ref=other, dst_ref=other, send_sem=send.at[3 * nt + t], recv_sem=recv.at[3 * nt + t],
                device_id=sibling, device_id_type=MESH).wait_recv()
        for rc in copies + handed:
            rc.wait_send()

    out_shapes = [jax.ShapeDtypeStruct((2 * s.shape[-2], cols(s.shape)), F32) for s in sums]
    scratch = [pltpu.VMEM((3, s.shape[-2], cols(s.shape)), BF16) for s in sums]
    return _comm_call(name, body, sums, out_shapes, scratch, 4 * nt)


VEC_SHAPE = (8, D_MODEL + LANES)
VEC_SLOTS = dict(norm1_g=(slice(0, 1), slice(0, D_MODEL)), norm2_g=(slice(1, 2), slice(0, D_MODEL)),
                 final_g=(slice(2, 3), slice(0, D_MODEL)), sgu_ln_g=(slice(3, 4), slice(0, SGU_W)),
                 sgu_ln_b=(slice(3, 4), slice(SGU_W, 2 * SGU_W)), b_spatial=(slice(0, 8), slice(D_MODEL, D_MODEL + LANES)),
                 loss=(slice(4, 5), slice(0, LANES)))
VEC_PARAMS = ("norm1_g", "norm2_g", "final_g", "sgu_ln_g", "sgu_ln_b", "b_spatial")
SMALL_PARAMS = VEC_PARAMS + ("w_spatial",)
W_SPATIAL_2D = (SGU_GROUPS * SGU_CHUNK, SGU_CHUNK)


def _small_step(partials, w, m, v):
    def shape2d(name):
        if name == "w_spatial":
            return W_SPATIAL_2D
        rows, cols = VEC_SLOTS[name]
        return (rows.stop - rows.start, cols.stop - cols.start)

    g_names = VEC_PARAMS + ("loss", "w_spatial")
    ins = [partials[n].reshape(shape2d(n)) for n in g_names]
    for src in (w, m, v):
        ins += [src[n].reshape(shape2d(n)) for n in SMALL_PARAMS]
    ng, npar = len(g_names), len(SMALL_PARAMS)

    def body(*refs):
        g_in = dict(zip(g_names, refs[:ng]))
        w_in, m_in, v_in = (dict(zip(SMALL_PARAMS, refs[ng + k * npar:ng + (k + 1) * npar])) for k in range(3))
        o0 = ng + 3 * npar
        g_out = dict(zip(g_names, refs[o0:o0 + ng]))
        d_out, m_out, v_out = (dict(zip(SMALL_PARAMS, refs[o0 + ng + k * npar:o0 + ng + (k + 1) * npar])) for k in range(3))
        vec, vec_pair, vec_slot, ws_pair, ws_slot, vw, vm, vv, send, recv = refs[o0 + ng + 3 * npar:]
        x, y, c, me, chips = _place()
        sibling = (x, y, 1 - c)

        def pack(dst, parts):
            dst[...] = jnp.zeros(VEC_SHAPE, F32)
            for n, ref in parts.items():
                if n in VEC_SLOTS:
                    dst[VEC_SLOTS[n]] = ref[...]

        pack(vec, g_in)
        copies = []

        def allreduce(k0, src, pair, slot):
            first = pltpu.make_async_remote_copy(src_ref=src, dst_ref=pair, send_sem=send.at[k0], recv_sem=recv.at[k0],
                                                 device_id=sibling, device_id_type=MESH)
            first.start()
            first.wait_recv()
            slot[me] = src[...] + pair[...]
            arrivals = []
            for j, chip in enumerate(chips):
                theirs = 2 * chip[0] + chip[1]
                rc = pltpu.make_async_remote_copy(src_ref=slot.at[me], dst_ref=slot.at[me], send_sem=send.at[k0 + 1 + j],
                                                  recv_sem=recv.at[k0 + 1 + j], device_id=(*chip, c), device_id_type=MESH)
                rc.start()
                arrivals.append(pltpu.make_async_remote_copy(
                    src_ref=slot.at[theirs], dst_ref=slot.at[theirs], send_sem=send.at[k0 + 1 + j],
                    recv_sem=recv.at[k0 + 1 + j], device_id=(*chip, c), device_id_type=MESH))
                copies.append(rc)
            copies.append(first)
            return arrivals

        arrivals = allreduce(0, vec, vec_pair, vec_slot) + allreduce(4, g_in["w_spatial"], ws_pair, ws_slot)
        pack(vw, w_in)
        pack(vm, m_in)
        pack(vv, v_in)
        for a in arrivals:
            a.wait_recv()
        for rc in copies:
            rc.wait_send()

        g_vec = ((vec_slot[0] + vec_slot[1]) + vec_slot[2]) + vec_slot[3]
        d_vec, m_vec, v_vec = _adamw_math(g_vec, vw[...], vm[...], vv[...])
        vec[...] = g_vec
        vw[...] = d_vec
        vm[...] = m_vec
        vv[...] = v_vec
        for n in VEC_PARAMS + ("loss",):
            g_out[n][...] = vec[VEC_SLOTS[n]]
        for n in VEC_PARAMS:
            d_out[n][...] = vw[VEC_SLOTS[n]]
            m_out[n][...] = vm[VEC_SLOTS[n]]
            v_out[n][...] = vv[VEC_SLOTS[n]]

        def spatial(r0, ck):
            rows = pl.ds(r0, ck)
            g = ((ws_slot[0, rows, :] + ws_slot[1, rows, :]) + ws_slot[2, rows, :]) + ws_slot[3, rows, :]
            d_, m_, v_ = _adamw_math(g, w_in["w_spatial"][rows, :], m_in["w_spatial"][rows, :], v_in["w_spatial"][rows, :])
            g_out["w_spatial"][rows, :] = g
            d_out["w_spatial"][rows, :] = d_
            m_out["w_spatial"][rows, :] = m_
            v_out["w_spatial"][rows, :] = v_

        _for_row_chunks(W_SPATIAL_2D[0], spatial)

    out_shapes = [jax.ShapeDtypeStruct(shape2d(n), F32) for n in g_names + SMALL_PARAMS * 3]
    outs = pl.pallas_call(
        body, in_specs=[VMEM_SPEC] * len(ins), out_specs=[VMEM_SPEC] * len(out_shapes), out_shape=out_shapes,
        scratch_shapes=[pltpu.VMEM(VEC_SHAPE, F32), pltpu.VMEM(VEC_SHAPE, F32), pltpu.VMEM((N_CHIPS,) + VEC_SHAPE, F32),
                        pltpu.VMEM(W_SPATIAL_2D, F32), pltpu.VMEM((N_CHIPS,) + W_SPATIAL_2D, F32),
                        pltpu.VMEM(VEC_SHAPE, F32), pltpu.VMEM(VEC_SHAPE, F32), pltpu.VMEM(VEC_SHAPE, F32),
                        pltpu.SemaphoreType.DMA((8,)), pltpu.SemaphoreType.DMA((8,))],
        name="small_params_step")(*ins)
    grads = dict(zip(g_names, outs[:ng]))
    rest = [dict(zip(SMALL_PARAMS, outs[ng + k * npar:ng + (k + 1) * npar])) for k in range(3)]
    return grads, rest[0], rest[1], rest[2]


BIG = ("w_in", "w_proj_attn", "w_proj_sgu", "w_out", "w_ffn_gate", "w_ffn_up", "w_ffn_down")
COMM_GROUPS = (("w_in",), ("w_proj_attn", "w_proj_sgu", "w_out", "w_ffn_gate", "w_ffn_up", "w_ffn_down"))
WEIGHTS = ("norm1_g", "w_in", "sgu_ln_g", "sgu_ln_b", "w_spatial", "b_spatial", "w_proj_attn", "w_proj_sgu", "w_out",
           "norm2_g", "w_ffn_gate", "w_ffn_up", "w_ffn_down", "final_g")


def _cols_from_chips(g):
    return jnp.transpose(g, (1, 0, 2)).reshape(g.shape[1], N_CHIPS * g.shape[2])


def _permute_cols(w, perm):
    return jnp.concatenate([w[:, 512 * b:512 * (b + 1)] for b in perm], axis=1)


def kernel(x, positions, norm1_g, w_in, sgu_ln_g, sgu_ln_b, w_spatial, b_spatial, w_proj_attn, w_proj_sgu, w_out, norm2_g, w_ffn_gate, w_ffn_up, w_ffn_down, final_g, loss_target, m_norm1_g, m_w_in, m_sgu_ln_g, m_sgu_ln_b, m_w_spatial, m_b_spatial, m_w_proj_attn, m_w_proj_sgu, m_w_out, m_norm2_g, m_w_ffn_gate, m_w_ffn_up, m_w_ffn_down, m_final_g, v_norm1_g, v_w_in, v_sgu_ln_g, v_sgu_ln_b, v_w_spatial, v_b_spatial, v_w_proj_attn, v_w_proj_sgu, v_w_out, v_norm2_g, v_w_ffn_gate, v_w_ffn_up, v_w_ffn_down, v_final_g):
    w = dict(norm1_g=norm1_g, w_in=w_in, sgu_ln_g=sgu_ln_g, sgu_ln_b=sgu_ln_b, w_spatial=w_spatial, b_spatial=b_spatial,
             w_proj_attn=w_proj_attn, w_proj_sgu=w_proj_sgu, w_out=w_out, norm2_g=norm2_g, w_ffn_gate=w_ffn_gate,
             w_ffn_up=w_ffn_up, w_ffn_down=w_ffn_down, final_g=final_g)
    m = dict(norm1_g=m_norm1_g, w_in=m_w_in, sgu_ln_g=m_sgu_ln_g, sgu_ln_b=m_sgu_ln_b, w_spatial=m_w_spatial,
             b_spatial=m_b_spatial, w_proj_attn=m_w_proj_attn, w_proj_sgu=m_w_proj_sgu, w_out=m_w_out, norm2_g=m_norm2_g,
             w_ffn_gate=m_w_ffn_gate, w_ffn_up=m_w_ffn_up, w_ffn_down=m_w_ffn_down, final_g=m_final_g)
    v = dict(norm1_g=v_norm1_g, w_in=v_w_in, sgu_ln_g=v_sgu_ln_g, sgu_ln_b=v_sgu_ln_b, w_spatial=v_w_spatial,
             b_spatial=v_b_spatial, w_proj_attn=v_w_proj_attn, w_proj_sgu=v_w_proj_sgu, w_out=v_w_out, norm2_g=v_norm2_g,
             w_ffn_gate=v_w_ffn_gate, w_ffn_up=v_w_ffn_up, w_ffn_down=v_w_ffn_down, final_g=v_final_g)
    t = x.shape[1]

    shards = {n: _ew(f"cast_{n}", lambda a: (a,), [w[n][0]], [BF16])[0] for n in BIG}
    gath = {}
    for i, grp in enumerate(COMM_GROUPS):
        gath.update(zip(grp, _gather_weights(f"gather_weights_{i}", [shards[n] for n in grp])))
    w_p = _permute_cols(_cols_from_chips(gath["w_in"]), PERM)
    w_pa = _cols_from_chips(gath["w_proj_attn"])
    w_ps = _cols_from_chips(gath["w_proj_sgu"])
    w_o = gath["w_out"].reshape(D_MODEL, D_MODEL)

    dx, big, small = _local_step(
        x[0], positions.reshape(t, 1), loss_target[0], norm1_g, sgu_ln_g, sgu_ln_b, w_spatial[0], b_spatial[0], norm2_g,
        final_g.reshape(1, D_MODEL), w_p, w_pa, w_ps, w_o, gath["w_ffn_gate"], gath["w_ffn_up"], gath["w_ffn_down"])

    big["w_in"] = _permute_cols(big["w_in"], INV_PERM)
    big["w_out"] = big["w_out"].reshape(N_CHIPS, D_MODEL // N_CHIPS, D_MODEL)
    grads = {}
    for i, grp in enumerate(COMM_GROUPS):
        sums = _pair_reduce(f"rs_pair_reduce_{i}", [big[n] for n in grp])
        grads.update(zip(grp, _chip_reduce(f"rs_chip_reduce_{i}", sums)))

    delta, new_m, new_v = {}, {}, {}
    for n in BIG:
        shp = w[n].shape
        g_, d_, m_, v_ = _adamw(f"adamw_{n}", grads[n], w[n][0], m[n][0], v[n][0])
        grads[n], delta[n], new_m[n], new_v[n] = g_.reshape(shp), d_.reshape(shp), m_.reshape(shp), v_.reshape(shp)

    g_s, d_s, m_s, v_s = _small_step(small, w, m, v)
    loss = g_s["loss"][0, 0]
    for n in SMALL_PARAMS:
        shp = w[n].shape
        grads[n], delta[n], new_m[n], new_v[n] = (a[n].reshape(shp) for a in (g_s, d_s, m_s, v_s))

    return (loss, dx.reshape(x.shape), *[grads[n] for n in WEIGHTS], *[delta[n] for n in WEIGHTS],
            *[new_m[n] for n in WEIGHTS], *[new_v[n] for n in WEIGHTS])
```

```python
import functools

import numpy as np
import jax
import jax.numpy as jnp
from jax import lax
from jax.experimental import pallas as pl
from jax.experimental.pallas import tpu as pltpu

F32, BF16 = jnp.float32, jnp.bfloat16
MESH = pl.DeviceIdType.MESH

D_MODEL = 1024
HEAD_DIM = 64
ATTN_W = 512
DILATIONS = (1, 4, 16)
BLK = 128
ROPE_DIM = 16
ROPE_THETA = 500000.0
SGU_W = 512
SGU_CHUNK = 128
SGU_GROUPS = 8
D_FF = 2816
N_CHIPS = 4
FF_SHARD = D_FF // N_CHIPS
IN_COLS = 7680
EPS = 1e-6
NEG = -1e30
LANES = 128
VMEM_LIMIT = 52 * 1024 * 1024

ADAM_LR, ADAM_B1, ADAM_B2, ADAM_EPS, ADAM_WD, ADAM_STEP = 0.001, 0.9, 0.999, 1e-08, 0.01, 10

PERM = (11, 12, 13, 14, 9, 10, 0, 3, 6, 1, 4, 7, 2, 5, 8)
INV_PERM = tuple(int(i) for i in np.argsort(np.array(PERM)))


def _cparams(ngrid):
    return pltpu.CompilerParams(dimension_semantics=("arbitrary",) * ngrid, vmem_limit_bytes=VMEM_LIMIT)


def _full(shape):
    return pl.BlockSpec(shape, lambda *_: (0,) * len(shape))


def _resident(shape):
    return pl.BlockSpec(shape, lambda *_: (0,) * len(shape), pipeline_mode=pl.Buffered(1))


NN = ((1,), (0,))
NT = ((1,), (1,))
TN = ((0,), (0,))


def _mm(name, grid, pairs, dims, acc_shape, epi, *, extras=(), outs=(), reds=(), aliases=None):
    nk = grid[-1]
    npair, nex, nout, nred = len(pairs), len(extras), len(outs), len(reds)

    def body(*refs):
        a_refs = refs[:npair]
        b_refs = refs[npair:2 * npair]
        p0 = 2 * npair
        e_refs = refs[p0:p0 + nex]
        o_refs = refs[p0 + nex:p0 + nex + nout]
        r_refs = refs[p0 + nex + nout:p0 + nex + nout + nred]
        ids = [pl.program_id(a) for a in range(len(grid))]
        k = ids[-1]
        if nred:
            first = ids[0] == 0
            for v in ids[1:]:
                first = first & (v == 0)

            @pl.when(first)
            def _():
                for r in r_refs:
                    r[...] = jnp.zeros(r.shape, r.dtype)

        part = None
        for a_ref, b_ref in zip(a_refs, b_refs):
            d = lax.dot_general(a_ref[...], b_ref[...], (dims, ((), ())), preferred_element_type=F32)
            part = d if part is None else part + d
        if nk == 1:
            epi(part, e_refs, o_refs, r_refs, ids)
        else:
            acc_ref = refs[-1]

            @pl.when(k == 0)
            def _():
                acc_ref[...] = part

            @pl.when(k > 0)
            def _():
                acc_ref[...] += part

            @pl.when(k == nk - 1)
            def _():
                epi(acc_ref[...], e_refs, o_refs, r_refs, ids)

    in_specs = [p[1] for p in pairs] + [p[3] for p in pairs] + [e[1] for e in extras]
    args = [p[0] for p in pairs] + [p[2] for p in pairs] + [e[0] for e in extras]
    out_shape = [jax.ShapeDtypeStruct(o[0], o[1]) for o in outs] + [jax.ShapeDtypeStruct(r, F32) for r in reds]
    out_specs = [o[2] for o in outs] + [_full(r) for r in reds]
    scratch_shapes = [pltpu.VMEM(acc_shape, F32)] if nk > 1 else []
    return pl.pallas_call(
        body, grid=grid, in_specs=in_specs, out_specs=out_specs, out_shape=out_shape, scratch_shapes=scratch_shapes,
        input_output_aliases=aliases or {}, compiler_params=_cparams(len(grid)), name=name)(*args)


def _rope(v, cos_t, sin_t):
    half = ROPE_DIM // 2
    first = (lax.broadcasted_iota(jnp.int32, cos_t.shape, 1) % HEAD_DIM) < half
    outs = []
    for cs in range(v.shape[1] // LANES):
        x = v[:, cs * LANES:(cs + 1) * LANES]
        partner = jnp.where(first, pltpu.roll(x, LANES - half, axis=1), pltpu.roll(x, half, axis=1))
        outs.append(x * cos_t + partner * sin_t)
    return outs[0] if len(outs) == 1 else jnp.concatenate(outs, axis=1)


def _spread_heads(v2, upper):
    other = pltpu.roll(v2, HEAD_DIM, axis=1)
    h0 = jnp.where(upper, other, v2)
    h1 = jnp.where(upper, v2, other)
    return jnp.concatenate([jnp.concatenate([h0, h0], axis=1), jnp.concatenate([h1, h1], axis=1)], axis=0)


def _sigmoid(v):
    return 1.0 / (1.0 + jnp.exp(-v))


def _rms_stats(v):
    r = lax.rsqrt(jnp.mean(v * v, axis=-1, keepdims=True) + EPS)
    return v * r, r


def _rms_bwd(dy, xhat, r, g):
    dxh = dy * g
    return r * (dxh - xhat * jnp.mean(dxh * xhat, axis=-1, keepdims=True))


def _head_sum_matrix():
    idx = np.arange(ATTN_W) // HEAD_DIM
    return jnp.asarray((idx[:, None] == idx[None, :]).astype(np.float32), dtype=BF16)


def _group_sum(v, e):
    hi = v.astype(BF16)
    lo = (v - hi.astype(F32)).astype(BF16)
    return jnp.dot(hi, e, preferred_element_type=F32) + jnp.dot(lo, e, preferred_element_type=F32)


TILE = 512


def _to_slabs(slab_ref, v):
    for cs in range(slab_ref.shape[0]):
        slab_ref[cs] = v[:, cs * LANES:(cs + 1) * LANES]


def _from_slabs(slab_ref):
    return jnp.concatenate([slab_ref[cs] for cs in range(slab_ref.shape[0])], axis=1)


def _class_rows(slab_ref, r, dil):
    n = slab_ref.shape[1] // dil
    return jnp.concatenate([slab_ref.at[cs][pl.ds(r, n, stride=dil), :] for cs in range(slab_ref.shape[0])], axis=1)


def _put_class_rows(slab_ref, r, dil, v):
    n = slab_ref.shape[1] // dil
    for cs in range(slab_ref.shape[0]):
        slab_ref.at[cs][pl.ds(r, n, stride=dil), :] = v[:, cs * LANES:(cs + 1) * LANES]


def _natural_from_group(slab_ref, grp_ref):
    dil = grp_ref.shape[0]
    for r in range(dil):
        _put_class_rows(slab_ref, r, dil, grp_ref[r].astype(F32))
    return _from_slabs(slab_ref)


def _group_from_natural(slab_ref, grp_ref, v):
    dil = grp_ref.shape[0]
    _to_slabs(slab_ref, v)
    for r in range(dil):
        grp_ref[r] = _class_rows(slab_ref, r, dil).astype(grp_ref.dtype)


def _group_spec(dil, tile, width):
    return pl.BlockSpec((dil, tile // dil, width), lambda i, *_: (0, i, 0))


def _slabs(tile, width):
    return pltpu.VMEM((width // LANES, tile, LANES), F32)


def _rope_consts():
    lane = np.arange(LANES) % HEAD_DIM
    fi = lane % (ROPE_DIM // 2)
    invf = np.where(lane < ROPE_DIM, ROPE_THETA ** (-(2.0 * fi) / ROPE_DIM), 0.0)
    sgn = np.where(lane < ROPE_DIM // 2, -1.0, np.where(lane < ROPE_DIM, 1.0, 0.0))
    return (jnp.asarray(invf.astype(np.float32)).reshape(1, LANES), jnp.asarray(sgn.astype(np.float32)).reshape(1, LANES))


def _rope_tables(pos_col):
    t = pos_col.shape[0]
    tile = min(t, TILE)
    invf, sgn = _rope_consts()

    def body(p_ref, f_ref, s_ref, c0, s0, c1, s1, c2, s2, slab_c, slab_s):
        ang = p_ref[...].astype(F32) * f_ref[...]
        cos, sin = jnp.cos(ang), jnp.sin(ang) * s_ref[...]
        c0[...] = cos
        s0[...] = sin
        _group_from_natural(slab_c, c1, cos)
        _group_from_natural(slab_s, s1, sin)
        for r in range(DILATIONS[2]):
            c2[r] = _class_rows(slab_c, r, DILATIONS[2])
            s2[r] = _class_rows(slab_s, r, DILATIONS[2])

    nat = pl.BlockSpec((tile, LANES), lambda i: (i, 0))
    specs, shapes = [nat, nat], [(t, LANES)] * 2
    for d in DILATIONS[1:]:
        specs += [_group_spec(d, tile, LANES)] * 2
        shapes += [(d, t // d, LANES)] * 2
    outs = pl.pallas_call(
        body, grid=(t // tile,),
        in_specs=[pl.BlockSpec((tile, 1), lambda i: (i, 0)), _full((1, LANES)), _full((1, LANES))],
        out_specs=specs, out_shape=[jax.ShapeDtypeStruct(s, F32) for s in shapes],
        scratch_shapes=[_slabs(tile, LANES)] * 2,
        compiler_params=_cparams(1), name="rope_tables")(pos_col, invf, sgn)
    return [(outs[2 * g].reshape(t, LANES), outs[2 * g + 1].reshape(t, LANES)) for g in range(len(DILATIONS))]


def _norm_fwd(x, g):
    t = x.shape[0]
    tile = min(t, TILE)

    def body(x_ref, g_ref, h0_ref, h1_ref, h2_ref, slab):
        xhat, _ = _rms_stats(x_ref[...])
        hn = xhat * g_ref[...]
        h0_ref[...] = hn.astype(BF16)
        _group_from_natural(slab, h1_ref, hn)
        for r in range(DILATIONS[2]):
            h2_ref[r] = _class_rows(slab, r, DILATIONS[2]).astype(BF16)

    nat = pl.BlockSpec((tile, D_MODEL), lambda i: (i, 0))
    return pl.pallas_call(
        body, grid=(t // tile,),
        in_specs=[nat, _full((1, D_MODEL))],
        out_specs=[nat] + [_group_spec(d, tile, D_MODEL) for d in DILATIONS[1:]],
        out_shape=[jax.ShapeDtypeStruct((t, D_MODEL), BF16)]
        + [jax.ShapeDtypeStruct((d, t // d, D_MODEL), BF16) for d in DILATIONS[1:]],
        scratch_shapes=[_slabs(tile, D_MODEL)],
        compiler_params=_cparams(1), name="norm1_fwd")(x, g)


GU_COLS = 3072
GROUP_COLS = 1536


def _in_proj(hs, w_p, tables):
    t = hs[0].shape[0]
    tm = min(t, 1024)

    def body_gu(h_ref, w_ref, o_ref):
        o_ref[...] = jnp.dot(h_ref[...], w_ref[...], preferred_element_type=F32).astype(BF16)

    gu = _token_call("in_proj_gates_uv", body_gu, t, tm,
                     [(hs[0], _rows_spec(tm, D_MODEL)),
                      (w_p, pl.BlockSpec((D_MODEL, GU_COLS), lambda i: (0, 0), pipeline_mode=pl.Buffered(1)))],
                     [((t, GU_COLS), BF16, _rows_spec(tm, GU_COLS))])[0]

    qkvs = []
    for g in range(len(DILATIONS)):

        def body_qkv(h_ref, w_ref, cos_ref, sin_ref, o_ref):
            acc = jnp.dot(h_ref[...], w_ref[...], preferred_element_type=F32)
            cos_w, sin_w = cos_ref[...], sin_ref[...]
            o_ref[:, 0:ATTN_W] = (_rope(acc[:, 0:ATTN_W], cos_w, sin_w) * HEAD_DIM ** -0.5).astype(BF16)
            o_ref[:, ATTN_W:2 * ATTN_W] = _rope(acc[:, ATTN_W:2 * ATTN_W], cos_w, sin_w).astype(BF16)
            o_ref[:, 2 * ATTN_W:] = acc[:, 2 * ATTN_W:].astype(BF16)

        col = (GU_COLS + g * GROUP_COLS) // GROUP_COLS
        cos_t, sin_t = tables[g]
        qkvs.append(_token_call(
            f"in_proj_qkv_g{g}", body_qkv, t, tm,
            [(hs[g].reshape(t, D_MODEL), _rows_spec(tm, D_MODEL)),
             (w_p, pl.BlockSpec((D_MODEL, GROUP_COLS), lambda i, col=col: (0, col), pipeline_mode=pl.Buffered(1))),
             (cos_t, _rows_spec(tm, LANES)), (sin_t, _rows_spec(tm, LANES))],
            [((t, GROUP_COLS), BF16, _rows_spec(tm, GROUP_COLS))])[0])
    return gu, qkvs


def _attn_masks(n):
    row = lax.broadcasted_iota(jnp.int32, (2 * BLK, 2 * BLK), 0) % BLK
    col = lax.broadcasted_iota(jnp.int32, (2 * BLK, 2 * BLK), 1)
    diff = BLK + row - col
    valid = (diff >= 0) & (diff <= BLK) & ((col >= BLK) | (n > 0))
    upper = lax.broadcasted_iota(jnp.int32, (BLK, LANES), 1) >= HEAD_DIM
    return valid, upper


def _stack_heads(v2, upper):
    zero = jnp.zeros_like(v2)
    return jnp.concatenate([jnp.where(upper, zero, v2), jnp.where(upper, v2, zero)], axis=0)


def _unstack_heads(v, upper):
    return jnp.where(upper, v[BLK:], v[:BLK])


def _attn_fwd(qkv, g, dil):
    t = qkv.shape[0]
    length = t // dil
    nb = length // BLK
    view = qkv.reshape(dil, length, GROUP_COLS)

    def body(q_ref, kc_ref, kp_ref, vc_ref, vp_ref, o_ref, l_ref):
        n = pl.program_id(1)
        valid, upper = _attn_masks(n)
        for p in range(ATTN_W // LANES):
            sl = slice(p * LANES, (p + 1) * LANES)
            qs = _stack_heads(q_ref[:, sl], upper)
            k2 = jnp.concatenate([kp_ref[:, sl], kc_ref[:, sl]], axis=0)
            v2 = jnp.concatenate([vp_ref[:, sl], vc_ref[:, sl]], axis=0)
            s = lax.dot_general(qs, k2, (NT, ((), ())), preferred_element_type=F32)
            s = jnp.where(valid, s, NEG)
            m = jnp.max(s, axis=1, keepdims=True)
            pe = jnp.exp(s - m)
            den = jnp.sum(pe, axis=1, keepdims=True)
            o = jnp.dot(pe.astype(BF16), v2, preferred_element_type=F32) / den
            lse = jnp.broadcast_to(m + jnp.log(den), (2 * BLK, LANES))
            o_ref[:, sl] = _unstack_heads(o, upper)
            l_ref[:, sl] = _unstack_heads(lse, upper)

    cur = lambda part: pl.BlockSpec((None, BLK, ATTN_W), lambda r, n: (r, n, part))
    prev = lambda part: pl.BlockSpec((None, BLK, ATTN_W), lambda r, n: (r, jnp.maximum(n - 1, 0), part))
    out_spec = pl.BlockSpec((None, BLK, ATTN_W), lambda r, n: (r, n, 0))
    return pl.pallas_call(
        body, grid=(dil, nb),
        in_specs=[cur(0), cur(1), prev(1), cur(2), prev(2)],
        out_specs=[out_spec, out_spec],
        out_shape=[jax.ShapeDtypeStruct((dil, length, ATTN_W), F32)] * 2,
        compiler_params=_cparams(2), name=f"attn_fwd_g{g}")(view, view, view, view, view)


def _alphas(l0, l1, l2):
    m = jnp.maximum(jnp.maximum(l0, l1), l2)
    e0, e1, e2 = jnp.exp(l0 - m), jnp.exp(l1 - m), jnp.exp(l2 - m)
    inv = 1.0 / (e0 + e1 + e2)
    return e0 * inv, e1 * inv, e2 * inv


def _natural_group_values(o_refs, l_refs, slabs):
    os_ = [o_refs[0][0]] + [_natural_from_group(slabs[2 * g - 2], o_refs[g]) for g in (1, 2)]
    ls_ = [l_refs[0][0]] + [_natural_from_group(slabs[2 * g - 1], l_refs[g]) for g in (1, 2)]
    return os_, ls_


def _combine_fwd(os_, ls_):
    t = os_[0].shape[1]
    tile = min(t, TILE)

    def body(o0, o1, o2, l0, l1, l2, a_ref, *slabs):
        ov, lv = _natural_group_values((o0, o1, o2), (l0, l1, l2), slabs)
        a0, a1, a2 = _alphas(*lv)
        a_ref[...] = (a0 * ov[0] + a1 * ov[1] + a2 * ov[2]).astype(BF16)

    specs = [_group_spec(d, tile, ATTN_W) for d in DILATIONS]
    return pl.pallas_call(
        body, grid=(t // tile,), in_specs=specs * 2, out_specs=pl.BlockSpec((tile, ATTN_W), lambda i: (i, 0)),
        out_shape=jax.ShapeDtypeStruct((t, ATTN_W), BF16),
        scratch_shapes=[_slabs(tile, ATTN_W)] * 4,
        compiler_params=_cparams(1), name="combine_fwd")(*os_, *ls_)


def _combine_bwd(dattn, os_, ls_):
    t = dattn.shape[0]
    tile = min(t, TILE)
    e = _head_sum_matrix()

    def body(d_ref, o0, o1, o2, l0, l1, l2, e_ref, do0, do1, do2, c0, c1, c2, *slabs):
        ov, lv = _natural_group_values((o0, o1, o2), (l0, l1, l2), slabs)
        alphas = _alphas(*lv)
        d = d_ref[...]
        attn = alphas[0] * ov[0] + alphas[1] * ov[1] + alphas[2] * ov[2]
        s = _group_sum(d * attn, e_ref[...])
        do0[0] = (alphas[0] * d).astype(BF16)
        c0[0] = -alphas[0] * s
        for g, do_ref, c_ref in ((1, do1, c1), (2, do2, c2)):
            _group_from_natural(slabs[2 * g - 2], do_ref, alphas[g] * d)
            _group_from_natural(slabs[2 * g - 1], c_ref, -alphas[g] * s)

    specs = [_group_spec(d, tile, ATTN_W) for d in DILATIONS]
    shapes = [(d, t // d, ATTN_W) for d in DILATIONS]
    outs = pl.pallas_call(
        body, grid=(t // tile,),
        in_specs=[pl.BlockSpec((tile, ATTN_W), lambda i: (i, 0))] + specs * 2 + [_full((ATTN_W, ATTN_W))],
        out_specs=specs * 2,
        out_shape=[jax.ShapeDtypeStruct(s, BF16) for s in shapes] + [jax.ShapeDtypeStruct(s, F32) for s in shapes],
        scratch_shapes=[_slabs(tile, ATTN_W)] * 4,
        compiler_params=_cparams(1), name="combine_bwd")(dattn, *os_, *ls_, e)
    return outs[:3], outs[3:]


def _attn_bwd(qkv, do, cc, lse, cos_t, sin_t, g, dil):
    t = qkv.shape[0]
    length = t // dil
    nb = length // BLK
    qkv_v = qkv.reshape(dil, length, GROUP_COLS)
    cos_v, sin_v = (a.reshape(dil, length, LANES) for a in (cos_t, sin_t))
    scale = HEAD_DIM ** -0.5

    def body(q_ref, kc_ref, kp_ref, vc_ref, vp_ref, do_ref, c_ref, l_ref, cosc, sinc, cosp, sinp,
             out_ref, dq_s, dk_s, dv_s):
        n = pl.program_id(1)
        valid, upper = _attn_masks(n)

        @pl.when(n < nb)
        def _():
            cos_c, sin_c = cosc[...], sinc[...]
            cos_p, sin_p = cosp[...], sinp[...]
            dq_parts, dkp_parts, dkc_parts, dvp_parts, dvc_parts = [], [], [], [], []
            for p in range(ATTN_W // LANES):
                sl = slice(p * LANES, (p + 1) * LANES)
                qs = _stack_heads(q_ref[:, sl], upper)
                dos = _stack_heads(do_ref[:, sl], upper)
                k2 = jnp.concatenate([kp_ref[:, sl], kc_ref[:, sl]], axis=0)
                v2 = jnp.concatenate([vp_ref[:, sl], vc_ref[:, sl]], axis=0)
                l_col = _spread_heads(l_ref[:, sl], upper)
                c_col = _spread_heads(c_ref[:, sl], upper)
                s = lax.dot_general(qs, k2, (NT, ((), ())), preferred_element_type=F32)
                pe = jnp.exp(jnp.where(valid, s, NEG) - l_col)
                dpv = lax.dot_general(dos, v2, (NT, ((), ())), preferred_element_type=F32)
                ds = (pe * (dpv + c_col)).astype(BF16)
                dq2 = _unstack_heads(jnp.dot(ds, k2, preferred_element_type=F32), upper)
                dk2 = lax.dot_general(ds, qs, (TN, ((), ())), preferred_element_type=F32)
                dv2 = lax.dot_general(pe.astype(BF16), dos, (TN, ((), ())), preferred_element_type=F32)
                dq_parts.append(dq2)
                dkp_parts.append(dk2[:BLK])
                dkc_parts.append(dk2[BLK:])
                dvp_parts.append(dv2[:BLK])
                dvc_parts.append(dv2[BLK:])
            dq = _rope(jnp.concatenate(dq_parts, axis=1) * scale, cos_c, -sin_c)
            dkc = _rope(jnp.concatenate(dkc_parts, axis=1), cos_c, -sin_c)
            dkp = _rope(jnp.concatenate(dkp_parts, axis=1), cos_p, -sin_p)
            dvp = jnp.concatenate(dvp_parts, axis=1)
            dvc = jnp.concatenate(dvc_parts, axis=1)

            @pl.when(n > 0)
            def _():
                out_ref[:, 0:ATTN_W] = dq_s[...].astype(BF16)
                out_ref[:, ATTN_W:2 * ATTN_W] = (dk_s[...] + dkp).astype(BF16)
                out_ref[:, 2 * ATTN_W:3 * ATTN_W] = (dv_s[...] + dvp).astype(BF16)

            dq_s[...] = dq
            dk_s[...] = dkc
            dv_s[...] = dvc

        @pl.when(n == nb)
        def _():
            out_ref[:, 0:ATTN_W] = dq_s[...].astype(BF16)
            out_ref[:, ATTN_W:2 * ATTN_W] = dk_s[...].astype(BF16)
            out_ref[:, 2 * ATTN_W:3 * ATTN_W] = dv_s[...].astype(BF16)

    nc = lambda n: jnp.minimum(n, nb - 1)
    npv = lambda n: jnp.maximum(jnp.minimum(n, nb - 1) - 1, 0)
    cur = lambda part: pl.BlockSpec((None, BLK, ATTN_W), lambda r, n: (r, nc(n), part))
    prev = lambda part: pl.BlockSpec((None, BLK, ATTN_W), lambda r, n: (r, npv(n), part))
    row = pl.BlockSpec((None, BLK, ATTN_W), lambda r, n: (r, nc(n), 0))
    tab_c = pl.BlockSpec((None, BLK, LANES), lambda r, n: (r, nc(n), 0))
    tab_p = pl.BlockSpec((None, BLK, LANES), lambda r, n: (r, npv(n), 0))
    out_spec = pl.BlockSpec((None, BLK, GROUP_COLS), lambda r, n: (r, jnp.maximum(n - 1, 0), 0))
    out = pl.pallas_call(
        body, grid=(dil, nb + 1),
        in_specs=[cur(0), cur(1), prev(1), cur(2), prev(2), row, row, row, tab_c, tab_c, tab_p, tab_p],
        out_specs=out_spec,
        out_shape=jax.ShapeDtypeStruct((dil, length, GROUP_COLS), BF16),
        scratch_shapes=[pltpu.VMEM((BLK, ATTN_W), F32)] * 3,
        compiler_params=_cparams(2), name=f"attn_bwd_g{g}")(
            qkv_v, qkv_v, qkv_v, qkv_v, qkv_v, do, cc, lse, cos_v, sin_v, cos_v, sin_v)
    return out.reshape(t, GROUP_COLS)


SQRT_HALF = 0.7071067811865476
INV_SQRT_2PI = 0.3989422804014327


def _sgu_core(uv, g, b, w_ref, bias):
    cdf = 0.5 * (1.0 + lax.erf(uv * SQRT_HALF))
    z = uv * cdf
    u, v = z[:, :SGU_W], z[:, SGU_W:]
    mu = jnp.mean(v, axis=1, keepdims=True)
    xc = v - mu
    rs = lax.rsqrt(jnp.mean(xc * xc, axis=1, keepdims=True) + EPS)
    xhat = xc * rs
    vn = xhat * g + b
    row = lax.broadcasted_iota(jnp.int32, (SGU_CHUNK, SGU_CHUNK), 0)
    col = lax.broadcasted_iota(jnp.int32, (SGU_CHUNK, SGU_CHUNK), 1)
    tril = row >= col
    upper = lax.broadcasted_iota(jnp.int32, (SGU_CHUNK, LANES), 1) >= SGU_W // SGU_GROUPS
    ws, vlo, vhi, mixed = [], [], [], []
    for pr in range(SGU_W // LANES):
        sl = slice(pr * LANES, (pr + 1) * LANES)
        w0 = jnp.where(tril, w_ref[2 * pr], 0.0).astype(BF16)
        w1 = jnp.where(tril, w_ref[2 * pr + 1], 0.0).astype(BF16)
        vn2 = vn[:, sl]
        lo = jnp.where(upper, 0.0, vn2).astype(BF16)
        hi = jnp.where(upper, vn2, 0.0).astype(BF16)
        mixed.append(jnp.dot(w0, lo, preferred_element_type=F32) + jnp.dot(w1, hi, preferred_element_type=F32)
                     + bias[:, sl])
        ws.append((w0, w1))
        vlo.append(lo)
        vhi.append(hi)
    return cdf, u, xhat, rs, jnp.concatenate(mixed, axis=1), ws, vlo, vhi, tril, upper


def _sgu_fwd(gu, ln_g, ln_b, w_s, bias_exp):
    t = gu.shape[0]

    def body(uv_ref, g_ref, b_ref, w_ref, bias_ref, o_ref):
        _, u, _, _, mixed, *_ = _sgu_core(uv_ref[...].astype(F32), g_ref[...], b_ref[...], w_ref, bias_ref[...])
        o_ref[...] = (u * mixed).astype(BF16)

    return pl.pallas_call(
        body, grid=(t // SGU_CHUNK,),
        in_specs=[pl.BlockSpec((SGU_CHUNK, 2 * SGU_W), lambda n: (n, 2)), _full((1, SGU_W)), _full((1, SGU_W)),
                  _full((SGU_GROUPS, SGU_CHUNK, SGU_CHUNK)), _full((SGU_CHUNK, SGU_W))],
        out_specs=pl.BlockSpec((SGU_CHUNK, SGU_W), lambda n: (n, 0)),
        out_shape=jax.ShapeDtypeStruct((t, SGU_W), BF16),
        compiler_params=_cparams(1), name="sgu_fwd")(gu, ln_g, ln_b, w_s, bias_exp)


def _sgu_bwd(dproj, gu, dsgu, ln_g, ln_b, w_s, bias_exp):
    t = gu.shape[0]
    nchunks = t // SGU_CHUNK
    e = _head_sum_matrix()

    def body(dp_in, uv_ref, ds_ref, g_ref, b_ref, w_ref, bias_ref, e_ref, out_ref, dw_ref, dbias_ref, dg_ref, db_ref):
        n = pl.program_id(0)

        @pl.when(n == 0)
        def _():
            dw_ref[...] = jnp.zeros(dw_ref.shape, F32)
            dbias_ref[...] = jnp.zeros(dbias_ref.shape, F32)
            dg_ref[...] = jnp.zeros(dg_ref.shape, F32)
            db_ref[...] = jnp.zeros(db_ref.shape, F32)

        uv = uv_ref[...].astype(F32)
        g = g_ref[...]
        cdf, u, xhat, rs, mixed, ws, vlo, vhi, tril, upper = _sgu_core(uv, g, b_ref[...], w_ref, bias_ref[...])
        dsg = ds_ref[...]
        du = dsg * mixed
        dmixed = dsg * u
        dbias_ref[...] += dmixed
        dvn = []
        for pr in range(SGU_W // LANES):
            sl = slice(pr * LANES, (pr + 1) * LANES)
            dm2 = dmixed[:, sl]
            dlo = jnp.where(upper, 0.0, dm2).astype(BF16)
            dhi = jnp.where(upper, dm2, 0.0).astype(BF16)
            w0, w1 = ws[pr]
            dvn.append(lax.dot_general(w0, dlo, (TN, ((), ())), preferred_element_type=F32)
                       + lax.dot_general(w1, dhi, (TN, ((), ())), preferred_element_type=F32))
            dw0 = lax.dot_general(dlo, vlo[pr], (NT, ((), ())), preferred_element_type=F32)
            dw1 = lax.dot_general(dhi, vhi[pr], (NT, ((), ())), preferred_element_type=F32)
            dw_ref[2 * pr] += jnp.where(tril, dw0, 0.0)
            dw_ref[2 * pr + 1] += jnp.where(tril, dw1, 0.0)
        dvn = jnp.concatenate(dvn, axis=1)
        dg_ref[...] += jnp.sum(dvn * xhat, axis=0, keepdims=True)
        db_ref[...] += jnp.sum(dvn, axis=0, keepdims=True)
        dxh = dvn * g
        dv = rs * (dxh - jnp.mean(dxh, axis=1, keepdims=True) - xhat * jnp.mean(dxh * xhat, axis=1, keepdims=True))
        dz = jnp.concatenate([du, dv], axis=1)
        dgelu = cdf + uv * (INV_SQRT_2PI * jnp.exp(-0.5 * uv * uv))
        out_ref[...] = (dz * dgelu).astype(BF16)

        @pl.when(n == nchunks - 1)
        def _():
            dbias_ref[...] = _group_sum(dbias_ref[...], e_ref[...])

    outs = pl.pallas_call(
        body, grid=(nchunks,),
        in_specs=[pl.BlockSpec(memory_space=pl.ANY), pl.BlockSpec((SGU_CHUNK, 2 * SGU_W), lambda n: (n, 2)),
                  pl.BlockSpec((SGU_CHUNK, SGU_W), lambda n: (n, 0)), _full((1, SGU_W)), _full((1, SGU_W)),
                  _full((SGU_GROUPS, SGU_CHUNK, SGU_CHUNK)), _full((SGU_CHUNK, SGU_W)), _full((ATTN_W, ATTN_W))],
        out_specs=[pl.BlockSpec((SGU_CHUNK, 2 * SGU_W), lambda n: (n, 2)), _full((SGU_GROUPS, SGU_CHUNK, SGU_CHUNK)),
                   _full((SGU_CHUNK, SGU_W)), _full((1, SGU_W)), _full((1, SGU_W))],
        out_shape=[jax.ShapeDtypeStruct(dproj.shape, BF16), jax.ShapeDtypeStruct((SGU_GROUPS, SGU_CHUNK, SGU_CHUNK), F32),
                   jax.ShapeDtypeStruct((SGU_CHUNK, SGU_W), F32), jax.ShapeDtypeStruct((1, SGU_W), F32),
                   jax.ShapeDtypeStruct((1, SGU_W), F32)],
        input_output_aliases={0: 0},
        compiler_params=_cparams(1), name="sgu_bwd")(dproj, gu, dsgu, ln_g, ln_b, w_s, bias_exp, e)
    return outs


def _merge_fwd(attn, sgu, gu, x, w_pa, w_ps, w_out, g2):
    t = x.shape[0]
    tm = min(t, 256)

    def body(a_ref, s_ref, ga_ref, gb_ref, x_ref, wpa, wps, wo, g_ref, pa_ref, ps_ref, m_ref, x1_ref, h2_ref):
        pa = jnp.dot(a_ref[...], wpa[...], preferred_element_type=F32)
        ps = jnp.dot(s_ref[...], wps[...], preferred_element_type=F32)
        merged = (_sigmoid(ga_ref[...].astype(F32)) * pa + _sigmoid(gb_ref[...].astype(F32)) * ps).astype(BF16)
        x1 = x_ref[...] + jnp.dot(merged, wo[...], preferred_element_type=F32)
        xhat, _ = _rms_stats(x1)
        pa_ref[...] = pa.astype(BF16)
        ps_ref[...] = ps.astype(BF16)
        m_ref[...] = merged
        x1_ref[...] = x1
        h2_ref[...] = (xhat * g_ref[...]).astype(BF16)

    half = pl.BlockSpec((tm, ATTN_W), lambda i: (i, 0))
    full = pl.BlockSpec((tm, D_MODEL), lambda i: (i, 0))
    return pl.pallas_call(
        body, grid=(t // tm,),
        in_specs=[half, half, pl.BlockSpec((tm, D_MODEL), lambda i: (i, 0)), pl.BlockSpec((tm, D_MODEL), lambda i: (i, 1)),
                  full, _full((ATTN_W, D_MODEL)), _full((SGU_W, D_MODEL)), _full((D_MODEL, D_MODEL)), _full((1, D_MODEL))],
        out_specs=[full] * 5,
        out_shape=[jax.ShapeDtypeStruct((t, D_MODEL), BF16), jax.ShapeDtypeStruct((t, D_MODEL), BF16),
                   jax.ShapeDtypeStruct((t, D_MODEL), BF16), jax.ShapeDtypeStruct((t, D_MODEL), F32),
                   jax.ShapeDtypeStruct((t, D_MODEL), BF16)],
        compiler_params=_cparams(1), name="merge_fwd")(attn, sgu, gu, gu, x, w_pa, w_ps, w_out, g2)


def _merge_bwd(dx1b, gu, pa, ps, w_pa, w_ps, w_out):
    t = dx1b.shape[0]
    tm = min(t, 256)

    def body(d_ref, ga_ref, gb_ref, pa_ref, ps_ref, wpa, wps, wo, out_ref, dpa_ref, dps_ref, da_ref, dsg_ref):
        dm = lax.dot_general(d_ref[...], wo[...], (NT, ((), ())), preferred_element_type=F32)
        sa, sb = _sigmoid(ga_ref[...].astype(F32)), _sigmoid(gb_ref[...].astype(F32))
        dpa = (dm * sa).astype(BF16)
        dps = (dm * sb).astype(BF16)
        out_ref[:, 0:D_MODEL] = (dm * pa_ref[...].astype(F32) * sa * (1.0 - sa)).astype(BF16)
        out_ref[:, D_MODEL:2 * D_MODEL] = (dm * ps_ref[...].astype(F32) * sb * (1.0 - sb)).astype(BF16)
        out_ref[:, 2 * D_MODEL:GU_COLS] = jnp.zeros((tm, GU_COLS - 2 * D_MODEL), BF16)
        dpa_ref[...] = dpa
        dps_ref[...] = dps
        da_ref[...] = lax.dot_general(dpa, wpa[...], (NT, ((), ())), preferred_element_type=F32)
        dsg_ref[...] = lax.dot_general(dps, wps[...], (NT, ((), ())), preferred_element_type=F32)

    half = pl.BlockSpec((tm, ATTN_W), lambda i: (i, 0))
    full = pl.BlockSpec((tm, D_MODEL), lambda i: (i, 0))
    return pl.pallas_call(
        body, grid=(t // tm,),
        in_specs=[full, pl.BlockSpec((tm, D_MODEL), lambda i: (i, 0)),
                  pl.BlockSpec((tm, D_MODEL), lambda i: (i, 1)), full, full,
                  _full((ATTN_W, D_MODEL)), _full((SGU_W, D_MODEL)), _full((D_MODEL, D_MODEL))],
        out_specs=[pl.BlockSpec((tm, GU_COLS), lambda i: (i, 0)), full, full, half, half],
        out_shape=[jax.ShapeDtypeStruct((t, GU_COLS), BF16), jax.ShapeDtypeStruct((t, D_MODEL), BF16),
                   jax.ShapeDtypeStruct((t, D_MODEL), BF16), jax.ShapeDtypeStruct((t, ATTN_W), F32),
                   jax.ShapeDtypeStruct((t, SGU_W), F32)],
        compiler_params=_cparams(1), name="merge_bwd")(dx1b, gu, gu, pa, ps, w_pa, w_ps, w_out)


def _token_call(name, body, t, tm, ins, outs, reds=(), scratch=()):
    return pl.pallas_call(
        body, grid=(t // tm,), in_specs=[s for _, s in ins],
        out_specs=[o[2] for o in outs] + [_full(r) for r in reds],
        out_shape=[jax.ShapeDtypeStruct(o[0], o[1]) for o in outs] + [jax.ShapeDtypeStruct(r, F32) for r in reds],
        scratch_shapes=list(scratch), compiler_params=_cparams(1), name=name)(*[a for a, _ in ins])


def _rows_spec(tm, width):
    return pl.BlockSpec((tm, width), lambda i: (i, 0))


def _chips_spec(tm):
    return pl.BlockSpec((N_CHIPS, tm, FF_SHARD), lambda i: (0, i, 0))


def _zero_at_start(*refs):
    @pl.when(pl.program_id(0) == 0)
    def _():
        for r in refs:
            r[...] = jnp.zeros(r.shape, r.dtype)


def _ffn_fwd(h2, w_g, w_u):
    t = h2.shape[0]
    tm = min(t, 512)

    def body(h_ref, wg_ref, wu_ref, a_ref, b_ref, ff_ref):
        h = h_ref[...]
        for s in range(N_CHIPS):
            a = jnp.dot(h, wg_ref[s], preferred_element_type=F32)
            b = jnp.dot(h, wu_ref[s], preferred_element_type=F32)
            a_ref[s] = a.astype(BF16)
            b_ref[s] = b.astype(BF16)
            ff_ref[s] = (a * _sigmoid(a) * b).astype(BF16)

    shp = (N_CHIPS, t, FF_SHARD)
    w_spec = _resident((N_CHIPS, D_MODEL, FF_SHARD))
    return _token_call("ffn_fwd", body, t, tm, [(h2, _rows_spec(tm, D_MODEL)), (w_g, w_spec), (w_u, w_spec)],
                       [(shp, BF16, _chips_spec(tm))] * 3)


def _ffn_down_loss(ff, w_d, x1, tgt, gf):
    t = x1.shape[0]
    tm = min(t, 512)

    def body(ff_ref, wd_ref, x1_ref, tgt_ref, g_ref, dx2_ref, dx2b_ref, loss_ref, dgf_ref):
        _zero_at_start(loss_ref, dgf_ref)
        acc = jnp.dot(ff_ref[0], wd_ref[0], preferred_element_type=F32)
        for s in range(1, N_CHIPS):
            acc = acc + jnp.dot(ff_ref[s], wd_ref[s], preferred_element_type=F32)
        x2 = x1_ref[...] + acc
        g = g_ref[...]
        xhat, rr = _rms_stats(x2)
        diff = xhat * g - tgt_ref[...]
        rows = jnp.sum(diff * diff, axis=1, keepdims=True)
        loss_ref[...] += jnp.broadcast_to(jnp.sum(rows, axis=0, keepdims=True) * (0.5 / D_MODEL), (1, LANES))
        dy = diff * (1.0 / D_MODEL)
        dgf_ref[...] += jnp.sum(dy * xhat, axis=0, keepdims=True)
        dx2 = _rms_bwd(dy, xhat, rr, g)
        dx2_ref[...] = dx2
        dx2b_ref[...] = dx2.astype(BF16)

    row = _rows_spec(tm, D_MODEL)
    return _token_call("ffn_down_loss", body, t, tm,
                       [(ff, _chips_spec(tm)), (w_d, _resident((N_CHIPS, FF_SHARD, D_MODEL))), (x1, row), (tgt, row),
                        (gf, _full((1, D_MODEL)))],
                       [((t, D_MODEL), F32, row), ((t, D_MODEL), BF16, row)], reds=[(1, LANES), (1, D_MODEL)])


def _ffn_bwd_act(dx2b, w_d, a, b):
    t = dx2b.shape[0]
    tm = min(t, 512)

    def body(d_ref, wd_ref, a_ref, b_ref, da_ref, db_ref):
        d = d_ref[...]
        for s in range(N_CHIPS):
            dff = lax.dot_general(d, wd_ref[s], (NT, ((), ())), preferred_element_type=F32)
            av, bv = a_ref[s].astype(F32), b_ref[s].astype(F32)
            sg = _sigmoid(av)
            da_ref[s] = (dff * bv * (sg * (1.0 + av * (1.0 - sg)))).astype(BF16)
            db_ref[s] = (dff * (av * sg)).astype(BF16)

    shp = (N_CHIPS, t, FF_SHARD)
    return _token_call("ffn_bwd_act", body, t, tm,
                       [(dx2b, _rows_spec(tm, D_MODEL)), (w_d, _resident((N_CHIPS, FF_SHARD, D_MODEL))),
                        (a, _chips_spec(tm)), (b, _chips_spec(tm))],
                       [(shp, BF16, _chips_spec(tm))] * 2)


def _ffn_bwd_in(da, db, w_g, w_u, x1, dx2, g2):
    t = x1.shape[0]
    tm = min(t, 512)

    def body(da_ref, db_ref, wg_ref, wu_ref, x1_ref, dx2_ref, g_ref, dx1_ref, dx1b_ref, dg_ref):
        _zero_at_start(dg_ref)
        acc = None
        for s in range(N_CHIPS):
            part = (lax.dot_general(da_ref[s], wg_ref[s], (NT, ((), ())), preferred_element_type=F32)
                    + lax.dot_general(db_ref[s], wu_ref[s], (NT, ((), ())), preferred_element_type=F32))
            acc = part if acc is None else acc + part
        xhat, rr = _rms_stats(x1_ref[...])
        dg_ref[...] += jnp.sum(acc * xhat, axis=0, keepdims=True)
        dx1 = dx2_ref[...] + _rms_bwd(acc, xhat, rr, g_ref[...])
        dx1_ref[...] = dx1
        dx1b_ref[...] = dx1.astype(BF16)

    row = _rows_spec(tm, D_MODEL)
    w_spec = _resident((N_CHIPS, D_MODEL, FF_SHARD))
    return _token_call("ffn_bwd_in", body, t, tm,
                       [(da, _chips_spec(tm)), (db, _chips_spec(tm)), (w_g, w_spec), (w_u, w_spec), (x1, row), (dx2, row),
                        (g2, _full((1, D_MODEL)))],
                       [((t, D_MODEL), F32, row), ((t, D_MODEL), BF16, row)], reds=[(1, D_MODEL)])


def _in_proj_bwd(dgu, dqkvs, w_p, x, dx1, g1):
    t = x.shape[0]
    tile = min(t, TILE)
    tm = min(t, 1024)
    nat_cols = GU_COLS + GROUP_COLS

    dhs = []
    for g in (1, 2):
        col = (GU_COLS + g * GROUP_COLS) // GROUP_COLS

        def body_g(d_ref, w_ref, o_ref):
            o_ref[...] = lax.dot_general(d_ref[...], w_ref[...], (NT, ((), ())), preferred_element_type=F32)

        dh = _token_call(
            f"in_proj_bwd_g{g}", body_g, t, tm,
            [(dqkvs[g], _rows_spec(tm, GROUP_COLS)),
             (w_p, pl.BlockSpec((D_MODEL, GROUP_COLS), lambda i, col=col: (0, col), pipeline_mode=pl.Buffered(1)))],
            [((t, D_MODEL), F32, _rows_spec(tm, D_MODEL))])[0]
        dhs.append(dh.reshape(DILATIONS[g], t // DILATIONS[g], D_MODEL))

    def body(dgu_ref, dq0_ref, w_ref, x_ref, dx1_ref, g_ref, dh1_ref, dh2_ref, dx_ref, dg_ref, slab):
        _zero_at_start(dg_ref)
        dh = lax.dot_general(dgu_ref[...], w_ref[:, 0:GU_COLS], (NT, ((), ())), preferred_element_type=F32)
        dh = dh + lax.dot_general(dq0_ref[...], w_ref[:, GU_COLS:nat_cols], (NT, ((), ())), preferred_element_type=F32)
        dh = dh + _natural_from_group(slab, dh1_ref)
        dh = dh + _natural_from_group(slab, dh2_ref)
        xhat, rr = _rms_stats(x_ref[...])
        dg_ref[...] += jnp.sum(dh * xhat, axis=0, keepdims=True)
        dx_ref[...] = dx1_ref[...] + _rms_bwd(dh, xhat, rr, g_ref[...])

    row = _rows_spec(tile, D_MODEL)
    return _token_call(
        "in_proj_bwd", body, t, tile,
        [(dgu, _rows_spec(tile, GU_COLS)), (dqkvs[0], _rows_spec(tile, GROUP_COLS)),
         (w_p, pl.BlockSpec((D_MODEL, nat_cols), lambda i: (0, 0), pipeline_mode=pl.Buffered(1))),
         (x, row), (dx1, row), (g1, _full((1, D_MODEL))),
         (dhs[0], _group_spec(DILATIONS[1], tile, D_MODEL)), (dhs[1], _group_spec(DILATIONS[2], tile, D_MODEL))],
        [((t, D_MODEL), F32, row)], reds=[(1, D_MODEL)], scratch=[_slabs(tile, D_MODEL)])


def _epi_bf16(acc, e, o, r, ids):
    o[0][...] = acc.astype(BF16)


WGRAD_TK = 2048


def _wgrad_2d(name, a, b, tm, tn):
    t, k1 = a.shape
    n = b.shape[1]
    tk = min(t, WGRAD_TK)
    return _mm(name, (k1 // tm, n // tn, t // tk),
               [(a, pl.BlockSpec((tk, tm), lambda i, j, k: (k, i)), b, pl.BlockSpec((tk, tn), lambda i, j, k: (k, j)))],
               TN, (tm, tn), _epi_bf16, outs=[((k1, n), BF16, pl.BlockSpec((tm, tn), lambda i, j, k: (i, j)))])[0]


def _wgrad_in(hs, dgu, dqkvs):
    t = dgu.shape[0]
    tk = min(t, WGRAD_TK)
    dst = None
    parts = [(hs[0], dgu, 0)] + [(hs[g].reshape(t, D_MODEL), dqkvs[g], GU_COLS // GROUP_COLS + g) for g in range(3)]
    for n, (a, b, col0) in enumerate(parts):
        dst = _mm(f"wgrad_in_{n}", (1, b.shape[1] // GROUP_COLS, t // tk),
                  [(a, pl.BlockSpec((tk, D_MODEL), lambda i, j, k: (k, 0)), b,
                    pl.BlockSpec((tk, GROUP_COLS), lambda i, j, k: (k, j)))],
                  TN, (D_MODEL, GROUP_COLS), _epi_bf16,
                  extras=[] if dst is None else [(dst, pl.BlockSpec(memory_space=pl.ANY))],
                  outs=[((D_MODEL, IN_COLS), BF16, pl.BlockSpec((D_MODEL, GROUP_COLS), lambda i, j, k, col0=col0: (0, j + col0)))],
                  aliases=None if dst is None else {2: 0})[0]
    return dst


def _wgrad_ff_in(name, h2, da):
    t = h2.shape[0]
    tk = min(t, WGRAD_TK)
    return _mm(name, (N_CHIPS, 1, t // tk),
               [(h2, pl.BlockSpec((tk, D_MODEL), lambda i, j, k: (k, 0)),
                 da, pl.BlockSpec((None, tk, FF_SHARD), lambda i, j, k: (i, k, 0)))],
               TN, (D_MODEL, FF_SHARD), _epi_bf16,
               outs=[((N_CHIPS, D_MODEL, FF_SHARD), BF16, pl.BlockSpec((None, D_MODEL, FF_SHARD), lambda i, j, k: (i, 0, 0)))])[0]


def _wgrad_ff_down(ff, dx2b):
    t = dx2b.shape[0]
    tk = min(t, WGRAD_TK)
    return _mm("wgrad_ffn_down", (N_CHIPS, 1, t // tk),
               [(ff, pl.BlockSpec((None, tk, FF_SHARD), lambda i, j, k: (i, k, 0)),
                 dx2b, pl.BlockSpec((tk, D_MODEL), lambda i, j, k: (k, 0)))],
               TN, (FF_SHARD, D_MODEL), _epi_bf16,
               outs=[((N_CHIPS, FF_SHARD, D_MODEL), BF16, pl.BlockSpec((None, FF_SHARD, D_MODEL), lambda i, j, k: (i, 0, 0)))])[0]


def _local_step(x, pos_col, tgt, g1, ln_g, ln_b, w_s, b_s, g2, gf, w_p, w_pa, w_ps, w_out, w_g, w_u, w_d, on_grads=None):
    tables = _rope_tables(pos_col)
    bias_exp = jnp.repeat(jnp.transpose(b_s), SGU_W // SGU_GROUPS, axis=1)

    hs = _norm_fwd(x, g1)
    gu, qkvs = _in_proj(hs, w_p, tables)
    os_, ls_ = [], []
    for g, dil in enumerate(DILATIONS):
        o, lse = _attn_fwd(qkvs[g], g, dil)
        os_.append(o)
        ls_.append(lse)
    attn = _combine_fwd(os_, ls_)
    sgu = _sgu_fwd(gu, ln_g, ln_b, w_s, bias_exp)
    pa, ps, merged, x1, h2 = _merge_fwd(attn, sgu, gu, x, w_pa, w_ps, w_out, g2)
    a, b, ff = _ffn_fwd(h2, w_g, w_u)
    dx2, dx2b, loss, dgf = _ffn_down_loss(ff, w_d, x1, tgt, gf)

    da, db = _ffn_bwd_act(dx2b, w_d, a, b)
    dw_d = _wgrad_ff_down(ff, dx2b)
    dx1, dx1b, dg2 = _ffn_bwd_in(da, db, w_g, w_u, x1, dx2, g2)
    dw_g = _wgrad_ff_in("wgrad_ffn_gate", h2, da)
    dw_u = _wgrad_ff_in("wgrad_ffn_up", h2, db)

    dgu, dpa, dps, dattn, dsgu = _merge_bwd(dx1b, gu, pa, ps, w_pa, w_ps, w_out)
    dw_out = _wgrad_2d("wgrad_out", merged, dx1b, D_MODEL, D_MODEL)
    dw_pa = _wgrad_2d("wgrad_proj_attn", attn, dpa, ATTN_W, D_MODEL)
    dw_ps = _wgrad_2d("wgrad_proj_sgu", sgu, dps, SGU_W, D_MODEL)
    if on_grads is not None:
        ln_g = ln_g + on_grads(1, dict(w_proj_attn=dw_pa, w_proj_sgu=dw_ps, w_out=dw_out, w_ffn_gate=dw_g, w_ffn_up=dw_u,
                                       w_ffn_down=dw_d))[:, :SGU_W]
    dgu, dw_s, dbias, dln_g, dln_b = _sgu_bwd(dgu, gu, dsgu, ln_g, ln_b, w_s, bias_exp)
    dos, ccs = _combine_bwd(dattn, os_, ls_)
    dqkvs = [_attn_bwd(qkvs[g], dos[g], ccs[g], ls_[g], *tables[g], g, dil) for g, dil in enumerate(DILATIONS)]
    dw_p = _wgrad_in(hs, dgu, dqkvs)
    if on_grads is not None:
        g1 = g1 + on_grads(0, dict(w_in=dw_p))
    dx, dg1 = _in_proj_bwd(dgu, dqkvs, w_p, x, dx1, g1)

    db_s = jnp.transpose(dbias[:, ::SGU_W // SGU_GROUPS])
    small = dict(loss=loss, norm1_g=dg1, sgu_ln_g=dln_g, sgu_ln_b=dln_b, w_spatial=dw_s, b_spatial=db_s,
                 norm2_g=dg2, final_g=dgf)
    big = dict(w_in=dw_p, w_proj_attn=dw_pa, w_proj_sgu=dw_ps, w_out=dw_out, w_ffn_gate=dw_g, w_ffn_up=dw_u,
               w_ffn_down=dw_d)
    return dx, big, small


def _ew(name, fn, ins, out_dtypes):
    shp = ins[0].shape
    rows, cols = shp
    tr = next((cand for cand in (256, 352, 128) if rows % cand == 0 and rows > cand), rows)

    def body(*refs):
        res = fn(*[r[...] for r in refs[:len(ins)]])
        for o_ref, v in zip(refs[len(ins):], res):
            o_ref[...] = v.astype(o_ref.dtype)

    spec = pl.BlockSpec((tr, cols), lambda i: (i, 0))
    return pl.pallas_call(
        body, grid=(rows // tr,), in_specs=[spec] * len(ins), out_specs=[spec] * len(out_dtypes),
        out_shape=[jax.ShapeDtypeStruct(shp, d) for d in out_dtypes],
        compiler_params=_cparams(1), name=name)(*ins)


def _adamw_math(g, w, m, v):
    m = ADAM_B1 * m + (1.0 - ADAM_B1) * g
    v = ADAM_B2 * v + (1.0 - ADAM_B2) * (g * g)
    m_hat = m / (1.0 - ADAM_B1 ** ADAM_STEP)
    v_hat = v / (1.0 - ADAM_B2 ** ADAM_STEP)
    delta = -ADAM_LR * (m_hat / (jnp.sqrt(v_hat) + ADAM_EPS) + ADAM_WD * w)
    return delta, m, v


def _adamw(name, g, w, m, v):
    return _ew(name, lambda g_, w_, m_, v_: (g_,) + _adamw_math(g_, w_, m_, v_), [g, w, m, v], [F32] * 4)


VMEM_SPEC = pl.BlockSpec(memory_space=pltpu.VMEM)


def _for_row_chunks(rows, fn):
    ck = next(c for c in (64, 32, 16) if rows % c == 0)

    def step(i, carry):
        fn(pl.multiple_of(i * ck, ck), ck)
        return carry

    lax.fori_loop(0, rows // ck, step, 0)


def _place():
    x, y, c = lax.axis_index("x"), lax.axis_index("y"), lax.axis_index("c")
    chips = [(1 - x, y), (x, 1 - y), (1 - x, 1 - y)]
    return x, y, c, 2 * x + y, chips


def _rows(ref, start, size):
    if len(ref.shape) == 2:
        return ref.at[pl.ds(start, size), :]
    return ref.at[:, pl.ds(start, size), :]


def _comm_call(name, body, ins, out_shapes, scratch, n_remote):
    return pl.pallas_call(
        body, in_specs=[VMEM_SPEC] * len(ins), out_specs=[VMEM_SPEC] * len(out_shapes),
        out_shape=out_shapes,
        scratch_shapes=list(scratch) + [pltpu.SemaphoreType.DMA((n_remote,)), pltpu.SemaphoreType.DMA((n_remote,))],
        compiler_params=pltpu.CompilerParams(vmem_limit_bytes=VMEM_LIMIT), name=name)(*ins)


def _gather_weights(name, shards):
    nt = len(shards)

    def body(*refs):
        ins, outs = refs[:nt], refs[nt:2 * nt]
        send, recv = refs[2 * nt:]
        x, y, c, me, chips = _place()
        sibling = (x, y, 1 - c)
        firsts, passed, expects = [], [], []
        for t in range(nt):
            kh = ins[t].shape[0] // 2
            for j, chip in enumerate(chips):
                k = t * 3 + j
                theirs = 2 * chip[0] + chip[1]
                firsts.append(pltpu.make_async_remote_copy(
                    src_ref=_rows(ins[t], c * kh, kh), dst_ref=_rows(outs[t].at[me], c * kh, kh),
                    send_sem=send.at[k], recv_sem=recv.at[k], device_id=(*chip, c), device_id_type=MESH))
                landed = _rows(outs[t].at[theirs], c * kh, kh)
                expects.append(pltpu.make_async_remote_copy(
                    src_ref=landed, dst_ref=landed, send_sem=send.at[k], recv_sem=recv.at[k],
                    device_id=(*chip, c), device_id_type=MESH))
                passed.append(pltpu.make_async_remote_copy(
                    src_ref=landed, dst_ref=landed, send_sem=send.at[3 * nt + k], recv_sem=recv.at[3 * nt + k],
                    device_id=sibling, device_id_type=MESH))
        for cp in firsts:
            cp.start()
        for t in range(nt):
            mine = outs[t].at[me]

            def put(r0, ck, src=ins[t], dst=mine):
                dst[pl.ds(r0, ck), :] = src[pl.ds(r0, ck), :]

            _for_row_chunks(ins[t].shape[0], put)
        for k in range(3 * nt):
            expects[k].wait_recv()
            passed[k].start()
        for t in range(nt):
            kh = ins[t].shape[0] // 2
            for j, chip in enumerate(chips):
                k = t * 3 + j
                theirs = 2 * chip[0] + chip[1]
                other = _rows(outs[t].at[theirs], (1 - c) * kh, kh)
                pltpu.make_async_remote_copy(
                    src_ref=other, dst_ref=other, send_sem=send.at[3 * nt + k], recv_sem=recv.at[3 * nt + k],
                    device_id=sibling, device_id_type=MESH).wait_recv()
        for cp in firsts + passed:
            cp.wait_send()

    out_shapes = [jax.ShapeDtypeStruct((N_CHIPS,) + s.shape, s.dtype) for s in shards]
    return _comm_call(name, body, shards, out_shapes, [], 6 * nt)


def _pair_reduce(name, grads):
    nt = len(grads)

    def half(s):
        shp = list(s.shape)
        shp[-2] //= 2
        return tuple(shp)

    def body(*refs):
        ins, outs, got = refs[:nt], refs[nt:2 * nt], refs[2 * nt:3 * nt]
        send, recv = refs[3 * nt:]
        x, y, c, me, chips = _place()
        copies = []
        for t in range(nt):
            kh = ins[t].shape[-2] // 2
            rc = pltpu.make_async_remote_copy(
                src_ref=_rows(ins[t], (1 - c) * kh, kh), dst_ref=got[t], send_sem=send.at[t], recv_sem=recv.at[t],
                device_id=(x, y, 1 - c), device_id_type=MESH)
            rc.start()
            copies.append(rc)
        for t in range(nt):
            kh = ins[t].shape[-2] // 2
            copies[t].wait_recv()
            for lead in ([()] if len(ins[t].shape) == 2 else [(s,) for s in range(ins[t].shape[0])]):

                def add(r0, ck, lead=lead, src=ins[t], oth=got[t], dst=outs[t], kh=kh):
                    own = src[lead + (pl.ds(pl.multiple_of(c * kh + r0, ck), ck), slice(None))]
                    rows = lead + (pl.ds(r0, ck), slice(None))
                    dst[rows] = (own.astype(F32) + oth[rows].astype(F32)).astype(BF16)

                _for_row_chunks(kh, add)
        for rc in copies:
            rc.wait_send()

    shapes = [half(g) for g in grads]
    return _comm_call(name, body, grads, [jax.ShapeDtypeStruct(s, BF16) for s in shapes],
                      [pltpu.VMEM(s, BF16) for s in shapes], nt)


HBM_SPEC = pl.BlockSpec(memory_space=pltpu.HBM)
SEM_SPEC = pl.BlockSpec(memory_space=pltpu.SEMAPHORE)
DATAFLOW = pltpu.SideEffectType.DATAFLOW_SIDE_EFFECTING
TOKEN_SHAPE = (1, D_MODEL)


def _shard_cols(shape):
    return shape[2] if len(shape) == 3 else shape[1] // N_CHIPS


def _chip_piece(ref, j):
    if len(ref.shape) == 3:
        return ref.at[j]
    n4 = ref.shape[1] // N_CHIPS
    return ref.at[:, pl.ds(j * n4, n4)]


def _exchange_copies(sums, lands, send, recv):
    x, y, c, me, chips = _place()
    return [pltpu.make_async_remote_copy(
        src_ref=_chip_piece(sums[t], 2 * chip[0] + chip[1]), dst_ref=lands[t].at[j], send_sem=send.at[t * 3 + j],
        recv_sem=recv.at[t * 3 + j], device_id=(*chip, c), device_id_type=MESH)
        for t in range(len(sums)) for j, chip in enumerate(chips)]


def _chip_exchange_start(name, sums):
    nt = len(sums)
    lands = [lax.empty((3, s.shape[-2], _shard_cols(s.shape)), BF16) for s in sums]

    def body(*refs):
        send, recv = refs[2 * nt], refs[2 * nt + 1]
        for cp in _exchange_copies(refs[:nt], refs[nt:2 * nt], send, recv):
            cp.start()
        refs[-1][...] = jnp.zeros(TOKEN_SHAPE, F32)

    hbm = lambda a: pltpu.with_memory_space_constraint(a, pltpu.HBM)
    outs = pl.pallas_call(
        body, name=name,
        out_shape=[pltpu.SemaphoreType.DMA((3 * nt,)), pltpu.SemaphoreType.DMA((3 * nt,))]
        + [pltpu.HBM(s.shape, s.dtype) for s in sums] + [pltpu.HBM(l.shape, l.dtype) for l in lands]
        + [jax.ShapeDtypeStruct(TOKEN_SHAPE, F32)],
        in_specs=[HBM_SPEC] * (2 * nt), out_specs=[SEM_SPEC, SEM_SPEC] + [HBM_SPEC] * (2 * nt) + [VMEM_SPEC],
        input_output_aliases={i: 2 + i for i in range(2 * nt)},
        compiler_params=pltpu.CompilerParams(has_side_effects=DATAFLOW))(*[hbm(a) for a in list(sums) + lands])
    return outs[0], outs[1], outs[2:2 + nt], outs[2 + nt:2 + 2 * nt], outs[-1]


def _chip_exchange_wait(name, send, recv, sums, lands, after):
    nt = len(sums)

    def body(*refs):
        for cp in _exchange_copies(refs[:nt], refs[nt:2 * nt], refs[2 * nt], refs[2 * nt + 1]):
            cp.wait_send()
            cp.wait_recv()

    outs = pl.pallas_call(
        body, name=name,
        out_shape=[pltpu.HBM(s.shape, s.dtype) for s in sums] + [pltpu.HBM(l.shape, l.dtype) for l in lands],
        in_specs=[HBM_SPEC] * (2 * nt) + [SEM_SPEC, SEM_SPEC, pl.BlockSpec(memory_space=pl.ANY)],
        out_specs=[HBM_SPEC] * (2 * nt), input_output_aliases={i: i for i in range(2 * nt)},
        compiler_params=pltpu.CompilerParams(has_side_effects=DATAFLOW))(*sums, *lands, send, recv, after)
    return outs[:nt], outs[nt:]


def _chip_sum(name, sums, lands):
    nt = len(sums)

    def body(*refs):
        ins, slots, outs = refs[:nt], refs[nt:2 * nt], refs[2 * nt:3 * nt]
        send, recv = refs[3 * nt:]
        x, y, c, me, chips = _place()
        sibling = (x, y, 1 - c)
        handed = []
        for t in range(nt):
            kh, n4 = ins[t].shape[-2], outs[t].shape[1]
            for jj in range(N_CHIPS):

                @pl.when(me == jj)
                def _(jj=jj, src=ins[t], slot=slots[t], dst=outs[t], kh=kh, n4=n4):
                    def add(r0, ck):
                        rows = pl.ds(r0, ck)
                        own = src[jj, rows, :] if len(src.shape) == 3 else src[rows, jj * n4:(jj + 1) * n4]
                        acc = ((own.astype(F32) + slot[0, rows, :].astype(F32)) + slot[1, rows, :].astype(F32)) \
                            + slot[2, rows, :].astype(F32)
                        dst[pl.ds(pl.multiple_of(c * kh + r0, ck), ck), :] = acc

                    _for_row_chunks(kh, add)

            rc = pltpu.make_async_remote_copy(
                src_ref=_rows(outs[t], c * kh, kh), dst_ref=_rows(outs[t], c * kh, kh), send_sem=send.at[t],
                recv_sem=recv.at[t], device_id=sibling, device_id_type=MESH)
            rc.start()
            handed.append(rc)
        for t in range(nt):
            kh = ins[t].shape[-2]
            other = _rows(outs[t], (1 - c) * kh, kh)
            pltpu.make_async_remote_copy(
                src_ref=other, dst_ref=other, send_sem=send.at[t], recv_sem=recv.at[t],
                device_id=sibling, device_id_type=MESH).wait_recv()
        for rc in handed:
            rc.wait_send()

    out_shapes = [jax.ShapeDtypeStruct((2 * s.shape[-2], _shard_cols(s.shape)), F32) for s in sums]
    return _comm_call(name, body, list(sums) + list(lands), out_shapes, [], nt)


VEC_SHAPE = (8, D_MODEL + LANES)
VEC_SLOTS = dict(norm1_g=(slice(0, 1), slice(0, D_MODEL)), norm2_g=(slice(1, 2), slice(0, D_MODEL)),
                 final_g=(slice(2, 3), slice(0, D_MODEL)), sgu_ln_g=(slice(3, 4), slice(0, SGU_W)),
                 sgu_ln_b=(slice(3, 4), slice(SGU_W, 2 * SGU_W)), b_spatial=(slice(0, 8), slice(D_MODEL, D_MODEL + LANES)),
                 loss=(slice(4, 5), slice(0, LANES)))
VEC_PARAMS = ("norm1_g", "norm2_g", "final_g", "sgu_ln_g", "sgu_ln_b", "b_spatial")
SMALL_PARAMS = VEC_PARAMS + ("w_spatial",)
W_SPATIAL_2D = (SGU_GROUPS * SGU_CHUNK, SGU_CHUNK)


def _small_step(partials, w, m, v):
    def shape2d(name):
        if name == "w_spatial":
            return W_SPATIAL_2D
        rows, cols = VEC_SLOTS[name]
        return (rows.stop - rows.start, cols.stop - cols.start)

    g_names = VEC_PARAMS + ("loss", "w_spatial")
    ins = [partials[n].reshape(shape2d(n)) for n in g_names]
    for src in (w, m, v):
        ins += [src[n].reshape(shape2d(n)) for n in SMALL_PARAMS]
    ng, npar = len(g_names), len(SMALL_PARAMS)

    def body(*refs):
        g_in = dict(zip(g_names, refs[:ng]))
        w_in, m_in, v_in = (dict(zip(SMALL_PARAMS, refs[ng + k * npar:ng + (k + 1) * npar])) for k in range(3))
        o0 = ng + 3 * npar
        g_out = dict(zip(g_names, refs[o0:o0 + ng]))
        d_out, m_out, v_out = (dict(zip(SMALL_PARAMS, refs[o0 + ng + k * npar:o0 + ng + (k + 1) * npar])) for k in range(3))
        vec, vec_pair, vec_slot, ws_pair, ws_slot, vw, vm, vv, send, recv = refs[o0 + ng + 3 * npar:]
        x, y, c, me, chips = _place()
        sibling = (x, y, 1 - c)

        def pack(dst, parts):
            dst[...] = jnp.zeros(VEC_SHAPE, F32)
            for n, ref in parts.items():
                if n in VEC_SLOTS:
                    dst[VEC_SLOTS[n]] = ref[...]

        pack(vec, g_in)
        copies = []

        def allreduce(k0, src, pair, slot):
            first = pltpu.make_async_remote_copy(src_ref=src, dst_ref=pair, send_sem=send.at[k0], recv_sem=recv.at[k0],
                                                 device_id=sibling, device_id_type=MESH)
            first.start()
            first.wait_recv()
            slot[me] = src[...] + pair[...]
            arrivals = []
            for j, chip in enumerate(chips):
                theirs = 2 * chip[0] + chip[1]
                rc = pltpu.make_async_remote_copy(src_ref=slot.at[me], dst_ref=slot.at[me], send_sem=send.at[k0 + 1 + j],
                                                  recv_sem=recv.at[k0 + 1 + j], device_id=(*chip, c), device_id_type=MESH)
                rc.start()
                arrivals.append(pltpu.make_async_remote_copy(
                    src_ref=slot.at[theirs], dst_ref=slot.at[theirs], send_sem=send.at[k0 + 1 + j],
                    recv_sem=recv.at[k0 + 1 + j], device_id=(*chip, c), device_id_type=MESH))
                copies.append(rc)
            copies.append(first)
            return arrivals

        arrivals = allreduce(0, vec, vec_pair, vec_slot) + allreduce(4, g_in["w_spatial"], ws_pair, ws_slot)
        pack(vw, w_in)
        pack(vm, m_in)
        pack(vv, v_in)
        for a in arrivals:
            a.wait_recv()
        for rc in copies:
            rc.wait_send()

        g_vec = ((vec_slot[0] + vec_slot[1]) + vec_slot[2]) + vec_slot[3]
        d_vec, m_vec, v_vec = _adamw_math(g_vec, vw[...], vm[...], vv[...])
        vec[...] = g_vec
        vw[...] = d_vec
        vm[...] = m_vec
        vv[...] = v_vec
        for n in VEC_PARAMS + ("loss",):
            g_out[n][...] = vec[VEC_SLOTS[n]]
        for n in VEC_PARAMS:
            d_out[n][...] = vw[VEC_SLOTS[n]]
            m_out[n][...] = vm[VEC_SLOTS[n]]
            v_out[n][...] = vv[VEC_SLOTS[n]]

        def spatial(r0, ck):
            rows = pl.ds(r0, ck)
            g = ((ws_slot[0, rows, :] + ws_slot[1, rows, :]) + ws_slot[2, rows, :]) + ws_slot[3, rows, :]
            d_, m_, v_ = _adamw_math(g, w_in["w_spatial"][rows, :], m_in["w_spatial"][rows, :], v_in["w_spatial"][rows, :])
            g_out["w_spatial"][rows, :] = g
            d_out["w_spatial"][rows, :] = d_
            m_out["w_spatial"][rows, :] = m_
            v_out["w_spatial"][rows, :] = v_

        _for_row_chunks(W_SPATIAL_2D[0], spatial)

    out_shapes = [jax.ShapeDtypeStruct(shape2d(n), F32) for n in g_names + SMALL_PARAMS * 3]
    outs = pl.pallas_call(
        body, in_specs=[VMEM_SPEC] * len(ins), out_specs=[VMEM_SPEC] * len(out_shapes), out_shape=out_shapes,
        scratch_shapes=[pltpu.VMEM(VEC_SHAPE, F32), pltpu.VMEM(VEC_SHAPE, F32), pltpu.VMEM((N_CHIPS,) + VEC_SHAPE, F32),
                        pltpu.VMEM(W_SPATIAL_2D, F32), pltpu.VMEM((N_CHIPS,) + W_SPATIAL_2D, F32),
                        pltpu.VMEM(VEC_SHAPE, F32), pltpu.VMEM(VEC_SHAPE, F32), pltpu.VMEM(VEC_SHAPE, F32),
                        pltpu.SemaphoreType.DMA((8,)), pltpu.SemaphoreType.DMA((8,))],
        name="small_params_step")(*ins)
    grads = dict(zip(g_names, outs[:ng]))
    rest = [dict(zip(SMALL_PARAMS, outs[ng + k * npar:ng + (k + 1) * npar])) for k in range(3)]
    return grads, rest[0], rest[1], rest[2]


BIG = ("w_in", "w_proj_attn", "w_proj_sgu", "w_out", "w_ffn_gate", "w_ffn_up", "w_ffn_down")
COMM_GROUPS = (("w_in",), ("w_proj_attn", "w_proj_sgu", "w_out", "w_ffn_gate", "w_ffn_up", "w_ffn_down"))
WEIGHTS = ("norm1_g", "w_in", "sgu_ln_g", "sgu_ln_b", "w_spatial", "b_spatial", "w_proj_attn", "w_proj_sgu", "w_out",
           "norm2_g", "w_ffn_gate", "w_ffn_up", "w_ffn_down", "final_g")


def _cols_from_chips(g):
    return jnp.transpose(g, (1, 0, 2)).reshape(g.shape[1], N_CHIPS * g.shape[2])


def _permute_cols(w, perm):
    return jnp.concatenate([w[:, 512 * b:512 * (b + 1)] for b in perm], axis=1)


def kernel(x, positions, norm1_g, w_in, sgu_ln_g, sgu_ln_b, w_spatial, b_spatial, w_proj_attn, w_proj_sgu, w_out, norm2_g, w_ffn_gate, w_ffn_up, w_ffn_down, final_g, loss_target, m_norm1_g, m_w_in, m_sgu_ln_g, m_sgu_ln_b, m_w_spatial, m_b_spatial, m_w_proj_attn, m_w_proj_sgu, m_w_out, m_norm2_g, m_w_ffn_gate, m_w_ffn_up, m_w_ffn_down, m_final_g, v_norm1_g, v_w_in, v_sgu_ln_g, v_sgu_ln_b, v_w_spatial, v_b_spatial, v_w_proj_attn, v_w_proj_sgu, v_w_out, v_norm2_g, v_w_ffn_gate, v_w_ffn_up, v_w_ffn_down, v_final_g):
    w = dict(norm1_g=norm1_g, w_in=w_in, sgu_ln_g=sgu_ln_g, sgu_ln_b=sgu_ln_b, w_spatial=w_spatial, b_spatial=b_spatial,
             w_proj_attn=w_proj_attn, w_proj_sgu=w_proj_sgu, w_out=w_out, norm2_g=norm2_g, w_ffn_gate=w_ffn_gate,
             w_ffn_up=w_ffn_up, w_ffn_down=w_ffn_down, final_g=final_g)
    m = dict(norm1_g=m_norm1_g, w_in=m_w_in, sgu_ln_g=m_sgu_ln_g, sgu_ln_b=m_sgu_ln_b, w_spatial=m_w_spatial,
             b_spatial=m_b_spatial, w_proj_attn=m_w_proj_attn, w_proj_sgu=m_w_proj_sgu, w_out=m_w_out, norm2_g=m_norm2_g,
             w_ffn_gate=m_w_ffn_gate, w_ffn_up=m_w_ffn_up, w_ffn_down=m_w_ffn_down, final_g=m_final_g)
    v = dict(norm1_g=v_norm1_g, w_in=v_w_in, sgu_ln_g=v_sgu_ln_g, sgu_ln_b=v_sgu_ln_b, w_spatial=v_w_spatial,
             b_spatial=v_b_spatial, w_proj_attn=v_w_proj_attn, w_proj_sgu=v_w_proj_sgu, w_out=v_w_out, norm2_g=v_norm2_g,
             w_ffn_gate=v_w_ffn_gate, w_ffn_up=v_w_ffn_up, w_ffn_down=v_w_ffn_down, final_g=v_final_g)
    t = x.shape[1]

    shards = {n: _ew(f"cast_{n}", lambda a: (a,), [w[n][0]], [BF16])[0] for n in BIG}
    gath = {}
    for i, grp in enumerate(COMM_GROUPS):
        gath.update(zip(grp, _gather_weights(f"gather_weights_{i}", [shards[n] for n in grp])))
    w_p = _permute_cols(_cols_from_chips(gath["w_in"]), PERM)
    w_pa = _cols_from_chips(gath["w_proj_attn"])
    w_ps = _cols_from_chips(gath["w_proj_sgu"])
    w_o = gath["w_out"].reshape(D_MODEL, D_MODEL)

    exchanges = {}

    def on_grads(i, partials):
        if "w_in" in partials:
            partials["w_in"] = _permute_cols(partials["w_in"], INV_PERM)
        if "w_out" in partials:
            partials["w_out"] = partials["w_out"].reshape(N_CHIPS, D_MODEL // N_CHIPS, D_MODEL)
        sums = _pair_reduce(f"rs_pair_reduce_{i}", [partials[n] for n in COMM_GROUPS[i]])
        *exchanges[i], token = _chip_exchange_start(f"rs_exchange_start_{i}", sums)
        return token

    dx, _, small = _local_step(
        x[0], positions.reshape(t, 1), loss_target[0], norm1_g, sgu_ln_g, sgu_ln_b, w_spatial[0], b_spatial[0], norm2_g,
        final_g.reshape(1, D_MODEL), w_p, w_pa, w_ps, w_o, gath["w_ffn_gate"], gath["w_ffn_up"], gath["w_ffn_down"],
        on_grads=on_grads)

    grads = {}
    for i in (1, 0):
        send, recv, sums, lands = exchanges[i]
        sums, lands = _chip_exchange_wait(f"rs_exchange_wait_{i}", send, recv, sums, lands, dx)
        grads.update(zip(COMM_GROUPS[i], _chip_sum(f"rs_chip_sum_{i}", sums, lands)))

    delta, new_m, new_v = {}, {}, {}
    for n in BIG:
        shp = w[n].shape
        g_, d_, m_, v_ = _adamw(f"adamw_{n}", grads[n], w[n][0], m[n][0], v[n][0])
        grads[n], delta[n], new_m[n], new_v[n] = g_.reshape(shp), d_.reshape(shp), m_.reshape(shp), v_.reshape(shp)

    g_s, d_s, m_s, v_s = _small_step(small, w, m, v)
    loss = g_s["loss"][0, 0]
    for n in SMALL_PARAMS:
        shp = w[n].shape
        grads[n], delta[n], new_m[n], new_v[n] = (a[n].reshape(shp) for a in (g_s, d_s, m_s, v_s))

    return (loss, dx.reshape(x.shape), *[grads[n] for n in WEIGHTS], *[delta[n] for n in WEIGHTS],
            *[new_m[n] for n in WEIGHTS], *[new_v[n] for n in WEIGHTS])
```

```python
import functools

import numpy as np
import jax
import jax.numpy as jnp
from jax import lax
from jax.experimental import pallas as pl
from jax.experimental.pallas import tpu as pltpu

F32, BF16 = jnp.float32, jnp.bfloat16
MESH = pl.DeviceIdType.MESH

D_MODEL = 1024
HEAD_DIM = 64
ATTN_W = 512
DILATIONS = (1, 4, 16)
BLK = 128
ROPE_DIM = 16
ROPE_THETA = 500000.0
SGU_W = 512
SGU_CHUNK = 128
SGU_GROUPS = 8
D_FF = 2816
N_CHIPS = 4
FF_SHARD = D_FF // N_CHIPS
IN_COLS = 7680
EPS = 1e-6
NEG = -1e30
LANES = 128
VMEM_LIMIT = 52 * 1024 * 1024

ADAM_LR, ADAM_B1, ADAM_B2, ADAM_EPS, ADAM_WD, ADAM_STEP = 0.001, 0.9, 0.999, 1e-08, 0.01, 10

PERM = (11, 12, 13, 14, 9, 10, 0, 3, 6, 1, 4, 7, 2, 5, 8)
INV_PERM = tuple(int(i) for i in np.argsort(np.array(PERM)))


def _cparams(ngrid):
    return pltpu.CompilerParams(dimension_semantics=("arbitrary",) * ngrid, vmem_limit_bytes=VMEM_LIMIT)


def _full(shape):
    return pl.BlockSpec(shape, lambda *_: (0,) * len(shape))


def _resident(shape):
    return pl.BlockSpec(shape, lambda *_: (0,) * len(shape), pipeline_mode=pl.Buffered(1))


NN = ((1,), (0,))
NT = ((1,), (1,))
TN = ((0,), (0,))


def _mm(name, grid, pairs, dims, acc_shape, epi, *, extras=(), outs=(), reds=(), aliases=None):
    nk = grid[-1]
    npair, nex, nout, nred = len(pairs), len(extras), len(outs), len(reds)

    def body(*refs):
        a_refs = refs[:npair]
        b_refs = refs[npair:2 * npair]
        p0 = 2 * npair
        e_refs = refs[p0:p0 + nex]
        o_refs = refs[p0 + nex:p0 + nex + nout]
        r_refs = refs[p0 + nex + nout:p0 + nex + nout + nred]
        ids = [pl.program_id(a) for a in range(len(grid))]
        k = ids[-1]
        if nred:
            first = ids[0] == 0
            for v in ids[1:]:
                first = first & (v == 0)

            @pl.when(first)
            def _():
                for r in r_refs:
                    r[...] = jnp.zeros(r.shape, r.dtype)

        part = None
        for a_ref, b_ref in zip(a_refs, b_refs):
            d = lax.dot_general(a_ref[...], b_ref[...], (dims, ((), ())), preferred_element_type=F32)
            part = d if part is None else part + d
        if nk == 1:
            epi(part, e_refs, o_refs, r_refs, ids)
        else:
            acc_ref = refs[-1]

            @pl.when(k == 0)
            def _():
                acc_ref[...] = part

            @pl.when(k > 0)
            def _():
                acc_ref[...] += part

            @pl.when(k == nk - 1)
            def _():
                epi(acc_ref[...], e_refs, o_refs, r_refs, ids)

    in_specs = [p[1] for p in pairs] + [p[3] for p in pairs] + [e[1] for e in extras]
    args = [p[0] for p in pairs] + [p[2] for p in pairs] + [e[0] for e in extras]
    out_shape = [jax.ShapeDtypeStruct(o[0], o[1]) for o in outs] + [jax.ShapeDtypeStruct(r, F32) for r in reds]
    out_specs = [o[2] for o in outs] + [_full(r) for r in reds]
    scratch_shapes = [pltpu.VMEM(acc_shape, F32)] if nk > 1 else []
    return pl.pallas_call(
        body, grid=grid, in_specs=in_specs, out_specs=out_specs, out_shape=out_shape, scratch_shapes=scratch_shapes,
        input_output_aliases=aliases or {}, compiler_params=_cparams(len(grid)), name=name)(*args)


def _rope(v, cos_t, sin_t):
    half = ROPE_DIM // 2
    first = (lax.broadcasted_iota(jnp.int32, cos_t.shape, 1) % HEAD_DIM) < half
    outs = []
    for cs in range(v.shape[1] // LANES):
        x = v[:, cs * LANES:(cs + 1) * LANES]
        partner = jnp.where(first, pltpu.roll(x, LANES - half, axis=1), pltpu.roll(x, half, axis=1))
        outs.append(x * cos_t + partner * sin_t)
    return outs[0] if len(outs) == 1 else jnp.concatenate(outs, axis=1)


def _spread_heads(v2, upper):
    other = pltpu.roll(v2, HEAD_DIM, axis=1)
    h0 = jnp.where(upper, other, v2)
    h1 = jnp.where(upper, v2, other)
    return jnp.concatenate([jnp.concatenate([h0, h0], axis=1), jnp.concatenate([h1, h1], axis=1)], axis=0)


def _sigmoid(v):
    return 1.0 / (1.0 + jnp.exp(-v))


def _rms_stats(v):
    r = lax.rsqrt(jnp.mean(v * v, axis=-1, keepdims=True) + EPS)
    return v * r, r


def _rms_bwd(dy, xhat, r, g):
    dxh = dy * g
    return r * (dxh - xhat * jnp.mean(dxh * xhat, axis=-1, keepdims=True))


def _head_sum_matrix():
    idx = np.arange(ATTN_W) // HEAD_DIM
    return jnp.asarray((idx[:, None] == idx[None, :]).astype(np.float32), dtype=BF16)


def _group_sum(v, e):
    hi = v.astype(BF16)
    lo = (v - hi.astype(F32)).astype(BF16)
    return jnp.dot(hi, e, preferred_element_type=F32) + jnp.dot(lo, e, preferred_element_type=F32)


TILE = 512


def _to_slabs(slab_ref, v):
    for cs in range(slab_ref.shape[0]):
        slab_ref[cs] = v[:, cs * LANES:(cs + 1) * LANES]


def _from_slabs(slab_ref):
    return jnp.concatenate([slab_ref[cs] for cs in range(slab_ref.shape[0])], axis=1)


def _class_rows(slab_ref, r, dil):
    n = slab_ref.shape[1] // dil
    return jnp.concatenate([slab_ref.at[cs][pl.ds(r, n, stride=dil), :] for cs in range(slab_ref.shape[0])], axis=1)


def _put_class_rows(slab_ref, r, dil, v):
    n = slab_ref.shape[1] // dil
    for cs in range(slab_ref.shape[0]):
        slab_ref.at[cs][pl.ds(r, n, stride=dil), :] = v[:, cs * LANES:(cs + 1) * LANES]


def _natural_from_group(slab_ref, grp_ref):
    dil = grp_ref.shape[0]
    for r in range(dil):
        _put_class_rows(slab_ref, r, dil, grp_ref[r].astype(F32))
    return _from_slabs(slab_ref)


def _group_from_natural(slab_ref, grp_ref, v):
    dil = grp_ref.shape[0]
    _to_slabs(slab_ref, v)
    for r in range(dil):
        grp_ref[r] = _class_rows(slab_ref, r, dil).astype(grp_ref.dtype)


def _group_spec(dil, tile, width):
    return pl.BlockSpec((dil, tile // dil, width), lambda i, *_: (0, i, 0))


def _slabs(tile, width):
    return pltpu.VMEM((width // LANES, tile, LANES), F32)


def _rope_consts():
    lane = np.arange(LANES) % HEAD_DIM
    fi = lane % (ROPE_DIM // 2)
    invf = np.where(lane < ROPE_DIM, ROPE_THETA ** (-(2.0 * fi) / ROPE_DIM), 0.0)
    sgn = np.where(lane < ROPE_DIM // 2, -1.0, np.where(lane < ROPE_DIM, 1.0, 0.0))
    return (jnp.asarray(invf.astype(np.float32)).reshape(1, LANES), jnp.asarray(sgn.astype(np.float32)).reshape(1, LANES))


def _rope_tables(pos_col):
    t = pos_col.shape[0]
    tile = min(t, TILE)
    invf, sgn = _rope_consts()

    def body(p_ref, f_ref, s_ref, c0, s0, c1, s1, c2, s2, slab_c, slab_s):
        ang = p_ref[...].astype(F32) * f_ref[...]
        cos, sin = jnp.cos(ang), jnp.sin(ang) * s_ref[...]
        c0[...] = cos
        s0[...] = sin
        _group_from_natural(slab_c, c1, cos)
        _group_from_natural(slab_s, s1, sin)
        for r in range(DILATIONS[2]):
            c2[r] = _class_rows(slab_c, r, DILATIONS[2])
            s2[r] = _class_rows(slab_s, r, DILATIONS[2])

    nat = pl.BlockSpec((tile, LANES), lambda i: (i, 0))
    specs, shapes = [nat, nat], [(t, LANES)] * 2
    for d in DILATIONS[1:]:
        specs += [_group_spec(d, tile, LANES)] * 2
        shapes += [(d, t // d, LANES)] * 2
    outs = pl.pallas_call(
        body, grid=(t // tile,),
        in_specs=[pl.BlockSpec((tile, 1), lambda i: (i, 0)), _full((1, LANES)), _full((1, LANES))],
        out_specs=specs, out_shape=[jax.ShapeDtypeStruct(s, F32) for s in shapes],
        scratch_shapes=[_slabs(tile, LANES)] * 2,
        compiler_params=_cparams(1), name="rope_tables")(pos_col, invf, sgn)
    return [(outs[2 * g].reshape(t, LANES), outs[2 * g + 1].reshape(t, LANES)) for g in range(len(DILATIONS))]


def _norm_fwd(x, g):
    t = x.shape[0]
    tile = min(t, TILE)

    def body(x_ref, g_ref, h0_ref, h1_ref, h2_ref, slab):
        xhat, _ = _rms_stats(x_ref[...])
        hn = xhat * g_ref[...]
        h0_ref[...] = hn.astype(BF16)
        _group_from_natural(slab, h1_ref, hn)
        for r in range(DILATIONS[2]):
            h2_ref[r] = _class_rows(slab, r, DILATIONS[2]).astype(BF16)

    nat = pl.BlockSpec((tile, D_MODEL), lambda i: (i, 0))
    return pl.pallas_call(
        body, grid=(t // tile,),
        in_specs=[nat, _full((1, D_MODEL))],
        out_specs=[nat] + [_group_spec(d, tile, D_MODEL) for d in DILATIONS[1:]],
        out_shape=[jax.ShapeDtypeStruct((t, D_MODEL), BF16)]
        + [jax.ShapeDtypeStruct((d, t // d, D_MODEL), BF16) for d in DILATIONS[1:]],
        scratch_shapes=[_slabs(tile, D_MODEL)],
        compiler_params=_cparams(1), name="norm1_fwd")(x, g)


GU_COLS = 3072
GROUP_COLS = 1536


def _in_proj(hs, w_p, tables):
    t = hs[0].shape[0]
    tm = min(t, 1024)

    def body_gu(h_ref, w_ref, o_ref):
        o_ref[...] = jnp.dot(h_ref[...], w_ref[...], preferred_element_type=F32).astype(BF16)

    gu = _token_call("in_proj_gates_uv", body_gu, t, tm,
                     [(hs[0], _rows_spec(tm, D_MODEL)),
                      (w_p, pl.BlockSpec((D_MODEL, GU_COLS), lambda i: (0, 0), pipeline_mode=pl.Buffered(1)))],
                     [((t, GU_COLS), BF16, _rows_spec(tm, GU_COLS))])[0]

    qkvs = []
    for g in range(len(DILATIONS)):

        def body_qkv(h_ref, w_ref, cos_ref, sin_ref, o_ref):
            acc = jnp.dot(h_ref[...], w_ref[...], preferred_element_type=F32)
            cos_w, sin_w = cos_ref[...], sin_ref[...]
            o_ref[:, 0:ATTN_W] = (_rope(acc[:, 0:ATTN_W], cos_w, sin_w) * HEAD_DIM ** -0.5).astype(BF16)
            o_ref[:, ATTN_W:2 * ATTN_W] = _rope(acc[:, ATTN_W:2 * ATTN_W], cos_w, sin_w).astype(BF16)
            o_ref[:, 2 * ATTN_W:] = acc[:, 2 * ATTN_W:].astype(BF16)

        col = (GU_COLS + g * GROUP_COLS) // GROUP_COLS
        cos_t, sin_t = tables[g]
        qkvs.append(_token_call(
            f"in_proj_qkv_g{g}", body_qkv, t, tm,
            [(hs[g].reshape(t, D_MODEL), _rows_spec(tm, D_MODEL)),
             (w_p, pl.BlockSpec((D_MODEL, GROUP_COLS), lambda i, col=col: (0, col), pipeline_mode=pl.Buffered(1))),
             (cos_t, _rows_spec(tm, LANES)), (sin_t, _rows_spec(tm, LANES))],
            [((t, GROUP_COLS), BF16, _rows_spec(tm, GROUP_COLS))])[0])
    return gu, qkvs


def _attn_masks(n):
    row = lax.broadcasted_iota(jnp.int32, (2 * BLK, 2 * BLK), 0) % BLK
    col = lax.broadcasted_iota(jnp.int32, (2 * BLK, 2 * BLK), 1)
    diff = BLK + row - col
    valid = (diff >= 0) & (diff <= BLK) & ((col >= BLK) | (n > 0))
    upper = lax.broadcasted_iota(jnp.int32, (BLK, LANES), 1) >= HEAD_DIM
    return valid, upper


def _stack_heads(v2, upper):
    zero = jnp.zeros_like(v2)
    return jnp.concatenate([jnp.where(upper, zero, v2), jnp.where(upper, v2, zero)], axis=0)


def _unstack_heads(v, upper):
    return jnp.where(upper, v[BLK:], v[:BLK])


def _attn_fwd(qkv, g, dil):
    t = qkv.shape[0]
    length = t // dil
    nb = length // BLK
    view = qkv.reshape(dil, length, GROUP_COLS)

    def body(q_ref, kc_ref, kp_ref, vc_ref, vp_ref, o_ref, l_ref):
        n = pl.program_id(1)
        valid, upper = _attn_masks(n)
        for p in range(ATTN_W // LANES):
            sl = slice(p * LANES, (p + 1) * LANES)
            qs = _stack_heads(q_ref[:, sl], upper)
            k2 = jnp.concatenate([kp_ref[:, sl], kc_ref[:, sl]], axis=0)
            v2 = jnp.concatenate([vp_ref[:, sl], vc_ref[:, sl]], axis=0)
            s = lax.dot_general(qs, k2, (NT, ((), ())), preferred_element_type=F32)
            s = jnp.where(valid, s, NEG)
            m = jnp.max(s, axis=1, keepdims=True)
            pe = jnp.exp(s - m)
            den = jnp.sum(pe, axis=1, keepdims=True)
            o = jnp.dot(pe.astype(BF16), v2, preferred_element_type=F32) / den
            lse = jnp.broadcast_to(m + jnp.log(den), (2 * BLK, LANES))
            o_ref[:, sl] = _unstack_heads(o, upper)
            l_ref[:, sl] = _unstack_heads(lse, upper)

    cur = lambda part: pl.BlockSpec((None, BLK, ATTN_W), lambda r, n: (r, n, part))
    prev = lambda part: pl.BlockSpec((None, BLK, ATTN_W), lambda r, n: (r, jnp.maximum(n - 1, 0), part))
    out_spec = pl.BlockSpec((None, BLK, ATTN_W), lambda r, n: (r, n, 0))
    return pl.pallas_call(
        body, grid=(dil, nb),
        in_specs=[cur(0), cur(1), prev(1), cur(2), prev(2)],
        out_specs=[out_spec, out_spec],
        out_shape=[jax.ShapeDtypeStruct((dil, length, ATTN_W), F32)] * 2,
        compiler_params=_cparams(2), name=f"attn_fwd_g{g}")(view, view, view, view, view)


def _alphas(l0, l1, l2):
    m = jnp.maximum(jnp.maximum(l0, l1), l2)
    e0, e1, e2 = jnp.exp(l0 - m), jnp.exp(l1 - m), jnp.exp(l2 - m)
    inv = 1.0 / (e0 + e1 + e2)
    return e0 * inv, e1 * inv, e2 * inv


def _natural_group_values(o_refs, l_refs, slabs):
    os_ = [o_refs[0][0]] + [_natural_from_group(slabs[2 * g - 2], o_refs[g]) for g in (1, 2)]
    ls_ = [l_refs[0][0]] + [_natural_from_group(slabs[2 * g - 1], l_refs[g]) for g in (1, 2)]
    return os_, ls_


def _combine_fwd(os_, ls_):
    t = os_[0].shape[1]
    tile = min(t, TILE)

    def body(o0, o1, o2, l0, l1, l2, a_ref, *slabs):
        ov, lv = _natural_group_values((o0, o1, o2), (l0, l1, l2), slabs)
        a0, a1, a2 = _alphas(*lv)
        a_ref[...] = (a0 * ov[0] + a1 * ov[1] + a2 * ov[2]).astype(BF16)

    specs = [_group_spec(d, tile, ATTN_W) for d in DILATIONS]
    return pl.pallas_call(
        body, grid=(t // tile,), in_specs=specs * 2, out_specs=pl.BlockSpec((tile, ATTN_W), lambda i: (i, 0)),
        out_shape=jax.ShapeDtypeStruct((t, ATTN_W), BF16),
        scratch_shapes=[_slabs(tile, ATTN_W)] * 4,
        compiler_params=_cparams(1), name="combine_fwd")(*os_, *ls_)


def _combine_bwd(dattn, os_, ls_):
    t = dattn.shape[0]
    tile = min(t, TILE)
    e = _head_sum_matrix()

    def body(d_ref, o0, o1, o2, l0, l1, l2, e_ref, do0, do1, do2, c0, c1, c2, *slabs):
        ov, lv = _natural_group_values((o0, o1, o2), (l0, l1, l2), slabs)
        alphas = _alphas(*lv)
        d = d_ref[...]
        attn = alphas[0] * ov[0] + alphas[1] * ov[1] + alphas[2] * ov[2]
        s = _group_sum(d * attn, e_ref[...])
        do0[0] = (alphas[0] * d).astype(BF16)
        c0[0] = -alphas[0] * s
        for g, do_ref, c_ref in ((1, do1, c1), (2, do2, c2)):
            _group_from_natural(slabs[2 * g - 2], do_ref, alphas[g] * d)
            _group_from_natural(slabs[2 * g - 1], c_ref, -alphas[g] * s)

    specs = [_group_spec(d, tile, ATTN_W) for d in DILATIONS]
    shapes = [(d, t // d, ATTN_W) for d in DILATIONS]
    outs = pl.pallas_call(
        body, grid=(t // tile,),
        in_specs=[pl.BlockSpec((tile, ATTN_W), lambda i: (i, 0))] + specs * 2 + [_full((ATTN_W, ATTN_W))],
        out_specs=specs * 2,
        out_shape=[jax.ShapeDtypeStruct(s, BF16) for s in shapes] + [jax.ShapeDtypeStruct(s, F32) for s in shapes],
        scratch_shapes=[_slabs(tile, ATTN_W)] * 4,
        compiler_params=_cparams(1), name="combine_bwd")(dattn, *os_, *ls_, e)
    return outs[:3], outs[3:]


def _attn_bwd(qkv, do, cc, lse, cos_t, sin_t, g, dil):
    t = qkv.shape[0]
    length = t // dil
    nb = length // BLK
    qkv_v = qkv.reshape(dil, length, GROUP_COLS)
    cos_v, sin_v = (a.reshape(dil, length, LANES) for a in (cos_t, sin_t))
    scale = HEAD_DIM ** -0.5

    def body(q_ref, kc_ref, kp_ref, vc_ref, vp_ref, do_ref, c_ref, l_ref, cosc, sinc, cosp, sinp,
             out_ref, dq_s, dk_s, dv_s):
        n = pl.program_id(1)
        valid, upper = _attn_masks(n)

        @pl.when(n < nb)
        def _():
            cos_c, sin_c = cosc[...], sinc[...]
            cos_p, sin_p = cosp[...], sinp[...]
            dq_parts, dkp_parts, dkc_parts, dvp_parts, dvc_parts = [], [], [], [], []
            for p in range(ATTN_W // LANES):
                sl = slice(p * LANES, (p + 1) * LANES)
                qs = _stack_heads(q_ref[:, sl], upper)
                dos = _stack_heads(do_ref[:, sl], upper)
                k2 = jnp.concatenate([kp_ref[:, sl], kc_ref[:, sl]], axis=0)
                v2 = jnp.concatenate([vp_ref[:, sl], vc_ref[:, sl]], axis=0)
                l_col = _spread_heads(l_ref[:, sl], upper)
                c_col = _spread_heads(c_ref[:, sl], upper)
                s = lax.dot_general(qs, k2, (NT, ((), ())), preferred_element_type=F32)
                pe = jnp.exp(jnp.where(valid, s, NEG) - l_col)
                dpv = lax.dot_general(dos, v2, (NT, ((), ())), preferred_element_type=F32)
                ds = (pe * (dpv + c_col)).astype(BF16)
                dq2 = _unstack_heads(jnp.dot(ds, k2, preferred_element_type=F32), upper)
                dk2 = lax.dot_general(ds, qs, (TN, ((), ())), preferred_element_type=F32)
                dv2 = lax.dot_general(pe.astype(BF16), dos, (TN, ((), ())), preferred_element_type=F32)
                dq_parts.append(dq2)
                dkp_parts.append(dk2[:BLK])
                dkc_parts.append(dk2[BLK:])
                dvp_parts.append(dv2[:BLK])
                dvc_parts.append(dv2[BLK:])
            dq = _rope(jnp.concatenate(dq_parts, axis=1) * scale, cos_c, -sin_c)
            dkc = _rope(jnp.concatenate(dkc_parts, axis=1), cos_c, -sin_c)
            dkp = _rope(jnp.concatenate(dkp_parts, axis=1), cos_p, -sin_p)
            dvp = jnp.concatenate(dvp_parts, axis=1)
            dvc = jnp.concatenate(dvc_parts, axis=1)

            @pl.when(n > 0)
            def _():
                out_ref[:, 0:ATTN_W] = dq_s[...].astype(BF16)
                out_ref[:, ATTN_W:2 * ATTN_W] = (dk_s[...] + dkp).astype(BF16)
                out_ref[:, 2 * ATTN_W:3 * ATTN_W] = (dv_s[...] + dvp).astype(BF16)

            dq_s[...] = dq
            dk_s[...] = dkc
            dv_s[...] = dvc

        @pl.when(n == nb)
        def _():
            out_ref[:, 0:ATTN_W] = dq_s[...].astype(BF16)
            out_ref[:, ATTN_W:2 * ATTN_W] = dk_s[...].astype(BF16)
            out_ref[:, 2 * ATTN_W:3 * ATTN_W] = dv_s[...].astype(BF16)

    nc = lambda n: jnp.minimum(n, nb - 1)
    npv = lambda n: jnp.maximum(jnp.minimum(n, nb - 1) - 1, 0)
    cur = lambda part: pl.BlockSpec((None, BLK, ATTN_W), lambda r, n: (r, nc(n), part))
    prev = lambda part: pl.BlockSpec((None, BLK, ATTN_W), lambda r, n: (r, npv(n), part))
    row = pl.BlockSpec((None, BLK, ATTN_W), lambda r, n: (r, nc(n), 0))
    tab_c = pl.BlockSpec((None, BLK, LANES), lambda r, n: (r, nc(n), 0))
    tab_p = pl.BlockSpec((None, BLK, LANES), lambda r, n: (r, npv(n), 0))
    out_spec = pl.BlockSpec((None, BLK, GROUP_COLS), lambda r, n: (r, jnp.maximum(n - 1, 0), 0))
    out = pl.pallas_call(
        body, grid=(dil, nb + 1),
        in_specs=[cur(0), cur(1), prev(1), cur(2), prev(2), row, row, row, tab_c, tab_c, tab_p, tab_p],
        out_specs=out_spec,
        out_shape=jax.ShapeDtypeStruct((dil, length, GROUP_COLS), BF16),
        scratch_shapes=[pltpu.VMEM((BLK, ATTN_W), F32)] * 3,
        compiler_params=_cparams(2), name=f"attn_bwd_g{g}")(
            qkv_v, qkv_v, qkv_v, qkv_v, qkv_v, do, cc, lse, cos_v, sin_v, cos_v, sin_v)
    return out.reshape(t, GROUP_COLS)


SQRT_HALF = 0.7071067811865476
INV_SQRT_2PI = 0.3989422804014327


def _sgu_core(uv, g, b, w_ref, bias):
    cdf = 0.5 * (1.0 + lax.erf(uv * SQRT_HALF))
    z = uv * cdf
    u, v = z[:, :SGU_W], z[:, SGU_W:]
    mu = jnp.mean(v, axis=1, keepdims=True)
    xc = v - mu
    rs = lax.rsqrt(jnp.mean(xc * xc, axis=1, keepdims=True) + EPS)
    xhat = xc * rs
    vn = xhat * g + b
    row = lax.broadcasted_iota(jnp.int32, (SGU_CHUNK, SGU_CHUNK), 0)
    col = lax.broadcasted_iota(jnp.int32, (SGU_CHUNK, SGU_CHUNK), 1)
    tril = row >= col
    upper = lax.broadcasted_iota(jnp.int32, (SGU_CHUNK, LANES), 1) >= SGU_W // SGU_GROUPS
    ws, vlo, vhi, mixed = [], [], [], []
    for pr in range(SGU_W // LANES):
        sl = slice(pr * LANES, (pr + 1) * LANES)
        w0 = jnp.where(tril, w_ref[2 * pr], 0.0).astype(BF16)
        w1 = jnp.where(tril, w_ref[2 * pr + 1], 0.0).astype(BF16)
        vn2 = vn[:, sl]
        lo = jnp.where(upper, 0.0, vn2).astype(BF16)
        hi = jnp.where(upper, vn2, 0.0).astype(BF16)
        mixed.append(jnp.dot(w0, lo, preferred_element_type=F32) + jnp.dot(w1, hi, preferred_element_type=F32)
                     + bias[:, sl])
        ws.append((w0, w1))
        vlo.append(lo)
        vhi.append(hi)
    return cdf, u, xhat, rs, jnp.concatenate(mixed, axis=1), ws, vlo, vhi, tril, upper


def _sgu_fwd(gu, ln_g, ln_b, w_s, bias_exp):
    t = gu.shape[0]

    def body(uv_ref, g_ref, b_ref, w_ref, bias_ref, o_ref):
        _, u, _, _, mixed, *_ = _sgu_core(uv_ref[...].astype(F32), g_ref[...], b_ref[...], w_ref, bias_ref[...])
        o_ref[...] = (u * mixed).astype(BF16)

    return pl.pallas_call(
        body, grid=(t // SGU_CHUNK,),
        in_specs=[pl.BlockSpec((SGU_CHUNK, 2 * SGU_W), lambda n: (n, 2)), _full((1, SGU_W)), _full((1, SGU_W)),
                  _full((SGU_GROUPS, SGU_CHUNK, SGU_CHUNK)), _full((SGU_CHUNK, SGU_W))],
        out_specs=pl.BlockSpec((SGU_CHUNK, SGU_W), lambda n: (n, 0)),
        out_shape=jax.ShapeDtypeStruct((t, SGU_W), BF16),
        compiler_params=_cparams(1), name="sgu_fwd")(gu, ln_g, ln_b, w_s, bias_exp)


def _sgu_bwd(dproj, gu, dsgu, ln_g, ln_b, w_s, bias_exp):
    t = gu.shape[0]
    nchunks = t // SGU_CHUNK
    e = _head_sum_matrix()

    def body(dp_in, uv_ref, ds_ref, g_ref, b_ref, w_ref, bias_ref, e_ref, out_ref, dw_ref, dbias_ref, dg_ref, db_ref):
        n = pl.program_id(0)

        @pl.when(n == 0)
        def _():
            dw_ref[...] = jnp.zeros(dw_ref.shape, F32)
            dbias_ref[...] = jnp.zeros(dbias_ref.shape, F32)
            dg_ref[...] = jnp.zeros(dg_ref.shape, F32)
            db_ref[...] = jnp.zeros(db_ref.shape, F32)

        uv = uv_ref[...].astype(F32)
        g = g_ref[...]
        cdf, u, xhat, rs, mixed, ws, vlo, vhi, tril, upper = _sgu_core(uv, g, b_ref[...], w_ref, bias_ref[...])
        dsg = ds_ref[...]
        du = dsg * mixed
        dmixed = dsg * u
        dbias_ref[...] += dmixed
        dvn = []
        for pr in range(SGU_W // LANES):
            sl = slice(pr * LANES, (pr + 1) * LANES)
            dm2 = dmixed[:, sl]
            dlo = jnp.where(upper, 0.0, dm2).astype(BF16)
            dhi = jnp.where(upper, dm2, 0.0).astype(BF16)
            w0, w1 = ws[pr]
            dvn.append(lax.dot_general(w0, dlo, (TN, ((), ())), preferred_element_type=F32)
                       + lax.dot_general(w1, dhi, (TN, ((), ())), preferred_element_type=F32))
            dw0 = lax.dot_general(dlo, vlo[pr], (NT, ((), ())), preferred_element_type=F32)
            dw1 = lax.dot_general(dhi, vhi[pr], (NT, ((), ())), preferred_element_type=F32)
            dw_ref[2 * pr] += jnp.where(tril, dw0, 0.0)
            dw_ref[2 * pr + 1] += jnp.where(tril, dw1, 0.0)
        dvn = jnp.concatenate(dvn, axis=1)
        dg_ref[...] += jnp.sum(dvn * xhat, axis=0, keepdims=True)
        db_ref[...] += jnp.sum(dvn, axis=0, keepdims=True)
        dxh = dvn * g
        dv = rs * (dxh - jnp.mean(dxh, axis=1, keepdims=True) - xhat * jnp.mean(dxh * xhat, axis=1, keepdims=True))
        dz = jnp.concatenate([du, dv], axis=1)
        dgelu = cdf + uv * (INV_SQRT_2PI * jnp.exp(-0.5 * uv * uv))
        out_ref[...] = (dz * dgelu).astype(BF16)

        @pl.when(n == nchunks - 1)
        def _():
            dbias_ref[...] = _group_sum(dbias_ref[...], e_ref[...])

    outs = pl.pallas_call(
        body, grid=(nchunks,),
        in_specs=[pl.BlockSpec(memory_space=pl.ANY), pl.BlockSpec((SGU_CHUNK, 2 * SGU_W), lambda n: (n, 2)),
                  pl.BlockSpec((SGU_CHUNK, SGU_W), lambda n: (n, 0)), _full((1, SGU_W)), _full((1, SGU_W)),
                  _full((SGU_GROUPS, SGU_CHUNK, SGU_CHUNK)), _full((SGU_CHUNK, SGU_W)), _full((ATTN_W, ATTN_W))],
        out_specs=[pl.BlockSpec((SGU_CHUNK, 2 * SGU_W), lambda n: (n, 2)), _full((SGU_GROUPS, SGU_CHUNK, SGU_CHUNK)),
                   _full((SGU_CHUNK, SGU_W)), _full((1, SGU_W)), _full((1, SGU_W))],
        out_shape=[jax.ShapeDtypeStruct(dproj.shape, BF16), jax.ShapeDtypeStruct((SGU_GROUPS, SGU_CHUNK, SGU_CHUNK), F32),
                   jax.ShapeDtypeStruct((SGU_CHUNK, SGU_W), F32), jax.ShapeDtypeStruct((1, SGU_W), F32),
                   jax.ShapeDtypeStruct((1, SGU_W), F32)],
        input_output_aliases={0: 0},
        compiler_params=_cparams(1), name="sgu_bwd")(dproj, gu, dsgu, ln_g, ln_b, w_s, bias_exp, e)
    return outs


def _merge_fwd(attn, sgu, gu, x, w_pa, w_ps, w_out, g2):
    t = x.shape[0]
    tm = min(t, 256)

    def body(a_ref, s_ref, ga_ref, gb_ref, x_ref, wpa, wps, wo, g_ref, pa_ref, ps_ref, m_ref, x1_ref, h2_ref):
        pa = jnp.dot(a_ref[...], wpa[...], preferred_element_type=F32)
        ps = jnp.dot(s_ref[...], wps[...], preferred_element_type=F32)
        merged = (_sigmoid(ga_ref[...].astype(F32)) * pa + _sigmoid(gb_ref[...].astype(F32)) * ps).astype(BF16)
        x1 = x_ref[...] + jnp.dot(merged, wo[...], preferred_element_type=F32)
        xhat, _ = _rms_stats(x1)
        pa_ref[...] = pa.astype(BF16)
        ps_ref[...] = ps.astype(BF16)
        m_ref[...] = merged
        x1_ref[...] = x1
        h2_ref[...] = (xhat * g_ref[...]).astype(BF16)

    half = pl.BlockSpec((tm, ATTN_W), lambda i: (i, 0))
    full = pl.BlockSpec((tm, D_MODEL), lambda i: (i, 0))
    return pl.pallas_call(
        body, grid=(t // tm,),
        in_specs=[half, half, pl.BlockSpec((tm, D_MODEL), lambda i: (i, 0)), pl.BlockSpec((tm, D_MODEL), lambda i: (i, 1)),
                  full, _full((ATTN_W, D_MODEL)), _full((SGU_W, D_MODEL)), _full((D_MODEL, D_MODEL)), _full((1, D_MODEL))],
        out_specs=[full] * 5,
        out_shape=[jax.ShapeDtypeStruct((t, D_MODEL), BF16), jax.ShapeDtypeStruct((t, D_MODEL), BF16),
                   jax.ShapeDtypeStruct((t, D_MODEL), BF16), jax.ShapeDtypeStruct((t, D_MODEL), F32),
                   jax.ShapeDtypeStruct((t, D_MODEL), BF16)],
        compiler_params=_cparams(1), name="merge_fwd")(attn, sgu, gu, gu, x, w_pa, w_ps, w_out, g2)


def _merge_bwd(dx1b, gu, pa, ps, w_pa, w_ps, w_out):
    t = dx1b.shape[0]
    tm = min(t, 256)

    def body(d_ref, ga_ref, gb_ref, pa_ref, ps_ref, wpa, wps, wo, out_ref, dpa_ref, dps_ref, da_ref, dsg_ref):
        dm = lax.dot_general(d_ref[...], wo[...], (NT, ((), ())), preferred_element_type=F32)
        sa, sb = _sigmoid(ga_ref[...].astype(F32)), _sigmoid(gb_ref[...].astype(F32))
        dpa = (dm * sa).astype(BF16)
        dps = (dm * sb).astype(BF16)
        out_ref[:, 0:D_MODEL] = (dm * pa_ref[...].astype(F32) * sa * (1.0 - sa)).astype(BF16)
        out_ref[:, D_MODEL:2 * D_MODEL] = (dm * ps_ref[...].astype(F32) * sb * (1.0 - sb)).astype(BF16)
        out_ref[:, 2 * D_MODEL:GU_COLS] = jnp.zeros((tm, GU_COLS - 2 * D_MODEL), BF16)
        dpa_ref[...] = dpa
        dps_ref[...] = dps
        da_ref[...] = lax.dot_general(dpa, wpa[...], (NT, ((), ())), preferred_element_type=F32)
        dsg_ref[...] = lax.dot_general(dps, wps[...], (NT, ((), ())), preferred_element_type=F32)

    half = pl.BlockSpec((tm, ATTN_W), lambda i: (i, 0))
    full = pl.BlockSpec((tm, D_MODEL), lambda i: (i, 0))
    return pl.pallas_call(
        body, grid=(t // tm,),
        in_specs=[full, pl.BlockSpec((tm, D_MODEL), lambda i: (i, 0)),
                  pl.BlockSpec((tm, D_MODEL), lambda i: (i, 1)), full, full,
                  _full((ATTN_W, D_MODEL)), _full((SGU_W, D_MODEL)), _full((D_MODEL, D_MODEL))],
        out_specs=[pl.BlockSpec((tm, GU_COLS), lambda i: (i, 0)), full, full, half, half],
        out_shape=[jax.ShapeDtypeStruct((t, GU_COLS), BF16), jax.ShapeDtypeStruct((t, D_MODEL), BF16),
                   jax.ShapeDtypeStruct((t, D_MODEL), BF16), jax.ShapeDtypeStruct((t, ATTN_W), F32),
                   jax.ShapeDtypeStruct((t, SGU_W), F32)],
        compiler_params=_cparams(1), name="merge_bwd")(dx1b, gu, gu, pa, ps, w_pa, w_ps, w_out)


def _token_call(name, body, t, tm, ins, outs, reds=(), scratch=()):
    return pl.pallas_call(
        body, grid=(t // tm,), in_specs=[s for _, s in ins],
        out_specs=[o[2] for o in outs] + [_full(r) for r in reds],
        out_shape=[jax.ShapeDtypeStruct(o[0], o[1]) for o in outs] + [jax.ShapeDtypeStruct(r, F32) for r in reds],
        scratch_shapes=list(scratch), compiler_params=_cparams(1), name=name)(*[a for a, _ in ins])


def _rows_spec(tm, width):
    return pl.BlockSpec((tm, width), lambda i: (i, 0))


def _chips_spec(tm):
    return pl.BlockSpec((N_CHIPS, tm, FF_SHARD), lambda i: (0, i, 0))


def _zero_at_start(*refs):
    @pl.when(pl.program_id(0) == 0)
    def _():
        for r in refs:
            r[...] = jnp.zeros(r.shape, r.dtype)


def _ffn_fwd(h2, w_g, w_u):
    t = h2.shape[0]
    tm = min(t, 512)

    def body(h_ref, wg_ref, wu_ref, a_ref, b_ref, ff_ref):
        h = h_ref[...]
        for s in range(N_CHIPS):
            a = jnp.dot(h, wg_ref[s], preferred_element_type=F32)
            b = jnp.dot(h, wu_ref[s], preferred_element_type=F32)
            a_ref[s] = a.astype(BF16)
            b_ref[s] = b.astype(BF16)
            ff_ref[s] = (a * _sigmoid(a) * b).astype(BF16)

    shp = (N_CHIPS, t, FF_SHARD)
    w_spec = _resident((N_CHIPS, D_MODEL, FF_SHARD))
    return _token_call("ffn_fwd", body, t, tm, [(h2, _rows_spec(tm, D_MODEL)), (w_g, w_spec), (w_u, w_spec)],
                       [(shp, BF16, _chips_spec(tm))] * 3)


def _ffn_down_loss(ff, w_d, x1, tgt, gf):
    t = x1.shape[0]
    tm = min(t, 512)

    def body(ff_ref, wd_ref, x1_ref, tgt_ref, g_ref, dx2_ref, dx2b_ref, loss_ref, dgf_ref):
        _zero_at_start(loss_ref, dgf_ref)
        acc = jnp.dot(ff_ref[0], wd_ref[0], preferred_element_type=F32)
        for s in range(1, N_CHIPS):
            acc = acc + jnp.dot(ff_ref[s], wd_ref[s], preferred_element_type=F32)
        x2 = x1_ref[...] + acc
        g = g_ref[...]
        xhat, rr = _rms_stats(x2)
        diff = xhat * g - tgt_ref[...]
        rows = jnp.sum(diff * diff, axis=1, keepdims=True)
        loss_ref[...] += jnp.broadcast_to(jnp.sum(rows, axis=0, keepdims=True) * (0.5 / D_MODEL), (1, LANES))
        dy = diff * (1.0 / D_MODEL)
        dgf_ref[...] += jnp.sum(dy * xhat, axis=0, keepdims=True)
        dx2 = _rms_bwd(dy, xhat, rr, g)
        dx2_ref[...] = dx2
        dx2b_ref[...] = dx2.astype(BF16)

    row = _rows_spec(tm, D_MODEL)
    return _token_call("ffn_down_loss", body, t, tm,
                       [(ff, _chips_spec(tm)), (w_d, _resident((N_CHIPS, FF_SHARD, D_MODEL))), (x1, row), (tgt, row),
                        (gf, _full((1, D_MODEL)))],
                       [((t, D_MODEL), F32, row), ((t, D_MODEL), BF16, row)], reds=[(1, LANES), (1, D_MODEL)])


def _ffn_bwd_act(dx2b, w_d, a, b):
    t = dx2b.shape[0]
    tm = min(t, 512)

    def body(d_ref, wd_ref, a_ref, b_ref, da_ref, db_ref):
        d = d_ref[...]
        for s in range(N_CHIPS):
            dff = lax.dot_general(d, wd_ref[s], (NT, ((), ())), preferred_element_type=F32)
            av, bv = a_ref[s].astype(F32), b_ref[s].astype(F32)
            sg = _sigmoid(av)
            da_ref[s] = (dff * bv * (sg * (1.0 + av * (1.0 - sg)))).astype(BF16)
            db_ref[s] = (dff * (av * sg)).astype(BF16)

    shp = (N_CHIPS, t, FF_SHARD)
    return _token_call("ffn_bwd_act", body, t, tm,
                       [(dx2b, _rows_spec(tm, D_MODEL)), (w_d, _resident((N_CHIPS, FF_SHARD, D_MODEL))),
                        (a, _chips_spec(tm)), (b, _chips_spec(tm))],
                       [(shp, BF16, _chips_spec(tm))] * 2)


def _ffn_bwd_in(da, db, w_g, w_u, x1, dx2, g2):
    t = x1.shape[0]
    tm = min(t, 512)

    def body(da_ref, db_ref, wg_ref, wu_ref, x1_ref, dx2_ref, g_ref, dx1_ref, dx1b_ref, dg_ref):
        _zero_at_start(dg_ref)
        acc = None
        for s in range(N_CHIPS):
            part = (lax.dot_general(da_ref[s], wg_ref[s], (NT, ((), ())), preferred_element_type=F32)
                    + lax.dot_general(db_ref[s], wu_ref[s], (NT, ((), ())), preferred_element_type=F32))
            acc = part if acc is None else acc + part
        xhat, rr = _rms_stats(x1_ref[...])
        dg_ref[...] += jnp.sum(acc * xhat, axis=0, keepdims=True)
        dx1 = dx2_ref[...] + _rms_bwd(acc, xhat, rr, g_ref[...])
        dx1_ref[...] = dx1
        dx1b_ref[...] = dx1.astype(BF16)

    row = _rows_spec(tm, D_MODEL)
    w_spec = _resident((N_CHIPS, D_MODEL, FF_SHARD))
    return _token_call("ffn_bwd_in", body, t, tm,
                       [(da, _chips_spec(tm)), (db, _chips_spec(tm)), (w_g, w_spec), (w_u, w_spec), (x1, row), (dx2, row),
                        (g2, _full((1, D_MODEL)))],
                       [((t, D_MODEL), F32, row), ((t, D_MODEL), BF16, row)], reds=[(1, D_MODEL)])


def _in_proj_bwd(dgu, dqkvs, w_p, x, dx1, g1):
    t = x.shape[0]
    tile = min(t, TILE)
    tm = min(t, 1024)
    nat_cols = GU_COLS + GROUP_COLS

    dhs = []
    for g in (1, 2):
        col = (GU_COLS + g * GROUP_COLS) // GROUP_COLS

        def body_g(d_ref, w_ref, o_ref):
            o_ref[...] = lax.dot_general(d_ref[...], w_ref[...], (NT, ((), ())), preferred_element_type=F32)

        dh = _token_call(
            f"in_proj_bwd_g{g}", body_g, t, tm,
            [(dqkvs[g], _rows_spec(tm, GROUP_COLS)),
             (w_p, pl.BlockSpec((D_MODEL, GROUP_COLS), lambda i, col=col: (0, col), pipeline_mode=pl.Buffered(1)))],
            [((t, D_MODEL), F32, _rows_spec(tm, D_MODEL))])[0]
        dhs.append(dh.reshape(DILATIONS[g], t // DILATIONS[g], D_MODEL))

    def body(dgu_ref, dq0_ref, w_ref, x_ref, dx1_ref, g_ref, dh1_ref, dh2_ref, dx_ref, dg_ref, slab):
        _zero_at_start(dg_ref)
        dh = lax.dot_general(dgu_ref[...], w_ref[:, 0:GU_COLS], (NT, ((), ())), preferred_element_type=F32)
        dh = dh + lax.dot_general(dq0_ref[...], w_ref[:, GU_COLS:nat_cols], (NT, ((), ())), preferred_element_type=F32)
        dh = dh + _natural_from_group(slab, dh1_ref)
        dh = dh + _natural_from_group(slab, dh2_ref)
        xhat, rr = _rms_stats(x_ref[...])
        dg_ref[...] += jnp.sum(dh * xhat, axis=0, keepdims=True)
        dx_ref[...] = dx1_ref[...] + _rms_bwd(dh, xhat, rr, g_ref[...])

    row = _rows_spec(tile, D_MODEL)
    return _token_call(
        "in_proj_bwd", body, t, tile,
        [(dgu, _rows_spec(tile, GU_COLS)), (dqkvs[0], _rows_spec(tile, GROUP_COLS)),
         (w_p, pl.BlockSpec((D_MODEL, nat_cols), lambda i: (0, 0), pipeline_mode=pl.Buffered(1))),
         (x, row), (dx1, row), (g1, _full((1, D_MODEL))),
         (dhs[0], _group_spec(DILATIONS[1], tile, D_MODEL)), (dhs[1], _group_spec(DILATIONS[2], tile, D_MODEL))],
        [((t, D_MODEL), F32, row)], reds=[(1, D_MODEL)], scratch=[_slabs(tile, D_MODEL)])


def _epi_bf16(acc, e, o, r, ids):
    o[0][...] = acc.astype(BF16)


WGRAD_TK = 2048


def _wgrad_2d(name, a, b, tm, tn):
    t, k1 = a.shape
    n = b.shape[1]
    tk = min(t, WGRAD_TK)
    return _mm(name, (k1 // tm, n // tn, t // tk),
               [(a, pl.BlockSpec((tk, tm), lambda i, j, k: (k, i)), b, pl.BlockSpec((tk, tn), lambda i, j, k: (k, j)))],
               TN, (tm, tn), _epi_bf16, outs=[((k1, n), BF16, pl.BlockSpec((tm, tn), lambda i, j, k: (i, j)))])[0]


def _wgrad_in(hs, dgu, dqkvs):
    t = dgu.shape[0]
    tk = min(t, WGRAD_TK)
    dst = None
    parts = [(hs[0], dgu, 0)] + [(hs[g].reshape(t, D_MODEL), dqkvs[g], GU_COLS // GROUP_COLS + g) for g in range(3)]
    for n, (a, b, col0) in enumerate(parts):
        dst = _mm(f"wgrad_in_{n}", (1, b.shape[1] // GROUP_COLS, t // tk),
                  [(a, pl.BlockSpec((tk, D_MODEL), lambda i, j, k: (k, 0)), b,
                    pl.BlockSpec((tk, GROUP_COLS), lambda i, j, k: (k, j)))],
                  TN, (D_MODEL, GROUP_COLS), _epi_bf16,
                  extras=[] if dst is None else [(dst, pl.BlockSpec(memory_space=pl.ANY))],
                  outs=[((D_MODEL, IN_COLS), BF16, pl.BlockSpec((D_MODEL, GROUP_COLS), lambda i, j, k, col0=col0: (0, j + col0)))],
                  aliases=None if dst is None else {2: 0})[0]
    return dst


def _wgrad_ff_in(name, h2, da):
    t = h2.shape[0]
    tk = min(t, WGRAD_TK)
    return _mm(name, (N_CHIPS, 1, t // tk),
               [(h2, pl.BlockSpec((tk, D_MODEL), lambda i, j, k: (k, 0)),
                 da, pl.BlockSpec((None, tk, FF_SHARD), lambda i, j, k: (i, k, 0)))],
               TN, (D_MODEL, FF_SHARD), _epi_bf16,
               outs=[((N_CHIPS, D_MODEL, FF_SHARD), BF16, pl.BlockSpec((None, D_MODEL, FF_SHARD), lambda i, j, k: (i, 0, 0)))])[0]


def _wgrad_ff_down(ff, dx2b):
    t = dx2b.shape[0]
    tk = min(t, WGRAD_TK)
    return _mm("wgrad_ffn_down", (N_CHIPS, 1, t // tk),
               [(ff, pl.BlockSpec((None, tk, FF_SHARD), lambda i, j, k: (i, k, 0)),
                 dx2b, pl.BlockSpec((tk, D_MODEL), lambda i, j, k: (k, 0)))],
               TN, (FF_SHARD, D_MODEL), _epi_bf16,
               outs=[((N_CHIPS, FF_SHARD, D_MODEL), BF16, pl.BlockSpec((None, FF_SHARD, D_MODEL), lambda i, j, k: (i, 0, 0)))])[0]


def _local_step(x, pos_col, tgt, g1, ln_g, ln_b, w_s, b_s, g2, gf, w_p, late_weights, on_grads=None):
    tables = _rope_tables(pos_col)
    bias_exp = jnp.repeat(jnp.transpose(b_s), SGU_W // SGU_GROUPS, axis=1)

    hs = _norm_fwd(x, g1)
    gu, qkvs = _in_proj(hs, w_p, tables)
    os_, ls_ = [], []
    for g, dil in enumerate(DILATIONS):
        o, lse = _attn_fwd(qkvs[g], g, dil)
        os_.append(o)
        ls_.append(lse)
    attn = _combine_fwd(os_, ls_)
    sgu = _sgu_fwd(gu, ln_g, ln_b, w_s, bias_exp)
    w_pa, w_ps, w_out, w_g, w_u, w_d = late_weights(sgu)
    pa, ps, merged, x1, h2 = _merge_fwd(attn, sgu, gu, x, w_pa, w_ps, w_out, g2)
    a, b, ff = _ffn_fwd(h2, w_g, w_u)
    dx2, dx2b, loss, dgf = _ffn_down_loss(ff, w_d, x1, tgt, gf)

    da, db = _ffn_bwd_act(dx2b, w_d, a, b)
    dw_d = _wgrad_ff_down(ff, dx2b)
    dx1, dx1b, dg2 = _ffn_bwd_in(da, db, w_g, w_u, x1, dx2, g2)
    dw_g = _wgrad_ff_in("wgrad_ffn_gate", h2, da)
    dw_u = _wgrad_ff_in("wgrad_ffn_up", h2, db)

    dgu, dpa, dps, dattn, dsgu = _merge_bwd(dx1b, gu, pa, ps, w_pa, w_ps, w_out)
    dw_out = _wgrad_2d("wgrad_out", merged, dx1b, D_MODEL, D_MODEL)
    dw_pa = _wgrad_2d("wgrad_proj_attn", attn, dpa, ATTN_W, D_MODEL)
    dw_ps = _wgrad_2d("wgrad_proj_sgu", sgu, dps, SGU_W, D_MODEL)
    if on_grads is not None:
        ln_g = ln_g + on_grads(1, dict(w_proj_attn=dw_pa, w_proj_sgu=dw_ps, w_out=dw_out, w_ffn_gate=dw_g, w_ffn_up=dw_u,
                                       w_ffn_down=dw_d))[:, :SGU_W]
    dgu, dw_s, dbias, dln_g, dln_b = _sgu_bwd(dgu, gu, dsgu, ln_g, ln_b, w_s, bias_exp)
    dos, ccs = _combine_bwd(dattn, os_, ls_)
    dqkvs = [_attn_bwd(qkvs[g], dos[g], ccs[g], ls_[g], *tables[g], g, dil) for g, dil in enumerate(DILATIONS)]
    dw_p = _wgrad_in(hs, dgu, dqkvs)
    if on_grads is not None:
        g1 = g1 + on_grads(0, dict(w_in=dw_p))
    dx, dg1 = _in_proj_bwd(dgu, dqkvs, w_p, x, dx1, g1)

    db_s = jnp.transpose(dbias[:, ::SGU_W // SGU_GROUPS])
    small = dict(loss=loss, norm1_g=dg1, sgu_ln_g=dln_g, sgu_ln_b=dln_b, w_spatial=dw_s, b_spatial=db_s,
                 norm2_g=dg2, final_g=dgf)
    big = dict(w_in=dw_p, w_proj_attn=dw_pa, w_proj_sgu=dw_ps, w_out=dw_out, w_ffn_gate=dw_g, w_ffn_up=dw_u,
               w_ffn_down=dw_d)
    return dx, big, small


def _ew(name, fn, ins, out_dtypes):
    shp = ins[0].shape
    rows, cols = shp
    tr = next((cand for cand in (256, 352, 128) if rows % cand == 0 and rows > cand), rows)

    def body(*refs):
        res = fn(*[r[...] for r in refs[:len(ins)]])
        for o_ref, v in zip(refs[len(ins):], res):
            o_ref[...] = v.astype(o_ref.dtype)

    spec = pl.BlockSpec((tr, cols), lambda i: (i, 0))
    return pl.pallas_call(
        body, grid=(rows // tr,), in_specs=[spec] * len(ins), out_specs=[spec] * len(out_dtypes),
        out_shape=[jax.ShapeDtypeStruct(shp, d) for d in out_dtypes],
        compiler_params=_cparams(1), name=name)(*ins)


def _adamw_math(g, w, m, v):
    m = ADAM_B1 * m + (1.0 - ADAM_B1) * g
    v = ADAM_B2 * v + (1.0 - ADAM_B2) * (g * g)
    m_hat = m / (1.0 - ADAM_B1 ** ADAM_STEP)
    v_hat = v / (1.0 - ADAM_B2 ** ADAM_STEP)
    delta = -ADAM_LR * (m_hat / (jnp.sqrt(v_hat) + ADAM_EPS) + ADAM_WD * w)
    return delta, m, v


def _adamw(name, g, w, m, v):
    return _ew(name, lambda g_, w_, m_, v_: (g_,) + _adamw_math(g_, w_, m_, v_), [g, w, m, v], [F32] * 4)


VMEM_SPEC = pl.BlockSpec(memory_space=pltpu.VMEM)


def _for_row_chunks(rows, fn):
    ck = next(c for c in (64, 32, 16) if rows % c == 0)

    def step(i, carry):
        fn(pl.multiple_of(i * ck, ck), ck)
        return carry

    lax.fori_loop(0, rows // ck, step, 0)


def _place():
    x, y, c = lax.axis_index("x"), lax.axis_index("y"), lax.axis_index("c")
    chips = [(1 - x, y), (x, 1 - y), (1 - x, 1 - y)]
    return x, y, c, 2 * x + y, chips


def _rows(ref, start, size):
    if len(ref.shape) == 2:
        return ref.at[pl.ds(start, size), :]
    return ref.at[:, pl.ds(start, size), :]


def _comm_call(name, body, ins, out_shapes, scratch, n_remote):
    return pl.pallas_call(
        body, in_specs=[VMEM_SPEC] * len(ins), out_specs=[VMEM_SPEC] * len(out_shapes),
        out_shape=out_shapes,
        scratch_shapes=list(scratch) + [pltpu.SemaphoreType.DMA((n_remote,)), pltpu.SemaphoreType.DMA((n_remote,))],
        compiler_params=pltpu.CompilerParams(vmem_limit_bytes=VMEM_LIMIT), name=name)(*ins)


def _gather_weights(name, shards):
    nt = len(shards)

    def body(*refs):
        ins, outs = refs[:nt], refs[nt:2 * nt]
        send, recv = refs[2 * nt:]
        x, y, c, me, chips = _place()
        sibling = (x, y, 1 - c)
        firsts, passed, expects = [], [], []
        for t in range(nt):
            kh = ins[t].shape[0] // 2
            for j, chip in enumerate(chips):
                k = t * 3 + j
                theirs = 2 * chip[0] + chip[1]
                firsts.append(pltpu.make_async_remote_copy(
                    src_ref=_rows(ins[t], c * kh, kh), dst_ref=_rows(outs[t].at[me], c * kh, kh),
                    send_sem=send.at[k], recv_sem=recv.at[k], device_id=(*chip, c), device_id_type=MESH))
                landed = _rows(outs[t].at[theirs], c * kh, kh)
                expects.append(pltpu.make_async_remote_copy(
                    src_ref=landed, dst_ref=landed, send_sem=send.at[k], recv_sem=recv.at[k],
                    device_id=(*chip, c), device_id_type=MESH))
                passed.append(pltpu.make_async_remote_copy(
                    src_ref=landed, dst_ref=landed, send_sem=send.at[3 * nt + k], recv_sem=recv.at[3 * nt + k],
                    device_id=sibling, device_id_type=MESH))
        for cp in firsts:
            cp.start()
        for t in range(nt):
            mine = outs[t].at[me]

            def put(r0, ck, src=ins[t], dst=mine):
                dst[pl.ds(r0, ck), :] = src[pl.ds(r0, ck), :]

            _for_row_chunks(ins[t].shape[0], put)
        for k in range(3 * nt):
            expects[k].wait_recv()
            passed[k].start()
        for t in range(nt):
            kh = ins[t].shape[0] // 2
            for j, chip in enumerate(chips):
                k = t * 3 + j
                theirs = 2 * chip[0] + chip[1]
                other = _rows(outs[t].at[theirs], (1 - c) * kh, kh)
                pltpu.make_async_remote_copy(
                    src_ref=other, dst_ref=other, send_sem=send.at[3 * nt + k], recv_sem=recv.at[3 * nt + k],
                    device_id=sibling, device_id_type=MESH).wait_recv()
        for cp in firsts + passed:
            cp.wait_send()

    out_shapes = [jax.ShapeDtypeStruct((N_CHIPS,) + s.shape, s.dtype) for s in shards]
    return _comm_call(name, body, shards, out_shapes, [], 6 * nt)


def _pair_reduce(name, grads):
    nt = len(grads)

    def half(s):
        shp = list(s.shape)
        shp[-2] //= 2
        return tuple(shp)

    def body(*refs):
        ins, outs, got = refs[:nt], refs[nt:2 * nt], refs[2 * nt:3 * nt]
        send, recv = refs[3 * nt:]
        x, y, c, me, chips = _place()
        copies = []
        for t in range(nt):
            kh = ins[t].shape[-2] // 2
            rc = pltpu.make_async_remote_copy(
                src_ref=_rows(ins[t], (1 - c) * kh, kh), dst_ref=got[t], send_sem=send.at[t], recv_sem=recv.at[t],
                device_id=(x, y, 1 - c), device_id_type=MESH)
            rc.start()
            copies.append(rc)
        for t in range(nt):
            kh = ins[t].shape[-2] // 2
            copies[t].wait_recv()
            for lead in ([()] if len(ins[t].shape) == 2 else [(s,) for s in range(ins[t].shape[0])]):

                def add(r0, ck, lead=lead, src=ins[t], oth=got[t], dst=outs[t], kh=kh):
                    own = src[lead + (pl.ds(pl.multiple_of(c * kh + r0, ck), ck), slice(None))]
                    rows = lead + (pl.ds(r0, ck), slice(None))
                    dst[rows] = (own.astype(F32) + oth[rows].astype(F32)).astype(BF16)

                _for_row_chunks(kh, add)
        for rc in copies:
            rc.wait_send()

    shapes = [half(g) for g in grads]
    return _comm_call(name, body, grads, [jax.ShapeDtypeStruct(s, BF16) for s in shapes],
                      [pltpu.VMEM(s, BF16) for s in shapes], nt)


HBM_SPEC = pl.BlockSpec(memory_space=pltpu.HBM)
SEM_SPEC = pl.BlockSpec(memory_space=pltpu.SEMAPHORE)
DATAFLOW = pltpu.SideEffectType.DATAFLOW_SIDE_EFFECTING
TOKEN_SHAPE = (1, D_MODEL)


def _shard_cols(shape):
    return shape[2] if len(shape) == 3 else shape[1] // N_CHIPS


def _chip_piece(ref, j):
    if len(ref.shape) == 3:
        return ref.at[j]
    n4 = ref.shape[1] // N_CHIPS
    return ref.at[:, pl.ds(j * n4, n4)]


def _exchange_copies(sums, lands, send, recv):
    x, y, c, me, chips = _place()
    return [pltpu.make_async_remote_copy(
        src_ref=_chip_piece(sums[t], 2 * chip[0] + chip[1]), dst_ref=lands[t].at[j], send_sem=send.at[t * 3 + j],
        recv_sem=recv.at[t * 3 + j], device_id=(*chip, c), device_id_type=MESH)
        for t in range(len(sums)) for j, chip in enumerate(chips)]


def _gather_copies(shards, lands, send, recv):
    x, y, c, me, chips = _place()
    return [pltpu.make_async_remote_copy(
        src_ref=shards[t], dst_ref=lands[t].at[me], send_sem=send.at[t * 3 + j], recv_sem=recv.at[t * 3 + j],
        device_id=(*chip, c), device_id_type=MESH)
        for t in range(len(shards)) for j, chip in enumerate(chips)]


def _split_start(name, copies, srcs, land_shapes):
    nt = len(srcs)
    lands = [lax.empty(s, BF16) for s in land_shapes]

    def body(*refs):
        send, recv = refs[2 * nt], refs[2 * nt + 1]
        for cp in copies(refs[:nt], refs[nt:2 * nt], send, recv):
            cp.start()
        refs[-1][...] = jnp.zeros(TOKEN_SHAPE, F32)

    hbm = lambda a: pltpu.with_memory_space_constraint(a, pltpu.HBM)
    outs = pl.pallas_call(
        body, name=name,
        out_shape=[pltpu.SemaphoreType.DMA((3 * nt,)), pltpu.SemaphoreType.DMA((3 * nt,))]
        + [pltpu.HBM(s.shape, s.dtype) for s in srcs] + [pltpu.HBM(l.shape, l.dtype) for l in lands]
        + [jax.ShapeDtypeStruct(TOKEN_SHAPE, F32)],
        in_specs=[HBM_SPEC] * (2 * nt), out_specs=[SEM_SPEC, SEM_SPEC] + [HBM_SPEC] * (2 * nt) + [VMEM_SPEC],
        input_output_aliases={i: 2 + i for i in range(2 * nt)},
        compiler_params=pltpu.CompilerParams(has_side_effects=DATAFLOW))(*[hbm(a) for a in list(srcs) + lands])
    return outs[0], outs[1], outs[2:2 + nt], outs[2 + nt:2 + 2 * nt], outs[-1]


def _split_wait(name, copies, send, recv, srcs, lands, after):
    nt = len(srcs)

    def body(*refs):
        for cp in copies(refs[:nt], refs[nt:2 * nt], refs[2 * nt], refs[2 * nt + 1]):
            cp.wait_send()
            cp.wait_recv()

    outs = pl.pallas_call(
        body, name=name,
        out_shape=[pltpu.HBM(s.shape, s.dtype) for s in srcs] + [pltpu.HBM(l.shape, l.dtype) for l in lands],
        in_specs=[HBM_SPEC] * (2 * nt) + [SEM_SPEC, SEM_SPEC, pl.BlockSpec(memory_space=pl.ANY)],
        out_specs=[HBM_SPEC] * (2 * nt), input_output_aliases={i: i for i in range(2 * nt)},
        compiler_params=pltpu.CompilerParams(has_side_effects=DATAFLOW))(*srcs, *lands, send, recv, after)
    return outs[:nt], outs[nt:]


def _chip_sum(name, sums, lands):
    nt = len(sums)

    def body(*refs):
        ins, slots, outs = refs[:nt], refs[nt:2 * nt], refs[2 * nt:3 * nt]
        send, recv = refs[3 * nt:]
        x, y, c, me, chips = _place()
        sibling = (x, y, 1 - c)
        handed = []
        for t in range(nt):
            kh, n4 = ins[t].shape[-2], outs[t].shape[1]
            for jj in range(N_CHIPS):

                @pl.when(me == jj)
                def _(jj=jj, src=ins[t], slot=slots[t], dst=outs[t], kh=kh, n4=n4):
                    def add(r0, ck):
                        rows = pl.ds(r0, ck)
                        own = src[jj, rows, :] if len(src.shape) == 3 else src[rows, jj * n4:(jj + 1) * n4]
                        acc = ((own.astype(F32) + slot[0, rows, :].astype(F32)) + slot[1, rows, :].astype(F32)) \
                            + slot[2, rows, :].astype(F32)
                        dst[pl.ds(pl.multiple_of(c * kh + r0, ck), ck), :] = acc

                    _for_row_chunks(kh, add)

            rc = pltpu.make_async_remote_copy(
                src_ref=_rows(outs[t], c * kh, kh), dst_ref=_rows(outs[t], c * kh, kh), send_sem=send.at[t],
                recv_sem=recv.at[t], device_id=sibling, device_id_type=MESH)
            rc.start()
            handed.append(rc)
        for t in range(nt):
            kh = ins[t].shape[-2]
            other = _rows(outs[t], (1 - c) * kh, kh)
            pltpu.make_async_remote_copy(
                src_ref=other, dst_ref=other, send_sem=send.at[t], recv_sem=recv.at[t],
                device_id=sibling, device_id_type=MESH).wait_recv()
        for rc in handed:
            rc.wait_send()

    out_shapes = [jax.ShapeDtypeStruct((2 * s.shape[-2], _shard_cols(s.shape)), F32) for s in sums]
    return _comm_call(name, body, list(sums) + list(lands), out_shapes, [], nt)


VEC_SHAPE = (8, D_MODEL + LANES)
VEC_SLOTS = dict(norm1_g=(slice(0, 1), slice(0, D_MODEL)), norm2_g=(slice(1, 2), slice(0, D_MODEL)),
                 final_g=(slice(2, 3), slice(0, D_MODEL)), sgu_ln_g=(slice(3, 4), slice(0, SGU_W)),
                 sgu_ln_b=(slice(3, 4), slice(SGU_W, 2 * SGU_W)), b_spatial=(slice(0, 8), slice(D_MODEL, D_MODEL + LANES)),
                 loss=(slice(4, 5), slice(0, LANES)))
VEC_PARAMS = ("norm1_g", "norm2_g", "final_g", "sgu_ln_g", "sgu_ln_b", "b_spatial")
SMALL_PARAMS = VEC_PARAMS + ("w_spatial",)
W_SPATIAL_2D = (SGU_GROUPS * SGU_CHUNK, SGU_CHUNK)


def _small_step(partials, w, m, v):
    def shape2d(name):
        if name == "w_spatial":
            return W_SPATIAL_2D
        rows, cols = VEC_SLOTS[name]
        return (rows.stop - rows.start, cols.stop - cols.start)

    g_names = VEC_PARAMS + ("loss", "w_spatial")
    ins = [partials[n].reshape(shape2d(n)) for n in g_names]
    for src in (w, m, v):
        ins += [src[n].reshape(shape2d(n)) for n in SMALL_PARAMS]
    ng, npar = len(g_names), len(SMALL_PARAMS)

    def body(*refs):
        g_in = dict(zip(g_names, refs[:ng]))
        w_in, m_in, v_in = (dict(zip(SMALL_PARAMS, refs[ng + k * npar:ng + (k + 1) * npar])) for k in range(3))
        o0 = ng + 3 * npar
        g_out = dict(zip(g_names, refs[o0:o0 + ng]))
        d_out, m_out, v_out = (dict(zip(SMALL_PARAMS, refs[o0 + ng + k * npar:o0 + ng + (k + 1) * npar])) for k in range(3))
        vec, vec_pair, vec_slot, ws_pair, ws_slot, vw, vm, vv, send, recv = refs[o0 + ng + 3 * npar:]
        x, y, c, me, chips = _place()
        sibling = (x, y, 1 - c)

        def pack(dst, parts):
            dst[...] = jnp.zeros(VEC_SHAPE, F32)
            for n, ref in parts.items():
                if n in VEC_SLOTS:
                    dst[VEC_SLOTS[n]] = ref[...]

        pack(vec, g_in)
        copies = []

        def allreduce(k0, src, pair, slot):
            first = pltpu.make_async_remote_copy(src_ref=src, dst_ref=pair, send_sem=send.at[k0], recv_sem=recv.at[k0],
                                                 device_id=sibling, device_id_type=MESH)
            first.start()
            first.wait_recv()
            slot[me] = src[...] + pair[...]
            arrivals = []
            for j, chip in enumerate(chips):
                theirs = 2 * chip[0] + chip[1]
                rc = pltpu.make_async_remote_copy(src_ref=slot.at[me], dst_ref=slot.at[me], send_sem=send.at[k0 + 1 + j],
                                                  recv_sem=recv.at[k0 + 1 + j], device_id=(*chip, c), device_id_type=MESH)
                rc.start()
                arrivals.append(pltpu.make_async_remote_copy(
                    src_ref=slot.at[theirs], dst_ref=slot.at[theirs], send_sem=send.at[k0 + 1 + j],
                    recv_sem=recv.at[k0 + 1 + j], device_id=(*chip, c), device_id_type=MESH))
                copies.append(rc)
            copies.append(first)
            return arrivals

        arrivals = allreduce(0, vec, vec_pair, vec_slot) + allreduce(4, g_in["w_spatial"], ws_pair, ws_slot)
        pack(vw, w_in)
        pack(vm, m_in)
        pack(vv, v_in)
        for a in arrivals:
            a.wait_recv()
        for rc in copies:
            rc.wait_send()

        g_vec = ((vec_slot[0] + vec_slot[1]) + vec_slot[2]) + vec_slot[3]
        d_vec, m_vec, v_vec = _adamw_math(g_vec, vw[...], vm[...], vv[...])
        vec[...] = g_vec
        vw[...] = d_vec
        vm[...] = m_vec
        vv[...] = v_vec
        for n in VEC_PARAMS + ("loss",):
            g_out[n][...] = vec[VEC_SLOTS[n]]
        for n in VEC_PARAMS:
            d_out[n][...] = vw[VEC_SLOTS[n]]
            m_out[n][...] = vm[VEC_SLOTS[n]]
            v_out[n][...] = vv[VEC_SLOTS[n]]

        def spatial(r0, ck):
            rows = pl.ds(r0, ck)
            g = ((ws_slot[0, rows, :] + ws_slot[1, rows, :]) + ws_slot[2, rows, :]) + ws_slot[3, rows, :]
            d_, m_, v_ = _adamw_math(g, w_in["w_spatial"][rows, :], m_in["w_spatial"][rows, :], v_in["w_spatial"][rows, :])
            g_out["w_spatial"][rows, :] = g
            d_out["w_spatial"][rows, :] = d_
            m_out["w_spatial"][rows, :] = m_
            v_out["w_spatial"][rows, :] = v_

        _for_row_chunks(W_SPATIAL_2D[0], spatial)

    out_shapes = [jax.ShapeDtypeStruct(shape2d(n), F32) for n in g_names + SMALL_PARAMS * 3]
    outs = pl.pallas_call(
        body, in_specs=[VMEM_SPEC] * len(ins), out_specs=[VMEM_SPEC] * len(out_shapes), out_shape=out_shapes,
        scratch_shapes=[pltpu.VMEM(VEC_SHAPE, F32), pltpu.VMEM(VEC_SHAPE, F32), pltpu.VMEM((N_CHIPS,) + VEC_SHAPE, F32),
                        pltpu.VMEM(W_SPATIAL_2D, F32), pltpu.VMEM((N_CHIPS,) + W_SPATIAL_2D, F32),
                        pltpu.VMEM(VEC_SHAPE, F32), pltpu.VMEM(VEC_SHAPE, F32), pltpu.VMEM(VEC_SHAPE, F32),
                        pltpu.SemaphoreType.DMA((8,)), pltpu.SemaphoreType.DMA((8,))],
        name="small_params_step")(*ins)
    grads = dict(zip(g_names, outs[:ng]))
    rest = [dict(zip(SMALL_PARAMS, outs[ng + k * npar:ng + (k + 1) * npar])) for k in range(3)]
    return grads, rest[0], rest[1], rest[2]


BIG = ("w_in", "w_proj_attn", "w_proj_sgu", "w_out", "w_ffn_gate", "w_ffn_up", "w_ffn_down")
COMM_GROUPS = (("w_in",), ("w_proj_attn", "w_proj_sgu", "w_out", "w_ffn_gate", "w_ffn_up", "w_ffn_down"))
WEIGHTS = ("norm1_g", "w_in", "sgu_ln_g", "sgu_ln_b", "w_spatial", "b_spatial", "w_proj_attn", "w_proj_sgu", "w_out",
           "norm2_g", "w_ffn_gate", "w_ffn_up", "w_ffn_down", "final_g")


def _cols_from_chips(g):
    return jnp.transpose(g, (1, 0, 2)).reshape(g.shape[1], N_CHIPS * g.shape[2])


def _permute_cols(w, perm):
    return jnp.concatenate([w[:, 512 * b:512 * (b + 1)] for b in perm], axis=1)


def kernel(x, positions, norm1_g, w_in, sgu_ln_g, sgu_ln_b, w_spatial, b_spatial, w_proj_attn, w_proj_sgu, w_out, norm2_g, w_ffn_gate, w_ffn_up, w_ffn_down, final_g, loss_target, m_norm1_g, m_w_in, m_sgu_ln_g, m_sgu_ln_b, m_w_spatial, m_b_spatial, m_w_proj_attn, m_w_proj_sgu, m_w_out, m_norm2_g, m_w_ffn_gate, m_w_ffn_up, m_w_ffn_down, m_final_g, v_norm1_g, v_w_in, v_sgu_ln_g, v_sgu_ln_b, v_w_spatial, v_b_spatial, v_w_proj_attn, v_w_proj_sgu, v_w_out, v_norm2_g, v_w_ffn_gate, v_w_ffn_up, v_w_ffn_down, v_final_g):
    w = dict(norm1_g=norm1_g, w_in=w_in, sgu_ln_g=sgu_ln_g, sgu_ln_b=sgu_ln_b, w_spatial=w_spatial, b_spatial=b_spatial,
             w_proj_attn=w_proj_attn, w_proj_sgu=w_proj_sgu, w_out=w_out, norm2_g=norm2_g, w_ffn_gate=w_ffn_gate,
             w_ffn_up=w_ffn_up, w_ffn_down=w_ffn_down, final_g=final_g)
    m = dict(norm1_g=m_norm1_g, w_in=m_w_in, sgu_ln_g=m_sgu_ln_g, sgu_ln_b=m_sgu_ln_b, w_spatial=m_w_spatial,
             b_spatial=m_b_spatial, w_proj_attn=m_w_proj_attn, w_proj_sgu=m_w_proj_sgu, w_out=m_w_out, norm2_g=m_norm2_g,
             w_ffn_gate=m_w_ffn_gate, w_ffn_up=m_w_ffn_up, w_ffn_down=m_w_ffn_down, final_g=m_final_g)
    v = dict(norm1_g=v_norm1_g, w_in=v_w_in, sgu_ln_g=v_sgu_ln_g, sgu_ln_b=v_sgu_ln_b, w_spatial=v_w_spatial,
             b_spatial=v_b_spatial, w_proj_attn=v_w_proj_attn, w_proj_sgu=v_w_proj_sgu, w_out=v_w_out, norm2_g=v_norm2_g,
             w_ffn_gate=v_w_ffn_gate, w_ffn_up=v_w_ffn_up, w_ffn_down=v_w_ffn_down, final_g=v_final_g)
    t = x.shape[1]

    shards = {n: _ew(f"cast_{n}", lambda a: (a,), [w[n][0]], [BF16])[0] for n in BIG}
    late = COMM_GROUPS[1]
    gath_in, late_shards = lax.optimization_barrier(
        (_gather_weights("gather_weights_0", [shards["w_in"]])[0], [shards[n] for n in late]))
    w_p = _permute_cols(_cols_from_chips(gath_in), PERM)
    send, recv, late_shards, lands, token = _split_start(
        "gather_start_1", _gather_copies, late_shards, [(N_CHIPS,) + s.shape for s in late_shards])

    def late_weights(after):
        srcs, filled = _split_wait("gather_wait_1", _gather_copies, send, recv, late_shards, lands, after)
        me = 2 * lax.axis_index("x") + lax.axis_index("y")
        gath = {n: lax.dynamic_update_slice(f, s[None], (me, 0, 0)) for n, f, s in zip(late, filled, srcs)}
        return (_cols_from_chips(gath["w_proj_attn"]), _cols_from_chips(gath["w_proj_sgu"]),
                gath["w_out"].reshape(D_MODEL, D_MODEL), gath["w_ffn_gate"], gath["w_ffn_up"], gath["w_ffn_down"])

    exchanges = {}

    def on_grads(i, partials):
        if "w_in" in partials:
            partials["w_in"] = _permute_cols(partials["w_in"], INV_PERM)
        if "w_out" in partials:
            partials["w_out"] = partials["w_out"].reshape(N_CHIPS, D_MODEL // N_CHIPS, D_MODEL)
        sums = _pair_reduce(f"rs_pair_reduce_{i}", [partials[n] for n in COMM_GROUPS[i]])
        *exchanges[i], started = _split_start(
            f"rs_exchange_start_{i}", _exchange_copies, sums, [(3, s.shape[-2], _shard_cols(s.shape)) for s in sums])
        return started

    dx, _, small = _local_step(
        x[0], positions.reshape(t, 1), loss_target[0], norm1_g + token, sgu_ln_g, sgu_ln_b, w_spatial[0], b_spatial[0],
        norm2_g, final_g.reshape(1, D_MODEL), w_p, late_weights, on_grads=on_grads)

    grads = {}
    for i in (1, 0):
        sums, filled = _split_wait(f"rs_exchange_wait_{i}", _exchange_copies, *exchanges[i], dx)
        grads.update(zip(COMM_GROUPS[i], _chip_sum(f"rs_chip_sum_{i}", sums, filled)))

    delta, new_m, new_v = {}, {}, {}
    for n in BIG:
        shp = w[n].shape
        g_, d_, m_, v_ = _adamw(f"adamw_{n}", grads[n], w[n][0], m[n][0], v[n][0])
        grads[n], delta[n], new_m[n], new_v[n] = g_.reshape(shp), d_.reshape(shp), m_.reshape(shp), v_.reshape(shp)

    g_s, d_s, m_s, v_s = _small_step(small, w, m, v)
    loss = g_s["loss"][0, 0]
    for n in SMALL_PARAMS:
        shp = w[n].shape
        grads[n], delta[n], new_m[n], new_v[n] = (a[n].reshape(shp) for a in (g_s, d_s, m_s, v_s))

    return (loss, dx.reshape(x.shape), *[grads[n] for n in WEIGHTS], *[delta[n] for n in WEIGHTS],
            *[new_m[n] for n in WEIGHTS], *[new_v[n] for n in WEIGHTS])
```

```python
import functools

import numpy as np
import jax
import jax.numpy as jnp
from jax import lax
from jax.experimental import pallas as pl
from jax.experimental.pallas import tpu as pltpu

F32, BF16 = jnp.float32, jnp.bfloat16
MESH = pl.DeviceIdType.MESH

D_MODEL = 1024
HEAD_DIM = 64
ATTN_W = 512
DILATIONS = (1, 4, 16)
BLK = 128
ROPE_DIM = 16
ROPE_THETA = 500000.0
SGU_W = 512
SGU_CHUNK = 128
SGU_GROUPS = 8
D_FF = 2816
N_CHIPS = 4
FF_SHARD = D_FF // N_CHIPS
IN_COLS = 7680
EPS = 1e-6
NEG = -1e30
LANES = 128
VMEM_LIMIT = 52 * 1024 * 1024

ADAM_LR, ADAM_B1, ADAM_B2, ADAM_EPS, ADAM_WD, ADAM_STEP = 0.001, 0.9, 0.999, 1e-08, 0.01, 10

QKV_BLOCKS = 9


def _w_in_block(part, g):
    return part * len(DILATIONS) + g


def _cparams(ngrid):
    return pltpu.CompilerParams(dimension_semantics=("arbitrary",) * ngrid, vmem_limit_bytes=VMEM_LIMIT)


def _full(shape):
    return pl.BlockSpec(shape, lambda *_: (0,) * len(shape))


def _resident(shape):
    return pl.BlockSpec(shape, lambda *_: (0,) * len(shape), pipeline_mode=pl.Buffered(1))


NN = ((1,), (0,))
NT = ((1,), (1,))
TN = ((0,), (0,))


def _mm(name, grid, pairs, dims, acc_shape, epi, *, extras=(), outs=(), reds=(), aliases=None):
    nk = grid[-1]
    npair, nex, nout, nred = len(pairs), len(extras), len(outs), len(reds)

    def body(*refs):
        a_refs = refs[:npair]
        b_refs = refs[npair:2 * npair]
        p0 = 2 * npair
        e_refs = refs[p0:p0 + nex]
        o_refs = refs[p0 + nex:p0 + nex + nout]
        r_refs = refs[p0 + nex + nout:p0 + nex + nout + nred]
        ids = [pl.program_id(a) for a in range(len(grid))]
        k = ids[-1]
        if nred:
            first = ids[0] == 0
            for v in ids[1:]:
                first = first & (v == 0)

            @pl.when(first)
            def _():
                for r in r_refs:
                    r[...] = jnp.zeros(r.shape, r.dtype)

        part = None
        for a_ref, b_ref in zip(a_refs, b_refs):
            d = lax.dot_general(a_ref[...], b_ref[...], (dims, ((), ())), preferred_element_type=F32)
            part = d if part is None else part + d
        if nk == 1:
            epi(part, e_refs, o_refs, r_refs, ids)
        else:
            acc_ref = refs[-1]

            @pl.when(k == 0)
            def _():
                acc_ref[...] = part

            @pl.when(k > 0)
            def _():
                acc_ref[...] += part

            @pl.when(k == nk - 1)
            def _():
                epi(acc_ref[...], e_refs, o_refs, r_refs, ids)

    in_specs = [p[1] for p in pairs] + [p[3] for p in pairs] + [e[1] for e in extras]
    args = [p[0] for p in pairs] + [p[2] for p in pairs] + [e[0] for e in extras]
    out_shape = [jax.ShapeDtypeStruct(o[0], o[1]) for o in outs] + [jax.ShapeDtypeStruct(r, F32) for r in reds]
    out_specs = [o[2] for o in outs] + [_full(r) for r in reds]
    scratch_shapes = [pltpu.VMEM(acc_shape, F32)] if nk > 1 else []
    return pl.pallas_call(
        body, grid=grid, in_specs=in_specs, out_specs=out_specs, out_shape=out_shape, scratch_shapes=scratch_shapes,
        input_output_aliases=aliases or {}, compiler_params=_cparams(len(grid)), name=name)(*args)


def _rope(v, cos_t, sin_t):
    half = ROPE_DIM // 2
    first = (lax.broadcasted_iota(jnp.int32, cos_t.shape, 1) % HEAD_DIM) < half
    outs = []
    for cs in range(v.shape[1] // LANES):
        x = v[:, cs * LANES:(cs + 1) * LANES]
        partner = jnp.where(first, pltpu.roll(x, LANES - half, axis=1), pltpu.roll(x, half, axis=1))
        outs.append(x * cos_t + partner * sin_t)
    return outs[0] if len(outs) == 1 else jnp.concatenate(outs, axis=1)


def _spread_heads(v2, upper):
    other = pltpu.roll(v2, HEAD_DIM, axis=1)
    h0 = jnp.where(upper, other, v2)
    h1 = jnp.where(upper, v2, other)
    return jnp.concatenate([jnp.concatenate([h0, h0], axis=1), jnp.concatenate([h1, h1], axis=1)], axis=0)


def _sigmoid(v):
    return 0.5 * jnp.tanh(0.5 * v) + 0.5


def _rms_stats(v):
    r = lax.rsqrt(jnp.mean(v * v, axis=-1, keepdims=True) + EPS)
    return v * r, r


def _rms_bwd(dy, xhat, r, g):
    dxh = dy * g
    return r * (dxh - xhat * jnp.mean(dxh * xhat, axis=-1, keepdims=True))


def _head_sum_matrix():
    idx = np.arange(ATTN_W) // HEAD_DIM
    return jnp.asarray((idx[:, None] == idx[None, :]).astype(np.float32), dtype=BF16)


def _group_sum(v, e):
    hi = v.astype(BF16)
    lo = (v - hi.astype(F32)).astype(BF16)
    return jnp.dot(hi, e, preferred_element_type=F32) + jnp.dot(lo, e, preferred_element_type=F32)


TILE = 512


def _to_slabs(slab_ref, v):
    for cs in range(slab_ref.shape[0]):
        slab_ref[cs] = v[:, cs * LANES:(cs + 1) * LANES]


def _from_slabs(slab_ref):
    return jnp.concatenate([slab_ref[cs] for cs in range(slab_ref.shape[0])], axis=1)


def _class_rows(slab_ref, r, dil):
    n = slab_ref.shape[1] // dil
    return jnp.concatenate([slab_ref.at[cs][pl.ds(r, n, stride=dil), :] for cs in range(slab_ref.shape[0])], axis=1)


def _put_class_rows(slab_ref, r, dil, v):
    n = slab_ref.shape[1] // dil
    for cs in range(slab_ref.shape[0]):
        slab_ref.at[cs][pl.ds(r, n, stride=dil), :] = v[:, cs * LANES:(cs + 1) * LANES]


def _natural_from_group(slab_ref, grp_ref):
    dil = grp_ref.shape[0]
    for r in range(dil):
        _put_class_rows(slab_ref, r, dil, grp_ref[r].astype(F32))
    return _from_slabs(slab_ref)


def _group_from_natural(slab_ref, grp_ref, v):
    dil = grp_ref.shape[0]
    _to_slabs(slab_ref, v)
    for r in range(dil):
        grp_ref[r] = _class_rows(slab_ref, r, dil).astype(grp_ref.dtype)


def _group_spec(dil, tile, width):
    return pl.BlockSpec((dil, tile // dil, width), lambda i, *_: (0, i, 0))


def _slabs(tile, width):
    return pltpu.VMEM((width // LANES, tile, LANES), F32)


def _rope_consts():
    lane = np.arange(LANES) % HEAD_DIM
    fi = lane % (ROPE_DIM // 2)
    invf = np.where(lane < ROPE_DIM, ROPE_THETA ** (-(2.0 * fi) / ROPE_DIM), 0.0)
    sgn = np.where(lane < ROPE_DIM // 2, -1.0, np.where(lane < ROPE_DIM, 1.0, 0.0))
    return (jnp.asarray(invf.astype(np.float32)).reshape(1, LANES), jnp.asarray(sgn.astype(np.float32)).reshape(1, LANES))


def _rope_tables(pos_col):
    t = pos_col.shape[0]
    tile = min(t, TILE)
    invf, sgn = _rope_consts()

    def body(p_ref, f_ref, s_ref, c0, s0, c1, s1, c2, s2, slab_c, slab_s):
        ang = p_ref[...].astype(F32) * f_ref[...]
        cos, sin = jnp.cos(ang), jnp.sin(ang) * s_ref[...]
        c0[...] = cos
        s0[...] = sin
        _group_from_natural(slab_c, c1, cos)
        _group_from_natural(slab_s, s1, sin)
        for r in range(DILATIONS[2]):
            c2[r] = _class_rows(slab_c, r, DILATIONS[2])
            s2[r] = _class_rows(slab_s, r, DILATIONS[2])

    nat = pl.BlockSpec((tile, LANES), lambda i: (i, 0))
    specs, shapes = [nat, nat], [(t, LANES)] * 2
    for d in DILATIONS[1:]:
        specs += [_group_spec(d, tile, LANES)] * 2
        shapes += [(d, t // d, LANES)] * 2
    outs = pl.pallas_call(
        body, grid=(t // tile,),
        in_specs=[pl.BlockSpec((tile, 1), lambda i: (i, 0)), _full((1, LANES)), _full((1, LANES))],
        out_specs=specs, out_shape=[jax.ShapeDtypeStruct(s, F32) for s in shapes],
        scratch_shapes=[_slabs(tile, LANES)] * 2,
        compiler_params=_cparams(1), name="rope_tables")(pos_col, invf, sgn)
    return [(outs[2 * g].reshape(t, LANES), outs[2 * g + 1].reshape(t, LANES)) for g in range(len(DILATIONS))]


def _norm_fwd(x, g):
    t = x.shape[0]
    tile = min(t, TILE)

    def body(x_ref, g_ref, h0_ref, h1_ref, h2_ref, slab):
        xhat, _ = _rms_stats(x_ref[...])
        hn = xhat * g_ref[...]
        h0_ref[...] = hn.astype(BF16)
        _group_from_natural(slab, h1_ref, hn)
        for r in range(DILATIONS[2]):
            h2_ref[r] = _class_rows(slab, r, DILATIONS[2]).astype(BF16)

    nat = pl.BlockSpec((tile, D_MODEL), lambda i: (i, 0))
    return pl.pallas_call(
        body, grid=(t // tile,),
        in_specs=[nat, _full((1, D_MODEL))],
        out_specs=[nat] + [_group_spec(d, tile, D_MODEL) for d in DILATIONS[1:]],
        out_shape=[jax.ShapeDtypeStruct((t, D_MODEL), BF16)]
        + [jax.ShapeDtypeStruct((d, t // d, D_MODEL), BF16) for d in DILATIONS[1:]],
        scratch_shapes=[_slabs(tile, D_MODEL)],
        compiler_params=_cparams(1), name="norm1_fwd")(x, g)


GU_COLS = 3072
GROUP_COLS = 1536
GU_HALF = GU_COLS // 2


def _w_in_spec(width, block):
    return pl.BlockSpec((D_MODEL, width), lambda i: (0, block), pipeline_mode=pl.Buffered(1))


def _gu_w_specs():
    first = QKV_BLOCKS * ATTN_W // GU_HALF
    return [_w_in_spec(GU_HALF, first), _w_in_spec(GU_HALF, first + 1)]


def _group_w_specs(g):
    return [_w_in_spec(ATTN_W, _w_in_block(part, g)) for part in range(3)]


def _in_proj(hs, w_in, tables):
    t = hs[0].shape[0]
    tm = min(t, 1024)

    def body_gu(h_ref, w0_ref, w1_ref, o_ref):
        h = h_ref[...]
        o_ref[:, 0:GU_HALF] = jnp.dot(h, w0_ref[...], preferred_element_type=F32).astype(BF16)
        o_ref[:, GU_HALF:] = jnp.dot(h, w1_ref[...], preferred_element_type=F32).astype(BF16)

    gu = _token_call("in_proj_gates_uv", body_gu, t, tm,
                     [(hs[0], _rows_spec(tm, D_MODEL))] + [(w_in, s) for s in _gu_w_specs()],
                     [((t, GU_COLS), BF16, _rows_spec(tm, GU_COLS))])[0]

    qkvs = []
    for g in range(len(DILATIONS)):

        def body_qkv(h_ref, wq_ref, wk_ref, wv_ref, cos_ref, sin_ref, o_ref):
            h = h_ref[...]
            cos_w, sin_w = cos_ref[...], sin_ref[...]
            q = jnp.dot(h, wq_ref[...], preferred_element_type=F32)
            o_ref[:, 0:ATTN_W] = (_rope(q, cos_w, sin_w) * HEAD_DIM ** -0.5).astype(BF16)
            k = jnp.dot(h, wk_ref[...], preferred_element_type=F32)
            o_ref[:, ATTN_W:2 * ATTN_W] = _rope(k, cos_w, sin_w).astype(BF16)
            o_ref[:, 2 * ATTN_W:] = jnp.dot(h, wv_ref[...], preferred_element_type=F32).astype(BF16)

        cos_t, sin_t = tables[g]
        qkvs.append(_token_call(
            f"in_proj_qkv_g{g}", body_qkv, t, tm,
            [(hs[g].reshape(t, D_MODEL), _rows_spec(tm, D_MODEL))] + [(w_in, s) for s in _group_w_specs(g)]
            + [(cos_t, _rows_spec(tm, LANES)), (sin_t, _rows_spec(tm, LANES))],
            [((t, GROUP_COLS), BF16, _rows_spec(tm, GROUP_COLS))])[0])
    return gu, qkvs


def _attn_masks(n):
    row = lax.broadcasted_iota(jnp.int32, (2 * BLK, 2 * BLK), 0) % BLK
    col = lax.broadcasted_iota(jnp.int32, (2 * BLK, 2 * BLK), 1)
    diff = BLK + row - col
    valid = (diff >= 0) & (diff <= BLK) & ((col >= BLK) | (n > 0))
    upper = lax.broadcasted_iota(jnp.int32, (BLK, LANES), 1) >= HEAD_DIM
    return valid, upper


def _stack_heads(v2, upper):
    zero = jnp.zeros_like(v2)
    return jnp.concatenate([jnp.where(upper, zero, v2), jnp.where(upper, v2, zero)], axis=0)


def _unstack_heads(v, upper):
    return jnp.where(upper, v[BLK:], v[:BLK])


def _attn_fwd(qkv, g, dil):
    t = qkv.shape[0]
    length = t // dil
    nb = length // BLK
    view = qkv.reshape(dil, length, GROUP_COLS)

    def body(q_ref, kc_ref, kp_ref, vc_ref, vp_ref, o_ref, l_ref):
        n = pl.program_id(1)
        valid, upper = _attn_masks(n)
        for p in range(ATTN_W // LANES):
            sl = slice(p * LANES, (p + 1) * LANES)
            qs = _stack_heads(q_ref[:, sl], upper)
            k2 = jnp.concatenate([kp_ref[:, sl], kc_ref[:, sl]], axis=0)
            v2 = jnp.concatenate([vp_ref[:, sl], vc_ref[:, sl]], axis=0)
            s = lax.dot_general(qs, k2, (NT, ((), ())), preferred_element_type=F32)
            s = jnp.where(valid, s, NEG)
            m = jnp.max(s, axis=1, keepdims=True)
            pe = jnp.exp(s - m)
            den = jnp.sum(pe, axis=1, keepdims=True)
            o = jnp.dot(pe.astype(BF16), v2, preferred_element_type=F32) / den
            lse = jnp.broadcast_to(m + jnp.log(den), (2 * BLK, LANES))
            o_ref[:, sl] = _unstack_heads(o, upper)
            l_ref[:, sl] = _unstack_heads(lse, upper)

    cur = lambda part: pl.BlockSpec((None, BLK, ATTN_W), lambda r, n: (r, n, part))
    prev = lambda part: pl.BlockSpec((None, BLK, ATTN_W), lambda r, n: (r, jnp.maximum(n - 1, 0), part))
    out_spec = pl.BlockSpec((None, BLK, ATTN_W), lambda r, n: (r, n, 0))
    return pl.pallas_call(
        body, grid=(dil, nb),
        in_specs=[cur(0), cur(1), prev(1), cur(2), prev(2)],
        out_specs=[out_spec, out_spec],
        out_shape=[jax.ShapeDtypeStruct((dil, length, ATTN_W), F32)] * 2,
        compiler_params=_cparams(2), name=f"attn_fwd_g{g}")(view, view, view, view, view)


def _alphas(l0, l1, l2):
    m = jnp.maximum(jnp.maximum(l0, l1), l2)
    e0, e1, e2 = jnp.exp(l0 - m), jnp.exp(l1 - m), jnp.exp(l2 - m)
    inv = 1.0 / (e0 + e1 + e2)
    return e0 * inv, e1 * inv, e2 * inv


def _natural_group_values(o_refs, l_refs, slabs):
    os_ = [o_refs[0][0]] + [_natural_from_group(slabs[2 * g - 2], o_refs[g]) for g in (1, 2)]
    ls_ = [l_refs[0][0]] + [_natural_from_group(slabs[2 * g - 1], l_refs[g]) for g in (1, 2)]
    return os_, ls_


def _combine_fwd(os_, ls_):
    t = os_[0].shape[1]
    tile = min(t, TILE)

    def body(o0, o1, o2, l0, l1, l2, a_ref, *slabs):
        ov, lv = _natural_group_values((o0, o1, o2), (l0, l1, l2), slabs)
        a0, a1, a2 = _alphas(*lv)
        a_ref[...] = (a0 * ov[0] + a1 * ov[1] + a2 * ov[2]).astype(BF16)

    specs = [_group_spec(d, tile, ATTN_W) for d in DILATIONS]
    return pl.pallas_call(
        body, grid=(t // tile,), in_specs=specs * 2, out_specs=pl.BlockSpec((tile, ATTN_W), lambda i: (i, 0)),
        out_shape=jax.ShapeDtypeStruct((t, ATTN_W), BF16),
        scratch_shapes=[_slabs(tile, ATTN_W)] * 4,
        compiler_params=_cparams(1), name="combine_fwd")(*os_, *ls_)


def _combine_bwd(dattn, os_, ls_):
    t = dattn.shape[0]
    tile = min(t, TILE)
    e = _head_sum_matrix()

    def body(d_ref, o0, o1, o2, l0, l1, l2, e_ref, do0, do1, do2, c0, c1, c2, *slabs):
        ov, lv = _natural_group_values((o0, o1, o2), (l0, l1, l2), slabs)
        alphas = _alphas(*lv)
        d = d_ref[...]
        attn = alphas[0] * ov[0] + alphas[1] * ov[1] + alphas[2] * ov[2]
        s = _group_sum(d * attn, e_ref[...])
        do0[0] = (alphas[0] * d).astype(BF16)
        c0[0] = -alphas[0] * s
        for g, do_ref, c_ref in ((1, do1, c1), (2, do2, c2)):
            _group_from_natural(slabs[2 * g - 2], do_ref, alphas[g] * d)
            _group_from_natural(slabs[2 * g - 1], c_ref, -alphas[g] * s)

    specs = [_group_spec(d, tile, ATTN_W) for d in DILATIONS]
    shapes = [(d, t // d, ATTN_W) for d in DILATIONS]
    outs = pl.pallas_call(
        body, grid=(t // tile,),
        in_specs=[pl.BlockSpec((tile, ATTN_W), lambda i: (i, 0))] + specs * 2 + [_full((ATTN_W, ATTN_W))],
        out_specs=specs * 2,
        out_shape=[jax.ShapeDtypeStruct(s, BF16) for s in shapes] + [jax.ShapeDtypeStruct(s, F32) for s in shapes],
        scratch_shapes=[_slabs(tile, ATTN_W)] * 4,
        compiler_params=_cparams(1), name="combine_bwd")(dattn, *os_, *ls_, e)
    return outs[:3], outs[3:]


def _attn_bwd(qkv, do, cc, lse, cos_t, sin_t, g, dil):
    t = qkv.shape[0]
    length = t // dil
    nb = length // BLK
    qkv_v = qkv.reshape(dil, length, GROUP_COLS)
    cos_v, sin_v = (a.reshape(dil, length, LANES) for a in (cos_t, sin_t))
    scale = HEAD_DIM ** -0.5

    def body(q_ref, kc_ref, kp_ref, vc_ref, vp_ref, do_ref, c_ref, l_ref, cosc, sinc, cosp, sinp,
             out_ref, dq_s, dk_s, dv_s):
        n = pl.program_id(1)
        valid, upper = _attn_masks(n)

        @pl.when(n < nb)
        def _():
            cos_c, sin_c = cosc[...], sinc[...]
            cos_p, sin_p = cosp[...], sinp[...]
            dq_parts, dkp_parts, dkc_parts, dvp_parts, dvc_parts = [], [], [], [], []
            for p in range(ATTN_W // LANES):
                sl = slice(p * LANES, (p + 1) * LANES)
                qs = _stack_heads(q_ref[:, sl], upper)
                dos = _stack_heads(do_ref[:, sl], upper)
                k2 = jnp.concatenate([kp_ref[:, sl], kc_ref[:, sl]], axis=0)
                v2 = jnp.concatenate([vp_ref[:, sl], vc_ref[:, sl]], axis=0)
                l_col = _spread_heads(l_ref[:, sl], upper)
                c_col = _spread_heads(c_ref[:, sl], upper)
                s = lax.dot_general(qs, k2, (NT, ((), ())), preferred_element_type=F32)
                pe = jnp.exp(jnp.where(valid, s, NEG) - l_col)
                dpv = lax.dot_general(dos, v2, (NT, ((), ())), preferred_element_type=F32)
                ds = (pe * (dpv + c_col)).astype(BF16)
                dq2 = _unstack_heads(jnp.dot(ds, k2, preferred_element_type=F32), upper)
                dk2 = lax.dot_general(ds, qs, (TN, ((), ())), preferred_element_type=F32)
                dv2 = lax.dot_general(pe.astype(BF16), dos, (TN, ((), ())), preferred_element_type=F32)
                dq_parts.append(dq2)
                dkp_parts.append(dk2[:BLK])
                dkc_parts.append(dk2[BLK:])
                dvp_parts.append(dv2[:BLK])
                dvc_parts.append(dv2[BLK:])
            dq = _rope(jnp.concatenate(dq_parts, axis=1) * scale, cos_c, -sin_c)
            dkc = _rope(jnp.concatenate(dkc_parts, axis=1), cos_c, -sin_c)
            dkp = _rope(jnp.concatenate(dkp_parts, axis=1), cos_p, -sin_p)
            dvp = jnp.concatenate(dvp_parts, axis=1)
            dvc = jnp.concatenate(dvc_parts, axis=1)

            @pl.when(n > 0)
            def _():
                out_ref[:, 0:ATTN_W] = dq_s[...].astype(BF16)
                out_ref[:, ATTN_W:2 * ATTN_W] = (dk_s[...] + dkp).astype(BF16)
                out_ref[:, 2 * ATTN_W:3 * ATTN_W] = (dv_s[...] + dvp).astype(BF16)

            dq_s[...] = dq
            dk_s[...] = dkc
            dv_s[...] = dvc

        @pl.when(n == nb)
        def _():
            out_ref[:, 0:ATTN_W] = dq_s[...].astype(BF16)
            out_ref[:, ATTN_W:2 * ATTN_W] = dk_s[...].astype(BF16)
            out_ref[:, 2 * ATTN_W:3 * ATTN_W] = dv_s[...].astype(BF16)

    nc = lambda n: jnp.minimum(n, nb - 1)
    npv = lambda n: jnp.maximum(jnp.minimum(n, nb - 1) - 1, 0)
    cur = lambda part: pl.BlockSpec((None, BLK, ATTN_W), lambda r, n: (r, nc(n), part))
    prev = lambda part: pl.BlockSpec((None, BLK, ATTN_W), lambda r, n: (r, npv(n), part))
    row = pl.BlockSpec((None, BLK, ATTN_W), lambda r, n: (r, nc(n), 0))
    tab_c = pl.BlockSpec((None, BLK, LANES), lambda r, n: (r, nc(n), 0))
    tab_p = pl.BlockSpec((None, BLK, LANES), lambda r, n: (r, npv(n), 0))
    out_spec = pl.BlockSpec((None, BLK, GROUP_COLS), lambda r, n: (r, jnp.maximum(n - 1, 0), 0))
    out = pl.pallas_call(
        body, grid=(dil, nb + 1),
        in_specs=[cur(0), cur(1), prev(1), cur(2), prev(2), row, row, row, tab_c, tab_c, tab_p, tab_p],
        out_specs=out_spec,
        out_shape=jax.ShapeDtypeStruct((dil, length, GROUP_COLS), BF16),
        scratch_shapes=[pltpu.VMEM((BLK, ATTN_W), F32)] * 3,
        compiler_params=_cparams(2), name=f"attn_bwd_g{g}")(
            qkv_v, qkv_v, qkv_v, qkv_v, qkv_v, do, cc, lse, cos_v, sin_v, cos_v, sin_v)
    return out.reshape(t, GROUP_COLS)


SQRT_HALF = 0.7071067811865476
INV_SQRT_2PI = 0.3989422804014327


def _sgu_core(uv, g, b, w_ref, bias):
    cdf = 0.5 * (1.0 + lax.erf(uv * SQRT_HALF))
    z = uv * cdf
    u, v = z[:, :SGU_W], z[:, SGU_W:]
    mu = jnp.mean(v, axis=1, keepdims=True)
    xc = v - mu
    rs = lax.rsqrt(jnp.mean(xc * xc, axis=1, keepdims=True) + EPS)
    xhat = xc * rs
    vn = xhat * g + b
    row = lax.broadcasted_iota(jnp.int32, (SGU_CHUNK, SGU_CHUNK), 0)
    col = lax.broadcasted_iota(jnp.int32, (SGU_CHUNK, SGU_CHUNK), 1)
    tril = row >= col
    upper = lax.broadcasted_iota(jnp.int32, (SGU_CHUNK, LANES), 1) >= SGU_W // SGU_GROUPS
    ws, vlo, vhi, mixed = [], [], [], []
    for pr in range(SGU_W // LANES):
        sl = slice(pr * LANES, (pr + 1) * LANES)
        w0 = jnp.where(tril, w_ref[2 * pr], 0.0).astype(BF16)
        w1 = jnp.where(tril, w_ref[2 * pr + 1], 0.0).astype(BF16)
        vn2 = vn[:, sl]
        lo = jnp.where(upper, 0.0, vn2).astype(BF16)
        hi = jnp.where(upper, vn2, 0.0).astype(BF16)
        mixed.append(jnp.dot(w0, lo, preferred_element_type=F32) + jnp.dot(w1, hi, preferred_element_type=F32)
                     + bias[:, sl])
        ws.append((w0, w1))
        vlo.append(lo)
        vhi.append(hi)
    return cdf, u, xhat, rs, jnp.concatenate(mixed, axis=1), ws, vlo, vhi, tril, upper


def _sgu_fwd(gu, ln_g, ln_b, w_s, bias_exp):
    t = gu.shape[0]

    def body(uv_ref, g_ref, b_ref, w_ref, bias_ref, o_ref):
        _, u, _, _, mixed, *_ = _sgu_core(uv_ref[...].astype(F32), g_ref[...], b_ref[...], w_ref, bias_ref[...])
        o_ref[...] = (u * mixed).astype(BF16)

    return pl.pallas_call(
        body, grid=(t // SGU_CHUNK,),
        in_specs=[pl.BlockSpec((SGU_CHUNK, 2 * SGU_W), lambda n: (n, 0)), _full((1, SGU_W)), _full((1, SGU_W)),
                  _full((SGU_GROUPS, SGU_CHUNK, SGU_CHUNK)), _full((SGU_CHUNK, SGU_W))],
        out_specs=pl.BlockSpec((SGU_CHUNK, SGU_W), lambda n: (n, 0)),
        out_shape=jax.ShapeDtypeStruct((t, SGU_W), BF16),
        compiler_params=_cparams(1), name="sgu_fwd")(gu, ln_g, ln_b, w_s, bias_exp)


def _sgu_bwd(dproj, gu, dsgu, ln_g, ln_b, w_s, bias_exp):
    t = gu.shape[0]
    nchunks = t // SGU_CHUNK
    e = _head_sum_matrix()

    def body(dp_in, uv_ref, ds_ref, g_ref, b_ref, w_ref, bias_ref, e_ref, out_ref, dw_ref, dbias_ref, dg_ref, db_ref):
        n = pl.program_id(0)

        @pl.when(n == 0)
        def _():
            dw_ref[...] = jnp.zeros(dw_ref.shape, F32)
            dbias_ref[...] = jnp.zeros(dbias_ref.shape, F32)
            dg_ref[...] = jnp.zeros(dg_ref.shape, F32)
            db_ref[...] = jnp.zeros(db_ref.shape, F32)

        uv = uv_ref[...].astype(F32)
        g = g_ref[...]
        cdf, u, xhat, rs, mixed, ws, vlo, vhi, tril, upper = _sgu_core(uv, g, b_ref[...], w_ref, bias_ref[...])
        dsg = ds_ref[...]
        du = dsg * mixed
        dmixed = dsg * u
        dbias_ref[...] += dmixed
        dvn = []
        for pr in range(SGU_W // LANES):
            sl = slice(pr * LANES, (pr + 1) * LANES)
            dm2 = dmixed[:, sl]
            dlo = jnp.where(upper, 0.0, dm2).astype(BF16)
            dhi = jnp.where(upper, dm2, 0.0).astype(BF16)
            w0, w1 = ws[pr]
            dvn.append(lax.dot_general(w0, dlo, (TN, ((), ())), preferred_element_type=F32)
                       + lax.dot_general(w1, dhi, (TN, ((), ())), preferred_element_type=F32))
            dw0 = lax.dot_general(dlo, vlo[pr], (NT, ((), ())), preferred_element_type=F32)
            dw1 = lax.dot_general(dhi, vhi[pr], (NT, ((), ())), preferred_element_type=F32)
            dw_ref[2 * pr] += jnp.where(tril, dw0, 0.0)
            dw_ref[2 * pr + 1] += jnp.where(tril, dw1, 0.0)
        dvn = jnp.concatenate(dvn, axis=1)
        dg_ref[...] += jnp.sum(dvn * xhat, axis=0, keepdims=True)
        db_ref[...] += jnp.sum(dvn, axis=0, keepdims=True)
        dxh = dvn * g
        dv = rs * (dxh - jnp.mean(dxh, axis=1, keepdims=True) - xhat * jnp.mean(dxh * xhat, axis=1, keepdims=True))
        dz = jnp.concatenate([du, dv], axis=1)
        dgelu = cdf + uv * (INV_SQRT_2PI * jnp.exp(-0.5 * uv * uv))
        out_ref[...] = (dz * dgelu).astype(BF16)

        @pl.when(n == nchunks - 1)
        def _():
            dbias_ref[...] = _group_sum(dbias_ref[...], e_ref[...])

    outs = pl.pallas_call(
        body, grid=(nchunks,),
        in_specs=[pl.BlockSpec(memory_space=pl.ANY), pl.BlockSpec((SGU_CHUNK, 2 * SGU_W), lambda n: (n, 0)),
                  pl.BlockSpec((SGU_CHUNK, SGU_W), lambda n: (n, 0)), _full((1, SGU_W)), _full((1, SGU_W)),
                  _full((SGU_GROUPS, SGU_CHUNK, SGU_CHUNK)), _full((SGU_CHUNK, SGU_W)), _full((ATTN_W, ATTN_W))],
        out_specs=[pl.BlockSpec((SGU_CHUNK, 2 * SGU_W), lambda n: (n, 0)), _full((SGU_GROUPS, SGU_CHUNK, SGU_CHUNK)),
                   _full((SGU_CHUNK, SGU_W)), _full((1, SGU_W)), _full((1, SGU_W))],
        out_shape=[jax.ShapeDtypeStruct(dproj.shape, BF16), jax.ShapeDtypeStruct((SGU_GROUPS, SGU_CHUNK, SGU_CHUNK), F32),
                   jax.ShapeDtypeStruct((SGU_CHUNK, SGU_W), F32), jax.ShapeDtypeStruct((1, SGU_W), F32),
                   jax.ShapeDtypeStruct((1, SGU_W), F32)],
        input_output_aliases={0: 0},
        compiler_params=_cparams(1), name="sgu_bwd")(dproj, gu, dsgu, ln_g, ln_b, w_s, bias_exp, e)
    return outs


def _merge_fwd(attn, sgu, gu, x, w_pa, w_ps, w_out, g2):
    t = x.shape[0]
    tm = min(t, 512)

    def body(a_ref, s_ref, ga_ref, gb_ref, x_ref, wpa, wps, wo, g_ref, pa_ref, ps_ref, m_ref, x1_ref, h2_ref):
        pa = jnp.dot(a_ref[...], wpa[...], preferred_element_type=F32)
        ps = jnp.dot(s_ref[...], wps[...], preferred_element_type=F32)
        merged = (_sigmoid(ga_ref[...].astype(F32)) * pa + _sigmoid(gb_ref[...].astype(F32)) * ps).astype(BF16)
        x1 = x_ref[...] + jnp.dot(merged, wo[...], preferred_element_type=F32)
        xhat, _ = _rms_stats(x1)
        pa_ref[...] = pa.astype(BF16)
        ps_ref[...] = ps.astype(BF16)
        m_ref[...] = merged
        x1_ref[...] = x1
        h2_ref[...] = (xhat * g_ref[...]).astype(BF16)

    half = pl.BlockSpec((tm, ATTN_W), lambda i: (i, 0))
    full = pl.BlockSpec((tm, D_MODEL), lambda i: (i, 0))
    return pl.pallas_call(
        body, grid=(t // tm,),
        in_specs=[half, half, pl.BlockSpec((tm, D_MODEL), lambda i: (i, 1)), pl.BlockSpec((tm, D_MODEL), lambda i: (i, 2)),
                  full, _resident((ATTN_W, D_MODEL)), _resident((SGU_W, D_MODEL)), _resident((D_MODEL, D_MODEL)),
                  _full((1, D_MODEL))],
        out_specs=[full] * 5,
        out_shape=[jax.ShapeDtypeStruct((t, D_MODEL), BF16), jax.ShapeDtypeStruct((t, D_MODEL), BF16),
                   jax.ShapeDtypeStruct((t, D_MODEL), BF16), jax.ShapeDtypeStruct((t, D_MODEL), F32),
                   jax.ShapeDtypeStruct((t, D_MODEL), BF16)],
        compiler_params=_cparams(1), name="merge_fwd")(attn, sgu, gu, gu, x, w_pa, w_ps, w_out, g2)


def _merge_bwd(dx1b, gu, pa, ps, w_pa, w_ps, w_out):
    t = dx1b.shape[0]
    tm = min(t, 512)

    def body(d_ref, ga_ref, gb_ref, pa_ref, ps_ref, wpa, wps, wo, out_ref, dpa_ref, dps_ref, da_ref, dsg_ref):
        dm = lax.dot_general(d_ref[...], wo[...], (NT, ((), ())), preferred_element_type=F32)
        sa, sb = _sigmoid(ga_ref[...].astype(F32)), _sigmoid(gb_ref[...].astype(F32))
        dpa = (dm * sa).astype(BF16)
        dps = (dm * sb).astype(BF16)
        out_ref[:, 0:D_MODEL] = jnp.zeros((tm, D_MODEL), BF16)
        out_ref[:, D_MODEL:2 * D_MODEL] = (dm * pa_ref[...].astype(F32) * sa * (1.0 - sa)).astype(BF16)
        out_ref[:, 2 * D_MODEL:GU_COLS] = (dm * ps_ref[...].astype(F32) * sb * (1.0 - sb)).astype(BF16)
        dpa_ref[...] = dpa
        dps_ref[...] = dps
        da_ref[...] = lax.dot_general(dpa, wpa[...], (NT, ((), ())), preferred_element_type=F32)
        dsg_ref[...] = lax.dot_general(dps, wps[...], (NT, ((), ())), preferred_element_type=F32)

    half = pl.BlockSpec((tm, ATTN_W), lambda i: (i, 0))
    full = pl.BlockSpec((tm, D_MODEL), lambda i: (i, 0))
    return pl.pallas_call(
        body, grid=(t // tm,),
        in_specs=[full, pl.BlockSpec((tm, D_MODEL), lambda i: (i, 1)),
                  pl.BlockSpec((tm, D_MODEL), lambda i: (i, 2)), full, full,
                  _resident((ATTN_W, D_MODEL)), _resident((SGU_W, D_MODEL)), _resident((D_MODEL, D_MODEL))],
        out_specs=[pl.BlockSpec((tm, GU_COLS), lambda i: (i, 0)), full, full, half, half],
        out_shape=[jax.ShapeDtypeStruct((t, GU_COLS), BF16), jax.ShapeDtypeStruct((t, D_MODEL), BF16),
                   jax.ShapeDtypeStruct((t, D_MODEL), BF16), jax.ShapeDtypeStruct((t, ATTN_W), F32),
                   jax.ShapeDtypeStruct((t, SGU_W), F32)],
        compiler_params=_cparams(1), name="merge_bwd")(dx1b, gu, gu, pa, ps, w_pa, w_ps, w_out)


def _token_call(name, body, t, tm, ins, outs, reds=(), scratch=()):
    return pl.pallas_call(
        body, grid=(t // tm,), in_specs=[s for _, s in ins],
        out_specs=[o[2] for o in outs] + [_full(r) for r in reds],
        out_shape=[jax.ShapeDtypeStruct(o[0], o[1]) for o in outs] + [jax.ShapeDtypeStruct(r, F32) for r in reds],
        scratch_shapes=list(scratch), compiler_params=_cparams(1), name=name)(*[a for a, _ in ins])


def _rows_spec(tm, width):
    return pl.BlockSpec((tm, width), lambda i: (i, 0))


def _chips_spec(tm):
    return pl.BlockSpec((N_CHIPS, tm, FF_SHARD), lambda i: (0, i, 0))


def _zero_at_start(*refs):
    @pl.when(pl.program_id(0) == 0)
    def _():
        for r in refs:
            r[...] = jnp.zeros(r.shape, r.dtype)


def _ffn_fwd(h2, w_g, w_u):
    t = h2.shape[0]
    tm = min(t, 512)

    def body(h_ref, wg_ref, wu_ref, a_ref, b_ref, ff_ref):
        h = h_ref[...]
        for s in range(N_CHIPS):
            a = jnp.dot(h, wg_ref[s], preferred_element_type=F32)
            b = jnp.dot(h, wu_ref[s], preferred_element_type=F32)
            a_ref[s] = a.astype(BF16)
            b_ref[s] = b.astype(BF16)
            ff_ref[s] = (a * _sigmoid(a) * b).astype(BF16)

    shp = (N_CHIPS, t, FF_SHARD)
    w_spec = _resident((N_CHIPS, D_MODEL, FF_SHARD))
    return _token_call("ffn_fwd", body, t, tm, [(h2, _rows_spec(tm, D_MODEL)), (w_g, w_spec), (w_u, w_spec)],
                       [(shp, BF16, _chips_spec(tm))] * 3)


def _ffn_down_loss(ff, w_d, x1, tgt, gf):
    t = x1.shape[0]
    tm = min(t, 512)

    def body(ff_ref, wd_ref, x1_ref, tgt_ref, g_ref, dx2_ref, dx2b_ref, loss_ref, dgf_ref):
        _zero_at_start(loss_ref, dgf_ref)
        acc = jnp.dot(ff_ref[0], wd_ref[0], preferred_element_type=F32)
        for s in range(1, N_CHIPS):
            acc = acc + jnp.dot(ff_ref[s], wd_ref[s], preferred_element_type=F32)
        x2 = x1_ref[...] + acc
        g = g_ref[...]
        xhat, rr = _rms_stats(x2)
        diff = xhat * g - tgt_ref[...]
        rows = jnp.sum(diff * diff, axis=1, keepdims=True)
        loss_ref[...] += jnp.broadcast_to(jnp.sum(rows, axis=0, keepdims=True) * (0.5 / D_MODEL), (1, LANES))
        dy = diff * (1.0 / D_MODEL)
        dgf_ref[...] += jnp.sum(dy * xhat, axis=0, keepdims=True)
        dx2 = _rms_bwd(dy, xhat, rr, g)
        dx2_ref[...] = dx2
        dx2b_ref[...] = dx2.astype(BF16)

    row = _rows_spec(tm, D_MODEL)
    return _token_call("ffn_down_loss", body, t, tm,
                       [(ff, _chips_spec(tm)), (w_d, _resident((N_CHIPS, FF_SHARD, D_MODEL))), (x1, row), (tgt, row),
                        (gf, _full((1, D_MODEL)))],
                       [((t, D_MODEL), F32, row), ((t, D_MODEL), BF16, row)], reds=[(1, LANES), (1, D_MODEL)])


def _ffn_bwd_act(dx2b, w_d, a, b):
    t = dx2b.shape[0]
    tm = min(t, 512)

    def body(d_ref, wd_ref, a_ref, b_ref, da_ref, db_ref):
        d = d_ref[...]
        for s in range(N_CHIPS):
            dff = lax.dot_general(d, wd_ref[s], (NT, ((), ())), preferred_element_type=F32)
            av, bv = a_ref[s].astype(F32), b_ref[s].astype(F32)
            sg = _sigmoid(av)
            da_ref[s] = (dff * bv * (sg * (1.0 + av * (1.0 - sg)))).astype(BF16)
            db_ref[s] = (dff * (av * sg)).astype(BF16)

    shp = (N_CHIPS, t, FF_SHARD)
    return _token_call("ffn_bwd_act", body, t, tm,
                       [(dx2b, _rows_spec(tm, D_MODEL)), (w_d, _resident((N_CHIPS, FF_SHARD, D_MODEL))),
                        (a, _chips_spec(tm)), (b, _chips_spec(tm))],
                       [(shp, BF16, _chips_spec(tm))] * 2)


def _ffn_bwd_in(da, db, w_g, w_u, x1, dx2, g2):
    t = x1.shape[0]
    tm = min(t, 512)

    def body(da_ref, db_ref, wg_ref, wu_ref, x1_ref, dx2_ref, g_ref, dx1_ref, dx1b_ref, dg_ref):
        _zero_at_start(dg_ref)
        acc = None
        for s in range(N_CHIPS):
            part = (lax.dot_general(da_ref[s], wg_ref[s], (NT, ((), ())), preferred_element_type=F32)
                    + lax.dot_general(db_ref[s], wu_ref[s], (NT, ((), ())), preferred_element_type=F32))
            acc = part if acc is None else acc + part
        xhat, rr = _rms_stats(x1_ref[...])
        dg_ref[...] += jnp.sum(acc * xhat, axis=0, keepdims=True)
        dx1 = dx2_ref[...] + _rms_bwd(acc, xhat, rr, g_ref[...])
        dx1_ref[...] = dx1
        dx1b_ref[...] = dx1.astype(BF16)

    row = _rows_spec(tm, D_MODEL)
    w_spec = _resident((N_CHIPS, D_MODEL, FF_SHARD))
    return _token_call("ffn_bwd_in", body, t, tm,
                       [(da, _chips_spec(tm)), (db, _chips_spec(tm)), (w_g, w_spec), (w_u, w_spec), (x1, row), (dx2, row),
                        (g2, _full((1, D_MODEL)))],
                       [((t, D_MODEL), F32, row), ((t, D_MODEL), BF16, row)], reds=[(1, D_MODEL)])


def _group_dh(d_ref, w_refs):
    dh = None
    for part, w_ref in enumerate(w_refs):
        term = lax.dot_general(d_ref[:, part * ATTN_W:(part + 1) * ATTN_W], w_ref[...], (NT, ((), ())),
                               preferred_element_type=F32)
        dh = term if dh is None else dh + term
    return dh


def _in_proj_bwd(dgu, dqkvs, w_in, x, dx1, g1):
    t = x.shape[0]
    tile = min(t, TILE)
    tm = min(t, 1024)

    dhs = []
    for g in (1, 2):

        def body_g(d_ref, wq_ref, wk_ref, wv_ref, o_ref):
            o_ref[...] = _group_dh(d_ref, (wq_ref, wk_ref, wv_ref))

        dh = _token_call(
            f"in_proj_bwd_g{g}", body_g, t, tm,
            [(dqkvs[g], _rows_spec(tm, GROUP_COLS))] + [(w_in, s) for s in _group_w_specs(g)],
            [((t, D_MODEL), F32, _rows_spec(tm, D_MODEL))])[0]
        dhs.append(dh.reshape(DILATIONS[g], t // DILATIONS[g], D_MODEL))

    def body(dgu_ref, dq0_ref, w0_ref, w1_ref, wq_ref, wk_ref, wv_ref, x_ref, dx1_ref, g_ref, dh1_ref, dh2_ref,
             dx_ref, dg_ref, slab):
        _zero_at_start(dg_ref)
        dh = lax.dot_general(dgu_ref[:, 0:GU_HALF], w0_ref[...], (NT, ((), ())), preferred_element_type=F32)
        dh = dh + lax.dot_general(dgu_ref[:, GU_HALF:], w1_ref[...], (NT, ((), ())), preferred_element_type=F32)
        dh = dh + _group_dh(dq0_ref, (wq_ref, wk_ref, wv_ref))
        dh = dh + _natural_from_group(slab, dh1_ref)
        dh = dh + _natural_from_group(slab, dh2_ref)
        xhat, rr = _rms_stats(x_ref[...])
        dg_ref[...] += jnp.sum(dh * xhat, axis=0, keepdims=True)
        dx_ref[...] = dx1_ref[...] + _rms_bwd(dh, xhat, rr, g_ref[...])

    row = _rows_spec(tile, D_MODEL)
    return _token_call(
        "in_proj_bwd", body, t, tile,
        [(dgu, _rows_spec(tile, GU_COLS)), (dqkvs[0], _rows_spec(tile, GROUP_COLS))]
        + [(w_in, s) for s in _gu_w_specs() + _group_w_specs(0)]
        + [(x, row), (dx1, row), (g1, _full((1, D_MODEL))),
         (dhs[0], _group_spec(DILATIONS[1], tile, D_MODEL)), (dhs[1], _group_spec(DILATIONS[2], tile, D_MODEL))],
        [((t, D_MODEL), F32, row)], reds=[(1, D_MODEL)], scratch=[_slabs(tile, D_MODEL)])


def _epi_bf16(acc, e, o, r, ids):
    o[0][...] = acc.astype(BF16)


WGRAD_TK = 2048


def _wgrad_2d(name, a, b, tm, tn):
    t, k1 = a.shape
    n = b.shape[1]
    tk = min(t, WGRAD_TK)
    return _mm(name, (k1 // tm, n // tn, t // tk),
               [(a, pl.BlockSpec((tk, tm), lambda i, j, k: (k, i)), b, pl.BlockSpec((tk, tn), lambda i, j, k: (k, j)))],
               TN, (tm, tn), _epi_bf16, outs=[((k1, n), BF16, pl.BlockSpec((tm, tn), lambda i, j, k: (i, j)))])[0]


def _wgrad_in(hs, dgu, dqkvs):
    t = dgu.shape[0]
    tk = min(t, WGRAD_TK)
    gu_block = QKV_BLOCKS * ATTN_W // GU_HALF
    parts = [(hs[0], dgu, GU_HALF, lambda j: j + gu_block)]
    parts += [(hs[g].reshape(t, D_MODEL), dqkvs[g], ATTN_W, lambda j, g=g: _w_in_block(j, g)) for g in range(3)]
    dst = None
    for n, (a, b, tn, block_of) in enumerate(parts):
        dst = _mm(f"wgrad_in_{n}", (1, b.shape[1] // tn, t // tk),
                  [(a, pl.BlockSpec((tk, D_MODEL), lambda i, j, k: (k, 0)), b,
                    pl.BlockSpec((tk, tn), lambda i, j, k: (k, j)))],
                  TN, (D_MODEL, tn), _epi_bf16,
                  extras=[] if dst is None else [(dst, pl.BlockSpec(memory_space=pl.ANY))],
                  outs=[((D_MODEL, IN_COLS), BF16,
                         pl.BlockSpec((D_MODEL, tn), lambda i, j, k, block_of=block_of: (0, block_of(j))))],
                  aliases=None if dst is None else {2: 0})[0]
    return dst


def _wgrad_ff_in(name, h2, da):
    t = h2.shape[0]
    tk = min(t, WGRAD_TK)
    return _mm(name, (N_CHIPS, 1, t // tk),
               [(h2, pl.BlockSpec((tk, D_MODEL), lambda i, j, k: (k, 0)),
                 da, pl.BlockSpec((None, tk, FF_SHARD), lambda i, j, k: (i, k, 0)))],
               TN, (D_MODEL, FF_SHARD), _epi_bf16,
               outs=[((N_CHIPS, D_MODEL, FF_SHARD), BF16, pl.BlockSpec((None, D_MODEL, FF_SHARD), lambda i, j, k: (i, 0, 0)))])[0]


def _wgrad_ff_down(ff, dx2b):
    t = dx2b.shape[0]
    tk = min(t, WGRAD_TK)
    return _mm("wgrad_ffn_down", (N_CHIPS, 1, t // tk),
               [(ff, pl.BlockSpec((None, tk, FF_SHARD), lambda i, j, k: (i, k, 0)),
                 dx2b, pl.BlockSpec((tk, D_MODEL), lambda i, j, k: (k, 0)))],
               TN, (FF_SHARD, D_MODEL), _epi_bf16,
               outs=[((N_CHIPS, FF_SHARD, D_MODEL), BF16, pl.BlockSpec((None, FF_SHARD, D_MODEL), lambda i, j, k: (i, 0, 0)))])[0]


def _local_step(x, pos_col, tgt, g1, ln_g, ln_b, w_s, b_s, g2, gf, w_p, late_weights, on_grads=None):
    tables = _rope_tables(pos_col)
    bias_exp = jnp.repeat(jnp.transpose(b_s), SGU_W // SGU_GROUPS, axis=1)

    hs = _norm_fwd(x, g1)
    gu, qkvs = _in_proj(hs, w_p, tables)
    os_, ls_ = [], []
    for g, dil in enumerate(DILATIONS):
        o, lse = _attn_fwd(qkvs[g], g, dil)
        os_.append(o)
        ls_.append(lse)
    attn = _combine_fwd(os_, ls_)
    sgu = _sgu_fwd(gu, ln_g, ln_b, w_s, bias_exp)
    w_pa, w_ps, w_out, w_g, w_u, w_d = late_weights(sgu)
    pa, ps, merged, x1, h2 = _merge_fwd(attn, sgu, gu, x, w_pa, w_ps, w_out, g2)
    a, b, ff = _ffn_fwd(h2, w_g, w_u)
    dx2, dx2b, loss, dgf = _ffn_down_loss(ff, w_d, x1, tgt, gf)

    da, db = _ffn_bwd_act(dx2b, w_d, a, b)
    dw_d = _wgrad_ff_down(ff, dx2b)
    dx1, dx1b, dg2 = _ffn_bwd_in(da, db, w_g, w_u, x1, dx2, g2)
    dw_g = _wgrad_ff_in("wgrad_ffn_gate", h2, da)
    dw_u = _wgrad_ff_in("wgrad_ffn_up", h2, db)

    dgu, dpa, dps, dattn, dsgu = _merge_bwd(dx1b, gu, pa, ps, w_pa, w_ps, w_out)
    dw_out = _wgrad_2d("wgrad_out", merged, dx1b, D_MODEL, D_MODEL)
    dw_pa = _wgrad_2d("wgrad_proj_attn", attn, dpa, ATTN_W, D_MODEL)
    dw_ps = _wgrad_2d("wgrad_proj_sgu", sgu, dps, SGU_W, D_MODEL)
    if on_grads is not None:
        ln_g = ln_g + on_grads(1, dict(w_proj_attn=dw_pa, w_proj_sgu=dw_ps, w_out=dw_out, w_ffn_gate=dw_g, w_ffn_up=dw_u,
                                       w_ffn_down=dw_d))[:, :SGU_W]
    dgu, dw_s, dbias, dln_g, dln_b = _sgu_bwd(dgu, gu, dsgu, ln_g, ln_b, w_s, bias_exp)
    dos, ccs = _combine_bwd(dattn, os_, ls_)
    dqkvs = [_attn_bwd(qkvs[g], dos[g], ccs[g], ls_[g], *tables[g], g, dil) for g, dil in enumerate(DILATIONS)]
    dw_p = _wgrad_in(hs, dgu, dqkvs)
    if on_grads is not None:
        g1 = g1 + on_grads(0, dict(w_in=dw_p))
    dx, dg1 = _in_proj_bwd(dgu, dqkvs, w_p, x, dx1, g1)

    db_s = jnp.transpose(dbias[:, ::SGU_W // SGU_GROUPS])
    small = dict(loss=loss, norm1_g=dg1, sgu_ln_g=dln_g, sgu_ln_b=dln_b, w_spatial=dw_s, b_spatial=db_s,
                 norm2_g=dg2, final_g=dgf)
    big = dict(w_in=dw_p, w_proj_attn=dw_pa, w_proj_sgu=dw_ps, w_out=dw_out, w_ffn_gate=dw_g, w_ffn_up=dw_u,
               w_ffn_down=dw_d)
    return dx, big, small


def _ew(name, fn, ins, out_dtypes):
    shp = ins[0].shape
    rows, cols = shp
    tr = next((cand for cand in (256, 352, 128) if rows % cand == 0 and rows > cand), rows)

    def body(*refs):
        res = fn(*[r[...] for r in refs[:len(ins)]])
        for o_ref, v in zip(refs[len(ins):], res):
            o_ref[...] = v.astype(o_ref.dtype)

    spec = pl.BlockSpec((tr, cols), lambda i: (i, 0))
    return pl.pallas_call(
        body, grid=(rows // tr,), in_specs=[spec] * len(ins), out_specs=[spec] * len(out_dtypes),
        out_shape=[jax.ShapeDtypeStruct(shp, d) for d in out_dtypes],
        compiler_params=_cparams(1), name=name)(*ins)


def _adamw_math(g, w, m, v):
    m = ADAM_B1 * m + (1.0 - ADAM_B1) * g
    v = ADAM_B2 * v + (1.0 - ADAM_B2) * (g * g)
    m_hat = m / (1.0 - ADAM_B1 ** ADAM_STEP)
    v_hat = v / (1.0 - ADAM_B2 ** ADAM_STEP)
    delta = -ADAM_LR * (m_hat / (jnp.sqrt(v_hat) + ADAM_EPS) + ADAM_WD * w)
    return delta, m, v


def _adamw(name, g, w, m, v):
    return _ew(name, lambda g_, w_, m_, v_: (g_,) + _adamw_math(g_, w_, m_, v_), [g, w, m, v], [F32] * 4)


VMEM_SPEC = pl.BlockSpec(memory_space=pltpu.VMEM)


def _for_row_chunks(rows, fn):
    ck = next(c for c in (64, 32, 16) if rows % c == 0)

    def step(i, carry):
        fn(pl.multiple_of(i * ck, ck), ck)
        return carry

    lax.fori_loop(0, rows // ck, step, 0)


def _place():
    x, y, c = lax.axis_index("x"), lax.axis_index("y"), lax.axis_index("c")
    chips = [(1 - x, y), (x, 1 - y), (1 - x, 1 - y)]
    return x, y, c, 2 * x + y, chips


def _rows(ref, start, size):
    if len(ref.shape) == 2:
        return ref.at[pl.ds(start, size), :]
    return ref.at[:, pl.ds(start, size), :]


def _comm_call(name, body, ins, out_shapes, scratch, n_remote):
    return pl.pallas_call(
        body, in_specs=[VMEM_SPEC] * len(ins), out_specs=[VMEM_SPEC] * len(out_shapes),
        out_shape=out_shapes,
        scratch_shapes=list(scratch) + [pltpu.SemaphoreType.DMA((n_remote,)), pltpu.SemaphoreType.DMA((n_remote,))],
        compiler_params=pltpu.CompilerParams(vmem_limit_bytes=VMEM_LIMIT), name=name)(*ins)


def _gather_weights(name, shards):
    nt = len(shards)

    def body(*refs):
        ins, outs = refs[:nt], refs[nt:2 * nt]
        send, recv = refs[2 * nt:]
        x, y, c, me, chips = _place()
        sibling = (x, y, 1 - c)
        firsts, passed, expects = [], [], []
        for t in range(nt):
            kh = ins[t].shape[0] // 2
            for j, chip in enumerate(chips):
                k = t * 3 + j
                theirs = 2 * chip[0] + chip[1]
                firsts.append(pltpu.make_async_remote_copy(
                    src_ref=_rows(ins[t], c * kh, kh), dst_ref=_rows(outs[t].at[me], c * kh, kh),
                    send_sem=send.at[k], recv_sem=recv.at[k], device_id=(*chip, c), device_id_type=MESH))
                landed = _rows(outs[t].at[theirs], c * kh, kh)
                expects.append(pltpu.make_async_remote_copy(
                    src_ref=landed, dst_ref=landed, send_sem=send.at[k], recv_sem=recv.at[k],
                    device_id=(*chip, c), device_id_type=MESH))
                passed.append(pltpu.make_async_remote_copy(
                    src_ref=landed, dst_ref=landed, send_sem=send.at[3 * nt + k], recv_sem=recv.at[3 * nt + k],
                    device_id=sibling, device_id_type=MESH))
        for cp in firsts:
            cp.start()
        for t in range(nt):
            mine = outs[t].at[me]

            def put(r0, ck, src=ins[t], dst=mine):
                dst[pl.ds(r0, ck), :] = src[pl.ds(r0, ck), :]

            _for_row_chunks(ins[t].shape[0], put)
        for k in range(3 * nt):
            expects[k].wait_recv()
            passed[k].start()
        for t in range(nt):
            kh = ins[t].shape[0] // 2
            for j, chip in enumerate(chips):
                k = t * 3 + j
                theirs = 2 * chip[0] + chip[1]
                other = _rows(outs[t].at[theirs], (1 - c) * kh, kh)
                pltpu.make_async_remote_copy(
                    src_ref=other, dst_ref=other, send_sem=send.at[3 * nt + k], recv_sem=recv.at[3 * nt + k],
                    device_id=sibling, device_id_type=MESH).wait_recv()
        for cp in firsts + passed:
            cp.wait_send()

    out_shapes = [jax.ShapeDtypeStruct((N_CHIPS,) + s.shape, s.dtype) for s in shards]
    return _comm_call(name, body, shards, out_shapes, [], 6 * nt)


def _pair_reduce(name, grads):
    nt = len(grads)

    def half(s):
        shp = list(s.shape)
        shp[-2] //= 2
        return tuple(shp)

    def body(*refs):
        ins, outs, got = refs[:nt], refs[nt:2 * nt], refs[2 * nt:3 * nt]
        send, recv = refs[3 * nt:]
        x, y, c, me, chips = _place()
        copies = []
        for t in range(nt):
            kh = ins[t].shape[-2] // 2
            rc = pltpu.make_async_remote_copy(
                src_ref=_rows(ins[t], (1 - c) * kh, kh), dst_ref=got[t], send_sem=send.at[t], recv_sem=recv.at[t],
                device_id=(x, y, 1 - c), device_id_type=MESH)
            rc.start()
            copies.append(rc)
        for t in range(nt):
            kh = ins[t].shape[-2] // 2
            copies[t].wait_recv()
            for lead in ([()] if len(ins[t].shape) == 2 else [(s,) for s in range(ins[t].shape[0])]):

                def add(r0, ck, lead=lead, src=ins[t], oth=got[t], dst=outs[t], kh=kh):
                    own = src[lead + (pl.ds(pl.multiple_of(c * kh + r0, ck), ck), slice(None))]
                    rows = lead + (pl.ds(r0, ck), slice(None))
                    dst[rows] = (own.astype(F32) + oth[rows].astype(F32)).astype(BF16)

                _for_row_chunks(kh, add)
        for rc in copies:
            rc.wait_send()

    shapes = [half(g) for g in grads]
    return _comm_call(name, body, grads, [jax.ShapeDtypeStruct(s, BF16) for s in shapes],
                      [pltpu.VMEM(s, BF16) for s in shapes], nt)


HBM_SPEC = pl.BlockSpec(memory_space=pltpu.HBM)
SEM_SPEC = pl.BlockSpec(memory_space=pltpu.SEMAPHORE)
DATAFLOW = pltpu.SideEffectType.DATAFLOW_SIDE_EFFECTING
TOKEN_SHAPE = (1, D_MODEL)


def _shard_cols(shape):
    return shape[2] if len(shape) == 3 else shape[1] // N_CHIPS


def _chip_piece(ref, j):
    if len(ref.shape) == 3:
        return ref.at[j]
    n4 = ref.shape[1] // N_CHIPS
    return ref.at[:, pl.ds(j * n4, n4)]


def _exchange_copies(sums, lands, send, recv):
    x, y, c, me, chips = _place()
    return [pltpu.make_async_remote_copy(
        src_ref=_chip_piece(sums[t], 2 * chip[0] + chip[1]), dst_ref=lands[t].at[j], send_sem=send.at[t * 3 + j],
        recv_sem=recv.at[t * 3 + j], device_id=(*chip, c), device_id_type=MESH)
        for t in range(len(sums)) for j, chip in enumerate(chips)]


def _gather_copies(shards, lands, send, recv):
    x, y, c, me, chips = _place()
    return [pltpu.make_async_remote_copy(
        src_ref=shards[t], dst_ref=lands[t].at[me], send_sem=send.at[t * 3 + j], recv_sem=recv.at[t * 3 + j],
        device_id=(*chip, c), device_id_type=MESH)
        for t in range(len(shards)) for j, chip in enumerate(chips)]


def _split_start(name, copies, srcs, land_shapes):
    nt = len(srcs)
    lands = [lax.empty(s, BF16) for s in land_shapes]

    def body(*refs):
        send, recv = refs[2 * nt], refs[2 * nt + 1]
        for cp in copies(refs[:nt], refs[nt:2 * nt], send, recv):
            cp.start()
        refs[-1][...] = jnp.zeros(TOKEN_SHAPE, F32)

    hbm = lambda a: pltpu.with_memory_space_constraint(a, pltpu.HBM)
    outs = pl.pallas_call(
        body, name=name,
        out_shape=[pltpu.SemaphoreType.DMA((3 * nt,)), pltpu.SemaphoreType.DMA((3 * nt,))]
        + [pltpu.HBM(s.shape, s.dtype) for s in srcs] + [pltpu.HBM(l.shape, l.dtype) for l in lands]
        + [jax.ShapeDtypeStruct(TOKEN_SHAPE, F32)],
        in_specs=[HBM_SPEC] * (2 * nt), out_specs=[SEM_SPEC, SEM_SPEC] + [HBM_SPEC] * (2 * nt) + [VMEM_SPEC],
        input_output_aliases={i: 2 + i for i in range(2 * nt)},
        compiler_params=pltpu.CompilerParams(has_side_effects=DATAFLOW))(*[hbm(a) for a in list(srcs) + lands])
    return outs[0], outs[1], outs[2:2 + nt], outs[2 + nt:2 + 2 * nt], outs[-1]


def _split_wait(name, copies, send, recv, srcs, lands, after):
    nt = len(srcs)

    def body(*refs):
        for cp in copies(refs[:nt], refs[nt:2 * nt], refs[2 * nt], refs[2 * nt + 1]):
            cp.wait_send()
            cp.wait_recv()

    outs = pl.pallas_call(
        body, name=name,
        out_shape=[pltpu.HBM(s.shape, s.dtype) for s in srcs] + [pltpu.HBM(l.shape, l.dtype) for l in lands],
        in_specs=[HBM_SPEC] * (2 * nt) + [SEM_SPEC, SEM_SPEC, pl.BlockSpec(memory_space=pl.ANY)],
        out_specs=[HBM_SPEC] * (2 * nt), input_output_aliases={i: i for i in range(2 * nt)},
        compiler_params=pltpu.CompilerParams(has_side_effects=DATAFLOW))(*srcs, *lands, send, recv, after)
    return outs[:nt], outs[nt:]


def _chip_sum(name, sums, lands):
    nt = len(sums)

    def body(*refs):
        ins, slots, outs = refs[:nt], refs[nt:2 * nt], refs[2 * nt:3 * nt]
        send, recv = refs[3 * nt:]
        x, y, c, me, chips = _place()
        sibling = (x, y, 1 - c)
        handed = []
        for t in range(nt):
            kh, n4 = ins[t].shape[-2], outs[t].shape[1]
            for jj in range(N_CHIPS):

                @pl.when(me == jj)
                def _(jj=jj, src=ins[t], slot=slots[t], dst=outs[t], kh=kh, n4=n4):
                    def add(r0, ck):
                        rows = pl.ds(r0, ck)
                        own = src[jj, rows, :] if len(src.shape) == 3 else src[rows, jj * n4:(jj + 1) * n4]
                        acc = ((own.astype(F32) + slot[0, rows, :].astype(F32)) + slot[1, rows, :].astype(F32)) \
                            + slot[2, rows, :].astype(F32)
                        dst[pl.ds(pl.multiple_of(c * kh + r0, ck), ck), :] = acc

                    _for_row_chunks(kh, add)

            rc = pltpu.make_async_remote_copy(
                src_ref=_rows(outs[t], c * kh, kh), dst_ref=_rows(outs[t], c * kh, kh), send_sem=send.at[t],
                recv_sem=recv.at[t], device_id=sibling, device_id_type=MESH)
            rc.start()
            handed.append(rc)
        for t in range(nt):
            kh = ins[t].shape[-2]
            other = _rows(outs[t], (1 - c) * kh, kh)
            pltpu.make_async_remote_copy(
                src_ref=other, dst_ref=other, send_sem=send.at[t], recv_sem=recv.at[t],
                device_id=sibling, device_id_type=MESH).wait_recv()
        for rc in handed:
            rc.wait_send()

    out_shapes = [jax.ShapeDtypeStruct((2 * s.shape[-2], _shard_cols(s.shape)), F32) for s in sums]
    return _comm_call(name, body, list(sums) + list(lands), out_shapes, [], nt)


VEC_SHAPE = (8, D_MODEL + LANES)
VEC_SLOTS = dict(norm1_g=(slice(0, 1), slice(0, D_MODEL)), norm2_g=(slice(1, 2), slice(0, D_MODEL)),
                 final_g=(slice(2, 3), slice(0, D_MODEL)), sgu_ln_g=(slice(3, 4), slice(0, SGU_W)),
                 sgu_ln_b=(slice(3, 4), slice(SGU_W, 2 * SGU_W)), b_spatial=(slice(0, 8), slice(D_MODEL, D_MODEL + LANES)),
                 loss=(slice(4, 5), slice(0, LANES)))
VEC_PARAMS = ("norm1_g", "norm2_g", "final_g", "sgu_ln_g", "sgu_ln_b", "b_spatial")
SMALL_PARAMS = VEC_PARAMS + ("w_spatial",)
W_SPATIAL_2D = (SGU_GROUPS * SGU_CHUNK, SGU_CHUNK)


def _small_step(partials, w, m, v):
    def shape2d(name):
        if name == "w_spatial":
            return W_SPATIAL_2D
        rows, cols = VEC_SLOTS[name]
        return (rows.stop - rows.start, cols.stop - cols.start)

    g_names = VEC_PARAMS + ("loss", "w_spatial")
    ng, npar = len(g_names), len(SMALL_PARAMS)

    def pack(dst, parts):
        dst[...] = jnp.zeros(VEC_SHAPE, F32)
        for n, ref in parts.items():
            if n in VEC_SLOTS:
                dst[VEC_SLOTS[n]] = ref[...]

    def reduce_body(*refs):
        g_in = dict(zip(g_names, refs[:ng]))
        vec_out, ws_out, vec, vec_pair, vec_slot, ws_pair, ws_slot, send, recv = refs[ng:]
        x, y, c, me, chips = _place()
        sibling = (x, y, 1 - c)
        pack(vec, g_in)
        copies = []

        def allreduce(k0, src, pair, slot):
            first = pltpu.make_async_remote_copy(src_ref=src, dst_ref=pair, send_sem=send.at[k0], recv_sem=recv.at[k0],
                                                 device_id=sibling, device_id_type=MESH)
            first.start()
            first.wait_recv()
            slot[me] = src[...] + pair[...]
            arrivals = []
            for j, chip in enumerate(chips):
                theirs = 2 * chip[0] + chip[1]
                rc = pltpu.make_async_remote_copy(src_ref=slot.at[me], dst_ref=slot.at[me], send_sem=send.at[k0 + 1 + j],
                                                  recv_sem=recv.at[k0 + 1 + j], device_id=(*chip, c), device_id_type=MESH)
                rc.start()
                arrivals.append(pltpu.make_async_remote_copy(
                    src_ref=slot.at[theirs], dst_ref=slot.at[theirs], send_sem=send.at[k0 + 1 + j],
                    recv_sem=recv.at[k0 + 1 + j], device_id=(*chip, c), device_id_type=MESH))
                copies.append(rc)
            copies.append(first)
            return arrivals

        arrivals = allreduce(0, vec, vec_pair, vec_slot) + allreduce(4, g_in["w_spatial"], ws_pair, ws_slot)
        for a in arrivals:
            a.wait_recv()
        vec_out[...] = ((vec_slot[0] + vec_slot[1]) + vec_slot[2]) + vec_slot[3]

        def spatial(r0, ck):
            rows = pl.ds(r0, ck)
            ws_out[rows, :] = ((ws_slot[0, rows, :] + ws_slot[1, rows, :]) + ws_slot[2, rows, :]) + ws_slot[3, rows, :]

        _for_row_chunks(W_SPATIAL_2D[0], spatial)
        for rc in copies:
            rc.wait_send()

    g_vec, g_ws = pl.pallas_call(
        reduce_body, in_specs=[VMEM_SPEC] * ng, out_specs=[VMEM_SPEC] * 2,
        out_shape=[jax.ShapeDtypeStruct(VEC_SHAPE, F32), jax.ShapeDtypeStruct(W_SPATIAL_2D, F32)],
        scratch_shapes=[pltpu.VMEM(VEC_SHAPE, F32), pltpu.VMEM(VEC_SHAPE, F32), pltpu.VMEM((N_CHIPS,) + VEC_SHAPE, F32),
                        pltpu.VMEM(W_SPATIAL_2D, F32), pltpu.VMEM((N_CHIPS,) + W_SPATIAL_2D, F32),
                        pltpu.SemaphoreType.DMA((8,)), pltpu.SemaphoreType.DMA((8,))],
        name="small_params_allreduce")(*[partials[n].reshape(shape2d(n)) for n in g_names])

    def update_body(*refs):
        gv_ref, gw_ref = refs[:2]
        w_in, m_in, v_in = (dict(zip(SMALL_PARAMS, refs[2 + k * npar:2 + (k + 1) * npar])) for k in range(3))
        o0 = 2 + 3 * npar
        g_out = dict(zip(g_names, refs[o0:o0 + ng]))
        d_out, m_out, v_out = (dict(zip(SMALL_PARAMS, refs[o0 + ng + k * npar:o0 + ng + (k + 1) * npar])) for k in range(3))
        vw, vm, vv = refs[o0 + ng + 3 * npar:]
        pack(vw, w_in)
        pack(vm, m_in)
        pack(vv, v_in)
        d_vec, m_vec, v_vec = _adamw_math(gv_ref[...], vw[...], vm[...], vv[...])
        vw[...] = d_vec
        vm[...] = m_vec
        vv[...] = v_vec
        for n in VEC_PARAMS + ("loss",):
            g_out[n][...] = gv_ref[VEC_SLOTS[n]]
        for n in VEC_PARAMS:
            d_out[n][...] = vw[VEC_SLOTS[n]]
            m_out[n][...] = vm[VEC_SLOTS[n]]
            v_out[n][...] = vv[VEC_SLOTS[n]]

        def spatial(r0, ck):
            rows = pl.ds(r0, ck)
            g = gw_ref[rows, :]
            d_, m_, v_ = _adamw_math(g, w_in["w_spatial"][rows, :], m_in["w_spatial"][rows, :], v_in["w_spatial"][rows, :])
            g_out["w_spatial"][rows, :] = g
            d_out["w_spatial"][rows, :] = d_
            m_out["w_spatial"][rows, :] = m_
            v_out["w_spatial"][rows, :] = v_

        _for_row_chunks(W_SPATIAL_2D[0], spatial)

    ins = [g_vec, g_ws]
    for src in (w, m, v):
        ins += [src[n].reshape(shape2d(n)) for n in SMALL_PARAMS]
    out_shapes = [jax.ShapeDtypeStruct(shape2d(n), F32) for n in g_names + SMALL_PARAMS * 3]
    outs = pl.pallas_call(
        update_body, in_specs=[VMEM_SPEC] * len(ins), out_specs=[VMEM_SPEC] * len(out_shapes), out_shape=out_shapes,
        scratch_shapes=[pltpu.VMEM(VEC_SHAPE, F32)] * 3, name="small_params_update")(*ins)
    grads = dict(zip(g_names, outs[:ng]))
    rest = [dict(zip(SMALL_PARAMS, outs[ng + k * npar:ng + (k + 1) * npar])) for k in range(3)]
    return grads, rest[0], rest[1], rest[2]


BIG = ("w_in", "w_proj_attn", "w_proj_sgu", "w_out", "w_ffn_gate", "w_ffn_up", "w_ffn_down")
COMM_GROUPS = (("w_in",), ("w_proj_attn", "w_proj_sgu", "w_out", "w_ffn_gate", "w_ffn_up", "w_ffn_down"))
WEIGHTS = ("norm1_g", "w_in", "sgu_ln_g", "sgu_ln_b", "w_spatial", "b_spatial", "w_proj_attn", "w_proj_sgu", "w_out",
           "norm2_g", "w_ffn_gate", "w_ffn_up", "w_ffn_down", "final_g")


def _cols_from_chips(g):
    return jnp.transpose(g, (1, 0, 2)).reshape(g.shape[1], N_CHIPS * g.shape[2])


def kernel(x, positions, norm1_g, w_in, sgu_ln_g, sgu_ln_b, w_spatial, b_spatial, w_proj_attn, w_proj_sgu, w_out, norm2_g, w_ffn_gate, w_ffn_up, w_ffn_down, final_g, loss_target, m_norm1_g, m_w_in, m_sgu_ln_g, m_sgu_ln_b, m_w_spatial, m_b_spatial, m_w_proj_attn, m_w_proj_sgu, m_w_out, m_norm2_g, m_w_ffn_gate, m_w_ffn_up, m_w_ffn_down, m_final_g, v_norm1_g, v_w_in, v_sgu_ln_g, v_sgu_ln_b, v_w_spatial, v_b_spatial, v_w_proj_attn, v_w_proj_sgu, v_w_out, v_norm2_g, v_w_ffn_gate, v_w_ffn_up, v_w_ffn_down, v_final_g):
    w = dict(norm1_g=norm1_g, w_in=w_in, sgu_ln_g=sgu_ln_g, sgu_ln_b=sgu_ln_b, w_spatial=w_spatial, b_spatial=b_spatial,
             w_proj_attn=w_proj_attn, w_proj_sgu=w_proj_sgu, w_out=w_out, norm2_g=norm2_g, w_ffn_gate=w_ffn_gate,
             w_ffn_up=w_ffn_up, w_ffn_down=w_ffn_down, final_g=final_g)
    m = dict(norm1_g=m_norm1_g, w_in=m_w_in, sgu_ln_g=m_sgu_ln_g, sgu_ln_b=m_sgu_ln_b, w_spatial=m_w_spatial,
             b_spatial=m_b_spatial, w_proj_attn=m_w_proj_attn, w_proj_sgu=m_w_proj_sgu, w_out=m_w_out, norm2_g=m_norm2_g,
             w_ffn_gate=m_w_ffn_gate, w_ffn_up=m_w_ffn_up, w_ffn_down=m_w_ffn_down, final_g=m_final_g)
    v = dict(norm1_g=v_norm1_g, w_in=v_w_in, sgu_ln_g=v_sgu_ln_g, sgu_ln_b=v_sgu_ln_b, w_spatial=v_w_spatial,
             b_spatial=v_b_spatial, w_proj_attn=v_w_proj_attn, w_proj_sgu=v_w_proj_sgu, w_out=v_w_out, norm2_g=v_norm2_g,
             w_ffn_gate=v_w_ffn_gate, w_ffn_up=v_w_ffn_up, w_ffn_down=v_w_ffn_down, final_g=v_final_g)
    t = x.shape[1]

    shards = {n: _ew(f"cast_{n}", lambda a: (a,), [w[n][0]], [BF16])[0] for n in BIG}
    late = COMM_GROUPS[1]
    gath_in, late_shards = lax.optimization_barrier(
        (_gather_weights("gather_weights_0", [shards["w_in"]])[0], [shards[n] for n in late]))
    w_p = _cols_from_chips(gath_in)
    send, recv, late_shards, lands, token = _split_start(
        "gather_start_1", _gather_copies, late_shards, [(N_CHIPS,) + s.shape for s in late_shards])

    def late_weights(after):
        srcs, filled = _split_wait("gather_wait_1", _gather_copies, send, recv, late_shards, lands, after)
        me = 2 * lax.axis_index("x") + lax.axis_index("y")
        gath = {n: lax.dynamic_update_slice(f, s[None], (me, 0, 0)) for n, f, s in zip(late, filled, srcs)}
        return (_cols_from_chips(gath["w_proj_attn"]), _cols_from_chips(gath["w_proj_sgu"]),
                gath["w_out"].reshape(D_MODEL, D_MODEL), gath["w_ffn_gate"], gath["w_ffn_up"], gath["w_ffn_down"])

    exchanges = {}

    def on_grads(i, partials):
        if "w_out" in partials:
            partials["w_out"] = partials["w_out"].reshape(N_CHIPS, D_MODEL // N_CHIPS, D_MODEL)
        sums = _pair_reduce(f"rs_pair_reduce_{i}", [partials[n] for n in COMM_GROUPS[i]])
        *exchanges[i], started = _split_start(
            f"rs_exchange_start_{i}", _exchange_copies, sums, [(3, s.shape[-2], _shard_cols(s.shape)) for s in sums])
        return started

    dx, _, small = _local_step(
        x[0], positions.reshape(t, 1), loss_target[0], norm1_g + token, sgu_ln_g, sgu_ln_b, w_spatial[0], b_spatial[0],
        norm2_g, final_g.reshape(1, D_MODEL), w_p, late_weights, on_grads=on_grads)

    grads = {}
    for i in (1, 0):
        sums, filled = _split_wait(f"rs_exchange_wait_{i}", _exchange_copies, *exchanges[i], dx)
        grads.update(zip(COMM_GROUPS[i], _chip_sum(f"rs_chip_sum_{i}", sums, filled)))

    delta, new_m, new_v = {}, {}, {}
    for n in BIG:
        shp = w[n].shape
        g_, d_, m_, v_ = _adamw(f"adamw_{n}", grads[n], w[n][0], m[n][0], v[n][0])
        grads[n], delta[n], new_m[n], new_v[n] = g_.reshape(shp), d_.reshape(shp), m_.reshape(shp), v_.reshape(shp)

    g_s, d_s, m_s, v_s = _small_step(small, w, m, v)
    loss = g_s["loss"][0, 0]
    for n in SMALL_PARAMS:
        shp = w[n].shape
        grads[n], delta[n], new_m[n], new_v[n] = (a[n].reshape(shp) for a in (g_s, d_s, m_s, v_s))

    return (loss, dx.reshape(x.shape), *[grads[n] for n in WEIGHTS], *[delta[n] for n in WEIGHTS],
            *[new_m[n] for n in WEIGHTS], *[new_v[n] for n in WEIGHTS])
```

```python
import functools

import numpy as np
import jax
import jax.numpy as jnp
from jax import lax
from jax.experimental import pallas as pl
from jax.experimental.pallas import tpu as pltpu

F32, BF16 = jnp.float32, jnp.bfloat16
MESH = pl.DeviceIdType.MESH

D_MODEL = 1024
HEAD_DIM = 64
ATTN_W = 512
DILATIONS = (1, 4, 16)
BLK = 128
ROPE_DIM = 16
ROPE_THETA = 500000.0
SGU_W = 512
SGU_CHUNK = 128
SGU_GROUPS = 8
D_FF = 2816
N_CHIPS = 4
FF_SHARD = D_FF // N_CHIPS
IN_COLS = 7680
EPS = 1e-6
NEG = -1e30
LANES = 128
VMEM_LIMIT = 52 * 1024 * 1024

ADAM_LR, ADAM_B1, ADAM_B2, ADAM_EPS, ADAM_WD, ADAM_STEP = 0.001, 0.9, 0.999, 1e-08, 0.01, 10

QKV_BLOCKS = 9


def _w_in_block(part, g):
    return part * len(DILATIONS) + g


def _cparams(ngrid):
    return pltpu.CompilerParams(dimension_semantics=("arbitrary",) * ngrid, vmem_limit_bytes=VMEM_LIMIT)


def _full(shape):
    return pl.BlockSpec(shape, lambda *_: (0,) * len(shape))


def _resident(shape):
    return pl.BlockSpec(shape, lambda *_: (0,) * len(shape), pipeline_mode=pl.Buffered(1))


NN = ((1,), (0,))
NT = ((1,), (1,))
TN = ((0,), (0,))


def _mm(name, grid, pairs, dims, acc_shape, epi, *, extras=(), outs=(), reds=(), aliases=None):
    nk = grid[-1]
    npair, nex, nout, nred = len(pairs), len(extras), len(outs), len(reds)

    def body(*refs):
        a_refs = refs[:npair]
        b_refs = refs[npair:2 * npair]
        p0 = 2 * npair
        e_refs = refs[p0:p0 + nex]
        o_refs = refs[p0 + nex:p0 + nex + nout]
        r_refs = refs[p0 + nex + nout:p0 + nex + nout + nred]
        ids = [pl.program_id(a) for a in range(len(grid))]
        k = ids[-1]
        if nred:
            first = ids[0] == 0
            for v in ids[1:]:
                first = first & (v == 0)

            @pl.when(first)
            def _():
                for r in r_refs:
                    r[...] = jnp.zeros(r.shape, r.dtype)

        part = None
        for a_ref, b_ref in zip(a_refs, b_refs):
            d = lax.dot_general(a_ref[...], b_ref[...], (dims, ((), ())), preferred_element_type=F32)
            part = d if part is None else part + d
        if nk == 1:
            epi(part, e_refs, o_refs, r_refs, ids)
        else:
            acc_ref = refs[-1]

            @pl.when(k == 0)
            def _():
                acc_ref[...] = part

            @pl.when(k > 0)
            def _():
                acc_ref[...] += part

            @pl.when(k == nk - 1)
            def _():
                epi(acc_ref[...], e_refs, o_refs, r_refs, ids)

    in_specs = [p[1] for p in pairs] + [p[3] for p in pairs] + [e[1] for e in extras]
    args = [p[0] for p in pairs] + [p[2] for p in pairs] + [e[0] for e in extras]
    out_shape = [jax.ShapeDtypeStruct(o[0], o[1]) for o in outs] + [jax.ShapeDtypeStruct(r, F32) for r in reds]
    out_specs = [o[2] for o in outs] + [_full(r) for r in reds]
    scratch_shapes = [pltpu.VMEM(acc_shape, F32)] if nk > 1 else []
    return pl.pallas_call(
        body, grid=grid, in_specs=in_specs, out_specs=out_specs, out_shape=out_shape, scratch_shapes=scratch_shapes,
        input_output_aliases=aliases or {}, compiler_params=_cparams(len(grid)), name=name)(*args)


def _rope(v, cos_t, sin_t):
    half = ROPE_DIM // 2
    first = (lax.broadcasted_iota(jnp.int32, cos_t.shape, 1) % HEAD_DIM) < half
    outs = []
    for cs in range(v.shape[1] // LANES):
        x = v[:, cs * LANES:(cs + 1) * LANES]
        partner = jnp.where(first, pltpu.roll(x, LANES - half, axis=1), pltpu.roll(x, half, axis=1))
        outs.append(x * cos_t + partner * sin_t)
    return outs[0] if len(outs) == 1 else jnp.concatenate(outs, axis=1)


def _spread_heads(v2, upper):
    other = pltpu.roll(v2, HEAD_DIM, axis=1)
    h0 = jnp.where(upper, other, v2)
    h1 = jnp.where(upper, v2, other)
    return jnp.concatenate([jnp.concatenate([h0, h0], axis=1), jnp.concatenate([h1, h1], axis=1)], axis=0)


def _sigmoid(v):
    return 0.5 * jnp.tanh(0.5 * v) + 0.5


def _rms_stats(v):
    r = lax.rsqrt(jnp.mean(v * v, axis=-1, keepdims=True) + EPS)
    return v * r, r


def _rms_bwd(dy, xhat, r, g):
    dxh = dy * g
    return r * (dxh - xhat * jnp.mean(dxh * xhat, axis=-1, keepdims=True))


def _head_sum_matrix():
    idx = np.arange(ATTN_W) // HEAD_DIM
    return jnp.asarray((idx[:, None] == idx[None, :]).astype(np.float32), dtype=BF16)


def _group_sum(v, e):
    hi = v.astype(BF16)
    lo = (v - hi.astype(F32)).astype(BF16)
    return jnp.dot(hi, e, preferred_element_type=F32) + jnp.dot(lo, e, preferred_element_type=F32)


TILE = 512


def _to_slabs(slab_ref, v):
    for cs in range(slab_ref.shape[0]):
        slab_ref[cs] = v[:, cs * LANES:(cs + 1) * LANES]


def _from_slabs(slab_ref):
    return jnp.concatenate([slab_ref[cs] for cs in range(slab_ref.shape[0])], axis=1)


def _class_rows(slab_ref, r, dil):
    n = slab_ref.shape[1] // dil
    return jnp.concatenate([slab_ref.at[cs][pl.ds(r, n, stride=dil), :] for cs in range(slab_ref.shape[0])], axis=1)


def _put_class_rows(slab_ref, r, dil, v):
    n = slab_ref.shape[1] // dil
    for cs in range(slab_ref.shape[0]):
        slab_ref.at[cs][pl.ds(r, n, stride=dil), :] = v[:, cs * LANES:(cs + 1) * LANES]


def _natural_from_group(slab_ref, grp_ref):
    dil = grp_ref.shape[0]
    for r in range(dil):
        _put_class_rows(slab_ref, r, dil, grp_ref[r].astype(F32))
    return _from_slabs(slab_ref)


def _group_from_natural(slab_ref, grp_ref, v):
    dil = grp_ref.shape[0]
    _to_slabs(slab_ref, v)
    for r in range(dil):
        grp_ref[r] = _class_rows(slab_ref, r, dil).astype(grp_ref.dtype)


def _group_spec(dil, tile, width):
    return pl.BlockSpec((dil, tile // dil, width), lambda i, *_: (0, i, 0))


def _slabs(tile, width):
    return pltpu.VMEM((width // LANES, tile, LANES), F32)


def _rope_consts():
    lane = np.arange(LANES) % HEAD_DIM
    fi = lane % (ROPE_DIM // 2)
    invf = np.where(lane < ROPE_DIM, ROPE_THETA ** (-(2.0 * fi) / ROPE_DIM), 0.0)
    sgn = np.where(lane < ROPE_DIM // 2, -1.0, np.where(lane < ROPE_DIM, 1.0, 0.0))
    return (jnp.asarray(invf.astype(np.float32)).reshape(1, LANES), jnp.asarray(sgn.astype(np.float32)).reshape(1, LANES))


def _rope_tables(pos_col):
    t = pos_col.shape[0]
    tile = min(t, TILE)
    invf, sgn = _rope_consts()

    def body(p_ref, f_ref, s_ref, c0, s0, c1, s1, c2, s2, slab_c, slab_s):
        ang = p_ref[...].astype(F32) * f_ref[...]
        cos, sin = jnp.cos(ang), jnp.sin(ang) * s_ref[...]
        c0[...] = cos
        s0[...] = sin
        _group_from_natural(slab_c, c1, cos)
        _group_from_natural(slab_s, s1, sin)
        for r in range(DILATIONS[2]):
            c2[r] = _class_rows(slab_c, r, DILATIONS[2])
            s2[r] = _class_rows(slab_s, r, DILATIONS[2])

    nat = pl.BlockSpec((tile, LANES), lambda i: (i, 0))
    specs, shapes = [nat, nat], [(t, LANES)] * 2
    for d in DILATIONS[1:]:
        specs += [_group_spec(d, tile, LANES)] * 2
        shapes += [(d, t // d, LANES)] * 2
    outs = pl.pallas_call(
        body, grid=(t // tile,),
        in_specs=[pl.BlockSpec((tile, 1), lambda i: (i, 0)), _full((1, LANES)), _full((1, LANES))],
        out_specs=specs, out_shape=[jax.ShapeDtypeStruct(s, F32) for s in shapes],
        scratch_shapes=[_slabs(tile, LANES)] * 2,
        compiler_params=_cparams(1), name="rope_tables")(pos_col, invf, sgn)
    return [(outs[2 * g].reshape(t, LANES), outs[2 * g + 1].reshape(t, LANES)) for g in range(len(DILATIONS))]


def _norm_fwd(x, g):
    t = x.shape[0]
    tile = min(t, TILE)

    def body(x_ref, g_ref, h0_ref, h1_ref, h2_ref, slab):
        xhat, _ = _rms_stats(x_ref[...])
        hn = xhat * g_ref[...]
        h0_ref[...] = hn.astype(BF16)
        _group_from_natural(slab, h1_ref, hn)
        for r in range(DILATIONS[2]):
            h2_ref[r] = _class_rows(slab, r, DILATIONS[2]).astype(BF16)

    nat = pl.BlockSpec((tile, D_MODEL), lambda i: (i, 0))
    return pl.pallas_call(
        body, grid=(t // tile,),
        in_specs=[nat, _full((1, D_MODEL))],
        out_specs=[nat] + [_group_spec(d, tile, D_MODEL) for d in DILATIONS[1:]],
        out_shape=[jax.ShapeDtypeStruct((t, D_MODEL), BF16)]
        + [jax.ShapeDtypeStruct((d, t // d, D_MODEL), BF16) for d in DILATIONS[1:]],
        scratch_shapes=[_slabs(tile, D_MODEL)],
        compiler_params=_cparams(1), name="norm1_fwd")(x, g)


GU_COLS = 3072
GROUP_COLS = 1536
GU_HALF = GU_COLS // 2


def _w_in_spec(width, block):
    return pl.BlockSpec((D_MODEL, width), lambda i: (0, block), pipeline_mode=pl.Buffered(1))


def _gu_w_specs():
    first = QKV_BLOCKS * ATTN_W // GU_HALF
    return [_w_in_spec(GU_HALF, first), _w_in_spec(GU_HALF, first + 1)]


def _group_w_specs(g):
    return [_w_in_spec(ATTN_W, _w_in_block(part, g)) for part in range(3)]


def _in_proj(hs, w_in, tables):
    t = hs[0].shape[0]
    tm = min(t, 1024)

    def body_gu(h_ref, w0_ref, w1_ref, o_ref):
        h = h_ref[...]
        o_ref[:, 0:GU_HALF] = jnp.dot(h, w0_ref[...], preferred_element_type=F32).astype(BF16)
        o_ref[:, GU_HALF:] = jnp.dot(h, w1_ref[...], preferred_element_type=F32).astype(BF16)

    gu = _token_call("in_proj_gates_uv", body_gu, t, tm,
                     [(hs[0], _rows_spec(tm, D_MODEL))] + [(w_in, s) for s in _gu_w_specs()],
                     [((t, GU_COLS), BF16, _rows_spec(tm, GU_COLS))])[0]

    qkvs = []
    for g in range(len(DILATIONS)):

        def body_qkv(h_ref, wq_ref, wk_ref, wv_ref, cos_ref, sin_ref, o_ref):
            h = h_ref[...]
            cos_w, sin_w = cos_ref[...], sin_ref[...]
            q = jnp.dot(h, wq_ref[...], preferred_element_type=F32)
            o_ref[:, 0:ATTN_W] = (_rope(q, cos_w, sin_w) * HEAD_DIM ** -0.5).astype(BF16)
            k = jnp.dot(h, wk_ref[...], preferred_element_type=F32)
            o_ref[:, ATTN_W:2 * ATTN_W] = _rope(k, cos_w, sin_w).astype(BF16)
            o_ref[:, 2 * ATTN_W:] = jnp.dot(h, wv_ref[...], preferred_element_type=F32).astype(BF16)

        cos_t, sin_t = tables[g]
        qkvs.append(_token_call(
            f"in_proj_qkv_g{g}", body_qkv, t, tm,
            [(hs[g].reshape(t, D_MODEL), _rows_spec(tm, D_MODEL))] + [(w_in, s) for s in _group_w_specs(g)]
            + [(cos_t, _rows_spec(tm, LANES)), (sin_t, _rows_spec(tm, LANES))],
            [((t, GROUP_COLS), BF16, _rows_spec(tm, GROUP_COLS))])[0])
    return gu, qkvs


def _attn_masks(n):
    row = lax.broadcasted_iota(jnp.int32, (2 * BLK, 2 * BLK), 0) % BLK
    col = lax.broadcasted_iota(jnp.int32, (2 * BLK, 2 * BLK), 1)
    diff = BLK + row - col
    valid = (diff >= 0) & (diff <= BLK) & ((col >= BLK) | (n > 0))
    upper = lax.broadcasted_iota(jnp.int32, (BLK, LANES), 1) >= HEAD_DIM
    return valid, upper


def _stack_heads(v2, upper):
    zero = jnp.zeros_like(v2)
    return jnp.concatenate([jnp.where(upper, zero, v2), jnp.where(upper, v2, zero)], axis=0)


def _unstack_heads(v, upper):
    return jnp.where(upper, v[BLK:], v[:BLK])


def _attn_fwd(qkv, g, dil):
    t = qkv.shape[0]
    length = t // dil
    nb = length // BLK
    view = qkv.reshape(dil, length, GROUP_COLS)

    def body(q_ref, kc_ref, kp_ref, vc_ref, vp_ref, o_ref, l_ref):
        n = pl.program_id(1)
        valid, upper = _attn_masks(n)
        for p in range(ATTN_W // LANES):
            sl = slice(p * LANES, (p + 1) * LANES)
            qs = _stack_heads(q_ref[:, sl], upper)
            k2 = jnp.concatenate([kp_ref[:, sl], kc_ref[:, sl]], axis=0)
            v2 = jnp.concatenate([vp_ref[:, sl], vc_ref[:, sl]], axis=0)
            s = lax.dot_general(qs, k2, (NT, ((), ())), preferred_element_type=F32)
            s = jnp.where(valid, s, NEG)
            m = jnp.max(s, axis=1, keepdims=True)
            pe = jnp.exp(s - m)
            den = jnp.sum(pe, axis=1, keepdims=True)
            o = jnp.dot(pe.astype(BF16), v2, preferred_element_type=F32) / den
            lse = jnp.broadcast_to(m + jnp.log(den), (2 * BLK, LANES))
            o_ref[:, sl] = _unstack_heads(o, upper)
            l_ref[:, sl] = _unstack_heads(lse, upper)

    cur = lambda part: pl.BlockSpec((None, BLK, ATTN_W), lambda r, n: (r, n, part))
    prev = lambda part: pl.BlockSpec((None, BLK, ATTN_W), lambda r, n: (r, jnp.maximum(n - 1, 0), part))
    out_spec = pl.BlockSpec((None, BLK, ATTN_W), lambda r, n: (r, n, 0))
    return pl.pallas_call(
        body, grid=(dil, nb),
        in_specs=[cur(0), cur(1), prev(1), cur(2), prev(2)],
        out_specs=[out_spec, out_spec],
        out_shape=[jax.ShapeDtypeStruct((dil, length, ATTN_W), F32)] * 2,
        compiler_params=_cparams(2), name=f"attn_fwd_g{g}")(view, view, view, view, view)


def _alphas(l0, l1, l2):
    m = jnp.maximum(jnp.maximum(l0, l1), l2)
    e0, e1, e2 = jnp.exp(l0 - m), jnp.exp(l1 - m), jnp.exp(l2 - m)
    inv = 1.0 / (e0 + e1 + e2)
    return e0 * inv, e1 * inv, e2 * inv


def _natural_group_values(o_refs, l_refs, slabs):
    os_ = [o_refs[0][0]] + [_natural_from_group(slabs[2 * g - 2], o_refs[g]) for g in (1, 2)]
    ls_ = [l_refs[0][0]] + [_natural_from_group(slabs[2 * g - 1], l_refs[g]) for g in (1, 2)]
    return os_, ls_


def _combine_fwd(os_, ls_):
    t = os_[0].shape[1]
    tile = min(t, TILE)

    def body(o0, o1, o2, l0, l1, l2, a_ref, *slabs):
        ov, lv = _natural_group_values((o0, o1, o2), (l0, l1, l2), slabs)
        a0, a1, a2 = _alphas(*lv)
        a_ref[...] = (a0 * ov[0] + a1 * ov[1] + a2 * ov[2]).astype(BF16)

    specs = [_group_spec(d, tile, ATTN_W) for d in DILATIONS]
    return pl.pallas_call(
        body, grid=(t // tile,), in_specs=specs * 2, out_specs=pl.BlockSpec((tile, ATTN_W), lambda i: (i, 0)),
        out_shape=jax.ShapeDtypeStruct((t, ATTN_W), BF16),
        scratch_shapes=[_slabs(tile, ATTN_W)] * 4,
        compiler_params=_cparams(1), name="combine_fwd")(*os_, *ls_)


def _combine_bwd(dattn, os_, ls_):
    t = dattn.shape[0]
    tile = min(t, TILE)
    e = _head_sum_matrix()

    def body(d_ref, o0, o1, o2, l0, l1, l2, e_ref, do0, do1, do2, c0, c1, c2, *slabs):
        ov, lv = _natural_group_values((o0, o1, o2), (l0, l1, l2), slabs)
        alphas = _alphas(*lv)
        d = d_ref[...]
        attn = alphas[0] * ov[0] + alphas[1] * ov[1] + alphas[2] * ov[2]
        s = _group_sum(d * attn, e_ref[...])
        do0[0] = (alphas[0] * d).astype(BF16)
        c0[0] = -alphas[0] * s
        for g, do_ref, c_ref in ((1, do1, c1), (2, do2, c2)):
            _group_from_natural(slabs[2 * g - 2], do_ref, alphas[g] * d)
            _group_from_natural(slabs[2 * g - 1], c_ref, -alphas[g] * s)

    specs = [_group_spec(d, tile, ATTN_W) for d in DILATIONS]
    shapes = [(d, t // d, ATTN_W) for d in DILATIONS]
    outs = pl.pallas_call(
        body, grid=(t // tile,),
        in_specs=[pl.BlockSpec((tile, ATTN_W), lambda i: (i, 0))] + specs * 2 + [_full((ATTN_W, ATTN_W))],
        out_specs=specs * 2,
        out_shape=[jax.ShapeDtypeStruct(s, BF16) for s in shapes] + [jax.ShapeDtypeStruct(s, F32) for s in shapes],
        scratch_shapes=[_slabs(tile, ATTN_W)] * 4,
        compiler_params=_cparams(1), name="combine_bwd")(dattn, *os_, *ls_, e)
    return outs[:3], outs[3:]


def _attn_bwd(qkv, do, cc, lse, cos_t, sin_t, g, dil):
    t = qkv.shape[0]
    length = t // dil
    nb = length // BLK
    qkv_v = qkv.reshape(dil, length, GROUP_COLS)
    cos_v, sin_v = (a.reshape(dil, length, LANES) for a in (cos_t, sin_t))
    scale = HEAD_DIM ** -0.5

    def body(q_ref, kc_ref, kp_ref, vc_ref, vp_ref, do_ref, c_ref, l_ref, cosc, sinc, cosp, sinp,
             out_ref, dq_s, dk_s, dv_s):
        n = pl.program_id(1)
        valid, upper = _attn_masks(n)

        @pl.when(n < nb)
        def _():
            cos_c, sin_c = cosc[...], sinc[...]
            cos_p, sin_p = cosp[...], sinp[...]
            dq_parts, dkp_parts, dkc_parts, dvp_parts, dvc_parts = [], [], [], [], []
            for p in range(ATTN_W // LANES):
                sl = slice(p * LANES, (p + 1) * LANES)
                qs = _stack_heads(q_ref[:, sl], upper)
                dos = _stack_heads(do_ref[:, sl], upper)
                k2 = jnp.concatenate([kp_ref[:, sl], kc_ref[:, sl]], axis=0)
                v2 = jnp.concatenate([vp_ref[:, sl], vc_ref[:, sl]], axis=0)
                l_col = _spread_heads(l_ref[:, sl], upper)
                c_col = _spread_heads(c_ref[:, sl], upper)
                s = lax.dot_general(qs, k2, (NT, ((), ())), preferred_element_type=F32)
                pe = jnp.exp(jnp.where(valid, s, NEG) - l_col)
                dpv = lax.dot_general(dos, v2, (NT, ((), ())), preferred_element_type=F32)
                ds = (pe * (dpv + c_col)).astype(BF16)
                dq2 = _unstack_heads(jnp.dot(ds, k2, preferred_element_type=F32), upper)
                dk2 = lax.dot_general(ds, qs, (TN, ((), ())), preferred_element_type=F32)
                dv2 = lax.dot_general(pe.astype(BF16), dos, (TN, ((), ())), preferred_element_type=F32)
                dq_parts.append(dq2)
                dkp_parts.append(dk2[:BLK])
                dkc_parts.append(dk2[BLK:])
                dvp_parts.append(dv2[:BLK])
                dvc_parts.append(dv2[BLK:])
            dq = _rope(jnp.concatenate(dq_parts, axis=1) * scale, cos_c, -sin_c)
            dkc = _rope(jnp.concatenate(dkc_parts, axis=1), cos_c, -sin_c)
            dkp = _rope(jnp.concatenate(dkp_parts, axis=1), cos_p, -sin_p)
            dvp = jnp.concatenate(dvp_parts, axis=1)
            dvc = jnp.concatenate(dvc_parts, axis=1)

            @pl.when(n > 0)
            def _():
                out_ref[:, 0:ATTN_W] = dq_s[...].astype(BF16)
                out_ref[:, ATTN_W:2 * ATTN_W] = (dk_s[...] + dkp).astype(BF16)
                out_ref[:, 2 * ATTN_W:3 * ATTN_W] = (dv_s[...] + dvp).astype(BF16)

            dq_s[...] = dq
            dk_s[...] = dkc
            dv_s[...] = dvc

        @pl.when(n == nb)
        def _():
            out_ref[:, 0:ATTN_W] = dq_s[...].astype(BF16)
            out_ref[:, ATTN_W:2 * ATTN_W] = dk_s[...].astype(BF16)
            out_ref[:, 2 * ATTN_W:3 * ATTN_W] = dv_s[...].astype(BF16)

    nc = lambda n: jnp.minimum(n, nb - 1)
    npv = lambda n: jnp.maximum(jnp.minimum(n, nb - 1) - 1, 0)
    cur = lambda part: pl.BlockSpec((None, BLK, ATTN_W), lambda r, n: (r, nc(n), part))
    prev = lambda part: pl.BlockSpec((None, BLK, ATTN_W), lambda r, n: (r, npv(n), part))
    row = pl.BlockSpec((None, BLK, ATTN_W), lambda r, n: (r, nc(n), 0))
    tab_c = pl.BlockSpec((None, BLK, LANES), lambda r, n: (r, nc(n), 0))
    tab_p = pl.BlockSpec((None, BLK, LANES), lambda r, n: (r, npv(n), 0))
    out_spec = pl.BlockSpec((None, BLK, GROUP_COLS), lambda r, n: (r, jnp.maximum(n - 1, 0), 0))
    out = pl.pallas_call(
        body, grid=(dil, nb + 1),
        in_specs=[cur(0), cur(1), prev(1), cur(2), prev(2), row, row, row, tab_c, tab_c, tab_p, tab_p],
        out_specs=out_spec,
        out_shape=jax.ShapeDtypeStruct((dil, length, GROUP_COLS), BF16),
        scratch_shapes=[pltpu.VMEM((BLK, ATTN_W), F32)] * 3,
        compiler_params=_cparams(2), name=f"attn_bwd_g{g}")(
            qkv_v, qkv_v, qkv_v, qkv_v, qkv_v, do, cc, lse, cos_v, sin_v, cos_v, sin_v)
    return out.reshape(t, GROUP_COLS)


SQRT_HALF = 0.7071067811865476
INV_SQRT_2PI = 0.3989422804014327


def _sgu_core(uv, g, b, w_ref, bias):
    cdf = 0.5 * (1.0 + lax.erf(uv * SQRT_HALF))
    z = uv * cdf
    u, v = z[:, :SGU_W], z[:, SGU_W:]
    mu = jnp.mean(v, axis=1, keepdims=True)
    xc = v - mu
    rs = lax.rsqrt(jnp.mean(xc * xc, axis=1, keepdims=True) + EPS)
    xhat = xc * rs
    vn = xhat * g + b
    row = lax.broadcasted_iota(jnp.int32, (SGU_CHUNK, SGU_CHUNK), 0)
    col = lax.broadcasted_iota(jnp.int32, (SGU_CHUNK, SGU_CHUNK), 1)
    tril = row >= col
    upper = lax.broadcasted_iota(jnp.int32, (SGU_CHUNK, LANES), 1) >= SGU_W // SGU_GROUPS
    ws, vlo, vhi, mixed = [], [], [], []
    for pr in range(SGU_W // LANES):
        sl = slice(pr * LANES, (pr + 1) * LANES)
        w0 = jnp.where(tril, w_ref[2 * pr], 0.0).astype(BF16)
        w1 = jnp.where(tril, w_ref[2 * pr + 1], 0.0).astype(BF16)
        vn2 = vn[:, sl]
        lo = jnp.where(upper, 0.0, vn2).astype(BF16)
        hi = jnp.where(upper, vn2, 0.0).astype(BF16)
        mixed.append(jnp.dot(w0, lo, preferred_element_type=F32) + jnp.dot(w1, hi, preferred_element_type=F32)
                     + bias[:, sl])
        ws.append((w0, w1))
        vlo.append(lo)
        vhi.append(hi)
    return cdf, u, xhat, rs, jnp.concatenate(mixed, axis=1), ws, vlo, vhi, tril, upper


def _sgu_fwd(gu, ln_g, ln_b, w_s, bias_exp):
    t = gu.shape[0]

    def body(uv_ref, g_ref, b_ref, w_ref, bias_ref, o_ref):
        _, u, _, _, mixed, *_ = _sgu_core(uv_ref[...].astype(F32), g_ref[...], b_ref[...], w_ref, bias_ref[...])
        o_ref[...] = (u * mixed).astype(BF16)

    return pl.pallas_call(
        body, grid=(t // SGU_CHUNK,),
        in_specs=[pl.BlockSpec((SGU_CHUNK, 2 * SGU_W), lambda n: (n, 0)), _full((1, SGU_W)), _full((1, SGU_W)),
                  _full((SGU_GROUPS, SGU_CHUNK, SGU_CHUNK)), _full((SGU_CHUNK, SGU_W))],
        out_specs=pl.BlockSpec((SGU_CHUNK, SGU_W), lambda n: (n, 0)),
        out_shape=jax.ShapeDtypeStruct((t, SGU_W), BF16),
        compiler_params=_cparams(1), name="sgu_fwd")(gu, ln_g, ln_b, w_s, bias_exp)


def _sgu_bwd(dproj, gu, dsgu, ln_g, ln_b, w_s, bias_exp):
    t = gu.shape[0]
    nchunks = t // SGU_CHUNK
    e = _head_sum_matrix()

    def body(dp_in, uv_ref, ds_ref, g_ref, b_ref, w_ref, bias_ref, e_ref, out_ref, dw_ref, dbias_ref, dg_ref, db_ref):
        n = pl.program_id(0)

        @pl.when(n == 0)
        def _():
            dw_ref[...] = jnp.zeros(dw_ref.shape, F32)
            dbias_ref[...] = jnp.zeros(dbias_ref.shape, F32)
            dg_ref[...] = jnp.zeros(dg_ref.shape, F32)
            db_ref[...] = jnp.zeros(db_ref.shape, F32)

        uv = uv_ref[...].astype(F32)
        g = g_ref[...]
        cdf, u, xhat, rs, mixed, ws, vlo, vhi, tril, upper = _sgu_core(uv, g, b_ref[...], w_ref, bias_ref[...])
        dsg = ds_ref[...]
        du = dsg * mixed
        dmixed = dsg * u
        dbias_ref[...] += dmixed
        dvn = []
        for pr in range(SGU_W // LANES):
            sl = slice(pr * LANES, (pr + 1) * LANES)
            dm2 = dmixed[:, sl]
            dlo = jnp.where(upper, 0.0, dm2).astype(BF16)
            dhi = jnp.where(upper, dm2, 0.0).astype(BF16)
            w0, w1 = ws[pr]
            dvn.append(lax.dot_general(w0, dlo, (TN, ((), ())), preferred_element_type=F32)
                       + lax.dot_general(w1, dhi, (TN, ((), ())), preferred_element_type=F32))
            dw0 = lax.dot_general(dlo, vlo[pr], (NT, ((), ())), preferred_element_type=F32)
            dw1 = lax.dot_general(dhi, vhi[pr], (NT, ((), ())), preferred_element_type=F32)
            dw_ref[2 * pr] += jnp.where(tril, dw0, 0.0)
            dw_ref[2 * pr + 1] += jnp.where(tril, dw1, 0.0)
        dvn = jnp.concatenate(dvn, axis=1)
        dg_ref[...] += jnp.sum(dvn * xhat, axis=0, keepdims=True)
        db_ref[...] += jnp.sum(dvn, axis=0, keepdims=True)
        dxh = dvn * g
        dv = rs * (dxh - jnp.mean(dxh, axis=1, keepdims=True) - xhat * jnp.mean(dxh * xhat, axis=1, keepdims=True))
        dz = jnp.concatenate([du, dv], axis=1)
        dgelu = cdf + uv * (INV_SQRT_2PI * jnp.exp(-0.5 * uv * uv))
        out_ref[...] = (dz * dgelu).astype(BF16)

        @pl.when(n == nchunks - 1)
        def _():
            dbias_ref[...] = _group_sum(dbias_ref[...], e_ref[...])

    outs = pl.pallas_call(
        body, grid=(nchunks,),
        in_specs=[pl.BlockSpec(memory_space=pl.ANY), pl.BlockSpec((SGU_CHUNK, 2 * SGU_W), lambda n: (n, 0)),
                  pl.BlockSpec((SGU_CHUNK, SGU_W), lambda n: (n, 0)), _full((1, SGU_W)), _full((1, SGU_W)),
                  _full((SGU_GROUPS, SGU_CHUNK, SGU_CHUNK)), _full((SGU_CHUNK, SGU_W)), _full((ATTN_W, ATTN_W))],
        out_specs=[pl.BlockSpec((SGU_CHUNK, 2 * SGU_W), lambda n: (n, 0)), _full((SGU_GROUPS, SGU_CHUNK, SGU_CHUNK)),
                   _full((SGU_CHUNK, SGU_W)), _full((1, SGU_W)), _full((1, SGU_W))],
        out_shape=[jax.ShapeDtypeStruct(dproj.shape, BF16), jax.ShapeDtypeStruct((SGU_GROUPS, SGU_CHUNK, SGU_CHUNK), F32),
                   jax.ShapeDtypeStruct((SGU_CHUNK, SGU_W), F32), jax.ShapeDtypeStruct((1, SGU_W), F32),
                   jax.ShapeDtypeStruct((1, SGU_W), F32)],
        input_output_aliases={0: 0},
        compiler_params=_cparams(1), name="sgu_bwd")(dproj, gu, dsgu, ln_g, ln_b, w_s, bias_exp, e)
    return outs


def _merge_fwd(attn, sgu, gu, x, w_pa, w_ps, w_out, g2):
    t = x.shape[0]
    tm = min(t, 512)

    def body(a_ref, s_ref, ga_ref, gb_ref, x_ref, wpa, wps, wo, g_ref, pa_ref, ps_ref, m_ref, x1_ref, h2_ref):
        pa = jnp.dot(a_ref[...], wpa[...], preferred_element_type=F32)
        ps = jnp.dot(s_ref[...], wps[...], preferred_element_type=F32)
        merged = (_sigmoid(ga_ref[...].astype(F32)) * pa + _sigmoid(gb_ref[...].astype(F32)) * ps).astype(BF16)
        x1 = x_ref[...] + jnp.dot(merged, wo[...], preferred_element_type=F32)
        xhat, _ = _rms_stats(x1)
        pa_ref[...] = pa.astype(BF16)
        ps_ref[...] = ps.astype(BF16)
        m_ref[...] = merged
        x1_ref[...] = x1
        h2_ref[...] = (xhat * g_ref[...]).astype(BF16)

    half = pl.BlockSpec((tm, ATTN_W), lambda i: (i, 0))
    full = pl.BlockSpec((tm, D_MODEL), lambda i: (i, 0))
    return pl.pallas_call(
        body, grid=(t // tm,),
        in_specs=[half, half, pl.BlockSpec((tm, D_MODEL), lambda i: (i, 1)), pl.BlockSpec((tm, D_MODEL), lambda i: (i, 2)),
                  full, _resident((ATTN_W, D_MODEL)), _resident((SGU_W, D_MODEL)), _resident((D_MODEL, D_MODEL)),
                  _full((1, D_MODEL))],
        out_specs=[full] * 5,
        out_shape=[jax.ShapeDtypeStruct((t, D_MODEL), BF16), jax.ShapeDtypeStruct((t, D_MODEL), BF16),
                   jax.ShapeDtypeStruct((t, D_MODEL), BF16), jax.ShapeDtypeStruct((t, D_MODEL), F32),
                   jax.ShapeDtypeStruct((t, D_MODEL), BF16)],
        compiler_params=_cparams(1), name="merge_fwd")(attn, sgu, gu, gu, x, w_pa, w_ps, w_out, g2)


def _merge_bwd(dx1b, gu, pa, ps, w_pa, w_ps, w_out):
    t = dx1b.shape[0]
    tm = min(t, 512)

    def body(d_ref, ga_ref, gb_ref, pa_ref, ps_ref, wpa, wps, wo, out_ref, dpa_ref, dps_ref, da_ref, dsg_ref):
        dm = lax.dot_general(d_ref[...], wo[...], (NT, ((), ())), preferred_element_type=F32)
        sa, sb = _sigmoid(ga_ref[...].astype(F32)), _sigmoid(gb_ref[...].astype(F32))
        dpa = (dm * sa).astype(BF16)
        dps = (dm * sb).astype(BF16)
        out_ref[:, 0:D_MODEL] = jnp.zeros((tm, D_MODEL), BF16)
        out_ref[:, D_MODEL:2 * D_MODEL] = (dm * pa_ref[...].astype(F32) * sa * (1.0 - sa)).astype(BF16)
        out_ref[:, 2 * D_MODEL:GU_COLS] = (dm * ps_ref[...].astype(F32) * sb * (1.0 - sb)).astype(BF16)
        dpa_ref[...] = dpa
        dps_ref[...] = dps
        da_ref[...] = lax.dot_general(dpa, wpa[...], (NT, ((), ())), preferred_element_type=F32)
        dsg_ref[...] = lax.dot_general(dps, wps[...], (NT, ((), ())), preferred_element_type=F32)

    half = pl.BlockSpec((tm, ATTN_W), lambda i: (i, 0))
    full = pl.BlockSpec((tm, D_MODEL), lambda i: (i, 0))
    return pl.pallas_call(
        body, grid=(t // tm,),
        in_specs=[full, pl.BlockSpec((tm, D_MODEL), lambda i: (i, 1)),
                  pl.BlockSpec((tm, D_MODEL), lambda i: (i, 2)), full, full,
                  _resident((ATTN_W, D_MODEL)), _resident((SGU_W, D_MODEL)), _resident((D_MODEL, D_MODEL))],
        out_specs=[pl.BlockSpec((tm, GU_COLS), lambda i: (i, 0)), full, full, half, half],
        out_shape=[jax.ShapeDtypeStruct((t, GU_COLS), BF16), jax.ShapeDtypeStruct((t, D_MODEL), BF16),
                   jax.ShapeDtypeStruct((t, D_MODEL), BF16), jax.ShapeDtypeStruct((t, ATTN_W), F32),
                   jax.ShapeDtypeStruct((t, SGU_W), F32)],
        compiler_params=_cparams(1), name="merge_bwd")(dx1b, gu, gu, pa, ps, w_pa, w_ps, w_out)


def _token_call(name, body, t, tm, ins, outs, reds=(), scratch=()):
    return pl.pallas_call(
        body, grid=(t // tm,), in_specs=[s for _, s in ins],
        out_specs=[o[2] for o in outs] + [_full(r) for r in reds],
        out_shape=[jax.ShapeDtypeStruct(o[0], o[1]) for o in outs] + [jax.ShapeDtypeStruct(r, F32) for r in reds],
        scratch_shapes=list(scratch), compiler_params=_cparams(1), name=name)(*[a for a, _ in ins])


def _rows_spec(tm, width):
    return pl.BlockSpec((tm, width), lambda i: (i, 0))


def _chips_spec(tm):
    return pl.BlockSpec((N_CHIPS, tm, FF_SHARD), lambda i: (0, i, 0))


def _zero_at_start(*refs):
    @pl.when(pl.program_id(0) == 0)
    def _():
        for r in refs:
            r[...] = jnp.zeros(r.shape, r.dtype)


def _ffn_fwd(h2, w_g, w_u):
    t = h2.shape[0]
    tm = min(t, 512)

    def body(h_ref, wg_ref, wu_ref, a_ref, b_ref, ff_ref):
        h = h_ref[...]
        for s in range(N_CHIPS):
            a = jnp.dot(h, wg_ref[s], preferred_element_type=F32)
            b = jnp.dot(h, wu_ref[s], preferred_element_type=F32)
            a_ref[s] = a.astype(BF16)
            b_ref[s] = b.astype(BF16)
            ff_ref[s] = (a * _sigmoid(a) * b).astype(BF16)

    shp = (N_CHIPS, t, FF_SHARD)
    w_spec = _resident((N_CHIPS, D_MODEL, FF_SHARD))
    return _token_call("ffn_fwd", body, t, tm, [(h2, _rows_spec(tm, D_MODEL)), (w_g, w_spec), (w_u, w_spec)],
                       [(shp, BF16, _chips_spec(tm))] * 3)


def _ffn_down_loss(ff, w_d, x1, tgt, gf):
    t = x1.shape[0]
    tm = min(t, 512)

    def body(ff_ref, wd_ref, x1_ref, tgt_ref, g_ref, dx2_ref, dx2b_ref, loss_ref, dgf_ref):
        _zero_at_start(loss_ref, dgf_ref)
        acc = jnp.dot(ff_ref[0], wd_ref[0], preferred_element_type=F32)
        for s in range(1, N_CHIPS):
            acc = acc + jnp.dot(ff_ref[s], wd_ref[s], preferred_element_type=F32)
        x2 = x1_ref[...] + acc
        g = g_ref[...]
        xhat, rr = _rms_stats(x2)
        diff = xhat * g - tgt_ref[...]
        rows = jnp.sum(diff * diff, axis=1, keepdims=True)
        loss_ref[...] += jnp.broadcast_to(jnp.sum(rows, axis=0, keepdims=True) * (0.5 / D_MODEL), (1, LANES))
        dy = diff * (1.0 / D_MODEL)
        dgf_ref[...] += jnp.sum(dy * xhat, axis=0, keepdims=True)
        dx2 = _rms_bwd(dy, xhat, rr, g)
        dx2_ref[...] = dx2
        dx2b_ref[...] = dx2.astype(BF16)

    row = _rows_spec(tm, D_MODEL)
    return _token_call("ffn_down_loss", body, t, tm,
                       [(ff, _chips_spec(tm)), (w_d, _resident((N_CHIPS, FF_SHARD, D_MODEL))), (x1, row), (tgt, row),
                        (gf, _full((1, D_MODEL)))],
                       [((t, D_MODEL), F32, row), ((t, D_MODEL), BF16, row)], reds=[(1, LANES), (1, D_MODEL)])


def _ffn_bwd_act(dx2b, w_d, a, b):
    t = dx2b.shape[0]
    tm = min(t, 512)

    def body(d_ref, wd_ref, a_ref, b_ref, da_ref, db_ref):
        d = d_ref[...]
        for s in range(N_CHIPS):
            dff = lax.dot_general(d, wd_ref[s], (NT, ((), ())), preferred_element_type=F32)
            av, bv = a_ref[s].astype(F32), b_ref[s].astype(F32)
            sg = _sigmoid(av)
            da_ref[s] = (dff * bv * (sg * (1.0 + av * (1.0 - sg)))).astype(BF16)
            db_ref[s] = (dff * (av * sg)).astype(BF16)

    shp = (N_CHIPS, t, FF_SHARD)
    return _token_call("ffn_bwd_act", body, t, tm,
                       [(dx2b, _rows_spec(tm, D_MODEL)), (w_d, _resident((N_CHIPS, FF_SHARD, D_MODEL))),
                        (a, _chips_spec(tm)), (b, _chips_spec(tm))],
                       [(shp, BF16, _chips_spec(tm))] * 2)


def _ffn_bwd_in(da, db, w_g, w_u, x1, dx2, g2):
    t = x1.shape[0]
    tm = min(t, 512)

    def body(da_ref, db_ref, wg_ref, wu_ref, x1_ref, dx2_ref, g_ref, dx1_ref, dx1b_ref, dg_ref):
        _zero_at_start(dg_ref)
        acc = None
        for s in range(N_CHIPS):
            part = (lax.dot_general(da_ref[s], wg_ref[s], (NT, ((), ())), preferred_element_type=F32)
                    + lax.dot_general(db_ref[s], wu_ref[s], (NT, ((), ())), preferred_element_type=F32))
            acc = part if acc is None else acc + part
        xhat, rr = _rms_stats(x1_ref[...])
        dg_ref[...] += jnp.sum(acc * xhat, axis=0, keepdims=True)
        dx1 = dx2_ref[...] + _rms_bwd(acc, xhat, rr, g_ref[...])
        dx1_ref[...] = dx1
        dx1b_ref[...] = dx1.astype(BF16)

    row = _rows_spec(tm, D_MODEL)
    w_spec = _resident((N_CHIPS, D_MODEL, FF_SHARD))
    return _token_call("ffn_bwd_in", body, t, tm,
                       [(da, _chips_spec(tm)), (db, _chips_spec(tm)), (w_g, w_spec), (w_u, w_spec), (x1, row), (dx2, row),
                        (g2, _full((1, D_MODEL)))],
                       [((t, D_MODEL), F32, row), ((t, D_MODEL), BF16, row)], reds=[(1, D_MODEL)])


def _group_dh(d_ref, w_refs):
    dh = None
    for part, w_ref in enumerate(w_refs):
        term = lax.dot_general(d_ref[:, part * ATTN_W:(part + 1) * ATTN_W], w_ref[...], (NT, ((), ())),
                               preferred_element_type=F32)
        dh = term if dh is None else dh + term
    return dh


def _in_proj_bwd(dgu, dqkvs, w_in, x, dx1, g1):
    t = x.shape[0]
    tile = min(t, TILE)
    tm = min(t, 1024)

    dhs = []
    for g in (1, 2):

        def body_g(d_ref, wq_ref, wk_ref, wv_ref, o_ref):
            o_ref[...] = _group_dh(d_ref, (wq_ref, wk_ref, wv_ref))

        dh = _token_call(
            f"in_proj_bwd_g{g}", body_g, t, tm,
            [(dqkvs[g], _rows_spec(tm, GROUP_COLS))] + [(w_in, s) for s in _group_w_specs(g)],
            [((t, D_MODEL), F32, _rows_spec(tm, D_MODEL))])[0]
        dhs.append(dh.reshape(DILATIONS[g], t // DILATIONS[g], D_MODEL))

    def body(dgu_ref, dq0_ref, w0_ref, w1_ref, wq_ref, wk_ref, wv_ref, x_ref, dx1_ref, g_ref, dh1_ref, dh2_ref,
             dx_ref, dg_ref, slab):
        _zero_at_start(dg_ref)
        dh = lax.dot_general(dgu_ref[:, 0:GU_HALF], w0_ref[...], (NT, ((), ())), preferred_element_type=F32)
        dh = dh + lax.dot_general(dgu_ref[:, GU_HALF:], w1_ref[...], (NT, ((), ())), preferred_element_type=F32)
        dh = dh + _group_dh(dq0_ref, (wq_ref, wk_ref, wv_ref))
        dh = dh + _natural_from_group(slab, dh1_ref)
        dh = dh + _natural_from_group(slab, dh2_ref)
        xhat, rr = _rms_stats(x_ref[...])
        dg_ref[...] += jnp.sum(dh * xhat, axis=0, keepdims=True)
        dx_ref[...] = dx1_ref[...] + _rms_bwd(dh, xhat, rr, g_ref[...])

    row = _rows_spec(tile, D_MODEL)
    return _token_call(
        "in_proj_bwd", body, t, tile,
        [(dgu, _rows_spec(tile, GU_COLS)), (dqkvs[0], _rows_spec(tile, GROUP_COLS))]
        + [(w_in, s) for s in _gu_w_specs() + _group_w_specs(0)]
        + [(x, row), (dx1, row), (g1, _full((1, D_MODEL))),
         (dhs[0], _group_spec(DILATIONS[1], tile, D_MODEL)), (dhs[1], _group_spec(DILATIONS[2], tile, D_MODEL))],
        [((t, D_MODEL), F32, row)], reds=[(1, D_MODEL)], scratch=[_slabs(tile, D_MODEL)])


def _epi_bf16(acc, e, o, r, ids):
    o[0][...] = acc.astype(BF16)


WGRAD_TK = 2048


def _wgrad_2d(name, a, b, tm, tn):
    t, k1 = a.shape
    n = b.shape[1]
    tk = min(t, WGRAD_TK)
    return _mm(name, (k1 // tm, n // tn, t // tk),
               [(a, pl.BlockSpec((tk, tm), lambda i, j, k: (k, i)), b, pl.BlockSpec((tk, tn), lambda i, j, k: (k, j)))],
               TN, (tm, tn), _epi_bf16, outs=[((k1, n), BF16, pl.BlockSpec((tm, tn), lambda i, j, k: (i, j)))])[0]


def _wgrad_in(hs, dgu, dqkvs):
    t = dgu.shape[0]
    tk = min(t, WGRAD_TK)
    gu_block = QKV_BLOCKS * ATTN_W // GU_HALF
    parts = [(hs[0], dgu, GU_HALF, lambda j: j + gu_block)]
    parts += [(hs[g].reshape(t, D_MODEL), dqkvs[g], ATTN_W, lambda j, g=g: _w_in_block(j, g)) for g in range(3)]
    dst = None
    for n, (a, b, tn, block_of) in enumerate(parts):
        dst = _mm(f"wgrad_in_{n}", (1, b.shape[1] // tn, t // tk),
                  [(a, pl.BlockSpec((tk, D_MODEL), lambda i, j, k: (k, 0)), b,
                    pl.BlockSpec((tk, tn), lambda i, j, k: (k, j)))],
                  TN, (D_MODEL, tn), _epi_bf16,
                  extras=[] if dst is None else [(dst, pl.BlockSpec(memory_space=pl.ANY))],
                  outs=[((D_MODEL, IN_COLS), BF16,
                         pl.BlockSpec((D_MODEL, tn), lambda i, j, k, block_of=block_of: (0, block_of(j))))],
                  aliases=None if dst is None else {2: 0})[0]
    return dst


def _wgrad_ff_in(name, h2, da):
    t = h2.shape[0]
    tk = min(t, WGRAD_TK)
    return _mm(name, (N_CHIPS, 1, t // tk),
               [(h2, pl.BlockSpec((tk, D_MODEL), lambda i, j, k: (k, 0)),
                 da, pl.BlockSpec((None, tk, FF_SHARD), lambda i, j, k: (i, k, 0)))],
               TN, (D_MODEL, FF_SHARD), _epi_bf16,
               outs=[((N_CHIPS, D_MODEL, FF_SHARD), BF16, pl.BlockSpec((None, D_MODEL, FF_SHARD), lambda i, j, k: (i, 0, 0)))])[0]


def _wgrad_ff_down(ff, dx2b):
    t = dx2b.shape[0]
    tk = min(t, WGRAD_TK)
    return _mm("wgrad_ffn_down", (N_CHIPS, 1, t // tk),
               [(ff, pl.BlockSpec((None, tk, FF_SHARD), lambda i, j, k: (i, k, 0)),
                 dx2b, pl.BlockSpec((tk, D_MODEL), lambda i, j, k: (k, 0)))],
               TN, (FF_SHARD, D_MODEL), _epi_bf16,
               outs=[((N_CHIPS, FF_SHARD, D_MODEL), BF16, pl.BlockSpec((None, FF_SHARD, D_MODEL), lambda i, j, k: (i, 0, 0)))])[0]


def _local_step(x, pos_col, tgt, g1, ln_g, ln_b, w_s, b_s, g2, gf, first_weight, late_weights, on_grads=None):
    tables = _rope_tables(pos_col)
    bias_exp = jnp.repeat(jnp.transpose(b_s), SGU_W // SGU_GROUPS, axis=1)

    hs = _norm_fwd(x, g1)
    w_p = first_weight(hs[0])
    gu, qkvs = _in_proj(hs, w_p, tables)
    os_, ls_ = [], []
    for g, dil in enumerate(DILATIONS):
        o, lse = _attn_fwd(qkvs[g], g, dil)
        os_.append(o)
        ls_.append(lse)
    attn = _combine_fwd(os_, ls_)
    sgu = _sgu_fwd(gu, ln_g, ln_b, w_s, bias_exp)
    w_pa, w_ps, w_out, w_g, w_u, w_d = late_weights(attn)
    pa, ps, merged, x1, h2 = _merge_fwd(attn, sgu, gu, x, w_pa, w_ps, w_out, g2)
    a, b, ff = _ffn_fwd(h2, w_g, w_u)
    dx2, dx2b, loss, dgf = _ffn_down_loss(ff, w_d, x1, tgt, gf)

    da, db = _ffn_bwd_act(dx2b, w_d, a, b)
    dw_d = _wgrad_ff_down(ff, dx2b)
    dx1, dx1b, dg2 = _ffn_bwd_in(da, db, w_g, w_u, x1, dx2, g2)
    dw_g = _wgrad_ff_in("wgrad_ffn_gate", h2, da)
    dw_u = _wgrad_ff_in("wgrad_ffn_up", h2, db)

    dgu, dpa, dps, dattn, dsgu = _merge_bwd(dx1b, gu, pa, ps, w_pa, w_ps, w_out)
    dw_out = _wgrad_2d("wgrad_out", merged, dx1b, D_MODEL, D_MODEL)
    dw_pa = _wgrad_2d("wgrad_proj_attn", attn, dpa, ATTN_W, D_MODEL)
    dw_ps = _wgrad_2d("wgrad_proj_sgu", sgu, dps, SGU_W, D_MODEL)
    if on_grads is not None:
        ln_g = ln_g + on_grads(1, dict(w_proj_attn=dw_pa, w_proj_sgu=dw_ps, w_out=dw_out, w_ffn_gate=dw_g, w_ffn_up=dw_u,
                                       w_ffn_down=dw_d))[:, :SGU_W]
    dgu, dw_s, dbias, dln_g, dln_b = _sgu_bwd(dgu, gu, dsgu, ln_g, ln_b, w_s, bias_exp)
    dos, ccs = _combine_bwd(dattn, os_, ls_)
    dqkvs = [_attn_bwd(qkvs[g], dos[g], ccs[g], ls_[g], *tables[g], g, dil) for g, dil in enumerate(DILATIONS)]
    dw_p = _wgrad_in(hs, dgu, dqkvs)
    if on_grads is not None:
        g1 = g1 + on_grads(0, dict(w_in=dw_p))
    dx, dg1 = _in_proj_bwd(dgu, dqkvs, w_p, x, dx1, g1)

    db_s = jnp.transpose(dbias[:, ::SGU_W // SGU_GROUPS])
    small = dict(loss=loss, norm1_g=dg1, sgu_ln_g=dln_g, sgu_ln_b=dln_b, w_spatial=dw_s, b_spatial=db_s,
                 norm2_g=dg2, final_g=dgf)
    big = dict(w_in=dw_p, w_proj_attn=dw_pa, w_proj_sgu=dw_ps, w_out=dw_out, w_ffn_gate=dw_g, w_ffn_up=dw_u,
               w_ffn_down=dw_d)
    return dx, big, small


def _ew(name, fn, ins, out_dtypes):
    shp = ins[0].shape
    rows, cols = shp
    tr = next((cand for cand in (256, 352, 128) if rows % cand == 0 and rows > cand), rows)

    def body(*refs):
        res = fn(*[r[...] for r in refs[:len(ins)]])
        for o_ref, v in zip(refs[len(ins):], res):
            o_ref[...] = v.astype(o_ref.dtype)

    spec = pl.BlockSpec((tr, cols), lambda i: (i, 0))
    return pl.pallas_call(
        body, grid=(rows // tr,), in_specs=[spec] * len(ins), out_specs=[spec] * len(out_dtypes),
        out_shape=[jax.ShapeDtypeStruct(shp, d) for d in out_dtypes],
        compiler_params=_cparams(1), name=name)(*ins)


def _adamw_math(g, w, m, v):
    m = ADAM_B1 * m + (1.0 - ADAM_B1) * g
    v = ADAM_B2 * v + (1.0 - ADAM_B2) * (g * g)
    m_hat = m / (1.0 - ADAM_B1 ** ADAM_STEP)
    v_hat = v / (1.0 - ADAM_B2 ** ADAM_STEP)
    delta = -ADAM_LR * (m_hat / (jnp.sqrt(v_hat) + ADAM_EPS) + ADAM_WD * w)
    return delta, m, v


def _adamw(name, g, w, m, v):
    return _ew(name, lambda g_, w_, m_, v_: (g_,) + _adamw_math(g_, w_, m_, v_), [g, w, m, v], [F32] * 4)


VMEM_SPEC = pl.BlockSpec(memory_space=pltpu.VMEM)


def _for_row_chunks(rows, fn):
    ck = next(c for c in (64, 32, 16) if rows % c == 0)

    def step(i, carry):
        fn(pl.multiple_of(i * ck, ck), ck)
        return carry

    lax.fori_loop(0, rows // ck, step, 0)


def _place():
    x, y, c = lax.axis_index("x"), lax.axis_index("y"), lax.axis_index("c")
    chips = [(1 - x, y), (x, 1 - y), (1 - x, 1 - y)]
    return x, y, c, 2 * x + y, chips


def _rows(ref, start, size):
    if len(ref.shape) == 2:
        return ref.at[pl.ds(start, size), :]
    return ref.at[:, pl.ds(start, size), :]


def _comm_call(name, body, ins, out_shapes, scratch, n_remote):
    return pl.pallas_call(
        body, in_specs=[VMEM_SPEC] * len(ins), out_specs=[VMEM_SPEC] * len(out_shapes),
        out_shape=out_shapes,
        scratch_shapes=list(scratch) + [pltpu.SemaphoreType.DMA((n_remote,)), pltpu.SemaphoreType.DMA((n_remote,))],
        compiler_params=pltpu.CompilerParams(vmem_limit_bytes=VMEM_LIMIT), name=name)(*ins)


def _gather_finish(name, shard, landed):
    k_rows, n = shard.shape
    kh = k_rows // 2

    def body(shard_ref, land_ref, out_ref, send, recv):
        x, y, c, me, chips = _place()
        passed = []
        for j, chip in enumerate(chips):
            theirs = 2 * chip[0] + chip[1]
            cp = pltpu.make_async_remote_copy(
                src_ref=land_ref.at[j], dst_ref=_rows(out_ref.at[theirs], c * kh, kh), send_sem=send.at[j],
                recv_sem=recv.at[j], device_id=(x, y, 1 - c), device_id_type=MESH)
            cp.start()
            passed.append(cp)
        mine = out_ref.at[me]

        def put_own(r0, ck):
            mine[pl.ds(r0, ck), :] = shard_ref[pl.ds(r0, ck), :]

        _for_row_chunks(k_rows, put_own)
        for j, chip in enumerate(chips):
            slot = out_ref.at[2 * chip[0] + chip[1]]

            def put_half(r0, ck, j=j, slot=slot):
                slot[pl.ds(pl.multiple_of(c * kh + r0, ck), ck), :] = land_ref[j, pl.ds(r0, ck), :]

            _for_row_chunks(kh, put_half)
        for j, chip in enumerate(chips):
            other = _rows(out_ref.at[2 * chip[0] + chip[1]], (1 - c) * kh, kh)
            pltpu.make_async_remote_copy(src_ref=other, dst_ref=other, send_sem=send.at[j], recv_sem=recv.at[j],
                                         device_id=(x, y, 1 - c), device_id_type=MESH).wait_recv()
        for cp in passed:
            cp.wait_send()

    return _comm_call(name, body, [shard, landed], [jax.ShapeDtypeStruct((N_CHIPS, k_rows, n), shard.dtype)], [], 3)[0]


def _pair_reduce(name, grads):
    nt = len(grads)

    def half(s):
        shp = list(s.shape)
        shp[-2] //= 2
        return tuple(shp)

    def body(*refs):
        ins, outs, got = refs[:nt], refs[nt:2 * nt], refs[2 * nt:3 * nt]
        send, recv = refs[3 * nt:]
        x, y, c, me, chips = _place()
        copies = []
        for t in range(nt):
            kh = ins[t].shape[-2] // 2
            rc = pltpu.make_async_remote_copy(
                src_ref=_rows(ins[t], (1 - c) * kh, kh), dst_ref=got[t], send_sem=send.at[t], recv_sem=recv.at[t],
                device_id=(x, y, 1 - c), device_id_type=MESH)
            rc.start()
            copies.append(rc)
        for t in range(nt):
            kh = ins[t].shape[-2] // 2
            copies[t].wait_recv()
            for lead in ([()] if len(ins[t].shape) == 2 else [(s,) for s in range(ins[t].shape[0])]):

                def add(r0, ck, lead=lead, src=ins[t], oth=got[t], dst=outs[t], kh=kh):
                    own = src[lead + (pl.ds(pl.multiple_of(c * kh + r0, ck), ck), slice(None))]
                    rows = lead + (pl.ds(r0, ck), slice(None))
                    dst[rows] = (own.astype(F32) + oth[rows].astype(F32)).astype(BF16)

                _for_row_chunks(kh, add)
        for rc in copies:
            rc.wait_send()

    shapes = [half(g) for g in grads]
    return _comm_call(name, body, grads, [jax.ShapeDtypeStruct(s, BF16) for s in shapes],
                      [pltpu.VMEM(s, BF16) for s in shapes], nt)


HBM_SPEC = pl.BlockSpec(memory_space=pltpu.HBM)
SEM_SPEC = pl.BlockSpec(memory_space=pltpu.SEMAPHORE)
DATAFLOW = pltpu.SideEffectType.DATAFLOW_SIDE_EFFECTING
TOKEN_SHAPE = (1, D_MODEL)


def _shard_cols(shape):
    return shape[2] if len(shape) == 3 else shape[1] // N_CHIPS


def _chip_piece(ref, j):
    if len(ref.shape) == 3:
        return ref.at[j]
    n4 = ref.shape[1] // N_CHIPS
    return ref.at[:, pl.ds(j * n4, n4)]


def _exchange_copies(sums, lands, send, recv):
    x, y, c, me, chips = _place()
    return [pltpu.make_async_remote_copy(
        src_ref=_chip_piece(sums[t], 2 * chip[0] + chip[1]), dst_ref=lands[t].at[j], send_sem=send.at[t * 3 + j],
        recv_sem=recv.at[t * 3 + j], device_id=(*chip, c), device_id_type=MESH)
        for t in range(len(sums)) for j, chip in enumerate(chips)]


def _gather_copies(shards, lands, send, recv):
    x, y, c, me, chips = _place()
    return [pltpu.make_async_remote_copy(
        src_ref=shards[t], dst_ref=lands[t].at[me], send_sem=send.at[t * 3 + j], recv_sem=recv.at[t * 3 + j],
        device_id=(*chip, c), device_id_type=MESH)
        for t in range(len(shards)) for j, chip in enumerate(chips)]


def _gather_half_copies(shards, lands, send, recv):
    x, y, c, me, chips = _place()
    return [pltpu.make_async_remote_copy(
        src_ref=_rows(shards[t], c * (shards[t].shape[0] // 2), shards[t].shape[0] // 2), dst_ref=lands[t].at[j],
        send_sem=send.at[t * 3 + j], recv_sem=recv.at[t * 3 + j], device_id=(*chip, c), device_id_type=MESH)
        for t in range(len(shards)) for j, chip in enumerate(chips)]


def _split_start(name, copies, srcs, land_shapes):
    nt = len(srcs)
    lands = [lax.empty(s, BF16) for s in land_shapes]

    def body(*refs):
        send, recv = refs[2 * nt], refs[2 * nt + 1]
        for cp in copies(refs[:nt], refs[nt:2 * nt], send, recv):
            cp.start()
        refs[-1][...] = jnp.zeros(TOKEN_SHAPE, F32)

    hbm = lambda a: pltpu.with_memory_space_constraint(a, pltpu.HBM)
    outs = pl.pallas_call(
        body, name=name,
        out_shape=[pltpu.SemaphoreType.DMA((3 * nt,)), pltpu.SemaphoreType.DMA((3 * nt,))]
        + [pltpu.HBM(s.shape, s.dtype) for s in srcs] + [pltpu.HBM(l.shape, l.dtype) for l in lands]
        + [jax.ShapeDtypeStruct(TOKEN_SHAPE, F32)],
        in_specs=[HBM_SPEC] * (2 * nt), out_specs=[SEM_SPEC, SEM_SPEC] + [HBM_SPEC] * (2 * nt) + [VMEM_SPEC],
        input_output_aliases={i: 2 + i for i in range(2 * nt)},
        compiler_params=pltpu.CompilerParams(has_side_effects=DATAFLOW))(*[hbm(a) for a in list(srcs) + lands])
    return outs[0], outs[1], outs[2:2 + nt], outs[2 + nt:2 + 2 * nt], outs[-1]


def _split_wait(name, copies, send, recv, srcs, lands, after):
    nt = len(srcs)

    def body(*refs):
        for cp in copies(refs[:nt], refs[nt:2 * nt], refs[2 * nt], refs[2 * nt + 1]):
            cp.wait_send()
            cp.wait_recv()

    outs = pl.pallas_call(
        body, name=name,
        out_shape=[pltpu.HBM(s.shape, s.dtype) for s in srcs] + [pltpu.HBM(l.shape, l.dtype) for l in lands],
        in_specs=[HBM_SPEC] * (2 * nt) + [SEM_SPEC, SEM_SPEC, pl.BlockSpec(memory_space=pl.ANY)],
        out_specs=[HBM_SPEC] * (2 * nt), input_output_aliases={i: i for i in range(2 * nt)},
        compiler_params=pltpu.CompilerParams(has_side_effects=DATAFLOW))(*srcs, *lands, send, recv, after)
    return outs[:nt], outs[nt:]


def _chip_sum(name, sums, lands):
    nt = len(sums)

    def body(*refs):
        ins, slots, outs = refs[:nt], refs[nt:2 * nt], refs[2 * nt:3 * nt]
        send, recv = refs[3 * nt:]
        x, y, c, me, chips = _place()
        sibling = (x, y, 1 - c)
        handed = []
        for t in range(nt):
            kh, n4 = ins[t].shape[-2], outs[t].shape[1]
            for jj in range(N_CHIPS):

                @pl.when(me == jj)
                def _(jj=jj, src=ins[t], slot=slots[t], dst=outs[t], kh=kh, n4=n4):
                    def add(r0, ck):
                        rows = pl.ds(r0, ck)
                        own = src[jj, rows, :] if len(src.shape) == 3 else src[rows, jj * n4:(jj + 1) * n4]
                        acc = ((own.astype(F32) + slot[0, rows, :].astype(F32)) + slot[1, rows, :].astype(F32)) \
                            + slot[2, rows, :].astype(F32)
                        dst[pl.ds(pl.multiple_of(c * kh + r0, ck), ck), :] = acc

                    _for_row_chunks(kh, add)

            rc = pltpu.make_async_remote_copy(
                src_ref=_rows(outs[t], c * kh, kh), dst_ref=_rows(outs[t], c * kh, kh), send_sem=send.at[t],
                recv_sem=recv.at[t], device_id=sibling, device_id_type=MESH)
            rc.start()
            handed.append(rc)
        for t in range(nt):
            kh = ins[t].shape[-2]
            other = _rows(outs[t], (1 - c) * kh, kh)
            pltpu.make_async_remote_copy(
                src_ref=other, dst_ref=other, send_sem=send.at[t], recv_sem=recv.at[t],
                device_id=sibling, device_id_type=MESH).wait_recv()
        for rc in handed:
            rc.wait_send()

    out_shapes = [jax.ShapeDtypeStruct((2 * s.shape[-2], _shard_cols(s.shape)), F32) for s in sums]
    return _comm_call(name, body, list(sums) + list(lands), out_shapes, [], nt)


VEC_SHAPE = (8, D_MODEL + LANES)
VEC_SLOTS = dict(norm1_g=(slice(0, 1), slice(0, D_MODEL)), norm2_g=(slice(1, 2), slice(0, D_MODEL)),
                 final_g=(slice(2, 3), slice(0, D_MODEL)), sgu_ln_g=(slice(3, 4), slice(0, SGU_W)),
                 sgu_ln_b=(slice(3, 4), slice(SGU_W, 2 * SGU_W)), b_spatial=(slice(0, 8), slice(D_MODEL, D_MODEL + LANES)),
                 loss=(slice(4, 5), slice(0, LANES)))
VEC_PARAMS = ("norm1_g", "norm2_g", "final_g", "sgu_ln_g", "sgu_ln_b", "b_spatial")
SMALL_PARAMS = VEC_PARAMS + ("w_spatial",)
W_SPATIAL_2D = (SGU_GROUPS * SGU_CHUNK, SGU_CHUNK)


def _small_step(partials, w, m, v):
    def shape2d(name):
        if name == "w_spatial":
            return W_SPATIAL_2D
        rows, cols = VEC_SLOTS[name]
        return (rows.stop - rows.start, cols.stop - cols.start)

    g_names = VEC_PARAMS + ("loss", "w_spatial")
    ng, npar = len(g_names), len(SMALL_PARAMS)

    def pack(dst, parts):
        dst[...] = jnp.zeros(VEC_SHAPE, F32)
        for n, ref in parts.items():
            if n in VEC_SLOTS:
                dst[VEC_SLOTS[n]] = ref[...]

    def reduce_body(*refs):
        g_in = dict(zip(g_names, refs[:ng]))
        vec_out, ws_out, vec, vec_pair, vec_slot, ws_pair, ws_slot, send, recv = refs[ng:]
        x, y, c, me, chips = _place()
        sibling = (x, y, 1 - c)
        pack(vec, g_in)
        copies = []

        def allreduce(k0, src, pair, slot):
            first = pltpu.make_async_remote_copy(src_ref=src, dst_ref=pair, send_sem=send.at[k0], recv_sem=recv.at[k0],
                                                 device_id=sibling, device_id_type=MESH)
            first.start()
            first.wait_recv()
            slot[me] = src[...] + pair[...]
            arrivals = []
            for j, chip in enumerate(chips):
                theirs = 2 * chip[0] + chip[1]
                rc = pltpu.make_async_remote_copy(src_ref=slot.at[me], dst_ref=slot.at[me], send_sem=send.at[k0 + 1 + j],
                                                  recv_sem=recv.at[k0 + 1 + j], device_id=(*chip, c), device_id_type=MESH)
                rc.start()
                arrivals.append(pltpu.make_async_remote_copy(
                    src_ref=slot.at[theirs], dst_ref=slot.at[theirs], send_sem=send.at[k0 + 1 + j],
                    recv_sem=recv.at[k0 + 1 + j], device_id=(*chip, c), device_id_type=MESH))
                copies.append(rc)
            copies.append(first)
            return arrivals

        arrivals = allreduce(0, vec, vec_pair, vec_slot) + allreduce(4, g_in["w_spatial"], ws_pair, ws_slot)
        for a in arrivals:
            a.wait_recv()
        vec_out[...] = ((vec_slot[0] + vec_slot[1]) + vec_slot[2]) + vec_slot[3]

        def spatial(r0, ck):
            rows = pl.ds(r0, ck)
            ws_out[rows, :] = ((ws_slot[0, rows, :] + ws_slot[1, rows, :]) + ws_slot[2, rows, :]) + ws_slot[3, rows, :]

        _for_row_chunks(W_SPATIAL_2D[0], spatial)
        for rc in copies:
            rc.wait_send()

    g_vec, g_ws = pl.pallas_call(
        reduce_body, in_specs=[VMEM_SPEC] * ng, out_specs=[VMEM_SPEC] * 2,
        out_shape=[jax.ShapeDtypeStruct(VEC_SHAPE, F32), jax.ShapeDtypeStruct(W_SPATIAL_2D, F32)],
        scratch_shapes=[pltpu.VMEM(VEC_SHAPE, F32), pltpu.VMEM(VEC_SHAPE, F32), pltpu.VMEM((N_CHIPS,) + VEC_SHAPE, F32),
                        pltpu.VMEM(W_SPATIAL_2D, F32), pltpu.VMEM((N_CHIPS,) + W_SPATIAL_2D, F32),
                        pltpu.SemaphoreType.DMA((8,)), pltpu.SemaphoreType.DMA((8,))],
        name="small_params_allreduce")(*[partials[n].reshape(shape2d(n)) for n in g_names])

    def update_body(*refs):
        gv_ref, gw_ref = refs[:2]
        w_in, m_in, v_in = (dict(zip(SMALL_PARAMS, refs[2 + k * npar:2 + (k + 1) * npar])) for k in range(3))
        o0 = 2 + 3 * npar
        g_out = dict(zip(g_names, refs[o0:o0 + ng]))
        d_out, m_out, v_out = (dict(zip(SMALL_PARAMS, refs[o0 + ng + k * npar:o0 + ng + (k + 1) * npar])) for k in range(3))
        vw, vm, vv = refs[o0 + ng + 3 * npar:]
        pack(vw, w_in)
        pack(vm, m_in)
        pack(vv, v_in)
        d_vec, m_vec, v_vec = _adamw_math(gv_ref[...], vw[...], vm[...], vv[...])
        vw[...] = d_vec
        vm[...] = m_vec
        vv[...] = v_vec
        for n in VEC_PARAMS + ("loss",):
            g_out[n][...] = gv_ref[VEC_SLOTS[n]]
        for n in VEC_PARAMS:
            d_out[n][...] = vw[VEC_SLOTS[n]]
            m_out[n][...] = vm[VEC_SLOTS[n]]
            v_out[n][...] = vv[VEC_SLOTS[n]]

        def spatial(r0, ck):
            rows = pl.ds(r0, ck)
            g = gw_ref[rows, :]
            d_, m_, v_ = _adamw_math(g, w_in["w_spatial"][rows, :], m_in["w_spatial"][rows, :], v_in["w_spatial"][rows, :])
            g_out["w_spatial"][rows, :] = g
            d_out["w_spatial"][rows, :] = d_
            m_out["w_spatial"][rows, :] = m_
            v_out["w_spatial"][rows, :] = v_

        _for_row_chunks(W_SPATIAL_2D[0], spatial)

    ins = [g_vec, g_ws]
    for src in (w, m, v):
        ins += [src[n].reshape(shape2d(n)) for n in SMALL_PARAMS]
    out_shapes = [jax.ShapeDtypeStruct(shape2d(n), F32) for n in g_names + SMALL_PARAMS * 3]
    outs = pl.pallas_call(
        update_body, in_specs=[VMEM_SPEC] * len(ins), out_specs=[VMEM_SPEC] * len(out_shapes), out_shape=out_shapes,
        scratch_shapes=[pltpu.VMEM(VEC_SHAPE, F32)] * 3, name="small_params_update")(*ins)
    grads = dict(zip(g_names, outs[:ng]))
    rest = [dict(zip(SMALL_PARAMS, outs[ng + k * npar:ng + (k + 1) * npar])) for k in range(3)]
    return grads, rest[0], rest[1], rest[2]


BIG = ("w_in", "w_proj_attn", "w_proj_sgu", "w_out", "w_ffn_gate", "w_ffn_up", "w_ffn_down")
COMM_GROUPS = (("w_in",), ("w_proj_attn", "w_proj_sgu", "w_out", "w_ffn_gate", "w_ffn_up", "w_ffn_down"))
WEIGHTS = ("norm1_g", "w_in", "sgu_ln_g", "sgu_ln_b", "w_spatial", "b_spatial", "w_proj_attn", "w_proj_sgu", "w_out",
           "norm2_g", "w_ffn_gate", "w_ffn_up", "w_ffn_down", "final_g")


def _cols_from_chips(g):
    return jnp.transpose(g, (1, 0, 2)).reshape(g.shape[1], N_CHIPS * g.shape[2])


def kernel(x, positions, norm1_g, w_in, sgu_ln_g, sgu_ln_b, w_spatial, b_spatial, w_proj_attn, w_proj_sgu, w_out, norm2_g, w_ffn_gate, w_ffn_up, w_ffn_down, final_g, loss_target, m_norm1_g, m_w_in, m_sgu_ln_g, m_sgu_ln_b, m_w_spatial, m_b_spatial, m_w_proj_attn, m_w_proj_sgu, m_w_out, m_norm2_g, m_w_ffn_gate, m_w_ffn_up, m_w_ffn_down, m_final_g, v_norm1_g, v_w_in, v_sgu_ln_g, v_sgu_ln_b, v_w_spatial, v_b_spatial, v_w_proj_attn, v_w_proj_sgu, v_w_out, v_norm2_g, v_w_ffn_gate, v_w_ffn_up, v_w_ffn_down, v_final_g):
    w = dict(norm1_g=norm1_g, w_in=w_in, sgu_ln_g=sgu_ln_g, sgu_ln_b=sgu_ln_b, w_spatial=w_spatial, b_spatial=b_spatial,
             w_proj_attn=w_proj_attn, w_proj_sgu=w_proj_sgu, w_out=w_out, norm2_g=norm2_g, w_ffn_gate=w_ffn_gate,
             w_ffn_up=w_ffn_up, w_ffn_down=w_ffn_down, final_g=final_g)
    m = dict(norm1_g=m_norm1_g, w_in=m_w_in, sgu_ln_g=m_sgu_ln_g, sgu_ln_b=m_sgu_ln_b, w_spatial=m_w_spatial,
             b_spatial=m_b_spatial, w_proj_attn=m_w_proj_attn, w_proj_sgu=m_w_proj_sgu, w_out=m_w_out, norm2_g=m_norm2_g,
             w_ffn_gate=m_w_ffn_gate, w_ffn_up=m_w_ffn_up, w_ffn_down=m_w_ffn_down, final_g=m_final_g)
    v = dict(norm1_g=v_norm1_g, w_in=v_w_in, sgu_ln_g=v_sgu_ln_g, sgu_ln_b=v_sgu_ln_b, w_spatial=v_w_spatial,
             b_spatial=v_b_spatial, w_proj_attn=v_w_proj_attn, w_proj_sgu=v_w_proj_sgu, w_out=v_w_out, norm2_g=v_norm2_g,
             w_ffn_gate=v_w_ffn_gate, w_ffn_up=v_w_ffn_up, w_ffn_down=v_w_ffn_down, final_g=v_final_g)
    t = x.shape[1]

    shards = {n: _ew(f"cast_{n}", lambda a: (a,), [w[n][0]], [BF16])[0] for n in BIG}
    late = COMM_GROUPS[1]
    k_in, n_in = shards["w_in"].shape
    *first, token = _split_start("gather_start_0", _gather_half_copies, [shards["w_in"]], [(3, k_in // 2, n_in)])
    pending = {}

    def first_weight(after):
        srcs, filled = _split_wait("gather_wait_0", _gather_half_copies, *first, after)
        gath_in, late_shards = lax.optimization_barrier(
            (_gather_finish("gather_finish_0", srcs[0], filled[0]), [shards[n] for n in late]))
        *pending["late"], _ = _split_start(
            "gather_start_1", _gather_copies, late_shards, [(N_CHIPS,) + s.shape for s in late_shards])
        return _cols_from_chips(gath_in)

    def late_weights(after):
        srcs, filled = _split_wait("gather_wait_1", _gather_copies, *pending["late"], after)
        me = 2 * lax.axis_index("x") + lax.axis_index("y")
        gath = {n: lax.dynamic_update_slice(f, s[None], (me, 0, 0)) for n, f, s in zip(late, filled, srcs)}
        return (_cols_from_chips(gath["w_proj_attn"]), _cols_from_chips(gath["w_proj_sgu"]),
                gath["w_out"].reshape(D_MODEL, D_MODEL), gath["w_ffn_gate"], gath["w_ffn_up"], gath["w_ffn_down"])

    exchanges = {}

    def on_grads(i, partials):
        if "w_out" in partials:
            partials["w_out"] = partials["w_out"].reshape(N_CHIPS, D_MODEL // N_CHIPS, D_MODEL)
        sums = _pair_reduce(f"rs_pair_reduce_{i}", [partials[n] for n in COMM_GROUPS[i]])
        *exchanges[i], started = _split_start(
            f"rs_exchange_start_{i}", _exchange_copies, sums, [(3, s.shape[-2], _shard_cols(s.shape)) for s in sums])
        return started

    dx, _, small = _local_step(
        x[0], positions.reshape(t, 1), loss_target[0], norm1_g + token, sgu_ln_g, sgu_ln_b, w_spatial[0], b_spatial[0],
        norm2_g, final_g.reshape(1, D_MODEL), first_weight, late_weights, on_grads=on_grads)

    grads = {}
    for i in (1, 0):
        sums, filled = _split_wait(f"rs_exchange_wait_{i}", _exchange_copies, *exchanges[i], dx)
        grads.update(zip(COMM_GROUPS[i], _chip_sum(f"rs_chip_sum_{i}", sums, filled)))

    delta, new_m, new_v = {}, {}, {}
    for n in BIG:
        shp = w[n].shape
        g_, d_, m_, v_ = _adamw(f"adamw_{n}", grads[n], w[n][0], m[n][0], v[n][0])
        grads[n], delta[n], new_m[n], new_v[n] = g_.reshape(shp), d_.reshape(shp), m_.reshape(shp), v_.reshape(shp)

    g_s, d_s, m_s, v_s = _small_step(small, w, m, v)
    loss = g_s["loss"][0, 0]
    for n in SMALL_PARAMS:
        shp = w[n].shape
        grads[n], delta[n], new_m[n], new_v[n] = (a[n].reshape(shp) for a in (g_s, d_s, m_s, v_s))

    return (loss, dx.reshape(x.shape), *[grads[n] for n in WEIGHTS], *[delta[n] for n in WEIGHTS],
            *[new_m[n] for n in WEIGHTS], *[new_v[n] for n in WEIGHTS])
```

```python
import functools

import numpy as np
import jax
import jax.numpy as jnp
from jax import lax
from jax.experimental import pallas as pl
from jax.experimental.pallas import tpu as pltpu

F32, BF16 = jnp.float32, jnp.bfloat16
MESH = pl.DeviceIdType.MESH

D_MODEL = 1024
HEAD_DIM = 64
ATTN_W = 512
DILATIONS = (1, 4, 16)
BLK = 128
ROPE_DIM = 16
ROPE_THETA = 500000.0
SGU_W = 512
SGU_CHUNK = 128
SGU_GROUPS = 8
D_FF = 2816
N_CHIPS = 4
FF_SHARD = D_FF // N_CHIPS
IN_COLS = 7680
EPS = 1e-6
NEG = -1e30
LANES = 128
VMEM_LIMIT = 52 * 1024 * 1024

ADAM_LR, ADAM_B1, ADAM_B2, ADAM_EPS, ADAM_WD, ADAM_STEP = 0.001, 0.9, 0.999, 1e-08, 0.01, 10

QKV_BLOCKS = 9


def _w_in_block(part, g):
    return part * len(DILATIONS) + g


def _cparams(ngrid):
    return pltpu.CompilerParams(dimension_semantics=("arbitrary",) * ngrid, vmem_limit_bytes=VMEM_LIMIT)


def _full(shape):
    return pl.BlockSpec(shape, lambda *_: (0,) * len(shape))


def _resident(shape):
    return pl.BlockSpec(shape, lambda *_: (0,) * len(shape), pipeline_mode=pl.Buffered(1))


NN = ((1,), (0,))
NT = ((1,), (1,))
TN = ((0,), (0,))


def _mm(name, grid, pairs, dims, acc_shape, epi, *, extras=(), outs=(), reds=(), aliases=None):
    nk = grid[-1]
    npair, nex, nout, nred = len(pairs), len(extras), len(outs), len(reds)

    def body(*refs):
        a_refs = refs[:npair]
        b_refs = refs[npair:2 * npair]
        p0 = 2 * npair
        e_refs = refs[p0:p0 + nex]
        o_refs = refs[p0 + nex:p0 + nex + nout]
        r_refs = refs[p0 + nex + nout:p0 + nex + nout + nred]
        ids = [pl.program_id(a) for a in range(len(grid))]
        k = ids[-1]
        if nred:
            first = ids[0] == 0
            for v in ids[1:]:
                first = first & (v == 0)

            @pl.when(first)
            def _():
                for r in r_refs:
                    r[...] = jnp.zeros(r.shape, r.dtype)

        part = None
        for a_ref, b_ref in zip(a_refs, b_refs):
            d = lax.dot_general(a_ref[...], b_ref[...], (dims, ((), ())), preferred_element_type=F32)
            part = d if part is None else part + d
        if nk == 1:
            epi(part, e_refs, o_refs, r_refs, ids)
        else:
            acc_ref = refs[-1]

            @pl.when(k == 0)
            def _():
                acc_ref[...] = part

            @pl.when(k > 0)
            def _():
                acc_ref[...] += part

            @pl.when(k == nk - 1)
            def _():
                epi(acc_ref[...], e_refs, o_refs, r_refs, ids)

    in_specs = [p[1] for p in pairs] + [p[3] for p in pairs] + [e[1] for e in extras]
    args = [p[0] for p in pairs] + [p[2] for p in pairs] + [e[0] for e in extras]
    out_shape = [jax.ShapeDtypeStruct(o[0], o[1]) for o in outs] + [jax.ShapeDtypeStruct(r, F32) for r in reds]
    out_specs = [o[2] for o in outs] + [_full(r) for r in reds]
    scratch_shapes = [pltpu.VMEM(acc_shape, F32)] if nk > 1 else []
    return pl.pallas_call(
        body, grid=grid, in_specs=in_specs, out_specs=out_specs, out_shape=out_shape, scratch_shapes=scratch_shapes,
        input_output_aliases=aliases or {}, compiler_params=_cparams(len(grid)), name=name)(*args)


def _rope(v, cos_t, sin_t):
    half = ROPE_DIM // 2
    first = (lax.broadcasted_iota(jnp.int32, cos_t.shape, 1) % HEAD_DIM) < half
    outs = []
    for cs in range(v.shape[1] // LANES):
        x = v[:, cs * LANES:(cs + 1) * LANES]
        partner = jnp.where(first, pltpu.roll(x, LANES - half, axis=1), pltpu.roll(x, half, axis=1))
        outs.append(x * cos_t + partner * sin_t)
    return outs[0] if len(outs) == 1 else jnp.concatenate(outs, axis=1)


def _spread_heads(v2, upper):
    other = pltpu.roll(v2, HEAD_DIM, axis=1)
    h0 = jnp.where(upper, other, v2)
    h1 = jnp.where(upper, v2, other)
    return jnp.concatenate([jnp.concatenate([h0, h0], axis=1), jnp.concatenate([h1, h1], axis=1)], axis=0)


def _sigmoid(v):
    return 0.5 * jnp.tanh(0.5 * v) + 0.5


def _rms_stats(v):
    r = lax.rsqrt(jnp.mean(v * v, axis=-1, keepdims=True) + EPS)
    return v * r, r


def _rms_bwd(dy, xhat, r, g):
    dxh = dy * g
    return r * (dxh - xhat * jnp.mean(dxh * xhat, axis=-1, keepdims=True))


def _head_sum_matrix():
    idx = np.arange(ATTN_W) // HEAD_DIM
    return jnp.asarray((idx[:, None] == idx[None, :]).astype(np.float32), dtype=BF16)


def _group_sum(v, e):
    hi = v.astype(BF16)
    lo = (v - hi.astype(F32)).astype(BF16)
    return jnp.dot(hi, e, preferred_element_type=F32) + jnp.dot(lo, e, preferred_element_type=F32)


TILE = 512


def _to_slabs(slab_ref, v):
    for cs in range(slab_ref.shape[0]):
        slab_ref[cs] = v[:, cs * LANES:(cs + 1) * LANES]


def _from_slabs(slab_ref):
    return jnp.concatenate([slab_ref[cs] for cs in range(slab_ref.shape[0])], axis=1)


def _class_rows(slab_ref, r, dil):
    n = slab_ref.shape[1] // dil
    return jnp.concatenate([slab_ref.at[cs][pl.ds(r, n, stride=dil), :] for cs in range(slab_ref.shape[0])], axis=1)


def _put_class_rows(slab_ref, r, dil, v):
    n = slab_ref.shape[1] // dil
    for cs in range(slab_ref.shape[0]):
        slab_ref.at[cs][pl.ds(r, n, stride=dil), :] = v[:, cs * LANES:(cs + 1) * LANES]


def _natural_from_group(slab_ref, grp_ref):
    dil = grp_ref.shape[0]
    for r in range(dil):
        _put_class_rows(slab_ref, r, dil, grp_ref[r].astype(F32))
    return _from_slabs(slab_ref)


def _group_from_natural(slab_ref, grp_ref, v):
    dil = grp_ref.shape[0]
    _to_slabs(slab_ref, v)
    for r in range(dil):
        grp_ref[r] = _class_rows(slab_ref, r, dil).astype(grp_ref.dtype)


def _group_spec(dil, tile, width):
    return pl.BlockSpec((dil, tile // dil, width), lambda i, *_: (0, i, 0))


def _slabs(tile, width):
    return pltpu.VMEM((width // LANES, tile, LANES), F32)


def _rope_consts():
    lane = np.arange(LANES) % HEAD_DIM
    fi = lane % (ROPE_DIM // 2)
    invf = np.where(lane < ROPE_DIM, ROPE_THETA ** (-(2.0 * fi) / ROPE_DIM), 0.0)
    sgn = np.where(lane < ROPE_DIM // 2, -1.0, np.where(lane < ROPE_DIM, 1.0, 0.0))
    return (jnp.asarray(invf.astype(np.float32)).reshape(1, LANES), jnp.asarray(sgn.astype(np.float32)).reshape(1, LANES))


def _rope_tables(pos_col):
    t = pos_col.shape[0]
    tile = min(t, TILE)
    invf, sgn = _rope_consts()

    def body(p_ref, f_ref, s_ref, c0, s0, c1, s1, c2, s2, slab_c, slab_s):
        ang = p_ref[...].astype(F32) * f_ref[...]
        cos, sin = jnp.cos(ang), jnp.sin(ang) * s_ref[...]
        c0[...] = cos
        s0[...] = sin
        _group_from_natural(slab_c, c1, cos)
        _group_from_natural(slab_s, s1, sin)
        for r in range(DILATIONS[2]):
            c2[r] = _class_rows(slab_c, r, DILATIONS[2])
            s2[r] = _class_rows(slab_s, r, DILATIONS[2])

    nat = pl.BlockSpec((tile, LANES), lambda i: (i, 0))
    specs, shapes = [nat, nat], [(t, LANES)] * 2
    for d in DILATIONS[1:]:
        specs += [_group_spec(d, tile, LANES)] * 2
        shapes += [(d, t // d, LANES)] * 2
    outs = pl.pallas_call(
        body, grid=(t // tile,),
        in_specs=[pl.BlockSpec((tile, 1), lambda i: (i, 0)), _full((1, LANES)), _full((1, LANES))],
        out_specs=specs, out_shape=[jax.ShapeDtypeStruct(s, F32) for s in shapes],
        scratch_shapes=[_slabs(tile, LANES)] * 2,
        compiler_params=_cparams(1), name="rope_tables")(pos_col, invf, sgn)
    return [(outs[2 * g].reshape(t, LANES), outs[2 * g + 1].reshape(t, LANES)) for g in range(len(DILATIONS))]


def _norm_fwd(x, g):
    t = x.shape[0]
    tile = min(t, TILE)

    def body(x_ref, g_ref, h0_ref, h1_ref, h2_ref, slab):
        xhat, _ = _rms_stats(x_ref[...])
        hn = xhat * g_ref[...]
        h0_ref[...] = hn.astype(BF16)
        _group_from_natural(slab, h1_ref, hn)
        for r in range(DILATIONS[2]):
            h2_ref[r] = _class_rows(slab, r, DILATIONS[2]).astype(BF16)

    nat = pl.BlockSpec((tile, D_MODEL), lambda i: (i, 0))
    return pl.pallas_call(
        body, grid=(t // tile,),
        in_specs=[nat, _full((1, D_MODEL))],
        out_specs=[nat] + [_group_spec(d, tile, D_MODEL) for d in DILATIONS[1:]],
        out_shape=[jax.ShapeDtypeStruct((t, D_MODEL), BF16)]
        + [jax.ShapeDtypeStruct((d, t // d, D_MODEL), BF16) for d in DILATIONS[1:]],
        scratch_shapes=[_slabs(tile, D_MODEL)],
        compiler_params=_cparams(1), name="norm1_fwd")(x, g)


GU_COLS = 3072
GROUP_COLS = 1536
GU_HALF = GU_COLS // 2


def _w_in_spec(width, block):
    return pl.BlockSpec((D_MODEL, width), lambda i: (0, block), pipeline_mode=pl.Buffered(1))


def _gu_w_specs():
    first = QKV_BLOCKS * ATTN_W // GU_HALF
    return [_w_in_spec(GU_HALF, first), _w_in_spec(GU_HALF, first + 1)]


def _group_w_specs(g):
    return [_w_in_spec(ATTN_W, _w_in_block(part, g)) for part in range(3)]


def _in_proj(hs, w_in, tables):
    t = hs[0].shape[0]
    tm = min(t, 1024)

    def body_gu(h_ref, w0_ref, w1_ref, o_ref):
        h = h_ref[...]
        o_ref[:, 0:GU_HALF] = jnp.dot(h, w0_ref[...], preferred_element_type=F32).astype(BF16)
        o_ref[:, GU_HALF:] = jnp.dot(h, w1_ref[...], preferred_element_type=F32).astype(BF16)

    gu = _token_call("in_proj_gates_uv", body_gu, t, tm,
                     [(hs[0], _rows_spec(tm, D_MODEL))] + [(w_in, s) for s in _gu_w_specs()],
                     [((t, GU_COLS), BF16, _rows_spec(tm, GU_COLS))])[0]

    qkvs = []
    for g in range(len(DILATIONS)):

        def body_qkv(h_ref, wq_ref, wk_ref, wv_ref, cos_ref, sin_ref, o_ref):
            h = h_ref[...]
            cos_w, sin_w = cos_ref[...], sin_ref[...]
            q = jnp.dot(h, wq_ref[...], preferred_element_type=F32)
            o_ref[:, 0:ATTN_W] = (_rope(q, cos_w, sin_w) * HEAD_DIM ** -0.5).astype(BF16)
            k = jnp.dot(h, wk_ref[...], preferred_element_type=F32)
            o_ref[:, ATTN_W:2 * ATTN_W] = _rope(k, cos_w, sin_w).astype(BF16)
            o_ref[:, 2 * ATTN_W:] = jnp.dot(h, wv_ref[...], preferred_element_type=F32).astype(BF16)

        cos_t, sin_t = tables[g]
        qkvs.append(_token_call(
            f"in_proj_qkv_g{g}", body_qkv, t, tm,
            [(hs[g].reshape(t, D_MODEL), _rows_spec(tm, D_MODEL))] + [(w_in, s) for s in _group_w_specs(g)]
            + [(cos_t, _rows_spec(tm, LANES)), (sin_t, _rows_spec(tm, LANES))],
            [((t, GROUP_COLS), BF16, _rows_spec(tm, GROUP_COLS))])[0])
    return gu, qkvs


def _attn_masks(n):
    row = lax.broadcasted_iota(jnp.int32, (2 * BLK, 2 * BLK), 0) % BLK
    col = lax.broadcasted_iota(jnp.int32, (2 * BLK, 2 * BLK), 1)
    diff = BLK + row - col
    valid = (diff >= 0) & (diff <= BLK) & ((col >= BLK) | (n > 0))
    upper = lax.broadcasted_iota(jnp.int32, (BLK, LANES), 1) >= HEAD_DIM
    return valid, upper


def _stack_heads(v2, upper):
    zero = jnp.zeros_like(v2)
    return jnp.concatenate([jnp.where(upper, zero, v2), jnp.where(upper, v2, zero)], axis=0)


def _unstack_heads(v, upper):
    return jnp.where(upper, v[BLK:], v[:BLK])


def _attn_fwd(qkv, g, dil):
    t = qkv.shape[0]
    length = t // dil
    nb = length // BLK
    view = qkv.reshape(dil, length, GROUP_COLS)

    def body(q_ref, kc_ref, kp_ref, vc_ref, vp_ref, o_ref, l_ref):
        n = pl.program_id(1)
        valid, upper = _attn_masks(n)
        for p in range(ATTN_W // LANES):
            sl = slice(p * LANES, (p + 1) * LANES)
            qs = _stack_heads(q_ref[:, sl], upper)
            k2 = jnp.concatenate([kp_ref[:, sl], kc_ref[:, sl]], axis=0)
            v2 = jnp.concatenate([vp_ref[:, sl], vc_ref[:, sl]], axis=0)
            s = lax.dot_general(qs, k2, (NT, ((), ())), preferred_element_type=F32)
            s = jnp.where(valid, s, NEG)
            m = jnp.max(s, axis=1, keepdims=True)
            pe = jnp.exp(s - m)
            den = jnp.sum(pe, axis=1, keepdims=True)
            o = jnp.dot(pe.astype(BF16), v2, preferred_element_type=F32) / den
            lse = jnp.broadcast_to(m + jnp.log(den), (2 * BLK, LANES))
            o_ref[:, sl] = _unstack_heads(o, upper).astype(BF16)
            l_ref[:, sl] = _unstack_heads(lse, upper)

    cur = lambda part: pl.BlockSpec((None, BLK, ATTN_W), lambda r, n: (r, n, part))
    prev = lambda part: pl.BlockSpec((None, BLK, ATTN_W), lambda r, n: (r, jnp.maximum(n - 1, 0), part))
    out_spec = pl.BlockSpec((None, BLK, ATTN_W), lambda r, n: (r, n, 0))
    return pl.pallas_call(
        body, grid=(dil, nb),
        in_specs=[cur(0), cur(1), prev(1), cur(2), prev(2)],
        out_specs=[out_spec, out_spec],
        out_shape=[jax.ShapeDtypeStruct((dil, length, ATTN_W), BF16), jax.ShapeDtypeStruct((dil, length, ATTN_W), F32)],
        compiler_params=_cparams(2), name=f"attn_fwd_g{g}")(view, view, view, view, view)


def _alphas(l0, l1, l2):
    m = jnp.maximum(jnp.maximum(l0, l1), l2)
    e0, e1, e2 = jnp.exp(l0 - m), jnp.exp(l1 - m), jnp.exp(l2 - m)
    inv = 1.0 / (e0 + e1 + e2)
    return e0 * inv, e1 * inv, e2 * inv


def _natural_group_values(o_refs, l_refs, slabs):
    os_ = [o_refs[0][0].astype(F32)] + [_natural_from_group(slabs[2 * g - 2], o_refs[g]) for g in (1, 2)]
    ls_ = [l_refs[0][0]] + [_natural_from_group(slabs[2 * g - 1], l_refs[g]) for g in (1, 2)]
    return os_, ls_


def _combine_fwd(os_, ls_):
    t = os_[0].shape[1]
    tile = min(t, TILE)

    def body(o0, o1, o2, l0, l1, l2, a_ref, *slabs):
        ov, lv = _natural_group_values((o0, o1, o2), (l0, l1, l2), slabs)
        a0, a1, a2 = _alphas(*lv)
        a_ref[...] = (a0 * ov[0] + a1 * ov[1] + a2 * ov[2]).astype(BF16)

    specs = [_group_spec(d, tile, ATTN_W) for d in DILATIONS]
    return pl.pallas_call(
        body, grid=(t // tile,), in_specs=specs * 2, out_specs=pl.BlockSpec((tile, ATTN_W), lambda i: (i, 0)),
        out_shape=jax.ShapeDtypeStruct((t, ATTN_W), BF16),
        scratch_shapes=[_slabs(tile, ATTN_W)] * 4,
        compiler_params=_cparams(1), name="combine_fwd")(*os_, *ls_)


def _combine_bwd(dattn, os_, ls_):
    t = dattn.shape[0]
    tile = min(t, TILE)
    e = _head_sum_matrix()

    def body(d_ref, o0, o1, o2, l0, l1, l2, e_ref, do0, do1, do2, c0, c1, c2, *slabs):
        ov, lv = _natural_group_values((o0, o1, o2), (l0, l1, l2), slabs)
        alphas = _alphas(*lv)
        d = d_ref[...]
        attn = alphas[0] * ov[0] + alphas[1] * ov[1] + alphas[2] * ov[2]
        s = _group_sum(d * attn, e_ref[...])
        do0[0] = (alphas[0] * d).astype(BF16)
        c0[0] = -alphas[0] * s
        for g, do_ref, c_ref in ((1, do1, c1), (2, do2, c2)):
            _group_from_natural(slabs[2 * g - 2], do_ref, alphas[g] * d)
            _group_from_natural(slabs[2 * g - 1], c_ref, -alphas[g] * s)

    specs = [_group_spec(d, tile, ATTN_W) for d in DILATIONS]
    shapes = [(d, t // d, ATTN_W) for d in DILATIONS]
    outs = pl.pallas_call(
        body, grid=(t // tile,),
        in_specs=[pl.BlockSpec((tile, ATTN_W), lambda i: (i, 0))] + specs * 2 + [_full((ATTN_W, ATTN_W))],
        out_specs=specs * 2,
        out_shape=[jax.ShapeDtypeStruct(s, BF16) for s in shapes] + [jax.ShapeDtypeStruct(s, F32) for s in shapes],
        scratch_shapes=[_slabs(tile, ATTN_W)] * 4,
        compiler_params=_cparams(1), name="combine_bwd")(dattn, *os_, *ls_, e)
    return outs[:3], outs[3:]


def _attn_bwd(qkv, do, cc, lse, cos_t, sin_t, g, dil):
    t = qkv.shape[0]
    length = t // dil
    nb = length // BLK
    qkv_v = qkv.reshape(dil, length, GROUP_COLS)
    cos_v, sin_v = (a.reshape(dil, length, LANES) for a in (cos_t, sin_t))
    scale = HEAD_DIM ** -0.5

    def body(q_ref, kc_ref, kp_ref, vc_ref, vp_ref, do_ref, c_ref, l_ref, cosc, sinc, cosp, sinp,
             out_ref, dq_s, dk_s, dv_s):
        n = pl.program_id(1)
        valid, upper = _attn_masks(n)

        @pl.when(n < nb)
        def _():
            cos_c, sin_c = cosc[...], -sinc[...]
            cos_p, sin_p = cosp[...], -sinp[...]
            dq_parts, dkp_parts, dkc_parts, dvp_parts, dvc_parts = [], [], [], [], []
            for p in range(ATTN_W // LANES):
                sl = slice(p * LANES, (p + 1) * LANES)
                qs = _stack_heads(q_ref[:, sl], upper)
                dos = _stack_heads(do_ref[:, sl], upper)
                k2 = jnp.concatenate([kp_ref[:, sl], kc_ref[:, sl]], axis=0)
                v2 = jnp.concatenate([vp_ref[:, sl], vc_ref[:, sl]], axis=0)
                l_col = _spread_heads(l_ref[:, sl], upper)
                c_col = _spread_heads(c_ref[:, sl], upper)
                s = lax.dot_general(qs, k2, (NT, ((), ())), preferred_element_type=F32)
                pe = jnp.exp(jnp.where(valid, s, NEG) - l_col)
                dpv = lax.dot_general(dos, v2, (NT, ((), ())), preferred_element_type=F32)
                ds = (pe * (dpv + c_col)).astype(BF16)
                dq2 = _unstack_heads(jnp.dot(ds, k2, preferred_element_type=F32), upper)
                dk2 = lax.dot_general(ds, qs, (TN, ((), ())), preferred_element_type=F32)
                dv2 = lax.dot_general(pe.astype(BF16), dos, (TN, ((), ())), preferred_element_type=F32)
                dq_parts.append(dq2)
                dkp_parts.append(dk2[:BLK])
                dkc_parts.append(dk2[BLK:])
                dvp_parts.append(dv2[:BLK])
                dvc_parts.append(dv2[BLK:])
            dq = _rope(jnp.concatenate(dq_parts, axis=1) * scale, cos_c, sin_c)
            dkc = _rope(jnp.concatenate(dkc_parts, axis=1), cos_c, sin_c)
            dkp = _rope(jnp.concatenate(dkp_parts, axis=1), cos_p, sin_p)
            dvp = jnp.concatenate(dvp_parts, axis=1)
            dvc = jnp.concatenate(dvc_parts, axis=1)

            @pl.when(n > 0)
            def _():
                out_ref[:, 0:ATTN_W] = dq_s[...].astype(BF16)
                out_ref[:, ATTN_W:2 * ATTN_W] = (dk_s[...] + dkp).astype(BF16)
                out_ref[:, 2 * ATTN_W:3 * ATTN_W] = (dv_s[...] + dvp).astype(BF16)

            dq_s[...] = dq
            dk_s[...] = dkc
            dv_s[...] = dvc

        @pl.when(n == nb)
        def _():
            out_ref[:, 0:ATTN_W] = dq_s[...].astype(BF16)
            out_ref[:, ATTN_W:2 * ATTN_W] = dk_s[...].astype(BF16)
            out_ref[:, 2 * ATTN_W:3 * ATTN_W] = dv_s[...].astype(BF16)

    nc = lambda n: jnp.minimum(n, nb - 1)
    npv = lambda n: jnp.maximum(jnp.minimum(n, nb - 1) - 1, 0)
    cur = lambda part: pl.BlockSpec((None, BLK, ATTN_W), lambda r, n: (r, nc(n), part))
    prev = lambda part: pl.BlockSpec((None, BLK, ATTN_W), lambda r, n: (r, npv(n), part))
    row = pl.BlockSpec((None, BLK, ATTN_W), lambda r, n: (r, nc(n), 0))
    tab_c = pl.BlockSpec((None, BLK, LANES), lambda r, n: (r, nc(n), 0))
    tab_p = pl.BlockSpec((None, BLK, LANES), lambda r, n: (r, npv(n), 0))
    out_spec = pl.BlockSpec((None, BLK, GROUP_COLS), lambda r, n: (r, jnp.maximum(n - 1, 0), 0))
    out = pl.pallas_call(
        body, grid=(dil, nb + 1),
        in_specs=[cur(0), cur(1), prev(1), cur(2), prev(2), row, row, row, tab_c, tab_c, tab_p, tab_p],
        out_specs=out_spec,
        out_shape=jax.ShapeDtypeStruct((dil, length, GROUP_COLS), BF16),
        scratch_shapes=[pltpu.VMEM((BLK, ATTN_W), F32)] * 3,
        compiler_params=_cparams(2), name=f"attn_bwd_g{g}")(
            qkv_v, qkv_v, qkv_v, qkv_v, qkv_v, do, cc, lse, cos_v, sin_v, cos_v, sin_v)
    return out.reshape(t, GROUP_COLS)


SQRT_HALF = 0.7071067811865476
INV_SQRT_2PI = 0.3989422804014327


def _sgu_core(uv, g, b, w_ref, bias):
    cdf = 0.5 * (1.0 + lax.erf(uv * SQRT_HALF))
    z = uv * cdf
    u, v = z[:, :SGU_W], z[:, SGU_W:]
    mu = jnp.mean(v, axis=1, keepdims=True)
    xc = v - mu
    rs = lax.rsqrt(jnp.mean(xc * xc, axis=1, keepdims=True) + EPS)
    xhat = xc * rs
    vn = xhat * g + b
    row = lax.broadcasted_iota(jnp.int32, (SGU_CHUNK, SGU_CHUNK), 0)
    col = lax.broadcasted_iota(jnp.int32, (SGU_CHUNK, SGU_CHUNK), 1)
    tril = row >= col
    upper = lax.broadcasted_iota(jnp.int32, (SGU_CHUNK, LANES), 1) >= SGU_W // SGU_GROUPS
    ws, vlo, vhi, mixed = [], [], [], []
    for pr in range(SGU_W // LANES):
        sl = slice(pr * LANES, (pr + 1) * LANES)
        w0 = jnp.where(tril, w_ref[2 * pr], 0.0).astype(BF16)
        w1 = jnp.where(tril, w_ref[2 * pr + 1], 0.0).astype(BF16)
        vn2 = vn[:, sl]
        lo = jnp.where(upper, 0.0, vn2).astype(BF16)
        hi = jnp.where(upper, vn2, 0.0).astype(BF16)
        mixed.append(jnp.dot(w0, lo, preferred_element_type=F32) + jnp.dot(w1, hi, preferred_element_type=F32)
                     + bias[:, sl])
        ws.append((w0, w1))
        vlo.append(lo)
        vhi.append(hi)
    return cdf, u, xhat, rs, jnp.concatenate(mixed, axis=1), ws, vlo, vhi, tril, upper


def _sgu_fwd(gu, ln_g, ln_b, w_s, bias_exp):
    t = gu.shape[0]

    def body(uv_ref, g_ref, b_ref, w_ref, bias_ref, o_ref):
        _, u, _, _, mixed, *_ = _sgu_core(uv_ref[...].astype(F32), g_ref[...], b_ref[...], w_ref, bias_ref[...])
        o_ref[...] = (u * mixed).astype(BF16)

    return pl.pallas_call(
        body, grid=(t // SGU_CHUNK,),
        in_specs=[pl.BlockSpec((SGU_CHUNK, 2 * SGU_W), lambda n: (n, 0)), _full((1, SGU_W)), _full((1, SGU_W)),
                  _full((SGU_GROUPS, SGU_CHUNK, SGU_CHUNK)), _full((SGU_CHUNK, SGU_W))],
        out_specs=pl.BlockSpec((SGU_CHUNK, SGU_W), lambda n: (n, 0)),
        out_shape=jax.ShapeDtypeStruct((t, SGU_W), BF16),
        compiler_params=_cparams(1), name="sgu_fwd")(gu, ln_g, ln_b, w_s, bias_exp)


def _sgu_bwd(dproj, gu, dsgu, ln_g, ln_b, w_s, bias_exp):
    t = gu.shape[0]
    nchunks = t // SGU_CHUNK
    e = _head_sum_matrix()

    def body(dp_in, uv_ref, ds_ref, g_ref, b_ref, w_ref, bias_ref, e_ref, out_ref, dw_ref, dbias_ref, dg_ref, db_ref):
        n = pl.program_id(0)

        @pl.when(n == 0)
        def _():
            dw_ref[...] = jnp.zeros(dw_ref.shape, F32)
            dbias_ref[...] = jnp.zeros(dbias_ref.shape, F32)
            dg_ref[...] = jnp.zeros(dg_ref.shape, F32)
            db_ref[...] = jnp.zeros(db_ref.shape, F32)

        uv = uv_ref[...].astype(F32)
        g = g_ref[...]
        cdf, u, xhat, rs, mixed, ws, vlo, vhi, tril, upper = _sgu_core(uv, g, b_ref[...], w_ref, bias_ref[...])
        dsg = ds_ref[...]
        du = dsg * mixed
        dmixed = dsg * u
        dbias_ref[...] += dmixed
        dvn = []
        for pr in range(SGU_W // LANES):
            sl = slice(pr * LANES, (pr + 1) * LANES)
            dm2 = dmixed[:, sl]
            dlo = jnp.where(upper, 0.0, dm2).astype(BF16)
            dhi = jnp.where(upper, dm2, 0.0).astype(BF16)
            w0, w1 = ws[pr]
            dvn.append(lax.dot_general(w0, dlo, (TN, ((), ())), preferred_element_type=F32)
                       + lax.dot_general(w1, dhi, (TN, ((), ())), preferred_element_type=F32))
            dw0 = lax.dot_general(dlo, vlo[pr], (NT, ((), ())), preferred_element_type=F32)
            dw1 = lax.dot_general(dhi, vhi[pr], (NT, ((), ())), preferred_element_type=F32)
            dw_ref[2 * pr] += jnp.where(tril, dw0, 0.0)
            dw_ref[2 * pr + 1] += jnp.where(tril, dw1, 0.0)
        dvn = jnp.concatenate(dvn, axis=1)
        dg_ref[...] += jnp.sum(dvn * xhat, axis=0, keepdims=True)
        db_ref[...] += jnp.sum(dvn, axis=0, keepdims=True)
        dxh = dvn * g
        dv = rs * (dxh - jnp.mean(dxh, axis=1, keepdims=True) - xhat * jnp.mean(dxh * xhat, axis=1, keepdims=True))
        dz = jnp.concatenate([du, dv], axis=1)
        dgelu = cdf + uv * (INV_SQRT_2PI * jnp.exp(-0.5 * uv * uv))
        out_ref[...] = (dz * dgelu).astype(BF16)

        @pl.when(n == nchunks - 1)
        def _():
            dbias_ref[...] = _group_sum(dbias_ref[...], e_ref[...])

    outs = pl.pallas_call(
        body, grid=(nchunks,),
        in_specs=[pl.BlockSpec(memory_space=pl.ANY), pl.BlockSpec((SGU_CHUNK, 2 * SGU_W), lambda n: (n, 0)),
                  pl.BlockSpec((SGU_CHUNK, SGU_W), lambda n: (n, 0)), _full((1, SGU_W)), _full((1, SGU_W)),
                  _full((SGU_GROUPS, SGU_CHUNK, SGU_CHUNK)), _full((SGU_CHUNK, SGU_W)), _full((ATTN_W, ATTN_W))],
        out_specs=[pl.BlockSpec((SGU_CHUNK, 2 * SGU_W), lambda n: (n, 0)), _full((SGU_GROUPS, SGU_CHUNK, SGU_CHUNK)),
                   _full((SGU_CHUNK, SGU_W)), _full((1, SGU_W)), _full((1, SGU_W))],
        out_shape=[jax.ShapeDtypeStruct(dproj.shape, BF16), jax.ShapeDtypeStruct((SGU_GROUPS, SGU_CHUNK, SGU_CHUNK), F32),
                   jax.ShapeDtypeStruct((SGU_CHUNK, SGU_W), F32), jax.ShapeDtypeStruct((1, SGU_W), F32),
                   jax.ShapeDtypeStruct((1, SGU_W), F32)],
        input_output_aliases={0: 0},
        compiler_params=_cparams(1), name="sgu_bwd")(dproj, gu, dsgu, ln_g, ln_b, w_s, bias_exp, e)
    return outs


def _merge_fwd(attn, sgu, gu, x, w_pa, w_ps, w_out, g2):
    t = x.shape[0]
    tm = min(t, 512)

    def body(a_ref, s_ref, ga_ref, gb_ref, x_ref, wpa, wps, wo, g_ref, pa_ref, ps_ref, m_ref, x1_ref, h2_ref):
        pa = jnp.dot(a_ref[...], wpa[...], preferred_element_type=F32)
        ps = jnp.dot(s_ref[...], wps[...], preferred_element_type=F32)
        merged = (_sigmoid(ga_ref[...].astype(F32)) * pa + _sigmoid(gb_ref[...].astype(F32)) * ps).astype(BF16)
        x1 = x_ref[...] + jnp.dot(merged, wo[...], preferred_element_type=F32)
        xhat, _ = _rms_stats(x1)
        pa_ref[...] = pa.astype(BF16)
        ps_ref[...] = ps.astype(BF16)
        m_ref[...] = merged
        x1_ref[...] = x1
        h2_ref[...] = (xhat * g_ref[...]).astype(BF16)

    half = pl.BlockSpec((tm, ATTN_W), lambda i: (i, 0))
    full = pl.BlockSpec((tm, D_MODEL), lambda i: (i, 0))
    return pl.pallas_call(
        body, grid=(t // tm,),
        in_specs=[half, half, pl.BlockSpec((tm, D_MODEL), lambda i: (i, 1)), pl.BlockSpec((tm, D_MODEL), lambda i: (i, 2)),
                  full, _resident((ATTN_W, D_MODEL)), _resident((SGU_W, D_MODEL)), _resident((D_MODEL, D_MODEL)),
                  _full((1, D_MODEL))],
        out_specs=[full] * 5,
        out_shape=[jax.ShapeDtypeStruct((t, D_MODEL), BF16), jax.ShapeDtypeStruct((t, D_MODEL), BF16),
                   jax.ShapeDtypeStruct((t, D_MODEL), BF16), jax.ShapeDtypeStruct((t, D_MODEL), F32),
                   jax.ShapeDtypeStruct((t, D_MODEL), BF16)],
        compiler_params=_cparams(1), name="merge_fwd")(attn, sgu, gu, gu, x, w_pa, w_ps, w_out, g2)


def _merge_bwd(dx1b, gu, pa, ps, w_pa, w_ps, w_out):
    t = dx1b.shape[0]
    tm = min(t, 512)

    def body(d_ref, ga_ref, gb_ref, pa_ref, ps_ref, wpa, wps, wo, out_ref, dpa_ref, dps_ref, da_ref, dsg_ref):
        dm = lax.dot_general(d_ref[...], wo[...], (NT, ((), ())), preferred_element_type=F32)
        sa, sb = _sigmoid(ga_ref[...].astype(F32)), _sigmoid(gb_ref[...].astype(F32))
        dpa = (dm * sa).astype(BF16)
        dps = (dm * sb).astype(BF16)
        out_ref[:, 0:D_MODEL] = jnp.zeros((tm, D_MODEL), BF16)
        out_ref[:, D_MODEL:2 * D_MODEL] = (dm * pa_ref[...].astype(F32) * sa * (1.0 - sa)).astype(BF16)
        out_ref[:, 2 * D_MODEL:GU_COLS] = (dm * ps_ref[...].astype(F32) * sb * (1.0 - sb)).astype(BF16)
        dpa_ref[...] = dpa
        dps_ref[...] = dps
        da_ref[...] = lax.dot_general(dpa, wpa[...], (NT, ((), ())), preferred_element_type=F32)
        dsg_ref[...] = lax.dot_general(dps, wps[...], (NT, ((), ())), preferred_element_type=F32)

    half = pl.BlockSpec((tm, ATTN_W), lambda i: (i, 0))
    full = pl.BlockSpec((tm, D_MODEL), lambda i: (i, 0))
    return pl.pallas_call(
        body, grid=(t // tm,),
        in_specs=[full, pl.BlockSpec((tm, D_MODEL), lambda i: (i, 1)),
                  pl.BlockSpec((tm, D_MODEL), lambda i: (i, 2)), full, full,
                  _resident((ATTN_W, D_MODEL)), _resident((SGU_W, D_MODEL)), _resident((D_MODEL, D_MODEL))],
        out_specs=[pl.BlockSpec((tm, GU_COLS), lambda i: (i, 0)), full, full, half, half],
        out_shape=[jax.ShapeDtypeStruct((t, GU_COLS), BF16), jax.ShapeDtypeStruct((t, D_MODEL), BF16),
                   jax.ShapeDtypeStruct((t, D_MODEL), BF16), jax.ShapeDtypeStruct((t, ATTN_W), F32),
                   jax.ShapeDtypeStruct((t, SGU_W), F32)],
        compiler_params=_cparams(1), name="merge_bwd")(dx1b, gu, gu, pa, ps, w_pa, w_ps, w_out)


def _token_call(name, body, t, tm, ins, outs, reds=(), scratch=()):
    return pl.pallas_call(
        body, grid=(t // tm,), in_specs=[s for _, s in ins],
        out_specs=[o[2] for o in outs] + [_full(r) for r in reds],
        out_shape=[jax.ShapeDtypeStruct(o[0], o[1]) for o in outs] + [jax.ShapeDtypeStruct(r, F32) for r in reds],
        scratch_shapes=list(scratch), compiler_params=_cparams(1), name=name)(*[a for a, _ in ins])


def _rows_spec(tm, width):
    return pl.BlockSpec((tm, width), lambda i: (i, 0))


def _chips_spec(tm):
    return pl.BlockSpec((N_CHIPS, tm, FF_SHARD), lambda i: (0, i, 0))


def _zero_at_start(*refs):
    @pl.when(pl.program_id(0) == 0)
    def _():
        for r in refs:
            r[...] = jnp.zeros(r.shape, r.dtype)


def _ffn_fwd(h2, w_g, w_u):
    t = h2.shape[0]
    tm = min(t, 512)

    def body(h_ref, wg_ref, wu_ref, a_ref, b_ref, ff_ref):
        h = h_ref[...]
        for s in range(N_CHIPS):
            a = jnp.dot(h, wg_ref[s], preferred_element_type=F32)
            b = jnp.dot(h, wu_ref[s], preferred_element_type=F32)
            a_ref[s] = a.astype(BF16)
            b_ref[s] = b.astype(BF16)
            ff_ref[s] = (a * _sigmoid(a) * b).astype(BF16)

    shp = (N_CHIPS, t, FF_SHARD)
    w_spec = _resident((N_CHIPS, D_MODEL, FF_SHARD))
    return _token_call("ffn_fwd", body, t, tm, [(h2, _rows_spec(tm, D_MODEL)), (w_g, w_spec), (w_u, w_spec)],
                       [(shp, BF16, _chips_spec(tm))] * 3)


def _ffn_down_loss(ff, w_d, x1, tgt, gf):
    t = x1.shape[0]
    tm = min(t, 512)

    def body(ff_ref, wd_ref, x1_ref, tgt_ref, g_ref, dx2_ref, dx2b_ref, loss_ref, dgf_ref):
        _zero_at_start(loss_ref, dgf_ref)
        acc = jnp.dot(ff_ref[0], wd_ref[0], preferred_element_type=F32)
        for s in range(1, N_CHIPS):
            acc = acc + jnp.dot(ff_ref[s], wd_ref[s], preferred_element_type=F32)
        x2 = x1_ref[...] + acc
        g = g_ref[...]
        xhat, rr = _rms_stats(x2)
        diff = xhat * g - tgt_ref[...]
        rows = jnp.sum(diff * diff, axis=1, keepdims=True)
        loss_ref[...] += jnp.broadcast_to(jnp.sum(rows, axis=0, keepdims=True) * (0.5 / D_MODEL), (1, LANES))
        dy = diff * (1.0 / D_MODEL)
        dgf_ref[...] += jnp.sum(dy * xhat, axis=0, keepdims=True)
        dx2 = _rms_bwd(dy, xhat, rr, g)
        dx2_ref[...] = dx2
        dx2b_ref[...] = dx2.astype(BF16)

    row = _rows_spec(tm, D_MODEL)
    return _token_call("ffn_down_loss", body, t, tm,
                       [(ff, _chips_spec(tm)), (w_d, _resident((N_CHIPS, FF_SHARD, D_MODEL))), (x1, row), (tgt, row),
                        (gf, _full((1, D_MODEL)))],
                       [((t, D_MODEL), F32, row), ((t, D_MODEL), BF16, row)], reds=[(1, LANES), (1, D_MODEL)])


def _ffn_bwd_act(dx2b, w_d, a, b):
    t = dx2b.shape[0]
    tm = min(t, 512)

    def body(d_ref, wd_ref, a_ref, b_ref, da_ref, db_ref):
        d = d_ref[...]
        for s in range(N_CHIPS):
            dff = lax.dot_general(d, wd_ref[s], (NT, ((), ())), preferred_element_type=F32)
            av, bv = a_ref[s].astype(F32), b_ref[s].astype(F32)
            sg = _sigmoid(av)
            da_ref[s] = (dff * bv * (sg * (1.0 + av * (1.0 - sg)))).astype(BF16)
            db_ref[s] = (dff * (av * sg)).astype(BF16)

    shp = (N_CHIPS, t, FF_SHARD)
    return _token_call("ffn_bwd_act", body, t, tm,
                       [(dx2b, _rows_spec(tm, D_MODEL)), (w_d, _resident((N_CHIPS, FF_SHARD, D_MODEL))),
                        (a, _chips_spec(tm)), (b, _chips_spec(tm))],
                       [(shp, BF16, _chips_spec(tm))] * 2)


def _ffn_bwd_in(da, db, w_g, w_u, x1, dx2, g2):
    t = x1.shape[0]
    tm = min(t, 512)

    def body(da_ref, db_ref, wg_ref, wu_ref, x1_ref, dx2_ref, g_ref, dx1_ref, dx1b_ref, dg_ref):
        _zero_at_start(dg_ref)
        acc = None
        for s in range(N_CHIPS):
            part = (lax.dot_general(da_ref[s], wg_ref[s], (NT, ((), ())), preferred_element_type=F32)
                    + lax.dot_general(db_ref[s], wu_ref[s], (NT, ((), ())), preferred_element_type=F32))
            acc = part if acc is None else acc + part
        xhat, rr = _rms_stats(x1_ref[...])
        dg_ref[...] += jnp.sum(acc * xhat, axis=0, keepdims=True)
        dx1 = dx2_ref[...] + _rms_bwd(acc, xhat, rr, g_ref[...])
        dx1_ref[...] = dx1
        dx1b_ref[...] = dx1.astype(BF16)

    row = _rows_spec(tm, D_MODEL)
    w_spec = _resident((N_CHIPS, D_MODEL, FF_SHARD))
    return _token_call("ffn_bwd_in", body, t, tm,
                       [(da, _chips_spec(tm)), (db, _chips_spec(tm)), (w_g, w_spec), (w_u, w_spec), (x1, row), (dx2, row),
                        (g2, _full((1, D_MODEL)))],
                       [((t, D_MODEL), F32, row), ((t, D_MODEL), BF16, row)], reds=[(1, D_MODEL)])


def _group_dh(d, w_refs):
    dh = None
    for part, w_ref in enumerate(w_refs):
        term = lax.dot_general(d[:, part * ATTN_W:(part + 1) * ATTN_W], w_ref[...], (NT, ((), ())),
                               preferred_element_type=F32)
        dh = term if dh is None else dh + term
    return dh


def _in_proj_bwd(dgu, dqkvs, w_in, x, dx1, g1):
    t = x.shape[0]
    tile = min(t, TILE)
    ngroups = len(DILATIONS)

    def body(*refs):
        dgu_ref, dq_refs = refs[0], refs[1:1 + ngroups]
        w0_ref, w1_ref = refs[1 + ngroups:3 + ngroups]
        wg_refs = [refs[3 + ngroups + 3 * g:6 + ngroups + 3 * g] for g in range(ngroups)]
        x_ref, dx1_ref, g_ref, dx_ref, dg_ref, slab = refs[3 + 4 * ngroups:]
        _zero_at_start(dg_ref)
        dh = lax.dot_general(dgu_ref[:, 0:GU_HALF], w0_ref[...], (NT, ((), ())), preferred_element_type=F32)
        dh = dh + lax.dot_general(dgu_ref[:, GU_HALF:], w1_ref[...], (NT, ((), ())), preferred_element_type=F32)
        dh = dh + _group_dh(dq_refs[0][0], wg_refs[0])
        for g in range(1, ngroups):
            dil = DILATIONS[g]
            part = _group_dh(dq_refs[g][...].reshape(tile, GROUP_COLS), wg_refs[g])
            for r in range(dil):
                _put_class_rows(slab, r, dil, part[r * (tile // dil):(r + 1) * (tile // dil)])
            dh = dh + _from_slabs(slab)
        xhat, rr = _rms_stats(x_ref[...])
        dg_ref[...] += jnp.sum(dh * xhat, axis=0, keepdims=True)
        dx_ref[...] = dx1_ref[...] + _rms_bwd(dh, xhat, rr, g_ref[...])

    row = _rows_spec(tile, D_MODEL)
    group_ins = [(dqkvs[g].reshape(d, t // d, GROUP_COLS), _group_spec(d, tile, GROUP_COLS)) for g, d in enumerate(DILATIONS)]
    w_specs = _gu_w_specs() + [s for g in range(ngroups) for s in _group_w_specs(g)]
    return _token_call(
        "in_proj_bwd", body, t, tile,
        [(dgu, _rows_spec(tile, GU_COLS))] + group_ins + [(w_in, s) for s in w_specs]
        + [(x, row), (dx1, row), (g1, _full((1, D_MODEL)))],
        [((t, D_MODEL), F32, row)], reds=[(1, D_MODEL)], scratch=[_slabs(tile, D_MODEL)])


def _epi_bf16(acc, e, o, r, ids):
    o[0][...] = acc.astype(BF16)


WGRAD_TK = 2048


def _wgrad_2d(name, a, b, tm, tn):
    t, k1 = a.shape
    n = b.shape[1]
    tk = min(t, WGRAD_TK)
    return _mm(name, (k1 // tm, n // tn, t // tk),
               [(a, pl.BlockSpec((tk, tm), lambda i, j, k: (k, i)), b, pl.BlockSpec((tk, tn), lambda i, j, k: (k, j)))],
               TN, (tm, tn), _epi_bf16, outs=[((k1, n), BF16, pl.BlockSpec((tm, tn), lambda i, j, k: (i, j)))])[0]


def _wgrad_in(hs, dgu, dqkvs):
    t = dgu.shape[0]
    tk = min(t, WGRAD_TK)
    gu_block = QKV_BLOCKS * ATTN_W // GU_HALF
    parts = [(hs[0], dgu, GU_HALF, lambda j: j + gu_block)]
    parts += [(hs[g].reshape(t, D_MODEL), dqkvs[g], ATTN_W, lambda j, g=g: _w_in_block(j, g)) for g in range(3)]
    dst = None
    for n, (a, b, tn, block_of) in enumerate(parts):
        dst = _mm(f"wgrad_in_{n}", (1, b.shape[1] // tn, t // tk),
                  [(a, pl.BlockSpec((tk, D_MODEL), lambda i, j, k: (k, 0)), b,
                    pl.BlockSpec((tk, tn), lambda i, j, k: (k, j)))],
                  TN, (D_MODEL, tn), _epi_bf16,
                  extras=[] if dst is None else [(dst, pl.BlockSpec(memory_space=pl.ANY))],
                  outs=[((D_MODEL, IN_COLS), BF16,
                         pl.BlockSpec((D_MODEL, tn), lambda i, j, k, block_of=block_of: (0, block_of(j))))],
                  aliases=None if dst is None else {2: 0})[0]
    return dst


def _wgrad_ff_in(name, h2, da):
    t = h2.shape[0]
    tk = min(t, WGRAD_TK)
    return _mm(name, (N_CHIPS, 1, t // tk),
               [(h2, pl.BlockSpec((tk, D_MODEL), lambda i, j, k: (k, 0)),
                 da, pl.BlockSpec((None, tk, FF_SHARD), lambda i, j, k: (i, k, 0)))],
               TN, (D_MODEL, FF_SHARD), _epi_bf16,
               outs=[((N_CHIPS, D_MODEL, FF_SHARD), BF16, pl.BlockSpec((None, D_MODEL, FF_SHARD), lambda i, j, k: (i, 0, 0)))])[0]


def _wgrad_ff_down(ff, dx2b):
    t = dx2b.shape[0]
    tk = min(t, WGRAD_TK)
    return _mm("wgrad_ffn_down", (N_CHIPS, 1, t // tk),
               [(ff, pl.BlockSpec((None, tk, FF_SHARD), lambda i, j, k: (i, k, 0)),
                 dx2b, pl.BlockSpec((tk, D_MODEL), lambda i, j, k: (k, 0)))],
               TN, (FF_SHARD, D_MODEL), _epi_bf16,
               outs=[((N_CHIPS, FF_SHARD, D_MODEL), BF16, pl.BlockSpec((None, FF_SHARD, D_MODEL), lambda i, j, k: (i, 0, 0)))])[0]


def _local_step(x, pos_col, tgt, g1, ln_g, ln_b, w_s, b_s, g2, gf, first_weight, late_weights, on_grads=None):
    tables = _rope_tables(pos_col)
    bias_exp = jnp.repeat(jnp.transpose(b_s), SGU_W // SGU_GROUPS, axis=1)

    hs = _norm_fwd(x, g1)
    w_p = first_weight(hs[0])
    gu, qkvs = _in_proj(hs, w_p, tables)
    os_, ls_ = [], []
    for g, dil in enumerate(DILATIONS):
        o, lse = _attn_fwd(qkvs[g], g, dil)
        os_.append(o)
        ls_.append(lse)
    attn = _combine_fwd(os_, ls_)
    sgu = _sgu_fwd(gu, ln_g, ln_b, w_s, bias_exp)
    w_pa, w_ps, w_out, w_g, w_u, w_d = late_weights(attn)
    pa, ps, merged, x1, h2 = _merge_fwd(attn, sgu, gu, x, w_pa, w_ps, w_out, g2)
    a, b, ff = _ffn_fwd(h2, w_g, w_u)
    dx2, dx2b, loss, dgf = _ffn_down_loss(ff, w_d, x1, tgt, gf)

    da, db = _ffn_bwd_act(dx2b, w_d, a, b)
    dw_d = _wgrad_ff_down(ff, dx2b)
    dx1, dx1b, dg2 = _ffn_bwd_in(da, db, w_g, w_u, x1, dx2, g2)
    dw_g = _wgrad_ff_in("wgrad_ffn_gate", h2, da)
    dw_u = _wgrad_ff_in("wgrad_ffn_up", h2, db)

    dgu, dpa, dps, dattn, dsgu = _merge_bwd(dx1b, gu, pa, ps, w_pa, w_ps, w_out)
    dw_out = _wgrad_2d("wgrad_out", merged, dx1b, D_MODEL, D_MODEL)
    dw_pa = _wgrad_2d("wgrad_proj_attn", attn, dpa, ATTN_W, D_MODEL)
    dw_ps = _wgrad_2d("wgrad_proj_sgu", sgu, dps, SGU_W, D_MODEL)
    if on_grads is not None:
        ln_g = ln_g + on_grads(1, dict(w_proj_attn=dw_pa, w_proj_sgu=dw_ps, w_out=dw_out, w_ffn_gate=dw_g, w_ffn_up=dw_u,
                                       w_ffn_down=dw_d))[:, :SGU_W]
    dgu, dw_s, dbias, dln_g, dln_b = _sgu_bwd(dgu, gu, dsgu, ln_g, ln_b, w_s, bias_exp)
    dos, ccs = _combine_bwd(dattn, os_, ls_)
    dqkvs = [_attn_bwd(qkvs[g], dos[g], ccs[g], ls_[g], *tables[g], g, dil) for g, dil in enumerate(DILATIONS)]
    dw_p = _wgrad_in(hs, dgu, dqkvs)
    if on_grads is not None:
        g1 = g1 + on_grads(0, dict(w_in=dw_p))
    dx, dg1 = _in_proj_bwd(dgu, dqkvs, w_p, x, dx1, g1)

    db_s = jnp.transpose(dbias[:, ::SGU_W // SGU_GROUPS])
    small = dict(loss=loss, norm1_g=dg1, sgu_ln_g=dln_g, sgu_ln_b=dln_b, w_spatial=dw_s, b_spatial=db_s,
                 norm2_g=dg2, final_g=dgf)
    big = dict(w_in=dw_p, w_proj_attn=dw_pa, w_proj_sgu=dw_ps, w_out=dw_out, w_ffn_gate=dw_g, w_ffn_up=dw_u,
               w_ffn_down=dw_d)
    return dx, big, small


def _ew(name, fn, ins, out_dtypes):
    shp = ins[0].shape
    rows, cols = shp
    tr = next((cand for cand in (256, 352, 128) if rows % cand == 0 and rows > cand), rows)

    def body(*refs):
        res = fn(*[r[...] for r in refs[:len(ins)]])
        for o_ref, v in zip(refs[len(ins):], res):
            o_ref[...] = v.astype(o_ref.dtype)

    spec = pl.BlockSpec((tr, cols), lambda i: (i, 0))
    return pl.pallas_call(
        body, grid=(rows // tr,), in_specs=[spec] * len(ins), out_specs=[spec] * len(out_dtypes),
        out_shape=[jax.ShapeDtypeStruct(shp, d) for d in out_dtypes],
        compiler_params=_cparams(1), name=name)(*ins)


def _adamw_math(g, w, m, v):
    m = ADAM_B1 * m + (1.0 - ADAM_B1) * g
    v = ADAM_B2 * v + (1.0 - ADAM_B2) * (g * g)
    m_hat = m / (1.0 - ADAM_B1 ** ADAM_STEP)
    v_hat = v / (1.0 - ADAM_B2 ** ADAM_STEP)
    delta = -ADAM_LR * (m_hat / (jnp.sqrt(v_hat) + ADAM_EPS) + ADAM_WD * w)
    return delta, m, v


def _adamw(name, g, w, m, v):
    return _ew(name, lambda g_, w_, m_, v_: (g_,) + _adamw_math(g_, w_, m_, v_), [g, w, m, v], [F32] * 4)


VMEM_SPEC = pl.BlockSpec(memory_space=pltpu.VMEM)


def _for_row_chunks(rows, fn):
    ck = next(c for c in (64, 32, 16) if rows % c == 0)

    def step(i, carry):
        fn(pl.multiple_of(i * ck, ck), ck)
        return carry

    lax.fori_loop(0, rows // ck, step, 0)


def _place():
    x, y, c = lax.axis_index("x"), lax.axis_index("y"), lax.axis_index("c")
    chips = [(1 - x, y), (x, 1 - y), (1 - x, 1 - y)]
    return x, y, c, 2 * x + y, chips


def _rows(ref, start, size):
    if len(ref.shape) == 2:
        return ref.at[pl.ds(start, size), :]
    return ref.at[:, pl.ds(start, size), :]


def _comm_call(name, body, ins, out_shapes, scratch, n_remote):
    return pl.pallas_call(
        body, in_specs=[VMEM_SPEC] * len(ins), out_specs=[VMEM_SPEC] * len(out_shapes),
        out_shape=out_shapes,
        scratch_shapes=list(scratch) + [pltpu.SemaphoreType.DMA((n_remote,)), pltpu.SemaphoreType.DMA((n_remote,))],
        compiler_params=pltpu.CompilerParams(vmem_limit_bytes=VMEM_LIMIT), name=name)(*ins)


def _gather_finish(name, shard, landed):
    k_rows, n = shard.shape
    kh = k_rows // 2

    def body(shard_ref, land_ref, out_ref, send, recv):
        x, y, c, me, chips = _place()
        passed = []
        for j, chip in enumerate(chips):
            theirs = 2 * chip[0] + chip[1]
            cp = pltpu.make_async_remote_copy(
                src_ref=land_ref.at[j], dst_ref=_rows(out_ref.at[theirs], c * kh, kh), send_sem=send.at[j],
                recv_sem=recv.at[j], device_id=(x, y, 1 - c), device_id_type=MESH)
            cp.start()
            passed.append(cp)
        mine = out_ref.at[me]

        def put_own(r0, ck):
            mine[pl.ds(r0, ck), :] = shard_ref[pl.ds(r0, ck), :]

        _for_row_chunks(k_rows, put_own)
        for j, chip in enumerate(chips):
            slot = out_ref.at[2 * chip[0] + chip[1]]

            def put_half(r0, ck, j=j, slot=slot):
                slot[pl.ds(pl.multiple_of(c * kh + r0, ck), ck), :] = land_ref[j, pl.ds(r0, ck), :]

            _for_row_chunks(kh, put_half)
        for j, chip in enumerate(chips):
            other = _rows(out_ref.at[2 * chip[0] + chip[1]], (1 - c) * kh, kh)
            pltpu.make_async_remote_copy(src_ref=other, dst_ref=other, send_sem=send.at[j], recv_sem=recv.at[j],
                                         device_id=(x, y, 1 - c), device_id_type=MESH).wait_recv()
        for cp in passed:
            cp.wait_send()

    return _comm_call(name, body, [shard, landed], [jax.ShapeDtypeStruct((N_CHIPS, k_rows, n), shard.dtype)], [], 3)[0]


def _pair_reduce(name, grads):
    nt = len(grads)

    def half(s):
        shp = list(s.shape)
        shp[-2] //= 2
        return tuple(shp)

    def body(*refs):
        ins, outs, got = refs[:nt], refs[nt:2 * nt], refs[2 * nt:3 * nt]
        send, recv = refs[3 * nt:]
        x, y, c, me, chips = _place()
        copies = []
        for t in range(nt):
            kh = ins[t].shape[-2] // 2
            rc = pltpu.make_async_remote_copy(
                src_ref=_rows(ins[t], (1 - c) * kh, kh), dst_ref=got[t], send_sem=send.at[t], recv_sem=recv.at[t],
                device_id=(x, y, 1 - c), device_id_type=MESH)
            rc.start()
            copies.append(rc)
        for t in range(nt):
            kh = ins[t].shape[-2] // 2
            copies[t].wait_recv()
            for lead in ([()] if len(ins[t].shape) == 2 else [(s,) for s in range(ins[t].shape[0])]):

                def add(r0, ck, lead=lead, src=ins[t], oth=got[t], dst=outs[t], kh=kh):
                    own = src[lead + (pl.ds(pl.multiple_of(c * kh + r0, ck), ck), slice(None))]
                    rows = lead + (pl.ds(r0, ck), slice(None))
                    dst[rows] = (own.astype(F32) + oth[rows].astype(F32)).astype(BF16)

                _for_row_chunks(kh, add)
        for rc in copies:
            rc.wait_send()

    shapes = [half(g) for g in grads]
    return _comm_call(name, body, grads, [jax.ShapeDtypeStruct(s, BF16) for s in shapes],
                      [pltpu.VMEM(s, BF16) for s in shapes], nt)


HBM_SPEC = pl.BlockSpec(memory_space=pltpu.HBM)
SEM_SPEC = pl.BlockSpec(memory_space=pltpu.SEMAPHORE)
DATAFLOW = pltpu.SideEffectType.DATAFLOW_SIDE_EFFECTING
TOKEN_SHAPE = (1, D_MODEL)


def _shard_cols(shape):
    return shape[2] if len(shape) == 3 else shape[1] // N_CHIPS


def _chip_piece(ref, j):
    if len(ref.shape) == 3:
        return ref.at[j]
    n4 = ref.shape[1] // N_CHIPS
    return ref.at[:, pl.ds(j * n4, n4)]


def _exchange_copies(sums, lands, send, recv):
    x, y, c, me, chips = _place()
    return [pltpu.make_async_remote_copy(
        src_ref=_chip_piece(sums[t], 2 * chip[0] + chip[1]), dst_ref=lands[t].at[j], send_sem=send.at[t * 3 + j],
        recv_sem=recv.at[t * 3 + j], device_id=(*chip, c), device_id_type=MESH)
        for t in range(len(sums)) for j, chip in enumerate(chips)]


def _gather_copies(shards, lands, send, recv):
    x, y, c, me, chips = _place()
    return [pltpu.make_async_remote_copy(
        src_ref=shards[t], dst_ref=lands[t].at[me], send_sem=send.at[t * 3 + j], recv_sem=recv.at[t * 3 + j],
        device_id=(*chip, c), device_id_type=MESH)
        for t in range(len(shards)) for j, chip in enumerate(chips)]


def _gather_half_copies(shards, lands, send, recv):
    x, y, c, me, chips = _place()
    return [pltpu.make_async_remote_copy(
        src_ref=_rows(shards[t], c * (shards[t].shape[0] // 2), shards[t].shape[0] // 2), dst_ref=lands[t].at[j],
        send_sem=send.at[t * 3 + j], recv_sem=recv.at[t * 3 + j], device_id=(*chip, c), device_id_type=MESH)
        for t in range(len(shards)) for j, chip in enumerate(chips)]


def _split_start(name, copies, srcs, land_shapes):
    nt = len(srcs)
    lands = [lax.empty(s, BF16) for s in land_shapes]

    def body(*refs):
        send, recv = refs[2 * nt], refs[2 * nt + 1]
        for cp in copies(refs[:nt], refs[nt:2 * nt], send, recv):
            cp.start()
        refs[-1][...] = jnp.zeros(TOKEN_SHAPE, F32)

    hbm = lambda a: pltpu.with_memory_space_constraint(a, pltpu.HBM)
    outs = pl.pallas_call(
        body, name=name,
        out_shape=[pltpu.SemaphoreType.DMA((3 * nt,)), pltpu.SemaphoreType.DMA((3 * nt,))]
        + [pltpu.HBM(s.shape, s.dtype) for s in srcs] + [pltpu.HBM(l.shape, l.dtype) for l in lands]
        + [jax.ShapeDtypeStruct(TOKEN_SHAPE, F32)],
        in_specs=[HBM_SPEC] * (2 * nt), out_specs=[SEM_SPEC, SEM_SPEC] + [HBM_SPEC] * (2 * nt) + [VMEM_SPEC],
        input_output_aliases={i: 2 + i for i in range(2 * nt)},
        compiler_params=pltpu.CompilerParams(has_side_effects=DATAFLOW))(*[hbm(a) for a in list(srcs) + lands])
    return outs[0], outs[1], outs[2:2 + nt], outs[2 + nt:2 + 2 * nt], outs[-1]


def _split_wait(name, copies, send, recv, srcs, lands, after):
    nt = len(srcs)

    def body(*refs):
        for cp in copies(refs[:nt], refs[nt:2 * nt], refs[2 * nt], refs[2 * nt + 1]):
            cp.wait_send()
            cp.wait_recv()

    outs = pl.pallas_call(
        body, name=name,
        out_shape=[pltpu.HBM(s.shape, s.dtype) for s in srcs] + [pltpu.HBM(l.shape, l.dtype) for l in lands],
        in_specs=[HBM_SPEC] * (2 * nt) + [SEM_SPEC, SEM_SPEC, pl.BlockSpec(memory_space=pl.ANY)],
        out_specs=[HBM_SPEC] * (2 * nt), input_output_aliases={i: i for i in range(2 * nt)},
        compiler_params=pltpu.CompilerParams(has_side_effects=DATAFLOW))(*srcs, *lands, send, recv, after)
    return outs[:nt], outs[nt:]


def _chip_sum(name, sums, lands):
    nt = len(sums)

    def body(*refs):
        ins, slots, outs = refs[:nt], refs[nt:2 * nt], refs[2 * nt:3 * nt]
        send, recv = refs[3 * nt:]
        x, y, c, me, chips = _place()
        sibling = (x, y, 1 - c)
        handed = []
        for t in range(nt):
            kh, n4 = ins[t].shape[-2], outs[t].shape[1]
            for jj in range(N_CHIPS):

                @pl.when(me == jj)
                def _(jj=jj, src=ins[t], slot=slots[t], dst=outs[t], kh=kh, n4=n4):
                    def add(r0, ck):
                        rows = pl.ds(r0, ck)
                        own = src[jj, rows, :] if len(src.shape) == 3 else src[rows, jj * n4:(jj + 1) * n4]
                        acc = ((own.astype(F32) + slot[0, rows, :].astype(F32)) + slot[1, rows, :].astype(F32)) \
                            + slot[2, rows, :].astype(F32)
                        dst[pl.ds(pl.multiple_of(c * kh + r0, ck), ck), :] = acc

                    _for_row_chunks(kh, add)

            rc = pltpu.make_async_remote_copy(
                src_ref=_rows(outs[t], c * kh, kh), dst_ref=_rows(outs[t], c * kh, kh), send_sem=send.at[t],
                recv_sem=recv.at[t], device_id=sibling, device_id_type=MESH)
            rc.start()
            handed.append(rc)
        for t in range(nt):
            kh = ins[t].shape[-2]
            other = _rows(outs[t], (1 - c) * kh, kh)
            pltpu.make_async_remote_copy(
                src_ref=other, dst_ref=other, send_sem=send.at[t], recv_sem=recv.at[t],
                device_id=sibling, device_id_type=MESH).wait_recv()
        for rc in handed:
            rc.wait_send()

    out_shapes = [jax.ShapeDtypeStruct((2 * s.shape[-2], _shard_cols(s.shape)), F32) for s in sums]
    return _comm_call(name, body, list(sums) + list(lands), out_shapes, [], nt)


VEC_SHAPE = (8, D_MODEL + LANES)
VEC_SLOTS = dict(norm1_g=(slice(0, 1), slice(0, D_MODEL)), norm2_g=(slice(1, 2), slice(0, D_MODEL)),
                 final_g=(slice(2, 3), slice(0, D_MODEL)), sgu_ln_g=(slice(3, 4), slice(0, SGU_W)),
                 sgu_ln_b=(slice(3, 4), slice(SGU_W, 2 * SGU_W)), b_spatial=(slice(0, 8), slice(D_MODEL, D_MODEL + LANES)),
                 loss=(slice(4, 5), slice(0, LANES)))
VEC_PARAMS = ("norm1_g", "norm2_g", "final_g", "sgu_ln_g", "sgu_ln_b", "b_spatial")
SMALL_PARAMS = VEC_PARAMS + ("w_spatial",)
W_SPATIAL_2D = (SGU_GROUPS * SGU_CHUNK, SGU_CHUNK)


def _small_step(partials, w, m, v):
    def shape2d(name):
        if name == "w_spatial":
            return W_SPATIAL_2D
        rows, cols = VEC_SLOTS[name]
        return (rows.stop - rows.start, cols.stop - cols.start)

    g_names = VEC_PARAMS + ("loss", "w_spatial")
    ng, npar = len(g_names), len(SMALL_PARAMS)

    def pack(dst, parts):
        dst[...] = jnp.zeros(VEC_SHAPE, F32)
        for n, ref in parts.items():
            if n in VEC_SLOTS:
                dst[VEC_SLOTS[n]] = ref[...]

    def reduce_body(*refs):
        g_in = dict(zip(g_names, refs[:ng]))
        vec_out, ws_out, vec, vec_pair, vec_slot, ws_pair, ws_slot, send, recv = refs[ng:]
        x, y, c, me, chips = _place()
        sibling = (x, y, 1 - c)
        pack(vec, g_in)
        copies = []

        def allreduce(k0, src, pair, slot):
            first = pltpu.make_async_remote_copy(src_ref=src, dst_ref=pair, send_sem=send.at[k0], recv_sem=recv.at[k0],
                                                 device_id=sibling, device_id_type=MESH)
            first.start()
            first.wait_recv()
            slot[me] = src[...] + pair[...]
            arrivals = []
            for j, chip in enumerate(chips):
                theirs = 2 * chip[0] + chip[1]
                rc = pltpu.make_async_remote_copy(src_ref=slot.at[me], dst_ref=slot.at[me], send_sem=send.at[k0 + 1 + j],
                                                  recv_sem=recv.at[k0 + 1 + j], device_id=(*chip, c), device_id_type=MESH)
                rc.start()
                arrivals.append(pltpu.make_async_remote_copy(
                    src_ref=slot.at[theirs], dst_ref=slot.at[theirs], send_sem=send.at[k0 + 1 + j],
                    recv_sem=recv.at[k0 + 1 + j], device_id=(*chip, c), device_id_type=MESH))
                copies.append(rc)
            copies.append(first)
            return arrivals

        arrivals = allreduce(0, vec, vec_pair, vec_slot) + allreduce(4, g_in["w_spatial"], ws_pair, ws_slot)
        for a in arrivals:
            a.wait_recv()
        vec_out[...] = ((vec_slot[0] + vec_slot[1]) + vec_slot[2]) + vec_slot[3]

        def spatial(r0, ck):
            rows = pl.ds(r0, ck)
            ws_out[rows, :] = ((ws_slot[0, rows, :] + ws_slot[1, rows, :]) + ws_slot[2, rows, :]) + ws_slot[3, rows, :]

        _for_row_chunks(W_SPATIAL_2D[0], spatial)
        for rc in copies:
            rc.wait_send()

    g_vec, g_ws = pl.pallas_call(
        reduce_body, in_specs=[VMEM_SPEC] * ng, out_specs=[VMEM_SPEC] * 2,
        out_shape=[jax.ShapeDtypeStruct(VEC_SHAPE, F32), jax.ShapeDtypeStruct(W_SPATIAL_2D, F32)],
        scratch_shapes=[pltpu.VMEM(VEC_SHAPE, F32), pltpu.VMEM(VEC_SHAPE, F32), pltpu.VMEM((N_CHIPS,) + VEC_SHAPE, F32),
                        pltpu.VMEM(W_SPATIAL_2D, F32), pltpu.VMEM((N_CHIPS,) + W_SPATIAL_2D, F32),
                        pltpu.SemaphoreType.DMA((8,)), pltpu.SemaphoreType.DMA((8,))],
        name="small_params_allreduce")(*[partials[n].reshape(shape2d(n)) for n in g_names])

    def update_body(*refs):
        gv_ref, gw_ref = refs[:2]
        w_in, m_in, v_in = (dict(zip(SMALL_PARAMS, refs[2 + k * npar:2 + (k + 1) * npar])) for k in range(3))
        o0 = 2 + 3 * npar
        g_out = dict(zip(g_names, refs[o0:o0 + ng]))
        d_out, m_out, v_out = (dict(zip(SMALL_PARAMS, refs[o0 + ng + k * npar:o0 + ng + (k + 1) * npar])) for k in range(3))
        vw, vm, vv = refs[o0 + ng + 3 * npar:]
        pack(vw, w_in)
        pack(vm, m_in)
        pack(vv, v_in)
        d_vec, m_vec, v_vec = _adamw_math(gv_ref[...], vw[...], vm[...], vv[...])
        vw[...] = d_vec
        vm[...] = m_vec
        vv[...] = v_vec
        for n in VEC_PARAMS + ("loss",):
            g_out[n][...] = gv_ref[VEC_SLOTS[n]]
        for n in VEC_PARAMS:
            d_out[n][...] = vw[VEC_SLOTS[n]]
            m_out[n][...] = vm[VEC_SLOTS[n]]
            v_out[n][...] = vv[VEC_SLOTS[n]]

        def spatial(r0, ck):
            rows = pl.ds(r0, ck)
            g = gw_ref[rows, :]
            d_, m_, v_ = _adamw_math(g, w_in["w_spatial"][rows, :], m_in["w_spatial"][rows, :], v_in["w_spatial"][rows, :])
            g_out["w_spatial"][rows, :] = g
            d_out["w_spatial"][rows, :] = d_
            m_out["w_spatial"][rows, :] = m_
            v_out["w_spatial"][rows, :] = v_

        _for_row_chunks(W_SPATIAL_2D[0], spatial)

    ins = [g_vec, g_ws]
    for src in (w, m, v):
        ins += [src[n].reshape(shape2d(n)) for n in SMALL_PARAMS]
    out_shapes = [jax.ShapeDtypeStruct(shape2d(n), F32) for n in g_names + SMALL_PARAMS * 3]
    outs = pl.pallas_call(
        update_body, in_specs=[VMEM_SPEC] * len(ins), out_specs=[VMEM_SPEC] * len(out_shapes), out_shape=out_shapes,
        scratch_shapes=[pltpu.VMEM(VEC_SHAPE, F32)] * 3, name="small_params_update")(*ins)
    grads = dict(zip(g_names, outs[:ng]))
    rest = [dict(zip(SMALL_PARAMS, outs[ng + k * npar:ng + (k + 1) * npar])) for k in range(3)]
    return grads, rest[0], rest[1], rest[2]


BIG = ("w_in", "w_proj_attn", "w_proj_sgu", "w_out", "w_ffn_gate", "w_ffn_up", "w_ffn_down")
COMM_GROUPS = (("w_in",), ("w_proj_attn", "w_proj_sgu", "w_out", "w_ffn_gate", "w_ffn_up", "w_ffn_down"))
WEIGHTS = ("norm1_g", "w_in", "sgu_ln_g", "sgu_ln_b", "w_spatial", "b_spatial", "w_proj_attn", "w_proj_sgu", "w_out",
           "norm2_g", "w_ffn_gate", "w_ffn_up", "w_ffn_down", "final_g")


def _cols_from_chips(g):
    return jnp.transpose(g, (1, 0, 2)).reshape(g.shape[1], N_CHIPS * g.shape[2])


def kernel(x, positions, norm1_g, w_in, sgu_ln_g, sgu_ln_b, w_spatial, b_spatial, w_proj_attn, w_proj_sgu, w_out, norm2_g, w_ffn_gate, w_ffn_up, w_ffn_down, final_g, loss_target, m_norm1_g, m_w_in, m_sgu_ln_g, m_sgu_ln_b, m_w_spatial, m_b_spatial, m_w_proj_attn, m_w_proj_sgu, m_w_out, m_norm2_g, m_w_ffn_gate, m_w_ffn_up, m_w_ffn_down, m_final_g, v_norm1_g, v_w_in, v_sgu_ln_g, v_sgu_ln_b, v_w_spatial, v_b_spatial, v_w_proj_attn, v_w_proj_sgu, v_w_out, v_norm2_g, v_w_ffn_gate, v_w_ffn_up, v_w_ffn_down, v_final_g):
    w = dict(norm1_g=norm1_g, w_in=w_in, sgu_ln_g=sgu_ln_g, sgu_ln_b=sgu_ln_b, w_spatial=w_spatial, b_spatial=b_spatial,
             w_proj_attn=w_proj_attn, w_proj_sgu=w_proj_sgu, w_out=w_out, norm2_g=norm2_g, w_ffn_gate=w_ffn_gate,
             w_ffn_up=w_ffn_up, w_ffn_down=w_ffn_down, final_g=final_g)
    m = dict(norm1_g=m_norm1_g, w_in=m_w_in, sgu_ln_g=m_sgu_ln_g, sgu_ln_b=m_sgu_ln_b, w_spatial=m_w_spatial,
             b_spatial=m_b_spatial, w_proj_attn=m_w_proj_attn, w_proj_sgu=m_w_proj_sgu, w_out=m_w_out, norm2_g=m_norm2_g,
             w_ffn_gate=m_w_ffn_gate, w_ffn_up=m_w_ffn_up, w_ffn_down=m_w_ffn_down, final_g=m_final_g)
    v = dict(norm1_g=v_norm1_g, w_in=v_w_in, sgu_ln_g=v_sgu_ln_g, sgu_ln_b=v_sgu_ln_b, w_spatial=v_w_spatial,
             b_spatial=v_b_spatial, w_proj_attn=v_w_proj_attn, w_proj_sgu=v_w_proj_sgu, w_out=v_w_out, norm2_g=v_norm2_g,
             w_ffn_gate=v_w_ffn_gate, w_ffn_up=v_w_ffn_up, w_ffn_down=v_w_ffn_down, final_g=v_final_g)
    t = x.shape[1]

    shards = {n: _ew(f"cast_{n}", lambda a: (a,), [w[n][0]], [BF16])[0] for n in BIG}
    late = COMM_GROUPS[1]
    k_in, n_in = shards["w_in"].shape
    *first, token = _split_start("gather_start_0", _gather_half_copies, [shards["w_in"]], [(3, k_in // 2, n_in)])
    pending = {}

    def first_weight(after):
        srcs, filled = _split_wait("gather_wait_0", _gather_half_copies, *first, after)
        gath_in, late_shards = lax.optimization_barrier(
            (_gather_finish("gather_finish_0", srcs[0], filled[0]), [shards[n] for n in late]))
        *pending["late"], _ = _split_start(
            "gather_start_1", _gather_copies, late_shards, [(N_CHIPS,) + s.shape for s in late_shards])
        return _cols_from_chips(gath_in)

    def late_weights(after):
        srcs, filled = _split_wait("gather_wait_1", _gather_copies, *pending["late"], after)
        me = 2 * lax.axis_index("x") + lax.axis_index("y")
        gath = {n: lax.dynamic_update_slice(f, s[None], (me, 0, 0)) for n, f, s in zip(late, filled, srcs)}
        return (_cols_from_chips(gath["w_proj_attn"]), _cols_from_chips(gath["w_proj_sgu"]),
                gath["w_out"].reshape(D_MODEL, D_MODEL), gath["w_ffn_gate"], gath["w_ffn_up"], gath["w_ffn_down"])

    exchanges = {}

    def on_grads(i, partials):
        if "w_out" in partials:
            partials["w_out"] = partials["w_out"].reshape(N_CHIPS, D_MODEL // N_CHIPS, D_MODEL)
        sums = _pair_reduce(f"rs_pair_reduce_{i}", [partials[n] for n in COMM_GROUPS[i]])
        *exchanges[i], started = _split_start(
            f"rs_exchange_start_{i}", _exchange_copies, sums, [(3, s.shape[-2], _shard_cols(s.shape)) for s in sums])
        return started

    dx, _, small = _local_step(
        x[0], positions.reshape(t, 1), loss_target[0], norm1_g + token, sgu_ln_g, sgu_ln_b, w_spatial[0], b_spatial[0],
        norm2_g, final_g.reshape(1, D_MODEL), first_weight, late_weights, on_grads=on_grads)

    grads = {}
    for i in (1, 0):
        sums, filled = _split_wait(f"rs_exchange_wait_{i}", _exchange_copies, *exchanges[i], dx)
        grads.update(zip(COMM_GROUPS[i], _chip_sum(f"rs_chip_sum_{i}", sums, filled)))

    delta, new_m, new_v = {}, {}, {}
    for n in BIG:
        shp = w[n].shape
        g_, d_, m_, v_ = _adamw(f"adamw_{n}", grads[n], w[n][0], m[n][0], v[n][0])
        grads[n], delta[n], new_m[n], new_v[n] = g_.reshape(shp), d_.reshape(shp), m_.reshape(shp), v_.reshape(shp)

    g_s, d_s, m_s, v_s = _small_step(small, w, m, v)
    loss = g_s["loss"][0, 0]
    for n in SMALL_PARAMS:
        shp = w[n].shape
        grads[n], delta[n], new_m[n], new_v[n] = (a[n].reshape(shp) for a in (g_s, d_s, m_s, v_s))

    return (loss, dx.reshape(x.shape), *[grads[n] for n in WEIGHTS], *[delta[n] for n in WEIGHTS],
            *[new_m[n] for n in WEIGHTS], *[new_v[n] for n in WEIGHTS])
```

```python
import functools

import numpy as np
import jax
import jax.numpy as jnp
from jax import lax
from jax.experimental import pallas as pl
from jax.experimental.pallas import tpu as pltpu

F32, BF16 = jnp.float32, jnp.bfloat16
MESH = pl.DeviceIdType.MESH

D_MODEL = 1024
HEAD_DIM = 64
ATTN_W = 512
DILATIONS = (1, 4, 16)
BLK = 128
ROPE_DIM = 16
ROPE_THETA = 500000.0
SGU_W = 512
SGU_CHUNK = 128
SGU_GROUPS = 8
D_FF = 2816
N_CHIPS = 4
FF_SHARD = D_FF // N_CHIPS
IN_COLS = 7680
EPS = 1e-6
NEG = -1e30
LANES = 128
VMEM_LIMIT = 52 * 1024 * 1024

ADAM_LR, ADAM_B1, ADAM_B2, ADAM_EPS, ADAM_WD, ADAM_STEP = 0.001, 0.9, 0.999, 1e-08, 0.01, 10

QKV_BLOCKS = 9


def _w_in_block(part, g):
    return part * len(DILATIONS) + g


def _cparams(ngrid):
    return pltpu.CompilerParams(dimension_semantics=("arbitrary",) * ngrid, vmem_limit_bytes=VMEM_LIMIT)


def _full(shape):
    return pl.BlockSpec(shape, lambda *_: (0,) * len(shape))


def _resident(shape):
    return pl.BlockSpec(shape, lambda *_: (0,) * len(shape), pipeline_mode=pl.Buffered(1))


NN = ((1,), (0,))
NT = ((1,), (1,))
TN = ((0,), (0,))


def _mm(name, grid, pairs, dims, acc_shape, epi, *, extras=(), outs=(), reds=(), aliases=None):
    nk = grid[-1]
    npair, nex, nout, nred = len(pairs), len(extras), len(outs), len(reds)

    def body(*refs):
        a_refs = refs[:npair]
        b_refs = refs[npair:2 * npair]
        p0 = 2 * npair
        e_refs = refs[p0:p0 + nex]
        o_refs = refs[p0 + nex:p0 + nex + nout]
        r_refs = refs[p0 + nex + nout:p0 + nex + nout + nred]
        ids = [pl.program_id(a) for a in range(len(grid))]
        k = ids[-1]
        if nred:
            first = ids[0] == 0
            for v in ids[1:]:
                first = first & (v == 0)

            @pl.when(first)
            def _():
                for r in r_refs:
                    r[...] = jnp.zeros(r.shape, r.dtype)

        part = None
        for a_ref, b_ref in zip(a_refs, b_refs):
            d = lax.dot_general(a_ref[...], b_ref[...], (dims, ((), ())), preferred_element_type=F32)
            part = d if part is None else part + d
        if nk == 1:
            epi(part, e_refs, o_refs, r_refs, ids)
        else:
            acc_ref = refs[-1]

            @pl.when(k == 0)
            def _():
                acc_ref[...] = part

            @pl.when(k > 0)
            def _():
                acc_ref[...] += part

            @pl.when(k == nk - 1)
            def _():
                epi(acc_ref[...], e_refs, o_refs, r_refs, ids)

    in_specs = [p[1] for p in pairs] + [p[3] for p in pairs] + [e[1] for e in extras]
    args = [p[0] for p in pairs] + [p[2] for p in pairs] + [e[0] for e in extras]
    out_shape = [jax.ShapeDtypeStruct(o[0], o[1]) for o in outs] + [jax.ShapeDtypeStruct(r, F32) for r in reds]
    out_specs = [o[2] for o in outs] + [_full(r) for r in reds]
    scratch_shapes = [pltpu.VMEM(acc_shape, F32)] if nk > 1 else []
    return pl.pallas_call(
        body, grid=grid, in_specs=in_specs, out_specs=out_specs, out_shape=out_shape, scratch_shapes=scratch_shapes,
        input_output_aliases=aliases or {}, compiler_params=_cparams(len(grid)), name=name)(*args)


def _rope(v, cos_t, sin_t):
    half = ROPE_DIM // 2
    first = (lax.broadcasted_iota(jnp.int32, cos_t.shape, 1) % HEAD_DIM) < half
    outs = []
    for cs in range(v.shape[1] // LANES):
        x = v[:, cs * LANES:(cs + 1) * LANES]
        partner = jnp.where(first, pltpu.roll(x, LANES - half, axis=1), pltpu.roll(x, half, axis=1))
        outs.append(x * cos_t + partner * sin_t)
    return outs[0] if len(outs) == 1 else jnp.concatenate(outs, axis=1)


def _spread_heads(v2, upper):
    other = pltpu.roll(v2, HEAD_DIM, axis=1)
    h0 = jnp.where(upper, other, v2)
    h1 = jnp.where(upper, v2, other)
    return jnp.concatenate([jnp.concatenate([h0, h0], axis=1), jnp.concatenate([h1, h1], axis=1)], axis=0)


def _sigmoid(v):
    return 0.5 * jnp.tanh(0.5 * v) + 0.5


def _rms_stats(v):
    r = lax.rsqrt(jnp.mean(v * v, axis=-1, keepdims=True) + EPS)
    return v * r, r


def _rms_bwd(dy, xhat, r, g):
    dxh = dy * g
    return r * (dxh - xhat * jnp.mean(dxh * xhat, axis=-1, keepdims=True))


def _head_sum_matrix():
    idx = np.arange(ATTN_W) // HEAD_DIM
    return jnp.asarray((idx[:, None] == idx[None, :]).astype(np.float32), dtype=BF16)


def _group_sum(v, e):
    hi = v.astype(BF16)
    lo = (v - hi.astype(F32)).astype(BF16)
    return jnp.dot(hi, e, preferred_element_type=F32) + jnp.dot(lo, e, preferred_element_type=F32)


TILE = 512


def _to_slabs(slab_ref, v):
    for cs in range(slab_ref.shape[0]):
        slab_ref[cs] = v[:, cs * LANES:(cs + 1) * LANES]


def _from_slabs(slab_ref):
    return jnp.concatenate([slab_ref[cs] for cs in range(slab_ref.shape[0])], axis=1)


def _class_rows(slab_ref, r, dil):
    n = slab_ref.shape[1] // dil
    return jnp.concatenate([slab_ref.at[cs][pl.ds(r, n, stride=dil), :] for cs in range(slab_ref.shape[0])], axis=1)


def _put_class_rows(slab_ref, r, dil, v):
    n = slab_ref.shape[1] // dil
    for cs in range(slab_ref.shape[0]):
        slab_ref.at[cs][pl.ds(r, n, stride=dil), :] = v[:, cs * LANES:(cs + 1) * LANES]


def _natural_from_group(slab_ref, grp_ref):
    dil = grp_ref.shape[0]
    for r in range(dil):
        _put_class_rows(slab_ref, r, dil, grp_ref[r].astype(F32))
    return _from_slabs(slab_ref)


def _group_from_natural(slab_ref, grp_ref, v):
    dil = grp_ref.shape[0]
    _to_slabs(slab_ref, v)
    for r in range(dil):
        grp_ref[r] = _class_rows(slab_ref, r, dil).astype(grp_ref.dtype)


def _group_spec(dil, tile, width):
    return pl.BlockSpec((dil, tile // dil, width), lambda i, *_: (0, i, 0))


def _slabs(tile, width):
    return pltpu.VMEM((width // LANES, tile, LANES), F32)


def _rope_consts():
    lane = np.arange(LANES) % HEAD_DIM
    fi = lane % (ROPE_DIM // 2)
    invf = np.where(lane < ROPE_DIM, ROPE_THETA ** (-(2.0 * fi) / ROPE_DIM), 0.0)
    sgn = np.where(lane < ROPE_DIM // 2, -1.0, np.where(lane < ROPE_DIM, 1.0, 0.0))
    return (jnp.asarray(invf.astype(np.float32)).reshape(1, LANES), jnp.asarray(sgn.astype(np.float32)).reshape(1, LANES))


def _rope_tables(pos_col):
    t = pos_col.shape[0]
    tile = min(t, TILE)
    invf, sgn = _rope_consts()

    def body(p_ref, f_ref, s_ref, c0, s0, c1, s1, c2, s2, slab_c, slab_s):
        ang = p_ref[...].astype(F32) * f_ref[...]
        cos, sin = jnp.cos(ang), jnp.sin(ang) * s_ref[...]
        c0[...] = cos
        s0[...] = sin
        _group_from_natural(slab_c, c1, cos)
        _group_from_natural(slab_s, s1, sin)
        for r in range(DILATIONS[2]):
            c2[r] = _class_rows(slab_c, r, DILATIONS[2])
            s2[r] = _class_rows(slab_s, r, DILATIONS[2])

    nat = pl.BlockSpec((tile, LANES), lambda i: (i, 0))
    specs, shapes = [nat, nat], [(t, LANES)] * 2
    for d in DILATIONS[1:]:
        specs += [_group_spec(d, tile, LANES)] * 2
        shapes += [(d, t // d, LANES)] * 2
    outs = pl.pallas_call(
        body, grid=(t // tile,),
        in_specs=[pl.BlockSpec((tile, 1), lambda i: (i, 0)), _full((1, LANES)), _full((1, LANES))],
        out_specs=specs, out_shape=[jax.ShapeDtypeStruct(s, F32) for s in shapes],
        scratch_shapes=[_slabs(tile, LANES)] * 2,
        compiler_params=_cparams(1), name="rope_tables")(pos_col, invf, sgn)
    return [(outs[2 * g].reshape(t, LANES), outs[2 * g + 1].reshape(t, LANES)) for g in range(len(DILATIONS))]


def _norm_fwd(x, g):
    t = x.shape[0]
    tile = min(t, TILE)

    def body(x_ref, g_ref, h0_ref, h1_ref, h2_ref, slab):
        xhat, _ = _rms_stats(x_ref[...])
        hn = xhat * g_ref[...]
        h0_ref[...] = hn.astype(BF16)
        _group_from_natural(slab, h1_ref, hn)
        for r in range(DILATIONS[2]):
            h2_ref[r] = _class_rows(slab, r, DILATIONS[2]).astype(BF16)

    nat = pl.BlockSpec((tile, D_MODEL), lambda i: (i, 0))
    return pl.pallas_call(
        body, grid=(t // tile,),
        in_specs=[nat, _full((1, D_MODEL))],
        out_specs=[nat] + [_group_spec(d, tile, D_MODEL) for d in DILATIONS[1:]],
        out_shape=[jax.ShapeDtypeStruct((t, D_MODEL), BF16)]
        + [jax.ShapeDtypeStruct((d, t // d, D_MODEL), BF16) for d in DILATIONS[1:]],
        scratch_shapes=[_slabs(tile, D_MODEL)],
        compiler_params=_cparams(1), name="norm1_fwd")(x, g)


GU_COLS = 3072
GROUP_COLS = 1536
GU_HALF = GU_COLS // 2


def _w_in_spec(width, block):
    return pl.BlockSpec((D_MODEL, width), lambda i: (0, block), pipeline_mode=pl.Buffered(1))


def _gu_w_specs():
    first = QKV_BLOCKS * ATTN_W // GU_HALF
    return [_w_in_spec(GU_HALF, first), _w_in_spec(GU_HALF, first + 1)]


def _group_w_specs(g):
    return [_w_in_spec(ATTN_W, _w_in_block(part, g)) for part in range(3)]


def _in_proj(hs, w_in, tables):
    t = hs[0].shape[0]
    tm = min(t, 1024)

    def body_gu(h_ref, w0_ref, w1_ref, o_ref):
        h = h_ref[...]
        o_ref[:, 0:GU_HALF] = jnp.dot(h, w0_ref[...], preferred_element_type=F32).astype(BF16)
        o_ref[:, GU_HALF:] = jnp.dot(h, w1_ref[...], preferred_element_type=F32).astype(BF16)

    gu = _token_call("in_proj_gates_uv", body_gu, t, tm,
                     [(hs[0], _rows_spec(tm, D_MODEL))] + [(w_in, s) for s in _gu_w_specs()],
                     [((t, GU_COLS), BF16, _rows_spec(tm, GU_COLS))])[0]

    qkvs = []
    for g in range(len(DILATIONS)):

        def body_qkv(h_ref, wq_ref, wk_ref, wv_ref, cos_ref, sin_ref, o_ref):
            h = h_ref[...]
            cos_w, sin_w = cos_ref[...], sin_ref[...]
            q = jnp.dot(h, wq_ref[...], preferred_element_type=F32)
            o_ref[:, 0:ATTN_W] = (_rope(q, cos_w, sin_w) * HEAD_DIM ** -0.5).astype(BF16)
            k = jnp.dot(h, wk_ref[...], preferred_element_type=F32)
            o_ref[:, ATTN_W:2 * ATTN_W] = _rope(k, cos_w, sin_w).astype(BF16)
            o_ref[:, 2 * ATTN_W:] = jnp.dot(h, wv_ref[...], preferred_element_type=F32).astype(BF16)

        cos_t, sin_t = tables[g]
        qkvs.append(_token_call(
            f"in_proj_qkv_g{g}", body_qkv, t, tm,
            [(hs[g].reshape(t, D_MODEL), _rows_spec(tm, D_MODEL))] + [(w_in, s) for s in _group_w_specs(g)]
            + [(cos_t, _rows_spec(tm, LANES)), (sin_t, _rows_spec(tm, LANES))],
            [((t, GROUP_COLS), BF16, _rows_spec(tm, GROUP_COLS))])[0])
    return gu, qkvs


def _attn_masks(n):
    row = lax.broadcasted_iota(jnp.int32, (2 * BLK, 2 * BLK), 0) % BLK
    col = lax.broadcasted_iota(jnp.int32, (2 * BLK, 2 * BLK), 1)
    diff = BLK + row - col
    valid = (diff >= 0) & (diff <= BLK) & ((col >= BLK) | (n > 0))
    upper = lax.broadcasted_iota(jnp.int32, (BLK, LANES), 1) >= HEAD_DIM
    return valid, upper


def _stack_heads(v2, upper):
    zero = jnp.zeros_like(v2)
    return jnp.concatenate([jnp.where(upper, zero, v2), jnp.where(upper, v2, zero)], axis=0)


def _unstack_heads(v, upper):
    return jnp.where(upper, v[BLK:], v[:BLK])


def _attn_fwd(qkv, g, dil):
    t = qkv.shape[0]
    length = t // dil
    nb = length // BLK
    view = qkv.reshape(dil, length, GROUP_COLS)

    def body(q_ref, kc_ref, kp_ref, vc_ref, vp_ref, o_ref, l_ref):
        n = pl.program_id(1)
        valid, upper = _attn_masks(n)
        for p in range(ATTN_W // LANES):
            sl = slice(p * LANES, (p + 1) * LANES)
            qs = _stack_heads(q_ref[:, sl], upper)
            k2 = jnp.concatenate([kp_ref[:, sl], kc_ref[:, sl]], axis=0)
            v2 = jnp.concatenate([vp_ref[:, sl], vc_ref[:, sl]], axis=0)
            s = lax.dot_general(qs, k2, (NT, ((), ())), preferred_element_type=F32)
            s = jnp.where(valid, s, NEG)
            m = jnp.max(s, axis=1, keepdims=True)
            pe = jnp.exp(s - m)
            den = jnp.sum(pe, axis=1, keepdims=True)
            o = jnp.dot(pe.astype(BF16), v2, preferred_element_type=F32) / den
            lse = jnp.broadcast_to(m + jnp.log(den), (2 * BLK, LANES))
            o_ref[:, sl] = _unstack_heads(o, upper).astype(BF16)
            l_ref[:, sl] = _unstack_heads(lse, upper)

    cur = lambda part: pl.BlockSpec((None, BLK, ATTN_W), lambda r, n: (r, n, part))
    prev = lambda part: pl.BlockSpec((None, BLK, ATTN_W), lambda r, n: (r, jnp.maximum(n - 1, 0), part))
    out_spec = pl.BlockSpec((None, BLK, ATTN_W), lambda r, n: (r, n, 0))
    return pl.pallas_call(
        body, grid=(dil, nb),
        in_specs=[cur(0), cur(1), prev(1), cur(2), prev(2)],
        out_specs=[out_spec, out_spec],
        out_shape=[jax.ShapeDtypeStruct((dil, length, ATTN_W), BF16), jax.ShapeDtypeStruct((dil, length, ATTN_W), F32)],
        compiler_params=_cparams(2), name=f"attn_fwd_g{g}")(view, view, view, view, view)


def _alphas(l0, l1, l2):
    m = jnp.maximum(jnp.maximum(l0, l1), l2)
    e0, e1, e2 = jnp.exp(l0 - m), jnp.exp(l1 - m), jnp.exp(l2 - m)
    inv = 1.0 / (e0 + e1 + e2)
    return e0 * inv, e1 * inv, e2 * inv


def _natural_group_values(o_refs, l_refs, slabs):
    os_ = [o_refs[0][0].astype(F32)] + [_natural_from_group(slabs[2 * g - 2], o_refs[g]) for g in (1, 2)]
    ls_ = [l_refs[0][0]] + [_natural_from_group(slabs[2 * g - 1], l_refs[g]) for g in (1, 2)]
    return os_, ls_


def _combine_fwd(os_, ls_):
    t = os_[0].shape[1]
    tile = min(t, TILE)

    def body(o0, o1, o2, l0, l1, l2, a_ref, *slabs):
        ov, lv = _natural_group_values((o0, o1, o2), (l0, l1, l2), slabs)
        a0, a1, a2 = _alphas(*lv)
        a_ref[...] = (a0 * ov[0] + a1 * ov[1] + a2 * ov[2]).astype(BF16)

    specs = [_group_spec(d, tile, ATTN_W) for d in DILATIONS]
    return pl.pallas_call(
        body, grid=(t // tile,), in_specs=specs * 2, out_specs=pl.BlockSpec((tile, ATTN_W), lambda i: (i, 0)),
        out_shape=jax.ShapeDtypeStruct((t, ATTN_W), BF16),
        scratch_shapes=[_slabs(tile, ATTN_W)] * 4,
        compiler_params=_cparams(1), name="combine_fwd")(*os_, *ls_)


def _combine_bwd(dattn, os_, ls_):
    t = dattn.shape[0]
    tile = min(t, TILE)
    e = _head_sum_matrix()

    def body(d_ref, o0, o1, o2, l0, l1, l2, e_ref, do0, do1, do2, c0, c1, c2, *slabs):
        ov, lv = _natural_group_values((o0, o1, o2), (l0, l1, l2), slabs)
        alphas = _alphas(*lv)
        d = d_ref[...]
        attn = alphas[0] * ov[0] + alphas[1] * ov[1] + alphas[2] * ov[2]
        s = _group_sum(d * attn, e_ref[...])
        do0[0] = (alphas[0] * d).astype(BF16)
        c0[0] = -alphas[0] * s
        for g, do_ref, c_ref in ((1, do1, c1), (2, do2, c2)):
            _group_from_natural(slabs[2 * g - 2], do_ref, alphas[g] * d)
            _group_from_natural(slabs[2 * g - 1], c_ref, -alphas[g] * s)

    specs = [_group_spec(d, tile, ATTN_W) for d in DILATIONS]
    shapes = [(d, t // d, ATTN_W) for d in DILATIONS]
    outs = pl.pallas_call(
        body, grid=(t // tile,),
        in_specs=[pl.BlockSpec((tile, ATTN_W), lambda i: (i, 0))] + specs * 2 + [_full((ATTN_W, ATTN_W))],
        out_specs=specs * 2,
        out_shape=[jax.ShapeDtypeStruct(s, BF16) for s in shapes] + [jax.ShapeDtypeStruct(s, F32) for s in shapes],
        scratch_shapes=[_slabs(tile, ATTN_W)] * 4,
        compiler_params=_cparams(1), name="combine_bwd")(dattn, *os_, *ls_, e)
    return outs[:3], outs[3:]


def _attn_bwd(qkv, do, cc, lse, cos_t, sin_t, g, dil):
    t = qkv.shape[0]
    length = t // dil
    nb = length // BLK
    qkv_v = qkv.reshape(dil, length, GROUP_COLS)
    cos_v, sin_v = (a.reshape(dil, length, LANES) for a in (cos_t, sin_t))
    scale = HEAD_DIM ** -0.5

    def body(q_ref, kc_ref, kp_ref, vc_ref, vp_ref, do_ref, c_ref, l_ref, cosc, sinc, cosp, sinp,
             out_ref, dq_s, dk_s, dv_s):
        n = pl.program_id(1)
        valid, upper = _attn_masks(n)

        @pl.when(n < nb)
        def _():
            cos_c, sin_c = cosc[...], -sinc[...]
            cos_p, sin_p = cosp[...], -sinp[...]
            dq_parts, dkp_parts, dkc_parts, dvp_parts, dvc_parts = [], [], [], [], []
            for p in range(ATTN_W // LANES):
                sl = slice(p * LANES, (p + 1) * LANES)
                qs = _stack_heads(q_ref[:, sl], upper)
                dos = _stack_heads(do_ref[:, sl], upper)
                k2 = jnp.concatenate([kp_ref[:, sl], kc_ref[:, sl]], axis=0)
                v2 = jnp.concatenate([vp_ref[:, sl], vc_ref[:, sl]], axis=0)
                l_col = _spread_heads(l_ref[:, sl], upper)
                c_col = _spread_heads(c_ref[:, sl], upper)
                s = lax.dot_general(qs, k2, (NT, ((), ())), preferred_element_type=F32)
                pe = jnp.exp(jnp.where(valid, s, NEG) - l_col)
                dpv = lax.dot_general(dos, v2, (NT, ((), ())), preferred_element_type=F32)
                ds = (pe * (dpv + c_col)).astype(BF16)
                dq2 = _unstack_heads(jnp.dot(ds, k2, preferred_element_type=F32), upper)
                dk2 = lax.dot_general(ds, qs, (TN, ((), ())), preferred_element_type=F32)
                dv2 = lax.dot_general(pe.astype(BF16), dos, (TN, ((), ())), preferred_element_type=F32)
                dq_parts.append(dq2)
                dkp_parts.append(dk2[:BLK])
                dkc_parts.append(dk2[BLK:])
                dvp_parts.append(dv2[:BLK])
                dvc_parts.append(dv2[BLK:])
            dq = _rope(jnp.concatenate(dq_parts, axis=1) * scale, cos_c, sin_c)
            dkc = _rope(jnp.concatenate(dkc_parts, axis=1), cos_c, sin_c)
            dkp = _rope(jnp.concatenate(dkp_parts, axis=1), cos_p, sin_p)
            dvp = jnp.concatenate(dvp_parts, axis=1)
            dvc = jnp.concatenate(dvc_parts, axis=1)

            @pl.when(n > 0)
            def _():
                out_ref[:, 0:ATTN_W] = dq_s[...].astype(BF16)
                out_ref[:, ATTN_W:2 * ATTN_W] = (dk_s[...] + dkp).astype(BF16)
                out_ref[:, 2 * ATTN_W:3 * ATTN_W] = (dv_s[...] + dvp).astype(BF16)

            dq_s[...] = dq
            dk_s[...] = dkc
            dv_s[...] = dvc

        @pl.when(n == nb)
        def _():
            out_ref[:, 0:ATTN_W] = dq_s[...].astype(BF16)
            out_ref[:, ATTN_W:2 * ATTN_W] = dk_s[...].astype(BF16)
            out_ref[:, 2 * ATTN_W:3 * ATTN_W] = dv_s[...].astype(BF16)

    nc = lambda n: jnp.minimum(n, nb - 1)
    npv = lambda n: jnp.maximum(jnp.minimum(n, nb - 1) - 1, 0)
    cur = lambda part: pl.BlockSpec((None, BLK, ATTN_W), lambda r, n: (r, nc(n), part))
    prev = lambda part: pl.BlockSpec((None, BLK, ATTN_W), lambda r, n: (r, npv(n), part))
    row = pl.BlockSpec((None, BLK, ATTN_W), lambda r, n: (r, nc(n), 0))
    tab_c = pl.BlockSpec((None, BLK, LANES), lambda r, n: (r, nc(n), 0))
    tab_p = pl.BlockSpec((None, BLK, LANES), lambda r, n: (r, npv(n), 0))
    out_spec = pl.BlockSpec((None, BLK, GROUP_COLS), lambda r, n: (r, jnp.maximum(n - 1, 0), 0))
    out = pl.pallas_call(
        body, grid=(dil, nb + 1),
        in_specs=[cur(0), cur(1), prev(1), cur(2), prev(2), row, row, row, tab_c, tab_c, tab_p, tab_p],
        out_specs=out_spec,
        out_shape=jax.ShapeDtypeStruct((dil, length, GROUP_COLS), BF16),
        scratch_shapes=[pltpu.VMEM((BLK, ATTN_W), F32)] * 3,
        compiler_params=_cparams(2), name=f"attn_bwd_g{g}")(
            qkv_v, qkv_v, qkv_v, qkv_v, qkv_v, do, cc, lse, cos_v, sin_v, cos_v, sin_v)
    return out.reshape(t, GROUP_COLS)


SQRT_HALF = 0.7071067811865476
INV_SQRT_2PI = 0.3989422804014327


def _sgu_core(uv, g, b, w_ref, bias):
    cdf = 0.5 * (1.0 + lax.erf(uv * SQRT_HALF))
    z = uv * cdf
    u, v = z[:, :SGU_W], z[:, SGU_W:]
    mu = jnp.mean(v, axis=1, keepdims=True)
    xc = v - mu
    rs = lax.rsqrt(jnp.mean(xc * xc, axis=1, keepdims=True) + EPS)
    xhat = xc * rs
    vn = xhat * g + b
    row = lax.broadcasted_iota(jnp.int32, (SGU_CHUNK, SGU_CHUNK), 0)
    col = lax.broadcasted_iota(jnp.int32, (SGU_CHUNK, SGU_CHUNK), 1)
    tril = row >= col
    upper = lax.broadcasted_iota(jnp.int32, (SGU_CHUNK, LANES), 1) >= SGU_W // SGU_GROUPS
    ws, vlo, vhi, mixed = [], [], [], []
    for pr in range(SGU_W // LANES):
        sl = slice(pr * LANES, (pr + 1) * LANES)
        w0 = jnp.where(tril, w_ref[2 * pr], 0.0).astype(BF16)
        w1 = jnp.where(tril, w_ref[2 * pr + 1], 0.0).astype(BF16)
        vn2 = vn[:, sl]
        lo = jnp.where(upper, 0.0, vn2).astype(BF16)
        hi = jnp.where(upper, vn2, 0.0).astype(BF16)
        mixed.append(jnp.dot(w0, lo, preferred_element_type=F32) + jnp.dot(w1, hi, preferred_element_type=F32)
                     + bias[:, sl])
        ws.append((w0, w1))
        vlo.append(lo)
        vhi.append(hi)
    return cdf, u, xhat, rs, jnp.concatenate(mixed, axis=1), ws, vlo, vhi, tril, upper


def _sgu_fwd(gu, ln_g, ln_b, w_s, bias_exp):
    t = gu.shape[0]

    def body(uv_ref, g_ref, b_ref, w_ref, bias_ref, o_ref):
        _, u, _, _, mixed, *_ = _sgu_core(uv_ref[...].astype(F32), g_ref[...], b_ref[...], w_ref, bias_ref[...])
        o_ref[...] = (u * mixed).astype(BF16)

    return pl.pallas_call(
        body, grid=(t // SGU_CHUNK,),
        in_specs=[pl.BlockSpec((SGU_CHUNK, 2 * SGU_W), lambda n: (n, 0)), _full((1, SGU_W)), _full((1, SGU_W)),
                  _full((SGU_GROUPS, SGU_CHUNK, SGU_CHUNK)), _full((SGU_CHUNK, SGU_W))],
        out_specs=pl.BlockSpec((SGU_CHUNK, SGU_W), lambda n: (n, 0)),
        out_shape=jax.ShapeDtypeStruct((t, SGU_W), BF16),
        compiler_params=_cparams(1), name="sgu_fwd")(gu, ln_g, ln_b, w_s, bias_exp)


def _sgu_bwd(dproj, gu, dsgu, ln_g, ln_b, w_s, bias_exp):
    t = gu.shape[0]
    nchunks = t // SGU_CHUNK
    e = _head_sum_matrix()

    def body(dp_in, uv_ref, ds_ref, g_ref, b_ref, w_ref, bias_ref, e_ref, out_ref, dw_ref, dbias_ref, dg_ref, db_ref):
        n = pl.program_id(0)

        @pl.when(n == 0)
        def _():
            dw_ref[...] = jnp.zeros(dw_ref.shape, F32)
            dbias_ref[...] = jnp.zeros(dbias_ref.shape, F32)
            dg_ref[...] = jnp.zeros(dg_ref.shape, F32)
            db_ref[...] = jnp.zeros(db_ref.shape, F32)

        uv = uv_ref[...].astype(F32)
        g = g_ref[...]
        cdf, u, xhat, rs, mixed, ws, vlo, vhi, tril, upper = _sgu_core(uv, g, b_ref[...], w_ref, bias_ref[...])
        dsg = ds_ref[...]
        du = dsg * mixed
        dmixed = dsg * u
        dbias_ref[...] += dmixed
        dvn = []
        for pr in range(SGU_W // LANES):
            sl = slice(pr * LANES, (pr + 1) * LANES)
            dm2 = dmixed[:, sl]
            dlo = jnp.where(upper, 0.0, dm2).astype(BF16)
            dhi = jnp.where(upper, dm2, 0.0).astype(BF16)
            w0, w1 = ws[pr]
            dvn.append(lax.dot_general(w0, dlo, (TN, ((), ())), preferred_element_type=F32)
                       + lax.dot_general(w1, dhi, (TN, ((), ())), preferred_element_type=F32))
            dw0 = lax.dot_general(dlo, vlo[pr], (NT, ((), ())), preferred_element_type=F32)
            dw1 = lax.dot_general(dhi, vhi[pr], (NT, ((), ())), preferred_element_type=F32)
            dw_ref[2 * pr] += jnp.where(tril, dw0, 0.0)
            dw_ref[2 * pr + 1] += jnp.where(tril, dw1, 0.0)
        dvn = jnp.concatenate(dvn, axis=1)
        dg_ref[...] += jnp.sum(dvn * xhat, axis=0, keepdims=True)
        db_ref[...] += jnp.sum(dvn, axis=0, keepdims=True)
        dxh = dvn * g
        dv = rs * (dxh - jnp.mean(dxh, axis=1, keepdims=True) - xhat * jnp.mean(dxh * xhat, axis=1, keepdims=True))
        dz = jnp.concatenate([du, dv], axis=1)
        dgelu = cdf + uv * (INV_SQRT_2PI * jnp.exp(-0.5 * uv * uv))
        out_ref[...] = (dz * dgelu).astype(BF16)

        @pl.when(n == nchunks - 1)
        def _():
            dbias_ref[...] = _group_sum(dbias_ref[...], e_ref[...])

    outs = pl.pallas_call(
        body, grid=(nchunks,),
        in_specs=[pl.BlockSpec(memory_space=pl.ANY), pl.BlockSpec((SGU_CHUNK, 2 * SGU_W), lambda n: (n, 0)),
                  pl.BlockSpec((SGU_CHUNK, SGU_W), lambda n: (n, 0)), _full((1, SGU_W)), _full((1, SGU_W)),
                  _full((SGU_GROUPS, SGU_CHUNK, SGU_CHUNK)), _full((SGU_CHUNK, SGU_W)), _full((ATTN_W, ATTN_W))],
        out_specs=[pl.BlockSpec((SGU_CHUNK, 2 * SGU_W), lambda n: (n, 0)), _full((SGU_GROUPS, SGU_CHUNK, SGU_CHUNK)),
                   _full((SGU_CHUNK, SGU_W)), _full((1, SGU_W)), _full((1, SGU_W))],
        out_shape=[jax.ShapeDtypeStruct(dproj.shape, BF16), jax.ShapeDtypeStruct((SGU_GROUPS, SGU_CHUNK, SGU_CHUNK), F32),
                   jax.ShapeDtypeStruct((SGU_CHUNK, SGU_W), F32), jax.ShapeDtypeStruct((1, SGU_W), F32),
                   jax.ShapeDtypeStruct((1, SGU_W), F32)],
        input_output_aliases={0: 0},
        compiler_params=_cparams(1), name="sgu_bwd")(dproj, gu, dsgu, ln_g, ln_b, w_s, bias_exp, e)
    return outs


def _merge_fwd(attn, sgu, gu, x, w_pa, w_ps, w_out, g2):
    t = x.shape[0]
    tm = min(t, 512)

    def body(a_ref, s_ref, ga_ref, gb_ref, x_ref, wpa, wps, wo, g_ref, pa_ref, ps_ref, m_ref, x1_ref, h2_ref):
        pa = jnp.dot(a_ref[...], wpa[...], preferred_element_type=F32)
        ps = jnp.dot(s_ref[...], wps[...], preferred_element_type=F32)
        merged = (_sigmoid(ga_ref[...].astype(F32)) * pa + _sigmoid(gb_ref[...].astype(F32)) * ps).astype(BF16)
        x1 = x_ref[...] + jnp.dot(merged, wo[...], preferred_element_type=F32)
        xhat, _ = _rms_stats(x1)
        pa_ref[...] = pa.astype(BF16)
        ps_ref[...] = ps.astype(BF16)
        m_ref[...] = merged
        x1_ref[...] = x1
        h2_ref[...] = (xhat * g_ref[...]).astype(BF16)

    half = pl.BlockSpec((tm, ATTN_W), lambda i: (i, 0))
    full = pl.BlockSpec((tm, D_MODEL), lambda i: (i, 0))
    return pl.pallas_call(
        body, grid=(t // tm,),
        in_specs=[half, half, pl.BlockSpec((tm, D_MODEL), lambda i: (i, 1)), pl.BlockSpec((tm, D_MODEL), lambda i: (i, 2)),
                  full, _resident((ATTN_W, D_MODEL)), _resident((SGU_W, D_MODEL)), _resident((D_MODEL, D_MODEL)),
                  _full((1, D_MODEL))],
        out_specs=[full] * 5,
        out_shape=[jax.ShapeDtypeStruct((t, D_MODEL), BF16), jax.ShapeDtypeStruct((t, D_MODEL), BF16),
                   jax.ShapeDtypeStruct((t, D_MODEL), BF16), jax.ShapeDtypeStruct((t, D_MODEL), F32),
                   jax.ShapeDtypeStruct((t, D_MODEL), BF16)],
        compiler_params=_cparams(1), name="merge_fwd")(attn, sgu, gu, gu, x, w_pa, w_ps, w_out, g2)


def _merge_bwd(dx1b, gu, pa, ps, w_pa, w_ps, w_out):
    t = dx1b.shape[0]
    tm = min(t, 512)

    def body(d_ref, ga_ref, gb_ref, pa_ref, ps_ref, wpa, wps, wo, out_ref, dpa_ref, dps_ref, da_ref, dsg_ref):
        dm = lax.dot_general(d_ref[...], wo[...], (NT, ((), ())), preferred_element_type=F32)
        sa, sb = _sigmoid(ga_ref[...].astype(F32)), _sigmoid(gb_ref[...].astype(F32))
        dpa = (dm * sa).astype(BF16)
        dps = (dm * sb).astype(BF16)
        out_ref[:, 0:D_MODEL] = jnp.zeros((tm, D_MODEL), BF16)
        out_ref[:, D_MODEL:2 * D_MODEL] = (dm * pa_ref[...].astype(F32) * sa * (1.0 - sa)).astype(BF16)
        out_ref[:, 2 * D_MODEL:GU_COLS] = (dm * ps_ref[...].astype(F32) * sb * (1.0 - sb)).astype(BF16)
        dpa_ref[...] = dpa
        dps_ref[...] = dps
        da_ref[...] = lax.dot_general(dpa, wpa[...], (NT, ((), ())), preferred_element_type=F32)
        dsg_ref[...] = lax.dot_general(dps, wps[...], (NT, ((), ())), preferred_element_type=F32)

    half = pl.BlockSpec((tm, ATTN_W), lambda i: (i, 0))
    full = pl.BlockSpec((tm, D_MODEL), lambda i: (i, 0))
    return pl.pallas_call(
        body, grid=(t // tm,),
        in_specs=[full, pl.BlockSpec((tm, D_MODEL), lambda i: (i, 1)),
                  pl.BlockSpec((tm, D_MODEL), lambda i: (i, 2)), full, full,
                  _resident((ATTN_W, D_MODEL)), _resident((SGU_W, D_MODEL)), _resident((D_MODEL, D_MODEL))],
        out_specs=[pl.BlockSpec((tm, GU_COLS), lambda i: (i, 0)), full, full, half, half],
        out_shape=[jax.ShapeDtypeStruct((t, GU_COLS), BF16), jax.ShapeDtypeStruct((t, D_MODEL), BF16),
                   jax.ShapeDtypeStruct((t, D_MODEL), BF16), jax.ShapeDtypeStruct((t, ATTN_W), F32),
                   jax.ShapeDtypeStruct((t, SGU_W), F32)],
        compiler_params=_cparams(1), name="merge_bwd")(dx1b, gu, gu, pa, ps, w_pa, w_ps, w_out)


def _token_call(name, body, t, tm, ins, outs, reds=(), scratch=()):
    return pl.pallas_call(
        body, grid=(t // tm,), in_specs=[s for _, s in ins],
        out_specs=[o[2] for o in outs] + [_full(r) for r in reds],
        out_shape=[jax.ShapeDtypeStruct(o[0], o[1]) for o in outs] + [jax.ShapeDtypeStruct(r, F32) for r in reds],
        scratch_shapes=list(scratch), compiler_params=_cparams(1), name=name)(*[a for a, _ in ins])


def _rows_spec(tm, width):
    return pl.BlockSpec((tm, width), lambda i: (i, 0))


def _chips_spec(tm):
    return pl.BlockSpec((N_CHIPS, tm, FF_SHARD), lambda i: (0, i, 0))


def _zero_at_start(*refs):
    @pl.when(pl.program_id(0) == 0)
    def _():
        for r in refs:
            r[...] = jnp.zeros(r.shape, r.dtype)


def _ffn_fwd(h2, w_g, w_u):
    t = h2.shape[0]
    tm = min(t, 512)

    def body(h_ref, wg_ref, wu_ref, a_ref, b_ref, ff_ref):
        h = h_ref[...]
        for s in range(N_CHIPS):
            a = jnp.dot(h, wg_ref[s], preferred_element_type=F32)
            b = jnp.dot(h, wu_ref[s], preferred_element_type=F32)
            a_ref[s] = a.astype(BF16)
            b_ref[s] = b.astype(BF16)
            ff_ref[s] = (a * _sigmoid(a) * b).astype(BF16)

    shp = (N_CHIPS, t, FF_SHARD)
    w_spec = _resident((N_CHIPS, D_MODEL, FF_SHARD))
    return _token_call("ffn_fwd", body, t, tm, [(h2, _rows_spec(tm, D_MODEL)), (w_g, w_spec), (w_u, w_spec)],
                       [(shp, BF16, _chips_spec(tm))] * 3)


def _ffn_down_loss(ff, w_d, x1, tgt, gf):
    t = x1.shape[0]
    tm = min(t, 512)

    def body(ff_ref, wd_ref, x1_ref, tgt_ref, g_ref, dx2_ref, dx2b_ref, loss_ref, dgf_ref):
        _zero_at_start(loss_ref, dgf_ref)
        acc = jnp.dot(ff_ref[0], wd_ref[0], preferred_element_type=F32)
        for s in range(1, N_CHIPS):
            acc = acc + jnp.dot(ff_ref[s], wd_ref[s], preferred_element_type=F32)
        x2 = x1_ref[...] + acc
        g = g_ref[...]
        xhat, rr = _rms_stats(x2)
        diff = xhat * g - tgt_ref[...]
        rows = jnp.sum(diff * diff, axis=1, keepdims=True)
        loss_ref[...] += jnp.broadcast_to(jnp.sum(rows, axis=0, keepdims=True) * (0.5 / D_MODEL), (1, LANES))
        dy = diff * (1.0 / D_MODEL)
        dgf_ref[...] += jnp.sum(dy * xhat, axis=0, keepdims=True)
        dx2 = _rms_bwd(dy, xhat, rr, g)
        dx2_ref[...] = dx2
        dx2b_ref[...] = dx2.astype(BF16)

    row = _rows_spec(tm, D_MODEL)
    return _token_call("ffn_down_loss", body, t, tm,
                       [(ff, _chips_spec(tm)), (w_d, _resident((N_CHIPS, FF_SHARD, D_MODEL))), (x1, row), (tgt, row),
                        (gf, _full((1, D_MODEL)))],
                       [((t, D_MODEL), F32, row), ((t, D_MODEL), BF16, row)], reds=[(1, LANES), (1, D_MODEL)])


def _ffn_bwd_act(dx2b, w_d, a, b):
    t = dx2b.shape[0]
    tm = min(t, 512)

    def body(d_ref, wd_ref, a_ref, b_ref, da_ref, db_ref):
        d = d_ref[...]
        for s in range(N_CHIPS):
            dff = lax.dot_general(d, wd_ref[s], (NT, ((), ())), preferred_element_type=F32)
            av, bv = a_ref[s].astype(F32), b_ref[s].astype(F32)
            sg = _sigmoid(av)
            da_ref[s] = (dff * bv * (sg * (1.0 + av * (1.0 - sg)))).astype(BF16)
            db_ref[s] = (dff * (av * sg)).astype(BF16)

    shp = (N_CHIPS, t, FF_SHARD)
    return _token_call("ffn_bwd_act", body, t, tm,
                       [(dx2b, _rows_spec(tm, D_MODEL)), (w_d, _resident((N_CHIPS, FF_SHARD, D_MODEL))),
                        (a, _chips_spec(tm)), (b, _chips_spec(tm))],
                       [(shp, BF16, _chips_spec(tm))] * 2)


def _ffn_bwd_in(da, db, w_g, w_u, x1, dx2, g2):
    t = x1.shape[0]
    tm = min(t, 512)

    def body(da_ref, db_ref, wg_ref, wu_ref, x1_ref, dx2_ref, g_ref, dx1_ref, dx1b_ref, dg_ref):
        _zero_at_start(dg_ref)
        acc = None
        for s in range(N_CHIPS):
            part = (lax.dot_general(da_ref[s], wg_ref[s], (NT, ((), ())), preferred_element_type=F32)
                    + lax.dot_general(db_ref[s], wu_ref[s], (NT, ((), ())), preferred_element_type=F32))
            acc = part if acc is None else acc + part
        xhat, rr = _rms_stats(x1_ref[...])
        dg_ref[...] += jnp.sum(acc * xhat, axis=0, keepdims=True)
        dx1 = dx2_ref[...] + _rms_bwd(acc, xhat, rr, g_ref[...])
        dx1_ref[...] = dx1
        dx1b_ref[...] = dx1.astype(BF16)

    row = _rows_spec(tm, D_MODEL)
    w_spec = _resident((N_CHIPS, D_MODEL, FF_SHARD))
    return _token_call("ffn_bwd_in", body, t, tm,
                       [(da, _chips_spec(tm)), (db, _chips_spec(tm)), (w_g, w_spec), (w_u, w_spec), (x1, row), (dx2, row),
                        (g2, _full((1, D_MODEL)))],
                       [((t, D_MODEL), F32, row), ((t, D_MODEL), BF16, row)], reds=[(1, D_MODEL)])


def _group_dh(d, w_refs):
    dh = None
    for part, w_ref in enumerate(w_refs):
        term = lax.dot_general(d[:, part * ATTN_W:(part + 1) * ATTN_W], w_ref[...], (NT, ((), ())),
                               preferred_element_type=F32)
        dh = term if dh is None else dh + term
    return dh


def _in_proj_bwd(dgu, dqkvs, w_in, x, dx1, g1):
    t = x.shape[0]
    tile = min(t, TILE)
    ngroups = len(DILATIONS)

    def body(*refs):
        dgu_ref, dq_refs = refs[0], refs[1:1 + ngroups]
        w0_ref, w1_ref = refs[1 + ngroups:3 + ngroups]
        wg_refs = [refs[3 + ngroups + 3 * g:6 + ngroups + 3 * g] for g in range(ngroups)]
        x_ref, dx1_ref, g_ref, dx_ref, dg_ref, slab = refs[3 + 4 * ngroups:]
        _zero_at_start(dg_ref)
        dh = lax.dot_general(dgu_ref[:, 0:GU_HALF], w0_ref[...], (NT, ((), ())), preferred_element_type=F32)
        dh = dh + lax.dot_general(dgu_ref[:, GU_HALF:], w1_ref[...], (NT, ((), ())), preferred_element_type=F32)
        dh = dh + _group_dh(dq_refs[0][0], wg_refs[0])
        for g in range(1, ngroups):
            dil = DILATIONS[g]
            part = _group_dh(dq_refs[g][...].reshape(tile, GROUP_COLS), wg_refs[g])
            for r in range(dil):
                _put_class_rows(slab, r, dil, part[r * (tile // dil):(r + 1) * (tile // dil)])
            dh = dh + _from_slabs(slab)
        xhat, rr = _rms_stats(x_ref[...])
        dg_ref[...] += jnp.sum(dh * xhat, axis=0, keepdims=True)
        dx_ref[...] = dx1_ref[...] + _rms_bwd(dh, xhat, rr, g_ref[...])

    row = _rows_spec(tile, D_MODEL)
    group_ins = [(dqkvs[g].reshape(d, t // d, GROUP_COLS), _group_spec(d, tile, GROUP_COLS)) for g, d in enumerate(DILATIONS)]
    w_specs = _gu_w_specs() + [s for g in range(ngroups) for s in _group_w_specs(g)]
    return _token_call(
        "in_proj_bwd", body, t, tile,
        [(dgu, _rows_spec(tile, GU_COLS))] + group_ins + [(w_in, s) for s in w_specs]
        + [(x, row), (dx1, row), (g1, _full((1, D_MODEL)))],
        [((t, D_MODEL), F32, row)], reds=[(1, D_MODEL)], scratch=[_slabs(tile, D_MODEL)])


def _epi_bf16(acc, e, o, r, ids):
    o[0][...] = acc.astype(BF16)


WGRAD_TK = 2048


def _wgrad_2d(name, a, b, tm, tn):
    t, k1 = a.shape
    n = b.shape[1]
    tk = min(t, WGRAD_TK)
    return _mm(name, (k1 // tm, n // tn, t // tk),
               [(a, pl.BlockSpec((tk, tm), lambda i, j, k: (k, i)), b, pl.BlockSpec((tk, tn), lambda i, j, k: (k, j)))],
               TN, (tm, tn), _epi_bf16, outs=[((k1, n), BF16, pl.BlockSpec((tm, tn), lambda i, j, k: (i, j)))])[0]


def _wgrad_in(hs, dgu, dqkvs):
    t = dgu.shape[0]
    tk = min(t, WGRAD_TK)
    gu_block = QKV_BLOCKS * ATTN_W // GU_HALF
    parts = [(hs[0], dgu, GU_HALF, lambda j: j + gu_block)]
    parts += [(hs[g].reshape(t, D_MODEL), dqkvs[g], ATTN_W, lambda j, g=g: _w_in_block(j, g)) for g in range(3)]
    dst = None
    for n, (a, b, tn, block_of) in enumerate(parts):
        dst = _mm(f"wgrad_in_{n}", (1, b.shape[1] // tn, t // tk),
                  [(a, pl.BlockSpec((tk, D_MODEL), lambda i, j, k: (k, 0)), b,
                    pl.BlockSpec((tk, tn), lambda i, j, k: (k, j)))],
                  TN, (D_MODEL, tn), _epi_bf16,
                  extras=[] if dst is None else [(dst, pl.BlockSpec(memory_space=pl.ANY))],
                  outs=[((D_MODEL, IN_COLS), BF16,
                         pl.BlockSpec((D_MODEL, tn), lambda i, j, k, block_of=block_of: (0, block_of(j))))],
                  aliases=None if dst is None else {2: 0})[0]
    return dst


def _wgrad_ff_in(name, h2, da):
    t = h2.shape[0]
    tk = min(t, WGRAD_TK)
    return _mm(name, (N_CHIPS, 1, t // tk),
               [(h2, pl.BlockSpec((tk, D_MODEL), lambda i, j, k: (k, 0)),
                 da, pl.BlockSpec((None, tk, FF_SHARD), lambda i, j, k: (i, k, 0)))],
               TN, (D_MODEL, FF_SHARD), _epi_bf16,
               outs=[((N_CHIPS, D_MODEL, FF_SHARD), BF16, pl.BlockSpec((None, D_MODEL, FF_SHARD), lambda i, j, k: (i, 0, 0)))])[0]


def _wgrad_ff_down(ff, dx2b):
    t = dx2b.shape[0]
    tk = min(t, WGRAD_TK)
    return _mm("wgrad_ffn_down", (N_CHIPS, 1, t // tk),
               [(ff, pl.BlockSpec((None, tk, FF_SHARD), lambda i, j, k: (i, k, 0)),
                 dx2b, pl.BlockSpec((tk, D_MODEL), lambda i, j, k: (k, 0)))],
               TN, (FF_SHARD, D_MODEL), _epi_bf16,
               outs=[((N_CHIPS, FF_SHARD, D_MODEL), BF16, pl.BlockSpec((None, FF_SHARD, D_MODEL), lambda i, j, k: (i, 0, 0)))])[0]


def _local_step(x, pos_col, tgt, g1, ln_g, ln_b, w_s, b_s, g2, gf, first_weight, late_weights, on_grads=None):
    tables = _rope_tables(pos_col)
    bias_exp = jnp.repeat(jnp.transpose(b_s), SGU_W // SGU_GROUPS, axis=1)

    hs = _norm_fwd(x, g1)
    w_p = first_weight(hs[0])
    gu, qkvs = _in_proj(hs, w_p, tables)
    os_, ls_ = [], []
    for g, dil in enumerate(DILATIONS):
        o, lse = _attn_fwd(qkvs[g], g, dil)
        os_.append(o)
        ls_.append(lse)
    attn = _combine_fwd(os_, ls_)
    sgu = _sgu_fwd(gu, ln_g, ln_b, w_s, bias_exp)
    w_pa, w_ps, w_out, w_g, w_u, w_d = late_weights(attn)
    pa, ps, merged, x1, h2 = _merge_fwd(attn, sgu, gu, x, w_pa, w_ps, w_out, g2)
    a, b, ff = _ffn_fwd(h2, w_g, w_u)
    dx2, dx2b, loss, dgf = _ffn_down_loss(ff, w_d, x1, tgt, gf)

    da, db = _ffn_bwd_act(dx2b, w_d, a, b)
    dw_d = _wgrad_ff_down(ff, dx2b)
    dx1, dx1b, dg2 = _ffn_bwd_in(da, db, w_g, w_u, x1, dx2, g2)
    dw_g = _wgrad_ff_in("wgrad_ffn_gate", h2, da)
    dw_u = _wgrad_ff_in("wgrad_ffn_up", h2, db)

    dgu, dpa, dps, dattn, dsgu = _merge_bwd(dx1b, gu, pa, ps, w_pa, w_ps, w_out)
    dw_out = _wgrad_2d("wgrad_out", merged, dx1b, D_MODEL, D_MODEL)
    dw_pa = _wgrad_2d("wgrad_proj_attn", attn, dpa, ATTN_W, D_MODEL)
    dw_ps = _wgrad_2d("wgrad_proj_sgu", sgu, dps, SGU_W, D_MODEL)
    if on_grads is not None:
        ln_g = ln_g + on_grads(1, dict(w_proj_attn=dw_pa, w_proj_sgu=dw_ps, w_out=dw_out, w_ffn_gate=dw_g, w_ffn_up=dw_u,
                                       w_ffn_down=dw_d))[:, :SGU_W]
    dgu, dw_s, dbias, dln_g, dln_b = _sgu_bwd(dgu, gu, dsgu, ln_g, ln_b, w_s, bias_exp)
    dos, ccs = _combine_bwd(dattn, os_, ls_)
    dqkvs = [_attn_bwd(qkvs[g], dos[g], ccs[g], ls_[g], *tables[g], g, dil) for g, dil in enumerate(DILATIONS)]
    dw_p = _wgrad_in(hs, dgu, dqkvs)
    if on_grads is not None:
        g1 = g1 + on_grads(0, dict(w_in=dw_p))
    dx, dg1 = _in_proj_bwd(dgu, dqkvs, w_p, x, dx1, g1)

    db_s = jnp.transpose(dbias[:, ::SGU_W // SGU_GROUPS])
    small = dict(loss=loss, norm1_g=dg1, sgu_ln_g=dln_g, sgu_ln_b=dln_b, w_spatial=dw_s, b_spatial=db_s,
                 norm2_g=dg2, final_g=dgf)
    big = dict(w_in=dw_p, w_proj_attn=dw_pa, w_proj_sgu=dw_ps, w_out=dw_out, w_ffn_gate=dw_g, w_ffn_up=dw_u,
               w_ffn_down=dw_d)
    return dx, big, small


def _ew(name, fn, ins, out_dtypes):
    shp = ins[0].shape
    rows, cols = shp
    tr = next((cand for cand in (256, 352, 128) if rows % cand == 0 and rows > cand), rows)

    def body(*refs):
        res = fn(*[r[...] for r in refs[:len(ins)]])
        for o_ref, v in zip(refs[len(ins):], res):
            o_ref[...] = v.astype(o_ref.dtype)

    spec = pl.BlockSpec((tr, cols), lambda i: (i, 0))
    return pl.pallas_call(
        body, grid=(rows // tr,), in_specs=[spec] * len(ins), out_specs=[spec] * len(out_dtypes),
        out_shape=[jax.ShapeDtypeStruct(shp, d) for d in out_dtypes],
        compiler_params=_cparams(1), name=name)(*ins)


def _adamw_math(g, w, m, v):
    m = ADAM_B1 * m + (1.0 - ADAM_B1) * g
    v = ADAM_B2 * v + (1.0 - ADAM_B2) * (g * g)
    m_hat = m / (1.0 - ADAM_B1 ** ADAM_STEP)
    v_hat = v / (1.0 - ADAM_B2 ** ADAM_STEP)
    delta = -ADAM_LR * (m_hat / (jnp.sqrt(v_hat) + ADAM_EPS) + ADAM_WD * w)
    return delta, m, v


def _adamw(name, g, w, m, v):
    return _ew(name, lambda g_, w_, m_, v_: (g_,) + _adamw_math(g_, w_, m_, v_), [g, w, m, v], [F32] * 4)


VMEM_SPEC = pl.BlockSpec(memory_space=pltpu.VMEM)


def _for_row_chunks(rows, fn):
    ck = next(c for c in (64, 32, 16) if rows % c == 0)

    def step(i, carry):
        fn(pl.multiple_of(i * ck, ck), ck)
        return carry

    lax.fori_loop(0, rows // ck, step, 0)


def _place():
    x, y, c = lax.axis_index("x"), lax.axis_index("y"), lax.axis_index("c")
    chips = [(1 - x, y), (x, 1 - y), (1 - x, 1 - y)]
    return x, y, c, 2 * x + y, chips


def _rows(ref, start, size):
    if len(ref.shape) == 2:
        return ref.at[pl.ds(start, size), :]
    return ref.at[:, pl.ds(start, size), :]


def _comm_call(name, body, ins, out_shapes, scratch, n_remote):
    return pl.pallas_call(
        body, in_specs=[VMEM_SPEC] * len(ins), out_specs=[VMEM_SPEC] * len(out_shapes),
        out_shape=out_shapes,
        scratch_shapes=list(scratch) + [pltpu.SemaphoreType.DMA((n_remote,)), pltpu.SemaphoreType.DMA((n_remote,))],
        compiler_params=pltpu.CompilerParams(vmem_limit_bytes=VMEM_LIMIT), name=name)(*ins)


def _gather_finish(name, shard, landed):
    k_rows, n = shard.shape
    kh = k_rows // 2

    def body(shard_ref, land_ref, out_ref, send, recv):
        x, y, c, me, chips = _place()
        passed = []
        for j, chip in enumerate(chips):
            theirs = 2 * chip[0] + chip[1]
            cp = pltpu.make_async_remote_copy(
                src_ref=land_ref.at[j], dst_ref=_rows(out_ref.at[theirs], c * kh, kh), send_sem=send.at[j],
                recv_sem=recv.at[j], device_id=(x, y, 1 - c), device_id_type=MESH)
            cp.start()
            passed.append(cp)
        mine = out_ref.at[me]

        def put_own(r0, ck):
            mine[pl.ds(r0, ck), :] = shard_ref[pl.ds(r0, ck), :]

        _for_row_chunks(k_rows, put_own)
        for j, chip in enumerate(chips):
            slot = out_ref.at[2 * chip[0] + chip[1]]

            def put_half(r0, ck, j=j, slot=slot):
                slot[pl.ds(pl.multiple_of(c * kh + r0, ck), ck), :] = land_ref[j, pl.ds(r0, ck), :]

            _for_row_chunks(kh, put_half)
        for j, chip in enumerate(chips):
            other = _rows(out_ref.at[2 * chip[0] + chip[1]], (1 - c) * kh, kh)
            pltpu.make_async_remote_copy(src_ref=other, dst_ref=other, send_sem=send.at[j], recv_sem=recv.at[j],
                                         device_id=(x, y, 1 - c), device_id_type=MESH).wait_recv()
        for cp in passed:
            cp.wait_send()

    return _comm_call(name, body, [shard, landed], [jax.ShapeDtypeStruct((N_CHIPS, k_rows, n), shard.dtype)], [], 3)[0]


HBM_SPEC = pl.BlockSpec(memory_space=pltpu.HBM)
SEM_SPEC = pl.BlockSpec(memory_space=pltpu.SEMAPHORE)
DATAFLOW = pltpu.SideEffectType.DATAFLOW_SIDE_EFFECTING
TOKEN_SHAPE = (1, D_MODEL)
N_PEERS = 7


def _peers():
    x, y, c = lax.axis_index("x"), lax.axis_index("y"), lax.axis_index("c")
    flip = lambda v, f: 1 - v if f else v
    return [(flip(x, k & 4), flip(y, k & 2), flip(c, k & 1)) for k in range(1, N_PEERS + 1)]


def _piece_shape(shape):
    return (shape[-2] // 2, shape[2] if len(shape) == 3 else shape[1] // N_CHIPS)


def _device_piece(ref, chip, core):
    kh, n4 = _piece_shape(ref.shape)
    if len(ref.shape) == 3:
        return ref.at[chip, pl.ds(core * kh, kh), :]
    return ref.at[pl.ds(core * kh, kh), pl.ds(chip * n4, n4)]


def _exchange_copies(partials, lands, send, recv):
    return [pltpu.make_async_remote_copy(
        src_ref=_device_piece(partials[t], 2 * px + py, pc), dst_ref=lands[t].at[k], send_sem=send.at[t * N_PEERS + k],
        recv_sem=recv.at[t * N_PEERS + k], device_id=(px, py, pc), device_id_type=MESH)
        for t in range(len(partials)) for k, (px, py, pc) in enumerate(_peers())]


def _gather_copies(shards, lands, send, recv):
    x, y, c, me, chips = _place()
    return [pltpu.make_async_remote_copy(
        src_ref=shards[t], dst_ref=lands[t].at[me], send_sem=send.at[t * 3 + j], recv_sem=recv.at[t * 3 + j],
        device_id=(*chip, c), device_id_type=MESH)
        for t in range(len(shards)) for j, chip in enumerate(chips)]


def _gather_half_copies(shards, lands, send, recv):
    x, y, c, me, chips = _place()
    return [pltpu.make_async_remote_copy(
        src_ref=_rows(shards[t], c * (shards[t].shape[0] // 2), shards[t].shape[0] // 2), dst_ref=lands[t].at[j],
        send_sem=send.at[t * 3 + j], recv_sem=recv.at[t * 3 + j], device_id=(*chip, c), device_id_type=MESH)
        for t in range(len(shards)) for j, chip in enumerate(chips)]


def _split_start(name, copies, per_tensor, srcs, land_shapes):
    nt = len(srcs)
    lands = [lax.empty(s, BF16) for s in land_shapes]
    nsem = nt * per_tensor

    def body(*refs):
        send, recv = refs[2 * nt], refs[2 * nt + 1]
        for cp in copies(refs[:nt], refs[nt:2 * nt], send, recv):
            cp.start()
        refs[-1][...] = jnp.zeros(TOKEN_SHAPE, F32)

    hbm = lambda a: pltpu.with_memory_space_constraint(a, pltpu.HBM)
    outs = pl.pallas_call(
        body, name=name,
        out_shape=[pltpu.SemaphoreType.DMA((nsem,)), pltpu.SemaphoreType.DMA((nsem,))]
        + [pltpu.HBM(s.shape, s.dtype) for s in srcs] + [pltpu.HBM(l.shape, l.dtype) for l in lands]
        + [jax.ShapeDtypeStruct(TOKEN_SHAPE, F32)],
        in_specs=[HBM_SPEC] * (2 * nt), out_specs=[SEM_SPEC, SEM_SPEC] + [HBM_SPEC] * (2 * nt) + [VMEM_SPEC],
        input_output_aliases={i: 2 + i for i in range(2 * nt)},
        compiler_params=pltpu.CompilerParams(has_side_effects=DATAFLOW))(*[hbm(a) for a in list(srcs) + lands])
    return outs[0], outs[1], outs[2:2 + nt], outs[2 + nt:2 + 2 * nt], outs[-1]


def _split_wait(name, copies, send, recv, srcs, lands, after):
    nt = len(srcs)

    def body(*refs):
        for cp in copies(refs[:nt], refs[nt:2 * nt], refs[2 * nt], refs[2 * nt + 1]):
            cp.wait_send()
            cp.wait_recv()

    outs = pl.pallas_call(
        body, name=name,
        out_shape=[pltpu.HBM(s.shape, s.dtype) for s in srcs] + [pltpu.HBM(l.shape, l.dtype) for l in lands],
        in_specs=[HBM_SPEC] * (2 * nt) + [SEM_SPEC, SEM_SPEC, pl.BlockSpec(memory_space=pl.ANY)],
        out_specs=[HBM_SPEC] * (2 * nt), input_output_aliases={i: i for i in range(2 * nt)},
        compiler_params=pltpu.CompilerParams(has_side_effects=DATAFLOW))(*srcs, *lands, send, recv, after)
    return outs[:nt], outs[nt:]


def _device_sum(name, partials, lands):
    nt = len(partials)

    def body(*refs):
        ins, slots, outs, owns = refs[:nt], refs[nt:2 * nt], refs[2 * nt:3 * nt], refs[3 * nt:4 * nt]
        send, recv, loc = refs[4 * nt:]
        x, y, c, me, chips = _place()
        sibling = (x, y, 1 - c)
        loads = [pltpu.make_async_copy(_device_piece(ins[t], me, c), owns[t], loc.at[t]) for t in range(nt)]
        for cp in loads:
            cp.start()
        handed = []
        for t in range(nt):
            kh = owns[t].shape[0]
            loads[t].wait()

            def add(r0, ck, own=owns[t], slot=slots[t], dst=outs[t], kh=kh):
                rows = pl.ds(r0, ck)
                acc = own[rows, :].astype(F32)
                for k in range(N_PEERS):
                    acc = acc + slot[k, rows, :].astype(F32)
                dst[pl.ds(pl.multiple_of(c * kh + r0, ck), ck), :] = acc

            _for_row_chunks(kh, add)
            rc = pltpu.make_async_remote_copy(
                src_ref=_rows(outs[t], c * kh, kh), dst_ref=_rows(outs[t], c * kh, kh), send_sem=send.at[t],
                recv_sem=recv.at[t], device_id=sibling, device_id_type=MESH)
            rc.start()
            handed.append(rc)
        for t in range(nt):
            kh = owns[t].shape[0]
            other = _rows(outs[t], (1 - c) * kh, kh)
            pltpu.make_async_remote_copy(
                src_ref=other, dst_ref=other, send_sem=send.at[t], recv_sem=recv.at[t],
                device_id=sibling, device_id_type=MESH).wait_recv()
        for rc in handed:
            rc.wait_send()

    pieces = [_piece_shape(p.shape) for p in partials]
    return pl.pallas_call(
        body, in_specs=[pl.BlockSpec(memory_space=pl.ANY)] * nt + [VMEM_SPEC] * nt, out_specs=[VMEM_SPEC] * nt,
        out_shape=[jax.ShapeDtypeStruct((2 * kh, n4), F32) for kh, n4 in pieces],
        scratch_shapes=[pltpu.VMEM(p, BF16) for p in pieces]
        + [pltpu.SemaphoreType.DMA((nt,)), pltpu.SemaphoreType.DMA((nt,)), pltpu.SemaphoreType.DMA((nt,))],
        compiler_params=pltpu.CompilerParams(vmem_limit_bytes=VMEM_LIMIT), name=name)(*partials, *lands)


VEC_SHAPE = (8, D_MODEL + LANES)
VEC_SLOTS = dict(norm1_g=(slice(0, 1), slice(0, D_MODEL)), norm2_g=(slice(1, 2), slice(0, D_MODEL)),
                 final_g=(slice(2, 3), slice(0, D_MODEL)), sgu_ln_g=(slice(3, 4), slice(0, SGU_W)),
                 sgu_ln_b=(slice(3, 4), slice(SGU_W, 2 * SGU_W)), b_spatial=(slice(0, 8), slice(D_MODEL, D_MODEL + LANES)),
                 loss=(slice(4, 5), slice(0, LANES)))
VEC_PARAMS = ("norm1_g", "norm2_g", "final_g", "sgu_ln_g", "sgu_ln_b", "b_spatial")
SMALL_PARAMS = VEC_PARAMS + ("w_spatial",)
W_SPATIAL_2D = (SGU_GROUPS * SGU_CHUNK, SGU_CHUNK)


def _small_step(partials, w, m, v):
    def shape2d(name):
        if name == "w_spatial":
            return W_SPATIAL_2D
        rows, cols = VEC_SLOTS[name]
        return (rows.stop - rows.start, cols.stop - cols.start)

    g_names = VEC_PARAMS + ("loss", "w_spatial")
    ng, npar = len(g_names), len(SMALL_PARAMS)

    def pack(dst, parts):
        dst[...] = jnp.zeros(VEC_SHAPE, F32)
        for n, ref in parts.items():
            if n in VEC_SLOTS:
                dst[VEC_SLOTS[n]] = ref[...]

    def reduce_body(*refs):
        g_in = dict(zip(g_names, refs[:ng]))
        vec_out, ws_out, vec, vec_pair, vec_slot, ws_pair, ws_slot, send, recv = refs[ng:]
        x, y, c, me, chips = _place()
        sibling = (x, y, 1 - c)
        pack(vec, g_in)
        copies = []

        def allreduce(k0, src, pair, slot):
            first = pltpu.make_async_remote_copy(src_ref=src, dst_ref=pair, send_sem=send.at[k0], recv_sem=recv.at[k0],
                                                 device_id=sibling, device_id_type=MESH)
            first.start()
            first.wait_recv()
            slot[me] = src[...] + pair[...]
            arrivals = []
            for j, chip in enumerate(chips):
                theirs = 2 * chip[0] + chip[1]
                rc = pltpu.make_async_remote_copy(src_ref=slot.at[me], dst_ref=slot.at[me], send_sem=send.at[k0 + 1 + j],
                                                  recv_sem=recv.at[k0 + 1 + j], device_id=(*chip, c), device_id_type=MESH)
                rc.start()
                arrivals.append(pltpu.make_async_remote_copy(
                    src_ref=slot.at[theirs], dst_ref=slot.at[theirs], send_sem=send.at[k0 + 1 + j],
                    recv_sem=recv.at[k0 + 1 + j], device_id=(*chip, c), device_id_type=MESH))
                copies.append(rc)
            copies.append(first)
            return arrivals

        arrivals = allreduce(0, vec, vec_pair, vec_slot) + allreduce(4, g_in["w_spatial"], ws_pair, ws_slot)
        for a in arrivals:
            a.wait_recv()
        vec_out[...] = ((vec_slot[0] + vec_slot[1]) + vec_slot[2]) + vec_slot[3]

        def spatial(r0, ck):
            rows = pl.ds(r0, ck)
            ws_out[rows, :] = ((ws_slot[0, rows, :] + ws_slot[1, rows, :]) + ws_slot[2, rows, :]) + ws_slot[3, rows, :]

        _for_row_chunks(W_SPATIAL_2D[0], spatial)
        for rc in copies:
            rc.wait_send()

    g_vec, g_ws = pl.pallas_call(
        reduce_body, in_specs=[VMEM_SPEC] * ng, out_specs=[VMEM_SPEC] * 2,
        out_shape=[jax.ShapeDtypeStruct(VEC_SHAPE, F32), jax.ShapeDtypeStruct(W_SPATIAL_2D, F32)],
        scratch_shapes=[pltpu.VMEM(VEC_SHAPE, F32), pltpu.VMEM(VEC_SHAPE, F32), pltpu.VMEM((N_CHIPS,) + VEC_SHAPE, F32),
                        pltpu.VMEM(W_SPATIAL_2D, F32), pltpu.VMEM((N_CHIPS,) + W_SPATIAL_2D, F32),
                        pltpu.SemaphoreType.DMA((8,)), pltpu.SemaphoreType.DMA((8,))],
        name="small_params_allreduce")(*[partials[n].reshape(shape2d(n)) for n in g_names])

    def update_body(*refs):
        gv_ref, gw_ref = refs[:2]
        w_in, m_in, v_in = (dict(zip(SMALL_PARAMS, refs[2 + k * npar:2 + (k + 1) * npar])) for k in range(3))
        o0 = 2 + 3 * npar
        g_out = dict(zip(g_names, refs[o0:o0 + ng]))
        d_out, m_out, v_out = (dict(zip(SMALL_PARAMS, refs[o0 + ng + k * npar:o0 + ng + (k + 1) * npar])) for k in range(3))
        vw, vm, vv = refs[o0 + ng + 3 * npar:]
        pack(vw, w_in)
        pack(vm, m_in)
        pack(vv, v_in)
        d_vec, m_vec, v_vec = _adamw_math(gv_ref[...], vw[...], vm[...], vv[...])
        vw[...] = d_vec
        vm[...] = m_vec
        vv[...] = v_vec
        for n in VEC_PARAMS + ("loss",):
            g_out[n][...] = gv_ref[VEC_SLOTS[n]]
        for n in VEC_PARAMS:
            d_out[n][...] = vw[VEC_SLOTS[n]]
            m_out[n][...] = vm[VEC_SLOTS[n]]
            v_out[n][...] = vv[VEC_SLOTS[n]]

        def spatial(r0, ck):
            rows = pl.ds(r0, ck)
            g = gw_ref[rows, :]
            d_, m_, v_ = _adamw_math(g, w_in["w_spatial"][rows, :], m_in["w_spatial"][rows, :], v_in["w_spatial"][rows, :])
            g_out["w_spatial"][rows, :] = g
            d_out["w_spatial"][rows, :] = d_
            m_out["w_spatial"][rows, :] = m_
            v_out["w_spatial"][rows, :] = v_

        _for_row_chunks(W_SPATIAL_2D[0], spatial)

    ins = [g_vec, g_ws]
    for src in (w, m, v):
        ins += [src[n].reshape(shape2d(n)) for n in SMALL_PARAMS]
    out_shapes = [jax.ShapeDtypeStruct(shape2d(n), F32) for n in g_names + SMALL_PARAMS * 3]
    outs = pl.pallas_call(
        update_body, in_specs=[VMEM_SPEC] * len(ins), out_specs=[VMEM_SPEC] * len(out_shapes), out_shape=out_shapes,
        scratch_shapes=[pltpu.VMEM(VEC_SHAPE, F32)] * 3, name="small_params_update")(*ins)
    grads = dict(zip(g_names, outs[:ng]))
    rest = [dict(zip(SMALL_PARAMS, outs[ng + k * npar:ng + (k + 1) * npar])) for k in range(3)]
    return grads, rest[0], rest[1], rest[2]


BIG = ("w_in", "w_proj_attn", "w_proj_sgu", "w_out", "w_ffn_gate", "w_ffn_up", "w_ffn_down")
COMM_GROUPS = (("w_in",), ("w_proj_attn", "w_proj_sgu", "w_out", "w_ffn_gate", "w_ffn_up", "w_ffn_down"))
WEIGHTS = ("norm1_g", "w_in", "sgu_ln_g", "sgu_ln_b", "w_spatial", "b_spatial", "w_proj_attn", "w_proj_sgu", "w_out",
           "norm2_g", "w_ffn_gate", "w_ffn_up", "w_ffn_down", "final_g")


def _cols_from_chips(g):
    return jnp.transpose(g, (1, 0, 2)).reshape(g.shape[1], N_CHIPS * g.shape[2])


def kernel(x, positions, norm1_g, w_in, sgu_ln_g, sgu_ln_b, w_spatial, b_spatial, w_proj_attn, w_proj_sgu, w_out, norm2_g, w_ffn_gate, w_ffn_up, w_ffn_down, final_g, loss_target, m_norm1_g, m_w_in, m_sgu_ln_g, m_sgu_ln_b, m_w_spatial, m_b_spatial, m_w_proj_attn, m_w_proj_sgu, m_w_out, m_norm2_g, m_w_ffn_gate, m_w_ffn_up, m_w_ffn_down, m_final_g, v_norm1_g, v_w_in, v_sgu_ln_g, v_sgu_ln_b, v_w_spatial, v_b_spatial, v_w_proj_attn, v_w_proj_sgu, v_w_out, v_norm2_g, v_w_ffn_gate, v_w_ffn_up, v_w_ffn_down, v_final_g):
    w = dict(norm1_g=norm1_g, w_in=w_in, sgu_ln_g=sgu_ln_g, sgu_ln_b=sgu_ln_b, w_spatial=w_spatial, b_spatial=b_spatial,
             w_proj_attn=w_proj_attn, w_proj_sgu=w_proj_sgu, w_out=w_out, norm2_g=norm2_g, w_ffn_gate=w_ffn_gate,
             w_ffn_up=w_ffn_up, w_ffn_down=w_ffn_down, final_g=final_g)
    m = dict(norm1_g=m_norm1_g, w_in=m_w_in, sgu_ln_g=m_sgu_ln_g, sgu_ln_b=m_sgu_ln_b, w_spatial=m_w_spatial,
             b_spatial=m_b_spatial, w_proj_attn=m_w_proj_attn, w_proj_sgu=m_w_proj_sgu, w_out=m_w_out, norm2_g=m_norm2_g,
             w_ffn_gate=m_w_ffn_gate, w_ffn_up=m_w_ffn_up, w_ffn_down=m_w_ffn_down, final_g=m_final_g)
    v = dict(norm1_g=v_norm1_g, w_in=v_w_in, sgu_ln_g=v_sgu_ln_g, sgu_ln_b=v_sgu_ln_b, w_spatial=v_w_spatial,
             b_spatial=v_b_spatial, w_proj_attn=v_w_proj_attn, w_proj_sgu=v_w_proj_sgu, w_out=v_w_out, norm2_g=v_norm2_g,
             w_ffn_gate=v_w_ffn_gate, w_ffn_up=v_w_ffn_up, w_ffn_down=v_w_ffn_down, final_g=v_final_g)
    t = x.shape[1]

    shards = {n: _ew(f"cast_{n}", lambda a: (a,), [w[n][0]], [BF16])[0] for n in BIG}
    late = COMM_GROUPS[1]
    k_in, n_in = shards["w_in"].shape
    *first, token = _split_start("gather_start_0", _gather_half_copies, 3, [shards["w_in"]], [(3, k_in // 2, n_in)])
    pending = {}

    def first_weight(after):
        srcs, filled = _split_wait("gather_wait_0", _gather_half_copies, *first, after)
        gath_in, late_shards = lax.optimization_barrier(
            (_gather_finish("gather_finish_0", srcs[0], filled[0]), [shards[n] for n in late]))
        *pending["late"], _ = _split_start(
            "gather_start_1", _gather_copies, 3, late_shards, [(N_CHIPS,) + s.shape for s in late_shards])
        return _cols_from_chips(gath_in)

    def late_weights(after):
        srcs, filled = _split_wait("gather_wait_1", _gather_copies, *pending["late"], after)
        me = 2 * lax.axis_index("x") + lax.axis_index("y")
        gath = {n: lax.dynamic_update_slice(f, s[None], (me, 0, 0)) for n, f, s in zip(late, filled, srcs)}
        return (_cols_from_chips(gath["w_proj_attn"]), _cols_from_chips(gath["w_proj_sgu"]),
                gath["w_out"].reshape(D_MODEL, D_MODEL), gath["w_ffn_gate"], gath["w_ffn_up"], gath["w_ffn_down"])

    exchanges = {}

    def on_grads(i, partials):
        if "w_out" in partials:
            partials["w_out"] = partials["w_out"].reshape(N_CHIPS, D_MODEL // N_CHIPS, D_MODEL)
        parts = [partials[n] for n in COMM_GROUPS[i]]
        *exchanges[i], started = _split_start(
            f"rs_exchange_start_{i}", _exchange_copies, N_PEERS, parts, [(N_PEERS,) + _piece_shape(p.shape) for p in parts])
        return started

    dx, _, small = _local_step(
        x[0], positions.reshape(t, 1), loss_target[0], norm1_g + token, sgu_ln_g, sgu_ln_b, w_spatial[0], b_spatial[0],
        norm2_g, final_g.reshape(1, D_MODEL), first_weight, late_weights, on_grads=on_grads)

    grads = {}
    for i in (1, 0):
        parts, filled = _split_wait(f"rs_exchange_wait_{i}", _exchange_copies, *exchanges[i], dx)
        grads.update(zip(COMM_GROUPS[i], _device_sum(f"rs_device_sum_{i}", parts, filled)))

    delta, new_m, new_v = {}, {}, {}
    for n in BIG:
        shp = w[n].shape
        g_, d_, m_, v_ = _adamw(f"adamw_{n}", grads[n], w[n][0], m[n][0], v[n][0])
        grads[n], delta[n], new_m[n], new_v[n] = g_.reshape(shp), d_.reshape(shp), m_.reshape(shp), v_.reshape(shp)

    g_s, d_s, m_s, v_s = _small_step(small, w, m, v)
    loss = g_s["loss"][0, 0]
    for n in SMALL_PARAMS:
        shp = w[n].shape
        grads[n], delta[n], new_m[n], new_v[n] = (a[n].reshape(shp) for a in (g_s, d_s, m_s, v_s))

    return (loss, dx.reshape(x.shape), *[grads[n] for n in WEIGHTS], *[delta[n] for n in WEIGHTS],
            *[new_m[n] for n in WEIGHTS], *[new_v[n] for n in WEIGHTS])
```

```python
import functools

import numpy as np
import jax
import jax.numpy as jnp
from jax import lax
from jax.experimental import pallas as pl
from jax.experimental.pallas import tpu as pltpu

F32, BF16 = jnp.float32, jnp.bfloat16
MESH = pl.DeviceIdType.MESH

D_MODEL = 1024
HEAD_DIM = 64
ATTN_W = 512
DILATIONS = (1, 4, 16)
BLK = 128
ROPE_DIM = 16
ROPE_THETA = 500000.0
SGU_W = 512
SGU_CHUNK = 128
SGU_GROUPS = 8
D_FF = 2816
N_CHIPS = 4
FF_SHARD = D_FF // N_CHIPS
IN_COLS = 7680
EPS = 1e-6
NEG = -1e30
LANES = 128
VMEM_LIMIT = 52 * 1024 * 1024

ADAM_LR, ADAM_B1, ADAM_B2, ADAM_EPS, ADAM_WD, ADAM_STEP = 0.001, 0.9, 0.999, 1e-08, 0.01, 10

QKV_BLOCKS = 9


def _w_in_block(part, g):
    return part * len(DILATIONS) + g


def _cparams(ngrid):
    return pltpu.CompilerParams(dimension_semantics=("arbitrary",) * ngrid, vmem_limit_bytes=VMEM_LIMIT)


def _full(shape):
    return pl.BlockSpec(shape, lambda *_: (0,) * len(shape))


def _resident(shape):
    return pl.BlockSpec(shape, lambda *_: (0,) * len(shape), pipeline_mode=pl.Buffered(1))


NN = ((1,), (0,))
NT = ((1,), (1,))
TN = ((0,), (0,))


def _mm(name, grid, pairs, dims, acc_shape, epi, *, extras=(), outs=(), reds=(), aliases=None):
    nk = grid[-1]
    npair, nex, nout, nred = len(pairs), len(extras), len(outs), len(reds)

    def body(*refs):
        a_refs = refs[:npair]
        b_refs = refs[npair:2 * npair]
        p0 = 2 * npair
        e_refs = refs[p0:p0 + nex]
        o_refs = refs[p0 + nex:p0 + nex + nout]
        r_refs = refs[p0 + nex + nout:p0 + nex + nout + nred]
        ids = [pl.program_id(a) for a in range(len(grid))]
        k = ids[-1]
        if nred:
            first = ids[0] == 0
            for v in ids[1:]:
                first = first & (v == 0)

            @pl.when(first)
            def _():
                for r in r_refs:
                    r[...] = jnp.zeros(r.shape, r.dtype)

        part = None
        for a_ref, b_ref in zip(a_refs, b_refs):
            d = lax.dot_general(a_ref[...], b_ref[...], (dims, ((), ())), preferred_element_type=F32)
            part = d if part is None else part + d
        if nk == 1:
            epi(part, e_refs, o_refs, r_refs, ids)
        else:
            acc_ref = refs[-1]

            @pl.when(k == 0)
            def _():
                acc_ref[...] = part

            @pl.when(k > 0)
            def _():
                acc_ref[...] += part

            @pl.when(k == nk - 1)
            def _():
                epi(acc_ref[...], e_refs, o_refs, r_refs, ids)

    in_specs = [p[1] for p in pairs] + [p[3] for p in pairs] + [e[1] for e in extras]
    args = [p[0] for p in pairs] + [p[2] for p in pairs] + [e[0] for e in extras]
    out_shape = [jax.ShapeDtypeStruct(o[0], o[1]) for o in outs] + [jax.ShapeDtypeStruct(r, F32) for r in reds]
    out_specs = [o[2] for o in outs] + [_full(r) for r in reds]
    scratch_shapes = [pltpu.VMEM(acc_shape, F32)] if nk > 1 else []
    return pl.pallas_call(
        body, grid=grid, in_specs=in_specs, out_specs=out_specs, out_shape=out_shape, scratch_shapes=scratch_shapes,
        input_output_aliases=aliases or {}, compiler_params=_cparams(len(grid)), name=name)(*args)


def _rope(v, cos_t, sin_t):
    half = ROPE_DIM // 2
    first = (lax.broadcasted_iota(jnp.int32, cos_t.shape, 1) % HEAD_DIM) < half
    outs = []
    for cs in range(v.shape[1] // LANES):
        x = v[:, cs * LANES:(cs + 1) * LANES]
        partner = jnp.where(first, pltpu.roll(x, LANES - half, axis=1), pltpu.roll(x, half, axis=1))
        outs.append(x * cos_t + partner * sin_t)
    return outs[0] if len(outs) == 1 else jnp.concatenate(outs, axis=1)


def _spread_heads(v2, upper):
    other = pltpu.roll(v2, HEAD_DIM, axis=1)
    h0 = jnp.where(upper, other, v2)
    h1 = jnp.where(upper, v2, other)
    return jnp.concatenate([jnp.concatenate([h0, h0], axis=1), jnp.concatenate([h1, h1], axis=1)], axis=0)


def _sigmoid(v):
    return 0.5 * jnp.tanh(0.5 * v) + 0.5


def _rms_stats(v):
    r = lax.rsqrt(jnp.mean(v * v, axis=-1, keepdims=True) + EPS)
    return v * r, r


def _rms_bwd(dy, xhat, r, g):
    dxh = dy * g
    return r * (dxh - xhat * jnp.mean(dxh * xhat, axis=-1, keepdims=True))


def _head_sum_matrix():
    idx = np.arange(ATTN_W) // HEAD_DIM
    return jnp.asarray((idx[:, None] == idx[None, :]).astype(np.float32), dtype=BF16)


def _group_sum(v, e):
    hi = v.astype(BF16)
    lo = (v - hi.astype(F32)).astype(BF16)
    return jnp.dot(hi, e, preferred_element_type=F32) + jnp.dot(lo, e, preferred_element_type=F32)


TILE = 512


def _to_slabs(slab_ref, v):
    for cs in range(slab_ref.shape[0]):
        slab_ref[cs] = v[:, cs * LANES:(cs + 1) * LANES]


def _from_slabs(slab_ref):
    return jnp.concatenate([slab_ref[cs] for cs in range(slab_ref.shape[0])], axis=1)


def _class_rows(slab_ref, r, dil):
    n = slab_ref.shape[1] // dil
    return jnp.concatenate([slab_ref.at[cs][pl.ds(r, n, stride=dil), :] for cs in range(slab_ref.shape[0])], axis=1)


def _put_class_rows(slab_ref, r, dil, v):
    n = slab_ref.shape[1] // dil
    for cs in range(slab_ref.shape[0]):
        slab_ref.at[cs][pl.ds(r, n, stride=dil), :] = v[:, cs * LANES:(cs + 1) * LANES]


def _natural_from_group(slab_ref, grp_ref):
    dil = grp_ref.shape[0]
    for r in range(dil):
        _put_class_rows(slab_ref, r, dil, grp_ref[r].astype(F32))
    return _from_slabs(slab_ref)


def _group_from_natural(slab_ref, grp_ref, v):
    dil = grp_ref.shape[0]
    _to_slabs(slab_ref, v)
    for r in range(dil):
        grp_ref[r] = _class_rows(slab_ref, r, dil).astype(grp_ref.dtype)


def _group_spec(dil, tile, width):
    return pl.BlockSpec((dil, tile // dil, width), lambda i, *_: (0, i, 0))


def _slabs(tile, width):
    return pltpu.VMEM((width // LANES, tile, LANES), F32)


def _rope_consts():
    lane = np.arange(LANES) % HEAD_DIM
    fi = lane % (ROPE_DIM // 2)
    invf = np.where(lane < ROPE_DIM, ROPE_THETA ** (-(2.0 * fi) / ROPE_DIM), 0.0)
    sgn = np.where(lane < ROPE_DIM // 2, -1.0, np.where(lane < ROPE_DIM, 1.0, 0.0))
    return (jnp.asarray(invf.astype(np.float32)).reshape(1, LANES), jnp.asarray(sgn.astype(np.float32)).reshape(1, LANES))


def _rope_tables(pos_col):
    t = pos_col.shape[0]
    tile = min(t, TILE)
    invf, sgn = _rope_consts()

    def body(p_ref, f_ref, s_ref, c0, s0, c1, s1, c2, s2, slab_c, slab_s):
        ang = p_ref[...].astype(F32) * f_ref[...]
        cos, sin = jnp.cos(ang), jnp.sin(ang) * s_ref[...]
        c0[...] = cos
        s0[...] = sin
        _group_from_natural(slab_c, c1, cos)
        _group_from_natural(slab_s, s1, sin)
        for r in range(DILATIONS[2]):
            c2[r] = _class_rows(slab_c, r, DILATIONS[2])
            s2[r] = _class_rows(slab_s, r, DILATIONS[2])

    nat = pl.BlockSpec((tile, LANES), lambda i: (i, 0))
    specs, shapes = [nat, nat], [(t, LANES)] * 2
    for d in DILATIONS[1:]:
        specs += [_group_spec(d, tile, LANES)] * 2
        shapes += [(d, t // d, LANES)] * 2
    outs = pl.pallas_call(
        body, grid=(t // tile,),
        in_specs=[pl.BlockSpec((tile, 1), lambda i: (i, 0)), _full((1, LANES)), _full((1, LANES))],
        out_specs=specs, out_shape=[jax.ShapeDtypeStruct(s, F32) for s in shapes],
        scratch_shapes=[_slabs(tile, LANES)] * 2,
        compiler_params=_cparams(1), name="rope_tables")(pos_col, invf, sgn)
    return [(outs[2 * g].reshape(t, LANES), outs[2 * g + 1].reshape(t, LANES)) for g in range(len(DILATIONS))]


def _norm_fwd(x, g):
    t = x.shape[0]
    tile = min(t, TILE)

    def body(x_ref, g_ref, h0_ref, h1_ref, h2_ref, slab):
        xhat, _ = _rms_stats(x_ref[...])
        hn = xhat * g_ref[...]
        h0_ref[...] = hn.astype(BF16)
        _group_from_natural(slab, h1_ref, hn)
        for r in range(DILATIONS[2]):
            h2_ref[r] = _class_rows(slab, r, DILATIONS[2]).astype(BF16)

    nat = pl.BlockSpec((tile, D_MODEL), lambda i: (i, 0))
    return pl.pallas_call(
        body, grid=(t // tile,),
        in_specs=[nat, _full((1, D_MODEL))],
        out_specs=[nat] + [_group_spec(d, tile, D_MODEL) for d in DILATIONS[1:]],
        out_shape=[jax.ShapeDtypeStruct((t, D_MODEL), BF16)]
        + [jax.ShapeDtypeStruct((d, t // d, D_MODEL), BF16) for d in DILATIONS[1:]],
        scratch_shapes=[_slabs(tile, D_MODEL)],
        compiler_params=_cparams(1), name="norm1_fwd")(x, g)


GU_COLS = 3072
GROUP_COLS = 1536
GU_HALF = GU_COLS // 2


def _w_in_spec(width, block):
    return pl.BlockSpec((D_MODEL, width), lambda i: (0, block), pipeline_mode=pl.Buffered(1))


def _gu_w_specs():
    first = QKV_BLOCKS * ATTN_W // GU_HALF
    return [_w_in_spec(GU_HALF, first), _w_in_spec(GU_HALF, first + 1)]


def _group_w_specs(g):
    return [_w_in_spec(ATTN_W, _w_in_block(part, g)) for part in range(3)]


def _in_proj(hs, w_in, tables):
    t = hs[0].shape[0]
    tm = min(t, 1024)

    def body_gu(h_ref, w0_ref, w1_ref, o_ref):
        h = h_ref[...]
        o_ref[:, 0:GU_HALF] = jnp.dot(h, w0_ref[...], preferred_element_type=F32).astype(BF16)
        o_ref[:, GU_HALF:] = jnp.dot(h, w1_ref[...], preferred_element_type=F32).astype(BF16)

    gu = _token_call("in_proj_gates_uv", body_gu, t, tm,
                     [(hs[0], _rows_spec(tm, D_MODEL))] + [(w_in, s) for s in _gu_w_specs()],
                     [((t, GU_COLS), BF16, _rows_spec(tm, GU_COLS))])[0]

    qkvs = []
    for g in range(len(DILATIONS)):

        def body_qkv(h_ref, wq_ref, wk_ref, wv_ref, cos_ref, sin_ref, o_ref):
            h = h_ref[...]
            cos_w, sin_w = cos_ref[...], sin_ref[...]
            q = jnp.dot(h, wq_ref[...], preferred_element_type=F32)
            o_ref[:, 0:ATTN_W] = (_rope(q, cos_w, sin_w) * HEAD_DIM ** -0.5).astype(BF16)
            k = jnp.dot(h, wk_ref[...], preferred_element_type=F32)
            o_ref[:, ATTN_W:2 * ATTN_W] = _rope(k, cos_w, sin_w).astype(BF16)
            o_ref[:, 2 * ATTN_W:] = jnp.dot(h, wv_ref[...], preferred_element_type=F32).astype(BF16)

        cos_t, sin_t = tables[g]
        qkvs.append(_token_call(
            f"in_proj_qkv_g{g}", body_qkv, t, tm,
            [(hs[g].reshape(t, D_MODEL), _rows_spec(tm, D_MODEL))] + [(w_in, s) for s in _group_w_specs(g)]
            + [(cos_t, _rows_spec(tm, LANES)), (sin_t, _rows_spec(tm, LANES))],
            [((t, GROUP_COLS), BF16, _rows_spec(tm, GROUP_COLS))])[0])
    return gu, qkvs


def _attn_masks(n):
    row = lax.broadcasted_iota(jnp.int32, (2 * BLK, 2 * BLK), 0) % BLK
    col = lax.broadcasted_iota(jnp.int32, (2 * BLK, 2 * BLK), 1)
    diff = BLK + row - col
    valid = (diff >= 0) & (diff <= BLK) & ((col >= BLK) | (n > 0))
    upper = lax.broadcasted_iota(jnp.int32, (BLK, LANES), 1) >= HEAD_DIM
    return valid, upper


def _stack_heads(v2, upper):
    zero = jnp.zeros_like(v2)
    return jnp.concatenate([jnp.where(upper, zero, v2), jnp.where(upper, v2, zero)], axis=0)


def _unstack_heads(v, upper):
    return jnp.where(upper, v[BLK:], v[:BLK])


def _attn_fwd(qkv, g, dil):
    t = qkv.shape[0]
    length = t // dil
    nb = length // BLK
    view = qkv.reshape(dil, length, GROUP_COLS)

    def body(q_ref, kc_ref, kp_ref, vc_ref, vp_ref, o_ref, l_ref):
        n = pl.program_id(1)
        valid, upper = _attn_masks(n)
        for p in range(ATTN_W // LANES):
            sl = slice(p * LANES, (p + 1) * LANES)
            qs = _stack_heads(q_ref[:, sl], upper)
            k2 = jnp.concatenate([kp_ref[:, sl], kc_ref[:, sl]], axis=0)
            v2 = jnp.concatenate([vp_ref[:, sl], vc_ref[:, sl]], axis=0)
            s = lax.dot_general(qs, k2, (NT, ((), ())), preferred_element_type=F32)
            s = jnp.where(valid, s, NEG)
            m = jnp.max(s, axis=1, keepdims=True)
            pe = jnp.exp(s - m)
            den = jnp.sum(pe, axis=1, keepdims=True)
            o = jnp.dot(pe.astype(BF16), v2, preferred_element_type=F32) / den
            lse = jnp.broadcast_to(m + jnp.log(den), (2 * BLK, LANES))
            o_ref[:, sl] = _unstack_heads(o, upper).astype(BF16)
            l_ref[:, sl] = _unstack_heads(lse, upper)

    cur = lambda part: pl.BlockSpec((None, BLK, ATTN_W), lambda r, n: (r, n, part))
    prev = lambda part: pl.BlockSpec((None, BLK, ATTN_W), lambda r, n: (r, jnp.maximum(n - 1, 0), part))
    out_spec = pl.BlockSpec((None, BLK, ATTN_W), lambda r, n: (r, n, 0))
    return pl.pallas_call(
        body, grid=(dil, nb),
        in_specs=[cur(0), cur(1), prev(1), cur(2), prev(2)],
        out_specs=[out_spec, out_spec],
        out_shape=[jax.ShapeDtypeStruct((dil, length, ATTN_W), BF16), jax.ShapeDtypeStruct((dil, length, ATTN_W), F32)],
        compiler_params=_cparams(2), name=f"attn_fwd_g{g}")(view, view, view, view, view)


def _alphas(l0, l1, l2):
    m = jnp.maximum(jnp.maximum(l0, l1), l2)
    e0, e1, e2 = jnp.exp(l0 - m), jnp.exp(l1 - m), jnp.exp(l2 - m)
    inv = 1.0 / (e0 + e1 + e2)
    return e0 * inv, e1 * inv, e2 * inv


def _natural_group_values(o_refs, l_refs, slabs):
    os_ = [o_refs[0][0].astype(F32)] + [_natural_from_group(slabs[2 * g - 2], o_refs[g]) for g in (1, 2)]
    ls_ = [l_refs[0][0]] + [_natural_from_group(slabs[2 * g - 1], l_refs[g]) for g in (1, 2)]
    return os_, ls_


def _combine_fwd(os_, ls_):
    t = os_[0].shape[1]
    tile = min(t, TILE)

    def body(o0, o1, o2, l0, l1, l2, a_ref, *slabs):
        ov, lv = _natural_group_values((o0, o1, o2), (l0, l1, l2), slabs)
        a0, a1, a2 = _alphas(*lv)
        a_ref[...] = (a0 * ov[0] + a1 * ov[1] + a2 * ov[2]).astype(BF16)

    specs = [_group_spec(d, tile, ATTN_W) for d in DILATIONS]
    return pl.pallas_call(
        body, grid=(t // tile,), in_specs=specs * 2, out_specs=pl.BlockSpec((tile, ATTN_W), lambda i: (i, 0)),
        out_shape=jax.ShapeDtypeStruct((t, ATTN_W), BF16),
        scratch_shapes=[_slabs(tile, ATTN_W)] * 4,
        compiler_params=_cparams(1), name="combine_fwd")(*os_, *ls_)


def _combine_bwd(dattn, os_, ls_):
    t = dattn.shape[0]
    tile = min(t, TILE)
    e = _head_sum_matrix()

    def body(d_ref, o0, o1, o2, l0, l1, l2, e_ref, do0, do1, do2, c0, c1, c2, *slabs):
        ov, lv = _natural_group_values((o0, o1, o2), (l0, l1, l2), slabs)
        alphas = _alphas(*lv)
        d = d_ref[...]
        attn = alphas[0] * ov[0] + alphas[1] * ov[1] + alphas[2] * ov[2]
        s = _group_sum(d * attn, e_ref[...])
        do0[0] = (alphas[0] * d).astype(BF16)
        c0[0] = -alphas[0] * s
        for g, do_ref, c_ref in ((1, do1, c1), (2, do2, c2)):
            _group_from_natural(slabs[2 * g - 2], do_ref, alphas[g] * d)
            _group_from_natural(slabs[2 * g - 1], c_ref, -alphas[g] * s)

    specs = [_group_spec(d, tile, ATTN_W) for d in DILATIONS]
    shapes = [(d, t // d, ATTN_W) for d in DILATIONS]
    outs = pl.pallas_call(
        body, grid=(t // tile,),
        in_specs=[pl.BlockSpec((tile, ATTN_W), lambda i: (i, 0))] + specs * 2 + [_full((ATTN_W, ATTN_W))],
        out_specs=specs * 2,
        out_shape=[jax.ShapeDtypeStruct(s, BF16) for s in shapes] + [jax.ShapeDtypeStruct(s, F32) for s in shapes],
        scratch_shapes=[_slabs(tile, ATTN_W)] * 4,
        compiler_params=_cparams(1), name="combine_bwd")(dattn, *os_, *ls_, e)
    return outs[:3], outs[3:]


def _attn_bwd(qkv, do, cc, lse, cos_t, sin_t, g, dil):
    t = qkv.shape[0]
    length = t // dil
    nb = length // BLK
    qkv_v = qkv.reshape(dil, length, GROUP_COLS)
    cos_v, sin_v = (a.reshape(dil, length, LANES) for a in (cos_t, sin_t))
    scale = HEAD_DIM ** -0.5

    def body(q_ref, kc_ref, kp_ref, vc_ref, vp_ref, do_ref, c_ref, l_ref, cosc, sinc, cosp, sinp,
             out_ref, dq_s, dk_s, dv_s):
        n = pl.program_id(1)
        valid, upper = _attn_masks(n)

        @pl.when(n < nb)
        def _():
            cos_c, sin_c = cosc[...], -sinc[...]
            cos_p, sin_p = cosp[...], -sinp[...]
            dq_parts, dkp_parts, dkc_parts, dvp_parts, dvc_parts = [], [], [], [], []
            for p in range(ATTN_W // LANES):
                sl = slice(p * LANES, (p + 1) * LANES)
                qs = _stack_heads(q_ref[:, sl], upper)
                dos = _stack_heads(do_ref[:, sl], upper)
                k2 = jnp.concatenate([kp_ref[:, sl], kc_ref[:, sl]], axis=0)
                v2 = jnp.concatenate([vp_ref[:, sl], vc_ref[:, sl]], axis=0)
                l_col = _spread_heads(l_ref[:, sl], upper)
                c_col = _spread_heads(c_ref[:, sl], upper)
                s = lax.dot_general(qs, k2, (NT, ((), ())), preferred_element_type=F32)
                pe = jnp.exp(jnp.where(valid, s, NEG) - l_col)
                dpv = lax.dot_general(dos, v2, (NT, ((), ())), preferred_element_type=F32)
                ds = (pe * (dpv + c_col)).astype(BF16)
                dq2 = _unstack_heads(jnp.dot(ds, k2, preferred_element_type=F32), upper)
                dk2 = lax.dot_general(ds, qs, (TN, ((), ())), preferred_element_type=F32)
                dv2 = lax.dot_general(pe.astype(BF16), dos, (TN, ((), ())), preferred_element_type=F32)
                dq_parts.append(dq2)
                dkp_parts.append(dk2[:BLK])
                dkc_parts.append(dk2[BLK:])
                dvp_parts.append(dv2[:BLK])
                dvc_parts.append(dv2[BLK:])
            dq = _rope(jnp.concatenate(dq_parts, axis=1) * scale, cos_c, sin_c)
            dkc = _rope(jnp.concatenate(dkc_parts, axis=1), cos_c, sin_c)
            dkp = _rope(jnp.concatenate(dkp_parts, axis=1), cos_p, sin_p)
            dvp = jnp.concatenate(dvp_parts, axis=1)
            dvc = jnp.concatenate(dvc_parts, axis=1)

            @pl.when(n > 0)
            def _():
                out_ref[:, 0:ATTN_W] = dq_s[...].astype(BF16)
                out_ref[:, ATTN_W:2 * ATTN_W] = (dk_s[...] + dkp).astype(BF16)
                out_ref[:, 2 * ATTN_W:3 * ATTN_W] = (dv_s[...] + dvp).astype(BF16)

            dq_s[...] = dq
            dk_s[...] = dkc
            dv_s[...] = dvc

        @pl.when(n == nb)
        def _():
            out_ref[:, 0:ATTN_W] = dq_s[...].astype(BF16)
            out_ref[:, ATTN_W:2 * ATTN_W] = dk_s[...].astype(BF16)
            out_ref[:, 2 * ATTN_W:3 * ATTN_W] = dv_s[...].astype(BF16)

    nc = lambda n: jnp.minimum(n, nb - 1)
    npv = lambda n: jnp.maximum(jnp.minimum(n, nb - 1) - 1, 0)
    cur = lambda part: pl.BlockSpec((None, BLK, ATTN_W), lambda r, n: (r, nc(n), part))
    prev = lambda part: pl.BlockSpec((None, BLK, ATTN_W), lambda r, n: (r, npv(n), part))
    row = pl.BlockSpec((None, BLK, ATTN_W), lambda r, n: (r, nc(n), 0))
    tab_c = pl.BlockSpec((None, BLK, LANES), lambda r, n: (r, nc(n), 0))
    tab_p = pl.BlockSpec((None, BLK, LANES), lambda r, n: (r, npv(n), 0))
    out_spec = pl.BlockSpec((None, BLK, GROUP_COLS), lambda r, n: (r, jnp.maximum(n - 1, 0), 0))
    out = pl.pallas_call(
        body, grid=(dil, nb + 1),
        in_specs=[cur(0), cur(1), prev(1), cur(2), prev(2), row, row, row, tab_c, tab_c, tab_p, tab_p],
        out_specs=out_spec,
        out_shape=jax.ShapeDtypeStruct((dil, length, GROUP_COLS), BF16),
        scratch_shapes=[pltpu.VMEM((BLK, ATTN_W), F32)] * 3,
        compiler_params=_cparams(2), name=f"attn_bwd_g{g}")(
            qkv_v, qkv_v, qkv_v, qkv_v, qkv_v, do, cc, lse, cos_v, sin_v, cos_v, sin_v)
    return out.reshape(t, GROUP_COLS)


SQRT_HALF = 0.7071067811865476
INV_SQRT_2PI = 0.3989422804014327


def _sgu_core(uv, g, b, w_ref, bias):
    cdf = 0.5 * (1.0 + lax.erf(uv * SQRT_HALF))
    z = uv * cdf
    u, v = z[:, :SGU_W], z[:, SGU_W:]
    mu = jnp.mean(v, axis=1, keepdims=True)
    xc = v - mu
    rs = lax.rsqrt(jnp.mean(xc * xc, axis=1, keepdims=True) + EPS)
    xhat = xc * rs
    vn = xhat * g + b
    row = lax.broadcasted_iota(jnp.int32, (SGU_CHUNK, SGU_CHUNK), 0)
    col = lax.broadcasted_iota(jnp.int32, (SGU_CHUNK, SGU_CHUNK), 1)
    tril = row >= col
    upper = lax.broadcasted_iota(jnp.int32, (SGU_CHUNK, LANES), 1) >= SGU_W // SGU_GROUPS
    ws, vlo, vhi, mixed = [], [], [], []
    for pr in range(SGU_W // LANES):
        sl = slice(pr * LANES, (pr + 1) * LANES)
        w0 = jnp.where(tril, w_ref[2 * pr], 0.0).astype(BF16)
        w1 = jnp.where(tril, w_ref[2 * pr + 1], 0.0).astype(BF16)
        vn2 = vn[:, sl]
        lo = jnp.where(upper, 0.0, vn2).astype(BF16)
        hi = jnp.where(upper, vn2, 0.0).astype(BF16)
        mixed.append(jnp.dot(w0, lo, preferred_element_type=F32) + jnp.dot(w1, hi, preferred_element_type=F32)
                     + bias[:, sl])
        ws.append((w0, w1))
        vlo.append(lo)
        vhi.append(hi)
    return cdf, u, xhat, rs, jnp.concatenate(mixed, axis=1), ws, vlo, vhi, tril, upper


SGU_STEP = 4 * SGU_CHUNK


def _for_chunks(step_rows, fn):
    def one(ci, carry):
        fn(pl.ds(pl.multiple_of(ci * SGU_CHUNK, SGU_CHUNK), SGU_CHUNK))
        return carry

    lax.fori_loop(0, step_rows // SGU_CHUNK, one, 0)


def _sgu_fwd(gu, ln_g, ln_b, w_s, bias_exp):
    t = gu.shape[0]
    step = min(t, SGU_STEP)

    def body(uv_ref, g_ref, b_ref, w_ref, bias_ref, o_ref):
        def chunk(rows):
            _, u, _, _, mixed, *_ = _sgu_core(uv_ref[rows, :].astype(F32), g_ref[...], b_ref[...], w_ref, bias_ref[...])
            o_ref[rows, :] = (u * mixed).astype(BF16)

        _for_chunks(step, chunk)

    return pl.pallas_call(
        body, grid=(t // step,),
        in_specs=[pl.BlockSpec((step, 2 * SGU_W), lambda n: (n, 0)), _full((1, SGU_W)), _full((1, SGU_W)),
                  _full((SGU_GROUPS, SGU_CHUNK, SGU_CHUNK)), _full((SGU_CHUNK, SGU_W))],
        out_specs=pl.BlockSpec((step, SGU_W), lambda n: (n, 0)),
        out_shape=jax.ShapeDtypeStruct((t, SGU_W), BF16),
        compiler_params=_cparams(1), name="sgu_fwd")(gu, ln_g, ln_b, w_s, bias_exp)


def _sgu_bwd(dproj, gu, dsgu, ln_g, ln_b, w_s, bias_exp):
    t = gu.shape[0]
    step = min(t, SGU_STEP)
    nsteps = t // step
    e = _head_sum_matrix()

    def body(dp_in, uv_ref, ds_ref, g_ref, b_ref, w_ref, bias_ref, e_ref, out_ref, dw_ref, dbias_ref, dg_ref, db_ref):
        n = pl.program_id(0)

        @pl.when(n == 0)
        def _():
            dw_ref[...] = jnp.zeros(dw_ref.shape, F32)
            dbias_ref[...] = jnp.zeros(dbias_ref.shape, F32)
            dg_ref[...] = jnp.zeros(dg_ref.shape, F32)
            db_ref[...] = jnp.zeros(db_ref.shape, F32)

        _for_chunks(step, functools.partial(chunk, uv_ref, ds_ref, g_ref, b_ref, w_ref, bias_ref, out_ref, dw_ref, dbias_ref,
                                            dg_ref, db_ref))

        @pl.when(n == nsteps - 1)
        def _():
            dbias_ref[...] = _group_sum(dbias_ref[...], e_ref[...])

    def chunk(uv_ref, ds_ref, g_ref, b_ref, w_ref, bias_ref, out_ref, dw_ref, dbias_ref, dg_ref, db_ref, rows):
        uv = uv_ref[rows, :].astype(F32)
        g = g_ref[...]
        cdf, u, xhat, rs, mixed, ws, vlo, vhi, tril, upper = _sgu_core(uv, g, b_ref[...], w_ref, bias_ref[...])
        dsg = ds_ref[rows, :]
        du = dsg * mixed
        dmixed = dsg * u
        dbias_ref[...] += dmixed
        dvn = []
        for pr in range(SGU_W // LANES):
            sl = slice(pr * LANES, (pr + 1) * LANES)
            dm2 = dmixed[:, sl]
            dlo = jnp.where(upper, 0.0, dm2).astype(BF16)
            dhi = jnp.where(upper, dm2, 0.0).astype(BF16)
            w0, w1 = ws[pr]
            dvn.append(lax.dot_general(w0, dlo, (TN, ((), ())), preferred_element_type=F32)
                       + lax.dot_general(w1, dhi, (TN, ((), ())), preferred_element_type=F32))
            dw0 = lax.dot_general(dlo, vlo[pr], (NT, ((), ())), preferred_element_type=F32)
            dw1 = lax.dot_general(dhi, vhi[pr], (NT, ((), ())), preferred_element_type=F32)
            dw_ref[2 * pr] += jnp.where(tril, dw0, 0.0)
            dw_ref[2 * pr + 1] += jnp.where(tril, dw1, 0.0)
        dvn = jnp.concatenate(dvn, axis=1)
        dg_ref[...] += jnp.sum(dvn * xhat, axis=0, keepdims=True)
        db_ref[...] += jnp.sum(dvn, axis=0, keepdims=True)
        dxh = dvn * g
        dv = rs * (dxh - jnp.mean(dxh, axis=1, keepdims=True) - xhat * jnp.mean(dxh * xhat, axis=1, keepdims=True))
        dz = jnp.concatenate([du, dv], axis=1)
        dgelu = cdf + uv * (INV_SQRT_2PI * jnp.exp(-0.5 * uv * uv))
        out_ref[rows, :] = (dz * dgelu).astype(BF16)

    outs = pl.pallas_call(
        body, grid=(nsteps,),
        in_specs=[pl.BlockSpec(memory_space=pl.ANY), pl.BlockSpec((step, 2 * SGU_W), lambda n: (n, 0)),
                  pl.BlockSpec((step, SGU_W), lambda n: (n, 0)), _full((1, SGU_W)), _full((1, SGU_W)),
                  _full((SGU_GROUPS, SGU_CHUNK, SGU_CHUNK)), _full((SGU_CHUNK, SGU_W)), _full((ATTN_W, ATTN_W))],
        out_specs=[pl.BlockSpec((step, 2 * SGU_W), lambda n: (n, 0)), _full((SGU_GROUPS, SGU_CHUNK, SGU_CHUNK)),
                   _full((SGU_CHUNK, SGU_W)), _full((1, SGU_W)), _full((1, SGU_W))],
        out_shape=[jax.ShapeDtypeStruct(dproj.shape, BF16), jax.ShapeDtypeStruct((SGU_GROUPS, SGU_CHUNK, SGU_CHUNK), F32),
                   jax.ShapeDtypeStruct((SGU_CHUNK, SGU_W), F32), jax.ShapeDtypeStruct((1, SGU_W), F32),
                   jax.ShapeDtypeStruct((1, SGU_W), F32)],
        input_output_aliases={0: 0},
        compiler_params=_cparams(1), name="sgu_bwd")(dproj, gu, dsgu, ln_g, ln_b, w_s, bias_exp, e)
    return outs


def _merge_fwd(attn, sgu, gu, x, w_pa, w_ps, w_out, g2):
    t = x.shape[0]
    tm = min(t, 512)

    def body(a_ref, s_ref, ga_ref, gb_ref, x_ref, wpa, wps, wo, g_ref, pa_ref, ps_ref, m_ref, x1_ref, h2_ref):
        pa = jnp.dot(a_ref[...], wpa[...], preferred_element_type=F32)
        ps = jnp.dot(s_ref[...], wps[...], preferred_element_type=F32)
        merged = (_sigmoid(ga_ref[...].astype(F32)) * pa + _sigmoid(gb_ref[...].astype(F32)) * ps).astype(BF16)
        x1 = x_ref[...] + jnp.dot(merged, wo[...], preferred_element_type=F32)
        xhat, _ = _rms_stats(x1)
        pa_ref[...] = pa.astype(BF16)
        ps_ref[...] = ps.astype(BF16)
        m_ref[...] = merged
        x1_ref[...] = x1
        h2_ref[...] = (xhat * g_ref[...]).astype(BF16)

    half = pl.BlockSpec((tm, ATTN_W), lambda i: (i, 0))
    full = pl.BlockSpec((tm, D_MODEL), lambda i: (i, 0))
    return pl.pallas_call(
        body, grid=(t // tm,),
        in_specs=[half, half, pl.BlockSpec((tm, D_MODEL), lambda i: (i, 1)), pl.BlockSpec((tm, D_MODEL), lambda i: (i, 2)),
                  full, _resident((ATTN_W, D_MODEL)), _resident((SGU_W, D_MODEL)), _resident((D_MODEL, D_MODEL)),
                  _full((1, D_MODEL))],
        out_specs=[full] * 5,
        out_shape=[jax.ShapeDtypeStruct((t, D_MODEL), BF16), jax.ShapeDtypeStruct((t, D_MODEL), BF16),
                   jax.ShapeDtypeStruct((t, D_MODEL), BF16), jax.ShapeDtypeStruct((t, D_MODEL), F32),
                   jax.ShapeDtypeStruct((t, D_MODEL), BF16)],
        compiler_params=_cparams(1), name="merge_fwd")(attn, sgu, gu, gu, x, w_pa, w_ps, w_out, g2)


def _merge_bwd(dx1b, gu, pa, ps, w_pa, w_ps, w_out):
    t = dx1b.shape[0]
    tm = min(t, 512)

    def body(d_ref, ga_ref, gb_ref, pa_ref, ps_ref, wpa, wps, wo, out_ref, dpa_ref, dps_ref, da_ref, dsg_ref):
        dm = lax.dot_general(d_ref[...], wo[...], (NT, ((), ())), preferred_element_type=F32)
        sa, sb = _sigmoid(ga_ref[...].astype(F32)), _sigmoid(gb_ref[...].astype(F32))
        dpa = (dm * sa).astype(BF16)
        dps = (dm * sb).astype(BF16)
        out_ref[:, 0:D_MODEL] = jnp.zeros((tm, D_MODEL), BF16)
        out_ref[:, D_MODEL:2 * D_MODEL] = (dm * pa_ref[...].astype(F32) * sa * (1.0 - sa)).astype(BF16)
        out_ref[:, 2 * D_MODEL:GU_COLS] = (dm * ps_ref[...].astype(F32) * sb * (1.0 - sb)).astype(BF16)
        dpa_ref[...] = dpa
        dps_ref[...] = dps
        da_ref[...] = lax.dot_general(dpa, wpa[...], (NT, ((), ())), preferred_element_type=F32)
        dsg_ref[...] = lax.dot_general(dps, wps[...], (NT, ((), ())), preferred_element_type=F32)

    half = pl.BlockSpec((tm, ATTN_W), lambda i: (i, 0))
    full = pl.BlockSpec((tm, D_MODEL), lambda i: (i, 0))
    return pl.pallas_call(
        body, grid=(t // tm,),
        in_specs=[full, pl.BlockSpec((tm, D_MODEL), lambda i: (i, 1)),
                  pl.BlockSpec((tm, D_MODEL), lambda i: (i, 2)), full, full,
                  _resident((ATTN_W, D_MODEL)), _resident((SGU_W, D_MODEL)), _resident((D_MODEL, D_MODEL))],
        out_specs=[pl.BlockSpec((tm, GU_COLS), lambda i: (i, 0)), full, full, half, half],
        out_shape=[jax.ShapeDtypeStruct((t, GU_COLS), BF16), jax.ShapeDtypeStruct((t, D_MODEL), BF16),
                   jax.ShapeDtypeStruct((t, D_MODEL), BF16), jax.ShapeDtypeStruct((t, ATTN_W), F32),
                   jax.ShapeDtypeStruct((t, SGU_W), F32)],
        compiler_params=_cparams(1), name="merge_bwd")(dx1b, gu, gu, pa, ps, w_pa, w_ps, w_out)


def _token_call(name, body, t, tm, ins, outs, reds=(), scratch=()):
    return pl.pallas_call(
        body, grid=(t // tm,), in_specs=[s for _, s in ins],
        out_specs=[o[2] for o in outs] + [_full(r) for r in reds],
        out_shape=[jax.ShapeDtypeStruct(o[0], o[1]) for o in outs] + [jax.ShapeDtypeStruct(r, F32) for r in reds],
        scratch_shapes=list(scratch), compiler_params=_cparams(1), name=name)(*[a for a, _ in ins])


def _rows_spec(tm, width):
    return pl.BlockSpec((tm, width), lambda i: (i, 0))


def _chips_spec(tm):
    return pl.BlockSpec((N_CHIPS, tm, FF_SHARD), lambda i: (0, i, 0))


def _zero_at_start(*refs):
    @pl.when(pl.program_id(0) == 0)
    def _():
        for r in refs:
            r[...] = jnp.zeros(r.shape, r.dtype)


def _ffn_fwd(h2, w_g, w_u):
    t = h2.shape[0]
    tm = min(t, 512)

    def body(h_ref, wg_ref, wu_ref, a_ref, b_ref, ff_ref):
        h = h_ref[...]
        for s in range(N_CHIPS):
            a = jnp.dot(h, wg_ref[s], preferred_element_type=F32)
            b = jnp.dot(h, wu_ref[s], preferred_element_type=F32)
            a_ref[s] = a.astype(BF16)
            b_ref[s] = b.astype(BF16)
            ff_ref[s] = (a * _sigmoid(a) * b).astype(BF16)

    shp = (N_CHIPS, t, FF_SHARD)
    w_spec = _resident((N_CHIPS, D_MODEL, FF_SHARD))
    return _token_call("ffn_fwd", body, t, tm, [(h2, _rows_spec(tm, D_MODEL)), (w_g, w_spec), (w_u, w_spec)],
                       [(shp, BF16, _chips_spec(tm))] * 3)


def _ffn_down_loss(ff, w_d, x1, tgt, gf):
    t = x1.shape[0]
    tm = min(t, 512)

    def body(ff_ref, wd_ref, x1_ref, tgt_ref, g_ref, dx2_ref, dx2b_ref, loss_ref, dgf_ref):
        _zero_at_start(loss_ref, dgf_ref)
        acc = jnp.dot(ff_ref[0], wd_ref[0], preferred_element_type=F32)
        for s in range(1, N_CHIPS):
            acc = acc + jnp.dot(ff_ref[s], wd_ref[s], preferred_element_type=F32)
        x2 = x1_ref[...] + acc
        g = g_ref[...]
        xhat, rr = _rms_stats(x2)
        diff = xhat * g - tgt_ref[...]
        rows = jnp.sum(diff * diff, axis=1, keepdims=True)
        loss_ref[...] += jnp.broadcast_to(jnp.sum(rows, axis=0, keepdims=True) * (0.5 / D_MODEL), (1, LANES))
        dy = diff * (1.0 / D_MODEL)
        dgf_ref[...] += jnp.sum(dy * xhat, axis=0, keepdims=True)
        dx2 = _rms_bwd(dy, xhat, rr, g)
        dx2_ref[...] = dx2
        dx2b_ref[...] = dx2.astype(BF16)

    row = _rows_spec(tm, D_MODEL)
    return _token_call("ffn_down_loss", body, t, tm,
                       [(ff, _chips_spec(tm)), (w_d, _resident((N_CHIPS, FF_SHARD, D_MODEL))), (x1, row), (tgt, row),
                        (gf, _full((1, D_MODEL)))],
                       [((t, D_MODEL), F32, row), ((t, D_MODEL), BF16, row)], reds=[(1, LANES), (1, D_MODEL)])


def _ffn_bwd_act(dx2b, w_d, a, b):
    t = dx2b.shape[0]
    tm = min(t, 512)

    def body(d_ref, wd_ref, a_ref, b_ref, da_ref, db_ref):
        d = d_ref[...]
        for s in range(N_CHIPS):
            dff = lax.dot_general(d, wd_ref[s], (NT, ((), ())), preferred_element_type=F32)
            av, bv = a_ref[s].astype(F32), b_ref[s].astype(F32)
            sg = _sigmoid(av)
            da_ref[s] = (dff * bv * (sg * (1.0 + av * (1.0 - sg)))).astype(BF16)
            db_ref[s] = (dff * (av * sg)).astype(BF16)

    shp = (N_CHIPS, t, FF_SHARD)
    return _token_call("ffn_bwd_act", body, t, tm,
                       [(dx2b, _rows_spec(tm, D_MODEL)), (w_d, _resident((N_CHIPS, FF_SHARD, D_MODEL))),
                        (a, _chips_spec(tm)), (b, _chips_spec(tm))],
                       [(shp, BF16, _chips_spec(tm))] * 2)


def _ffn_bwd_in(da, db, w_g, w_u, x1, dx2, g2):
    t = x1.shape[0]
    tm = min(t, 512)

    def body(da_ref, db_ref, wg_ref, wu_ref, x1_ref, dx2_ref, g_ref, dx1_ref, dx1b_ref, dg_ref):
        _zero_at_start(dg_ref)
        acc = None
        for s in range(N_CHIPS):
            part = (lax.dot_general(da_ref[s], wg_ref[s], (NT, ((), ())), preferred_element_type=F32)
                    + lax.dot_general(db_ref[s], wu_ref[s], (NT, ((), ())), preferred_element_type=F32))
            acc = part if acc is None else acc + part
        xhat, rr = _rms_stats(x1_ref[...])
        dg_ref[...] += jnp.sum(acc * xhat, axis=0, keepdims=True)
        dx1 = dx2_ref[...] + _rms_bwd(acc, xhat, rr, g_ref[...])
        dx1_ref[...] = dx1
        dx1b_ref[...] = dx1.astype(BF16)

    row = _rows_spec(tm, D_MODEL)
    w_spec = _resident((N_CHIPS, D_MODEL, FF_SHARD))
    return _token_call("ffn_bwd_in", body, t, tm,
                       [(da, _chips_spec(tm)), (db, _chips_spec(tm)), (w_g, w_spec), (w_u, w_spec), (x1, row), (dx2, row),
                        (g2, _full((1, D_MODEL)))],
                       [((t, D_MODEL), F32, row), ((t, D_MODEL), BF16, row)], reds=[(1, D_MODEL)])


def _group_dh(d, w_refs):
    dh = None
    for part, w_ref in enumerate(w_refs):
        term = lax.dot_general(d[:, part * ATTN_W:(part + 1) * ATTN_W], w_ref[...], (NT, ((), ())),
                               preferred_element_type=F32)
        dh = term if dh is None else dh + term
    return dh


def _in_proj_bwd(dgu, dqkvs, w_in, x, dx1, g1):
    t = x.shape[0]
    tile = min(t, TILE)
    ngroups = len(DILATIONS)

    def body(*refs):
        dgu_ref, dq_refs = refs[0], refs[1:1 + ngroups]
        w0_ref, w1_ref = refs[1 + ngroups:3 + ngroups]
        wg_refs = [refs[3 + ngroups + 3 * g:6 + ngroups + 3 * g] for g in range(ngroups)]
        x_ref, dx1_ref, g_ref, dx_ref, dg_ref, slab = refs[3 + 4 * ngroups:]
        _zero_at_start(dg_ref)
        dh = lax.dot_general(dgu_ref[:, 0:GU_HALF], w0_ref[...], (NT, ((), ())), preferred_element_type=F32)
        dh = dh + lax.dot_general(dgu_ref[:, GU_HALF:], w1_ref[...], (NT, ((), ())), preferred_element_type=F32)
        dh = dh + _group_dh(dq_refs[0][0], wg_refs[0])
        for g in range(1, ngroups):
            dil = DILATIONS[g]
            part = _group_dh(dq_refs[g][...].reshape(tile, GROUP_COLS), wg_refs[g])
            for r in range(dil):
                _put_class_rows(slab, r, dil, part[r * (tile // dil):(r + 1) * (tile // dil)])
            dh = dh + _from_slabs(slab)
        xhat, rr = _rms_stats(x_ref[...])
        dg_ref[...] += jnp.sum(dh * xhat, axis=0, keepdims=True)
        dx_ref[...] = dx1_ref[...] + _rms_bwd(dh, xhat, rr, g_ref[...])

    row = _rows_spec(tile, D_MODEL)
    group_ins = [(dqkvs[g].reshape(d, t // d, GROUP_COLS), _group_spec(d, tile, GROUP_COLS)) for g, d in enumerate(DILATIONS)]
    w_specs = _gu_w_specs() + [s for g in range(ngroups) for s in _group_w_specs(g)]
    return _token_call(
        "in_proj_bwd", body, t, tile,
        [(dgu, _rows_spec(tile, GU_COLS))] + group_ins + [(w_in, s) for s in w_specs]
        + [(x, row), (dx1, row), (g1, _full((1, D_MODEL)))],
        [((t, D_MODEL), F32, row)], reds=[(1, D_MODEL)], scratch=[_slabs(tile, D_MODEL)])


def _epi_bf16(acc, e, o, r, ids):
    o[0][...] = acc.astype(BF16)


WGRAD_TK = 2048


def _wgrad_2d(name, a, b, tm, tn):
    t, k1 = a.shape
    n = b.shape[1]
    tk = min(t, WGRAD_TK)
    return _mm(name, (k1 // tm, n // tn, t // tk),
               [(a, pl.BlockSpec((tk, tm), lambda i, j, k: (k, i)), b, pl.BlockSpec((tk, tn), lambda i, j, k: (k, j)))],
               TN, (tm, tn), _epi_bf16, outs=[((k1, n), BF16, pl.BlockSpec((tm, tn), lambda i, j, k: (i, j)))])[0]


def _wgrad_in(hs, dgu, dqkvs):
    t = dgu.shape[0]
    tk = min(t, WGRAD_TK)
    gu_block = QKV_BLOCKS * ATTN_W // GU_HALF
    parts = [(hs[0], dgu, GU_HALF, lambda j: j + gu_block)]
    parts += [(hs[g].reshape(t, D_MODEL), dqkvs[g], ATTN_W, lambda j, g=g: _w_in_block(j, g)) for g in range(3)]
    dst = None
    for n, (a, b, tn, block_of) in enumerate(parts):
        dst = _mm(f"wgrad_in_{n}", (1, b.shape[1] // tn, t // tk),
                  [(a, pl.BlockSpec((tk, D_MODEL), lambda i, j, k: (k, 0)), b,
                    pl.BlockSpec((tk, tn), lambda i, j, k: (k, j)))],
                  TN, (D_MODEL, tn), _epi_bf16,
                  extras=[] if dst is None else [(dst, pl.BlockSpec(memory_space=pl.ANY))],
                  outs=[((D_MODEL, IN_COLS), BF16,
                         pl.BlockSpec((D_MODEL, tn), lambda i, j, k, block_of=block_of: (0, block_of(j))))],
                  aliases=None if dst is None else {2: 0})[0]
    return dst


def _wgrad_ff_in(name, h2, da):
    t = h2.shape[0]
    tk = min(t, WGRAD_TK)
    return _mm(name, (N_CHIPS, 1, t // tk),
               [(h2, pl.BlockSpec((tk, D_MODEL), lambda i, j, k: (k, 0)),
                 da, pl.BlockSpec((None, tk, FF_SHARD), lambda i, j, k: (i, k, 0)))],
               TN, (D_MODEL, FF_SHARD), _epi_bf16,
               outs=[((N_CHIPS, D_MODEL, FF_SHARD), BF16, pl.BlockSpec((None, D_MODEL, FF_SHARD), lambda i, j, k: (i, 0, 0)))])[0]


def _wgrad_ff_down(ff, dx2b):
    t = dx2b.shape[0]
    tk = min(t, WGRAD_TK)
    return _mm("wgrad_ffn_down", (N_CHIPS, 1, t // tk),
               [(ff, pl.BlockSpec((None, tk, FF_SHARD), lambda i, j, k: (i, k, 0)),
                 dx2b, pl.BlockSpec((tk, D_MODEL), lambda i, j, k: (k, 0)))],
               TN, (FF_SHARD, D_MODEL), _epi_bf16,
               outs=[((N_CHIPS, FF_SHARD, D_MODEL), BF16, pl.BlockSpec((None, FF_SHARD, D_MODEL), lambda i, j, k: (i, 0, 0)))])[0]


def _local_step(x, pos_col, tgt, g1, ln_g, ln_b, w_s, b_s, g2, gf, first_weight, late_weights, on_grads=None):
    tables = _rope_tables(pos_col)
    bias_exp = jnp.repeat(jnp.transpose(b_s), SGU_W // SGU_GROUPS, axis=1)

    hs = _norm_fwd(x, g1)
    w_p = first_weight(hs[0])
    gu, qkvs = _in_proj(hs, w_p, tables)
    os_, ls_ = [], []
    for g, dil in enumerate(DILATIONS):
        o, lse = _attn_fwd(qkvs[g], g, dil)
        os_.append(o)
        ls_.append(lse)
    attn = _combine_fwd(os_, ls_)
    sgu = _sgu_fwd(gu, ln_g, ln_b, w_s, bias_exp)
    w_pa, w_ps, w_out, w_g, w_u, w_d = late_weights(attn)
    pa, ps, merged, x1, h2 = _merge_fwd(attn, sgu, gu, x, w_pa, w_ps, w_out, g2)
    a, b, ff = _ffn_fwd(h2, w_g, w_u)
    dx2, dx2b, loss, dgf = _ffn_down_loss(ff, w_d, x1, tgt, gf)

    da, db = _ffn_bwd_act(dx2b, w_d, a, b)
    dw_d = _wgrad_ff_down(ff, dx2b)
    dx1, dx1b, dg2 = _ffn_bwd_in(da, db, w_g, w_u, x1, dx2, g2)
    dw_g = _wgrad_ff_in("wgrad_ffn_gate", h2, da)
    dw_u = _wgrad_ff_in("wgrad_ffn_up", h2, db)

    dgu, dpa, dps, dattn, dsgu = _merge_bwd(dx1b, gu, pa, ps, w_pa, w_ps, w_out)
    dw_out = _wgrad_2d("wgrad_out", merged, dx1b, D_MODEL, D_MODEL)
    dw_pa = _wgrad_2d("wgrad_proj_attn", attn, dpa, ATTN_W, D_MODEL)
    dw_ps = _wgrad_2d("wgrad_proj_sgu", sgu, dps, SGU_W, D_MODEL)
    if on_grads is not None:
        ln_g = ln_g + on_grads(1, dict(w_proj_attn=dw_pa, w_proj_sgu=dw_ps, w_out=dw_out, w_ffn_gate=dw_g, w_ffn_up=dw_u,
                                       w_ffn_down=dw_d))[:, :SGU_W]
    dgu, dw_s, dbias, dln_g, dln_b = _sgu_bwd(dgu, gu, dsgu, ln_g, ln_b, w_s, bias_exp)
    dos, ccs = _combine_bwd(dattn, os_, ls_)
    dqkvs = [_attn_bwd(qkvs[g], dos[g], ccs[g], ls_[g], *tables[g], g, dil) for g, dil in enumerate(DILATIONS)]
    dw_p = _wgrad_in(hs, dgu, dqkvs)
    if on_grads is not None:
        g1 = g1 + on_grads(0, dict(w_in=dw_p))
    dx, dg1 = _in_proj_bwd(dgu, dqkvs, w_p, x, dx1, g1)

    db_s = jnp.transpose(dbias[:, ::SGU_W // SGU_GROUPS])
    small = dict(loss=loss, norm1_g=dg1, sgu_ln_g=dln_g, sgu_ln_b=dln_b, w_spatial=dw_s, b_spatial=db_s,
                 norm2_g=dg2, final_g=dgf)
    big = dict(w_in=dw_p, w_proj_attn=dw_pa, w_proj_sgu=dw_ps, w_out=dw_out, w_ffn_gate=dw_g, w_ffn_up=dw_u,
               w_ffn_down=dw_d)
    return dx, big, small


def _ew(name, fn, ins, out_dtypes):
    shp = ins[0].shape
    rows, cols = shp
    tr = next((cand for cand in (256, 352, 128) if rows % cand == 0 and rows > cand), rows)

    def body(*refs):
        res = fn(*[r[...] for r in refs[:len(ins)]])
        for o_ref, v in zip(refs[len(ins):], res):
            o_ref[...] = v.astype(o_ref.dtype)

    spec = pl.BlockSpec((tr, cols), lambda i: (i, 0))
    return pl.pallas_call(
        body, grid=(rows // tr,), in_specs=[spec] * len(ins), out_specs=[spec] * len(out_dtypes),
        out_shape=[jax.ShapeDtypeStruct(shp, d) for d in out_dtypes],
        compiler_params=_cparams(1), name=name)(*ins)


def _adamw_math(g, w, m, v):
    m = ADAM_B1 * m + (1.0 - ADAM_B1) * g
    v = ADAM_B2 * v + (1.0 - ADAM_B2) * (g * g)
    m_hat = m / (1.0 - ADAM_B1 ** ADAM_STEP)
    v_hat = v / (1.0 - ADAM_B2 ** ADAM_STEP)
    delta = -ADAM_LR * (m_hat / (jnp.sqrt(v_hat) + ADAM_EPS) + ADAM_WD * w)
    return delta, m, v


def _adamw(name, g, w, m, v):
    return _ew(name, lambda g_, w_, m_, v_: (g_,) + _adamw_math(g_, w_, m_, v_), [g, w, m, v], [F32] * 4)


VMEM_SPEC = pl.BlockSpec(memory_space=pltpu.VMEM)


def _for_row_chunks(rows, fn):
    ck = next(c for c in (64, 32, 16) if rows % c == 0)

    def step(i, carry):
        fn(pl.multiple_of(i * ck, ck), ck)
        return carry

    lax.fori_loop(0, rows // ck, step, 0)


def _place():
    x, y, c = lax.axis_index("x"), lax.axis_index("y"), lax.axis_index("c")
    chips = [(1 - x, y), (x, 1 - y), (1 - x, 1 - y)]
    return x, y, c, 2 * x + y, chips


def _rows(ref, start, size):
    if len(ref.shape) == 2:
        return ref.at[pl.ds(start, size), :]
    return ref.at[:, pl.ds(start, size), :]


def _comm_call(name, body, ins, out_shapes, scratch, n_remote):
    return pl.pallas_call(
        body, in_specs=[VMEM_SPEC] * len(ins), out_specs=[VMEM_SPEC] * len(out_shapes),
        out_shape=out_shapes,
        scratch_shapes=list(scratch) + [pltpu.SemaphoreType.DMA((n_remote,)), pltpu.SemaphoreType.DMA((n_remote,))],
        compiler_params=pltpu.CompilerParams(vmem_limit_bytes=VMEM_LIMIT), name=name)(*ins)


def _gather_finish(name, shard, landed):
    k_rows, n = shard.shape
    kh = k_rows // 2

    def body(shard_ref, land_ref, out_ref, send, recv):
        x, y, c, me, chips = _place()
        passed = []
        for j, chip in enumerate(chips):
            theirs = 2 * chip[0] + chip[1]
            cp = pltpu.make_async_remote_copy(
                src_ref=land_ref.at[j], dst_ref=_rows(out_ref.at[theirs], c * kh, kh), send_sem=send.at[j],
                recv_sem=recv.at[j], device_id=(x, y, 1 - c), device_id_type=MESH)
            cp.start()
            passed.append(cp)
        mine = out_ref.at[me]

        def put_own(r0, ck):
            mine[pl.ds(r0, ck), :] = shard_ref[pl.ds(r0, ck), :]

        _for_row_chunks(k_rows, put_own)
        for j, chip in enumerate(chips):
            slot = out_ref.at[2 * chip[0] + chip[1]]

            def put_half(r0, ck, j=j, slot=slot):
                slot[pl.ds(pl.multiple_of(c * kh + r0, ck), ck), :] = land_ref[j, pl.ds(r0, ck), :]

            _for_row_chunks(kh, put_half)
        for j, chip in enumerate(chips):
            other = _rows(out_ref.at[2 * chip[0] + chip[1]], (1 - c) * kh, kh)
            pltpu.make_async_remote_copy(src_ref=other, dst_ref=other, send_sem=send.at[j], recv_sem=recv.at[j],
                                         device_id=(x, y, 1 - c), device_id_type=MESH).wait_recv()
        for cp in passed:
            cp.wait_send()

    return _comm_call(name, body, [shard, landed], [jax.ShapeDtypeStruct((N_CHIPS, k_rows, n), shard.dtype)], [], 3)[0]


HBM_SPEC = pl.BlockSpec(memory_space=pltpu.HBM)
SEM_SPEC = pl.BlockSpec(memory_space=pltpu.SEMAPHORE)
DATAFLOW = pltpu.SideEffectType.DATAFLOW_SIDE_EFFECTING
TOKEN_SHAPE = (1, D_MODEL)
N_PEERS = 7


def _peers():
    x, y, c = lax.axis_index("x"), lax.axis_index("y"), lax.axis_index("c")
    flip = lambda v, f: 1 - v if f else v
    return [(flip(x, k & 4), flip(y, k & 2), flip(c, k & 1)) for k in range(1, N_PEERS + 1)]


def _piece_shape(shape):
    return (shape[-2] // 2, shape[2] if len(shape) == 3 else shape[1] // N_CHIPS)


def _device_piece(ref, chip, core):
    kh, n4 = _piece_shape(ref.shape)
    if len(ref.shape) == 3:
        return ref.at[chip, pl.ds(core * kh, kh), :]
    return ref.at[pl.ds(core * kh, kh), pl.ds(chip * n4, n4)]


def _exchange_copies(partials, lands, send, recv):
    return [pltpu.make_async_remote_copy(
        src_ref=_device_piece(partials[t], 2 * px + py, pc), dst_ref=lands[t].at[k], send_sem=send.at[t * N_PEERS + k],
        recv_sem=recv.at[t * N_PEERS + k], device_id=(px, py, pc), device_id_type=MESH)
        for t in range(len(partials)) for k, (px, py, pc) in enumerate(_peers())]


def _gather_copies(shards, lands, send, recv):
    x, y, c, me, chips = _place()
    return [pltpu.make_async_remote_copy(
        src_ref=shards[t], dst_ref=lands[t].at[me], send_sem=send.at[t * 3 + j], recv_sem=recv.at[t * 3 + j],
        device_id=(*chip, c), device_id_type=MESH)
        for t in range(len(shards)) for j, chip in enumerate(chips)]


def _gather_half_copies(shards, lands, send, recv):
    x, y, c, me, chips = _place()
    return [pltpu.make_async_remote_copy(
        src_ref=_rows(shards[t], c * (shards[t].shape[0] // 2), shards[t].shape[0] // 2), dst_ref=lands[t].at[j],
        send_sem=send.at[t * 3 + j], recv_sem=recv.at[t * 3 + j], device_id=(*chip, c), device_id_type=MESH)
        for t in range(len(shards)) for j, chip in enumerate(chips)]


def _split_start(name, copies, per_tensor, srcs, land_shapes):
    nt = len(srcs)
    lands = [lax.empty(s, BF16) for s in land_shapes]
    nsem = nt * per_tensor

    def body(*refs):
        send, recv = refs[2 * nt], refs[2 * nt + 1]
        for cp in copies(refs[:nt], refs[nt:2 * nt], send, recv):
            cp.start()
        refs[-1][...] = jnp.zeros(TOKEN_SHAPE, F32)

    hbm = lambda a: pltpu.with_memory_space_constraint(a, pltpu.HBM)
    outs = pl.pallas_call(
        body, name=name,
        out_shape=[pltpu.SemaphoreType.DMA((nsem,)), pltpu.SemaphoreType.DMA((nsem,))]
        + [pltpu.HBM(s.shape, s.dtype) for s in srcs] + [pltpu.HBM(l.shape, l.dtype) for l in lands]
        + [jax.ShapeDtypeStruct(TOKEN_SHAPE, F32)],
        in_specs=[HBM_SPEC] * (2 * nt), out_specs=[SEM_SPEC, SEM_SPEC] + [HBM_SPEC] * (2 * nt) + [VMEM_SPEC],
        input_output_aliases={i: 2 + i for i in range(2 * nt)},
        compiler_params=pltpu.CompilerParams(has_side_effects=DATAFLOW))(*[hbm(a) for a in list(srcs) + lands])
    return outs[0], outs[1], outs[2:2 + nt], outs[2 + nt:2 + 2 * nt], outs[-1]


def _split_wait(name, copies, send, recv, srcs, lands, after):
    nt = len(srcs)

    def body(*refs):
        for cp in copies(refs[:nt], refs[nt:2 * nt], refs[2 * nt], refs[2 * nt + 1]):
            cp.wait_send()
            cp.wait_recv()

    outs = pl.pallas_call(
        body, name=name,
        out_shape=[pltpu.HBM(s.shape, s.dtype) for s in srcs] + [pltpu.HBM(l.shape, l.dtype) for l in lands],
        in_specs=[HBM_SPEC] * (2 * nt) + [SEM_SPEC, SEM_SPEC, pl.BlockSpec(memory_space=pl.ANY)],
        out_specs=[HBM_SPEC] * (2 * nt), input_output_aliases={i: i for i in range(2 * nt)},
        compiler_params=pltpu.CompilerParams(has_side_effects=DATAFLOW))(*srcs, *lands, send, recv, after)
    return outs[:nt], outs[nt:]


def _device_sum(name, partials, lands):
    nt = len(partials)

    def body(*refs):
        ins, slots, outs, owns = refs[:nt], refs[nt:2 * nt], refs[2 * nt:3 * nt], refs[3 * nt:4 * nt]
        send, recv, loc = refs[4 * nt:]
        x, y, c, me, chips = _place()
        sibling = (x, y, 1 - c)
        loads = [pltpu.make_async_copy(_device_piece(ins[t], me, c), owns[t], loc.at[t]) for t in range(nt)]
        for cp in loads:
            cp.start()
        handed = []
        for t in range(nt):
            kh = owns[t].shape[0]
            loads[t].wait()

            def add(r0, ck, own=owns[t], slot=slots[t], dst=outs[t], kh=kh):
                rows = pl.ds(r0, ck)
                acc = own[rows, :].astype(F32)
                for k in range(N_PEERS):
                    acc = acc + slot[k, rows, :].astype(F32)
                dst[pl.ds(pl.multiple_of(c * kh + r0, ck), ck), :] = acc

            _for_row_chunks(kh, add)
            rc = pltpu.make_async_remote_copy(
                src_ref=_rows(outs[t], c * kh, kh), dst_ref=_rows(outs[t], c * kh, kh), send_sem=send.at[t],
                recv_sem=recv.at[t], device_id=sibling, device_id_type=MESH)
            rc.start()
            handed.append(rc)
        for t in range(nt):
            kh = owns[t].shape[0]
            other = _rows(outs[t], (1 - c) * kh, kh)
            pltpu.make_async_remote_copy(
                src_ref=other, dst_ref=other, send_sem=send.at[t], recv_sem=recv.at[t],
                device_id=sibling, device_id_type=MESH).wait_recv()
        for rc in handed:
            rc.wait_send()

    pieces = [_piece_shape(p.shape) for p in partials]
    return pl.pallas_call(
        body, in_specs=[pl.BlockSpec(memory_space=pl.ANY)] * nt + [VMEM_SPEC] * nt, out_specs=[VMEM_SPEC] * nt,
        out_shape=[jax.ShapeDtypeStruct((2 * kh, n4), F32) for kh, n4 in pieces],
        scratch_shapes=[pltpu.VMEM(p, BF16) for p in pieces]
        + [pltpu.SemaphoreType.DMA((nt,)), pltpu.SemaphoreType.DMA((nt,)), pltpu.SemaphoreType.DMA((nt,))],
        compiler_params=pltpu.CompilerParams(vmem_limit_bytes=VMEM_LIMIT), name=name)(*partials, *lands)


VEC_SHAPE = (8, D_MODEL + LANES)
VEC_SLOTS = dict(norm1_g=(slice(0, 1), slice(0, D_MODEL)), norm2_g=(slice(1, 2), slice(0, D_MODEL)),
                 final_g=(slice(2, 3), slice(0, D_MODEL)), sgu_ln_g=(slice(3, 4), slice(0, SGU_W)),
                 sgu_ln_b=(slice(3, 4), slice(SGU_W, 2 * SGU_W)), b_spatial=(slice(0, 8), slice(D_MODEL, D_MODEL + LANES)),
                 loss=(slice(4, 5), slice(0, LANES)))
VEC_PARAMS = ("norm1_g", "norm2_g", "final_g", "sgu_ln_g", "sgu_ln_b", "b_spatial")
SMALL_PARAMS = VEC_PARAMS + ("w_spatial",)
W_SPATIAL_2D = (SGU_GROUPS * SGU_CHUNK, SGU_CHUNK)


def _small_step(partials, w, m, v):
    def shape2d(name):
        if name == "w_spatial":
            return W_SPATIAL_2D
        rows, cols = VEC_SLOTS[name]
        return (rows.stop - rows.start, cols.stop - cols.start)

    g_names = VEC_PARAMS + ("loss", "w_spatial")
    ng, npar = len(g_names), len(SMALL_PARAMS)

    def pack(dst, parts):
        dst[...] = jnp.zeros(VEC_SHAPE, F32)
        for n, ref in parts.items():
            if n in VEC_SLOTS:
                dst[VEC_SLOTS[n]] = ref[...]

    def reduce_body(*refs):
        g_in = dict(zip(g_names, refs[:ng]))
        vec_out, ws_out, vec, vec_pair, vec_slot, ws_pair, ws_slot, send, recv = refs[ng:]
        x, y, c, me, chips = _place()
        sibling = (x, y, 1 - c)
        pack(vec, g_in)
        copies = []

        def allreduce(k0, src, pair, slot):
            first = pltpu.make_async_remote_copy(src_ref=src, dst_ref=pair, send_sem=send.at[k0], recv_sem=recv.at[k0],
                                                 device_id=sibling, device_id_type=MESH)
            first.start()
            first.wait_recv()
            slot[me] = src[...] + pair[...]
            arrivals = []
            for j, chip in enumerate(chips):
                theirs = 2 * chip[0] + chip[1]
                rc = pltpu.make_async_remote_copy(src_ref=slot.at[me], dst_ref=slot.at[me], send_sem=send.at[k0 + 1 + j],
                                                  recv_sem=recv.at[k0 + 1 + j], device_id=(*chip, c), device_id_type=MESH)
                rc.start()
                arrivals.append(pltpu.make_async_remote_copy(
                    src_ref=slot.at[theirs], dst_ref=slot.at[theirs], send_sem=send.at[k0 + 1 + j],
                    recv_sem=recv.at[k0 + 1 + j], device_id=(*chip, c), device_id_type=MESH))
                copies.append(rc)
            copies.append(first)
            return arrivals

        arrivals = allreduce(0, vec, vec_pair, vec_slot) + allreduce(4, g_in["w_spatial"], ws_pair, ws_slot)
        for a in arrivals:
            a.wait_recv()
        vec_out[...] = ((vec_slot[0] + vec_slot[1]) + vec_slot[2]) + vec_slot[3]

        def spatial(r0, ck):
            rows = pl.ds(r0, ck)
            ws_out[rows, :] = ((ws_slot[0, rows, :] + ws_slot[1, rows, :]) + ws_slot[2, rows, :]) + ws_slot[3, rows, :]

        _for_row_chunks(W_SPATIAL_2D[0], spatial)
        for rc in copies:
            rc.wait_send()

    g_vec, g_ws = pl.pallas_call(
        reduce_body, in_specs=[VMEM_SPEC] * ng, out_specs=[VMEM_SPEC] * 2,
        out_shape=[jax.ShapeDtypeStruct(VEC_SHAPE, F32), jax.ShapeDtypeStruct(W_SPATIAL_2D, F32)],
        scratch_shapes=[pltpu.VMEM(VEC_SHAPE, F32), pltpu.VMEM(VEC_SHAPE, F32), pltpu.VMEM((N_CHIPS,) + VEC_SHAPE, F32),
                        pltpu.VMEM(W_SPATIAL_2D, F32), pltpu.VMEM((N_CHIPS,) + W_SPATIAL_2D, F32),
                        pltpu.SemaphoreType.DMA((8,)), pltpu.SemaphoreType.DMA((8,))],
        name="small_params_allreduce")(*[partials[n].reshape(shape2d(n)) for n in g_names])

    def update_body(*refs):
        gv_ref, gw_ref = refs[:2]
        w_in, m_in, v_in = (dict(zip(SMALL_PARAMS, refs[2 + k * npar:2 + (k + 1) * npar])) for k in range(3))
        o0 = 2 + 3 * npar
        g_out = dict(zip(g_names, refs[o0:o0 + ng]))
        d_out, m_out, v_out = (dict(zip(SMALL_PARAMS, refs[o0 + ng + k * npar:o0 + ng + (k + 1) * npar])) for k in range(3))
        vw, vm, vv = refs[o0 + ng + 3 * npar:]
        pack(vw, w_in)
        pack(vm, m_in)
        pack(vv, v_in)
        d_vec, m_vec, v_vec = _adamw_math(gv_ref[...], vw[...], vm[...], vv[...])
        vw[...] = d_vec
        vm[...] = m_vec
        vv[...] = v_vec
        for n in VEC_PARAMS + ("loss",):
            g_out[n][...] = gv_ref[VEC_SLOTS[n]]
        for n in VEC_PARAMS:
            d_out[n][...] = vw[VEC_SLOTS[n]]
            m_out[n][...] = vm[VEC_SLOTS[n]]
            v_out[n][...] = vv[VEC_SLOTS[n]]

        def spatial(r0, ck):
            rows = pl.ds(r0, ck)
            g = gw_ref[rows, :]
            d_, m_, v_ = _adamw_math(g, w_in["w_spatial"][rows, :], m_in["w_spatial"][rows, :], v_in["w_spatial"][rows, :])
            g_out["w_spatial"][rows, :] = g
            d_out["w_spatial"][rows, :] = d_
            m_out["w_spatial"][rows, :] = m_
            v_out["w_spatial"][rows, :] = v_

        _for_row_chunks(W_SPATIAL_2D[0], spatial)

    ins = [g_vec, g_ws]
    for src in (w, m, v):
        ins += [src[n].reshape(shape2d(n)) for n in SMALL_PARAMS]
    out_shapes = [jax.ShapeDtypeStruct(shape2d(n), F32) for n in g_names + SMALL_PARAMS * 3]
    outs = pl.pallas_call(
        update_body, in_specs=[VMEM_SPEC] * len(ins), out_specs=[VMEM_SPEC] * len(out_shapes), out_shape=out_shapes,
        scratch_shapes=[pltpu.VMEM(VEC_SHAPE, F32)] * 3, name="small_params_update")(*ins)
    grads = dict(zip(g_names, outs[:ng]))
    rest = [dict(zip(SMALL_PARAMS, outs[ng + k * npar:ng + (k + 1) * npar])) for k in range(3)]
    return grads, rest[0], rest[1], rest[2]


BIG = ("w_in", "w_proj_attn", "w_proj_sgu", "w_out", "w_ffn_gate", "w_ffn_up", "w_ffn_down")
COMM_GROUPS = (("w_in",), ("w_proj_attn", "w_proj_sgu", "w_out", "w_ffn_gate", "w_ffn_up", "w_ffn_down"))
WEIGHTS = ("norm1_g", "w_in", "sgu_ln_g", "sgu_ln_b", "w_spatial", "b_spatial", "w_proj_attn", "w_proj_sgu", "w_out",
           "norm2_g", "w_ffn_gate", "w_ffn_up", "w_ffn_down", "final_g")


def _cols_from_chips(g):
    return jnp.transpose(g, (1, 0, 2)).reshape(g.shape[1], N_CHIPS * g.shape[2])


def kernel(x, positions, norm1_g, w_in, sgu_ln_g, sgu_ln_b, w_spatial, b_spatial, w_proj_attn, w_proj_sgu, w_out, norm2_g, w_ffn_gate, w_ffn_up, w_ffn_down, final_g, loss_target, m_norm1_g, m_w_in, m_sgu_ln_g, m_sgu_ln_b, m_w_spatial, m_b_spatial, m_w_proj_attn, m_w_proj_sgu, m_w_out, m_norm2_g, m_w_ffn_gate, m_w_ffn_up, m_w_ffn_down, m_final_g, v_norm1_g, v_w_in, v_sgu_ln_g, v_sgu_ln_b, v_w_spatial, v_b_spatial, v_w_proj_attn, v_w_proj_sgu, v_w_out, v_norm2_g, v_w_ffn_gate, v_w_ffn_up, v_w_ffn_down, v_final_g):
    w = dict(norm1_g=norm1_g, w_in=w_in, sgu_ln_g=sgu_ln_g, sgu_ln_b=sgu_ln_b, w_spatial=w_spatial, b_spatial=b_spatial,
             w_proj_attn=w_proj_attn, w_proj_sgu=w_proj_sgu, w_out=w_out, norm2_g=norm2_g, w_ffn_gate=w_ffn_gate,
             w_ffn_up=w_ffn_up, w_ffn_down=w_ffn_down, final_g=final_g)
    m = dict(norm1_g=m_norm1_g, w_in=m_w_in, sgu_ln_g=m_sgu_ln_g, sgu_ln_b=m_sgu_ln_b, w_spatial=m_w_spatial,
             b_spatial=m_b_spatial, w_proj_attn=m_w_proj_attn, w_proj_sgu=m_w_proj_sgu, w_out=m_w_out, norm2_g=m_norm2_g,
             w_ffn_gate=m_w_ffn_gate, w_ffn_up=m_w_ffn_up, w_ffn_down=m_w_ffn_down, final_g=m_final_g)
    v = dict(norm1_g=v_norm1_g, w_in=v_w_in, sgu_ln_g=v_sgu_ln_g, sgu_ln_b=v_sgu_ln_b, w_spatial=v_w_spatial,
             b_spatial=v_b_spatial, w_proj_attn=v_w_proj_attn, w_proj_sgu=v_w_proj_sgu, w_out=v_w_out, norm2_g=v_norm2_g,
             w_ffn_gate=v_w_ffn_gate, w_ffn_up=v_w_ffn_up, w_ffn_down=v_w_ffn_down, final_g=v_final_g)
    t = x.shape[1]

    shards = {n: _ew(f"cast_{n}", lambda a: (a,), [w[n][0]], [BF16])[0] for n in BIG}
    late = COMM_GROUPS[1]
    k_in, n_in = shards["w_in"].shape
    *first, token = _split_start("gather_start_0", _gather_half_copies, 3, [shards["w_in"]], [(3, k_in // 2, n_in)])
    pending = {}

    def first_weight(after):
        srcs, filled = _split_wait("gather_wait_0", _gather_half_copies, *first, after)
        gath_in, late_shards = lax.optimization_barrier(
            (_gather_finish("gather_finish_0", srcs[0], filled[0]), [shards[n] for n in late]))
        *pending["late"], _ = _split_start(
            "gather_start_1", _gather_copies, 3, late_shards, [(N_CHIPS,) + s.shape for s in late_shards])
        return _cols_from_chips(gath_in)

    def late_weights(after):
        srcs, filled = _split_wait("gather_wait_1", _gather_copies, *pending["late"], after)
        me = 2 * lax.axis_index("x") + lax.axis_index("y")
        gath = {n: lax.dynamic_update_slice(f, s[None], (me, 0, 0)) for n, f, s in zip(late, filled, srcs)}
        return (_cols_from_chips(gath["w_proj_attn"]), _cols_from_chips(gath["w_proj_sgu"]),
                gath["w_out"].reshape(D_MODEL, D_MODEL), gath["w_ffn_gate"], gath["w_ffn_up"], gath["w_ffn_down"])

    exchanges = {}

    def on_grads(i, partials):
        if "w_out" in partials:
            partials["w_out"] = partials["w_out"].reshape(N_CHIPS, D_MODEL // N_CHIPS, D_MODEL)
        parts = [partials[n] for n in COMM_GROUPS[i]]
        *exchanges[i], started = _split_start(
            f"rs_exchange_start_{i}", _exchange_copies, N_PEERS, parts, [(N_PEERS,) + _piece_shape(p.shape) for p in parts])
        return started

    dx, _, small = _local_step(
        x[0], positions.reshape(t, 1), loss_target[0], norm1_g + token, sgu_ln_g, sgu_ln_b, w_spatial[0], b_spatial[0],
        norm2_g, final_g.reshape(1, D_MODEL), first_weight, late_weights, on_grads=on_grads)

    grads = {}
    for i in (1, 0):
        parts, filled = _split_wait(f"rs_exchange_wait_{i}", _exchange_copies, *exchanges[i], dx)
        grads.update(zip(COMM_GROUPS[i], _device_sum(f"rs_device_sum_{i}", parts, filled)))

    delta, new_m, new_v = {}, {}, {}
    for n in BIG:
        shp = w[n].shape
        g_, d_, m_, v_ = _adamw(f"adamw_{n}", grads[n], w[n][0], m[n][0], v[n][0])
        grads[n], delta[n], new_m[n], new_v[n] = g_.reshape(shp), d_.reshape(shp), m_.reshape(shp), v_.reshape(shp)

    g_s, d_s, m_s, v_s = _small_step(small, w, m, v)
    loss = g_s["loss"][0, 0]
    for n in SMALL_PARAMS:
        shp = w[n].shape
        grads[n], delta[n], new_m[n], new_v[n] = (a[n].reshape(shp) for a in (g_s, d_s, m_s, v_s))

    return (loss, dx.reshape(x.shape), *[grads[n] for n in WEIGHTS], *[delta[n] for n in WEIGHTS],
            *[new_m[n] for n in WEIGHTS], *[new_v[n] for n in WEIGHTS])
```

```python
import functools

import numpy as np
import jax
import jax.numpy as jnp
from jax import lax
from jax.experimental import pallas as pl
from jax.experimental.pallas import tpu as pltpu

F32, BF16 = jnp.float32, jnp.bfloat16
MESH = pl.DeviceIdType.MESH

D_MODEL = 1024
HEAD_DIM = 64
ATTN_W = 512
DILATIONS = (1, 4, 16)
BLK = 128
ATTN_BLOCKS_PER_STEP = 4
ROPE_DIM = 16
ROPE_THETA = 500000.0
SGU_W = 512
SGU_CHUNK = 128
SGU_GROUPS = 8
D_FF = 2816
N_CHIPS = 4
FF_SHARD = D_FF // N_CHIPS
IN_COLS = 7680
EPS = 1e-6
NEG = -1e30
LANES = 128
VMEM_LIMIT = 52 * 1024 * 1024

ADAM_LR, ADAM_B1, ADAM_B2, ADAM_EPS, ADAM_WD, ADAM_STEP = 0.001, 0.9, 0.999, 1e-08, 0.01, 10

QKV_BLOCKS = 9


def _w_in_block(part, g):
    return part * len(DILATIONS) + g


def _cparams(ngrid):
    return pltpu.CompilerParams(dimension_semantics=("arbitrary",) * ngrid, vmem_limit_bytes=VMEM_LIMIT)


def _full(shape):
    return pl.BlockSpec(shape, lambda *_: (0,) * len(shape))


def _resident(shape):
    return pl.BlockSpec(shape, lambda *_: (0,) * len(shape), pipeline_mode=pl.Buffered(1))


NN = ((1,), (0,))
NT = ((1,), (1,))
TN = ((0,), (0,))


def _mm(name, grid, pairs, dims, acc_shape, epi, *, extras=(), outs=(), reds=(), aliases=None):
    nk = grid[-1]
    npair, nex, nout, nred = len(pairs), len(extras), len(outs), len(reds)

    def body(*refs):
        a_refs = refs[:npair]
        b_refs = refs[npair:2 * npair]
        p0 = 2 * npair
        e_refs = refs[p0:p0 + nex]
        o_refs = refs[p0 + nex:p0 + nex + nout]
        r_refs = refs[p0 + nex + nout:p0 + nex + nout + nred]
        ids = [pl.program_id(a) for a in range(len(grid))]
        k = ids[-1]
        if nred:
            first = ids[0] == 0
            for v in ids[1:]:
                first = first & (v == 0)

            @pl.when(first)
            def _():
                for r in r_refs:
                    r[...] = jnp.zeros(r.shape, r.dtype)

        part = None
        for a_ref, b_ref in zip(a_refs, b_refs):
            d = lax.dot_general(a_ref[...], b_ref[...], (dims, ((), ())), preferred_element_type=F32)
            part = d if part is None else part + d
        if nk == 1:
            epi(part, e_refs, o_refs, r_refs, ids)
        else:
            acc_ref = refs[-1]

            @pl.when(k == 0)
            def _():
                acc_ref[...] = part

            @pl.when(k > 0)
            def _():
                acc_ref[...] += part

            @pl.when(k == nk - 1)
            def _():
                epi(acc_ref[...], e_refs, o_refs, r_refs, ids)

    in_specs = [p[1] for p in pairs] + [p[3] for p in pairs] + [e[1] for e in extras]
    args = [p[0] for p in pairs] + [p[2] for p in pairs] + [e[0] for e in extras]
    out_shape = [jax.ShapeDtypeStruct(o[0], o[1]) for o in outs] + [jax.ShapeDtypeStruct(r, F32) for r in reds]
    out_specs = [o[2] for o in outs] + [_full(r) for r in reds]
    scratch_shapes = [pltpu.VMEM(acc_shape, F32)] if nk > 1 else []
    return pl.pallas_call(
        body, grid=grid, in_specs=in_specs, out_specs=out_specs, out_shape=out_shape, scratch_shapes=scratch_shapes,
        input_output_aliases=aliases or {}, compiler_params=_cparams(len(grid)), name=name)(*args)


def _rope(v, cos_t, sin_t):
    half = ROPE_DIM // 2
    first = (lax.broadcasted_iota(jnp.int32, cos_t.shape, 1) % HEAD_DIM) < half
    outs = []
    for cs in range(v.shape[1] // LANES):
        x = v[:, cs * LANES:(cs + 1) * LANES]
        partner = jnp.where(first, pltpu.roll(x, LANES - half, axis=1), pltpu.roll(x, half, axis=1))
        outs.append(x * cos_t + partner * sin_t)
    return outs[0] if len(outs) == 1 else jnp.concatenate(outs, axis=1)


def _spread_heads(v2, upper):
    other = pltpu.roll(v2, HEAD_DIM, axis=1)
    h0 = jnp.where(upper, other, v2)
    h1 = jnp.where(upper, v2, other)
    return jnp.concatenate([jnp.concatenate([h0, h0], axis=1), jnp.concatenate([h1, h1], axis=1)], axis=0)


def _sigmoid(v):
    return 0.5 * jnp.tanh(0.5 * v) + 0.5


def _rms_stats(v):
    r = lax.rsqrt(jnp.mean(v * v, axis=-1, keepdims=True) + EPS)
    return v * r, r


def _rms_bwd(dy, xhat, r, g):
    dxh = dy * g
    return r * (dxh - xhat * jnp.mean(dxh * xhat, axis=-1, keepdims=True))


def _head_sum_matrix():
    idx = np.arange(ATTN_W) // HEAD_DIM
    return jnp.asarray((idx[:, None] == idx[None, :]).astype(np.float32), dtype=BF16)


def _group_sum(v, e):
    hi = v.astype(BF16)
    lo = (v - hi.astype(F32)).astype(BF16)
    return jnp.dot(hi, e, preferred_element_type=F32) + jnp.dot(lo, e, preferred_element_type=F32)


TILE = 512


def _to_slabs(slab_ref, v):
    for cs in range(slab_ref.shape[0]):
        slab_ref[cs] = v[:, cs * LANES:(cs + 1) * LANES]


def _from_slabs(slab_ref):
    return jnp.concatenate([slab_ref[cs] for cs in range(slab_ref.shape[0])], axis=1)


def _class_rows(slab_ref, r, dil):
    n = slab_ref.shape[1] // dil
    return jnp.concatenate([slab_ref.at[cs][pl.ds(r, n, stride=dil), :] for cs in range(slab_ref.shape[0])], axis=1)


def _put_class_rows(slab_ref, r, dil, v):
    n = slab_ref.shape[1] // dil
    for cs in range(slab_ref.shape[0]):
        slab_ref.at[cs][pl.ds(r, n, stride=dil), :] = v[:, cs * LANES:(cs + 1) * LANES]


def _natural_from_group(slab_ref, grp_ref):
    dil = grp_ref.shape[0]
    for r in range(dil):
        _put_class_rows(slab_ref, r, dil, grp_ref[r].astype(F32))
    return _from_slabs(slab_ref)


def _group_from_natural(slab_ref, grp_ref, v):
    dil = grp_ref.shape[0]
    _to_slabs(slab_ref, v)
    for r in range(dil):
        grp_ref[r] = _class_rows(slab_ref, r, dil).astype(grp_ref.dtype)


def _group_spec(dil, tile, width):
    return pl.BlockSpec((dil, tile // dil, width), lambda i, *_: (0, i, 0))


def _slabs(tile, width):
    return pltpu.VMEM((width // LANES, tile, LANES), F32)


def _rope_consts():
    lane = np.arange(LANES) % HEAD_DIM
    fi = lane % (ROPE_DIM // 2)
    invf = np.where(lane < ROPE_DIM, ROPE_THETA ** (-(2.0 * fi) / ROPE_DIM), 0.0)
    sgn = np.where(lane < ROPE_DIM // 2, -1.0, np.where(lane < ROPE_DIM, 1.0, 0.0))
    return (jnp.asarray(invf.astype(np.float32)).reshape(1, LANES), jnp.asarray(sgn.astype(np.float32)).reshape(1, LANES))


def _rope_tables(pos_col):
    t = pos_col.shape[0]
    tile = min(t, TILE)
    invf, sgn = _rope_consts()

    def body(p_ref, f_ref, s_ref, c0, s0, c1, s1, c2, s2, slab_c, slab_s):
        ang = p_ref[...].astype(F32) * f_ref[...]
        cos, sin = jnp.cos(ang), jnp.sin(ang) * s_ref[...]
        c0[...] = cos
        s0[...] = sin
        _group_from_natural(slab_c, c1, cos)
        _group_from_natural(slab_s, s1, sin)
        for r in range(DILATIONS[2]):
            c2[r] = _class_rows(slab_c, r, DILATIONS[2])
            s2[r] = _class_rows(slab_s, r, DILATIONS[2])

    nat = pl.BlockSpec((tile, LANES), lambda i: (i, 0))
    specs, shapes = [nat, nat], [(t, LANES)] * 2
    for d in DILATIONS[1:]:
        specs += [_group_spec(d, tile, LANES)] * 2
        shapes += [(d, t // d, LANES)] * 2
    outs = pl.pallas_call(
        body, grid=(t // tile,),
        in_specs=[pl.BlockSpec((tile, 1), lambda i: (i, 0)), _full((1, LANES)), _full((1, LANES))],
        out_specs=specs, out_shape=[jax.ShapeDtypeStruct(s, F32) for s in shapes],
        scratch_shapes=[_slabs(tile, LANES)] * 2,
        compiler_params=_cparams(1), name="rope_tables")(pos_col, invf, sgn)
    return [(outs[2 * g].reshape(t, LANES), outs[2 * g + 1].reshape(t, LANES)) for g in range(len(DILATIONS))]


def _norm_fwd(x, g):
    t = x.shape[0]
    tile = min(t, TILE)

    def body(x_ref, g_ref, h0_ref, h1_ref, h2_ref, slab):
        xhat, _ = _rms_stats(x_ref[...])
        hn = xhat * g_ref[...]
        h0_ref[...] = hn.astype(BF16)
        _group_from_natural(slab, h1_ref, hn)
        for r in range(DILATIONS[2]):
            h2_ref[r] = _class_rows(slab, r, DILATIONS[2]).astype(BF16)

    nat = pl.BlockSpec((tile, D_MODEL), lambda i: (i, 0))
    return pl.pallas_call(
        body, grid=(t // tile,),
        in_specs=[nat, _full((1, D_MODEL))],
        out_specs=[nat] + [_group_spec(d, tile, D_MODEL) for d in DILATIONS[1:]],
        out_shape=[jax.ShapeDtypeStruct((t, D_MODEL), BF16)]
        + [jax.ShapeDtypeStruct((d, t // d, D_MODEL), BF16) for d in DILATIONS[1:]],
        scratch_shapes=[_slabs(tile, D_MODEL)],
        compiler_params=_cparams(1), name="norm1_fwd")(x, g)


GU_COLS = 3072
GROUP_COLS = 1536
GU_HALF = GU_COLS // 2


def _w_in_spec(width, block):
    return pl.BlockSpec((D_MODEL, width), lambda i: (0, block), pipeline_mode=pl.Buffered(1))


def _gu_w_specs():
    first = QKV_BLOCKS * ATTN_W // GU_HALF
    return [_w_in_spec(GU_HALF, first), _w_in_spec(GU_HALF, first + 1)]


def _group_w_specs(g):
    return [_w_in_spec(ATTN_W, _w_in_block(part, g)) for part in range(3)]


def _in_proj(hs, w_in, tables):
    t = hs[0].shape[0]
    tm = min(t, 1024)

    def body_gu(h_ref, w0_ref, w1_ref, o_ref):
        h = h_ref[...]
        o_ref[:, 0:GU_HALF] = jnp.dot(h, w0_ref[...], preferred_element_type=F32).astype(BF16)
        o_ref[:, GU_HALF:] = jnp.dot(h, w1_ref[...], preferred_element_type=F32).astype(BF16)

    gu = _token_call("in_proj_gates_uv", body_gu, t, tm,
                     [(hs[0], _rows_spec(tm, D_MODEL))] + [(w_in, s) for s in _gu_w_specs()],
                     [((t, GU_COLS), BF16, _rows_spec(tm, GU_COLS))])[0]

    qkvs = []
    for g in range(len(DILATIONS)):

        def body_qkv(h_ref, wq_ref, wk_ref, wv_ref, cos_ref, sin_ref, o_ref):
            h = h_ref[...]
            cos_w, sin_w = cos_ref[...], sin_ref[...]
            q = jnp.dot(h, wq_ref[...], preferred_element_type=F32)
            o_ref[:, 0:ATTN_W] = (_rope(q, cos_w, sin_w) * HEAD_DIM ** -0.5).astype(BF16)
            k = jnp.dot(h, wk_ref[...], preferred_element_type=F32)
            o_ref[:, ATTN_W:2 * ATTN_W] = _rope(k, cos_w, sin_w).astype(BF16)
            o_ref[:, 2 * ATTN_W:] = jnp.dot(h, wv_ref[...], preferred_element_type=F32).astype(BF16)

        cos_t, sin_t = tables[g]
        qkvs.append(_token_call(
            f"in_proj_qkv_g{g}", body_qkv, t, tm,
            [(hs[g].reshape(t, D_MODEL), _rows_spec(tm, D_MODEL))] + [(w_in, s) for s in _group_w_specs(g)]
            + [(cos_t, _rows_spec(tm, LANES)), (sin_t, _rows_spec(tm, LANES))],
            [((t, GROUP_COLS), BF16, _rows_spec(tm, GROUP_COLS))])[0])
    return gu, qkvs


def _attn_masks(n):
    row = lax.broadcasted_iota(jnp.int32, (2 * BLK, 2 * BLK), 0) % BLK
    col = lax.broadcasted_iota(jnp.int32, (2 * BLK, 2 * BLK), 1)
    diff = BLK + row - col
    valid = (diff >= 0) & (diff <= BLK) & ((col >= BLK) | (n > 0))
    upper = lax.broadcasted_iota(jnp.int32, (BLK, LANES), 1) >= HEAD_DIM
    return valid, upper


def _stack_heads(v2, upper):
    zero = jnp.zeros_like(v2)
    return jnp.concatenate([jnp.where(upper, zero, v2), jnp.where(upper, v2, zero)], axis=0)


def _unstack_heads(v, upper):
    return jnp.where(upper, v[BLK:], v[:BLK])


def _attn_fwd(qkv, g, dil):
    t = qkv.shape[0]
    length = t // dil
    nb = length // BLK
    per_step = min(nb, ATTN_BLOCKS_PER_STEP)
    view = qkv.reshape(dil, length, GROUP_COLS)

    def body(q_ref, kc_ref, kp_ref, vc_ref, vp_ref, o_ref, l_ref, kwin, vwin):
        n = pl.program_id(1)
        kwin[0:BLK] = kp_ref[...]
        kwin[BLK:] = kc_ref[...]
        vwin[0:BLK] = vp_ref[...]
        vwin[BLK:] = vc_ref[...]

        def block(b, carry):
            valid, upper = _attn_masks(n * per_step + b)
            rows = pl.ds(pl.multiple_of(b * BLK, BLK), BLK)
            window = pl.ds(pl.multiple_of(b * BLK, BLK), 2 * BLK)
            for p in range(ATTN_W // LANES):
                sl = slice(p * LANES, (p + 1) * LANES)
                qs = _stack_heads(q_ref[rows, sl], upper)
                s = lax.dot_general(qs, kwin[window, sl], (NT, ((), ())), preferred_element_type=F32)
                s = jnp.where(valid, s, NEG)
                m = jnp.max(s, axis=1, keepdims=True)
                pe = jnp.exp(s - m)
                den = jnp.sum(pe, axis=1, keepdims=True)
                o = jnp.dot(pe.astype(BF16), vwin[window, sl], preferred_element_type=F32) / den
                lse = jnp.broadcast_to(m + jnp.log(den), (2 * BLK, LANES))
                o_ref[rows, sl] = _unstack_heads(o, upper).astype(BF16)
                l_ref[rows, sl] = _unstack_heads(lse, upper)
            return carry

        lax.fori_loop(0, per_step, block, 0)

    rows = per_step * BLK
    cur = lambda part: pl.BlockSpec((None, rows, ATTN_W), lambda r, n: (r, n, part))
    prev = lambda part: pl.BlockSpec((None, BLK, ATTN_W), lambda r, n: (r, jnp.maximum(n * per_step - 1, 0), part))
    out_spec = pl.BlockSpec((None, rows, ATTN_W), lambda r, n: (r, n, 0))
    return pl.pallas_call(
        body, grid=(dil, nb // per_step),
        in_specs=[cur(0), cur(1), prev(1), cur(2), prev(2)],
        out_specs=[out_spec, out_spec],
        out_shape=[jax.ShapeDtypeStruct((dil, length, ATTN_W), BF16), jax.ShapeDtypeStruct((dil, length, ATTN_W), F32)],
        scratch_shapes=[pltpu.VMEM((rows + BLK, ATTN_W), BF16)] * 2,
        compiler_params=_cparams(2), name=f"attn_fwd_g{g}")(view, view, view, view, view)


def _alphas(l0, l1, l2):
    m = jnp.maximum(jnp.maximum(l0, l1), l2)
    e0, e1, e2 = jnp.exp(l0 - m), jnp.exp(l1 - m), jnp.exp(l2 - m)
    inv = 1.0 / (e0 + e1 + e2)
    return e0 * inv, e1 * inv, e2 * inv


def _natural_group_values(o_refs, l_refs, slabs):
    os_ = [o_refs[0][0].astype(F32)] + [_natural_from_group(slabs[2 * g - 2], o_refs[g]) for g in (1, 2)]
    ls_ = [l_refs[0][0]] + [_natural_from_group(slabs[2 * g - 1], l_refs[g]) for g in (1, 2)]
    return os_, ls_


def _combine_fwd(os_, ls_):
    t = os_[0].shape[1]
    tile = min(t, TILE)

    def body(o0, o1, o2, l0, l1, l2, a_ref, *slabs):
        ov, lv = _natural_group_values((o0, o1, o2), (l0, l1, l2), slabs)
        a0, a1, a2 = _alphas(*lv)
        a_ref[...] = (a0 * ov[0] + a1 * ov[1] + a2 * ov[2]).astype(BF16)

    specs = [_group_spec(d, tile, ATTN_W) for d in DILATIONS]
    return pl.pallas_call(
        body, grid=(t // tile,), in_specs=specs * 2, out_specs=pl.BlockSpec((tile, ATTN_W), lambda i: (i, 0)),
        out_shape=jax.ShapeDtypeStruct((t, ATTN_W), BF16),
        scratch_shapes=[_slabs(tile, ATTN_W)] * 4,
        compiler_params=_cparams(1), name="combine_fwd")(*os_, *ls_)


def _combine_bwd(dattn, os_, ls_):
    t = dattn.shape[0]
    tile = min(t, TILE)
    e = _head_sum_matrix()

    def body(d_ref, o0, o1, o2, l0, l1, l2, e_ref, do0, do1, do2, c0, c1, c2, *slabs):
        ov, lv = _natural_group_values((o0, o1, o2), (l0, l1, l2), slabs)
        alphas = _alphas(*lv)
        d = d_ref[...]
        attn = alphas[0] * ov[0] + alphas[1] * ov[1] + alphas[2] * ov[2]
        s = _group_sum(d * attn, e_ref[...])
        do0[0] = (alphas[0] * d).astype(BF16)
        c0[0] = -alphas[0] * s
        for g, do_ref, c_ref in ((1, do1, c1), (2, do2, c2)):
            _group_from_natural(slabs[2 * g - 2], do_ref, alphas[g] * d)
            _group_from_natural(slabs[2 * g - 1], c_ref, -alphas[g] * s)

    specs = [_group_spec(d, tile, ATTN_W) for d in DILATIONS]
    shapes = [(d, t // d, ATTN_W) for d in DILATIONS]
    outs = pl.pallas_call(
        body, grid=(t // tile,),
        in_specs=[pl.BlockSpec((tile, ATTN_W), lambda i: (i, 0))] + specs * 2 + [_full((ATTN_W, ATTN_W))],
        out_specs=specs * 2,
        out_shape=[jax.ShapeDtypeStruct(s, BF16) for s in shapes] + [jax.ShapeDtypeStruct(s, F32) for s in shapes],
        scratch_shapes=[_slabs(tile, ATTN_W)] * 4,
        compiler_params=_cparams(1), name="combine_bwd")(dattn, *os_, *ls_, e)
    return outs[:3], outs[3:]


def _attn_bwd(qkv, do, cc, lse, cos_t, sin_t, g, dil):
    t = qkv.shape[0]
    length = t // dil
    nb = length // BLK
    per_step = min(nb, ATTN_BLOCKS_PER_STEP)
    nsteps = nb // per_step
    rows_per_step = per_step * BLK
    qkv_v = qkv.reshape(dil, length, GROUP_COLS)
    cos_v, sin_v = (a.reshape(dil, length, LANES) for a in (cos_t, sin_t))
    scale = HEAD_DIM ** -0.5
    dq_cols, dk_cols, dv_cols = (slice(i * ATTN_W, (i + 1) * ATTN_W) for i in range(3))

    def body(q_ref, kc_ref, kp_ref, vc_ref, vp_ref, do_ref, c_ref, l_ref, cosc, sinc, cosp, sinp,
             out_ref, acc, kwin, vwin, cwin, swin):
        n = pl.program_id(1)

        def one_block(b):
            valid, upper = _attn_masks(n * per_step + b)
            start = b * BLK if isinstance(b, int) else pl.multiple_of(b * BLK, BLK)
            rows, before, window = pl.ds(start, BLK), pl.ds(start, BLK), pl.ds(start, 2 * BLK)
            own = pl.ds(start + BLK, BLK)
            dq_parts, dkp_parts, dkc_parts, dvp_parts, dvc_parts = [], [], [], [], []
            for p in range(ATTN_W // LANES):
                sl = slice(p * LANES, (p + 1) * LANES)
                qs = _stack_heads(q_ref[rows, sl], upper)
                dos = _stack_heads(do_ref[rows, sl], upper)
                k2 = kwin[window, sl]
                l_col = _spread_heads(l_ref[rows, sl], upper)
                c_col = _spread_heads(c_ref[rows, sl], upper)
                s = lax.dot_general(qs, k2, (NT, ((), ())), preferred_element_type=F32)
                pe = jnp.exp(jnp.where(valid, s, NEG) - l_col)
                dpv = lax.dot_general(dos, vwin[window, sl], (NT, ((), ())), preferred_element_type=F32)
                ds = (pe * (dpv + c_col)).astype(BF16)
                dq2 = _unstack_heads(jnp.dot(ds, k2, preferred_element_type=F32), upper)
                dk2 = lax.dot_general(ds, qs, (TN, ((), ())), preferred_element_type=F32)
                dv2 = lax.dot_general(pe.astype(BF16), dos, (TN, ((), ())), preferred_element_type=F32)
                dq_parts.append(dq2)
                dkp_parts.append(dk2[:BLK])
                dkc_parts.append(dk2[BLK:])
                dvp_parts.append(dv2[:BLK])
                dvc_parts.append(dv2[BLK:])
            dq = _rope(jnp.concatenate(dq_parts, axis=1) * scale, cwin[own, :], swin[own, :])
            dkc = _rope(jnp.concatenate(dkc_parts, axis=1), cwin[own, :], swin[own, :])
            dkp = _rope(jnp.concatenate(dkp_parts, axis=1), cwin[before, :], swin[before, :])
            return dq, dkp, dkc, jnp.concatenate(dvp_parts, axis=1), jnp.concatenate(dvc_parts, axis=1)

        @pl.when(n < nsteps)
        def _():
            kwin[0:BLK] = kp_ref[...]
            kwin[BLK:] = kc_ref[...]
            vwin[0:BLK] = vp_ref[...]
            vwin[BLK:] = vc_ref[...]
            cwin[0:BLK] = cosp[...]
            cwin[BLK:] = cosc[...]
            swin[0:BLK] = -sinp[...]
            swin[BLK:] = -sinc[...]
            dq, dkp, dkc, dvp, dvc = one_block(0)
            last = slice(rows_per_step - BLK, rows_per_step)

            @pl.when(n > 0)
            def _():
                if per_step > 1:
                    out_ref[0:rows_per_step - BLK, :] = acc[0:rows_per_step - BLK, :].astype(BF16)
                out_ref[last, dq_cols] = acc[last, dq_cols].astype(BF16)
                out_ref[last, dk_cols] = (acc[last, dk_cols] + dkp).astype(BF16)
                out_ref[last, dv_cols] = (acc[last, dv_cols] + dvp).astype(BF16)

            acc[0:BLK, dq_cols] = dq
            acc[0:BLK, dk_cols] = dkc
            acc[0:BLK, dv_cols] = dvc

            def later(b, carry):
                dq, dkp, dkc, dvp, dvc = one_block(b)
                start = pl.multiple_of(b * BLK, BLK)
                before, rows = pl.ds(start - BLK, BLK), pl.ds(start, BLK)
                acc[before, dk_cols] += dkp
                acc[before, dv_cols] += dvp
                acc[rows, dq_cols] = dq
                acc[rows, dk_cols] = dkc
                acc[rows, dv_cols] = dvc
                return carry

            lax.fori_loop(1, per_step, later, 0)

        @pl.when(n == nsteps)
        def _():
            out_ref[...] = acc[...].astype(BF16)

    nc = lambda n: jnp.minimum(n, nsteps - 1)
    npv = lambda n: jnp.maximum(jnp.minimum(n, nsteps - 1) * per_step - 1, 0)
    cur = lambda part: pl.BlockSpec((None, rows_per_step, ATTN_W), lambda r, n: (r, nc(n), part))
    prev = lambda part: pl.BlockSpec((None, BLK, ATTN_W), lambda r, n: (r, npv(n), part))
    row = pl.BlockSpec((None, rows_per_step, ATTN_W), lambda r, n: (r, nc(n), 0))
    tab_c = pl.BlockSpec((None, rows_per_step, LANES), lambda r, n: (r, nc(n), 0))
    tab_p = pl.BlockSpec((None, BLK, LANES), lambda r, n: (r, npv(n), 0))
    out_spec = pl.BlockSpec((None, rows_per_step, GROUP_COLS), lambda r, n: (r, jnp.maximum(n - 1, 0), 0))
    out = pl.pallas_call(
        body, grid=(dil, nsteps + 1),
        in_specs=[cur(0), cur(1), prev(1), cur(2), prev(2), row, row, row, tab_c, tab_c, tab_p, tab_p],
        out_specs=out_spec,
        out_shape=jax.ShapeDtypeStruct((dil, length, GROUP_COLS), BF16),
        scratch_shapes=[pltpu.VMEM((rows_per_step, GROUP_COLS), F32)]
        + [pltpu.VMEM((rows_per_step + BLK, ATTN_W), BF16)] * 2 + [pltpu.VMEM((rows_per_step + BLK, LANES), F32)] * 2,
        compiler_params=_cparams(2), name=f"attn_bwd_g{g}")(
            qkv_v, qkv_v, qkv_v, qkv_v, qkv_v, do, cc, lse, cos_v, sin_v, cos_v, sin_v)
    return out.reshape(t, GROUP_COLS)


SQRT_HALF = 0.7071067811865476
INV_SQRT_2PI = 0.3989422804014327


def _sgu_core(uv, g, b, w_ref, bias):
    cdf = 0.5 * (1.0 + lax.erf(uv * SQRT_HALF))
    z = uv * cdf
    u, v = z[:, :SGU_W], z[:, SGU_W:]
    mu = jnp.mean(v, axis=1, keepdims=True)
    xc = v - mu
    rs = lax.rsqrt(jnp.mean(xc * xc, axis=1, keepdims=True) + EPS)
    xhat = xc * rs
    vn = xhat * g + b
    row = lax.broadcasted_iota(jnp.int32, (SGU_CHUNK, SGU_CHUNK), 0)
    col = lax.broadcasted_iota(jnp.int32, (SGU_CHUNK, SGU_CHUNK), 1)
    tril = row >= col
    upper = lax.broadcasted_iota(jnp.int32, (SGU_CHUNK, LANES), 1) >= SGU_W // SGU_GROUPS
    ws, vlo, vhi, mixed = [], [], [], []
    for pr in range(SGU_W // LANES):
        sl = slice(pr * LANES, (pr + 1) * LANES)
        w0 = jnp.where(tril, w_ref[2 * pr], 0.0).astype(BF16)
        w1 = jnp.where(tril, w_ref[2 * pr + 1], 0.0).astype(BF16)
        vn2 = vn[:, sl]
        lo = jnp.where(upper, 0.0, vn2).astype(BF16)
        hi = jnp.where(upper, vn2, 0.0).astype(BF16)
        mixed.append(jnp.dot(w0, lo, preferred_element_type=F32) + jnp.dot(w1, hi, preferred_element_type=F32)
                     + bias[:, sl])
        ws.append((w0, w1))
        vlo.append(lo)
        vhi.append(hi)
    return cdf, u, xhat, rs, jnp.concatenate(mixed, axis=1), ws, vlo, vhi, tril, upper


SGU_STEP = 4 * SGU_CHUNK


def _for_chunks(step_rows, fn):
    def one(ci, carry):
        fn(pl.ds(pl.multiple_of(ci * SGU_CHUNK, SGU_CHUNK), SGU_CHUNK))
        return carry

    lax.fori_loop(0, step_rows // SGU_CHUNK, one, 0)


def _sgu_fwd(gu, ln_g, ln_b, w_s, bias_exp):
    t = gu.shape[0]
    step = min(t, SGU_STEP)

    def body(uv_ref, g_ref, b_ref, w_ref, bias_ref, o_ref):
        def chunk(rows):
            _, u, _, _, mixed, *_ = _sgu_core(uv_ref[rows, :].astype(F32), g_ref[...], b_ref[...], w_ref, bias_ref[...])
            o_ref[rows, :] = (u * mixed).astype(BF16)

        _for_chunks(step, chunk)

    return pl.pallas_call(
        body, grid=(t // step,),
        in_specs=[pl.BlockSpec((step, 2 * SGU_W), lambda n: (n, 0)), _full((1, SGU_W)), _full((1, SGU_W)),
                  _full((SGU_GROUPS, SGU_CHUNK, SGU_CHUNK)), _full((SGU_CHUNK, SGU_W))],
        out_specs=pl.BlockSpec((step, SGU_W), lambda n: (n, 0)),
        out_shape=jax.ShapeDtypeStruct((t, SGU_W), BF16),
        compiler_params=_cparams(1), name="sgu_fwd")(gu, ln_g, ln_b, w_s, bias_exp)


def _sgu_bwd(dproj, gu, dsgu, ln_g, ln_b, w_s, bias_exp):
    t = gu.shape[0]
    step = min(t, SGU_STEP)
    nsteps = t // step
    e = _head_sum_matrix()

    def body(dp_in, uv_ref, ds_ref, g_ref, b_ref, w_ref, bias_ref, e_ref, out_ref, dw_ref, dbias_ref, dg_ref, db_ref):
        n = pl.program_id(0)

        @pl.when(n == 0)
        def _():
            dw_ref[...] = jnp.zeros(dw_ref.shape, F32)
            dbias_ref[...] = jnp.zeros(dbias_ref.shape, F32)
            dg_ref[...] = jnp.zeros(dg_ref.shape, F32)
            db_ref[...] = jnp.zeros(db_ref.shape, F32)

        _for_chunks(step, functools.partial(chunk, uv_ref, ds_ref, g_ref, b_ref, w_ref, bias_ref, out_ref, dw_ref, dbias_ref,
                                            dg_ref, db_ref))

        @pl.when(n == nsteps - 1)
        def _():
            dbias_ref[...] = _group_sum(dbias_ref[...], e_ref[...])

    def chunk(uv_ref, ds_ref, g_ref, b_ref, w_ref, bias_ref, out_ref, dw_ref, dbias_ref, dg_ref, db_ref, rows):
        uv = uv_ref[rows, :].astype(F32)
        g = g_ref[...]
        cdf, u, xhat, rs, mixed, ws, vlo, vhi, tril, upper = _sgu_core(uv, g, b_ref[...], w_ref, bias_ref[...])
        dsg = ds_ref[rows, :]
        du = dsg * mixed
        dmixed = dsg * u
        dbias_ref[...] += dmixed
        dvn = []
        for pr in range(SGU_W // LANES):
            sl = slice(pr * LANES, (pr + 1) * LANES)
            dm2 = dmixed[:, sl]
            dlo = jnp.where(upper, 0.0, dm2).astype(BF16)
            dhi = jnp.where(upper, dm2, 0.0).astype(BF16)
            w0, w1 = ws[pr]
            dvn.append(lax.dot_general(w0, dlo, (TN, ((), ())), preferred_element_type=F32)
                       + lax.dot_general(w1, dhi, (TN, ((), ())), preferred_element_type=F32))
            dw0 = lax.dot_general(dlo, vlo[pr], (NT, ((), ())), preferred_element_type=F32)
            dw1 = lax.dot_general(dhi, vhi[pr], (NT, ((), ())), preferred_element_type=F32)
            dw_ref[2 * pr] += jnp.where(tril, dw0, 0.0)
            dw_ref[2 * pr + 1] += jnp.where(tril, dw1, 0.0)
        dvn = jnp.concatenate(dvn, axis=1)
        dg_ref[...] += jnp.sum(dvn * xhat, axis=0, keepdims=True)
        db_ref[...] += jnp.sum(dvn, axis=0, keepdims=True)
        dxh = dvn * g
        dv = rs * (dxh - jnp.mean(dxh, axis=1, keepdims=True) - xhat * jnp.mean(dxh * xhat, axis=1, keepdims=True))
        dz = jnp.concatenate([du, dv], axis=1)
        dgelu = cdf + uv * (INV_SQRT_2PI * jnp.exp(-0.5 * uv * uv))
        out_ref[rows, :] = (dz * dgelu).astype(BF16)

    outs = pl.pallas_call(
        body, grid=(nsteps,),
        in_specs=[pl.BlockSpec(memory_space=pl.ANY), pl.BlockSpec((step, 2 * SGU_W), lambda n: (n, 0)),
                  pl.BlockSpec((step, SGU_W), lambda n: (n, 0)), _full((1, SGU_W)), _full((1, SGU_W)),
                  _full((SGU_GROUPS, SGU_CHUNK, SGU_CHUNK)), _full((SGU_CHUNK, SGU_W)), _full((ATTN_W, ATTN_W))],
        out_specs=[pl.BlockSpec((step, 2 * SGU_W), lambda n: (n, 0)), _full((SGU_GROUPS, SGU_CHUNK, SGU_CHUNK)),
                   _full((SGU_CHUNK, SGU_W)), _full((1, SGU_W)), _full((1, SGU_W))],
        out_shape=[jax.ShapeDtypeStruct(dproj.shape, BF16), jax.ShapeDtypeStruct((SGU_GROUPS, SGU_CHUNK, SGU_CHUNK), F32),
                   jax.ShapeDtypeStruct((SGU_CHUNK, SGU_W), F32), jax.ShapeDtypeStruct((1, SGU_W), F32),
                   jax.ShapeDtypeStruct((1, SGU_W), F32)],
        input_output_aliases={0: 0},
        compiler_params=_cparams(1), name="sgu_bwd")(dproj, gu, dsgu, ln_g, ln_b, w_s, bias_exp, e)
    return outs


def _merge_fwd(attn, sgu, gu, x, w_pa, w_ps, w_out, g2):
    t = x.shape[0]
    tm = min(t, 512)

    def body(a_ref, s_ref, ga_ref, gb_ref, x_ref, wpa, wps, wo, g_ref, pa_ref, ps_ref, m_ref, x1_ref, h2_ref):
        pa = jnp.dot(a_ref[...], wpa[...], preferred_element_type=F32)
        ps = jnp.dot(s_ref[...], wps[...], preferred_element_type=F32)
        merged = (_sigmoid(ga_ref[...].astype(F32)) * pa + _sigmoid(gb_ref[...].astype(F32)) * ps).astype(BF16)
        x1 = x_ref[...] + jnp.dot(merged, wo[...], preferred_element_type=F32)
        xhat, _ = _rms_stats(x1)
        pa_ref[...] = pa.astype(BF16)
        ps_ref[...] = ps.astype(BF16)
        m_ref[...] = merged
        x1_ref[...] = x1
        h2_ref[...] = (xhat * g_ref[...]).astype(BF16)

    half = pl.BlockSpec((tm, ATTN_W), lambda i: (i, 0))
    full = pl.BlockSpec((tm, D_MODEL), lambda i: (i, 0))
    return pl.pallas_call(
        body, grid=(t // tm,),
        in_specs=[half, half, pl.BlockSpec((tm, D_MODEL), lambda i: (i, 1)), pl.BlockSpec((tm, D_MODEL), lambda i: (i, 2)),
                  full, _resident((ATTN_W, D_MODEL)), _resident((SGU_W, D_MODEL)), _resident((D_MODEL, D_MODEL)),
                  _full((1, D_MODEL))],
        out_specs=[full] * 5,
        out_shape=[jax.ShapeDtypeStruct((t, D_MODEL), BF16), jax.ShapeDtypeStruct((t, D_MODEL), BF16),
                   jax.ShapeDtypeStruct((t, D_MODEL), BF16), jax.ShapeDtypeStruct((t, D_MODEL), F32),
                   jax.ShapeDtypeStruct((t, D_MODEL), BF16)],
        compiler_params=_cparams(1), name="merge_fwd")(attn, sgu, gu, gu, x, w_pa, w_ps, w_out, g2)


def _merge_bwd(dx1b, gu, pa, ps, w_pa, w_ps, w_out):
    t = dx1b.shape[0]
    tm = min(t, 512)

    def body(d_ref, ga_ref, gb_ref, pa_ref, ps_ref, wpa, wps, wo, out_ref, dpa_ref, dps_ref, da_ref, dsg_ref):
        dm = lax.dot_general(d_ref[...], wo[...], (NT, ((), ())), preferred_element_type=F32)
        sa, sb = _sigmoid(ga_ref[...].astype(F32)), _sigmoid(gb_ref[...].astype(F32))
        dpa = (dm * sa).astype(BF16)
        dps = (dm * sb).astype(BF16)
        out_ref[:, 0:D_MODEL] = jnp.zeros((tm, D_MODEL), BF16)
        out_ref[:, D_MODEL:2 * D_MODEL] = (dm * pa_ref[...].astype(F32) * sa * (1.0 - sa)).astype(BF16)
        out_ref[:, 2 * D_MODEL:GU_COLS] = (dm * ps_ref[...].astype(F32) * sb * (1.0 - sb)).astype(BF16)
        dpa_ref[...] = dpa
        dps_ref[...] = dps
        da_ref[...] = lax.dot_general(dpa, wpa[...], (NT, ((), ())), preferred_element_type=F32)
        dsg_ref[...] = lax.dot_general(dps, wps[...], (NT, ((), ())), preferred_element_type=F32)

    half = pl.BlockSpec((tm, ATTN_W), lambda i: (i, 0))
    full = pl.BlockSpec((tm, D_MODEL), lambda i: (i, 0))
    return pl.pallas_call(
        body, grid=(t // tm,),
        in_specs=[full, pl.BlockSpec((tm, D_MODEL), lambda i: (i, 1)),
                  pl.BlockSpec((tm, D_MODEL), lambda i: (i, 2)), full, full,
                  _resident((ATTN_W, D_MODEL)), _resident((SGU_W, D_MODEL)), _resident((D_MODEL, D_MODEL))],
        out_specs=[pl.BlockSpec((tm, GU_COLS), lambda i: (i, 0)), full, full, half, half],
        out_shape=[jax.ShapeDtypeStruct((t, GU_COLS), BF16), jax.ShapeDtypeStruct((t, D_MODEL), BF16),
                   jax.ShapeDtypeStruct((t, D_MODEL), BF16), jax.ShapeDtypeStruct((t, ATTN_W), F32),
                   jax.ShapeDtypeStruct((t, SGU_W), F32)],
        compiler_params=_cparams(1), name="merge_bwd")(dx1b, gu, gu, pa, ps, w_pa, w_ps, w_out)


def _token_call(name, body, t, tm, ins, outs, reds=(), scratch=()):
    return pl.pallas_call(
        body, grid=(t // tm,), in_specs=[s for _, s in ins],
        out_specs=[o[2] for o in outs] + [_full(r) for r in reds],
        out_shape=[jax.ShapeDtypeStruct(o[0], o[1]) for o in outs] + [jax.ShapeDtypeStruct(r, F32) for r in reds],
        scratch_shapes=list(scratch), compiler_params=_cparams(1), name=name)(*[a for a, _ in ins])


def _rows_spec(tm, width):
    return pl.BlockSpec((tm, width), lambda i: (i, 0))


def _chips_spec(tm):
    return pl.BlockSpec((N_CHIPS, tm, FF_SHARD), lambda i: (0, i, 0))


def _zero_at_start(*refs):
    @pl.when(pl.program_id(0) == 0)
    def _():
        for r in refs:
            r[...] = jnp.zeros(r.shape, r.dtype)


def _ffn_fwd(h2, w_g, w_u):
    t = h2.shape[0]
    tm = min(t, 512)

    def body(h_ref, wg_ref, wu_ref, a_ref, b_ref, ff_ref):
        h = h_ref[...]
        for s in range(N_CHIPS):
            a = jnp.dot(h, wg_ref[s], preferred_element_type=F32)
            b = jnp.dot(h, wu_ref[s], preferred_element_type=F32)
            a_ref[s] = a.astype(BF16)
            b_ref[s] = b.astype(BF16)
            ff_ref[s] = (a * _sigmoid(a) * b).astype(BF16)

    shp = (N_CHIPS, t, FF_SHARD)
    w_spec = _resident((N_CHIPS, D_MODEL, FF_SHARD))
    return _token_call("ffn_fwd", body, t, tm, [(h2, _rows_spec(tm, D_MODEL)), (w_g, w_spec), (w_u, w_spec)],
                       [(shp, BF16, _chips_spec(tm))] * 3)


def _ffn_down_loss(ff, w_d, x1, tgt, gf):
    t = x1.shape[0]
    tm = min(t, 512)

    def body(ff_ref, wd_ref, x1_ref, tgt_ref, g_ref, dx2_ref, dx2b_ref, loss_ref, dgf_ref):
        _zero_at_start(loss_ref, dgf_ref)
        acc = jnp.dot(ff_ref[0], wd_ref[0], preferred_element_type=F32)
        for s in range(1, N_CHIPS):
            acc = acc + jnp.dot(ff_ref[s], wd_ref[s], preferred_element_type=F32)
        x2 = x1_ref[...] + acc
        g = g_ref[...]
        xhat, rr = _rms_stats(x2)
        diff = xhat * g - tgt_ref[...]
        rows = jnp.sum(diff * diff, axis=1, keepdims=True)
        loss_ref[...] += jnp.broadcast_to(jnp.sum(rows, axis=0, keepdims=True) * (0.5 / D_MODEL), (1, LANES))
        dy = diff * (1.0 / D_MODEL)
        dgf_ref[...] += jnp.sum(dy * xhat, axis=0, keepdims=True)
        dx2 = _rms_bwd(dy, xhat, rr, g)
        dx2_ref[...] = dx2
        dx2b_ref[...] = dx2.astype(BF16)

    row = _rows_spec(tm, D_MODEL)
    return _token_call("ffn_down_loss", body, t, tm,
                       [(ff, _chips_spec(tm)), (w_d, _resident((N_CHIPS, FF_SHARD, D_MODEL))), (x1, row), (tgt, row),
                        (gf, _full((1, D_MODEL)))],
                       [((t, D_MODEL), F32, row), ((t, D_MODEL), BF16, row)], reds=[(1, LANES), (1, D_MODEL)])


def _ffn_bwd_act(dx2b, w_d, a, b):
    t = dx2b.shape[0]
    tm = min(t, 512)

    def body(d_ref, wd_ref, a_ref, b_ref, da_ref, db_ref):
        d = d_ref[...]
        for s in range(N_CHIPS):
            dff = lax.dot_general(d, wd_ref[s], (NT, ((), ())), preferred_element_type=F32)
            av, bv = a_ref[s].astype(F32), b_ref[s].astype(F32)
            sg = _sigmoid(av)
            da_ref[s] = (dff * bv * (sg * (1.0 + av * (1.0 - sg)))).astype(BF16)
            db_ref[s] = (dff * (av * sg)).astype(BF16)

    shp = (N_CHIPS, t, FF_SHARD)
    return _token_call("ffn_bwd_act", body, t, tm,
                       [(dx2b, _rows_spec(tm, D_MODEL)), (w_d, _resident((N_CHIPS, FF_SHARD, D_MODEL))),
                        (a, _chips_spec(tm)), (b, _chips_spec(tm))],
                       [(shp, BF16, _chips_spec(tm))] * 2)


def _ffn_bwd_in(da, db, w_g, w_u, x1, dx2, g2):
    t = x1.shape[0]
    tm = min(t, 512)

    def body(da_ref, db_ref, wg_ref, wu_ref, x1_ref, dx2_ref, g_ref, dx1_ref, dx1b_ref, dg_ref):
        _zero_at_start(dg_ref)
        acc = None
        for s in range(N_CHIPS):
            part = (lax.dot_general(da_ref[s], wg_ref[s], (NT, ((), ())), preferred_element_type=F32)
                    + lax.dot_general(db_ref[s], wu_ref[s], (NT, ((), ())), preferred_element_type=F32))
            acc = part if acc is None else acc + part
        xhat, rr = _rms_stats(x1_ref[...])
        dg_ref[...] += jnp.sum(acc * xhat, axis=0, keepdims=True)
        dx1 = dx2_ref[...] + _rms_bwd(acc, xhat, rr, g_ref[...])
        dx1_ref[...] = dx1
        dx1b_ref[...] = dx1.astype(BF16)

    row = _rows_spec(tm, D_MODEL)
    w_spec = _resident((N_CHIPS, D_MODEL, FF_SHARD))
    return _token_call("ffn_bwd_in", body, t, tm,
                       [(da, _chips_spec(tm)), (db, _chips_spec(tm)), (w_g, w_spec), (w_u, w_spec), (x1, row), (dx2, row),
                        (g2, _full((1, D_MODEL)))],
                       [((t, D_MODEL), F32, row), ((t, D_MODEL), BF16, row)], reds=[(1, D_MODEL)])


def _group_dh(d, w_refs):
    dh = None
    for part, w_ref in enumerate(w_refs):
        term = lax.dot_general(d[:, part * ATTN_W:(part + 1) * ATTN_W], w_ref[...], (NT, ((), ())),
                               preferred_element_type=F32)
        dh = term if dh is None else dh + term
    return dh


def _in_proj_bwd(dgu, dqkvs, w_in, x, dx1, g1):
    t = x.shape[0]
    tile = min(t, TILE)
    ngroups = len(DILATIONS)

    def body(*refs):
        dgu_ref, dq_refs = refs[0], refs[1:1 + ngroups]
        w0_ref, w1_ref = refs[1 + ngroups:3 + ngroups]
        wg_refs = [refs[3 + ngroups + 3 * g:6 + ngroups + 3 * g] for g in range(ngroups)]
        x_ref, dx1_ref, g_ref, dx_ref, dg_ref, slab = refs[3 + 4 * ngroups:]
        _zero_at_start(dg_ref)
        dh = lax.dot_general(dgu_ref[:, 0:GU_HALF], w0_ref[...], (NT, ((), ())), preferred_element_type=F32)
        dh = dh + lax.dot_general(dgu_ref[:, GU_HALF:], w1_ref[...], (NT, ((), ())), preferred_element_type=F32)
        dh = dh + _group_dh(dq_refs[0][0], wg_refs[0])
        for g in range(1, ngroups):
            dil = DILATIONS[g]
            part = _group_dh(dq_refs[g][...].reshape(tile, GROUP_COLS), wg_refs[g])
            for r in range(dil):
                _put_class_rows(slab, r, dil, part[r * (tile // dil):(r + 1) * (tile // dil)])
            dh = dh + _from_slabs(slab)
        xhat, rr = _rms_stats(x_ref[...])
        dg_ref[...] += jnp.sum(dh * xhat, axis=0, keepdims=True)
        dx_ref[...] = dx1_ref[...] + _rms_bwd(dh, xhat, rr, g_ref[...])

    row = _rows_spec(tile, D_MODEL)
    group_ins = [(dqkvs[g].reshape(d, t // d, GROUP_COLS), _group_spec(d, tile, GROUP_COLS)) for g, d in enumerate(DILATIONS)]
    w_specs = _gu_w_specs() + [s for g in range(ngroups) for s in _group_w_specs(g)]
    return _token_call(
        "in_proj_bwd", body, t, tile,
        [(dgu, _rows_spec(tile, GU_COLS))] + group_ins + [(w_in, s) for s in w_specs]
        + [(x, row), (dx1, row), (g1, _full((1, D_MODEL)))],
        [((t, D_MODEL), F32, row)], reds=[(1, D_MODEL)], scratch=[_slabs(tile, D_MODEL)])


def _epi_bf16(acc, e, o, r, ids):
    o[0][...] = acc.astype(BF16)


WGRAD_TK = 2048


def _wgrad_2d(name, a, b, tm, tn):
    t, k1 = a.shape
    n = b.shape[1]
    tk = min(t, WGRAD_TK)
    return _mm(name, (k1 // tm, n // tn, t // tk),
               [(a, pl.BlockSpec((tk, tm), lambda i, j, k: (k, i)), b, pl.BlockSpec((tk, tn), lambda i, j, k: (k, j)))],
               TN, (tm, tn), _epi_bf16, outs=[((k1, n), BF16, pl.BlockSpec((tm, tn), lambda i, j, k: (i, j)))])[0]


def _wgrad_in(hs, dgu, dqkvs):
    t = dgu.shape[0]
    tk = min(t, WGRAD_TK)
    gu_block = QKV_BLOCKS * ATTN_W // GU_HALF
    parts = [(hs[0], dgu, GU_HALF, lambda j: j + gu_block)]
    parts += [(hs[g].reshape(t, D_MODEL), dqkvs[g], ATTN_W, lambda j, g=g: _w_in_block(j, g)) for g in range(3)]
    dst = None
    for n, (a, b, tn, block_of) in enumerate(parts):
        dst = _mm(f"wgrad_in_{n}", (1, b.shape[1] // tn, t // tk),
                  [(a, pl.BlockSpec((tk, D_MODEL), lambda i, j, k: (k, 0)), b,
                    pl.BlockSpec((tk, tn), lambda i, j, k: (k, j)))],
                  TN, (D_MODEL, tn), _epi_bf16,
                  extras=[] if dst is None else [(dst, pl.BlockSpec(memory_space=pl.ANY))],
                  outs=[((D_MODEL, IN_COLS), BF16,
                         pl.BlockSpec((D_MODEL, tn), lambda i, j, k, block_of=block_of: (0, block_of(j))))],
                  aliases=None if dst is None else {2: 0})[0]
    return dst


def _wgrad_ff_in(name, h2, da):
    t = h2.shape[0]
    tk = min(t, WGRAD_TK)
    return _mm(name, (N_CHIPS, 1, t // tk),
               [(h2, pl.BlockSpec((tk, D_MODEL), lambda i, j, k: (k, 0)),
                 da, pl.BlockSpec((None, tk, FF_SHARD), lambda i, j, k: (i, k, 0)))],
               TN, (D_MODEL, FF_SHARD), _epi_bf16,
               outs=[((N_CHIPS, D_MODEL, FF_SHARD), BF16, pl.BlockSpec((None, D_MODEL, FF_SHARD), lambda i, j, k: (i, 0, 0)))])[0]


def _wgrad_ff_down(ff, dx2b):
    t = dx2b.shape[0]
    tk = min(t, WGRAD_TK)
    return _mm("wgrad_ffn_down", (N_CHIPS, 1, t // tk),
               [(ff, pl.BlockSpec((None, tk, FF_SHARD), lambda i, j, k: (i, k, 0)),
                 dx2b, pl.BlockSpec((tk, D_MODEL), lambda i, j, k: (k, 0)))],
               TN, (FF_SHARD, D_MODEL), _epi_bf16,
               outs=[((N_CHIPS, FF_SHARD, D_MODEL), BF16, pl.BlockSpec((None, FF_SHARD, D_MODEL), lambda i, j, k: (i, 0, 0)))])[0]


def _local_step(x, pos_col, tgt, g1, ln_g, ln_b, w_s, b_s, g2, gf, first_weight, late_weights, on_grads=None):
    tables = _rope_tables(pos_col)
    bias_exp = jnp.repeat(jnp.transpose(b_s), SGU_W // SGU_GROUPS, axis=1)

    hs = _norm_fwd(x, g1)
    w_p = first_weight(hs[0])
    gu, qkvs = _in_proj(hs, w_p, tables)
    os_, ls_ = [], []
    for g, dil in enumerate(DILATIONS):
        o, lse = _attn_fwd(qkvs[g], g, dil)
        os_.append(o)
        ls_.append(lse)
    attn = _combine_fwd(os_, ls_)
    sgu = _sgu_fwd(gu, ln_g, ln_b, w_s, bias_exp)
    w_pa, w_ps, w_out, w_g, w_u, w_d = late_weights(attn)
    pa, ps, merged, x1, h2 = _merge_fwd(attn, sgu, gu, x, w_pa, w_ps, w_out, g2)
    a, b, ff = _ffn_fwd(h2, w_g, w_u)
    dx2, dx2b, loss, dgf = _ffn_down_loss(ff, w_d, x1, tgt, gf)

    da, db = _ffn_bwd_act(dx2b, w_d, a, b)
    dw_d = _wgrad_ff_down(ff, dx2b)
    dx1, dx1b, dg2 = _ffn_bwd_in(da, db, w_g, w_u, x1, dx2, g2)
    dw_g = _wgrad_ff_in("wgrad_ffn_gate", h2, da)
    dw_u = _wgrad_ff_in("wgrad_ffn_up", h2, db)

    dgu, dpa, dps, dattn, dsgu = _merge_bwd(dx1b, gu, pa, ps, w_pa, w_ps, w_out)
    dw_out = _wgrad_2d("wgrad_out", merged, dx1b, D_MODEL, D_MODEL)
    dw_pa = _wgrad_2d("wgrad_proj_attn", attn, dpa, ATTN_W, D_MODEL)
    dw_ps = _wgrad_2d("wgrad_proj_sgu", sgu, dps, SGU_W, D_MODEL)
    if on_grads is not None:
        ln_g = ln_g + on_grads(1, dict(w_proj_attn=dw_pa, w_proj_sgu=dw_ps, w_out=dw_out, w_ffn_gate=dw_g, w_ffn_up=dw_u,
                                       w_ffn_down=dw_d))[:, :SGU_W]
    dgu, dw_s, dbias, dln_g, dln_b = _sgu_bwd(dgu, gu, dsgu, ln_g, ln_b, w_s, bias_exp)
    dos, ccs = _combine_bwd(dattn, os_, ls_)
    dqkvs = [_attn_bwd(qkvs[g], dos[g], ccs[g], ls_[g], *tables[g], g, dil) for g, dil in enumerate(DILATIONS)]
    dw_p = _wgrad_in(hs, dgu, dqkvs)
    if on_grads is not None:
        g1 = g1 + on_grads(0, dict(w_in=dw_p))
    dx, dg1 = _in_proj_bwd(dgu, dqkvs, w_p, x, dx1, g1)

    db_s = jnp.transpose(dbias[:, ::SGU_W // SGU_GROUPS])
    small = dict(loss=loss, norm1_g=dg1, sgu_ln_g=dln_g, sgu_ln_b=dln_b, w_spatial=dw_s, b_spatial=db_s,
                 norm2_g=dg2, final_g=dgf)
    big = dict(w_in=dw_p, w_proj_attn=dw_pa, w_proj_sgu=dw_ps, w_out=dw_out, w_ffn_gate=dw_g, w_ffn_up=dw_u,
               w_ffn_down=dw_d)
    return dx, big, small


def _ew(name, fn, ins, out_dtypes):
    shp = ins[0].shape
    rows, cols = shp
    tr = next((cand for cand in (256, 352, 128) if rows % cand == 0 and rows > cand), rows)

    def body(*refs):
        res = fn(*[r[...] for r in refs[:len(ins)]])
        for o_ref, v in zip(refs[len(ins):], res):
            o_ref[...] = v.astype(o_ref.dtype)

    spec = pl.BlockSpec((tr, cols), lambda i: (i, 0))
    return pl.pallas_call(
        body, grid=(rows // tr,), in_specs=[spec] * len(ins), out_specs=[spec] * len(out_dtypes),
        out_shape=[jax.ShapeDtypeStruct(shp, d) for d in out_dtypes],
        compiler_params=_cparams(1), name=name)(*ins)


def _adamw_math(g, w, m, v):
    m = ADAM_B1 * m + (1.0 - ADAM_B1) * g
    v = ADAM_B2 * v + (1.0 - ADAM_B2) * (g * g)
    m_hat = m / (1.0 - ADAM_B1 ** ADAM_STEP)
    v_hat = v / (1.0 - ADAM_B2 ** ADAM_STEP)
    delta = -ADAM_LR * (m_hat / (jnp.sqrt(v_hat) + ADAM_EPS) + ADAM_WD * w)
    return delta, m, v


def _adamw(name, g, w, m, v):
    return _ew(name, lambda g_, w_, m_, v_: (g_,) + _adamw_math(g_, w_, m_, v_), [g, w, m, v], [F32] * 4)


VMEM_SPEC = pl.BlockSpec(memory_space=pltpu.VMEM)


def _for_row_chunks(rows, fn):
    ck = next(c for c in (64, 32, 16) if rows % c == 0)

    def step(i, carry):
        fn(pl.multiple_of(i * ck, ck), ck)
        return carry

    lax.fori_loop(0, rows // ck, step, 0)


def _place():
    x, y, c = lax.axis_index("x"), lax.axis_index("y"), lax.axis_index("c")
    chips = [(1 - x, y), (x, 1 - y), (1 - x, 1 - y)]
    return x, y, c, 2 * x + y, chips


def _rows(ref, start, size):
    if len(ref.shape) == 2:
        return ref.at[pl.ds(start, size), :]
    return ref.at[:, pl.ds(start, size), :]


def _comm_call(name, body, ins, out_shapes, scratch, n_remote):
    return pl.pallas_call(
        body, in_specs=[VMEM_SPEC] * len(ins), out_specs=[VMEM_SPEC] * len(out_shapes),
        out_shape=out_shapes,
        scratch_shapes=list(scratch) + [pltpu.SemaphoreType.DMA((n_remote,)), pltpu.SemaphoreType.DMA((n_remote,))],
        compiler_params=pltpu.CompilerParams(vmem_limit_bytes=VMEM_LIMIT), name=name)(*ins)


def _gather_finish(name, shard, landed):
    k_rows, n = shard.shape
    kh = k_rows // 2

    def body(shard_ref, land_ref, out_ref, send, recv):
        x, y, c, me, chips = _place()
        passed = []
        for j, chip in enumerate(chips):
            theirs = 2 * chip[0] + chip[1]
            cp = pltpu.make_async_remote_copy(
                src_ref=land_ref.at[j], dst_ref=_rows(out_ref.at[theirs], c * kh, kh), send_sem=send.at[j],
                recv_sem=recv.at[j], device_id=(x, y, 1 - c), device_id_type=MESH)
            cp.start()
            passed.append(cp)
        mine = out_ref.at[me]

        def put_own(r0, ck):
            mine[pl.ds(r0, ck), :] = shard_ref[pl.ds(r0, ck), :]

        _for_row_chunks(k_rows, put_own)
        for j, chip in enumerate(chips):
            slot = out_ref.at[2 * chip[0] + chip[1]]

            def put_half(r0, ck, j=j, slot=slot):
                slot[pl.ds(pl.multiple_of(c * kh + r0, ck), ck), :] = land_ref[j, pl.ds(r0, ck), :]

            _for_row_chunks(kh, put_half)
        for j, chip in enumerate(chips):
            other = _rows(out_ref.at[2 * chip[0] + chip[1]], (1 - c) * kh, kh)
            pltpu.make_async_remote_copy(src_ref=other, dst_ref=other, send_sem=send.at[j], recv_sem=recv.at[j],
                                         device_id=(x, y, 1 - c), device_id_type=MESH).wait_recv()
        for cp in passed:
            cp.wait_send()

    return _comm_call(name, body, [shard, landed], [jax.ShapeDtypeStruct((N_CHIPS, k_rows, n), shard.dtype)], [], 3)[0]


HBM_SPEC = pl.BlockSpec(memory_space=pltpu.HBM)
SEM_SPEC = pl.BlockSpec(memory_space=pltpu.SEMAPHORE)
DATAFLOW = pltpu.SideEffectType.DATAFLOW_SIDE_EFFECTING
TOKEN_SHAPE = (1, D_MODEL)
N_PEERS = 7


def _peers():
    x, y, c = lax.axis_index("x"), lax.axis_index("y"), lax.axis_index("c")
    flip = lambda v, f: 1 - v if f else v
    return [(flip(x, k & 4), flip(y, k & 2), flip(c, k & 1)) for k in range(1, N_PEERS + 1)]


def _piece_shape(shape):
    return (shape[-2] // 2, shape[2] if len(shape) == 3 else shape[1] // N_CHIPS)


def _device_piece(ref, chip, core):
    kh, n4 = _piece_shape(ref.shape)
    if len(ref.shape) == 3:
        return ref.at[chip, pl.ds(core * kh, kh), :]
    return ref.at[pl.ds(core * kh, kh), pl.ds(chip * n4, n4)]


def _exchange_copies(partials, lands, send, recv):
    return [pltpu.make_async_remote_copy(
        src_ref=_device_piece(partials[t], 2 * px + py, pc), dst_ref=lands[t].at[k], send_sem=send.at[t * N_PEERS + k],
        recv_sem=recv.at[t * N_PEERS + k], device_id=(px, py, pc), device_id_type=MESH)
        for t in range(len(partials)) for k, (px, py, pc) in enumerate(_peers())]


def _gather_copies(shards, lands, send, recv):
    x, y, c, me, chips = _place()
    return [pltpu.make_async_remote_copy(
        src_ref=shards[t], dst_ref=lands[t].at[me], send_sem=send.at[t * 3 + j], recv_sem=recv.at[t * 3 + j],
        device_id=(*chip, c), device_id_type=MESH)
        for t in range(len(shards)) for j, chip in enumerate(chips)]


def _gather_half_copies(shards, lands, send, recv):
    x, y, c, me, chips = _place()
    return [pltpu.make_async_remote_copy(
        src_ref=_rows(shards[t], c * (shards[t].shape[0] // 2), shards[t].shape[0] // 2), dst_ref=lands[t].at[j],
        send_sem=send.at[t * 3 + j], recv_sem=recv.at[t * 3 + j], device_id=(*chip, c), device_id_type=MESH)
        for t in range(len(shards)) for j, chip in enumerate(chips)]


def _split_start(name, copies, per_tensor, srcs, land_shapes):
    nt = len(srcs)
    lands = [lax.empty(s, BF16) for s in land_shapes]
    nsem = nt * per_tensor

    def body(*refs):
        send, recv = refs[2 * nt], refs[2 * nt + 1]
        for cp in copies(refs[:nt], refs[nt:2 * nt], send, recv):
            cp.start()
        refs[-1][...] = jnp.zeros(TOKEN_SHAPE, F32)

    hbm = lambda a: pltpu.with_memory_space_constraint(a, pltpu.HBM)
    outs = pl.pallas_call(
        body, name=name,
        out_shape=[pltpu.SemaphoreType.DMA((nsem,)), pltpu.SemaphoreType.DMA((nsem,))]
        + [pltpu.HBM(s.shape, s.dtype) for s in srcs] + [pltpu.HBM(l.shape, l.dtype) for l in lands]
        + [jax.ShapeDtypeStruct(TOKEN_SHAPE, F32)],
        in_specs=[HBM_SPEC] * (2 * nt), out_specs=[SEM_SPEC, SEM_SPEC] + [HBM_SPEC] * (2 * nt) + [VMEM_SPEC],
        input_output_aliases={i: 2 + i for i in range(2 * nt)},
        compiler_params=pltpu.CompilerParams(has_side_effects=DATAFLOW))(*[hbm(a) for a in list(srcs) + lands])
    return outs[0], outs[1], outs[2:2 + nt], outs[2 + nt:2 + 2 * nt], outs[-1]


def _split_wait(name, copies, send, recv, srcs, lands, after):
    nt = len(srcs)

    def body(*refs):
        for cp in copies(refs[:nt], refs[nt:2 * nt], refs[2 * nt], refs[2 * nt + 1]):
            cp.wait_send()
            cp.wait_recv()

    outs = pl.pallas_call(
        body, name=name,
        out_shape=[pltpu.HBM(s.shape, s.dtype) for s in srcs] + [pltpu.HBM(l.shape, l.dtype) for l in lands],
        in_specs=[HBM_SPEC] * (2 * nt) + [SEM_SPEC, SEM_SPEC, pl.BlockSpec(memory_space=pl.ANY)],
        out_specs=[HBM_SPEC] * (2 * nt), input_output_aliases={i: i for i in range(2 * nt)},
        compiler_params=pltpu.CompilerParams(has_side_effects=DATAFLOW))(*srcs, *lands, send, recv, after)
    return outs[:nt], outs[nt:]


def _device_sum(name, partials, lands):
    nt = len(partials)

    def body(*refs):
        ins, slots, outs, owns = refs[:nt], refs[nt:2 * nt], refs[2 * nt:3 * nt], refs[3 * nt:4 * nt]
        send, recv, loc = refs[4 * nt:]
        x, y, c, me, chips = _place()
        sibling = (x, y, 1 - c)
        loads = [pltpu.make_async_copy(_device_piece(ins[t], me, c), owns[t], loc.at[t]) for t in range(nt)]
        for cp in loads:
            cp.start()
        handed = []
        for t in range(nt):
            kh = owns[t].shape[0]
            loads[t].wait()

            def add(r0, ck, own=owns[t], slot=slots[t], dst=outs[t], kh=kh):
                rows = pl.ds(r0, ck)
                acc = own[rows, :].astype(F32)
                for k in range(N_PEERS):
                    acc = acc + slot[k, rows, :].astype(F32)
                dst[pl.ds(pl.multiple_of(c * kh + r0, ck), ck), :] = acc

            _for_row_chunks(kh, add)
            rc = pltpu.make_async_remote_copy(
                src_ref=_rows(outs[t], c * kh, kh), dst_ref=_rows(outs[t], c * kh, kh), send_sem=send.at[t],
                recv_sem=recv.at[t], device_id=sibling, device_id_type=MESH)
            rc.start()
            handed.append(rc)
        for t in range(nt):
            kh = owns[t].shape[0]
            other = _rows(outs[t], (1 - c) * kh, kh)
            pltpu.make_async_remote_copy(
                src_ref=other, dst_ref=other, send_sem=send.at[t], recv_sem=recv.at[t],
                device_id=sibling, device_id_type=MESH).wait_recv()
        for rc in handed:
            rc.wait_send()

    pieces = [_piece_shape(p.shape) for p in partials]
    return pl.pallas_call(
        body, in_specs=[pl.BlockSpec(memory_space=pl.ANY)] * nt + [VMEM_SPEC] * nt, out_specs=[VMEM_SPEC] * nt,
        out_shape=[jax.ShapeDtypeStruct((2 * kh, n4), F32) for kh, n4 in pieces],
        scratch_shapes=[pltpu.VMEM(p, BF16) for p in pieces]
        + [pltpu.SemaphoreType.DMA((nt,)), pltpu.SemaphoreType.DMA((nt,)), pltpu.SemaphoreType.DMA((nt,))],
        compiler_params=pltpu.CompilerParams(vmem_limit_bytes=VMEM_LIMIT), name=name)(*partials, *lands)


VEC_SHAPE = (8, D_MODEL + LANES)
VEC_SLOTS = dict(norm1_g=(slice(0, 1), slice(0, D_MODEL)), norm2_g=(slice(1, 2), slice(0, D_MODEL)),
                 final_g=(slice(2, 3), slice(0, D_MODEL)), sgu_ln_g=(slice(3, 4), slice(0, SGU_W)),
                 sgu_ln_b=(slice(3, 4), slice(SGU_W, 2 * SGU_W)), b_spatial=(slice(0, 8), slice(D_MODEL, D_MODEL + LANES)),
                 loss=(slice(4, 5), slice(0, LANES)))
VEC_PARAMS = ("norm1_g", "norm2_g", "final_g", "sgu_ln_g", "sgu_ln_b", "b_spatial")
SMALL_PARAMS = VEC_PARAMS + ("w_spatial",)
W_SPATIAL_2D = (SGU_GROUPS * SGU_CHUNK, SGU_CHUNK)


def _small_step(partials, w, m, v):
    def shape2d(name):
        if name == "w_spatial":
            return W_SPATIAL_2D
        rows, cols = VEC_SLOTS[name]
        return (rows.stop - rows.start, cols.stop - cols.start)

    g_names = VEC_PARAMS + ("loss", "w_spatial")
    ng, npar = len(g_names), len(SMALL_PARAMS)

    def pack(dst, parts):
        dst[...] = jnp.zeros(VEC_SHAPE, F32)
        for n, ref in parts.items():
            if n in VEC_SLOTS:
                dst[VEC_SLOTS[n]] = ref[...]

    def reduce_body(*refs):
        g_in = dict(zip(g_names, refs[:ng]))
        vec_out, ws_out, vec, vec_pair, vec_slot, ws_pair, ws_slot, send, recv = refs[ng:]
        x, y, c, me, chips = _place()
        sibling = (x, y, 1 - c)
        pack(vec, g_in)
        copies = []

        def allreduce(k0, src, pair, slot):
            first = pltpu.make_async_remote_copy(src_ref=src, dst_ref=pair, send_sem=send.at[k0], recv_sem=recv.at[k0],
                                                 device_id=sibling, device_id_type=MESH)
            first.start()
            first.wait_recv()
            slot[me] = src[...] + pair[...]
            arrivals = []
            for j, chip in enumerate(chips):
                theirs = 2 * chip[0] + chip[1]
                rc = pltpu.make_async_remote_copy(src_ref=slot.at[me], dst_ref=slot.at[me], send_sem=send.at[k0 + 1 + j],
                                                  recv_sem=recv.at[k0 + 1 + j], device_id=(*chip, c), device_id_type=MESH)
                rc.start()
                arrivals.append(pltpu.make_async_remote_copy(
                    src_ref=slot.at[theirs], dst_ref=slot.at[theirs], send_sem=send.at[k0 + 1 + j],
                    recv_sem=recv.at[k0 + 1 + j], device_id=(*chip, c), device_id_type=MESH))
                copies.append(rc)
            copies.append(first)
            return arrivals

        arrivals = allreduce(0, vec, vec_pair, vec_slot) + allreduce(4, g_in["w_spatial"], ws_pair, ws_slot)
        for a in arrivals:
            a.wait_recv()
        vec_out[...] = ((vec_slot[0] + vec_slot[1]) + vec_slot[2]) + vec_slot[3]

        def spatial(r0, ck):
            rows = pl.ds(r0, ck)
            ws_out[rows, :] = ((ws_slot[0, rows, :] + ws_slot[1, rows, :]) + ws_slot[2, rows, :]) + ws_slot[3, rows, :]

        _for_row_chunks(W_SPATIAL_2D[0], spatial)
        for rc in copies:
            rc.wait_send()

    g_vec, g_ws = pl.pallas_call(
        reduce_body, in_specs=[VMEM_SPEC] * ng, out_specs=[VMEM_SPEC] * 2,
        out_shape=[jax.ShapeDtypeStruct(VEC_SHAPE, F32), jax.ShapeDtypeStruct(W_SPATIAL_2D, F32)],
        scratch_shapes=[pltpu.VMEM(VEC_SHAPE, F32), pltpu.VMEM(VEC_SHAPE, F32), pltpu.VMEM((N_CHIPS,) + VEC_SHAPE, F32),
                        pltpu.VMEM(W_SPATIAL_2D, F32), pltpu.VMEM((N_CHIPS,) + W_SPATIAL_2D, F32),
                        pltpu.SemaphoreType.DMA((8,)), pltpu.SemaphoreType.DMA((8,))],
        name="small_params_allreduce")(*[partials[n].reshape(shape2d(n)) for n in g_names])

    def update_body(*refs):
        gv_ref, gw_ref = refs[:2]
        w_in, m_in, v_in = (dict(zip(SMALL_PARAMS, refs[2 + k * npar:2 + (k + 1) * npar])) for k in range(3))
        o0 = 2 + 3 * npar
        g_out = dict(zip(g_names, refs[o0:o0 + ng]))
        d_out, m_out, v_out = (dict(zip(SMALL_PARAMS, refs[o0 + ng + k * npar:o0 + ng + (k + 1) * npar])) for k in range(3))
        vw, vm, vv = refs[o0 + ng + 3 * npar:]
        pack(vw, w_in)
        pack(vm, m_in)
        pack(vv, v_in)
        d_vec, m_vec, v_vec = _adamw_math(gv_ref[...], vw[...], vm[...], vv[...])
        vw[...] = d_vec
        vm[...] = m_vec
        vv[...] = v_vec
        for n in VEC_PARAMS + ("loss",):
            g_out[n][...] = gv_ref[VEC_SLOTS[n]]
        for n in VEC_PARAMS:
            d_out[n][...] = vw[VEC_SLOTS[n]]
            m_out[n][...] = vm[VEC_SLOTS[n]]
            v_out[n][...] = vv[VEC_SLOTS[n]]

        def spatial(r0, ck):
            rows = pl.ds(r0, ck)
            g = gw_ref[rows, :]
            d_, m_, v_ = _adamw_math(g, w_in["w_spatial"][rows, :], m_in["w_spatial"][rows, :], v_in["w_spatial"][rows, :])
            g_out["w_spatial"][rows, :] = g
            d_out["w_spatial"][rows, :] = d_
            m_out["w_spatial"][rows, :] = m_
            v_out["w_spatial"][rows, :] = v_

        _for_row_chunks(W_SPATIAL_2D[0], spatial)

    ins = [g_vec, g_ws]
    for src in (w, m, v):
        ins += [src[n].reshape(shape2d(n)) for n in SMALL_PARAMS]
    out_shapes = [jax.ShapeDtypeStruct(shape2d(n), F32) for n in g_names + SMALL_PARAMS * 3]
    outs = pl.pallas_call(
        update_body, in_specs=[VMEM_SPEC] * len(ins), out_specs=[VMEM_SPEC] * len(out_shapes), out_shape=out_shapes,
        scratch_shapes=[pltpu.VMEM(VEC_SHAPE, F32)] * 3, name="small_params_update")(*ins)
    grads = dict(zip(g_names, outs[:ng]))
    rest = [dict(zip(SMALL_PARAMS, outs[ng + k * npar:ng + (k + 1) * npar])) for k in range(3)]
    return grads, rest[0], rest[1], rest[2]


BIG = ("w_in", "w_proj_attn", "w_proj_sgu", "w_out", "w_ffn_gate", "w_ffn_up", "w_ffn_down")
COMM_GROUPS = (("w_in",), ("w_proj_attn", "w_proj_sgu", "w_out", "w_ffn_gate", "w_ffn_up", "w_ffn_down"))
WEIGHTS = ("norm1_g", "w_in", "sgu_ln_g", "sgu_ln_b", "w_spatial", "b_spatial", "w_proj_attn", "w_proj_sgu", "w_out",
           "norm2_g", "w_ffn_gate", "w_ffn_up", "w_ffn_down", "final_g")


def _cols_from_chips(g):
    return jnp.transpose(g, (1, 0, 2)).reshape(g.shape[1], N_CHIPS * g.shape[2])


def kernel(x, positions, norm1_g, w_in, sgu_ln_g, sgu_ln_b, w_spatial, b_spatial, w_proj_attn, w_proj_sgu, w_out, norm2_g, w_ffn_gate, w_ffn_up, w_ffn_down, final_g, loss_target, m_norm1_g, m_w_in, m_sgu_ln_g, m_sgu_ln_b, m_w_spatial, m_b_spatial, m_w_proj_attn, m_w_proj_sgu, m_w_out, m_norm2_g, m_w_ffn_gate, m_w_ffn_up, m_w_ffn_down, m_final_g, v_norm1_g, v_w_in, v_sgu_ln_g, v_sgu_ln_b, v_w_spatial, v_b_spatial, v_w_proj_attn, v_w_proj_sgu, v_w_out, v_norm2_g, v_w_ffn_gate, v_w_ffn_up, v_w_ffn_down, v_final_g):
    w = dict(norm1_g=norm1_g, w_in=w_in, sgu_ln_g=sgu_ln_g, sgu_ln_b=sgu_ln_b, w_spatial=w_spatial, b_spatial=b_spatial,
             w_proj_attn=w_proj_attn, w_proj_sgu=w_proj_sgu, w_out=w_out, norm2_g=norm2_g, w_ffn_gate=w_ffn_gate,
             w_ffn_up=w_ffn_up, w_ffn_down=w_ffn_down, final_g=final_g)
    m = dict(norm1_g=m_norm1_g, w_in=m_w_in, sgu_ln_g=m_sgu_ln_g, sgu_ln_b=m_sgu_ln_b, w_spatial=m_w_spatial,
             b_spatial=m_b_spatial, w_proj_attn=m_w_proj_attn, w_proj_sgu=m_w_proj_sgu, w_out=m_w_out, norm2_g=m_norm2_g,
             w_ffn_gate=m_w_ffn_gate, w_ffn_up=m_w_ffn_up, w_ffn_down=m_w_ffn_down, final_g=m_final_g)
    v = dict(norm1_g=v_norm1_g, w_in=v_w_in, sgu_ln_g=v_sgu_ln_g, sgu_ln_b=v_sgu_ln_b, w_spatial=v_w_spatial,
             b_spatial=v_b_spatial, w_proj_attn=v_w_proj_attn, w_proj_sgu=v_w_proj_sgu, w_out=v_w_out, norm2_g=v_norm2_g,
             w_ffn_gate=v_w_ffn_gate, w_ffn_up=v_w_ffn_up, w_ffn_down=v_w_ffn_down, final_g=v_final_g)
    t = x.shape[1]

    shards = {n: _ew(f"cast_{n}", lambda a: (a,), [w[n][0]], [BF16])[0] for n in BIG}
    late = COMM_GROUPS[1]
    k_in, n_in = shards["w_in"].shape
    *first, token = _split_start("gather_start_0", _gather_half_copies, 3, [shards["w_in"]], [(3, k_in // 2, n_in)])
    pending = {}

    def first_weight(after):
        srcs, filled = _split_wait("gather_wait_0", _gather_half_copies, *first, after)
        gath_in, late_shards = lax.optimization_barrier(
            (_gather_finish("gather_finish_0", srcs[0], filled[0]), [shards[n] for n in late]))
        *pending["late"], _ = _split_start(
            "gather_start_1", _gather_copies, 3, late_shards, [(N_CHIPS,) + s.shape for s in late_shards])
        return _cols_from_chips(gath_in)

    def late_weights(after):
        srcs, filled = _split_wait("gather_wait_1", _gather_copies, *pending["late"], after)
        me = 2 * lax.axis_index("x") + lax.axis_index("y")
        gath = {n: lax.dynamic_update_slice(f, s[None], (me, 0, 0)) for n, f, s in zip(late, filled, srcs)}
        return (_cols_from_chips(gath["w_proj_attn"]), _cols_from_chips(gath["w_proj_sgu"]),
                gath["w_out"].reshape(D_MODEL, D_MODEL), gath["w_ffn_gate"], gath["w_ffn_up"], gath["w_ffn_down"])

    exchanges = {}

    def on_grads(i, partials):
        if "w_out" in partials:
            partials["w_out"] = partials["w_out"].reshape(N_CHIPS, D_MODEL // N_CHIPS, D_MODEL)
        parts = [partials[n] for n in COMM_GROUPS[i]]
        *exchanges[i], started = _split_start(
            f"rs_exchange_start_{i}", _exchange_copies, N_PEERS, parts, [(N_PEERS,) + _piece_shape(p.shape) for p in parts])
        return started

    dx, _, small = _local_step(
        x[0], positions.reshape(t, 1), loss_target[0], norm1_g + token, sgu_ln_g, sgu_ln_b, w_spatial[0], b_spatial[0],
        norm2_g, final_g.reshape(1, D_MODEL), first_weight, late_weights, on_grads=on_grads)

    grads = {}
    for i in (1, 0):
        parts, filled = _split_wait(f"rs_exchange_wait_{i}", _exchange_copies, *exchanges[i], dx)
        grads.update(zip(COMM_GROUPS[i], _device_sum(f"rs_device_sum_{i}", parts, filled)))

    delta, new_m, new_v = {}, {}, {}
    for n in BIG:
        shp = w[n].shape
        g_, d_, m_, v_ = _adamw(f"adamw_{n}", grads[n], w[n][0], m[n][0], v[n][0])
        grads[n], delta[n], new_m[n], new_v[n] = g_.reshape(shp), d_.reshape(shp), m_.reshape(shp), v_.reshape(shp)

    g_s, d_s, m_s, v_s = _small_step(small, w, m, v)
    loss = g_s["loss"][0, 0]
    for n in SMALL_PARAMS:
        shp = w[n].shape
        grads[n], delta[n], new_m[n], new_v[n] = (a[n].reshape(shp) for a in (g_s, d_s, m_s, v_s))

    return (loss, dx.reshape(x.shape), *[grads[n] for n in WEIGHTS], *[delta[n] for n in WEIGHTS],
            *[new_m[n] for n in WEIGHTS], *[new_v[n] for n in WEIGHTS])
```

```python
import functools

import numpy as np
import jax
import jax.numpy as jnp
from jax import lax
from jax.experimental import pallas as pl
from jax.experimental.pallas import tpu as pltpu

F32, BF16 = jnp.float32, jnp.bfloat16
MESH = pl.DeviceIdType.MESH

D_MODEL = 1024
HEAD_DIM = 64
ATTN_W = 512
DILATIONS = (1, 4, 16)
BLK = 128
ATTN_BLOCKS_PER_STEP = 4
ROPE_DIM = 16
ROPE_THETA = 500000.0
SGU_W = 512
SGU_CHUNK = 128
SGU_GROUPS = 8
D_FF = 2816
N_CHIPS = 4
FF_SHARD = D_FF // N_CHIPS
IN_COLS = 7680
EPS = 1e-6
NEG = -1e30
LANES = 128
VMEM_LIMIT = 52 * 1024 * 1024

ADAM_LR, ADAM_B1, ADAM_B2, ADAM_EPS, ADAM_WD, ADAM_STEP = 0.001, 0.9, 0.999, 1e-08, 0.01, 10

QKV_BLOCKS = 9


def _w_in_block(part, g):
    return part * len(DILATIONS) + g


def _cparams(ngrid):
    return pltpu.CompilerParams(dimension_semantics=("arbitrary",) * ngrid, vmem_limit_bytes=VMEM_LIMIT)


def _full(shape):
    return pl.BlockSpec(shape, lambda *_: (0,) * len(shape))


def _resident(shape):
    return pl.BlockSpec(shape, lambda *_: (0,) * len(shape), pipeline_mode=pl.Buffered(1))


NN = ((1,), (0,))
NT = ((1,), (1,))
TN = ((0,), (0,))


def _mm(name, grid, pairs, dims, acc_shape, epi, *, extras=(), outs=(), reds=(), aliases=None):
    nk = grid[-1]
    npair, nex, nout, nred = len(pairs), len(extras), len(outs), len(reds)

    def body(*refs):
        a_refs = refs[:npair]
        b_refs = refs[npair:2 * npair]
        p0 = 2 * npair
        e_refs = refs[p0:p0 + nex]
        o_refs = refs[p0 + nex:p0 + nex + nout]
        r_refs = refs[p0 + nex + nout:p0 + nex + nout + nred]
        ids = [pl.program_id(a) for a in range(len(grid))]
        k = ids[-1]
        if nred:
            first = ids[0] == 0
            for v in ids[1:]:
                first = first & (v == 0)

            @pl.when(first)
            def _():
                for r in r_refs:
                    r[...] = jnp.zeros(r.shape, r.dtype)

        part = None
        for a_ref, b_ref in zip(a_refs, b_refs):
            d = lax.dot_general(a_ref[...], b_ref[...], (dims, ((), ())), preferred_element_type=F32)
            part = d if part is None else part + d
        if nk == 1:
            epi(part, e_refs, o_refs, r_refs, ids)
        else:
            acc_ref = refs[-1]

            @pl.when(k == 0)
            def _():
                acc_ref[...] = part

            @pl.when(k > 0)
            def _():
                acc_ref[...] += part

            @pl.when(k == nk - 1)
            def _():
                epi(acc_ref[...], e_refs, o_refs, r_refs, ids)

    in_specs = [p[1] for p in pairs] + [p[3] for p in pairs] + [e[1] for e in extras]
    args = [p[0] for p in pairs] + [p[2] for p in pairs] + [e[0] for e in extras]
    out_shape = [jax.ShapeDtypeStruct(o[0], o[1]) for o in outs] + [jax.ShapeDtypeStruct(r, F32) for r in reds]
    out_specs = [o[2] for o in outs] + [_full(r) for r in reds]
    scratch_shapes = [pltpu.VMEM(acc_shape, F32)] if nk > 1 else []
    return pl.pallas_call(
        body, grid=grid, in_specs=in_specs, out_specs=out_specs, out_shape=out_shape, scratch_shapes=scratch_shapes,
        input_output_aliases=aliases or {}, compiler_params=_cparams(len(grid)), name=name)(*args)


def _rope(v, cos_t, sin_t):
    half = ROPE_DIM // 2
    first = (lax.broadcasted_iota(jnp.int32, cos_t.shape, 1) % HEAD_DIM) < half
    outs = []
    for cs in range(v.shape[1] // LANES):
        x = v[:, cs * LANES:(cs + 1) * LANES]
        partner = jnp.where(first, pltpu.roll(x, LANES - half, axis=1), pltpu.roll(x, half, axis=1))
        outs.append(x * cos_t + partner * sin_t)
    return outs[0] if len(outs) == 1 else jnp.concatenate(outs, axis=1)


def _spread_heads(v2, upper):
    other = pltpu.roll(v2, HEAD_DIM, axis=1)
    h0 = jnp.where(upper, other, v2)
    h1 = jnp.where(upper, v2, other)
    return jnp.concatenate([jnp.concatenate([h0, h0], axis=1), jnp.concatenate([h1, h1], axis=1)], axis=0)


def _sigmoid(v):
    return 0.5 * jnp.tanh(0.5 * v) + 0.5


def _rms_stats(v):
    r = lax.rsqrt(jnp.mean(v * v, axis=-1, keepdims=True) + EPS)
    return v * r, r


def _rms_bwd(dy, xhat, r, g):
    dxh = dy * g
    return r * (dxh - xhat * jnp.mean(dxh * xhat, axis=-1, keepdims=True))


def _head_sum_matrix():
    idx = np.arange(ATTN_W) // HEAD_DIM
    return jnp.asarray((idx[:, None] == idx[None, :]).astype(np.float32), dtype=BF16)


def _group_sum(v, e):
    hi = v.astype(BF16)
    lo = (v - hi.astype(F32)).astype(BF16)
    return jnp.dot(hi, e, preferred_element_type=F32) + jnp.dot(lo, e, preferred_element_type=F32)


TILE = 512


def _to_slabs(slab_ref, v):
    for cs in range(slab_ref.shape[0]):
        slab_ref[cs] = v[:, cs * LANES:(cs + 1) * LANES]


def _from_slabs(slab_ref):
    return jnp.concatenate([slab_ref[cs] for cs in range(slab_ref.shape[0])], axis=1)


def _class_rows(slab_ref, r, dil):
    n = slab_ref.shape[1] // dil
    return jnp.concatenate([slab_ref.at[cs][pl.ds(r, n, stride=dil), :] for cs in range(slab_ref.shape[0])], axis=1)


def _put_class_rows(slab_ref, r, dil, v):
    n = slab_ref.shape[1] // dil
    for cs in range(slab_ref.shape[0]):
        slab_ref.at[cs][pl.ds(r, n, stride=dil), :] = v[:, cs * LANES:(cs + 1) * LANES]


def _natural_from_group(slab_ref, grp_ref):
    dil = grp_ref.shape[0]
    for r in range(dil):
        _put_class_rows(slab_ref, r, dil, grp_ref[r].astype(F32))
    return _from_slabs(slab_ref)


def _group_from_natural(slab_ref, grp_ref, v):
    dil = grp_ref.shape[0]
    _to_slabs(slab_ref, v)
    for r in range(dil):
        grp_ref[r] = _class_rows(slab_ref, r, dil).astype(grp_ref.dtype)


def _group_spec(dil, tile, width):
    return pl.BlockSpec((dil, tile // dil, width), lambda i, *_: (0, i, 0))


def _slabs(tile, width):
    return pltpu.VMEM((width // LANES, tile, LANES), F32)


def _rope_consts():
    lane = np.arange(LANES) % HEAD_DIM
    fi = lane % (ROPE_DIM // 2)
    invf = np.where(lane < ROPE_DIM, ROPE_THETA ** (-(2.0 * fi) / ROPE_DIM), 0.0)
    sgn = np.where(lane < ROPE_DIM // 2, -1.0, np.where(lane < ROPE_DIM, 1.0, 0.0))
    return (jnp.asarray(invf.astype(np.float32)).reshape(1, LANES), jnp.asarray(sgn.astype(np.float32)).reshape(1, LANES))


def _rope_tables(pos_col):
    t = pos_col.shape[0]
    tile = min(t, TILE)
    invf, sgn = _rope_consts()

    def body(p_ref, f_ref, s_ref, c0, s0, c1, s1, c2, s2, slab_c, slab_s):
        ang = p_ref[...].astype(F32) * f_ref[...]
        cos, sin = jnp.cos(ang), jnp.sin(ang) * s_ref[...]
        c0[...] = cos
        s0[...] = sin
        _group_from_natural(slab_c, c1, cos)
        _group_from_natural(slab_s, s1, sin)
        for r in range(DILATIONS[2]):
            c2[r] = _class_rows(slab_c, r, DILATIONS[2])
            s2[r] = _class_rows(slab_s, r, DILATIONS[2])

    nat = pl.BlockSpec((tile, LANES), lambda i: (i, 0))
    specs, shapes = [nat, nat], [(t, LANES)] * 2
    for d in DILATIONS[1:]:
        specs += [_group_spec(d, tile, LANES)] * 2
        shapes += [(d, t // d, LANES)] * 2
    outs = pl.pallas_call(
        body, grid=(t // tile,),
        in_specs=[pl.BlockSpec((tile, 1), lambda i: (i, 0)), _full((1, LANES)), _full((1, LANES))],
        out_specs=specs, out_shape=[jax.ShapeDtypeStruct(s, F32) for s in shapes],
        scratch_shapes=[_slabs(tile, LANES)] * 2,
        compiler_params=_cparams(1), name="rope_tables")(pos_col, invf, sgn)
    return [(outs[2 * g].reshape(t, LANES), outs[2 * g + 1].reshape(t, LANES)) for g in range(len(DILATIONS))]


def _norm_fwd(x, g):
    t = x.shape[0]
    tile = min(t, TILE)

    def body(x_ref, g_ref, h0_ref, h1_ref, h2_ref, slab):
        xhat, _ = _rms_stats(x_ref[...])
        hn = xhat * g_ref[...]
        h0_ref[...] = hn.astype(BF16)
        _group_from_natural(slab, h1_ref, hn)
        for r in range(DILATIONS[2]):
            h2_ref[r] = _class_rows(slab, r, DILATIONS[2]).astype(BF16)

    nat = pl.BlockSpec((tile, D_MODEL), lambda i: (i, 0))
    return pl.pallas_call(
        body, grid=(t // tile,),
        in_specs=[nat, _full((1, D_MODEL))],
        out_specs=[nat] + [_group_spec(d, tile, D_MODEL) for d in DILATIONS[1:]],
        out_shape=[jax.ShapeDtypeStruct((t, D_MODEL), BF16)]
        + [jax.ShapeDtypeStruct((d, t // d, D_MODEL), BF16) for d in DILATIONS[1:]],
        scratch_shapes=[_slabs(tile, D_MODEL)],
        compiler_params=_cparams(1), name="norm1_fwd")(x, g)


GU_COLS = 3072
GROUP_COLS = 1536
GU_HALF = GU_COLS // 2


def _w_in_spec(width, block):
    return pl.BlockSpec((D_MODEL, width), lambda i: (0, block), pipeline_mode=pl.Buffered(1))


def _gu_w_specs():
    first = QKV_BLOCKS * ATTN_W // GU_HALF
    return [_w_in_spec(GU_HALF, first), _w_in_spec(GU_HALF, first + 1)]


def _group_w_specs(g):
    return [_w_in_spec(ATTN_W, _w_in_block(part, g)) for part in range(3)]


def _in_proj(hs, w_in, tables):
    t = hs[0].shape[0]
    tm = min(t, 1024)

    def body_gu(h_ref, w0_ref, w1_ref, o_ref):
        h = h_ref[...]
        o_ref[:, 0:GU_HALF] = jnp.dot(h, w0_ref[...], preferred_element_type=F32).astype(BF16)
        o_ref[:, GU_HALF:] = jnp.dot(h, w1_ref[...], preferred_element_type=F32).astype(BF16)

    gu = _token_call("in_proj_gates_uv", body_gu, t, tm,
                     [(hs[0], _rows_spec(tm, D_MODEL))] + [(w_in, s) for s in _gu_w_specs()],
                     [((t, GU_COLS), BF16, _rows_spec(tm, GU_COLS))])[0]

    qkvs = []
    for g in range(len(DILATIONS)):

        def body_qkv(h_ref, wq_ref, wk_ref, wv_ref, cos_ref, sin_ref, o_ref):
            h = h_ref[...]
            cos_w, sin_w = cos_ref[...], sin_ref[...]
            q = jnp.dot(h, wq_ref[...], preferred_element_type=F32)
            o_ref[:, 0:ATTN_W] = (_rope(q, cos_w, sin_w) * HEAD_DIM ** -0.5).astype(BF16)
            k = jnp.dot(h, wk_ref[...], preferred_element_type=F32)
            o_ref[:, ATTN_W:2 * ATTN_W] = _rope(k, cos_w, sin_w).astype(BF16)
            o_ref[:, 2 * ATTN_W:] = jnp.dot(h, wv_ref[...], preferred_element_type=F32).astype(BF16)

        cos_t, sin_t = tables[g]
        qkvs.append(_token_call(
            f"in_proj_qkv_g{g}", body_qkv, t, tm,
            [(hs[g].reshape(t, D_MODEL), _rows_spec(tm, D_MODEL))] + [(w_in, s) for s in _group_w_specs(g)]
            + [(cos_t, _rows_spec(tm, LANES)), (sin_t, _rows_spec(tm, LANES))],
            [((t, GROUP_COLS), BF16, _rows_spec(tm, GROUP_COLS))])[0])
    return gu, qkvs


def _attn_masks(n):
    row = lax.broadcasted_iota(jnp.int32, (2 * BLK, 2 * BLK), 0) % BLK
    col = lax.broadcasted_iota(jnp.int32, (2 * BLK, 2 * BLK), 1)
    diff = BLK + row - col
    valid = (diff >= 0) & (diff <= BLK) & ((col >= BLK) | (n > 0))
    upper = lax.broadcasted_iota(jnp.int32, (BLK, LANES), 1) >= HEAD_DIM
    return valid, upper


def _stack_heads(v2, upper):
    zero = jnp.zeros_like(v2)
    return jnp.concatenate([jnp.where(upper, zero, v2), jnp.where(upper, v2, zero)], axis=0)


def _unstack_heads(v, upper):
    return jnp.where(upper, v[BLK:], v[:BLK])


def _attn_fwd(qkv, g, dil):
    t = qkv.shape[0]
    length = t // dil
    nb = length // BLK
    per_step = min(nb, ATTN_BLOCKS_PER_STEP)
    view = qkv.reshape(dil, length, GROUP_COLS)

    def body(q_ref, kc_ref, kp_ref, vc_ref, vp_ref, o_ref, l_ref, kwin, vwin):
        n = pl.program_id(1)
        kwin[0:BLK] = kp_ref[...]
        kwin[BLK:] = kc_ref[...]
        vwin[0:BLK] = vp_ref[...]
        vwin[BLK:] = vc_ref[...]

        def block(b, carry):
            valid, upper = _attn_masks(n * per_step + b)
            rows = pl.ds(pl.multiple_of(b * BLK, BLK), BLK)
            window = pl.ds(pl.multiple_of(b * BLK, BLK), 2 * BLK)
            for p in range(ATTN_W // LANES):
                sl = slice(p * LANES, (p + 1) * LANES)
                qs = _stack_heads(q_ref[rows, sl], upper)
                s = lax.dot_general(qs, kwin[window, sl], (NT, ((), ())), preferred_element_type=F32)
                s = jnp.where(valid, s, NEG)
                m = jnp.max(s, axis=1, keepdims=True)
                pe = jnp.exp(s - m)
                den = jnp.sum(pe, axis=1, keepdims=True)
                o = jnp.dot(pe.astype(BF16), vwin[window, sl], preferred_element_type=F32) / den
                lse = jnp.broadcast_to(m + jnp.log(den), (2 * BLK, LANES))
                o_ref[rows, sl] = _unstack_heads(o, upper).astype(BF16)
                l_ref[rows, sl] = _unstack_heads(lse, upper)
            return carry

        lax.fori_loop(0, per_step, block, 0)

    rows = per_step * BLK
    cur = lambda part: pl.BlockSpec((None, rows, ATTN_W), lambda r, n: (r, n, part))
    prev = lambda part: pl.BlockSpec((None, BLK, ATTN_W), lambda r, n: (r, jnp.maximum(n * per_step - 1, 0), part))
    out_spec = pl.BlockSpec((None, rows, ATTN_W), lambda r, n: (r, n, 0))
    return pl.pallas_call(
        body, grid=(dil, nb // per_step),
        in_specs=[cur(0), cur(1), prev(1), cur(2), prev(2)],
        out_specs=[out_spec, out_spec],
        out_shape=[jax.ShapeDtypeStruct((dil, length, ATTN_W), BF16), jax.ShapeDtypeStruct((dil, length, ATTN_W), F32)],
        scratch_shapes=[pltpu.VMEM((rows + BLK, ATTN_W), BF16)] * 2,
        compiler_params=_cparams(2), name=f"attn_fwd_g{g}")(view, view, view, view, view)


def _alphas(l0, l1, l2):
    m = jnp.maximum(jnp.maximum(l0, l1), l2)
    e0, e1, e2 = jnp.exp(l0 - m), jnp.exp(l1 - m), jnp.exp(l2 - m)
    inv = 1.0 / (e0 + e1 + e2)
    return e0 * inv, e1 * inv, e2 * inv


def _natural_group_values(o_refs, l_refs, slabs):
    os_ = [o_refs[0][0].astype(F32)] + [_natural_from_group(slabs[2 * g - 2], o_refs[g]) for g in (1, 2)]
    ls_ = [l_refs[0][0]] + [_natural_from_group(slabs[2 * g - 1], l_refs[g]) for g in (1, 2)]
    return os_, ls_


def _combine_fwd(os_, ls_):
    t = os_[0].shape[1]
    tile = min(t, TILE)

    def body(o0, o1, o2, l0, l1, l2, a_ref, *slabs):
        ov, lv = _natural_group_values((o0, o1, o2), (l0, l1, l2), slabs)
        a0, a1, a2 = _alphas(*lv)
        a_ref[...] = (a0 * ov[0] + a1 * ov[1] + a2 * ov[2]).astype(BF16)

    specs = [_group_spec(d, tile, ATTN_W) for d in DILATIONS]
    return pl.pallas_call(
        body, grid=(t // tile,), in_specs=specs * 2, out_specs=pl.BlockSpec((tile, ATTN_W), lambda i: (i, 0)),
        out_shape=jax.ShapeDtypeStruct((t, ATTN_W), BF16),
        scratch_shapes=[_slabs(tile, ATTN_W)] * 4,
        compiler_params=_cparams(1), name="combine_fwd")(*os_, *ls_)


def _combine_bwd(dattn, os_, ls_):
    t = dattn.shape[0]
    tile = min(t, TILE)
    e = _head_sum_matrix()

    def body(d_ref, o0, o1, o2, l0, l1, l2, e_ref, do0, do1, do2, c0, c1, c2, *slabs):
        ov, lv = _natural_group_values((o0, o1, o2), (l0, l1, l2), slabs)
        alphas = _alphas(*lv)
        d = d_ref[...]
        attn = alphas[0] * ov[0] + alphas[1] * ov[1] + alphas[2] * ov[2]
        s = _group_sum(d * attn, e_ref[...])
        do0[0] = (alphas[0] * d).astype(BF16)
        c0[0] = -alphas[0] * s
        for g, do_ref, c_ref in ((1, do1, c1), (2, do2, c2)):
            _group_from_natural(slabs[2 * g - 2], do_ref, alphas[g] * d)
            _group_from_natural(slabs[2 * g - 1], c_ref, -alphas[g] * s)

    specs = [_group_spec(d, tile, ATTN_W) for d in DILATIONS]
    shapes = [(d, t // d, ATTN_W) for d in DILATIONS]
    outs = pl.pallas_call(
        body, grid=(t // tile,),
        in_specs=[pl.BlockSpec((tile, ATTN_W), lambda i: (i, 0))] + specs * 2 + [_full((ATTN_W, ATTN_W))],
        out_specs=specs * 2,
        out_shape=[jax.ShapeDtypeStruct(s, BF16) for s in shapes] + [jax.ShapeDtypeStruct(s, F32) for s in shapes],
        scratch_shapes=[_slabs(tile, ATTN_W)] * 4,
        compiler_params=_cparams(1), name="combine_bwd")(dattn, *os_, *ls_, e)
    return outs[:3], outs[3:]


def _attn_bwd(qkv, do, cc, lse, cos_t, sin_t, g, dil):
    t = qkv.shape[0]
    length = t // dil
    nb = length // BLK
    per_step = min(nb, ATTN_BLOCKS_PER_STEP)
    nsteps = nb // per_step
    rows_per_step = per_step * BLK
    qkv_v = qkv.reshape(dil, length, GROUP_COLS)
    cos_v, sin_v = (a.reshape(dil, length, LANES) for a in (cos_t, sin_t))
    scale = HEAD_DIM ** -0.5
    dq_cols, dk_cols, dv_cols = (slice(i * ATTN_W, (i + 1) * ATTN_W) for i in range(3))

    def body(q_ref, kc_ref, kp_ref, vc_ref, vp_ref, do_ref, c_ref, l_ref, cosc, sinc, cosp, sinp,
             out_ref, acc, kwin, vwin, cwin, swin):
        n = pl.program_id(1)

        def one_block(b):
            valid, upper = _attn_masks(n * per_step + b)
            start = b * BLK if isinstance(b, int) else pl.multiple_of(b * BLK, BLK)
            rows, before, window = pl.ds(start, BLK), pl.ds(start, BLK), pl.ds(start, 2 * BLK)
            own = pl.ds(start + BLK, BLK)
            dq_parts, dkp_parts, dkc_parts, dvp_parts, dvc_parts = [], [], [], [], []
            for p in range(ATTN_W // LANES):
                sl = slice(p * LANES, (p + 1) * LANES)
                qs = _stack_heads(q_ref[rows, sl], upper)
                dos = _stack_heads(do_ref[rows, sl], upper)
                k2 = kwin[window, sl]
                l_col = _spread_heads(l_ref[rows, sl], upper)
                c_col = _spread_heads(c_ref[rows, sl], upper)
                s = lax.dot_general(qs, k2, (NT, ((), ())), preferred_element_type=F32)
                pe = jnp.exp(jnp.where(valid, s, NEG) - l_col)
                dpv = lax.dot_general(dos, vwin[window, sl], (NT, ((), ())), preferred_element_type=F32)
                ds = (pe * (dpv + c_col)).astype(BF16)
                dq2 = _unstack_heads(jnp.dot(ds, k2, preferred_element_type=F32), upper)
                dk2 = lax.dot_general(ds, qs, (TN, ((), ())), preferred_element_type=F32)
                dv2 = lax.dot_general(pe.astype(BF16), dos, (TN, ((), ())), preferred_element_type=F32)
                dq_parts.append(dq2)
                dkp_parts.append(dk2[:BLK])
                dkc_parts.append(dk2[BLK:])
                dvp_parts.append(dv2[:BLK])
                dvc_parts.append(dv2[BLK:])
            dq = _rope(jnp.concatenate(dq_parts, axis=1) * scale, cwin[own, :], swin[own, :])
            dkc = _rope(jnp.concatenate(dkc_parts, axis=1), cwin[own, :], swin[own, :])
            dkp = _rope(jnp.concatenate(dkp_parts, axis=1), cwin[before, :], swin[before, :])
            return dq, dkp, dkc, jnp.concatenate(dvp_parts, axis=1), jnp.concatenate(dvc_parts, axis=1)

        @pl.when(n < nsteps)
        def _():
            kwin[0:BLK] = kp_ref[...]
            kwin[BLK:] = kc_ref[...]
            vwin[0:BLK] = vp_ref[...]
            vwin[BLK:] = vc_ref[...]
            cwin[0:BLK] = cosp[...]
            cwin[BLK:] = cosc[...]
            swin[0:BLK] = -sinp[...]
            swin[BLK:] = -sinc[...]
            dq, dkp, dkc, dvp, dvc = one_block(0)
            last = slice(rows_per_step - BLK, rows_per_step)

            @pl.when(n > 0)
            def _():
                if per_step > 1:
                    out_ref[0:rows_per_step - BLK, :] = acc[0:rows_per_step - BLK, :].astype(BF16)
                out_ref[last, dq_cols] = acc[last, dq_cols].astype(BF16)
                out_ref[last, dk_cols] = (acc[last, dk_cols] + dkp).astype(BF16)
                out_ref[last, dv_cols] = (acc[last, dv_cols] + dvp).astype(BF16)

            acc[0:BLK, dq_cols] = dq
            acc[0:BLK, dk_cols] = dkc
            acc[0:BLK, dv_cols] = dvc

            def later(b, carry):
                dq, dkp, dkc, dvp, dvc = one_block(b)
                start = pl.multiple_of(b * BLK, BLK)
                before, rows = pl.ds(start - BLK, BLK), pl.ds(start, BLK)
                acc[before, dk_cols] += dkp
                acc[before, dv_cols] += dvp
                acc[rows, dq_cols] = dq
                acc[rows, dk_cols] = dkc
                acc[rows, dv_cols] = dvc
                return carry

            lax.fori_loop(1, per_step, later, 0)

        @pl.when(n == flush_at)
        def _():
            out_ref[...] = acc[...].astype(BF16)

    flush_at = nsteps - 1 if nsteps == 1 else nsteps
    out_lag = 0 if nsteps == 1 else 1
    nc = lambda n: jnp.minimum(n, nsteps - 1)
    npv = lambda n: jnp.maximum(jnp.minimum(n, nsteps - 1) * per_step - 1, 0)
    cur = lambda part: pl.BlockSpec((None, rows_per_step, ATTN_W), lambda r, n: (r, nc(n), part))
    prev = lambda part: pl.BlockSpec((None, BLK, ATTN_W), lambda r, n: (r, npv(n), part))
    row = pl.BlockSpec((None, rows_per_step, ATTN_W), lambda r, n: (r, nc(n), 0))
    tab_c = pl.BlockSpec((None, rows_per_step, LANES), lambda r, n: (r, nc(n), 0))
    tab_p = pl.BlockSpec((None, BLK, LANES), lambda r, n: (r, npv(n), 0))
    out_spec = pl.BlockSpec((None, rows_per_step, GROUP_COLS), lambda r, n: (r, jnp.maximum(n - out_lag, 0), 0))
    out = pl.pallas_call(
        body, grid=(dil, nsteps + out_lag),
        in_specs=[cur(0), cur(1), prev(1), cur(2), prev(2), row, row, row, tab_c, tab_c, tab_p, tab_p],
        out_specs=out_spec,
        out_shape=jax.ShapeDtypeStruct((dil, length, GROUP_COLS), BF16),
        scratch_shapes=[pltpu.VMEM((rows_per_step, GROUP_COLS), F32)]
        + [pltpu.VMEM((rows_per_step + BLK, ATTN_W), BF16)] * 2 + [pltpu.VMEM((rows_per_step + BLK, LANES), F32)] * 2,
        compiler_params=_cparams(2), name=f"attn_bwd_g{g}")(
            qkv_v, qkv_v, qkv_v, qkv_v, qkv_v, do, cc, lse, cos_v, sin_v, cos_v, sin_v)
    return out.reshape(t, GROUP_COLS)


SQRT_HALF = 0.7071067811865476
INV_SQRT_2PI = 0.3989422804014327


def _sgu_core(uv, g, b, w_ref, bias):
    cdf = 0.5 * (1.0 + lax.erf(uv * SQRT_HALF))
    z = uv * cdf
    u, v = z[:, :SGU_W], z[:, SGU_W:]
    mu = jnp.mean(v, axis=1, keepdims=True)
    xc = v - mu
    rs = lax.rsqrt(jnp.mean(xc * xc, axis=1, keepdims=True) + EPS)
    xhat = xc * rs
    vn = xhat * g + b
    row = lax.broadcasted_iota(jnp.int32, (SGU_CHUNK, SGU_CHUNK), 0)
    col = lax.broadcasted_iota(jnp.int32, (SGU_CHUNK, SGU_CHUNK), 1)
    tril = row >= col
    upper = lax.broadcasted_iota(jnp.int32, (SGU_CHUNK, LANES), 1) >= SGU_W // SGU_GROUPS
    ws, vlo, vhi, mixed = [], [], [], []
    for pr in range(SGU_W // LANES):
        sl = slice(pr * LANES, (pr + 1) * LANES)
        w0 = jnp.where(tril, w_ref[2 * pr], 0.0).astype(BF16)
        w1 = jnp.where(tril, w_ref[2 * pr + 1], 0.0).astype(BF16)
        vn2 = vn[:, sl]
        lo = jnp.where(upper, 0.0, vn2).astype(BF16)
        hi = jnp.where(upper, vn2, 0.0).astype(BF16)
        mixed.append(jnp.dot(w0, lo, preferred_element_type=F32) + jnp.dot(w1, hi, preferred_element_type=F32)
                     + bias[:, sl])
        ws.append((w0, w1))
        vlo.append(lo)
        vhi.append(hi)
    return cdf, u, xhat, rs, jnp.concatenate(mixed, axis=1), ws, vlo, vhi, tril, upper


SGU_STEP = 4 * SGU_CHUNK


def _for_chunks(step_rows, fn):
    def one(ci, carry):
        fn(pl.ds(pl.multiple_of(ci * SGU_CHUNK, SGU_CHUNK), SGU_CHUNK))
        return carry

    lax.fori_loop(0, step_rows // SGU_CHUNK, one, 0)


def _sgu_fwd(gu, ln_g, ln_b, w_s, bias_exp):
    t = gu.shape[0]
    step = min(t, SGU_STEP)

    def body(uv_ref, g_ref, b_ref, w_ref, bias_ref, o_ref):
        def chunk(rows):
            _, u, _, _, mixed, *_ = _sgu_core(uv_ref[rows, :].astype(F32), g_ref[...], b_ref[...], w_ref, bias_ref[...])
            o_ref[rows, :] = (u * mixed).astype(BF16)

        _for_chunks(step, chunk)

    return pl.pallas_call(
        body, grid=(t // step,),
        in_specs=[pl.BlockSpec((step, 2 * SGU_W), lambda n: (n, 0)), _full((1, SGU_W)), _full((1, SGU_W)),
                  _full((SGU_GROUPS, SGU_CHUNK, SGU_CHUNK)), _full((SGU_CHUNK, SGU_W))],
        out_specs=pl.BlockSpec((step, SGU_W), lambda n: (n, 0)),
        out_shape=jax.ShapeDtypeStruct((t, SGU_W), BF16),
        compiler_params=_cparams(1), name="sgu_fwd")(gu, ln_g, ln_b, w_s, bias_exp)


def _sgu_bwd(dproj, gu, dsgu, ln_g, ln_b, w_s, bias_exp):
    t = gu.shape[0]
    step = min(t, SGU_STEP)
    nsteps = t // step
    e = _head_sum_matrix()

    def body(dp_in, uv_ref, ds_ref, g_ref, b_ref, w_ref, bias_ref, e_ref, out_ref, dw_ref, dbias_ref, dg_ref, db_ref):
        n = pl.program_id(0)

        @pl.when(n == 0)
        def _():
            dw_ref[...] = jnp.zeros(dw_ref.shape, F32)
            dbias_ref[...] = jnp.zeros(dbias_ref.shape, F32)
            dg_ref[...] = jnp.zeros(dg_ref.shape, F32)
            db_ref[...] = jnp.zeros(db_ref.shape, F32)

        _for_chunks(step, functools.partial(chunk, uv_ref, ds_ref, g_ref, b_ref, w_ref, bias_ref, out_ref, dw_ref, dbias_ref,
                                            dg_ref, db_ref))

        @pl.when(n == nsteps - 1)
        def _():
            dbias_ref[...] = _group_sum(dbias_ref[...], e_ref[...])

    def chunk(uv_ref, ds_ref, g_ref, b_ref, w_ref, bias_ref, out_ref, dw_ref, dbias_ref, dg_ref, db_ref, rows):
        uv = uv_ref[rows, :].astype(F32)
        g = g_ref[...]
        cdf, u, xhat, rs, mixed, ws, vlo, vhi, tril, upper = _sgu_core(uv, g, b_ref[...], w_ref, bias_ref[...])
        dsg = ds_ref[rows, :]
        du = dsg * mixed
        dmixed = dsg * u
        dbias_ref[...] += dmixed
        dvn = []
        for pr in range(SGU_W // LANES):
            sl = slice(pr * LANES, (pr + 1) * LANES)
            dm2 = dmixed[:, sl]
            dlo = jnp.where(upper, 0.0, dm2).astype(BF16)
            dhi = jnp.where(upper, dm2, 0.0).astype(BF16)
            w0, w1 = ws[pr]
            dvn.append(lax.dot_general(w0, dlo, (TN, ((), ())), preferred_element_type=F32)
                       + lax.dot_general(w1, dhi, (TN, ((), ())), preferred_element_type=F32))
            dw0 = lax.dot_general(dlo, vlo[pr], (NT, ((), ())), preferred_element_type=F32)
            dw1 = lax.dot_general(dhi, vhi[pr], (NT, ((), ())), preferred_element_type=F32)
            dw_ref[2 * pr] += jnp.where(tril, dw0, 0.0)
            dw_ref[2 * pr + 1] += jnp.where(tril, dw1, 0.0)
        dvn = jnp.concatenate(dvn, axis=1)
        dg_ref[...] += jnp.sum(dvn * xhat, axis=0, keepdims=True)
        db_ref[...] += jnp.sum(dvn, axis=0, keepdims=True)
        dxh = dvn * g
        dv = rs * (dxh - jnp.mean(dxh, axis=1, keepdims=True) - xhat * jnp.mean(dxh * xhat, axis=1, keepdims=True))
        dz = jnp.concatenate([du, dv], axis=1)
        dgelu = cdf + uv * (INV_SQRT_2PI * jnp.exp(-0.5 * uv * uv))
        out_ref[rows, :] = (dz * dgelu).astype(BF16)

    outs = pl.pallas_call(
        body, grid=(nsteps,),
        in_specs=[pl.BlockSpec(memory_space=pl.ANY), pl.BlockSpec((step, 2 * SGU_W), lambda n: (n, 0)),
                  pl.BlockSpec((step, SGU_W), lambda n: (n, 0)), _full((1, SGU_W)), _full((1, SGU_W)),
                  _full((SGU_GROUPS, SGU_CHUNK, SGU_CHUNK)), _full((SGU_CHUNK, SGU_W)), _full((ATTN_W, ATTN_W))],
        out_specs=[pl.BlockSpec((step, 2 * SGU_W), lambda n: (n, 0)), _full((SGU_GROUPS, SGU_CHUNK, SGU_CHUNK)),
                   _full((SGU_CHUNK, SGU_W)), _full((1, SGU_W)), _full((1, SGU_W))],
        out_shape=[jax.ShapeDtypeStruct(dproj.shape, BF16), jax.ShapeDtypeStruct((SGU_GROUPS, SGU_CHUNK, SGU_CHUNK), F32),
                   jax.ShapeDtypeStruct((SGU_CHUNK, SGU_W), F32), jax.ShapeDtypeStruct((1, SGU_W), F32),
                   jax.ShapeDtypeStruct((1, SGU_W), F32)],
        input_output_aliases={0: 0},
        compiler_params=_cparams(1), name="sgu_bwd")(dproj, gu, dsgu, ln_g, ln_b, w_s, bias_exp, e)
    return outs


def _merge_fwd(attn, sgu, gu, x, w_pa, w_ps, w_out, g2):
    t = x.shape[0]
    tm = min(t, 512)

    def body(a_ref, s_ref, ga_ref, gb_ref, x_ref, wpa, wps, wo, g_ref, pa_ref, ps_ref, m_ref, x1_ref, h2_ref):
        pa = jnp.dot(a_ref[...], wpa[...], preferred_element_type=F32)
        ps = jnp.dot(s_ref[...], wps[...], preferred_element_type=F32)
        merged = (_sigmoid(ga_ref[...].astype(F32)) * pa + _sigmoid(gb_ref[...].astype(F32)) * ps).astype(BF16)
        x1 = x_ref[...] + jnp.dot(merged, wo[...], preferred_element_type=F32)
        xhat, _ = _rms_stats(x1)
        pa_ref[...] = pa.astype(BF16)
        ps_ref[...] = ps.astype(BF16)
        m_ref[...] = merged
        x1_ref[...] = x1
        h2_ref[...] = (xhat * g_ref[...]).astype(BF16)

    half = pl.BlockSpec((tm, ATTN_W), lambda i: (i, 0))
    full = pl.BlockSpec((tm, D_MODEL), lambda i: (i, 0))
    return pl.pallas_call(
        body, grid=(t // tm,),
        in_specs=[half, half, pl.BlockSpec((tm, D_MODEL), lambda i: (i, 1)), pl.BlockSpec((tm, D_MODEL), lambda i: (i, 2)),
                  full, _resident((ATTN_W, D_MODEL)), _resident((SGU_W, D_MODEL)), _resident((D_MODEL, D_MODEL)),
                  _full((1, D_MODEL))],
        out_specs=[full] * 5,
        out_shape=[jax.ShapeDtypeStruct((t, D_MODEL), BF16), jax.ShapeDtypeStruct((t, D_MODEL), BF16),
                   jax.ShapeDtypeStruct((t, D_MODEL), BF16), jax.ShapeDtypeStruct((t, D_MODEL), F32),
                   jax.ShapeDtypeStruct((t, D_MODEL), BF16)],
        compiler_params=_cparams(1), name="merge_fwd")(attn, sgu, gu, gu, x, w_pa, w_ps, w_out, g2)


def _merge_bwd(dx1b, gu, pa, ps, w_pa, w_ps, w_out):
    t = dx1b.shape[0]
    tm = min(t, 512)

    def body(d_ref, ga_ref, gb_ref, pa_ref, ps_ref, wpa, wps, wo, out_ref, dpa_ref, dps_ref, da_ref, dsg_ref):
        dm = lax.dot_general(d_ref[...], wo[...], (NT, ((), ())), preferred_element_type=F32)
        sa, sb = _sigmoid(ga_ref[...].astype(F32)), _sigmoid(gb_ref[...].astype(F32))
        dpa = (dm * sa).astype(BF16)
        dps = (dm * sb).astype(BF16)
        out_ref[:, 0:D_MODEL] = jnp.zeros((tm, D_MODEL), BF16)
        out_ref[:, D_MODEL:2 * D_MODEL] = (dm * pa_ref[...].astype(F32) * sa * (1.0 - sa)).astype(BF16)
        out_ref[:, 2 * D_MODEL:GU_COLS] = (dm * ps_ref[...].astype(F32) * sb * (1.0 - sb)).astype(BF16)
        dpa_ref[...] = dpa
        dps_ref[...] = dps
        da_ref[...] = lax.dot_general(dpa, wpa[...], (NT, ((), ())), preferred_element_type=F32)
        dsg_ref[...] = lax.dot_general(dps, wps[...], (NT, ((), ())), preferred_element_type=F32)

    half = pl.BlockSpec((tm, ATTN_W), lambda i: (i, 0))
    full = pl.BlockSpec((tm, D_MODEL), lambda i: (i, 0))
    return pl.pallas_call(
        body, grid=(t // tm,),
        in_specs=[full, pl.BlockSpec((tm, D_MODEL), lambda i: (i, 1)),
                  pl.BlockSpec((tm, D_MODEL), lambda i: (i, 2)), full, full,
                  _resident((ATTN_W, D_MODEL)), _resident((SGU_W, D_MODEL)), _resident((D_MODEL, D_MODEL))],
        out_specs=[pl.BlockSpec((tm, GU_COLS), lambda i: (i, 0)), full, full, half, half],
        out_shape=[jax.ShapeDtypeStruct((t, GU_COLS), BF16), jax.ShapeDtypeStruct((t, D_MODEL), BF16),
                   jax.ShapeDtypeStruct((t, D_MODEL), BF16), jax.ShapeDtypeStruct((t, ATTN_W), F32),
                   jax.ShapeDtypeStruct((t, SGU_W), F32)],
        compiler_params=_cparams(1), name="merge_bwd")(dx1b, gu, gu, pa, ps, w_pa, w_ps, w_out)


def _token_call(name, body, t, tm, ins, outs, reds=(), scratch=()):
    return pl.pallas_call(
        body, grid=(t // tm,), in_specs=[s for _, s in ins],
        out_specs=[o[2] for o in outs] + [_full(r) for r in reds],
        out_shape=[jax.ShapeDtypeStruct(o[0], o[1]) for o in outs] + [jax.ShapeDtypeStruct(r, F32) for r in reds],
        scratch_shapes=list(scratch), compiler_params=_cparams(1), name=name)(*[a for a, _ in ins])


def _rows_spec(tm, width):
    return pl.BlockSpec((tm, width), lambda i: (i, 0))


def _chips_spec(tm):
    return pl.BlockSpec((N_CHIPS, tm, FF_SHARD), lambda i: (0, i, 0))


def _zero_at_start(*refs):
    @pl.when(pl.program_id(0) == 0)
    def _():
        for r in refs:
            r[...] = jnp.zeros(r.shape, r.dtype)


def _ffn_fwd(h2, w_g, w_u):
    t = h2.shape[0]
    tm = min(t, 512)

    def body(h_ref, wg_ref, wu_ref, a_ref, b_ref, ff_ref):
        h = h_ref[...]
        for s in range(N_CHIPS):
            a = jnp.dot(h, wg_ref[s], preferred_element_type=F32)
            b = jnp.dot(h, wu_ref[s], preferred_element_type=F32)
            a_ref[s] = a.astype(BF16)
            b_ref[s] = b.astype(BF16)
            ff_ref[s] = (a * _sigmoid(a) * b).astype(BF16)

    shp = (N_CHIPS, t, FF_SHARD)
    w_spec = _resident((N_CHIPS, D_MODEL, FF_SHARD))
    return _token_call("ffn_fwd", body, t, tm, [(h2, _rows_spec(tm, D_MODEL)), (w_g, w_spec), (w_u, w_spec)],
                       [(shp, BF16, _chips_spec(tm))] * 3)


def _ffn_down_loss(ff, w_d, x1, tgt, gf):
    t = x1.shape[0]
    tm = min(t, 512)

    def body(ff_ref, wd_ref, x1_ref, tgt_ref, g_ref, dx2_ref, dx2b_ref, loss_ref, dgf_ref):
        _zero_at_start(loss_ref, dgf_ref)
        acc = jnp.dot(ff_ref[0], wd_ref[0], preferred_element_type=F32)
        for s in range(1, N_CHIPS):
            acc = acc + jnp.dot(ff_ref[s], wd_ref[s], preferred_element_type=F32)
        x2 = x1_ref[...] + acc
        g = g_ref[...]
        xhat, rr = _rms_stats(x2)
        diff = xhat * g - tgt_ref[...]
        rows = jnp.sum(diff * diff, axis=1, keepdims=True)
        loss_ref[...] += jnp.broadcast_to(jnp.sum(rows, axis=0, keepdims=True) * (0.5 / D_MODEL), (1, LANES))
        dy = diff * (1.0 / D_MODEL)
        dgf_ref[...] += jnp.sum(dy * xhat, axis=0, keepdims=True)
        dx2 = _rms_bwd(dy, xhat, rr, g)
        dx2_ref[...] = dx2
        dx2b_ref[...] = dx2.astype(BF16)

    row = _rows_spec(tm, D_MODEL)
    return _token_call("ffn_down_loss", body, t, tm,
                       [(ff, _chips_spec(tm)), (w_d, _resident((N_CHIPS, FF_SHARD, D_MODEL))), (x1, row), (tgt, row),
                        (gf, _full((1, D_MODEL)))],
                       [((t, D_MODEL), F32, row), ((t, D_MODEL), BF16, row)], reds=[(1, LANES), (1, D_MODEL)])


def _ffn_bwd_act(dx2b, w_d, a, b):
    t = dx2b.shape[0]
    tm = min(t, 512)

    def body(d_ref, wd_ref, a_ref, b_ref, da_ref, db_ref):
        d = d_ref[...]
        for s in range(N_CHIPS):
            dff = lax.dot_general(d, wd_ref[s], (NT, ((), ())), preferred_element_type=F32)
            av, bv = a_ref[s].astype(F32), b_ref[s].astype(F32)
            sg = _sigmoid(av)
            da_ref[s] = (dff * bv * (sg * (1.0 + av * (1.0 - sg)))).astype(BF16)
            db_ref[s] = (dff * (av * sg)).astype(BF16)

    shp = (N_CHIPS, t, FF_SHARD)
    return _token_call("ffn_bwd_act", body, t, tm,
                       [(dx2b, _rows_spec(tm, D_MODEL)), (w_d, _resident((N_CHIPS, FF_SHARD, D_MODEL))),
                        (a, _chips_spec(tm)), (b, _chips_spec(tm))],
                       [(shp, BF16, _chips_spec(tm))] * 2)


def _ffn_bwd_in(da, db, w_g, w_u, x1, dx2, g2):
    t = x1.shape[0]
    tm = min(t, 512)

    def body(da_ref, db_ref, wg_ref, wu_ref, x1_ref, dx2_ref, g_ref, dx1_ref, dx1b_ref, dg_ref):
        _zero_at_start(dg_ref)
        acc = None
        for s in range(N_CHIPS):
            part = (lax.dot_general(da_ref[s], wg_ref[s], (NT, ((), ())), preferred_element_type=F32)
                    + lax.dot_general(db_ref[s], wu_ref[s], (NT, ((), ())), preferred_element_type=F32))
            acc = part if acc is None else acc + part
        xhat, rr = _rms_stats(x1_ref[...])
        dg_ref[...] += jnp.sum(acc * xhat, axis=0, keepdims=True)
        dx1 = dx2_ref[...] + _rms_bwd(acc, xhat, rr, g_ref[...])
        dx1_ref[...] = dx1
        dx1b_ref[...] = dx1.astype(BF16)

    row = _rows_spec(tm, D_MODEL)
    w_spec = _resident((N_CHIPS, D_MODEL, FF_SHARD))
    return _token_call("ffn_bwd_in", body, t, tm,
                       [(da, _chips_spec(tm)), (db, _chips_spec(tm)), (w_g, w_spec), (w_u, w_spec), (x1, row), (dx2, row),
                        (g2, _full((1, D_MODEL)))],
                       [((t, D_MODEL), F32, row), ((t, D_MODEL), BF16, row)], reds=[(1, D_MODEL)])


def _group_dh(d, w_refs):
    dh = None
    for part, w_ref in enumerate(w_refs):
        term = lax.dot_general(d[:, part * ATTN_W:(part + 1) * ATTN_W], w_ref[...], (NT, ((), ())),
                               preferred_element_type=F32)
        dh = term if dh is None else dh + term
    return dh


def _in_proj_bwd(dgu, dqkvs, w_in, x, dx1, g1):
    t = x.shape[0]
    tile = min(t, TILE)
    ngroups = len(DILATIONS)

    def body(*refs):
        dgu_ref, dq_refs = refs[0], refs[1:1 + ngroups]
        w0_ref, w1_ref = refs[1 + ngroups:3 + ngroups]
        wg_refs = [refs[3 + ngroups + 3 * g:6 + ngroups + 3 * g] for g in range(ngroups)]
        x_ref, dx1_ref, g_ref, dx_ref, dg_ref, slab = refs[3 + 4 * ngroups:]
        _zero_at_start(dg_ref)
        dh = lax.dot_general(dgu_ref[:, 0:GU_HALF], w0_ref[...], (NT, ((), ())), preferred_element_type=F32)
        dh = dh + lax.dot_general(dgu_ref[:, GU_HALF:], w1_ref[...], (NT, ((), ())), preferred_element_type=F32)
        dh = dh + _group_dh(dq_refs[0][0], wg_refs[0])
        for g in range(1, ngroups):
            dil = DILATIONS[g]
            part = _group_dh(dq_refs[g][...].reshape(tile, GROUP_COLS), wg_refs[g])
            for r in range(dil):
                _put_class_rows(slab, r, dil, part[r * (tile // dil):(r + 1) * (tile // dil)])
            dh = dh + _from_slabs(slab)
        xhat, rr = _rms_stats(x_ref[...])
        dg_ref[...] += jnp.sum(dh * xhat, axis=0, keepdims=True)
        dx_ref[...] = dx1_ref[...] + _rms_bwd(dh, xhat, rr, g_ref[...])

    row = _rows_spec(tile, D_MODEL)
    group_ins = [(dqkvs[g].reshape(d, t // d, GROUP_COLS), _group_spec(d, tile, GROUP_COLS)) for g, d in enumerate(DILATIONS)]
    w_specs = _gu_w_specs() + [s for g in range(ngroups) for s in _group_w_specs(g)]
    return _token_call(
        "in_proj_bwd", body, t, tile,
        [(dgu, _rows_spec(tile, GU_COLS))] + group_ins + [(w_in, s) for s in w_specs]
        + [(x, row), (dx1, row), (g1, _full((1, D_MODEL)))],
        [((t, D_MODEL), F32, row)], reds=[(1, D_MODEL)], scratch=[_slabs(tile, D_MODEL)])


def _epi_bf16(acc, e, o, r, ids):
    o[0][...] = acc.astype(BF16)


WGRAD_TK = 2048


def _wgrad_2d(name, a, b, tm, tn):
    t, k1 = a.shape
    n = b.shape[1]
    tk = min(t, WGRAD_TK)
    return _mm(name, (k1 // tm, n // tn, t // tk),
               [(a, pl.BlockSpec((tk, tm), lambda i, j, k: (k, i)), b, pl.BlockSpec((tk, tn), lambda i, j, k: (k, j)))],
               TN, (tm, tn), _epi_bf16, outs=[((k1, n), BF16, pl.BlockSpec((tm, tn), lambda i, j, k: (i, j)))])[0]


def _wgrad_in(hs, dgu, dqkvs):
    t = dgu.shape[0]
    tk = min(t, WGRAD_TK)
    gu_block = QKV_BLOCKS * ATTN_W // GU_HALF
    parts = [(hs[0], dgu, GU_HALF, lambda j: j + gu_block)]
    parts += [(hs[g].reshape(t, D_MODEL), dqkvs[g], ATTN_W, lambda j, g=g: _w_in_block(j, g)) for g in range(3)]
    dst = None
    for n, (a, b, tn, block_of) in enumerate(parts):
        dst = _mm(f"wgrad_in_{n}", (1, b.shape[1] // tn, t // tk),
                  [(a, pl.BlockSpec((tk, D_MODEL), lambda i, j, k: (k, 0)), b,
                    pl.BlockSpec((tk, tn), lambda i, j, k: (k, j)))],
                  TN, (D_MODEL, tn), _epi_bf16,
                  extras=[] if dst is None else [(dst, pl.BlockSpec(memory_space=pl.ANY))],
                  outs=[((D_MODEL, IN_COLS), BF16,
                         pl.BlockSpec((D_MODEL, tn), lambda i, j, k, block_of=block_of: (0, block_of(j))))],
                  aliases=None if dst is None else {2: 0})[0]
    return dst


def _wgrad_ff_in(name, h2, da):
    t = h2.shape[0]
    tk = min(t, WGRAD_TK)
    return _mm(name, (N_CHIPS, 1, t // tk),
               [(h2, pl.BlockSpec((tk, D_MODEL), lambda i, j, k: (k, 0)),
                 da, pl.BlockSpec((None, tk, FF_SHARD), lambda i, j, k: (i, k, 0)))],
               TN, (D_MODEL, FF_SHARD), _epi_bf16,
               outs=[((N_CHIPS, D_MODEL, FF_SHARD), BF16, pl.BlockSpec((None, D_MODEL, FF_SHARD), lambda i, j, k: (i, 0, 0)))])[0]


def _wgrad_ff_down(ff, dx2b):
    t = dx2b.shape[0]
    tk = min(t, WGRAD_TK)
    return _mm("wgrad_ffn_down", (N_CHIPS, 1, t // tk),
               [(ff, pl.BlockSpec((None, tk, FF_SHARD), lambda i, j, k: (i, k, 0)),
                 dx2b, pl.BlockSpec((tk, D_MODEL), lambda i, j, k: (k, 0)))],
               TN, (FF_SHARD, D_MODEL), _epi_bf16,
               outs=[((N_CHIPS, FF_SHARD, D_MODEL), BF16, pl.BlockSpec((None, FF_SHARD, D_MODEL), lambda i, j, k: (i, 0, 0)))])[0]


def _local_step(x, pos_col, tgt, g1, ln_g, ln_b, w_s, b_s, g2, gf, first_weight, late_weights, on_grads=None):
    tables = _rope_tables(pos_col)
    bias_exp = jnp.repeat(jnp.transpose(b_s), SGU_W // SGU_GROUPS, axis=1)

    hs = _norm_fwd(x, g1)
    w_p = first_weight(hs[0])
    gu, qkvs = _in_proj(hs, w_p, tables)
    os_, ls_ = [], []
    for g, dil in enumerate(DILATIONS):
        o, lse = _attn_fwd(qkvs[g], g, dil)
        os_.append(o)
        ls_.append(lse)
    attn = _combine_fwd(os_, ls_)
    sgu = _sgu_fwd(gu, ln_g, ln_b, w_s, bias_exp)
    w_pa, w_ps, w_out, w_g, w_u, w_d = late_weights(attn)
    pa, ps, merged, x1, h2 = _merge_fwd(attn, sgu, gu, x, w_pa, w_ps, w_out, g2)
    a, b, ff = _ffn_fwd(h2, w_g, w_u)
    dx2, dx2b, loss, dgf = _ffn_down_loss(ff, w_d, x1, tgt, gf)

    da, db = _ffn_bwd_act(dx2b, w_d, a, b)
    dw_d = _wgrad_ff_down(ff, dx2b)
    dx1, dx1b, dg2 = _ffn_bwd_in(da, db, w_g, w_u, x1, dx2, g2)
    dw_g = _wgrad_ff_in("wgrad_ffn_gate", h2, da)
    dw_u = _wgrad_ff_in("wgrad_ffn_up", h2, db)

    dgu, dpa, dps, dattn, dsgu = _merge_bwd(dx1b, gu, pa, ps, w_pa, w_ps, w_out)
    dw_out = _wgrad_2d("wgrad_out", merged, dx1b, D_MODEL, D_MODEL)
    dw_pa = _wgrad_2d("wgrad_proj_attn", attn, dpa, ATTN_W, D_MODEL)
    dw_ps = _wgrad_2d("wgrad_proj_sgu", sgu, dps, SGU_W, D_MODEL)
    if on_grads is not None:
        ln_g = ln_g + on_grads(1, dict(w_proj_attn=dw_pa, w_proj_sgu=dw_ps, w_out=dw_out, w_ffn_gate=dw_g, w_ffn_up=dw_u,
                                       w_ffn_down=dw_d))[:, :SGU_W]
    dgu, dw_s, dbias, dln_g, dln_b = _sgu_bwd(dgu, gu, dsgu, ln_g, ln_b, w_s, bias_exp)
    dos, ccs = _combine_bwd(dattn, os_, ls_)
    dqkvs = [_attn_bwd(qkvs[g], dos[g], ccs[g], ls_[g], *tables[g], g, dil) for g, dil in enumerate(DILATIONS)]
    dw_p = _wgrad_in(hs, dgu, dqkvs)
    if on_grads is not None:
        g1 = g1 + on_grads(0, dict(w_in=dw_p))
    dx, dg1 = _in_proj_bwd(dgu, dqkvs, w_p, x, dx1, g1)

    db_s = jnp.transpose(dbias[:, ::SGU_W // SGU_GROUPS])
    small = dict(loss=loss, norm1_g=dg1, sgu_ln_g=dln_g, sgu_ln_b=dln_b, w_spatial=dw_s, b_spatial=db_s,
                 norm2_g=dg2, final_g=dgf)
    big = dict(w_in=dw_p, w_proj_attn=dw_pa, w_proj_sgu=dw_ps, w_out=dw_out, w_ffn_gate=dw_g, w_ffn_up=dw_u,
               w_ffn_down=dw_d)
    return dx, big, small


def _ew(name, fn, ins, out_dtypes):
    shp = ins[0].shape
    rows, cols = shp
    tr = next((cand for cand in (256, 352, 128) if rows % cand == 0 and rows > cand), rows)

    def body(*refs):
        res = fn(*[r[...] for r in refs[:len(ins)]])
        for o_ref, v in zip(refs[len(ins):], res):
            o_ref[...] = v.astype(o_ref.dtype)

    spec = pl.BlockSpec((tr, cols), lambda i: (i, 0))
    return pl.pallas_call(
        body, grid=(rows // tr,), in_specs=[spec] * len(ins), out_specs=[spec] * len(out_dtypes),
        out_shape=[jax.ShapeDtypeStruct(shp, d) for d in out_dtypes],
        compiler_params=_cparams(1), name=name)(*ins)


def _adamw_math(g, w, m, v):
    m = ADAM_B1 * m + (1.0 - ADAM_B1) * g
    v = ADAM_B2 * v + (1.0 - ADAM_B2) * (g * g)
    m_hat = m / (1.0 - ADAM_B1 ** ADAM_STEP)
    v_hat = v / (1.0 - ADAM_B2 ** ADAM_STEP)
    delta = -ADAM_LR * (m_hat / (jnp.sqrt(v_hat) + ADAM_EPS) + ADAM_WD * w)
    return delta, m, v


def _adamw(name, g, w, m, v):
    return _ew(name, lambda g_, w_, m_, v_: (g_,) + _adamw_math(g_, w_, m_, v_), [g, w, m, v], [F32] * 4)


VMEM_SPEC = pl.BlockSpec(memory_space=pltpu.VMEM)


def _for_row_chunks(rows, fn):
    ck = next(c for c in (64, 32, 16) if rows % c == 0)

    def step(i, carry):
        fn(pl.multiple_of(i * ck, ck), ck)
        return carry

    lax.fori_loop(0, rows // ck, step, 0)


def _place():
    x, y, c = lax.axis_index("x"), lax.axis_index("y"), lax.axis_index("c")
    chips = [(1 - x, y), (x, 1 - y), (1 - x, 1 - y)]
    return x, y, c, 2 * x + y, chips


def _rows(ref, start, size):
    if len(ref.shape) == 2:
        return ref.at[pl.ds(start, size), :]
    return ref.at[:, pl.ds(start, size), :]


def _comm_call(name, body, ins, out_shapes, scratch, n_remote):
    return pl.pallas_call(
        body, in_specs=[VMEM_SPEC] * len(ins), out_specs=[VMEM_SPEC] * len(out_shapes),
        out_shape=out_shapes,
        scratch_shapes=list(scratch) + [pltpu.SemaphoreType.DMA((n_remote,)), pltpu.SemaphoreType.DMA((n_remote,))],
        compiler_params=pltpu.CompilerParams(vmem_limit_bytes=VMEM_LIMIT), name=name)(*ins)


def _gather_finish(name, shard, landed):
    k_rows, n = shard.shape
    kh = k_rows // 2

    def body(shard_ref, land_ref, out_ref, send, recv):
        x, y, c, me, chips = _place()
        passed = []
        for j, chip in enumerate(chips):
            theirs = 2 * chip[0] + chip[1]
            cp = pltpu.make_async_remote_copy(
                src_ref=land_ref.at[j], dst_ref=_rows(out_ref.at[theirs], c * kh, kh), send_sem=send.at[j],
                recv_sem=recv.at[j], device_id=(x, y, 1 - c), device_id_type=MESH)
            cp.start()
            passed.append(cp)
        mine = out_ref.at[me]

        def put_own(r0, ck):
            mine[pl.ds(r0, ck), :] = shard_ref[pl.ds(r0, ck), :]

        _for_row_chunks(k_rows, put_own)
        for j, chip in enumerate(chips):
            slot = out_ref.at[2 * chip[0] + chip[1]]

            def put_half(r0, ck, j=j, slot=slot):
                slot[pl.ds(pl.multiple_of(c * kh + r0, ck), ck), :] = land_ref[j, pl.ds(r0, ck), :]

            _for_row_chunks(kh, put_half)
        for j, chip in enumerate(chips):
            other = _rows(out_ref.at[2 * chip[0] + chip[1]], (1 - c) * kh, kh)
            pltpu.make_async_remote_copy(src_ref=other, dst_ref=other, send_sem=send.at[j], recv_sem=recv.at[j],
                                         device_id=(x, y, 1 - c), device_id_type=MESH).wait_recv()
        for cp in passed:
            cp.wait_send()

    return _comm_call(name, body, [shard, landed], [jax.ShapeDtypeStruct((N_CHIPS, k_rows, n), shard.dtype)], [], 3)[0]


HBM_SPEC = pl.BlockSpec(memory_space=pltpu.HBM)
SEM_SPEC = pl.BlockSpec(memory_space=pltpu.SEMAPHORE)
DATAFLOW = pltpu.SideEffectType.DATAFLOW_SIDE_EFFECTING
TOKEN_SHAPE = (1, D_MODEL)
N_PEERS = 7


def _peers():
    x, y, c = lax.axis_index("x"), lax.axis_index("y"), lax.axis_index("c")
    flip = lambda v, f: 1 - v if f else v
    return [(flip(x, k & 4), flip(y, k & 2), flip(c, k & 1)) for k in range(1, N_PEERS + 1)]


def _piece_shape(shape):
    return (shape[-2] // 2, shape[2] if len(shape) == 3 else shape[1] // N_CHIPS)


def _device_piece(ref, chip, core):
    kh, n4 = _piece_shape(ref.shape)
    if len(ref.shape) == 3:
        return ref.at[chip, pl.ds(core * kh, kh), :]
    return ref.at[pl.ds(core * kh, kh), pl.ds(chip * n4, n4)]


def _exchange_copies(partials, lands, send, recv):
    return [pltpu.make_async_remote_copy(
        src_ref=_device_piece(partials[t], 2 * px + py, pc), dst_ref=lands[t].at[k], send_sem=send.at[t * N_PEERS + k],
        recv_sem=recv.at[t * N_PEERS + k], device_id=(px, py, pc), device_id_type=MESH)
        for t in range(len(partials)) for k, (px, py, pc) in enumerate(_peers())]


def _gather_copies(shards, lands, send, recv):
    x, y, c, me, chips = _place()
    return [pltpu.make_async_remote_copy(
        src_ref=shards[t], dst_ref=lands[t].at[me], send_sem=send.at[t * 3 + j], recv_sem=recv.at[t * 3 + j],
        device_id=(*chip, c), device_id_type=MESH)
        for t in range(len(shards)) for j, chip in enumerate(chips)]


def _gather_half_copies(shards, lands, send, recv):
    x, y, c, me, chips = _place()
    return [pltpu.make_async_remote_copy(
        src_ref=_rows(shards[t], c * (shards[t].shape[0] // 2), shards[t].shape[0] // 2), dst_ref=lands[t].at[j],
        send_sem=send.at[t * 3 + j], recv_sem=recv.at[t * 3 + j], device_id=(*chip, c), device_id_type=MESH)
        for t in range(len(shards)) for j, chip in enumerate(chips)]


def _split_start(name, copies, per_tensor, srcs, land_shapes):
    nt = len(srcs)
    lands = [lax.empty(s, BF16) for s in land_shapes]
    nsem = nt * per_tensor

    def body(*refs):
        send, recv = refs[2 * nt], refs[2 * nt + 1]
        for cp in copies(refs[:nt], refs[nt:2 * nt], send, recv):
            cp.start()
        refs[-1][...] = jnp.zeros(TOKEN_SHAPE, F32)

    hbm = lambda a: pltpu.with_memory_space_constraint(a, pltpu.HBM)
    outs = pl.pallas_call(
        body, name=name,
        out_shape=[pltpu.SemaphoreType.DMA((nsem,)), pltpu.SemaphoreType.DMA((nsem,))]
        + [pltpu.HBM(s.shape, s.dtype) for s in srcs] + [pltpu.HBM(l.shape, l.dtype) for l in lands]
        + [jax.ShapeDtypeStruct(TOKEN_SHAPE, F32)],
        in_specs=[HBM_SPEC] * (2 * nt), out_specs=[SEM_SPEC, SEM_SPEC] + [HBM_SPEC] * (2 * nt) + [VMEM_SPEC],
        input_output_aliases={i: 2 + i for i in range(2 * nt)},
        compiler_params=pltpu.CompilerParams(has_side_effects=DATAFLOW))(*[hbm(a) for a in list(srcs) + lands])
    return outs[0], outs[1], outs[2:2 + nt], outs[2 + nt:2 + 2 * nt], outs[-1]


def _split_wait(name, copies, send, recv, srcs, lands, after):
    nt = len(srcs)

    def body(*refs):
        for cp in copies(refs[:nt], refs[nt:2 * nt], refs[2 * nt], refs[2 * nt + 1]):
            cp.wait_send()
            cp.wait_recv()

    outs = pl.pallas_call(
        body, name=name,
        out_shape=[pltpu.HBM(s.shape, s.dtype) for s in srcs] + [pltpu.HBM(l.shape, l.dtype) for l in lands],
        in_specs=[HBM_SPEC] * (2 * nt) + [SEM_SPEC, SEM_SPEC, pl.BlockSpec(memory_space=pl.ANY)],
        out_specs=[HBM_SPEC] * (2 * nt), input_output_aliases={i: i for i in range(2 * nt)},
        compiler_params=pltpu.CompilerParams(has_side_effects=DATAFLOW))(*srcs, *lands, send, recv, after)
    return outs[:nt], outs[nt:]


def _device_sum(name, partials, lands):
    nt = len(partials)

    def body(*refs):
        ins, slots, outs, owns = refs[:nt], refs[nt:2 * nt], refs[2 * nt:3 * nt], refs[3 * nt:4 * nt]
        send, recv, loc = refs[4 * nt:]
        x, y, c, me, chips = _place()
        sibling = (x, y, 1 - c)
        loads = [pltpu.make_async_copy(_device_piece(ins[t], me, c), owns[t], loc.at[t]) for t in range(nt)]
        for cp in loads:
            cp.start()
        handed = []
        for t in range(nt):
            kh = owns[t].shape[0]
            loads[t].wait()

            def add(r0, ck, own=owns[t], slot=slots[t], dst=outs[t], kh=kh):
                rows = pl.ds(r0, ck)
                acc = own[rows, :].astype(F32)
                for k in range(N_PEERS):
                    acc = acc + slot[k, rows, :].astype(F32)
                dst[pl.ds(pl.multiple_of(c * kh + r0, ck), ck), :] = acc

            _for_row_chunks(kh, add)
            rc = pltpu.make_async_remote_copy(
                src_ref=_rows(outs[t], c * kh, kh), dst_ref=_rows(outs[t], c * kh, kh), send_sem=send.at[t],
                recv_sem=recv.at[t], device_id=sibling, device_id_type=MESH)
            rc.start()
            handed.append(rc)
        for t in range(nt):
            kh = owns[t].shape[0]
            other = _rows(outs[t], (1 - c) * kh, kh)
            pltpu.make_async_remote_copy(
                src_ref=other, dst_ref=other, send_sem=send.at[t], recv_sem=recv.at[t],
                device_id=sibling, device_id_type=MESH).wait_recv()
        for rc in handed:
            rc.wait_send()

    pieces = [_piece_shape(p.shape) for p in partials]
    return pl.pallas_call(
        body, in_specs=[pl.BlockSpec(memory_space=pl.ANY)] * nt + [VMEM_SPEC] * nt, out_specs=[VMEM_SPEC] * nt,
        out_shape=[jax.ShapeDtypeStruct((2 * kh, n4), F32) for kh, n4 in pieces],
        scratch_shapes=[pltpu.VMEM(p, BF16) for p in pieces]
        + [pltpu.SemaphoreType.DMA((nt,)), pltpu.SemaphoreType.DMA((nt,)), pltpu.SemaphoreType.DMA((nt,))],
        compiler_params=pltpu.CompilerParams(vmem_limit_bytes=VMEM_LIMIT), name=name)(*partials, *lands)


VEC_SHAPE = (8, D_MODEL + LANES)
VEC_SLOTS = dict(norm1_g=(slice(0, 1), slice(0, D_MODEL)), norm2_g=(slice(1, 2), slice(0, D_MODEL)),
                 final_g=(slice(2, 3), slice(0, D_MODEL)), sgu_ln_g=(slice(3, 4), slice(0, SGU_W)),
                 sgu_ln_b=(slice(3, 4), slice(SGU_W, 2 * SGU_W)), b_spatial=(slice(0, 8), slice(D_MODEL, D_MODEL + LANES)),
                 loss=(slice(4, 5), slice(0, LANES)))
VEC_PARAMS = ("norm1_g", "norm2_g", "final_g", "sgu_ln_g", "sgu_ln_b", "b_spatial")
SMALL_PARAMS = VEC_PARAMS + ("w_spatial",)
W_SPATIAL_2D = (SGU_GROUPS * SGU_CHUNK, SGU_CHUNK)


def _small_step(partials, w, m, v):
    def shape2d(name):
        if name == "w_spatial":
            return W_SPATIAL_2D
        rows, cols = VEC_SLOTS[name]
        return (rows.stop - rows.start, cols.stop - cols.start)

    g_names = VEC_PARAMS + ("loss", "w_spatial")
    ng, npar = len(g_names), len(SMALL_PARAMS)

    def pack(dst, parts):
        dst[...] = jnp.zeros(VEC_SHAPE, F32)
        for n, ref in parts.items():
            if n in VEC_SLOTS:
                dst[VEC_SLOTS[n]] = ref[...]

    def reduce_body(*refs):
        g_in = dict(zip(g_names, refs[:ng]))
        vec_out, ws_out, vec, vec_pair, vec_slot, ws_pair, ws_slot, send, recv = refs[ng:]
        x, y, c, me, chips = _place()
        sibling = (x, y, 1 - c)
        pack(vec, g_in)
        copies = []

        def allreduce(k0, src, pair, slot):
            first = pltpu.make_async_remote_copy(src_ref=src, dst_ref=pair, send_sem=send.at[k0], recv_sem=recv.at[k0],
                                                 device_id=sibling, device_id_type=MESH)
            first.start()
            first.wait_recv()
            slot[me] = src[...] + pair[...]
            arrivals = []
            for j, chip in enumerate(chips):
                theirs = 2 * chip[0] + chip[1]
                rc = pltpu.make_async_remote_copy(src_ref=slot.at[me], dst_ref=slot.at[me], send_sem=send.at[k0 + 1 + j],
                                                  recv_sem=recv.at[k0 + 1 + j], device_id=(*chip, c), device_id_type=MESH)
                rc.start()
                arrivals.append(pltpu.make_async_remote_copy(
                    src_ref=slot.at[theirs], dst_ref=slot.at[theirs], send_sem=send.at[k0 + 1 + j],
                    recv_sem=recv.at[k0 + 1 + j], device_id=(*chip, c), device_id_type=MESH))
                copies.append(rc)
            copies.append(first)
            return arrivals

        arrivals = allreduce(0, vec, vec_pair, vec_slot) + allreduce(4, g_in["w_spatial"], ws_pair, ws_slot)
        for a in arrivals:
            a.wait_recv()
        vec_out[...] = ((vec_slot[0] + vec_slot[1]) + vec_slot[2]) + vec_slot[3]

        def spatial(r0, ck):
            rows = pl.ds(r0, ck)
            ws_out[rows, :] = ((ws_slot[0, rows, :] + ws_slot[1, rows, :]) + ws_slot[2, rows, :]) + ws_slot[3, rows, :]

        _for_row_chunks(W_SPATIAL_2D[0], spatial)
        for rc in copies:
            rc.wait_send()

    g_vec, g_ws = pl.pallas_call(
        reduce_body, in_specs=[VMEM_SPEC] * ng, out_specs=[VMEM_SPEC] * 2,
        out_shape=[jax.ShapeDtypeStruct(VEC_SHAPE, F32), jax.ShapeDtypeStruct(W_SPATIAL_2D, F32)],
        scratch_shapes=[pltpu.VMEM(VEC_SHAPE, F32), pltpu.VMEM(VEC_SHAPE, F32), pltpu.VMEM((N_CHIPS,) + VEC_SHAPE, F32),
                        pltpu.VMEM(W_SPATIAL_2D, F32), pltpu.VMEM((N_CHIPS,) + W_SPATIAL_2D, F32),
                        pltpu.SemaphoreType.DMA((8,)), pltpu.SemaphoreType.DMA((8,))],
        name="small_params_allreduce")(*[partials[n].reshape(shape2d(n)) for n in g_names])

    def update_body(*refs):
        gv_ref, gw_ref = refs[:2]
        w_in, m_in, v_in = (dict(zip(SMALL_PARAMS, refs[2 + k * npar:2 + (k + 1) * npar])) for k in range(3))
        o0 = 2 + 3 * npar
        g_out = dict(zip(g_names, refs[o0:o0 + ng]))
        d_out, m_out, v_out = (dict(zip(SMALL_PARAMS, refs[o0 + ng + k * npar:o0 + ng + (k + 1) * npar])) for k in range(3))
        vw, vm, vv = refs[o0 + ng + 3 * npar:]
        pack(vw, w_in)
        pack(vm, m_in)
        pack(vv, v_in)
        d_vec, m_vec, v_vec = _adamw_math(gv_ref[...], vw[...], vm[...], vv[...])
        vw[...] = d_vec
        vm[...] = m_vec
        vv[...] = v_vec
        for n in VEC_PARAMS + ("loss",):
            g_out[n][...] = gv_ref[VEC_SLOTS[n]]
        for n in VEC_PARAMS:
            d_out[n][...] = vw[VEC_SLOTS[n]]
            m_out[n][...] = vm[VEC_SLOTS[n]]
            v_out[n][...] = vv[VEC_SLOTS[n]]

        def spatial(r0, ck):
            rows = pl.ds(r0, ck)
            g = gw_ref[rows, :]
            d_, m_, v_ = _adamw_math(g, w_in["w_spatial"][rows, :], m_in["w_spatial"][rows, :], v_in["w_spatial"][rows, :])
            g_out["w_spatial"][rows, :] = g
            d_out["w_spatial"][rows, :] = d_
            m_out["w_spatial"][rows, :] = m_
            v_out["w_spatial"][rows, :] = v_

        _for_row_chunks(W_SPATIAL_2D[0], spatial)

    ins = [g_vec, g_ws]
    for src in (w, m, v):
        ins += [src[n].reshape(shape2d(n)) for n in SMALL_PARAMS]
    out_shapes = [jax.ShapeDtypeStruct(shape2d(n), F32) for n in g_names + SMALL_PARAMS * 3]
    outs = pl.pallas_call(
        update_body, in_specs=[VMEM_SPEC] * len(ins), out_specs=[VMEM_SPEC] * len(out_shapes), out_shape=out_shapes,
        scratch_shapes=[pltpu.VMEM(VEC_SHAPE, F32)] * 3, name="small_params_update")(*ins)
    grads = dict(zip(g_names, outs[:ng]))
    rest = [dict(zip(SMALL_PARAMS, outs[ng + k * npar:ng + (k + 1) * npar])) for k in range(3)]
    return grads, rest[0], rest[1], rest[2]


BIG = ("w_in", "w_proj_attn", "w_proj_sgu", "w_out", "w_ffn_gate", "w_ffn_up", "w_ffn_down")
COMM_GROUPS = (("w_in",), ("w_proj_attn", "w_proj_sgu", "w_out", "w_ffn_gate", "w_ffn_up", "w_ffn_down"))
WEIGHTS = ("norm1_g", "w_in", "sgu_ln_g", "sgu_ln_b", "w_spatial", "b_spatial", "w_proj_attn", "w_proj_sgu", "w_out",
           "norm2_g", "w_ffn_gate", "w_ffn_up", "w_ffn_down", "final_g")


def _cols_from_chips(g):
    return jnp.transpose(g, (1, 0, 2)).reshape(g.shape[1], N_CHIPS * g.shape[2])


def kernel(x, positions, norm1_g, w_in, sgu_ln_g, sgu_ln_b, w_spatial, b_spatial, w_proj_attn, w_proj_sgu, w_out, norm2_g, w_ffn_gate, w_ffn_up, w_ffn_down, final_g, loss_target, m_norm1_g, m_w_in, m_sgu_ln_g, m_sgu_ln_b, m_w_spatial, m_b_spatial, m_w_proj_attn, m_w_proj_sgu, m_w_out, m_norm2_g, m_w_ffn_gate, m_w_ffn_up, m_w_ffn_down, m_final_g, v_norm1_g, v_w_in, v_sgu_ln_g, v_sgu_ln_b, v_w_spatial, v_b_spatial, v_w_proj_attn, v_w_proj_sgu, v_w_out, v_norm2_g, v_w_ffn_gate, v_w_ffn_up, v_w_ffn_down, v_final_g):
    w = dict(norm1_g=norm1_g, w_in=w_in, sgu_ln_g=sgu_ln_g, sgu_ln_b=sgu_ln_b, w_spatial=w_spatial, b_spatial=b_spatial,
             w_proj_attn=w_proj_attn, w_proj_sgu=w_proj_sgu, w_out=w_out, norm2_g=norm2_g, w_ffn_gate=w_ffn_gate,
             w_ffn_up=w_ffn_up, w_ffn_down=w_ffn_down, final_g=final_g)
    m = dict(norm1_g=m_norm1_g, w_in=m_w_in, sgu_ln_g=m_sgu_ln_g, sgu_ln_b=m_sgu_ln_b, w_spatial=m_w_spatial,
             b_spatial=m_b_spatial, w_proj_attn=m_w_proj_attn, w_proj_sgu=m_w_proj_sgu, w_out=m_w_out, norm2_g=m_norm2_g,
             w_ffn_gate=m_w_ffn_gate, w_ffn_up=m_w_ffn_up, w_ffn_down=m_w_ffn_down, final_g=m_final_g)
    v = dict(norm1_g=v_norm1_g, w_in=v_w_in, sgu_ln_g=v_sgu_ln_g, sgu_ln_b=v_sgu_ln_b, w_spatial=v_w_spatial,
             b_spatial=v_b_spatial, w_proj_attn=v_w_proj_attn, w_proj_sgu=v_w_proj_sgu, w_out=v_w_out, norm2_g=v_norm2_g,
             w_ffn_gate=v_w_ffn_gate, w_ffn_up=v_w_ffn_up, w_ffn_down=v_w_ffn_down, final_g=v_final_g)
    t = x.shape[1]

    shards = {n: _ew(f"cast_{n}", lambda a: (a,), [w[n][0]], [BF16])[0] for n in BIG}
    late = COMM_GROUPS[1]
    k_in, n_in = shards["w_in"].shape
    *first, token = _split_start("gather_start_0", _gather_half_copies, 3, [shards["w_in"]], [(3, k_in // 2, n_in)])
    pending = {}

    def first_weight(after):
        srcs, filled = _split_wait("gather_wait_0", _gather_half_copies, *first, after)
        gath_in, late_shards = lax.optimization_barrier(
            (_gather_finish("gather_finish_0", srcs[0], filled[0]), [shards[n] for n in late]))
        *pending["late"], _ = _split_start(
            "gather_start_1", _gather_copies, 3, late_shards, [(N_CHIPS,) + s.shape for s in late_shards])
        return _cols_from_chips(gath_in)

    def late_weights(after):
        srcs, filled = _split_wait("gather_wait_1", _gather_copies, *pending["late"], after)
        me = 2 * lax.axis_index("x") + lax.axis_index("y")
        gath = {n: lax.dynamic_update_slice(f, s[None], (me, 0, 0)) for n, f, s in zip(late, filled, srcs)}
        return (_cols_from_chips(gath["w_proj_attn"]), _cols_from_chips(gath["w_proj_sgu"]),
                gath["w_out"].reshape(D_MODEL, D_MODEL), gath["w_ffn_gate"], gath["w_ffn_up"], gath["w_ffn_down"])

    exchanges = {}

    def on_grads(i, partials):
        if "w_out" in partials:
            partials["w_out"] = partials["w_out"].reshape(N_CHIPS, D_MODEL // N_CHIPS, D_MODEL)
        parts = [partials[n] for n in COMM_GROUPS[i]]
        *exchanges[i], started = _split_start(
            f"rs_exchange_start_{i}", _exchange_copies, N_PEERS, parts, [(N_PEERS,) + _piece_shape(p.shape) for p in parts])
        return started

    dx, _, small = _local_step(
        x[0], positions.reshape(t, 1), loss_target[0], norm1_g + token, sgu_ln_g, sgu_ln_b, w_spatial[0], b_spatial[0],
        norm2_g, final_g.reshape(1, D_MODEL), first_weight, late_weights, on_grads=on_grads)

    grads = {}
    for i in (1, 0):
        parts, filled = _split_wait(f"rs_exchange_wait_{i}", _exchange_copies, *exchanges[i], dx)
        grads.update(zip(COMM_GROUPS[i], _device_sum(f"rs_device_sum_{i}", parts, filled)))

    delta, new_m, new_v = {}, {}, {}
    for n in BIG:
        shp = w[n].shape
        g_, d_, m_, v_ = _adamw(f"adamw_{n}", grads[n], w[n][0], m[n][0], v[n][0])
        grads[n], delta[n], new_m[n], new_v[n] = g_.reshape(shp), d_.reshape(shp), m_.reshape(shp), v_.reshape(shp)

    g_s, d_s, m_s, v_s = _small_step(small, w, m, v)
    loss = g_s["loss"][0, 0]
    for n in SMALL_PARAMS:
        shp = w[n].shape
        grads[n], delta[n], new_m[n], new_v[n] = (a[n].reshape(shp) for a in (g_s, d_s, m_s, v_s))

    return (loss, dx.reshape(x.shape), *[grads[n] for n in WEIGHTS], *[delta[n] for n in WEIGHTS],
            *[new_m[n] for n in WEIGHTS], *[new_v[n] for n in WEIGHTS])
```

```python
import functools

import numpy as np
import jax
import jax.numpy as jnp
from jax import lax
from jax.experimental import pallas as pl
from jax.experimental.pallas import tpu as pltpu

F32, BF16 = jnp.float32, jnp.bfloat16
MESH = pl.DeviceIdType.MESH

D_MODEL = 1024
HEAD_DIM = 64
ATTN_W = 512
DILATIONS = (1, 4, 16)
BLK = 128
ATTN_BLOCKS_PER_STEP = 8
ROPE_DIM = 16
ROPE_THETA = 500000.0
SGU_W = 512
SGU_CHUNK = 128
SGU_GROUPS = 8
D_FF = 2816
N_CHIPS = 4
FF_SHARD = D_FF // N_CHIPS
IN_COLS = 7680
EPS = 1e-6
NEG = -1e30
LANES = 128
VMEM_LIMIT = 52 * 1024 * 1024

ADAM_LR, ADAM_B1, ADAM_B2, ADAM_EPS, ADAM_WD, ADAM_STEP = 0.001, 0.9, 0.999, 1e-08, 0.01, 10

QKV_BLOCKS = 9


def _w_in_block(part, g):
    return part * len(DILATIONS) + g


def _cparams(ngrid):
    return pltpu.CompilerParams(dimension_semantics=("arbitrary",) * ngrid, vmem_limit_bytes=VMEM_LIMIT)


def _full(shape):
    return pl.BlockSpec(shape, lambda *_: (0,) * len(shape))


def _resident(shape):
    return pl.BlockSpec(shape, lambda *_: (0,) * len(shape), pipeline_mode=pl.Buffered(1))


NN = ((1,), (0,))
NT = ((1,), (1,))
TN = ((0,), (0,))


def _mm(name, grid, pairs, dims, acc_shape, epi, *, extras=(), outs=(), reds=(), aliases=None):
    nk = grid[-1]
    npair, nex, nout, nred = len(pairs), len(extras), len(outs), len(reds)

    def body(*refs):
        a_refs = refs[:npair]
        b_refs = refs[npair:2 * npair]
        p0 = 2 * npair
        e_refs = refs[p0:p0 + nex]
        o_refs = refs[p0 + nex:p0 + nex + nout]
        r_refs = refs[p0 + nex + nout:p0 + nex + nout + nred]
        ids = [pl.program_id(a) for a in range(len(grid))]
        k = ids[-1]
        if nred:
            first = ids[0] == 0
            for v in ids[1:]:
                first = first & (v == 0)

            @pl.when(first)
            def _():
                for r in r_refs:
                    r[...] = jnp.zeros(r.shape, r.dtype)

        part = None
        for a_ref, b_ref in zip(a_refs, b_refs):
            d = lax.dot_general(a_ref[...], b_ref[...], (dims, ((), ())), preferred_element_type=F32)
            part = d if part is None else part + d
        if nk == 1:
            epi(part, e_refs, o_refs, r_refs, ids)
        else:
            acc_ref = refs[-1]

            @pl.when(k == 0)
            def _():
                acc_ref[...] = part

            @pl.when(k > 0)
            def _():
                acc_ref[...] += part

            @pl.when(k == nk - 1)
            def _():
                epi(acc_ref[...], e_refs, o_refs, r_refs, ids)

    in_specs = [p[1] for p in pairs] + [p[3] for p in pairs] + [e[1] for e in extras]
    args = [p[0] for p in pairs] + [p[2] for p in pairs] + [e[0] for e in extras]
    out_shape = [jax.ShapeDtypeStruct(o[0], o[1]) for o in outs] + [jax.ShapeDtypeStruct(r, F32) for r in reds]
    out_specs = [o[2] for o in outs] + [_full(r) for r in reds]
    scratch_shapes = [pltpu.VMEM(acc_shape, F32)] if nk > 1 else []
    return pl.pallas_call(
        body, grid=grid, in_specs=in_specs, out_specs=out_specs, out_shape=out_shape, scratch_shapes=scratch_shapes,
        input_output_aliases=aliases or {}, compiler_params=_cparams(len(grid)), name=name)(*args)


def _rope(v, cos_t, sin_t):
    half = ROPE_DIM // 2
    first = (lax.broadcasted_iota(jnp.int32, cos_t.shape, 1) % HEAD_DIM) < half
    outs = []
    for cs in range(v.shape[1] // LANES):
        x = v[:, cs * LANES:(cs + 1) * LANES]
        partner = jnp.where(first, pltpu.roll(x, LANES - half, axis=1), pltpu.roll(x, half, axis=1))
        outs.append(x * cos_t + partner * sin_t)
    return outs[0] if len(outs) == 1 else jnp.concatenate(outs, axis=1)


def _spread_heads(v2, upper):
    other = pltpu.roll(v2, HEAD_DIM, axis=1)
    h0 = jnp.where(upper, other, v2)
    h1 = jnp.where(upper, v2, other)
    return jnp.concatenate([jnp.concatenate([h0, h0], axis=1), jnp.concatenate([h1, h1], axis=1)], axis=0)


def _sigmoid(v):
    return 0.5 * jnp.tanh(0.5 * v) + 0.5


def _rms_stats(v):
    r = lax.rsqrt(jnp.mean(v * v, axis=-1, keepdims=True) + EPS)
    return v * r, r


def _rms_bwd(dy, xhat, r, g):
    dxh = dy * g
    return r * (dxh - xhat * jnp.mean(dxh * xhat, axis=-1, keepdims=True))


def _head_sum_matrix():
    idx = np.arange(ATTN_W) // HEAD_DIM
    return jnp.asarray((idx[:, None] == idx[None, :]).astype(np.float32), dtype=BF16)


def _group_sum(v, e):
    hi = v.astype(BF16)
    lo = (v - hi.astype(F32)).astype(BF16)
    return jnp.dot(hi, e, preferred_element_type=F32) + jnp.dot(lo, e, preferred_element_type=F32)


TILE = 512


def _to_slabs(slab_ref, v):
    for cs in range(slab_ref.shape[0]):
        slab_ref[cs] = v[:, cs * LANES:(cs + 1) * LANES]


def _from_slabs(slab_ref):
    return jnp.concatenate([slab_ref[cs] for cs in range(slab_ref.shape[0])], axis=1)


def _class_rows(slab_ref, r, dil):
    n = slab_ref.shape[1] // dil
    return jnp.concatenate([slab_ref.at[cs][pl.ds(r, n, stride=dil), :] for cs in range(slab_ref.shape[0])], axis=1)


def _put_class_rows(slab_ref, r, dil, v):
    n = slab_ref.shape[1] // dil
    for cs in range(slab_ref.shape[0]):
        slab_ref.at[cs][pl.ds(r, n, stride=dil), :] = v[:, cs * LANES:(cs + 1) * LANES]


def _natural_from_group(slab_ref, grp_ref):
    dil = grp_ref.shape[0]
    for r in range(dil):
        _put_class_rows(slab_ref, r, dil, grp_ref[r].astype(F32))
    return _from_slabs(slab_ref)


def _group_from_natural(slab_ref, grp_ref, v):
    dil = grp_ref.shape[0]
    _to_slabs(slab_ref, v)
    for r in range(dil):
        grp_ref[r] = _class_rows(slab_ref, r, dil).astype(grp_ref.dtype)


def _group_spec(dil, tile, width):
    return pl.BlockSpec((dil, tile // dil, width), lambda i, *_: (0, i, 0))


def _slabs(tile, width):
    return pltpu.VMEM((width // LANES, tile, LANES), F32)


def _rope_consts():
    lane = np.arange(LANES) % HEAD_DIM
    fi = lane % (ROPE_DIM // 2)
    invf = np.where(lane < ROPE_DIM, ROPE_THETA ** (-(2.0 * fi) / ROPE_DIM), 0.0)
    sgn = np.where(lane < ROPE_DIM // 2, -1.0, np.where(lane < ROPE_DIM, 1.0, 0.0))
    return (jnp.asarray(invf.astype(np.float32)).reshape(1, LANES), jnp.asarray(sgn.astype(np.float32)).reshape(1, LANES))


def _rope_tables(pos_col):
    t = pos_col.shape[0]
    tile = min(t, TILE)
    invf, sgn = _rope_consts()

    def body(p_ref, f_ref, s_ref, c0, s0, c1, s1, c2, s2, slab_c, slab_s):
        ang = p_ref[...].astype(F32) * f_ref[...]
        cos, sin = jnp.cos(ang), jnp.sin(ang) * s_ref[...]
        c0[...] = cos
        s0[...] = sin
        _group_from_natural(slab_c, c1, cos)
        _group_from_natural(slab_s, s1, sin)
        for r in range(DILATIONS[2]):
            c2[r] = _class_rows(slab_c, r, DILATIONS[2])
            s2[r] = _class_rows(slab_s, r, DILATIONS[2])

    nat = pl.BlockSpec((tile, LANES), lambda i: (i, 0))
    specs, shapes = [nat, nat], [(t, LANES)] * 2
    for d in DILATIONS[1:]:
        specs += [_group_spec(d, tile, LANES)] * 2
        shapes += [(d, t // d, LANES)] * 2
    outs = pl.pallas_call(
        body, grid=(t // tile,),
        in_specs=[pl.BlockSpec((tile, 1), lambda i: (i, 0)), _full((1, LANES)), _full((1, LANES))],
        out_specs=specs, out_shape=[jax.ShapeDtypeStruct(s, F32) for s in shapes],
        scratch_shapes=[_slabs(tile, LANES)] * 2,
        compiler_params=_cparams(1), name="rope_tables")(pos_col, invf, sgn)
    return [(outs[2 * g].reshape(t, LANES), outs[2 * g + 1].reshape(t, LANES)) for g in range(len(DILATIONS))]


def _norm_fwd(x, g):
    t = x.shape[0]
    tile = min(t, TILE)

    def body(x_ref, g_ref, h0_ref, h1_ref, h2_ref, slab):
        xhat, _ = _rms_stats(x_ref[...])
        hn = xhat * g_ref[...]
        h0_ref[...] = hn.astype(BF16)
        _group_from_natural(slab, h1_ref, hn)
        for r in range(DILATIONS[2]):
            h2_ref[r] = _class_rows(slab, r, DILATIONS[2]).astype(BF16)

    nat = pl.BlockSpec((tile, D_MODEL), lambda i: (i, 0))
    return pl.pallas_call(
        body, grid=(t // tile,),
        in_specs=[nat, _full((1, D_MODEL))],
        out_specs=[nat] + [_group_spec(d, tile, D_MODEL) for d in DILATIONS[1:]],
        out_shape=[jax.ShapeDtypeStruct((t, D_MODEL), BF16)]
        + [jax.ShapeDtypeStruct((d, t // d, D_MODEL), BF16) for d in DILATIONS[1:]],
        scratch_shapes=[_slabs(tile, D_MODEL)],
        compiler_params=_cparams(1), name="norm1_fwd")(x, g)


GU_COLS = 3072
GROUP_COLS = 1536
GU_HALF = GU_COLS // 2


def _w_in_spec(width, block):
    return pl.BlockSpec((D_MODEL, width), lambda i: (0, block), pipeline_mode=pl.Buffered(1))


def _gu_w_specs():
    first = QKV_BLOCKS * ATTN_W // GU_HALF
    return [_w_in_spec(GU_HALF, first), _w_in_spec(GU_HALF, first + 1)]


def _group_w_specs(g):
    return [_w_in_spec(ATTN_W, _w_in_block(part, g)) for part in range(3)]


def _in_proj(hs, w_in, tables):
    t = hs[0].shape[0]
    tm = min(t, 1024)

    def body_gu(h_ref, w0_ref, w1_ref, o_ref):
        h = h_ref[...]
        o_ref[:, 0:GU_HALF] = jnp.dot(h, w0_ref[...], preferred_element_type=F32).astype(BF16)
        o_ref[:, GU_HALF:] = jnp.dot(h, w1_ref[...], preferred_element_type=F32).astype(BF16)

    gu = _token_call("in_proj_gates_uv", body_gu, t, tm,
                     [(hs[0], _rows_spec(tm, D_MODEL))] + [(w_in, s) for s in _gu_w_specs()],
                     [((t, GU_COLS), BF16, _rows_spec(tm, GU_COLS))])[0]

    qkvs = []
    for g in range(len(DILATIONS)):

        def body_qkv(h_ref, wq_ref, wk_ref, wv_ref, cos_ref, sin_ref, o_ref):
            h = h_ref[...]
            cos_w, sin_w = cos_ref[...], sin_ref[...]
            q = jnp.dot(h, wq_ref[...], preferred_element_type=F32)
            o_ref[:, 0:ATTN_W] = (_rope(q, cos_w, sin_w) * HEAD_DIM ** -0.5).astype(BF16)
            k = jnp.dot(h, wk_ref[...], preferred_element_type=F32)
            o_ref[:, ATTN_W:2 * ATTN_W] = _rope(k, cos_w, sin_w).astype(BF16)
            o_ref[:, 2 * ATTN_W:] = jnp.dot(h, wv_ref[...], preferred_element_type=F32).astype(BF16)

        cos_t, sin_t = tables[g]
        qkvs.append(_token_call(
            f"in_proj_qkv_g{g}", body_qkv, t, tm,
            [(hs[g].reshape(t, D_MODEL), _rows_spec(tm, D_MODEL))] + [(w_in, s) for s in _group_w_specs(g)]
            + [(cos_t, _rows_spec(tm, LANES)), (sin_t, _rows_spec(tm, LANES))],
            [((t, GROUP_COLS), BF16, _rows_spec(tm, GROUP_COLS))])[0])
    return gu, qkvs


def _attn_masks(n):
    row = lax.broadcasted_iota(jnp.int32, (2 * BLK, 2 * BLK), 0) % BLK
    col = lax.broadcasted_iota(jnp.int32, (2 * BLK, 2 * BLK), 1)
    diff = BLK + row - col
    valid = (diff >= 0) & (diff <= BLK) & ((col >= BLK) | (n > 0))
    upper = lax.broadcasted_iota(jnp.int32, (BLK, LANES), 1) >= HEAD_DIM
    return valid, upper


def _stack_heads(v2, upper):
    zero = jnp.zeros_like(v2)
    return jnp.concatenate([jnp.where(upper, zero, v2), jnp.where(upper, v2, zero)], axis=0)


def _unstack_heads(v, upper):
    return jnp.where(upper, v[BLK:], v[:BLK])


def _attn_fwd(qkv, g, dil):
    t = qkv.shape[0]
    length = t // dil
    nb = length // BLK
    per_step = min(nb, ATTN_BLOCKS_PER_STEP)
    view = qkv.reshape(dil, length, GROUP_COLS)

    def body(q_ref, kc_ref, kp_ref, vc_ref, vp_ref, o_ref, l_ref, kwin, vwin):
        n = pl.program_id(1)
        kwin[0:BLK] = kp_ref[...]
        kwin[BLK:] = kc_ref[...]
        vwin[0:BLK] = vp_ref[...]
        vwin[BLK:] = vc_ref[...]

        def block(b, carry):
            valid, upper = _attn_masks(n * per_step + b)
            rows = pl.ds(pl.multiple_of(b * BLK, BLK), BLK)
            window = pl.ds(pl.multiple_of(b * BLK, BLK), 2 * BLK)
            for p in range(ATTN_W // LANES):
                sl = slice(p * LANES, (p + 1) * LANES)
                qs = _stack_heads(q_ref[rows, sl], upper)
                s = lax.dot_general(qs, kwin[window, sl], (NT, ((), ())), preferred_element_type=F32)
                s = jnp.where(valid, s, NEG)
                m = jnp.max(s, axis=1, keepdims=True)
                pe = jnp.exp(s - m)
                den = jnp.sum(pe, axis=1, keepdims=True)
                o = jnp.dot(pe.astype(BF16), vwin[window, sl], preferred_element_type=F32) / den
                lse = jnp.broadcast_to(m + jnp.log(den), (2 * BLK, LANES))
                o_ref[rows, sl] = _unstack_heads(o, upper).astype(BF16)
                l_ref[rows, sl] = _unstack_heads(lse, upper)
            return carry

        lax.fori_loop(0, per_step, block, 0)

    rows = per_step * BLK
    cur = lambda part: pl.BlockSpec((None, rows, ATTN_W), lambda r, n: (r, n, part))
    prev = lambda part: pl.BlockSpec((None, BLK, ATTN_W), lambda r, n: (r, jnp.maximum(n * per_step - 1, 0), part))
    out_spec = pl.BlockSpec((None, rows, ATTN_W), lambda r, n: (r, n, 0))
    return pl.pallas_call(
        body, grid=(dil, nb // per_step),
        in_specs=[cur(0), cur(1), prev(1), cur(2), prev(2)],
        out_specs=[out_spec, out_spec],
        out_shape=[jax.ShapeDtypeStruct((dil, length, ATTN_W), BF16), jax.ShapeDtypeStruct((dil, length, ATTN_W), F32)],
        scratch_shapes=[pltpu.VMEM((rows + BLK, ATTN_W), BF16)] * 2,
        compiler_params=_cparams(2), name=f"attn_fwd_g{g}")(view, view, view, view, view)


def _alphas(l0, l1, l2):
    m = jnp.maximum(jnp.maximum(l0, l1), l2)
    e0, e1, e2 = jnp.exp(l0 - m), jnp.exp(l1 - m), jnp.exp(l2 - m)
    inv = 1.0 / (e0 + e1 + e2)
    return e0 * inv, e1 * inv, e2 * inv


def _natural_group_values(o_refs, l_refs, slabs):
    os_ = [o_refs[0][0].astype(F32)] + [_natural_from_group(slabs[2 * g - 2], o_refs[g]) for g in (1, 2)]
    ls_ = [l_refs[0][0]] + [_natural_from_group(slabs[2 * g - 1], l_refs[g]) for g in (1, 2)]
    return os_, ls_


def _combine_fwd(os_, ls_):
    t = os_[0].shape[1]
    tile = min(t, TILE)

    def body(o0, o1, o2, l0, l1, l2, a_ref, *slabs):
        ov, lv = _natural_group_values((o0, o1, o2), (l0, l1, l2), slabs)
        a0, a1, a2 = _alphas(*lv)
        a_ref[...] = (a0 * ov[0] + a1 * ov[1] + a2 * ov[2]).astype(BF16)

    specs = [_group_spec(d, tile, ATTN_W) for d in DILATIONS]
    return pl.pallas_call(
        body, grid=(t // tile,), in_specs=specs * 2, out_specs=pl.BlockSpec((tile, ATTN_W), lambda i: (i, 0)),
        out_shape=jax.ShapeDtypeStruct((t, ATTN_W), BF16),
        scratch_shapes=[_slabs(tile, ATTN_W)] * 4,
        compiler_params=_cparams(1), name="combine_fwd")(*os_, *ls_)


def _combine_bwd(dattn, os_, ls_):
    t = dattn.shape[0]
    tile = min(t, TILE)
    e = _head_sum_matrix()

    def body(d_ref, o0, o1, o2, l0, l1, l2, e_ref, do0, do1, do2, c0, c1, c2, *slabs):
        ov, lv = _natural_group_values((o0, o1, o2), (l0, l1, l2), slabs)
        alphas = _alphas(*lv)
        d = d_ref[...]
        attn = alphas[0] * ov[0] + alphas[1] * ov[1] + alphas[2] * ov[2]
        s = _group_sum(d * attn, e_ref[...])
        do0[0] = (alphas[0] * d).astype(BF16)
        c0[0] = -alphas[0] * s
        for g, do_ref, c_ref in ((1, do1, c1), (2, do2, c2)):
            _group_from_natural(slabs[2 * g - 2], do_ref, alphas[g] * d)
            _group_from_natural(slabs[2 * g - 1], c_ref, -alphas[g] * s)

    specs = [_group_spec(d, tile, ATTN_W) for d in DILATIONS]
    shapes = [(d, t // d, ATTN_W) for d in DILATIONS]
    outs = pl.pallas_call(
        body, grid=(t // tile,),
        in_specs=[pl.BlockSpec((tile, ATTN_W), lambda i: (i, 0))] + specs * 2 + [_full((ATTN_W, ATTN_W))],
        out_specs=specs * 2,
        out_shape=[jax.ShapeDtypeStruct(s, BF16) for s in shapes] + [jax.ShapeDtypeStruct(s, F32) for s in shapes],
        scratch_shapes=[_slabs(tile, ATTN_W)] * 4,
        compiler_params=_cparams(1), name="combine_bwd")(dattn, *os_, *ls_, e)
    return outs[:3], outs[3:]


def _attn_bwd(qkv, do, cc, lse, cos_t, sin_t, g, dil):
    t = qkv.shape[0]
    length = t // dil
    nb = length // BLK
    per_step = min(nb, ATTN_BLOCKS_PER_STEP)
    nsteps = nb // per_step
    rows_per_step = per_step * BLK
    qkv_v = qkv.reshape(dil, length, GROUP_COLS)
    cos_v, sin_v = (a.reshape(dil, length, LANES) for a in (cos_t, sin_t))
    scale = HEAD_DIM ** -0.5
    dq_cols, dk_cols, dv_cols = (slice(i * ATTN_W, (i + 1) * ATTN_W) for i in range(3))

    def body(q_ref, kc_ref, kp_ref, vc_ref, vp_ref, do_ref, c_ref, l_ref, cosc, sinc, cosp, sinp,
             out_ref, acc, kwin, vwin, cwin, swin):
        n = pl.program_id(1)

        def one_block(b):
            valid, upper = _attn_masks(n * per_step + b)
            start = b * BLK if isinstance(b, int) else pl.multiple_of(b * BLK, BLK)
            rows, before, window = pl.ds(start, BLK), pl.ds(start, BLK), pl.ds(start, 2 * BLK)
            own = pl.ds(start + BLK, BLK)
            dq_parts, dkp_parts, dkc_parts, dvp_parts, dvc_parts = [], [], [], [], []
            for p in range(ATTN_W // LANES):
                sl = slice(p * LANES, (p + 1) * LANES)
                qs = _stack_heads(q_ref[rows, sl], upper)
                dos = _stack_heads(do_ref[rows, sl], upper)
                k2 = kwin[window, sl]
                l_col = _spread_heads(l_ref[rows, sl], upper)
                c_col = _spread_heads(c_ref[rows, sl], upper)
                s = lax.dot_general(qs, k2, (NT, ((), ())), preferred_element_type=F32)
                pe = jnp.exp(jnp.where(valid, s, NEG) - l_col)
                dpv = lax.dot_general(dos, vwin[window, sl], (NT, ((), ())), preferred_element_type=F32)
                ds = (pe * (dpv + c_col)).astype(BF16)
                dq2 = _unstack_heads(jnp.dot(ds, k2, preferred_element_type=F32), upper)
                dk2 = lax.dot_general(ds, qs, (TN, ((), ())), preferred_element_type=F32)
                dv2 = lax.dot_general(pe.astype(BF16), dos, (TN, ((), ())), preferred_element_type=F32)
                dq_parts.append(dq2)
                dkp_parts.append(dk2[:BLK])
                dkc_parts.append(dk2[BLK:])
                dvp_parts.append(dv2[:BLK])
                dvc_parts.append(dv2[BLK:])
            dq = _rope(jnp.concatenate(dq_parts, axis=1) * scale, cwin[own, :], swin[own, :])
            dkc = _rope(jnp.concatenate(dkc_parts, axis=1), cwin[own, :], swin[own, :])
            dkp = _rope(jnp.concatenate(dkp_parts, axis=1), cwin[before, :], swin[before, :])
            return dq, dkp, dkc, jnp.concatenate(dvp_parts, axis=1), jnp.concatenate(dvc_parts, axis=1)

        @pl.when(n < nsteps)
        def _():
            kwin[0:BLK] = kp_ref[...]
            kwin[BLK:] = kc_ref[...]
            vwin[0:BLK] = vp_ref[...]
            vwin[BLK:] = vc_ref[...]
            cwin[0:BLK] = cosp[...]
            cwin[BLK:] = cosc[...]
            swin[0:BLK] = -sinp[...]
            swin[BLK:] = -sinc[...]
            dq, dkp, dkc, dvp, dvc = one_block(0)
            last = slice(rows_per_step - BLK, rows_per_step)

            @pl.when(n > 0)
            def _():
                if per_step > 1:
                    out_ref[0:rows_per_step - BLK, :] = acc[0:rows_per_step - BLK, :].astype(BF16)
                out_ref[last, dq_cols] = acc[last, dq_cols].astype(BF16)
                out_ref[last, dk_cols] = (acc[last, dk_cols] + dkp).astype(BF16)
                out_ref[last, dv_cols] = (acc[last, dv_cols] + dvp).astype(BF16)

            acc[0:BLK, dq_cols] = dq
            acc[0:BLK, dk_cols] = dkc
            acc[0:BLK, dv_cols] = dvc

            def later(b, carry):
                dq, dkp, dkc, dvp, dvc = one_block(b)
                start = pl.multiple_of(b * BLK, BLK)
                before, rows = pl.ds(start - BLK, BLK), pl.ds(start, BLK)
                acc[before, dk_cols] += dkp
                acc[before, dv_cols] += dvp
                acc[rows, dq_cols] = dq
                acc[rows, dk_cols] = dkc
                acc[rows, dv_cols] = dvc
                return carry

            lax.fori_loop(1, per_step, later, 0)

        @pl.when(n == flush_at)
        def _():
            out_ref[...] = acc[...].astype(BF16)

    flush_at = nsteps - 1 if nsteps == 1 else nsteps
    out_lag = 0 if nsteps == 1 else 1
    nc = lambda n: jnp.minimum(n, nsteps - 1)
    npv = lambda n: jnp.maximum(jnp.minimum(n, nsteps - 1) * per_step - 1, 0)
    cur = lambda part: pl.BlockSpec((None, rows_per_step, ATTN_W), lambda r, n: (r, nc(n), part))
    prev = lambda part: pl.BlockSpec((None, BLK, ATTN_W), lambda r, n: (r, npv(n), part))
    row = pl.BlockSpec((None, rows_per_step, ATTN_W), lambda r, n: (r, nc(n), 0))
    tab_c = pl.BlockSpec((None, rows_per_step, LANES), lambda r, n: (r, nc(n), 0))
    tab_p = pl.BlockSpec((None, BLK, LANES), lambda r, n: (r, npv(n), 0))
    out_spec = pl.BlockSpec((None, rows_per_step, GROUP_COLS), lambda r, n: (r, jnp.maximum(n - out_lag, 0), 0))
    out = pl.pallas_call(
        body, grid=(dil, nsteps + out_lag),
        in_specs=[cur(0), cur(1), prev(1), cur(2), prev(2), row, row, row, tab_c, tab_c, tab_p, tab_p],
        out_specs=out_spec,
        out_shape=jax.ShapeDtypeStruct((dil, length, GROUP_COLS), BF16),
        scratch_shapes=[pltpu.VMEM((rows_per_step, GROUP_COLS), F32)]
        + [pltpu.VMEM((rows_per_step + BLK, ATTN_W), BF16)] * 2 + [pltpu.VMEM((rows_per_step + BLK, LANES), F32)] * 2,
        compiler_params=_cparams(2), name=f"attn_bwd_g{g}")(
            qkv_v, qkv_v, qkv_v, qkv_v, qkv_v, do, cc, lse, cos_v, sin_v, cos_v, sin_v)
    return out.reshape(t, GROUP_COLS)


SQRT_HALF = 0.7071067811865476
INV_SQRT_2PI = 0.3989422804014327


def _sgu_core(uv, g, b, w_ref, bias):
    cdf = 0.5 * (1.0 + lax.erf(uv * SQRT_HALF))
    z = uv * cdf
    u, v = z[:, :SGU_W], z[:, SGU_W:]
    mu = jnp.mean(v, axis=1, keepdims=True)
    xc = v - mu
    rs = lax.rsqrt(jnp.mean(xc * xc, axis=1, keepdims=True) + EPS)
    xhat = xc * rs
    vn = xhat * g + b
    row = lax.broadcasted_iota(jnp.int32, (SGU_CHUNK, SGU_CHUNK), 0)
    col = lax.broadcasted_iota(jnp.int32, (SGU_CHUNK, SGU_CHUNK), 1)
    tril = row >= col
    upper = lax.broadcasted_iota(jnp.int32, (SGU_CHUNK, LANES), 1) >= SGU_W // SGU_GROUPS
    ws, vlo, vhi, mixed = [], [], [], []
    for pr in range(SGU_W // LANES):
        sl = slice(pr * LANES, (pr + 1) * LANES)
        w0 = jnp.where(tril, w_ref[2 * pr], 0.0).astype(BF16)
        w1 = jnp.where(tril, w_ref[2 * pr + 1], 0.0).astype(BF16)
        vn2 = vn[:, sl]
        lo = jnp.where(upper, 0.0, vn2).astype(BF16)
        hi = jnp.where(upper, vn2, 0.0).astype(BF16)
        mixed.append(jnp.dot(w0, lo, preferred_element_type=F32) + jnp.dot(w1, hi, preferred_element_type=F32)
                     + bias[:, sl])
        ws.append((w0, w1))
        vlo.append(lo)
        vhi.append(hi)
    return cdf, u, xhat, rs, jnp.concatenate(mixed, axis=1), ws, vlo, vhi, tril, upper


SGU_STEP = 4 * SGU_CHUNK


def _for_chunks(step_rows, fn):
    def one(ci, carry):
        fn(pl.ds(pl.multiple_of(ci * SGU_CHUNK, SGU_CHUNK), SGU_CHUNK))
        return carry

    lax.fori_loop(0, step_rows // SGU_CHUNK, one, 0)


def _sgu_fwd(gu, ln_g, ln_b, w_s, bias_exp):
    t = gu.shape[0]
    step = min(t, SGU_STEP)

    def body(uv_ref, g_ref, b_ref, w_ref, bias_ref, o_ref):
        def chunk(rows):
            _, u, _, _, mixed, *_ = _sgu_core(uv_ref[rows, :].astype(F32), g_ref[...], b_ref[...], w_ref, bias_ref[...])
            o_ref[rows, :] = (u * mixed).astype(BF16)

        _for_chunks(step, chunk)

    return pl.pallas_call(
        body, grid=(t // step,),
        in_specs=[pl.BlockSpec((step, 2 * SGU_W), lambda n: (n, 0)), _full((1, SGU_W)), _full((1, SGU_W)),
                  _full((SGU_GROUPS, SGU_CHUNK, SGU_CHUNK)), _full((SGU_CHUNK, SGU_W))],
        out_specs=pl.BlockSpec((step, SGU_W), lambda n: (n, 0)),
        out_shape=jax.ShapeDtypeStruct((t, SGU_W), BF16),
        compiler_params=_cparams(1), name="sgu_fwd")(gu, ln_g, ln_b, w_s, bias_exp)


def _sgu_bwd(dproj, gu, dsgu, ln_g, ln_b, w_s, bias_exp):
    t = gu.shape[0]
    step = min(t, SGU_STEP)
    nsteps = t // step
    e = _head_sum_matrix()

    def body(dp_in, uv_ref, ds_ref, g_ref, b_ref, w_ref, bias_ref, e_ref, out_ref, dw_ref, dbias_ref, dg_ref, db_ref):
        n = pl.program_id(0)

        @pl.when(n == 0)
        def _():
            dw_ref[...] = jnp.zeros(dw_ref.shape, F32)
            dbias_ref[...] = jnp.zeros(dbias_ref.shape, F32)
            dg_ref[...] = jnp.zeros(dg_ref.shape, F32)
            db_ref[...] = jnp.zeros(db_ref.shape, F32)

        _for_chunks(step, functools.partial(chunk, uv_ref, ds_ref, g_ref, b_ref, w_ref, bias_ref, out_ref, dw_ref, dbias_ref,
                                            dg_ref, db_ref))

        @pl.when(n == nsteps - 1)
        def _():
            dbias_ref[...] = _group_sum(dbias_ref[...], e_ref[...])

    def chunk(uv_ref, ds_ref, g_ref, b_ref, w_ref, bias_ref, out_ref, dw_ref, dbias_ref, dg_ref, db_ref, rows):
        uv = uv_ref[rows, :].astype(F32)
        g = g_ref[...]
        cdf, u, xhat, rs, mixed, ws, vlo, vhi, tril, upper = _sgu_core(uv, g, b_ref[...], w_ref, bias_ref[...])
        dsg = ds_ref[rows, :]
        du = dsg * mixed
        dmixed = dsg * u
        dbias_ref[...] += dmixed
        dvn = []
        for pr in range(SGU_W // LANES):
            sl = slice(pr * LANES, (pr + 1) * LANES)
            dm2 = dmixed[:, sl]
            dlo = jnp.where(upper, 0.0, dm2).astype(BF16)
            dhi = jnp.where(upper, dm2, 0.0).astype(BF16)
            w0, w1 = ws[pr]
            dvn.append(lax.dot_general(w0, dlo, (TN, ((), ())), preferred_element_type=F32)
                       + lax.dot_general(w1, dhi, (TN, ((), ())), preferred_element_type=F32))
            dw0 = lax.dot_general(dlo, vlo[pr], (NT, ((), ())), preferred_element_type=F32)
            dw1 = lax.dot_general(dhi, vhi[pr], (NT, ((), ())), preferred_element_type=F32)
            dw_ref[2 * pr] += jnp.where(tril, dw0, 0.0)
            dw_ref[2 * pr + 1] += jnp.where(tril, dw1, 0.0)
        dvn = jnp.concatenate(dvn, axis=1)
        dg_ref[...] += jnp.sum(dvn * xhat, axis=0, keepdims=True)
        db_ref[...] += jnp.sum(dvn, axis=0, keepdims=True)
        dxh = dvn * g
        dv = rs * (dxh - jnp.mean(dxh, axis=1, keepdims=True) - xhat * jnp.mean(dxh * xhat, axis=1, keepdims=True))
        dz = jnp.concatenate([du, dv], axis=1)
        dgelu = cdf + uv * (INV_SQRT_2PI * jnp.exp(-0.5 * uv * uv))
        out_ref[rows, :] = (dz * dgelu).astype(BF16)

    outs = pl.pallas_call(
        body, grid=(nsteps,),
        in_specs=[pl.BlockSpec(memory_space=pl.ANY), pl.BlockSpec((step, 2 * SGU_W), lambda n: (n, 0)),
                  pl.BlockSpec((step, SGU_W), lambda n: (n, 0)), _full((1, SGU_W)), _full((1, SGU_W)),
                  _full((SGU_GROUPS, SGU_CHUNK, SGU_CHUNK)), _full((SGU_CHUNK, SGU_W)), _full((ATTN_W, ATTN_W))],
        out_specs=[pl.BlockSpec((step, 2 * SGU_W), lambda n: (n, 0)), _full((SGU_GROUPS, SGU_CHUNK, SGU_CHUNK)),
                   _full((SGU_CHUNK, SGU_W)), _full((1, SGU_W)), _full((1, SGU_W))],
        out_shape=[jax.ShapeDtypeStruct(dproj.shape, BF16), jax.ShapeDtypeStruct((SGU_GROUPS, SGU_CHUNK, SGU_CHUNK), F32),
                   jax.ShapeDtypeStruct((SGU_CHUNK, SGU_W), F32), jax.ShapeDtypeStruct((1, SGU_W), F32),
                   jax.ShapeDtypeStruct((1, SGU_W), F32)],
        input_output_aliases={0: 0},
        compiler_params=_cparams(1), name="sgu_bwd")(dproj, gu, dsgu, ln_g, ln_b, w_s, bias_exp, e)
    return outs


def _merge_fwd(attn, sgu, gu, x, w_pa, w_ps, w_out, g2):
    t = x.shape[0]
    tm = min(t, 512)

    def body(a_ref, s_ref, ga_ref, gb_ref, x_ref, wpa, wps, wo, g_ref, pa_ref, ps_ref, m_ref, x1_ref, h2_ref):
        pa = jnp.dot(a_ref[...], wpa[...], preferred_element_type=F32)
        ps = jnp.dot(s_ref[...], wps[...], preferred_element_type=F32)
        merged = (_sigmoid(ga_ref[...].astype(F32)) * pa + _sigmoid(gb_ref[...].astype(F32)) * ps).astype(BF16)
        x1 = x_ref[...] + jnp.dot(merged, wo[...], preferred_element_type=F32)
        xhat, _ = _rms_stats(x1)
        pa_ref[...] = pa.astype(BF16)
        ps_ref[...] = ps.astype(BF16)
        m_ref[...] = merged
        x1_ref[...] = x1
        h2_ref[...] = (xhat * g_ref[...]).astype(BF16)

    half = pl.BlockSpec((tm, ATTN_W), lambda i: (i, 0))
    full = pl.BlockSpec((tm, D_MODEL), lambda i: (i, 0))
    return pl.pallas_call(
        body, grid=(t // tm,),
        in_specs=[half, half, pl.BlockSpec((tm, D_MODEL), lambda i: (i, 1)), pl.BlockSpec((tm, D_MODEL), lambda i: (i, 2)),
                  full, _resident((ATTN_W, D_MODEL)), _resident((SGU_W, D_MODEL)), _resident((D_MODEL, D_MODEL)),
                  _full((1, D_MODEL))],
        out_specs=[full] * 5,
        out_shape=[jax.ShapeDtypeStruct((t, D_MODEL), BF16), jax.ShapeDtypeStruct((t, D_MODEL), BF16),
                   jax.ShapeDtypeStruct((t, D_MODEL), BF16), jax.ShapeDtypeStruct((t, D_MODEL), F32),
                   jax.ShapeDtypeStruct((t, D_MODEL), BF16)],
        compiler_params=_cparams(1), name="merge_fwd")(attn, sgu, gu, gu, x, w_pa, w_ps, w_out, g2)


def _merge_bwd(dx1b, gu, pa, ps, w_pa, w_ps, w_out):
    t = dx1b.shape[0]
    tm = min(t, 512)

    def body(d_ref, ga_ref, gb_ref, pa_ref, ps_ref, wpa, wps, wo, out_ref, dpa_ref, dps_ref, da_ref, dsg_ref):
        dm = lax.dot_general(d_ref[...], wo[...], (NT, ((), ())), preferred_element_type=F32)
        sa, sb = _sigmoid(ga_ref[...].astype(F32)), _sigmoid(gb_ref[...].astype(F32))
        dpa = (dm * sa).astype(BF16)
        dps = (dm * sb).astype(BF16)
        out_ref[:, 0:D_MODEL] = jnp.zeros((tm, D_MODEL), BF16)
        out_ref[:, D_MODEL:2 * D_MODEL] = (dm * pa_ref[...].astype(F32) * sa * (1.0 - sa)).astype(BF16)
        out_ref[:, 2 * D_MODEL:GU_COLS] = (dm * ps_ref[...].astype(F32) * sb * (1.0 - sb)).astype(BF16)
        dpa_ref[...] = dpa
        dps_ref[...] = dps
        da_ref[...] = lax.dot_general(dpa, wpa[...], (NT, ((), ())), preferred_element_type=F32)
        dsg_ref[...] = lax.dot_general(dps, wps[...], (NT, ((), ())), preferred_element_type=F32)

    half = pl.BlockSpec((tm, ATTN_W), lambda i: (i, 0))
    full = pl.BlockSpec((tm, D_MODEL), lambda i: (i, 0))
    return pl.pallas_call(
        body, grid=(t // tm,),
        in_specs=[full, pl.BlockSpec((tm, D_MODEL), lambda i: (i, 1)),
                  pl.BlockSpec((tm, D_MODEL), lambda i: (i, 2)), full, full,
                  _resident((ATTN_W, D_MODEL)), _resident((SGU_W, D_MODEL)), _resident((D_MODEL, D_MODEL))],
        out_specs=[pl.BlockSpec((tm, GU_COLS), lambda i: (i, 0)), full, full, half, half],
        out_shape=[jax.ShapeDtypeStruct((t, GU_COLS), BF16), jax.ShapeDtypeStruct((t, D_MODEL), BF16),
                   jax.ShapeDtypeStruct((t, D_MODEL), BF16), jax.ShapeDtypeStruct((t, ATTN_W), F32),
                   jax.ShapeDtypeStruct((t, SGU_W), F32)],
        compiler_params=_cparams(1), name="merge_bwd")(dx1b, gu, gu, pa, ps, w_pa, w_ps, w_out)


def _token_call(name, body, t, tm, ins, outs, reds=(), scratch=()):
    return pl.pallas_call(
        body, grid=(t // tm,), in_specs=[s for _, s in ins],
        out_specs=[o[2] for o in outs] + [_full(r) for r in reds],
        out_shape=[jax.ShapeDtypeStruct(o[0], o[1]) for o in outs] + [jax.ShapeDtypeStruct(r, F32) for r in reds],
        scratch_shapes=list(scratch), compiler_params=_cparams(1), name=name)(*[a for a, _ in ins])


def _rows_spec(tm, width):
    return pl.BlockSpec((tm, width), lambda i: (i, 0))


def _chips_spec(tm):
    return pl.BlockSpec((N_CHIPS, tm, FF_SHARD), lambda i: (0, i, 0))


def _zero_at_start(*refs):
    @pl.when(pl.program_id(0) == 0)
    def _():
        for r in refs:
            r[...] = jnp.zeros(r.shape, r.dtype)


def _ffn_fwd(h2, w_g, w_u):
    t = h2.shape[0]
    tm = min(t, 512)

    def body(h_ref, wg_ref, wu_ref, a_ref, b_ref, ff_ref):
        h = h_ref[...]
        for s in range(N_CHIPS):
            a = jnp.dot(h, wg_ref[s], preferred_element_type=F32)
            b = jnp.dot(h, wu_ref[s], preferred_element_type=F32)
            a_ref[s] = a.astype(BF16)
            b_ref[s] = b.astype(BF16)
            ff_ref[s] = (a * _sigmoid(a) * b).astype(BF16)

    shp = (N_CHIPS, t, FF_SHARD)
    w_spec = _resident((N_CHIPS, D_MODEL, FF_SHARD))
    return _token_call("ffn_fwd", body, t, tm, [(h2, _rows_spec(tm, D_MODEL)), (w_g, w_spec), (w_u, w_spec)],
                       [(shp, BF16, _chips_spec(tm))] * 3)


def _ffn_down_loss(ff, w_d, x1, tgt, gf):
    t = x1.shape[0]
    tm = min(t, 512)

    def body(ff_ref, wd_ref, x1_ref, tgt_ref, g_ref, dx2_ref, dx2b_ref, loss_ref, dgf_ref):
        _zero_at_start(loss_ref, dgf_ref)
        acc = jnp.dot(ff_ref[0], wd_ref[0], preferred_element_type=F32)
        for s in range(1, N_CHIPS):
            acc = acc + jnp.dot(ff_ref[s], wd_ref[s], preferred_element_type=F32)
        x2 = x1_ref[...] + acc
        g = g_ref[...]
        xhat, rr = _rms_stats(x2)
        diff = xhat * g - tgt_ref[...]
        rows = jnp.sum(diff * diff, axis=1, keepdims=True)
        loss_ref[...] += jnp.broadcast_to(jnp.sum(rows, axis=0, keepdims=True) * (0.5 / D_MODEL), (1, LANES))
        dy = diff * (1.0 / D_MODEL)
        dgf_ref[...] += jnp.sum(dy * xhat, axis=0, keepdims=True)
        dx2 = _rms_bwd(dy, xhat, rr, g)
        dx2_ref[...] = dx2
        dx2b_ref[...] = dx2.astype(BF16)

    row = _rows_spec(tm, D_MODEL)
    return _token_call("ffn_down_loss", body, t, tm,
                       [(ff, _chips_spec(tm)), (w_d, _resident((N_CHIPS, FF_SHARD, D_MODEL))), (x1, row), (tgt, row),
                        (gf, _full((1, D_MODEL)))],
                       [((t, D_MODEL), F32, row), ((t, D_MODEL), BF16, row)], reds=[(1, LANES), (1, D_MODEL)])


def _ffn_bwd_act(dx2b, w_d, a, b):
    t = dx2b.shape[0]
    tm = min(t, 512)

    def body(d_ref, wd_ref, a_ref, b_ref, da_ref, db_ref):
        d = d_ref[...]
        for s in range(N_CHIPS):
            dff = lax.dot_general(d, wd_ref[s], (NT, ((), ())), preferred_element_type=F32)
            av, bv = a_ref[s].astype(F32), b_ref[s].astype(F32)
            sg = _sigmoid(av)
            da_ref[s] = (dff * bv * (sg * (1.0 + av * (1.0 - sg)))).astype(BF16)
            db_ref[s] = (dff * (av * sg)).astype(BF16)

    shp = (N_CHIPS, t, FF_SHARD)
    return _token_call("ffn_bwd_act", body, t, tm,
                       [(dx2b, _rows_spec(tm, D_MODEL)), (w_d, _resident((N_CHIPS, FF_SHARD, D_MODEL))),
                        (a, _chips_spec(tm)), (b, _chips_spec(tm))],
                       [(shp, BF16, _chips_spec(tm))] * 2)


def _ffn_bwd_in(da, db, w_g, w_u, x1, dx2, g2):
    t = x1.shape[0]
    tm = min(t, 512)

    def body(da_ref, db_ref, wg_ref, wu_ref, x1_ref, dx2_ref, g_ref, dx1_ref, dx1b_ref, dg_ref):
        _zero_at_start(dg_ref)
        acc = None
        for s in range(N_CHIPS):
            part = (lax.dot_general(da_ref[s], wg_ref[s], (NT, ((), ())), preferred_element_type=F32)
                    + lax.dot_general(db_ref[s], wu_ref[s], (NT, ((), ())), preferred_element_type=F32))
            acc = part if acc is None else acc + part
        xhat, rr = _rms_stats(x1_ref[...])
        dg_ref[...] += jnp.sum(acc * xhat, axis=0, keepdims=True)
        dx1 = dx2_ref[...] + _rms_bwd(acc, xhat, rr, g_ref[...])
        dx1_ref[...] = dx1
        dx1b_ref[...] = dx1.astype(BF16)

    row = _rows_spec(tm, D_MODEL)
    w_spec = _resident((N_CHIPS, D_MODEL, FF_SHARD))
    return _token_call("ffn_bwd_in", body, t, tm,
                       [(da, _chips_spec(tm)), (db, _chips_spec(tm)), (w_g, w_spec), (w_u, w_spec), (x1, row), (dx2, row),
                        (g2, _full((1, D_MODEL)))],
                       [((t, D_MODEL), F32, row), ((t, D_MODEL), BF16, row)], reds=[(1, D_MODEL)])


def _group_dh(d, w_refs):
    dh = None
    for part, w_ref in enumerate(w_refs):
        term = lax.dot_general(d[:, part * ATTN_W:(part + 1) * ATTN_W], w_ref[...], (NT, ((), ())),
                               preferred_element_type=F32)
        dh = term if dh is None else dh + term
    return dh


def _in_proj_bwd(dgu, dqkvs, w_in, x, dx1, g1):
    t = x.shape[0]
    tile = min(t, TILE)
    ngroups = len(DILATIONS)

    def body(*refs):
        dgu_ref, dq_refs = refs[0], refs[1:1 + ngroups]
        w0_ref, w1_ref = refs[1 + ngroups:3 + ngroups]
        wg_refs = [refs[3 + ngroups + 3 * g:6 + ngroups + 3 * g] for g in range(ngroups)]
        x_ref, dx1_ref, g_ref, dx_ref, dg_ref, slab = refs[3 + 4 * ngroups:]
        _zero_at_start(dg_ref)
        dh = lax.dot_general(dgu_ref[:, 0:GU_HALF], w0_ref[...], (NT, ((), ())), preferred_element_type=F32)
        dh = dh + lax.dot_general(dgu_ref[:, GU_HALF:], w1_ref[...], (NT, ((), ())), preferred_element_type=F32)
        dh = dh + _group_dh(dq_refs[0][0], wg_refs[0])
        for g in range(1, ngroups):
            dil = DILATIONS[g]
            part = _group_dh(dq_refs[g][...].reshape(tile, GROUP_COLS), wg_refs[g])
            for r in range(dil):
                _put_class_rows(slab, r, dil, part[r * (tile // dil):(r + 1) * (tile // dil)])
            dh = dh + _from_slabs(slab)
        xhat, rr = _rms_stats(x_ref[...])
        dg_ref[...] += jnp.sum(dh * xhat, axis=0, keepdims=True)
        dx_ref[...] = dx1_ref[...] + _rms_bwd(dh, xhat, rr, g_ref[...])

    row = _rows_spec(tile, D_MODEL)
    group_ins = [(dqkvs[g].reshape(d, t // d, GROUP_COLS), _group_spec(d, tile, GROUP_COLS)) for g, d in enumerate(DILATIONS)]
    w_specs = _gu_w_specs() + [s for g in range(ngroups) for s in _group_w_specs(g)]
    return _token_call(
        "in_proj_bwd", body, t, tile,
        [(dgu, _rows_spec(tile, GU_COLS))] + group_ins + [(w_in, s) for s in w_specs]
        + [(x, row), (dx1, row), (g1, _full((1, D_MODEL)))],
        [((t, D_MODEL), F32, row)], reds=[(1, D_MODEL)], scratch=[_slabs(tile, D_MODEL)])


def _epi_bf16(acc, e, o, r, ids):
    o[0][...] = acc.astype(BF16)


WGRAD_TK = 2048


def _wgrad_2d(name, a, b, tm, tn):
    t, k1 = a.shape
    n = b.shape[1]
    tk = min(t, WGRAD_TK)
    return _mm(name, (k1 // tm, n // tn, t // tk),
               [(a, pl.BlockSpec((tk, tm), lambda i, j, k: (k, i)), b, pl.BlockSpec((tk, tn), lambda i, j, k: (k, j)))],
               TN, (tm, tn), _epi_bf16, outs=[((k1, n), BF16, pl.BlockSpec((tm, tn), lambda i, j, k: (i, j)))])[0]


def _wgrad_in(hs, dgu, dqkvs):
    t = dgu.shape[0]
    tk = min(t, WGRAD_TK)
    gu_block = QKV_BLOCKS * ATTN_W // GU_HALF
    parts = [(hs[0], dgu, GU_HALF, lambda j: j + gu_block)]
    parts += [(hs[g].reshape(t, D_MODEL), dqkvs[g], ATTN_W, lambda j, g=g: _w_in_block(j, g)) for g in range(3)]
    dst = None
    for n, (a, b, tn, block_of) in enumerate(parts):
        dst = _mm(f"wgrad_in_{n}", (1, b.shape[1] // tn, t // tk),
                  [(a, pl.BlockSpec((tk, D_MODEL), lambda i, j, k: (k, 0)), b,
                    pl.BlockSpec((tk, tn), lambda i, j, k: (k, j)))],
                  TN, (D_MODEL, tn), _epi_bf16,
                  extras=[] if dst is None else [(dst, pl.BlockSpec(memory_space=pl.ANY))],
                  outs=[((D_MODEL, IN_COLS), BF16,
                         pl.BlockSpec((D_MODEL, tn), lambda i, j, k, block_of=block_of: (0, block_of(j))))],
                  aliases=None if dst is None else {2: 0})[0]
    return dst


def _wgrad_ff_in(name, h2, da):
    t = h2.shape[0]
    tk = min(t, WGRAD_TK)
    return _mm(name, (N_CHIPS, 1, t // tk),
               [(h2, pl.BlockSpec((tk, D_MODEL), lambda i, j, k: (k, 0)),
                 da, pl.BlockSpec((None, tk, FF_SHARD), lambda i, j, k: (i, k, 0)))],
               TN, (D_MODEL, FF_SHARD), _epi_bf16,
               outs=[((N_CHIPS, D_MODEL, FF_SHARD), BF16, pl.BlockSpec((None, D_MODEL, FF_SHARD), lambda i, j, k: (i, 0, 0)))])[0]


def _wgrad_ff_down(ff, dx2b):
    t = dx2b.shape[0]
    tk = min(t, WGRAD_TK)
    return _mm("wgrad_ffn_down", (N_CHIPS, 1, t // tk),
               [(ff, pl.BlockSpec((None, tk, FF_SHARD), lambda i, j, k: (i, k, 0)),
                 dx2b, pl.BlockSpec((tk, D_MODEL), lambda i, j, k: (k, 0)))],
               TN, (FF_SHARD, D_MODEL), _epi_bf16,
               outs=[((N_CHIPS, FF_SHARD, D_MODEL), BF16, pl.BlockSpec((None, FF_SHARD, D_MODEL), lambda i, j, k: (i, 0, 0)))])[0]


def _local_step(x, pos_col, tgt, g1, ln_g, ln_b, w_s, b_s, g2, gf, first_weight, late_weights, on_grads=None):
    tables = _rope_tables(pos_col)
    bias_exp = jnp.repeat(jnp.transpose(b_s), SGU_W // SGU_GROUPS, axis=1)

    hs = _norm_fwd(x, g1)
    w_p = first_weight(hs[0])
    gu, qkvs = _in_proj(hs, w_p, tables)
    os_, ls_ = [], []
    for g, dil in enumerate(DILATIONS):
        o, lse = _attn_fwd(qkvs[g], g, dil)
        os_.append(o)
        ls_.append(lse)
    attn = _combine_fwd(os_, ls_)
    sgu = _sgu_fwd(gu, ln_g, ln_b, w_s, bias_exp)
    w_pa, w_ps, w_out, w_g, w_u, w_d = late_weights(attn)
    pa, ps, merged, x1, h2 = _merge_fwd(attn, sgu, gu, x, w_pa, w_ps, w_out, g2)
    a, b, ff = _ffn_fwd(h2, w_g, w_u)
    dx2, dx2b, loss, dgf = _ffn_down_loss(ff, w_d, x1, tgt, gf)

    da, db = _ffn_bwd_act(dx2b, w_d, a, b)
    dw_d = _wgrad_ff_down(ff, dx2b)
    dx1, dx1b, dg2 = _ffn_bwd_in(da, db, w_g, w_u, x1, dx2, g2)
    dw_g = _wgrad_ff_in("wgrad_ffn_gate", h2, da)
    dw_u = _wgrad_ff_in("wgrad_ffn_up", h2, db)

    dgu, dpa, dps, dattn, dsgu = _merge_bwd(dx1b, gu, pa, ps, w_pa, w_ps, w_out)
    dw_out = _wgrad_2d("wgrad_out", merged, dx1b, D_MODEL, D_MODEL)
    dw_pa = _wgrad_2d("wgrad_proj_attn", attn, dpa, ATTN_W, D_MODEL)
    dw_ps = _wgrad_2d("wgrad_proj_sgu", sgu, dps, SGU_W, D_MODEL)
    if on_grads is not None:
        ln_g = ln_g + on_grads(1, dict(w_proj_attn=dw_pa, w_proj_sgu=dw_ps, w_out=dw_out, w_ffn_gate=dw_g, w_ffn_up=dw_u,
                                       w_ffn_down=dw_d))[:, :SGU_W]
    dgu, dw_s, dbias, dln_g, dln_b = _sgu_bwd(dgu, gu, dsgu, ln_g, ln_b, w_s, bias_exp)
    dos, ccs = _combine_bwd(dattn, os_, ls_)
    dqkvs = [_attn_bwd(qkvs[g], dos[g], ccs[g], ls_[g], *tables[g], g, dil) for g, dil in enumerate(DILATIONS)]
    dw_p = _wgrad_in(hs, dgu, dqkvs)
    if on_grads is not None:
        g1 = g1 + on_grads(0, dict(w_in=dw_p))
    dx, dg1 = _in_proj_bwd(dgu, dqkvs, w_p, x, dx1, g1)

    db_s = jnp.transpose(dbias[:, ::SGU_W // SGU_GROUPS])
    small = dict(loss=loss, norm1_g=dg1, sgu_ln_g=dln_g, sgu_ln_b=dln_b, w_spatial=dw_s, b_spatial=db_s,
                 norm2_g=dg2, final_g=dgf)
    big = dict(w_in=dw_p, w_proj_attn=dw_pa, w_proj_sgu=dw_ps, w_out=dw_out, w_ffn_gate=dw_g, w_ffn_up=dw_u,
               w_ffn_down=dw_d)
    return dx, big, small


def _ew(name, fn, ins, out_dtypes):
    shp = ins[0].shape
    rows, cols = shp
    tr = next((cand for cand in (256, 352, 128) if rows % cand == 0 and rows > cand), rows)

    def body(*refs):
        res = fn(*[r[...] for r in refs[:len(ins)]])
        for o_ref, v in zip(refs[len(ins):], res):
            o_ref[...] = v.astype(o_ref.dtype)

    spec = pl.BlockSpec((tr, cols), lambda i: (i, 0))
    return pl.pallas_call(
        body, grid=(rows // tr,), in_specs=[spec] * len(ins), out_specs=[spec] * len(out_dtypes),
        out_shape=[jax.ShapeDtypeStruct(shp, d) for d in out_dtypes],
        compiler_params=_cparams(1), name=name)(*ins)


def _adamw_math(g, w, m, v):
    m = ADAM_B1 * m + (1.0 - ADAM_B1) * g
    v = ADAM_B2 * v + (1.0 - ADAM_B2) * (g * g)
    m_hat = m / (1.0 - ADAM_B1 ** ADAM_STEP)
    v_hat = v / (1.0 - ADAM_B2 ** ADAM_STEP)
    delta = -ADAM_LR * (m_hat / (jnp.sqrt(v_hat) + ADAM_EPS) + ADAM_WD * w)
    return delta, m, v


def _adamw(name, g, w, m, v):
    return _ew(name, lambda g_, w_, m_, v_: (g_,) + _adamw_math(g_, w_, m_, v_), [g, w, m, v], [F32] * 4)


VMEM_SPEC = pl.BlockSpec(memory_space=pltpu.VMEM)


def _for_row_chunks(rows, fn):
    ck = next(c for c in (64, 32, 16) if rows % c == 0)

    def step(i, carry):
        fn(pl.multiple_of(i * ck, ck), ck)
        return carry

    lax.fori_loop(0, rows // ck, step, 0)


def _place():
    x, y, c = lax.axis_index("x"), lax.axis_index("y"), lax.axis_index("c")
    chips = [(1 - x, y), (x, 1 - y), (1 - x, 1 - y)]
    return x, y, c, 2 * x + y, chips


def _rows(ref, start, size):
    if len(ref.shape) == 2:
        return ref.at[pl.ds(start, size), :]
    return ref.at[:, pl.ds(start, size), :]


def _comm_call(name, body, ins, out_shapes, scratch, n_remote):
    return pl.pallas_call(
        body, in_specs=[VMEM_SPEC] * len(ins), out_specs=[VMEM_SPEC] * len(out_shapes),
        out_shape=out_shapes,
        scratch_shapes=list(scratch) + [pltpu.SemaphoreType.DMA((n_remote,)), pltpu.SemaphoreType.DMA((n_remote,))],
        compiler_params=pltpu.CompilerParams(vmem_limit_bytes=VMEM_LIMIT), name=name)(*ins)


def _gather_finish(name, shard, landed):
    k_rows, n = shard.shape
    kh = k_rows // 2

    def body(shard_ref, land_ref, out_ref, send, recv):
        x, y, c, me, chips = _place()
        passed = []
        for j, chip in enumerate(chips):
            theirs = 2 * chip[0] + chip[1]
            cp = pltpu.make_async_remote_copy(
                src_ref=land_ref.at[j], dst_ref=_rows(out_ref.at[theirs], c * kh, kh), send_sem=send.at[j],
                recv_sem=recv.at[j], device_id=(x, y, 1 - c), device_id_type=MESH)
            cp.start()
            passed.append(cp)
        mine = out_ref.at[me]

        def put_own(r0, ck):
            mine[pl.ds(r0, ck), :] = shard_ref[pl.ds(r0, ck), :]

        _for_row_chunks(k_rows, put_own)
        for j, chip in enumerate(chips):
            slot = out_ref.at[2 * chip[0] + chip[1]]

            def put_half(r0, ck, j=j, slot=slot):
                slot[pl.ds(pl.multiple_of(c * kh + r0, ck), ck), :] = land_ref[j, pl.ds(r0, ck), :]

            _for_row_chunks(kh, put_half)
        for j, chip in enumerate(chips):
            other = _rows(out_ref.at[2 * chip[0] + chip[1]], (1 - c) * kh, kh)
            pltpu.make_async_remote_copy(src_ref=other, dst_ref=other, send_sem=send.at[j], recv_sem=recv.at[j],
                                         device_id=(x, y, 1 - c), device_id_type=MESH).wait_recv()
        for cp in passed:
            cp.wait_send()

    return _comm_call(name, body, [shard, landed], [jax.ShapeDtypeStruct((N_CHIPS, k_rows, n), shard.dtype)], [], 3)[0]


HBM_SPEC = pl.BlockSpec(memory_space=pltpu.HBM)
SEM_SPEC = pl.BlockSpec(memory_space=pltpu.SEMAPHORE)
DATAFLOW = pltpu.SideEffectType.DATAFLOW_SIDE_EFFECTING
TOKEN_SHAPE = (1, D_MODEL)
N_PEERS = 7


def _peers():
    x, y, c = lax.axis_index("x"), lax.axis_index("y"), lax.axis_index("c")
    flip = lambda v, f: 1 - v if f else v
    return [(flip(x, k & 4), flip(y, k & 2), flip(c, k & 1)) for k in range(1, N_PEERS + 1)]


def _piece_shape(shape):
    return (shape[-2] // 2, shape[2] if len(shape) == 3 else shape[1] // N_CHIPS)


def _device_piece(ref, chip, core):
    kh, n4 = _piece_shape(ref.shape)
    if len(ref.shape) == 3:
        return ref.at[chip, pl.ds(core * kh, kh), :]
    return ref.at[pl.ds(core * kh, kh), pl.ds(chip * n4, n4)]


def _exchange_copies(partials, lands, send, recv):
    return [pltpu.make_async_remote_copy(
        src_ref=_device_piece(partials[t], 2 * px + py, pc), dst_ref=lands[t].at[k], send_sem=send.at[t * N_PEERS + k],
        recv_sem=recv.at[t * N_PEERS + k], device_id=(px, py, pc), device_id_type=MESH)
        for t in range(len(partials)) for k, (px, py, pc) in enumerate(_peers())]


def _gather_copies(shards, lands, send, recv):
    x, y, c, me, chips = _place()
    return [pltpu.make_async_remote_copy(
        src_ref=shards[t], dst_ref=lands[t].at[me], send_sem=send.at[t * 3 + j], recv_sem=recv.at[t * 3 + j],
        device_id=(*chip, c), device_id_type=MESH)
        for t in range(len(shards)) for j, chip in enumerate(chips)]


def _gather_half_copies(shards, lands, send, recv):
    x, y, c, me, chips = _place()
    return [pltpu.make_async_remote_copy(
        src_ref=_rows(shards[t], c * (shards[t].shape[0] // 2), shards[t].shape[0] // 2), dst_ref=lands[t].at[j],
        send_sem=send.at[t * 3 + j], recv_sem=recv.at[t * 3 + j], device_id=(*chip, c), device_id_type=MESH)
        for t in range(len(shards)) for j, chip in enumerate(chips)]


def _split_start(name, copies, per_tensor, srcs, land_shapes):
    nt = len(srcs)
    lands = [lax.empty(s, BF16) for s in land_shapes]
    nsem = nt * per_tensor

    def body(*refs):
        send, recv = refs[2 * nt], refs[2 * nt + 1]
        for cp in copies(refs[:nt], refs[nt:2 * nt], send, recv):
            cp.start()
        refs[-1][...] = jnp.zeros(TOKEN_SHAPE, F32)

    hbm = lambda a: pltpu.with_memory_space_constraint(a, pltpu.HBM)
    outs = pl.pallas_call(
        body, name=name,
        out_shape=[pltpu.SemaphoreType.DMA((nsem,)), pltpu.SemaphoreType.DMA((nsem,))]
        + [pltpu.HBM(s.shape, s.dtype) for s in srcs] + [pltpu.HBM(l.shape, l.dtype) for l in lands]
        + [jax.ShapeDtypeStruct(TOKEN_SHAPE, F32)],
        in_specs=[HBM_SPEC] * (2 * nt), out_specs=[SEM_SPEC, SEM_SPEC] + [HBM_SPEC] * (2 * nt) + [VMEM_SPEC],
        input_output_aliases={i: 2 + i for i in range(2 * nt)},
        compiler_params=pltpu.CompilerParams(has_side_effects=DATAFLOW))(*[hbm(a) for a in list(srcs) + lands])
    return outs[0], outs[1], outs[2:2 + nt], outs[2 + nt:2 + 2 * nt], outs[-1]


def _split_wait(name, copies, send, recv, srcs, lands, after):
    nt = len(srcs)

    def body(*refs):
        for cp in copies(refs[:nt], refs[nt:2 * nt], refs[2 * nt], refs[2 * nt + 1]):
            cp.wait_send()
            cp.wait_recv()

    outs = pl.pallas_call(
        body, name=name,
        out_shape=[pltpu.HBM(s.shape, s.dtype) for s in srcs] + [pltpu.HBM(l.shape, l.dtype) for l in lands],
        in_specs=[HBM_SPEC] * (2 * nt) + [SEM_SPEC, SEM_SPEC, pl.BlockSpec(memory_space=pl.ANY)],
        out_specs=[HBM_SPEC] * (2 * nt), input_output_aliases={i: i for i in range(2 * nt)},
        compiler_params=pltpu.CompilerParams(has_side_effects=DATAFLOW))(*srcs, *lands, send, recv, after)
    return outs[:nt], outs[nt:]


def _device_sum(name, partials, lands):
    nt = len(partials)

    def body(*refs):
        ins, slots, outs, owns = refs[:nt], refs[nt:2 * nt], refs[2 * nt:3 * nt], refs[3 * nt:4 * nt]
        send, recv, loc = refs[4 * nt:]
        x, y, c, me, chips = _place()
        sibling = (x, y, 1 - c)
        loads = [pltpu.make_async_copy(_device_piece(ins[t], me, c), owns[t], loc.at[t]) for t in range(nt)]
        for cp in loads:
            cp.start()
        handed = []
        for t in range(nt):
            kh = owns[t].shape[0]
            loads[t].wait()

            def add(r0, ck, own=owns[t], slot=slots[t], dst=outs[t], kh=kh):
                rows = pl.ds(r0, ck)
                acc = own[rows, :].astype(F32)
                for k in range(N_PEERS):
                    acc = acc + slot[k, rows, :].astype(F32)
                dst[pl.ds(pl.multiple_of(c * kh + r0, ck), ck), :] = acc

            _for_row_chunks(kh, add)
            rc = pltpu.make_async_remote_copy(
                src_ref=_rows(outs[t], c * kh, kh), dst_ref=_rows(outs[t], c * kh, kh), send_sem=send.at[t],
                recv_sem=recv.at[t], device_id=sibling, device_id_type=MESH)
            rc.start()
            handed.append(rc)
        for t in range(nt):
            kh = owns[t].shape[0]
            other = _rows(outs[t], (1 - c) * kh, kh)
            pltpu.make_async_remote_copy(
                src_ref=other, dst_ref=other, send_sem=send.at[t], recv_sem=recv.at[t],
                device_id=sibling, device_id_type=MESH).wait_recv()
        for rc in handed:
            rc.wait_send()

    pieces = [_piece_shape(p.shape) for p in partials]
    return pl.pallas_call(
        body, in_specs=[pl.BlockSpec(memory_space=pl.ANY)] * nt + [VMEM_SPEC] * nt, out_specs=[VMEM_SPEC] * nt,
        out_shape=[jax.ShapeDtypeStruct((2 * kh, n4), F32) for kh, n4 in pieces],
        scratch_shapes=[pltpu.VMEM(p, BF16) for p in pieces]
        + [pltpu.SemaphoreType.DMA((nt,)), pltpu.SemaphoreType.DMA((nt,)), pltpu.SemaphoreType.DMA((nt,))],
        compiler_params=pltpu.CompilerParams(vmem_limit_bytes=VMEM_LIMIT), name=name)(*partials, *lands)


VEC_SHAPE = (8, D_MODEL + LANES)
VEC_SLOTS = dict(norm1_g=(slice(0, 1), slice(0, D_MODEL)), norm2_g=(slice(1, 2), slice(0, D_MODEL)),
                 final_g=(slice(2, 3), slice(0, D_MODEL)), sgu_ln_g=(slice(3, 4), slice(0, SGU_W)),
                 sgu_ln_b=(slice(3, 4), slice(SGU_W, 2 * SGU_W)), b_spatial=(slice(0, 8), slice(D_MODEL, D_MODEL + LANES)),
                 loss=(slice(4, 5), slice(0, LANES)))
VEC_PARAMS = ("norm1_g", "norm2_g", "final_g", "sgu_ln_g", "sgu_ln_b", "b_spatial")
SMALL_PARAMS = VEC_PARAMS + ("w_spatial",)
W_SPATIAL_2D = (SGU_GROUPS * SGU_CHUNK, SGU_CHUNK)


def _small_step(partials, w, m, v):
    def shape2d(name):
        if name == "w_spatial":
            return W_SPATIAL_2D
        rows, cols = VEC_SLOTS[name]
        return (rows.stop - rows.start, cols.stop - cols.start)

    g_names = VEC_PARAMS + ("loss", "w_spatial")
    ng, npar = len(g_names), len(SMALL_PARAMS)

    def pack(dst, parts):
        dst[...] = jnp.zeros(VEC_SHAPE, F32)
        for n, ref in parts.items():
            if n in VEC_SLOTS:
                dst[VEC_SLOTS[n]] = ref[...]

    def reduce_body(*refs):
        g_in = dict(zip(g_names, refs[:ng]))
        vec_out, ws_out, vec, vec_pair, vec_slot, ws_pair, ws_slot, send, recv = refs[ng:]
        x, y, c, me, chips = _place()
        sibling = (x, y, 1 - c)
        pack(vec, g_in)
        copies = []

        def allreduce(k0, src, pair, slot):
            first = pltpu.make_async_remote_copy(src_ref=src, dst_ref=pair, send_sem=send.at[k0], recv_sem=recv.at[k0],
                                                 device_id=sibling, device_id_type=MESH)
            first.start()
            first.wait_recv()
            slot[me] = src[...] + pair[...]
            arrivals = []
            for j, chip in enumerate(chips):
                theirs = 2 * chip[0] + chip[1]
                rc = pltpu.make_async_remote_copy(src_ref=slot.at[me], dst_ref=slot.at[me], send_sem=send.at[k0 + 1 + j],
                                                  recv_sem=recv.at[k0 + 1 + j], device_id=(*chip, c), device_id_type=MESH)
                rc.start()
                arrivals.append(pltpu.make_async_remote_copy(
                    src_ref=slot.at[theirs], dst_ref=slot.at[theirs], send_sem=send.at[k0 + 1 + j],
                    recv_sem=recv.at[k0 + 1 + j], device_id=(*chip, c), device_id_type=MESH))
                copies.append(rc)
            copies.append(first)
            return arrivals

        arrivals = allreduce(0, vec, vec_pair, vec_slot) + allreduce(4, g_in["w_spatial"], ws_pair, ws_slot)
        for a in arrivals:
            a.wait_recv()
        vec_out[...] = ((vec_slot[0] + vec_slot[1]) + vec_slot[2]) + vec_slot[3]

        def spatial(r0, ck):
            rows = pl.ds(r0, ck)
            ws_out[rows, :] = ((ws_slot[0, rows, :] + ws_slot[1, rows, :]) + ws_slot[2, rows, :]) + ws_slot[3, rows, :]

        _for_row_chunks(W_SPATIAL_2D[0], spatial)
        for rc in copies:
            rc.wait_send()

    g_vec, g_ws = pl.pallas_call(
        reduce_body, in_specs=[VMEM_SPEC] * ng, out_specs=[VMEM_SPEC] * 2,
        out_shape=[jax.ShapeDtypeStruct(VEC_SHAPE, F32), jax.ShapeDtypeStruct(W_SPATIAL_2D, F32)],
        scratch_shapes=[pltpu.VMEM(VEC_SHAPE, F32), pltpu.VMEM(VEC_SHAPE, F32), pltpu.VMEM((N_CHIPS,) + VEC_SHAPE, F32),
                        pltpu.VMEM(W_SPATIAL_2D, F32), pltpu.VMEM((N_CHIPS,) + W_SPATIAL_2D, F32),
                        pltpu.SemaphoreType.DMA((8,)), pltpu.SemaphoreType.DMA((8,))],
        name="small_params_allreduce")(*[partials[n].reshape(shape2d(n)) for n in g_names])

    def update_body(*refs):
        gv_ref, gw_ref = refs[:2]
        w_in, m_in, v_in = (dict(zip(SMALL_PARAMS, refs[2 + k * npar:2 + (k + 1) * npar])) for k in range(3))
        o0 = 2 + 3 * npar
        g_out = dict(zip(g_names, refs[o0:o0 + ng]))
        d_out, m_out, v_out = (dict(zip(SMALL_PARAMS, refs[o0 + ng + k * npar:o0 + ng + (k + 1) * npar])) for k in range(3))
        vw, vm, vv = refs[o0 + ng + 3 * npar:]
        pack(vw, w_in)
        pack(vm, m_in)
        pack(vv, v_in)
        d_vec, m_vec, v_vec = _adamw_math(gv_ref[...], vw[...], vm[...], vv[...])
        vw[...] = d_vec
        vm[...] = m_vec
        vv[...] = v_vec
        for n in VEC_PARAMS + ("loss",):
            g_out[n][...] = gv_ref[VEC_SLOTS[n]]
        for n in VEC_PARAMS:
            d_out[n][...] = vw[VEC_SLOTS[n]]
            m_out[n][...] = vm[VEC_SLOTS[n]]
            v_out[n][...] = vv[VEC_SLOTS[n]]

        def spatial(r0, ck):
            rows = pl.ds(r0, ck)
            g = gw_ref[rows, :]
            d_, m_, v_ = _adamw_math(g, w_in["w_spatial"][rows, :], m_in["w_spatial"][rows, :], v_in["w_spatial"][rows, :])
            g_out["w_spatial"][rows, :] = g
            d_out["w_spatial"][rows, :] = d_
            m_out["w_spatial"][rows, :] = m_
            v_out["w_spatial"][rows, :] = v_

        _for_row_chunks(W_SPATIAL_2D[0], spatial)

    ins = [g_vec, g_ws]
    for src in (w, m, v):
        ins += [src[n].reshape(shape2d(n)) for n in SMALL_PARAMS]
    out_shapes = [jax.ShapeDtypeStruct(shape2d(n), F32) for n in g_names + SMALL_PARAMS * 3]
    outs = pl.pallas_call(
        update_body, in_specs=[VMEM_SPEC] * len(ins), out_specs=[VMEM_SPEC] * len(out_shapes), out_shape=out_shapes,
        scratch_shapes=[pltpu.VMEM(VEC_SHAPE, F32)] * 3, name="small_params_update")(*ins)
    grads = dict(zip(g_names, outs[:ng]))
    rest = [dict(zip(SMALL_PARAMS, outs[ng + k * npar:ng + (k + 1) * npar])) for k in range(3)]
    return grads, rest[0], rest[1], rest[2]


BIG = ("w_in", "w_proj_attn", "w_proj_sgu", "w_out", "w_ffn_gate", "w_ffn_up", "w_ffn_down")
COMM_GROUPS = (("w_in",), ("w_proj_attn", "w_proj_sgu", "w_out", "w_ffn_gate", "w_ffn_up", "w_ffn_down"))
WEIGHTS = ("norm1_g", "w_in", "sgu_ln_g", "sgu_ln_b", "w_spatial", "b_spatial", "w_proj_attn", "w_proj_sgu", "w_out",
           "norm2_g", "w_ffn_gate", "w_ffn_up", "w_ffn_down", "final_g")


def _cols_from_chips(g):
    return jnp.transpose(g, (1, 0, 2)).reshape(g.shape[1], N_CHIPS * g.shape[2])


def kernel(x, positions, norm1_g, w_in, sgu_ln_g, sgu_ln_b, w_spatial, b_spatial, w_proj_attn, w_proj_sgu, w_out, norm2_g, w_ffn_gate, w_ffn_up, w_ffn_down, final_g, loss_target, m_norm1_g, m_w_in, m_sgu_ln_g, m_sgu_ln_b, m_w_spatial, m_b_spatial, m_w_proj_attn, m_w_proj_sgu, m_w_out, m_norm2_g, m_w_ffn_gate, m_w_ffn_up, m_w_ffn_down, m_final_g, v_norm1_g, v_w_in, v_sgu_ln_g, v_sgu_ln_b, v_w_spatial, v_b_spatial, v_w_proj_attn, v_w_proj_sgu, v_w_out, v_norm2_g, v_w_ffn_gate, v_w_ffn_up, v_w_ffn_down, v_final_g):
    w = dict(norm1_g=norm1_g, w_in=w_in, sgu_ln_g=sgu_ln_g, sgu_ln_b=sgu_ln_b, w_spatial=w_spatial, b_spatial=b_spatial,
             w_proj_attn=w_proj_attn, w_proj_sgu=w_proj_sgu, w_out=w_out, norm2_g=norm2_g, w_ffn_gate=w_ffn_gate,
             w_ffn_up=w_ffn_up, w_ffn_down=w_ffn_down, final_g=final_g)
    m = dict(norm1_g=m_norm1_g, w_in=m_w_in, sgu_ln_g=m_sgu_ln_g, sgu_ln_b=m_sgu_ln_b, w_spatial=m_w_spatial,
             b_spatial=m_b_spatial, w_proj_attn=m_w_proj_attn, w_proj_sgu=m_w_proj_sgu, w_out=m_w_out, norm2_g=m_norm2_g,
             w_ffn_gate=m_w_ffn_gate, w_ffn_up=m_w_ffn_up, w_ffn_down=m_w_ffn_down, final_g=m_final_g)
    v = dict(norm1_g=v_norm1_g, w_in=v_w_in, sgu_ln_g=v_sgu_ln_g, sgu_ln_b=v_sgu_ln_b, w_spatial=v_w_spatial,
             b_spatial=v_b_spatial, w_proj_attn=v_w_proj_attn, w_proj_sgu=v_w_proj_sgu, w_out=v_w_out, norm2_g=v_norm2_g,
             w_ffn_gate=v_w_ffn_gate, w_ffn_up=v_w_ffn_up, w_ffn_down=v_w_ffn_down, final_g=v_final_g)
    t = x.shape[1]

    shards = {n: _ew(f"cast_{n}", lambda a: (a,), [w[n][0]], [BF16])[0] for n in BIG}
    late = COMM_GROUPS[1]
    k_in, n_in = shards["w_in"].shape
    *first, token = _split_start("gather_start_0", _gather_half_copies, 3, [shards["w_in"]], [(3, k_in // 2, n_in)])
    pending = {}

    def first_weight(after):
        srcs, filled = _split_wait("gather_wait_0", _gather_half_copies, *first, after)
        gath_in, late_shards = lax.optimization_barrier(
            (_gather_finish("gather_finish_0", srcs[0], filled[0]), [shards[n] for n in late]))
        *pending["late"], _ = _split_start(
            "gather_start_1", _gather_copies, 3, late_shards, [(N_CHIPS,) + s.shape for s in late_shards])
        return _cols_from_chips(gath_in)

    def late_weights(after):
        srcs, filled = _split_wait("gather_wait_1", _gather_copies, *pending["late"], after)
        me = 2 * lax.axis_index("x") + lax.axis_index("y")
        gath = {n: lax.dynamic_update_slice(f, s[None], (me, 0, 0)) for n, f, s in zip(late, filled, srcs)}
        return (_cols_from_chips(gath["w_proj_attn"]), _cols_from_chips(gath["w_proj_sgu"]),
                gath["w_out"].reshape(D_MODEL, D_MODEL), gath["w_ffn_gate"], gath["w_ffn_up"], gath["w_ffn_down"])

    exchanges = {}

    def on_grads(i, partials):
        if "w_out" in partials:
            partials["w_out"] = partials["w_out"].reshape(N_CHIPS, D_MODEL // N_CHIPS, D_MODEL)
        parts = [partials[n] for n in COMM_GROUPS[i]]
        *exchanges[i], started = _split_start(
            f"rs_exchange_start_{i}", _exchange_copies, N_PEERS, parts, [(N_PEERS,) + _piece_shape(p.shape) for p in parts])
        return started

    dx, _, small = _local_step(
        x[0], positions.reshape(t, 1), loss_target[0], norm1_g + token, sgu_ln_g, sgu_ln_b, w_spatial[0], b_spatial[0],
        norm2_g, final_g.reshape(1, D_MODEL), first_weight, late_weights, on_grads=on_grads)

    grads = {}
    for i in (1, 0):
        parts, filled = _split_wait(f"rs_exchange_wait_{i}", _exchange_copies, *exchanges[i], dx)
        grads.update(zip(COMM_GROUPS[i], _device_sum(f"rs_device_sum_{i}", parts, filled)))

    delta, new_m, new_v = {}, {}, {}
    for n in BIG:
        shp = w[n].shape
        g_, d_, m_, v_ = _adamw(f"adamw_{n}", grads[n], w[n][0], m[n][0], v[n][0])
        grads[n], delta[n], new_m[n], new_v[n] = g_.reshape(shp), d_.reshape(shp), m_.reshape(shp), v_.reshape(shp)

    g_s, d_s, m_s, v_s = _small_step(small, w, m, v)
    loss = g_s["loss"][0, 0]
    for n in SMALL_PARAMS:
        shp = w[n].shape
        grads[n], delta[n], new_m[n], new_v[n] = (a[n].reshape(shp) for a in (g_s, d_s, m_s, v_s))

    return (loss, dx.reshape(x.shape), *[grads[n] for n in WEIGHTS], *[delta[n] for n in WEIGHTS],
            *[new_m[n] for n in WEIGHTS], *[new_v[n] for n in WEIGHTS])
```

```python
import functools

import numpy as np
import jax
import jax.numpy as jnp
from jax import lax
from jax.experimental import pallas as pl
from jax.experimental.pallas import tpu as pltpu

F32, BF16 = jnp.float32, jnp.bfloat16
MESH = pl.DeviceIdType.MESH

D_MODEL = 1024
HEAD_DIM = 64
ATTN_W = 512
DILATIONS = (1, 4, 16)
BLK = 128
ATTN_BLOCKS_PER_STEP = 4
ROPE_DIM = 16
ROPE_THETA = 500000.0
SGU_W = 512
SGU_CHUNK = 128
SGU_GROUPS = 8
D_FF = 2816
N_CHIPS = 4
FF_SHARD = D_FF // N_CHIPS
IN_COLS = 7680
EPS = 1e-6
NEG = -1e30
LANES = 128
VMEM_LIMIT = 52 * 1024 * 1024

ADAM_LR, ADAM_B1, ADAM_B2, ADAM_EPS, ADAM_WD, ADAM_STEP = 0.001, 0.9, 0.999, 1e-08, 0.01, 10

QKV_BLOCKS = 9


def _w_in_block(part, g):
    return part * len(DILATIONS) + g


def _cparams(ngrid):
    return pltpu.CompilerParams(dimension_semantics=("arbitrary",) * ngrid, vmem_limit_bytes=VMEM_LIMIT)


def _full(shape):
    return pl.BlockSpec(shape, lambda *_: (0,) * len(shape))


def _resident(shape):
    return pl.BlockSpec(shape, lambda *_: (0,) * len(shape), pipeline_mode=pl.Buffered(1))


NN = ((1,), (0,))
NT = ((1,), (1,))
TN = ((0,), (0,))


def _mm(name, grid, pairs, dims, acc_shape, epi, *, extras=(), outs=(), reds=(), aliases=None):
    nk = grid[-1]
    npair, nex, nout, nred = len(pairs), len(extras), len(outs), len(reds)

    def body(*refs):
        a_refs = refs[:npair]
        b_refs = refs[npair:2 * npair]
        p0 = 2 * npair
        e_refs = refs[p0:p0 + nex]
        o_refs = refs[p0 + nex:p0 + nex + nout]
        r_refs = refs[p0 + nex + nout:p0 + nex + nout + nred]
        ids = [pl.program_id(a) for a in range(len(grid))]
        k = ids[-1]
        if nred:
            first = ids[0] == 0
            for v in ids[1:]:
                first = first & (v == 0)

            @pl.when(first)
            def _():
                for r in r_refs:
                    r[...] = jnp.zeros(r.shape, r.dtype)

        part = None
        for a_ref, b_ref in zip(a_refs, b_refs):
            d = lax.dot_general(a_ref[...], b_ref[...], (dims, ((), ())), preferred_element_type=F32)
            part = d if part is None else part + d
        if nk == 1:
            epi(part, e_refs, o_refs, r_refs, ids)
        else:
            acc_ref = refs[-1]

            @pl.when(k == 0)
            def _():
                acc_ref[...] = part

            @pl.when(k > 0)
            def _():
                acc_ref[...] += part

            @pl.when(k == nk - 1)
            def _():
                epi(acc_ref[...], e_refs, o_refs, r_refs, ids)

    in_specs = [p[1] for p in pairs] + [p[3] for p in pairs] + [e[1] for e in extras]
    args = [p[0] for p in pairs] + [p[2] for p in pairs] + [e[0] for e in extras]
    out_shape = [jax.ShapeDtypeStruct(o[0], o[1]) for o in outs] + [jax.ShapeDtypeStruct(r, F32) for r in reds]
    out_specs = [o[2] for o in outs] + [_full(r) for r in reds]
    scratch_shapes = [pltpu.VMEM(acc_shape, F32)] if nk > 1 else []
    return pl.pallas_call(
        body, grid=grid, in_specs=in_specs, out_specs=out_specs, out_shape=out_shape, scratch_shapes=scratch_shapes,
        input_output_aliases=aliases or {}, compiler_params=_cparams(len(grid)), name=name)(*args)


def _rope(v, cos_t, sin_t):
    half = ROPE_DIM // 2
    first = (lax.broadcasted_iota(jnp.int32, cos_t.shape, 1) % HEAD_DIM) < half
    outs = []
    for cs in range(v.shape[1] // LANES):
        x = v[:, cs * LANES:(cs + 1) * LANES]
        partner = jnp.where(first, pltpu.roll(x, LANES - half, axis=1), pltpu.roll(x, half, axis=1))
        outs.append(x * cos_t + partner * sin_t)
    return outs[0] if len(outs) == 1 else jnp.concatenate(outs, axis=1)


def _spread_heads(v2, upper):
    other = pltpu.roll(v2, HEAD_DIM, axis=1)
    h0 = jnp.where(upper, other, v2)
    h1 = jnp.where(upper, v2, other)
    return jnp.concatenate([jnp.concatenate([h0, h0], axis=1), jnp.concatenate([h1, h1], axis=1)], axis=0)


def _sigmoid(v):
    return 0.5 * jnp.tanh(0.5 * v) + 0.5


def _rms_stats(v):
    r = lax.rsqrt(jnp.mean(v * v, axis=-1, keepdims=True) + EPS)
    return v * r, r


def _rms_bwd(dy, xhat, r, g):
    dxh = dy * g
    return r * (dxh - xhat * jnp.mean(dxh * xhat, axis=-1, keepdims=True))


def _head_sum_matrix():
    idx = np.arange(ATTN_W) // HEAD_DIM
    return jnp.asarray((idx[:, None] == idx[None, :]).astype(np.float32), dtype=BF16)


def _group_sum(v, e):
    hi = v.astype(BF16)
    lo = (v - hi.astype(F32)).astype(BF16)
    return jnp.dot(hi, e, preferred_element_type=F32) + jnp.dot(lo, e, preferred_element_type=F32)


TILE = 512


def _to_slabs(slab_ref, v):
    for cs in range(slab_ref.shape[0]):
        slab_ref[cs] = v[:, cs * LANES:(cs + 1) * LANES]


def _from_slabs(slab_ref):
    return jnp.concatenate([slab_ref[cs] for cs in range(slab_ref.shape[0])], axis=1)


def _class_rows(slab_ref, r, dil):
    n = slab_ref.shape[1] // dil
    return jnp.concatenate([slab_ref.at[cs][pl.ds(r, n, stride=dil), :] for cs in range(slab_ref.shape[0])], axis=1)


def _put_class_rows(slab_ref, r, dil, v):
    n = slab_ref.shape[1] // dil
    for cs in range(slab_ref.shape[0]):
        slab_ref.at[cs][pl.ds(r, n, stride=dil), :] = v[:, cs * LANES:(cs + 1) * LANES]


def _natural_from_group(slab_ref, grp_ref):
    dil = grp_ref.shape[0]
    for r in range(dil):
        _put_class_rows(slab_ref, r, dil, grp_ref[r].astype(F32))
    return _from_slabs(slab_ref)


def _group_from_natural(slab_ref, grp_ref, v):
    dil = grp_ref.shape[0]
    _to_slabs(slab_ref, v)
    for r in range(dil):
        grp_ref[r] = _class_rows(slab_ref, r, dil).astype(grp_ref.dtype)


def _group_spec(dil, tile, width):
    return pl.BlockSpec((dil, tile // dil, width), lambda i, *_: (0, i, 0))


def _slabs(tile, width):
    return pltpu.VMEM((width // LANES, tile, LANES), F32)


def _rope_consts():
    lane = np.arange(LANES) % HEAD_DIM
    fi = lane % (ROPE_DIM // 2)
    invf = np.where(lane < ROPE_DIM, ROPE_THETA ** (-(2.0 * fi) / ROPE_DIM), 0.0)
    sgn = np.where(lane < ROPE_DIM // 2, -1.0, np.where(lane < ROPE_DIM, 1.0, 0.0))
    return (jnp.asarray(invf.astype(np.float32)).reshape(1, LANES), jnp.asarray(sgn.astype(np.float32)).reshape(1, LANES))


def _rope_tables(pos_col):
    t = pos_col.shape[0]
    tile = min(t, TILE)
    invf, sgn = _rope_consts()

    def body(p_ref, f_ref, s_ref, c0, s0, c1, s1, c2, s2, slab_c, slab_s):
        ang = p_ref[...].astype(F32) * f_ref[...]
        cos, sin = jnp.cos(ang), jnp.sin(ang) * s_ref[...]
        c0[...] = cos
        s0[...] = sin
        _group_from_natural(slab_c, c1, cos)
        _group_from_natural(slab_s, s1, sin)
        for r in range(DILATIONS[2]):
            c2[r] = _class_rows(slab_c, r, DILATIONS[2])
            s2[r] = _class_rows(slab_s, r, DILATIONS[2])

    nat = pl.BlockSpec((tile, LANES), lambda i: (i, 0))
    specs, shapes = [nat, nat], [(t, LANES)] * 2
    for d in DILATIONS[1:]:
        specs += [_group_spec(d, tile, LANES)] * 2
        shapes += [(d, t // d, LANES)] * 2
    outs = pl.pallas_call(
        body, grid=(t // tile,),
        in_specs=[pl.BlockSpec((tile, 1), lambda i: (i, 0)), _full((1, LANES)), _full((1, LANES))],
        out_specs=specs, out_shape=[jax.ShapeDtypeStruct(s, F32) for s in shapes],
        scratch_shapes=[_slabs(tile, LANES)] * 2,
        compiler_params=_cparams(1), name="rope_tables")(pos_col, invf, sgn)
    return [(outs[2 * g].reshape(t, LANES), outs[2 * g + 1].reshape(t, LANES)) for g in range(len(DILATIONS))]


def _norm_fwd(x, g):
    t = x.shape[0]
    tile = min(t, TILE)

    def body(x_ref, g_ref, h0_ref, h1_ref, h2_ref, slab):
        xhat, _ = _rms_stats(x_ref[...])
        hn = xhat * g_ref[...]
        h0_ref[...] = hn.astype(BF16)
        _group_from_natural(slab, h1_ref, hn)
        for r in range(DILATIONS[2]):
            h2_ref[r] = _class_rows(slab, r, DILATIONS[2]).astype(BF16)

    nat = pl.BlockSpec((tile, D_MODEL), lambda i: (i, 0))
    return pl.pallas_call(
        body, grid=(t // tile,),
        in_specs=[nat, _full((1, D_MODEL))],
        out_specs=[nat] + [_group_spec(d, tile, D_MODEL) for d in DILATIONS[1:]],
        out_shape=[jax.ShapeDtypeStruct((t, D_MODEL), BF16)]
        + [jax.ShapeDtypeStruct((d, t // d, D_MODEL), BF16) for d in DILATIONS[1:]],
        scratch_shapes=[_slabs(tile, D_MODEL)],
        compiler_params=_cparams(1), name="norm1_fwd")(x, g)


GU_COLS = 3072
GROUP_COLS = 1536
GU_HALF = GU_COLS // 2


def _w_in_spec(width, block):
    return pl.BlockSpec((D_MODEL, width), lambda i: (0, block), pipeline_mode=pl.Buffered(1))


def _gu_w_specs():
    first = QKV_BLOCKS * ATTN_W // GU_HALF
    return [_w_in_spec(GU_HALF, first), _w_in_spec(GU_HALF, first + 1)]


def _group_w_specs(g):
    return [_w_in_spec(ATTN_W, _w_in_block(part, g)) for part in range(3)]


def _in_proj(hs, w_in, tables):
    t = hs[0].shape[0]
    tm = min(t, 1024)

    def body_gu(h_ref, w0_ref, w1_ref, o_ref):
        h = h_ref[...]
        o_ref[:, 0:GU_HALF] = jnp.dot(h, w0_ref[...], preferred_element_type=F32).astype(BF16)
        o_ref[:, GU_HALF:] = jnp.dot(h, w1_ref[...], preferred_element_type=F32).astype(BF16)

    gu = _token_call("in_proj_gates_uv", body_gu, t, tm,
                     [(hs[0], _rows_spec(tm, D_MODEL))] + [(w_in, s) for s in _gu_w_specs()],
                     [((t, GU_COLS), BF16, _rows_spec(tm, GU_COLS))])[0]

    qkvs = []
    for g in range(len(DILATIONS)):

        def body_qkv(h_ref, wq_ref, wk_ref, wv_ref, cos_ref, sin_ref, o_ref):
            h = h_ref[...]
            cos_w, sin_w = cos_ref[...], sin_ref[...]
            q = jnp.dot(h, wq_ref[...], preferred_element_type=F32)
            o_ref[:, 0:ATTN_W] = (_rope(q, cos_w, sin_w) * HEAD_DIM ** -0.5).astype(BF16)
            k = jnp.dot(h, wk_ref[...], preferred_element_type=F32)
            o_ref[:, ATTN_W:2 * ATTN_W] = _rope(k, cos_w, sin_w).astype(BF16)
            o_ref[:, 2 * ATTN_W:] = jnp.dot(h, wv_ref[...], preferred_element_type=F32).astype(BF16)

        cos_t, sin_t = tables[g]
        qkvs.append(_token_call(
            f"in_proj_qkv_g{g}", body_qkv, t, tm,
            [(hs[g].reshape(t, D_MODEL), _rows_spec(tm, D_MODEL))] + [(w_in, s) for s in _group_w_specs(g)]
            + [(cos_t, _rows_spec(tm, LANES)), (sin_t, _rows_spec(tm, LANES))],
            [((t, GROUP_COLS), BF16, _rows_spec(tm, GROUP_COLS))])[0])
    return gu, qkvs


def _attn_masks(n):
    row = lax.broadcasted_iota(jnp.int32, (2 * BLK, 2 * BLK), 0) % BLK
    col = lax.broadcasted_iota(jnp.int32, (2 * BLK, 2 * BLK), 1)
    diff = BLK + row - col
    valid = (diff >= 0) & (diff <= BLK) & ((col >= BLK) | (n > 0))
    upper = lax.broadcasted_iota(jnp.int32, (BLK, LANES), 1) >= HEAD_DIM
    return valid, upper


def _stack_heads(v2, upper):
    zero = jnp.zeros_like(v2)
    return jnp.concatenate([jnp.where(upper, zero, v2), jnp.where(upper, v2, zero)], axis=0)


def _unstack_heads(v, upper):
    return jnp.where(upper, v[BLK:], v[:BLK])


def _attn_fwd(qkv, g, dil):
    t = qkv.shape[0]
    length = t // dil
    nb = length // BLK
    per_step = min(nb, ATTN_BLOCKS_PER_STEP)
    view = qkv.reshape(dil, length, GROUP_COLS)

    def body(q_ref, kc_ref, kp_ref, vc_ref, vp_ref, o_ref, l_ref, kwin, vwin):
        n = pl.program_id(1)
        kwin[0:BLK] = kp_ref[...]
        kwin[BLK:] = kc_ref[...]
        vwin[0:BLK] = vp_ref[...]
        vwin[BLK:] = vc_ref[...]

        def block(b, carry):
            valid, upper = _attn_masks(n * per_step + b)
            rows = pl.ds(pl.multiple_of(b * BLK, BLK), BLK)
            window = pl.ds(pl.multiple_of(b * BLK, BLK), 2 * BLK)
            for p in range(ATTN_W // LANES):
                sl = slice(p * LANES, (p + 1) * LANES)
                qs = _stack_heads(q_ref[rows, sl], upper)
                s = lax.dot_general(qs, kwin[window, sl], (NT, ((), ())), preferred_element_type=F32)
                s = jnp.where(valid, s, NEG)
                m = jnp.max(s, axis=1, keepdims=True)
                pe = jnp.exp(s - m)
                den = jnp.sum(pe, axis=1, keepdims=True)
                o = jnp.dot(pe.astype(BF16), vwin[window, sl], preferred_element_type=F32) / den
                lse = jnp.broadcast_to(m + jnp.log(den), (2 * BLK, LANES))
                o_ref[rows, sl] = _unstack_heads(o, upper).astype(BF16)
                l_ref[rows, sl] = _unstack_heads(lse, upper)
            return carry

        lax.fori_loop(0, per_step, block, 0)

    rows = per_step * BLK
    cur = lambda part: pl.BlockSpec((None, rows, ATTN_W), lambda r, n: (r, n, part))
    prev = lambda part: pl.BlockSpec((None, BLK, ATTN_W), lambda r, n: (r, jnp.maximum(n * per_step - 1, 0), part))
    out_spec = pl.BlockSpec((None, rows, ATTN_W), lambda r, n: (r, n, 0))
    return pl.pallas_call(
        body, grid=(dil, nb // per_step),
        in_specs=[cur(0), cur(1), prev(1), cur(2), prev(2)],
        out_specs=[out_spec, out_spec],
        out_shape=[jax.ShapeDtypeStruct((dil, length, ATTN_W), BF16), jax.ShapeDtypeStruct((dil, length, ATTN_W), F32)],
        scratch_shapes=[pltpu.VMEM((rows + BLK, ATTN_W), BF16)] * 2,
        compiler_params=_cparams(2), name=f"attn_fwd_g{g}")(view, view, view, view, view)


def _alphas(l0, l1, l2):
    m = jnp.maximum(jnp.maximum(l0, l1), l2)
    e0, e1, e2 = jnp.exp(l0 - m), jnp.exp(l1 - m), jnp.exp(l2 - m)
    inv = 1.0 / (e0 + e1 + e2)
    return e0 * inv, e1 * inv, e2 * inv


def _natural_group_values(o_refs, l_refs, slabs):
    os_ = [o_refs[0][0].astype(F32)] + [_natural_from_group(slabs[2 * g - 2], o_refs[g]) for g in (1, 2)]
    ls_ = [l_refs[0][0]] + [_natural_from_group(slabs[2 * g - 1], l_refs[g]) for g in (1, 2)]
    return os_, ls_


def _combine_fwd(os_, ls_):
    t = os_[0].shape[1]
    tile = min(t, TILE)

    def body(o0, o1, o2, l0, l1, l2, a_ref, *slabs):
        ov, lv = _natural_group_values((o0, o1, o2), (l0, l1, l2), slabs)
        a0, a1, a2 = _alphas(*lv)
        a_ref[...] = (a0 * ov[0] + a1 * ov[1] + a2 * ov[2]).astype(BF16)

    specs = [_group_spec(d, tile, ATTN_W) for d in DILATIONS]
    return pl.pallas_call(
        body, grid=(t // tile,), in_specs=specs * 2, out_specs=pl.BlockSpec((tile, ATTN_W), lambda i: (i, 0)),
        out_shape=jax.ShapeDtypeStruct((t, ATTN_W), BF16),
        scratch_shapes=[_slabs(tile, ATTN_W)] * 4,
        compiler_params=_cparams(1), name="combine_fwd")(*os_, *ls_)


def _combine_bwd(dattn, os_, ls_):
    t = dattn.shape[0]
    tile = min(t, TILE)
    e = _head_sum_matrix()

    def body(d_ref, o0, o1, o2, l0, l1, l2, e_ref, do0, do1, do2, c0, c1, c2, *slabs):
        ov, lv = _natural_group_values((o0, o1, o2), (l0, l1, l2), slabs)
        alphas = _alphas(*lv)
        d = d_ref[...]
        attn = alphas[0] * ov[0] + alphas[1] * ov[1] + alphas[2] * ov[2]
        s = _group_sum(d * attn, e_ref[...])
        do0[0] = (alphas[0] * d).astype(BF16)
        c0[0] = -alphas[0] * s
        for g, do_ref, c_ref in ((1, do1, c1), (2, do2, c2)):
            _group_from_natural(slabs[2 * g - 2], do_ref, alphas[g] * d)
            _group_from_natural(slabs[2 * g - 1], c_ref, -alphas[g] * s)

    specs = [_group_spec(d, tile, ATTN_W) for d in DILATIONS]
    shapes = [(d, t // d, ATTN_W) for d in DILATIONS]
    outs = pl.pallas_call(
        body, grid=(t // tile,),
        in_specs=[pl.BlockSpec((tile, ATTN_W), lambda i: (i, 0))] + specs * 2 + [_full((ATTN_W, ATTN_W))],
        out_specs=specs * 2,
        out_shape=[jax.ShapeDtypeStruct(s, BF16) for s in shapes] + [jax.ShapeDtypeStruct(s, F32) for s in shapes],
        scratch_shapes=[_slabs(tile, ATTN_W)] * 4,
        compiler_params=_cparams(1), name="combine_bwd")(dattn, *os_, *ls_, e)
    return outs[:3], outs[3:]


def _attn_bwd(qkv, do, cc, lse, cos_t, sin_t, g, dil):
    t = qkv.shape[0]
    length = t // dil
    nb = length // BLK
    per_step = min(nb, ATTN_BLOCKS_PER_STEP)
    nsteps = nb // per_step
    rows_per_step = per_step * BLK
    qkv_v = qkv.reshape(dil, length, GROUP_COLS)
    cos_v, sin_v = (a.reshape(dil, length, LANES) for a in (cos_t, sin_t))
    scale = HEAD_DIM ** -0.5
    dq_cols, dk_cols, dv_cols = (slice(i * ATTN_W, (i + 1) * ATTN_W) for i in range(3))

    def body(q_ref, kc_ref, kp_ref, vc_ref, vp_ref, do_ref, c_ref, l_ref, cosc, sinc, cosp, sinp,
             out_ref, acc, kwin, vwin, cwin, swin):
        n = pl.program_id(1)

        def one_block(b):
            valid, upper = _attn_masks(n * per_step + b)
            start = b * BLK if isinstance(b, int) else pl.multiple_of(b * BLK, BLK)
            rows, before, window = pl.ds(start, BLK), pl.ds(start, BLK), pl.ds(start, 2 * BLK)
            own = pl.ds(start + BLK, BLK)
            dq_parts, dkp_parts, dkc_parts, dvp_parts, dvc_parts = [], [], [], [], []
            for p in range(ATTN_W // LANES):
                sl = slice(p * LANES, (p + 1) * LANES)
                qs = _stack_heads(q_ref[rows, sl], upper)
                dos = _stack_heads(do_ref[rows, sl], upper)
                k2 = kwin[window, sl]
                l_col = _spread_heads(l_ref[rows, sl], upper)
                c_col = _spread_heads(c_ref[rows, sl], upper)
                s = lax.dot_general(qs, k2, (NT, ((), ())), preferred_element_type=F32)
                pe = jnp.exp(jnp.where(valid, s, NEG) - l_col)
                dpv = lax.dot_general(dos, vwin[window, sl], (NT, ((), ())), preferred_element_type=F32)
                ds = (pe * (dpv + c_col)).astype(BF16)
                dq2 = _unstack_heads(jnp.dot(ds, k2, preferred_element_type=F32), upper)
                dk2 = lax.dot_general(ds, qs, (TN, ((), ())), preferred_element_type=F32)
                dv2 = lax.dot_general(pe.astype(BF16), dos, (TN, ((), ())), preferred_element_type=F32)
                dq_parts.append(dq2)
                dkp_parts.append(dk2[:BLK])
                dkc_parts.append(dk2[BLK:])
                dvp_parts.append(dv2[:BLK])
                dvc_parts.append(dv2[BLK:])
            dq = _rope(jnp.concatenate(dq_parts, axis=1) * scale, cwin[own, :], swin[own, :])
            dkc = _rope(jnp.concatenate(dkc_parts, axis=1), cwin[own, :], swin[own, :])
            dkp = _rope(jnp.concatenate(dkp_parts, axis=1), cwin[before, :], swin[before, :])
            return dq, dkp, dkc, jnp.concatenate(dvp_parts, axis=1), jnp.concatenate(dvc_parts, axis=1)

        @pl.when(n < nsteps)
        def _():
            kwin[0:BLK] = kp_ref[...]
            kwin[BLK:] = kc_ref[...]
            vwin[0:BLK] = vp_ref[...]
            vwin[BLK:] = vc_ref[...]
            cwin[0:BLK] = cosp[...]
            cwin[BLK:] = cosc[...]
            swin[0:BLK] = -sinp[...]
            swin[BLK:] = -sinc[...]
            dq, dkp, dkc, dvp, dvc = one_block(0)
            last = slice(rows_per_step - BLK, rows_per_step)

            @pl.when(n > 0)
            def _():
                if per_step > 1:
                    out_ref[0:rows_per_step - BLK, :] = acc[0:rows_per_step - BLK, :].astype(BF16)
                out_ref[last, dq_cols] = acc[last, dq_cols].astype(BF16)
                out_ref[last, dk_cols] = (acc[last, dk_cols] + dkp).astype(BF16)
                out_ref[last, dv_cols] = (acc[last, dv_cols] + dvp).astype(BF16)

            acc[0:BLK, dq_cols] = dq
            acc[0:BLK, dk_cols] = dkc
            acc[0:BLK, dv_cols] = dvc

            def later(b, carry):
                dq, dkp, dkc, dvp, dvc = one_block(b)
                start = pl.multiple_of(b * BLK, BLK)
                before, rows = pl.ds(start - BLK, BLK), pl.ds(start, BLK)
                acc[before, dk_cols] += dkp
                acc[before, dv_cols] += dvp
                acc[rows, dq_cols] = dq
                acc[rows, dk_cols] = dkc
                acc[rows, dv_cols] = dvc
                return carry

            lax.fori_loop(1, per_step, later, 0)

        @pl.when(n == flush_at)
        def _():
            out_ref[...] = acc[...].astype(BF16)

    flush_at = nsteps - 1 if nsteps == 1 else nsteps
    out_lag = 0 if nsteps == 1 else 1
    nc = lambda n: jnp.minimum(n, nsteps - 1)
    npv = lambda n: jnp.maximum(jnp.minimum(n, nsteps - 1) * per_step - 1, 0)
    cur = lambda part: pl.BlockSpec((None, rows_per_step, ATTN_W), lambda r, n: (r, nc(n), part))
    prev = lambda part: pl.BlockSpec((None, BLK, ATTN_W), lambda r, n: (r, npv(n), part))
    row = pl.BlockSpec((None, rows_per_step, ATTN_W), lambda r, n: (r, nc(n), 0))
    tab_c = pl.BlockSpec((None, rows_per_step, LANES), lambda r, n: (r, nc(n), 0))
    tab_p = pl.BlockSpec((None, BLK, LANES), lambda r, n: (r, npv(n), 0))
    out_spec = pl.BlockSpec((None, rows_per_step, GROUP_COLS), lambda r, n: (r, jnp.maximum(n - out_lag, 0), 0))
    out = pl.pallas_call(
        body, grid=(dil, nsteps + out_lag),
        in_specs=[cur(0), cur(1), prev(1), cur(2), prev(2), row, row, row, tab_c, tab_c, tab_p, tab_p],
        out_specs=out_spec,
        out_shape=jax.ShapeDtypeStruct((dil, length, GROUP_COLS), BF16),
        scratch_shapes=[pltpu.VMEM((rows_per_step, GROUP_COLS), F32)]
        + [pltpu.VMEM((rows_per_step + BLK, ATTN_W), BF16)] * 2 + [pltpu.VMEM((rows_per_step + BLK, LANES), F32)] * 2,
        compiler_params=_cparams(2), name=f"attn_bwd_g{g}")(
            qkv_v, qkv_v, qkv_v, qkv_v, qkv_v, do, cc, lse, cos_v, sin_v, cos_v, sin_v)
    return out.reshape(t, GROUP_COLS)


SQRT_HALF = 0.7071067811865476
INV_SQRT_2PI = 0.3989422804014327


def _sgu_core(uv, g, b, w_ref, bias):
    cdf = 0.5 * (1.0 + lax.erf(uv * SQRT_HALF))
    z = uv * cdf
    u, v = z[:, :SGU_W], z[:, SGU_W:]
    mu = jnp.mean(v, axis=1, keepdims=True)
    xc = v - mu
    rs = lax.rsqrt(jnp.mean(xc * xc, axis=1, keepdims=True) + EPS)
    xhat = xc * rs
    vn = xhat * g + b
    row = lax.broadcasted_iota(jnp.int32, (SGU_CHUNK, SGU_CHUNK), 0)
    col = lax.broadcasted_iota(jnp.int32, (SGU_CHUNK, SGU_CHUNK), 1)
    tril = row >= col
    upper = lax.broadcasted_iota(jnp.int32, (SGU_CHUNK, LANES), 1) >= SGU_W // SGU_GROUPS
    ws, vlo, vhi, mixed = [], [], [], []
    for pr in range(SGU_W // LANES):
        sl = slice(pr * LANES, (pr + 1) * LANES)
        w0 = jnp.where(tril, w_ref[2 * pr], 0.0).astype(BF16)
        w1 = jnp.where(tril, w_ref[2 * pr + 1], 0.0).astype(BF16)
        vn2 = vn[:, sl]
        lo = jnp.where(upper, 0.0, vn2).astype(BF16)
        hi = jnp.where(upper, vn2, 0.0).astype(BF16)
        mixed.append(jnp.dot(w0, lo, preferred_element_type=F32) + jnp.dot(w1, hi, preferred_element_type=F32)
                     + bias[:, sl])
        ws.append((w0, w1))
        vlo.append(lo)
        vhi.append(hi)
    return cdf, u, xhat, rs, jnp.concatenate(mixed, axis=1), ws, vlo, vhi, tril, upper


SGU_STEP = 4 * SGU_CHUNK


def _for_chunks(step_rows, fn):
    def one(ci, carry):
        fn(pl.ds(pl.multiple_of(ci * SGU_CHUNK, SGU_CHUNK), SGU_CHUNK))
        return carry

    lax.fori_loop(0, step_rows // SGU_CHUNK, one, 0)


def _sgu_fwd(gu, ln_g, ln_b, w_s, bias_exp):
    t = gu.shape[0]
    step = min(t, SGU_STEP)

    def body(uv_ref, g_ref, b_ref, w_ref, bias_ref, o_ref):
        def chunk(rows):
            _, u, _, _, mixed, *_ = _sgu_core(uv_ref[rows, :].astype(F32), g_ref[...], b_ref[...], w_ref, bias_ref[...])
            o_ref[rows, :] = (u * mixed).astype(BF16)

        _for_chunks(step, chunk)

    return pl.pallas_call(
        body, grid=(t // step,),
        in_specs=[pl.BlockSpec((step, 2 * SGU_W), lambda n: (n, 0)), _full((1, SGU_W)), _full((1, SGU_W)),
                  _full((SGU_GROUPS, SGU_CHUNK, SGU_CHUNK)), _full((SGU_CHUNK, SGU_W))],
        out_specs=pl.BlockSpec((step, SGU_W), lambda n: (n, 0)),
        out_shape=jax.ShapeDtypeStruct((t, SGU_W), BF16),
        compiler_params=_cparams(1), name="sgu_fwd")(gu, ln_g, ln_b, w_s, bias_exp)


def _sgu_bwd(dproj, gu, dsgu, ln_g, ln_b, w_s, bias_exp):
    t = gu.shape[0]
    step = min(t, SGU_STEP)
    nsteps = t // step
    e = _head_sum_matrix()

    def body(dp_in, uv_ref, ds_ref, g_ref, b_ref, w_ref, bias_ref, e_ref, out_ref, dw_ref, dbias_ref, dg_ref, db_ref):
        n = pl.program_id(0)

        @pl.when(n == 0)
        def _():
            dw_ref[...] = jnp.zeros(dw_ref.shape, F32)
            dbias_ref[...] = jnp.zeros(dbias_ref.shape, F32)
            dg_ref[...] = jnp.zeros(dg_ref.shape, F32)
            db_ref[...] = jnp.zeros(db_ref.shape, F32)

        _for_chunks(step, functools.partial(chunk, uv_ref, ds_ref, g_ref, b_ref, w_ref, bias_ref, out_ref, dw_ref, dbias_ref,
                                            dg_ref, db_ref))

        @pl.when(n == nsteps - 1)
        def _():
            dbias_ref[...] = _group_sum(dbias_ref[...], e_ref[...])

    def chunk(uv_ref, ds_ref, g_ref, b_ref, w_ref, bias_ref, out_ref, dw_ref, dbias_ref, dg_ref, db_ref, rows):
        uv = uv_ref[rows, :].astype(F32)
        g = g_ref[...]
        cdf, u, xhat, rs, mixed, ws, vlo, vhi, tril, upper = _sgu_core(uv, g, b_ref[...], w_ref, bias_ref[...])
        dsg = ds_ref[rows, :]
        du = dsg * mixed
        dmixed = dsg * u
        dbias_ref[...] += dmixed
        dvn = []
        for pr in range(SGU_W // LANES):
            sl = slice(pr * LANES, (pr + 1) * LANES)
            dm2 = dmixed[:, sl]
            dlo = jnp.where(upper, 0.0, dm2).astype(BF16)
            dhi = jnp.where(upper, dm2, 0.0).astype(BF16)
            w0, w1 = ws[pr]
            dvn.append(lax.dot_general(w0, dlo, (TN, ((), ())), preferred_element_type=F32)
                       + lax.dot_general(w1, dhi, (TN, ((), ())), preferred_element_type=F32))
            dw0 = lax.dot_general(dlo, vlo[pr], (NT, ((), ())), preferred_element_type=F32)
            dw1 = lax.dot_general(dhi, vhi[pr], (NT, ((), ())), preferred_element_type=F32)
            dw_ref[2 * pr] += jnp.where(tril, dw0, 0.0)
            dw_ref[2 * pr + 1] += jnp.where(tril, dw1, 0.0)
        dvn = jnp.concatenate(dvn, axis=1)
        dg_ref[...] += jnp.sum(dvn * xhat, axis=0, keepdims=True)
        db_ref[...] += jnp.sum(dvn, axis=0, keepdims=True)
        dxh = dvn * g
        dv = rs * (dxh - jnp.mean(dxh, axis=1, keepdims=True) - xhat * jnp.mean(dxh * xhat, axis=1, keepdims=True))
        dz = jnp.concatenate([du, dv], axis=1)
        dgelu = cdf + uv * (INV_SQRT_2PI * jnp.exp(-0.5 * uv * uv))
        out_ref[rows, :] = (dz * dgelu).astype(BF16)

    outs = pl.pallas_call(
        body, grid=(nsteps,),
        in_specs=[pl.BlockSpec(memory_space=pl.ANY), pl.BlockSpec((step, 2 * SGU_W), lambda n: (n, 0)),
                  pl.BlockSpec((step, SGU_W), lambda n: (n, 0)), _full((1, SGU_W)), _full((1, SGU_W)),
                  _full((SGU_GROUPS, SGU_CHUNK, SGU_CHUNK)), _full((SGU_CHUNK, SGU_W)), _full((ATTN_W, ATTN_W))],
        out_specs=[pl.BlockSpec((step, 2 * SGU_W), lambda n: (n, 0)), _full((SGU_GROUPS, SGU_CHUNK, SGU_CHUNK)),
                   _full((SGU_CHUNK, SGU_W)), _full((1, SGU_W)), _full((1, SGU_W))],
        out_shape=[jax.ShapeDtypeStruct(dproj.shape, BF16), jax.ShapeDtypeStruct((SGU_GROUPS, SGU_CHUNK, SGU_CHUNK), F32),
                   jax.ShapeDtypeStruct((SGU_CHUNK, SGU_W), F32), jax.ShapeDtypeStruct((1, SGU_W), F32),
                   jax.ShapeDtypeStruct((1, SGU_W), F32)],
        input_output_aliases={0: 0},
        compiler_params=_cparams(1), name="sgu_bwd")(dproj, gu, dsgu, ln_g, ln_b, w_s, bias_exp, e)
    return outs


def _merge_fwd(attn, sgu, gu, x, w_pa, w_ps, w_out, g2):
    t = x.shape[0]
    tm = min(t, 512)

    def body(a_ref, s_ref, ga_ref, gb_ref, x_ref, wpa, wps, wo, g_ref, pa_ref, ps_ref, m_ref, x1_ref, h2_ref):
        pa = jnp.dot(a_ref[...], wpa[...], preferred_element_type=F32)
        ps = jnp.dot(s_ref[...], wps[...], preferred_element_type=F32)
        merged = (_sigmoid(ga_ref[...].astype(F32)) * pa + _sigmoid(gb_ref[...].astype(F32)) * ps).astype(BF16)
        x1 = x_ref[...] + jnp.dot(merged, wo[...], preferred_element_type=F32)
        xhat, _ = _rms_stats(x1)
        pa_ref[...] = pa.astype(BF16)
        ps_ref[...] = ps.astype(BF16)
        m_ref[...] = merged
        x1_ref[...] = x1
        h2_ref[...] = (xhat * g_ref[...]).astype(BF16)

    half = pl.BlockSpec((tm, ATTN_W), lambda i: (i, 0))
    full = pl.BlockSpec((tm, D_MODEL), lambda i: (i, 0))
    return pl.pallas_call(
        body, grid=(t // tm,),
        in_specs=[half, half, pl.BlockSpec((tm, D_MODEL), lambda i: (i, 1)), pl.BlockSpec((tm, D_MODEL), lambda i: (i, 2)),
                  full, _resident((ATTN_W, D_MODEL)), _resident((SGU_W, D_MODEL)), _resident((D_MODEL, D_MODEL)),
                  _full((1, D_MODEL))],
        out_specs=[full] * 5,
        out_shape=[jax.ShapeDtypeStruct((t, D_MODEL), BF16), jax.ShapeDtypeStruct((t, D_MODEL), BF16),
                   jax.ShapeDtypeStruct((t, D_MODEL), BF16), jax.ShapeDtypeStruct((t, D_MODEL), F32),
                   jax.ShapeDtypeStruct((t, D_MODEL), BF16)],
        compiler_params=_cparams(1), name="merge_fwd")(attn, sgu, gu, gu, x, w_pa, w_ps, w_out, g2)


def _merge_bwd(dx1b, gu, pa, ps, w_pa, w_ps, w_out):
    t = dx1b.shape[0]
    tm = min(t, 512)

    def body(d_ref, ga_ref, gb_ref, pa_ref, ps_ref, wpa, wps, wo, out_ref, dpa_ref, dps_ref, da_ref, dsg_ref):
        dm = lax.dot_general(d_ref[...], wo[...], (NT, ((), ())), preferred_element_type=F32)
        sa, sb = _sigmoid(ga_ref[...].astype(F32)), _sigmoid(gb_ref[...].astype(F32))
        dpa = (dm * sa).astype(BF16)
        dps = (dm * sb).astype(BF16)
        out_ref[:, 0:D_MODEL] = jnp.zeros((tm, D_MODEL), BF16)
        out_ref[:, D_MODEL:2 * D_MODEL] = (dm * pa_ref[...].astype(F32) * sa * (1.0 - sa)).astype(BF16)
        out_ref[:, 2 * D_MODEL:GU_COLS] = (dm * ps_ref[...].astype(F32) * sb * (1.0 - sb)).astype(BF16)
        dpa_ref[...] = dpa
        dps_ref[...] = dps
        da_ref[...] = lax.dot_general(dpa, wpa[...], (NT, ((), ())), preferred_element_type=F32)
        dsg_ref[...] = lax.dot_general(dps, wps[...], (NT, ((), ())), preferred_element_type=F32)

    half = pl.BlockSpec((tm, ATTN_W), lambda i: (i, 0))
    full = pl.BlockSpec((tm, D_MODEL), lambda i: (i, 0))
    return pl.pallas_call(
        body, grid=(t // tm,),
        in_specs=[full, pl.BlockSpec((tm, D_MODEL), lambda i: (i, 1)),
                  pl.BlockSpec((tm, D_MODEL), lambda i: (i, 2)), full, full,
                  _resident((ATTN_W, D_MODEL)), _resident((SGU_W, D_MODEL)), _resident((D_MODEL, D_MODEL))],
        out_specs=[pl.BlockSpec((tm, GU_COLS), lambda i: (i, 0)), full, full, half, half],
        out_shape=[jax.ShapeDtypeStruct((t, GU_COLS), BF16), jax.ShapeDtypeStruct((t, D_MODEL), BF16),
                   jax.ShapeDtypeStruct((t, D_MODEL), BF16), jax.ShapeDtypeStruct((t, ATTN_W), F32),
                   jax.ShapeDtypeStruct((t, SGU_W), F32)],
        compiler_params=_cparams(1), name="merge_bwd")(dx1b, gu, gu, pa, ps, w_pa, w_ps, w_out)


def _token_call(name, body, t, tm, ins, outs, reds=(), scratch=()):
    return pl.pallas_call(
        body, grid=(t // tm,), in_specs=[s for _, s in ins],
        out_specs=[o[2] for o in outs] + [_full(r) for r in reds],
        out_shape=[jax.ShapeDtypeStruct(o[0], o[1]) for o in outs] + [jax.ShapeDtypeStruct(r, F32) for r in reds],
        scratch_shapes=list(scratch), compiler_params=_cparams(1), name=name)(*[a for a, _ in ins])


def _rows_spec(tm, width):
    return pl.BlockSpec((tm, width), lambda i: (i, 0))


def _chips_spec(tm):
    return pl.BlockSpec((N_CHIPS, tm, FF_SHARD), lambda i: (0, i, 0))


def _zero_at_start(*refs):
    @pl.when(pl.program_id(0) == 0)
    def _():
        for r in refs:
            r[...] = jnp.zeros(r.shape, r.dtype)


def _ffn_fwd(h2, w_g, w_u):
    t = h2.shape[0]
    tm = min(t, 512)

    def body(h_ref, wg_ref, wu_ref, a_ref, b_ref, ff_ref):
        h = h_ref[...]
        for s in range(N_CHIPS):
            a = jnp.dot(h, wg_ref[s], preferred_element_type=F32)
            b = jnp.dot(h, wu_ref[s], preferred_element_type=F32)
            a_ref[s] = a.astype(BF16)
            b_ref[s] = b.astype(BF16)
            ff_ref[s] = (a * _sigmoid(a) * b).astype(BF16)

    shp = (N_CHIPS, t, FF_SHARD)
    w_spec = _resident((N_CHIPS, D_MODEL, FF_SHARD))
    return _token_call("ffn_fwd", body, t, tm, [(h2, _rows_spec(tm, D_MODEL)), (w_g, w_spec), (w_u, w_spec)],
                       [(shp, BF16, _chips_spec(tm))] * 3)


def _ffn_down_loss(ff, w_d, x1, tgt, gf):
    t = x1.shape[0]
    tm = min(t, 512)

    def body(ff_ref, wd_ref, x1_ref, tgt_ref, g_ref, dx2_ref, dx2b_ref, loss_ref, dgf_ref):
        _zero_at_start(loss_ref, dgf_ref)
        acc = jnp.dot(ff_ref[0], wd_ref[0], preferred_element_type=F32)
        for s in range(1, N_CHIPS):
            acc = acc + jnp.dot(ff_ref[s], wd_ref[s], preferred_element_type=F32)
        x2 = x1_ref[...] + acc
        g = g_ref[...]
        xhat, rr = _rms_stats(x2)
        diff = xhat * g - tgt_ref[...]
        rows = jnp.sum(diff * diff, axis=1, keepdims=True)
        loss_ref[...] += jnp.broadcast_to(jnp.sum(rows, axis=0, keepdims=True) * (0.5 / D_MODEL), (1, LANES))
        dy = diff * (1.0 / D_MODEL)
        dgf_ref[...] += jnp.sum(dy * xhat, axis=0, keepdims=True)
        dx2 = _rms_bwd(dy, xhat, rr, g)
        dx2_ref[...] = dx2
        dx2b_ref[...] = dx2.astype(BF16)

    row = _rows_spec(tm, D_MODEL)
    return _token_call("ffn_down_loss", body, t, tm,
                       [(ff, _chips_spec(tm)), (w_d, _resident((N_CHIPS, FF_SHARD, D_MODEL))), (x1, row), (tgt, row),
                        (gf, _full((1, D_MODEL)))],
                       [((t, D_MODEL), F32, row), ((t, D_MODEL), BF16, row)], reds=[(1, LANES), (1, D_MODEL)])


def _ffn_bwd_act(dx2b, w_d, a, b):
    t = dx2b.shape[0]
    tm = min(t, 512)

    def body(d_ref, wd_ref, a_ref, b_ref, da_ref, db_ref):
        d = d_ref[...]
        for s in range(N_CHIPS):
            dff = lax.dot_general(d, wd_ref[s], (NT, ((), ())), preferred_element_type=F32)
            av, bv = a_ref[s].astype(F32), b_ref[s].astype(F32)
            sg = _sigmoid(av)
            da_ref[s] = (dff * bv * (sg * (1.0 + av * (1.0 - sg)))).astype(BF16)
            db_ref[s] = (dff * (av * sg)).astype(BF16)

    shp = (N_CHIPS, t, FF_SHARD)
    return _token_call("ffn_bwd_act", body, t, tm,
                       [(dx2b, _rows_spec(tm, D_MODEL)), (w_d, _resident((N_CHIPS, FF_SHARD, D_MODEL))),
                        (a, _chips_spec(tm)), (b, _chips_spec(tm))],
                       [(shp, BF16, _chips_spec(tm))] * 2)


def _ffn_bwd_in(da, db, w_g, w_u, x1, dx2, g2):
    t = x1.shape[0]
    tm = min(t, 512)

    def body(da_ref, db_ref, wg_ref, wu_ref, x1_ref, dx2_ref, g_ref, dx1_ref, dx1b_ref, dg_ref):
        _zero_at_start(dg_ref)
        acc = None
        for s in range(N_CHIPS):
            part = (lax.dot_general(da_ref[s], wg_ref[s], (NT, ((), ())), preferred_element_type=F32)
                    + lax.dot_general(db_ref[s], wu_ref[s], (NT, ((), ())), preferred_element_type=F32))
            acc = part if acc is None else acc + part
        xhat, rr = _rms_stats(x1_ref[...])
        dg_ref[...] += jnp.sum(acc * xhat, axis=0, keepdims=True)
        dx1 = dx2_ref[...] + _rms_bwd(acc, xhat, rr, g_ref[...])
        dx1_ref[...] = dx1
        dx1b_ref[...] = dx1.astype(BF16)

    row = _rows_spec(tm, D_MODEL)
    w_spec = _resident((N_CHIPS, D_MODEL, FF_SHARD))
    return _token_call("ffn_bwd_in", body, t, tm,
                       [(da, _chips_spec(tm)), (db, _chips_spec(tm)), (w_g, w_spec), (w_u, w_spec), (x1, row), (dx2, row),
                        (g2, _full((1, D_MODEL)))],
                       [((t, D_MODEL), F32, row), ((t, D_MODEL), BF16, row)], reds=[(1, D_MODEL)])


def _group_dh(d, w_refs):
    dh = None
    for part, w_ref in enumerate(w_refs):
        term = lax.dot_general(d[:, part * ATTN_W:(part + 1) * ATTN_W], w_ref[...], (NT, ((), ())),
                               preferred_element_type=F32)
        dh = term if dh is None else dh + term
    return dh


def _in_proj_bwd(dgu, dqkvs, w_in, x, dx1, g1):
    t = x.shape[0]
    tile = min(t, TILE)
    ngroups = len(DILATIONS)

    def body(*refs):
        dgu_ref, dq_refs = refs[0], refs[1:1 + ngroups]
        w0_ref, w1_ref = refs[1 + ngroups:3 + ngroups]
        wg_refs = [refs[3 + ngroups + 3 * g:6 + ngroups + 3 * g] for g in range(ngroups)]
        x_ref, dx1_ref, g_ref, dx_ref, dg_ref, slab = refs[3 + 4 * ngroups:]
        _zero_at_start(dg_ref)
        dh = lax.dot_general(dgu_ref[:, 0:GU_HALF], w0_ref[...], (NT, ((), ())), preferred_element_type=F32)
        dh = dh + lax.dot_general(dgu_ref[:, GU_HALF:], w1_ref[...], (NT, ((), ())), preferred_element_type=F32)
        dh = dh + _group_dh(dq_refs[0][0], wg_refs[0])
        for g in range(1, ngroups):
            dil = DILATIONS[g]
            part = _group_dh(dq_refs[g][...].reshape(tile, GROUP_COLS), wg_refs[g])
            for r in range(dil):
                _put_class_rows(slab, r, dil, part[r * (tile // dil):(r + 1) * (tile // dil)])
            dh = dh + _from_slabs(slab)
        xhat, rr = _rms_stats(x_ref[...])
        dg_ref[...] += jnp.sum(dh * xhat, axis=0, keepdims=True)
        dx_ref[...] = dx1_ref[...] + _rms_bwd(dh, xhat, rr, g_ref[...])

    row = _rows_spec(tile, D_MODEL)
    group_ins = [(dqkvs[g].reshape(d, t // d, GROUP_COLS), _group_spec(d, tile, GROUP_COLS)) for g, d in enumerate(DILATIONS)]
    w_specs = _gu_w_specs() + [s for g in range(ngroups) for s in _group_w_specs(g)]
    return _token_call(
        "in_proj_bwd", body, t, tile,
        [(dgu, _rows_spec(tile, GU_COLS))] + group_ins + [(w_in, s) for s in w_specs]
        + [(x, row), (dx1, row), (g1, _full((1, D_MODEL)))],
        [((t, D_MODEL), F32, row)], reds=[(1, D_MODEL)], scratch=[_slabs(tile, D_MODEL)])


def _epi_bf16(acc, e, o, r, ids):
    o[0][...] = acc.astype(BF16)


WGRAD_TK = 2048


def _wgrad_2d(name, a, b, tm, tn):
    t, k1 = a.shape
    n = b.shape[1]
    tk = min(t, WGRAD_TK)
    return _mm(name, (k1 // tm, n // tn, t // tk),
               [(a, pl.BlockSpec((tk, tm), lambda i, j, k: (k, i)), b, pl.BlockSpec((tk, tn), lambda i, j, k: (k, j)))],
               TN, (tm, tn), _epi_bf16, outs=[((k1, n), BF16, pl.BlockSpec((tm, tn), lambda i, j, k: (i, j)))])[0]


def _wgrad_in(hs, dgu, dqkvs):
    t = dgu.shape[0]
    tk = min(t, WGRAD_TK)
    gu_block = QKV_BLOCKS * ATTN_W // GU_HALF
    parts = [(hs[0], dgu, GU_HALF, lambda j: j + gu_block)]
    parts += [(hs[g].reshape(t, D_MODEL), dqkvs[g], ATTN_W, lambda j, g=g: _w_in_block(j, g)) for g in range(3)]
    dst = None
    for n, (a, b, tn, block_of) in enumerate(parts):
        dst = _mm(f"wgrad_in_{n}", (1, b.shape[1] // tn, t // tk),
                  [(a, pl.BlockSpec((tk, D_MODEL), lambda i, j, k: (k, 0)), b,
                    pl.BlockSpec((tk, tn), lambda i, j, k: (k, j)))],
                  TN, (D_MODEL, tn), _epi_bf16,
                  extras=[] if dst is None else [(dst, pl.BlockSpec(memory_space=pl.ANY))],
                  outs=[((D_MODEL, IN_COLS), BF16,
                         pl.BlockSpec((D_MODEL, tn), lambda i, j, k, block_of=block_of: (0, block_of(j))))],
                  aliases=None if dst is None else {2: 0})[0]
    return dst


def _wgrad_ff_in(name, h2, da):
    t = h2.shape[0]
    tk = min(t, WGRAD_TK)
    return _mm(name, (N_CHIPS, 1, t // tk),
               [(h2, pl.BlockSpec((tk, D_MODEL), lambda i, j, k: (k, 0)),
                 da, pl.BlockSpec((None, tk, FF_SHARD), lambda i, j, k: (i, k, 0)))],
               TN, (D_MODEL, FF_SHARD), _epi_bf16,
               outs=[((N_CHIPS, D_MODEL, FF_SHARD), BF16, pl.BlockSpec((None, D_MODEL, FF_SHARD), lambda i, j, k: (i, 0, 0)))])[0]


def _wgrad_ff_down(ff, dx2b):
    t = dx2b.shape[0]
    tk = min(t, WGRAD_TK)
    return _mm("wgrad_ffn_down", (N_CHIPS, 1, t // tk),
               [(ff, pl.BlockSpec((None, tk, FF_SHARD), lambda i, j, k: (i, k, 0)),
                 dx2b, pl.BlockSpec((tk, D_MODEL), lambda i, j, k: (k, 0)))],
               TN, (FF_SHARD, D_MODEL), _epi_bf16,
               outs=[((N_CHIPS, FF_SHARD, D_MODEL), BF16, pl.BlockSpec((None, FF_SHARD, D_MODEL), lambda i, j, k: (i, 0, 0)))])[0]


def _local_step(x, pos_col, tgt, g1, ln_g, ln_b, w_s, b_s, g2, gf, first_weight, late_weights, on_grads=None):
    tables = _rope_tables(pos_col)
    bias_exp = jnp.repeat(jnp.transpose(b_s), SGU_W // SGU_GROUPS, axis=1)

    hs = _norm_fwd(x, g1)
    w_p = first_weight(hs[0])
    gu, qkvs = _in_proj(hs, w_p, tables)
    os_, ls_ = [], []
    for g, dil in enumerate(DILATIONS):
        o, lse = _attn_fwd(qkvs[g], g, dil)
        os_.append(o)
        ls_.append(lse)
    attn = _combine_fwd(os_, ls_)
    sgu = _sgu_fwd(gu, ln_g, ln_b, w_s, bias_exp)
    w_pa, w_ps, w_out, w_g, w_u, w_d = late_weights(attn)
    pa, ps, merged, x1, h2 = _merge_fwd(attn, sgu, gu, x, w_pa, w_ps, w_out, g2)
    a, b, ff = _ffn_fwd(h2, w_g, w_u)
    dx2, dx2b, loss, dgf = _ffn_down_loss(ff, w_d, x1, tgt, gf)

    da, db = _ffn_bwd_act(dx2b, w_d, a, b)
    dw_d = _wgrad_ff_down(ff, dx2b)
    dx1, dx1b, dg2 = _ffn_bwd_in(da, db, w_g, w_u, x1, dx2, g2)
    dw_g = _wgrad_ff_in("wgrad_ffn_gate", h2, da)
    dw_u = _wgrad_ff_in("wgrad_ffn_up", h2, db)

    dgu, dpa, dps, dattn, dsgu = _merge_bwd(dx1b, gu, pa, ps, w_pa, w_ps, w_out)
    dw_out = _wgrad_2d("wgrad_out", merged, dx1b, D_MODEL, D_MODEL)
    dw_pa = _wgrad_2d("wgrad_proj_attn", attn, dpa, ATTN_W, D_MODEL)
    dw_ps = _wgrad_2d("wgrad_proj_sgu", sgu, dps, SGU_W, D_MODEL)
    if on_grads is not None:
        ln_g = ln_g + on_grads(1, dict(w_proj_attn=dw_pa, w_proj_sgu=dw_ps, w_out=dw_out, w_ffn_gate=dw_g, w_ffn_up=dw_u,
                                       w_ffn_down=dw_d))[:, :SGU_W]
    dgu, dw_s, dbias, dln_g, dln_b = _sgu_bwd(dgu, gu, dsgu, ln_g, ln_b, w_s, bias_exp)
    dos, ccs = _combine_bwd(dattn, os_, ls_)
    dqkvs = [_attn_bwd(qkvs[g], dos[g], ccs[g], ls_[g], *tables[g], g, dil) for g, dil in enumerate(DILATIONS)]
    dw_p = _wgrad_in(hs, dgu, dqkvs)
    if on_grads is not None:
        g1 = g1 + on_grads(0, dict(w_in=dw_p))
    dx, dg1 = _in_proj_bwd(dgu, dqkvs, w_p, x, dx1, g1)

    db_s = jnp.transpose(dbias[:, ::SGU_W // SGU_GROUPS])
    small = dict(loss=loss, norm1_g=dg1, sgu_ln_g=dln_g, sgu_ln_b=dln_b, w_spatial=dw_s, b_spatial=db_s,
                 norm2_g=dg2, final_g=dgf)
    big = dict(w_in=dw_p, w_proj_attn=dw_pa, w_proj_sgu=dw_ps, w_out=dw_out, w_ffn_gate=dw_g, w_ffn_up=dw_u,
               w_ffn_down=dw_d)
    return dx, big, small


def _ew(name, fn, ins, out_dtypes):
    shp = ins[0].shape
    rows, cols = shp
    tr = next((cand for cand in (256, 352, 128) if rows % cand == 0 and rows > cand), rows)

    def body(*refs):
        res = fn(*[r[...] for r in refs[:len(ins)]])
        for o_ref, v in zip(refs[len(ins):], res):
            o_ref[...] = v.astype(o_ref.dtype)

    spec = pl.BlockSpec((tr, cols), lambda i: (i, 0))
    return pl.pallas_call(
        body, grid=(rows // tr,), in_specs=[spec] * len(ins), out_specs=[spec] * len(out_dtypes),
        out_shape=[jax.ShapeDtypeStruct(shp, d) for d in out_dtypes],
        compiler_params=_cparams(1), name=name)(*ins)


def _adamw_math(g, w, m, v):
    m = ADAM_B1 * m + (1.0 - ADAM_B1) * g
    v = ADAM_B2 * v + (1.0 - ADAM_B2) * (g * g)
    m_hat = m / (1.0 - ADAM_B1 ** ADAM_STEP)
    v_hat = v / (1.0 - ADAM_B2 ** ADAM_STEP)
    delta = -ADAM_LR * (m_hat / (jnp.sqrt(v_hat) + ADAM_EPS) + ADAM_WD * w)
    return delta, m, v


def _adamw(name, g, w, m, v):
    return _ew(name, lambda g_, w_, m_, v_: (g_,) + _adamw_math(g_, w_, m_, v_), [g, w, m, v], [F32] * 4)


VMEM_SPEC = pl.BlockSpec(memory_space=pltpu.VMEM)


def _for_row_chunks(rows, fn):
    ck = next(c for c in (64, 32, 16) if rows % c == 0)

    def step(i, carry):
        fn(pl.multiple_of(i * ck, ck), ck)
        return carry

    lax.fori_loop(0, rows // ck, step, 0)


def _place():
    x, y, c = lax.axis_index("x"), lax.axis_index("y"), lax.axis_index("c")
    chips = [(1 - x, y), (x, 1 - y), (1 - x, 1 - y)]
    return x, y, c, 2 * x + y, chips


def _rows(ref, start, size):
    if len(ref.shape) == 2:
        return ref.at[pl.ds(start, size), :]
    return ref.at[:, pl.ds(start, size), :]


def _comm_call(name, body, ins, out_shapes, scratch, n_remote):
    return pl.pallas_call(
        body, in_specs=[VMEM_SPEC] * len(ins), out_specs=[VMEM_SPEC] * len(out_shapes),
        out_shape=out_shapes,
        scratch_shapes=list(scratch) + [pltpu.SemaphoreType.DMA((n_remote,)), pltpu.SemaphoreType.DMA((n_remote,))],
        compiler_params=pltpu.CompilerParams(vmem_limit_bytes=VMEM_LIMIT), name=name)(*ins)


def _gather_finish(name, shard, landed):
    k_rows, n = shard.shape
    kh = k_rows // 2

    def body(shard_ref, land_ref, out_ref, send, recv):
        x, y, c, me, chips = _place()
        passed = []
        for j, chip in enumerate(chips):
            theirs = 2 * chip[0] + chip[1]
            cp = pltpu.make_async_remote_copy(
                src_ref=land_ref.at[j], dst_ref=_rows(out_ref.at[theirs], c * kh, kh), send_sem=send.at[j],
                recv_sem=recv.at[j], device_id=(x, y, 1 - c), device_id_type=MESH)
            cp.start()
            passed.append(cp)
        mine = out_ref.at[me]

        def put_own(r0, ck):
            mine[pl.ds(r0, ck), :] = shard_ref[pl.ds(r0, ck), :]

        _for_row_chunks(k_rows, put_own)
        for j, chip in enumerate(chips):
            slot = out_ref.at[2 * chip[0] + chip[1]]

            def put_half(r0, ck, j=j, slot=slot):
                slot[pl.ds(pl.multiple_of(c * kh + r0, ck), ck), :] = land_ref[j, pl.ds(r0, ck), :]

            _for_row_chunks(kh, put_half)
        for j, chip in enumerate(chips):
            other = _rows(out_ref.at[2 * chip[0] + chip[1]], (1 - c) * kh, kh)
            pltpu.make_async_remote_copy(src_ref=other, dst_ref=other, send_sem=send.at[j], recv_sem=recv.at[j],
                                         device_id=(x, y, 1 - c), device_id_type=MESH).wait_recv()
        for cp in passed:
            cp.wait_send()

    return _comm_call(name, body, [shard, landed], [jax.ShapeDtypeStruct((N_CHIPS, k_rows, n), shard.dtype)], [], 3)[0]


HBM_SPEC = pl.BlockSpec(memory_space=pltpu.HBM)
SEM_SPEC = pl.BlockSpec(memory_space=pltpu.SEMAPHORE)
DATAFLOW = pltpu.SideEffectType.DATAFLOW_SIDE_EFFECTING
TOKEN_SHAPE = (1, D_MODEL)
N_PEERS = 7


def _peers():
    x, y, c = lax.axis_index("x"), lax.axis_index("y"), lax.axis_index("c")
    flip = lambda v, f: 1 - v if f else v
    return [(flip(x, k & 4), flip(y, k & 2), flip(c, k & 1)) for k in range(1, N_PEERS + 1)]


def _piece_shape(shape):
    return (shape[-2] // 2, shape[2] if len(shape) == 3 else shape[1] // N_CHIPS)


def _device_piece(ref, chip, core):
    kh, n4 = _piece_shape(ref.shape)
    if len(ref.shape) == 3:
        return ref.at[chip, pl.ds(core * kh, kh), :]
    return ref.at[pl.ds(core * kh, kh), pl.ds(chip * n4, n4)]


def _exchange_copies(partials, lands, send, recv):
    return [pltpu.make_async_remote_copy(
        src_ref=_device_piece(partials[t], 2 * px + py, pc), dst_ref=lands[t].at[k], send_sem=send.at[t * N_PEERS + k],
        recv_sem=recv.at[t * N_PEERS + k], device_id=(px, py, pc), device_id_type=MESH)
        for t in range(len(partials)) for k, (px, py, pc) in enumerate(_peers())]


def _gather_copies(shards, lands, send, recv):
    x, y, c, me, chips = _place()
    return [pltpu.make_async_remote_copy(
        src_ref=shards[t], dst_ref=lands[t].at[me], send_sem=send.at[t * 3 + j], recv_sem=recv.at[t * 3 + j],
        device_id=(*chip, c), device_id_type=MESH)
        for t in range(len(shards)) for j, chip in enumerate(chips)]


def _gather_half_copies(shards, lands, send, recv):
    x, y, c, me, chips = _place()
    return [pltpu.make_async_remote_copy(
        src_ref=_rows(shards[t], c * (shards[t].shape[0] // 2), shards[t].shape[0] // 2), dst_ref=lands[t].at[j],
        send_sem=send.at[t * 3 + j], recv_sem=recv.at[t * 3 + j], device_id=(*chip, c), device_id_type=MESH)
        for t in range(len(shards)) for j, chip in enumerate(chips)]


def _split_start(name, copies, per_tensor, srcs, land_shapes):
    nt = len(srcs)
    lands = [lax.empty(s, BF16) for s in land_shapes]
    nsem = nt * per_tensor

    def body(*refs):
        send, recv = refs[2 * nt], refs[2 * nt + 1]
        for cp in copies(refs[:nt], refs[nt:2 * nt], send, recv):
            cp.start()
        refs[-1][...] = jnp.zeros(TOKEN_SHAPE, F32)

    hbm = lambda a: pltpu.with_memory_space_constraint(a, pltpu.HBM)
    outs = pl.pallas_call(
        body, name=name,
        out_shape=[pltpu.SemaphoreType.DMA((nsem,)), pltpu.SemaphoreType.DMA((nsem,))]
        + [pltpu.HBM(s.shape, s.dtype) for s in srcs] + [pltpu.HBM(l.shape, l.dtype) for l in lands]
        + [jax.ShapeDtypeStruct(TOKEN_SHAPE, F32)],
        in_specs=[HBM_SPEC] * (2 * nt), out_specs=[SEM_SPEC, SEM_SPEC] + [HBM_SPEC] * (2 * nt) + [VMEM_SPEC],
        input_output_aliases={i: 2 + i for i in range(2 * nt)},
        compiler_params=pltpu.CompilerParams(has_side_effects=DATAFLOW))(*[hbm(a) for a in list(srcs) + lands])
    return outs[0], outs[1], outs[2:2 + nt], outs[2 + nt:2 + 2 * nt], outs[-1]


def _split_wait(name, copies, send, recv, srcs, lands, after):
    nt = len(srcs)

    def body(*refs):
        for cp in copies(refs[:nt], refs[nt:2 * nt], refs[2 * nt], refs[2 * nt + 1]):
            cp.wait_send()
            cp.wait_recv()

    outs = pl.pallas_call(
        body, name=name,
        out_shape=[pltpu.HBM(s.shape, s.dtype) for s in srcs] + [pltpu.HBM(l.shape, l.dtype) for l in lands],
        in_specs=[HBM_SPEC] * (2 * nt) + [SEM_SPEC, SEM_SPEC, pl.BlockSpec(memory_space=pl.ANY)],
        out_specs=[HBM_SPEC] * (2 * nt), input_output_aliases={i: i for i in range(2 * nt)},
        compiler_params=pltpu.CompilerParams(has_side_effects=DATAFLOW))(*srcs, *lands, send, recv, after)
    return outs[:nt], outs[nt:]


def _device_sum(name, partials, lands):
    nt = len(partials)

    def body(*refs):
        ins, slots, outs, owns = refs[:nt], refs[nt:2 * nt], refs[2 * nt:3 * nt], refs[3 * nt:4 * nt]
        send, recv, loc = refs[4 * nt:]
        x, y, c, me, chips = _place()
        sibling = (x, y, 1 - c)
        loads = [pltpu.make_async_copy(_device_piece(ins[t], me, c), owns[t], loc.at[t]) for t in range(nt)]
        for cp in loads:
            cp.start()
        handed = []
        for t in range(nt):
            kh = owns[t].shape[0]
            loads[t].wait()

            def add(r0, ck, own=owns[t], slot=slots[t], dst=outs[t], kh=kh):
                rows = pl.ds(r0, ck)
                acc = own[rows, :].astype(F32)
                for k in range(N_PEERS):
                    acc = acc + slot[k, rows, :].astype(F32)
                dst[pl.ds(pl.multiple_of(c * kh + r0, ck), ck), :] = acc

            _for_row_chunks(kh, add)
            rc = pltpu.make_async_remote_copy(
                src_ref=_rows(outs[t], c * kh, kh), dst_ref=_rows(outs[t], c * kh, kh), send_sem=send.at[t],
                recv_sem=recv.at[t], device_id=sibling, device_id_type=MESH)
            rc.start()
            handed.append(rc)
        for t in range(nt):
            kh = owns[t].shape[0]
            other = _rows(outs[t], (1 - c) * kh, kh)
            pltpu.make_async_remote_copy(
                src_ref=other, dst_ref=other, send_sem=send.at[t], recv_sem=recv.at[t],
                device_id=sibling, device_id_type=MESH).wait_recv()
        for rc in handed:
            rc.wait_send()

    pieces = [_piece_shape(p.shape) for p in partials]
    return pl.pallas_call(
        body, in_specs=[pl.BlockSpec(memory_space=pl.ANY)] * nt + [VMEM_SPEC] * nt, out_specs=[VMEM_SPEC] * nt,
        out_shape=[jax.ShapeDtypeStruct((2 * kh, n4), F32) for kh, n4 in pieces],
        scratch_shapes=[pltpu.VMEM(p, BF16) for p in pieces]
        + [pltpu.SemaphoreType.DMA((nt,)), pltpu.SemaphoreType.DMA((nt,)), pltpu.SemaphoreType.DMA((nt,))],
        compiler_params=pltpu.CompilerParams(vmem_limit_bytes=VMEM_LIMIT), name=name)(*partials, *lands)


VEC_SHAPE = (8, D_MODEL + LANES)
VEC_SLOTS = dict(norm1_g=(slice(0, 1), slice(0, D_MODEL)), norm2_g=(slice(1, 2), slice(0, D_MODEL)),
                 final_g=(slice(2, 3), slice(0, D_MODEL)), sgu_ln_g=(slice(3, 4), slice(0, SGU_W)),
                 sgu_ln_b=(slice(3, 4), slice(SGU_W, 2 * SGU_W)), b_spatial=(slice(0, 8), slice(D_MODEL, D_MODEL + LANES)),
                 loss=(slice(4, 5), slice(0, LANES)))
VEC_PARAMS = ("norm1_g", "norm2_g", "final_g", "sgu_ln_g", "sgu_ln_b", "b_spatial")
SMALL_PARAMS = VEC_PARAMS + ("w_spatial",)
W_SPATIAL_2D = (SGU_GROUPS * SGU_CHUNK, SGU_CHUNK)


def _small_step(partials, w, m, v):
    def shape2d(name):
        if name == "w_spatial":
            return W_SPATIAL_2D
        rows, cols = VEC_SLOTS[name]
        return (rows.stop - rows.start, cols.stop - cols.start)

    g_names = VEC_PARAMS + ("loss", "w_spatial")
    ng, npar = len(g_names), len(SMALL_PARAMS)

    def pack(dst, parts):
        dst[...] = jnp.zeros(VEC_SHAPE, F32)
        for n, ref in parts.items():
            if n in VEC_SLOTS:
                dst[VEC_SLOTS[n]] = ref[...]

    def reduce_body(*refs):
        g_in = dict(zip(g_names, refs[:ng]))
        vec_out, ws_out, vec, vec_pair, vec_slot, ws_pair, ws_slot, send, recv = refs[ng:]
        x, y, c, me, chips = _place()
        sibling = (x, y, 1 - c)
        pack(vec, g_in)
        copies = []

        def allreduce(k0, src, pair, slot):
            first = pltpu.make_async_remote_copy(src_ref=src, dst_ref=pair, send_sem=send.at[k0], recv_sem=recv.at[k0],
                                                 device_id=sibling, device_id_type=MESH)
            first.start()
            first.wait_recv()
            slot[me] = src[...] + pair[...]
            arrivals = []
            for j, chip in enumerate(chips):
                theirs = 2 * chip[0] + chip[1]
                rc = pltpu.make_async_remote_copy(src_ref=slot.at[me], dst_ref=slot.at[me], send_sem=send.at[k0 + 1 + j],
                                                  recv_sem=recv.at[k0 + 1 + j], device_id=(*chip, c), device_id_type=MESH)
                rc.start()
                arrivals.append(pltpu.make_async_remote_copy(
                    src_ref=slot.at[theirs], dst_ref=slot.at[theirs], send_sem=send.at[k0 + 1 + j],
                    recv_sem=recv.at[k0 + 1 + j], device_id=(*chip, c), device_id_type=MESH))
                copies.append(rc)
            copies.append(first)
            return arrivals

        arrivals = allreduce(0, vec, vec_pair, vec_slot) + allreduce(4, g_in["w_spatial"], ws_pair, ws_slot)
        for a in arrivals:
            a.wait_recv()
        vec_out[...] = ((vec_slot[0] + vec_slot[1]) + vec_slot[2]) + vec_slot[3]

        def spatial(r0, ck):
            rows = pl.ds(r0, ck)
            ws_out[rows, :] = ((ws_slot[0, rows, :] + ws_slot[1, rows, :]) + ws_slot[2, rows, :]) + ws_slot[3, rows, :]

        _for_row_chunks(W_SPATIAL_2D[0], spatial)
        for rc in copies:
            rc.wait_send()

    g_vec, g_ws = pl.pallas_call(
        reduce_body, in_specs=[VMEM_SPEC] * ng, out_specs=[VMEM_SPEC] * 2,
        out_shape=[jax.ShapeDtypeStruct(VEC_SHAPE, F32), jax.ShapeDtypeStruct(W_SPATIAL_2D, F32)],
        scratch_shapes=[pltpu.VMEM(VEC_SHAPE, F32), pltpu.VMEM(VEC_SHAPE, F32), pltpu.VMEM((N_CHIPS,) + VEC_SHAPE, F32),
                        pltpu.VMEM(W_SPATIAL_2D, F32), pltpu.VMEM((N_CHIPS,) + W_SPATIAL_2D, F32),
                        pltpu.SemaphoreType.DMA((8,)), pltpu.SemaphoreType.DMA((8,))],
        name="small_params_allreduce")(*[partials[n].reshape(shape2d(n)) for n in g_names])

    def update_body(*refs):
        gv_ref, gw_ref = refs[:2]
        w_in, m_in, v_in = (dict(zip(SMALL_PARAMS, refs[2 + k * npar:2 + (k + 1) * npar])) for k in range(3))
        o0 = 2 + 3 * npar
        g_out = dict(zip(g_names, refs[o0:o0 + ng]))
        d_out, m_out, v_out = (dict(zip(SMALL_PARAMS, refs[o0 + ng + k * npar:o0 + ng + (k + 1) * npar])) for k in range(3))
        vw, vm, vv = refs[o0 + ng + 3 * npar:]
        pack(vw, w_in)
        pack(vm, m_in)
        pack(vv, v_in)
        d_vec, m_vec, v_vec = _adamw_math(gv_ref[...], vw[...], vm[...], vv[...])
        vw[...] = d_vec
        vm[...] = m_vec
        vv[...] = v_vec
        for n in VEC_PARAMS + ("loss",):
            g_out[n][...] = gv_ref[VEC_SLOTS[n]]
        for n in VEC_PARAMS:
            d_out[n][...] = vw[VEC_SLOTS[n]]
            m_out[n][...] = vm[VEC_SLOTS[n]]
            v_out[n][...] = vv[VEC_SLOTS[n]]

        def spatial(r0, ck):
            rows = pl.ds(r0, ck)
            g = gw_ref[rows, :]
            d_, m_, v_ = _adamw_math(g, w_in["w_spatial"][rows, :], m_in["w_spatial"][rows, :], v_in["w_spatial"][rows, :])
            g_out["w_spatial"][rows, :] = g
            d_out["w_spatial"][rows, :] = d_
            m_out["w_spatial"][rows, :] = m_
            v_out["w_spatial"][rows, :] = v_

        _for_row_chunks(W_SPATIAL_2D[0], spatial)

    ins = [g_vec, g_ws]
    for src in (w, m, v):
        ins += [src[n].reshape(shape2d(n)) for n in SMALL_PARAMS]
    out_shapes = [jax.ShapeDtypeStruct(shape2d(n), F32) for n in g_names + SMALL_PARAMS * 3]
    outs = pl.pallas_call(
        update_body, in_specs=[VMEM_SPEC] * len(ins), out_specs=[VMEM_SPEC] * len(out_shapes), out_shape=out_shapes,
        scratch_shapes=[pltpu.VMEM(VEC_SHAPE, F32)] * 3, name="small_params_update")(*ins)
    grads = dict(zip(g_names, outs[:ng]))
    rest = [dict(zip(SMALL_PARAMS, outs[ng + k * npar:ng + (k + 1) * npar])) for k in range(3)]
    return grads, rest[0], rest[1], rest[2]


BIG = ("w_in", "w_proj_attn", "w_proj_sgu", "w_out", "w_ffn_gate", "w_ffn_up", "w_ffn_down")
COMM_GROUPS = (("w_in",), ("w_proj_attn", "w_proj_sgu", "w_out", "w_ffn_gate", "w_ffn_up", "w_ffn_down"))
WEIGHTS = ("norm1_g", "w_in", "sgu_ln_g", "sgu_ln_b", "w_spatial", "b_spatial", "w_proj_attn", "w_proj_sgu", "w_out",
           "norm2_g", "w_ffn_gate", "w_ffn_up", "w_ffn_down", "final_g")


def _cols_from_chips(g):
    return jnp.transpose(g, (1, 0, 2)).reshape(g.shape[1], N_CHIPS * g.shape[2])


def kernel(x, positions, norm1_g, w_in, sgu_ln_g, sgu_ln_b, w_spatial, b_spatial, w_proj_attn, w_proj_sgu, w_out, norm2_g, w_ffn_gate, w_ffn_up, w_ffn_down, final_g, loss_target, m_norm1_g, m_w_in, m_sgu_ln_g, m_sgu_ln_b, m_w_spatial, m_b_spatial, m_w_proj_attn, m_w_proj_sgu, m_w_out, m_norm2_g, m_w_ffn_gate, m_w_ffn_up, m_w_ffn_down, m_final_g, v_norm1_g, v_w_in, v_sgu_ln_g, v_sgu_ln_b, v_w_spatial, v_b_spatial, v_w_proj_attn, v_w_proj_sgu, v_w_out, v_norm2_g, v_w_ffn_gate, v_w_ffn_up, v_w_ffn_down, v_final_g):
    w = dict(norm1_g=norm1_g, w_in=w_in, sgu_ln_g=sgu_ln_g, sgu_ln_b=sgu_ln_b, w_spatial=w_spatial, b_spatial=b_spatial,
             w_proj_attn=w_proj_attn, w_proj_sgu=w_proj_sgu, w_out=w_out, norm2_g=norm2_g, w_ffn_gate=w_ffn_gate,
             w_ffn_up=w_ffn_up, w_ffn_down=w_ffn_down, final_g=final_g)
    m = dict(norm1_g=m_norm1_g, w_in=m_w_in, sgu_ln_g=m_sgu_ln_g, sgu_ln_b=m_sgu_ln_b, w_spatial=m_w_spatial,
             b_spatial=m_b_spatial, w_proj_attn=m_w_proj_attn, w_proj_sgu=m_w_proj_sgu, w_out=m_w_out, norm2_g=m_norm2_g,
             w_ffn_gate=m_w_ffn_gate, w_ffn_up=m_w_ffn_up, w_ffn_down=m_w_ffn_down, final_g=m_final_g)
    v = dict(norm1_g=v_norm1_g, w_in=v_w_in, sgu_ln_g=v_sgu_ln_g, sgu_ln_b=v_sgu_ln_b, w_spatial=v_w_spatial,
             b_spatial=v_b_spatial, w_proj_attn=v_w_proj_attn, w_proj_sgu=v_w_proj_sgu, w_out=v_w_out, norm2_g=v_norm2_g,
             w_ffn_gate=v_w_ffn_gate, w_ffn_up=v_w_ffn_up, w_ffn_down=v_w_ffn_down, final_g=v_final_g)
    t = x.shape[1]

    shards = {n: _ew(f"cast_{n}", lambda a: (a,), [w[n][0]], [BF16])[0] for n in BIG}
    late = COMM_GROUPS[1]
    k_in, n_in = shards["w_in"].shape
    *first, token = _split_start("gather_start_0", _gather_half_copies, 3, [shards["w_in"]], [(3, k_in // 2, n_in)])
    pending = {}

    def first_weight(after):
        srcs, filled = _split_wait("gather_wait_0", _gather_half_copies, *first, after)
        gath_in, late_shards = lax.optimization_barrier(
            (_gather_finish("gather_finish_0", srcs[0], filled[0]), [shards[n] for n in late]))
        *pending["late"], _ = _split_start(
            "gather_start_1", _gather_copies, 3, late_shards, [(N_CHIPS,) + s.shape for s in late_shards])
        return _cols_from_chips(gath_in)

    def late_weights(after):
        srcs, filled = _split_wait("gather_wait_1", _gather_copies, *pending["late"], after)
        me = 2 * lax.axis_index("x") + lax.axis_index("y")
        gath = {n: lax.dynamic_update_slice(f, s[None], (me, 0, 0)) for n, f, s in zip(late, filled, srcs)}
        return (_cols_from_chips(gath["w_proj_attn"]), _cols_from_chips(gath["w_proj_sgu"]),
                gath["w_out"].reshape(D_MODEL, D_MODEL), gath["w_ffn_gate"], gath["w_ffn_up"], gath["w_ffn_down"])

    exchanges = {}

    def on_grads(i, partials):
        if "w_out" in partials:
            partials["w_out"] = partials["w_out"].reshape(N_CHIPS, D_MODEL // N_CHIPS, D_MODEL)
        parts = [partials[n] for n in COMM_GROUPS[i]]
        *exchanges[i], started = _split_start(
            f"rs_exchange_start_{i}", _exchange_copies, N_PEERS, parts, [(N_PEERS,) + _piece_shape(p.shape) for p in parts])
        return started

    dx, _, small = _local_step(
        x[0], positions.reshape(t, 1), loss_target[0], norm1_g + token, sgu_ln_g, sgu_ln_b, w_spatial[0], b_spatial[0],
        norm2_g, final_g.reshape(1, D_MODEL), first_weight, late_weights, on_grads=on_grads)

    grads = {}
    for i in (1, 0):
        parts, filled = _split_wait(f"rs_exchange_wait_{i}", _exchange_copies, *exchanges[i], dx)
        grads.update(zip(COMM_GROUPS[i], _device_sum(f"rs_device_sum_{i}", parts, filled)))

    delta, new_m, new_v = {}, {}, {}
    for n in BIG:
        shp = w[n].shape
        flip = jnp.transpose if shp[-1] % LANES else (lambda a: a)
        outs = _adamw(f"adamw_{n}", flip(grads[n]), flip(w[n][0]), flip(m[n][0]), flip(v[n][0]))
        grads[n], delta[n], new_m[n], new_v[n] = (flip(a).reshape(shp) for a in outs)

    g_s, d_s, m_s, v_s = _small_step(small, w, m, v)
    loss = g_s["loss"][0, 0]
    for n in SMALL_PARAMS:
        shp = w[n].shape
        grads[n], delta[n], new_m[n], new_v[n] = (a[n].reshape(shp) for a in (g_s, d_s, m_s, v_s))

    return (loss, dx.reshape(x.shape), *[grads[n] for n in WEIGHTS], *[delta[n] for n in WEIGHTS],
            *[new_m[n] for n in WEIGHTS], *[new_v[n] for n in WEIGHTS])
```

```python
import functools

import numpy as np
import jax
import jax.numpy as jnp
from jax import lax
from jax.experimental import pallas as pl
from jax.experimental.pallas import tpu as pltpu

F32, BF16 = jnp.float32, jnp.bfloat16
MESH = pl.DeviceIdType.MESH

D_MODEL = 1024
HEAD_DIM = 64
ATTN_W = 512
DILATIONS = (1, 4, 16)
BLK = 128
ATTN_BLOCKS_PER_STEP = 4
ROPE_DIM = 16
ROPE_THETA = 500000.0
SGU_W = 512
SGU_CHUNK = 128
SGU_GROUPS = 8
D_FF = 2816
N_CHIPS = 4
FF_SHARD = D_FF // N_CHIPS
IN_COLS = 7680
EPS = 1e-6
NEG = -1e30
LANES = 128
VMEM_LIMIT = 52 * 1024 * 1024

ADAM_LR, ADAM_B1, ADAM_B2, ADAM_EPS, ADAM_WD, ADAM_STEP = 0.001, 0.9, 0.999, 1e-08, 0.01, 10

QKV_BLOCKS = 9


def _w_in_block(part, g):
    return part * len(DILATIONS) + g


def _cparams(ngrid):
    return pltpu.CompilerParams(dimension_semantics=("arbitrary",) * ngrid, vmem_limit_bytes=VMEM_LIMIT)


def _full(shape):
    return pl.BlockSpec(shape, lambda *_: (0,) * len(shape))


def _resident(shape):
    return pl.BlockSpec(shape, lambda *_: (0,) * len(shape), pipeline_mode=pl.Buffered(1))


NT = ((1,), (1,))
TN = ((0,), (0,))


def _rope(v, cos_t, sin_t):
    half = ROPE_DIM // 2
    first = (lax.broadcasted_iota(jnp.int32, cos_t.shape, 1) % HEAD_DIM) < half
    outs = []
    for cs in range(v.shape[1] // LANES):
        x = v[:, cs * LANES:(cs + 1) * LANES]
        partner = jnp.where(first, pltpu.roll(x, LANES - half, axis=1), pltpu.roll(x, half, axis=1))
        outs.append(x * cos_t + partner * sin_t)
    return outs[0] if len(outs) == 1 else jnp.concatenate(outs, axis=1)


def _spread_heads(v2, upper):
    other = pltpu.roll(v2, HEAD_DIM, axis=1)
    h0 = jnp.where(upper, other, v2)
    h1 = jnp.where(upper, v2, other)
    return jnp.concatenate([jnp.concatenate([h0, h0], axis=1), jnp.concatenate([h1, h1], axis=1)], axis=0)


def _sigmoid(v):
    return 0.5 * jnp.tanh(0.5 * v) + 0.5


def _rms_stats(v):
    r = lax.rsqrt(jnp.mean(v * v, axis=-1, keepdims=True) + EPS)
    return v * r, r


def _rms_bwd(dy, xhat, r, g):
    dxh = dy * g
    return r * (dxh - xhat * jnp.mean(dxh * xhat, axis=-1, keepdims=True))


def _head_sum_matrix():
    idx = np.arange(ATTN_W) // HEAD_DIM
    return jnp.asarray((idx[:, None] == idx[None, :]).astype(np.float32), dtype=BF16)


def _group_sum(v, e):
    hi = v.astype(BF16)
    lo = (v - hi.astype(F32)).astype(BF16)
    return jnp.dot(hi, e, preferred_element_type=F32) + jnp.dot(lo, e, preferred_element_type=F32)


TILE = 512


def _to_slabs(slab_ref, v):
    for cs in range(slab_ref.shape[0]):
        slab_ref[cs] = v[:, cs * LANES:(cs + 1) * LANES]


def _from_slabs(slab_ref):
    return jnp.concatenate([slab_ref[cs] for cs in range(slab_ref.shape[0])], axis=1)


def _class_rows(slab_ref, r, dil):
    n = slab_ref.shape[1] // dil
    return jnp.concatenate([slab_ref.at[cs][pl.ds(r, n, stride=dil), :] for cs in range(slab_ref.shape[0])], axis=1)


def _put_class_rows(slab_ref, r, dil, v):
    n = slab_ref.shape[1] // dil
    for cs in range(slab_ref.shape[0]):
        slab_ref.at[cs][pl.ds(r, n, stride=dil), :] = v[:, cs * LANES:(cs + 1) * LANES]


def _natural_from_group(slab_ref, grp_ref):
    dil = grp_ref.shape[0]
    for r in range(dil):
        _put_class_rows(slab_ref, r, dil, grp_ref[r].astype(F32))
    return _from_slabs(slab_ref)


def _group_from_natural(slab_ref, grp_ref, v):
    dil = grp_ref.shape[0]
    _to_slabs(slab_ref, v)
    for r in range(dil):
        grp_ref[r] = _class_rows(slab_ref, r, dil).astype(grp_ref.dtype)


def _group_spec(dil, tile, width):
    return pl.BlockSpec((dil, tile // dil, width), lambda i, *_: (0, i, 0))


def _slabs(tile, width):
    return pltpu.VMEM((width // LANES, tile, LANES), F32)


def _rope_consts():
    lane = np.arange(LANES) % HEAD_DIM
    fi = lane % (ROPE_DIM // 2)
    invf = np.where(lane < ROPE_DIM, ROPE_THETA ** (-(2.0 * fi) / ROPE_DIM), 0.0)
    sgn = np.where(lane < ROPE_DIM // 2, -1.0, np.where(lane < ROPE_DIM, 1.0, 0.0))
    return (jnp.asarray(invf.astype(np.float32)).reshape(1, LANES), jnp.asarray(sgn.astype(np.float32)).reshape(1, LANES))


def _rope_tables(pos_col):
    t = pos_col.shape[0]
    tile = min(t, TILE)
    invf, sgn = _rope_consts()

    def body(p_ref, f_ref, s_ref, c0, s0, c1, s1, c2, s2, slab_c, slab_s):
        ang = p_ref[...].astype(F32) * f_ref[...]
        cos, sin = jnp.cos(ang), jnp.sin(ang) * s_ref[...]
        c0[...] = cos
        s0[...] = sin
        _group_from_natural(slab_c, c1, cos)
        _group_from_natural(slab_s, s1, sin)
        for r in range(DILATIONS[2]):
            c2[r] = _class_rows(slab_c, r, DILATIONS[2])
            s2[r] = _class_rows(slab_s, r, DILATIONS[2])

    nat = pl.BlockSpec((tile, LANES), lambda i: (i, 0))
    specs, shapes = [nat, nat], [(t, LANES)] * 2
    for d in DILATIONS[1:]:
        specs += [_group_spec(d, tile, LANES)] * 2
        shapes += [(d, t // d, LANES)] * 2
    outs = pl.pallas_call(
        body, grid=(t // tile,),
        in_specs=[pl.BlockSpec((tile, 1), lambda i: (i, 0)), _full((1, LANES)), _full((1, LANES))],
        out_specs=specs, out_shape=[jax.ShapeDtypeStruct(s, F32) for s in shapes],
        scratch_shapes=[_slabs(tile, LANES)] * 2,
        compiler_params=_cparams(1), name="rope_tables")(pos_col, invf, sgn)
    return [(outs[2 * g].reshape(t, LANES), outs[2 * g + 1].reshape(t, LANES)) for g in range(len(DILATIONS))]


def _norm_fwd(x, g):
    t = x.shape[0]
    tile = min(t, TILE)

    def body(x_ref, g_ref, h0_ref, h1_ref, h2_ref, slab):
        xhat, _ = _rms_stats(x_ref[...])
        hn = xhat * g_ref[...]
        h0_ref[...] = hn.astype(BF16)
        _group_from_natural(slab, h1_ref, hn)
        for r in range(DILATIONS[2]):
            h2_ref[r] = _class_rows(slab, r, DILATIONS[2]).astype(BF16)

    nat = pl.BlockSpec((tile, D_MODEL), lambda i: (i, 0))
    return pl.pallas_call(
        body, grid=(t // tile,),
        in_specs=[nat, _full((1, D_MODEL))],
        out_specs=[nat] + [_group_spec(d, tile, D_MODEL) for d in DILATIONS[1:]],
        out_shape=[jax.ShapeDtypeStruct((t, D_MODEL), BF16)]
        + [jax.ShapeDtypeStruct((d, t // d, D_MODEL), BF16) for d in DILATIONS[1:]],
        scratch_shapes=[_slabs(tile, D_MODEL)],
        compiler_params=_cparams(1), name="norm1_fwd")(x, g)


GU_COLS = 3072
GROUP_COLS = 1536
GU_HALF = GU_COLS // 2


def _w_in_spec(width, block):
    return pl.BlockSpec((D_MODEL, width), lambda i: (0, block), pipeline_mode=pl.Buffered(1))


def _gu_w_specs():
    first = QKV_BLOCKS * ATTN_W // GU_HALF
    return [_w_in_spec(GU_HALF, first), _w_in_spec(GU_HALF, first + 1)]


def _group_w_specs(g):
    return [_w_in_spec(ATTN_W, _w_in_block(part, g)) for part in range(3)]


def _in_proj(hs, w_in, tables):
    t = hs[0].shape[0]
    tm = min(t, 1024)

    def body_gu(h_ref, w0_ref, w1_ref, o_ref):
        h = h_ref[...]
        o_ref[:, 0:GU_HALF] = jnp.dot(h, w0_ref[...], preferred_element_type=F32).astype(BF16)
        o_ref[:, GU_HALF:] = jnp.dot(h, w1_ref[...], preferred_element_type=F32).astype(BF16)

    gu = _token_call("in_proj_gates_uv", body_gu, t, tm,
                     [(hs[0], _rows_spec(tm, D_MODEL))] + [(w_in, s) for s in _gu_w_specs()],
                     [((t, GU_COLS), BF16, _rows_spec(tm, GU_COLS))])[0]

    qkvs = []
    for g in range(len(DILATIONS)):

        def body_qkv(h_ref, wq_ref, wk_ref, wv_ref, cos_ref, sin_ref, o_ref):
            h = h_ref[...]
            cos_w, sin_w = cos_ref[...], sin_ref[...]
            q = jnp.dot(h, wq_ref[...], preferred_element_type=F32)
            o_ref[:, 0:ATTN_W] = (_rope(q, cos_w, sin_w) * HEAD_DIM ** -0.5).astype(BF16)
            k = jnp.dot(h, wk_ref[...], preferred_element_type=F32)
            o_ref[:, ATTN_W:2 * ATTN_W] = _rope(k, cos_w, sin_w).astype(BF16)
            o_ref[:, 2 * ATTN_W:] = jnp.dot(h, wv_ref[...], preferred_element_type=F32).astype(BF16)

        cos_t, sin_t = tables[g]
        qkvs.append(_token_call(
            f"in_proj_qkv_g{g}", body_qkv, t, tm,
            [(hs[g].reshape(t, D_MODEL), _rows_spec(tm, D_MODEL))] + [(w_in, s) for s in _group_w_specs(g)]
            + [(cos_t, _rows_spec(tm, LANES)), (sin_t, _rows_spec(tm, LANES))],
            [((t, GROUP_COLS), BF16, _rows_spec(tm, GROUP_COLS))])[0])
    return gu, qkvs


def _attn_masks(n):
    row = lax.broadcasted_iota(jnp.int32, (2 * BLK, 2 * BLK), 0) % BLK
    col = lax.broadcasted_iota(jnp.int32, (2 * BLK, 2 * BLK), 1)
    diff = BLK + row - col
    valid = (diff >= 0) & (diff <= BLK) & ((col >= BLK) | (n > 0))
    upper = lax.broadcasted_iota(jnp.int32, (BLK, LANES), 1) >= HEAD_DIM
    return valid, upper


def _stack_heads(v2, upper):
    zero = jnp.zeros_like(v2)
    return jnp.concatenate([jnp.where(upper, zero, v2), jnp.where(upper, v2, zero)], axis=0)


def _unstack_heads(v, upper):
    return jnp.where(upper, v[BLK:], v[:BLK])


def _attn_fwd(qkv, g, dil):
    t = qkv.shape[0]
    length = t // dil
    nb = length // BLK
    per_step = min(nb, ATTN_BLOCKS_PER_STEP)
    view = qkv.reshape(dil, length, GROUP_COLS)

    def body(q_ref, kc_ref, kp_ref, vc_ref, vp_ref, o_ref, l_ref, kwin, vwin):
        n = pl.program_id(1)
        kwin[0:BLK] = kp_ref[...]
        kwin[BLK:] = kc_ref[...]
        vwin[0:BLK] = vp_ref[...]
        vwin[BLK:] = vc_ref[...]

        def block(b, carry):
            valid, upper = _attn_masks(n * per_step + b)
            rows = pl.ds(pl.multiple_of(b * BLK, BLK), BLK)
            window = pl.ds(pl.multiple_of(b * BLK, BLK), 2 * BLK)
            for p in range(ATTN_W // LANES):
                sl = slice(p * LANES, (p + 1) * LANES)
                qs = _stack_heads(q_ref[rows, sl], upper)
                s = lax.dot_general(qs, kwin[window, sl], (NT, ((), ())), preferred_element_type=F32)
                s = jnp.where(valid, s, NEG)
                m = jnp.max(s, axis=1, keepdims=True)
                pe = jnp.exp(s - m)
                den = jnp.sum(pe, axis=1, keepdims=True)
                o = jnp.dot(pe.astype(BF16), vwin[window, sl], preferred_element_type=F32) / den
                lse = jnp.broadcast_to(m + jnp.log(den), (2 * BLK, LANES))
                o_ref[rows, sl] = _unstack_heads(o, upper).astype(BF16)
                l_ref[rows, sl] = _unstack_heads(lse, upper)
            return carry

        lax.fori_loop(0, per_step, block, 0)

    rows = per_step * BLK
    cur = lambda part: pl.BlockSpec((None, rows, ATTN_W), lambda r, n: (r, n, part))
    prev = lambda part: pl.BlockSpec((None, BLK, ATTN_W), lambda r, n: (r, jnp.maximum(n * per_step - 1, 0), part))
    out_spec = pl.BlockSpec((None, rows, ATTN_W), lambda r, n: (r, n, 0))
    return pl.pallas_call(
        body, grid=(dil, nb // per_step),
        in_specs=[cur(0), cur(1), prev(1), cur(2), prev(2)],
        out_specs=[out_spec, out_spec],
        out_shape=[jax.ShapeDtypeStruct((dil, length, ATTN_W), BF16), jax.ShapeDtypeStruct((dil, length, ATTN_W), F32)],
        scratch_shapes=[pltpu.VMEM((rows + BLK, ATTN_W), BF16)] * 2,
        compiler_params=_cparams(2), name=f"attn_fwd_g{g}")(view, view, view, view, view)


def _alphas(l0, l1, l2):
    m = jnp.maximum(jnp.maximum(l0, l1), l2)
    e0, e1, e2 = jnp.exp(l0 - m), jnp.exp(l1 - m), jnp.exp(l2 - m)
    inv = 1.0 / (e0 + e1 + e2)
    return e0 * inv, e1 * inv, e2 * inv


def _natural_group_values(o_refs, l_refs, slabs):
    os_ = [o_refs[0][0].astype(F32)] + [_natural_from_group(slabs[2 * g - 2], o_refs[g]) for g in (1, 2)]
    ls_ = [l_refs[0][0]] + [_natural_from_group(slabs[2 * g - 1], l_refs[g]) for g in (1, 2)]
    return os_, ls_


def _combine_fwd(os_, ls_):
    t = os_[0].shape[1]
    tile = min(t, TILE)

    def body(o0, o1, o2, l0, l1, l2, a_ref, *slabs):
        ov, lv = _natural_group_values((o0, o1, o2), (l0, l1, l2), slabs)
        a0, a1, a2 = _alphas(*lv)
        a_ref[...] = (a0 * ov[0] + a1 * ov[1] + a2 * ov[2]).astype(BF16)

    specs = [_group_spec(d, tile, ATTN_W) for d in DILATIONS]
    return pl.pallas_call(
        body, grid=(t // tile,), in_specs=specs * 2, out_specs=pl.BlockSpec((tile, ATTN_W), lambda i: (i, 0)),
        out_shape=jax.ShapeDtypeStruct((t, ATTN_W), BF16),
        scratch_shapes=[_slabs(tile, ATTN_W)] * 4,
        compiler_params=_cparams(1), name="combine_fwd")(*os_, *ls_)


def _combine_bwd(dattn, os_, ls_):
    t = dattn.shape[0]
    tile = min(t, TILE)
    e = _head_sum_matrix()

    def body(d_ref, o0, o1, o2, l0, l1, l2, e_ref, do0, do1, do2, c0, c1, c2, *slabs):
        ov, lv = _natural_group_values((o0, o1, o2), (l0, l1, l2), slabs)
        alphas = _alphas(*lv)
        d = d_ref[...]
        attn = alphas[0] * ov[0] + alphas[1] * ov[1] + alphas[2] * ov[2]
        s = _group_sum(d * attn, e_ref[...])
        do0[0] = (alphas[0] * d).astype(BF16)
        c0[0] = -alphas[0] * s
        for g, do_ref, c_ref in ((1, do1, c1), (2, do2, c2)):
            _group_from_natural(slabs[2 * g - 2], do_ref, alphas[g] * d)
            _group_from_natural(slabs[2 * g - 1], c_ref, -alphas[g] * s)

    specs = [_group_spec(d, tile, ATTN_W) for d in DILATIONS]
    shapes = [(d, t // d, ATTN_W) for d in DILATIONS]
    outs = pl.pallas_call(
        body, grid=(t // tile,),
        in_specs=[pl.BlockSpec((tile, ATTN_W), lambda i: (i, 0))] + specs * 2 + [_full((ATTN_W, ATTN_W))],
        out_specs=specs * 2,
        out_shape=[jax.ShapeDtypeStruct(s, BF16) for s in shapes] + [jax.ShapeDtypeStruct(s, F32) for s in shapes],
        scratch_shapes=[_slabs(tile, ATTN_W)] * 4,
        compiler_params=_cparams(1), name="combine_bwd")(dattn, *os_, *ls_, e)
    return outs[:3], outs[3:]


def _attn_bwd(qkv, do, cc, lse, cos_t, sin_t, g, dil):
    t = qkv.shape[0]
    length = t // dil
    nb = length // BLK
    per_step = min(nb, ATTN_BLOCKS_PER_STEP)
    nsteps = nb // per_step
    rows_per_step = per_step * BLK
    qkv_v = qkv.reshape(dil, length, GROUP_COLS)
    cos_v, sin_v = (a.reshape(dil, length, LANES) for a in (cos_t, sin_t))
    scale = HEAD_DIM ** -0.5
    dq_cols, dk_cols, dv_cols = (slice(i * ATTN_W, (i + 1) * ATTN_W) for i in range(3))

    def body(q_ref, kc_ref, kp_ref, vc_ref, vp_ref, do_ref, c_ref, l_ref, cosc, sinc, cosp, sinp,
             out_ref, acc, kwin, vwin, cwin, swin):
        n = pl.program_id(1)

        def one_block(b):
            valid, upper = _attn_masks(n * per_step + b)
            start = b * BLK if isinstance(b, int) else pl.multiple_of(b * BLK, BLK)
            rows, before, window = pl.ds(start, BLK), pl.ds(start, BLK), pl.ds(start, 2 * BLK)
            own = pl.ds(start + BLK, BLK)
            dq_parts, dkp_parts, dkc_parts, dvp_parts, dvc_parts = [], [], [], [], []
            for p in range(ATTN_W // LANES):
                sl = slice(p * LANES, (p + 1) * LANES)
                qs = _stack_heads(q_ref[rows, sl], upper)
                dos = _stack_heads(do_ref[rows, sl], upper)
                k2 = kwin[window, sl]
                l_col = _spread_heads(l_ref[rows, sl], upper)
                c_col = _spread_heads(c_ref[rows, sl], upper)
                s = lax.dot_general(qs, k2, (NT, ((), ())), preferred_element_type=F32)
                pe = jnp.exp(jnp.where(valid, s, NEG) - l_col)
                dpv = lax.dot_general(dos, vwin[window, sl], (NT, ((), ())), preferred_element_type=F32)
                ds = (pe * (dpv + c_col)).astype(BF16)
                dq2 = _unstack_heads(jnp.dot(ds, k2, preferred_element_type=F32), upper)
                dk2 = lax.dot_general(ds, qs, (TN, ((), ())), preferred_element_type=F32)
                dv2 = lax.dot_general(pe.astype(BF16), dos, (TN, ((), ())), preferred_element_type=F32)
                dq_parts.append(dq2)
                dkp_parts.append(dk2[:BLK])
                dkc_parts.append(dk2[BLK:])
                dvp_parts.append(dv2[:BLK])
                dvc_parts.append(dv2[BLK:])
            dq = _rope(jnp.concatenate(dq_parts, axis=1) * scale, cwin[own, :], swin[own, :])
            dkc = _rope(jnp.concatenate(dkc_parts, axis=1), cwin[own, :], swin[own, :])
            dkp = _rope(jnp.concatenate(dkp_parts, axis=1), cwin[before, :], swin[before, :])
            return dq, dkp, dkc, jnp.concatenate(dvp_parts, axis=1), jnp.concatenate(dvc_parts, axis=1)

        @pl.when(n < nsteps)
        def _():
            kwin[0:BLK] = kp_ref[...]
            kwin[BLK:] = kc_ref[...]
            vwin[0:BLK] = vp_ref[...]
            vwin[BLK:] = vc_ref[...]
            cwin[0:BLK] = cosp[...]
            cwin[BLK:] = cosc[...]
            swin[0:BLK] = -sinp[...]
            swin[BLK:] = -sinc[...]
            dq, dkp, dkc, dvp, dvc = one_block(0)
            last = slice(rows_per_step - BLK, rows_per_step)

            @pl.when(n > 0)
            def _():
                if per_step > 1:
                    out_ref[0:rows_per_step - BLK, :] = acc[0:rows_per_step - BLK, :].astype(BF16)
                out_ref[last, dq_cols] = acc[last, dq_cols].astype(BF16)
                out_ref[last, dk_cols] = (acc[last, dk_cols] + dkp).astype(BF16)
                out_ref[last, dv_cols] = (acc[last, dv_cols] + dvp).astype(BF16)

            acc[0:BLK, dq_cols] = dq
            acc[0:BLK, dk_cols] = dkc
            acc[0:BLK, dv_cols] = dvc

            def later(b, carry):
                dq, dkp, dkc, dvp, dvc = one_block(b)
                start = pl.multiple_of(b * BLK, BLK)
                before, rows = pl.ds(start - BLK, BLK), pl.ds(start, BLK)
                acc[before, dk_cols] += dkp
                acc[before, dv_cols] += dvp
                acc[rows, dq_cols] = dq
                acc[rows, dk_cols] = dkc
                acc[rows, dv_cols] = dvc
                return carry

            lax.fori_loop(1, per_step, later, 0)

        @pl.when(n == flush_at)
        def _():
            out_ref[...] = acc[...].astype(BF16)

    flush_at = nsteps - 1 if nsteps == 1 else nsteps
    out_lag = 0 if nsteps == 1 else 1
    nc = lambda n: jnp.minimum(n, nsteps - 1)
    npv = lambda n: jnp.maximum(jnp.minimum(n, nsteps - 1) * per_step - 1, 0)
    cur = lambda part: pl.BlockSpec((None, rows_per_step, ATTN_W), lambda r, n: (r, nc(n), part))
    prev = lambda part: pl.BlockSpec((None, BLK, ATTN_W), lambda r, n: (r, npv(n), part))
    row = pl.BlockSpec((None, rows_per_step, ATTN_W), lambda r, n: (r, nc(n), 0))
    tab_c = pl.BlockSpec((None, rows_per_step, LANES), lambda r, n: (r, nc(n), 0))
    tab_p = pl.BlockSpec((None, BLK, LANES), lambda r, n: (r, npv(n), 0))
    out_spec = pl.BlockSpec((None, rows_per_step, GROUP_COLS), lambda r, n: (r, jnp.maximum(n - out_lag, 0), 0))
    out = pl.pallas_call(
        body, grid=(dil, nsteps + out_lag),
        in_specs=[cur(0), cur(1), prev(1), cur(2), prev(2), row, row, row, tab_c, tab_c, tab_p, tab_p],
        out_specs=out_spec,
        out_shape=jax.ShapeDtypeStruct((dil, length, GROUP_COLS), BF16),
        scratch_shapes=[pltpu.VMEM((rows_per_step, GROUP_COLS), F32)]
        + [pltpu.VMEM((rows_per_step + BLK, ATTN_W), BF16)] * 2 + [pltpu.VMEM((rows_per_step + BLK, LANES), F32)] * 2,
        compiler_params=_cparams(2), name=f"attn_bwd_g{g}")(
            qkv_v, qkv_v, qkv_v, qkv_v, qkv_v, do, cc, lse, cos_v, sin_v, cos_v, sin_v)
    return out.reshape(t, GROUP_COLS)


SQRT_HALF = 0.7071067811865476
INV_SQRT_2PI = 0.3989422804014327


def _sgu_core(uv, g, b, w_ref, bias):
    cdf = 0.5 * (1.0 + lax.erf(uv * SQRT_HALF))
    z = uv * cdf
    u, v = z[:, :SGU_W], z[:, SGU_W:]
    mu = jnp.mean(v, axis=1, keepdims=True)
    xc = v - mu
    rs = lax.rsqrt(jnp.mean(xc * xc, axis=1, keepdims=True) + EPS)
    xhat = xc * rs
    vn = xhat * g + b
    row = lax.broadcasted_iota(jnp.int32, (SGU_CHUNK, SGU_CHUNK), 0)
    col = lax.broadcasted_iota(jnp.int32, (SGU_CHUNK, SGU_CHUNK), 1)
    tril = row >= col
    upper = lax.broadcasted_iota(jnp.int32, (SGU_CHUNK, LANES), 1) >= SGU_W // SGU_GROUPS
    ws, vlo, vhi, mixed = [], [], [], []
    for pr in range(SGU_W // LANES):
        sl = slice(pr * LANES, (pr + 1) * LANES)
        w0 = jnp.where(tril, w_ref[2 * pr], 0.0).astype(BF16)
        w1 = jnp.where(tril, w_ref[2 * pr + 1], 0.0).astype(BF16)
        vn2 = vn[:, sl]
        lo = jnp.where(upper, 0.0, vn2).astype(BF16)
        hi = jnp.where(upper, vn2, 0.0).astype(BF16)
        mixed.append(jnp.dot(w0, lo, preferred_element_type=F32) + jnp.dot(w1, hi, preferred_element_type=F32)
                     + bias[:, sl])
        ws.append((w0, w1))
        vlo.append(lo)
        vhi.append(hi)
    return cdf, u, xhat, rs, jnp.concatenate(mixed, axis=1), ws, vlo, vhi, tril, upper


SGU_STEP = 4 * SGU_CHUNK


def _for_chunks(step_rows, fn):
    def one(ci, carry):
        fn(pl.ds(pl.multiple_of(ci * SGU_CHUNK, SGU_CHUNK), SGU_CHUNK))
        return carry

    lax.fori_loop(0, step_rows // SGU_CHUNK, one, 0)


def _sgu_fwd(gu, ln_g, ln_b, w_s, bias_exp):
    t = gu.shape[0]
    step = min(t, SGU_STEP)

    def body(uv_ref, g_ref, b_ref, w_ref, bias_ref, o_ref):
        def chunk(rows):
            _, u, _, _, mixed, *_ = _sgu_core(uv_ref[rows, :].astype(F32), g_ref[...], b_ref[...], w_ref, bias_ref[...])
            o_ref[rows, :] = (u * mixed).astype(BF16)

        _for_chunks(step, chunk)

    return pl.pallas_call(
        body, grid=(t // step,),
        in_specs=[pl.BlockSpec((step, 2 * SGU_W), lambda n: (n, 0)), _full((1, SGU_W)), _full((1, SGU_W)),
                  _full((SGU_GROUPS, SGU_CHUNK, SGU_CHUNK)), _full((SGU_CHUNK, SGU_W))],
        out_specs=pl.BlockSpec((step, SGU_W), lambda n: (n, 0)),
        out_shape=jax.ShapeDtypeStruct((t, SGU_W), BF16),
        compiler_params=_cparams(1), name="sgu_fwd")(gu, ln_g, ln_b, w_s, bias_exp)


def _sgu_bwd(dproj, gu, dsgu, ln_g, ln_b, w_s, bias_exp):
    t = gu.shape[0]
    step = min(t, SGU_STEP)
    nsteps = t // step
    e = _head_sum_matrix()

    def body(dp_in, uv_ref, ds_ref, g_ref, b_ref, w_ref, bias_ref, e_ref, out_ref, dw_ref, dbias_ref, dg_ref, db_ref):
        n = pl.program_id(0)

        @pl.when(n == 0)
        def _():
            dw_ref[...] = jnp.zeros(dw_ref.shape, F32)
            dbias_ref[...] = jnp.zeros(dbias_ref.shape, F32)
            dg_ref[...] = jnp.zeros(dg_ref.shape, F32)
            db_ref[...] = jnp.zeros(db_ref.shape, F32)

        _for_chunks(step, functools.partial(chunk, uv_ref, ds_ref, g_ref, b_ref, w_ref, bias_ref, out_ref, dw_ref, dbias_ref,
                                            dg_ref, db_ref))

        @pl.when(n == nsteps - 1)
        def _():
            dbias_ref[...] = _group_sum(dbias_ref[...], e_ref[...])

    def chunk(uv_ref, ds_ref, g_ref, b_ref, w_ref, bias_ref, out_ref, dw_ref, dbias_ref, dg_ref, db_ref, rows):
        uv = uv_ref[rows, :].astype(F32)
        g = g_ref[...]
        cdf, u, xhat, rs, mixed, ws, vlo, vhi, tril, upper = _sgu_core(uv, g, b_ref[...], w_ref, bias_ref[...])
        dsg = ds_ref[rows, :]
        du = dsg * mixed
        dmixed = dsg * u
        dbias_ref[...] += dmixed
        dvn = []
        for pr in range(SGU_W // LANES):
            sl = slice(pr * LANES, (pr + 1) * LANES)
            dm2 = dmixed[:, sl]
            dlo = jnp.where(upper, 0.0, dm2).astype(BF16)
            dhi = jnp.where(upper, dm2, 0.0).astype(BF16)
            w0, w1 = ws[pr]
            dvn.append(lax.dot_general(w0, dlo, (TN, ((), ())), preferred_element_type=F32)
                       + lax.dot_general(w1, dhi, (TN, ((), ())), preferred_element_type=F32))
            dw0 = lax.dot_general(dlo, vlo[pr], (NT, ((), ())), preferred_element_type=F32)
            dw1 = lax.dot_general(dhi, vhi[pr], (NT, ((), ())), preferred_element_type=F32)
            dw_ref[2 * pr] += jnp.where(tril, dw0, 0.0)
            dw_ref[2 * pr + 1] += jnp.where(tril, dw1, 0.0)
        dvn = jnp.concatenate(dvn, axis=1)
        dg_ref[...] += jnp.sum(dvn * xhat, axis=0, keepdims=True)
        db_ref[...] += jnp.sum(dvn, axis=0, keepdims=True)
        dxh = dvn * g
        dv = rs * (dxh - jnp.mean(dxh, axis=1, keepdims=True) - xhat * jnp.mean(dxh * xhat, axis=1, keepdims=True))
        dz = jnp.concatenate([du, dv], axis=1)
        dgelu = cdf + uv * (INV_SQRT_2PI * jnp.exp(-0.5 * uv * uv))
        out_ref[rows, :] = (dz * dgelu).astype(BF16)

    outs = pl.pallas_call(
        body, grid=(nsteps,),
        in_specs=[pl.BlockSpec(memory_space=pl.ANY), pl.BlockSpec((step, 2 * SGU_W), lambda n: (n, 0)),
                  pl.BlockSpec((step, SGU_W), lambda n: (n, 0)), _full((1, SGU_W)), _full((1, SGU_W)),
                  _full((SGU_GROUPS, SGU_CHUNK, SGU_CHUNK)), _full((SGU_CHUNK, SGU_W)), _full((ATTN_W, ATTN_W))],
        out_specs=[pl.BlockSpec((step, 2 * SGU_W), lambda n: (n, 0)), _full((SGU_GROUPS, SGU_CHUNK, SGU_CHUNK)),
                   _full((SGU_CHUNK, SGU_W)), _full((1, SGU_W)), _full((1, SGU_W))],
        out_shape=[jax.ShapeDtypeStruct(dproj.shape, BF16), jax.ShapeDtypeStruct((SGU_GROUPS, SGU_CHUNK, SGU_CHUNK), F32),
                   jax.ShapeDtypeStruct((SGU_CHUNK, SGU_W), F32), jax.ShapeDtypeStruct((1, SGU_W), F32),
                   jax.ShapeDtypeStruct((1, SGU_W), F32)],
        input_output_aliases={0: 0},
        compiler_params=_cparams(1), name="sgu_bwd")(dproj, gu, dsgu, ln_g, ln_b, w_s, bias_exp, e)
    return outs


def _merge_fwd(attn, sgu, gu, x, w_pa, w_ps, w_out, g2):
    t = x.shape[0]
    tm = min(t, 512)

    def body(a_ref, s_ref, ga_ref, gb_ref, x_ref, wpa, wps, wo, g_ref, pa_ref, ps_ref, m_ref, x1_ref, h2_ref):
        pa = jnp.dot(a_ref[...], wpa[...], preferred_element_type=F32)
        ps = jnp.dot(s_ref[...], wps[...], preferred_element_type=F32)
        merged = (_sigmoid(ga_ref[...].astype(F32)) * pa + _sigmoid(gb_ref[...].astype(F32)) * ps).astype(BF16)
        x1 = x_ref[...] + jnp.dot(merged, wo[...], preferred_element_type=F32)
        xhat, _ = _rms_stats(x1)
        pa_ref[...] = pa.astype(BF16)
        ps_ref[...] = ps.astype(BF16)
        m_ref[...] = merged
        x1_ref[...] = x1
        h2_ref[...] = (xhat * g_ref[...]).astype(BF16)

    half = pl.BlockSpec((tm, ATTN_W), lambda i: (i, 0))
    full = pl.BlockSpec((tm, D_MODEL), lambda i: (i, 0))
    return pl.pallas_call(
        body, grid=(t // tm,),
        in_specs=[half, half, pl.BlockSpec((tm, D_MODEL), lambda i: (i, 1)), pl.BlockSpec((tm, D_MODEL), lambda i: (i, 2)),
                  full, _resident((ATTN_W, D_MODEL)), _resident((SGU_W, D_MODEL)), _resident((D_MODEL, D_MODEL)),
                  _full((1, D_MODEL))],
        out_specs=[full] * 5,
        out_shape=[jax.ShapeDtypeStruct((t, D_MODEL), BF16), jax.ShapeDtypeStruct((t, D_MODEL), BF16),
                   jax.ShapeDtypeStruct((t, D_MODEL), BF16), jax.ShapeDtypeStruct((t, D_MODEL), F32),
                   jax.ShapeDtypeStruct((t, D_MODEL), BF16)],
        compiler_params=_cparams(1), name="merge_fwd")(attn, sgu, gu, gu, x, w_pa, w_ps, w_out, g2)


def _merge_bwd(dx1b, gu, pa, ps, w_pa, w_ps, w_out):
    t = dx1b.shape[0]
    tm = min(t, 512)

    def body(d_ref, ga_ref, gb_ref, pa_ref, ps_ref, wpa, wps, wo, out_ref, dpa_ref, dps_ref, da_ref, dsg_ref):
        dm = lax.dot_general(d_ref[...], wo[...], (NT, ((), ())), preferred_element_type=F32)
        sa, sb = _sigmoid(ga_ref[...].astype(F32)), _sigmoid(gb_ref[...].astype(F32))
        dpa = (dm * sa).astype(BF16)
        dps = (dm * sb).astype(BF16)
        out_ref[:, 0:D_MODEL] = jnp.zeros((tm, D_MODEL), BF16)
        out_ref[:, D_MODEL:2 * D_MODEL] = (dm * pa_ref[...].astype(F32) * sa * (1.0 - sa)).astype(BF16)
        out_ref[:, 2 * D_MODEL:GU_COLS] = (dm * ps_ref[...].astype(F32) * sb * (1.0 - sb)).astype(BF16)
        dpa_ref[...] = dpa
        dps_ref[...] = dps
        da_ref[...] = lax.dot_general(dpa, wpa[...], (NT, ((), ())), preferred_element_type=F32)
        dsg_ref[...] = lax.dot_general(dps, wps[...], (NT, ((), ())), preferred_element_type=F32)

    half = pl.BlockSpec((tm, ATTN_W), lambda i: (i, 0))
    full = pl.BlockSpec((tm, D_MODEL), lambda i: (i, 0))
    return pl.pallas_call(
        body, grid=(t // tm,),
        in_specs=[full, pl.BlockSpec((tm, D_MODEL), lambda i: (i, 1)),
                  pl.BlockSpec((tm, D_MODEL), lambda i: (i, 2)), full, full,
                  _resident((ATTN_W, D_MODEL)), _resident((SGU_W, D_MODEL)), _resident((D_MODEL, D_MODEL))],
        out_specs=[pl.BlockSpec((tm, GU_COLS), lambda i: (i, 0)), full, full, half, half],
        out_shape=[jax.ShapeDtypeStruct((t, GU_COLS), BF16), jax.ShapeDtypeStruct((t, D_MODEL), BF16),
                   jax.ShapeDtypeStruct((t, D_MODEL), BF16), jax.ShapeDtypeStruct((t, ATTN_W), F32),
                   jax.ShapeDtypeStruct((t, SGU_W), F32)],
        compiler_params=_cparams(1), name="merge_bwd")(dx1b, gu, gu, pa, ps, w_pa, w_ps, w_out)


def _token_call(name, body, t, tm, ins, outs, reds=(), scratch=()):
    return pl.pallas_call(
        body, grid=(t // tm,), in_specs=[s for _, s in ins],
        out_specs=[o[2] for o in outs] + [_full(r) for r in reds],
        out_shape=[jax.ShapeDtypeStruct(o[0], o[1]) for o in outs] + [jax.ShapeDtypeStruct(r, F32) for r in reds],
        scratch_shapes=list(scratch), compiler_params=_cparams(1), name=name)(*[a for a, _ in ins])


def _rows_spec(tm, width):
    return pl.BlockSpec((tm, width), lambda i: (i, 0))


def _chips_spec(tm):
    return pl.BlockSpec((N_CHIPS, tm, FF_SHARD), lambda i: (0, i, 0))


def _zero_at_start(*refs):
    @pl.when(pl.program_id(0) == 0)
    def _():
        for r in refs:
            r[...] = jnp.zeros(r.shape, r.dtype)


def _ffn_fwd(h2, w_g, w_u):
    t = h2.shape[0]
    tm = min(t, 512)

    def body(h_ref, wg_ref, wu_ref, a_ref, b_ref, ff_ref):
        h = h_ref[...]
        for s in range(N_CHIPS):
            a = jnp.dot(h, wg_ref[s], preferred_element_type=F32)
            b = jnp.dot(h, wu_ref[s], preferred_element_type=F32)
            a_ref[s] = a.astype(BF16)
            b_ref[s] = b.astype(BF16)
            ff_ref[s] = (a * _sigmoid(a) * b).astype(BF16)

    shp = (N_CHIPS, t, FF_SHARD)
    w_spec = _resident((N_CHIPS, D_MODEL, FF_SHARD))
    return _token_call("ffn_fwd", body, t, tm, [(h2, _rows_spec(tm, D_MODEL)), (w_g, w_spec), (w_u, w_spec)],
                       [(shp, BF16, _chips_spec(tm))] * 3)


def _ffn_down_loss(ff, w_d, x1, tgt, gf):
    t = x1.shape[0]
    tm = min(t, 512)

    def body(ff_ref, wd_ref, x1_ref, tgt_ref, g_ref, dx2_ref, dx2b_ref, loss_ref, dgf_ref):
        _zero_at_start(loss_ref, dgf_ref)
        acc = jnp.dot(ff_ref[0], wd_ref[0], preferred_element_type=F32)
        for s in range(1, N_CHIPS):
            acc = acc + jnp.dot(ff_ref[s], wd_ref[s], preferred_element_type=F32)
        x2 = x1_ref[...] + acc
        g = g_ref[...]
        xhat, rr = _rms_stats(x2)
        diff = xhat * g - tgt_ref[...]
        rows = jnp.sum(diff * diff, axis=1, keepdims=True)
        loss_ref[...] += jnp.broadcast_to(jnp.sum(rows, axis=0, keepdims=True) * (0.5 / D_MODEL), (1, LANES))
        dy = diff * (1.0 / D_MODEL)
        dgf_ref[...] += jnp.sum(dy * xhat, axis=0, keepdims=True)
        dx2 = _rms_bwd(dy, xhat, rr, g)
        dx2_ref[...] = dx2
        dx2b_ref[...] = dx2.astype(BF16)

    row = _rows_spec(tm, D_MODEL)
    return _token_call("ffn_down_loss", body, t, tm,
                       [(ff, _chips_spec(tm)), (w_d, _resident((N_CHIPS, FF_SHARD, D_MODEL))), (x1, row), (tgt, row),
                        (gf, _full((1, D_MODEL)))],
                       [((t, D_MODEL), F32, row), ((t, D_MODEL), BF16, row)], reds=[(1, LANES), (1, D_MODEL)])


def _ffn_bwd_act(dx2b, w_d, a, b):
    t = dx2b.shape[0]
    tm = min(t, 512)

    def body(d_ref, wd_ref, a_ref, b_ref, da_ref, db_ref):
        d = d_ref[...]
        for s in range(N_CHIPS):
            dff = lax.dot_general(d, wd_ref[s], (NT, ((), ())), preferred_element_type=F32)
            av, bv = a_ref[s].astype(F32), b_ref[s].astype(F32)
            sg = _sigmoid(av)
            da_ref[s] = (dff * bv * (sg * (1.0 + av * (1.0 - sg)))).astype(BF16)
            db_ref[s] = (dff * (av * sg)).astype(BF16)

    shp = (N_CHIPS, t, FF_SHARD)
    return _token_call("ffn_bwd_act", body, t, tm,
                       [(dx2b, _rows_spec(tm, D_MODEL)), (w_d, _resident((N_CHIPS, FF_SHARD, D_MODEL))),
                        (a, _chips_spec(tm)), (b, _chips_spec(tm))],
                       [(shp, BF16, _chips_spec(tm))] * 2)


def _ffn_bwd_in(da, db, w_g, w_u, x1, dx2, g2):
    t = x1.shape[0]
    tm = min(t, 512)

    def body(da_ref, db_ref, wg_ref, wu_ref, x1_ref, dx2_ref, g_ref, dx1_ref, dx1b_ref, dg_ref):
        _zero_at_start(dg_ref)
        acc = None
        for s in range(N_CHIPS):
            part = (lax.dot_general(da_ref[s], wg_ref[s], (NT, ((), ())), preferred_element_type=F32)
                    + lax.dot_general(db_ref[s], wu_ref[s], (NT, ((), ())), preferred_element_type=F32))
            acc = part if acc is None else acc + part
        xhat, rr = _rms_stats(x1_ref[...])
        dg_ref[...] += jnp.sum(acc * xhat, axis=0, keepdims=True)
        dx1 = dx2_ref[...] + _rms_bwd(acc, xhat, rr, g_ref[...])
        dx1_ref[...] = dx1
        dx1b_ref[...] = dx1.astype(BF16)

    row = _rows_spec(tm, D_MODEL)
    w_spec = _resident((N_CHIPS, D_MODEL, FF_SHARD))
    return _token_call("ffn_bwd_in", body, t, tm,
                       [(da, _chips_spec(tm)), (db, _chips_spec(tm)), (w_g, w_spec), (w_u, w_spec), (x1, row), (dx2, row),
                        (g2, _full((1, D_MODEL)))],
                       [((t, D_MODEL), F32, row), ((t, D_MODEL), BF16, row)], reds=[(1, D_MODEL)])


def _group_dh(d, w_refs):
    dh = None
    for part, w_ref in enumerate(w_refs):
        term = lax.dot_general(d[:, part * ATTN_W:(part + 1) * ATTN_W], w_ref[...], (NT, ((), ())),
                               preferred_element_type=F32)
        dh = term if dh is None else dh + term
    return dh


def _in_proj_bwd(dgu, dqkvs, w_in, x, dx1, g1):
    t = x.shape[0]
    tile = min(t, TILE)
    ngroups = len(DILATIONS)

    def body(*refs):
        dgu_ref, dq_refs = refs[0], refs[1:1 + ngroups]
        w0_ref, w1_ref = refs[1 + ngroups:3 + ngroups]
        wg_refs = [refs[3 + ngroups + 3 * g:6 + ngroups + 3 * g] for g in range(ngroups)]
        x_ref, dx1_ref, g_ref, dx_ref, dg_ref, slab = refs[3 + 4 * ngroups:]
        _zero_at_start(dg_ref)
        dh = lax.dot_general(dgu_ref[:, 0:GU_HALF], w0_ref[...], (NT, ((), ())), preferred_element_type=F32)
        dh = dh + lax.dot_general(dgu_ref[:, GU_HALF:], w1_ref[...], (NT, ((), ())), preferred_element_type=F32)
        dh = dh + _group_dh(dq_refs[0][0], wg_refs[0])
        for g in range(1, ngroups):
            dil = DILATIONS[g]
            part = _group_dh(dq_refs[g][...].reshape(tile, GROUP_COLS), wg_refs[g])
            for r in range(dil):
                _put_class_rows(slab, r, dil, part[r * (tile // dil):(r + 1) * (tile // dil)])
            dh = dh + _from_slabs(slab)
        xhat, rr = _rms_stats(x_ref[...])
        dg_ref[...] += jnp.sum(dh * xhat, axis=0, keepdims=True)
        dx_ref[...] = dx1_ref[...] + _rms_bwd(dh, xhat, rr, g_ref[...])

    row = _rows_spec(tile, D_MODEL)
    group_ins = [(dqkvs[g].reshape(d, t // d, GROUP_COLS), _group_spec(d, tile, GROUP_COLS)) for g, d in enumerate(DILATIONS)]
    w_specs = _gu_w_specs() + [s for g in range(ngroups) for s in _group_w_specs(g)]
    return _token_call(
        "in_proj_bwd", body, t, tile,
        [(dgu, _rows_spec(tile, GU_COLS))] + group_ins + [(w_in, s) for s in w_specs]
        + [(x, row), (dx1, row), (g1, _full((1, D_MODEL)))],
        [((t, D_MODEL), F32, row)], reds=[(1, D_MODEL)], scratch=[_slabs(tile, D_MODEL)])


WGRAD_TK = 2048


def _wgrad_mm(name, grid, a, a_spec, b, b_spec, acc_shape, out_shape, out_spec, dst=None):
    nk = grid[-1]

    def body(*refs):
        a_ref, b_ref, o_ref, acc_ref = refs[0], refs[1], refs[-2], refs[-1]
        k = pl.program_id(len(grid) - 1)
        part = lax.dot_general(a_ref[...], b_ref[...], (TN, ((), ())), preferred_element_type=F32)

        @pl.when(k == 0)
        def _():
            acc_ref[...] = part

        @pl.when(k > 0)
        def _():
            acc_ref[...] += part

        @pl.when(k == nk - 1)
        def _():
            o_ref[...] = acc_ref[...].astype(BF16)

    filled = [] if dst is None else [dst]
    return pl.pallas_call(
        body, grid=grid, in_specs=[a_spec, b_spec] + [pl.BlockSpec(memory_space=pl.ANY)] * len(filled),
        out_specs=out_spec, out_shape=jax.ShapeDtypeStruct(out_shape, BF16), scratch_shapes=[pltpu.VMEM(acc_shape, F32)],
        input_output_aliases={2: 0} if filled else {}, compiler_params=_cparams(len(grid)), name=name)(a, b, *filled)


def _wgrad_2d(name, a, b, tm, tn):
    t, k1 = a.shape
    n = b.shape[1]
    tk = min(t, WGRAD_TK)
    return _wgrad_mm(name, (k1 // tm, n // tn, t // tk), a, pl.BlockSpec((tk, tm), lambda i, j, k: (k, i)),
                     b, pl.BlockSpec((tk, tn), lambda i, j, k: (k, j)), (tm, tn), (k1, n),
                     pl.BlockSpec((tm, tn), lambda i, j, k: (i, j)))


def _wgrad_in(hs, dgu, dqkvs):
    t = dgu.shape[0]
    tk = min(t, WGRAD_TK)
    gu_block = QKV_BLOCKS * ATTN_W // GU_HALF
    parts = [(hs[0], dgu, GU_HALF, lambda j: j + gu_block)]
    parts += [(hs[g].reshape(t, D_MODEL), dqkvs[g], ATTN_W, lambda j, g=g: _w_in_block(j, g)) for g in range(3)]
    dst = None
    for n, (a, b, tn, block_of) in enumerate(parts):
        dst = _wgrad_mm(f"wgrad_in_{n}", (1, b.shape[1] // tn, t // tk),
                        a, pl.BlockSpec((tk, D_MODEL), lambda i, j, k: (k, 0)), b, pl.BlockSpec((tk, tn), lambda i, j, k: (k, j)),
                        (D_MODEL, tn), (D_MODEL, IN_COLS),
                        pl.BlockSpec((D_MODEL, tn), lambda i, j, k, block_of=block_of: (0, block_of(j))), dst=dst)
    return dst


def _wgrad_ff_in(name, h2, da):
    t = h2.shape[0]
    tk = min(t, WGRAD_TK)
    return _wgrad_mm(name, (N_CHIPS, 1, t // tk), h2, pl.BlockSpec((tk, D_MODEL), lambda i, j, k: (k, 0)),
                     da, pl.BlockSpec((None, tk, FF_SHARD), lambda i, j, k: (i, k, 0)), (D_MODEL, FF_SHARD),
                     (N_CHIPS, D_MODEL, FF_SHARD), pl.BlockSpec((None, D_MODEL, FF_SHARD), lambda i, j, k: (i, 0, 0)))


def _wgrad_ff_down(ff, dx2b):
    t = dx2b.shape[0]
    tk = min(t, WGRAD_TK)
    return _wgrad_mm("wgrad_ffn_down", (N_CHIPS, 1, t // tk), ff, pl.BlockSpec((None, tk, FF_SHARD), lambda i, j, k: (i, k, 0)),
                     dx2b, pl.BlockSpec((tk, D_MODEL), lambda i, j, k: (k, 0)), (FF_SHARD, D_MODEL),
                     (N_CHIPS, FF_SHARD, D_MODEL), pl.BlockSpec((None, FF_SHARD, D_MODEL), lambda i, j, k: (i, 0, 0)))


def _local_step(x, pos_col, tgt, g1, ln_g, ln_b, w_s, b_s, g2, gf, first_weight, late_weights, on_grads=None):
    tables = _rope_tables(pos_col)
    bias_exp = jnp.repeat(jnp.transpose(b_s), SGU_W // SGU_GROUPS, axis=1)

    hs = _norm_fwd(x, g1)
    w_p = first_weight(hs[0])
    gu, qkvs = _in_proj(hs, w_p, tables)
    os_, ls_ = [], []
    for g, dil in enumerate(DILATIONS):
        o, lse = _attn_fwd(qkvs[g], g, dil)
        os_.append(o)
        ls_.append(lse)
    attn = _combine_fwd(os_, ls_)
    sgu = _sgu_fwd(gu, ln_g, ln_b, w_s, bias_exp)
    w_pa, w_ps, w_out, w_g, w_u, w_d = late_weights(attn)
    pa, ps, merged, x1, h2 = _merge_fwd(attn, sgu, gu, x, w_pa, w_ps, w_out, g2)
    a, b, ff = _ffn_fwd(h2, w_g, w_u)
    dx2, dx2b, loss, dgf = _ffn_down_loss(ff, w_d, x1, tgt, gf)

    da, db = _ffn_bwd_act(dx2b, w_d, a, b)
    dw_d = _wgrad_ff_down(ff, dx2b)
    dx1, dx1b, dg2 = _ffn_bwd_in(da, db, w_g, w_u, x1, dx2, g2)
    dw_g = _wgrad_ff_in("wgrad_ffn_gate", h2, da)
    dw_u = _wgrad_ff_in("wgrad_ffn_up", h2, db)

    dgu, dpa, dps, dattn, dsgu = _merge_bwd(dx1b, gu, pa, ps, w_pa, w_ps, w_out)
    dw_out = _wgrad_2d("wgrad_out", merged, dx1b, D_MODEL, D_MODEL)
    dw_pa = _wgrad_2d("wgrad_proj_attn", attn, dpa, ATTN_W, D_MODEL)
    dw_ps = _wgrad_2d("wgrad_proj_sgu", sgu, dps, SGU_W, D_MODEL)
    if on_grads is not None:
        ln_g = ln_g + on_grads(1, dict(w_proj_attn=dw_pa, w_proj_sgu=dw_ps, w_out=dw_out, w_ffn_gate=dw_g, w_ffn_up=dw_u,
                                       w_ffn_down=dw_d))[:, :SGU_W]
    dgu, dw_s, dbias, dln_g, dln_b = _sgu_bwd(dgu, gu, dsgu, ln_g, ln_b, w_s, bias_exp)
    dos, ccs = _combine_bwd(dattn, os_, ls_)
    dqkvs = [_attn_bwd(qkvs[g], dos[g], ccs[g], ls_[g], *tables[g], g, dil) for g, dil in enumerate(DILATIONS)]
    dw_p = _wgrad_in(hs, dgu, dqkvs)
    if on_grads is not None:
        g1 = g1 + on_grads(0, dict(w_in=dw_p))
    dx, dg1 = _in_proj_bwd(dgu, dqkvs, w_p, x, dx1, g1)

    db_s = jnp.transpose(dbias[:, ::SGU_W // SGU_GROUPS])
    small = dict(loss=loss, norm1_g=dg1, sgu_ln_g=dln_g, sgu_ln_b=dln_b, w_spatial=dw_s, b_spatial=db_s,
                 norm2_g=dg2, final_g=dgf)
    big = dict(w_in=dw_p, w_proj_attn=dw_pa, w_proj_sgu=dw_ps, w_out=dw_out, w_ffn_gate=dw_g, w_ffn_up=dw_u,
               w_ffn_down=dw_d)
    return dx, big, small


def _ew(name, fn, ins, out_dtypes):
    shp = ins[0].shape
    rows, cols = shp
    tr = next((cand for cand in (256, 352, 128) if rows % cand == 0 and rows > cand), rows)

    def body(*refs):
        res = fn(*[r[...] for r in refs[:len(ins)]])
        for o_ref, v in zip(refs[len(ins):], res):
            o_ref[...] = v.astype(o_ref.dtype)

    spec = pl.BlockSpec((tr, cols), lambda i: (i, 0))
    return pl.pallas_call(
        body, grid=(rows // tr,), in_specs=[spec] * len(ins), out_specs=[spec] * len(out_dtypes),
        out_shape=[jax.ShapeDtypeStruct(shp, d) for d in out_dtypes],
        compiler_params=_cparams(1), name=name)(*ins)


def _adamw_math(g, w, m, v):
    m = ADAM_B1 * m + (1.0 - ADAM_B1) * g
    v = ADAM_B2 * v + (1.0 - ADAM_B2) * (g * g)
    m_hat = m / (1.0 - ADAM_B1 ** ADAM_STEP)
    v_hat = v / (1.0 - ADAM_B2 ** ADAM_STEP)
    delta = -ADAM_LR * (m_hat / (jnp.sqrt(v_hat) + ADAM_EPS) + ADAM_WD * w)
    return delta, m, v


def _adamw(name, g, w, m, v):
    return _ew(name, lambda g_, w_, m_, v_: (g_,) + _adamw_math(g_, w_, m_, v_), [g, w, m, v], [F32] * 4)


VMEM_SPEC = pl.BlockSpec(memory_space=pltpu.VMEM)


def _for_row_chunks(rows, fn):
    ck = next(c for c in (64, 32, 16) if rows % c == 0)

    def step(i, carry):
        fn(pl.multiple_of(i * ck, ck), ck)
        return carry

    lax.fori_loop(0, rows // ck, step, 0)


def _place():
    x, y, c = lax.axis_index("x"), lax.axis_index("y"), lax.axis_index("c")
    chips = [(1 - x, y), (x, 1 - y), (1 - x, 1 - y)]
    return x, y, c, 2 * x + y, chips


def _rows(ref, start, size):
    if len(ref.shape) == 2:
        return ref.at[pl.ds(start, size), :]
    return ref.at[:, pl.ds(start, size), :]


def _comm_call(name, body, ins, out_shapes, scratch, n_remote):
    return pl.pallas_call(
        body, in_specs=[VMEM_SPEC] * len(ins), out_specs=[VMEM_SPEC] * len(out_shapes),
        out_shape=out_shapes,
        scratch_shapes=list(scratch) + [pltpu.SemaphoreType.DMA((n_remote,)), pltpu.SemaphoreType.DMA((n_remote,))],
        compiler_params=pltpu.CompilerParams(vmem_limit_bytes=VMEM_LIMIT), name=name)(*ins)


def _gather_finish(name, shard, landed):
    k_rows, n = shard.shape
    kh = k_rows // 2

    def body(shard_ref, land_ref, out_ref, send, recv):
        x, y, c, me, chips = _place()
        passed = []
        for j, chip in enumerate(chips):
            theirs = 2 * chip[0] + chip[1]
            cp = pltpu.make_async_remote_copy(
                src_ref=land_ref.at[j], dst_ref=_rows(out_ref.at[theirs], c * kh, kh), send_sem=send.at[j],
                recv_sem=recv.at[j], device_id=(x, y, 1 - c), device_id_type=MESH)
            cp.start()
            passed.append(cp)
        mine = out_ref.at[me]

        def put_own(r0, ck):
            mine[pl.ds(r0, ck), :] = shard_ref[pl.ds(r0, ck), :]

        _for_row_chunks(k_rows, put_own)
        for j, chip in enumerate(chips):
            slot = out_ref.at[2 * chip[0] + chip[1]]

            def put_half(r0, ck, j=j, slot=slot):
                slot[pl.ds(pl.multiple_of(c * kh + r0, ck), ck), :] = land_ref[j, pl.ds(r0, ck), :]

            _for_row_chunks(kh, put_half)
        for j, chip in enumerate(chips):
            other = _rows(out_ref.at[2 * chip[0] + chip[1]], (1 - c) * kh, kh)
            pltpu.make_async_remote_copy(src_ref=other, dst_ref=other, send_sem=send.at[j], recv_sem=recv.at[j],
                                         device_id=(x, y, 1 - c), device_id_type=MESH).wait_recv()
        for cp in passed:
            cp.wait_send()

    return _comm_call(name, body, [shard, landed], [jax.ShapeDtypeStruct((N_CHIPS, k_rows, n), shard.dtype)], [], 3)[0]


HBM_SPEC = pl.BlockSpec(memory_space=pltpu.HBM)
SEM_SPEC = pl.BlockSpec(memory_space=pltpu.SEMAPHORE)
DATAFLOW = pltpu.SideEffectType.DATAFLOW_SIDE_EFFECTING
TOKEN_SHAPE = (1, D_MODEL)
N_PEERS = 7


def _peers():
    x, y, c = lax.axis_index("x"), lax.axis_index("y"), lax.axis_index("c")
    flip = lambda v, f: 1 - v if f else v
    return [(flip(x, k & 4), flip(y, k & 2), flip(c, k & 1)) for k in range(1, N_PEERS + 1)]


def _piece_shape(shape):
    return (shape[-2] // 2, shape[2] if len(shape) == 3 else shape[1] // N_CHIPS)


def _device_piece(ref, chip, core):
    kh, n4 = _piece_shape(ref.shape)
    if len(ref.shape) == 3:
        return ref.at[chip, pl.ds(core * kh, kh), :]
    return ref.at[pl.ds(core * kh, kh), pl.ds(chip * n4, n4)]


def _exchange_copies(partials, lands, send, recv):
    return [pltpu.make_async_remote_copy(
        src_ref=_device_piece(partials[t], 2 * px + py, pc), dst_ref=lands[t].at[k], send_sem=send.at[t * N_PEERS + k],
        recv_sem=recv.at[t * N_PEERS + k], device_id=(px, py, pc), device_id_type=MESH)
        for t in range(len(partials)) for k, (px, py, pc) in enumerate(_peers())]


def _gather_copies(shards, lands, send, recv):
    x, y, c, me, chips = _place()
    return [pltpu.make_async_remote_copy(
        src_ref=shards[t], dst_ref=lands[t].at[me], send_sem=send.at[t * 3 + j], recv_sem=recv.at[t * 3 + j],
        device_id=(*chip, c), device_id_type=MESH)
        for t in range(len(shards)) for j, chip in enumerate(chips)]


HALF_PIECES = 4


def _gather_half_copies(shards, lands, send, recv):
    x, y, c, me, chips = _place()
    copies = []
    for t in range(len(shards)):
        kh = shards[t].shape[0] // 2
        rows = kh // HALF_PIECES
        for j, chip in enumerate(chips):
            for q in range(HALF_PIECES):
                k = (t * 3 + j) * HALF_PIECES + q
                copies.append(pltpu.make_async_remote_copy(
                    src_ref=_rows(shards[t], c * kh + q * rows, rows), dst_ref=_rows(lands[t].at[j], q * rows, rows),
                    send_sem=send.at[k], recv_sem=recv.at[k], device_id=(*chip, c), device_id_type=MESH))
    return copies


def _split_start(name, copies, per_tensor, srcs, land_shapes):
    nt = len(srcs)
    lands = [lax.empty(s, BF16) for s in land_shapes]
    nsem = nt * per_tensor

    def body(*refs):
        send, recv = refs[2 * nt], refs[2 * nt + 1]
        for cp in copies(refs[:nt], refs[nt:2 * nt], send, recv):
            cp.start()
        refs[-1][...] = jnp.zeros(TOKEN_SHAPE, F32)

    hbm = lambda a: pltpu.with_memory_space_constraint(a, pltpu.HBM)
    outs = pl.pallas_call(
        body, name=name,
        out_shape=[pltpu.SemaphoreType.DMA((nsem,)), pltpu.SemaphoreType.DMA((nsem,))]
        + [pltpu.HBM(s.shape, s.dtype) for s in srcs] + [pltpu.HBM(l.shape, l.dtype) for l in lands]
        + [jax.ShapeDtypeStruct(TOKEN_SHAPE, F32)],
        in_specs=[HBM_SPEC] * (2 * nt), out_specs=[SEM_SPEC, SEM_SPEC] + [HBM_SPEC] * (2 * nt) + [VMEM_SPEC],
        input_output_aliases={i: 2 + i for i in range(2 * nt)},
        compiler_params=pltpu.CompilerParams(has_side_effects=DATAFLOW))(*[hbm(a) for a in list(srcs) + lands])
    return outs[0], outs[1], outs[2:2 + nt], outs[2 + nt:2 + 2 * nt], outs[-1]


def _split_wait(name, copies, send, recv, srcs, lands, after):
    nt = len(srcs)

    def body(*refs):
        for cp in copies(refs[:nt], refs[nt:2 * nt], refs[2 * nt], refs[2 * nt + 1]):
            cp.wait_send()
            cp.wait_recv()

    outs = pl.pallas_call(
        body, name=name,
        out_shape=[pltpu.HBM(s.shape, s.dtype) for s in srcs] + [pltpu.HBM(l.shape, l.dtype) for l in lands],
        in_specs=[HBM_SPEC] * (2 * nt) + [SEM_SPEC, SEM_SPEC, pl.BlockSpec(memory_space=pl.ANY)],
        out_specs=[HBM_SPEC] * (2 * nt), input_output_aliases={i: i for i in range(2 * nt)},
        compiler_params=pltpu.CompilerParams(has_side_effects=DATAFLOW))(*srcs, *lands, send, recv, after)
    return outs[:nt], outs[nt:]


def _device_sum(name, partials, lands):
    nt = len(partials)

    def body(*refs):
        ins, slots, outs, owns = refs[:nt], refs[nt:2 * nt], refs[2 * nt:3 * nt], refs[3 * nt:4 * nt]
        send, recv, loc = refs[4 * nt:]
        x, y, c, me, chips = _place()
        sibling = (x, y, 1 - c)
        loads = [pltpu.make_async_copy(_device_piece(ins[t], me, c), owns[t], loc.at[t]) for t in range(nt)]
        for cp in loads:
            cp.start()
        handed = []
        for t in range(nt):
            kh = owns[t].shape[0]
            loads[t].wait()

            def add(r0, ck, own=owns[t], slot=slots[t], dst=outs[t], kh=kh):
                rows = pl.ds(r0, ck)
                acc = own[rows, :].astype(F32)
                for k in range(N_PEERS):
                    acc = acc + slot[k, rows, :].astype(F32)
                dst[pl.ds(pl.multiple_of(c * kh + r0, ck), ck), :] = acc

            _for_row_chunks(kh, add)
            rc = pltpu.make_async_remote_copy(
                src_ref=_rows(outs[t], c * kh, kh), dst_ref=_rows(outs[t], c * kh, kh), send_sem=send.at[t],
                recv_sem=recv.at[t], device_id=sibling, device_id_type=MESH)
            rc.start()
            handed.append(rc)
        for t in range(nt):
            kh = owns[t].shape[0]
            other = _rows(outs[t], (1 - c) * kh, kh)
            pltpu.make_async_remote_copy(
                src_ref=other, dst_ref=other, send_sem=send.at[t], recv_sem=recv.at[t],
                device_id=sibling, device_id_type=MESH).wait_recv()
        for rc in handed:
            rc.wait_send()

    pieces = [_piece_shape(p.shape) for p in partials]
    return pl.pallas_call(
        body, in_specs=[pl.BlockSpec(memory_space=pl.ANY)] * nt + [VMEM_SPEC] * nt, out_specs=[VMEM_SPEC] * nt,
        out_shape=[jax.ShapeDtypeStruct((2 * kh, n4), F32) for kh, n4 in pieces],
        scratch_shapes=[pltpu.VMEM(p, BF16) for p in pieces]
        + [pltpu.SemaphoreType.DMA((nt,)), pltpu.SemaphoreType.DMA((nt,)), pltpu.SemaphoreType.DMA((nt,))],
        compiler_params=pltpu.CompilerParams(vmem_limit_bytes=VMEM_LIMIT), name=name)(*partials, *lands)


VEC_SHAPE = (8, D_MODEL + LANES)
VEC_SLOTS = dict(norm1_g=(slice(0, 1), slice(0, D_MODEL)), norm2_g=(slice(1, 2), slice(0, D_MODEL)),
                 final_g=(slice(2, 3), slice(0, D_MODEL)), sgu_ln_g=(slice(3, 4), slice(0, SGU_W)),
                 sgu_ln_b=(slice(3, 4), slice(SGU_W, 2 * SGU_W)), b_spatial=(slice(0, 8), slice(D_MODEL, D_MODEL + LANES)),
                 loss=(slice(4, 5), slice(0, LANES)))
VEC_PARAMS = ("norm1_g", "norm2_g", "final_g", "sgu_ln_g", "sgu_ln_b", "b_spatial")
SMALL_PARAMS = VEC_PARAMS + ("w_spatial",)
W_SPATIAL_2D = (SGU_GROUPS * SGU_CHUNK, SGU_CHUNK)


def _small_step(partials, w, m, v):
    def shape2d(name):
        if name == "w_spatial":
            return W_SPATIAL_2D
        rows, cols = VEC_SLOTS[name]
        return (rows.stop - rows.start, cols.stop - cols.start)

    g_names = VEC_PARAMS + ("loss", "w_spatial")
    ng, npar = len(g_names), len(SMALL_PARAMS)

    def pack(dst, parts):
        dst[...] = jnp.zeros(VEC_SHAPE, F32)
        for n, ref in parts.items():
            if n in VEC_SLOTS:
                dst[VEC_SLOTS[n]] = ref[...]

    def reduce_body(*refs):
        g_in = dict(zip(g_names, refs[:ng]))
        vec_out, ws_out, vec, vec_pair, vec_slot, ws_pair, ws_slot, send, recv = refs[ng:]
        x, y, c, me, chips = _place()
        sibling = (x, y, 1 - c)
        pack(vec, g_in)
        copies = []

        def allreduce(k0, src, pair, slot):
            first = pltpu.make_async_remote_copy(src_ref=src, dst_ref=pair, send_sem=send.at[k0], recv_sem=recv.at[k0],
                                                 device_id=sibling, device_id_type=MESH)
            first.start()
            first.wait_recv()
            slot[me] = src[...] + pair[...]
            arrivals = []
            for j, chip in enumerate(chips):
                theirs = 2 * chip[0] + chip[1]
                rc = pltpu.make_async_remote_copy(src_ref=slot.at[me], dst_ref=slot.at[me], send_sem=send.at[k0 + 1 + j],
                                                  recv_sem=recv.at[k0 + 1 + j], device_id=(*chip, c), device_id_type=MESH)
                rc.start()
                arrivals.append(pltpu.make_async_remote_copy(
                    src_ref=slot.at[theirs], dst_ref=slot.at[theirs], send_sem=send.at[k0 + 1 + j],
                    recv_sem=recv.at[k0 + 1 + j], device_id=(*chip, c), device_id_type=MESH))
                copies.append(rc)
            copies.append(first)
            return arrivals

        arrivals = allreduce(0, vec, vec_pair, vec_slot) + allreduce(4, g_in["w_spatial"], ws_pair, ws_slot)
        for a in arrivals:
            a.wait_recv()
        vec_out[...] = ((vec_slot[0] + vec_slot[1]) + vec_slot[2]) + vec_slot[3]

        def spatial(r0, ck):
            rows = pl.ds(r0, ck)
            ws_out[rows, :] = ((ws_slot[0, rows, :] + ws_slot[1, rows, :]) + ws_slot[2, rows, :]) + ws_slot[3, rows, :]

        _for_row_chunks(W_SPATIAL_2D[0], spatial)
        for rc in copies:
            rc.wait_send()

    g_vec, g_ws = pl.pallas_call(
        reduce_body, in_specs=[VMEM_SPEC] * ng, out_specs=[VMEM_SPEC] * 2,
        out_shape=[jax.ShapeDtypeStruct(VEC_SHAPE, F32), jax.ShapeDtypeStruct(W_SPATIAL_2D, F32)],
        scratch_shapes=[pltpu.VMEM(VEC_SHAPE, F32), pltpu.VMEM(VEC_SHAPE, F32), pltpu.VMEM((N_CHIPS,) + VEC_SHAPE, F32),
                        pltpu.VMEM(W_SPATIAL_2D, F32), pltpu.VMEM((N_CHIPS,) + W_SPATIAL_2D, F32),
                        pltpu.SemaphoreType.DMA((8,)), pltpu.SemaphoreType.DMA((8,))],
        name="small_params_allreduce")(*[partials[n].reshape(shape2d(n)) for n in g_names])

    def update_body(*refs):
        gv_ref, gw_ref = refs[:2]
        w_in, m_in, v_in = (dict(zip(SMALL_PARAMS, refs[2 + k * npar:2 + (k + 1) * npar])) for k in range(3))
        o0 = 2 + 3 * npar
        g_out = dict(zip(g_names, refs[o0:o0 + ng]))
        d_out, m_out, v_out = (dict(zip(SMALL_PARAMS, refs[o0 + ng + k * npar:o0 + ng + (k + 1) * npar])) for k in range(3))
        vw, vm, vv = refs[o0 + ng + 3 * npar:]
        pack(vw, w_in)
        pack(vm, m_in)
        pack(vv, v_in)
        d_vec, m_vec, v_vec = _adamw_math(gv_ref[...], vw[...], vm[...], vv[...])
        vw[...] = d_vec
        vm[...] = m_vec
        vv[...] = v_vec
        for n in VEC_PARAMS + ("loss",):
            g_out[n][...] = gv_ref[VEC_SLOTS[n]]
        for n in VEC_PARAMS:
            d_out[n][...] = vw[VEC_SLOTS[n]]
            m_out[n][...] = vm[VEC_SLOTS[n]]
            v_out[n][...] = vv[VEC_SLOTS[n]]

        def spatial(r0, ck):
            rows = pl.ds(r0, ck)
            g = gw_ref[rows, :]
            d_, m_, v_ = _adamw_math(g, w_in["w_spatial"][rows, :], m_in["w_spatial"][rows, :], v_in["w_spatial"][rows, :])
            g_out["w_spatial"][rows, :] = g
            d_out["w_spatial"][rows, :] = d_
            m_out["w_spatial"][rows, :] = m_
            v_out["w_spatial"][rows, :] = v_

        _for_row_chunks(W_SPATIAL_2D[0], spatial)

    ins = [g_vec, g_ws]
    for src in (w, m, v):
        ins += [src[n].reshape(shape2d(n)) for n in SMALL_PARAMS]
    out_shapes = [jax.ShapeDtypeStruct(shape2d(n), F32) for n in g_names + SMALL_PARAMS * 3]
    outs = pl.pallas_call(
        update_body, in_specs=[VMEM_SPEC] * len(ins), out_specs=[VMEM_SPEC] * len(out_shapes), out_shape=out_shapes,
        scratch_shapes=[pltpu.VMEM(VEC_SHAPE, F32)] * 3, name="small_params_update")(*ins)
    grads = dict(zip(g_names, outs[:ng]))
    rest = [dict(zip(SMALL_PARAMS, outs[ng + k * npar:ng + (k + 1) * npar])) for k in range(3)]
    return grads, rest[0], rest[1], rest[2]


BIG = ("w_in", "w_proj_attn", "w_proj_sgu", "w_out", "w_ffn_gate", "w_ffn_up", "w_ffn_down")
COMM_GROUPS = (("w_in",), ("w_proj_attn", "w_proj_sgu", "w_out", "w_ffn_gate", "w_ffn_up", "w_ffn_down"))
WEIGHTS = ("norm1_g", "w_in", "sgu_ln_g", "sgu_ln_b", "w_spatial", "b_spatial", "w_proj_attn", "w_proj_sgu", "w_out",
           "norm2_g", "w_ffn_gate", "w_ffn_up", "w_ffn_down", "final_g")


def _cols_from_chips(g):
    return jnp.transpose(g, (1, 0, 2)).reshape(g.shape[1], N_CHIPS * g.shape[2])


def kernel(x, positions, norm1_g, w_in, sgu_ln_g, sgu_ln_b, w_spatial, b_spatial, w_proj_attn, w_proj_sgu, w_out, norm2_g, w_ffn_gate, w_ffn_up, w_ffn_down, final_g, loss_target, m_norm1_g, m_w_in, m_sgu_ln_g, m_sgu_ln_b, m_w_spatial, m_b_spatial, m_w_proj_attn, m_w_proj_sgu, m_w_out, m_norm2_g, m_w_ffn_gate, m_w_ffn_up, m_w_ffn_down, m_final_g, v_norm1_g, v_w_in, v_sgu_ln_g, v_sgu_ln_b, v_w_spatial, v_b_spatial, v_w_proj_attn, v_w_proj_sgu, v_w_out, v_norm2_g, v_w_ffn_gate, v_w_ffn_up, v_w_ffn_down, v_final_g):
    w = dict(norm1_g=norm1_g, w_in=w_in, sgu_ln_g=sgu_ln_g, sgu_ln_b=sgu_ln_b, w_spatial=w_spatial, b_spatial=b_spatial,
             w_proj_attn=w_proj_attn, w_proj_sgu=w_proj_sgu, w_out=w_out, norm2_g=norm2_g, w_ffn_gate=w_ffn_gate,
             w_ffn_up=w_ffn_up, w_ffn_down=w_ffn_down, final_g=final_g)
    m = dict(norm1_g=m_norm1_g, w_in=m_w_in, sgu_ln_g=m_sgu_ln_g, sgu_ln_b=m_sgu_ln_b, w_spatial=m_w_spatial,
             b_spatial=m_b_spatial, w_proj_attn=m_w_proj_attn, w_proj_sgu=m_w_proj_sgu, w_out=m_w_out, norm2_g=m_norm2_g,
             w_ffn_gate=m_w_ffn_gate, w_ffn_up=m_w_ffn_up, w_ffn_down=m_w_ffn_down, final_g=m_final_g)
    v = dict(norm1_g=v_norm1_g, w_in=v_w_in, sgu_ln_g=v_sgu_ln_g, sgu_ln_b=v_sgu_ln_b, w_spatial=v_w_spatial,
             b_spatial=v_b_spatial, w_proj_attn=v_w_proj_attn, w_proj_sgu=v_w_proj_sgu, w_out=v_w_out, norm2_g=v_norm2_g,
             w_ffn_gate=v_w_ffn_gate, w_ffn_up=v_w_ffn_up, w_ffn_down=v_w_ffn_down, final_g=v_final_g)
    t = x.shape[1]

    shards = {n: _ew(f"cast_{n}", lambda a: (a,), [w[n][0]], [BF16])[0] for n in BIG}
    late = COMM_GROUPS[1]
    k_in, n_in = shards["w_in"].shape
    *first, token = _split_start("gather_start_0", _gather_half_copies, 3 * HALF_PIECES, [shards["w_in"]],
                                 [(3, k_in // 2, n_in)])
    pending = {}

    def first_weight(after):
        srcs, filled = _split_wait("gather_wait_0", _gather_half_copies, *first, after)
        gath_in, late_shards = lax.optimization_barrier(
            (_gather_finish("gather_finish_0", srcs[0], filled[0]), [shards[n] for n in late]))
        *pending["late"], _ = _split_start(
            "gather_start_1", _gather_copies, 3, late_shards, [(N_CHIPS,) + s.shape for s in late_shards])
        return _cols_from_chips(gath_in)

    def late_weights(after):
        srcs, filled = _split_wait("gather_wait_1", _gather_copies, *pending["late"], after)
        me = 2 * lax.axis_index("x") + lax.axis_index("y")
        gath = {n: lax.dynamic_update_slice(f, s[None], (me, 0, 0)) for n, f, s in zip(late, filled, srcs)}
        return (_cols_from_chips(gath["w_proj_attn"]), _cols_from_chips(gath["w_proj_sgu"]),
                gath["w_out"].reshape(D_MODEL, D_MODEL), gath["w_ffn_gate"], gath["w_ffn_up"], gath["w_ffn_down"])

    exchanges = {}

    def on_grads(i, partials):
        if "w_out" in partials:
            partials["w_out"] = partials["w_out"].reshape(N_CHIPS, D_MODEL // N_CHIPS, D_MODEL)
        parts = [partials[n] for n in COMM_GROUPS[i]]
        *exchanges[i], started = _split_start(
            f"rs_exchange_start_{i}", _exchange_copies, N_PEERS, parts, [(N_PEERS,) + _piece_shape(p.shape) for p in parts])
        return started

    dx, _, small = _local_step(
        x[0], positions.reshape(t, 1), loss_target[0], norm1_g + token, sgu_ln_g, sgu_ln_b, w_spatial[0], b_spatial[0],
        norm2_g, final_g.reshape(1, D_MODEL), first_weight, late_weights, on_grads=on_grads)

    grads = {}
    for i in (1, 0):
        parts, filled = _split_wait(f"rs_exchange_wait_{i}", _exchange_copies, *exchanges[i], dx)
        grads.update(zip(COMM_GROUPS[i], _device_sum(f"rs_device_sum_{i}", parts, filled)))

    delta, new_m, new_v = {}, {}, {}
    for n in BIG:
        shp = w[n].shape
        flip = jnp.transpose if shp[-1] % LANES else (lambda a: a)
        outs = _adamw(f"adamw_{n}", flip(grads[n]), flip(w[n][0]), flip(m[n][0]), flip(v[n][0]))
        grads[n], delta[n], new_m[n], new_v[n] = (flip(a).reshape(shp) for a in outs)

    g_s, d_s, m_s, v_s = _small_step(small, w, m, v)
    loss = g_s["loss"][0, 0]
    for n in SMALL_PARAMS:
        shp = w[n].shape
        grads[n], delta[n], new_m[n], new_v[n] = (a[n].reshape(shp) for a in (g_s, d_s, m_s, v_s))

    return (loss, dx.reshape(x.shape), *[grads[n] for n in WEIGHTS], *[delta[n] for n in WEIGHTS],
            *[new_m[n] for n in WEIGHTS], *[new_v[n] for n in WEIGHTS])
```

```python
import functools

import numpy as np
import jax
import jax.numpy as jnp
from jax import lax
from jax.experimental import pallas as pl
from jax.experimental.pallas import tpu as pltpu

F32, BF16 = jnp.float32, jnp.bfloat16
MESH = pl.DeviceIdType.MESH

D_MODEL = 1024
HEAD_DIM = 64
ATTN_W = 512
DILATIONS = (1, 4, 16)
BLK = 128
ATTN_BLOCKS_PER_STEP = 4
ROPE_DIM = 16
ROPE_THETA = 500000.0
SGU_W = 512
SGU_CHUNK = 128
SGU_GROUPS = 8
D_FF = 2816
N_CHIPS = 4
FF_SHARD = D_FF // N_CHIPS
IN_COLS = 7680
EPS = 1e-6
NEG = -1e30
LANES = 128
VMEM_LIMIT = 52 * 1024 * 1024

ADAM_LR, ADAM_B1, ADAM_B2, ADAM_EPS, ADAM_WD, ADAM_STEP = 0.001, 0.9, 0.999, 1e-08, 0.01, 10

QKV_BLOCKS = 9


def _w_in_block(part, g):
    return part * len(DILATIONS) + g


def _cparams(ngrid):
    return pltpu.CompilerParams(dimension_semantics=("arbitrary",) * ngrid, vmem_limit_bytes=VMEM_LIMIT)


def _full(shape):
    return pl.BlockSpec(shape, lambda *_: (0,) * len(shape))


def _resident(shape):
    return pl.BlockSpec(shape, lambda *_: (0,) * len(shape), pipeline_mode=pl.Buffered(1))


NT = ((1,), (1,))
TN = ((0,), (0,))


def _rope(v, cos_t, sin_t):
    half = ROPE_DIM // 2
    first = (lax.broadcasted_iota(jnp.int32, cos_t.shape, 1) % HEAD_DIM) < half
    outs = []
    for cs in range(v.shape[1] // LANES):
        x = v[:, cs * LANES:(cs + 1) * LANES]
        partner = jnp.where(first, pltpu.roll(x, LANES - half, axis=1), pltpu.roll(x, half, axis=1))
        outs.append(x * cos_t + partner * sin_t)
    return outs[0] if len(outs) == 1 else jnp.concatenate(outs, axis=1)


def _spread_heads(v2, upper):
    other = pltpu.roll(v2, HEAD_DIM, axis=1)
    h0 = jnp.where(upper, other, v2)
    h1 = jnp.where(upper, v2, other)
    return jnp.concatenate([jnp.concatenate([h0, h0], axis=1), jnp.concatenate([h1, h1], axis=1)], axis=0)


def _sigmoid(v):
    return 0.5 * jnp.tanh(0.5 * v) + 0.5


def _rms_stats(v):
    r = lax.rsqrt(jnp.mean(v * v, axis=-1, keepdims=True) + EPS)
    return v * r, r


def _rms_bwd(dy, xhat, r, g):
    dxh = dy * g
    return r * (dxh - xhat * jnp.mean(dxh * xhat, axis=-1, keepdims=True))


def _head_sum_matrix():
    idx = np.arange(ATTN_W) // HEAD_DIM
    return jnp.asarray((idx[:, None] == idx[None, :]).astype(np.float32), dtype=BF16)


def _group_sum(v, e):
    hi = v.astype(BF16)
    lo = (v - hi.astype(F32)).astype(BF16)
    return jnp.dot(hi, e, preferred_element_type=F32) + jnp.dot(lo, e, preferred_element_type=F32)


TILE = 512


def _to_slabs(slab_ref, v):
    for cs in range(slab_ref.shape[0]):
        slab_ref[cs] = v[:, cs * LANES:(cs + 1) * LANES]


def _from_slabs(slab_ref):
    return jnp.concatenate([slab_ref[cs] for cs in range(slab_ref.shape[0])], axis=1)


def _class_rows(slab_ref, r, dil):
    n = slab_ref.shape[1] // dil
    return jnp.concatenate([slab_ref.at[cs][pl.ds(r, n, stride=dil), :] for cs in range(slab_ref.shape[0])], axis=1)


def _put_class_rows(slab_ref, r, dil, v):
    n = slab_ref.shape[1] // dil
    for cs in range(slab_ref.shape[0]):
        slab_ref.at[cs][pl.ds(r, n, stride=dil), :] = v[:, cs * LANES:(cs + 1) * LANES]


def _natural_from_group(slab_ref, grp_ref):
    dil = grp_ref.shape[0]
    for r in range(dil):
        _put_class_rows(slab_ref, r, dil, grp_ref[r].astype(F32))
    return _from_slabs(slab_ref)


def _group_from_natural(slab_ref, grp_ref, v):
    dil = grp_ref.shape[0]
    _to_slabs(slab_ref, v)
    for r in range(dil):
        grp_ref[r] = _class_rows(slab_ref, r, dil).astype(grp_ref.dtype)


def _group_spec(dil, tile, width):
    return pl.BlockSpec((dil, tile // dil, width), lambda i, *_: (0, i, 0))


def _slabs(tile, width):
    return pltpu.VMEM((width // LANES, tile, LANES), F32)


def _rope_consts():
    lane = np.arange(LANES) % HEAD_DIM
    fi = lane % (ROPE_DIM // 2)
    invf = np.where(lane < ROPE_DIM, ROPE_THETA ** (-(2.0 * fi) / ROPE_DIM), 0.0)
    sgn = np.where(lane < ROPE_DIM // 2, -1.0, np.where(lane < ROPE_DIM, 1.0, 0.0))
    return (jnp.asarray(invf.astype(np.float32)).reshape(1, LANES), jnp.asarray(sgn.astype(np.float32)).reshape(1, LANES))


def _rope_tables(pos_col):
    t = pos_col.shape[0]
    tile = min(t, TILE)
    invf, sgn = _rope_consts()

    def body(p_ref, f_ref, s_ref, c0, s0, c1, s1, c2, s2, slab_c, slab_s):
        ang = p_ref[...].astype(F32) * f_ref[...]
        cos, sin = jnp.cos(ang), jnp.sin(ang) * s_ref[...]
        c0[...] = cos
        s0[...] = sin
        _group_from_natural(slab_c, c1, cos)
        _group_from_natural(slab_s, s1, sin)
        for r in range(DILATIONS[2]):
            c2[r] = _class_rows(slab_c, r, DILATIONS[2])
            s2[r] = _class_rows(slab_s, r, DILATIONS[2])

    nat = pl.BlockSpec((tile, LANES), lambda i: (i, 0))
    specs, shapes = [nat, nat], [(t, LANES)] * 2
    for d in DILATIONS[1:]:
        specs += [_group_spec(d, tile, LANES)] * 2
        shapes += [(d, t // d, LANES)] * 2
    outs = pl.pallas_call(
        body, grid=(t // tile,),
        in_specs=[pl.BlockSpec((tile, 1), lambda i: (i, 0)), _full((1, LANES)), _full((1, LANES))],
        out_specs=specs, out_shape=[jax.ShapeDtypeStruct(s, F32) for s in shapes],
        scratch_shapes=[_slabs(tile, LANES)] * 2,
        compiler_params=_cparams(1), name="rope_tables")(pos_col, invf, sgn)
    return [(outs[2 * g].reshape(t, LANES), outs[2 * g + 1].reshape(t, LANES)) for g in range(len(DILATIONS))]


def _norm_fwd(x, g):
    t = x.shape[0]
    tile = min(t, TILE)

    def body(x_ref, g_ref, h0_ref, h1_ref, h2_ref, slab):
        xhat, _ = _rms_stats(x_ref[...])
        hn = xhat * g_ref[...]
        h0_ref[...] = hn.astype(BF16)
        _group_from_natural(slab, h1_ref, hn)
        for r in range(DILATIONS[2]):
            h2_ref[r] = _class_rows(slab, r, DILATIONS[2]).astype(BF16)

    nat = pl.BlockSpec((tile, D_MODEL), lambda i: (i, 0))
    return pl.pallas_call(
        body, grid=(t // tile,),
        in_specs=[nat, _full((1, D_MODEL))],
        out_specs=[nat] + [_group_spec(d, tile, D_MODEL) for d in DILATIONS[1:]],
        out_shape=[jax.ShapeDtypeStruct((t, D_MODEL), BF16)]
        + [jax.ShapeDtypeStruct((d, t // d, D_MODEL), BF16) for d in DILATIONS[1:]],
        scratch_shapes=[_slabs(tile, D_MODEL)],
        compiler_params=_cparams(1), name="norm1_fwd")(x, g)


GU_COLS = 3072
GROUP_COLS = 1536
GU_HALF = GU_COLS // 2


def _w_in_spec(width, block):
    return pl.BlockSpec((D_MODEL, width), lambda i: (0, block), pipeline_mode=pl.Buffered(1))


def _gu_w_specs():
    first = QKV_BLOCKS * ATTN_W // GU_HALF
    return [_w_in_spec(GU_HALF, first), _w_in_spec(GU_HALF, first + 1)]


def _group_w_specs(g):
    return [_w_in_spec(ATTN_W, _w_in_block(part, g)) for part in range(3)]


def _in_proj(hs, w_in, tables):
    t = hs[0].shape[0]
    tm = min(t, 1024)

    def body_gu(h_ref, w0_ref, w1_ref, o_ref):
        h = h_ref[...]
        o_ref[:, 0:GU_HALF] = jnp.dot(h, w0_ref[...], preferred_element_type=F32).astype(BF16)
        o_ref[:, GU_HALF:] = jnp.dot(h, w1_ref[...], preferred_element_type=F32).astype(BF16)

    gu = _token_call("in_proj_gates_uv", body_gu, t, tm,
                     [(hs[0], _rows_spec(tm, D_MODEL))] + [(w_in, s) for s in _gu_w_specs()],
                     [((t, GU_COLS), BF16, _rows_spec(tm, GU_COLS))])[0]

    qkvs = []
    for g in range(len(DILATIONS)):

        def body_qkv(h_ref, wq_ref, wk_ref, wv_ref, cos_ref, sin_ref, o_ref):
            h = h_ref[...]
            cos_w, sin_w = cos_ref[...], sin_ref[...]
            q = jnp.dot(h, wq_ref[...], preferred_element_type=F32)
            o_ref[:, 0:ATTN_W] = (_rope(q, cos_w, sin_w) * HEAD_DIM ** -0.5).astype(BF16)
            k = jnp.dot(h, wk_ref[...], preferred_element_type=F32)
            o_ref[:, ATTN_W:2 * ATTN_W] = _rope(k, cos_w, sin_w).astype(BF16)
            o_ref[:, 2 * ATTN_W:] = jnp.dot(h, wv_ref[...], preferred_element_type=F32).astype(BF16)

        cos_t, sin_t = tables[g]
        qkvs.append(_token_call(
            f"in_proj_qkv_g{g}", body_qkv, t, tm,
            [(hs[g].reshape(t, D_MODEL), _rows_spec(tm, D_MODEL))] + [(w_in, s) for s in _group_w_specs(g)]
            + [(cos_t, _rows_spec(tm, LANES)), (sin_t, _rows_spec(tm, LANES))],
            [((t, GROUP_COLS), BF16, _rows_spec(tm, GROUP_COLS))])[0])
    return gu, qkvs


def _attn_masks(n):
    row = lax.broadcasted_iota(jnp.int32, (2 * BLK, 2 * BLK), 0) % BLK
    col = lax.broadcasted_iota(jnp.int32, (2 * BLK, 2 * BLK), 1)
    diff = BLK + row - col
    valid = (diff >= 0) & (diff <= BLK) & ((col >= BLK) | (n > 0))
    upper = lax.broadcasted_iota(jnp.int32, (BLK, LANES), 1) >= HEAD_DIM
    return valid, upper


def _stack_heads(v2, upper):
    zero = jnp.zeros_like(v2)
    return jnp.concatenate([jnp.where(upper, zero, v2), jnp.where(upper, v2, zero)], axis=0)


def _unstack_heads(v, upper):
    return jnp.where(upper, v[BLK:], v[:BLK])


def _attn_fwd(qkv, g, dil):
    t = qkv.shape[0]
    length = t // dil
    nb = length // BLK
    per_step = min(nb, ATTN_BLOCKS_PER_STEP)
    view = qkv.reshape(dil, length, GROUP_COLS)

    def body(q_ref, kc_ref, kp_ref, vc_ref, vp_ref, o_ref, l_ref, kwin, vwin):
        n = pl.program_id(1)
        kwin[0:BLK] = kp_ref[...]
        kwin[BLK:] = kc_ref[...]
        vwin[0:BLK] = vp_ref[...]
        vwin[BLK:] = vc_ref[...]

        def block(b, carry):
            valid, upper = _attn_masks(n * per_step + b)
            rows = pl.ds(pl.multiple_of(b * BLK, BLK), BLK)
            window = pl.ds(pl.multiple_of(b * BLK, BLK), 2 * BLK)
            slabs = [slice(p * LANES, (p + 1) * LANES) for p in range(ATTN_W // LANES)]
            ss = [lax.dot_general(_stack_heads(q_ref[rows, sl], upper), kwin[window, sl], (NT, ((), ())),
                                  preferred_element_type=F32) for sl in slabs]
            soft = []
            for s in ss:
                s = jnp.where(valid, s, NEG)
                m = jnp.max(s, axis=1, keepdims=True)
                pe = jnp.exp(s - m)
                soft.append((m, pe, jnp.sum(pe, axis=1, keepdims=True)))
            for sl, (m, pe, den) in zip(slabs, soft):
                o = jnp.dot(pe.astype(BF16), vwin[window, sl], preferred_element_type=F32) / den
                lse = jnp.broadcast_to(m + jnp.log(den), (2 * BLK, LANES))
                o_ref[rows, sl] = _unstack_heads(o, upper).astype(BF16)
                l_ref[rows, sl] = _unstack_heads(lse, upper)
            return carry

        lax.fori_loop(0, per_step, block, 0)

    rows = per_step * BLK
    cur = lambda part: pl.BlockSpec((None, rows, ATTN_W), lambda r, n: (r, n, part))
    prev = lambda part: pl.BlockSpec((None, BLK, ATTN_W), lambda r, n: (r, jnp.maximum(n * per_step - 1, 0), part))
    out_spec = pl.BlockSpec((None, rows, ATTN_W), lambda r, n: (r, n, 0))
    return pl.pallas_call(
        body, grid=(dil, nb // per_step),
        in_specs=[cur(0), cur(1), prev(1), cur(2), prev(2)],
        out_specs=[out_spec, out_spec],
        out_shape=[jax.ShapeDtypeStruct((dil, length, ATTN_W), BF16), jax.ShapeDtypeStruct((dil, length, ATTN_W), F32)],
        scratch_shapes=[pltpu.VMEM((rows + BLK, ATTN_W), BF16)] * 2,
        compiler_params=_cparams(2), name=f"attn_fwd_g{g}")(view, view, view, view, view)


def _alphas(l0, l1, l2):
    m = jnp.maximum(jnp.maximum(l0, l1), l2)
    e0, e1, e2 = jnp.exp(l0 - m), jnp.exp(l1 - m), jnp.exp(l2 - m)
    inv = 1.0 / (e0 + e1 + e2)
    return e0 * inv, e1 * inv, e2 * inv


def _natural_group_values(o_refs, l_refs, slabs):
    os_ = [o_refs[0][0].astype(F32)] + [_natural_from_group(slabs[2 * g - 2], o_refs[g]) for g in (1, 2)]
    ls_ = [l_refs[0][0]] + [_natural_from_group(slabs[2 * g - 1], l_refs[g]) for g in (1, 2)]
    return os_, ls_


def _combine_fwd(os_, ls_):
    t = os_[0].shape[1]
    tile = min(t, TILE)

    def body(o0, o1, o2, l0, l1, l2, a_ref, *slabs):
        ov, lv = _natural_group_values((o0, o1, o2), (l0, l1, l2), slabs)
        a0, a1, a2 = _alphas(*lv)
        a_ref[...] = (a0 * ov[0] + a1 * ov[1] + a2 * ov[2]).astype(BF16)

    specs = [_group_spec(d, tile, ATTN_W) for d in DILATIONS]
    return pl.pallas_call(
        body, grid=(t // tile,), in_specs=specs * 2, out_specs=pl.BlockSpec((tile, ATTN_W), lambda i: (i, 0)),
        out_shape=jax.ShapeDtypeStruct((t, ATTN_W), BF16),
        scratch_shapes=[_slabs(tile, ATTN_W)] * 4,
        compiler_params=_cparams(1), name="combine_fwd")(*os_, *ls_)


def _combine_bwd(dattn, os_, ls_):
    t = dattn.shape[0]
    tile = min(t, TILE)
    e = _head_sum_matrix()

    def body(d_ref, o0, o1, o2, l0, l1, l2, e_ref, do0, do1, do2, c0, c1, c2, *slabs):
        ov, lv = _natural_group_values((o0, o1, o2), (l0, l1, l2), slabs)
        alphas = _alphas(*lv)
        d = d_ref[...]
        attn = alphas[0] * ov[0] + alphas[1] * ov[1] + alphas[2] * ov[2]
        s = _group_sum(d * attn, e_ref[...])
        do0[0] = (alphas[0] * d).astype(BF16)
        c0[0] = -alphas[0] * s
        for g, do_ref, c_ref in ((1, do1, c1), (2, do2, c2)):
            _group_from_natural(slabs[2 * g - 2], do_ref, alphas[g] * d)
            _group_from_natural(slabs[2 * g - 1], c_ref, -alphas[g] * s)

    specs = [_group_spec(d, tile, ATTN_W) for d in DILATIONS]
    shapes = [(d, t // d, ATTN_W) for d in DILATIONS]
    outs = pl.pallas_call(
        body, grid=(t // tile,),
        in_specs=[pl.BlockSpec((tile, ATTN_W), lambda i: (i, 0))] + specs * 2 + [_full((ATTN_W, ATTN_W))],
        out_specs=specs * 2,
        out_shape=[jax.ShapeDtypeStruct(s, BF16) for s in shapes] + [jax.ShapeDtypeStruct(s, F32) for s in shapes],
        scratch_shapes=[_slabs(tile, ATTN_W)] * 4,
        compiler_params=_cparams(1), name="combine_bwd")(dattn, *os_, *ls_, e)
    return outs[:3], outs[3:]


def _attn_bwd(qkv, do, cc, lse, cos_t, sin_t, g, dil):
    t = qkv.shape[0]
    length = t // dil
    nb = length // BLK
    per_step = min(nb, ATTN_BLOCKS_PER_STEP)
    nsteps = nb // per_step
    rows_per_step = per_step * BLK
    qkv_v = qkv.reshape(dil, length, GROUP_COLS)
    cos_v, sin_v = (a.reshape(dil, length, LANES) for a in (cos_t, sin_t))
    scale = HEAD_DIM ** -0.5
    dq_cols, dk_cols, dv_cols = (slice(i * ATTN_W, (i + 1) * ATTN_W) for i in range(3))

    def body(q_ref, kc_ref, kp_ref, vc_ref, vp_ref, do_ref, c_ref, l_ref, cosc, sinc, cosp, sinp,
             out_ref, acc, kwin, vwin, cwin, swin):
        n = pl.program_id(1)

        def one_block(b):
            valid, upper = _attn_masks(n * per_step + b)
            start = b * BLK if isinstance(b, int) else pl.multiple_of(b * BLK, BLK)
            rows, before, window = pl.ds(start, BLK), pl.ds(start, BLK), pl.ds(start, 2 * BLK)
            own = pl.ds(start + BLK, BLK)
            dq_parts, dkp_parts, dkc_parts, dvp_parts, dvc_parts = [], [], [], [], []
            npairs = ATTN_W // LANES
            slabs = [slice(p * LANES, (p + 1) * LANES) for p in range(npairs)]
            qss = [_stack_heads(q_ref[rows, sl], upper) for sl in slabs]
            doss = [_stack_heads(do_ref[rows, sl], upper) for sl in slabs]
            ss = [lax.dot_general(qss[p], kwin[window, slabs[p]], (NT, ((), ())), preferred_element_type=F32) for p in range(npairs)]
            dpvs = [lax.dot_general(doss[p], vwin[window, slabs[p]], (NT, ((), ())), preferred_element_type=F32)
                    for p in range(npairs)]
            pes = [jnp.exp(jnp.where(valid, ss[p], NEG) - _spread_heads(l_ref[rows, slabs[p]], upper)) for p in range(npairs)]
            dss = [(pes[p] * (dpvs[p] + _spread_heads(c_ref[rows, slabs[p]], upper))).astype(BF16) for p in range(npairs)]
            for p in range(npairs):
                qs, dos, ds = qss[p], doss[p], dss[p]
                dq2 = _unstack_heads(jnp.dot(ds, kwin[window, slabs[p]], preferred_element_type=F32), upper)
                dk2 = lax.dot_general(ds, qs, (TN, ((), ())), preferred_element_type=F32)
                dv2 = lax.dot_general(pes[p].astype(BF16), dos, (TN, ((), ())), preferred_element_type=F32)
                dq_parts.append(dq2)
                dkp_parts.append(dk2[:BLK])
                dkc_parts.append(dk2[BLK:])
                dvp_parts.append(dv2[:BLK])
                dvc_parts.append(dv2[BLK:])
            dq = _rope(jnp.concatenate(dq_parts, axis=1) * scale, cwin[own, :], swin[own, :])
            dkc = _rope(jnp.concatenate(dkc_parts, axis=1), cwin[own, :], swin[own, :])
            dkp = _rope(jnp.concatenate(dkp_parts, axis=1), cwin[before, :], swin[before, :])
            return dq, dkp, dkc, jnp.concatenate(dvp_parts, axis=1), jnp.concatenate(dvc_parts, axis=1)

        @pl.when(n < nsteps)
        def _():
            kwin[0:BLK] = kp_ref[...]
            kwin[BLK:] = kc_ref[...]
            vwin[0:BLK] = vp_ref[...]
            vwin[BLK:] = vc_ref[...]
            cwin[0:BLK] = cosp[...]
            cwin[BLK:] = cosc[...]
            swin[0:BLK] = -sinp[...]
            swin[BLK:] = -sinc[...]
            dq, dkp, dkc, dvp, dvc = one_block(0)
            last = slice(rows_per_step - BLK, rows_per_step)

            @pl.when(n > 0)
            def _():
                if per_step > 1:
                    out_ref[0:rows_per_step - BLK, :] = acc[0:rows_per_step - BLK, :].astype(BF16)
                out_ref[last, dq_cols] = acc[last, dq_cols].astype(BF16)
                out_ref[last, dk_cols] = (acc[last, dk_cols] + dkp).astype(BF16)
                out_ref[last, dv_cols] = (acc[last, dv_cols] + dvp).astype(BF16)

            acc[0:BLK, dq_cols] = dq
            acc[0:BLK, dk_cols] = dkc
            acc[0:BLK, dv_cols] = dvc

            def later(b, carry):
                dq, dkp, dkc, dvp, dvc = one_block(b)
                start = pl.multiple_of(b * BLK, BLK)
                before, rows = pl.ds(start - BLK, BLK), pl.ds(start, BLK)
                acc[before, dk_cols] += dkp
                acc[before, dv_cols] += dvp
                acc[rows, dq_cols] = dq
                acc[rows, dk_cols] = dkc
                acc[rows, dv_cols] = dvc
                return carry

            lax.fori_loop(1, per_step, later, 0)

        @pl.when(n == flush_at)
        def _():
            out_ref[...] = acc[...].astype(BF16)

    flush_at = nsteps - 1 if nsteps == 1 else nsteps
    out_lag = 0 if nsteps == 1 else 1
    nc = lambda n: jnp.minimum(n, nsteps - 1)
    npv = lambda n: jnp.maximum(jnp.minimum(n, nsteps - 1) * per_step - 1, 0)
    cur = lambda part: pl.BlockSpec((None, rows_per_step, ATTN_W), lambda r, n: (r, nc(n), part))
    prev = lambda part: pl.BlockSpec((None, BLK, ATTN_W), lambda r, n: (r, npv(n), part))
    row = pl.BlockSpec((None, rows_per_step, ATTN_W), lambda r, n: (r, nc(n), 0))
    tab_c = pl.BlockSpec((None, rows_per_step, LANES), lambda r, n: (r, nc(n), 0))
    tab_p = pl.BlockSpec((None, BLK, LANES), lambda r, n: (r, npv(n), 0))
    out_spec = pl.BlockSpec((None, rows_per_step, GROUP_COLS), lambda r, n: (r, jnp.maximum(n - out_lag, 0), 0))
    out = pl.pallas_call(
        body, grid=(dil, nsteps + out_lag),
        in_specs=[cur(0), cur(1), prev(1), cur(2), prev(2), row, row, row, tab_c, tab_c, tab_p, tab_p],
        out_specs=out_spec,
        out_shape=jax.ShapeDtypeStruct((dil, length, GROUP_COLS), BF16),
        scratch_shapes=[pltpu.VMEM((rows_per_step, GROUP_COLS), F32)]
        + [pltpu.VMEM((rows_per_step + BLK, ATTN_W), BF16)] * 2 + [pltpu.VMEM((rows_per_step + BLK, LANES), F32)] * 2,
        compiler_params=_cparams(2), name=f"attn_bwd_g{g}")(
            qkv_v, qkv_v, qkv_v, qkv_v, qkv_v, do, cc, lse, cos_v, sin_v, cos_v, sin_v)
    return out.reshape(t, GROUP_COLS)


SQRT_HALF = 0.7071067811865476
INV_SQRT_2PI = 0.3989422804014327


def _sgu_core(uv, g, b, w_ref, bias):
    cdf = 0.5 * (1.0 + lax.erf(uv * SQRT_HALF))
    z = uv * cdf
    u, v = z[:, :SGU_W], z[:, SGU_W:]
    mu = jnp.mean(v, axis=1, keepdims=True)
    xc = v - mu
    rs = lax.rsqrt(jnp.mean(xc * xc, axis=1, keepdims=True) + EPS)
    xhat = xc * rs
    vn = xhat * g + b
    row = lax.broadcasted_iota(jnp.int32, (SGU_CHUNK, SGU_CHUNK), 0)
    col = lax.broadcasted_iota(jnp.int32, (SGU_CHUNK, SGU_CHUNK), 1)
    tril = row >= col
    upper = lax.broadcasted_iota(jnp.int32, (SGU_CHUNK, LANES), 1) >= SGU_W // SGU_GROUPS
    ws, vlo, vhi, mixed = [], [], [], []
    for pr in range(SGU_W // LANES):
        sl = slice(pr * LANES, (pr + 1) * LANES)
        w0 = jnp.where(tril, w_ref[2 * pr], 0.0).astype(BF16)
        w1 = jnp.where(tril, w_ref[2 * pr + 1], 0.0).astype(BF16)
        vn2 = vn[:, sl]
        lo = jnp.where(upper, 0.0, vn2).astype(BF16)
        hi = jnp.where(upper, vn2, 0.0).astype(BF16)
        mixed.append(jnp.dot(w0, lo, preferred_element_type=F32) + jnp.dot(w1, hi, preferred_element_type=F32)
                     + bias[:, sl])
        ws.append((w0, w1))
        vlo.append(lo)
        vhi.append(hi)
    return cdf, u, xhat, rs, jnp.concatenate(mixed, axis=1), ws, vlo, vhi, tril, upper


SGU_STEP = 4 * SGU_CHUNK


def _for_chunks(step_rows, fn):
    def one(ci, carry):
        fn(pl.ds(pl.multiple_of(ci * SGU_CHUNK, SGU_CHUNK), SGU_CHUNK))
        return carry

    lax.fori_loop(0, step_rows // SGU_CHUNK, one, 0)


def _sgu_fwd(gu, ln_g, ln_b, w_s, bias_exp):
    t = gu.shape[0]
    step = min(t, SGU_STEP)

    def body(uv_ref, g_ref, b_ref, w_ref, bias_ref, o_ref):
        def chunk(rows):
            _, u, _, _, mixed, *_ = _sgu_core(uv_ref[rows, :].astype(F32), g_ref[...], b_ref[...], w_ref, bias_ref[...])
            o_ref[rows, :] = (u * mixed).astype(BF16)

        _for_chunks(step, chunk)

    return pl.pallas_call(
        body, grid=(t // step,),
        in_specs=[pl.BlockSpec((step, 2 * SGU_W), lambda n: (n, 0)), _full((1, SGU_W)), _full((1, SGU_W)),
                  _full((SGU_GROUPS, SGU_CHUNK, SGU_CHUNK)), _full((SGU_CHUNK, SGU_W))],
        out_specs=pl.BlockSpec((step, SGU_W), lambda n: (n, 0)),
        out_shape=jax.ShapeDtypeStruct((t, SGU_W), BF16),
        compiler_params=_cparams(1), name="sgu_fwd")(gu, ln_g, ln_b, w_s, bias_exp)


def _sgu_bwd(dproj, gu, dsgu, ln_g, ln_b, w_s, bias_exp):
    t = gu.shape[0]
    step = min(t, SGU_STEP)
    nsteps = t // step
    e = _head_sum_matrix()

    def body(dp_in, uv_ref, ds_ref, g_ref, b_ref, w_ref, bias_ref, e_ref, out_ref, dw_ref, dbias_ref, dg_ref, db_ref):
        n = pl.program_id(0)

        @pl.when(n == 0)
        def _():
            dw_ref[...] = jnp.zeros(dw_ref.shape, F32)
            dbias_ref[...] = jnp.zeros(dbias_ref.shape, F32)
            dg_ref[...] = jnp.zeros(dg_ref.shape, F32)
            db_ref[...] = jnp.zeros(db_ref.shape, F32)

        _for_chunks(step, functools.partial(chunk, uv_ref, ds_ref, g_ref, b_ref, w_ref, bias_ref, out_ref, dw_ref, dbias_ref,
                                            dg_ref, db_ref))

        @pl.when(n == nsteps - 1)
        def _():
            dbias_ref[...] = _group_sum(dbias_ref[...], e_ref[...])

    def chunk(uv_ref, ds_ref, g_ref, b_ref, w_ref, bias_ref, out_ref, dw_ref, dbias_ref, dg_ref, db_ref, rows):
        uv = uv_ref[rows, :].astype(F32)
        g = g_ref[...]
        cdf, u, xhat, rs, mixed, ws, vlo, vhi, tril, upper = _sgu_core(uv, g, b_ref[...], w_ref, bias_ref[...])
        dsg = ds_ref[rows, :]
        du = dsg * mixed
        dmixed = dsg * u
        dbias_ref[...] += dmixed
        dvn = []
        for pr in range(SGU_W // LANES):
            sl = slice(pr * LANES, (pr + 1) * LANES)
            dm2 = dmixed[:, sl]
            dlo = jnp.where(upper, 0.0, dm2).astype(BF16)
            dhi = jnp.where(upper, dm2, 0.0).astype(BF16)
            w0, w1 = ws[pr]
            dvn.append(lax.dot_general(w0, dlo, (TN, ((), ())), preferred_element_type=F32)
                       + lax.dot_general(w1, dhi, (TN, ((), ())), preferred_element_type=F32))
            dw0 = lax.dot_general(dlo, vlo[pr], (NT, ((), ())), preferred_element_type=F32)
            dw1 = lax.dot_general(dhi, vhi[pr], (NT, ((), ())), preferred_element_type=F32)
            dw_ref[2 * pr] += jnp.where(tril, dw0, 0.0)
            dw_ref[2 * pr + 1] += jnp.where(tril, dw1, 0.0)
        dvn = jnp.concatenate(dvn, axis=1)
        dg_ref[...] += jnp.sum(dvn * xhat, axis=0, keepdims=True)
        db_ref[...] += jnp.sum(dvn, axis=0, keepdims=True)
        dxh = dvn * g
        dv = rs * (dxh - jnp.mean(dxh, axis=1, keepdims=True) - xhat * jnp.mean(dxh * xhat, axis=1, keepdims=True))
        dz = jnp.concatenate([du, dv], axis=1)
        dgelu = cdf + uv * (INV_SQRT_2PI * jnp.exp(-0.5 * uv * uv))
        out_ref[rows, :] = (dz * dgelu).astype(BF16)

    outs = pl.pallas_call(
        body, grid=(nsteps,),
        in_specs=[pl.BlockSpec(memory_space=pl.ANY), pl.BlockSpec((step, 2 * SGU_W), lambda n: (n, 0)),
                  pl.BlockSpec((step, SGU_W), lambda n: (n, 0)), _full((1, SGU_W)), _full((1, SGU_W)),
                  _full((SGU_GROUPS, SGU_CHUNK, SGU_CHUNK)), _full((SGU_CHUNK, SGU_W)), _full((ATTN_W, ATTN_W))],
        out_specs=[pl.BlockSpec((step, 2 * SGU_W), lambda n: (n, 0)), _full((SGU_GROUPS, SGU_CHUNK, SGU_CHUNK)),
                   _full((SGU_CHUNK, SGU_W)), _full((1, SGU_W)), _full((1, SGU_W))],
        out_shape=[jax.ShapeDtypeStruct(dproj.shape, BF16), jax.ShapeDtypeStruct((SGU_GROUPS, SGU_CHUNK, SGU_CHUNK), F32),
                   jax.ShapeDtypeStruct((SGU_CHUNK, SGU_W), F32), jax.ShapeDtypeStruct((1, SGU_W), F32),
                   jax.ShapeDtypeStruct((1, SGU_W), F32)],
        input_output_aliases={0: 0},
        compiler_params=_cparams(1), name="sgu_bwd")(dproj, gu, dsgu, ln_g, ln_b, w_s, bias_exp, e)
    return outs


def _merge_fwd(attn, sgu, gu, x, w_pa, w_ps, w_out, g2):
    t = x.shape[0]
    tm = min(t, 512)

    def body(a_ref, s_ref, ga_ref, gb_ref, x_ref, wpa, wps, wo, g_ref, pa_ref, ps_ref, m_ref, x1_ref, h2_ref):
        pa = jnp.dot(a_ref[...], wpa[...], preferred_element_type=F32)
        ps = jnp.dot(s_ref[...], wps[...], preferred_element_type=F32)
        merged = (_sigmoid(ga_ref[...].astype(F32)) * pa + _sigmoid(gb_ref[...].astype(F32)) * ps).astype(BF16)
        x1 = x_ref[...] + jnp.dot(merged, wo[...], preferred_element_type=F32)
        xhat, _ = _rms_stats(x1)
        pa_ref[...] = pa.astype(BF16)
        ps_ref[...] = ps.astype(BF16)
        m_ref[...] = merged
        x1_ref[...] = x1
        h2_ref[...] = (xhat * g_ref[...]).astype(BF16)

    half = pl.BlockSpec((tm, ATTN_W), lambda i: (i, 0))
    full = pl.BlockSpec((tm, D_MODEL), lambda i: (i, 0))
    return pl.pallas_call(
        body, grid=(t // tm,),
        in_specs=[half, half, pl.BlockSpec((tm, D_MODEL), lambda i: (i, 1)), pl.BlockSpec((tm, D_MODEL), lambda i: (i, 2)),
                  full, _resident((ATTN_W, D_MODEL)), _resident((SGU_W, D_MODEL)), _resident((D_MODEL, D_MODEL)),
                  _full((1, D_MODEL))],
        out_specs=[full] * 5,
        out_shape=[jax.ShapeDtypeStruct((t, D_MODEL), BF16), jax.ShapeDtypeStruct((t, D_MODEL), BF16),
                   jax.ShapeDtypeStruct((t, D_MODEL), BF16), jax.ShapeDtypeStruct((t, D_MODEL), F32),
                   jax.ShapeDtypeStruct((t, D_MODEL), BF16)],
        compiler_params=_cparams(1), name="merge_fwd")(attn, sgu, gu, gu, x, w_pa, w_ps, w_out, g2)


def _merge_bwd(dx1b, gu, pa, ps, w_pa, w_ps, w_out):
    t = dx1b.shape[0]
    tm = min(t, 512)

    def body(d_ref, ga_ref, gb_ref, pa_ref, ps_ref, wpa, wps, wo, out_ref, dpa_ref, dps_ref, da_ref, dsg_ref):
        dm = lax.dot_general(d_ref[...], wo[...], (NT, ((), ())), preferred_element_type=F32)
        sa, sb = _sigmoid(ga_ref[...].astype(F32)), _sigmoid(gb_ref[...].astype(F32))
        dpa = (dm * sa).astype(BF16)
        dps = (dm * sb).astype(BF16)
        out_ref[:, 0:D_MODEL] = jnp.zeros((tm, D_MODEL), BF16)
        out_ref[:, D_MODEL:2 * D_MODEL] = (dm * pa_ref[...].astype(F32) * sa * (1.0 - sa)).astype(BF16)
        out_ref[:, 2 * D_MODEL:GU_COLS] = (dm * ps_ref[...].astype(F32) * sb * (1.0 - sb)).astype(BF16)
        dpa_ref[...] = dpa
        dps_ref[...] = dps
        da_ref[...] = lax.dot_general(dpa, wpa[...], (NT, ((), ())), preferred_element_type=F32)
        dsg_ref[...] = lax.dot_general(dps, wps[...], (NT, ((), ())), preferred_element_type=F32)

    half = pl.BlockSpec((tm, ATTN_W), lambda i: (i, 0))
    full = pl.BlockSpec((tm, D_MODEL), lambda i: (i, 0))
    return pl.pallas_call(
        body, grid=(t // tm,),
        in_specs=[full, pl.BlockSpec((tm, D_MODEL), lambda i: (i, 1)),
                  pl.BlockSpec((tm, D_MODEL), lambda i: (i, 2)), full, full,
                  _resident((ATTN_W, D_MODEL)), _resident((SGU_W, D_MODEL)), _resident((D_MODEL, D_MODEL))],
        out_specs=[pl.BlockSpec((tm, GU_COLS), lambda i: (i, 0)), full, full, half, half],
        out_shape=[jax.ShapeDtypeStruct((t, GU_COLS), BF16), jax.ShapeDtypeStruct((t, D_MODEL), BF16),
                   jax.ShapeDtypeStruct((t, D_MODEL), BF16), jax.ShapeDtypeStruct((t, ATTN_W), F32),
                   jax.ShapeDtypeStruct((t, SGU_W), F32)],
        compiler_params=_cparams(1), name="merge_bwd")(dx1b, gu, gu, pa, ps, w_pa, w_ps, w_out)


def _token_call(name, body, t, tm, ins, outs, reds=(), scratch=()):
    return pl.pallas_call(
        body, grid=(t // tm,), in_specs=[s for _, s in ins],
        out_specs=[o[2] for o in outs] + [_full(r) for r in reds],
        out_shape=[jax.ShapeDtypeStruct(o[0], o[1]) for o in outs] + [jax.ShapeDtypeStruct(r, F32) for r in reds],
        scratch_shapes=list(scratch), compiler_params=_cparams(1), name=name)(*[a for a, _ in ins])


def _rows_spec(tm, width):
    return pl.BlockSpec((tm, width), lambda i: (i, 0))


def _chips_spec(tm):
    return pl.BlockSpec((N_CHIPS, tm, FF_SHARD), lambda i: (0, i, 0))


def _zero_at_start(*refs):
    @pl.when(pl.program_id(0) == 0)
    def _():
        for r in refs:
            r[...] = jnp.zeros(r.shape, r.dtype)


def _ffn_fwd(h2, w_g, w_u):
    t = h2.shape[0]
    tm = min(t, 512)

    def body(h_ref, wg_ref, wu_ref, a_ref, b_ref, ff_ref):
        h = h_ref[...]
        for s in range(N_CHIPS):
            a = jnp.dot(h, wg_ref[s], preferred_element_type=F32)
            b = jnp.dot(h, wu_ref[s], preferred_element_type=F32)
            a_ref[s] = a.astype(BF16)
            b_ref[s] = b.astype(BF16)
            ff_ref[s] = (a * _sigmoid(a) * b).astype(BF16)

    shp = (N_CHIPS, t, FF_SHARD)
    w_spec = _resident((N_CHIPS, D_MODEL, FF_SHARD))
    return _token_call("ffn_fwd", body, t, tm, [(h2, _rows_spec(tm, D_MODEL)), (w_g, w_spec), (w_u, w_spec)],
                       [(shp, BF16, _chips_spec(tm))] * 3)


def _ffn_down_loss(ff, w_d, x1, tgt, gf):
    t = x1.shape[0]
    tm = min(t, 512)

    def body(ff_ref, wd_ref, x1_ref, tgt_ref, g_ref, dx2_ref, dx2b_ref, loss_ref, dgf_ref):
        _zero_at_start(loss_ref, dgf_ref)
        acc = jnp.dot(ff_ref[0], wd_ref[0], preferred_element_type=F32)
        for s in range(1, N_CHIPS):
            acc = acc + jnp.dot(ff_ref[s], wd_ref[s], preferred_element_type=F32)
        x2 = x1_ref[...] + acc
        g = g_ref[...]
        xhat, rr = _rms_stats(x2)
        diff = xhat * g - tgt_ref[...]
        rows = jnp.sum(diff * diff, axis=1, keepdims=True)
        loss_ref[...] += jnp.broadcast_to(jnp.sum(rows, axis=0, keepdims=True) * (0.5 / D_MODEL), (1, LANES))
        dy = diff * (1.0 / D_MODEL)
        dgf_ref[...] += jnp.sum(dy * xhat, axis=0, keepdims=True)
        dx2 = _rms_bwd(dy, xhat, rr, g)
        dx2_ref[...] = dx2
        dx2b_ref[...] = dx2.astype(BF16)

    row = _rows_spec(tm, D_MODEL)
    return _token_call("ffn_down_loss", body, t, tm,
                       [(ff, _chips_spec(tm)), (w_d, _resident((N_CHIPS, FF_SHARD, D_MODEL))), (x1, row), (tgt, row),
                        (gf, _full((1, D_MODEL)))],
                       [((t, D_MODEL), F32, row), ((t, D_MODEL), BF16, row)], reds=[(1, LANES), (1, D_MODEL)])


def _ffn_bwd_act(dx2b, w_d, a, b):
    t = dx2b.shape[0]
    tm = min(t, 512)

    def body(d_ref, wd_ref, a_ref, b_ref, da_ref, db_ref):
        d = d_ref[...]
        for s in range(N_CHIPS):
            dff = lax.dot_general(d, wd_ref[s], (NT, ((), ())), preferred_element_type=F32)
            av, bv = a_ref[s].astype(F32), b_ref[s].astype(F32)
            sg = _sigmoid(av)
            da_ref[s] = (dff * bv * (sg * (1.0 + av * (1.0 - sg)))).astype(BF16)
            db_ref[s] = (dff * (av * sg)).astype(BF16)

    shp = (N_CHIPS, t, FF_SHARD)
    return _token_call("ffn_bwd_act", body, t, tm,
                       [(dx2b, _rows_spec(tm, D_MODEL)), (w_d, _resident((N_CHIPS, FF_SHARD, D_MODEL))),
                        (a, _chips_spec(tm)), (b, _chips_spec(tm))],
                       [(shp, BF16, _chips_spec(tm))] * 2)


def _ffn_bwd_in(da, db, w_g, w_u, x1, dx2, g2):
    t = x1.shape[0]
    tm = min(t, 512)

    def body(da_ref, db_ref, wg_ref, wu_ref, x1_ref, dx2_ref, g_ref, dx1_ref, dx1b_ref, dg_ref):
        _zero_at_start(dg_ref)
        acc = None
        for s in range(N_CHIPS):
            part = (lax.dot_general(da_ref[s], wg_ref[s], (NT, ((), ())), preferred_element_type=F32)
                    + lax.dot_general(db_ref[s], wu_ref[s], (NT, ((), ())), preferred_element_type=F32))
            acc = part if acc is None else acc + part
        xhat, rr = _rms_stats(x1_ref[...])
        dg_ref[...] += jnp.sum(acc * xhat, axis=0, keepdims=True)
        dx1 = dx2_ref[...] + _rms_bwd(acc, xhat, rr, g_ref[...])
        dx1_ref[...] = dx1
        dx1b_ref[...] = dx1.astype(BF16)

    row = _rows_spec(tm, D_MODEL)
    w_spec = _resident((N_CHIPS, D_MODEL, FF_SHARD))
    return _token_call("ffn_bwd_in", body, t, tm,
                       [(da, _chips_spec(tm)), (db, _chips_spec(tm)), (w_g, w_spec), (w_u, w_spec), (x1, row), (dx2, row),
                        (g2, _full((1, D_MODEL)))],
                       [((t, D_MODEL), F32, row), ((t, D_MODEL), BF16, row)], reds=[(1, D_MODEL)])


def _group_dh(d, w_refs):
    dh = None
    for part, w_ref in enumerate(w_refs):
        term = lax.dot_general(d[:, part * ATTN_W:(part + 1) * ATTN_W], w_ref[...], (NT, ((), ())),
                               preferred_element_type=F32)
        dh = term if dh is None else dh + term
    return dh


def _in_proj_bwd(dgu, dqkvs, w_in, x, dx1, g1):
    t = x.shape[0]
    tile = min(t, TILE)
    ngroups = len(DILATIONS)

    def body(*refs):
        dgu_ref, dq_refs = refs[0], refs[1:1 + ngroups]
        w0_ref, w1_ref = refs[1 + ngroups:3 + ngroups]
        wg_refs = [refs[3 + ngroups + 3 * g:6 + ngroups + 3 * g] for g in range(ngroups)]
        x_ref, dx1_ref, g_ref, dx_ref, dg_ref, slab = refs[3 + 4 * ngroups:]
        _zero_at_start(dg_ref)
        dh = lax.dot_general(dgu_ref[:, 0:GU_HALF], w0_ref[...], (NT, ((), ())), preferred_element_type=F32)
        dh = dh + lax.dot_general(dgu_ref[:, GU_HALF:], w1_ref[...], (NT, ((), ())), preferred_element_type=F32)
        dh = dh + _group_dh(dq_refs[0][0], wg_refs[0])
        for g in range(1, ngroups):
            dil = DILATIONS[g]
            part = _group_dh(dq_refs[g][...].reshape(tile, GROUP_COLS), wg_refs[g])
            for r in range(dil):
                _put_class_rows(slab, r, dil, part[r * (tile // dil):(r + 1) * (tile // dil)])
            dh = dh + _from_slabs(slab)
        xhat, rr = _rms_stats(x_ref[...])
        dg_ref[...] += jnp.sum(dh * xhat, axis=0, keepdims=True)
        dx_ref[...] = dx1_ref[...] + _rms_bwd(dh, xhat, rr, g_ref[...])

    row = _rows_spec(tile, D_MODEL)
    group_ins = [(dqkvs[g].reshape(d, t // d, GROUP_COLS), _group_spec(d, tile, GROUP_COLS)) for g, d in enumerate(DILATIONS)]
    w_specs = _gu_w_specs() + [s for g in range(ngroups) for s in _group_w_specs(g)]
    return _token_call(
        "in_proj_bwd", body, t, tile,
        [(dgu, _rows_spec(tile, GU_COLS))] + group_ins + [(w_in, s) for s in w_specs]
        + [(x, row), (dx1, row), (g1, _full((1, D_MODEL)))],
        [((t, D_MODEL), F32, row)], reds=[(1, D_MODEL)], scratch=[_slabs(tile, D_MODEL)])


WGRAD_TK = 2048


def _wgrad_mm(name, grid, a, a_spec, b, b_spec, acc_shape, out_shape, out_spec, dst=None):
    nk = grid[-1]

    def body(*refs):
        a_ref, b_ref, o_ref, acc_ref = refs[0], refs[1], refs[-2], refs[-1]
        k = pl.program_id(len(grid) - 1)
        part = lax.dot_general(a_ref[...], b_ref[...], (TN, ((), ())), preferred_element_type=F32)

        @pl.when(k == 0)
        def _():
            acc_ref[...] = part

        @pl.when(k > 0)
        def _():
            acc_ref[...] += part

        @pl.when(k == nk - 1)
        def _():
            o_ref[...] = acc_ref[...].astype(BF16)

    filled = [] if dst is None else [dst]
    return pl.pallas_call(
        body, grid=grid, in_specs=[a_spec, b_spec] + [pl.BlockSpec(memory_space=pl.ANY)] * len(filled),
        out_specs=out_spec, out_shape=jax.ShapeDtypeStruct(out_shape, BF16), scratch_shapes=[pltpu.VMEM(acc_shape, F32)],
        input_output_aliases={2: 0} if filled else {}, compiler_params=_cparams(len(grid)), name=name)(a, b, *filled)


def _wgrad_2d(name, a, b, tm, tn):
    t, k1 = a.shape
    n = b.shape[1]
    tk = min(t, WGRAD_TK)
    return _wgrad_mm(name, (k1 // tm, n // tn, t // tk), a, pl.BlockSpec((tk, tm), lambda i, j, k: (k, i)),
                     b, pl.BlockSpec((tk, tn), lambda i, j, k: (k, j)), (tm, tn), (k1, n),
                     pl.BlockSpec((tm, tn), lambda i, j, k: (i, j)))


def _wgrad_in(hs, dgu, dqkvs):
    t = dgu.shape[0]
    tk = min(t, WGRAD_TK)
    gu_block = QKV_BLOCKS * ATTN_W // GU_HALF
    parts = [(hs[0], dgu, GU_HALF, lambda j: j + gu_block)]
    parts += [(hs[g].reshape(t, D_MODEL), dqkvs[g], ATTN_W, lambda j, g=g: _w_in_block(j, g)) for g in range(3)]
    dst = None
    for n, (a, b, tn, block_of) in enumerate(parts):
        dst = _wgrad_mm(f"wgrad_in_{n}", (1, b.shape[1] // tn, t // tk),
                        a, pl.BlockSpec((tk, D_MODEL), lambda i, j, k: (k, 0)), b, pl.BlockSpec((tk, tn), lambda i, j, k: (k, j)),
                        (D_MODEL, tn), (D_MODEL, IN_COLS),
                        pl.BlockSpec((D_MODEL, tn), lambda i, j, k, block_of=block_of: (0, block_of(j))), dst=dst)
    return dst


def _wgrad_ff_in(name, h2, da):
    t = h2.shape[0]
    tk = min(t, WGRAD_TK)
    return _wgrad_mm(name, (N_CHIPS, 1, t // tk), h2, pl.BlockSpec((tk, D_MODEL), lambda i, j, k: (k, 0)),
                     da, pl.BlockSpec((None, tk, FF_SHARD), lambda i, j, k: (i, k, 0)), (D_MODEL, FF_SHARD),
                     (N_CHIPS, D_MODEL, FF_SHARD), pl.BlockSpec((None, D_MODEL, FF_SHARD), lambda i, j, k: (i, 0, 0)))


def _wgrad_ff_down(ff, dx2b):
    t = dx2b.shape[0]
    tk = min(t, WGRAD_TK)
    return _wgrad_mm("wgrad_ffn_down", (N_CHIPS, 1, t // tk), ff, pl.BlockSpec((None, tk, FF_SHARD), lambda i, j, k: (i, k, 0)),
                     dx2b, pl.BlockSpec((tk, D_MODEL), lambda i, j, k: (k, 0)), (FF_SHARD, D_MODEL),
                     (N_CHIPS, FF_SHARD, D_MODEL), pl.BlockSpec((None, FF_SHARD, D_MODEL), lambda i, j, k: (i, 0, 0)))


def _local_step(x, pos_col, tgt, g1, ln_g, ln_b, w_s, b_s, g2, gf, first_weight, late_weights, on_grads=None):
    tables = _rope_tables(pos_col)
    bias_exp = jnp.repeat(jnp.transpose(b_s), SGU_W // SGU_GROUPS, axis=1)

    hs = _norm_fwd(x, g1)
    w_p = first_weight(hs[0])
    gu, qkvs = _in_proj(hs, w_p, tables)
    os_, ls_ = [], []
    for g, dil in enumerate(DILATIONS):
        o, lse = _attn_fwd(qkvs[g], g, dil)
        os_.append(o)
        ls_.append(lse)
    attn = _combine_fwd(os_, ls_)
    sgu = _sgu_fwd(gu, ln_g, ln_b, w_s, bias_exp)
    w_pa, w_ps, w_out, w_g, w_u, w_d = late_weights(attn)
    pa, ps, merged, x1, h2 = _merge_fwd(attn, sgu, gu, x, w_pa, w_ps, w_out, g2)
    a, b, ff = _ffn_fwd(h2, w_g, w_u)
    dx2, dx2b, loss, dgf = _ffn_down_loss(ff, w_d, x1, tgt, gf)

    da, db = _ffn_bwd_act(dx2b, w_d, a, b)
    dw_d = _wgrad_ff_down(ff, dx2b)
    dx1, dx1b, dg2 = _ffn_bwd_in(da, db, w_g, w_u, x1, dx2, g2)
    dw_g = _wgrad_ff_in("wgrad_ffn_gate", h2, da)
    dw_u = _wgrad_ff_in("wgrad_ffn_up", h2, db)

    dgu, dpa, dps, dattn, dsgu = _merge_bwd(dx1b, gu, pa, ps, w_pa, w_ps, w_out)
    dw_out = _wgrad_2d("wgrad_out", merged, dx1b, D_MODEL, D_MODEL)
    dw_pa = _wgrad_2d("wgrad_proj_attn", attn, dpa, ATTN_W, D_MODEL)
    dw_ps = _wgrad_2d("wgrad_proj_sgu", sgu, dps, SGU_W, D_MODEL)
    if on_grads is not None:
        ln_g = ln_g + on_grads(1, dict(w_proj_attn=dw_pa, w_proj_sgu=dw_ps, w_out=dw_out, w_ffn_gate=dw_g, w_ffn_up=dw_u,
                                       w_ffn_down=dw_d))[:, :SGU_W]
    dgu, dw_s, dbias, dln_g, dln_b = _sgu_bwd(dgu, gu, dsgu, ln_g, ln_b, w_s, bias_exp)
    dos, ccs = _combine_bwd(dattn, os_, ls_)
    dqkvs = [_attn_bwd(qkvs[g], dos[g], ccs[g], ls_[g], *tables[g], g, dil) for g, dil in enumerate(DILATIONS)]
    dw_p = _wgrad_in(hs, dgu, dqkvs)
    if on_grads is not None:
        g1 = g1 + on_grads(0, dict(w_in=dw_p))
    dx, dg1 = _in_proj_bwd(dgu, dqkvs, w_p, x, dx1, g1)

    db_s = jnp.transpose(dbias[:, ::SGU_W // SGU_GROUPS])
    small = dict(loss=loss, norm1_g=dg1, sgu_ln_g=dln_g, sgu_ln_b=dln_b, w_spatial=dw_s, b_spatial=db_s,
                 norm2_g=dg2, final_g=dgf)
    big = dict(w_in=dw_p, w_proj_attn=dw_pa, w_proj_sgu=dw_ps, w_out=dw_out, w_ffn_gate=dw_g, w_ffn_up=dw_u,
               w_ffn_down=dw_d)
    return dx, big, small


def _ew(name, fn, ins, out_dtypes):
    shp = ins[0].shape
    rows, cols = shp
    tr = next((cand for cand in (256, 352, 128) if rows % cand == 0 and rows > cand), rows)

    def body(*refs):
        res = fn(*[r[...] for r in refs[:len(ins)]])
        for o_ref, v in zip(refs[len(ins):], res):
            o_ref[...] = v.astype(o_ref.dtype)

    spec = pl.BlockSpec((tr, cols), lambda i: (i, 0))
    return pl.pallas_call(
        body, grid=(rows // tr,), in_specs=[spec] * len(ins), out_specs=[spec] * len(out_dtypes),
        out_shape=[jax.ShapeDtypeStruct(shp, d) for d in out_dtypes],
        compiler_params=_cparams(1), name=name)(*ins)


def _adamw_math(g, w, m, v):
    m = ADAM_B1 * m + (1.0 - ADAM_B1) * g
    v = ADAM_B2 * v + (1.0 - ADAM_B2) * (g * g)
    m_hat = m / (1.0 - ADAM_B1 ** ADAM_STEP)
    v_hat = v / (1.0 - ADAM_B2 ** ADAM_STEP)
    delta = -ADAM_LR * (m_hat / (jnp.sqrt(v_hat) + ADAM_EPS) + ADAM_WD * w)
    return delta, m, v


def _adamw(name, g, w, m, v):
    return _ew(name, lambda g_, w_, m_, v_: (g_,) + _adamw_math(g_, w_, m_, v_), [g, w, m, v], [F32] * 4)


VMEM_SPEC = pl.BlockSpec(memory_space=pltpu.VMEM)


def _for_row_chunks(rows, fn):
    ck = next(c for c in (64, 32, 16) if rows % c == 0)

    def step(i, carry):
        fn(pl.multiple_of(i * ck, ck), ck)
        return carry

    lax.fori_loop(0, rows // ck, step, 0)


def _place():
    x, y, c = lax.axis_index("x"), lax.axis_index("y"), lax.axis_index("c")
    chips = [(1 - x, y), (x, 1 - y), (1 - x, 1 - y)]
    return x, y, c, 2 * x + y, chips


def _rows(ref, start, size):
    if len(ref.shape) == 2:
        return ref.at[pl.ds(start, size), :]
    return ref.at[:, pl.ds(start, size), :]


def _comm_call(name, body, ins, out_shapes, scratch, n_remote):
    return pl.pallas_call(
        body, in_specs=[VMEM_SPEC] * len(ins), out_specs=[VMEM_SPEC] * len(out_shapes),
        out_shape=out_shapes,
        scratch_shapes=list(scratch) + [pltpu.SemaphoreType.DMA((n_remote,)), pltpu.SemaphoreType.DMA((n_remote,))],
        compiler_params=pltpu.CompilerParams(vmem_limit_bytes=VMEM_LIMIT), name=name)(*ins)


def _gather_finish(name, shard, landed):
    k_rows, n = shard.shape
    kh = k_rows // 2

    def body(shard_ref, land_ref, out_ref, send, recv):
        x, y, c, me, chips = _place()
        passed = []
        for j, chip in enumerate(chips):
            theirs = 2 * chip[0] + chip[1]
            cp = pltpu.make_async_remote_copy(
                src_ref=land_ref.at[j], dst_ref=_rows(out_ref.at[theirs], c * kh, kh), send_sem=send.at[j],
                recv_sem=recv.at[j], device_id=(x, y, 1 - c), device_id_type=MESH)
            cp.start()
            passed.append(cp)
        mine = out_ref.at[me]

        def put_own(r0, ck):
            mine[pl.ds(r0, ck), :] = shard_ref[pl.ds(r0, ck), :]

        _for_row_chunks(k_rows, put_own)
        for j, chip in enumerate(chips):
            slot = out_ref.at[2 * chip[0] + chip[1]]

            def put_half(r0, ck, j=j, slot=slot):
                slot[pl.ds(pl.multiple_of(c * kh + r0, ck), ck), :] = land_ref[j, pl.ds(r0, ck), :]

            _for_row_chunks(kh, put_half)
        for j, chip in enumerate(chips):
            other = _rows(out_ref.at[2 * chip[0] + chip[1]], (1 - c) * kh, kh)
            pltpu.make_async_remote_copy(src_ref=other, dst_ref=other, send_sem=send.at[j], recv_sem=recv.at[j],
                                         device_id=(x, y, 1 - c), device_id_type=MESH).wait_recv()
        for cp in passed:
            cp.wait_send()

    return _comm_call(name, body, [shard, landed], [jax.ShapeDtypeStruct((N_CHIPS, k_rows, n), shard.dtype)], [], 3)[0]


HBM_SPEC = pl.BlockSpec(memory_space=pltpu.HBM)
SEM_SPEC = pl.BlockSpec(memory_space=pltpu.SEMAPHORE)
DATAFLOW = pltpu.SideEffectType.DATAFLOW_SIDE_EFFECTING
TOKEN_SHAPE = (1, D_MODEL)
N_PEERS = 7


def _peers():
    x, y, c = lax.axis_index("x"), lax.axis_index("y"), lax.axis_index("c")
    flip = lambda v, f: 1 - v if f else v
    return [(flip(x, k & 4), flip(y, k & 2), flip(c, k & 1)) for k in range(1, N_PEERS + 1)]


def _piece_shape(shape):
    return (shape[-2] // 2, shape[2] if len(shape) == 3 else shape[1] // N_CHIPS)


def _device_piece(ref, chip, core):
    kh, n4 = _piece_shape(ref.shape)
    if len(ref.shape) == 3:
        return ref.at[chip, pl.ds(core * kh, kh), :]
    return ref.at[pl.ds(core * kh, kh), pl.ds(chip * n4, n4)]


def _exchange_copies(partials, lands, send, recv):
    return [pltpu.make_async_remote_copy(
        src_ref=_device_piece(partials[t], 2 * px + py, pc), dst_ref=lands[t].at[k], send_sem=send.at[t * N_PEERS + k],
        recv_sem=recv.at[t * N_PEERS + k], device_id=(px, py, pc), device_id_type=MESH)
        for t in range(len(partials)) for k, (px, py, pc) in enumerate(_peers())]


def _gather_copies(shards, lands, send, recv):
    x, y, c, me, chips = _place()
    return [pltpu.make_async_remote_copy(
        src_ref=shards[t], dst_ref=lands[t].at[me], send_sem=send.at[t * 3 + j], recv_sem=recv.at[t * 3 + j],
        device_id=(*chip, c), device_id_type=MESH)
        for t in range(len(shards)) for j, chip in enumerate(chips)]


def _gather_half_copies(shards, lands, send, recv):
    x, y, c, me, chips = _place()
    return [pltpu.make_async_remote_copy(
        src_ref=_rows(shards[t], c * (shards[t].shape[0] // 2), shards[t].shape[0] // 2), dst_ref=lands[t].at[j],
        send_sem=send.at[t * 3 + j], recv_sem=recv.at[t * 3 + j], device_id=(*chip, c), device_id_type=MESH)
        for t in range(len(shards)) for j, chip in enumerate(chips)]


def _split_start(name, copies, per_tensor, srcs, land_shapes):
    nt = len(srcs)
    lands = [lax.empty(s, BF16) for s in land_shapes]
    nsem = nt * per_tensor

    def body(*refs):
        send, recv = refs[2 * nt], refs[2 * nt + 1]
        for cp in copies(refs[:nt], refs[nt:2 * nt], send, recv):
            cp.start()
        refs[-1][...] = jnp.zeros(TOKEN_SHAPE, F32)

    hbm = lambda a: pltpu.with_memory_space_constraint(a, pltpu.HBM)
    outs = pl.pallas_call(
        body, name=name,
        out_shape=[pltpu.SemaphoreType.DMA((nsem,)), pltpu.SemaphoreType.DMA((nsem,))]
        + [pltpu.HBM(s.shape, s.dtype) for s in srcs] + [pltpu.HBM(l.shape, l.dtype) for l in lands]
        + [jax.ShapeDtypeStruct(TOKEN_SHAPE, F32)],
        in_specs=[HBM_SPEC] * (2 * nt), out_specs=[SEM_SPEC, SEM_SPEC] + [HBM_SPEC] * (2 * nt) + [VMEM_SPEC],
        input_output_aliases={i: 2 + i for i in range(2 * nt)},
        compiler_params=pltpu.CompilerParams(has_side_effects=DATAFLOW))(*[hbm(a) for a in list(srcs) + lands])
    return outs[0], outs[1], outs[2:2 + nt], outs[2 + nt:2 + 2 * nt], outs[-1]


def _split_wait(name, copies, send, recv, srcs, lands, after):
    nt = len(srcs)

    def body(*refs):
        for cp in copies(refs[:nt], refs[nt:2 * nt], refs[2 * nt], refs[2 * nt + 1]):
            cp.wait_send()
            cp.wait_recv()

    outs = pl.pallas_call(
        body, name=name,
        out_shape=[pltpu.HBM(s.shape, s.dtype) for s in srcs] + [pltpu.HBM(l.shape, l.dtype) for l in lands],
        in_specs=[HBM_SPEC] * (2 * nt) + [SEM_SPEC, SEM_SPEC, pl.BlockSpec(memory_space=pl.ANY)],
        out_specs=[HBM_SPEC] * (2 * nt), input_output_aliases={i: i for i in range(2 * nt)},
        compiler_params=pltpu.CompilerParams(has_side_effects=DATAFLOW))(*srcs, *lands, send, recv, after)
    return outs[:nt], outs[nt:]


def _device_sum(name, partials, lands):
    nt = len(partials)

    def body(*refs):
        ins, slots, outs, owns = refs[:nt], refs[nt:2 * nt], refs[2 * nt:3 * nt], refs[3 * nt:4 * nt]
        send, recv, loc = refs[4 * nt:]
        x, y, c, me, chips = _place()
        sibling = (x, y, 1 - c)
        loads = [pltpu.make_async_copy(_device_piece(ins[t], me, c), owns[t], loc.at[t]) for t in range(nt)]
        for cp in loads:
            cp.start()
        handed = []
        for t in range(nt):
            kh = owns[t].shape[0]
            loads[t].wait()

            def add(r0, ck, own=owns[t], slot=slots[t], dst=outs[t], kh=kh):
                rows = pl.ds(r0, ck)
                acc = own[rows, :].astype(F32)
                for k in range(N_PEERS):
                    acc = acc + slot[k, rows, :].astype(F32)
                dst[pl.ds(pl.multiple_of(c * kh + r0, ck), ck), :] = acc

            _for_row_chunks(kh, add)
            rc = pltpu.make_async_remote_copy(
                src_ref=_rows(outs[t], c * kh, kh), dst_ref=_rows(outs[t], c * kh, kh), send_sem=send.at[t],
                recv_sem=recv.at[t], device_id=sibling, device_id_type=MESH)
            rc.start()
            handed.append(rc)
        for t in range(nt):
            kh = owns[t].shape[0]
            other = _rows(outs[t], (1 - c) * kh, kh)
            pltpu.make_async_remote_copy(
                src_ref=other, dst_ref=other, send_sem=send.at[t], recv_sem=recv.at[t],
                device_id=sibling, device_id_type=MESH).wait_recv()
        for rc in handed:
            rc.wait_send()

    pieces = [_piece_shape(p.shape) for p in partials]
    return pl.pallas_call(
        body, in_specs=[pl.BlockSpec(memory_space=pl.ANY)] * nt + [VMEM_SPEC] * nt, out_specs=[VMEM_SPEC] * nt,
        out_shape=[jax.ShapeDtypeStruct((2 * kh, n4), F32) for kh, n4 in pieces],
        scratch_shapes=[pltpu.VMEM(p, BF16) for p in pieces]
        + [pltpu.SemaphoreType.DMA((nt,)), pltpu.SemaphoreType.DMA((nt,)), pltpu.SemaphoreType.DMA((nt,))],
        compiler_params=pltpu.CompilerParams(vmem_limit_bytes=VMEM_LIMIT), name=name)(*partials, *lands)


VEC_SHAPE = (8, D_MODEL + LANES)
VEC_SLOTS = dict(norm1_g=(slice(0, 1), slice(0, D_MODEL)), norm2_g=(slice(1, 2), slice(0, D_MODEL)),
                 final_g=(slice(2, 3), slice(0, D_MODEL)), sgu_ln_g=(slice(3, 4), slice(0, SGU_W)),
                 sgu_ln_b=(slice(3, 4), slice(SGU_W, 2 * SGU_W)), b_spatial=(slice(0, 8), slice(D_MODEL, D_MODEL + LANES)),
                 loss=(slice(4, 5), slice(0, LANES)))
VEC_PARAMS = ("norm1_g", "norm2_g", "final_g", "sgu_ln_g", "sgu_ln_b", "b_spatial")
SMALL_PARAMS = VEC_PARAMS + ("w_spatial",)
W_SPATIAL_2D = (SGU_GROUPS * SGU_CHUNK, SGU_CHUNK)


def _small_step(partials, w, m, v):
    def shape2d(name):
        if name == "w_spatial":
            return W_SPATIAL_2D
        rows, cols = VEC_SLOTS[name]
        return (rows.stop - rows.start, cols.stop - cols.start)

    g_names = VEC_PARAMS + ("loss", "w_spatial")
    ng, npar = len(g_names), len(SMALL_PARAMS)

    def pack(dst, parts):
        dst[...] = jnp.zeros(VEC_SHAPE, F32)
        for n, ref in parts.items():
            if n in VEC_SLOTS:
                dst[VEC_SLOTS[n]] = ref[...]

    def reduce_body(*refs):
        g_in = dict(zip(g_names, refs[:ng]))
        vec_out, ws_out, vec, vec_pair, vec_slot, ws_pair, ws_slot, send, recv = refs[ng:]
        x, y, c, me, chips = _place()
        sibling = (x, y, 1 - c)
        pack(vec, g_in)
        copies = []

        def allreduce(k0, src, pair, slot):
            first = pltpu.make_async_remote_copy(src_ref=src, dst_ref=pair, send_sem=send.at[k0], recv_sem=recv.at[k0],
                                                 device_id=sibling, device_id_type=MESH)
            first.start()
            first.wait_recv()
            slot[me] = src[...] + pair[...]
            arrivals = []
            for j, chip in enumerate(chips):
                theirs = 2 * chip[0] + chip[1]
                rc = pltpu.make_async_remote_copy(src_ref=slot.at[me], dst_ref=slot.at[me], send_sem=send.at[k0 + 1 + j],
                                                  recv_sem=recv.at[k0 + 1 + j], device_id=(*chip, c), device_id_type=MESH)
                rc.start()
                arrivals.append(pltpu.make_async_remote_copy(
                    src_ref=slot.at[theirs], dst_ref=slot.at[theirs], send_sem=send.at[k0 + 1 + j],
                    recv_sem=recv.at[k0 + 1 + j], device_id=(*chip, c), device_id_type=MESH))
                copies.append(rc)
            copies.append(first)
            return arrivals

        arrivals = allreduce(0, vec, vec_pair, vec_slot) + allreduce(4, g_in["w_spatial"], ws_pair, ws_slot)
        for a in arrivals:
            a.wait_recv()
        vec_out[...] = ((vec_slot[0] + vec_slot[1]) + vec_slot[2]) + vec_slot[3]

        def spatial(r0, ck):
            rows = pl.ds(r0, ck)
            ws_out[rows, :] = ((ws_slot[0, rows, :] + ws_slot[1, rows, :]) + ws_slot[2, rows, :]) + ws_slot[3, rows, :]

        _for_row_chunks(W_SPATIAL_2D[0], spatial)
        for rc in copies:
            rc.wait_send()

    g_vec, g_ws = pl.pallas_call(
        reduce_body, in_specs=[VMEM_SPEC] * ng, out_specs=[VMEM_SPEC] * 2,
        out_shape=[jax.ShapeDtypeStruct(VEC_SHAPE, F32), jax.ShapeDtypeStruct(W_SPATIAL_2D, F32)],
        scratch_shapes=[pltpu.VMEM(VEC_SHAPE, F32), pltpu.VMEM(VEC_SHAPE, F32), pltpu.VMEM((N_CHIPS,) + VEC_SHAPE, F32),
                        pltpu.VMEM(W_SPATIAL_2D, F32), pltpu.VMEM((N_CHIPS,) + W_SPATIAL_2D, F32),
                        pltpu.SemaphoreType.DMA((8,)), pltpu.SemaphoreType.DMA((8,))],
        name="small_params_allreduce")(*[partials[n].reshape(shape2d(n)) for n in g_names])

    def update_body(*refs):
        gv_ref, gw_ref = refs[:2]
        w_in, m_in, v_in = (dict(zip(SMALL_PARAMS, refs[2 + k * npar:2 + (k + 1) * npar])) for k in range(3))
        o0 = 2 + 3 * npar
        g_out = dict(zip(g_names, refs[o0:o0 + ng]))
        d_out, m_out, v_out = (dict(zip(SMALL_PARAMS, refs[o0 + ng + k * npar:o0 + ng + (k + 1) * npar])) for k in range(3))
        vw, vm, vv = refs[o0 + ng + 3 * npar:]
        pack(vw, w_in)
        pack(vm, m_in)
        pack(vv, v_in)
        d_vec, m_vec, v_vec = _adamw_math(gv_ref[...], vw[...], vm[...], vv[...])
        vw[...] = d_vec
        vm[...] = m_vec
        vv[...] = v_vec
        for n in VEC_PARAMS + ("loss",):
            g_out[n][...] = gv_ref[VEC_SLOTS[n]]
        for n in VEC_PARAMS:
            d_out[n][...] = vw[VEC_SLOTS[n]]
            m_out[n][...] = vm[VEC_SLOTS[n]]
            v_out[n][...] = vv[VEC_SLOTS[n]]

        def spatial(r0, ck):
            rows = pl.ds(r0, ck)
            g = gw_ref[rows, :]
            d_, m_, v_ = _adamw_math(g, w_in["w_spatial"][rows, :], m_in["w_spatial"][rows, :], v_in["w_spatial"][rows, :])
            g_out["w_spatial"][rows, :] = g
            d_out["w_spatial"][rows, :] = d_
            m_out["w_spatial"][rows, :] = m_
            v_out["w_spatial"][rows, :] = v_

        _for_row_chunks(W_SPATIAL_2D[0], spatial)

    ins = [g_vec, g_ws]
    for src in (w, m, v):
        ins += [src[n].reshape(shape2d(n)) for n in SMALL_PARAMS]
    out_shapes = [jax.ShapeDtypeStruct(shape2d(n), F32) for n in g_names + SMALL_PARAMS * 3]
    outs = pl.pallas_call(
        update_body, in_specs=[VMEM_SPEC] * len(ins), out_specs=[VMEM_SPEC] * len(out_shapes), out_shape=out_shapes,
        scratch_shapes=[pltpu.VMEM(VEC_SHAPE, F32)] * 3, name="small_params_update")(*ins)
    grads = dict(zip(g_names, outs[:ng]))
    rest = [dict(zip(SMALL_PARAMS, outs[ng + k * npar:ng + (k + 1) * npar])) for k in range(3)]
    return grads, rest[0], rest[1], rest[2]


BIG = ("w_in", "w_proj_attn", "w_proj_sgu", "w_out", "w_ffn_gate", "w_ffn_up", "w_ffn_down")
COMM_GROUPS = (("w_in",), ("w_proj_attn", "w_proj_sgu", "w_out", "w_ffn_gate", "w_ffn_up", "w_ffn_down"))
WEIGHTS = ("norm1_g", "w_in", "sgu_ln_g", "sgu_ln_b", "w_spatial", "b_spatial", "w_proj_attn", "w_proj_sgu", "w_out",
           "norm2_g", "w_ffn_gate", "w_ffn_up", "w_ffn_down", "final_g")


def _cols_from_chips(g):
    return jnp.transpose(g, (1, 0, 2)).reshape(g.shape[1], N_CHIPS * g.shape[2])


def kernel(x, positions, norm1_g, w_in, sgu_ln_g, sgu_ln_b, w_spatial, b_spatial, w_proj_attn, w_proj_sgu, w_out, norm2_g, w_ffn_gate, w_ffn_up, w_ffn_down, final_g, loss_target, m_norm1_g, m_w_in, m_sgu_ln_g, m_sgu_ln_b, m_w_spatial, m_b_spatial, m_w_proj_attn, m_w_proj_sgu, m_w_out, m_norm2_g, m_w_ffn_gate, m_w_ffn_up, m_w_ffn_down, m_final_g, v_norm1_g, v_w_in, v_sgu_ln_g, v_sgu_ln_b, v_w_spatial, v_b_spatial, v_w_proj_attn, v_w_proj_sgu, v_w_out, v_norm2_g, v_w_ffn_gate, v_w_ffn_up, v_w_ffn_down, v_final_g):
    w = dict(norm1_g=norm1_g, w_in=w_in, sgu_ln_g=sgu_ln_g, sgu_ln_b=sgu_ln_b, w_spatial=w_spatial, b_spatial=b_spatial,
             w_proj_attn=w_proj_attn, w_proj_sgu=w_proj_sgu, w_out=w_out, norm2_g=norm2_g, w_ffn_gate=w_ffn_gate,
             w_ffn_up=w_ffn_up, w_ffn_down=w_ffn_down, final_g=final_g)
    m = dict(norm1_g=m_norm1_g, w_in=m_w_in, sgu_ln_g=m_sgu_ln_g, sgu_ln_b=m_sgu_ln_b, w_spatial=m_w_spatial,
             b_spatial=m_b_spatial, w_proj_attn=m_w_proj_attn, w_proj_sgu=m_w_proj_sgu, w_out=m_w_out, norm2_g=m_norm2_g,
             w_ffn_gate=m_w_ffn_gate, w_ffn_up=m_w_ffn_up, w_ffn_down=m_w_ffn_down, final_g=m_final_g)
    v = dict(norm1_g=v_norm1_g, w_in=v_w_in, sgu_ln_g=v_sgu_ln_g, sgu_ln_b=v_sgu_ln_b, w_spatial=v_w_spatial,
             b_spatial=v_b_spatial, w_proj_attn=v_w_proj_attn, w_proj_sgu=v_w_proj_sgu, w_out=v_w_out, norm2_g=v_norm2_g,
             w_ffn_gate=v_w_ffn_gate, w_ffn_up=v_w_ffn_up, w_ffn_down=v_w_ffn_down, final_g=v_final_g)
    t = x.shape[1]

    shards = {n: _ew(f"cast_{n}", lambda a: (a,), [w[n][0]], [BF16])[0] for n in BIG}
    late = COMM_GROUPS[1]
    k_in, n_in = shards["w_in"].shape
    *first, token = _split_start("gather_start_0", _gather_half_copies, 3, [shards["w_in"]], [(3, k_in // 2, n_in)])
    pending = {}

    def first_weight(after):
        srcs, filled = _split_wait("gather_wait_0", _gather_half_copies, *first, after)
        gath_in, late_shards = lax.optimization_barrier(
            (_gather_finish("gather_finish_0", srcs[0], filled[0]), [shards[n] for n in late]))
        *pending["late"], _ = _split_start(
            "gather_start_1", _gather_copies, 3, late_shards, [(N_CHIPS,) + s.shape for s in late_shards])
        return _cols_from_chips(gath_in)

    def late_weights(after):
        srcs, filled = _split_wait("gather_wait_1", _gather_copies, *pending["late"], after)
        me = 2 * lax.axis_index("x") + lax.axis_index("y")
        gath = {n: lax.dynamic_update_slice(f, s[None], (me, 0, 0)) for n, f, s in zip(late, filled, srcs)}
        return (_cols_from_chips(gath["w_proj_attn"]), _cols_from_chips(gath["w_proj_sgu"]),
                gath["w_out"].reshape(D_MODEL, D_MODEL), gath["w_ffn_gate"], gath["w_ffn_up"], gath["w_ffn_down"])

    exchanges = {}

    def on_grads(i, partials):
        if "w_out" in partials:
            partials["w_out"] = partials["w_out"].reshape(N_CHIPS, D_MODEL // N_CHIPS, D_MODEL)
        parts = [partials[n] for n in COMM_GROUPS[i]]
        *exchanges[i], started = _split_start(
            f"rs_exchange_start_{i}", _exchange_copies, N_PEERS, parts, [(N_PEERS,) + _piece_shape(p.shape) for p in parts])
        return started

    dx, _, small = _local_step(
        x[0], positions.reshape(t, 1), loss_target[0], norm1_g + token, sgu_ln_g, sgu_ln_b, w_spatial[0], b_spatial[0],
        norm2_g, final_g.reshape(1, D_MODEL), first_weight, late_weights, on_grads=on_grads)

    grads = {}
    for i in (1, 0):
        parts, filled = _split_wait(f"rs_exchange_wait_{i}", _exchange_copies, *exchanges[i], dx)
        grads.update(zip(COMM_GROUPS[i], _device_sum(f"rs_device_sum_{i}", parts, filled)))

    delta, new_m, new_v = {}, {}, {}
    for n in BIG:
        shp = w[n].shape
        flip = jnp.transpose if shp[-1] % LANES else (lambda a: a)
        outs = _adamw(f"adamw_{n}", flip(grads[n]), flip(w[n][0]), flip(m[n][0]), flip(v[n][0]))
        grads[n], delta[n], new_m[n], new_v[n] = (flip(a).reshape(shp) for a in outs)

    g_s, d_s, m_s, v_s = _small_step(small, w, m, v)
    loss = g_s["loss"][0, 0]
    for n in SMALL_PARAMS:
        shp = w[n].shape
        grads[n], delta[n], new_m[n], new_v[n] = (a[n].reshape(shp) for a in (g_s, d_s, m_s, v_s))

    return (loss, dx.reshape(x.shape), *[grads[n] for n in WEIGHTS], *[delta[n] for n in WEIGHTS],
            *[new_m[n] for n in WEIGHTS], *[new_v[n] for n in WEIGHTS])
```

```python
import functools

import numpy as np
import jax
import jax.numpy as jnp
from jax import lax
from jax.experimental import pallas as pl
from jax.experimental.pallas import tpu as pltpu

F32, BF16 = jnp.float32, jnp.bfloat16
MESH = pl.DeviceIdType.MESH

D_MODEL = 1024
HEAD_DIM = 64
ATTN_W = 512
DILATIONS = (1, 4, 16)
BLK = 128
ATTN_BLOCKS_PER_STEP = 4
ROPE_DIM = 16
ROPE_THETA = 500000.0
SGU_W = 512
SGU_CHUNK = 128
SGU_GROUPS = 8
D_FF = 2816
N_CHIPS = 4
FF_SHARD = D_FF // N_CHIPS
IN_COLS = 7680
EPS = 1e-6
NEG = -1e30
LANES = 128
VMEM_LIMIT = 52 * 1024 * 1024

ADAM_LR, ADAM_B1, ADAM_B2, ADAM_EPS, ADAM_WD, ADAM_STEP = 0.001, 0.9, 0.999, 1e-08, 0.01, 10

QKV_BLOCKS = 9


def _w_in_block(part, g):
    return part * len(DILATIONS) + g


def _cparams(ngrid):
    return pltpu.CompilerParams(dimension_semantics=("arbitrary",) * ngrid, vmem_limit_bytes=VMEM_LIMIT)


def _full(shape):
    return pl.BlockSpec(shape, lambda *_: (0,) * len(shape))


def _resident(shape):
    return pl.BlockSpec(shape, lambda *_: (0,) * len(shape), pipeline_mode=pl.Buffered(1))


NT = ((1,), (1,))
TN = ((0,), (0,))


def _rope(v, cos_t, sin_t):
    half = ROPE_DIM // 2
    first = (lax.broadcasted_iota(jnp.int32, cos_t.shape, 1) % HEAD_DIM) < half
    outs = []
    for cs in range(v.shape[1] // LANES):
        x = v[:, cs * LANES:(cs + 1) * LANES]
        partner = jnp.where(first, pltpu.roll(x, LANES - half, axis=1), pltpu.roll(x, half, axis=1))
        outs.append(x * cos_t + partner * sin_t)
    return outs[0] if len(outs) == 1 else jnp.concatenate(outs, axis=1)


def _spread_heads(v2, upper):
    other = pltpu.roll(v2, HEAD_DIM, axis=1)
    h0 = jnp.where(upper, other, v2)
    h1 = jnp.where(upper, v2, other)
    return jnp.concatenate([jnp.concatenate([h0, h0], axis=1), jnp.concatenate([h1, h1], axis=1)], axis=0)


def _sigmoid(v):
    return 0.5 * jnp.tanh(0.5 * v) + 0.5


def _rms_stats(v):
    r = lax.rsqrt(jnp.mean(v * v, axis=-1, keepdims=True) + EPS)
    return v * r, r


def _rms_bwd(dy, xhat, r, g):
    dxh = dy * g
    return r * (dxh - xhat * jnp.mean(dxh * xhat, axis=-1, keepdims=True))


def _head_sum_matrix():
    idx = np.arange(ATTN_W) // HEAD_DIM
    return jnp.asarray((idx[:, None] == idx[None, :]).astype(np.float32), dtype=BF16)


def _group_sum(v, e):
    hi = v.astype(BF16)
    lo = (v - hi.astype(F32)).astype(BF16)
    return jnp.dot(hi, e, preferred_element_type=F32) + jnp.dot(lo, e, preferred_element_type=F32)


TILE = 512


def _to_slabs(slab_ref, v):
    for cs in range(slab_ref.shape[0]):
        slab_ref[cs] = v[:, cs * LANES:(cs + 1) * LANES]


def _from_slabs(slab_ref):
    return jnp.concatenate([slab_ref[cs] for cs in range(slab_ref.shape[0])], axis=1)


def _class_rows(slab_ref, r, dil):
    n = slab_ref.shape[1] // dil
    return jnp.concatenate([slab_ref.at[cs][pl.ds(r, n, stride=dil), :] for cs in range(slab_ref.shape[0])], axis=1)


def _put_class_rows(slab_ref, r, dil, v):
    n = slab_ref.shape[1] // dil
    for cs in range(slab_ref.shape[0]):
        slab_ref.at[cs][pl.ds(r, n, stride=dil), :] = v[:, cs * LANES:(cs + 1) * LANES]


def _natural_from_group(slab_ref, grp_ref):
    dil = grp_ref.shape[0]
    for r in range(dil):
        _put_class_rows(slab_ref, r, dil, grp_ref[r].astype(F32))
    return _from_slabs(slab_ref)


def _group_from_natural(slab_ref, grp_ref, v):
    dil = grp_ref.shape[0]
    _to_slabs(slab_ref, v)
    for r in range(dil):
        grp_ref[r] = _class_rows(slab_ref, r, dil).astype(grp_ref.dtype)


def _group_spec(dil, tile, width):
    return pl.BlockSpec((dil, tile // dil, width), lambda i, *_: (0, i, 0))


def _slabs(tile, width):
    return pltpu.VMEM((width // LANES, tile, LANES), F32)


def _rope_consts():
    lane = np.arange(LANES) % HEAD_DIM
    fi = lane % (ROPE_DIM // 2)
    invf = np.where(lane < ROPE_DIM, ROPE_THETA ** (-(2.0 * fi) / ROPE_DIM), 0.0)
    sgn = np.where(lane < ROPE_DIM // 2, -1.0, np.where(lane < ROPE_DIM, 1.0, 0.0))
    return (jnp.asarray(invf.astype(np.float32)).reshape(1, LANES), jnp.asarray(sgn.astype(np.float32)).reshape(1, LANES))


def _rope_tables(pos_col):
    t = pos_col.shape[0]
    tile = min(t, TILE)
    invf, sgn = _rope_consts()

    def body(p_ref, f_ref, s_ref, c0, s0, c1, s1, c2, s2, slab_c, slab_s):
        ang = p_ref[...].astype(F32) * f_ref[...]
        cos, sin = jnp.cos(ang), jnp.sin(ang) * s_ref[...]
        c0[...] = cos
        s0[...] = sin
        _group_from_natural(slab_c, c1, cos)
        _group_from_natural(slab_s, s1, sin)
        for r in range(DILATIONS[2]):
            c2[r] = _class_rows(slab_c, r, DILATIONS[2])
            s2[r] = _class_rows(slab_s, r, DILATIONS[2])

    nat = pl.BlockSpec((tile, LANES), lambda i: (i, 0))
    specs, shapes = [nat, nat], [(t, LANES)] * 2
    for d in DILATIONS[1:]:
        specs += [_group_spec(d, tile, LANES)] * 2
        shapes += [(d, t // d, LANES)] * 2
    outs = pl.pallas_call(
        body, grid=(t // tile,),
        in_specs=[pl.BlockSpec((tile, 1), lambda i: (i, 0)), _full((1, LANES)), _full((1, LANES))],
        out_specs=specs, out_shape=[jax.ShapeDtypeStruct(s, F32) for s in shapes],
        scratch_shapes=[_slabs(tile, LANES)] * 2,
        compiler_params=_cparams(1), name="rope_tables")(pos_col, invf, sgn)
    return [(outs[2 * g].reshape(t, LANES), outs[2 * g + 1].reshape(t, LANES)) for g in range(len(DILATIONS))]


def _norm_fwd(x, g):
    t = x.shape[0]
    tile = min(t, TILE)

    def body(x_ref, g_ref, h0_ref, h1_ref, h2_ref, slab):
        xhat, _ = _rms_stats(x_ref[...])
        hn = xhat * g_ref[...]
        h0_ref[...] = hn.astype(BF16)
        _group_from_natural(slab, h1_ref, hn)
        for r in range(DILATIONS[2]):
            h2_ref[r] = _class_rows(slab, r, DILATIONS[2]).astype(BF16)

    nat = pl.BlockSpec((tile, D_MODEL), lambda i: (i, 0))
    return pl.pallas_call(
        body, grid=(t // tile,),
        in_specs=[nat, _full((1, D_MODEL))],
        out_specs=[nat] + [_group_spec(d, tile, D_MODEL) for d in DILATIONS[1:]],
        out_shape=[jax.ShapeDtypeStruct((t, D_MODEL), BF16)]
        + [jax.ShapeDtypeStruct((d, t // d, D_MODEL), BF16) for d in DILATIONS[1:]],
        scratch_shapes=[_slabs(tile, D_MODEL)],
        compiler_params=_cparams(1), name="norm1_fwd")(x, g)


GU_COLS = 3072
GROUP_COLS = 1536
GU_HALF = GU_COLS // 2


def _w_in_spec(width, block):
    return pl.BlockSpec((D_MODEL, width), lambda i: (0, block), pipeline_mode=pl.Buffered(1))


def _gu_w_specs():
    first = QKV_BLOCKS * ATTN_W // GU_HALF
    return [_w_in_spec(GU_HALF, first), _w_in_spec(GU_HALF, first + 1)]


def _group_w_specs(g):
    return [_w_in_spec(ATTN_W, _w_in_block(part, g)) for part in range(3)]


def _in_proj(hs, w_in, tables):
    t = hs[0].shape[0]
    tm = min(t, 1024)

    def body_gu(h_ref, w0_ref, w1_ref, o_ref):
        h = h_ref[...]
        o_ref[:, 0:GU_HALF] = jnp.dot(h, w0_ref[...], preferred_element_type=F32).astype(BF16)
        o_ref[:, GU_HALF:] = jnp.dot(h, w1_ref[...], preferred_element_type=F32).astype(BF16)

    gu = _token_call("in_proj_gates_uv", body_gu, t, tm,
                     [(hs[0], _rows_spec(tm, D_MODEL))] + [(w_in, s) for s in _gu_w_specs()],
                     [((t, GU_COLS), BF16, _rows_spec(tm, GU_COLS))])[0]

    qkvs = []
    for g in range(len(DILATIONS)):

        def body_qkv(h_ref, wq_ref, wk_ref, wv_ref, cos_ref, sin_ref, o_ref):
            h = h_ref[...]
            cos_w, sin_w = cos_ref[...], sin_ref[...]
            q = jnp.dot(h, wq_ref[...], preferred_element_type=F32)
            o_ref[:, 0:ATTN_W] = (_rope(q, cos_w, sin_w) * HEAD_DIM ** -0.5).astype(BF16)
            k = jnp.dot(h, wk_ref[...], preferred_element_type=F32)
            o_ref[:, ATTN_W:2 * ATTN_W] = _rope(k, cos_w, sin_w).astype(BF16)
            o_ref[:, 2 * ATTN_W:] = jnp.dot(h, wv_ref[...], preferred_element_type=F32).astype(BF16)

        cos_t, sin_t = tables[g]
        qkvs.append(_token_call(
            f"in_proj_qkv_g{g}", body_qkv, t, tm,
            [(hs[g].reshape(t, D_MODEL), _rows_spec(tm, D_MODEL))] + [(w_in, s) for s in _group_w_specs(g)]
            + [(cos_t, _rows_spec(tm, LANES)), (sin_t, _rows_spec(tm, LANES))],
            [((t, GROUP_COLS), BF16, _rows_spec(tm, GROUP_COLS))])[0])
    return gu, qkvs


def _attn_masks(n):
    row = lax.broadcasted_iota(jnp.int32, (2 * BLK, 2 * BLK), 0) % BLK
    col = lax.broadcasted_iota(jnp.int32, (2 * BLK, 2 * BLK), 1)
    diff = BLK + row - col
    valid = (diff >= 0) & (diff <= BLK) & ((col >= BLK) | (n > 0))
    upper = lax.broadcasted_iota(jnp.int32, (BLK, LANES), 1) >= HEAD_DIM
    return valid, upper


def _stack_heads(v2, upper):
    zero = jnp.zeros_like(v2)
    return jnp.concatenate([jnp.where(upper, zero, v2), jnp.where(upper, v2, zero)], axis=0)


def _unstack_heads(v, upper):
    return jnp.where(upper, v[BLK:], v[:BLK])


def _attn_fwd(qkv, g, dil):
    t = qkv.shape[0]
    length = t // dil
    nb = length // BLK
    per_step = min(nb, ATTN_BLOCKS_PER_STEP)
    view = qkv.reshape(dil, length, GROUP_COLS)

    def body(q_ref, kc_ref, kp_ref, vc_ref, vp_ref, o_ref, l_ref, kwin, vwin):
        n = pl.program_id(1)
        kwin[0:BLK] = kp_ref[...]
        kwin[BLK:] = kc_ref[...]
        vwin[0:BLK] = vp_ref[...]
        vwin[BLK:] = vc_ref[...]

        def block(b, carry):
            valid, upper = _attn_masks(n * per_step + b)
            rows = pl.ds(pl.multiple_of(b * BLK, BLK), BLK)
            window = pl.ds(pl.multiple_of(b * BLK, BLK), 2 * BLK)
            slabs = [slice(p * LANES, (p + 1) * LANES) for p in range(ATTN_W // LANES)]
            ss = [lax.dot_general(_stack_heads(q_ref[rows, sl], upper), kwin[window, sl], (NT, ((), ())),
                                  preferred_element_type=F32) for sl in slabs]
            soft = []
            for s in ss:
                s = jnp.where(valid, s, NEG)
                m = jnp.max(s, axis=1, keepdims=True)
                pe = jnp.exp(s - m)
                soft.append((m, pe, jnp.sum(pe, axis=1, keepdims=True)))
            for sl, (m, pe, den) in zip(slabs, soft):
                o = jnp.dot(pe.astype(BF16), vwin[window, sl], preferred_element_type=F32) / den
                lse = jnp.broadcast_to(m + jnp.log(den), (2 * BLK, LANES))
                o_ref[rows, sl] = _unstack_heads(o, upper).astype(BF16)
                l_ref[rows, sl] = _unstack_heads(lse, upper)
            return carry

        lax.fori_loop(0, per_step, block, 0)

    rows = per_step * BLK
    cur = lambda part: pl.BlockSpec((None, rows, ATTN_W), lambda r, n: (r, n, part))
    prev = lambda part: pl.BlockSpec((None, BLK, ATTN_W), lambda r, n: (r, jnp.maximum(n * per_step - 1, 0), part))
    out_spec = pl.BlockSpec((None, rows, ATTN_W), lambda r, n: (r, n, 0))
    return pl.pallas_call(
        body, grid=(dil, nb // per_step),
        in_specs=[cur(0), cur(1), prev(1), cur(2), prev(2)],
        out_specs=[out_spec, out_spec],
        out_shape=[jax.ShapeDtypeStruct((dil, length, ATTN_W), BF16), jax.ShapeDtypeStruct((dil, length, ATTN_W), F32)],
        scratch_shapes=[pltpu.VMEM((rows + BLK, ATTN_W), BF16)] * 2,
        compiler_params=_cparams(2), name=f"attn_fwd_g{g}")(view, view, view, view, view)


def _alphas(l0, l1, l2):
    m = jnp.maximum(jnp.maximum(l0, l1), l2)
    e0, e1, e2 = jnp.exp(l0 - m), jnp.exp(l1 - m), jnp.exp(l2 - m)
    inv = 1.0 / (e0 + e1 + e2)
    return e0 * inv, e1 * inv, e2 * inv


def _natural_group_values(o_refs, l_refs, slabs):
    os_ = [o_refs[0][0].astype(F32)] + [_natural_from_group(slabs[2 * g - 2], o_refs[g]) for g in (1, 2)]
    ls_ = [l_refs[0][0]] + [_natural_from_group(slabs[2 * g - 1], l_refs[g]) for g in (1, 2)]
    return os_, ls_


def _combine_fwd(os_, ls_):
    t = os_[0].shape[1]
    tile = min(t, TILE)

    def body(o0, o1, o2, l0, l1, l2, a_ref, *slabs):
        ov, lv = _natural_group_values((o0, o1, o2), (l0, l1, l2), slabs)
        a0, a1, a2 = _alphas(*lv)
        a_ref[...] = (a0 * ov[0] + a1 * ov[1] + a2 * ov[2]).astype(BF16)

    specs = [_group_spec(d, tile, ATTN_W) for d in DILATIONS]
    return pl.pallas_call(
        body, grid=(t // tile,), in_specs=specs * 2, out_specs=pl.BlockSpec((tile, ATTN_W), lambda i: (i, 0)),
        out_shape=jax.ShapeDtypeStruct((t, ATTN_W), BF16),
        scratch_shapes=[_slabs(tile, ATTN_W)] * 4,
        compiler_params=_cparams(1), name="combine_fwd")(*os_, *ls_)


def _combine_bwd(dattn, os_, ls_):
    t = dattn.shape[0]
    tile = min(t, TILE)
    e = _head_sum_matrix()

    def body(d_ref, o0, o1, o2, l0, l1, l2, e_ref, do0, do1, do2, c0, c1, c2, *slabs):
        ov, lv = _natural_group_values((o0, o1, o2), (l0, l1, l2), slabs)
        alphas = _alphas(*lv)
        d = d_ref[...]
        attn = alphas[0] * ov[0] + alphas[1] * ov[1] + alphas[2] * ov[2]
        s = _group_sum(d * attn, e_ref[...])
        do0[0] = (alphas[0] * d).astype(BF16)
        c0[0] = -alphas[0] * s
        for g, do_ref, c_ref in ((1, do1, c1), (2, do2, c2)):
            _group_from_natural(slabs[2 * g - 2], do_ref, alphas[g] * d)
            _group_from_natural(slabs[2 * g - 1], c_ref, -alphas[g] * s)

    specs = [_group_spec(d, tile, ATTN_W) for d in DILATIONS]
    shapes = [(d, t // d, ATTN_W) for d in DILATIONS]
    outs = pl.pallas_call(
        body, grid=(t // tile,),
        in_specs=[pl.BlockSpec((tile, ATTN_W), lambda i: (i, 0))] + specs * 2 + [_full((ATTN_W, ATTN_W))],
        out_specs=specs * 2,
        out_shape=[jax.ShapeDtypeStruct(s, BF16) for s in shapes] + [jax.ShapeDtypeStruct(s, F32) for s in shapes],
        scratch_shapes=[_slabs(tile, ATTN_W)] * 4,
        compiler_params=_cparams(1), name="combine_bwd")(dattn, *os_, *ls_, e)
    return outs[:3], outs[3:]


def _attn_bwd(qkv, do, cc, lse, cos_t, sin_t, g, dil):
    t = qkv.shape[0]
    length = t // dil
    nb = length // BLK
    per_step = min(nb, ATTN_BLOCKS_PER_STEP)
    nsteps = nb // per_step
    rows_per_step = per_step * BLK
    qkv_v = qkv.reshape(dil, length, GROUP_COLS)
    cos_v, sin_v = (a.reshape(dil, length, LANES) for a in (cos_t, sin_t))
    scale = HEAD_DIM ** -0.5
    dq_cols, dk_cols, dv_cols = (slice(i * ATTN_W, (i + 1) * ATTN_W) for i in range(3))

    def body(q_ref, kc_ref, kp_ref, vc_ref, vp_ref, do_ref, c_ref, l_ref, cosc, sinc, cosp, sinp,
             out_ref, acc, kwin, vwin, cwin, swin):
        n = pl.program_id(1)

        def one_block(b):
            valid, upper = _attn_masks(n * per_step + b)
            start = b * BLK if isinstance(b, int) else pl.multiple_of(b * BLK, BLK)
            rows, before, window = pl.ds(start, BLK), pl.ds(start, BLK), pl.ds(start, 2 * BLK)
            own = pl.ds(start + BLK, BLK)
            dq_parts, dkp_parts, dkc_parts, dvp_parts, dvc_parts = [], [], [], [], []
            npairs = ATTN_W // LANES
            slabs = [slice(p * LANES, (p + 1) * LANES) for p in range(npairs)]
            qss = [_stack_heads(q_ref[rows, sl], upper) for sl in slabs]
            doss = [_stack_heads(do_ref[rows, sl], upper) for sl in slabs]
            ss = [lax.dot_general(qss[p], kwin[window, slabs[p]], (NT, ((), ())), preferred_element_type=F32) for p in range(npairs)]
            dpvs = [lax.dot_general(doss[p], vwin[window, slabs[p]], (NT, ((), ())), preferred_element_type=F32)
                    for p in range(npairs)]
            pes = [jnp.exp(jnp.where(valid, ss[p], NEG) - _spread_heads(l_ref[rows, slabs[p]], upper)) for p in range(npairs)]
            dss = [(pes[p] * (dpvs[p] + _spread_heads(c_ref[rows, slabs[p]], upper))).astype(BF16) for p in range(npairs)]
            for p in range(npairs):
                qs, dos, ds = qss[p], doss[p], dss[p]
                dq2 = _unstack_heads(jnp.dot(ds, kwin[window, slabs[p]], preferred_element_type=F32), upper)
                dk2 = lax.dot_general(ds, qs, (TN, ((), ())), preferred_element_type=F32)
                dv2 = lax.dot_general(pes[p].astype(BF16), dos, (TN, ((), ())), preferred_element_type=F32)
                dq_parts.append(dq2)
                dkp_parts.append(dk2[:BLK])
                dkc_parts.append(dk2[BLK:])
                dvp_parts.append(dv2[:BLK])
                dvc_parts.append(dv2[BLK:])
            dq = _rope(jnp.concatenate(dq_parts, axis=1) * scale, cwin[own, :], swin[own, :])
            dkc = _rope(jnp.concatenate(dkc_parts, axis=1), cwin[own, :], swin[own, :])
            dkp = _rope(jnp.concatenate(dkp_parts, axis=1), cwin[before, :], swin[before, :])
            return dq, dkp, dkc, jnp.concatenate(dvp_parts, axis=1), jnp.concatenate(dvc_parts, axis=1)

        @pl.when(n < nsteps)
        def _():
            kwin[0:BLK] = kp_ref[...]
            kwin[BLK:] = kc_ref[...]
            vwin[0:BLK] = vp_ref[...]
            vwin[BLK:] = vc_ref[...]
            cwin[0:BLK] = cosp[...]
            cwin[BLK:] = cosc[...]
            swin[0:BLK] = -sinp[...]
            swin[BLK:] = -sinc[...]
            dq, dkp, dkc, dvp, dvc = one_block(0)
            last = slice(rows_per_step - BLK, rows_per_step)

            @pl.when(n > 0)
            def _():
                if per_step > 1:
                    out_ref[0:rows_per_step - BLK, :] = acc[0:rows_per_step - BLK, :].astype(BF16)
                out_ref[last, dq_cols] = acc[last, dq_cols].astype(BF16)
                out_ref[last, dk_cols] = (acc[last, dk_cols] + dkp).astype(BF16)
                out_ref[last, dv_cols] = (acc[last, dv_cols] + dvp).astype(BF16)

            acc[0:BLK, dq_cols] = dq
            acc[0:BLK, dk_cols] = dkc
            acc[0:BLK, dv_cols] = dvc

            def later(b, carry):
                dq, dkp, dkc, dvp, dvc = one_block(b)
                start = pl.multiple_of(b * BLK, BLK)
                before, rows = pl.ds(start - BLK, BLK), pl.ds(start, BLK)
                acc[before, dk_cols] += dkp
                acc[before, dv_cols] += dvp
                acc[rows, dq_cols] = dq
                acc[rows, dk_cols] = dkc
                acc[rows, dv_cols] = dvc
                return carry

            lax.fori_loop(1, per_step, later, 0)

        @pl.when(n == flush_at)
        def _():
            out_ref[...] = acc[...].astype(BF16)

    flush_at = nsteps - 1 if nsteps == 1 else nsteps
    out_lag = 0 if nsteps == 1 else 1
    nc = lambda n: jnp.minimum(n, nsteps - 1)
    npv = lambda n: jnp.maximum(jnp.minimum(n, nsteps - 1) * per_step - 1, 0)
    cur = lambda part: pl.BlockSpec((None, rows_per_step, ATTN_W), lambda r, n: (r, nc(n), part))
    prev = lambda part: pl.BlockSpec((None, BLK, ATTN_W), lambda r, n: (r, npv(n), part))
    row = pl.BlockSpec((None, rows_per_step, ATTN_W), lambda r, n: (r, nc(n), 0))
    tab_c = pl.BlockSpec((None, rows_per_step, LANES), lambda r, n: (r, nc(n), 0))
    tab_p = pl.BlockSpec((None, BLK, LANES), lambda r, n: (r, npv(n), 0))
    out_spec = pl.BlockSpec((None, rows_per_step, GROUP_COLS), lambda r, n: (r, jnp.maximum(n - out_lag, 0), 0))
    out = pl.pallas_call(
        body, grid=(dil, nsteps + out_lag),
        in_specs=[cur(0), cur(1), prev(1), cur(2), prev(2), row, row, row, tab_c, tab_c, tab_p, tab_p],
        out_specs=out_spec,
        out_shape=jax.ShapeDtypeStruct((dil, length, GROUP_COLS), BF16),
        scratch_shapes=[pltpu.VMEM((rows_per_step, GROUP_COLS), F32)]
        + [pltpu.VMEM((rows_per_step + BLK, ATTN_W), BF16)] * 2 + [pltpu.VMEM((rows_per_step + BLK, LANES), F32)] * 2,
        compiler_params=_cparams(2), name=f"attn_bwd_g{g}")(
            qkv_v, qkv_v, qkv_v, qkv_v, qkv_v, do, cc, lse, cos_v, sin_v, cos_v, sin_v)
    return out.reshape(t, GROUP_COLS)


SQRT_HALF = 0.7071067811865476
INV_SQRT_2PI = 0.3989422804014327


def _sgu_core(uv, g, b, w_ref, bias):
    cdf = 0.5 * (1.0 + lax.erf(uv * SQRT_HALF))
    z = uv * cdf
    u, v = z[:, :SGU_W], z[:, SGU_W:]
    mu = jnp.mean(v, axis=1, keepdims=True)
    xc = v - mu
    rs = lax.rsqrt(jnp.mean(xc * xc, axis=1, keepdims=True) + EPS)
    xhat = xc * rs
    vn = xhat * g + b
    row = lax.broadcasted_iota(jnp.int32, (SGU_CHUNK, SGU_CHUNK), 0)
    col = lax.broadcasted_iota(jnp.int32, (SGU_CHUNK, SGU_CHUNK), 1)
    tril = row >= col
    upper = lax.broadcasted_iota(jnp.int32, (SGU_CHUNK, LANES), 1) >= SGU_W // SGU_GROUPS
    ws, vlo, vhi, mixed = [], [], [], []
    for pr in range(SGU_W // LANES):
        sl = slice(pr * LANES, (pr + 1) * LANES)
        w0 = jnp.where(tril, w_ref[2 * pr], 0.0).astype(BF16)
        w1 = jnp.where(tril, w_ref[2 * pr + 1], 0.0).astype(BF16)
        vn2 = vn[:, sl]
        lo = jnp.where(upper, 0.0, vn2).astype(BF16)
        hi = jnp.where(upper, vn2, 0.0).astype(BF16)
        mixed.append(jnp.dot(w0, lo, preferred_element_type=F32) + jnp.dot(w1, hi, preferred_element_type=F32)
                     + bias[:, sl])
        ws.append((w0, w1))
        vlo.append(lo)
        vhi.append(hi)
    return cdf, u, xhat, rs, jnp.concatenate(mixed, axis=1), ws, vlo, vhi, tril, upper


SGU_STEP = 4 * SGU_CHUNK


def _for_chunks(step_rows, fn):
    def one(ci, carry):
        fn(pl.ds(pl.multiple_of(ci * SGU_CHUNK, SGU_CHUNK), SGU_CHUNK))
        return carry

    lax.fori_loop(0, step_rows // SGU_CHUNK, one, 0)


def _sgu_fwd(gu, ln_g, ln_b, w_s, bias_exp):
    t = gu.shape[0]
    step = min(t, SGU_STEP)

    def body(uv_ref, g_ref, b_ref, w_ref, bias_ref, o_ref):
        def chunk(rows):
            _, u, _, _, mixed, *_ = _sgu_core(uv_ref[rows, :].astype(F32), g_ref[...], b_ref[...], w_ref, bias_ref[...])
            o_ref[rows, :] = (u * mixed).astype(BF16)

        _for_chunks(step, chunk)

    return pl.pallas_call(
        body, grid=(t // step,),
        in_specs=[pl.BlockSpec((step, 2 * SGU_W), lambda n: (n, 0)), _full((1, SGU_W)), _full((1, SGU_W)),
                  _full((SGU_GROUPS, SGU_CHUNK, SGU_CHUNK)), _full((SGU_CHUNK, SGU_W))],
        out_specs=pl.BlockSpec((step, SGU_W), lambda n: (n, 0)),
        out_shape=jax.ShapeDtypeStruct((t, SGU_W), BF16),
        compiler_params=_cparams(1), name="sgu_fwd")(gu, ln_g, ln_b, w_s, bias_exp)


def _sgu_bwd(dproj, gu, dsgu, ln_g, ln_b, w_s, bias_exp):
    t = gu.shape[0]
    step = min(t, SGU_STEP)
    nsteps = t // step
    e = _head_sum_matrix()

    def body(dp_in, uv_ref, ds_ref, g_ref, b_ref, w_ref, bias_ref, e_ref, out_ref, dw_ref, dbias_ref, dg_ref, db_ref):
        n = pl.program_id(0)

        @pl.when(n == 0)
        def _():
            dw_ref[...] = jnp.zeros(dw_ref.shape, F32)
            dbias_ref[...] = jnp.zeros(dbias_ref.shape, F32)
            dg_ref[...] = jnp.zeros(dg_ref.shape, F32)
            db_ref[...] = jnp.zeros(db_ref.shape, F32)

        _for_chunks(step, functools.partial(chunk, uv_ref, ds_ref, g_ref, b_ref, w_ref, bias_ref, out_ref, dw_ref, dbias_ref,
                                            dg_ref, db_ref))

        @pl.when(n == nsteps - 1)
        def _():
            dbias_ref[...] = _group_sum(dbias_ref[...], e_ref[...])

    def chunk(uv_ref, ds_ref, g_ref, b_ref, w_ref, bias_ref, out_ref, dw_ref, dbias_ref, dg_ref, db_ref, rows):
        uv = uv_ref[rows, :].astype(F32)
        g = g_ref[...]
        cdf, u, xhat, rs, mixed, ws, vlo, vhi, tril, upper = _sgu_core(uv, g, b_ref[...], w_ref, bias_ref[...])
        dsg = ds_ref[rows, :]
        du = dsg * mixed
        dmixed = dsg * u
        dbias_ref[...] += dmixed
        dvn = []
        for pr in range(SGU_W // LANES):
            sl = slice(pr * LANES, (pr + 1) * LANES)
            dm2 = dmixed[:, sl]
            dlo = jnp.where(upper, 0.0, dm2).astype(BF16)
            dhi = jnp.where(upper, dm2, 0.0).astype(BF16)
            w0, w1 = ws[pr]
            dvn.append(lax.dot_general(w0, dlo, (TN, ((), ())), preferred_element_type=F32)
                       + lax.dot_general(w1, dhi, (TN, ((), ())), preferred_element_type=F32))
            dw0 = lax.dot_general(dlo, vlo[pr], (NT, ((), ())), preferred_element_type=F32)
            dw1 = lax.dot_general(dhi, vhi[pr], (NT, ((), ())), preferred_element_type=F32)
            dw_ref[2 * pr] += jnp.where(tril, dw0, 0.0)
            dw_ref[2 * pr + 1] += jnp.where(tril, dw1, 0.0)
        dvn = jnp.concatenate(dvn, axis=1)
        dg_ref[...] += jnp.sum(dvn * xhat, axis=0, keepdims=True)
        db_ref[...] += jnp.sum(dvn, axis=0, keepdims=True)
        dxh = dvn * g
        dv = rs * (dxh - jnp.mean(dxh, axis=1, keepdims=True) - xhat * jnp.mean(dxh * xhat, axis=1, keepdims=True))
        dz = jnp.concatenate([du, dv], axis=1)
        dgelu = cdf + uv * (INV_SQRT_2PI * jnp.exp(-0.5 * uv * uv))
        out_ref[rows, :] = (dz * dgelu).astype(BF16)

    outs = pl.pallas_call(
        body, grid=(nsteps,),
        in_specs=[pl.BlockSpec(memory_space=pl.ANY), pl.BlockSpec((step, 2 * SGU_W), lambda n: (n, 0)),
                  pl.BlockSpec((step, SGU_W), lambda n: (n, 0)), _full((1, SGU_W)), _full((1, SGU_W)),
                  _full((SGU_GROUPS, SGU_CHUNK, SGU_CHUNK)), _full((SGU_CHUNK, SGU_W)), _full((ATTN_W, ATTN_W))],
        out_specs=[pl.BlockSpec((step, 2 * SGU_W), lambda n: (n, 0)), _full((SGU_GROUPS, SGU_CHUNK, SGU_CHUNK)),
                   _full((SGU_CHUNK, SGU_W)), _full((1, SGU_W)), _full((1, SGU_W))],
        out_shape=[jax.ShapeDtypeStruct(dproj.shape, BF16), jax.ShapeDtypeStruct((SGU_GROUPS, SGU_CHUNK, SGU_CHUNK), F32),
                   jax.ShapeDtypeStruct((SGU_CHUNK, SGU_W), F32), jax.ShapeDtypeStruct((1, SGU_W), F32),
                   jax.ShapeDtypeStruct((1, SGU_W), F32)],
        input_output_aliases={0: 0},
        compiler_params=_cparams(1), name="sgu_bwd")(dproj, gu, dsgu, ln_g, ln_b, w_s, bias_exp, e)
    return outs


def _merge_fwd(attn, sgu, gu, x, w_pa, w_ps, w_out, g2):
    t = x.shape[0]
    tm = min(t, 512)

    def body(a_ref, s_ref, ga_ref, gb_ref, x_ref, wpa, wps, wo, g_ref, pa_ref, ps_ref, m_ref, x1_ref, h2_ref):
        pa = jnp.dot(a_ref[...], wpa[...], preferred_element_type=F32)
        ps = jnp.dot(s_ref[...], wps[...], preferred_element_type=F32)
        merged = (_sigmoid(ga_ref[...].astype(F32)) * pa + _sigmoid(gb_ref[...].astype(F32)) * ps).astype(BF16)
        x1 = x_ref[...] + jnp.dot(merged, wo[...], preferred_element_type=F32)
        xhat, _ = _rms_stats(x1)
        pa_ref[...] = pa.astype(BF16)
        ps_ref[...] = ps.astype(BF16)
        m_ref[...] = merged
        x1_ref[...] = x1
        h2_ref[...] = (xhat * g_ref[...]).astype(BF16)

    half = pl.BlockSpec((tm, ATTN_W), lambda i: (i, 0))
    full = pl.BlockSpec((tm, D_MODEL), lambda i: (i, 0))
    return pl.pallas_call(
        body, grid=(t // tm,),
        in_specs=[half, half, pl.BlockSpec((tm, D_MODEL), lambda i: (i, 1)), pl.BlockSpec((tm, D_MODEL), lambda i: (i, 2)),
                  full, _resident((ATTN_W, D_MODEL)), _resident((SGU_W, D_MODEL)), _resident((D_MODEL, D_MODEL)),
                  _full((1, D_MODEL))],
        out_specs=[full] * 5,
        out_shape=[jax.ShapeDtypeStruct((t, D_MODEL), BF16), jax.ShapeDtypeStruct((t, D_MODEL), BF16),
                   jax.ShapeDtypeStruct((t, D_MODEL), BF16), jax.ShapeDtypeStruct((t, D_MODEL), F32),
                   jax.ShapeDtypeStruct((t, D_MODEL), BF16)],
        compiler_params=_cparams(1), name="merge_fwd")(attn, sgu, gu, gu, x, w_pa, w_ps, w_out, g2)


def _merge_bwd(dx1b, gu, pa, ps, w_pa, w_ps, w_out):
    t = dx1b.shape[0]
    tm = min(t, 512)

    def body(d_ref, ga_ref, gb_ref, pa_ref, ps_ref, wpa, wps, wo, out_ref, dpa_ref, dps_ref, da_ref, dsg_ref):
        dm = lax.dot_general(d_ref[...], wo[...], (NT, ((), ())), preferred_element_type=F32)
        sa, sb = _sigmoid(ga_ref[...].astype(F32)), _sigmoid(gb_ref[...].astype(F32))
        dpa = (dm * sa).astype(BF16)
        dps = (dm * sb).astype(BF16)
        out_ref[:, 0:D_MODEL] = jnp.zeros((tm, D_MODEL), BF16)
        out_ref[:, D_MODEL:2 * D_MODEL] = (dm * pa_ref[...].astype(F32) * sa * (1.0 - sa)).astype(BF16)
        out_ref[:, 2 * D_MODEL:GU_COLS] = (dm * ps_ref[...].astype(F32) * sb * (1.0 - sb)).astype(BF16)
        dpa_ref[...] = dpa
        dps_ref[...] = dps
        da_ref[...] = lax.dot_general(dpa, wpa[...], (NT, ((), ())), preferred_element_type=F32)
        dsg_ref[...] = lax.dot_general(dps, wps[...], (NT, ((), ())), preferred_element_type=F32)

    half = pl.BlockSpec((tm, ATTN_W), lambda i: (i, 0))
    full = pl.BlockSpec((tm, D_MODEL), lambda i: (i, 0))
    return pl.pallas_call(
        body, grid=(t // tm,),
        in_specs=[full, pl.BlockSpec((tm, D_MODEL), lambda i: (i, 1)),
                  pl.BlockSpec((tm, D_MODEL), lambda i: (i, 2)), full, full,
                  _resident((ATTN_W, D_MODEL)), _resident((SGU_W, D_MODEL)), _resident((D_MODEL, D_MODEL))],
        out_specs=[pl.BlockSpec((tm, GU_COLS), lambda i: (i, 0)), full, full, half, half],
        out_shape=[jax.ShapeDtypeStruct((t, GU_COLS), BF16), jax.ShapeDtypeStruct((t, D_MODEL), BF16),
                   jax.ShapeDtypeStruct((t, D_MODEL), BF16), jax.ShapeDtypeStruct((t, ATTN_W), F32),
                   jax.ShapeDtypeStruct((t, SGU_W), F32)],
        compiler_params=_cparams(1), name="merge_bwd")(dx1b, gu, gu, pa, ps, w_pa, w_ps, w_out)


def _token_call(name, body, t, tm, ins, outs, reds=(), scratch=()):
    return pl.pallas_call(
        body, grid=(t // tm,), in_specs=[s for _, s in ins],
        out_specs=[o[2] for o in outs] + [_full(r) for r in reds],
        out_shape=[jax.ShapeDtypeStruct(o[0], o[1]) for o in outs] + [jax.ShapeDtypeStruct(r, F32) for r in reds],
        scratch_shapes=list(scratch), compiler_params=_cparams(1), name=name)(*[a for a, _ in ins])


def _rows_spec(tm, width):
    return pl.BlockSpec((tm, width), lambda i: (i, 0))


def _chips_spec(tm):
    return pl.BlockSpec((N_CHIPS, tm, FF_SHARD), lambda i: (0, i, 0))


def _zero_at_start(*refs):
    @pl.when(pl.program_id(0) == 0)
    def _():
        for r in refs:
            r[...] = jnp.zeros(r.shape, r.dtype)


def _ffn_fwd(h2, w_g, w_u):
    t = h2.shape[0]
    tm = min(t, 512)

    def body(h_ref, wg_ref, wu_ref, fa_ref, fb_ref, ff_ref):
        h = h_ref[...]
        for s in range(N_CHIPS):
            a = jnp.dot(h, wg_ref[s], preferred_element_type=F32)
            b = jnp.dot(h, wu_ref[s], preferred_element_type=F32)
            sg = _sigmoid(a)
            silu = a * sg
            fa_ref[s] = (b * (sg * (1.0 + a * (1.0 - sg)))).astype(BF16)
            fb_ref[s] = silu.astype(BF16)
            ff_ref[s] = (silu * b).astype(BF16)

    shp = (N_CHIPS, t, FF_SHARD)
    w_spec = _resident((N_CHIPS, D_MODEL, FF_SHARD))
    return _token_call("ffn_fwd", body, t, tm, [(h2, _rows_spec(tm, D_MODEL)), (w_g, w_spec), (w_u, w_spec)],
                       [(shp, BF16, _chips_spec(tm))] * 3)


def _ffn_down_loss(ff, w_d, x1, tgt, gf):
    t = x1.shape[0]
    tm = min(t, 512)

    def body(ff_ref, wd_ref, x1_ref, tgt_ref, g_ref, dx2_ref, dx2b_ref, loss_ref, dgf_ref):
        _zero_at_start(loss_ref, dgf_ref)
        acc = jnp.dot(ff_ref[0], wd_ref[0], preferred_element_type=F32)
        for s in range(1, N_CHIPS):
            acc = acc + jnp.dot(ff_ref[s], wd_ref[s], preferred_element_type=F32)
        x2 = x1_ref[...] + acc
        g = g_ref[...]
        xhat, rr = _rms_stats(x2)
        diff = xhat * g - tgt_ref[...]
        rows = jnp.sum(diff * diff, axis=1, keepdims=True)
        loss_ref[...] += jnp.broadcast_to(jnp.sum(rows, axis=0, keepdims=True) * (0.5 / D_MODEL), (1, LANES))
        dy = diff * (1.0 / D_MODEL)
        dgf_ref[...] += jnp.sum(dy * xhat, axis=0, keepdims=True)
        dx2 = _rms_bwd(dy, xhat, rr, g)
        dx2_ref[...] = dx2
        dx2b_ref[...] = dx2.astype(BF16)

    row = _rows_spec(tm, D_MODEL)
    return _token_call("ffn_down_loss", body, t, tm,
                       [(ff, _chips_spec(tm)), (w_d, _resident((N_CHIPS, FF_SHARD, D_MODEL))), (x1, row), (tgt, row),
                        (gf, _full((1, D_MODEL)))],
                       [((t, D_MODEL), F32, row), ((t, D_MODEL), BF16, row)], reds=[(1, LANES), (1, D_MODEL)])


def _ffn_bwd_act(dx2b, w_d, fa, fb):
    t = dx2b.shape[0]
    tm = min(t, 512)

    def body(d_ref, wd_ref, fa_ref, fb_ref, da_ref, db_ref):
        d = d_ref[...]
        for s in range(N_CHIPS):
            dff = lax.dot_general(d, wd_ref[s], (NT, ((), ())), preferred_element_type=F32)
            da_ref[s] = (dff * fa_ref[s].astype(F32)).astype(BF16)
            db_ref[s] = (dff * fb_ref[s].astype(F32)).astype(BF16)

    shp = (N_CHIPS, t, FF_SHARD)
    return _token_call("ffn_bwd_act", body, t, tm,
                       [(dx2b, _rows_spec(tm, D_MODEL)), (w_d, _resident((N_CHIPS, FF_SHARD, D_MODEL))),
                        (fa, _chips_spec(tm)), (fb, _chips_spec(tm))],
                       [(shp, BF16, _chips_spec(tm))] * 2)


def _ffn_bwd_in(da, db, w_g, w_u, x1, dx2, g2):
    t = x1.shape[0]
    tm = min(t, 512)

    def body(da_ref, db_ref, wg_ref, wu_ref, x1_ref, dx2_ref, g_ref, dx1_ref, dx1b_ref, dg_ref):
        _zero_at_start(dg_ref)
        acc = None
        for s in range(N_CHIPS):
            part = (lax.dot_general(da_ref[s], wg_ref[s], (NT, ((), ())), preferred_element_type=F32)
                    + lax.dot_general(db_ref[s], wu_ref[s], (NT, ((), ())), preferred_element_type=F32))
            acc = part if acc is None else acc + part
        xhat, rr = _rms_stats(x1_ref[...])
        dg_ref[...] += jnp.sum(acc * xhat, axis=0, keepdims=True)
        dx1 = dx2_ref[...] + _rms_bwd(acc, xhat, rr, g_ref[...])
        dx1_ref[...] = dx1
        dx1b_ref[...] = dx1.astype(BF16)

    row = _rows_spec(tm, D_MODEL)
    w_spec = _resident((N_CHIPS, D_MODEL, FF_SHARD))
    return _token_call("ffn_bwd_in", body, t, tm,
                       [(da, _chips_spec(tm)), (db, _chips_spec(tm)), (w_g, w_spec), (w_u, w_spec), (x1, row), (dx2, row),
                        (g2, _full((1, D_MODEL)))],
                       [((t, D_MODEL), F32, row), ((t, D_MODEL), BF16, row)], reds=[(1, D_MODEL)])


def _group_dh(d, w_refs):
    dh = None
    for part, w_ref in enumerate(w_refs):
        term = lax.dot_general(d[:, part * ATTN_W:(part + 1) * ATTN_W], w_ref[...], (NT, ((), ())),
                               preferred_element_type=F32)
        dh = term if dh is None else dh + term
    return dh


def _in_proj_bwd(dgu, dqkvs, w_in, x, dx1, g1):
    t = x.shape[0]
    tile = min(t, TILE)
    ngroups = len(DILATIONS)

    def body(*refs):
        dgu_ref, dq_refs = refs[0], refs[1:1 + ngroups]
        w0_ref, w1_ref = refs[1 + ngroups:3 + ngroups]
        wg_refs = [refs[3 + ngroups + 3 * g:6 + ngroups + 3 * g] for g in range(ngroups)]
        x_ref, dx1_ref, g_ref, dx_ref, dg_ref, slab = refs[3 + 4 * ngroups:]
        _zero_at_start(dg_ref)
        dh = lax.dot_general(dgu_ref[:, 0:GU_HALF], w0_ref[...], (NT, ((), ())), preferred_element_type=F32)
        dh = dh + lax.dot_general(dgu_ref[:, GU_HALF:], w1_ref[...], (NT, ((), ())), preferred_element_type=F32)
        dh = dh + _group_dh(dq_refs[0][0], wg_refs[0])
        for g in range(1, ngroups):
            dil = DILATIONS[g]
            part = _group_dh(dq_refs[g][...].reshape(tile, GROUP_COLS), wg_refs[g])
            for r in range(dil):
                _put_class_rows(slab, r, dil, part[r * (tile // dil):(r + 1) * (tile // dil)])
            dh = dh + _from_slabs(slab)
        xhat, rr = _rms_stats(x_ref[...])
        dg_ref[...] += jnp.sum(dh * xhat, axis=0, keepdims=True)
        dx_ref[...] = dx1_ref[...] + _rms_bwd(dh, xhat, rr, g_ref[...])

    row = _rows_spec(tile, D_MODEL)
    group_ins = [(dqkvs[g].reshape(d, t // d, GROUP_COLS), _group_spec(d, tile, GROUP_COLS)) for g, d in enumerate(DILATIONS)]
    w_specs = _gu_w_specs() + [s for g in range(ngroups) for s in _group_w_specs(g)]
    return _token_call(
        "in_proj_bwd", body, t, tile,
        [(dgu, _rows_spec(tile, GU_COLS))] + group_ins + [(w_in, s) for s in w_specs]
        + [(x, row), (dx1, row), (g1, _full((1, D_MODEL)))],
        [((t, D_MODEL), F32, row)], reds=[(1, D_MODEL)], scratch=[_slabs(tile, D_MODEL)])


WGRAD_TK = 2048


def _wgrad_mm(name, grid, a, a_spec, b, b_spec, acc_shape, out_shape, out_spec, dst=None):
    nk = grid[-1]

    def body(*refs):
        a_ref, b_ref, o_ref, acc_ref = refs[0], refs[1], refs[-2], refs[-1]
        k = pl.program_id(len(grid) - 1)
        part = lax.dot_general(a_ref[...], b_ref[...], (TN, ((), ())), preferred_element_type=F32)

        @pl.when(k == 0)
        def _():
            acc_ref[...] = part

        @pl.when(k > 0)
        def _():
            acc_ref[...] += part

        @pl.when(k == nk - 1)
        def _():
            o_ref[...] = acc_ref[...].astype(BF16)

    filled = [] if dst is None else [dst]
    return pl.pallas_call(
        body, grid=grid, in_specs=[a_spec, b_spec] + [pl.BlockSpec(memory_space=pl.ANY)] * len(filled),
        out_specs=out_spec, out_shape=jax.ShapeDtypeStruct(out_shape, BF16), scratch_shapes=[pltpu.VMEM(acc_shape, F32)],
        input_output_aliases={2: 0} if filled else {}, compiler_params=_cparams(len(grid)), name=name)(a, b, *filled)


def _wgrad_2d(name, a, b, tm, tn):
    t, k1 = a.shape
    n = b.shape[1]
    tk = min(t, WGRAD_TK)
    return _wgrad_mm(name, (k1 // tm, n // tn, t // tk), a, pl.BlockSpec((tk, tm), lambda i, j, k: (k, i)),
                     b, pl.BlockSpec((tk, tn), lambda i, j, k: (k, j)), (tm, tn), (k1, n),
                     pl.BlockSpec((tm, tn), lambda i, j, k: (i, j)))


def _wgrad_in(hs, dgu, dqkvs):
    t = dgu.shape[0]
    tk = min(t, WGRAD_TK)
    gu_block = QKV_BLOCKS * ATTN_W // GU_HALF
    parts = [(hs[0], dgu, GU_HALF, lambda j: j + gu_block)]
    parts += [(hs[g].reshape(t, D_MODEL), dqkvs[g], ATTN_W, lambda j, g=g: _w_in_block(j, g)) for g in range(3)]
    dst = None
    for n, (a, b, tn, block_of) in enumerate(parts):
        dst = _wgrad_mm(f"wgrad_in_{n}", (1, b.shape[1] // tn, t // tk),
                        a, pl.BlockSpec((tk, D_MODEL), lambda i, j, k: (k, 0)), b, pl.BlockSpec((tk, tn), lambda i, j, k: (k, j)),
                        (D_MODEL, tn), (D_MODEL, IN_COLS),
                        pl.BlockSpec((D_MODEL, tn), lambda i, j, k, block_of=block_of: (0, block_of(j))), dst=dst)
    return dst


def _wgrad_ff_in(name, h2, da):
    t = h2.shape[0]
    tk = min(t, WGRAD_TK)
    return _wgrad_mm(name, (N_CHIPS, 1, t // tk), h2, pl.BlockSpec((tk, D_MODEL), lambda i, j, k: (k, 0)),
                     da, pl.BlockSpec((None, tk, FF_SHARD), lambda i, j, k: (i, k, 0)), (D_MODEL, FF_SHARD),
                     (N_CHIPS, D_MODEL, FF_SHARD), pl.BlockSpec((None, D_MODEL, FF_SHARD), lambda i, j, k: (i, 0, 0)))


def _wgrad_ff_down(ff, dx2b):
    t = dx2b.shape[0]
    tk = min(t, WGRAD_TK)
    return _wgrad_mm("wgrad_ffn_down", (N_CHIPS, 1, t // tk), ff, pl.BlockSpec((None, tk, FF_SHARD), lambda i, j, k: (i, k, 0)),
                     dx2b, pl.BlockSpec((tk, D_MODEL), lambda i, j, k: (k, 0)), (FF_SHARD, D_MODEL),
                     (N_CHIPS, FF_SHARD, D_MODEL), pl.BlockSpec((None, FF_SHARD, D_MODEL), lambda i, j, k: (i, 0, 0)))


def _local_step(x, pos_col, tgt, g1, ln_g, ln_b, w_s, b_s, g2, gf, first_weight, late_weights, on_grads=None):
    tables = _rope_tables(pos_col)
    bias_exp = jnp.repeat(jnp.transpose(b_s), SGU_W // SGU_GROUPS, axis=1)

    hs = _norm_fwd(x, g1)
    w_p = first_weight(hs[0])
    gu, qkvs = _in_proj(hs, w_p, tables)
    os_, ls_ = [], []
    for g, dil in enumerate(DILATIONS):
        o, lse = _attn_fwd(qkvs[g], g, dil)
        os_.append(o)
        ls_.append(lse)
    attn = _combine_fwd(os_, ls_)
    sgu = _sgu_fwd(gu, ln_g, ln_b, w_s, bias_exp)
    w_pa, w_ps, w_out, w_g, w_u, w_d = late_weights(attn)
    pa, ps, merged, x1, h2 = _merge_fwd(attn, sgu, gu, x, w_pa, w_ps, w_out, g2)
    fa, fb, ff = _ffn_fwd(h2, w_g, w_u)
    dx2, dx2b, loss, dgf = _ffn_down_loss(ff, w_d, x1, tgt, gf)

    da, db = _ffn_bwd_act(dx2b, w_d, fa, fb)
    dw_d = _wgrad_ff_down(ff, dx2b)
    dx1, dx1b, dg2 = _ffn_bwd_in(da, db, w_g, w_u, x1, dx2, g2)
    dw_g = _wgrad_ff_in("wgrad_ffn_gate", h2, da)
    dw_u = _wgrad_ff_in("wgrad_ffn_up", h2, db)

    dgu, dpa, dps, dattn, dsgu = _merge_bwd(dx1b, gu, pa, ps, w_pa, w_ps, w_out)
    dw_out = _wgrad_2d("wgrad_out", merged, dx1b, D_MODEL, D_MODEL)
    dw_pa = _wgrad_2d("wgrad_proj_attn", attn, dpa, ATTN_W, D_MODEL)
    dw_ps = _wgrad_2d("wgrad_proj_sgu", sgu, dps, SGU_W, D_MODEL)
    if on_grads is not None:
        ln_g = ln_g + on_grads(1, dict(w_proj_attn=dw_pa, w_proj_sgu=dw_ps, w_out=dw_out, w_ffn_gate=dw_g, w_ffn_up=dw_u,
                                       w_ffn_down=dw_d))[:, :SGU_W]
    dgu, dw_s, dbias, dln_g, dln_b = _sgu_bwd(dgu, gu, dsgu, ln_g, ln_b, w_s, bias_exp)
    dos, ccs = _combine_bwd(dattn, os_, ls_)
    dqkvs = [_attn_bwd(qkvs[g], dos[g], ccs[g], ls_[g], *tables[g], g, dil) for g, dil in enumerate(DILATIONS)]
    dw_p = _wgrad_in(hs, dgu, dqkvs)
    if on_grads is not None:
        g1 = g1 + on_grads(0, dict(w_in=dw_p))
    dx, dg1 = _in_proj_bwd(dgu, dqkvs, w_p, x, dx1, g1)

    db_s = jnp.transpose(dbias[:, ::SGU_W // SGU_GROUPS])
    small = dict(loss=loss, norm1_g=dg1, sgu_ln_g=dln_g, sgu_ln_b=dln_b, w_spatial=dw_s, b_spatial=db_s,
                 norm2_g=dg2, final_g=dgf)
    big = dict(w_in=dw_p, w_proj_attn=dw_pa, w_proj_sgu=dw_ps, w_out=dw_out, w_ffn_gate=dw_g, w_ffn_up=dw_u,
               w_ffn_down=dw_d)
    return dx, big, small


def _ew(name, fn, ins, out_dtypes):
    shp = ins[0].shape
    rows, cols = shp
    tr = next((cand for cand in (256, 352, 128) if rows % cand == 0 and rows > cand), rows)

    def body(*refs):
        res = fn(*[r[...] for r in refs[:len(ins)]])
        for o_ref, v in zip(refs[len(ins):], res):
            o_ref[...] = v.astype(o_ref.dtype)

    spec = pl.BlockSpec((tr, cols), lambda i: (i, 0))
    return pl.pallas_call(
        body, grid=(rows // tr,), in_specs=[spec] * len(ins), out_specs=[spec] * len(out_dtypes),
        out_shape=[jax.ShapeDtypeStruct(shp, d) for d in out_dtypes],
        compiler_params=_cparams(1), name=name)(*ins)


def _adamw_math(g, w, m, v):
    m = ADAM_B1 * m + (1.0 - ADAM_B1) * g
    v = ADAM_B2 * v + (1.0 - ADAM_B2) * (g * g)
    m_hat = m / (1.0 - ADAM_B1 ** ADAM_STEP)
    v_hat = v / (1.0 - ADAM_B2 ** ADAM_STEP)
    delta = -ADAM_LR * (m_hat / (jnp.sqrt(v_hat) + ADAM_EPS) + ADAM_WD * w)
    return delta, m, v


def _adamw(name, g, w, m, v):
    return _ew(name, lambda g_, w_, m_, v_: (g_,) + _adamw_math(g_, w_, m_, v_), [g, w, m, v], [F32] * 4)


VMEM_SPEC = pl.BlockSpec(memory_space=pltpu.VMEM)


def _for_row_chunks(rows, fn):
    ck = next(c for c in (64, 32, 16) if rows % c == 0)

    def step(i, carry):
        fn(pl.multiple_of(i * ck, ck), ck)
        return carry

    lax.fori_loop(0, rows // ck, step, 0)


def _place():
    x, y, c = lax.axis_index("x"), lax.axis_index("y"), lax.axis_index("c")
    chips = [(1 - x, y), (x, 1 - y), (1 - x, 1 - y)]
    return x, y, c, 2 * x + y, chips


def _rows(ref, start, size):
    if len(ref.shape) == 2:
        return ref.at[pl.ds(start, size), :]
    return ref.at[:, pl.ds(start, size), :]


def _comm_call(name, body, ins, out_shapes, scratch, n_remote):
    return pl.pallas_call(
        body, in_specs=[VMEM_SPEC] * len(ins), out_specs=[VMEM_SPEC] * len(out_shapes),
        out_shape=out_shapes,
        scratch_shapes=list(scratch) + [pltpu.SemaphoreType.DMA((n_remote,)), pltpu.SemaphoreType.DMA((n_remote,))],
        compiler_params=pltpu.CompilerParams(vmem_limit_bytes=VMEM_LIMIT), name=name)(*ins)


def _gather_finish(name, shard, landed):
    k_rows, n = shard.shape
    kh = k_rows // 2

    def body(shard_ref, land_ref, out_ref, send, recv):
        x, y, c, me, chips = _place()
        passed = []
        for j, chip in enumerate(chips):
            theirs = 2 * chip[0] + chip[1]
            cp = pltpu.make_async_remote_copy(
                src_ref=land_ref.at[j], dst_ref=_rows(out_ref.at[theirs], c * kh, kh), send_sem=send.at[j],
                recv_sem=recv.at[j], device_id=(x, y, 1 - c), device_id_type=MESH)
            cp.start()
            passed.append(cp)
        mine = out_ref.at[me]

        def put_own(r0, ck):
            mine[pl.ds(r0, ck), :] = shard_ref[pl.ds(r0, ck), :]

        _for_row_chunks(k_rows, put_own)
        for j, chip in enumerate(chips):
            slot = out_ref.at[2 * chip[0] + chip[1]]

            def put_half(r0, ck, j=j, slot=slot):
                slot[pl.ds(pl.multiple_of(c * kh + r0, ck), ck), :] = land_ref[j, pl.ds(r0, ck), :]

            _for_row_chunks(kh, put_half)
        for j, chip in enumerate(chips):
            other = _rows(out_ref.at[2 * chip[0] + chip[1]], (1 - c) * kh, kh)
            pltpu.make_async_remote_copy(src_ref=other, dst_ref=other, send_sem=send.at[j], recv_sem=recv.at[j],
                                         device_id=(x, y, 1 - c), device_id_type=MESH).wait_recv()
        for cp in passed:
            cp.wait_send()

    return _comm_call(name, body, [shard, landed], [jax.ShapeDtypeStruct((N_CHIPS, k_rows, n), shard.dtype)], [], 3)[0]


HBM_SPEC = pl.BlockSpec(memory_space=pltpu.HBM)
SEM_SPEC = pl.BlockSpec(memory_space=pltpu.SEMAPHORE)
DATAFLOW = pltpu.SideEffectType.DATAFLOW_SIDE_EFFECTING
TOKEN_SHAPE = (1, D_MODEL)
N_PEERS = 7


def _peers():
    x, y, c = lax.axis_index("x"), lax.axis_index("y"), lax.axis_index("c")
    flip = lambda v, f: 1 - v if f else v
    return [(flip(x, k & 4), flip(y, k & 2), flip(c, k & 1)) for k in range(1, N_PEERS + 1)]


def _piece_shape(shape):
    return (shape[-2] // 2, shape[2] if len(shape) == 3 else shape[1] // N_CHIPS)


def _device_piece(ref, chip, core):
    kh, n4 = _piece_shape(ref.shape)
    if len(ref.shape) == 3:
        return ref.at[chip, pl.ds(core * kh, kh), :]
    return ref.at[pl.ds(core * kh, kh), pl.ds(chip * n4, n4)]


def _exchange_copies(partials, lands, send, recv):
    return [pltpu.make_async_remote_copy(
        src_ref=_device_piece(partials[t], 2 * px + py, pc), dst_ref=lands[t].at[k], send_sem=send.at[t * N_PEERS + k],
        recv_sem=recv.at[t * N_PEERS + k], device_id=(px, py, pc), device_id_type=MESH)
        for t in range(len(partials)) for k, (px, py, pc) in enumerate(_peers())]


def _gather_copies(shards, lands, send, recv):
    x, y, c, me, chips = _place()
    return [pltpu.make_async_remote_copy(
        src_ref=shards[t], dst_ref=lands[t].at[me], send_sem=send.at[t * 3 + j], recv_sem=recv.at[t * 3 + j],
        device_id=(*chip, c), device_id_type=MESH)
        for t in range(len(shards)) for j, chip in enumerate(chips)]


def _gather_half_copies(shards, lands, send, recv):
    x, y, c, me, chips = _place()
    return [pltpu.make_async_remote_copy(
        src_ref=_rows(shards[t], c * (shards[t].shape[0] // 2), shards[t].shape[0] // 2), dst_ref=lands[t].at[j],
        send_sem=send.at[t * 3 + j], recv_sem=recv.at[t * 3 + j], device_id=(*chip, c), device_id_type=MESH)
        for t in range(len(shards)) for j, chip in enumerate(chips)]


def _split_start(name, copies, per_tensor, srcs, land_shapes):
    nt = len(srcs)
    lands = [lax.empty(s, BF16) for s in land_shapes]
    nsem = nt * per_tensor

    def body(*refs):
        send, recv = refs[2 * nt], refs[2 * nt + 1]
        for cp in copies(refs[:nt], refs[nt:2 * nt], send, recv):
            cp.start()
        refs[-1][...] = jnp.zeros(TOKEN_SHAPE, F32)

    hbm = lambda a: pltpu.with_memory_space_constraint(a, pltpu.HBM)
    outs = pl.pallas_call(
        body, name=name,
        out_shape=[pltpu.SemaphoreType.DMA((nsem,)), pltpu.SemaphoreType.DMA((nsem,))]
        + [pltpu.HBM(s.shape, s.dtype) for s in srcs] + [pltpu.HBM(l.shape, l.dtype) for l in lands]
        + [jax.ShapeDtypeStruct(TOKEN_SHAPE, F32)],
        in_specs=[HBM_SPEC] * (2 * nt), out_specs=[SEM_SPEC, SEM_SPEC] + [HBM_SPEC] * (2 * nt) + [VMEM_SPEC],
        input_output_aliases={i: 2 + i for i in range(2 * nt)},
        compiler_params=pltpu.CompilerParams(has_side_effects=DATAFLOW))(*[hbm(a) for a in list(srcs) + lands])
    return outs[0], outs[1], outs[2:2 + nt], outs[2 + nt:2 + 2 * nt], outs[-1]


def _split_wait(name, copies, send, recv, srcs, lands, after):
    nt = len(srcs)

    def body(*refs):
        for cp in copies(refs[:nt], refs[nt:2 * nt], refs[2 * nt], refs[2 * nt + 1]):
            cp.wait_send()
            cp.wait_recv()

    outs = pl.pallas_call(
        body, name=name,
        out_shape=[pltpu.HBM(s.shape, s.dtype) for s in srcs] + [pltpu.HBM(l.shape, l.dtype) for l in lands],
        in_specs=[HBM_SPEC] * (2 * nt) + [SEM_SPEC, SEM_SPEC, pl.BlockSpec(memory_space=pl.ANY)],
        out_specs=[HBM_SPEC] * (2 * nt), input_output_aliases={i: i for i in range(2 * nt)},
        compiler_params=pltpu.CompilerParams(has_side_effects=DATAFLOW))(*srcs, *lands, send, recv, after)
    return outs[:nt], outs[nt:]


def _device_sum(name, partials, lands):
    nt = len(partials)

    def body(*refs):
        ins, slots, outs, owns = refs[:nt], refs[nt:2 * nt], refs[2 * nt:3 * nt], refs[3 * nt:4 * nt]
        send, recv, loc = refs[4 * nt:]
        x, y, c, me, chips = _place()
        sibling = (x, y, 1 - c)
        loads = [pltpu.make_async_copy(_device_piece(ins[t], me, c), owns[t], loc.at[t]) for t in range(nt)]
        for cp in loads:
            cp.start()
        handed = []
        for t in range(nt):
            kh = owns[t].shape[0]
            loads[t].wait()

            def add(r0, ck, own=owns[t], slot=slots[t], dst=outs[t], kh=kh):
                rows = pl.ds(r0, ck)
                acc = own[rows, :].astype(F32)
                for k in range(N_PEERS):
                    acc = acc + slot[k, rows, :].astype(F32)
                dst[pl.ds(pl.multiple_of(c * kh + r0, ck), ck), :] = acc

            _for_row_chunks(kh, add)
            rc = pltpu.make_async_remote_copy(
                src_ref=_rows(outs[t], c * kh, kh), dst_ref=_rows(outs[t], c * kh, kh), send_sem=send.at[t],
                recv_sem=recv.at[t], device_id=sibling, device_id_type=MESH)
            rc.start()
            handed.append(rc)
        for t in range(nt):
            kh = owns[t].shape[0]
            other = _rows(outs[t], (1 - c) * kh, kh)
            pltpu.make_async_remote_copy(
                src_ref=other, dst_ref=other, send_sem=send.at[t], recv_sem=recv.at[t],
                device_id=sibling, device_id_type=MESH).wait_recv()
        for rc in handed:
            rc.wait_send()

    pieces = [_piece_shape(p.shape) for p in partials]
    return pl.pallas_call(
        body, in_specs=[pl.BlockSpec(memory_space=pl.ANY)] * nt + [VMEM_SPEC] * nt, out_specs=[VMEM_SPEC] * nt,
        out_shape=[jax.ShapeDtypeStruct((2 * kh, n4), F32) for kh, n4 in pieces],
        scratch_shapes=[pltpu.VMEM(p, BF16) for p in pieces]
        + [pltpu.SemaphoreType.DMA((nt,)), pltpu.SemaphoreType.DMA((nt,)), pltpu.SemaphoreType.DMA((nt,))],
        compiler_params=pltpu.CompilerParams(vmem_limit_bytes=VMEM_LIMIT), name=name)(*partials, *lands)


VEC_SHAPE = (8, D_MODEL + LANES)
VEC_SLOTS = dict(norm1_g=(slice(0, 1), slice(0, D_MODEL)), norm2_g=(slice(1, 2), slice(0, D_MODEL)),
                 final_g=(slice(2, 3), slice(0, D_MODEL)), sgu_ln_g=(slice(3, 4), slice(0, SGU_W)),
                 sgu_ln_b=(slice(3, 4), slice(SGU_W, 2 * SGU_W)), b_spatial=(slice(0, 8), slice(D_MODEL, D_MODEL + LANES)),
                 loss=(slice(4, 5), slice(0, LANES)))
VEC_PARAMS = ("norm1_g", "norm2_g", "final_g", "sgu_ln_g", "sgu_ln_b", "b_spatial")
SMALL_PARAMS = VEC_PARAMS + ("w_spatial",)
W_SPATIAL_2D = (SGU_GROUPS * SGU_CHUNK, SGU_CHUNK)


def _small_step(partials, w, m, v):
    def shape2d(name):
        if name == "w_spatial":
            return W_SPATIAL_2D
        rows, cols = VEC_SLOTS[name]
        return (rows.stop - rows.start, cols.stop - cols.start)

    g_names = VEC_PARAMS + ("loss", "w_spatial")
    ng, npar = len(g_names), len(SMALL_PARAMS)

    def pack(dst, parts):
        dst[...] = jnp.zeros(VEC_SHAPE, F32)
        for n, ref in parts.items():
            if n in VEC_SLOTS:
                dst[VEC_SLOTS[n]] = ref[...]

    def reduce_body(*refs):
        g_in = dict(zip(g_names, refs[:ng]))
        vec_out, ws_out, vec, vec_pair, vec_slot, ws_pair, ws_slot, send, recv = refs[ng:]
        x, y, c, me, chips = _place()
        sibling = (x, y, 1 - c)
        pack(vec, g_in)
        copies = []

        def allreduce(k0, src, pair, slot):
            first = pltpu.make_async_remote_copy(src_ref=src, dst_ref=pair, send_sem=send.at[k0], recv_sem=recv.at[k0],
                                                 device_id=sibling, device_id_type=MESH)
            first.start()
            first.wait_recv()
            slot[me] = src[...] + pair[...]
            arrivals = []
            for j, chip in enumerate(chips):
                theirs = 2 * chip[0] + chip[1]
                rc = pltpu.make_async_remote_copy(src_ref=slot.at[me], dst_ref=slot.at[me], send_sem=send.at[k0 + 1 + j],
                                                  recv_sem=recv.at[k0 + 1 + j], device_id=(*chip, c), device_id_type=MESH)
                rc.start()
                arrivals.append(pltpu.make_async_remote_copy(
                    src_ref=slot.at[theirs], dst_ref=slot.at[theirs], send_sem=send.at[k0 + 1 + j],
                    recv_sem=recv.at[k0 + 1 + j], device_id=(*chip, c), device_id_type=MESH))
                copies.append(rc)
            copies.append(first)
            return arrivals

        arrivals = allreduce(0, vec, vec_pair, vec_slot) + allreduce(4, g_in["w_spatial"], ws_pair, ws_slot)
        for a in arrivals:
            a.wait_recv()
        vec_out[...] = ((vec_slot[0] + vec_slot[1]) + vec_slot[2]) + vec_slot[3]

        def spatial(r0, ck):
            rows = pl.ds(r0, ck)
            ws_out[rows, :] = ((ws_slot[0, rows, :] + ws_slot[1, rows, :]) + ws_slot[2, rows, :]) + ws_slot[3, rows, :]

        _for_row_chunks(W_SPATIAL_2D[0], spatial)
        for rc in copies:
            rc.wait_send()

    g_vec, g_ws = pl.pallas_call(
        reduce_body, in_specs=[VMEM_SPEC] * ng, out_specs=[VMEM_SPEC] * 2,
        out_shape=[jax.ShapeDtypeStruct(VEC_SHAPE, F32), jax.ShapeDtypeStruct(W_SPATIAL_2D, F32)],
        scratch_shapes=[pltpu.VMEM(VEC_SHAPE, F32), pltpu.VMEM(VEC_SHAPE, F32), pltpu.VMEM((N_CHIPS,) + VEC_SHAPE, F32),
                        pltpu.VMEM(W_SPATIAL_2D, F32), pltpu.VMEM((N_CHIPS,) + W_SPATIAL_2D, F32),
                        pltpu.SemaphoreType.DMA((8,)), pltpu.SemaphoreType.DMA((8,))],
        name="small_params_allreduce")(*[partials[n].reshape(shape2d(n)) for n in g_names])

    def update_body(*refs):
        gv_ref, gw_ref = refs[:2]
        w_in, m_in, v_in = (dict(zip(SMALL_PARAMS, refs[2 + k * npar:2 + (k + 1) * npar])) for k in range(3))
        o0 = 2 + 3 * npar
        g_out = dict(zip(g_names, refs[o0:o0 + ng]))
        d_out, m_out, v_out = (dict(zip(SMALL_PARAMS, refs[o0 + ng + k * npar:o0 + ng + (k + 1) * npar])) for k in range(3))
        vw, vm, vv = refs[o0 + ng + 3 * npar:]
        pack(vw, w_in)
        pack(vm, m_in)
        pack(vv, v_in)
        d_vec, m_vec, v_vec = _adamw_math(gv_ref[...], vw[...], vm[...], vv[...])
        vw[...] = d_vec
        vm[...] = m_vec
        vv[...] = v_vec
        for n in VEC_PARAMS + ("loss",):
            g_out[n][...] = gv_ref[VEC_SLOTS[n]]
        for n in VEC_PARAMS:
            d_out[n][...] = vw[VEC_SLOTS[n]]
            m_out[n][...] = vm[VEC_SLOTS[n]]
            v_out[n][...] = vv[VEC_SLOTS[n]]

        def spatial(r0, ck):
            rows = pl.ds(r0, ck)
            g = gw_ref[rows, :]
            d_, m_, v_ = _adamw_math(g, w_in["w_spatial"][rows, :], m_in["w_spatial"][rows, :], v_in["w_spatial"][rows, :])
            g_out["w_spatial"][rows, :] = g
            d_out["w_spatial"][rows, :] = d_
            m_out["w_spatial"][rows, :] = m_
            v_out["w_spatial"][rows, :] = v_

        _for_row_chunks(W_SPATIAL_2D[0], spatial)

    ins = [g_vec, g_ws]
    for src in (w, m, v):
        ins += [src[n].reshape(shape2d(n)) for n in SMALL_PARAMS]
    out_shapes = [jax.ShapeDtypeStruct(shape2d(n), F32) for n in g_names + SMALL_PARAMS * 3]
    outs = pl.pallas_call(
        update_body, in_specs=[VMEM_SPEC] * len(ins), out_specs=[VMEM_SPEC] * len(out_shapes), out_shape=out_shapes,
        scratch_shapes=[pltpu.VMEM(VEC_SHAPE, F32)] * 3, name="small_params_update")(*ins)
    grads = dict(zip(g_names, outs[:ng]))
    rest = [dict(zip(SMALL_PARAMS, outs[ng + k * npar:ng + (k + 1) * npar])) for k in range(3)]
    return grads, rest[0], rest[1], rest[2]


BIG = ("w_in", "w_proj_attn", "w_proj_sgu", "w_out", "w_ffn_gate", "w_ffn_up", "w_ffn_down")
COMM_GROUPS = (("w_in",), ("w_proj_attn", "w_proj_sgu", "w_out", "w_ffn_gate", "w_ffn_up", "w_ffn_down"))
WEIGHTS = ("norm1_g", "w_in", "sgu_ln_g", "sgu_ln_b", "w_spatial", "b_spatial", "w_proj_attn", "w_proj_sgu", "w_out",
           "norm2_g", "w_ffn_gate", "w_ffn_up", "w_ffn_down", "final_g")


def _cols_from_chips(g):
    return jnp.transpose(g, (1, 0, 2)).reshape(g.shape[1], N_CHIPS * g.shape[2])


def kernel(x, positions, norm1_g, w_in, sgu_ln_g, sgu_ln_b, w_spatial, b_spatial, w_proj_attn, w_proj_sgu, w_out, norm2_g, w_ffn_gate, w_ffn_up, w_ffn_down, final_g, loss_target, m_norm1_g, m_w_in, m_sgu_ln_g, m_sgu_ln_b, m_w_spatial, m_b_spatial, m_w_proj_attn, m_w_proj_sgu, m_w_out, m_norm2_g, m_w_ffn_gate, m_w_ffn_up, m_w_ffn_down, m_final_g, v_norm1_g, v_w_in, v_sgu_ln_g, v_sgu_ln_b, v_w_spatial, v_b_spatial, v_w_proj_attn, v_w_proj_sgu, v_w_out, v_norm2_g, v_w_ffn_gate, v_w_ffn_up, v_w_ffn_down, v_final_g):
    w = dict(norm1_g=norm1_g, w_in=w_in, sgu_ln_g=sgu_ln_g, sgu_ln_b=sgu_ln_b, w_spatial=w_spatial, b_spatial=b_spatial,
             w_proj_attn=w_proj_attn, w_proj_sgu=w_proj_sgu, w_out=w_out, norm2_g=norm2_g, w_ffn_gate=w_ffn_gate,
             w_ffn_up=w_ffn_up, w_ffn_down=w_ffn_down, final_g=final_g)
    m = dict(norm1_g=m_norm1_g, w_in=m_w_in, sgu_ln_g=m_sgu_ln_g, sgu_ln_b=m_sgu_ln_b, w_spatial=m_w_spatial,
             b_spatial=m_b_spatial, w_proj_attn=m_w_proj_attn, w_proj_sgu=m_w_proj_sgu, w_out=m_w_out, norm2_g=m_norm2_g,
             w_ffn_gate=m_w_ffn_gate, w_ffn_up=m_w_ffn_up, w_ffn_down=m_w_ffn_down, final_g=m_final_g)
    v = dict(norm1_g=v_norm1_g, w_in=v_w_in, sgu_ln_g=v_sgu_ln_g, sgu_ln_b=v_sgu_ln_b, w_spatial=v_w_spatial,
             b_spatial=v_b_spatial, w_proj_attn=v_w_proj_attn, w_proj_sgu=v_w_proj_sgu, w_out=v_w_out, norm2_g=v_norm2_g,
             w_ffn_gate=v_w_ffn_gate, w_ffn_up=v_w_ffn_up, w_ffn_down=v_w_ffn_down, final_g=v_final_g)
    t = x.shape[1]

    shards = {n: _ew(f"cast_{n}", lambda a: (a,), [w[n][0]], [BF16])[0] for n in BIG}
    late = COMM_GROUPS[1]
    k_in, n_in = shards["w_in"].shape
    *first, token = _split_start("gather_start_0", _gather_half_copies, 3, [shards["w_in"]], [(3, k_in // 2, n_in)])
    pending = {}

    def first_weight(after):
        srcs, filled = _split_wait("gather_wait_0", _gather_half_copies, *first, after)
        gath_in, late_shards = lax.optimization_barrier(
            (_gather_finish("gather_finish_0", srcs[0], filled[0]), [shards[n] for n in late]))
        *pending["late"], _ = _split_start(
            "gather_start_1", _gather_copies, 3, late_shards, [(N_CHIPS,) + s.shape for s in late_shards])
        return _cols_from_chips(gath_in)

    def late_weights(after):
        srcs, filled = _split_wait("gather_wait_1", _gather_copies, *pending["late"], after)
        me = 2 * lax.axis_index("x") + lax.axis_index("y")
        gath = {n: lax.dynamic_update_slice(f, s[None], (me, 0, 0)) for n, f, s in zip(late, filled, srcs)}
        return (_cols_from_chips(gath["w_proj_attn"]), _cols_from_chips(gath["w_proj_sgu"]),
                gath["w_out"].reshape(D_MODEL, D_MODEL), gath["w_ffn_gate"], gath["w_ffn_up"], gath["w_ffn_down"])

    exchanges = {}

    def on_grads(i, partials):
        if "w_out" in partials:
            partials["w_out"] = partials["w_out"].reshape(N_CHIPS, D_MODEL // N_CHIPS, D_MODEL)
        parts = [partials[n] for n in COMM_GROUPS[i]]
        *exchanges[i], started = _split_start(
            f"rs_exchange_start_{i}", _exchange_copies, N_PEERS, parts, [(N_PEERS,) + _piece_shape(p.shape) for p in parts])
        return started

    dx, _, small = _local_step(
        x[0], positions.reshape(t, 1), loss_target[0], norm1_g + token, sgu_ln_g, sgu_ln_b, w_spatial[0], b_spatial[0],
        norm2_g, final_g.reshape(1, D_MODEL), first_weight, late_weights, on_grads=on_grads)

    grads = {}
    for i in (1, 0):
        parts, filled = _split_wait(f"rs_exchange_wait_{i}", _exchange_copies, *exchanges[i], dx)
        grads.update(zip(COMM_GROUPS[i], _device_sum(f"rs_device_sum_{i}", parts, filled)))

    delta, new_m, new_v = {}, {}, {}
    for n in BIG:
        shp = w[n].shape
        flip = jnp.transpose if shp[-1] % LANES else (lambda a: a)
        outs = _adamw(f"adamw_{n}", flip(grads[n]), flip(w[n][0]), flip(m[n][0]), flip(v[n][0]))
        grads[n], delta[n], new_m[n], new_v[n] = (flip(a).reshape(shp) for a in outs)

    g_s, d_s, m_s, v_s = _small_step(small, w, m, v)
    loss = g_s["loss"][0, 0]
    for n in SMALL_PARAMS:
        shp = w[n].shape
        grads[n], delta[n], new_m[n], new_v[n] = (a[n].reshape(shp) for a in (g_s, d_s, m_s, v_s))

    return (loss, dx.reshape(x.shape), *[grads[n] for n in WEIGHTS], *[delta[n] for n in WEIGHTS],
            *[new_m[n] for n in WEIGHTS], *[new_v[n] for n in WEIGHTS])
```

```python
import functools

import numpy as np
import jax
import jax.numpy as jnp
from jax import lax
from jax.experimental import pallas as pl
from jax.experimental.pallas import tpu as pltpu

F32, BF16 = jnp.float32, jnp.bfloat16
MESH = pl.DeviceIdType.MESH

D_MODEL = 1024
HEAD_DIM = 64
ATTN_W = 512
DILATIONS = (1, 4, 16)
BLK = 128
ATTN_BLOCKS_PER_STEP = 4
ROPE_DIM = 16
ROPE_THETA = 500000.0
SGU_W = 512
SGU_CHUNK = 128
SGU_GROUPS = 8
D_FF = 2816
N_CHIPS = 4
FF_SHARD = D_FF // N_CHIPS
IN_COLS = 7680
EPS = 1e-6
NEG = -1e30
LANES = 128
VMEM_LIMIT = 52 * 1024 * 1024

ADAM_LR, ADAM_B1, ADAM_B2, ADAM_EPS, ADAM_WD, ADAM_STEP = 0.001, 0.9, 0.999, 1e-08, 0.01, 10

QKV_BLOCKS = 9


def _w_in_block(part, g):
    return part * len(DILATIONS) + g


def _cparams(ngrid):
    return pltpu.CompilerParams(dimension_semantics=("arbitrary",) * ngrid, vmem_limit_bytes=VMEM_LIMIT)


def _full(shape):
    return pl.BlockSpec(shape, lambda *_: (0,) * len(shape))


def _resident(shape):
    return pl.BlockSpec(shape, lambda *_: (0,) * len(shape), pipeline_mode=pl.Buffered(1))


NT = ((1,), (1,))
TN = ((0,), (0,))


def _rope(v, cos_t, sin_t):
    half = ROPE_DIM // 2
    first = (lax.broadcasted_iota(jnp.int32, cos_t.shape, 1) % HEAD_DIM) < half
    outs = []
    for cs in range(v.shape[1] // LANES):
        x = v[:, cs * LANES:(cs + 1) * LANES]
        partner = jnp.where(first, pltpu.roll(x, LANES - half, axis=1), pltpu.roll(x, half, axis=1))
        outs.append(x * cos_t + partner * sin_t)
    return outs[0] if len(outs) == 1 else jnp.concatenate(outs, axis=1)


def _spread_heads(v2, upper):
    other = pltpu.roll(v2, HEAD_DIM, axis=1)
    h0 = jnp.where(upper, other, v2)
    h1 = jnp.where(upper, v2, other)
    return jnp.concatenate([jnp.concatenate([h0, h0], axis=1), jnp.concatenate([h1, h1], axis=1)], axis=0)


def _sigmoid(v):
    return 0.5 * jnp.tanh(0.5 * v) + 0.5


def _rms_stats(v):
    r = lax.rsqrt(jnp.mean(v * v, axis=-1, keepdims=True) + EPS)
    return v * r, r


def _rms_bwd(dy, xhat, r, g):
    dxh = dy * g
    return r * (dxh - xhat * jnp.mean(dxh * xhat, axis=-1, keepdims=True))


def _head_sum_matrix():
    idx = np.arange(ATTN_W) // HEAD_DIM
    return jnp.asarray((idx[:, None] == idx[None, :]).astype(np.float32), dtype=BF16)


def _group_sum(v, e):
    hi = v.astype(BF16)
    lo = (v - hi.astype(F32)).astype(BF16)
    return jnp.dot(hi, e, preferred_element_type=F32) + jnp.dot(lo, e, preferred_element_type=F32)


TILE = 512


def _to_slabs(slab_ref, v):
    for cs in range(slab_ref.shape[0]):
        slab_ref[cs] = v[:, cs * LANES:(cs + 1) * LANES]


def _from_slabs(slab_ref):
    return jnp.concatenate([slab_ref[cs] for cs in range(slab_ref.shape[0])], axis=1)


def _class_rows(slab_ref, r, dil):
    n = slab_ref.shape[1] // dil
    return jnp.concatenate([slab_ref.at[cs][pl.ds(r, n, stride=dil), :] for cs in range(slab_ref.shape[0])], axis=1)


def _put_class_rows(slab_ref, r, dil, v):
    n = slab_ref.shape[1] // dil
    for cs in range(slab_ref.shape[0]):
        slab_ref.at[cs][pl.ds(r, n, stride=dil), :] = v[:, cs * LANES:(cs + 1) * LANES]


def _natural_from_group(slab_ref, grp_ref):
    dil = grp_ref.shape[0]
    for r in range(dil):
        _put_class_rows(slab_ref, r, dil, grp_ref[r].astype(F32))
    return _from_slabs(slab_ref)


def _group_from_natural(slab_ref, grp_ref, v):
    dil = grp_ref.shape[0]
    _to_slabs(slab_ref, v)
    for r in range(dil):
        grp_ref[r] = _class_rows(slab_ref, r, dil).astype(grp_ref.dtype)


def _group_spec(dil, tile, width):
    return pl.BlockSpec((dil, tile // dil, width), lambda i, *_: (0, i, 0))


def _slabs(tile, width):
    return pltpu.VMEM((width // LANES, tile, LANES), F32)


def _rope_consts():
    lane = np.arange(LANES) % HEAD_DIM
    fi = lane % (ROPE_DIM // 2)
    invf = np.where(lane < ROPE_DIM, ROPE_THETA ** (-(2.0 * fi) / ROPE_DIM), 0.0)
    sgn = np.where(lane < ROPE_DIM // 2, -1.0, np.where(lane < ROPE_DIM, 1.0, 0.0))
    return (jnp.asarray(invf.astype(np.float32)).reshape(1, LANES), jnp.asarray(sgn.astype(np.float32)).reshape(1, LANES))


def _rope_tables(pos_col):
    t = pos_col.shape[0]
    tile = min(t, TILE)
    invf, sgn = _rope_consts()

    def body(p_ref, f_ref, s_ref, c0, s0, c1, s1, c2, s2, slab_c, slab_s):
        ang = p_ref[...].astype(F32) * f_ref[...]
        cos, sin = jnp.cos(ang), jnp.sin(ang) * s_ref[...]
        c0[...] = cos
        s0[...] = sin
        _group_from_natural(slab_c, c1, cos)
        _group_from_natural(slab_s, s1, sin)
        for r in range(DILATIONS[2]):
            c2[r] = _class_rows(slab_c, r, DILATIONS[2])
            s2[r] = _class_rows(slab_s, r, DILATIONS[2])

    nat = pl.BlockSpec((tile, LANES), lambda i: (i, 0))
    specs, shapes = [nat, nat], [(t, LANES)] * 2
    for d in DILATIONS[1:]:
        specs += [_group_spec(d, tile, LANES)] * 2
        shapes += [(d, t // d, LANES)] * 2
    outs = pl.pallas_call(
        body, grid=(t // tile,),
        in_specs=[pl.BlockSpec((tile, 1), lambda i: (i, 0)), _full((1, LANES)), _full((1, LANES))],
        out_specs=specs, out_shape=[jax.ShapeDtypeStruct(s, F32) for s in shapes],
        scratch_shapes=[_slabs(tile, LANES)] * 2,
        compiler_params=_cparams(1), name="rope_tables")(pos_col, invf, sgn)
    return [(outs[2 * g].reshape(t, LANES), outs[2 * g + 1].reshape(t, LANES)) for g in range(len(DILATIONS))]


def _norm_fwd(x, g):
    t = x.shape[0]
    tile = min(t, TILE)

    def body(x_ref, g_ref, h0_ref, h1_ref, h2_ref, slab):
        xhat, _ = _rms_stats(x_ref[...])
        hn = xhat * g_ref[...]
        h0_ref[...] = hn.astype(BF16)
        _group_from_natural(slab, h1_ref, hn)
        for r in range(DILATIONS[2]):
            h2_ref[r] = _class_rows(slab, r, DILATIONS[2]).astype(BF16)

    nat = pl.BlockSpec((tile, D_MODEL), lambda i: (i, 0))
    return pl.pallas_call(
        body, grid=(t // tile,),
        in_specs=[nat, _full((1, D_MODEL))],
        out_specs=[nat] + [_group_spec(d, tile, D_MODEL) for d in DILATIONS[1:]],
        out_shape=[jax.ShapeDtypeStruct((t, D_MODEL), BF16)]
        + [jax.ShapeDtypeStruct((d, t // d, D_MODEL), BF16) for d in DILATIONS[1:]],
        scratch_shapes=[_slabs(tile, D_MODEL)],
        compiler_params=_cparams(1), name="norm1_fwd")(x, g)


GU_COLS = 3072
GROUP_COLS = 1536
GU_HALF = GU_COLS // 2


def _w_in_spec(width, block):
    return pl.BlockSpec((D_MODEL, width), lambda i: (0, block), pipeline_mode=pl.Buffered(1))


def _gu_w_specs():
    first = QKV_BLOCKS * ATTN_W // GU_HALF
    return [_w_in_spec(GU_HALF, first), _w_in_spec(GU_HALF, first + 1)]


def _group_w_specs(g):
    return [_w_in_spec(ATTN_W, _w_in_block(part, g)) for part in range(3)]


def _in_proj(hs, w_in, tables):
    t = hs[0].shape[0]
    tm = min(t, 1024)

    def body_gu(h_ref, w0_ref, w1_ref, o_ref):
        h = h_ref[...]
        o_ref[:, 0:GU_HALF] = jnp.dot(h, w0_ref[...], preferred_element_type=F32).astype(BF16)
        o_ref[:, GU_HALF:] = jnp.dot(h, w1_ref[...], preferred_element_type=F32).astype(BF16)

    gu = _token_call("in_proj_gates_uv", body_gu, t, tm,
                     [(hs[0], _rows_spec(tm, D_MODEL))] + [(w_in, s) for s in _gu_w_specs()],
                     [((t, GU_COLS), BF16, _rows_spec(tm, GU_COLS))])[0]

    qkvs = []
    for g in range(len(DILATIONS)):

        def body_qkv(h_ref, wq_ref, wk_ref, wv_ref, cos_ref, sin_ref, o_ref):
            h = h_ref[...]
            cos_w, sin_w = cos_ref[...], sin_ref[...]
            q = jnp.dot(h, wq_ref[...], preferred_element_type=F32)
            o_ref[:, 0:ATTN_W] = (_rope(q, cos_w, sin_w) * HEAD_DIM ** -0.5).astype(BF16)
            k = jnp.dot(h, wk_ref[...], preferred_element_type=F32)
            o_ref[:, ATTN_W:2 * ATTN_W] = _rope(k, cos_w, sin_w).astype(BF16)
            o_ref[:, 2 * ATTN_W:] = jnp.dot(h, wv_ref[...], preferred_element_type=F32).astype(BF16)

        cos_t, sin_t = tables[g]
        qkvs.append(_token_call(
            f"in_proj_qkv_g{g}", body_qkv, t, tm,
            [(hs[g].reshape(t, D_MODEL), _rows_spec(tm, D_MODEL))] + [(w_in, s) for s in _group_w_specs(g)]
            + [(cos_t, _rows_spec(tm, LANES)), (sin_t, _rows_spec(tm, LANES))],
            [((t, GROUP_COLS), BF16, _rows_spec(tm, GROUP_COLS))])[0])
    return gu, qkvs


def _attn_masks(n):
    row = lax.broadcasted_iota(jnp.int32, (2 * BLK, 2 * BLK), 0) % BLK
    col = lax.broadcasted_iota(jnp.int32, (2 * BLK, 2 * BLK), 1)
    diff = BLK + row - col
    valid = (diff >= 0) & (diff <= BLK) & ((col >= BLK) | (n > 0))
    upper = lax.broadcasted_iota(jnp.int32, (BLK, LANES), 1) >= HEAD_DIM
    return valid, upper


def _stack_heads(v2, upper):
    zero = jnp.zeros_like(v2)
    return jnp.concatenate([jnp.where(upper, zero, v2), jnp.where(upper, v2, zero)], axis=0)


def _unstack_heads(v, upper):
    return jnp.where(upper, v[BLK:], v[:BLK])


def _attn_fwd(qkv, g, dil):
    t = qkv.shape[0]
    length = t // dil
    nb = length // BLK
    per_step = min(nb, ATTN_BLOCKS_PER_STEP)
    view = qkv.reshape(dil, length, GROUP_COLS)

    def body(q_ref, kc_ref, kp_ref, vc_ref, vp_ref, o_ref, l_ref, kwin, vwin):
        n = pl.program_id(1)
        kwin[0:BLK] = kp_ref[...]
        kwin[BLK:] = kc_ref[...]
        vwin[0:BLK] = vp_ref[...]
        vwin[BLK:] = vc_ref[...]

        def block(b, carry):
            valid, upper = _attn_masks(n * per_step + b)
            rows = pl.ds(pl.multiple_of(b * BLK, BLK), BLK)
            window = pl.ds(pl.multiple_of(b * BLK, BLK), 2 * BLK)
            slabs = [slice(p * LANES, (p + 1) * LANES) for p in range(ATTN_W // LANES)]
            ss = [lax.dot_general(_stack_heads(q_ref[rows, sl], upper), kwin[window, sl], (NT, ((), ())),
                                  preferred_element_type=F32) for sl in slabs]
            soft = []
            for s in ss:
                s = jnp.where(valid, s, NEG)
                m = jnp.max(s, axis=1, keepdims=True)
                pe = jnp.exp(s - m)
                soft.append((m, pe, jnp.sum(pe, axis=1, keepdims=True)))
            for sl, (m, pe, den) in zip(slabs, soft):
                o = jnp.dot(pe.astype(BF16), vwin[window, sl], preferred_element_type=F32) / den
                lse = jnp.broadcast_to(m + jnp.log(den), (2 * BLK, LANES))
                o_ref[rows, sl] = _unstack_heads(o, upper).astype(BF16)
                l_ref[rows, sl] = _unstack_heads(lse, upper)
            return carry

        lax.fori_loop(0, per_step, block, 0)

    rows = per_step * BLK
    cur = lambda part: pl.BlockSpec((None, rows, ATTN_W), lambda r, n: (r, n, part))
    prev = lambda part: pl.BlockSpec((None, BLK, ATTN_W), lambda r, n: (r, jnp.maximum(n * per_step - 1, 0), part))
    out_spec = pl.BlockSpec((None, rows, ATTN_W), lambda r, n: (r, n, 0))
    return pl.pallas_call(
        body, grid=(dil, nb // per_step),
        in_specs=[cur(0), cur(1), prev(1), cur(2), prev(2)],
        out_specs=[out_spec, out_spec],
        out_shape=[jax.ShapeDtypeStruct((dil, length, ATTN_W), BF16), jax.ShapeDtypeStruct((dil, length, ATTN_W), F32)],
        scratch_shapes=[pltpu.VMEM((rows + BLK, ATTN_W), BF16)] * 2,
        compiler_params=_cparams(2), name=f"attn_fwd_g{g}")(view, view, view, view, view)


def _alphas(l0, l1, l2):
    m = jnp.maximum(jnp.maximum(l0, l1), l2)
    e0, e1, e2 = jnp.exp(l0 - m), jnp.exp(l1 - m), jnp.exp(l2 - m)
    inv = 1.0 / (e0 + e1 + e2)
    return e0 * inv, e1 * inv, e2 * inv


def _natural_group_values(o_refs, l_refs, slabs):
    os_ = [o_refs[0][0].astype(F32)] + [_natural_from_group(slabs[2 * g - 2], o_refs[g]) for g in (1, 2)]
    ls_ = [l_refs[0][0]] + [_natural_from_group(slabs[2 * g - 1], l_refs[g]) for g in (1, 2)]
    return os_, ls_


def _combine_fwd(os_, ls_):
    t = os_[0].shape[1]
    tile = min(t, TILE)

    def body(o0, o1, o2, l0, l1, l2, a_ref, *slabs):
        ov, lv = _natural_group_values((o0, o1, o2), (l0, l1, l2), slabs)
        a0, a1, a2 = _alphas(*lv)
        a_ref[...] = (a0 * ov[0] + a1 * ov[1] + a2 * ov[2]).astype(BF16)

    specs = [_group_spec(d, tile, ATTN_W) for d in DILATIONS]
    return pl.pallas_call(
        body, grid=(t // tile,), in_specs=specs * 2, out_specs=pl.BlockSpec((tile, ATTN_W), lambda i: (i, 0)),
        out_shape=jax.ShapeDtypeStruct((t, ATTN_W), BF16),
        scratch_shapes=[_slabs(tile, ATTN_W)] * 4,
        compiler_params=_cparams(1), name="combine_fwd")(*os_, *ls_)


def _combine_bwd(dattn, os_, ls_):
    t = dattn.shape[0]
    tile = min(t, TILE)
    e = _head_sum_matrix()

    def body(d_ref, o0, o1, o2, l0, l1, l2, e_ref, do0, do1, do2, c0, c1, c2, *slabs):
        ov, lv = _natural_group_values((o0, o1, o2), (l0, l1, l2), slabs)
        alphas = _alphas(*lv)
        d = d_ref[...]
        attn = alphas[0] * ov[0] + alphas[1] * ov[1] + alphas[2] * ov[2]
        s = _group_sum(d * attn, e_ref[...])
        do0[0] = (alphas[0] * d).astype(BF16)
        c0[0] = -alphas[0] * s
        for g, do_ref, c_ref in ((1, do1, c1), (2, do2, c2)):
            _group_from_natural(slabs[2 * g - 2], do_ref, alphas[g] * d)
            _group_from_natural(slabs[2 * g - 1], c_ref, -alphas[g] * s)

    specs = [_group_spec(d, tile, ATTN_W) for d in DILATIONS]
    shapes = [(d, t // d, ATTN_W) for d in DILATIONS]
    outs = pl.pallas_call(
        body, grid=(t // tile,),
        in_specs=[pl.BlockSpec((tile, ATTN_W), lambda i: (i, 0))] + specs * 2 + [_full((ATTN_W, ATTN_W))],
        out_specs=specs * 2,
        out_shape=[jax.ShapeDtypeStruct(s, BF16) for s in shapes] + [jax.ShapeDtypeStruct(s, F32) for s in shapes],
        scratch_shapes=[_slabs(tile, ATTN_W)] * 4,
        compiler_params=_cparams(1), name="combine_bwd")(dattn, *os_, *ls_, e)
    return outs[:3], outs[3:]


def _attn_bwd(qkv, do, cc, lse, cos_t, sin_t, g, dil):
    t = qkv.shape[0]
    length = t // dil
    nb = length // BLK
    per_step = min(nb, ATTN_BLOCKS_PER_STEP)
    nsteps = nb // per_step
    rows_per_step = per_step * BLK
    qkv_v = qkv.reshape(dil, length, GROUP_COLS)
    cos_v, sin_v = (a.reshape(dil, length, LANES) for a in (cos_t, sin_t))
    scale = HEAD_DIM ** -0.5
    dq_cols, dk_cols, dv_cols = (slice(i * ATTN_W, (i + 1) * ATTN_W) for i in range(3))

    def body(q_ref, kc_ref, kp_ref, vc_ref, vp_ref, do_ref, c_ref, l_ref, cosc, sinc, cosp, sinp,
             out_ref, acc, kwin, vwin, cwin, swin):
        n = pl.program_id(1)

        def one_block(b):
            valid, upper = _attn_masks(n * per_step + b)
            start = b * BLK if isinstance(b, int) else pl.multiple_of(b * BLK, BLK)
            rows, before, window = pl.ds(start, BLK), pl.ds(start, BLK), pl.ds(start, 2 * BLK)
            own = pl.ds(start + BLK, BLK)
            dq_parts, dkp_parts, dkc_parts, dvp_parts, dvc_parts = [], [], [], [], []
            npairs = ATTN_W // LANES
            slabs = [slice(p * LANES, (p + 1) * LANES) for p in range(npairs)]
            qss = [_stack_heads(q_ref[rows, sl], upper) for sl in slabs]
            doss = [_stack_heads(do_ref[rows, sl], upper) for sl in slabs]
            ss = [lax.dot_general(qss[p], kwin[window, slabs[p]], (NT, ((), ())), preferred_element_type=F32) for p in range(npairs)]
            dpvs = [lax.dot_general(doss[p], vwin[window, slabs[p]], (NT, ((), ())), preferred_element_type=F32)
                    for p in range(npairs)]
            pes = [jnp.exp(jnp.where(valid, ss[p], NEG) - _spread_heads(l_ref[rows, slabs[p]], upper)) for p in range(npairs)]
            dss = [(pes[p] * (dpvs[p] + _spread_heads(c_ref[rows, slabs[p]], upper))).astype(BF16) for p in range(npairs)]
            for p in range(npairs):
                qs, dos, ds = qss[p], doss[p], dss[p]
                dq2 = _unstack_heads(jnp.dot(ds, kwin[window, slabs[p]], preferred_element_type=F32), upper)
                dk2 = lax.dot_general(ds, qs, (TN, ((), ())), preferred_element_type=F32)
                dv2 = lax.dot_general(pes[p].astype(BF16), dos, (TN, ((), ())), preferred_element_type=F32)
                dq_parts.append(dq2)
                dkp_parts.append(dk2[:BLK])
                dkc_parts.append(dk2[BLK:])
                dvp_parts.append(dv2[:BLK])
                dvc_parts.append(dv2[BLK:])
            dq = _rope(jnp.concatenate(dq_parts, axis=1) * scale, cwin[own, :], swin[own, :])
            dkc = _rope(jnp.concatenate(dkc_parts, axis=1), cwin[own, :], swin[own, :])
            dkp = _rope(jnp.concatenate(dkp_parts, axis=1), cwin[before, :], swin[before, :])
            return dq, dkp, dkc, jnp.concatenate(dvp_parts, axis=1), jnp.concatenate(dvc_parts, axis=1)

        @pl.when(n < nsteps)
        def _():
            kwin[0:BLK] = kp_ref[...]
            kwin[BLK:] = kc_ref[...]
            vwin[0:BLK] = vp_ref[...]
            vwin[BLK:] = vc_ref[...]
            cwin[0:BLK] = cosp[...]
            cwin[BLK:] = cosc[...]
            swin[0:BLK] = -sinp[...]
            swin[BLK:] = -sinc[...]
            dq, dkp, dkc, dvp, dvc = one_block(0)
            last = slice(rows_per_step - BLK, rows_per_step)

            @pl.when(n > 0)
            def _():
                if per_step > 1:
                    out_ref[0:rows_per_step - BLK, :] = acc[0:rows_per_step - BLK, :].astype(BF16)
                out_ref[last, dq_cols] = acc[last, dq_cols].astype(BF16)
                out_ref[last, dk_cols] = (acc[last, dk_cols] + dkp).astype(BF16)
                out_ref[last, dv_cols] = (acc[last, dv_cols] + dvp).astype(BF16)

            acc[0:BLK, dq_cols] = dq
            acc[0:BLK, dk_cols] = dkc
            acc[0:BLK, dv_cols] = dvc

            def later(b, carry):
                dq, dkp, dkc, dvp, dvc = one_block(b)
                start = pl.multiple_of(b * BLK, BLK)
                before, rows = pl.ds(start - BLK, BLK), pl.ds(start, BLK)
                acc[before, dk_cols] += dkp
                acc[before, dv_cols] += dvp
                acc[rows, dq_cols] = dq
                acc[rows, dk_cols] = dkc
                acc[rows, dv_cols] = dvc
                return carry

            lax.fori_loop(1, per_step, later, 0)

        @pl.when(n == flush_at)
        def _():
            out_ref[...] = acc[...].astype(BF16)

    flush_at = nsteps - 1 if nsteps == 1 else nsteps
    out_lag = 0 if nsteps == 1 else 1
    nc = lambda n: jnp.minimum(n, nsteps - 1)
    npv = lambda n: jnp.maximum(jnp.minimum(n, nsteps - 1) * per_step - 1, 0)
    cur = lambda part: pl.BlockSpec((None, rows_per_step, ATTN_W), lambda r, n: (r, nc(n), part))
    prev = lambda part: pl.BlockSpec((None, BLK, ATTN_W), lambda r, n: (r, npv(n), part))
    row = pl.BlockSpec((None, rows_per_step, ATTN_W), lambda r, n: (r, nc(n), 0))
    tab_c = pl.BlockSpec((None, rows_per_step, LANES), lambda r, n: (r, nc(n), 0))
    tab_p = pl.BlockSpec((None, BLK, LANES), lambda r, n: (r, npv(n), 0))
    out_spec = pl.BlockSpec((None, rows_per_step, GROUP_COLS), lambda r, n: (r, jnp.maximum(n - out_lag, 0), 0))
    out = pl.pallas_call(
        body, grid=(dil, nsteps + out_lag),
        in_specs=[cur(0), cur(1), prev(1), cur(2), prev(2), row, row, row, tab_c, tab_c, tab_p, tab_p],
        out_specs=out_spec,
        out_shape=jax.ShapeDtypeStruct((dil, length, GROUP_COLS), BF16),
        scratch_shapes=[pltpu.VMEM((rows_per_step, GROUP_COLS), F32)]
        + [pltpu.VMEM((rows_per_step + BLK, ATTN_W), BF16)] * 2 + [pltpu.VMEM((rows_per_step + BLK, LANES), F32)] * 2,
        compiler_params=_cparams(2), name=f"attn_bwd_g{g}")(
            qkv_v, qkv_v, qkv_v, qkv_v, qkv_v, do, cc, lse, cos_v, sin_v, cos_v, sin_v)
    return out.reshape(t, GROUP_COLS)


SQRT_HALF = 0.7071067811865476
INV_SQRT_2PI = 0.3989422804014327


def _sgu_core(uv, g, b, w_ref, bias):
    cdf = 0.5 * (1.0 + lax.erf(uv * SQRT_HALF))
    z = uv * cdf
    u, v = z[:, :SGU_W], z[:, SGU_W:]
    mu = jnp.mean(v, axis=1, keepdims=True)
    xc = v - mu
    rs = lax.rsqrt(jnp.mean(xc * xc, axis=1, keepdims=True) + EPS)
    xhat = xc * rs
    vn = xhat * g + b
    row = lax.broadcasted_iota(jnp.int32, (SGU_CHUNK, SGU_CHUNK), 0)
    col = lax.broadcasted_iota(jnp.int32, (SGU_CHUNK, SGU_CHUNK), 1)
    tril = row >= col
    upper = lax.broadcasted_iota(jnp.int32, (SGU_CHUNK, LANES), 1) >= SGU_W // SGU_GROUPS
    ws, vlo, vhi, mixed = [], [], [], []
    for pr in range(SGU_W // LANES):
        sl = slice(pr * LANES, (pr + 1) * LANES)
        w0 = jnp.where(tril, w_ref[2 * pr], 0.0).astype(BF16)
        w1 = jnp.where(tril, w_ref[2 * pr + 1], 0.0).astype(BF16)
        vn2 = vn[:, sl]
        lo = jnp.where(upper, 0.0, vn2).astype(BF16)
        hi = jnp.where(upper, vn2, 0.0).astype(BF16)
        mixed.append(jnp.dot(w0, lo, preferred_element_type=F32) + jnp.dot(w1, hi, preferred_element_type=F32)
                     + bias[:, sl])
        ws.append((w0, w1))
        vlo.append(lo)
        vhi.append(hi)
    return cdf, u, xhat, rs, jnp.concatenate(mixed, axis=1), ws, vlo, vhi, tril, upper


SGU_STEP = 4 * SGU_CHUNK


def _for_chunks(step_rows, fn):
    def one(ci, carry):
        fn(pl.ds(pl.multiple_of(ci * SGU_CHUNK, SGU_CHUNK), SGU_CHUNK))
        return carry

    lax.fori_loop(0, step_rows // SGU_CHUNK, one, 0)


def _sgu_fwd(gu, ln_g, ln_b, w_s, bias_exp):
    t = gu.shape[0]
    step = min(t, SGU_STEP)

    def body(uv_ref, g_ref, b_ref, w_ref, bias_ref, o_ref):
        def chunk(rows):
            _, u, _, _, mixed, *_ = _sgu_core(uv_ref[rows, :].astype(F32), g_ref[...], b_ref[...], w_ref, bias_ref[...])
            o_ref[rows, :] = (u * mixed).astype(BF16)

        _for_chunks(step, chunk)

    return pl.pallas_call(
        body, grid=(t // step,),
        in_specs=[pl.BlockSpec((step, 2 * SGU_W), lambda n: (n, 0)), _full((1, SGU_W)), _full((1, SGU_W)),
                  _full((SGU_GROUPS, SGU_CHUNK, SGU_CHUNK)), _full((SGU_CHUNK, SGU_W))],
        out_specs=pl.BlockSpec((step, SGU_W), lambda n: (n, 0)),
        out_shape=jax.ShapeDtypeStruct((t, SGU_W), BF16),
        compiler_params=_cparams(1), name="sgu_fwd")(gu, ln_g, ln_b, w_s, bias_exp)


def _sgu_bwd(dproj, gu, dsgu, ln_g, ln_b, w_s, bias_exp):
    t = gu.shape[0]
    step = min(t, SGU_STEP)
    nsteps = t // step
    e = _head_sum_matrix()

    def body(dp_in, uv_ref, ds_ref, g_ref, b_ref, w_ref, bias_ref, e_ref, out_ref, dw_ref, dbias_ref, dg_ref, db_ref):
        n = pl.program_id(0)

        @pl.when(n == 0)
        def _():
            dw_ref[...] = jnp.zeros(dw_ref.shape, F32)
            dbias_ref[...] = jnp.zeros(dbias_ref.shape, F32)
            dg_ref[...] = jnp.zeros(dg_ref.shape, F32)
            db_ref[...] = jnp.zeros(db_ref.shape, F32)

        _for_chunks(step, functools.partial(chunk, uv_ref, ds_ref, g_ref, b_ref, w_ref, bias_ref, out_ref, dw_ref, dbias_ref,
                                            dg_ref, db_ref))

        @pl.when(n == nsteps - 1)
        def _():
            dbias_ref[...] = _group_sum(dbias_ref[...], e_ref[...])

    def chunk(uv_ref, ds_ref, g_ref, b_ref, w_ref, bias_ref, out_ref, dw_ref, dbias_ref, dg_ref, db_ref, rows):
        uv = uv_ref[rows, :].astype(F32)
        g = g_ref[...]
        cdf, u, xhat, rs, mixed, ws, vlo, vhi, tril, upper = _sgu_core(uv, g, b_ref[...], w_ref, bias_ref[...])
        dsg = ds_ref[rows, :]
        du = dsg * mixed
        dmixed = dsg * u
        dbias_ref[...] += dmixed
        dvn = []
        for pr in range(SGU_W // LANES):
            sl = slice(pr * LANES, (pr + 1) * LANES)
            dm2 = dmixed[:, sl]
            dlo = jnp.where(upper, 0.0, dm2).astype(BF16)
            dhi = jnp.where(upper, dm2, 0.0).astype(BF16)
            w0, w1 = ws[pr]
            dvn.append(lax.dot_general(w0, dlo, (TN, ((), ())), preferred_element_type=F32)
                       + lax.dot_general(w1, dhi, (TN, ((), ())), preferred_element_type=F32))
            dw0 = lax.dot_general(dlo, vlo[pr], (NT, ((), ())), preferred_element_type=F32)
            dw1 = lax.dot_general(dhi, vhi[pr], (NT, ((), ())), preferred_element_type=F32)
            dw_ref[2 * pr] += jnp.where(tril, dw0, 0.0)
            dw_ref[2 * pr + 1] += jnp.where(tril, dw1, 0.0)
        dvn = jnp.concatenate(dvn, axis=1)
        dg_ref[...] += jnp.sum(dvn * xhat, axis=0, keepdims=True)
        db_ref[...] += jnp.sum(dvn, axis=0, keepdims=True)
        dxh = dvn * g
        dv = rs * (dxh - jnp.mean(dxh, axis=1, keepdims=True) - xhat * jnp.mean(dxh * xhat, axis=1, keepdims=True))
        dz = jnp.concatenate([du, dv], axis=1)
        dgelu = cdf + uv * (INV_SQRT_2PI * jnp.exp(-0.5 * uv * uv))
        out_ref[rows, :] = (dz * dgelu).astype(BF16)

    outs = pl.pallas_call(
        body, grid=(nsteps,),
        in_specs=[pl.BlockSpec(memory_space=pl.ANY), pl.BlockSpec((step, 2 * SGU_W), lambda n: (n, 0)),
                  pl.BlockSpec((step, SGU_W), lambda n: (n, 0)), _full((1, SGU_W)), _full((1, SGU_W)),
                  _full((SGU_GROUPS, SGU_CHUNK, SGU_CHUNK)), _full((SGU_CHUNK, SGU_W)), _full((ATTN_W, ATTN_W))],
        out_specs=[pl.BlockSpec((step, 2 * SGU_W), lambda n: (n, 0)), _full((SGU_GROUPS, SGU_CHUNK, SGU_CHUNK)),
                   _full((SGU_CHUNK, SGU_W)), _full((1, SGU_W)), _full((1, SGU_W))],
        out_shape=[jax.ShapeDtypeStruct(dproj.shape, BF16), jax.ShapeDtypeStruct((SGU_GROUPS, SGU_CHUNK, SGU_CHUNK), F32),
                   jax.ShapeDtypeStruct((SGU_CHUNK, SGU_W), F32), jax.ShapeDtypeStruct((1, SGU_W), F32),
                   jax.ShapeDtypeStruct((1, SGU_W), F32)],
        input_output_aliases={0: 0},
        compiler_params=_cparams(1), name="sgu_bwd")(dproj, gu, dsgu, ln_g, ln_b, w_s, bias_exp, e)
    return outs


def _merge_fwd(attn, sgu, gu, x, w_pa, w_ps, w_out, g2):
    t = x.shape[0]
    tm = min(t, 512)

    def body(a_ref, s_ref, ga_ref, gb_ref, x_ref, wpa, wps, wo, g_ref, pa_ref, ps_ref, m_ref, x1_ref, h2_ref):
        pa = jnp.dot(a_ref[...], wpa[...], preferred_element_type=F32)
        ps = jnp.dot(s_ref[...], wps[...], preferred_element_type=F32)
        merged = (_sigmoid(ga_ref[...].astype(F32)) * pa + _sigmoid(gb_ref[...].astype(F32)) * ps).astype(BF16)
        x1 = x_ref[...] + jnp.dot(merged, wo[...], preferred_element_type=F32)
        xhat, _ = _rms_stats(x1)
        pa_ref[...] = pa.astype(BF16)
        ps_ref[...] = ps.astype(BF16)
        m_ref[...] = merged
        x1_ref[...] = x1
        h2_ref[...] = (xhat * g_ref[...]).astype(BF16)

    half = pl.BlockSpec((tm, ATTN_W), lambda i: (i, 0))
    full = pl.BlockSpec((tm, D_MODEL), lambda i: (i, 0))
    return pl.pallas_call(
        body, grid=(t // tm,),
        in_specs=[half, half, pl.BlockSpec((tm, D_MODEL), lambda i: (i, 1)), pl.BlockSpec((tm, D_MODEL), lambda i: (i, 2)),
                  full, _resident((ATTN_W, D_MODEL)), _resident((SGU_W, D_MODEL)), _resident((D_MODEL, D_MODEL)),
                  _full((1, D_MODEL))],
        out_specs=[full] * 5,
        out_shape=[jax.ShapeDtypeStruct((t, D_MODEL), BF16), jax.ShapeDtypeStruct((t, D_MODEL), BF16),
                   jax.ShapeDtypeStruct((t, D_MODEL), BF16), jax.ShapeDtypeStruct((t, D_MODEL), F32),
                   jax.ShapeDtypeStruct((t, D_MODEL), BF16)],
        compiler_params=_cparams(1), name="merge_fwd")(attn, sgu, gu, gu, x, w_pa, w_ps, w_out, g2)


def _merge_bwd(dx1b, gu, pa, ps, w_pa, w_ps, w_out):
    t = dx1b.shape[0]
    tm = min(t, 512)

    def body(d_ref, ga_ref, gb_ref, pa_ref, ps_ref, wpa, wps, wo, out_ref, dpa_ref, dps_ref, da_ref, dsg_ref):
        dm = lax.dot_general(d_ref[...], wo[...], (NT, ((), ())), preferred_element_type=F32)
        sa, sb = _sigmoid(ga_ref[...].astype(F32)), _sigmoid(gb_ref[...].astype(F32))
        dpa = (dm * sa).astype(BF16)
        dps = (dm * sb).astype(BF16)
        out_ref[:, 0:D_MODEL] = jnp.zeros((tm, D_MODEL), BF16)
        out_ref[:, D_MODEL:2 * D_MODEL] = (dm * pa_ref[...].astype(F32) * sa * (1.0 - sa)).astype(BF16)
        out_ref[:, 2 * D_MODEL:GU_COLS] = (dm * ps_ref[...].astype(F32) * sb * (1.0 - sb)).astype(BF16)
        dpa_ref[...] = dpa
        dps_ref[...] = dps
        da_ref[...] = lax.dot_general(dpa, wpa[...], (NT, ((), ())), preferred_element_type=F32)
        dsg_ref[...] = lax.dot_general(dps, wps[...], (NT, ((), ())), preferred_element_type=F32)

    half = pl.BlockSpec((tm, ATTN_W), lambda i: (i, 0))
    full = pl.BlockSpec((tm, D_MODEL), lambda i: (i, 0))
    return pl.pallas_call(
        body, grid=(t // tm,),
        in_specs=[full, pl.BlockSpec((tm, D_MODEL), lambda i: (i, 1)),
                  pl.BlockSpec((tm, D_MODEL), lambda i: (i, 2)), full, full,
                  _resident((ATTN_W, D_MODEL)), _resident((SGU_W, D_MODEL)), _resident((D_MODEL, D_MODEL))],
        out_specs=[pl.BlockSpec((tm, GU_COLS), lambda i: (i, 0)), full, full, half, half],
        out_shape=[jax.ShapeDtypeStruct((t, GU_COLS), BF16), jax.ShapeDtypeStruct((t, D_MODEL), BF16),
                   jax.ShapeDtypeStruct((t, D_MODEL), BF16), jax.ShapeDtypeStruct((t, ATTN_W), F32),
                   jax.ShapeDtypeStruct((t, SGU_W), F32)],
        compiler_params=_cparams(1), name="merge_bwd")(dx1b, gu, gu, pa, ps, w_pa, w_ps, w_out)


def _token_call(name, body, t, tm, ins, outs, reds=(), scratch=()):
    return pl.pallas_call(
        body, grid=(t // tm,), in_specs=[s for _, s in ins],
        out_specs=[o[2] for o in outs] + [_full(r) for r in reds],
        out_shape=[jax.ShapeDtypeStruct(o[0], o[1]) for o in outs] + [jax.ShapeDtypeStruct(r, F32) for r in reds],
        scratch_shapes=list(scratch), compiler_params=_cparams(1), name=name)(*[a for a, _ in ins])


def _rows_spec(tm, width):
    return pl.BlockSpec((tm, width), lambda i: (i, 0))


def _chips_spec(tm):
    return pl.BlockSpec((N_CHIPS, tm, FF_SHARD), lambda i: (0, i, 0))


def _zero_at_start(*refs):
    @pl.when(pl.program_id(0) == 0)
    def _():
        for r in refs:
            r[...] = jnp.zeros(r.shape, r.dtype)


def _ffn_fwd(h2, w_g, w_u):
    t = h2.shape[0]
    tm = min(t, 512)

    def body(h_ref, wg_ref, wu_ref, fa_ref, fb_ref, ff_ref):
        h = h_ref[...]
        for s in range(N_CHIPS):
            a = jnp.dot(h, wg_ref[s], preferred_element_type=F32)
            b = jnp.dot(h, wu_ref[s], preferred_element_type=F32)
            sg = _sigmoid(a)
            silu = a * sg
            fa_ref[s] = (b * (sg * (1.0 + a * (1.0 - sg)))).astype(BF16)
            fb_ref[s] = silu.astype(BF16)
            ff_ref[s] = (silu * b).astype(BF16)

    shp = (N_CHIPS, t, FF_SHARD)
    w_spec = _resident((N_CHIPS, D_MODEL, FF_SHARD))
    return _token_call("ffn_fwd", body, t, tm, [(h2, _rows_spec(tm, D_MODEL)), (w_g, w_spec), (w_u, w_spec)],
                       [(shp, BF16, _chips_spec(tm))] * 3)


def _ffn_down_loss(ff, w_d, x1, tgt, gf):
    t = x1.shape[0]
    tm = min(t, 512)

    def body(ff_ref, wd_ref, x1_ref, tgt_ref, g_ref, dx2_ref, dx2b_ref, loss_ref, dgf_ref):
        _zero_at_start(loss_ref, dgf_ref)
        acc = jnp.dot(ff_ref[0], wd_ref[0], preferred_element_type=F32)
        for s in range(1, N_CHIPS):
            acc = acc + jnp.dot(ff_ref[s], wd_ref[s], preferred_element_type=F32)
        x2 = x1_ref[...] + acc
        g = g_ref[...]
        xhat, rr = _rms_stats(x2)
        diff = xhat * g - tgt_ref[...]
        rows = jnp.sum(diff * diff, axis=1, keepdims=True)
        loss_ref[...] += jnp.broadcast_to(jnp.sum(rows, axis=0, keepdims=True) * (0.5 / D_MODEL), (1, LANES))
        dy = diff * (1.0 / D_MODEL)
        dgf_ref[...] += jnp.sum(dy * xhat, axis=0, keepdims=True)
        dx2 = _rms_bwd(dy, xhat, rr, g)
        dx2_ref[...] = dx2
        dx2b_ref[...] = dx2.astype(BF16)

    row = _rows_spec(tm, D_MODEL)
    return _token_call("ffn_down_loss", body, t, tm,
                       [(ff, _chips_spec(tm)), (w_d, _resident((N_CHIPS, FF_SHARD, D_MODEL))), (x1, row), (tgt, row),
                        (gf, _full((1, D_MODEL)))],
                       [((t, D_MODEL), F32, row), ((t, D_MODEL), BF16, row)], reds=[(1, LANES), (1, D_MODEL)])


def _ffn_bwd_act(dx2b, w_d, fa, fb):
    t = dx2b.shape[0]
    tm = min(t, 512)

    def body(d_ref, wd_ref, fa_ref, fb_ref, da_ref, db_ref):
        d = d_ref[...]
        for s in range(N_CHIPS):
            dff = lax.dot_general(d, wd_ref[s], (NT, ((), ())), preferred_element_type=F32)
            da_ref[s] = (dff * fa_ref[s].astype(F32)).astype(BF16)
            db_ref[s] = (dff * fb_ref[s].astype(F32)).astype(BF16)

    shp = (N_CHIPS, t, FF_SHARD)
    return _token_call("ffn_bwd_act", body, t, tm,
                       [(dx2b, _rows_spec(tm, D_MODEL)), (w_d, _resident((N_CHIPS, FF_SHARD, D_MODEL))),
                        (fa, _chips_spec(tm)), (fb, _chips_spec(tm))],
                       [(shp, BF16, _chips_spec(tm))] * 2)


def _ffn_bwd_in(da, db, w_g, w_u, x1, dx2, g2):
    t = x1.shape[0]
    tm = min(t, 512)

    def body(da_ref, db_ref, wg_ref, wu_ref, x1_ref, dx2_ref, g_ref, dx1_ref, dx1b_ref, dg_ref):
        _zero_at_start(dg_ref)
        acc = None
        for s in range(N_CHIPS):
            part = (lax.dot_general(da_ref[s], wg_ref[s], (NT, ((), ())), preferred_element_type=F32)
                    + lax.dot_general(db_ref[s], wu_ref[s], (NT, ((), ())), preferred_element_type=F32))
            acc = part if acc is None else acc + part
        xhat, rr = _rms_stats(x1_ref[...])
        dg_ref[...] += jnp.sum(acc * xhat, axis=0, keepdims=True)
        dx1 = dx2_ref[...] + _rms_bwd(acc, xhat, rr, g_ref[...])
        dx1_ref[...] = dx1
        dx1b_ref[...] = dx1.astype(BF16)

    row = _rows_spec(tm, D_MODEL)
    w_spec = _resident((N_CHIPS, D_MODEL, FF_SHARD))
    return _token_call("ffn_bwd_in", body, t, tm,
                       [(da, _chips_spec(tm)), (db, _chips_spec(tm)), (w_g, w_spec), (w_u, w_spec), (x1, row), (dx2, row),
                        (g2, _full((1, D_MODEL)))],
                       [((t, D_MODEL), F32, row), ((t, D_MODEL), BF16, row)], reds=[(1, D_MODEL)])


def _group_dh(d, w_refs):
    dh = None
    for part, w_ref in enumerate(w_refs):
        term = lax.dot_general(d[:, part * ATTN_W:(part + 1) * ATTN_W], w_ref[...], (NT, ((), ())),
                               preferred_element_type=F32)
        dh = term if dh is None else dh + term
    return dh


def _in_proj_bwd(dgu, dqkvs, w_in, x, dx1, g1):
    t = x.shape[0]
    tile = min(t, TILE)
    ngroups = len(DILATIONS)

    def body(*refs):
        dgu_ref, dq_refs = refs[0], refs[1:1 + ngroups]
        w0_ref, w1_ref = refs[1 + ngroups:3 + ngroups]
        wg_refs = [refs[3 + ngroups + 3 * g:6 + ngroups + 3 * g] for g in range(ngroups)]
        x_ref, dx1_ref, g_ref, dx_ref, dg_ref = refs[3 + 4 * ngroups:5 + 4 * ngroups + 3]
        slabs = refs[5 + 4 * ngroups + 3:]
        _zero_at_start(dg_ref)
        for g in range(1, ngroups):
            dil = DILATIONS[g]
            part = _group_dh(dq_refs[g][...].reshape(tile, GROUP_COLS), wg_refs[g])
            for r in range(dil):
                _put_class_rows(slabs[g - 1], r, dil, part[r * (tile // dil):(r + 1) * (tile // dil)])
        dh = lax.dot_general(dgu_ref[:, 0:GU_HALF], w0_ref[...], (NT, ((), ())), preferred_element_type=F32)
        dh = dh + lax.dot_general(dgu_ref[:, GU_HALF:], w1_ref[...], (NT, ((), ())), preferred_element_type=F32)
        dh = dh + _group_dh(dq_refs[0][0], wg_refs[0])
        for slab in slabs:
            dh = dh + _from_slabs(slab)
        xhat, rr = _rms_stats(x_ref[...])
        dg_ref[...] += jnp.sum(dh * xhat, axis=0, keepdims=True)
        dx_ref[...] = dx1_ref[...] + _rms_bwd(dh, xhat, rr, g_ref[...])

    row = _rows_spec(tile, D_MODEL)
    group_ins = [(dqkvs[g].reshape(d, t // d, GROUP_COLS), _group_spec(d, tile, GROUP_COLS)) for g, d in enumerate(DILATIONS)]
    w_specs = _gu_w_specs() + [s for g in range(ngroups) for s in _group_w_specs(g)]
    return _token_call(
        "in_proj_bwd", body, t, tile,
        [(dgu, _rows_spec(tile, GU_COLS))] + group_ins + [(w_in, s) for s in w_specs]
        + [(x, row), (dx1, row), (g1, _full((1, D_MODEL)))],
        [((t, D_MODEL), F32, row)], reds=[(1, D_MODEL)], scratch=[_slabs(tile, D_MODEL)] * (ngroups - 1))


WGRAD_TK = 2048


def _wgrad_mm(name, grid, a, a_spec, b, b_spec, acc_shape, out_shape, out_spec, dst=None):
    nk = grid[-1]

    def body(*refs):
        a_ref, b_ref, o_ref, acc_ref = refs[0], refs[1], refs[-2], refs[-1]
        k = pl.program_id(len(grid) - 1)
        part = lax.dot_general(a_ref[...], b_ref[...], (TN, ((), ())), preferred_element_type=F32)

        @pl.when(k == 0)
        def _():
            acc_ref[...] = part

        @pl.when(k > 0)
        def _():
            acc_ref[...] += part

        @pl.when(k == nk - 1)
        def _():
            o_ref[...] = acc_ref[...].astype(BF16)

    filled = [] if dst is None else [dst]
    return pl.pallas_call(
        body, grid=grid, in_specs=[a_spec, b_spec] + [pl.BlockSpec(memory_space=pl.ANY)] * len(filled),
        out_specs=out_spec, out_shape=jax.ShapeDtypeStruct(out_shape, BF16), scratch_shapes=[pltpu.VMEM(acc_shape, F32)],
        input_output_aliases={2: 0} if filled else {}, compiler_params=_cparams(len(grid)), name=name)(a, b, *filled)


def _wgrad_2d(name, a, b, tm, tn):
    t, k1 = a.shape
    n = b.shape[1]
    tk = min(t, WGRAD_TK)
    return _wgrad_mm(name, (k1 // tm, n // tn, t // tk), a, pl.BlockSpec((tk, tm), lambda i, j, k: (k, i)),
                     b, pl.BlockSpec((tk, tn), lambda i, j, k: (k, j)), (tm, tn), (k1, n),
                     pl.BlockSpec((tm, tn), lambda i, j, k: (i, j)))


def _wgrad_in(hs, dgu, dqkvs):
    t = dgu.shape[0]
    tk = min(t, WGRAD_TK)
    gu_block = QKV_BLOCKS * ATTN_W // GU_HALF
    parts = [(hs[0], dgu, GU_HALF, lambda j: j + gu_block)]
    parts += [(hs[g].reshape(t, D_MODEL), dqkvs[g], ATTN_W, lambda j, g=g: _w_in_block(j, g)) for g in range(3)]
    dst = None
    for n, (a, b, tn, block_of) in enumerate(parts):
        dst = _wgrad_mm(f"wgrad_in_{n}", (1, b.shape[1] // tn, t // tk),
                        a, pl.BlockSpec((tk, D_MODEL), lambda i, j, k: (k, 0)), b, pl.BlockSpec((tk, tn), lambda i, j, k: (k, j)),
                        (D_MODEL, tn), (D_MODEL, IN_COLS),
                        pl.BlockSpec((D_MODEL, tn), lambda i, j, k, block_of=block_of: (0, block_of(j))), dst=dst)
    return dst


def _wgrad_ff_in(name, h2, da):
    t = h2.shape[0]
    tk = min(t, WGRAD_TK)
    return _wgrad_mm(name, (N_CHIPS, 1, t // tk), h2, pl.BlockSpec((tk, D_MODEL), lambda i, j, k: (k, 0)),
                     da, pl.BlockSpec((None, tk, FF_SHARD), lambda i, j, k: (i, k, 0)), (D_MODEL, FF_SHARD),
                     (N_CHIPS, D_MODEL, FF_SHARD), pl.BlockSpec((None, D_MODEL, FF_SHARD), lambda i, j, k: (i, 0, 0)))


def _wgrad_ff_down(ff, dx2b):
    t = dx2b.shape[0]
    tk = min(t, WGRAD_TK)
    return _wgrad_mm("wgrad_ffn_down", (N_CHIPS, 1, t // tk), ff, pl.BlockSpec((None, tk, FF_SHARD), lambda i, j, k: (i, k, 0)),
                     dx2b, pl.BlockSpec((tk, D_MODEL), lambda i, j, k: (k, 0)), (FF_SHARD, D_MODEL),
                     (N_CHIPS, FF_SHARD, D_MODEL), pl.BlockSpec((None, FF_SHARD, D_MODEL), lambda i, j, k: (i, 0, 0)))


def _local_step(x, pos_col, tgt, g1, ln_g, ln_b, w_s, b_s, g2, gf, first_weight, late_weights, on_grads=None):
    tables = _rope_tables(pos_col)
    bias_exp = jnp.repeat(jnp.transpose(b_s), SGU_W // SGU_GROUPS, axis=1)

    hs = _norm_fwd(x, g1)
    w_p = first_weight(hs[0])
    gu, qkvs = _in_proj(hs, w_p, tables)
    os_, ls_ = [], []
    for g, dil in enumerate(DILATIONS):
        o, lse = _attn_fwd(qkvs[g], g, dil)
        os_.append(o)
        ls_.append(lse)
    attn = _combine_fwd(os_, ls_)
    sgu = _sgu_fwd(gu, ln_g, ln_b, w_s, bias_exp)
    w_pa, w_ps, w_out, w_g, w_u, w_d = late_weights(attn)
    pa, ps, merged, x1, h2 = _merge_fwd(attn, sgu, gu, x, w_pa, w_ps, w_out, g2)
    fa, fb, ff = _ffn_fwd(h2, w_g, w_u)
    dx2, dx2b, loss, dgf = _ffn_down_loss(ff, w_d, x1, tgt, gf)

    da, db = _ffn_bwd_act(dx2b, w_d, fa, fb)
    dw_d = _wgrad_ff_down(ff, dx2b)
    dx1, dx1b, dg2 = _ffn_bwd_in(da, db, w_g, w_u, x1, dx2, g2)
    dw_g = _wgrad_ff_in("wgrad_ffn_gate", h2, da)
    dw_u = _wgrad_ff_in("wgrad_ffn_up", h2, db)

    dgu, dpa, dps, dattn, dsgu = _merge_bwd(dx1b, gu, pa, ps, w_pa, w_ps, w_out)
    dw_out = _wgrad_2d("wgrad_out", merged, dx1b, D_MODEL, D_MODEL)
    dw_pa = _wgrad_2d("wgrad_proj_attn", attn, dpa, ATTN_W, D_MODEL)
    dw_ps = _wgrad_2d("wgrad_proj_sgu", sgu, dps, SGU_W, D_MODEL)
    if on_grads is not None:
        ln_g = ln_g + on_grads(1, dict(w_proj_attn=dw_pa, w_proj_sgu=dw_ps, w_out=dw_out, w_ffn_gate=dw_g, w_ffn_up=dw_u,
                                       w_ffn_down=dw_d))[:, :SGU_W]
    dgu, dw_s, dbias, dln_g, dln_b = _sgu_bwd(dgu, gu, dsgu, ln_g, ln_b, w_s, bias_exp)
    dos, ccs = _combine_bwd(dattn, os_, ls_)
    dqkvs = [_attn_bwd(qkvs[g], dos[g], ccs[g], ls_[g], *tables[g], g, dil) for g, dil in enumerate(DILATIONS)]
    dw_p = _wgrad_in(hs, dgu, dqkvs)
    if on_grads is not None:
        g1 = g1 + on_grads(0, dict(w_in=dw_p))
    dx, dg1 = _in_proj_bwd(dgu, dqkvs, w_p, x, dx1, g1)

    db_s = jnp.transpose(dbias[:, ::SGU_W // SGU_GROUPS])
    small = dict(loss=loss, norm1_g=dg1, sgu_ln_g=dln_g, sgu_ln_b=dln_b, w_spatial=dw_s, b_spatial=db_s,
                 norm2_g=dg2, final_g=dgf)
    big = dict(w_in=dw_p, w_proj_attn=dw_pa, w_proj_sgu=dw_ps, w_out=dw_out, w_ffn_gate=dw_g, w_ffn_up=dw_u,
               w_ffn_down=dw_d)
    return dx, big, small


def _ew(name, fn, ins, out_dtypes):
    shp = ins[0].shape
    rows, cols = shp
    tr = next((cand for cand in (256, 352, 128) if rows % cand == 0 and rows > cand), rows)

    def body(*refs):
        res = fn(*[r[...] for r in refs[:len(ins)]])
        for o_ref, v in zip(refs[len(ins):], res):
            o_ref[...] = v.astype(o_ref.dtype)

    spec = pl.BlockSpec((tr, cols), lambda i: (i, 0))
    return pl.pallas_call(
        body, grid=(rows // tr,), in_specs=[spec] * len(ins), out_specs=[spec] * len(out_dtypes),
        out_shape=[jax.ShapeDtypeStruct(shp, d) for d in out_dtypes],
        compiler_params=_cparams(1), name=name)(*ins)


def _adamw_math(g, w, m, v):
    m = ADAM_B1 * m + (1.0 - ADAM_B1) * g
    v = ADAM_B2 * v + (1.0 - ADAM_B2) * (g * g)
    m_hat = m / (1.0 - ADAM_B1 ** ADAM_STEP)
    v_hat = v / (1.0 - ADAM_B2 ** ADAM_STEP)
    delta = -ADAM_LR * (m_hat / (jnp.sqrt(v_hat) + ADAM_EPS) + ADAM_WD * w)
    return delta, m, v


def _adamw(name, g, w, m, v):
    return _ew(name, lambda g_, w_, m_, v_: (g_,) + _adamw_math(g_, w_, m_, v_), [g, w, m, v], [F32] * 4)


VMEM_SPEC = pl.BlockSpec(memory_space=pltpu.VMEM)


def _for_row_chunks(rows, fn):
    ck = next(c for c in (64, 32, 16) if rows % c == 0)

    def step(i, carry):
        fn(pl.multiple_of(i * ck, ck), ck)
        return carry

    lax.fori_loop(0, rows // ck, step, 0)


def _place():
    x, y, c = lax.axis_index("x"), lax.axis_index("y"), lax.axis_index("c")
    chips = [(1 - x, y), (x, 1 - y), (1 - x, 1 - y)]
    return x, y, c, 2 * x + y, chips


def _rows(ref, start, size):
    if len(ref.shape) == 2:
        return ref.at[pl.ds(start, size), :]
    return ref.at[:, pl.ds(start, size), :]


def _comm_call(name, body, ins, out_shapes, scratch, n_remote):
    return pl.pallas_call(
        body, in_specs=[VMEM_SPEC] * len(ins), out_specs=[VMEM_SPEC] * len(out_shapes),
        out_shape=out_shapes,
        scratch_shapes=list(scratch) + [pltpu.SemaphoreType.DMA((n_remote,)), pltpu.SemaphoreType.DMA((n_remote,))],
        compiler_params=pltpu.CompilerParams(vmem_limit_bytes=VMEM_LIMIT), name=name)(*ins)


def _gather_finish(name, shard, landed):
    k_rows, n = shard.shape
    kh = k_rows // 2

    def body(shard_ref, land_ref, out_ref, send, recv):
        x, y, c, me, chips = _place()
        passed = []
        for j, chip in enumerate(chips):
            theirs = 2 * chip[0] + chip[1]
            cp = pltpu.make_async_remote_copy(
                src_ref=land_ref.at[j], dst_ref=_rows(out_ref.at[theirs], c * kh, kh), send_sem=send.at[j],
                recv_sem=recv.at[j], device_id=(x, y, 1 - c), device_id_type=MESH)
            cp.start()
            passed.append(cp)
        mine = out_ref.at[me]

        def put_own(r0, ck):
            mine[pl.ds(r0, ck), :] = shard_ref[pl.ds(r0, ck), :]

        _for_row_chunks(k_rows, put_own)
        for j, chip in enumerate(chips):
            slot = out_ref.at[2 * chip[0] + chip[1]]

            def put_half(r0, ck, j=j, slot=slot):
                slot[pl.ds(pl.multiple_of(c * kh + r0, ck), ck), :] = land_ref[j, pl.ds(r0, ck), :]

            _for_row_chunks(kh, put_half)
        for j, chip in enumerate(chips):
            other = _rows(out_ref.at[2 * chip[0] + chip[1]], (1 - c) * kh, kh)
            pltpu.make_async_remote_copy(src_ref=other, dst_ref=other, send_sem=send.at[j], recv_sem=recv.at[j],
                                         device_id=(x, y, 1 - c), device_id_type=MESH).wait_recv()
        for cp in passed:
            cp.wait_send()

    return _comm_call(name, body, [shard, landed], [jax.ShapeDtypeStruct((N_CHIPS, k_rows, n), shard.dtype)], [], 3)[0]


HBM_SPEC = pl.BlockSpec(memory_space=pltpu.HBM)
SEM_SPEC = pl.BlockSpec(memory_space=pltpu.SEMAPHORE)
DATAFLOW = pltpu.SideEffectType.DATAFLOW_SIDE_EFFECTING
TOKEN_SHAPE = (1, D_MODEL)
N_PEERS = 7


def _peers():
    x, y, c = lax.axis_index("x"), lax.axis_index("y"), lax.axis_index("c")
    flip = lambda v, f: 1 - v if f else v
    return [(flip(x, k & 4), flip(y, k & 2), flip(c, k & 1)) for k in range(1, N_PEERS + 1)]


def _piece_shape(shape):
    return (shape[-2] // 2, shape[2] if len(shape) == 3 else shape[1] // N_CHIPS)


def _device_piece(ref, chip, core):
    kh, n4 = _piece_shape(ref.shape)
    if len(ref.shape) == 3:
        return ref.at[chip, pl.ds(core * kh, kh), :]
    return ref.at[pl.ds(core * kh, kh), pl.ds(chip * n4, n4)]


def _exchange_copies(partials, lands, send, recv):
    return [pltpu.make_async_remote_copy(
        src_ref=_device_piece(partials[t], 2 * px + py, pc), dst_ref=lands[t].at[k], send_sem=send.at[t * N_PEERS + k],
        recv_sem=recv.at[t * N_PEERS + k], device_id=(px, py, pc), device_id_type=MESH)
        for t in range(len(partials)) for k, (px, py, pc) in enumerate(_peers())]


def _gather_copies(shards, lands, send, recv):
    x, y, c, me, chips = _place()
    return [pltpu.make_async_remote_copy(
        src_ref=shards[t], dst_ref=lands[t].at[me], send_sem=send.at[t * 3 + j], recv_sem=recv.at[t * 3 + j],
        device_id=(*chip, c), device_id_type=MESH)
        for t in range(len(shards)) for j, chip in enumerate(chips)]


def _gather_half_copies(shards, lands, send, recv):
    x, y, c, me, chips = _place()
    return [pltpu.make_async_remote_copy(
        src_ref=_rows(shards[t], c * (shards[t].shape[0] // 2), shards[t].shape[0] // 2), dst_ref=lands[t].at[j],
        send_sem=send.at[t * 3 + j], recv_sem=recv.at[t * 3 + j], device_id=(*chip, c), device_id_type=MESH)
        for t in range(len(shards)) for j, chip in enumerate(chips)]


def _split_start(name, copies, per_tensor, srcs, land_shapes):
    nt = len(srcs)
    lands = [lax.empty(s, BF16) for s in land_shapes]
    nsem = nt * per_tensor

    def body(*refs):
        send, recv = refs[2 * nt], refs[2 * nt + 1]
        for cp in copies(refs[:nt], refs[nt:2 * nt], send, recv):
            cp.start()
        refs[-1][...] = jnp.zeros(TOKEN_SHAPE, F32)

    hbm = lambda a: pltpu.with_memory_space_constraint(a, pltpu.HBM)
    outs = pl.pallas_call(
        body, name=name,
        out_shape=[pltpu.SemaphoreType.DMA((nsem,)), pltpu.SemaphoreType.DMA((nsem,))]
        + [pltpu.HBM(s.shape, s.dtype) for s in srcs] + [pltpu.HBM(l.shape, l.dtype) for l in lands]
        + [jax.ShapeDtypeStruct(TOKEN_SHAPE, F32)],
        in_specs=[HBM_SPEC] * (2 * nt), out_specs=[SEM_SPEC, SEM_SPEC] + [HBM_SPEC] * (2 * nt) + [VMEM_SPEC],
        input_output_aliases={i: 2 + i for i in range(2 * nt)},
        compiler_params=pltpu.CompilerParams(has_side_effects=DATAFLOW))(*[hbm(a) for a in list(srcs) + lands])
    return outs[0], outs[1], outs[2:2 + nt], outs[2 + nt:2 + 2 * nt], outs[-1]


def _split_wait(name, copies, send, recv, srcs, lands, after):
    nt = len(srcs)

    def body(*refs):
        for cp in copies(refs[:nt], refs[nt:2 * nt], refs[2 * nt], refs[2 * nt + 1]):
            cp.wait_send()
            cp.wait_recv()

    outs = pl.pallas_call(
        body, name=name,
        out_shape=[pltpu.HBM(s.shape, s.dtype) for s in srcs] + [pltpu.HBM(l.shape, l.dtype) for l in lands],
        in_specs=[HBM_SPEC] * (2 * nt) + [SEM_SPEC, SEM_SPEC, pl.BlockSpec(memory_space=pl.ANY)],
        out_specs=[HBM_SPEC] * (2 * nt), input_output_aliases={i: i for i in range(2 * nt)},
        compiler_params=pltpu.CompilerParams(has_side_effects=DATAFLOW))(*srcs, *lands, send, recv, after)
    return outs[:nt], outs[nt:]


def _device_sum(name, partials, lands):
    nt = len(partials)

    def body(*refs):
        ins, slots, outs, owns = refs[:nt], refs[nt:2 * nt], refs[2 * nt:3 * nt], refs[3 * nt:4 * nt]
        send, recv, loc = refs[4 * nt:]
        x, y, c, me, chips = _place()
        sibling = (x, y, 1 - c)
        loads = [pltpu.make_async_copy(_device_piece(ins[t], me, c), owns[t], loc.at[t]) for t in range(nt)]
        for cp in loads:
            cp.start()
        handed = []
        for t in range(nt):
            kh = owns[t].shape[0]
            loads[t].wait()

            def add(r0, ck, own=owns[t], slot=slots[t], dst=outs[t], kh=kh):
                rows = pl.ds(r0, ck)
                acc = own[rows, :].astype(F32)
                for k in range(N_PEERS):
                    acc = acc + slot[k, rows, :].astype(F32)
                dst[pl.ds(pl.multiple_of(c * kh + r0, ck), ck), :] = acc

            _for_row_chunks(kh, add)
            rc = pltpu.make_async_remote_copy(
                src_ref=_rows(outs[t], c * kh, kh), dst_ref=_rows(outs[t], c * kh, kh), send_sem=send.at[t],
                recv_sem=recv.at[t], device_id=sibling, device_id_type=MESH)
            rc.start()
            handed.append(rc)
        for t in range(nt):
            kh = owns[t].shape[0]
            other = _rows(outs[t], (1 - c) * kh, kh)
            pltpu.make_async_remote_copy(
                src_ref=other, dst_ref=other, send_sem=send.at[t], recv_sem=recv.at[t],
                device_id=sibling, device_id_type=MESH).wait_recv()
        for rc in handed:
            rc.wait_send()

    pieces = [_piece_shape(p.shape) for p in partials]
    return pl.pallas_call(
        body, in_specs=[pl.BlockSpec(memory_space=pl.ANY)] * nt + [VMEM_SPEC] * nt, out_specs=[VMEM_SPEC] * nt,
        out_shape=[jax.ShapeDtypeStruct((2 * kh, n4), F32) for kh, n4 in pieces],
        scratch_shapes=[pltpu.VMEM(p, BF16) for p in pieces]
        + [pltpu.SemaphoreType.DMA((nt,)), pltpu.SemaphoreType.DMA((nt,)), pltpu.SemaphoreType.DMA((nt,))],
        compiler_params=pltpu.CompilerParams(vmem_limit_bytes=VMEM_LIMIT), name=name)(*partials, *lands)


VEC_SHAPE = (8, D_MODEL + LANES)
VEC_SLOTS = dict(norm1_g=(slice(0, 1), slice(0, D_MODEL)), norm2_g=(slice(1, 2), slice(0, D_MODEL)),
                 final_g=(slice(2, 3), slice(0, D_MODEL)), sgu_ln_g=(slice(3, 4), slice(0, SGU_W)),
                 sgu_ln_b=(slice(3, 4), slice(SGU_W, 2 * SGU_W)), b_spatial=(slice(0, 8), slice(D_MODEL, D_MODEL + LANES)),
                 loss=(slice(4, 5), slice(0, LANES)))
VEC_PARAMS = ("norm1_g", "norm2_g", "final_g", "sgu_ln_g", "sgu_ln_b", "b_spatial")
SMALL_PARAMS = VEC_PARAMS + ("w_spatial",)
W_SPATIAL_2D = (SGU_GROUPS * SGU_CHUNK, SGU_CHUNK)


def _small_step(partials, w, m, v):
    def shape2d(name):
        if name == "w_spatial":
            return W_SPATIAL_2D
        rows, cols = VEC_SLOTS[name]
        return (rows.stop - rows.start, cols.stop - cols.start)

    g_names = VEC_PARAMS + ("loss", "w_spatial")
    ng, npar = len(g_names), len(SMALL_PARAMS)

    def pack(dst, parts):
        dst[...] = jnp.zeros(VEC_SHAPE, F32)
        for n, ref in parts.items():
            if n in VEC_SLOTS:
                dst[VEC_SLOTS[n]] = ref[...]

    def reduce_body(*refs):
        g_in = dict(zip(g_names, refs[:ng]))
        vec_out, ws_out, vec, vec_pair, vec_slot, ws_pair, ws_slot, send, recv = refs[ng:]
        x, y, c, me, chips = _place()
        sibling = (x, y, 1 - c)
        pack(vec, g_in)
        copies = []

        def allreduce(k0, src, pair, slot):
            first = pltpu.make_async_remote_copy(src_ref=src, dst_ref=pair, send_sem=send.at[k0], recv_sem=recv.at[k0],
                                                 device_id=sibling, device_id_type=MESH)
            first.start()
            first.wait_recv()
            slot[me] = src[...] + pair[...]
            arrivals = []
            for j, chip in enumerate(chips):
                theirs = 2 * chip[0] + chip[1]
                rc = pltpu.make_async_remote_copy(src_ref=slot.at[me], dst_ref=slot.at[me], send_sem=send.at[k0 + 1 + j],
                                                  recv_sem=recv.at[k0 + 1 + j], device_id=(*chip, c), device_id_type=MESH)
                rc.start()
                arrivals.append(pltpu.make_async_remote_copy(
                    src_ref=slot.at[theirs], dst_ref=slot.at[theirs], send_sem=send.at[k0 + 1 + j],
                    recv_sem=recv.at[k0 + 1 + j], device_id=(*chip, c), device_id_type=MESH))
                copies.append(rc)
            copies.append(first)
            return arrivals

        arrivals = allreduce(0, vec, vec_pair, vec_slot) + allreduce(4, g_in["w_spatial"], ws_pair, ws_slot)
        for a in arrivals:
            a.wait_recv()
        vec_out[...] = ((vec_slot[0] + vec_slot[1]) + vec_slot[2]) + vec_slot[3]

        def spatial(r0, ck):
            rows = pl.ds(r0, ck)
            ws_out[rows, :] = ((ws_slot[0, rows, :] + ws_slot[1, rows, :]) + ws_slot[2, rows, :]) + ws_slot[3, rows, :]

        _for_row_chunks(W_SPATIAL_2D[0], spatial)
        for rc in copies:
            rc.wait_send()

    g_vec, g_ws = pl.pallas_call(
        reduce_body, in_specs=[VMEM_SPEC] * ng, out_specs=[VMEM_SPEC] * 2,
        out_shape=[jax.ShapeDtypeStruct(VEC_SHAPE, F32), jax.ShapeDtypeStruct(W_SPATIAL_2D, F32)],
        scratch_shapes=[pltpu.VMEM(VEC_SHAPE, F32), pltpu.VMEM(VEC_SHAPE, F32), pltpu.VMEM((N_CHIPS,) + VEC_SHAPE, F32),
                        pltpu.VMEM(W_SPATIAL_2D, F32), pltpu.VMEM((N_CHIPS,) + W_SPATIAL_2D, F32),
                        pltpu.SemaphoreType.DMA((8,)), pltpu.SemaphoreType.DMA((8,))],
        name="small_params_allreduce")(*[partials[n].reshape(shape2d(n)) for n in g_names])

    def update_body(*refs):
        gv_ref, gw_ref = refs[:2]
        w_in, m_in, v_in = (dict(zip(SMALL_PARAMS, refs[2 + k * npar:2 + (k + 1) * npar])) for k in range(3))
        o0 = 2 + 3 * npar
        g_out = dict(zip(g_names, refs[o0:o0 + ng]))
        d_out, m_out, v_out = (dict(zip(SMALL_PARAMS, refs[o0 + ng + k * npar:o0 + ng + (k + 1) * npar])) for k in range(3))
        vw, vm, vv = refs[o0 + ng + 3 * npar:]
        pack(vw, w_in)
        pack(vm, m_in)
        pack(vv, v_in)
        d_vec, m_vec, v_vec = _adamw_math(gv_ref[...], vw[...], vm[...], vv[...])
        vw[...] = d_vec
        vm[...] = m_vec
        vv[...] = v_vec
        for n in VEC_PARAMS + ("loss",):
            g_out[n][...] = gv_ref[VEC_SLOTS[n]]
        for n in VEC_PARAMS:
            d_out[n][...] = vw[VEC_SLOTS[n]]
            m_out[n][...] = vm[VEC_SLOTS[n]]
            v_out[n][...] = vv[VEC_SLOTS[n]]

        def spatial(r0, ck):
            rows = pl.ds(r0, ck)
            g = gw_ref[rows, :]
            d_, m_, v_ = _adamw_math(g, w_in["w_spatial"][rows, :], m_in["w_spatial"][rows, :], v_in["w_spatial"][rows, :])
            g_out["w_spatial"][rows, :] = g
            d_out["w_spatial"][rows, :] = d_
            m_out["w_spatial"][rows, :] = m_
            v_out["w_spatial"][rows, :] = v_

        _for_row_chunks(W_SPATIAL_2D[0], spatial)

    ins = [g_vec, g_ws]
    for src in (w, m, v):
        ins += [src[n].reshape(shape2d(n)) for n in SMALL_PARAMS]
    out_shapes = [jax.ShapeDtypeStruct(shape2d(n), F32) for n in g_names + SMALL_PARAMS * 3]
    outs = pl.pallas_call(
        update_body, in_specs=[VMEM_SPEC] * len(ins), out_specs=[VMEM_SPEC] * len(out_shapes), out_shape=out_shapes,
        scratch_shapes=[pltpu.VMEM(VEC_SHAPE, F32)] * 3, name="small_params_update")(*ins)
    grads = dict(zip(g_names, outs[:ng]))
    rest = [dict(zip(SMALL_PARAMS, outs[ng + k * npar:ng + (k + 1) * npar])) for k in range(3)]
    return grads, rest[0], rest[1], rest[2]


BIG = ("w_in", "w_proj_attn", "w_proj_sgu", "w_out", "w_ffn_gate", "w_ffn_up", "w_ffn_down")
COMM_GROUPS = (("w_in",), ("w_proj_attn", "w_proj_sgu", "w_out", "w_ffn_gate", "w_ffn_up", "w_ffn_down"))
WEIGHTS = ("norm1_g", "w_in", "sgu_ln_g", "sgu_ln_b", "w_spatial", "b_spatial", "w_proj_attn", "w_proj_sgu", "w_out",
           "norm2_g", "w_ffn_gate", "w_ffn_up", "w_ffn_down", "final_g")


def _cols_from_chips(g):
    return jnp.transpose(g, (1, 0, 2)).reshape(g.shape[1], N_CHIPS * g.shape[2])


def kernel(x, positions, norm1_g, w_in, sgu_ln_g, sgu_ln_b, w_spatial, b_spatial, w_proj_attn, w_proj_sgu, w_out, norm2_g, w_ffn_gate, w_ffn_up, w_ffn_down, final_g, loss_target, m_norm1_g, m_w_in, m_sgu_ln_g, m_sgu_ln_b, m_w_spatial, m_b_spatial, m_w_proj_attn, m_w_proj_sgu, m_w_out, m_norm2_g, m_w_ffn_gate, m_w_ffn_up, m_w_ffn_down, m_final_g, v_norm1_g, v_w_in, v_sgu_ln_g, v_sgu_ln_b, v_w_spatial, v_b_spatial, v_w_proj_attn, v_w_proj_sgu, v_w_out, v_norm2_g, v_w_ffn_gate, v_w_ffn_up, v_w_ffn_down, v_final_g):
    w = dict(norm1_g=norm1_g, w_in=w_in, sgu_ln_g=sgu_ln_g, sgu_ln_b=sgu_ln_b, w_spatial=w_spatial, b_spatial=b_spatial,
             w_proj_attn=w_proj_attn, w_proj_sgu=w_proj_sgu, w_out=w_out, norm2_g=norm2_g, w_ffn_gate=w_ffn_gate,
             w_ffn_up=w_ffn_up, w_ffn_down=w_ffn_down, final_g=final_g)
    m = dict(norm1_g=m_norm1_g, w_in=m_w_in, sgu_ln_g=m_sgu_ln_g, sgu_ln_b=m_sgu_ln_b, w_spatial=m_w_spatial,
             b_spatial=m_b_spatial, w_proj_attn=m_w_proj_attn, w_proj_sgu=m_w_proj_sgu, w_out=m_w_out, norm2_g=m_norm2_g,
             w_ffn_gate=m_w_ffn_gate, w_ffn_up=m_w_ffn_up, w_ffn_down=m_w_ffn_down, final_g=m_final_g)
    v = dict(norm1_g=v_norm1_g, w_in=v_w_in, sgu_ln_g=v_sgu_ln_g, sgu_ln_b=v_sgu_ln_b, w_spatial=v_w_spatial,
             b_spatial=v_b_spatial, w_proj_attn=v_w_proj_attn, w_proj_sgu=v_w_proj_sgu, w_out=v_w_out, norm2_g=v_norm2_g,
             w_ffn_gate=v_w_ffn_gate, w_ffn_up=v_w_ffn_up, w_ffn_down=v_w_ffn_down, final_g=v_final_g)
    t = x.shape[1]

    shards = {n: _ew(f"cast_{n}", lambda a: (a,), [w[n][0]], [BF16])[0] for n in BIG}
    late = COMM_GROUPS[1]
    k_in, n_in = shards["w_in"].shape
    *first, token = _split_start("gather_start_0", _gather_half_copies, 3, [shards["w_in"]], [(3, k_in // 2, n_in)])
    pending = {}

    def first_weight(after):
        srcs, filled = _split_wait("gather_wait_0", _gather_half_copies, *first, after)
        gath_in, late_shards = lax.optimization_barrier(
            (_gather_finish("gather_finish_0", srcs[0], filled[0]), [shards[n] for n in late]))
        *pending["late"], _ = _split_start(
            "gather_start_1", _gather_copies, 3, late_shards, [(N_CHIPS,) + s.shape for s in late_shards])
        return _cols_from_chips(gath_in)

    def late_weights(after):
        srcs, filled = _split_wait("gather_wait_1", _gather_copies, *pending["late"], after)
        me = 2 * lax.axis_index("x") + lax.axis_index("y")
        gath = {n: lax.dynamic_update_slice(f, s[None], (me, 0, 0)) for n, f, s in zip(late, filled, srcs)}
        return (_cols_from_chips(gath["w_proj_attn"]), _cols_from_chips(gath["w_proj_sgu"]),
                gath["w_out"].reshape(D_MODEL, D_MODEL), gath["w_ffn_gate"], gath["w_ffn_up"], gath["w_ffn_down"])

    exchanges = {}

    def on_grads(i, partials):
        if "w_out" in partials:
            partials["w_out"] = partials["w_out"].reshape(N_CHIPS, D_MODEL // N_CHIPS, D_MODEL)
        parts = [partials[n] for n in COMM_GROUPS[i]]
        *exchanges[i], started = _split_start(
            f"rs_exchange_start_{i}", _exchange_copies, N_PEERS, parts, [(N_PEERS,) + _piece_shape(p.shape) for p in parts])
        return started

    dx, _, small = _local_step(
        x[0], positions.reshape(t, 1), loss_target[0], norm1_g + token, sgu_ln_g, sgu_ln_b, w_spatial[0], b_spatial[0],
        norm2_g, final_g.reshape(1, D_MODEL), first_weight, late_weights, on_grads=on_grads)

    grads = {}
    for i in (1, 0):
        parts, filled = _split_wait(f"rs_exchange_wait_{i}", _exchange_copies, *exchanges[i], dx)
        grads.update(zip(COMM_GROUPS[i], _device_sum(f"rs_device_sum_{i}", parts, filled)))

    delta, new_m, new_v = {}, {}, {}
    for n in BIG:
        shp = w[n].shape
        flip = jnp.transpose if shp[-1] % LANES else (lambda a: a)
        outs = _adamw(f"adamw_{n}", flip(grads[n]), flip(w[n][0]), flip(m[n][0]), flip(v[n][0]))
        grads[n], delta[n], new_m[n], new_v[n] = (flip(a).reshape(shp) for a in outs)

    g_s, d_s, m_s, v_s = _small_step(small, w, m, v)
    loss = g_s["loss"][0, 0]
    for n in SMALL_PARAMS:
        shp = w[n].shape
        grads[n], delta[n], new_m[n], new_v[n] = (a[n].reshape(shp) for a in (g_s, d_s, m_s, v_s))

    return (loss, dx.reshape(x.shape), *[grads[n] for n in WEIGHTS], *[delta[n] for n in WEIGHTS],
            *[new_m[n] for n in WEIGHTS], *[new_v[n] for n in WEIGHTS])
```

```python
import functools

import numpy as np
import jax
import jax.numpy as jnp
from jax import lax
from jax.experimental import pallas as pl
from jax.experimental.pallas import tpu as pltpu

F32, BF16 = jnp.float32, jnp.bfloat16
MESH = pl.DeviceIdType.MESH

D_MODEL = 1024
HEAD_DIM = 64
ATTN_W = 512
DILATIONS = (1, 4, 16)
BLK = 128
ATTN_BLOCKS_PER_STEP = 4
ROPE_DIM = 16
ROPE_THETA = 500000.0
SGU_W = 512
SGU_CHUNK = 128
SGU_GROUPS = 8
D_FF = 2816
N_CHIPS = 4
FF_SHARD = D_FF // N_CHIPS
IN_COLS = 7680
EPS = 1e-6
NEG = -1e30
LANES = 128
VMEM_LIMIT = 52 * 1024 * 1024

ADAM_LR, ADAM_B1, ADAM_B2, ADAM_EPS, ADAM_WD, ADAM_STEP = 0.001, 0.9, 0.999, 1e-08, 0.01, 10

QKV_BLOCKS = 9


def _w_in_block(part, g):
    return part * len(DILATIONS) + g


def _cparams(ngrid):
    return pltpu.CompilerParams(dimension_semantics=("arbitrary",) * ngrid, vmem_limit_bytes=VMEM_LIMIT)


def _full(shape):
    return pl.BlockSpec(shape, lambda *_: (0,) * len(shape))


def _resident(shape):
    return pl.BlockSpec(shape, lambda *_: (0,) * len(shape), pipeline_mode=pl.Buffered(1))


NT = ((1,), (1,))
TN = ((0,), (0,))


def _rope(v, cos_t, sin_t):
    half = ROPE_DIM // 2
    first = (lax.broadcasted_iota(jnp.int32, cos_t.shape, 1) % HEAD_DIM) < half
    outs = []
    for cs in range(v.shape[1] // LANES):
        x = v[:, cs * LANES:(cs + 1) * LANES]
        partner = jnp.where(first, pltpu.roll(x, LANES - half, axis=1), pltpu.roll(x, half, axis=1))
        outs.append(x * cos_t + partner * sin_t)
    return outs[0] if len(outs) == 1 else jnp.concatenate(outs, axis=1)


def _spread_heads(v2, upper):
    other = pltpu.roll(v2, HEAD_DIM, axis=1)
    h0 = jnp.where(upper, other, v2)
    h1 = jnp.where(upper, v2, other)
    return jnp.concatenate([jnp.concatenate([h0, h0], axis=1), jnp.concatenate([h1, h1], axis=1)], axis=0)


def _sigmoid(v):
    return 0.5 * jnp.tanh(0.5 * v) + 0.5


def _rms_stats(v):
    r = lax.rsqrt(jnp.mean(v * v, axis=-1, keepdims=True) + EPS)
    return v * r, r


def _rms_bwd(dy, xhat, r, g):
    dxh = dy * g
    return r * (dxh - xhat * jnp.mean(dxh * xhat, axis=-1, keepdims=True))


def _head_sum_matrix():
    idx = np.arange(ATTN_W) // HEAD_DIM
    return jnp.asarray((idx[:, None] == idx[None, :]).astype(np.float32), dtype=BF16)


def _group_sum(v, e):
    hi = v.astype(BF16)
    lo = (v - hi.astype(F32)).astype(BF16)
    return jnp.dot(hi, e, preferred_element_type=F32) + jnp.dot(lo, e, preferred_element_type=F32)


TILE = 512


def _to_slabs(slab_ref, v):
    for cs in range(slab_ref.shape[0]):
        slab_ref[cs] = v[:, cs * LANES:(cs + 1) * LANES]


def _from_slabs(slab_ref):
    return jnp.concatenate([slab_ref[cs] for cs in range(slab_ref.shape[0])], axis=1)


def _class_rows(slab_ref, r, dil):
    n = slab_ref.shape[1] // dil
    return jnp.concatenate([slab_ref.at[cs][pl.ds(r, n, stride=dil), :] for cs in range(slab_ref.shape[0])], axis=1)


def _put_class_rows(slab_ref, r, dil, v):
    n = slab_ref.shape[1] // dil
    for cs in range(slab_ref.shape[0]):
        slab_ref.at[cs][pl.ds(r, n, stride=dil), :] = v[:, cs * LANES:(cs + 1) * LANES]


def _natural_from_group(slab_ref, grp_ref):
    dil = grp_ref.shape[0]
    for r in range(dil):
        _put_class_rows(slab_ref, r, dil, grp_ref[r].astype(F32))
    return _from_slabs(slab_ref)


def _group_from_natural(slab_ref, grp_ref, v):
    dil = grp_ref.shape[0]
    _to_slabs(slab_ref, v)
    for r in range(dil):
        grp_ref[r] = _class_rows(slab_ref, r, dil).astype(grp_ref.dtype)


def _group_spec(dil, tile, width):
    return pl.BlockSpec((dil, tile // dil, width), lambda i, *_: (0, i, 0))


def _slabs(tile, width):
    return pltpu.VMEM((width // LANES, tile, LANES), F32)


def _rope_consts():
    lane = np.arange(LANES) % HEAD_DIM
    fi = lane % (ROPE_DIM // 2)
    invf = np.where(lane < ROPE_DIM, ROPE_THETA ** (-(2.0 * fi) / ROPE_DIM), 0.0)
    sgn = np.where(lane < ROPE_DIM // 2, -1.0, np.where(lane < ROPE_DIM, 1.0, 0.0))
    return (jnp.asarray(invf.astype(np.float32)).reshape(1, LANES), jnp.asarray(sgn.astype(np.float32)).reshape(1, LANES))


def _rope_tables(pos_col):
    t = pos_col.shape[0]
    tile = min(t, TILE)
    invf, sgn = _rope_consts()

    def body(p_ref, f_ref, s_ref, c0, s0, c1, s1, c2, s2, slab_c, slab_s):
        ang = p_ref[...].astype(F32) * f_ref[...]
        cos, sin = jnp.cos(ang), jnp.sin(ang) * s_ref[...]
        c0[...] = cos
        s0[...] = sin
        _group_from_natural(slab_c, c1, cos)
        _group_from_natural(slab_s, s1, sin)
        for r in range(DILATIONS[2]):
            c2[r] = _class_rows(slab_c, r, DILATIONS[2])
            s2[r] = _class_rows(slab_s, r, DILATIONS[2])

    nat = pl.BlockSpec((tile, LANES), lambda i: (i, 0))
    specs, shapes = [nat, nat], [(t, LANES)] * 2
    for d in DILATIONS[1:]:
        specs += [_group_spec(d, tile, LANES)] * 2
        shapes += [(d, t // d, LANES)] * 2
    outs = pl.pallas_call(
        body, grid=(t // tile,),
        in_specs=[pl.BlockSpec((tile, 1), lambda i: (i, 0)), _full((1, LANES)), _full((1, LANES))],
        out_specs=specs, out_shape=[jax.ShapeDtypeStruct(s, F32) for s in shapes],
        scratch_shapes=[_slabs(tile, LANES)] * 2,
        compiler_params=_cparams(1), name="rope_tables")(pos_col, invf, sgn)
    return [(outs[2 * g].reshape(t, LANES), outs[2 * g + 1].reshape(t, LANES)) for g in range(len(DILATIONS))]


def _norm_fwd(x, g):
    t = x.shape[0]
    tile = min(t, TILE)

    def body(x_ref, g_ref, h0_ref, h1_ref, h2_ref, slab):
        xhat, _ = _rms_stats(x_ref[...])
        hn = xhat * g_ref[...]
        h0_ref[...] = hn.astype(BF16)
        _group_from_natural(slab, h1_ref, hn)
        for r in range(DILATIONS[2]):
            h2_ref[r] = _class_rows(slab, r, DILATIONS[2]).astype(BF16)

    nat = pl.BlockSpec((tile, D_MODEL), lambda i: (i, 0))
    return pl.pallas_call(
        body, grid=(t // tile,),
        in_specs=[nat, _full((1, D_MODEL))],
        out_specs=[nat] + [_group_spec(d, tile, D_MODEL) for d in DILATIONS[1:]],
        out_shape=[jax.ShapeDtypeStruct((t, D_MODEL), BF16)]
        + [jax.ShapeDtypeStruct((d, t // d, D_MODEL), BF16) for d in DILATIONS[1:]],
        scratch_shapes=[_slabs(tile, D_MODEL)],
        compiler_params=_cparams(1), name="norm1_fwd")(x, g)


GU_COLS = 3072
GROUP_COLS = 1536
GU_HALF = GU_COLS // 2


def _w_in_spec(width, block):
    return pl.BlockSpec((D_MODEL, width), lambda i: (0, block), pipeline_mode=pl.Buffered(1))


def _gu_w_specs():
    first = QKV_BLOCKS * ATTN_W // GU_HALF
    return [_w_in_spec(GU_HALF, first), _w_in_spec(GU_HALF, first + 1)]


def _group_w_specs(g):
    return [_w_in_spec(ATTN_W, _w_in_block(part, g)) for part in range(3)]


def _in_proj(hs, w_in, tables):
    t = hs[0].shape[0]
    tm = min(t, 1024)

    def body_gu(h_ref, w0_ref, w1_ref, o_ref):
        h = h_ref[...]
        o_ref[:, 0:GU_HALF] = jnp.dot(h, w0_ref[...], preferred_element_type=F32).astype(BF16)
        o_ref[:, GU_HALF:] = jnp.dot(h, w1_ref[...], preferred_element_type=F32).astype(BF16)

    gu = _token_call("in_proj_gates_uv", body_gu, t, tm,
                     [(hs[0], _rows_spec(tm, D_MODEL))] + [(w_in, s) for s in _gu_w_specs()],
                     [((t, GU_COLS), BF16, _rows_spec(tm, GU_COLS))])[0]

    qkvs = []
    for g in range(len(DILATIONS)):

        def body_qkv(h_ref, wq_ref, wk_ref, wv_ref, cos_ref, sin_ref, o_ref):
            h = h_ref[...]
            cos_w, sin_w = cos_ref[...], sin_ref[...]
            q = jnp.dot(h, wq_ref[...], preferred_element_type=F32)
            o_ref[:, 0:ATTN_W] = (_rope(q, cos_w, sin_w) * HEAD_DIM ** -0.5).astype(BF16)
            k = jnp.dot(h, wk_ref[...], preferred_element_type=F32)
            o_ref[:, ATTN_W:2 * ATTN_W] = _rope(k, cos_w, sin_w).astype(BF16)
            o_ref[:, 2 * ATTN_W:] = jnp.dot(h, wv_ref[...], preferred_element_type=F32).astype(BF16)

        cos_t, sin_t = tables[g]
        qkvs.append(_token_call(
            f"in_proj_qkv_g{g}", body_qkv, t, tm,
            [(hs[g].reshape(t, D_MODEL), _rows_spec(tm, D_MODEL))] + [(w_in, s) for s in _group_w_specs(g)]
            + [(cos_t, _rows_spec(tm, LANES)), (sin_t, _rows_spec(tm, LANES))],
            [((t, GROUP_COLS), BF16, _rows_spec(tm, GROUP_COLS))])[0])
    return gu, qkvs


def _attn_masks(n):
    row = lax.broadcasted_iota(jnp.int32, (2 * BLK, 2 * BLK), 0) % BLK
    col = lax.broadcasted_iota(jnp.int32, (2 * BLK, 2 * BLK), 1)
    diff = BLK + row - col
    valid = (diff >= 0) & (diff <= BLK) & ((col >= BLK) | (n > 0))
    upper = lax.broadcasted_iota(jnp.int32, (BLK, LANES), 1) >= HEAD_DIM
    return valid, upper


def _stack_heads(v2, upper):
    zero = jnp.zeros_like(v2)
    return jnp.concatenate([jnp.where(upper, zero, v2), jnp.where(upper, v2, zero)], axis=0)


def _unstack_heads(v, upper):
    return jnp.where(upper, v[BLK:], v[:BLK])


def _attn_fwd(qkv, g, dil):
    t = qkv.shape[0]
    length = t // dil
    nb = length // BLK
    per_step = min(nb, ATTN_BLOCKS_PER_STEP)
    view = qkv.reshape(dil, length, GROUP_COLS)

    def body(q_ref, kc_ref, kp_ref, vc_ref, vp_ref, o_ref, l_ref, kwin, vwin):
        n = pl.program_id(1)
        kwin[0:BLK] = kp_ref[...]
        kwin[BLK:] = kc_ref[...]
        vwin[0:BLK] = vp_ref[...]
        vwin[BLK:] = vc_ref[...]

        def block(b, carry):
            valid, upper = _attn_masks(n * per_step + b)
            rows = pl.ds(pl.multiple_of(b * BLK, BLK), BLK)
            window = pl.ds(pl.multiple_of(b * BLK, BLK), 2 * BLK)
            slabs = [slice(p * LANES, (p + 1) * LANES) for p in range(ATTN_W // LANES)]
            ss = [lax.dot_general(_stack_heads(q_ref[rows, sl], upper), kwin[window, sl], (NT, ((), ())),
                                  preferred_element_type=F32) for sl in slabs]
            soft = []
            for s in ss:
                s = jnp.where(valid, s, NEG)
                m = jnp.max(s, axis=1, keepdims=True)
                pe = jnp.exp(s - m)
                soft.append((m, pe, jnp.sum(pe, axis=1, keepdims=True)))
            for sl, (m, pe, den) in zip(slabs, soft):
                o = jnp.dot(pe.astype(BF16), vwin[window, sl], preferred_element_type=F32) / den
                lse = jnp.broadcast_to(m + jnp.log(den), (2 * BLK, LANES))
                o_ref[rows, sl] = _unstack_heads(o, upper).astype(BF16)
                l_ref[rows, sl] = _unstack_heads(lse, upper)
            return carry

        lax.fori_loop(0, per_step, block, 0)

    rows = per_step * BLK
    cur = lambda part: pl.BlockSpec((None, rows, ATTN_W), lambda r, n: (r, n, part))
    prev = lambda part: pl.BlockSpec((None, BLK, ATTN_W), lambda r, n: (r, jnp.maximum(n * per_step - 1, 0), part))
    out_spec = pl.BlockSpec((None, rows, ATTN_W), lambda r, n: (r, n, 0))
    return pl.pallas_call(
        body, grid=(dil, nb // per_step),
        in_specs=[cur(0), cur(1), prev(1), cur(2), prev(2)],
        out_specs=[out_spec, out_spec],
        out_shape=[jax.ShapeDtypeStruct((dil, length, ATTN_W), BF16), jax.ShapeDtypeStruct((dil, length, ATTN_W), F32)],
        scratch_shapes=[pltpu.VMEM((rows + BLK, ATTN_W), BF16)] * 2,
        compiler_params=_cparams(2), name=f"attn_fwd_g{g}")(view, view, view, view, view)


def _alphas(l0, l1, l2):
    m = jnp.maximum(jnp.maximum(l0, l1), l2)
    e0, e1, e2 = jnp.exp(l0 - m), jnp.exp(l1 - m), jnp.exp(l2 - m)
    inv = 1.0 / (e0 + e1 + e2)
    return e0 * inv, e1 * inv, e2 * inv


def _natural_group_values(o_refs, l_refs, slabs):
    os_ = [o_refs[0][0].astype(F32)] + [_natural_from_group(slabs[2 * g - 2], o_refs[g]) for g in (1, 2)]
    ls_ = [l_refs[0][0]] + [_natural_from_group(slabs[2 * g - 1], l_refs[g]) for g in (1, 2)]
    return os_, ls_


def _combine_fwd(os_, ls_):
    t = os_[0].shape[1]
    tile = min(t, TILE)

    def body(o0, o1, o2, l0, l1, l2, a_ref, *slabs):
        ov, lv = _natural_group_values((o0, o1, o2), (l0, l1, l2), slabs)
        a0, a1, a2 = _alphas(*lv)
        a_ref[...] = (a0 * ov[0] + a1 * ov[1] + a2 * ov[2]).astype(BF16)

    specs = [_group_spec(d, tile, ATTN_W) for d in DILATIONS]
    return pl.pallas_call(
        body, grid=(t // tile,), in_specs=specs * 2, out_specs=pl.BlockSpec((tile, ATTN_W), lambda i: (i, 0)),
        out_shape=jax.ShapeDtypeStruct((t, ATTN_W), BF16),
        scratch_shapes=[_slabs(tile, ATTN_W)] * 4,
        compiler_params=_cparams(1), name="combine_fwd")(*os_, *ls_)


def _combine_bwd(dattn, os_, ls_):
    t = dattn.shape[0]
    tile = min(t, TILE)
    e = _head_sum_matrix()

    def body(d_ref, o0, o1, o2, l0, l1, l2, e_ref, do0, do1, do2, c0, c1, c2, *slabs):
        ov, lv = _natural_group_values((o0, o1, o2), (l0, l1, l2), slabs)
        alphas = _alphas(*lv)
        d = d_ref[...]
        attn = alphas[0] * ov[0] + alphas[1] * ov[1] + alphas[2] * ov[2]
        s = _group_sum(d * attn, e_ref[...])
        do0[0] = (alphas[0] * d).astype(BF16)
        c0[0] = -alphas[0] * s
        for g, do_ref, c_ref in ((1, do1, c1), (2, do2, c2)):
            _group_from_natural(slabs[2 * g - 2], do_ref, alphas[g] * d)
            _group_from_natural(slabs[2 * g - 1], c_ref, -alphas[g] * s)

    specs = [_group_spec(d, tile, ATTN_W) for d in DILATIONS]
    shapes = [(d, t // d, ATTN_W) for d in DILATIONS]
    outs = pl.pallas_call(
        body, grid=(t // tile,),
        in_specs=[pl.BlockSpec((tile, ATTN_W), lambda i: (i, 0))] + specs * 2 + [_full((ATTN_W, ATTN_W))],
        out_specs=specs * 2,
        out_shape=[jax.ShapeDtypeStruct(s, BF16) for s in shapes] + [jax.ShapeDtypeStruct(s, F32) for s in shapes],
        scratch_shapes=[_slabs(tile, ATTN_W)] * 4,
        compiler_params=_cparams(1), name="combine_bwd")(dattn, *os_, *ls_, e)
    return outs[:3], outs[3:]


def _attn_bwd(qkv, do, cc, lse, cos_t, sin_t, g, dil):
    t = qkv.shape[0]
    length = t // dil
    nb = length // BLK
    per_step = min(nb, ATTN_BLOCKS_PER_STEP)
    nsteps = nb // per_step
    rows_per_step = per_step * BLK
    qkv_v = qkv.reshape(dil, length, GROUP_COLS)
    cos_v, sin_v = (a.reshape(dil, length, LANES) for a in (cos_t, sin_t))
    scale = HEAD_DIM ** -0.5
    dq_cols, dk_cols, dv_cols = (slice(i * ATTN_W, (i + 1) * ATTN_W) for i in range(3))

    def body(q_ref, kc_ref, kp_ref, vc_ref, vp_ref, do_ref, c_ref, l_ref, cosc, sinc, cosp, sinp,
             out_ref, acc, kwin, vwin, cwin, swin):
        n = pl.program_id(1)

        def one_block(b):
            valid, upper = _attn_masks(n * per_step + b)
            start = b * BLK if isinstance(b, int) else pl.multiple_of(b * BLK, BLK)
            rows, before, window = pl.ds(start, BLK), pl.ds(start, BLK), pl.ds(start, 2 * BLK)
            own = pl.ds(start + BLK, BLK)
            dq_parts, dkp_parts, dkc_parts, dvp_parts, dvc_parts = [], [], [], [], []
            npairs = ATTN_W // LANES
            slabs = [slice(p * LANES, (p + 1) * LANES) for p in range(npairs)]
            qss = [_stack_heads(q_ref[rows, sl], upper) for sl in slabs]
            doss = [_stack_heads(do_ref[rows, sl], upper) for sl in slabs]
            ss = [lax.dot_general(qss[p], kwin[window, slabs[p]], (NT, ((), ())), preferred_element_type=F32) for p in range(npairs)]
            dpvs = [lax.dot_general(doss[p], vwin[window, slabs[p]], (NT, ((), ())), preferred_element_type=F32)
                    for p in range(npairs)]
            pes = [jnp.exp(jnp.where(valid, ss[p], NEG) - _spread_heads(l_ref[rows, slabs[p]], upper)) for p in range(npairs)]
            dss = [(pes[p] * (dpvs[p] + _spread_heads(c_ref[rows, slabs[p]], upper))).astype(BF16) for p in range(npairs)]
            for p in range(npairs):
                qs, dos, ds = qss[p], doss[p], dss[p]
                dq2 = _unstack_heads(jnp.dot(ds, kwin[window, slabs[p]], preferred_element_type=F32), upper)
                dk2 = lax.dot_general(ds, qs, (TN, ((), ())), preferred_element_type=F32)
                dv2 = lax.dot_general(pes[p].astype(BF16), dos, (TN, ((), ())), preferred_element_type=F32)
                dq_parts.append(dq2)
                dkp_parts.append(dk2[:BLK])
                dkc_parts.append(dk2[BLK:])
                dvp_parts.append(dv2[:BLK])
                dvc_parts.append(dv2[BLK:])
            dq = _rope(jnp.concatenate(dq_parts, axis=1) * scale, cwin[own, :], swin[own, :])
            dkc = _rope(jnp.concatenate(dkc_parts, axis=1), cwin[own, :], swin[own, :])
            dkp = _rope(jnp.concatenate(dkp_parts, axis=1), cwin[before, :], swin[before, :])
            return dq, dkp, dkc, jnp.concatenate(dvp_parts, axis=1), jnp.concatenate(dvc_parts, axis=1)

        @pl.when(n < nsteps)
        def _():
            kwin[0:BLK] = kp_ref[...]
            kwin[BLK:] = kc_ref[...]
            vwin[0:BLK] = vp_ref[...]
            vwin[BLK:] = vc_ref[...]
            cwin[0:BLK] = cosp[...]
            cwin[BLK:] = cosc[...]
            swin[0:BLK] = -sinp[...]
            swin[BLK:] = -sinc[...]
            dq, dkp, dkc, dvp, dvc = one_block(0)
            last = slice(rows_per_step - BLK, rows_per_step)

            @pl.when(n > 0)
            def _():
                if per_step > 1:
                    out_ref[0:rows_per_step - BLK, :] = acc[0:rows_per_step - BLK, :].astype(BF16)
                out_ref[last, dq_cols] = acc[last, dq_cols].astype(BF16)
                out_ref[last, dk_cols] = (acc[last, dk_cols] + dkp).astype(BF16)
                out_ref[last, dv_cols] = (acc[last, dv_cols] + dvp).astype(BF16)

            acc[0:BLK, dq_cols] = dq
            acc[0:BLK, dk_cols] = dkc
            acc[0:BLK, dv_cols] = dvc

            def later(b, carry):
                dq, dkp, dkc, dvp, dvc = one_block(b)
                start = pl.multiple_of(b * BLK, BLK)
                before, rows = pl.ds(start - BLK, BLK), pl.ds(start, BLK)
                acc[before, dk_cols] += dkp
                acc[before, dv_cols] += dvp
                acc[rows, dq_cols] = dq
                acc[rows, dk_cols] = dkc
                acc[rows, dv_cols] = dvc
                return carry

            lax.fori_loop(1, per_step, later, 0)

        @pl.when(n == flush_at)
        def _():
            out_ref[...] = acc[...].astype(BF16)

    flush_at = nsteps - 1 if nsteps == 1 else nsteps
    out_lag = 0 if nsteps == 1 else 1
    nc = lambda n: jnp.minimum(n, nsteps - 1)
    npv = lambda n: jnp.maximum(jnp.minimum(n, nsteps - 1) * per_step - 1, 0)
    cur = lambda part: pl.BlockSpec((None, rows_per_step, ATTN_W), lambda r, n: (r, nc(n), part))
    prev = lambda part: pl.BlockSpec((None, BLK, ATTN_W), lambda r, n: (r, npv(n), part))
    row = pl.BlockSpec((None, rows_per_step, ATTN_W), lambda r, n: (r, nc(n), 0))
    tab_c = pl.BlockSpec((None, rows_per_step, LANES), lambda r, n: (r, nc(n), 0))
    tab_p = pl.BlockSpec((None, BLK, LANES), lambda r, n: (r, npv(n), 0))
    out_spec = pl.BlockSpec((None, rows_per_step, GROUP_COLS), lambda r, n: (r, jnp.maximum(n - out_lag, 0), 0))
    out = pl.pallas_call(
        body, grid=(dil, nsteps + out_lag),
        in_specs=[cur(0), cur(1), prev(1), cur(2), prev(2), row, row, row, tab_c, tab_c, tab_p, tab_p],
        out_specs=out_spec,
        out_shape=jax.ShapeDtypeStruct((dil, length, GROUP_COLS), BF16),
        scratch_shapes=[pltpu.VMEM((rows_per_step, GROUP_COLS), F32)]
        + [pltpu.VMEM((rows_per_step + BLK, ATTN_W), BF16)] * 2 + [pltpu.VMEM((rows_per_step + BLK, LANES), F32)] * 2,
        compiler_params=_cparams(2), name=f"attn_bwd_g{g}")(
            qkv_v, qkv_v, qkv_v, qkv_v, qkv_v, do, cc, lse, cos_v, sin_v, cos_v, sin_v)
    return out.reshape(t, GROUP_COLS)


SQRT_HALF = 0.7071067811865476
INV_SQRT_2PI = 0.3989422804014327


def _sgu_core(uv, g, b, w_ref, bias):
    cdf = 0.5 * (1.0 + lax.erf(uv * SQRT_HALF))
    z = uv * cdf
    u, v = z[:, :SGU_W], z[:, SGU_W:]
    mu = jnp.mean(v, axis=1, keepdims=True)
    xc = v - mu
    rs = lax.rsqrt(jnp.mean(xc * xc, axis=1, keepdims=True) + EPS)
    xhat = xc * rs
    vn = xhat * g + b
    row = lax.broadcasted_iota(jnp.int32, (SGU_CHUNK, SGU_CHUNK), 0)
    col = lax.broadcasted_iota(jnp.int32, (SGU_CHUNK, SGU_CHUNK), 1)
    tril = row >= col
    upper = lax.broadcasted_iota(jnp.int32, (SGU_CHUNK, LANES), 1) >= SGU_W // SGU_GROUPS
    ws, vlo, vhi, mixed = [], [], [], []
    for pr in range(SGU_W // LANES):
        sl = slice(pr * LANES, (pr + 1) * LANES)
        w0 = jnp.where(tril, w_ref[2 * pr], 0.0).astype(BF16)
        w1 = jnp.where(tril, w_ref[2 * pr + 1], 0.0).astype(BF16)
        vn2 = vn[:, sl]
        lo = jnp.where(upper, 0.0, vn2).astype(BF16)
        hi = jnp.where(upper, vn2, 0.0).astype(BF16)
        mixed.append(jnp.dot(w0, lo, preferred_element_type=F32) + jnp.dot(w1, hi, preferred_element_type=F32)
                     + bias[:, sl])
        ws.append((w0, w1))
        vlo.append(lo)
        vhi.append(hi)
    return cdf, u, xhat, rs, jnp.concatenate(mixed, axis=1), ws, vlo, vhi, tril, upper


SGU_STEP = 4 * SGU_CHUNK


def _for_chunks(step_rows, fn):
    def one(ci, carry):
        fn(pl.ds(pl.multiple_of(ci * SGU_CHUNK, SGU_CHUNK), SGU_CHUNK))
        return carry

    lax.fori_loop(0, step_rows // SGU_CHUNK, one, 0)


def _sgu_fwd(gu, ln_g, ln_b, w_s, bias_exp):
    t = gu.shape[0]
    step = min(t, SGU_STEP)

    def body(uv_ref, g_ref, b_ref, w_ref, bias_ref, o_ref):
        def chunk(rows):
            _, u, _, _, mixed, *_ = _sgu_core(uv_ref[rows, :].astype(F32), g_ref[...], b_ref[...], w_ref, bias_ref[...])
            o_ref[rows, :] = (u * mixed).astype(BF16)

        _for_chunks(step, chunk)

    return pl.pallas_call(
        body, grid=(t // step,),
        in_specs=[pl.BlockSpec((step, 2 * SGU_W), lambda n: (n, 0)), _full((1, SGU_W)), _full((1, SGU_W)),
                  _full((SGU_GROUPS, SGU_CHUNK, SGU_CHUNK)), _full((SGU_CHUNK, SGU_W))],
        out_specs=pl.BlockSpec((step, SGU_W), lambda n: (n, 0)),
        out_shape=jax.ShapeDtypeStruct((t, SGU_W), BF16),
        compiler_params=_cparams(1), name="sgu_fwd")(gu, ln_g, ln_b, w_s, bias_exp)


def _sgu_bwd(dproj, gu, dsgu, ln_g, ln_b, w_s, bias_exp):
    t = gu.shape[0]
    step = min(t, SGU_STEP)
    nsteps = t // step
    e = _head_sum_matrix()

    def body(dp_in, uv_ref, ds_ref, g_ref, b_ref, w_ref, bias_ref, e_ref, out_ref, dw_ref, dbias_ref, dg_ref, db_ref):
        n = pl.program_id(0)

        @pl.when(n == 0)
        def _():
            dw_ref[...] = jnp.zeros(dw_ref.shape, F32)
            dbias_ref[...] = jnp.zeros(dbias_ref.shape, F32)
            dg_ref[...] = jnp.zeros(dg_ref.shape, F32)
            db_ref[...] = jnp.zeros(db_ref.shape, F32)

        _for_chunks(step, functools.partial(chunk, uv_ref, ds_ref, g_ref, b_ref, w_ref, bias_ref, out_ref, dw_ref, dbias_ref,
                                            dg_ref, db_ref))

        @pl.when(n == nsteps - 1)
        def _():
            dbias_ref[...] = _group_sum(dbias_ref[...], e_ref[...])

    def chunk(uv_ref, ds_ref, g_ref, b_ref, w_ref, bias_ref, out_ref, dw_ref, dbias_ref, dg_ref, db_ref, rows):
        uv = uv_ref[rows, :].astype(F32)
        g = g_ref[...]
        cdf, u, xhat, rs, mixed, ws, vlo, vhi, tril, upper = _sgu_core(uv, g, b_ref[...], w_ref, bias_ref[...])
        dsg = ds_ref[rows, :]
        du = dsg * mixed
        dmixed = dsg * u
        dbias_ref[...] += dmixed
        dvn = []
        for pr in range(SGU_W // LANES):
            sl = slice(pr * LANES, (pr + 1) * LANES)
            dm2 = dmixed[:, sl]
            dlo = jnp.where(upper, 0.0, dm2).astype(BF16)
            dhi = jnp.where(upper, dm2, 0.0).astype(BF16)
            w0, w1 = ws[pr]
            dvn.append(lax.dot_general(w0, dlo, (TN, ((), ())), preferred_element_type=F32)
                       + lax.dot_general(w1, dhi, (TN, ((), ())), preferred_element_type=F32))
            dw0 = lax.dot_general(dlo, vlo[pr], (NT, ((), ())), preferred_element_type=F32)
            dw1 = lax.dot_general(dhi, vhi[pr], (NT, ((), ())), preferred_element_type=F32)
            dw_ref[2 * pr] += jnp.where(tril, dw0, 0.0)
            dw_ref[2 * pr + 1] += jnp.where(tril, dw1, 0.0)
        dvn = jnp.concatenate(dvn, axis=1)
        dg_ref[...] += jnp.sum(dvn * xhat, axis=0, keepdims=True)
        db_ref[...] += jnp.sum(dvn, axis=0, keepdims=True)
        dxh = dvn * g
        dv = rs * (dxh - jnp.mean(dxh, axis=1, keepdims=True) - xhat * jnp.mean(dxh * xhat, axis=1, keepdims=True))
        dz = jnp.concatenate([du, dv], axis=1)
        dgelu = cdf + uv * (INV_SQRT_2PI * jnp.exp(-0.5 * uv * uv))
        out_ref[rows, :] = (dz * dgelu).astype(BF16)

    outs = pl.pallas_call(
        body, grid=(nsteps,),
        in_specs=[pl.BlockSpec(memory_space=pl.ANY), pl.BlockSpec((step, 2 * SGU_W), lambda n: (n, 0)),
                  pl.BlockSpec((step, SGU_W), lambda n: (n, 0)), _full((1, SGU_W)), _full((1, SGU_W)),
                  _full((SGU_GROUPS, SGU_CHUNK, SGU_CHUNK)), _full((SGU_CHUNK, SGU_W)), _full((ATTN_W, ATTN_W))],
        out_specs=[pl.BlockSpec((step, 2 * SGU_W), lambda n: (n, 0)), _full((SGU_GROUPS, SGU_CHUNK, SGU_CHUNK)),
                   _full((SGU_CHUNK, SGU_W)), _full((1, SGU_W)), _full((1, SGU_W))],
        out_shape=[jax.ShapeDtypeStruct(dproj.shape, BF16), jax.ShapeDtypeStruct((SGU_GROUPS, SGU_CHUNK, SGU_CHUNK), F32),
                   jax.ShapeDtypeStruct((SGU_CHUNK, SGU_W), F32), jax.ShapeDtypeStruct((1, SGU_W), F32),
                   jax.ShapeDtypeStruct((1, SGU_W), F32)],
        input_output_aliases={0: 0},
        compiler_params=_cparams(1), name="sgu_bwd")(dproj, gu, dsgu, ln_g, ln_b, w_s, bias_exp, e)
    return outs


def _merge_fwd(attn, sgu, gu, x, w_pa, w_ps, w_out, g2):
    t = x.shape[0]
    tm = min(t, 512)

    def body(a_ref, s_ref, ga_ref, gb_ref, x_ref, wpa, wps, wo, g_ref, pa_ref, ps_ref, m_ref, x1_ref, h2_ref):
        pa = jnp.dot(a_ref[...], wpa[...], preferred_element_type=F32)
        ps = jnp.dot(s_ref[...], wps[...], preferred_element_type=F32)
        merged = (_sigmoid(ga_ref[...].astype(F32)) * pa + _sigmoid(gb_ref[...].astype(F32)) * ps).astype(BF16)
        x1 = x_ref[...] + jnp.dot(merged, wo[...], preferred_element_type=F32)
        xhat, _ = _rms_stats(x1)
        pa_ref[...] = pa.astype(BF16)
        ps_ref[...] = ps.astype(BF16)
        m_ref[...] = merged
        x1_ref[...] = x1
        h2_ref[...] = (xhat * g_ref[...]).astype(BF16)

    half = pl.BlockSpec((tm, ATTN_W), lambda i: (i, 0))
    full = pl.BlockSpec((tm, D_MODEL), lambda i: (i, 0))
    return pl.pallas_call(
        body, grid=(t // tm,),
        in_specs=[half, half, pl.BlockSpec((tm, D_MODEL), lambda i: (i, 1)), pl.BlockSpec((tm, D_MODEL), lambda i: (i, 2)),
                  full, _resident((ATTN_W, D_MODEL)), _resident((SGU_W, D_MODEL)), _resident((D_MODEL, D_MODEL)),
                  _full((1, D_MODEL))],
        out_specs=[full] * 5,
        out_shape=[jax.ShapeDtypeStruct((t, D_MODEL), BF16), jax.ShapeDtypeStruct((t, D_MODEL), BF16),
                   jax.ShapeDtypeStruct((t, D_MODEL), BF16), jax.ShapeDtypeStruct((t, D_MODEL), F32),
                   jax.ShapeDtypeStruct((t, D_MODEL), BF16)],
        compiler_params=_cparams(1), name="merge_fwd")(attn, sgu, gu, gu, x, w_pa, w_ps, w_out, g2)


def _merge_bwd(dx1b, gu, pa, ps, w_pa, w_ps, w_out):
    t = dx1b.shape[0]
    tm = min(t, 512)

    def body(d_ref, ga_ref, gb_ref, pa_ref, ps_ref, wpa, wps, wo, out_ref, dpa_ref, dps_ref, da_ref, dsg_ref):
        dm = lax.dot_general(d_ref[...], wo[...], (NT, ((), ())), preferred_element_type=F32)
        sa, sb = _sigmoid(ga_ref[...].astype(F32)), _sigmoid(gb_ref[...].astype(F32))
        dpa = (dm * sa).astype(BF16)
        dps = (dm * sb).astype(BF16)
        out_ref[:, 0:D_MODEL] = jnp.zeros((tm, D_MODEL), BF16)
        out_ref[:, D_MODEL:2 * D_MODEL] = (dm * pa_ref[...].astype(F32) * sa * (1.0 - sa)).astype(BF16)
        out_ref[:, 2 * D_MODEL:GU_COLS] = (dm * ps_ref[...].astype(F32) * sb * (1.0 - sb)).astype(BF16)
        dpa_ref[...] = dpa
        dps_ref[...] = dps
        da_ref[...] = lax.dot_general(dpa, wpa[...], (NT, ((), ())), preferred_element_type=F32)
        dsg_ref[...] = lax.dot_general(dps, wps[...], (NT, ((), ())), preferred_element_type=F32)

    half = pl.BlockSpec((tm, ATTN_W), lambda i: (i, 0))
    full = pl.BlockSpec((tm, D_MODEL), lambda i: (i, 0))
    return pl.pallas_call(
        body, grid=(t // tm,),
        in_specs=[full, pl.BlockSpec((tm, D_MODEL), lambda i: (i, 1)),
                  pl.BlockSpec((tm, D_MODEL), lambda i: (i, 2)), full, full,
                  _resident((ATTN_W, D_MODEL)), _resident((SGU_W, D_MODEL)), _resident((D_MODEL, D_MODEL))],
        out_specs=[pl.BlockSpec((tm, GU_COLS), lambda i: (i, 0)), full, full, half, half],
        out_shape=[jax.ShapeDtypeStruct((t, GU_COLS), BF16), jax.ShapeDtypeStruct((t, D_MODEL), BF16),
                   jax.ShapeDtypeStruct((t, D_MODEL), BF16), jax.ShapeDtypeStruct((t, ATTN_W), F32),
                   jax.ShapeDtypeStruct((t, SGU_W), F32)],
        compiler_params=_cparams(1), name="merge_bwd")(dx1b, gu, gu, pa, ps, w_pa, w_ps, w_out)


def _token_call(name, body, t, tm, ins, outs, reds=(), scratch=()):
    return pl.pallas_call(
        body, grid=(t // tm,), in_specs=[s for _, s in ins],
        out_specs=[o[2] for o in outs] + [_full(r) for r in reds],
        out_shape=[jax.ShapeDtypeStruct(o[0], o[1]) for o in outs] + [jax.ShapeDtypeStruct(r, F32) for r in reds],
        scratch_shapes=list(scratch), compiler_params=_cparams(1), name=name)(*[a for a, _ in ins])


def _rows_spec(tm, width):
    return pl.BlockSpec((tm, width), lambda i: (i, 0))


def _chips_spec(tm):
    return pl.BlockSpec((N_CHIPS, tm, FF_SHARD), lambda i: (0, i, 0))


def _zero_at_start(*refs):
    @pl.when(pl.program_id(0) == 0)
    def _():
        for r in refs:
            r[...] = jnp.zeros(r.shape, r.dtype)


def _ffn_fwd(h2, w_g, w_u):
    t = h2.shape[0]
    tm = min(t, 512)

    def body(h_ref, wg_ref, wu_ref, fa_ref, fb_ref, ff_ref):
        h = h_ref[...]
        for s in range(N_CHIPS):
            a = jnp.dot(h, wg_ref[s], preferred_element_type=F32)
            b = jnp.dot(h, wu_ref[s], preferred_element_type=F32)
            sg = _sigmoid(a)
            silu = a * sg
            fa_ref[s] = (b * (sg * (1.0 + a * (1.0 - sg)))).astype(BF16)
            fb_ref[s] = silu.astype(BF16)
            ff_ref[s] = (silu * b).astype(BF16)

    shp = (N_CHIPS, t, FF_SHARD)
    w_spec = _resident((N_CHIPS, D_MODEL, FF_SHARD))
    return _token_call("ffn_fwd", body, t, tm, [(h2, _rows_spec(tm, D_MODEL)), (w_g, w_spec), (w_u, w_spec)],
                       [(shp, BF16, _chips_spec(tm))] * 3)


def _ffn_down_loss(ff, w_d, x1, tgt, gf):
    t = x1.shape[0]
    tm = min(t, 512)

    def body(ff_ref, wd_ref, x1_ref, tgt_ref, g_ref, dx2_ref, dx2b_ref, loss_ref, dgf_ref):
        _zero_at_start(loss_ref, dgf_ref)
        acc = jnp.dot(ff_ref[0], wd_ref[0], preferred_element_type=F32)
        for s in range(1, N_CHIPS):
            acc = acc + jnp.dot(ff_ref[s], wd_ref[s], preferred_element_type=F32)
        x2 = x1_ref[...] + acc
        g = g_ref[...]
        xhat, rr = _rms_stats(x2)
        diff = xhat * g - tgt_ref[...]
        rows = jnp.sum(diff * diff, axis=1, keepdims=True)
        loss_ref[...] += jnp.broadcast_to(jnp.sum(rows, axis=0, keepdims=True) * (0.5 / D_MODEL), (1, LANES))
        dy = diff * (1.0 / D_MODEL)
        dgf_ref[...] += jnp.sum(dy * xhat, axis=0, keepdims=True)
        dx2 = _rms_bwd(dy, xhat, rr, g)
        dx2_ref[...] = dx2
        dx2b_ref[...] = dx2.astype(BF16)

    row = _rows_spec(tm, D_MODEL)
    return _token_call("ffn_down_loss", body, t, tm,
                       [(ff, _chips_spec(tm)), (w_d, _resident((N_CHIPS, FF_SHARD, D_MODEL))), (x1, row), (tgt, row),
                        (gf, _full((1, D_MODEL)))],
                       [((t, D_MODEL), F32, row), ((t, D_MODEL), BF16, row)], reds=[(1, LANES), (1, D_MODEL)])


def _ffn_bwd_act(dx2b, w_d, fa, fb):
    t = dx2b.shape[0]
    tm = min(t, 512)

    def body(d_ref, wd_ref, fa_ref, fb_ref, da_ref, db_ref):
        d = d_ref[...]
        for s in range(N_CHIPS):
            dff = lax.dot_general(d, wd_ref[s], (NT, ((), ())), preferred_element_type=F32)
            da_ref[s] = (dff * fa_ref[s].astype(F32)).astype(BF16)
            db_ref[s] = (dff * fb_ref[s].astype(F32)).astype(BF16)

    shp = (N_CHIPS, t, FF_SHARD)
    return _token_call("ffn_bwd_act", body, t, tm,
                       [(dx2b, _rows_spec(tm, D_MODEL)), (w_d, _resident((N_CHIPS, FF_SHARD, D_MODEL))),
                        (fa, _chips_spec(tm)), (fb, _chips_spec(tm))],
                       [(shp, BF16, _chips_spec(tm))] * 2)


def _ffn_bwd_in(da, db, w_g, w_u, x1, dx2, g2):
    t = x1.shape[0]
    tm = min(t, 512)

    def body(da_ref, db_ref, wg_ref, wu_ref, x1_ref, dx2_ref, g_ref, dx1_ref, dx1b_ref, dg_ref):
        _zero_at_start(dg_ref)
        acc = None
        for s in range(N_CHIPS):
            part = (lax.dot_general(da_ref[s], wg_ref[s], (NT, ((), ())), preferred_element_type=F32)
                    + lax.dot_general(db_ref[s], wu_ref[s], (NT, ((), ())), preferred_element_type=F32))
            acc = part if acc is None else acc + part
        xhat, rr = _rms_stats(x1_ref[...])
        dg_ref[...] += jnp.sum(acc * xhat, axis=0, keepdims=True)
        dx1 = dx2_ref[...] + _rms_bwd(acc, xhat, rr, g_ref[...])
        dx1_ref[...] = dx1
        dx1b_ref[...] = dx1.astype(BF16)

    row = _rows_spec(tm, D_MODEL)
    w_spec = _resident((N_CHIPS, D_MODEL, FF_SHARD))
    return _token_call("ffn_bwd_in", body, t, tm,
                       [(da, _chips_spec(tm)), (db, _chips_spec(tm)), (w_g, w_spec), (w_u, w_spec), (x1, row), (dx2, row),
                        (g2, _full((1, D_MODEL)))],
                       [((t, D_MODEL), F32, row), ((t, D_MODEL), BF16, row)], reds=[(1, D_MODEL)])


def _group_dh(d, w_refs):
    dh = None
    for part, w_ref in enumerate(w_refs):
        term = lax.dot_general(d[:, part * ATTN_W:(part + 1) * ATTN_W], w_ref[...], (NT, ((), ())),
                               preferred_element_type=F32)
        dh = term if dh is None else dh + term
    return dh


def _in_proj_bwd(dgu, dqkvs, w_in, x, dx1, g1):
    t = x.shape[0]
    tile = min(t, TILE)
    ngroups = len(DILATIONS)

    def body(*refs):
        dgu_ref, dq_refs = refs[0], refs[1:1 + ngroups]
        w0_ref, w1_ref = refs[1 + ngroups:3 + ngroups]
        wg_refs = [refs[3 + ngroups + 3 * g:6 + ngroups + 3 * g] for g in range(ngroups)]
        x_ref, dx1_ref, g_ref, dx_ref, dg_ref = refs[3 + 4 * ngroups:5 + 4 * ngroups + 3]
        slabs = refs[5 + 4 * ngroups + 3:]
        _zero_at_start(dg_ref)
        for g in range(1, ngroups):
            dil = DILATIONS[g]
            part = _group_dh(dq_refs[g][...].reshape(tile, GROUP_COLS), wg_refs[g])
            for r in range(dil):
                _put_class_rows(slabs[g - 1], r, dil, part[r * (tile // dil):(r + 1) * (tile // dil)])
        dh = lax.dot_general(dgu_ref[:, 0:GU_HALF], w0_ref[...], (NT, ((), ())), preferred_element_type=F32)
        dh = dh + lax.dot_general(dgu_ref[:, GU_HALF:], w1_ref[...], (NT, ((), ())), preferred_element_type=F32)
        dh = dh + _group_dh(dq_refs[0][0], wg_refs[0])
        for slab in slabs:
            dh = dh + _from_slabs(slab)
        xhat, rr = _rms_stats(x_ref[...])
        dg_ref[...] += jnp.sum(dh * xhat, axis=0, keepdims=True)
        dx_ref[...] = dx1_ref[...] + _rms_bwd(dh, xhat, rr, g_ref[...])

    row = _rows_spec(tile, D_MODEL)
    group_ins = [(dqkvs[g].reshape(d, t // d, GROUP_COLS), _group_spec(d, tile, GROUP_COLS)) for g, d in enumerate(DILATIONS)]
    w_specs = _gu_w_specs() + [s for g in range(ngroups) for s in _group_w_specs(g)]
    return _token_call(
        "in_proj_bwd", body, t, tile,
        [(dgu, _rows_spec(tile, GU_COLS))] + group_ins + [(w_in, s) for s in w_specs]
        + [(x, row), (dx1, row), (g1, _full((1, D_MODEL)))],
        [((t, D_MODEL), F32, row)], reds=[(1, D_MODEL)], scratch=[_slabs(tile, D_MODEL)] * (ngroups - 1))


WGRAD_TK = 2048


def _wgrad_mm(name, grid, a, a_spec, b, b_spec, acc_shape, out_shape, out_spec, dst=None):
    nk = grid[-1]

    def body(*refs):
        a_ref, b_ref, o_ref, acc_ref = refs[0], refs[1], refs[-2], refs[-1]
        k = pl.program_id(len(grid) - 1)
        part = lax.dot_general(a_ref[...], b_ref[...], (TN, ((), ())), preferred_element_type=F32)

        @pl.when(k == 0)
        def _():
            acc_ref[...] = part

        @pl.when(k > 0)
        def _():
            acc_ref[...] += part

        @pl.when(k == nk - 1)
        def _():
            o_ref[...] = acc_ref[...].astype(BF16)

    filled = [] if dst is None else [dst]
    return pl.pallas_call(
        body, grid=grid, in_specs=[a_spec, b_spec] + [pl.BlockSpec(memory_space=pl.ANY)] * len(filled),
        out_specs=out_spec, out_shape=jax.ShapeDtypeStruct(out_shape, BF16), scratch_shapes=[pltpu.VMEM(acc_shape, F32)],
        input_output_aliases={2: 0} if filled else {}, compiler_params=_cparams(len(grid)), name=name)(a, b, *filled)


def _wgrad_2d(name, a, b, tm, tn):
    t, k1 = a.shape
    n = b.shape[1]
    tk = min(t, WGRAD_TK)
    return _wgrad_mm(name, (k1 // tm, n // tn, t // tk), a, pl.BlockSpec((tk, tm), lambda i, j, k: (k, i)),
                     b, pl.BlockSpec((tk, tn), lambda i, j, k: (k, j)), (tm, tn), (k1, n),
                     pl.BlockSpec((tm, tn), lambda i, j, k: (i, j)))


def _wgrad_in(hs, dgu, dqkvs):
    t = dgu.shape[0]
    tk = min(t, WGRAD_TK)
    gu_block = QKV_BLOCKS * ATTN_W // GU_HALF
    parts = [(hs[0], dgu, GU_HALF, lambda j: j + gu_block)]
    parts += [(hs[g].reshape(t, D_MODEL), dqkvs[g], ATTN_W, lambda j, g=g: _w_in_block(j, g)) for g in range(3)]
    dst = None
    for n, (a, b, tn, block_of) in enumerate(parts):
        dst = _wgrad_mm(f"wgrad_in_{n}", (1, b.shape[1] // tn, t // tk),
                        a, pl.BlockSpec((tk, D_MODEL), lambda i, j, k: (k, 0)), b, pl.BlockSpec((tk, tn), lambda i, j, k: (k, j)),
                        (D_MODEL, tn), (D_MODEL, IN_COLS),
                        pl.BlockSpec((D_MODEL, tn), lambda i, j, k, block_of=block_of: (0, block_of(j))), dst=dst)
    return dst


def _wgrad_ff_in(name, h2, da):
    t = h2.shape[0]
    tk = min(t, WGRAD_TK)
    return _wgrad_mm(name, (N_CHIPS, 1, t // tk), h2, pl.BlockSpec((tk, D_MODEL), lambda i, j, k: (k, 0)),
                     da, pl.BlockSpec((None, tk, FF_SHARD), lambda i, j, k: (i, k, 0)), (D_MODEL, FF_SHARD),
                     (N_CHIPS, D_MODEL, FF_SHARD), pl.BlockSpec((None, D_MODEL, FF_SHARD), lambda i, j, k: (i, 0, 0)))


def _wgrad_ff_down(ff, dx2b):
    t = dx2b.shape[0]
    tk = min(t, WGRAD_TK)
    return _wgrad_mm("wgrad_ffn_down", (N_CHIPS, 1, t // tk), ff, pl.BlockSpec((None, tk, FF_SHARD), lambda i, j, k: (i, k, 0)),
                     dx2b, pl.BlockSpec((tk, D_MODEL), lambda i, j, k: (k, 0)), (FF_SHARD, D_MODEL),
                     (N_CHIPS, FF_SHARD, D_MODEL), pl.BlockSpec((None, FF_SHARD, D_MODEL), lambda i, j, k: (i, 0, 0)))


def _local_step(x, pos_col, tgt, g1, ln_g, ln_b, w_s, b_s, g2, gf, first_weight, late_weights, on_grads=None):
    tables = _rope_tables(pos_col)
    bias_exp = jnp.repeat(jnp.transpose(b_s), SGU_W // SGU_GROUPS, axis=1)

    hs = _norm_fwd(x, g1)
    w_p = first_weight(hs[0])
    gu, qkvs = _in_proj(hs, w_p, tables)
    os_, ls_ = [], []
    for g, dil in enumerate(DILATIONS):
        o, lse = _attn_fwd(qkvs[g], g, dil)
        os_.append(o)
        ls_.append(lse)
    attn = _combine_fwd(os_, ls_)
    sgu = _sgu_fwd(gu, ln_g, ln_b, w_s, bias_exp)
    w_pa, w_ps, w_out, w_g, w_u, w_d = late_weights(attn)
    pa, ps, merged, x1, h2 = _merge_fwd(attn, sgu, gu, x, w_pa, w_ps, w_out, g2)
    fa, fb, ff = _ffn_fwd(h2, w_g, w_u)
    dx2, dx2b, loss, dgf = _ffn_down_loss(ff, w_d, x1, tgt, gf)

    da, db = _ffn_bwd_act(dx2b, w_d, fa, fb)
    dw_d = _wgrad_ff_down(ff, dx2b)
    dx1, dx1b, dg2 = _ffn_bwd_in(da, db, w_g, w_u, x1, dx2, g2)
    dw_g = _wgrad_ff_in("wgrad_ffn_gate", h2, da)
    dw_u = _wgrad_ff_in("wgrad_ffn_up", h2, db)

    dgu, dpa, dps, dattn, dsgu = _merge_bwd(dx1b, gu, pa, ps, w_pa, w_ps, w_out)
    dw_out = _wgrad_2d("wgrad_out", merged, dx1b, D_MODEL, D_MODEL)
    dw_pa = _wgrad_2d("wgrad_proj_attn", attn, dpa, ATTN_W, D_MODEL)
    dw_ps = _wgrad_2d("wgrad_proj_sgu", sgu, dps, SGU_W, D_MODEL)
    if on_grads is not None:
        ln_g = ln_g + on_grads(1, dict(w_proj_attn=dw_pa, w_proj_sgu=dw_ps, w_out=dw_out, w_ffn_gate=dw_g, w_ffn_up=dw_u,
                                       w_ffn_down=dw_d))[:, :SGU_W]
    dgu, dw_s, dbias, dln_g, dln_b = _sgu_bwd(dgu, gu, dsgu, ln_g, ln_b, w_s, bias_exp)
    dos, ccs = _combine_bwd(dattn, os_, ls_)
    dqkvs = [_attn_bwd(qkvs[g], dos[g], ccs[g], ls_[g], *tables[g], g, dil) for g, dil in enumerate(DILATIONS)]
    dw_p = _wgrad_in(hs, dgu, dqkvs)
    if on_grads is not None:
        g1 = g1 + on_grads(0, dict(w_in=dw_p))
    dx, dg1 = _in_proj_bwd(dgu, dqkvs, w_p, x, dx1, g1)

    db_s = jnp.transpose(dbias[:, ::SGU_W // SGU_GROUPS])
    small = dict(loss=loss, norm1_g=dg1, sgu_ln_g=dln_g, sgu_ln_b=dln_b, w_spatial=dw_s, b_spatial=db_s,
                 norm2_g=dg2, final_g=dgf)
    big = dict(w_in=dw_p, w_proj_attn=dw_pa, w_proj_sgu=dw_ps, w_out=dw_out, w_ffn_gate=dw_g, w_ffn_up=dw_u,
               w_ffn_down=dw_d)
    return dx, big, small


def _ew(name, fn, ins, out_dtypes):
    shp = ins[0].shape
    rows, cols = shp
    tr = next((cand for cand in (256, 352, 128) if rows % cand == 0 and rows > cand), rows)

    def body(*refs):
        res = fn(*[r[...] for r in refs[:len(ins)]])
        for o_ref, v in zip(refs[len(ins):], res):
            o_ref[...] = v.astype(o_ref.dtype)

    spec = pl.BlockSpec((tr, cols), lambda i: (i, 0))
    return pl.pallas_call(
        body, grid=(rows // tr,), in_specs=[spec] * len(ins), out_specs=[spec] * len(out_dtypes),
        out_shape=[jax.ShapeDtypeStruct(shp, d) for d in out_dtypes],
        compiler_params=_cparams(1), name=name)(*ins)


def _adamw_math(g, w, m, v):
    m = ADAM_B1 * m + (1.0 - ADAM_B1) * g
    v = ADAM_B2 * v + (1.0 - ADAM_B2) * (g * g)
    m_hat = m / (1.0 - ADAM_B1 ** ADAM_STEP)
    v_hat = v / (1.0 - ADAM_B2 ** ADAM_STEP)
    delta = -ADAM_LR * (m_hat / (jnp.sqrt(v_hat) + ADAM_EPS) + ADAM_WD * w)
    return delta, m, v


def _adamw(name, g, w, m, v):
    return _ew(name, lambda g_, w_, m_, v_: (g_,) + _adamw_math(g_, w_, m_, v_), [g, w, m, v], [F32] * 4)


VMEM_SPEC = pl.BlockSpec(memory_space=pltpu.VMEM)


def _for_row_chunks(rows, fn):
    ck = next(c for c in (64, 32, 16) if rows % c == 0)

    def step(i, carry):
        fn(pl.multiple_of(i * ck, ck), ck)
        return carry

    lax.fori_loop(0, rows // ck, step, 0)


def _place():
    x, y, c = lax.axis_index("x"), lax.axis_index("y"), lax.axis_index("c")
    chips = [(1 - x, y), (x, 1 - y), (1 - x, 1 - y)]
    return x, y, c, 2 * x + y, chips


def _rows(ref, start, size):
    if len(ref.shape) == 2:
        return ref.at[pl.ds(start, size), :]
    return ref.at[:, pl.ds(start, size), :]


def _comm_call(name, body, ins, out_shapes, scratch, n_remote):
    return pl.pallas_call(
        body, in_specs=[VMEM_SPEC] * len(ins), out_specs=[VMEM_SPEC] * len(out_shapes),
        out_shape=out_shapes,
        scratch_shapes=list(scratch) + [pltpu.SemaphoreType.DMA((n_remote,)), pltpu.SemaphoreType.DMA((n_remote,))],
        compiler_params=pltpu.CompilerParams(vmem_limit_bytes=VMEM_LIMIT), name=name)(*ins)


def _gather_finish(name, shard, landed):
    k_rows, n = shard.shape
    kh = k_rows // 2

    def body(shard_ref, land_ref, out_ref, send, recv):
        x, y, c, me, chips = _place()
        passed = []
        for j, chip in enumerate(chips):
            theirs = 2 * chip[0] + chip[1]
            cp = pltpu.make_async_remote_copy(
                src_ref=land_ref.at[j], dst_ref=_rows(out_ref.at[theirs], c * kh, kh), send_sem=send.at[j],
                recv_sem=recv.at[j], device_id=(x, y, 1 - c), device_id_type=MESH)
            cp.start()
            passed.append(cp)
        mine = out_ref.at[me]

        def put_own(r0, ck):
            mine[pl.ds(r0, ck), :] = shard_ref[pl.ds(r0, ck), :]

        _for_row_chunks(k_rows, put_own)
        for j, chip in enumerate(chips):
            slot = out_ref.at[2 * chip[0] + chip[1]]

            def put_half(r0, ck, j=j, slot=slot):
                slot[pl.ds(pl.multiple_of(c * kh + r0, ck), ck), :] = land_ref[j, pl.ds(r0, ck), :]

            _for_row_chunks(kh, put_half)
        for j, chip in enumerate(chips):
            other = _rows(out_ref.at[2 * chip[0] + chip[1]], (1 - c) * kh, kh)
            pltpu.make_async_remote_copy(src_ref=other, dst_ref=other, send_sem=send.at[j], recv_sem=recv.at[j],
                                         device_id=(x, y, 1 - c), device_id_type=MESH).wait_recv()
        for cp in passed:
            cp.wait_send()

    return _comm_call(name, body, [shard, landed], [jax.ShapeDtypeStruct((N_CHIPS, k_rows, n), shard.dtype)], [], 3)[0]


HBM_SPEC = pl.BlockSpec(memory_space=pltpu.HBM)
SEM_SPEC = pl.BlockSpec(memory_space=pltpu.SEMAPHORE)
DATAFLOW = pltpu.SideEffectType.DATAFLOW_SIDE_EFFECTING
TOKEN_SHAPE = (1, D_MODEL)
N_PEERS = 7


def _peers():
    x, y, c = lax.axis_index("x"), lax.axis_index("y"), lax.axis_index("c")
    flip = lambda v, f: 1 - v if f else v
    return [(flip(x, k & 4), flip(y, k & 2), flip(c, k & 1)) for k in range(1, N_PEERS + 1)]


def _piece_shape(shape):
    return (shape[-2] // 2, shape[2] if len(shape) == 3 else shape[1] // N_CHIPS)


def _device_piece(ref, chip, core):
    kh, n4 = _piece_shape(ref.shape)
    if len(ref.shape) == 3:
        return ref.at[chip, pl.ds(core * kh, kh), :]
    return ref.at[pl.ds(core * kh, kh), pl.ds(chip * n4, n4)]


def _exchange_copies(partials, lands, send, recv):
    return [pltpu.make_async_remote_copy(
        src_ref=_device_piece(partials[t], 2 * px + py, pc), dst_ref=lands[t].at[k], send_sem=send.at[t * N_PEERS + k],
        recv_sem=recv.at[t * N_PEERS + k], device_id=(px, py, pc), device_id_type=MESH)
        for t in range(len(partials)) for k, (px, py, pc) in enumerate(_peers())]


def _broadcast_copies(srcs, lands, send, recv):
    return [pltpu.make_async_remote_copy(
        src_ref=srcs[t], dst_ref=lands[t].at[k], send_sem=send.at[t * N_PEERS + k], recv_sem=recv.at[t * N_PEERS + k],
        device_id=peer, device_id_type=MESH)
        for t in range(len(srcs)) for k, peer in enumerate(_peers())]


def _gather_copies(shards, lands, send, recv):
    x, y, c, me, chips = _place()
    return [pltpu.make_async_remote_copy(
        src_ref=shards[t], dst_ref=lands[t].at[me], send_sem=send.at[t * 3 + j], recv_sem=recv.at[t * 3 + j],
        device_id=(*chip, c), device_id_type=MESH)
        for t in range(len(shards)) for j, chip in enumerate(chips)]


def _gather_half_copies(shards, lands, send, recv):
    x, y, c, me, chips = _place()
    return [pltpu.make_async_remote_copy(
        src_ref=_rows(shards[t], c * (shards[t].shape[0] // 2), shards[t].shape[0] // 2), dst_ref=lands[t].at[j],
        send_sem=send.at[t * 3 + j], recv_sem=recv.at[t * 3 + j], device_id=(*chip, c), device_id_type=MESH)
        for t in range(len(shards)) for j, chip in enumerate(chips)]


def _split_start(name, copies, per_tensor, srcs, land_shapes):
    nt = len(srcs)
    lands = [lax.empty(s, a.dtype) for s, a in zip(land_shapes, srcs)]
    nsem = nt * per_tensor

    def body(*refs):
        send, recv = refs[2 * nt], refs[2 * nt + 1]
        for cp in copies(refs[:nt], refs[nt:2 * nt], send, recv):
            cp.start()
        refs[-1][...] = jnp.zeros(TOKEN_SHAPE, F32)

    hbm = lambda a: pltpu.with_memory_space_constraint(a, pltpu.HBM)
    outs = pl.pallas_call(
        body, name=name,
        out_shape=[pltpu.SemaphoreType.DMA((nsem,)), pltpu.SemaphoreType.DMA((nsem,))]
        + [pltpu.HBM(s.shape, s.dtype) for s in srcs] + [pltpu.HBM(l.shape, l.dtype) for l in lands]
        + [jax.ShapeDtypeStruct(TOKEN_SHAPE, F32)],
        in_specs=[HBM_SPEC] * (2 * nt), out_specs=[SEM_SPEC, SEM_SPEC] + [HBM_SPEC] * (2 * nt) + [VMEM_SPEC],
        input_output_aliases={i: 2 + i for i in range(2 * nt)},
        compiler_params=pltpu.CompilerParams(has_side_effects=DATAFLOW))(*[hbm(a) for a in list(srcs) + lands])
    return outs[0], outs[1], outs[2:2 + nt], outs[2 + nt:2 + 2 * nt], outs[-1]


def _split_wait(name, copies, send, recv, srcs, lands, after):
    nt = len(srcs)

    def body(*refs):
        for cp in copies(refs[:nt], refs[nt:2 * nt], refs[2 * nt], refs[2 * nt + 1]):
            cp.wait_send()
            cp.wait_recv()

    outs = pl.pallas_call(
        body, name=name,
        out_shape=[pltpu.HBM(s.shape, s.dtype) for s in srcs] + [pltpu.HBM(l.shape, l.dtype) for l in lands],
        in_specs=[HBM_SPEC] * (2 * nt) + [SEM_SPEC, SEM_SPEC, pl.BlockSpec(memory_space=pl.ANY)],
        out_specs=[HBM_SPEC] * (2 * nt), input_output_aliases={i: i for i in range(2 * nt)},
        compiler_params=pltpu.CompilerParams(has_side_effects=DATAFLOW))(*srcs, *lands, send, recv, after)
    return outs[:nt], outs[nt:]


def _device_sum(name, partials, lands):
    nt = len(partials)

    def body(*refs):
        ins, slots, outs, owns = refs[:nt], refs[nt:2 * nt], refs[2 * nt:3 * nt], refs[3 * nt:4 * nt]
        send, recv, loc = refs[4 * nt:]
        x, y, c, me, chips = _place()
        sibling = (x, y, 1 - c)
        loads = [pltpu.make_async_copy(_device_piece(ins[t], me, c), owns[t], loc.at[t]) for t in range(nt)]
        for cp in loads:
            cp.start()
        handed = []
        for t in range(nt):
            kh = owns[t].shape[0]
            loads[t].wait()

            def add(r0, ck, own=owns[t], slot=slots[t], dst=outs[t], kh=kh):
                rows = pl.ds(r0, ck)
                acc = own[rows, :].astype(F32)
                for k in range(N_PEERS):
                    acc = acc + slot[k, rows, :].astype(F32)
                dst[pl.ds(pl.multiple_of(c * kh + r0, ck), ck), :] = acc

            _for_row_chunks(kh, add)
            rc = pltpu.make_async_remote_copy(
                src_ref=_rows(outs[t], c * kh, kh), dst_ref=_rows(outs[t], c * kh, kh), send_sem=send.at[t],
                recv_sem=recv.at[t], device_id=sibling, device_id_type=MESH)
            rc.start()
            handed.append(rc)
        for t in range(nt):
            kh = owns[t].shape[0]
            other = _rows(outs[t], (1 - c) * kh, kh)
            pltpu.make_async_remote_copy(
                src_ref=other, dst_ref=other, send_sem=send.at[t], recv_sem=recv.at[t],
                device_id=sibling, device_id_type=MESH).wait_recv()
        for rc in handed:
            rc.wait_send()

    pieces = [_piece_shape(p.shape) for p in partials]
    return pl.pallas_call(
        body, in_specs=[pl.BlockSpec(memory_space=pl.ANY)] * nt + [VMEM_SPEC] * nt, out_specs=[VMEM_SPEC] * nt,
        out_shape=[jax.ShapeDtypeStruct((2 * kh, n4), F32) for kh, n4 in pieces],
        scratch_shapes=[pltpu.VMEM(p, BF16) for p in pieces]
        + [pltpu.SemaphoreType.DMA((nt,)), pltpu.SemaphoreType.DMA((nt,)), pltpu.SemaphoreType.DMA((nt,))],
        compiler_params=pltpu.CompilerParams(vmem_limit_bytes=VMEM_LIMIT), name=name)(*partials, *lands)


VEC_SHAPE = (8, D_MODEL + LANES)
VEC_SLOTS = dict(norm1_g=(slice(0, 1), slice(0, D_MODEL)), norm2_g=(slice(1, 2), slice(0, D_MODEL)),
                 final_g=(slice(2, 3), slice(0, D_MODEL)), sgu_ln_g=(slice(3, 4), slice(0, SGU_W)),
                 sgu_ln_b=(slice(3, 4), slice(SGU_W, 2 * SGU_W)), b_spatial=(slice(0, 8), slice(D_MODEL, D_MODEL + LANES)),
                 loss=(slice(4, 5), slice(0, LANES)))
VEC_PARAMS = ("norm1_g", "norm2_g", "final_g", "sgu_ln_g", "sgu_ln_b", "b_spatial")
SMALL_PARAMS = VEC_PARAMS + ("w_spatial",)
W_SPATIAL_2D = (SGU_GROUPS * SGU_CHUNK, SGU_CHUNK)


SMALL_GRADS = VEC_PARAMS + ("loss", "w_spatial")


def _small_shape(name):
    if name == "w_spatial":
        return W_SPATIAL_2D
    rows, cols = VEC_SLOTS[name]
    return (rows.stop - rows.start, cols.stop - cols.start)


def _pack_small(dst, parts):
    dst[...] = jnp.zeros(VEC_SHAPE, F32)
    for n, ref in parts.items():
        if n in VEC_SLOTS:
            dst[VEC_SLOTS[n]] = ref[...]


def _small_start(partials):
    names = VEC_PARAMS + ("loss",)

    def body(*refs):
        _pack_small(refs[-1], dict(zip(names, refs[:-1])))

    vec = pl.pallas_call(
        body, in_specs=[VMEM_SPEC] * len(names), out_specs=VMEM_SPEC, out_shape=jax.ShapeDtypeStruct(VEC_SHAPE, F32),
        name="small_params_pack")(*[partials[n].reshape(_small_shape(n)) for n in names])
    srcs = [vec, partials["w_spatial"].reshape(W_SPATIAL_2D)]
    return _split_start("small_params_start", _broadcast_copies, N_PEERS, srcs, [(N_PEERS,) + s.shape for s in srcs])[:4]


def _small_finish(started, after, w, m, v):
    own, landed = _split_wait("small_params_wait", _broadcast_copies, *started, after)
    ng, npar = len(SMALL_GRADS), len(SMALL_PARAMS)

    def update_body(*refs):
        vec_own, ws_own, vec_slots, ws_slots = refs[:4]
        w_in, m_in, v_in = (dict(zip(SMALL_PARAMS, refs[4 + k * npar:4 + (k + 1) * npar])) for k in range(3))
        o0 = 4 + 3 * npar
        g_out = dict(zip(SMALL_GRADS, refs[o0:o0 + ng]))
        d_out, m_out, v_out = (dict(zip(SMALL_PARAMS, refs[o0 + ng + k * npar:o0 + ng + (k + 1) * npar])) for k in range(3))
        vg, vw, vm, vv = refs[o0 + ng + 3 * npar:]
        me = 4 * lax.axis_index("x") + 2 * lax.axis_index("y") + lax.axis_index("c")

        def device_sum(mine, slots, read):
            acc = None
            for i in range(N_PEERS + 1):
                k = me ^ i
                part = jnp.where(k == 0, read(mine), read(slots.at[jnp.maximum(k, 1) - 1]))
                acc = part if acc is None else acc + part
            return acc

        vg[...] = device_sum(vec_own, vec_slots, lambda ref: ref[...])
        _pack_small(vw, w_in)
        _pack_small(vm, m_in)
        _pack_small(vv, v_in)
        d_vec, m_vec, v_vec = _adamw_math(vg[...], vw[...], vm[...], vv[...])
        vw[...] = d_vec
        vm[...] = m_vec
        vv[...] = v_vec
        for n in VEC_PARAMS + ("loss",):
            g_out[n][...] = vg[VEC_SLOTS[n]]
        for n in VEC_PARAMS:
            d_out[n][...] = vw[VEC_SLOTS[n]]
            m_out[n][...] = vm[VEC_SLOTS[n]]
            v_out[n][...] = vv[VEC_SLOTS[n]]

        def spatial(r0, ck):
            rows = pl.ds(r0, ck)
            g = device_sum(ws_own, ws_slots, lambda ref: ref[rows, :])
            d_, m_, v_ = _adamw_math(g, w_in["w_spatial"][rows, :], m_in["w_spatial"][rows, :], v_in["w_spatial"][rows, :])
            g_out["w_spatial"][rows, :] = g
            d_out["w_spatial"][rows, :] = d_
            m_out["w_spatial"][rows, :] = m_
            v_out["w_spatial"][rows, :] = v_

        _for_row_chunks(W_SPATIAL_2D[0], spatial)

    ins = list(own) + list(landed)
    for src in (w, m, v):
        ins += [src[n].reshape(_small_shape(n)) for n in SMALL_PARAMS]
    out_shapes = [jax.ShapeDtypeStruct(_small_shape(n), F32) for n in SMALL_GRADS + SMALL_PARAMS * 3]
    outs = pl.pallas_call(
        update_body, in_specs=[VMEM_SPEC] * len(ins), out_specs=[VMEM_SPEC] * len(out_shapes), out_shape=out_shapes,
        scratch_shapes=[pltpu.VMEM(VEC_SHAPE, F32)] * 4, name="small_params_update")(*ins)
    grads = dict(zip(SMALL_GRADS, outs[:ng]))
    rest = [dict(zip(SMALL_PARAMS, outs[ng + k * npar:ng + (k + 1) * npar])) for k in range(3)]
    return grads, rest[0], rest[1], rest[2]


BIG = ("w_in", "w_proj_attn", "w_proj_sgu", "w_out", "w_ffn_gate", "w_ffn_up", "w_ffn_down")
COMM_GROUPS = (("w_in",), ("w_proj_attn", "w_proj_sgu", "w_out", "w_ffn_gate", "w_ffn_up", "w_ffn_down"))
WEIGHTS = ("norm1_g", "w_in", "sgu_ln_g", "sgu_ln_b", "w_spatial", "b_spatial", "w_proj_attn", "w_proj_sgu", "w_out",
           "norm2_g", "w_ffn_gate", "w_ffn_up", "w_ffn_down", "final_g")


def _cols_from_chips(g):
    return jnp.transpose(g, (1, 0, 2)).reshape(g.shape[1], N_CHIPS * g.shape[2])


def kernel(x, positions, norm1_g, w_in, sgu_ln_g, sgu_ln_b, w_spatial, b_spatial, w_proj_attn, w_proj_sgu, w_out, norm2_g, w_ffn_gate, w_ffn_up, w_ffn_down, final_g, loss_target, m_norm1_g, m_w_in, m_sgu_ln_g, m_sgu_ln_b, m_w_spatial, m_b_spatial, m_w_proj_attn, m_w_proj_sgu, m_w_out, m_norm2_g, m_w_ffn_gate, m_w_ffn_up, m_w_ffn_down, m_final_g, v_norm1_g, v_w_in, v_sgu_ln_g, v_sgu_ln_b, v_w_spatial, v_b_spatial, v_w_proj_attn, v_w_proj_sgu, v_w_out, v_norm2_g, v_w_ffn_gate, v_w_ffn_up, v_w_ffn_down, v_final_g):
    w = dict(norm1_g=norm1_g, w_in=w_in, sgu_ln_g=sgu_ln_g, sgu_ln_b=sgu_ln_b, w_spatial=w_spatial, b_spatial=b_spatial,
             w_proj_attn=w_proj_attn, w_proj_sgu=w_proj_sgu, w_out=w_out, norm2_g=norm2_g, w_ffn_gate=w_ffn_gate,
             w_ffn_up=w_ffn_up, w_ffn_down=w_ffn_down, final_g=final_g)
    m = dict(norm1_g=m_norm1_g, w_in=m_w_in, sgu_ln_g=m_sgu_ln_g, sgu_ln_b=m_sgu_ln_b, w_spatial=m_w_spatial,
             b_spatial=m_b_spatial, w_proj_attn=m_w_proj_attn, w_proj_sgu=m_w_proj_sgu, w_out=m_w_out, norm2_g=m_norm2_g,
             w_ffn_gate=m_w_ffn_gate, w_ffn_up=m_w_ffn_up, w_ffn_down=m_w_ffn_down, final_g=m_final_g)
    v = dict(norm1_g=v_norm1_g, w_in=v_w_in, sgu_ln_g=v_sgu_ln_g, sgu_ln_b=v_sgu_ln_b, w_spatial=v_w_spatial,
             b_spatial=v_b_spatial, w_proj_attn=v_w_proj_attn, w_proj_sgu=v_w_proj_sgu, w_out=v_w_out, norm2_g=v_norm2_g,
             w_ffn_gate=v_w_ffn_gate, w_ffn_up=v_w_ffn_up, w_ffn_down=v_w_ffn_down, final_g=v_final_g)
    t = x.shape[1]

    shards = {n: _ew(f"cast_{n}", lambda a: (a,), [w[n][0]], [BF16])[0] for n in BIG}
    late = COMM_GROUPS[1]
    k_in, n_in = shards["w_in"].shape
    *first, token = _split_start("gather_start_0", _gather_half_copies, 3, [shards["w_in"]], [(3, k_in // 2, n_in)])
    pending = {}

    def first_weight(after):
        srcs, filled = _split_wait("gather_wait_0", _gather_half_copies, *first, after)
        gath_in, late_shards = lax.optimization_barrier(
            (_gather_finish("gather_finish_0", srcs[0], filled[0]), [shards[n] for n in late]))
        *pending["late"], _ = _split_start(
            "gather_start_1", _gather_copies, 3, late_shards, [(N_CHIPS,) + s.shape for s in late_shards])
        return _cols_from_chips(gath_in)

    def late_weights(after):
        srcs, filled = _split_wait("gather_wait_1", _gather_copies, *pending["late"], after)
        me = 2 * lax.axis_index("x") + lax.axis_index("y")
        gath = {n: lax.dynamic_update_slice(f, s[None], (me, 0, 0)) for n, f, s in zip(late, filled, srcs)}
        return (_cols_from_chips(gath["w_proj_attn"]), _cols_from_chips(gath["w_proj_sgu"]),
                gath["w_out"].reshape(D_MODEL, D_MODEL), gath["w_ffn_gate"], gath["w_ffn_up"], gath["w_ffn_down"])

    exchanges = {}

    def on_grads(i, partials):
        if "w_out" in partials:
            partials["w_out"] = partials["w_out"].reshape(N_CHIPS, D_MODEL // N_CHIPS, D_MODEL)
        parts = [partials[n] for n in COMM_GROUPS[i]]
        *exchanges[i], started = _split_start(
            f"rs_exchange_start_{i}", _exchange_copies, N_PEERS, parts, [(N_PEERS,) + _piece_shape(p.shape) for p in parts])
        return started

    dx, _, small = _local_step(
        x[0], positions.reshape(t, 1), loss_target[0], norm1_g + token, sgu_ln_g, sgu_ln_b, w_spatial[0], b_spatial[0],
        norm2_g, final_g.reshape(1, D_MODEL), first_weight, late_weights, on_grads=on_grads)
    small_started = _small_start(small)

    grads = {}
    for i in (1, 0):
        parts, filled = _split_wait(f"rs_exchange_wait_{i}", _exchange_copies, *exchanges[i], dx)
        grads.update(zip(COMM_GROUPS[i], _device_sum(f"rs_device_sum_{i}", parts, filled)))

    delta, new_m, new_v = {}, {}, {}
    for n in BIG:
        shp = w[n].shape
        flip = jnp.transpose if shp[-1] % LANES else (lambda a: a)
        outs = _adamw(f"adamw_{n}", flip(grads[n]), flip(w[n][0]), flip(m[n][0]), flip(v[n][0]))
        grads[n], delta[n], new_m[n], new_v[n] = (flip(a).reshape(shp) for a in outs)

    g_s, d_s, m_s, v_s = _small_finish(small_started, outs[-1], w, m, v)
    loss = g_s["loss"][0, 0]
    for n in SMALL_PARAMS:
        shp = w[n].shape
        grads[n], delta[n], new_m[n], new_v[n] = (a[n].reshape(shp) for a in (g_s, d_s, m_s, v_s))

    return (loss, dx.reshape(x.shape), *[grads[n] for n in WEIGHTS], *[delta[n] for n in WEIGHTS],
            *[new_m[n] for n in WEIGHTS], *[new_v[n] for n in WEIGHTS])
```

```python
import functools

import numpy as np
import jax
import jax.numpy as jnp
from jax import lax
from jax.experimental import pallas as pl
from jax.experimental.pallas import tpu as pltpu

F32, BF16 = jnp.float32, jnp.bfloat16
MESH = pl.DeviceIdType.MESH

D_MODEL = 1024
HEAD_DIM = 64
ATTN_W = 512
DILATIONS = (1, 4, 16)
BLK = 128
ATTN_BLOCKS_PER_STEP = 4
ROPE_DIM = 16
ROPE_THETA = 500000.0
SGU_W = 512
SGU_CHUNK = 128
SGU_GROUPS = 8
D_FF = 2816
N_CHIPS = 4
FF_SHARD = D_FF // N_CHIPS
IN_COLS = 7680
EPS = 1e-6
NEG = -1e30
LANES = 128
VMEM_LIMIT = 52 * 1024 * 1024

ADAM_LR, ADAM_B1, ADAM_B2, ADAM_EPS, ADAM_WD, ADAM_STEP = 0.001, 0.9, 0.999, 1e-08, 0.01, 10

QKV_BLOCKS = 9


def _w_in_block(part, g):
    return part * len(DILATIONS) + g


def _cparams(ngrid):
    return pltpu.CompilerParams(dimension_semantics=("arbitrary",) * ngrid, vmem_limit_bytes=VMEM_LIMIT)


def _full(shape):
    return pl.BlockSpec(shape, lambda *_: (0,) * len(shape))


def _resident(shape):
    return pl.BlockSpec(shape, lambda *_: (0,) * len(shape), pipeline_mode=pl.Buffered(1))


NT = ((1,), (1,))
TN = ((0,), (0,))


def _rope(v, cos_t, sin_t):
    half = ROPE_DIM // 2
    first = (lax.broadcasted_iota(jnp.int32, cos_t.shape, 1) % HEAD_DIM) < half
    outs = []
    for cs in range(v.shape[1] // LANES):
        x = v[:, cs * LANES:(cs + 1) * LANES]
        partner = jnp.where(first, pltpu.roll(x, LANES - half, axis=1), pltpu.roll(x, half, axis=1))
        outs.append(x * cos_t + partner * sin_t)
    return outs[0] if len(outs) == 1 else jnp.concatenate(outs, axis=1)


def _spread_heads(v2, upper):
    other = pltpu.roll(v2, HEAD_DIM, axis=1)
    h0 = jnp.where(upper, other, v2)
    h1 = jnp.where(upper, v2, other)
    return jnp.concatenate([jnp.concatenate([h0, h0], axis=1), jnp.concatenate([h1, h1], axis=1)], axis=0)


def _sigmoid(v):
    return 0.5 * jnp.tanh(0.5 * v) + 0.5


def _rms_stats(v):
    r = lax.rsqrt(jnp.mean(v * v, axis=-1, keepdims=True) + EPS)
    return v * r, r


def _rms_bwd(dy, xhat, r, g):
    dxh = dy * g
    return r * (dxh - xhat * jnp.mean(dxh * xhat, axis=-1, keepdims=True))


def _head_sum_matrix():
    idx = np.arange(ATTN_W) // HEAD_DIM
    return jnp.asarray((idx[:, None] == idx[None, :]).astype(np.float32), dtype=BF16)


def _group_sum(v, e):
    hi = v.astype(BF16)
    lo = (v - hi.astype(F32)).astype(BF16)
    return jnp.dot(hi, e, preferred_element_type=F32) + jnp.dot(lo, e, preferred_element_type=F32)


TILE = 512


def _to_slabs(slab_ref, v):
    for cs in range(slab_ref.shape[0]):
        slab_ref[cs] = v[:, cs * LANES:(cs + 1) * LANES]


def _from_slabs(slab_ref):
    return jnp.concatenate([slab_ref[cs] for cs in range(slab_ref.shape[0])], axis=1)


def _class_rows(slab_ref, r, dil):
    n = slab_ref.shape[1] // dil
    return jnp.concatenate([slab_ref.at[cs][pl.ds(r, n, stride=dil), :] for cs in range(slab_ref.shape[0])], axis=1)


def _put_class_rows(slab_ref, r, dil, v):
    n = slab_ref.shape[1] // dil
    for cs in range(slab_ref.shape[0]):
        slab_ref.at[cs][pl.ds(r, n, stride=dil), :] = v[:, cs * LANES:(cs + 1) * LANES]


def _natural_from_group(slab_ref, grp_ref):
    dil = grp_ref.shape[0]
    for r in range(dil):
        _put_class_rows(slab_ref, r, dil, grp_ref[r].astype(F32))
    return _from_slabs(slab_ref)


def _group_from_natural(slab_ref, grp_ref, v):
    dil = grp_ref.shape[0]
    _to_slabs(slab_ref, v)
    for r in range(dil):
        grp_ref[r] = _class_rows(slab_ref, r, dil).astype(grp_ref.dtype)


def _group_spec(dil, tile, width):
    return pl.BlockSpec((dil, tile // dil, width), lambda i, *_: (0, i, 0))


def _slabs(tile, width):
    return pltpu.VMEM((width // LANES, tile, LANES), F32)


def _rope_consts():
    lane = np.arange(LANES) % HEAD_DIM
    fi = lane % (ROPE_DIM // 2)
    invf = np.where(lane < ROPE_DIM, ROPE_THETA ** (-(2.0 * fi) / ROPE_DIM), 0.0)
    sgn = np.where(lane < ROPE_DIM // 2, -1.0, np.where(lane < ROPE_DIM, 1.0, 0.0))
    return (jnp.asarray(invf.astype(np.float32)).reshape(1, LANES), jnp.asarray(sgn.astype(np.float32)).reshape(1, LANES))


def _rope_tables(pos_col):
    t = pos_col.shape[0]
    tile = min(t, TILE)
    invf, sgn = _rope_consts()

    def body(p_ref, f_ref, s_ref, c0, s0, c1, s1, c2, s2, slab_c, slab_s):
        ang = p_ref[...].astype(F32) * f_ref[...]
        cos, sin = jnp.cos(ang), jnp.sin(ang) * s_ref[...]
        c0[...] = cos
        s0[...] = sin
        _group_from_natural(slab_c, c1, cos)
        _group_from_natural(slab_s, s1, sin)
        for r in range(DILATIONS[2]):
            c2[r] = _class_rows(slab_c, r, DILATIONS[2])
            s2[r] = _class_rows(slab_s, r, DILATIONS[2])

    nat = pl.BlockSpec((tile, LANES), lambda i: (i, 0))
    specs, shapes = [nat, nat], [(t, LANES)] * 2
    for d in DILATIONS[1:]:
        specs += [_group_spec(d, tile, LANES)] * 2
        shapes += [(d, t // d, LANES)] * 2
    outs = pl.pallas_call(
        body, grid=(t // tile,),
        in_specs=[pl.BlockSpec((tile, 1), lambda i: (i, 0)), _full((1, LANES)), _full((1, LANES))],
        out_specs=specs, out_shape=[jax.ShapeDtypeStruct(s, F32) for s in shapes],
        scratch_shapes=[_slabs(tile, LANES)] * 2,
        compiler_params=_cparams(1), name="rope_tables")(pos_col, invf, sgn)
    return [(outs[2 * g].reshape(t, LANES), outs[2 * g + 1].reshape(t, LANES)) for g in range(len(DILATIONS))]


def _norm_fwd(x, g):
    t = x.shape[0]
    tile = min(t, TILE)

    def body(x_ref, g_ref, h0_ref, h1_ref, h2_ref, slab):
        xhat, _ = _rms_stats(x_ref[...])
        hn = xhat * g_ref[...]
        h0_ref[...] = hn.astype(BF16)
        _group_from_natural(slab, h1_ref, hn)
        for r in range(DILATIONS[2]):
            h2_ref[r] = _class_rows(slab, r, DILATIONS[2]).astype(BF16)

    nat = pl.BlockSpec((tile, D_MODEL), lambda i: (i, 0))
    return pl.pallas_call(
        body, grid=(t // tile,),
        in_specs=[nat, _full((1, D_MODEL))],
        out_specs=[nat] + [_group_spec(d, tile, D_MODEL) for d in DILATIONS[1:]],
        out_shape=[jax.ShapeDtypeStruct((t, D_MODEL), BF16)]
        + [jax.ShapeDtypeStruct((d, t // d, D_MODEL), BF16) for d in DILATIONS[1:]],
        scratch_shapes=[_slabs(tile, D_MODEL)],
        compiler_params=_cparams(1), name="norm1_fwd")(x, g)


GU_COLS = 3072
GROUP_COLS = 1536
GU_HALF = GU_COLS // 2


def _w_in_spec(width, block):
    return pl.BlockSpec((D_MODEL, width), lambda i: (0, block), pipeline_mode=pl.Buffered(1))


def _gu_w_specs():
    first = QKV_BLOCKS * ATTN_W // GU_HALF
    return [_w_in_spec(GU_HALF, first), _w_in_spec(GU_HALF, first + 1)]


def _group_w_specs(g):
    return [_w_in_spec(ATTN_W, _w_in_block(part, g)) for part in range(3)]


def _in_proj(hs, w_in, tables):
    t = hs[0].shape[0]
    tm = min(t, 1024)

    def body_gu(h_ref, w0_ref, w1_ref, o_ref):
        h = h_ref[...]
        o_ref[:, 0:GU_HALF] = jnp.dot(h, w0_ref[...], preferred_element_type=F32).astype(BF16)
        o_ref[:, GU_HALF:] = jnp.dot(h, w1_ref[...], preferred_element_type=F32).astype(BF16)

    gu = _token_call("in_proj_gates_uv", body_gu, t, tm,
                     [(hs[0], _rows_spec(tm, D_MODEL))] + [(w_in, s) for s in _gu_w_specs()],
                     [((t, GU_COLS), BF16, _rows_spec(tm, GU_COLS))])[0]

    qkvs = []
    for g in range(len(DILATIONS)):

        def body_qkv(h_ref, wq_ref, wk_ref, wv_ref, cos_ref, sin_ref, o_ref):
            h = h_ref[...]
            cos_w, sin_w = cos_ref[...], sin_ref[...]
            q = jnp.dot(h, wq_ref[...], preferred_element_type=F32)
            o_ref[:, 0:ATTN_W] = (_rope(q, cos_w, sin_w) * HEAD_DIM ** -0.5).astype(BF16)
            k = jnp.dot(h, wk_ref[...], preferred_element_type=F32)
            o_ref[:, ATTN_W:2 * ATTN_W] = _rope(k, cos_w, sin_w).astype(BF16)
            o_ref[:, 2 * ATTN_W:] = jnp.dot(h, wv_ref[...], preferred_element_type=F32).astype(BF16)

        cos_t, sin_t = tables[g]
        qkvs.append(_token_call(
            f"in_proj_qkv_g{g}", body_qkv, t, tm,
            [(hs[g].reshape(t, D_MODEL), _rows_spec(tm, D_MODEL))] + [(w_in, s) for s in _group_w_specs(g)]
            + [(cos_t, _rows_spec(tm, LANES)), (sin_t, _rows_spec(tm, LANES))],
            [((t, GROUP_COLS), BF16, _rows_spec(tm, GROUP_COLS))])[0])
    return gu, qkvs


def _attn_masks(n):
    row = lax.broadcasted_iota(jnp.int32, (2 * BLK, 2 * BLK), 0) % BLK
    col = lax.broadcasted_iota(jnp.int32, (2 * BLK, 2 * BLK), 1)
    diff = BLK + row - col
    valid = (diff >= 0) & (diff <= BLK) & ((col >= BLK) | (n > 0))
    upper = lax.broadcasted_iota(jnp.int32, (BLK, LANES), 1) >= HEAD_DIM
    return valid, upper


def _stack_heads(v2, upper):
    zero = jnp.zeros_like(v2)
    return jnp.concatenate([jnp.where(upper, zero, v2), jnp.where(upper, v2, zero)], axis=0)


def _unstack_heads(v, upper):
    return jnp.where(upper, v[BLK:], v[:BLK])


def _attn_fwd(qkv, g, dil):
    t = qkv.shape[0]
    length = t // dil
    nb = length // BLK
    per_step = min(nb, ATTN_BLOCKS_PER_STEP)
    view = qkv.reshape(dil, length, GROUP_COLS)

    def body(q_ref, kc_ref, kp_ref, vc_ref, vp_ref, o_ref, l_ref, kwin, vwin):
        n = pl.program_id(1)
        kwin[0:BLK] = kp_ref[...]
        kwin[BLK:] = kc_ref[...]
        vwin[0:BLK] = vp_ref[...]
        vwin[BLK:] = vc_ref[...]

        def block(b, carry):
            valid, upper = _attn_masks(n * per_step + b)
            rows = pl.ds(pl.multiple_of(b * BLK, BLK), BLK)
            window = pl.ds(pl.multiple_of(b * BLK, BLK), 2 * BLK)
            slabs = [slice(p * LANES, (p + 1) * LANES) for p in range(ATTN_W // LANES)]
            ss = [lax.dot_general(_stack_heads(q_ref[rows, sl], upper), kwin[window, sl], (NT, ((), ())),
                                  preferred_element_type=F32) for sl in slabs]
            soft = []
            for s in ss:
                s = jnp.where(valid, s, NEG)
                m = jnp.max(s, axis=1, keepdims=True)
                pe = jnp.exp(s - m)
                soft.append((m, pe, jnp.sum(pe, axis=1, keepdims=True)))
            for sl, (m, pe, den) in zip(slabs, soft):
                o = jnp.dot(pe.astype(BF16), vwin[window, sl], preferred_element_type=F32) / den
                lse = jnp.broadcast_to(m + jnp.log(den), (2 * BLK, LANES))
                o_ref[rows, sl] = _unstack_heads(o, upper).astype(BF16)
                l_ref[rows, sl] = _unstack_heads(lse, upper)
            return carry

        lax.fori_loop(0, per_step, block, 0)

    rows = per_step * BLK
    cur = lambda part: pl.BlockSpec((None, rows, ATTN_W), lambda r, n: (r, n, part))
    prev = lambda part: pl.BlockSpec((None, BLK, ATTN_W), lambda r, n: (r, jnp.maximum(n * per_step - 1, 0), part))
    out_spec = pl.BlockSpec((None, rows, ATTN_W), lambda r, n: (r, n, 0))
    return pl.pallas_call(
        body, grid=(dil, nb // per_step),
        in_specs=[cur(0), cur(1), prev(1), cur(2), prev(2)],
        out_specs=[out_spec, out_spec],
        out_shape=[jax.ShapeDtypeStruct((dil, length, ATTN_W), BF16), jax.ShapeDtypeStruct((dil, length, ATTN_W), F32)],
        scratch_shapes=[pltpu.VMEM((rows + BLK, ATTN_W), BF16)] * 2,
        compiler_params=_cparams(2), name=f"attn_fwd_g{g}")(view, view, view, view, view)


def _alphas(l0, l1, l2):
    m = jnp.maximum(jnp.maximum(l0, l1), l2)
    e0, e1, e2 = jnp.exp(l0 - m), jnp.exp(l1 - m), jnp.exp(l2 - m)
    inv = 1.0 / (e0 + e1 + e2)
    return e0 * inv, e1 * inv, e2 * inv


def _natural_group_values(o_refs, l_refs, slabs):
    os_ = [o_refs[0][0].astype(F32)] + [_natural_from_group(slabs[2 * g - 2], o_refs[g]) for g in (1, 2)]
    ls_ = [l_refs[0][0]] + [_natural_from_group(slabs[2 * g - 1], l_refs[g]) for g in (1, 2)]
    return os_, ls_


def _combine_fwd(os_, ls_):
    t = os_[0].shape[1]
    tile = min(t, TILE)

    def body(o0, o1, o2, l0, l1, l2, a_ref, *slabs):
        ov, lv = _natural_group_values((o0, o1, o2), (l0, l1, l2), slabs)
        a0, a1, a2 = _alphas(*lv)
        a_ref[...] = (a0 * ov[0] + a1 * ov[1] + a2 * ov[2]).astype(BF16)

    specs = [_group_spec(d, tile, ATTN_W) for d in DILATIONS]
    return pl.pallas_call(
        body, grid=(t // tile,), in_specs=specs * 2, out_specs=pl.BlockSpec((tile, ATTN_W), lambda i: (i, 0)),
        out_shape=jax.ShapeDtypeStruct((t, ATTN_W), BF16),
        scratch_shapes=[_slabs(tile, ATTN_W)] * 4,
        compiler_params=_cparams(1), name="combine_fwd")(*os_, *ls_)


def _combine_bwd(dattn, os_, ls_):
    t = dattn.shape[0]
    tile = min(t, TILE)
    e = _head_sum_matrix()

    def body(d_ref, o0, o1, o2, l0, l1, l2, e_ref, do0, do1, do2, c0, c1, c2, *slabs):
        ov, lv = _natural_group_values((o0, o1, o2), (l0, l1, l2), slabs)
        alphas = _alphas(*lv)
        d = d_ref[...]
        attn = alphas[0] * ov[0] + alphas[1] * ov[1] + alphas[2] * ov[2]
        s = _group_sum(d * attn, e_ref[...])
        do0[0] = (alphas[0] * d).astype(BF16)
        c0[0] = -alphas[0] * s
        for g, do_ref, c_ref in ((1, do1, c1), (2, do2, c2)):
            _group_from_natural(slabs[2 * g - 2], do_ref, alphas[g] * d)
            _group_from_natural(slabs[2 * g - 1], c_ref, -alphas[g] * s)

    specs = [_group_spec(d, tile, ATTN_W) for d in DILATIONS]
    shapes = [(d, t // d, ATTN_W) for d in DILATIONS]
    outs = pl.pallas_call(
        body, grid=(t // tile,),
        in_specs=[pl.BlockSpec((tile, ATTN_W), lambda i: (i, 0))] + specs * 2 + [_full((ATTN_W, ATTN_W))],
        out_specs=specs * 2,
        out_shape=[jax.ShapeDtypeStruct(s, BF16) for s in shapes] + [jax.ShapeDtypeStruct(s, F32) for s in shapes],
        scratch_shapes=[_slabs(tile, ATTN_W)] * 4,
        compiler_params=_cparams(1), name="combine_bwd")(dattn, *os_, *ls_, e)
    return outs[:3], outs[3:]


def _attn_bwd(qkv, do, cc, lse, cos_t, sin_t, g, dil):
    t = qkv.shape[0]
    length = t // dil
    nb = length // BLK
    per_step = min(nb, ATTN_BLOCKS_PER_STEP)
    nsteps = nb // per_step
    rows_per_step = per_step * BLK
    qkv_v = qkv.reshape(dil, length, GROUP_COLS)
    cos_v, sin_v = (a.reshape(dil, length, LANES) for a in (cos_t, sin_t))
    scale = HEAD_DIM ** -0.5
    dq_cols, dk_cols, dv_cols = (slice(i * ATTN_W, (i + 1) * ATTN_W) for i in range(3))

    def body(q_ref, kc_ref, kp_ref, vc_ref, vp_ref, do_ref, c_ref, l_ref, cosc, sinc, cosp, sinp,
             out_ref, acc, kwin, vwin, cwin, swin):
        n = pl.program_id(1)

        def one_block(b):
            valid, upper = _attn_masks(n * per_step + b)
            start = b * BLK if isinstance(b, int) else pl.multiple_of(b * BLK, BLK)
            rows, before, window = pl.ds(start, BLK), pl.ds(start, BLK), pl.ds(start, 2 * BLK)
            own = pl.ds(start + BLK, BLK)
            dq_parts, dkp_parts, dkc_parts, dvp_parts, dvc_parts = [], [], [], [], []
            npairs = ATTN_W // LANES
            slabs = [slice(p * LANES, (p + 1) * LANES) for p in range(npairs)]
            qss = [_stack_heads(q_ref[rows, sl], upper) for sl in slabs]
            doss = [_stack_heads(do_ref[rows, sl], upper) for sl in slabs]
            ss = [lax.dot_general(qss[p], kwin[window, slabs[p]], (NT, ((), ())), preferred_element_type=F32) for p in range(npairs)]
            dpvs = [lax.dot_general(doss[p], vwin[window, slabs[p]], (NT, ((), ())), preferred_element_type=F32)
                    for p in range(npairs)]
            pes = [jnp.exp(jnp.where(valid, ss[p], NEG) - _spread_heads(l_ref[rows, slabs[p]], upper)) for p in range(npairs)]
            dss = [(pes[p] * (dpvs[p] + _spread_heads(c_ref[rows, slabs[p]], upper))).astype(BF16) for p in range(npairs)]
            for p in range(npairs):
                qs, dos, ds = qss[p], doss[p], dss[p]
                dq2 = _unstack_heads(jnp.dot(ds, kwin[window, slabs[p]], preferred_element_type=F32), upper)
                dk2 = lax.dot_general(ds, qs, (TN, ((), ())), preferred_element_type=F32)
                dv2 = lax.dot_general(pes[p].astype(BF16), dos, (TN, ((), ())), preferred_element_type=F32)
                dq_parts.append(dq2)
                dkp_parts.append(dk2[:BLK])
                dkc_parts.append(dk2[BLK:])
                dvp_parts.append(dv2[:BLK])
                dvc_parts.append(dv2[BLK:])
            dq = _rope(jnp.concatenate(dq_parts, axis=1) * scale, cwin[own, :], swin[own, :])
            dkc = _rope(jnp.concatenate(dkc_parts, axis=1), cwin[own, :], swin[own, :])
            dkp = _rope(jnp.concatenate(dkp_parts, axis=1), cwin[before, :], swin[before, :])
            return dq, dkp, dkc, jnp.concatenate(dvp_parts, axis=1), jnp.concatenate(dvc_parts, axis=1)

        @pl.when(n < nsteps)
        def _():
            kwin[0:BLK] = kp_ref[...]
            kwin[BLK:] = kc_ref[...]
            vwin[0:BLK] = vp_ref[...]
            vwin[BLK:] = vc_ref[...]
            cwin[0:BLK] = cosp[...]
            cwin[BLK:] = cosc[...]
            swin[0:BLK] = -sinp[...]
            swin[BLK:] = -sinc[...]
            dq, dkp, dkc, dvp, dvc = one_block(0)
            last = slice(rows_per_step - BLK, rows_per_step)

            @pl.when(n > 0)
            def _():
                if per_step > 1:
                    out_ref[0:rows_per_step - BLK, :] = acc[0:rows_per_step - BLK, :].astype(BF16)
                out_ref[last, dq_cols] = acc[last, dq_cols].astype(BF16)
                out_ref[last, dk_cols] = (acc[last, dk_cols] + dkp).astype(BF16)
                out_ref[last, dv_cols] = (acc[last, dv_cols] + dvp).astype(BF16)

            acc[0:BLK, dq_cols] = dq
            acc[0:BLK, dk_cols] = dkc
            acc[0:BLK, dv_cols] = dvc

            def later(b, carry):
                dq, dkp, dkc, dvp, dvc = one_block(b)
                start = pl.multiple_of(b * BLK, BLK)
                before, rows = pl.ds(start - BLK, BLK), pl.ds(start, BLK)
                acc[before, dk_cols] += dkp
                acc[before, dv_cols] += dvp
                acc[rows, dq_cols] = dq
                acc[rows, dk_cols] = dkc
                acc[rows, dv_cols] = dvc
                return carry

            lax.fori_loop(1, per_step, later, 0)

        @pl.when(n == flush_at)
        def _():
            out_ref[...] = acc[...].astype(BF16)

    flush_at = nsteps - 1 if nsteps == 1 else nsteps
    out_lag = 0 if nsteps == 1 else 1
    nc = lambda n: jnp.minimum(n, nsteps - 1)
    npv = lambda n: jnp.maximum(jnp.minimum(n, nsteps - 1) * per_step - 1, 0)
    cur = lambda part: pl.BlockSpec((None, rows_per_step, ATTN_W), lambda r, n: (r, nc(n), part))
    prev = lambda part: pl.BlockSpec((None, BLK, ATTN_W), lambda r, n: (r, npv(n), part))
    row = pl.BlockSpec((None, rows_per_step, ATTN_W), lambda r, n: (r, nc(n), 0))
    tab_c = pl.BlockSpec((None, rows_per_step, LANES), lambda r, n: (r, nc(n), 0))
    tab_p = pl.BlockSpec((None, BLK, LANES), lambda r, n: (r, npv(n), 0))
    out_spec = pl.BlockSpec((None, rows_per_step, GROUP_COLS), lambda r, n: (r, jnp.maximum(n - out_lag, 0), 0))
    out = pl.pallas_call(
        body, grid=(dil, nsteps + out_lag),
        in_specs=[cur(0), cur(1), prev(1), cur(2), prev(2), row, row, row, tab_c, tab_c, tab_p, tab_p],
        out_specs=out_spec,
        out_shape=jax.ShapeDtypeStruct((dil, length, GROUP_COLS), BF16),
        scratch_shapes=[pltpu.VMEM((rows_per_step, GROUP_COLS), F32)]
        + [pltpu.VMEM((rows_per_step + BLK, ATTN_W), BF16)] * 2 + [pltpu.VMEM((rows_per_step + BLK, LANES), F32)] * 2,
        compiler_params=_cparams(2), name=f"attn_bwd_g{g}")(
            qkv_v, qkv_v, qkv_v, qkv_v, qkv_v, do, cc, lse, cos_v, sin_v, cos_v, sin_v)
    return out.reshape(t, GROUP_COLS)


SQRT_HALF = 0.7071067811865476
INV_SQRT_2PI = 0.3989422804014327


def _sgu_core(uv, g, b, w_ref, bias):
    cdf = 0.5 * (1.0 + lax.erf(uv * SQRT_HALF))
    z = uv * cdf
    u, v = z[:, :SGU_W], z[:, SGU_W:]
    mu = jnp.mean(v, axis=1, keepdims=True)
    xc = v - mu
    rs = lax.rsqrt(jnp.mean(xc * xc, axis=1, keepdims=True) + EPS)
    xhat = xc * rs
    vn = xhat * g + b
    row = lax.broadcasted_iota(jnp.int32, (SGU_CHUNK, SGU_CHUNK), 0)
    col = lax.broadcasted_iota(jnp.int32, (SGU_CHUNK, SGU_CHUNK), 1)
    tril = row >= col
    upper = lax.broadcasted_iota(jnp.int32, (SGU_CHUNK, LANES), 1) >= SGU_W // SGU_GROUPS
    ws, vlo, vhi, mixed = [], [], [], []
    for pr in range(SGU_W // LANES):
        sl = slice(pr * LANES, (pr + 1) * LANES)
        w0 = jnp.where(tril, w_ref[2 * pr], 0.0).astype(BF16)
        w1 = jnp.where(tril, w_ref[2 * pr + 1], 0.0).astype(BF16)
        vn2 = vn[:, sl]
        lo = jnp.where(upper, 0.0, vn2).astype(BF16)
        hi = jnp.where(upper, vn2, 0.0).astype(BF16)
        mixed.append(jnp.dot(w0, lo, preferred_element_type=F32) + jnp.dot(w1, hi, preferred_element_type=F32)
                     + bias[:, sl])
        ws.append((w0, w1))
        vlo.append(lo)
        vhi.append(hi)
    return cdf, u, xhat, rs, jnp.concatenate(mixed, axis=1), ws, vlo, vhi, tril, upper


SGU_STEP = 4 * SGU_CHUNK


def _for_chunks(step_rows, fn):
    def one(ci, carry):
        fn(pl.ds(pl.multiple_of(ci * SGU_CHUNK, SGU_CHUNK), SGU_CHUNK))
        return carry

    lax.fori_loop(0, step_rows // SGU_CHUNK, one, 0)


def _sgu_fwd(gu, ln_g, ln_b, w_s, bias_exp):
    t = gu.shape[0]
    step = min(t, SGU_STEP)

    def body(uv_ref, g_ref, b_ref, w_ref, bias_ref, o_ref):
        def chunk(rows):
            _, u, _, _, mixed, *_ = _sgu_core(uv_ref[rows, :].astype(F32), g_ref[...], b_ref[...], w_ref, bias_ref[...])
            o_ref[rows, :] = (u * mixed).astype(BF16)

        _for_chunks(step, chunk)

    return pl.pallas_call(
        body, grid=(t // step,),
        in_specs=[pl.BlockSpec((step, 2 * SGU_W), lambda n: (n, 0)), _full((1, SGU_W)), _full((1, SGU_W)),
                  _full((SGU_GROUPS, SGU_CHUNK, SGU_CHUNK)), _full((SGU_CHUNK, SGU_W))],
        out_specs=pl.BlockSpec((step, SGU_W), lambda n: (n, 0)),
        out_shape=jax.ShapeDtypeStruct((t, SGU_W), BF16),
        compiler_params=_cparams(1), name="sgu_fwd")(gu, ln_g, ln_b, w_s, bias_exp)


def _sgu_bwd(dproj, gu, dsgu, ln_g, ln_b, w_s, bias_exp):
    t = gu.shape[0]
    step = min(t, SGU_STEP)
    nsteps = t // step
    e = _head_sum_matrix()

    def body(dp_in, uv_ref, ds_ref, g_ref, b_ref, w_ref, bias_ref, e_ref, out_ref, dw_ref, dbias_ref, dg_ref, db_ref):
        n = pl.program_id(0)

        @pl.when(n == 0)
        def _():
            dw_ref[...] = jnp.zeros(dw_ref.shape, F32)
            dbias_ref[...] = jnp.zeros(dbias_ref.shape, F32)
            dg_ref[...] = jnp.zeros(dg_ref.shape, F32)
            db_ref[...] = jnp.zeros(db_ref.shape, F32)

        _for_chunks(step, functools.partial(chunk, uv_ref, ds_ref, g_ref, b_ref, w_ref, bias_ref, out_ref, dw_ref, dbias_ref,
                                            dg_ref, db_ref))

        @pl.when(n == nsteps - 1)
        def _():
            dbias_ref[...] = _group_sum(dbias_ref[...], e_ref[...])

    def chunk(uv_ref, ds_ref, g_ref, b_ref, w_ref, bias_ref, out_ref, dw_ref, dbias_ref, dg_ref, db_ref, rows):
        uv = uv_ref[rows, :].astype(F32)
        g = g_ref[...]
        cdf, u, xhat, rs, mixed, ws, vlo, vhi, tril, upper = _sgu_core(uv, g, b_ref[...], w_ref, bias_ref[...])
        dsg = ds_ref[rows, :]
        du = dsg * mixed
        dmixed = dsg * u
        dbias_ref[...] += dmixed
        dvn = []
        for pr in range(SGU_W // LANES):
            sl = slice(pr * LANES, (pr + 1) * LANES)
            dm2 = dmixed[:, sl]
            dlo = jnp.where(upper, 0.0, dm2).astype(BF16)
            dhi = jnp.where(upper, dm2, 0.0).astype(BF16)
            w0, w1 = ws[pr]
            dvn.append(lax.dot_general(w0, dlo, (TN, ((), ())), preferred_element_type=F32)
                       + lax.dot_general(w1, dhi, (TN, ((), ())), preferred_element_type=F32))
            dw0 = lax.dot_general(dlo, vlo[pr], (NT, ((), ())), preferred_element_type=F32)
            dw1 = lax.dot_general(dhi, vhi[pr], (NT, ((), ())), preferred_element_type=F32)
            dw_ref[2 * pr] += jnp.where(tril, dw0, 0.0)
            dw_ref[2 * pr + 1] += jnp.where(tril, dw1, 0.0)
        dvn = jnp.concatenate(dvn, axis=1)
        dg_ref[...] += jnp.sum(dvn * xhat, axis=0, keepdims=True)
        db_ref[...] += jnp.sum(dvn, axis=0, keepdims=True)
        dxh = dvn * g
        dv = rs * (dxh - jnp.mean(dxh, axis=1, keepdims=True) - xhat * jnp.mean(dxh * xhat, axis=1, keepdims=True))
        dz = jnp.concatenate([du, dv], axis=1)
        dgelu = cdf + uv * (INV_SQRT_2PI * jnp.exp(-0.5 * uv * uv))
        out_ref[rows, :] = (dz * dgelu).astype(BF16)

    outs = pl.pallas_call(
        body, grid=(nsteps,),
        in_specs=[pl.BlockSpec(memory_space=pl.ANY), pl.BlockSpec((step, 2 * SGU_W), lambda n: (n, 0)),
                  pl.BlockSpec((step, SGU_W), lambda n: (n, 0)), _full((1, SGU_W)), _full((1, SGU_W)),
                  _full((SGU_GROUPS, SGU_CHUNK, SGU_CHUNK)), _full((SGU_CHUNK, SGU_W)), _full((ATTN_W, ATTN_W))],
        out_specs=[pl.BlockSpec((step, 2 * SGU_W), lambda n: (n, 0)), _full((SGU_GROUPS, SGU_CHUNK, SGU_CHUNK)),
                   _full((SGU_CHUNK, SGU_W)), _full((1, SGU_W)), _full((1, SGU_W))],
        out_shape=[jax.ShapeDtypeStruct(dproj.shape, BF16), jax.ShapeDtypeStruct((SGU_GROUPS, SGU_CHUNK, SGU_CHUNK), F32),
                   jax.ShapeDtypeStruct((SGU_CHUNK, SGU_W), F32), jax.ShapeDtypeStruct((1, SGU_W), F32),
                   jax.ShapeDtypeStruct((1, SGU_W), F32)],
        input_output_aliases={0: 0},
        compiler_params=_cparams(1), name="sgu_bwd")(dproj, gu, dsgu, ln_g, ln_b, w_s, bias_exp, e)
    return outs


def _merge_fwd(attn, sgu, gu, x, w_pa, w_ps, w_out, g2):
    t = x.shape[0]
    tm = min(t, 512)

    def body(a_ref, s_ref, ga_ref, gb_ref, x_ref, wpa, wps, wo, g_ref, pa_ref, ps_ref, m_ref, x1_ref, h2_ref):
        pa = jnp.dot(a_ref[...], wpa[...], preferred_element_type=F32)
        ps = jnp.dot(s_ref[...], wps[...], preferred_element_type=F32)
        merged = (_sigmoid(ga_ref[...].astype(F32)) * pa + _sigmoid(gb_ref[...].astype(F32)) * ps).astype(BF16)
        x1 = x_ref[...] + jnp.dot(merged, wo[...], preferred_element_type=F32)
        xhat, _ = _rms_stats(x1)
        pa_ref[...] = pa.astype(BF16)
        ps_ref[...] = ps.astype(BF16)
        m_ref[...] = merged
        x1_ref[...] = x1
        h2_ref[...] = (xhat * g_ref[...]).astype(BF16)

    half = pl.BlockSpec((tm, ATTN_W), lambda i: (i, 0))
    full = pl.BlockSpec((tm, D_MODEL), lambda i: (i, 0))
    return pl.pallas_call(
        body, grid=(t // tm,),
        in_specs=[half, half, pl.BlockSpec((tm, D_MODEL), lambda i: (i, 1)), pl.BlockSpec((tm, D_MODEL), lambda i: (i, 2)),
                  full, _resident((ATTN_W, D_MODEL)), _resident((SGU_W, D_MODEL)), _resident((D_MODEL, D_MODEL)),
                  _full((1, D_MODEL))],
        out_specs=[full] * 5,
        out_shape=[jax.ShapeDtypeStruct((t, D_MODEL), BF16), jax.ShapeDtypeStruct((t, D_MODEL), BF16),
                   jax.ShapeDtypeStruct((t, D_MODEL), BF16), jax.ShapeDtypeStruct((t, D_MODEL), F32),
                   jax.ShapeDtypeStruct((t, D_MODEL), BF16)],
        compiler_params=_cparams(1), name="merge_fwd")(attn, sgu, gu, gu, x, w_pa, w_ps, w_out, g2)


def _merge_bwd(dx1b, gu, pa, ps, w_pa, w_ps, w_out):
    t = dx1b.shape[0]
    tm = min(t, 512)

    def body(d_ref, ga_ref, gb_ref, pa_ref, ps_ref, wpa, wps, wo, out_ref, dpa_ref, dps_ref, da_ref, dsg_ref):
        dm = lax.dot_general(d_ref[...], wo[...], (NT, ((), ())), preferred_element_type=F32)
        sa, sb = _sigmoid(ga_ref[...].astype(F32)), _sigmoid(gb_ref[...].astype(F32))
        dpa = (dm * sa).astype(BF16)
        dps = (dm * sb).astype(BF16)
        out_ref[:, 0:D_MODEL] = jnp.zeros((tm, D_MODEL), BF16)
        out_ref[:, D_MODEL:2 * D_MODEL] = (dm * pa_ref[...].astype(F32) * sa * (1.0 - sa)).astype(BF16)
        out_ref[:, 2 * D_MODEL:GU_COLS] = (dm * ps_ref[...].astype(F32) * sb * (1.0 - sb)).astype(BF16)
        dpa_ref[...] = dpa
        dps_ref[...] = dps
        da_ref[...] = lax.dot_general(dpa, wpa[...], (NT, ((), ())), preferred_element_type=F32)
        dsg_ref[...] = lax.dot_general(dps, wps[...], (NT, ((), ())), preferred_element_type=F32)

    half = pl.BlockSpec((tm, ATTN_W), lambda i: (i, 0))
    full = pl.BlockSpec((tm, D_MODEL), lambda i: (i, 0))
    return pl.pallas_call(
        body, grid=(t // tm,),
        in_specs=[full, pl.BlockSpec((tm, D_MODEL), lambda i: (i, 1)),
                  pl.BlockSpec((tm, D_MODEL), lambda i: (i, 2)), full, full,
                  _resident((ATTN_W, D_MODEL)), _resident((SGU_W, D_MODEL)), _resident((D_MODEL, D_MODEL))],
        out_specs=[pl.BlockSpec((tm, GU_COLS), lambda i: (i, 0)), full, full, half, half],
        out_shape=[jax.ShapeDtypeStruct((t, GU_COLS), BF16), jax.ShapeDtypeStruct((t, D_MODEL), BF16),
                   jax.ShapeDtypeStruct((t, D_MODEL), BF16), jax.ShapeDtypeStruct((t, ATTN_W), F32),
                   jax.ShapeDtypeStruct((t, SGU_W), F32)],
        compiler_params=_cparams(1), name="merge_bwd")(dx1b, gu, gu, pa, ps, w_pa, w_ps, w_out)


def _token_call(name, body, t, tm, ins, outs, reds=(), scratch=()):
    return pl.pallas_call(
        body, grid=(t // tm,), in_specs=[s for _, s in ins],
        out_specs=[o[2] for o in outs] + [_full(r) for r in reds],
        out_shape=[jax.ShapeDtypeStruct(o[0], o[1]) for o in outs] + [jax.ShapeDtypeStruct(r, F32) for r in reds],
        scratch_shapes=list(scratch), compiler_params=_cparams(1), name=name)(*[a for a, _ in ins])


def _rows_spec(tm, width):
    return pl.BlockSpec((tm, width), lambda i: (i, 0))


def _chips_spec(tm):
    return pl.BlockSpec((N_CHIPS, tm, FF_SHARD), lambda i: (0, i, 0))


def _zero_at_start(*refs):
    @pl.when(pl.program_id(0) == 0)
    def _():
        for r in refs:
            r[...] = jnp.zeros(r.shape, r.dtype)


def _ffn_fwd(h2, w_g, w_u):
    t = h2.shape[0]
    tm = min(t, 512)

    def body(h_ref, wg_ref, wu_ref, fa_ref, fb_ref, ff_ref):
        h = h_ref[...]
        for s in range(N_CHIPS):
            a = jnp.dot(h, wg_ref[s], preferred_element_type=F32)
            b = jnp.dot(h, wu_ref[s], preferred_element_type=F32)
            sg = _sigmoid(a)
            silu = a * sg
            fa_ref[s] = (b * (sg * (1.0 + a * (1.0 - sg)))).astype(BF16)
            fb_ref[s] = silu.astype(BF16)
            ff_ref[s] = (silu * b).astype(BF16)

    shp = (N_CHIPS, t, FF_SHARD)
    w_spec = _resident((N_CHIPS, D_MODEL, FF_SHARD))
    return _token_call("ffn_fwd", body, t, tm, [(h2, _rows_spec(tm, D_MODEL)), (w_g, w_spec), (w_u, w_spec)],
                       [(shp, BF16, _chips_spec(tm))] * 3)


def _ffn_down_loss(ff, w_d, x1, tgt, gf):
    t = x1.shape[0]
    tm = min(t, 512)

    def body(ff_ref, wd_ref, x1_ref, tgt_ref, g_ref, dx2_ref, dx2b_ref, loss_ref, dgf_ref):
        _zero_at_start(loss_ref, dgf_ref)
        acc = jnp.dot(ff_ref[0], wd_ref[0], preferred_element_type=F32)
        for s in range(1, N_CHIPS):
            acc = acc + jnp.dot(ff_ref[s], wd_ref[s], preferred_element_type=F32)
        x2 = x1_ref[...] + acc
        g = g_ref[...]
        xhat, rr = _rms_stats(x2)
        diff = xhat * g - tgt_ref[...]
        rows = jnp.sum(diff * diff, axis=1, keepdims=True)
        loss_ref[...] += jnp.broadcast_to(jnp.sum(rows, axis=0, keepdims=True) * (0.5 / D_MODEL), (1, LANES))
        dy = diff * (1.0 / D_MODEL)
        dgf_ref[...] += jnp.sum(dy * xhat, axis=0, keepdims=True)
        dx2 = _rms_bwd(dy, xhat, rr, g)
        dx2_ref[...] = dx2
        dx2b_ref[...] = dx2.astype(BF16)

    row = _rows_spec(tm, D_MODEL)
    return _token_call("ffn_down_loss", body, t, tm,
                       [(ff, _chips_spec(tm)), (w_d, _resident((N_CHIPS, FF_SHARD, D_MODEL))), (x1, row), (tgt, row),
                        (gf, _full((1, D_MODEL)))],
                       [((t, D_MODEL), F32, row), ((t, D_MODEL), BF16, row)], reds=[(1, LANES), (1, D_MODEL)])


def _ffn_bwd_act(dx2b, w_d, fa, fb):
    t = dx2b.shape[0]
    tm = min(t, 512)

    def body(d_ref, wd_ref, fa_ref, fb_ref, da_ref, db_ref):
        d = d_ref[...]
        for s in range(N_CHIPS):
            dff = lax.dot_general(d, wd_ref[s], (NT, ((), ())), preferred_element_type=F32)
            da_ref[s] = (dff * fa_ref[s].astype(F32)).astype(BF16)
            db_ref[s] = (dff * fb_ref[s].astype(F32)).astype(BF16)

    shp = (N_CHIPS, t, FF_SHARD)
    return _token_call("ffn_bwd_act", body, t, tm,
                       [(dx2b, _rows_spec(tm, D_MODEL)), (w_d, _resident((N_CHIPS, FF_SHARD, D_MODEL))),
                        (fa, _chips_spec(tm)), (fb, _chips_spec(tm))],
                       [(shp, BF16, _chips_spec(tm))] * 2)


def _ffn_bwd_in(da, db, w_g, w_u, x1, dx2, g2):
    t = x1.shape[0]
    tm = min(t, 512)

    def body(da_ref, db_ref, wg_ref, wu_ref, x1_ref, dx2_ref, g_ref, dx1_ref, dx1b_ref, dg_ref):
        _zero_at_start(dg_ref)
        acc = None
        for s in range(N_CHIPS):
            part = (lax.dot_general(da_ref[s], wg_ref[s], (NT, ((), ())), preferred_element_type=F32)
                    + lax.dot_general(db_ref[s], wu_ref[s], (NT, ((), ())), preferred_element_type=F32))
            acc = part if acc is None else acc + part
        xhat, rr = _rms_stats(x1_ref[...])
        dg_ref[...] += jnp.sum(acc * xhat, axis=0, keepdims=True)
        dx1 = dx2_ref[...] + _rms_bwd(acc, xhat, rr, g_ref[...])
        dx1_ref[...] = dx1
        dx1b_ref[...] = dx1.astype(BF16)

    row = _rows_spec(tm, D_MODEL)
    w_spec = _resident((N_CHIPS, D_MODEL, FF_SHARD))
    return _token_call("ffn_bwd_in", body, t, tm,
                       [(da, _chips_spec(tm)), (db, _chips_spec(tm)), (w_g, w_spec), (w_u, w_spec), (x1, row), (dx2, row),
                        (g2, _full((1, D_MODEL)))],
                       [((t, D_MODEL), F32, row), ((t, D_MODEL), BF16, row)], reds=[(1, D_MODEL)])


def _group_dh(d, w_refs):
    dh = None
    for part, w_ref in enumerate(w_refs):
        term = lax.dot_general(d[:, part * ATTN_W:(part + 1) * ATTN_W], w_ref[...], (NT, ((), ())),
                               preferred_element_type=F32)
        dh = term if dh is None else dh + term
    return dh


def _in_proj_bwd(dgu, dqkvs, w_in, x, dx1, g1):
    t = x.shape[0]
    tile = min(t, TILE)
    ngroups = len(DILATIONS)

    def body(*refs):
        dgu_ref, dq_refs = refs[0], refs[1:1 + ngroups]
        w0_ref, w1_ref = refs[1 + ngroups:3 + ngroups]
        wg_refs = [refs[3 + ngroups + 3 * g:6 + ngroups + 3 * g] for g in range(ngroups)]
        x_ref, dx1_ref, g_ref, dx_ref, dg_ref = refs[3 + 4 * ngroups:5 + 4 * ngroups + 3]
        slabs = refs[5 + 4 * ngroups + 3:]
        _zero_at_start(dg_ref)
        for g in range(1, ngroups):
            dil = DILATIONS[g]
            part = _group_dh(dq_refs[g][...].reshape(tile, GROUP_COLS), wg_refs[g])
            for r in range(dil):
                _put_class_rows(slabs[g - 1], r, dil, part[r * (tile // dil):(r + 1) * (tile // dil)])
        dh = lax.dot_general(dgu_ref[:, 0:GU_HALF], w0_ref[...], (NT, ((), ())), preferred_element_type=F32)
        dh = dh + lax.dot_general(dgu_ref[:, GU_HALF:], w1_ref[...], (NT, ((), ())), preferred_element_type=F32)
        dh = dh + _group_dh(dq_refs[0][0], wg_refs[0])
        for slab in slabs:
            dh = dh + _from_slabs(slab)
        xhat, rr = _rms_stats(x_ref[...])
        dg_ref[...] += jnp.sum(dh * xhat, axis=0, keepdims=True)
        dx_ref[...] = dx1_ref[...] + _rms_bwd(dh, xhat, rr, g_ref[...])

    row = _rows_spec(tile, D_MODEL)
    group_ins = [(dqkvs[g].reshape(d, t // d, GROUP_COLS), _group_spec(d, tile, GROUP_COLS)) for g, d in enumerate(DILATIONS)]
    w_specs = _gu_w_specs() + [s for g in range(ngroups) for s in _group_w_specs(g)]
    return _token_call(
        "in_proj_bwd", body, t, tile,
        [(dgu, _rows_spec(tile, GU_COLS))] + group_ins + [(w_in, s) for s in w_specs]
        + [(x, row), (dx1, row), (g1, _full((1, D_MODEL)))],
        [((t, D_MODEL), F32, row)], reds=[(1, D_MODEL)], scratch=[_slabs(tile, D_MODEL)] * (ngroups - 1))


WGRAD_TK = 2048


def _wgrad_mm(name, grid, a, a_spec, b, b_spec, acc_shape, out_shape, out_spec, dst=None):
    nk = grid[-1]

    def body(*refs):
        a_ref, b_ref, o_ref, acc_ref = refs[0], refs[1], refs[-2], refs[-1]
        k = pl.program_id(len(grid) - 1)
        part = lax.dot_general(a_ref[...], b_ref[...], (TN, ((), ())), preferred_element_type=F32)

        @pl.when(k == 0)
        def _():
            acc_ref[...] = part

        @pl.when(k > 0)
        def _():
            acc_ref[...] += part

        @pl.when(k == nk - 1)
        def _():
            o_ref[...] = acc_ref[...].astype(BF16)

    filled = [] if dst is None else [dst]
    return pl.pallas_call(
        body, grid=grid, in_specs=[a_spec, b_spec] + [pl.BlockSpec(memory_space=pl.ANY)] * len(filled),
        out_specs=out_spec, out_shape=jax.ShapeDtypeStruct(out_shape, BF16), scratch_shapes=[pltpu.VMEM(acc_shape, F32)],
        input_output_aliases={2: 0} if filled else {}, compiler_params=_cparams(len(grid)), name=name)(a, b, *filled)


def _wgrad_2d(name, a, b, tm, tn):
    t, k1 = a.shape
    n = b.shape[1]
    tk = min(t, WGRAD_TK)
    return _wgrad_mm(name, (k1 // tm, n // tn, t // tk), a, pl.BlockSpec((tk, tm), lambda i, j, k: (k, i)),
                     b, pl.BlockSpec((tk, tn), lambda i, j, k: (k, j)), (tm, tn), (k1, n),
                     pl.BlockSpec((tm, tn), lambda i, j, k: (i, j)))


def _wgrad_in(hs, dgu, dqkvs):
    t = dgu.shape[0]
    tk = min(t, WGRAD_TK)
    gu_block = QKV_BLOCKS * ATTN_W // GU_HALF
    parts = [(hs[0], dgu, GU_HALF, lambda j: j + gu_block)]
    parts += [(hs[g].reshape(t, D_MODEL), dqkvs[g], ATTN_W, lambda j, g=g: _w_in_block(j, g)) for g in range(3)]
    dst = None
    for n, (a, b, tn, block_of) in enumerate(parts):
        dst = _wgrad_mm(f"wgrad_in_{n}", (1, b.shape[1] // tn, t // tk),
                        a, pl.BlockSpec((tk, D_MODEL), lambda i, j, k: (k, 0)), b, pl.BlockSpec((tk, tn), lambda i, j, k: (k, j)),
                        (D_MODEL, tn), (D_MODEL, IN_COLS),
                        pl.BlockSpec((D_MODEL, tn), lambda i, j, k, block_of=block_of: (0, block_of(j))), dst=dst)
    return dst


def _wgrad_ff_in(name, h2, da):
    t = h2.shape[0]
    tk = min(t, WGRAD_TK)
    return _wgrad_mm(name, (N_CHIPS, 1, t // tk), h2, pl.BlockSpec((tk, D_MODEL), lambda i, j, k: (k, 0)),
                     da, pl.BlockSpec((None, tk, FF_SHARD), lambda i, j, k: (i, k, 0)), (D_MODEL, FF_SHARD),
                     (N_CHIPS, D_MODEL, FF_SHARD), pl.BlockSpec((None, D_MODEL, FF_SHARD), lambda i, j, k: (i, 0, 0)))


def _wgrad_ff_down(ff, dx2b):
    t = dx2b.shape[0]
    tk = min(t, WGRAD_TK)
    return _wgrad_mm("wgrad_ffn_down", (N_CHIPS, 1, t // tk), ff, pl.BlockSpec((None, tk, FF_SHARD), lambda i, j, k: (i, k, 0)),
                     dx2b, pl.BlockSpec((tk, D_MODEL), lambda i, j, k: (k, 0)), (FF_SHARD, D_MODEL),
                     (N_CHIPS, FF_SHARD, D_MODEL), pl.BlockSpec((None, FF_SHARD, D_MODEL), lambda i, j, k: (i, 0, 0)))


def _local_step(x, pos_col, tgt, g1, ln_g, ln_b, w_s, b_s, g2, gf, first_weight, late_weights, on_grads=None):
    tables = _rope_tables(pos_col)
    bias_exp = jnp.repeat(jnp.transpose(b_s), SGU_W // SGU_GROUPS, axis=1)

    hs = _norm_fwd(x, g1)
    w_p = first_weight(hs[0])
    gu, qkvs = _in_proj(hs, w_p, tables)
    os_, ls_ = [], []
    for g, dil in enumerate(DILATIONS):
        o, lse = _attn_fwd(qkvs[g], g, dil)
        os_.append(o)
        ls_.append(lse)
    attn = _combine_fwd(os_, ls_)
    sgu = _sgu_fwd(gu, ln_g, ln_b, w_s, bias_exp)
    w_pa, w_ps, w_out, w_g, w_u, w_d = late_weights(attn)
    pa, ps, merged, x1, h2 = _merge_fwd(attn, sgu, gu, x, w_pa, w_ps, w_out, g2)
    fa, fb, ff = _ffn_fwd(h2, w_g, w_u)
    dx2, dx2b, loss, dgf = _ffn_down_loss(ff, w_d, x1, tgt, gf)

    da, db = _ffn_bwd_act(dx2b, w_d, fa, fb)
    dw_d = _wgrad_ff_down(ff, dx2b)
    dx1, dx1b, dg2 = _ffn_bwd_in(da, db, w_g, w_u, x1, dx2, g2)
    dw_g = _wgrad_ff_in("wgrad_ffn_gate", h2, da)
    dw_u = _wgrad_ff_in("wgrad_ffn_up", h2, db)

    dgu, dpa, dps, dattn, dsgu = _merge_bwd(dx1b, gu, pa, ps, w_pa, w_ps, w_out)
    dw_out = _wgrad_2d("wgrad_out", merged, dx1b, D_MODEL, D_MODEL)
    dw_pa = _wgrad_2d("wgrad_proj_attn", attn, dpa, ATTN_W, D_MODEL)
    dw_ps = _wgrad_2d("wgrad_proj_sgu", sgu, dps, SGU_W, D_MODEL)
    if on_grads is not None:
        ln_g = ln_g + on_grads(1, dict(w_proj_attn=dw_pa, w_proj_sgu=dw_ps, w_out=dw_out, w_ffn_gate=dw_g, w_ffn_up=dw_u,
                                       w_ffn_down=dw_d))[:, :SGU_W]
    dgu, dw_s, dbias, dln_g, dln_b = _sgu_bwd(dgu, gu, dsgu, ln_g, ln_b, w_s, bias_exp)
    dos, ccs = _combine_bwd(dattn, os_, ls_)
    dqkvs = [_attn_bwd(qkvs[g], dos[g], ccs[g], ls_[g], *tables[g], g, dil) for g, dil in enumerate(DILATIONS)]
    dw_p = _wgrad_in(hs, dgu, dqkvs)
    if on_grads is not None:
        g1 = g1 + on_grads(0, dict(w_in=dw_p))
    dx, dg1 = _in_proj_bwd(dgu, dqkvs, w_p, x, dx1, g1)

    db_s = jnp.transpose(dbias[:, ::SGU_W // SGU_GROUPS])
    small = dict(loss=loss, norm1_g=dg1, sgu_ln_g=dln_g, sgu_ln_b=dln_b, w_spatial=dw_s, b_spatial=db_s,
                 norm2_g=dg2, final_g=dgf)
    big = dict(w_in=dw_p, w_proj_attn=dw_pa, w_proj_sgu=dw_ps, w_out=dw_out, w_ffn_gate=dw_g, w_ffn_up=dw_u,
               w_ffn_down=dw_d)
    return dx, big, small


def _ew(name, fn, ins, out_dtypes):
    shp = ins[0].shape
    rows, cols = shp
    tr = next((cand for cand in (256, 352, 128) if rows % cand == 0 and rows > cand), rows)

    def body(*refs):
        res = fn(*[r[...] for r in refs[:len(ins)]])
        for o_ref, v in zip(refs[len(ins):], res):
            o_ref[...] = v.astype(o_ref.dtype)

    spec = pl.BlockSpec((tr, cols), lambda i: (i, 0))
    return pl.pallas_call(
        body, grid=(rows // tr,), in_specs=[spec] * len(ins), out_specs=[spec] * len(out_dtypes),
        out_shape=[jax.ShapeDtypeStruct(shp, d) for d in out_dtypes],
        compiler_params=_cparams(1), name=name)(*ins)


def _adamw_math(g, w, m, v):
    m = ADAM_B1 * m + (1.0 - ADAM_B1) * g
    v = ADAM_B2 * v + (1.0 - ADAM_B2) * (g * g)
    m_hat = m / (1.0 - ADAM_B1 ** ADAM_STEP)
    v_hat = v / (1.0 - ADAM_B2 ** ADAM_STEP)
    delta = -ADAM_LR * (m_hat / (jnp.sqrt(v_hat) + ADAM_EPS) + ADAM_WD * w)
    return delta, m, v


def _adamw(name, g, w, m, v):
    return _ew(name, lambda g_, w_, m_, v_: (g_,) + _adamw_math(g_, w_, m_, v_), [g, w, m, v], [F32] * 4)


VMEM_SPEC = pl.BlockSpec(memory_space=pltpu.VMEM)


def _for_row_chunks(rows, fn):
    ck = next(c for c in (64, 32, 16) if rows % c == 0)

    def step(i, carry):
        fn(pl.multiple_of(i * ck, ck), ck)
        return carry

    lax.fori_loop(0, rows // ck, step, 0)


def _place():
    x, y, c = lax.axis_index("x"), lax.axis_index("y"), lax.axis_index("c")
    chips = [(1 - x, y), (x, 1 - y), (1 - x, 1 - y)]
    return x, y, c, 2 * x + y, chips


def _rows(ref, start, size):
    if len(ref.shape) == 2:
        return ref.at[pl.ds(start, size), :]
    return ref.at[:, pl.ds(start, size), :]


def _comm_call(name, body, ins, out_shapes, scratch, n_remote):
    return pl.pallas_call(
        body, in_specs=[VMEM_SPEC] * len(ins), out_specs=[VMEM_SPEC] * len(out_shapes),
        out_shape=out_shapes,
        scratch_shapes=list(scratch) + [pltpu.SemaphoreType.DMA((n_remote,)), pltpu.SemaphoreType.DMA((n_remote,))],
        compiler_params=pltpu.CompilerParams(vmem_limit_bytes=VMEM_LIMIT), name=name)(*ins)


def _gather_finish(name, shard, landed):
    k_rows, n = shard.shape
    kh = k_rows // 2

    def body(shard_ref, land_ref, out_ref, send, recv):
        x, y, c, me, chips = _place()
        passed = []
        for j, chip in enumerate(chips):
            theirs = 2 * chip[0] + chip[1]
            cp = pltpu.make_async_remote_copy(
                src_ref=land_ref.at[j], dst_ref=_rows(out_ref.at[theirs], c * kh, kh), send_sem=send.at[j],
                recv_sem=recv.at[j], device_id=(x, y, 1 - c), device_id_type=MESH)
            cp.start()
            passed.append(cp)
        mine = out_ref.at[me]

        def put_own(r0, ck):
            mine[pl.ds(r0, ck), :] = shard_ref[pl.ds(r0, ck), :]

        _for_row_chunks(k_rows, put_own)
        for j, chip in enumerate(chips):
            slot = out_ref.at[2 * chip[0] + chip[1]]

            def put_half(r0, ck, j=j, slot=slot):
                slot[pl.ds(pl.multiple_of(c * kh + r0, ck), ck), :] = land_ref[j, pl.ds(r0, ck), :]

            _for_row_chunks(kh, put_half)
        for j, chip in enumerate(chips):
            other = _rows(out_ref.at[2 * chip[0] + chip[1]], (1 - c) * kh, kh)
            pltpu.make_async_remote_copy(src_ref=other, dst_ref=other, send_sem=send.at[j], recv_sem=recv.at[j],
                                         device_id=(x, y, 1 - c), device_id_type=MESH).wait_recv()
        for cp in passed:
            cp.wait_send()

    return _comm_call(name, body, [shard, landed], [jax.ShapeDtypeStruct((N_CHIPS, k_rows, n), shard.dtype)], [], 3)[0]


HBM_SPEC = pl.BlockSpec(memory_space=pltpu.HBM)
SEM_SPEC = pl.BlockSpec(memory_space=pltpu.SEMAPHORE)
DATAFLOW = pltpu.SideEffectType.DATAFLOW_SIDE_EFFECTING
TOKEN_SHAPE = (1, D_MODEL)
N_PEERS = 7


def _peers():
    x, y, c = lax.axis_index("x"), lax.axis_index("y"), lax.axis_index("c")
    flip = lambda v, f: 1 - v if f else v
    return [(flip(x, k & 4), flip(y, k & 2), flip(c, k & 1)) for k in range(1, N_PEERS + 1)]


def _piece_shape(shape):
    return (shape[-2] // 2, shape[2] if len(shape) == 3 else shape[1] // N_CHIPS)


def _device_piece(ref, chip, core):
    kh, n4 = _piece_shape(ref.shape)
    if len(ref.shape) == 3:
        return ref.at[chip, pl.ds(core * kh, kh), :]
    return ref.at[pl.ds(core * kh, kh), pl.ds(chip * n4, n4)]


def _exchange_copies(partials, lands, send, recv):
    return [pltpu.make_async_remote_copy(
        src_ref=_device_piece(partials[t], 2 * px + py, pc), dst_ref=lands[t].at[k], send_sem=send.at[t * N_PEERS + k],
        recv_sem=recv.at[t * N_PEERS + k], device_id=(px, py, pc), device_id_type=MESH)
        for t in range(len(partials)) for k, (px, py, pc) in enumerate(_peers())]


def _broadcast_copies(srcs, lands, send, recv):
    return [pltpu.make_async_remote_copy(
        src_ref=srcs[t], dst_ref=lands[t].at[k], send_sem=send.at[t * N_PEERS + k], recv_sem=recv.at[t * N_PEERS + k],
        device_id=peer, device_id_type=MESH)
        for t in range(len(srcs)) for k, peer in enumerate(_peers())]


def _gather_copies(shards, lands, send, recv):
    x, y, c, me, chips = _place()
    return [pltpu.make_async_remote_copy(
        src_ref=shards[t], dst_ref=lands[t].at[me], send_sem=send.at[t * 3 + j], recv_sem=recv.at[t * 3 + j],
        device_id=(*chip, c), device_id_type=MESH)
        for t in range(len(shards)) for j, chip in enumerate(chips)]


def _gather_half_copies(shards, lands, send, recv):
    x, y, c, me, chips = _place()
    return [pltpu.make_async_remote_copy(
        src_ref=_rows(shards[t], c * (shards[t].shape[0] // 2), shards[t].shape[0] // 2), dst_ref=lands[t].at[j],
        send_sem=send.at[t * 3 + j], recv_sem=recv.at[t * 3 + j], device_id=(*chip, c), device_id_type=MESH)
        for t in range(len(shards)) for j, chip in enumerate(chips)]


def _split_start(name, copies, per_tensor, srcs, land_shapes):
    nt = len(srcs)
    lands = [lax.empty(s, a.dtype) for s, a in zip(land_shapes, srcs)]
    nsem = nt * per_tensor

    def body(*refs):
        send, recv = refs[2 * nt], refs[2 * nt + 1]
        for cp in copies(refs[:nt], refs[nt:2 * nt], send, recv):
            cp.start()
        refs[-1][...] = jnp.zeros(TOKEN_SHAPE, F32)

    hbm = lambda a: pltpu.with_memory_space_constraint(a, pltpu.HBM)
    outs = pl.pallas_call(
        body, name=name,
        out_shape=[pltpu.SemaphoreType.DMA((nsem,)), pltpu.SemaphoreType.DMA((nsem,))]
        + [pltpu.HBM(s.shape, s.dtype) for s in srcs] + [pltpu.HBM(l.shape, l.dtype) for l in lands]
        + [jax.ShapeDtypeStruct(TOKEN_SHAPE, F32)],
        in_specs=[HBM_SPEC] * (2 * nt), out_specs=[SEM_SPEC, SEM_SPEC] + [HBM_SPEC] * (2 * nt) + [VMEM_SPEC],
        input_output_aliases={i: 2 + i for i in range(2 * nt)},
        compiler_params=pltpu.CompilerParams(has_side_effects=DATAFLOW))(*[hbm(a) for a in list(srcs) + lands])
    return outs[0], outs[1], outs[2:2 + nt], outs[2 + nt:2 + 2 * nt], outs[-1]


def _split_wait(name, copies, send, recv, srcs, lands, after):
    nt = len(srcs)
    after = list(after) if isinstance(after, (list, tuple)) else [after]

    def body(*refs):
        for cp in copies(refs[:nt], refs[nt:2 * nt], refs[2 * nt], refs[2 * nt + 1]):
            cp.wait_send()
            cp.wait_recv()

    outs = pl.pallas_call(
        body, name=name,
        out_shape=[pltpu.HBM(s.shape, s.dtype) for s in srcs] + [pltpu.HBM(l.shape, l.dtype) for l in lands],
        in_specs=[HBM_SPEC] * (2 * nt) + [SEM_SPEC, SEM_SPEC] + [pl.BlockSpec(memory_space=pl.ANY)] * len(after),
        out_specs=[HBM_SPEC] * (2 * nt), input_output_aliases={i: i for i in range(2 * nt)},
        compiler_params=pltpu.CompilerParams(has_side_effects=DATAFLOW))(*srcs, *lands, send, recv, *after)
    return outs[:nt], outs[nt:]


def _device_sum(name, partials, lands):
    nt = len(partials)

    def body(*refs):
        ins, slots, outs, owns = refs[:nt], refs[nt:2 * nt], refs[2 * nt:3 * nt], refs[3 * nt:4 * nt]
        send, recv, loc = refs[4 * nt:]
        x, y, c, me, chips = _place()
        sibling = (x, y, 1 - c)
        loads = [pltpu.make_async_copy(_device_piece(ins[t], me, c), owns[t], loc.at[t]) for t in range(nt)]
        for cp in loads:
            cp.start()
        handed = []
        for t in range(nt):
            kh = owns[t].shape[0]
            loads[t].wait()

            def add(r0, ck, own=owns[t], slot=slots[t], dst=outs[t], kh=kh):
                rows = pl.ds(r0, ck)
                acc = own[rows, :].astype(F32)
                for k in range(N_PEERS):
                    acc = acc + slot[k, rows, :].astype(F32)
                dst[pl.ds(pl.multiple_of(c * kh + r0, ck), ck), :] = acc

            _for_row_chunks(kh, add)
            rc = pltpu.make_async_remote_copy(
                src_ref=_rows(outs[t], c * kh, kh), dst_ref=_rows(outs[t], c * kh, kh), send_sem=send.at[t],
                recv_sem=recv.at[t], device_id=sibling, device_id_type=MESH)
            rc.start()
            handed.append(rc)
        for t in range(nt):
            kh = owns[t].shape[0]
            other = _rows(outs[t], (1 - c) * kh, kh)
            pltpu.make_async_remote_copy(
                src_ref=other, dst_ref=other, send_sem=send.at[t], recv_sem=recv.at[t],
                device_id=sibling, device_id_type=MESH).wait_recv()
        for rc in handed:
            rc.wait_send()

    pieces = [_piece_shape(p.shape) for p in partials]
    return pl.pallas_call(
        body, in_specs=[pl.BlockSpec(memory_space=pl.ANY)] * nt + [VMEM_SPEC] * nt, out_specs=[VMEM_SPEC] * nt,
        out_shape=[jax.ShapeDtypeStruct((2 * kh, n4), F32) for kh, n4 in pieces],
        scratch_shapes=[pltpu.VMEM(p, BF16) for p in pieces]
        + [pltpu.SemaphoreType.DMA((nt,)), pltpu.SemaphoreType.DMA((nt,)), pltpu.SemaphoreType.DMA((nt,))],
        compiler_params=pltpu.CompilerParams(vmem_limit_bytes=VMEM_LIMIT), name=name)(*partials, *lands)


VEC_SHAPE = (8, D_MODEL + LANES)
VEC_SLOTS = dict(norm1_g=(slice(0, 1), slice(0, D_MODEL)), norm2_g=(slice(1, 2), slice(0, D_MODEL)),
                 final_g=(slice(2, 3), slice(0, D_MODEL)), sgu_ln_g=(slice(3, 4), slice(0, SGU_W)),
                 sgu_ln_b=(slice(3, 4), slice(SGU_W, 2 * SGU_W)), b_spatial=(slice(0, 8), slice(D_MODEL, D_MODEL + LANES)),
                 loss=(slice(4, 5), slice(0, LANES)))
VEC_PARAMS = ("norm1_g", "norm2_g", "final_g", "sgu_ln_g", "sgu_ln_b", "b_spatial")
SMALL_PARAMS = VEC_PARAMS + ("w_spatial",)
W_SPATIAL_2D = (SGU_GROUPS * SGU_CHUNK, SGU_CHUNK)


SMALL_GRADS = VEC_PARAMS + ("loss", "w_spatial")


def _small_shape(name):
    if name == "w_spatial":
        return W_SPATIAL_2D
    rows, cols = VEC_SLOTS[name]
    return (rows.stop - rows.start, cols.stop - cols.start)


def _pack_small(dst, parts):
    dst[...] = jnp.zeros(VEC_SHAPE, F32)
    for n, ref in parts.items():
        if n in VEC_SLOTS:
            dst[VEC_SLOTS[n]] = ref[...]


def _small_start(partials):
    names = VEC_PARAMS + ("loss",)

    def body(*refs):
        _pack_small(refs[-1], dict(zip(names, refs[:-1])))

    vec = pl.pallas_call(
        body, in_specs=[VMEM_SPEC] * len(names), out_specs=VMEM_SPEC, out_shape=jax.ShapeDtypeStruct(VEC_SHAPE, F32),
        name="small_params_pack")(*[partials[n].reshape(_small_shape(n)) for n in names])
    srcs = [vec, partials["w_spatial"].reshape(W_SPATIAL_2D)]
    return _split_start("small_params_start", _broadcast_copies, N_PEERS, srcs, [(N_PEERS,) + s.shape for s in srcs])


def _small_finish(started, after, w, m, v):
    own, landed = _split_wait("small_params_wait", _broadcast_copies, *started, after)
    ng, npar = len(SMALL_GRADS), len(SMALL_PARAMS)

    def update_body(*refs):
        vec_own, ws_own, vec_slots, ws_slots = refs[:4]
        w_in, m_in, v_in = (dict(zip(SMALL_PARAMS, refs[4 + k * npar:4 + (k + 1) * npar])) for k in range(3))
        o0 = 4 + 3 * npar
        g_out = dict(zip(SMALL_GRADS, refs[o0:o0 + ng]))
        d_out, m_out, v_out = (dict(zip(SMALL_PARAMS, refs[o0 + ng + k * npar:o0 + ng + (k + 1) * npar])) for k in range(3))
        vg, vw, vm, vv = refs[o0 + ng + 3 * npar:]
        me = 4 * lax.axis_index("x") + 2 * lax.axis_index("y") + lax.axis_index("c")

        def device_sum(mine, slots, read):
            acc = None
            for i in range(N_PEERS + 1):
                k = me ^ i
                part = jnp.where(k == 0, read(mine), read(slots.at[jnp.maximum(k, 1) - 1]))
                acc = part if acc is None else acc + part
            return acc

        vg[...] = device_sum(vec_own, vec_slots, lambda ref: ref[...])
        _pack_small(vw, w_in)
        _pack_small(vm, m_in)
        _pack_small(vv, v_in)
        d_vec, m_vec, v_vec = _adamw_math(vg[...], vw[...], vm[...], vv[...])
        vw[...] = d_vec
        vm[...] = m_vec
        vv[...] = v_vec
        for n in VEC_PARAMS + ("loss",):
            g_out[n][...] = vg[VEC_SLOTS[n]]
        for n in VEC_PARAMS:
            d_out[n][...] = vw[VEC_SLOTS[n]]
            m_out[n][...] = vm[VEC_SLOTS[n]]
            v_out[n][...] = vv[VEC_SLOTS[n]]

        def spatial(r0, ck):
            rows = pl.ds(r0, ck)
            g = device_sum(ws_own, ws_slots, lambda ref: ref[rows, :])
            d_, m_, v_ = _adamw_math(g, w_in["w_spatial"][rows, :], m_in["w_spatial"][rows, :], v_in["w_spatial"][rows, :])
            g_out["w_spatial"][rows, :] = g
            d_out["w_spatial"][rows, :] = d_
            m_out["w_spatial"][rows, :] = m_
            v_out["w_spatial"][rows, :] = v_

        _for_row_chunks(W_SPATIAL_2D[0], spatial)

    ins = list(own) + list(landed)
    for src in (w, m, v):
        ins += [src[n].reshape(_small_shape(n)) for n in SMALL_PARAMS]
    out_shapes = [jax.ShapeDtypeStruct(_small_shape(n), F32) for n in SMALL_GRADS + SMALL_PARAMS * 3]
    outs = pl.pallas_call(
        update_body, in_specs=[VMEM_SPEC] * len(ins), out_specs=[VMEM_SPEC] * len(out_shapes), out_shape=out_shapes,
        scratch_shapes=[pltpu.VMEM(VEC_SHAPE, F32)] * 4, name="small_params_update")(*ins)
    grads = dict(zip(SMALL_GRADS, outs[:ng]))
    rest = [dict(zip(SMALL_PARAMS, outs[ng + k * npar:ng + (k + 1) * npar])) for k in range(3)]
    return grads, rest[0], rest[1], rest[2]


BIG = ("w_in", "w_proj_attn", "w_proj_sgu", "w_out", "w_ffn_gate", "w_ffn_up", "w_ffn_down")
COMM_GROUPS = (("w_in",), ("w_proj_attn", "w_proj_sgu", "w_out", "w_ffn_gate", "w_ffn_up", "w_ffn_down"))
WEIGHTS = ("norm1_g", "w_in", "sgu_ln_g", "sgu_ln_b", "w_spatial", "b_spatial", "w_proj_attn", "w_proj_sgu", "w_out",
           "norm2_g", "w_ffn_gate", "w_ffn_up", "w_ffn_down", "final_g")


def _cols_from_chips(g):
    return jnp.transpose(g, (1, 0, 2)).reshape(g.shape[1], N_CHIPS * g.shape[2])


def kernel(x, positions, norm1_g, w_in, sgu_ln_g, sgu_ln_b, w_spatial, b_spatial, w_proj_attn, w_proj_sgu, w_out, norm2_g, w_ffn_gate, w_ffn_up, w_ffn_down, final_g, loss_target, m_norm1_g, m_w_in, m_sgu_ln_g, m_sgu_ln_b, m_w_spatial, m_b_spatial, m_w_proj_attn, m_w_proj_sgu, m_w_out, m_norm2_g, m_w_ffn_gate, m_w_ffn_up, m_w_ffn_down, m_final_g, v_norm1_g, v_w_in, v_sgu_ln_g, v_sgu_ln_b, v_w_spatial, v_b_spatial, v_w_proj_attn, v_w_proj_sgu, v_w_out, v_norm2_g, v_w_ffn_gate, v_w_ffn_up, v_w_ffn_down, v_final_g):
    w = dict(norm1_g=norm1_g, w_in=w_in, sgu_ln_g=sgu_ln_g, sgu_ln_b=sgu_ln_b, w_spatial=w_spatial, b_spatial=b_spatial,
             w_proj_attn=w_proj_attn, w_proj_sgu=w_proj_sgu, w_out=w_out, norm2_g=norm2_g, w_ffn_gate=w_ffn_gate,
             w_ffn_up=w_ffn_up, w_ffn_down=w_ffn_down, final_g=final_g)
    m = dict(norm1_g=m_norm1_g, w_in=m_w_in, sgu_ln_g=m_sgu_ln_g, sgu_ln_b=m_sgu_ln_b, w_spatial=m_w_spatial,
             b_spatial=m_b_spatial, w_proj_attn=m_w_proj_attn, w_proj_sgu=m_w_proj_sgu, w_out=m_w_out, norm2_g=m_norm2_g,
             w_ffn_gate=m_w_ffn_gate, w_ffn_up=m_w_ffn_up, w_ffn_down=m_w_ffn_down, final_g=m_final_g)
    v = dict(norm1_g=v_norm1_g, w_in=v_w_in, sgu_ln_g=v_sgu_ln_g, sgu_ln_b=v_sgu_ln_b, w_spatial=v_w_spatial,
             b_spatial=v_b_spatial, w_proj_attn=v_w_proj_attn, w_proj_sgu=v_w_proj_sgu, w_out=v_w_out, norm2_g=v_norm2_g,
             w_ffn_gate=v_w_ffn_gate, w_ffn_up=v_w_ffn_up, w_ffn_down=v_w_ffn_down, final_g=v_final_g)
    t = x.shape[1]

    shards = {n: _ew(f"cast_{n}", lambda a: (a,), [w[n][0]], [BF16])[0] for n in BIG}
    late = COMM_GROUPS[1]
    k_in, n_in = shards["w_in"].shape
    *first, token = _split_start("gather_start_0", _gather_half_copies, 3, [shards["w_in"]], [(3, k_in // 2, n_in)])
    pending = {}

    def first_weight(after):
        srcs, filled = _split_wait("gather_wait_0", _gather_half_copies, *first, after)
        gath_in, late_shards = lax.optimization_barrier(
            (_gather_finish("gather_finish_0", srcs[0], filled[0]), [shards[n] for n in late]))
        *pending["late"], _ = _split_start(
            "gather_start_1", _gather_copies, 3, late_shards, [(N_CHIPS,) + s.shape for s in late_shards])
        return _cols_from_chips(gath_in)

    def late_weights(after):
        srcs, filled = _split_wait("gather_wait_1", _gather_copies, *pending["late"], after)
        me = 2 * lax.axis_index("x") + lax.axis_index("y")
        gath = {n: lax.dynamic_update_slice(f, s[None], (me, 0, 0)) for n, f, s in zip(late, filled, srcs)}
        return (_cols_from_chips(gath["w_proj_attn"]), _cols_from_chips(gath["w_proj_sgu"]),
                gath["w_out"].reshape(D_MODEL, D_MODEL), gath["w_ffn_gate"], gath["w_ffn_up"], gath["w_ffn_down"])

    exchanges = {}

    def on_grads(i, partials):
        if "w_out" in partials:
            partials["w_out"] = partials["w_out"].reshape(N_CHIPS, D_MODEL // N_CHIPS, D_MODEL)
        parts = [partials[n] for n in COMM_GROUPS[i]]
        *exchanges[i], started = _split_start(
            f"rs_exchange_start_{i}", _exchange_copies, N_PEERS, parts, [(N_PEERS,) + _piece_shape(p.shape) for p in parts])
        return started

    dx, _, small = _local_step(
        x[0], positions.reshape(t, 1), loss_target[0], norm1_g + token, sgu_ln_g, sgu_ln_b, w_spatial[0], b_spatial[0],
        norm2_g, final_g.reshape(1, D_MODEL), first_weight, late_weights, on_grads=on_grads)
    *small_started, small_token = _small_start(small)

    grads = {}
    for i in (1, 0):
        parts, filled = _split_wait(f"rs_exchange_wait_{i}", _exchange_copies, *exchanges[i], small_token)
        grads.update(zip(COMM_GROUPS[i], _device_sum(f"rs_device_sum_{i}", parts, filled)))

    delta, new_m, new_v, updated = {}, {}, {}, []
    for n in BIG:
        shp = w[n].shape
        flip = jnp.transpose if shp[-1] % LANES else (lambda a: a)
        outs = _adamw(f"adamw_{n}", flip(grads[n]), flip(w[n][0]), flip(m[n][0]), flip(v[n][0]))
        grads[n], delta[n], new_m[n], new_v[n] = (flip(a).reshape(shp) for a in outs)
        updated.append(outs[-1])

    g_s, d_s, m_s, v_s = _small_finish(small_started, updated, w, m, v)
    loss = g_s["loss"][0, 0]
    for n in SMALL_PARAMS:
        shp = w[n].shape
        grads[n], delta[n], new_m[n], new_v[n] = (a[n].reshape(shp) for a in (g_s, d_s, m_s, v_s))

    return (loss, dx.reshape(x.shape), *[grads[n] for n in WEIGHTS], *[delta[n] for n in WEIGHTS],
            *[new_m[n] for n in WEIGHTS], *[new_v[n] for n in WEIGHTS])
```

```python
import functools

import numpy as np
import jax
import jax.numpy as jnp
from jax import lax
from jax.experimental import pallas as pl
from jax.experimental.pallas import tpu as pltpu

F32, BF16 = jnp.float32, jnp.bfloat16
MESH = pl.DeviceIdType.MESH

D_MODEL = 1024
HEAD_DIM = 64
ATTN_W = 512
DILATIONS = (1, 4, 16)
BLK = 128
ATTN_BLOCKS_PER_STEP = 4
ROPE_DIM = 16
ROPE_THETA = 500000.0
SGU_W = 512
SGU_CHUNK = 128
SGU_GROUPS = 8
D_FF = 2816
N_CHIPS = 4
FF_SHARD = D_FF // N_CHIPS
IN_COLS = 7680
EPS = 1e-6
NEG = -1e30
LANES = 128
VMEM_LIMIT = 52 * 1024 * 1024

ADAM_LR, ADAM_B1, ADAM_B2, ADAM_EPS, ADAM_WD, ADAM_STEP = 0.001, 0.9, 0.999, 1e-08, 0.01, 10

QKV_BLOCKS = 9


def _w_in_block(part, g):
    return part * len(DILATIONS) + g


def _cparams(ngrid):
    return pltpu.CompilerParams(dimension_semantics=("arbitrary",) * ngrid, vmem_limit_bytes=VMEM_LIMIT)


def _full(shape):
    return pl.BlockSpec(shape, lambda *_: (0,) * len(shape))


def _resident(shape):
    return pl.BlockSpec(shape, lambda *_: (0,) * len(shape), pipeline_mode=pl.Buffered(1))


NT = ((1,), (1,))
TN = ((0,), (0,))


def _rope(v, cos_t, sin_t):
    half = ROPE_DIM // 2
    first = (lax.broadcasted_iota(jnp.int32, cos_t.shape, 1) % HEAD_DIM) < half
    outs = []
    for cs in range(v.shape[1] // LANES):
        x = v[:, cs * LANES:(cs + 1) * LANES]
        partner = jnp.where(first, pltpu.roll(x, LANES - half, axis=1), pltpu.roll(x, half, axis=1))
        outs.append(x * cos_t + partner * sin_t)
    return outs[0] if len(outs) == 1 else jnp.concatenate(outs, axis=1)


def _spread_heads(v2, upper):
    other = pltpu.roll(v2, HEAD_DIM, axis=1)
    h0 = jnp.where(upper, other, v2)
    h1 = jnp.where(upper, v2, other)
    return jnp.concatenate([jnp.concatenate([h0, h0], axis=1), jnp.concatenate([h1, h1], axis=1)], axis=0)


def _sigmoid(v):
    return 0.5 * jnp.tanh(0.5 * v) + 0.5


def _rms_stats(v):
    r = lax.rsqrt(jnp.mean(v * v, axis=-1, keepdims=True) + EPS)
    return v * r, r


def _rms_bwd(dy, xhat, r, g):
    dxh = dy * g
    return r * (dxh - xhat * jnp.mean(dxh * xhat, axis=-1, keepdims=True))


def _head_sum_matrix():
    idx = np.arange(ATTN_W) // HEAD_DIM
    return jnp.asarray((idx[:, None] == idx[None, :]).astype(np.float32), dtype=BF16)


def _group_sum(v, e):
    hi = v.astype(BF16)
    lo = (v - hi.astype(F32)).astype(BF16)
    return jnp.dot(hi, e, preferred_element_type=F32) + jnp.dot(lo, e, preferred_element_type=F32)


TILE = 512


def _to_slabs(slab_ref, v):
    for cs in range(slab_ref.shape[0]):
        slab_ref[cs] = v[:, cs * LANES:(cs + 1) * LANES]


def _from_slabs(slab_ref):
    return jnp.concatenate([slab_ref[cs] for cs in range(slab_ref.shape[0])], axis=1)


def _class_rows(slab_ref, r, dil):
    n = slab_ref.shape[1] // dil
    return jnp.concatenate([slab_ref.at[cs][pl.ds(r, n, stride=dil), :] for cs in range(slab_ref.shape[0])], axis=1)


def _put_class_rows(slab_ref, r, dil, v):
    n = slab_ref.shape[1] // dil
    for cs in range(slab_ref.shape[0]):
        slab_ref.at[cs][pl.ds(r, n, stride=dil), :] = v[:, cs * LANES:(cs + 1) * LANES]


def _natural_from_group(slab_ref, grp_ref):
    dil = grp_ref.shape[0]
    for r in range(dil):
        _put_class_rows(slab_ref, r, dil, grp_ref[r].astype(F32))
    return _from_slabs(slab_ref)


def _group_from_natural(slab_ref, grp_ref, v):
    dil = grp_ref.shape[0]
    _to_slabs(slab_ref, v)
    for r in range(dil):
        grp_ref[r] = _class_rows(slab_ref, r, dil).astype(grp_ref.dtype)


def _group_spec(dil, tile, width):
    return pl.BlockSpec((dil, tile // dil, width), lambda i, *_: (0, i, 0))


def _slabs(tile, width):
    return pltpu.VMEM((width // LANES, tile, LANES), F32)


def _rope_consts():
    lane = np.arange(LANES) % HEAD_DIM
    fi = lane % (ROPE_DIM // 2)
    invf = np.where(lane < ROPE_DIM, ROPE_THETA ** (-(2.0 * fi) / ROPE_DIM), 0.0)
    sgn = np.where(lane < ROPE_DIM // 2, -1.0, np.where(lane < ROPE_DIM, 1.0, 0.0))
    return (jnp.asarray(invf.astype(np.float32)).reshape(1, LANES), jnp.asarray(sgn.astype(np.float32)).reshape(1, LANES))


def _rope_tables(pos_col):
    t = pos_col.shape[0]
    tile = min(t, TILE)
    invf, sgn = _rope_consts()

    def body(p_ref, f_ref, s_ref, c0, s0, c1, s1, c2, s2, slab_c, slab_s):
        ang = p_ref[...].astype(F32) * f_ref[...]
        cos, sin = jnp.cos(ang), jnp.sin(ang) * s_ref[...]
        c0[...] = cos
        s0[...] = sin
        _group_from_natural(slab_c, c1, cos)
        _group_from_natural(slab_s, s1, sin)
        for r in range(DILATIONS[2]):
            c2[r] = _class_rows(slab_c, r, DILATIONS[2])
            s2[r] = _class_rows(slab_s, r, DILATIONS[2])

    nat = pl.BlockSpec((tile, LANES), lambda i: (i, 0))
    specs, shapes = [nat, nat], [(t, LANES)] * 2
    for d in DILATIONS[1:]:
        specs += [_group_spec(d, tile, LANES)] * 2
        shapes += [(d, t // d, LANES)] * 2
    outs = pl.pallas_call(
        body, grid=(t // tile,),
        in_specs=[pl.BlockSpec((tile, 1), lambda i: (i, 0)), _full((1, LANES)), _full((1, LANES))],
        out_specs=specs, out_shape=[jax.ShapeDtypeStruct(s, F32) for s in shapes],
        scratch_shapes=[_slabs(tile, LANES)] * 2,
        compiler_params=_cparams(1), name="rope_tables")(pos_col, invf, sgn)
    return [(outs[2 * g].reshape(t, LANES), outs[2 * g + 1].reshape(t, LANES)) for g in range(len(DILATIONS))]


def _norm_fwd(x, g):
    t = x.shape[0]
    tile = min(t, TILE)

    def body(x_ref, g_ref, h0_ref, h1_ref, h2_ref, slab):
        xhat, _ = _rms_stats(x_ref[...])
        hn = xhat * g_ref[...]
        h0_ref[...] = hn.astype(BF16)
        _group_from_natural(slab, h1_ref, hn)
        for r in range(DILATIONS[2]):
            h2_ref[r] = _class_rows(slab, r, DILATIONS[2]).astype(BF16)

    nat = pl.BlockSpec((tile, D_MODEL), lambda i: (i, 0))
    return pl.pallas_call(
        body, grid=(t // tile,),
        in_specs=[nat, _full((1, D_MODEL))],
        out_specs=[nat] + [_group_spec(d, tile, D_MODEL) for d in DILATIONS[1:]],
        out_shape=[jax.ShapeDtypeStruct((t, D_MODEL), BF16)]
        + [jax.ShapeDtypeStruct((d, t // d, D_MODEL), BF16) for d in DILATIONS[1:]],
        scratch_shapes=[_slabs(tile, D_MODEL)],
        compiler_params=_cparams(1), name="norm1_fwd")(x, g)


GU_COLS = 3072
GROUP_COLS = 1536
GU_HALF = GU_COLS // 2


def _w_in_spec(width, block):
    return pl.BlockSpec((D_MODEL, width), lambda i: (0, block), pipeline_mode=pl.Buffered(1))


def _gu_w_specs():
    first = QKV_BLOCKS * ATTN_W // GU_HALF
    return [_w_in_spec(GU_HALF, first), _w_in_spec(GU_HALF, first + 1)]


def _group_w_specs(g):
    return [_w_in_spec(ATTN_W, _w_in_block(part, g)) for part in range(3)]


def _in_proj(hs, w_in, tables):
    t = hs[0].shape[0]
    tm = min(t, 1024)

    def body_gu(h_ref, w0_ref, w1_ref, o_ref):
        h = h_ref[...]
        o_ref[:, 0:GU_HALF] = jnp.dot(h, w0_ref[...], preferred_element_type=F32).astype(BF16)
        o_ref[:, GU_HALF:] = jnp.dot(h, w1_ref[...], preferred_element_type=F32).astype(BF16)

    gu = _token_call("in_proj_gates_uv", body_gu, t, tm,
                     [(hs[0], _rows_spec(tm, D_MODEL))] + [(w_in, s) for s in _gu_w_specs()],
                     [((t, GU_COLS), BF16, _rows_spec(tm, GU_COLS))])[0]

    qkvs = []
    for g in range(len(DILATIONS)):

        def body_qkv(h_ref, wq_ref, wk_ref, wv_ref, cos_ref, sin_ref, o_ref):
            h = h_ref[...]
            cos_w, sin_w = cos_ref[...], sin_ref[...]
            q = jnp.dot(h, wq_ref[...], preferred_element_type=F32)
            o_ref[:, 0:ATTN_W] = (_rope(q, cos_w, sin_w) * HEAD_DIM ** -0.5).astype(BF16)
            k = jnp.dot(h, wk_ref[...], preferred_element_type=F32)
            o_ref[:, ATTN_W:2 * ATTN_W] = _rope(k, cos_w, sin_w).astype(BF16)
            o_ref[:, 2 * ATTN_W:] = jnp.dot(h, wv_ref[...], preferred_element_type=F32).astype(BF16)

        cos_t, sin_t = tables[g]
        qkvs.append(_token_call(
            f"in_proj_qkv_g{g}", body_qkv, t, tm,
            [(hs[g].reshape(t, D_MODEL), _rows_spec(tm, D_MODEL))] + [(w_in, s) for s in _group_w_specs(g)]
            + [(cos_t, _rows_spec(tm, LANES)), (sin_t, _rows_spec(tm, LANES))],
            [((t, GROUP_COLS), BF16, _rows_spec(tm, GROUP_COLS))])[0])
    return gu, qkvs


def _attn_masks(n):
    row = lax.broadcasted_iota(jnp.int32, (2 * BLK, 2 * BLK), 0) % BLK
    col = lax.broadcasted_iota(jnp.int32, (2 * BLK, 2 * BLK), 1)
    diff = BLK + row - col
    valid = (diff >= 0) & (diff <= BLK) & ((col >= BLK) | (n > 0))
    upper = lax.broadcasted_iota(jnp.int32, (BLK, LANES), 1) >= HEAD_DIM
    return valid, upper


def _stack_heads(v2, upper):
    zero = jnp.zeros_like(v2)
    return jnp.concatenate([jnp.where(upper, zero, v2), jnp.where(upper, v2, zero)], axis=0)


def _unstack_heads(v, upper):
    return jnp.where(upper, v[BLK:], v[:BLK])


def _attn_fwd(qkv, g, dil):
    t = qkv.shape[0]
    length = t // dil
    nb = length // BLK
    per_step = min(nb, ATTN_BLOCKS_PER_STEP)
    view = qkv.reshape(dil, length, GROUP_COLS)

    def body(q_ref, kc_ref, kp_ref, vc_ref, vp_ref, o_ref, l_ref, kwin, vwin):
        n = pl.program_id(1)
        kwin[0:BLK] = kp_ref[...]
        kwin[BLK:] = kc_ref[...]
        vwin[0:BLK] = vp_ref[...]
        vwin[BLK:] = vc_ref[...]

        def block(b, carry):
            valid, upper = _attn_masks(n * per_step + b)
            rows = pl.ds(pl.multiple_of(b * BLK, BLK), BLK)
            window = pl.ds(pl.multiple_of(b * BLK, BLK), 2 * BLK)
            slabs = [slice(p * LANES, (p + 1) * LANES) for p in range(ATTN_W // LANES)]
            ss = [lax.dot_general(_stack_heads(q_ref[rows, sl], upper), kwin[window, sl], (NT, ((), ())),
                                  preferred_element_type=F32) for sl in slabs]
            soft = []
            for s in ss:
                s = jnp.where(valid, s, NEG)
                m = jnp.max(s, axis=1, keepdims=True)
                pe = jnp.exp(s - m)
                soft.append((m, pe, jnp.sum(pe, axis=1, keepdims=True)))
            for sl, (m, pe, den) in zip(slabs, soft):
                o = jnp.dot(pe.astype(BF16), vwin[window, sl], preferred_element_type=F32) / den
                lse = jnp.broadcast_to(m + jnp.log(den), (2 * BLK, LANES))
                o_ref[rows, sl] = _unstack_heads(o, upper).astype(BF16)
                l_ref[rows, sl] = _unstack_heads(lse, upper)
            return carry

        lax.fori_loop(0, per_step, block, 0)

    rows = per_step * BLK
    cur = lambda part: pl.BlockSpec((None, rows, ATTN_W), lambda r, n: (r, n, part))
    prev = lambda part: pl.BlockSpec((None, BLK, ATTN_W), lambda r, n: (r, jnp.maximum(n * per_step - 1, 0), part))
    out_spec = pl.BlockSpec((None, rows, ATTN_W), lambda r, n: (r, n, 0))
    return pl.pallas_call(
        body, grid=(dil, nb // per_step),
        in_specs=[cur(0), cur(1), prev(1), cur(2), prev(2)],
        out_specs=[out_spec, out_spec],
        out_shape=[jax.ShapeDtypeStruct((dil, length, ATTN_W), BF16), jax.ShapeDtypeStruct((dil, length, ATTN_W), F32)],
        scratch_shapes=[pltpu.VMEM((rows + BLK, ATTN_W), BF16)] * 2,
        compiler_params=_cparams(2), name=f"attn_fwd_g{g}")(view, view, view, view, view)


def _alphas(l0, l1, l2):
    m = jnp.maximum(jnp.maximum(l0, l1), l2)
    e0, e1, e2 = jnp.exp(l0 - m), jnp.exp(l1 - m), jnp.exp(l2 - m)
    inv = 1.0 / (e0 + e1 + e2)
    return e0 * inv, e1 * inv, e2 * inv


def _natural_group_values(o_refs, l_refs, slabs):
    os_ = [o_refs[0][0].astype(F32)] + [_natural_from_group(slabs[2 * g - 2], o_refs[g]) for g in (1, 2)]
    ls_ = [l_refs[0][0]] + [_natural_from_group(slabs[2 * g - 1], l_refs[g]) for g in (1, 2)]
    return os_, ls_


def _combine_fwd(os_, ls_):
    t = os_[0].shape[1]
    tile = min(t, TILE)

    def body(o0, o1, o2, l0, l1, l2, a_ref, *slabs):
        ov, lv = _natural_group_values((o0, o1, o2), (l0, l1, l2), slabs)
        a0, a1, a2 = _alphas(*lv)
        a_ref[...] = (a0 * ov[0] + a1 * ov[1] + a2 * ov[2]).astype(BF16)

    specs = [_group_spec(d, tile, ATTN_W) for d in DILATIONS]
    return pl.pallas_call(
        body, grid=(t // tile,), in_specs=specs * 2, out_specs=pl.BlockSpec((tile, ATTN_W), lambda i: (i, 0)),
        out_shape=jax.ShapeDtypeStruct((t, ATTN_W), BF16),
        scratch_shapes=[_slabs(tile, ATTN_W)] * 4,
        compiler_params=_cparams(1), name="combine_fwd")(*os_, *ls_)


def _combine_bwd(dattn, os_, ls_):
    t = dattn.shape[0]
    tile = min(t, TILE)
    e = _head_sum_matrix()

    def body(d_ref, o0, o1, o2, l0, l1, l2, e_ref, do0, do1, do2, c0, c1, c2, *slabs):
        ov, lv = _natural_group_values((o0, o1, o2), (l0, l1, l2), slabs)
        alphas = _alphas(*lv)
        d = d_ref[...]
        attn = alphas[0] * ov[0] + alphas[1] * ov[1] + alphas[2] * ov[2]
        s = _group_sum(d * attn, e_ref[...])
        do0[0] = (alphas[0] * d).astype(BF16)
        c0[0] = -alphas[0] * s
        for g, do_ref, c_ref in ((1, do1, c1), (2, do2, c2)):
            _group_from_natural(slabs[2 * g - 2], do_ref, alphas[g] * d)
            _group_from_natural(slabs[2 * g - 1], c_ref, -alphas[g] * s)

    specs = [_group_spec(d, tile, ATTN_W) for d in DILATIONS]
    shapes = [(d, t // d, ATTN_W) for d in DILATIONS]
    outs = pl.pallas_call(
        body, grid=(t // tile,),
        in_specs=[pl.BlockSpec((tile, ATTN_W), lambda i: (i, 0))] + specs * 2 + [_full((ATTN_W, ATTN_W))],
        out_specs=specs * 2,
        out_shape=[jax.ShapeDtypeStruct(s, BF16) for s in shapes] + [jax.ShapeDtypeStruct(s, F32) for s in shapes],
        scratch_shapes=[_slabs(tile, ATTN_W)] * 4,
        compiler_params=_cparams(1), name="combine_bwd")(dattn, *os_, *ls_, e)
    return outs[:3], outs[3:]


def _attn_bwd(qkv, do, cc, lse, cos_t, sin_t, g, dil):
    t = qkv.shape[0]
    length = t // dil
    nb = length // BLK
    per_step = min(nb, ATTN_BLOCKS_PER_STEP)
    nsteps = nb // per_step
    rows_per_step = per_step * BLK
    qkv_v = qkv.reshape(dil, length, GROUP_COLS)
    cos_v, sin_v = (a.reshape(dil, length, LANES) for a in (cos_t, sin_t))
    scale = HEAD_DIM ** -0.5
    dq_cols, dk_cols, dv_cols = (slice(i * ATTN_W, (i + 1) * ATTN_W) for i in range(3))

    def body(q_ref, kc_ref, kp_ref, vc_ref, vp_ref, do_ref, c_ref, l_ref, cosc, sinc, cosp, sinp,
             out_ref, acc, kwin, vwin, cwin, swin):
        n = pl.program_id(1)

        def one_block(b):
            valid, upper = _attn_masks(n * per_step + b)
            start = b * BLK if isinstance(b, int) else pl.multiple_of(b * BLK, BLK)
            rows, before, window = pl.ds(start, BLK), pl.ds(start, BLK), pl.ds(start, 2 * BLK)
            own = pl.ds(start + BLK, BLK)
            dq_parts, dkp_parts, dkc_parts, dvp_parts, dvc_parts = [], [], [], [], []
            npairs = ATTN_W // LANES
            slabs = [slice(p * LANES, (p + 1) * LANES) for p in range(npairs)]
            qss = [_stack_heads(q_ref[rows, sl], upper) for sl in slabs]
            doss = [_stack_heads(do_ref[rows, sl], upper) for sl in slabs]
            ss = [lax.dot_general(qss[p], kwin[window, slabs[p]], (NT, ((), ())), preferred_element_type=F32) for p in range(npairs)]
            dpvs = [lax.dot_general(doss[p], vwin[window, slabs[p]], (NT, ((), ())), preferred_element_type=F32)
                    for p in range(npairs)]
            pes = [jnp.exp(jnp.where(valid, ss[p], NEG) - _spread_heads(l_ref[rows, slabs[p]], upper)) for p in range(npairs)]
            dss = [(pes[p] * (dpvs[p] + _spread_heads(c_ref[rows, slabs[p]], upper))).astype(BF16) for p in range(npairs)]
            for p in range(npairs):
                qs, dos, ds = qss[p], doss[p], dss[p]
                dq2 = _unstack_heads(jnp.dot(ds, kwin[window, slabs[p]], preferred_element_type=F32), upper)
                dk2 = lax.dot_general(ds, qs, (TN, ((), ())), preferred_element_type=F32)
                dv2 = lax.dot_general(pes[p].astype(BF16), dos, (TN, ((), ())), preferred_element_type=F32)
                dq_parts.append(dq2)
                dkp_parts.append(dk2[:BLK])
                dkc_parts.append(dk2[BLK:])
                dvp_parts.append(dv2[:BLK])
                dvc_parts.append(dv2[BLK:])
            dq = _rope(jnp.concatenate(dq_parts, axis=1) * scale, cwin[own, :], swin[own, :])
            dkc = _rope(jnp.concatenate(dkc_parts, axis=1), cwin[own, :], swin[own, :])
            dkp = _rope(jnp.concatenate(dkp_parts, axis=1), cwin[before, :], swin[before, :])
            return dq, dkp, dkc, jnp.concatenate(dvp_parts, axis=1), jnp.concatenate(dvc_parts, axis=1)

        @pl.when(n < nsteps)
        def _():
            kwin[0:BLK] = kp_ref[...]
            kwin[BLK:] = kc_ref[...]
            vwin[0:BLK] = vp_ref[...]
            vwin[BLK:] = vc_ref[...]
            cwin[0:BLK] = cosp[...]
            cwin[BLK:] = cosc[...]
            swin[0:BLK] = -sinp[...]
            swin[BLK:] = -sinc[...]
            dq, dkp, dkc, dvp, dvc = one_block(0)
            last = slice(rows_per_step - BLK, rows_per_step)

            @pl.when(n > 0)
            def _():
                if per_step > 1:
                    out_ref[0:rows_per_step - BLK, :] = acc[0:rows_per_step - BLK, :].astype(BF16)
                out_ref[last, dq_cols] = acc[last, dq_cols].astype(BF16)
                out_ref[last, dk_cols] = (acc[last, dk_cols] + dkp).astype(BF16)
                out_ref[last, dv_cols] = (acc[last, dv_cols] + dvp).astype(BF16)

            acc[0:BLK, dq_cols] = dq
            acc[0:BLK, dk_cols] = dkc
            acc[0:BLK, dv_cols] = dvc

            def later(b, carry):
                dq, dkp, dkc, dvp, dvc = one_block(b)
                start = pl.multiple_of(b * BLK, BLK)
                before, rows = pl.ds(start - BLK, BLK), pl.ds(start, BLK)
                acc[before, dk_cols] += dkp
                acc[before, dv_cols] += dvp
                acc[rows, dq_cols] = dq
                acc[rows, dk_cols] = dkc
                acc[rows, dv_cols] = dvc
                return carry

            lax.fori_loop(1, per_step, later, 0)

        @pl.when(n == flush_at)
        def _():
            out_ref[...] = acc[...].astype(BF16)

    flush_at = nsteps - 1 if nsteps == 1 else nsteps
    out_lag = 0 if nsteps == 1 else 1
    nc = lambda n: jnp.minimum(n, nsteps - 1)
    npv = lambda n: jnp.maximum(jnp.minimum(n, nsteps - 1) * per_step - 1, 0)
    cur = lambda part: pl.BlockSpec((None, rows_per_step, ATTN_W), lambda r, n: (r, nc(n), part))
    prev = lambda part: pl.BlockSpec((None, BLK, ATTN_W), lambda r, n: (r, npv(n), part))
    row = pl.BlockSpec((None, rows_per_step, ATTN_W), lambda r, n: (r, nc(n), 0))
    tab_c = pl.BlockSpec((None, rows_per_step, LANES), lambda r, n: (r, nc(n), 0))
    tab_p = pl.BlockSpec((None, BLK, LANES), lambda r, n: (r, npv(n), 0))
    out_spec = pl.BlockSpec((None, rows_per_step, GROUP_COLS), lambda r, n: (r, jnp.maximum(n - out_lag, 0), 0))
    out = pl.pallas_call(
        body, grid=(dil, nsteps + out_lag),
        in_specs=[cur(0), cur(1), prev(1), cur(2), prev(2), row, row, row, tab_c, tab_c, tab_p, tab_p],
        out_specs=out_spec,
        out_shape=jax.ShapeDtypeStruct((dil, length, GROUP_COLS), BF16),
        scratch_shapes=[pltpu.VMEM((rows_per_step, GROUP_COLS), F32)]
        + [pltpu.VMEM((rows_per_step + BLK, ATTN_W), BF16)] * 2 + [pltpu.VMEM((rows_per_step + BLK, LANES), F32)] * 2,
        compiler_params=_cparams(2), name=f"attn_bwd_g{g}")(
            qkv_v, qkv_v, qkv_v, qkv_v, qkv_v, do, cc, lse, cos_v, sin_v, cos_v, sin_v)
    return out.reshape(t, GROUP_COLS)


SQRT_HALF = 0.7071067811865476
INV_SQRT_2PI = 0.3989422804014327


def _sgu_core(uv, g, b, w_ref, bias):
    cdf = 0.5 * (1.0 + lax.erf(uv * SQRT_HALF))
    z = uv * cdf
    u, v = z[:, :SGU_W], z[:, SGU_W:]
    mu = jnp.mean(v, axis=1, keepdims=True)
    xc = v - mu
    rs = lax.rsqrt(jnp.mean(xc * xc, axis=1, keepdims=True) + EPS)
    xhat = xc * rs
    vn = xhat * g + b
    row = lax.broadcasted_iota(jnp.int32, (SGU_CHUNK, SGU_CHUNK), 0)
    col = lax.broadcasted_iota(jnp.int32, (SGU_CHUNK, SGU_CHUNK), 1)
    tril = row >= col
    upper = lax.broadcasted_iota(jnp.int32, (SGU_CHUNK, LANES), 1) >= SGU_W // SGU_GROUPS
    ws, vlo, vhi, mixed = [], [], [], []
    for pr in range(SGU_W // LANES):
        sl = slice(pr * LANES, (pr + 1) * LANES)
        w0 = jnp.where(tril, w_ref[2 * pr], 0.0).astype(BF16)
        w1 = jnp.where(tril, w_ref[2 * pr + 1], 0.0).astype(BF16)
        vn2 = vn[:, sl]
        lo = jnp.where(upper, 0.0, vn2).astype(BF16)
        hi = jnp.where(upper, vn2, 0.0).astype(BF16)
        mixed.append(jnp.dot(w0, lo, preferred_element_type=F32) + jnp.dot(w1, hi, preferred_element_type=F32)
                     + bias[:, sl])
        ws.append((w0, w1))
        vlo.append(lo)
        vhi.append(hi)
    return cdf, u, xhat, rs, jnp.concatenate(mixed, axis=1), ws, vlo, vhi, tril, upper


SGU_STEP = 4 * SGU_CHUNK


def _for_chunks(step_rows, fn):
    def one(ci, carry):
        fn(pl.ds(pl.multiple_of(ci * SGU_CHUNK, SGU_CHUNK), SGU_CHUNK))
        return carry

    lax.fori_loop(0, step_rows // SGU_CHUNK, one, 0)


def _sgu_fwd(gu, ln_g, ln_b, w_s, bias_exp):
    t = gu.shape[0]
    step = min(t, SGU_STEP)

    def body(uv_ref, g_ref, b_ref, w_ref, bias_ref, o_ref):
        def chunk(rows):
            _, u, _, _, mixed, *_ = _sgu_core(uv_ref[rows, :].astype(F32), g_ref[...], b_ref[...], w_ref, bias_ref[...])
            o_ref[rows, :] = (u * mixed).astype(BF16)

        _for_chunks(step, chunk)

    return pl.pallas_call(
        body, grid=(t // step,),
        in_specs=[pl.BlockSpec((step, 2 * SGU_W), lambda n: (n, 0)), _full((1, SGU_W)), _full((1, SGU_W)),
                  _full((SGU_GROUPS, SGU_CHUNK, SGU_CHUNK)), _full((SGU_CHUNK, SGU_W))],
        out_specs=pl.BlockSpec((step, SGU_W), lambda n: (n, 0)),
        out_shape=jax.ShapeDtypeStruct((t, SGU_W), BF16),
        compiler_params=_cparams(1), name="sgu_fwd")(gu, ln_g, ln_b, w_s, bias_exp)


def _sgu_bwd(dproj, gu, dsgu, ln_g, ln_b, w_s, bias_exp):
    t = gu.shape[0]
    step = min(t, SGU_STEP)
    nsteps = t // step
    e = _head_sum_matrix()

    def body(dp_in, uv_ref, ds_ref, g_ref, b_ref, w_ref, bias_ref, e_ref, out_ref, dw_ref, dbias_ref, dg_ref, db_ref):
        n = pl.program_id(0)

        @pl.when(n == 0)
        def _():
            dw_ref[...] = jnp.zeros(dw_ref.shape, F32)
            dbias_ref[...] = jnp.zeros(dbias_ref.shape, F32)
            dg_ref[...] = jnp.zeros(dg_ref.shape, F32)
            db_ref[...] = jnp.zeros(db_ref.shape, F32)

        _for_chunks(step, functools.partial(chunk, uv_ref, ds_ref, g_ref, b_ref, w_ref, bias_ref, out_ref, dw_ref, dbias_ref,
                                            dg_ref, db_ref))

        @pl.when(n == nsteps - 1)
        def _():
            dbias_ref[...] = _group_sum(dbias_ref[...], e_ref[...])

    def chunk(uv_ref, ds_ref, g_ref, b_ref, w_ref, bias_ref, out_ref, dw_ref, dbias_ref, dg_ref, db_ref, rows):
        uv = uv_ref[rows, :].astype(F32)
        g = g_ref[...]
        cdf, u, xhat, rs, mixed, ws, vlo, vhi, tril, upper = _sgu_core(uv, g, b_ref[...], w_ref, bias_ref[...])
        dsg = ds_ref[rows, :]
        du = dsg * mixed
        dmixed = dsg * u
        dbias_ref[...] += dmixed
        dvn = []
        for pr in range(SGU_W // LANES):
            sl = slice(pr * LANES, (pr + 1) * LANES)
            dm2 = dmixed[:, sl]
            dlo = jnp.where(upper, 0.0, dm2).astype(BF16)
            dhi = jnp.where(upper, dm2, 0.0).astype(BF16)
            w0, w1 = ws[pr]
            dvn.append(lax.dot_general(w0, dlo, (TN, ((), ())), preferred_element_type=F32)
                       + lax.dot_general(w1, dhi, (TN, ((), ())), preferred_element_type=F32))
            dw0 = lax.dot_general(dlo, vlo[pr], (NT, ((), ())), preferred_element_type=F32)
            dw1 = lax.dot_general(dhi, vhi[pr], (NT, ((), ())), preferred_element_type=F32)
            dw_ref[2 * pr] += jnp.where(tril, dw0, 0.0)
            dw_ref[2 * pr + 1] += jnp.where(tril, dw1, 0.0)
        dvn = jnp.concatenate(dvn, axis=1)
        dg_ref[...] += jnp.sum(dvn * xhat, axis=0, keepdims=True)
        db_ref[...] += jnp.sum(dvn, axis=0, keepdims=True)
        dxh = dvn * g
        dv = rs * (dxh - jnp.mean(dxh, axis=1, keepdims=True) - xhat * jnp.mean(dxh * xhat, axis=1, keepdims=True))
        dz = jnp.concatenate([du, dv], axis=1)
        dgelu = cdf + uv * (INV_SQRT_2PI * jnp.exp(-0.5 * uv * uv))
        out_ref[rows, :] = (dz * dgelu).astype(BF16)

    outs = pl.pallas_call(
        body, grid=(nsteps,),
        in_specs=[pl.BlockSpec(memory_space=pl.ANY), pl.BlockSpec((step, 2 * SGU_W), lambda n: (n, 0)),
                  pl.BlockSpec((step, SGU_W), lambda n: (n, 0)), _full((1, SGU_W)), _full((1, SGU_W)),
                  _full((SGU_GROUPS, SGU_CHUNK, SGU_CHUNK)), _full((SGU_CHUNK, SGU_W)), _full((ATTN_W, ATTN_W))],
        out_specs=[pl.BlockSpec((step, 2 * SGU_W), lambda n: (n, 0)), _full((SGU_GROUPS, SGU_CHUNK, SGU_CHUNK)),
                   _full((SGU_CHUNK, SGU_W)), _full((1, SGU_W)), _full((1, SGU_W))],
        out_shape=[jax.ShapeDtypeStruct(dproj.shape, BF16), jax.ShapeDtypeStruct((SGU_GROUPS, SGU_CHUNK, SGU_CHUNK), F32),
                   jax.ShapeDtypeStruct((SGU_CHUNK, SGU_W), F32), jax.ShapeDtypeStruct((1, SGU_W), F32),
                   jax.ShapeDtypeStruct((1, SGU_W), F32)],
        input_output_aliases={0: 0},
        compiler_params=_cparams(1), name="sgu_bwd")(dproj, gu, dsgu, ln_g, ln_b, w_s, bias_exp, e)
    return outs


def _merge_fwd(attn, sgu, gu, x, w_pa, w_ps, w_out, g2):
    t = x.shape[0]
    tm = min(t, 512)

    def body(a_ref, s_ref, ga_ref, gb_ref, x_ref, wpa, wps, wo, g_ref, pa_ref, ps_ref, m_ref, x1_ref, h2_ref):
        pa = jnp.dot(a_ref[...], wpa[...], preferred_element_type=F32)
        ps = jnp.dot(s_ref[...], wps[...], preferred_element_type=F32)
        merged = (_sigmoid(ga_ref[...].astype(F32)) * pa + _sigmoid(gb_ref[...].astype(F32)) * ps).astype(BF16)
        x1 = x_ref[...] + jnp.dot(merged, wo[...], preferred_element_type=F32)
        xhat, _ = _rms_stats(x1)
        pa_ref[...] = pa.astype(BF16)
        ps_ref[...] = ps.astype(BF16)
        m_ref[...] = merged
        x1_ref[...] = x1
        h2_ref[...] = (xhat * g_ref[...]).astype(BF16)

    half = pl.BlockSpec((tm, ATTN_W), lambda i: (i, 0))
    full = pl.BlockSpec((tm, D_MODEL), lambda i: (i, 0))
    return pl.pallas_call(
        body, grid=(t // tm,),
        in_specs=[half, half, pl.BlockSpec((tm, D_MODEL), lambda i: (i, 1)), pl.BlockSpec((tm, D_MODEL), lambda i: (i, 2)),
                  full, _resident((ATTN_W, D_MODEL)), _resident((SGU_W, D_MODEL)), _resident((D_MODEL, D_MODEL)),
                  _full((1, D_MODEL))],
        out_specs=[full] * 5,
        out_shape=[jax.ShapeDtypeStruct((t, D_MODEL), BF16), jax.ShapeDtypeStruct((t, D_MODEL), BF16),
                   jax.ShapeDtypeStruct((t, D_MODEL), BF16), jax.ShapeDtypeStruct((t, D_MODEL), F32),
                   jax.ShapeDtypeStruct((t, D_MODEL), BF16)],
        compiler_params=_cparams(1), name="merge_fwd")(attn, sgu, gu, gu, x, w_pa, w_ps, w_out, g2)


def _merge_bwd(dx1b, gu, pa, ps, w_pa, w_ps, w_out):
    t = dx1b.shape[0]
    tm = min(t, 512)

    def body(d_ref, ga_ref, gb_ref, pa_ref, ps_ref, wpa, wps, wo, out_ref, dpa_ref, dps_ref, da_ref, dsg_ref):
        dm = lax.dot_general(d_ref[...], wo[...], (NT, ((), ())), preferred_element_type=F32)
        sa, sb = _sigmoid(ga_ref[...].astype(F32)), _sigmoid(gb_ref[...].astype(F32))
        dpa = (dm * sa).astype(BF16)
        dps = (dm * sb).astype(BF16)
        out_ref[:, 0:D_MODEL] = jnp.zeros((tm, D_MODEL), BF16)
        out_ref[:, D_MODEL:2 * D_MODEL] = (dm * pa_ref[...].astype(F32) * sa * (1.0 - sa)).astype(BF16)
        out_ref[:, 2 * D_MODEL:GU_COLS] = (dm * ps_ref[...].astype(F32) * sb * (1.0 - sb)).astype(BF16)
        dpa_ref[...] = dpa
        dps_ref[...] = dps
        da_ref[...] = lax.dot_general(dpa, wpa[...], (NT, ((), ())), preferred_element_type=F32)
        dsg_ref[...] = lax.dot_general(dps, wps[...], (NT, ((), ())), preferred_element_type=F32)

    half = pl.BlockSpec((tm, ATTN_W), lambda i: (i, 0))
    full = pl.BlockSpec((tm, D_MODEL), lambda i: (i, 0))
    return pl.pallas_call(
        body, grid=(t // tm,),
        in_specs=[full, pl.BlockSpec((tm, D_MODEL), lambda i: (i, 1)),
                  pl.BlockSpec((tm, D_MODEL), lambda i: (i, 2)), full, full,
                  _resident((ATTN_W, D_MODEL)), _resident((SGU_W, D_MODEL)), _resident((D_MODEL, D_MODEL))],
        out_specs=[pl.BlockSpec((tm, GU_COLS), lambda i: (i, 0)), full, full, half, half],
        out_shape=[jax.ShapeDtypeStruct((t, GU_COLS), BF16), jax.ShapeDtypeStruct((t, D_MODEL), BF16),
                   jax.ShapeDtypeStruct((t, D_MODEL), BF16), jax.ShapeDtypeStruct((t, ATTN_W), F32),
                   jax.ShapeDtypeStruct((t, SGU_W), F32)],
        compiler_params=_cparams(1), name="merge_bwd")(dx1b, gu, gu, pa, ps, w_pa, w_ps, w_out)


def _token_call(name, body, t, tm, ins, outs, reds=(), scratch=()):
    return pl.pallas_call(
        body, grid=(t // tm,), in_specs=[s for _, s in ins],
        out_specs=[o[2] for o in outs] + [_full(r) for r in reds],
        out_shape=[jax.ShapeDtypeStruct(o[0], o[1]) for o in outs] + [jax.ShapeDtypeStruct(r, F32) for r in reds],
        scratch_shapes=list(scratch), compiler_params=_cparams(1), name=name)(*[a for a, _ in ins])


def _rows_spec(tm, width):
    return pl.BlockSpec((tm, width), lambda i: (i, 0))


def _chips_spec(tm):
    return pl.BlockSpec((N_CHIPS, tm, FF_SHARD), lambda i: (0, i, 0))


def _zero_at_start(*refs):
    @pl.when(pl.program_id(0) == 0)
    def _():
        for r in refs:
            r[...] = jnp.zeros(r.shape, r.dtype)


def _ffn_fwd(h2, w_g, w_u):
    t = h2.shape[0]
    tm = min(t, 512)

    def body(h_ref, wg_ref, wu_ref, fa_ref, fb_ref, ff_ref):
        h = h_ref[...]
        for s in range(N_CHIPS):
            a = jnp.dot(h, wg_ref[s], preferred_element_type=F32)
            b = jnp.dot(h, wu_ref[s], preferred_element_type=F32)
            sg = _sigmoid(a)
            silu = a * sg
            fa_ref[s] = (b * (sg * (1.0 + a * (1.0 - sg)))).astype(BF16)
            fb_ref[s] = silu.astype(BF16)
            ff_ref[s] = (silu * b).astype(BF16)

    shp = (N_CHIPS, t, FF_SHARD)
    w_spec = _resident((N_CHIPS, D_MODEL, FF_SHARD))
    return _token_call("ffn_fwd", body, t, tm, [(h2, _rows_spec(tm, D_MODEL)), (w_g, w_spec), (w_u, w_spec)],
                       [(shp, BF16, _chips_spec(tm))] * 3)


def _ffn_down_loss(ff, w_d, x1, tgt, gf):
    t = x1.shape[0]
    tm = min(t, 512)

    def body(ff_ref, wd_ref, x1_ref, tgt_ref, g_ref, dx2_ref, dx2b_ref, loss_ref, dgf_ref):
        _zero_at_start(loss_ref, dgf_ref)
        acc = jnp.dot(ff_ref[0], wd_ref[0], preferred_element_type=F32)
        for s in range(1, N_CHIPS):
            acc = acc + jnp.dot(ff_ref[s], wd_ref[s], preferred_element_type=F32)
        x2 = x1_ref[...] + acc
        g = g_ref[...]
        xhat, rr = _rms_stats(x2)
        diff = xhat * g - tgt_ref[...]
        rows = jnp.sum(diff * diff, axis=1, keepdims=True)
        loss_ref[...] += jnp.broadcast_to(jnp.sum(rows, axis=0, keepdims=True) * (0.5 / D_MODEL), (1, LANES))
        dy = diff * (1.0 / D_MODEL)
        dgf_ref[...] += jnp.sum(dy * xhat, axis=0, keepdims=True)
        dx2 = _rms_bwd(dy, xhat, rr, g)
        dx2_ref[...] = dx2
        dx2b_ref[...] = dx2.astype(BF16)

    row = _rows_spec(tm, D_MODEL)
    return _token_call("ffn_down_loss", body, t, tm,
                       [(ff, _chips_spec(tm)), (w_d, _resident((N_CHIPS, FF_SHARD, D_MODEL))), (x1, row), (tgt, row),
                        (gf, _full((1, D_MODEL)))],
                       [((t, D_MODEL), F32, row), ((t, D_MODEL), BF16, row)], reds=[(1, LANES), (1, D_MODEL)])


def _ffn_bwd_act(dx2b, w_d, fa, fb):
    t = dx2b.shape[0]
    tm = min(t, 512)

    def body(d_ref, wd_ref, fa_ref, fb_ref, da_ref, db_ref):
        d = d_ref[...]
        for s in range(N_CHIPS):
            dff = lax.dot_general(d, wd_ref[s], (NT, ((), ())), preferred_element_type=F32)
            da_ref[s] = (dff * fa_ref[s].astype(F32)).astype(BF16)
            db_ref[s] = (dff * fb_ref[s].astype(F32)).astype(BF16)

    shp = (N_CHIPS, t, FF_SHARD)
    return _token_call("ffn_bwd_act", body, t, tm,
                       [(dx2b, _rows_spec(tm, D_MODEL)), (w_d, _resident((N_CHIPS, FF_SHARD, D_MODEL))),
                        (fa, _chips_spec(tm)), (fb, _chips_spec(tm))],
                       [(shp, BF16, _chips_spec(tm))] * 2)


def _ffn_bwd_in(da, db, w_g, w_u, x1, dx2, g2):
    t = x1.shape[0]
    tm = min(t, 512)

    def body(da_ref, db_ref, wg_ref, wu_ref, x1_ref, dx2_ref, g_ref, dx1_ref, dx1b_ref, dg_ref):
        _zero_at_start(dg_ref)
        acc = None
        for s in range(N_CHIPS):
            part = (lax.dot_general(da_ref[s], wg_ref[s], (NT, ((), ())), preferred_element_type=F32)
                    + lax.dot_general(db_ref[s], wu_ref[s], (NT, ((), ())), preferred_element_type=F32))
            acc = part if acc is None else acc + part
        xhat, rr = _rms_stats(x1_ref[...])
        dg_ref[...] += jnp.sum(acc * xhat, axis=0, keepdims=True)
        dx1 = dx2_ref[...] + _rms_bwd(acc, xhat, rr, g_ref[...])
        dx1_ref[...] = dx1
        dx1b_ref[...] = dx1.astype(BF16)

    row = _rows_spec(tm, D_MODEL)
    w_spec = _resident((N_CHIPS, D_MODEL, FF_SHARD))
    return _token_call("ffn_bwd_in", body, t, tm,
                       [(da, _chips_spec(tm)), (db, _chips_spec(tm)), (w_g, w_spec), (w_u, w_spec), (x1, row), (dx2, row),
                        (g2, _full((1, D_MODEL)))],
                       [((t, D_MODEL), F32, row), ((t, D_MODEL), BF16, row)], reds=[(1, D_MODEL)])


def _group_dh(d, w_refs):
    dh = None
    for part, w_ref in enumerate(w_refs):
        term = lax.dot_general(d[:, part * ATTN_W:(part + 1) * ATTN_W], w_ref[...], (NT, ((), ())),
                               preferred_element_type=F32)
        dh = term if dh is None else dh + term
    return dh


def _in_proj_bwd(dgu, dqkvs, w_in, x, dx1, g1):
    t = x.shape[0]
    tile = min(t, TILE)
    ngroups = len(DILATIONS)

    def body(*refs):
        dgu_ref, dq_refs = refs[0], refs[1:1 + ngroups]
        w0_ref, w1_ref = refs[1 + ngroups:3 + ngroups]
        wg_refs = [refs[3 + ngroups + 3 * g:6 + ngroups + 3 * g] for g in range(ngroups)]
        x_ref, dx1_ref, g_ref, dx_ref, dg_ref = refs[3 + 4 * ngroups:5 + 4 * ngroups + 3]
        slabs = refs[5 + 4 * ngroups + 3:]
        _zero_at_start(dg_ref)
        for g in range(1, ngroups):
            dil = DILATIONS[g]
            part = _group_dh(dq_refs[g][...].reshape(tile, GROUP_COLS), wg_refs[g])
            for r in range(dil):
                _put_class_rows(slabs[g - 1], r, dil, part[r * (tile // dil):(r + 1) * (tile // dil)])
        dh = lax.dot_general(dgu_ref[:, 0:GU_HALF], w0_ref[...], (NT, ((), ())), preferred_element_type=F32)
        dh = dh + lax.dot_general(dgu_ref[:, GU_HALF:], w1_ref[...], (NT, ((), ())), preferred_element_type=F32)
        dh = dh + _group_dh(dq_refs[0][0], wg_refs[0])
        for slab in slabs:
            dh = dh + _from_slabs(slab)
        xhat, rr = _rms_stats(x_ref[...])
        dg_ref[...] += jnp.sum(dh * xhat, axis=0, keepdims=True)
        dx_ref[...] = dx1_ref[...] + _rms_bwd(dh, xhat, rr, g_ref[...])

    row = _rows_spec(tile, D_MODEL)
    group_ins = [(dqkvs[g].reshape(d, t // d, GROUP_COLS), _group_spec(d, tile, GROUP_COLS)) for g, d in enumerate(DILATIONS)]
    w_specs = _gu_w_specs() + [s for g in range(ngroups) for s in _group_w_specs(g)]
    return _token_call(
        "in_proj_bwd", body, t, tile,
        [(dgu, _rows_spec(tile, GU_COLS))] + group_ins + [(w_in, s) for s in w_specs]
        + [(x, row), (dx1, row), (g1, _full((1, D_MODEL)))],
        [((t, D_MODEL), F32, row)], reds=[(1, D_MODEL)], scratch=[_slabs(tile, D_MODEL)] * (ngroups - 1))


WGRAD_TK = 2048


def _wgrad_mm(name, grid, a, a_spec, b, b_spec, acc_shape, out_shape, out_spec, dst=None):
    nk = grid[-1]

    def body(*refs):
        a_ref, b_ref, o_ref, acc_ref = refs[0], refs[1], refs[-2], refs[-1]
        k = pl.program_id(len(grid) - 1)
        part = lax.dot_general(a_ref[...], b_ref[...], (TN, ((), ())), preferred_element_type=F32)

        @pl.when(k == 0)
        def _():
            acc_ref[...] = part

        @pl.when(k > 0)
        def _():
            acc_ref[...] += part

        @pl.when(k == nk - 1)
        def _():
            o_ref[...] = acc_ref[...].astype(BF16)

    filled = [] if dst is None else [dst]
    return pl.pallas_call(
        body, grid=grid, in_specs=[a_spec, b_spec] + [pl.BlockSpec(memory_space=pl.ANY)] * len(filled),
        out_specs=out_spec, out_shape=jax.ShapeDtypeStruct(out_shape, BF16), scratch_shapes=[pltpu.VMEM(acc_shape, F32)],
        input_output_aliases={2: 0} if filled else {}, compiler_params=_cparams(len(grid)), name=name)(a, b, *filled)


def _wgrad_2d(name, a, b, tm, tn):
    t, k1 = a.shape
    n = b.shape[1]
    tk = min(t, WGRAD_TK)
    return _wgrad_mm(name, (k1 // tm, n // tn, t // tk), a, pl.BlockSpec((tk, tm), lambda i, j, k: (k, i)),
                     b, pl.BlockSpec((tk, tn), lambda i, j, k: (k, j)), (tm, tn), (k1, n),
                     pl.BlockSpec((tm, tn), lambda i, j, k: (i, j)))


def _wgrad_in(hs, dgu, dqkvs):
    t = dgu.shape[0]
    tk = min(t, WGRAD_TK)
    gu_block = QKV_BLOCKS * ATTN_W // GU_HALF
    parts = [(hs[0], dgu, GU_HALF, lambda j: j + gu_block)]
    parts += [(hs[g].reshape(t, D_MODEL), dqkvs[g], ATTN_W, lambda j, g=g: _w_in_block(j, g)) for g in range(3)]
    dst = None
    for n, (a, b, tn, block_of) in enumerate(parts):
        dst = _wgrad_mm(f"wgrad_in_{n}", (1, b.shape[1] // tn, t // tk),
                        a, pl.BlockSpec((tk, D_MODEL), lambda i, j, k: (k, 0)), b, pl.BlockSpec((tk, tn), lambda i, j, k: (k, j)),
                        (D_MODEL, tn), (D_MODEL, IN_COLS),
                        pl.BlockSpec((D_MODEL, tn), lambda i, j, k, block_of=block_of: (0, block_of(j))), dst=dst)
    return dst


def _wgrad_ff_in(name, h2, da):
    t = h2.shape[0]
    tk = min(t, WGRAD_TK)
    return _wgrad_mm(name, (N_CHIPS, 1, t // tk), h2, pl.BlockSpec((tk, D_MODEL), lambda i, j, k: (k, 0)),
                     da, pl.BlockSpec((None, tk, FF_SHARD), lambda i, j, k: (i, k, 0)), (D_MODEL, FF_SHARD),
                     (N_CHIPS, D_MODEL, FF_SHARD), pl.BlockSpec((None, D_MODEL, FF_SHARD), lambda i, j, k: (i, 0, 0)))


def _wgrad_ff_down(ff, dx2b):
    t = dx2b.shape[0]
    tk = min(t, WGRAD_TK)
    return _wgrad_mm("wgrad_ffn_down", (N_CHIPS, 1, t // tk), ff, pl.BlockSpec((None, tk, FF_SHARD), lambda i, j, k: (i, k, 0)),
                     dx2b, pl.BlockSpec((tk, D_MODEL), lambda i, j, k: (k, 0)), (FF_SHARD, D_MODEL),
                     (N_CHIPS, FF_SHARD, D_MODEL), pl.BlockSpec((None, FF_SHARD, D_MODEL), lambda i, j, k: (i, 0, 0)))


def _local_step(x, pos_col, tgt, g1, ln_g, ln_b, w_s, b_s, g2, gf, first_weight, late_weights, on_grads=None):
    tables = _rope_tables(pos_col)
    bias_exp = jnp.repeat(jnp.transpose(b_s), SGU_W // SGU_GROUPS, axis=1)

    hs = _norm_fwd(x, g1)
    w_p = first_weight(hs[0])
    gu, qkvs = _in_proj(hs, w_p, tables)
    os_, ls_ = [], []
    for g, dil in enumerate(DILATIONS):
        o, lse = _attn_fwd(qkvs[g], g, dil)
        os_.append(o)
        ls_.append(lse)
    attn = _combine_fwd(os_, ls_)
    sgu = _sgu_fwd(gu, ln_g, ln_b, w_s, bias_exp)
    w_pa, w_ps, w_out, w_g, w_u, w_d = late_weights(attn)
    pa, ps, merged, x1, h2 = _merge_fwd(attn, sgu, gu, x, w_pa, w_ps, w_out, g2)
    fa, fb, ff = _ffn_fwd(h2, w_g, w_u)
    dx2, dx2b, loss, dgf = _ffn_down_loss(ff, w_d, x1, tgt, gf)

    da, db = _ffn_bwd_act(dx2b, w_d, fa, fb)
    dw_d = _wgrad_ff_down(ff, dx2b)
    dx1, dx1b, dg2 = _ffn_bwd_in(da, db, w_g, w_u, x1, dx2, g2)
    dw_g = _wgrad_ff_in("wgrad_ffn_gate", h2, da)
    dw_u = _wgrad_ff_in("wgrad_ffn_up", h2, db)

    dgu, dpa, dps, dattn, dsgu = _merge_bwd(dx1b, gu, pa, ps, w_pa, w_ps, w_out)
    dw_out = _wgrad_2d("wgrad_out", merged, dx1b, D_MODEL, D_MODEL)
    dw_pa = _wgrad_2d("wgrad_proj_attn", attn, dpa, ATTN_W, D_MODEL)
    dw_ps = _wgrad_2d("wgrad_proj_sgu", sgu, dps, SGU_W, D_MODEL)
    if on_grads is not None:
        ln_g = ln_g + on_grads(1, dict(w_proj_attn=dw_pa, w_proj_sgu=dw_ps, w_out=dw_out, w_ffn_gate=dw_g, w_ffn_up=dw_u,
                                       w_ffn_down=dw_d))[:, :SGU_W]
    dgu, dw_s, dbias, dln_g, dln_b = _sgu_bwd(dgu, gu, dsgu, ln_g, ln_b, w_s, bias_exp)
    dos, ccs = _combine_bwd(dattn, os_, ls_)
    dqkvs = [_attn_bwd(qkvs[g], dos[g], ccs[g], ls_[g], *tables[g], g, dil) for g, dil in enumerate(DILATIONS)]
    dw_p = _wgrad_in(hs, dgu, dqkvs)
    if on_grads is not None:
        g1 = g1 + on_grads(0, dict(w_in=dw_p))
    dx, dg1 = _in_proj_bwd(dgu, dqkvs, w_p, x, dx1, g1)

    db_s = jnp.transpose(dbias[:, ::SGU_W // SGU_GROUPS])
    small = dict(loss=loss, norm1_g=dg1, sgu_ln_g=dln_g, sgu_ln_b=dln_b, w_spatial=dw_s, b_spatial=db_s,
                 norm2_g=dg2, final_g=dgf)
    big = dict(w_in=dw_p, w_proj_attn=dw_pa, w_proj_sgu=dw_ps, w_out=dw_out, w_ffn_gate=dw_g, w_ffn_up=dw_u,
               w_ffn_down=dw_d)
    return dx, big, small


def _ew(name, fn, ins, out_dtypes):
    shp = ins[0].shape
    rows, cols = shp
    tr = next((cand for cand in (256, 352, 128) if rows % cand == 0 and rows > cand), rows)

    def body(*refs):
        res = fn(*[r[...] for r in refs[:len(ins)]])
        for o_ref, v in zip(refs[len(ins):], res):
            o_ref[...] = v.astype(o_ref.dtype)

    spec = pl.BlockSpec((tr, cols), lambda i: (i, 0))
    return pl.pallas_call(
        body, grid=(rows // tr,), in_specs=[spec] * len(ins), out_specs=[spec] * len(out_dtypes),
        out_shape=[jax.ShapeDtypeStruct(shp, d) for d in out_dtypes],
        compiler_params=_cparams(1), name=name)(*ins)


def _adamw_math(g, w, m, v):
    m = ADAM_B1 * m + (1.0 - ADAM_B1) * g
    v = ADAM_B2 * v + (1.0 - ADAM_B2) * (g * g)
    m_hat = m / (1.0 - ADAM_B1 ** ADAM_STEP)
    v_hat = v / (1.0 - ADAM_B2 ** ADAM_STEP)
    delta = -ADAM_LR * (m_hat / (jnp.sqrt(v_hat) + ADAM_EPS) + ADAM_WD * w)
    return delta, m, v


def _adamw(name, g, w, m, v):
    return _ew(name, lambda g_, w_, m_, v_: (g_,) + _adamw_math(g_, w_, m_, v_), [g, w, m, v], [F32] * 4)


VMEM_SPEC = pl.BlockSpec(memory_space=pltpu.VMEM)


def _for_row_chunks(rows, fn):
    ck = next(c for c in (64, 32, 16) if rows % c == 0)

    def step(i, carry):
        fn(pl.multiple_of(i * ck, ck), ck)
        return carry

    lax.fori_loop(0, rows // ck, step, 0)


def _place():
    x, y, c = lax.axis_index("x"), lax.axis_index("y"), lax.axis_index("c")
    chips = [(1 - x, y), (x, 1 - y), (1 - x, 1 - y)]
    return x, y, c, 2 * x + y, chips


def _rows(ref, start, size):
    if len(ref.shape) == 2:
        return ref.at[pl.ds(start, size), :]
    return ref.at[:, pl.ds(start, size), :]


def _comm_call(name, body, ins, out_shapes, scratch, n_remote):
    return pl.pallas_call(
        body, in_specs=[VMEM_SPEC] * len(ins), out_specs=[VMEM_SPEC] * len(out_shapes),
        out_shape=out_shapes,
        scratch_shapes=list(scratch) + [pltpu.SemaphoreType.DMA((n_remote,)), pltpu.SemaphoreType.DMA((n_remote,))],
        compiler_params=pltpu.CompilerParams(vmem_limit_bytes=VMEM_LIMIT), name=name)(*ins)


def _gather_finish(name, shard, landed):
    k_rows, n = shard.shape
    kh = k_rows // 2

    def body(shard_ref, land_ref, out_ref, send, recv):
        x, y, c, me, chips = _place()
        passed = []
        for j, chip in enumerate(chips):
            theirs = 2 * chip[0] + chip[1]
            cp = pltpu.make_async_remote_copy(
                src_ref=land_ref.at[j], dst_ref=_rows(out_ref.at[theirs], c * kh, kh), send_sem=send.at[j],
                recv_sem=recv.at[j], device_id=(x, y, 1 - c), device_id_type=MESH)
            cp.start()
            passed.append(cp)
        mine = out_ref.at[me]

        def put_own(r0, ck):
            mine[pl.ds(r0, ck), :] = shard_ref[pl.ds(r0, ck), :]

        _for_row_chunks(k_rows, put_own)
        for j, chip in enumerate(chips):
            slot = out_ref.at[2 * chip[0] + chip[1]]

            def put_half(r0, ck, j=j, slot=slot):
                slot[pl.ds(pl.multiple_of(c * kh + r0, ck), ck), :] = land_ref[j, pl.ds(r0, ck), :]

            _for_row_chunks(kh, put_half)
        for j, chip in enumerate(chips):
            other = _rows(out_ref.at[2 * chip[0] + chip[1]], (1 - c) * kh, kh)
            pltpu.make_async_remote_copy(src_ref=other, dst_ref=other, send_sem=send.at[j], recv_sem=recv.at[j],
                                         device_id=(x, y, 1 - c), device_id_type=MESH).wait_recv()
        for cp in passed:
            cp.wait_send()

    return _comm_call(name, body, [shard, landed], [jax.ShapeDtypeStruct((N_CHIPS, k_rows, n), shard.dtype)], [], 3)[0]


HBM_SPEC = pl.BlockSpec(memory_space=pltpu.HBM)
SEM_SPEC = pl.BlockSpec(memory_space=pltpu.SEMAPHORE)
DATAFLOW = pltpu.SideEffectType.DATAFLOW_SIDE_EFFECTING
TOKEN_SHAPE = (1, D_MODEL)
N_PEERS = 7
SUM_SPLIT = 4
SUM_SPLIT_ELEMS = 512 * 1024


def _peers():
    x, y, c = lax.axis_index("x"), lax.axis_index("y"), lax.axis_index("c")
    flip = lambda v, f: 1 - v if f else v
    return [(flip(x, k & 4), flip(y, k & 2), flip(c, k & 1)) for k in range(1, N_PEERS + 1)]


def _piece_shape(shape):
    return (shape[-2] // 2, shape[2] if len(shape) == 3 else shape[1] // N_CHIPS)


def _device_piece(ref, chip, core):
    kh, n4 = _piece_shape(ref.shape)
    if len(ref.shape) == 3:
        return ref.at[chip, pl.ds(core * kh, kh), :]
    return ref.at[pl.ds(core * kh, kh), pl.ds(chip * n4, n4)]


def _exchange_copies(partials, lands, send, recv):
    return [pltpu.make_async_remote_copy(
        src_ref=_device_piece(partials[t], 2 * px + py, pc), dst_ref=lands[t].at[k], send_sem=send.at[t * N_PEERS + k],
        recv_sem=recv.at[t * N_PEERS + k], device_id=(px, py, pc), device_id_type=MESH)
        for t in range(len(partials)) for k, (px, py, pc) in enumerate(_peers())]


def _broadcast_copies(srcs, lands, send, recv):
    return [pltpu.make_async_remote_copy(
        src_ref=srcs[t], dst_ref=lands[t].at[k], send_sem=send.at[t * N_PEERS + k], recv_sem=recv.at[t * N_PEERS + k],
        device_id=peer, device_id_type=MESH)
        for t in range(len(srcs)) for k, peer in enumerate(_peers())]


def _gather_copies(shards, lands, send, recv):
    x, y, c, me, chips = _place()
    return [pltpu.make_async_remote_copy(
        src_ref=shards[t], dst_ref=lands[t].at[me], send_sem=send.at[t * 3 + j], recv_sem=recv.at[t * 3 + j],
        device_id=(*chip, c), device_id_type=MESH)
        for t in range(len(shards)) for j, chip in enumerate(chips)]


def _gather_half_copies(shards, lands, send, recv):
    x, y, c, me, chips = _place()
    return [pltpu.make_async_remote_copy(
        src_ref=_rows(shards[t], c * (shards[t].shape[0] // 2), shards[t].shape[0] // 2), dst_ref=lands[t].at[j],
        send_sem=send.at[t * 3 + j], recv_sem=recv.at[t * 3 + j], device_id=(*chip, c), device_id_type=MESH)
        for t in range(len(shards)) for j, chip in enumerate(chips)]


def _split_start(name, copies, per_tensor, srcs, land_shapes):
    nt = len(srcs)
    lands = [lax.empty(s, a.dtype) for s, a in zip(land_shapes, srcs)]
    nsem = nt * per_tensor

    def body(*refs):
        send, recv = refs[2 * nt], refs[2 * nt + 1]
        for cp in copies(refs[:nt], refs[nt:2 * nt], send, recv):
            cp.start()
        refs[-1][...] = jnp.zeros(TOKEN_SHAPE, F32)

    hbm = lambda a: pltpu.with_memory_space_constraint(a, pltpu.HBM)
    outs = pl.pallas_call(
        body, name=name,
        out_shape=[pltpu.SemaphoreType.DMA((nsem,)), pltpu.SemaphoreType.DMA((nsem,))]
        + [pltpu.HBM(s.shape, s.dtype) for s in srcs] + [pltpu.HBM(l.shape, l.dtype) for l in lands]
        + [jax.ShapeDtypeStruct(TOKEN_SHAPE, F32)],
        in_specs=[HBM_SPEC] * (2 * nt), out_specs=[SEM_SPEC, SEM_SPEC] + [HBM_SPEC] * (2 * nt) + [VMEM_SPEC],
        input_output_aliases={i: 2 + i for i in range(2 * nt)},
        compiler_params=pltpu.CompilerParams(has_side_effects=DATAFLOW))(*[hbm(a) for a in list(srcs) + lands])
    return outs[0], outs[1], outs[2:2 + nt], outs[2 + nt:2 + 2 * nt], outs[-1]


def _split_wait(name, copies, send, recv, srcs, lands, after):
    nt = len(srcs)
    after = list(after) if isinstance(after, (list, tuple)) else [after]

    def body(*refs):
        for cp in copies(refs[:nt], refs[nt:2 * nt], refs[2 * nt], refs[2 * nt + 1]):
            cp.wait_send()
            cp.wait_recv()

    outs = pl.pallas_call(
        body, name=name,
        out_shape=[pltpu.HBM(s.shape, s.dtype) for s in srcs] + [pltpu.HBM(l.shape, l.dtype) for l in lands],
        in_specs=[HBM_SPEC] * (2 * nt) + [SEM_SPEC, SEM_SPEC] + [pl.BlockSpec(memory_space=pl.ANY)] * len(after),
        out_specs=[HBM_SPEC] * (2 * nt), input_output_aliases={i: i for i in range(2 * nt)},
        compiler_params=pltpu.CompilerParams(has_side_effects=DATAFLOW))(*srcs, *lands, send, recv, *after)
    return outs[:nt], outs[nt:]


def _device_sum(name, partials, lands):
    nt = len(partials)
    pieces = [_piece_shape(p.shape) for p in partials]
    units = []
    for t, (kh, n4) in enumerate(pieces):
        split = SUM_SPLIT if kh * n4 >= SUM_SPLIT_ELEMS else 1
        units += [(t, j * (kh // split), kh // split) for j in range(split)]
    nu = len(units)

    def body(*refs):
        ins, slots, outs = refs[:nt], refs[nt:2 * nt], refs[2 * nt:3 * nt]
        owns, landed, sums = refs[3 * nt:4 * nt], refs[4 * nt:5 * nt], refs[5 * nt:6 * nt]
        loc, send, recv = refs[6 * nt:]
        x, y, c, me, chips = _place()
        sibling = (x, y, 1 - c)
        loads = []
        for u, (t, r0, rows) in enumerate(units):
            loads.append((
                pltpu.make_async_copy(_rows(_device_piece(ins[t], me, c), r0, rows), _rows(owns[t], r0, rows), loc.at[0, u]),
                pltpu.make_async_copy(_rows(slots[t], r0, rows), _rows(landed[t], r0, rows), loc.at[1, u])))
            for cp in loads[-1]:
                cp.start()
        stores = []
        for u, (t, r0, rows) in enumerate(units):
            for cp in loads[u]:
                cp.wait()

            def add(q0, ck, own=owns[t], slot=landed[t], dst=sums[t], r0=r0):
                at = pl.ds(pl.multiple_of(r0 + q0, ck), ck)
                acc = own[at, :].astype(F32)
                for k in range(N_PEERS):
                    acc = acc + slot[k, at, :].astype(F32)
                dst[at, :] = acc

            _for_row_chunks(rows, add)
            mine = _rows(outs[t], c * pieces[t][0] + r0, rows)
            stores.append((
                pltpu.make_async_copy(_rows(sums[t], r0, rows), mine, loc.at[2, u]),
                pltpu.make_async_remote_copy(src_ref=_rows(sums[t], r0, rows), dst_ref=mine, send_sem=send.at[u],
                                             recv_sem=recv.at[u], device_id=sibling, device_id_type=MESH)))
            for cp in stores[-1]:
                cp.start()
        for u, (t, r0, rows) in enumerate(units):
            pltpu.make_async_remote_copy(
                src_ref=_rows(sums[t], r0, rows), dst_ref=_rows(outs[t], (1 - c) * pieces[t][0] + r0, rows),
                send_sem=send.at[u], recv_sem=recv.at[u], device_id=sibling, device_id_type=MESH).wait_recv()
            stores[u][0].wait()
            stores[u][1].wait_send()

    any_spec = pl.BlockSpec(memory_space=pl.ANY)
    return pl.pallas_call(
        body, in_specs=[any_spec] * (2 * nt), out_specs=[any_spec] * nt,
        out_shape=[jax.ShapeDtypeStruct((2 * kh, n4), F32) for kh, n4 in pieces],
        scratch_shapes=[pltpu.VMEM(p, BF16) for p in pieces] + [pltpu.VMEM((N_PEERS,) + p, BF16) for p in pieces]
        + [pltpu.VMEM(p, F32) for p in pieces]
        + [pltpu.SemaphoreType.DMA((3, nu)), pltpu.SemaphoreType.DMA((nu,)), pltpu.SemaphoreType.DMA((nu,))],
        compiler_params=pltpu.CompilerParams(vmem_limit_bytes=VMEM_LIMIT), name=name)(*partials, *lands)


VEC_SHAPE = (8, D_MODEL + LANES)
VEC_SLOTS = dict(norm1_g=(slice(0, 1), slice(0, D_MODEL)), norm2_g=(slice(1, 2), slice(0, D_MODEL)),
                 final_g=(slice(2, 3), slice(0, D_MODEL)), sgu_ln_g=(slice(3, 4), slice(0, SGU_W)),
                 sgu_ln_b=(slice(3, 4), slice(SGU_W, 2 * SGU_W)), b_spatial=(slice(0, 8), slice(D_MODEL, D_MODEL + LANES)),
                 loss=(slice(4, 5), slice(0, LANES)))
VEC_PARAMS = ("norm1_g", "norm2_g", "final_g", "sgu_ln_g", "sgu_ln_b", "b_spatial")
SMALL_PARAMS = VEC_PARAMS + ("w_spatial",)
W_SPATIAL_2D = (SGU_GROUPS * SGU_CHUNK, SGU_CHUNK)


SMALL_GRADS = VEC_PARAMS + ("loss", "w_spatial")


def _small_shape(name):
    if name == "w_spatial":
        return W_SPATIAL_2D
    rows, cols = VEC_SLOTS[name]
    return (rows.stop - rows.start, cols.stop - cols.start)


def _pack_small(dst, parts):
    dst[...] = jnp.zeros(VEC_SHAPE, F32)
    for n, ref in parts.items():
        if n in VEC_SLOTS:
            dst[VEC_SLOTS[n]] = ref[...]


def _small_start(partials):
    names = VEC_PARAMS + ("loss",)

    def body(*refs):
        _pack_small(refs[-1], dict(zip(names, refs[:-1])))

    vec = pl.pallas_call(
        body, in_specs=[VMEM_SPEC] * len(names), out_specs=VMEM_SPEC, out_shape=jax.ShapeDtypeStruct(VEC_SHAPE, F32),
        name="small_params_pack")(*[partials[n].reshape(_small_shape(n)) for n in names])
    srcs = [vec, partials["w_spatial"].reshape(W_SPATIAL_2D)]
    return _split_start("small_params_start", _broadcast_copies, N_PEERS, srcs, [(N_PEERS,) + s.shape for s in srcs])


def _small_finish(started, after, w, m, v):
    own, landed = _split_wait("small_params_wait", _broadcast_copies, *started, after)
    ng, npar = len(SMALL_GRADS), len(SMALL_PARAMS)

    def update_body(*refs):
        vec_own, ws_own, vec_slots, ws_slots = refs[:4]
        w_in, m_in, v_in = (dict(zip(SMALL_PARAMS, refs[4 + k * npar:4 + (k + 1) * npar])) for k in range(3))
        o0 = 4 + 3 * npar
        g_out = dict(zip(SMALL_GRADS, refs[o0:o0 + ng]))
        d_out, m_out, v_out = (dict(zip(SMALL_PARAMS, refs[o0 + ng + k * npar:o0 + ng + (k + 1) * npar])) for k in range(3))
        vg, vw, vm, vv = refs[o0 + ng + 3 * npar:]
        me = 4 * lax.axis_index("x") + 2 * lax.axis_index("y") + lax.axis_index("c")

        def device_sum(mine, slots, read):
            acc = None
            for i in range(N_PEERS + 1):
                k = me ^ i
                part = jnp.where(k == 0, read(mine), read(slots.at[jnp.maximum(k, 1) - 1]))
                acc = part if acc is None else acc + part
            return acc

        vg[...] = device_sum(vec_own, vec_slots, lambda ref: ref[...])
        _pack_small(vw, w_in)
        _pack_small(vm, m_in)
        _pack_small(vv, v_in)
        d_vec, m_vec, v_vec = _adamw_math(vg[...], vw[...], vm[...], vv[...])
        vw[...] = d_vec
        vm[...] = m_vec
        vv[...] = v_vec
        for n in VEC_PARAMS + ("loss",):
            g_out[n][...] = vg[VEC_SLOTS[n]]
        for n in VEC_PARAMS:
            d_out[n][...] = vw[VEC_SLOTS[n]]
            m_out[n][...] = vm[VEC_SLOTS[n]]
            v_out[n][...] = vv[VEC_SLOTS[n]]

        def spatial(r0, ck):
            rows = pl.ds(r0, ck)
            g = device_sum(ws_own, ws_slots, lambda ref: ref[rows, :])
            d_, m_, v_ = _adamw_math(g, w_in["w_spatial"][rows, :], m_in["w_spatial"][rows, :], v_in["w_spatial"][rows, :])
            g_out["w_spatial"][rows, :] = g
            d_out["w_spatial"][rows, :] = d_
            m_out["w_spatial"][rows, :] = m_
            v_out["w_spatial"][rows, :] = v_

        _for_row_chunks(W_SPATIAL_2D[0], spatial)

    ins = list(own) + list(landed)
    for src in (w, m, v):
        ins += [src[n].reshape(_small_shape(n)) for n in SMALL_PARAMS]
    out_shapes = [jax.ShapeDtypeStruct(_small_shape(n), F32) for n in SMALL_GRADS + SMALL_PARAMS * 3]
    outs = pl.pallas_call(
        update_body, in_specs=[VMEM_SPEC] * len(ins), out_specs=[VMEM_SPEC] * len(out_shapes), out_shape=out_shapes,
        scratch_shapes=[pltpu.VMEM(VEC_SHAPE, F32)] * 4, name="small_params_update")(*ins)
    grads = dict(zip(SMALL_GRADS, outs[:ng]))
    rest = [dict(zip(SMALL_PARAMS, outs[ng + k * npar:ng + (k + 1) * npar])) for k in range(3)]
    return grads, rest[0], rest[1], rest[2]


BIG = ("w_in", "w_proj_attn", "w_proj_sgu", "w_out", "w_ffn_gate", "w_ffn_up", "w_ffn_down")
COMM_GROUPS = (("w_in",), ("w_proj_attn", "w_proj_sgu", "w_out", "w_ffn_gate", "w_ffn_up", "w_ffn_down"))
WEIGHTS = ("norm1_g", "w_in", "sgu_ln_g", "sgu_ln_b", "w_spatial", "b_spatial", "w_proj_attn", "w_proj_sgu", "w_out",
           "norm2_g", "w_ffn_gate", "w_ffn_up", "w_ffn_down", "final_g")


def _cols_from_chips(g):
    return jnp.transpose(g, (1, 0, 2)).reshape(g.shape[1], N_CHIPS * g.shape[2])


def kernel(x, positions, norm1_g, w_in, sgu_ln_g, sgu_ln_b, w_spatial, b_spatial, w_proj_attn, w_proj_sgu, w_out, norm2_g, w_ffn_gate, w_ffn_up, w_ffn_down, final_g, loss_target, m_norm1_g, m_w_in, m_sgu_ln_g, m_sgu_ln_b, m_w_spatial, m_b_spatial, m_w_proj_attn, m_w_proj_sgu, m_w_out, m_norm2_g, m_w_ffn_gate, m_w_ffn_up, m_w_ffn_down, m_final_g, v_norm1_g, v_w_in, v_sgu_ln_g, v_sgu_ln_b, v_w_spatial, v_b_spatial, v_w_proj_attn, v_w_proj_sgu, v_w_out, v_norm2_g, v_w_ffn_gate, v_w_ffn_up, v_w_ffn_down, v_final_g):
    w = dict(norm1_g=norm1_g, w_in=w_in, sgu_ln_g=sgu_ln_g, sgu_ln_b=sgu_ln_b, w_spatial=w_spatial, b_spatial=b_spatial,
             w_proj_attn=w_proj_attn, w_proj_sgu=w_proj_sgu, w_out=w_out, norm2_g=norm2_g, w_ffn_gate=w_ffn_gate,
             w_ffn_up=w_ffn_up, w_ffn_down=w_ffn_down, final_g=final_g)
    m = dict(norm1_g=m_norm1_g, w_in=m_w_in, sgu_ln_g=m_sgu_ln_g, sgu_ln_b=m_sgu_ln_b, w_spatial=m_w_spatial,
             b_spatial=m_b_spatial, w_proj_attn=m_w_proj_attn, w_proj_sgu=m_w_proj_sgu, w_out=m_w_out, norm2_g=m_norm2_g,
             w_ffn_gate=m_w_ffn_gate, w_ffn_up=m_w_ffn_up, w_ffn_down=m_w_ffn_down, final_g=m_final_g)
    v = dict(norm1_g=v_norm1_g, w_in=v_w_in, sgu_ln_g=v_sgu_ln_g, sgu_ln_b=v_sgu_ln_b, w_spatial=v_w_spatial,
             b_spatial=v_b_spatial, w_proj_attn=v_w_proj_attn, w_proj_sgu=v_w_proj_sgu, w_out=v_w_out, norm2_g=v_norm2_g,
             w_ffn_gate=v_w_ffn_gate, w_ffn_up=v_w_ffn_up, w_ffn_down=v_w_ffn_down, final_g=v_final_g)
    t = x.shape[1]

    shards = {n: _ew(f"cast_{n}", lambda a: (a,), [w[n][0]], [BF16])[0] for n in BIG}
    late = COMM_GROUPS[1]
    k_in, n_in = shards["w_in"].shape
    *first, token = _split_start("gather_start_0", _gather_half_copies, 3, [shards["w_in"]], [(3, k_in // 2, n_in)])
    pending = {}

    def first_weight(after):
        srcs, filled = _split_wait("gather_wait_0", _gather_half_copies, *first, after)
        gath_in, late_shards = lax.optimization_barrier(
            (_gather_finish("gather_finish_0", srcs[0], filled[0]), [shards[n] for n in late]))
        *pending["late"], _ = _split_start(
            "gather_start_1", _gather_copies, 3, late_shards, [(N_CHIPS,) + s.shape for s in late_shards])
        return _cols_from_chips(gath_in)

    def late_weights(after):
        srcs, filled = _split_wait("gather_wait_1", _gather_copies, *pending["late"], after)
        me = 2 * lax.axis_index("x") + lax.axis_index("y")
        gath = {n: lax.dynamic_update_slice(f, s[None], (me, 0, 0)) for n, f, s in zip(late, filled, srcs)}
        return (_cols_from_chips(gath["w_proj_attn"]), _cols_from_chips(gath["w_proj_sgu"]),
                gath["w_out"].reshape(D_MODEL, D_MODEL), gath["w_ffn_gate"], gath["w_ffn_up"], gath["w_ffn_down"])

    exchanges = {}

    def on_grads(i, partials):
        if "w_out" in partials:
            partials["w_out"] = partials["w_out"].reshape(N_CHIPS, D_MODEL // N_CHIPS, D_MODEL)
        parts = [partials[n] for n in COMM_GROUPS[i]]
        *exchanges[i], started = _split_start(
            f"rs_exchange_start_{i}", _exchange_copies, N_PEERS, parts, [(N_PEERS,) + _piece_shape(p.shape) for p in parts])
        return started

    dx, _, small = _local_step(
        x[0], positions.reshape(t, 1), loss_target[0], norm1_g + token, sgu_ln_g, sgu_ln_b, w_spatial[0], b_spatial[0],
        norm2_g, final_g.reshape(1, D_MODEL), first_weight, late_weights, on_grads=on_grads)
    *small_started, small_token = _small_start(small)

    grads = {}
    for i in (1, 0):
        parts, filled = _split_wait(f"rs_exchange_wait_{i}", _exchange_copies, *exchanges[i], small_token)
        grads.update(zip(COMM_GROUPS[i], _device_sum(f"rs_device_sum_{i}", parts, filled)))

    delta, new_m, new_v, updated = {}, {}, {}, []
    for n in BIG:
        shp = w[n].shape
        flip = jnp.transpose if shp[-1] % LANES else (lambda a: a)
        outs = _adamw(f"adamw_{n}", flip(grads[n]), flip(w[n][0]), flip(m[n][0]), flip(v[n][0]))
        grads[n], delta[n], new_m[n], new_v[n] = (flip(a).reshape(shp) for a in outs)
        updated.append(outs[-1])

    g_s, d_s, m_s, v_s = _small_finish(small_started, updated, w, m, v)
    loss = g_s["loss"][0, 0]
    for n in SMALL_PARAMS:
        shp = w[n].shape
        grads[n], delta[n], new_m[n], new_v[n] = (a[n].reshape(shp) for a in (g_s, d_s, m_s, v_s))

    return (loss, dx.reshape(x.shape), *[grads[n] for n in WEIGHTS], *[delta[n] for n in WEIGHTS],
            *[new_m[n] for n in WEIGHTS], *[new_v[n] for n in WEIGHTS])
```

```python
import functools

import numpy as np
import jax
import jax.numpy as jnp
from jax import lax
from jax.experimental import pallas as pl
from jax.experimental.pallas import tpu as pltpu

F32, BF16 = jnp.float32, jnp.bfloat16
MESH = pl.DeviceIdType.MESH

D_MODEL = 1024
HEAD_DIM = 64
ATTN_W = 512
DILATIONS = (1, 4, 16)
BLK = 128
ATTN_BLOCKS_PER_STEP = 4
ROPE_DIM = 16
ROPE_THETA = 500000.0
SGU_W = 512
SGU_CHUNK = 128
SGU_GROUPS = 8
D_FF = 2816
N_CHIPS = 4
FF_SHARD = D_FF // N_CHIPS
IN_COLS = 7680
EPS = 1e-6
NEG = -1e30
LANES = 128
VMEM_LIMIT = 52 * 1024 * 1024

ADAM_LR, ADAM_B1, ADAM_B2, ADAM_EPS, ADAM_WD, ADAM_STEP = 0.001, 0.9, 0.999, 1e-08, 0.01, 10

QKV_BLOCKS = 9


def _w_in_block(part, g):
    return part * len(DILATIONS) + g


def _cparams(ngrid):
    return pltpu.CompilerParams(dimension_semantics=("arbitrary",) * ngrid, vmem_limit_bytes=VMEM_LIMIT)


def _full(shape):
    return pl.BlockSpec(shape, lambda *_: (0,) * len(shape))


def _resident(shape):
    return pl.BlockSpec(shape, lambda *_: (0,) * len(shape), pipeline_mode=pl.Buffered(1))


NT = ((1,), (1,))
TN = ((0,), (0,))


def _rope(v, cos_t, sin_t):
    half = ROPE_DIM // 2
    first = (lax.broadcasted_iota(jnp.int32, cos_t.shape, 1) % HEAD_DIM) < half
    outs = []
    for cs in range(v.shape[1] // LANES):
        x = v[:, cs * LANES:(cs + 1) * LANES]
        partner = jnp.where(first, pltpu.roll(x, LANES - half, axis=1), pltpu.roll(x, half, axis=1))
        outs.append(x * cos_t + partner * sin_t)
    return outs[0] if len(outs) == 1 else jnp.concatenate(outs, axis=1)


def _spread_heads(v2, upper):
    other = pltpu.roll(v2, HEAD_DIM, axis=1)
    h0 = jnp.where(upper, other, v2)
    h1 = jnp.where(upper, v2, other)
    return jnp.concatenate([jnp.concatenate([h0, h0], axis=1), jnp.concatenate([h1, h1], axis=1)], axis=0)


def _sigmoid(v):
    return 0.5 * jnp.tanh(0.5 * v) + 0.5


def _rms_stats(v):
    r = lax.rsqrt(jnp.mean(v * v, axis=-1, keepdims=True) + EPS)
    return v * r, r


def _rms_bwd(dy, xhat, r, g):
    dxh = dy * g
    return r * (dxh - xhat * jnp.mean(dxh * xhat, axis=-1, keepdims=True))


def _head_sum_matrix():
    idx = np.arange(ATTN_W) // HEAD_DIM
    return jnp.asarray((idx[:, None] == idx[None, :]).astype(np.float32), dtype=BF16)


def _group_sum(v, e):
    hi = v.astype(BF16)
    lo = (v - hi.astype(F32)).astype(BF16)
    return jnp.dot(hi, e, preferred_element_type=F32) + jnp.dot(lo, e, preferred_element_type=F32)


TILE = 512


def _to_slabs(slab_ref, v):
    for cs in range(slab_ref.shape[0]):
        slab_ref[cs] = v[:, cs * LANES:(cs + 1) * LANES]


def _from_slabs(slab_ref):
    return jnp.concatenate([slab_ref[cs] for cs in range(slab_ref.shape[0])], axis=1)


def _class_rows(slab_ref, r, dil):
    n = slab_ref.shape[1] // dil
    return jnp.concatenate([slab_ref.at[cs][pl.ds(r, n, stride=dil), :] for cs in range(slab_ref.shape[0])], axis=1)


def _put_class_rows(slab_ref, r, dil, v):
    n = slab_ref.shape[1] // dil
    for cs in range(slab_ref.shape[0]):
        slab_ref.at[cs][pl.ds(r, n, stride=dil), :] = v[:, cs * LANES:(cs + 1) * LANES]


def _natural_from_group(slab_ref, grp_ref):
    dil = grp_ref.shape[0]
    for r in range(dil):
        _put_class_rows(slab_ref, r, dil, grp_ref[r].astype(F32))
    return _from_slabs(slab_ref)


def _group_from_natural(slab_ref, grp_ref, v):
    dil = grp_ref.shape[0]
    _to_slabs(slab_ref, v)
    for r in range(dil):
        grp_ref[r] = _class_rows(slab_ref, r, dil).astype(grp_ref.dtype)


def _group_spec(dil, tile, width):
    return pl.BlockSpec((dil, tile // dil, width), lambda i, *_: (0, i, 0))


def _slabs(tile, width):
    return pltpu.VMEM((width // LANES, tile, LANES), F32)


def _rope_consts():
    lane = np.arange(LANES) % HEAD_DIM
    fi = lane % (ROPE_DIM // 2)
    invf = np.where(lane < ROPE_DIM, ROPE_THETA ** (-(2.0 * fi) / ROPE_DIM), 0.0)
    sgn = np.where(lane < ROPE_DIM // 2, -1.0, np.where(lane < ROPE_DIM, 1.0, 0.0))
    return (jnp.asarray(invf.astype(np.float32)).reshape(1, LANES), jnp.asarray(sgn.astype(np.float32)).reshape(1, LANES))


def _rope_tables(pos_col):
    t = pos_col.shape[0]
    tile = min(t, TILE)
    invf, sgn = _rope_consts()

    def body(p_ref, f_ref, s_ref, c0, s0, c1, s1, c2, s2, slab_c, slab_s):
        ang = p_ref[...].astype(F32) * f_ref[...]
        cos, sin = jnp.cos(ang), jnp.sin(ang) * s_ref[...]
        c0[...] = cos
        s0[...] = sin
        _group_from_natural(slab_c, c1, cos)
        _group_from_natural(slab_s, s1, sin)
        for r in range(DILATIONS[2]):
            c2[r] = _class_rows(slab_c, r, DILATIONS[2])
            s2[r] = _class_rows(slab_s, r, DILATIONS[2])

    nat = pl.BlockSpec((tile, LANES), lambda i: (i, 0))
    specs, shapes = [nat, nat], [(t, LANES)] * 2
    for d in DILATIONS[1:]:
        specs += [_group_spec(d, tile, LANES)] * 2
        shapes += [(d, t // d, LANES)] * 2
    outs = pl.pallas_call(
        body, grid=(t // tile,),
        in_specs=[pl.BlockSpec((tile, 1), lambda i: (i, 0)), _full((1, LANES)), _full((1, LANES))],
        out_specs=specs, out_shape=[jax.ShapeDtypeStruct(s, F32) for s in shapes],
        scratch_shapes=[_slabs(tile, LANES)] * 2,
        compiler_params=_cparams(1), name="rope_tables")(pos_col, invf, sgn)
    return [(outs[2 * g].reshape(t, LANES), outs[2 * g + 1].reshape(t, LANES)) for g in range(len(DILATIONS))]


def _norm_fwd(x, g):
    t = x.shape[0]
    tile = min(t, TILE)

    def body(x_ref, g_ref, h0_ref, h1_ref, h2_ref, slab):
        xhat, _ = _rms_stats(x_ref[...])
        hn = xhat * g_ref[...]
        h0_ref[...] = hn.astype(BF16)
        _group_from_natural(slab, h1_ref, hn)
        for r in range(DILATIONS[2]):
            h2_ref[r] = _class_rows(slab, r, DILATIONS[2]).astype(BF16)

    nat = pl.BlockSpec((tile, D_MODEL), lambda i: (i, 0))
    return pl.pallas_call(
        body, grid=(t // tile,),
        in_specs=[nat, _full((1, D_MODEL))],
        out_specs=[nat] + [_group_spec(d, tile, D_MODEL) for d in DILATIONS[1:]],
        out_shape=[jax.ShapeDtypeStruct((t, D_MODEL), BF16)]
        + [jax.ShapeDtypeStruct((d, t // d, D_MODEL), BF16) for d in DILATIONS[1:]],
        scratch_shapes=[_slabs(tile, D_MODEL)],
        compiler_params=_cparams(1), name="norm1_fwd")(x, g)


GU_COLS = 3072
GROUP_COLS = 1536
GU_HALF = GU_COLS // 2


def _w_in_spec(width, block):
    return pl.BlockSpec((D_MODEL, width), lambda i: (0, block), pipeline_mode=pl.Buffered(1))


def _gu_w_specs():
    first = QKV_BLOCKS * ATTN_W // GU_HALF
    return [_w_in_spec(GU_HALF, first), _w_in_spec(GU_HALF, first + 1)]


def _group_w_specs(g):
    return [_w_in_spec(ATTN_W, _w_in_block(part, g)) for part in range(3)]


def _in_proj(hs, w_in, tables):
    t = hs[0].shape[0]
    tm = min(t, 1024)

    def body_gu(h_ref, w0_ref, w1_ref, o_ref):
        h = h_ref[...]
        o_ref[:, 0:GU_HALF] = jnp.dot(h, w0_ref[...], preferred_element_type=F32).astype(BF16)
        o_ref[:, GU_HALF:] = jnp.dot(h, w1_ref[...], preferred_element_type=F32).astype(BF16)

    gu = _token_call("in_proj_gates_uv", body_gu, t, tm,
                     [(hs[0], _rows_spec(tm, D_MODEL))] + [(w_in, s) for s in _gu_w_specs()],
                     [((t, GU_COLS), BF16, _rows_spec(tm, GU_COLS))])[0]

    qkvs = []
    for g in range(len(DILATIONS)):

        def body_qkv(h_ref, wq_ref, wk_ref, wv_ref, cos_ref, sin_ref, o_ref):
            h = h_ref[...]
            cos_w, sin_w = cos_ref[...], sin_ref[...]
            q = jnp.dot(h, wq_ref[...], preferred_element_type=F32)
            o_ref[:, 0:ATTN_W] = (_rope(q, cos_w, sin_w) * HEAD_DIM ** -0.5).astype(BF16)
            k = jnp.dot(h, wk_ref[...], preferred_element_type=F32)
            o_ref[:, ATTN_W:2 * ATTN_W] = _rope(k, cos_w, sin_w).astype(BF16)
            o_ref[:, 2 * ATTN_W:] = jnp.dot(h, wv_ref[...], preferred_element_type=F32).astype(BF16)

        cos_t, sin_t = tables[g]
        qkvs.append(_token_call(
            f"in_proj_qkv_g{g}", body_qkv, t, tm,
            [(hs[g].reshape(t, D_MODEL), _rows_spec(tm, D_MODEL))] + [(w_in, s) for s in _group_w_specs(g)]
            + [(cos_t, _rows_spec(tm, LANES)), (sin_t, _rows_spec(tm, LANES))],
            [((t, GROUP_COLS), BF16, _rows_spec(tm, GROUP_COLS))])[0])
    return gu, qkvs


def _attn_masks(n):
    row = lax.broadcasted_iota(jnp.int32, (2 * BLK, 2 * BLK), 0) % BLK
    col = lax.broadcasted_iota(jnp.int32, (2 * BLK, 2 * BLK), 1)
    diff = BLK + row - col
    valid = (diff >= 0) & (diff <= BLK) & ((col >= BLK) | (n > 0))
    upper = lax.broadcasted_iota(jnp.int32, (BLK, LANES), 1) >= HEAD_DIM
    return valid, upper


def _stack_heads(v2, upper):
    zero = jnp.zeros_like(v2)
    return jnp.concatenate([jnp.where(upper, zero, v2), jnp.where(upper, v2, zero)], axis=0)


def _unstack_heads(v, upper):
    return jnp.where(upper, v[BLK:], v[:BLK])


def _attn_fwd(qkv, g, dil):
    t = qkv.shape[0]
    length = t // dil
    nb = length // BLK
    per_step = min(nb, ATTN_BLOCKS_PER_STEP)
    view = qkv.reshape(dil, length, GROUP_COLS)

    def body(q_ref, kc_ref, kp_ref, vc_ref, vp_ref, o_ref, l_ref, kwin, vwin):
        n = pl.program_id(1)
        kwin[0:BLK] = kp_ref[...]
        kwin[BLK:] = kc_ref[...]
        vwin[0:BLK] = vp_ref[...]
        vwin[BLK:] = vc_ref[...]

        def block(b, carry):
            valid, upper = _attn_masks(n * per_step + b)
            rows = pl.ds(pl.multiple_of(b * BLK, BLK), BLK)
            window = pl.ds(pl.multiple_of(b * BLK, BLK), 2 * BLK)
            slabs = [slice(p * LANES, (p + 1) * LANES) for p in range(ATTN_W // LANES)]
            ss = [lax.dot_general(_stack_heads(q_ref[rows, sl], upper), kwin[window, sl], (NT, ((), ())),
                                  preferred_element_type=F32) for sl in slabs]
            soft = []
            for s in ss:
                s = jnp.where(valid, s, NEG)
                m = jnp.max(s, axis=1, keepdims=True)
                pe = jnp.exp(s - m)
                soft.append((m, pe, jnp.sum(pe, axis=1, keepdims=True)))
            for sl, (m, pe, den) in zip(slabs, soft):
                o = jnp.dot(pe.astype(BF16), vwin[window, sl], preferred_element_type=F32) / den
                lse = jnp.broadcast_to(m + jnp.log(den), (2 * BLK, LANES))
                o_ref[rows, sl] = _unstack_heads(o, upper).astype(BF16)
                l_ref[rows, sl] = _unstack_heads(lse, upper)
            return carry

        lax.fori_loop(0, per_step, block, 0)

    rows = per_step * BLK
    cur = lambda part: pl.BlockSpec((None, rows, ATTN_W), lambda r, n: (r, n, part))
    prev = lambda part: pl.BlockSpec((None, BLK, ATTN_W), lambda r, n: (r, jnp.maximum(n * per_step - 1, 0), part))
    out_spec = pl.BlockSpec((None, rows, ATTN_W), lambda r, n: (r, n, 0))
    return pl.pallas_call(
        body, grid=(dil, nb // per_step),
        in_specs=[cur(0), cur(1), prev(1), cur(2), prev(2)],
        out_specs=[out_spec, out_spec],
        out_shape=[jax.ShapeDtypeStruct((dil, length, ATTN_W), BF16), jax.ShapeDtypeStruct((dil, length, ATTN_W), F32)],
        scratch_shapes=[pltpu.VMEM((rows + BLK, ATTN_W), BF16)] * 2,
        compiler_params=_cparams(2), name=f"attn_fwd_g{g}")(view, view, view, view, view)


def _alphas(l0, l1, l2):
    m = jnp.maximum(jnp.maximum(l0, l1), l2)
    e0, e1, e2 = jnp.exp(l0 - m), jnp.exp(l1 - m), jnp.exp(l2 - m)
    inv = 1.0 / (e0 + e1 + e2)
    return e0 * inv, e1 * inv, e2 * inv


def _natural_group_values(o_refs, l_refs, slabs):
    os_ = [o_refs[0][0].astype(F32)] + [_natural_from_group(slabs[2 * g - 2], o_refs[g]) for g in (1, 2)]
    ls_ = [l_refs[0][0]] + [_natural_from_group(slabs[2 * g - 1], l_refs[g]) for g in (1, 2)]
    return os_, ls_


def _combine_fwd(os_, ls_):
    t = os_[0].shape[1]
    tile = min(t, TILE)

    def body(o0, o1, o2, l0, l1, l2, a_ref, *slabs):
        ov, lv = _natural_group_values((o0, o1, o2), (l0, l1, l2), slabs)
        a0, a1, a2 = _alphas(*lv)
        a_ref[...] = (a0 * ov[0] + a1 * ov[1] + a2 * ov[2]).astype(BF16)

    specs = [_group_spec(d, tile, ATTN_W) for d in DILATIONS]
    return pl.pallas_call(
        body, grid=(t // tile,), in_specs=specs * 2, out_specs=pl.BlockSpec((tile, ATTN_W), lambda i: (i, 0)),
        out_shape=jax.ShapeDtypeStruct((t, ATTN_W), BF16),
        scratch_shapes=[_slabs(tile, ATTN_W)] * 4,
        compiler_params=_cparams(1), name="combine_fwd")(*os_, *ls_)


def _combine_bwd(dattn, os_, ls_):
    t = dattn.shape[0]
    tile = min(t, TILE)
    e = _head_sum_matrix()

    def body(d_ref, o0, o1, o2, l0, l1, l2, e_ref, do0, do1, do2, c0, c1, c2, *slabs):
        ov, lv = _natural_group_values((o0, o1, o2), (l0, l1, l2), slabs)
        alphas = _alphas(*lv)
        d = d_ref[...]
        attn = alphas[0] * ov[0] + alphas[1] * ov[1] + alphas[2] * ov[2]
        s = _group_sum(d * attn, e_ref[...])
        do0[0] = (alphas[0] * d).astype(BF16)
        c0[0] = -alphas[0] * s
        for g, do_ref, c_ref in ((1, do1, c1), (2, do2, c2)):
            _group_from_natural(slabs[2 * g - 2], do_ref, alphas[g] * d)
            _group_from_natural(slabs[2 * g - 1], c_ref, -alphas[g] * s)

    specs = [_group_spec(d, tile, ATTN_W) for d in DILATIONS]
    shapes = [(d, t // d, ATTN_W) for d in DILATIONS]
    outs = pl.pallas_call(
        body, grid=(t // tile,),
        in_specs=[pl.BlockSpec((tile, ATTN_W), lambda i: (i, 0))] + specs * 2 + [_full((ATTN_W, ATTN_W))],
        out_specs=specs * 2,
        out_shape=[jax.ShapeDtypeStruct(s, BF16) for s in shapes] + [jax.ShapeDtypeStruct(s, F32) for s in shapes],
        scratch_shapes=[_slabs(tile, ATTN_W)] * 4,
        compiler_params=_cparams(1), name="combine_bwd")(dattn, *os_, *ls_, e)
    return outs[:3], outs[3:]


def _attn_bwd(qkv, do, cc, lse, cos_t, sin_t, g, dil):
    t = qkv.shape[0]
    length = t // dil
    nb = length // BLK
    per_step = min(nb, ATTN_BLOCKS_PER_STEP)
    nsteps = nb // per_step
    rows_per_step = per_step * BLK
    qkv_v = qkv.reshape(dil, length, GROUP_COLS)
    cos_v, sin_v = (a.reshape(dil, length, LANES) for a in (cos_t, sin_t))
    scale = HEAD_DIM ** -0.5
    dq_cols, dk_cols, dv_cols = (slice(i * ATTN_W, (i + 1) * ATTN_W) for i in range(3))

    def body(q_ref, kc_ref, kp_ref, vc_ref, vp_ref, do_ref, c_ref, l_ref, cosc, sinc, cosp, sinp,
             out_ref, acc, kwin, vwin, cwin, swin):
        n = pl.program_id(1)

        def one_block(b):
            valid, upper = _attn_masks(n * per_step + b)
            start = b * BLK if isinstance(b, int) else pl.multiple_of(b * BLK, BLK)
            rows, before, window = pl.ds(start, BLK), pl.ds(start, BLK), pl.ds(start, 2 * BLK)
            own = pl.ds(start + BLK, BLK)
            dq_parts, dkp_parts, dkc_parts, dvp_parts, dvc_parts = [], [], [], [], []
            npairs = ATTN_W // LANES
            slabs = [slice(p * LANES, (p + 1) * LANES) for p in range(npairs)]
            qss = [_stack_heads(q_ref[rows, sl], upper) for sl in slabs]
            doss = [_stack_heads(do_ref[rows, sl], upper) for sl in slabs]
            ss = [lax.dot_general(qss[p], kwin[window, slabs[p]], (NT, ((), ())), preferred_element_type=F32) for p in range(npairs)]
            dpvs = [lax.dot_general(doss[p], vwin[window, slabs[p]], (NT, ((), ())), preferred_element_type=F32)
                    for p in range(npairs)]
            pes = [jnp.exp(jnp.where(valid, ss[p], NEG) - _spread_heads(l_ref[rows, slabs[p]], upper)) for p in range(npairs)]
            dss = [(pes[p] * (dpvs[p] + _spread_heads(c_ref[rows, slabs[p]], upper))).astype(BF16) for p in range(npairs)]
            for p in range(npairs):
                qs, dos, ds = qss[p], doss[p], dss[p]
                dq2 = _unstack_heads(jnp.dot(ds, kwin[window, slabs[p]], preferred_element_type=F32), upper)
                dk2 = lax.dot_general(ds, qs, (TN, ((), ())), preferred_element_type=F32)
                dv2 = lax.dot_general(pes[p].astype(BF16), dos, (TN, ((), ())), preferred_element_type=F32)
                dq_parts.append(dq2)
                dkp_parts.append(dk2[:BLK])
                dkc_parts.append(dk2[BLK:])
                dvp_parts.append(dv2[:BLK])
                dvc_parts.append(dv2[BLK:])
            dq = _rope(jnp.concatenate(dq_parts, axis=1) * scale, cwin[own, :], swin[own, :])
            dkc = _rope(jnp.concatenate(dkc_parts, axis=1), cwin[own, :], swin[own, :])
            dkp = _rope(jnp.concatenate(dkp_parts, axis=1), cwin[before, :], swin[before, :])
            return dq, dkp, dkc, jnp.concatenate(dvp_parts, axis=1), jnp.concatenate(dvc_parts, axis=1)

        @pl.when(n < nsteps)
        def _():
            kwin[0:BLK] = kp_ref[...]
            kwin[BLK:] = kc_ref[...]
            vwin[0:BLK] = vp_ref[...]
            vwin[BLK:] = vc_ref[...]
            cwin[0:BLK] = cosp[...]
            cwin[BLK:] = cosc[...]
            swin[0:BLK] = -sinp[...]
            swin[BLK:] = -sinc[...]
            dq, dkp, dkc, dvp, dvc = one_block(0)
            last = slice(rows_per_step - BLK, rows_per_step)

            @pl.when(n > 0)
            def _():
                if per_step > 1:
                    out_ref[0:rows_per_step - BLK, :] = acc[0:rows_per_step - BLK, :].astype(BF16)
                out_ref[last, dq_cols] = acc[last, dq_cols].astype(BF16)
                out_ref[last, dk_cols] = (acc[last, dk_cols] + dkp).astype(BF16)
                out_ref[last, dv_cols] = (acc[last, dv_cols] + dvp).astype(BF16)

            acc[0:BLK, dq_cols] = dq
            acc[0:BLK, dk_cols] = dkc
            acc[0:BLK, dv_cols] = dvc

            def later(b, carry):
                dq, dkp, dkc, dvp, dvc = one_block(b)
                start = pl.multiple_of(b * BLK, BLK)
                before, rows = pl.ds(start - BLK, BLK), pl.ds(start, BLK)
                acc[before, dk_cols] += dkp
                acc[before, dv_cols] += dvp
                acc[rows, dq_cols] = dq
                acc[rows, dk_cols] = dkc
                acc[rows, dv_cols] = dvc
                return carry

            lax.fori_loop(1, per_step, later, 0)

        @pl.when(n == flush_at)
        def _():
            out_ref[...] = acc[...].astype(BF16)

    flush_at = nsteps - 1 if nsteps == 1 else nsteps
    out_lag = 0 if nsteps == 1 else 1
    nc = lambda n: jnp.minimum(n, nsteps - 1)
    npv = lambda n: jnp.maximum(jnp.minimum(n, nsteps - 1) * per_step - 1, 0)
    cur = lambda part: pl.BlockSpec((None, rows_per_step, ATTN_W), lambda r, n: (r, nc(n), part))
    prev = lambda part: pl.BlockSpec((None, BLK, ATTN_W), lambda r, n: (r, npv(n), part))
    row = pl.BlockSpec((None, rows_per_step, ATTN_W), lambda r, n: (r, nc(n), 0))
    tab_c = pl.BlockSpec((None, rows_per_step, LANES), lambda r, n: (r, nc(n), 0))
    tab_p = pl.BlockSpec((None, BLK, LANES), lambda r, n: (r, npv(n), 0))
    out_spec = pl.BlockSpec((None, rows_per_step, GROUP_COLS), lambda r, n: (r, jnp.maximum(n - out_lag, 0), 0))
    out = pl.pallas_call(
        body, grid=(dil, nsteps + out_lag),
        in_specs=[cur(0), cur(1), prev(1), cur(2), prev(2), row, row, row, tab_c, tab_c, tab_p, tab_p],
        out_specs=out_spec,
        out_shape=jax.ShapeDtypeStruct((dil, length, GROUP_COLS), BF16),
        scratch_shapes=[pltpu.VMEM((rows_per_step, GROUP_COLS), F32)]
        + [pltpu.VMEM((rows_per_step + BLK, ATTN_W), BF16)] * 2 + [pltpu.VMEM((rows_per_step + BLK, LANES), F32)] * 2,
        compiler_params=_cparams(2), name=f"attn_bwd_g{g}")(
            qkv_v, qkv_v, qkv_v, qkv_v, qkv_v, do, cc, lse, cos_v, sin_v, cos_v, sin_v)
    return out.reshape(t, GROUP_COLS)


SQRT_HALF = 0.7071067811865476
INV_SQRT_2PI = 0.3989422804014327


def _sgu_core(uv, g, b, w_ref, bias):
    cdf = 0.5 * (1.0 + lax.erf(uv * SQRT_HALF))
    z = uv * cdf
    u, v = z[:, :SGU_W], z[:, SGU_W:]
    mu = jnp.mean(v, axis=1, keepdims=True)
    xc = v - mu
    rs = lax.rsqrt(jnp.mean(xc * xc, axis=1, keepdims=True) + EPS)
    xhat = xc * rs
    vn = xhat * g + b
    row = lax.broadcasted_iota(jnp.int32, (SGU_CHUNK, SGU_CHUNK), 0)
    col = lax.broadcasted_iota(jnp.int32, (SGU_CHUNK, SGU_CHUNK), 1)
    tril = row >= col
    upper = lax.broadcasted_iota(jnp.int32, (SGU_CHUNK, LANES), 1) >= SGU_W // SGU_GROUPS
    ws, vlo, vhi, mixed = [], [], [], []
    for pr in range(SGU_W // LANES):
        sl = slice(pr * LANES, (pr + 1) * LANES)
        w0 = jnp.where(tril, w_ref[2 * pr], 0.0).astype(BF16)
        w1 = jnp.where(tril, w_ref[2 * pr + 1], 0.0).astype(BF16)
        vn2 = vn[:, sl]
        lo = jnp.where(upper, 0.0, vn2).astype(BF16)
        hi = jnp.where(upper, vn2, 0.0).astype(BF16)
        mixed.append(jnp.dot(w0, lo, preferred_element_type=F32) + jnp.dot(w1, hi, preferred_element_type=F32)
                     + bias[:, sl])
        ws.append((w0, w1))
        vlo.append(lo)
        vhi.append(hi)
    return cdf, u, xhat, rs, jnp.concatenate(mixed, axis=1), ws, vlo, vhi, tril, upper


SGU_STEP = 4 * SGU_CHUNK


def _for_chunks(step_rows, fn):
    def one(ci, carry):
        fn(pl.ds(pl.multiple_of(ci * SGU_CHUNK, SGU_CHUNK), SGU_CHUNK))
        return carry

    lax.fori_loop(0, step_rows // SGU_CHUNK, one, 0)


def _sgu_fwd(gu, ln_g, ln_b, w_s, bias_exp):
    t = gu.shape[0]
    step = min(t, SGU_STEP)

    def body(uv_ref, g_ref, b_ref, w_ref, bias_ref, o_ref):
        def chunk(rows):
            _, u, _, _, mixed, *_ = _sgu_core(uv_ref[rows, :].astype(F32), g_ref[...], b_ref[...], w_ref, bias_ref[...])
            o_ref[rows, :] = (u * mixed).astype(BF16)

        _for_chunks(step, chunk)

    return pl.pallas_call(
        body, grid=(t // step,),
        in_specs=[pl.BlockSpec((step, 2 * SGU_W), lambda n: (n, 0)), _full((1, SGU_W)), _full((1, SGU_W)),
                  _full((SGU_GROUPS, SGU_CHUNK, SGU_CHUNK)), _full((SGU_CHUNK, SGU_W))],
        out_specs=pl.BlockSpec((step, SGU_W), lambda n: (n, 0)),
        out_shape=jax.ShapeDtypeStruct((t, SGU_W), BF16),
        compiler_params=_cparams(1), name="sgu_fwd")(gu, ln_g, ln_b, w_s, bias_exp)


def _sgu_bwd(dproj, gu, dsgu, ln_g, ln_b, w_s, bias_exp):
    t = gu.shape[0]
    step = min(t, SGU_STEP)
    nsteps = t // step
    e = _head_sum_matrix()

    def body(dp_in, uv_ref, ds_ref, g_ref, b_ref, w_ref, bias_ref, e_ref, out_ref, dw_ref, dbias_ref, dg_ref, db_ref):
        n = pl.program_id(0)

        @pl.when(n == 0)
        def _():
            dw_ref[...] = jnp.zeros(dw_ref.shape, F32)
            dbias_ref[...] = jnp.zeros(dbias_ref.shape, F32)
            dg_ref[...] = jnp.zeros(dg_ref.shape, F32)
            db_ref[...] = jnp.zeros(db_ref.shape, F32)

        _for_chunks(step, functools.partial(chunk, uv_ref, ds_ref, g_ref, b_ref, w_ref, bias_ref, out_ref, dw_ref, dbias_ref,
                                            dg_ref, db_ref))

        @pl.when(n == nsteps - 1)
        def _():
            dbias_ref[...] = _group_sum(dbias_ref[...], e_ref[...])

    def chunk(uv_ref, ds_ref, g_ref, b_ref, w_ref, bias_ref, out_ref, dw_ref, dbias_ref, dg_ref, db_ref, rows):
        uv = uv_ref[rows, :].astype(F32)
        g = g_ref[...]
        cdf, u, xhat, rs, mixed, ws, vlo, vhi, tril, upper = _sgu_core(uv, g, b_ref[...], w_ref, bias_ref[...])
        dsg = ds_ref[rows, :]
        du = dsg * mixed
        dmixed = dsg * u
        dbias_ref[...] += dmixed
        dvn = []
        for pr in range(SGU_W // LANES):
            sl = slice(pr * LANES, (pr + 1) * LANES)
            dm2 = dmixed[:, sl]
            dlo = jnp.where(upper, 0.0, dm2).astype(BF16)
            dhi = jnp.where(upper, dm2, 0.0).astype(BF16)
            w0, w1 = ws[pr]
            dvn.append(lax.dot_general(w0, dlo, (TN, ((), ())), preferred_element_type=F32)
                       + lax.dot_general(w1, dhi, (TN, ((), ())), preferred_element_type=F32))
            dw0 = lax.dot_general(dlo, vlo[pr], (NT, ((), ())), preferred_element_type=F32)
            dw1 = lax.dot_general(dhi, vhi[pr], (NT, ((), ())), preferred_element_type=F32)
            dw_ref[2 * pr] += jnp.where(tril, dw0, 0.0)
            dw_ref[2 * pr + 1] += jnp.where(tril, dw1, 0.0)
        dvn = jnp.concatenate(dvn, axis=1)
        dg_ref[...] += jnp.sum(dvn * xhat, axis=0, keepdims=True)
        db_ref[...] += jnp.sum(dvn, axis=0, keepdims=True)
        dxh = dvn * g
        dv = rs * (dxh - jnp.mean(dxh, axis=1, keepdims=True) - xhat * jnp.mean(dxh * xhat, axis=1, keepdims=True))
        dz = jnp.concatenate([du, dv], axis=1)
        dgelu = cdf + uv * (INV_SQRT_2PI * jnp.exp(-0.5 * uv * uv))
        out_ref[rows, :] = (dz * dgelu).astype(BF16)

    outs = pl.pallas_call(
        body, grid=(nsteps,),
        in_specs=[pl.BlockSpec(memory_space=pl.ANY), pl.BlockSpec((step, 2 * SGU_W), lambda n: (n, 0)),
                  pl.BlockSpec((step, SGU_W), lambda n: (n, 0)), _full((1, SGU_W)), _full((1, SGU_W)),
                  _full((SGU_GROUPS, SGU_CHUNK, SGU_CHUNK)), _full((SGU_CHUNK, SGU_W)), _full((ATTN_W, ATTN_W))],
        out_specs=[pl.BlockSpec((step, 2 * SGU_W), lambda n: (n, 0)), _full((SGU_GROUPS, SGU_CHUNK, SGU_CHUNK)),
                   _full((SGU_CHUNK, SGU_W)), _full((1, SGU_W)), _full((1, SGU_W))],
        out_shape=[jax.ShapeDtypeStruct(dproj.shape, BF16), jax.ShapeDtypeStruct((SGU_GROUPS, SGU_CHUNK, SGU_CHUNK), F32),
                   jax.ShapeDtypeStruct((SGU_CHUNK, SGU_W), F32), jax.ShapeDtypeStruct((1, SGU_W), F32),
                   jax.ShapeDtypeStruct((1, SGU_W), F32)],
        input_output_aliases={0: 0},
        compiler_params=_cparams(1), name="sgu_bwd")(dproj, gu, dsgu, ln_g, ln_b, w_s, bias_exp, e)
    return outs


def _merge_fwd(attn, sgu, gu, x, w_pa, w_ps, w_out, g2):
    t = x.shape[0]
    tm = min(t, 512)

    def body(a_ref, s_ref, ga_ref, gb_ref, x_ref, wpa, wps, wo, g_ref, pa_ref, ps_ref, m_ref, x1_ref, h2_ref):
        pa = jnp.dot(a_ref[...], wpa[...], preferred_element_type=F32)
        ps = jnp.dot(s_ref[...], wps[...], preferred_element_type=F32)
        merged = (_sigmoid(ga_ref[...].astype(F32)) * pa + _sigmoid(gb_ref[...].astype(F32)) * ps).astype(BF16)
        x1 = x_ref[...] + jnp.dot(merged, wo[...], preferred_element_type=F32)
        xhat, _ = _rms_stats(x1)
        pa_ref[...] = pa.astype(BF16)
        ps_ref[...] = ps.astype(BF16)
        m_ref[...] = merged
        x1_ref[...] = x1
        h2_ref[...] = (xhat * g_ref[...]).astype(BF16)

    half = pl.BlockSpec((tm, ATTN_W), lambda i: (i, 0))
    full = pl.BlockSpec((tm, D_MODEL), lambda i: (i, 0))
    return pl.pallas_call(
        body, grid=(t // tm,),
        in_specs=[half, half, pl.BlockSpec((tm, D_MODEL), lambda i: (i, 1)), pl.BlockSpec((tm, D_MODEL), lambda i: (i, 2)),
                  full, _resident((ATTN_W, D_MODEL)), _resident((SGU_W, D_MODEL)), _resident((D_MODEL, D_MODEL)),
                  _full((1, D_MODEL))],
        out_specs=[full] * 5,
        out_shape=[jax.ShapeDtypeStruct((t, D_MODEL), BF16), jax.ShapeDtypeStruct((t, D_MODEL), BF16),
                   jax.ShapeDtypeStruct((t, D_MODEL), BF16), jax.ShapeDtypeStruct((t, D_MODEL), F32),
                   jax.ShapeDtypeStruct((t, D_MODEL), BF16)],
        compiler_params=_cparams(1), name="merge_fwd")(attn, sgu, gu, gu, x, w_pa, w_ps, w_out, g2)


def _merge_bwd(dx1b, gu, pa, ps, w_pa, w_ps, w_out):
    t = dx1b.shape[0]
    tm = min(t, 512)

    def body(d_ref, ga_ref, gb_ref, pa_ref, ps_ref, wpa, wps, wo, out_ref, dpa_ref, dps_ref, da_ref, dsg_ref):
        dm = lax.dot_general(d_ref[...], wo[...], (NT, ((), ())), preferred_element_type=F32)
        sa, sb = _sigmoid(ga_ref[...].astype(F32)), _sigmoid(gb_ref[...].astype(F32))
        dpa = (dm * sa).astype(BF16)
        dps = (dm * sb).astype(BF16)
        out_ref[:, 0:D_MODEL] = jnp.zeros((tm, D_MODEL), BF16)
        out_ref[:, D_MODEL:2 * D_MODEL] = (dm * pa_ref[...].astype(F32) * sa * (1.0 - sa)).astype(BF16)
        out_ref[:, 2 * D_MODEL:GU_COLS] = (dm * ps_ref[...].astype(F32) * sb * (1.0 - sb)).astype(BF16)
        dpa_ref[...] = dpa
        dps_ref[...] = dps
        da_ref[...] = lax.dot_general(dpa, wpa[...], (NT, ((), ())), preferred_element_type=F32)
        dsg_ref[...] = lax.dot_general(dps, wps[...], (NT, ((), ())), preferred_element_type=F32)

    half = pl.BlockSpec((tm, ATTN_W), lambda i: (i, 0))
    full = pl.BlockSpec((tm, D_MODEL), lambda i: (i, 0))
    return pl.pallas_call(
        body, grid=(t // tm,),
        in_specs=[full, pl.BlockSpec((tm, D_MODEL), lambda i: (i, 1)),
                  pl.BlockSpec((tm, D_MODEL), lambda i: (i, 2)), full, full,
                  _resident((ATTN_W, D_MODEL)), _resident((SGU_W, D_MODEL)), _resident((D_MODEL, D_MODEL))],
        out_specs=[pl.BlockSpec((tm, GU_COLS), lambda i: (i, 0)), full, full, half, half],
        out_shape=[jax.ShapeDtypeStruct((t, GU_COLS), BF16), jax.ShapeDtypeStruct((t, D_MODEL), BF16),
                   jax.ShapeDtypeStruct((t, D_MODEL), BF16), jax.ShapeDtypeStruct((t, ATTN_W), F32),
                   jax.ShapeDtypeStruct((t, SGU_W), F32)],
        compiler_params=_cparams(1), name="merge_bwd")(dx1b, gu, gu, pa, ps, w_pa, w_ps, w_out)


def _token_call(name, body, t, tm, ins, outs, reds=(), scratch=()):
    return pl.pallas_call(
        body, grid=(t // tm,), in_specs=[s for _, s in ins],
        out_specs=[o[2] for o in outs] + [_full(r) for r in reds],
        out_shape=[jax.ShapeDtypeStruct(o[0], o[1]) for o in outs] + [jax.ShapeDtypeStruct(r, F32) for r in reds],
        scratch_shapes=list(scratch), compiler_params=_cparams(1), name=name)(*[a for a, _ in ins])


def _rows_spec(tm, width):
    return pl.BlockSpec((tm, width), lambda i: (i, 0))


def _chips_spec(tm):
    return pl.BlockSpec((N_CHIPS, tm, FF_SHARD), lambda i: (0, i, 0))


def _zero_at_start(*refs):
    @pl.when(pl.program_id(0) == 0)
    def _():
        for r in refs:
            r[...] = jnp.zeros(r.shape, r.dtype)


def _ffn_fwd(h2, w_g, w_u):
    t = h2.shape[0]
    tm = min(t, 512)

    def body(h_ref, wg_ref, wu_ref, fa_ref, fb_ref, ff_ref):
        h = h_ref[...]
        for s in range(N_CHIPS):
            a = jnp.dot(h, wg_ref[s], preferred_element_type=F32)
            b = jnp.dot(h, wu_ref[s], preferred_element_type=F32)
            sg = _sigmoid(a)
            silu = a * sg
            fa_ref[s] = (b * (sg * (1.0 + a * (1.0 - sg)))).astype(BF16)
            fb_ref[s] = silu.astype(BF16)
            ff_ref[s] = (silu * b).astype(BF16)

    shp = (N_CHIPS, t, FF_SHARD)
    w_spec = _resident((N_CHIPS, D_MODEL, FF_SHARD))
    return _token_call("ffn_fwd", body, t, tm, [(h2, _rows_spec(tm, D_MODEL)), (w_g, w_spec), (w_u, w_spec)],
                       [(shp, BF16, _chips_spec(tm))] * 3)


def _ffn_down_loss(ff, w_d, x1, tgt, gf):
    t = x1.shape[0]
    tm = min(t, 512)

    def body(ff_ref, wd_ref, x1_ref, tgt_ref, g_ref, dx2_ref, dx2b_ref, loss_ref, dgf_ref):
        _zero_at_start(loss_ref, dgf_ref)
        acc = jnp.dot(ff_ref[0], wd_ref[0], preferred_element_type=F32)
        for s in range(1, N_CHIPS):
            acc = acc + jnp.dot(ff_ref[s], wd_ref[s], preferred_element_type=F32)
        x2 = x1_ref[...] + acc
        g = g_ref[...]
        xhat, rr = _rms_stats(x2)
        diff = xhat * g - tgt_ref[...]
        rows = jnp.sum(diff * diff, axis=1, keepdims=True)
        loss_ref[...] += jnp.broadcast_to(jnp.sum(rows, axis=0, keepdims=True) * (0.5 / D_MODEL), (1, LANES))
        dy = diff * (1.0 / D_MODEL)
        dgf_ref[...] += jnp.sum(dy * xhat, axis=0, keepdims=True)
        dx2 = _rms_bwd(dy, xhat, rr, g)
        dx2_ref[...] = dx2
        dx2b_ref[...] = dx2.astype(BF16)

    row = _rows_spec(tm, D_MODEL)
    return _token_call("ffn_down_loss", body, t, tm,
                       [(ff, _chips_spec(tm)), (w_d, _resident((N_CHIPS, FF_SHARD, D_MODEL))), (x1, row), (tgt, row),
                        (gf, _full((1, D_MODEL)))],
                       [((t, D_MODEL), F32, row), ((t, D_MODEL), BF16, row)], reds=[(1, LANES), (1, D_MODEL)])


def _ffn_bwd_act(dx2b, w_d, fa, fb):
    t = dx2b.shape[0]
    tm = min(t, 512)

    def body(d_ref, wd_ref, fa_ref, fb_ref, da_ref, db_ref):
        d = d_ref[...]
        for s in range(N_CHIPS):
            dff = lax.dot_general(d, wd_ref[s], (NT, ((), ())), preferred_element_type=F32)
            da_ref[s] = (dff * fa_ref[s].astype(F32)).astype(BF16)
            db_ref[s] = (dff * fb_ref[s].astype(F32)).astype(BF16)

    shp = (N_CHIPS, t, FF_SHARD)
    return _token_call("ffn_bwd_act", body, t, tm,
                       [(dx2b, _rows_spec(tm, D_MODEL)), (w_d, _resident((N_CHIPS, FF_SHARD, D_MODEL))),
                        (fa, _chips_spec(tm)), (fb, _chips_spec(tm))],
                       [(shp, BF16, _chips_spec(tm))] * 2)


def _ffn_bwd_in(da, db, w_g, w_u, x1, dx2, g2):
    t = x1.shape[0]
    tm = min(t, 512)

    def body(da_ref, db_ref, wg_ref, wu_ref, x1_ref, dx2_ref, g_ref, dx1_ref, dx1b_ref, dg_ref):
        _zero_at_start(dg_ref)
        acc = None
        for s in range(N_CHIPS):
            part = (lax.dot_general(da_ref[s], wg_ref[s], (NT, ((), ())), preferred_element_type=F32)
                    + lax.dot_general(db_ref[s], wu_ref[s], (NT, ((), ())), preferred_element_type=F32))
            acc = part if acc is None else acc + part
        xhat, rr = _rms_stats(x1_ref[...])
        dg_ref[...] += jnp.sum(acc * xhat, axis=0, keepdims=True)
        dx1 = dx2_ref[...] + _rms_bwd(acc, xhat, rr, g_ref[...])
        dx1_ref[...] = dx1
        dx1b_ref[...] = dx1.astype(BF16)

    row = _rows_spec(tm, D_MODEL)
    w_spec = _resident((N_CHIPS, D_MODEL, FF_SHARD))
    return _token_call("ffn_bwd_in", body, t, tm,
                       [(da, _chips_spec(tm)), (db, _chips_spec(tm)), (w_g, w_spec), (w_u, w_spec), (x1, row), (dx2, row),
                        (g2, _full((1, D_MODEL)))],
                       [((t, D_MODEL), F32, row), ((t, D_MODEL), BF16, row)], reds=[(1, D_MODEL)])


def _group_dh(d, w_refs):
    dh = None
    for part, w_ref in enumerate(w_refs):
        term = lax.dot_general(d[:, part * ATTN_W:(part + 1) * ATTN_W], w_ref[...], (NT, ((), ())),
                               preferred_element_type=F32)
        dh = term if dh is None else dh + term
    return dh


def _in_proj_bwd(dgu, dqkvs, w_in, x, dx1, g1):
    t = x.shape[0]
    tile = min(t, TILE)
    ngroups = len(DILATIONS)

    def body(*refs):
        dgu_ref, dq_refs = refs[0], refs[1:1 + ngroups]
        w0_ref, w1_ref = refs[1 + ngroups:3 + ngroups]
        wg_refs = [refs[3 + ngroups + 3 * g:6 + ngroups + 3 * g] for g in range(ngroups)]
        x_ref, dx1_ref, g_ref, dx_ref, dg_ref = refs[3 + 4 * ngroups:5 + 4 * ngroups + 3]
        slabs = refs[5 + 4 * ngroups + 3:]
        _zero_at_start(dg_ref)
        for g in range(1, ngroups):
            dil = DILATIONS[g]
            part = _group_dh(dq_refs[g][...].reshape(tile, GROUP_COLS), wg_refs[g])
            for r in range(dil):
                _put_class_rows(slabs[g - 1], r, dil, part[r * (tile // dil):(r + 1) * (tile // dil)])
        dh = lax.dot_general(dgu_ref[:, 0:GU_HALF], w0_ref[...], (NT, ((), ())), preferred_element_type=F32)
        dh = dh + lax.dot_general(dgu_ref[:, GU_HALF:], w1_ref[...], (NT, ((), ())), preferred_element_type=F32)
        dh = dh + _group_dh(dq_refs[0][0], wg_refs[0])
        for slab in slabs:
            dh = dh + _from_slabs(slab)
        xhat, rr = _rms_stats(x_ref[...])
        dg_ref[...] += jnp.sum(dh * xhat, axis=0, keepdims=True)
        dx_ref[...] = dx1_ref[...] + _rms_bwd(dh, xhat, rr, g_ref[...])

    row = _rows_spec(tile, D_MODEL)
    group_ins = [(dqkvs[g].reshape(d, t // d, GROUP_COLS), _group_spec(d, tile, GROUP_COLS)) for g, d in enumerate(DILATIONS)]
    w_specs = _gu_w_specs() + [s for g in range(ngroups) for s in _group_w_specs(g)]
    return _token_call(
        "in_proj_bwd", body, t, tile,
        [(dgu, _rows_spec(tile, GU_COLS))] + group_ins + [(w_in, s) for s in w_specs]
        + [(x, row), (dx1, row), (g1, _full((1, D_MODEL)))],
        [((t, D_MODEL), F32, row)], reds=[(1, D_MODEL)], scratch=[_slabs(tile, D_MODEL)] * (ngroups - 1))


WGRAD_TK = 2048


def _wgrad_mm(name, grid, a, a_spec, b, b_spec, acc_shape, out_shape, out_spec, dst=None):
    nk = grid[-1]

    def body(*refs):
        a_ref, b_ref, o_ref, acc_ref = refs[0], refs[1], refs[-2], refs[-1]
        k = pl.program_id(len(grid) - 1)
        part = lax.dot_general(a_ref[...], b_ref[...], (TN, ((), ())), preferred_element_type=F32)

        @pl.when(k == 0)
        def _():
            acc_ref[...] = part

        @pl.when(k > 0)
        def _():
            acc_ref[...] += part

        @pl.when(k == nk - 1)
        def _():
            o_ref[...] = acc_ref[...].astype(BF16)

    filled = [] if dst is None else [dst]
    return pl.pallas_call(
        body, grid=grid, in_specs=[a_spec, b_spec] + [pl.BlockSpec(memory_space=pl.ANY)] * len(filled),
        out_specs=out_spec, out_shape=jax.ShapeDtypeStruct(out_shape, BF16), scratch_shapes=[pltpu.VMEM(acc_shape, F32)],
        input_output_aliases={2: 0} if filled else {}, compiler_params=_cparams(len(grid)), name=name)(a, b, *filled)


def _wgrad_2d(name, a, b, tm, tn):
    t, k1 = a.shape
    n = b.shape[1]
    tk = min(t, WGRAD_TK)
    return _wgrad_mm(name, (k1 // tm, n // tn, t // tk), a, pl.BlockSpec((tk, tm), lambda i, j, k: (k, i)),
                     b, pl.BlockSpec((tk, tn), lambda i, j, k: (k, j)), (tm, tn), (k1, n),
                     pl.BlockSpec((tm, tn), lambda i, j, k: (i, j)))


def _wgrad_in(hs, dgu, dqkvs):
    t = dgu.shape[0]
    tk = min(t, WGRAD_TK)
    gu_block = QKV_BLOCKS * ATTN_W // GU_HALF
    parts = [(hs[0], dgu, GU_HALF, lambda j: j + gu_block)]
    parts += [(hs[g].reshape(t, D_MODEL), dqkvs[g], ATTN_W, lambda j, g=g: _w_in_block(j, g)) for g in range(3)]
    dst = None
    for n, (a, b, tn, block_of) in enumerate(parts):
        dst = _wgrad_mm(f"wgrad_in_{n}", (1, b.shape[1] // tn, t // tk),
                        a, pl.BlockSpec((tk, D_MODEL), lambda i, j, k: (k, 0)), b, pl.BlockSpec((tk, tn), lambda i, j, k: (k, j)),
                        (D_MODEL, tn), (D_MODEL, IN_COLS),
                        pl.BlockSpec((D_MODEL, tn), lambda i, j, k, block_of=block_of: (0, block_of(j))), dst=dst)
    return dst


def _wgrad_ff_in(name, h2, da):
    t = h2.shape[0]
    tk = min(t, WGRAD_TK)
    return _wgrad_mm(name, (N_CHIPS, 1, t // tk), h2, pl.BlockSpec((tk, D_MODEL), lambda i, j, k: (k, 0)),
                     da, pl.BlockSpec((None, tk, FF_SHARD), lambda i, j, k: (i, k, 0)), (D_MODEL, FF_SHARD),
                     (N_CHIPS, D_MODEL, FF_SHARD), pl.BlockSpec((None, D_MODEL, FF_SHARD), lambda i, j, k: (i, 0, 0)))


def _wgrad_ff_down(ff, dx2b):
    t = dx2b.shape[0]
    tk = min(t, WGRAD_TK)
    return _wgrad_mm("wgrad_ffn_down", (N_CHIPS, 1, t // tk), ff, pl.BlockSpec((None, tk, FF_SHARD), lambda i, j, k: (i, k, 0)),
                     dx2b, pl.BlockSpec((tk, D_MODEL), lambda i, j, k: (k, 0)), (FF_SHARD, D_MODEL),
                     (N_CHIPS, FF_SHARD, D_MODEL), pl.BlockSpec((None, FF_SHARD, D_MODEL), lambda i, j, k: (i, 0, 0)))


def _local_step(x, pos_col, tgt, g1, ln_g, ln_b, w_s, b_s, g2, gf, first_weight, late_weights, on_grads=None):
    tables = _rope_tables(pos_col)
    bias_exp = jnp.repeat(jnp.transpose(b_s), SGU_W // SGU_GROUPS, axis=1)

    hs = _norm_fwd(x, g1)
    w_p = first_weight([hs[0], bias_exp] + [table for pair in tables for table in pair])
    gu, qkvs = _in_proj(hs, w_p, tables)
    os_, ls_ = [], []
    for g, dil in enumerate(DILATIONS):
        o, lse = _attn_fwd(qkvs[g], g, dil)
        os_.append(o)
        ls_.append(lse)
    attn = _combine_fwd(os_, ls_)
    sgu = _sgu_fwd(gu, ln_g, ln_b, w_s, bias_exp)
    w_pa, w_ps, w_out, w_g, w_u, w_d = late_weights(attn)
    pa, ps, merged, x1, h2 = _merge_fwd(attn, sgu, gu, x, w_pa, w_ps, w_out, g2)
    fa, fb, ff = _ffn_fwd(h2, w_g, w_u)
    dx2, dx2b, loss, dgf = _ffn_down_loss(ff, w_d, x1, tgt, gf)

    da, db = _ffn_bwd_act(dx2b, w_d, fa, fb)
    dw_d = _wgrad_ff_down(ff, dx2b)
    dx1, dx1b, dg2 = _ffn_bwd_in(da, db, w_g, w_u, x1, dx2, g2)
    dw_g = _wgrad_ff_in("wgrad_ffn_gate", h2, da)
    dw_u = _wgrad_ff_in("wgrad_ffn_up", h2, db)

    dgu, dpa, dps, dattn, dsgu = _merge_bwd(dx1b, gu, pa, ps, w_pa, w_ps, w_out)
    dw_out = _wgrad_2d("wgrad_out", merged, dx1b, D_MODEL, D_MODEL)
    dw_pa = _wgrad_2d("wgrad_proj_attn", attn, dpa, ATTN_W, D_MODEL)
    dw_ps = _wgrad_2d("wgrad_proj_sgu", sgu, dps, SGU_W, D_MODEL)
    if on_grads is not None:
        ln_g = ln_g + on_grads(1, dict(w_proj_attn=dw_pa, w_proj_sgu=dw_ps, w_out=dw_out, w_ffn_gate=dw_g, w_ffn_up=dw_u,
                                       w_ffn_down=dw_d))[:, :SGU_W]
    dgu, dw_s, dbias, dln_g, dln_b = _sgu_bwd(dgu, gu, dsgu, ln_g, ln_b, w_s, bias_exp)
    dos, ccs = _combine_bwd(dattn, os_, ls_)
    dqkvs = [_attn_bwd(qkvs[g], dos[g], ccs[g], ls_[g], *tables[g], g, dil) for g, dil in enumerate(DILATIONS)]
    dw_p = _wgrad_in(hs, dgu, dqkvs)
    if on_grads is not None:
        g1 = g1 + on_grads(0, dict(w_in=dw_p))
    dx, dg1 = _in_proj_bwd(dgu, dqkvs, w_p, x, dx1, g1)

    db_s = jnp.transpose(dbias[:, ::SGU_W // SGU_GROUPS])
    small = dict(loss=loss, norm1_g=dg1, sgu_ln_g=dln_g, sgu_ln_b=dln_b, w_spatial=dw_s, b_spatial=db_s,
                 norm2_g=dg2, final_g=dgf)
    big = dict(w_in=dw_p, w_proj_attn=dw_pa, w_proj_sgu=dw_ps, w_out=dw_out, w_ffn_gate=dw_g, w_ffn_up=dw_u,
               w_ffn_down=dw_d)
    return dx, big, small


def _ew(name, fn, ins, out_dtypes):
    shp = ins[0].shape
    rows, cols = shp
    tr = next((cand for cand in (256, 352, 128) if rows % cand == 0 and rows > cand), rows)

    def body(*refs):
        res = fn(*[r[...] for r in refs[:len(ins)]])
        for o_ref, v in zip(refs[len(ins):], res):
            o_ref[...] = v.astype(o_ref.dtype)

    spec = pl.BlockSpec((tr, cols), lambda i: (i, 0))
    return pl.pallas_call(
        body, grid=(rows // tr,), in_specs=[spec] * len(ins), out_specs=[spec] * len(out_dtypes),
        out_shape=[jax.ShapeDtypeStruct(shp, d) for d in out_dtypes],
        compiler_params=_cparams(1), name=name)(*ins)


def _adamw_math(g, w, m, v):
    m = ADAM_B1 * m + (1.0 - ADAM_B1) * g
    v = ADAM_B2 * v + (1.0 - ADAM_B2) * (g * g)
    m_hat = m / (1.0 - ADAM_B1 ** ADAM_STEP)
    v_hat = v / (1.0 - ADAM_B2 ** ADAM_STEP)
    delta = -ADAM_LR * (m_hat / (jnp.sqrt(v_hat) + ADAM_EPS) + ADAM_WD * w)
    return delta, m, v


def _adamw(name, g, w, m, v):
    return _ew(name, lambda g_, w_, m_, v_: (g_,) + _adamw_math(g_, w_, m_, v_), [g, w, m, v], [F32] * 4)


VMEM_SPEC = pl.BlockSpec(memory_space=pltpu.VMEM)


def _for_row_chunks(rows, fn):
    ck = next(c for c in (64, 32, 16) if rows % c == 0)

    def step(i, carry):
        fn(pl.multiple_of(i * ck, ck), ck)
        return carry

    lax.fori_loop(0, rows // ck, step, 0)


def _place():
    x, y, c = lax.axis_index("x"), lax.axis_index("y"), lax.axis_index("c")
    chips = [(1 - x, y), (x, 1 - y), (1 - x, 1 - y)]
    return x, y, c, 2 * x + y, chips


def _rows(ref, start, size):
    if len(ref.shape) == 2:
        return ref.at[pl.ds(start, size), :]
    return ref.at[:, pl.ds(start, size), :]


def _comm_call(name, body, ins, out_shapes, scratch, n_remote):
    return pl.pallas_call(
        body, in_specs=[VMEM_SPEC] * len(ins), out_specs=[VMEM_SPEC] * len(out_shapes),
        out_shape=out_shapes,
        scratch_shapes=list(scratch) + [pltpu.SemaphoreType.DMA((n_remote,)), pltpu.SemaphoreType.DMA((n_remote,))],
        compiler_params=pltpu.CompilerParams(vmem_limit_bytes=VMEM_LIMIT), name=name)(*ins)


def _gather_finish(name, shard, landed):
    k_rows, n = shard.shape
    kh = k_rows // 2

    def body(shard_ref, land_ref, out_ref, send, recv):
        x, y, c, me, chips = _place()
        passed = []
        for j, chip in enumerate(chips):
            theirs = 2 * chip[0] + chip[1]
            cp = pltpu.make_async_remote_copy(
                src_ref=land_ref.at[j], dst_ref=_rows(out_ref.at[theirs], c * kh, kh), send_sem=send.at[j],
                recv_sem=recv.at[j], device_id=(x, y, 1 - c), device_id_type=MESH)
            cp.start()
            passed.append(cp)
        mine = out_ref.at[me]

        def put_own(r0, ck):
            mine[pl.ds(r0, ck), :] = shard_ref[pl.ds(r0, ck), :]

        _for_row_chunks(k_rows, put_own)
        for j, chip in enumerate(chips):
            slot = out_ref.at[2 * chip[0] + chip[1]]

            def put_half(r0, ck, j=j, slot=slot):
                slot[pl.ds(pl.multiple_of(c * kh + r0, ck), ck), :] = land_ref[j, pl.ds(r0, ck), :]

            _for_row_chunks(kh, put_half)
        for j, chip in enumerate(chips):
            other = _rows(out_ref.at[2 * chip[0] + chip[1]], (1 - c) * kh, kh)
            pltpu.make_async_remote_copy(src_ref=other, dst_ref=other, send_sem=send.at[j], recv_sem=recv.at[j],
                                         device_id=(x, y, 1 - c), device_id_type=MESH).wait_recv()
        for cp in passed:
            cp.wait_send()

    return _comm_call(name, body, [shard, landed], [jax.ShapeDtypeStruct((N_CHIPS, k_rows, n), shard.dtype)], [], 3)[0]


HBM_SPEC = pl.BlockSpec(memory_space=pltpu.HBM)
SEM_SPEC = pl.BlockSpec(memory_space=pltpu.SEMAPHORE)
DATAFLOW = pltpu.SideEffectType.DATAFLOW_SIDE_EFFECTING
TOKEN_SHAPE = (1, D_MODEL)
N_PEERS = 7
SUM_SPLIT = 4
SUM_SPLIT_ELEMS = 512 * 1024


def _peers():
    x, y, c = lax.axis_index("x"), lax.axis_index("y"), lax.axis_index("c")
    flip = lambda v, f: 1 - v if f else v
    return [(flip(x, k & 4), flip(y, k & 2), flip(c, k & 1)) for k in range(1, N_PEERS + 1)]


def _piece_shape(shape):
    return (shape[-2] // 2, shape[2] if len(shape) == 3 else shape[1] // N_CHIPS)


def _device_piece(ref, chip, core):
    kh, n4 = _piece_shape(ref.shape)
    if len(ref.shape) == 3:
        return ref.at[chip, pl.ds(core * kh, kh), :]
    return ref.at[pl.ds(core * kh, kh), pl.ds(chip * n4, n4)]


def _exchange_copies(partials, lands, send, recv):
    return [pltpu.make_async_remote_copy(
        src_ref=_device_piece(partials[t], 2 * px + py, pc), dst_ref=lands[t].at[k], send_sem=send.at[t * N_PEERS + k],
        recv_sem=recv.at[t * N_PEERS + k], device_id=(px, py, pc), device_id_type=MESH)
        for t in range(len(partials)) for k, (px, py, pc) in enumerate(_peers())]


def _broadcast_copies(srcs, lands, send, recv):
    return [pltpu.make_async_remote_copy(
        src_ref=srcs[t], dst_ref=lands[t].at[k], send_sem=send.at[t * N_PEERS + k], recv_sem=recv.at[t * N_PEERS + k],
        device_id=peer, device_id_type=MESH)
        for t in range(len(srcs)) for k, peer in enumerate(_peers())]


def _gather_copies(shards, lands, send, recv):
    x, y, c, me, chips = _place()
    return [pltpu.make_async_remote_copy(
        src_ref=shards[t], dst_ref=lands[t].at[me], send_sem=send.at[t * 3 + j], recv_sem=recv.at[t * 3 + j],
        device_id=(*chip, c), device_id_type=MESH)
        for t in range(len(shards)) for j, chip in enumerate(chips)]


def _gather_half_copies(shards, lands, send, recv):
    x, y, c, me, chips = _place()
    return [pltpu.make_async_remote_copy(
        src_ref=_rows(shards[t], c * (shards[t].shape[0] // 2), shards[t].shape[0] // 2), dst_ref=lands[t].at[j],
        send_sem=send.at[t * 3 + j], recv_sem=recv.at[t * 3 + j], device_id=(*chip, c), device_id_type=MESH)
        for t in range(len(shards)) for j, chip in enumerate(chips)]


def _split_start(name, copies, per_tensor, srcs, land_shapes):
    nt = len(srcs)
    lands = [lax.empty(s, a.dtype) for s, a in zip(land_shapes, srcs)]
    nsem = nt * per_tensor

    def body(*refs):
        send, recv = refs[2 * nt], refs[2 * nt + 1]
        for cp in copies(refs[:nt], refs[nt:2 * nt], send, recv):
            cp.start()
        refs[-1][...] = jnp.zeros(TOKEN_SHAPE, F32)

    hbm = lambda a: pltpu.with_memory_space_constraint(a, pltpu.HBM)
    outs = pl.pallas_call(
        body, name=name,
        out_shape=[pltpu.SemaphoreType.DMA((nsem,)), pltpu.SemaphoreType.DMA((nsem,))]
        + [pltpu.HBM(s.shape, s.dtype) for s in srcs] + [pltpu.HBM(l.shape, l.dtype) for l in lands]
        + [jax.ShapeDtypeStruct(TOKEN_SHAPE, F32)],
        in_specs=[HBM_SPEC] * (2 * nt), out_specs=[SEM_SPEC, SEM_SPEC] + [HBM_SPEC] * (2 * nt) + [VMEM_SPEC],
        input_output_aliases={i: 2 + i for i in range(2 * nt)},
        compiler_params=pltpu.CompilerParams(has_side_effects=DATAFLOW))(*[hbm(a) for a in list(srcs) + lands])
    return outs[0], outs[1], outs[2:2 + nt], outs[2 + nt:2 + 2 * nt], outs[-1]


def _split_wait(name, copies, send, recv, srcs, lands, after):
    nt = len(srcs)
    after = list(after) if isinstance(after, (list, tuple)) else [after]

    def body(*refs):
        for cp in copies(refs[:nt], refs[nt:2 * nt], refs[2 * nt], refs[2 * nt + 1]):
            cp.wait_send()
            cp.wait_recv()

    outs = pl.pallas_call(
        body, name=name,
        out_shape=[pltpu.HBM(s.shape, s.dtype) for s in srcs] + [pltpu.HBM(l.shape, l.dtype) for l in lands],
        in_specs=[HBM_SPEC] * (2 * nt) + [SEM_SPEC, SEM_SPEC] + [pl.BlockSpec(memory_space=pl.ANY)] * len(after),
        out_specs=[HBM_SPEC] * (2 * nt), input_output_aliases={i: i for i in range(2 * nt)},
        compiler_params=pltpu.CompilerParams(has_side_effects=DATAFLOW))(*srcs, *lands, send, recv, *after)
    return outs[:nt], outs[nt:]


def _device_sum(name, partials, lands):
    nt = len(partials)
    pieces = [_piece_shape(p.shape) for p in partials]
    units = []
    for t, (kh, n4) in enumerate(pieces):
        split = SUM_SPLIT if kh * n4 >= SUM_SPLIT_ELEMS else 1
        units += [(t, j * (kh // split), kh // split) for j in range(split)]
    nu = len(units)

    def body(*refs):
        ins, slots, outs = refs[:nt], refs[nt:2 * nt], refs[2 * nt:3 * nt]
        owns, landed, sums = refs[3 * nt:4 * nt], refs[4 * nt:5 * nt], refs[5 * nt:6 * nt]
        loc, send, recv = refs[6 * nt:]
        x, y, c, me, chips = _place()
        sibling = (x, y, 1 - c)
        loads = []
        for u, (t, r0, rows) in enumerate(units):
            loads.append((
                pltpu.make_async_copy(_rows(_device_piece(ins[t], me, c), r0, rows), _rows(owns[t], r0, rows), loc.at[0, u]),
                pltpu.make_async_copy(_rows(slots[t], r0, rows), _rows(landed[t], r0, rows), loc.at[1, u])))
            for cp in loads[-1]:
                cp.start()
        stores = []
        for u, (t, r0, rows) in enumerate(units):
            for cp in loads[u]:
                cp.wait()

            def add(q0, ck, own=owns[t], slot=landed[t], dst=sums[t], r0=r0):
                at = pl.ds(pl.multiple_of(r0 + q0, ck), ck)
                acc = own[at, :].astype(F32)
                for k in range(N_PEERS):
                    acc = acc + slot[k, at, :].astype(F32)
                dst[at, :] = acc

            _for_row_chunks(rows, add)
            mine = _rows(outs[t], c * pieces[t][0] + r0, rows)
            stores.append((
                pltpu.make_async_copy(_rows(sums[t], r0, rows), mine, loc.at[2, u]),
                pltpu.make_async_remote_copy(src_ref=_rows(sums[t], r0, rows), dst_ref=mine, send_sem=send.at[u],
                                             recv_sem=recv.at[u], device_id=sibling, device_id_type=MESH)))
            for cp in stores[-1]:
                cp.start()
        for u, (t, r0, rows) in enumerate(units):
            pltpu.make_async_remote_copy(
                src_ref=_rows(sums[t], r0, rows), dst_ref=_rows(outs[t], (1 - c) * pieces[t][0] + r0, rows),
                send_sem=send.at[u], recv_sem=recv.at[u], device_id=sibling, device_id_type=MESH).wait_recv()
            stores[u][0].wait()
            stores[u][1].wait_send()

    any_spec = pl.BlockSpec(memory_space=pl.ANY)
    return pl.pallas_call(
        body, in_specs=[any_spec] * (2 * nt), out_specs=[any_spec] * nt,
        out_shape=[jax.ShapeDtypeStruct((2 * kh, n4), F32) for kh, n4 in pieces],
        scratch_shapes=[pltpu.VMEM(p, BF16) for p in pieces] + [pltpu.VMEM((N_PEERS,) + p, BF16) for p in pieces]
        + [pltpu.VMEM(p, F32) for p in pieces]
        + [pltpu.SemaphoreType.DMA((3, nu)), pltpu.SemaphoreType.DMA((nu,)), pltpu.SemaphoreType.DMA((nu,))],
        compiler_params=pltpu.CompilerParams(vmem_limit_bytes=VMEM_LIMIT), name=name)(*partials, *lands)


VEC_SHAPE = (8, D_MODEL + LANES)
VEC_SLOTS = dict(norm1_g=(slice(0, 1), slice(0, D_MODEL)), norm2_g=(slice(1, 2), slice(0, D_MODEL)),
                 final_g=(slice(2, 3), slice(0, D_MODEL)), sgu_ln_g=(slice(3, 4), slice(0, SGU_W)),
                 sgu_ln_b=(slice(3, 4), slice(SGU_W, 2 * SGU_W)), b_spatial=(slice(0, 8), slice(D_MODEL, D_MODEL + LANES)),
                 loss=(slice(4, 5), slice(0, LANES)))
VEC_PARAMS = ("norm1_g", "norm2_g", "final_g", "sgu_ln_g", "sgu_ln_b", "b_spatial")
SMALL_PARAMS = VEC_PARAMS + ("w_spatial",)
W_SPATIAL_2D = (SGU_GROUPS * SGU_CHUNK, SGU_CHUNK)


SMALL_GRADS = VEC_PARAMS + ("loss", "w_spatial")


def _small_shape(name):
    if name == "w_spatial":
        return W_SPATIAL_2D
    rows, cols = VEC_SLOTS[name]
    return (rows.stop - rows.start, cols.stop - cols.start)


def _pack_small(dst, parts):
    dst[...] = jnp.zeros(VEC_SHAPE, F32)
    for n, ref in parts.items():
        if n in VEC_SLOTS:
            dst[VEC_SLOTS[n]] = ref[...]


def _small_start(partials):
    names = VEC_PARAMS + ("loss",)

    def body(*refs):
        _pack_small(refs[-1], dict(zip(names, refs[:-1])))

    vec = pl.pallas_call(
        body, in_specs=[VMEM_SPEC] * len(names), out_specs=VMEM_SPEC, out_shape=jax.ShapeDtypeStruct(VEC_SHAPE, F32),
        name="small_params_pack")(*[partials[n].reshape(_small_shape(n)) for n in names])
    srcs = [vec, partials["w_spatial"].reshape(W_SPATIAL_2D)]
    return _split_start("small_params_start", _broadcast_copies, N_PEERS, srcs, [(N_PEERS,) + s.shape for s in srcs])


def _small_finish(started, after, w, m, v):
    own, landed = _split_wait("small_params_wait", _broadcast_copies, *started, after)
    ng, npar = len(SMALL_GRADS), len(SMALL_PARAMS)

    def update_body(*refs):
        vec_own, ws_own, vec_slots, ws_slots = refs[:4]
        w_in, m_in, v_in = (dict(zip(SMALL_PARAMS, refs[4 + k * npar:4 + (k + 1) * npar])) for k in range(3))
        o0 = 4 + 3 * npar
        g_out = dict(zip(SMALL_GRADS, refs[o0:o0 + ng]))
        d_out, m_out, v_out = (dict(zip(SMALL_PARAMS, refs[o0 + ng + k * npar:o0 + ng + (k + 1) * npar])) for k in range(3))
        vg, vw, vm, vv = refs[o0 + ng + 3 * npar:]
        me = 4 * lax.axis_index("x") + 2 * lax.axis_index("y") + lax.axis_index("c")

        def device_sum(mine, slots, read):
            acc = None
            for i in range(N_PEERS + 1):
                k = me ^ i
                part = jnp.where(k == 0, read(mine), read(slots.at[jnp.maximum(k, 1) - 1]))
                acc = part if acc is None else acc + part
            return acc

        vg[...] = device_sum(vec_own, vec_slots, lambda ref: ref[...])
        _pack_small(vw, w_in)
        _pack_small(vm, m_in)
        _pack_small(vv, v_in)
        d_vec, m_vec, v_vec = _adamw_math(vg[...], vw[...], vm[...], vv[...])
        vw[...] = d_vec
        vm[...] = m_vec
        vv[...] = v_vec
        for n in VEC_PARAMS + ("loss",):
            g_out[n][...] = vg[VEC_SLOTS[n]]
        for n in VEC_PARAMS:
            d_out[n][...] = vw[VEC_SLOTS[n]]
            m_out[n][...] = vm[VEC_SLOTS[n]]
            v_out[n][...] = vv[VEC_SLOTS[n]]

        def spatial(r0, ck):
            rows = pl.ds(r0, ck)
            g = device_sum(ws_own, ws_slots, lambda ref: ref[rows, :])
            d_, m_, v_ = _adamw_math(g, w_in["w_spatial"][rows, :], m_in["w_spatial"][rows, :], v_in["w_spatial"][rows, :])
            g_out["w_spatial"][rows, :] = g
            d_out["w_spatial"][rows, :] = d_
            m_out["w_spatial"][rows, :] = m_
            v_out["w_spatial"][rows, :] = v_

        _for_row_chunks(W_SPATIAL_2D[0], spatial)

    ins = list(own) + list(landed)
    for src in (w, m, v):
        ins += [src[n].reshape(_small_shape(n)) for n in SMALL_PARAMS]
    out_shapes = [jax.ShapeDtypeStruct(_small_shape(n), F32) for n in SMALL_GRADS + SMALL_PARAMS * 3]
    outs = pl.pallas_call(
        update_body, in_specs=[VMEM_SPEC] * len(ins), out_specs=[VMEM_SPEC] * len(out_shapes), out_shape=out_shapes,
        scratch_shapes=[pltpu.VMEM(VEC_SHAPE, F32)] * 4, name="small_params_update")(*ins)
    grads = dict(zip(SMALL_GRADS, outs[:ng]))
    rest = [dict(zip(SMALL_PARAMS, outs[ng + k * npar:ng + (k + 1) * npar])) for k in range(3)]
    return grads, rest[0], rest[1], rest[2]


BIG = ("w_in", "w_proj_attn", "w_proj_sgu", "w_out", "w_ffn_gate", "w_ffn_up", "w_ffn_down")
COMM_GROUPS = (("w_in",), ("w_proj_attn", "w_proj_sgu", "w_out", "w_ffn_gate", "w_ffn_up", "w_ffn_down"))
WEIGHTS = ("norm1_g", "w_in", "sgu_ln_g", "sgu_ln_b", "w_spatial", "b_spatial", "w_proj_attn", "w_proj_sgu", "w_out",
           "norm2_g", "w_ffn_gate", "w_ffn_up", "w_ffn_down", "final_g")


def _cols_from_chips(g):
    return jnp.transpose(g, (1, 0, 2)).reshape(g.shape[1], N_CHIPS * g.shape[2])


def kernel(x, positions, norm1_g, w_in, sgu_ln_g, sgu_ln_b, w_spatial, b_spatial, w_proj_attn, w_proj_sgu, w_out, norm2_g, w_ffn_gate, w_ffn_up, w_ffn_down, final_g, loss_target, m_norm1_g, m_w_in, m_sgu_ln_g, m_sgu_ln_b, m_w_spatial, m_b_spatial, m_w_proj_attn, m_w_proj_sgu, m_w_out, m_norm2_g, m_w_ffn_gate, m_w_ffn_up, m_w_ffn_down, m_final_g, v_norm1_g, v_w_in, v_sgu_ln_g, v_sgu_ln_b, v_w_spatial, v_b_spatial, v_w_proj_attn, v_w_proj_sgu, v_w_out, v_norm2_g, v_w_ffn_gate, v_w_ffn_up, v_w_ffn_down, v_final_g):
    w = dict(norm1_g=norm1_g, w_in=w_in, sgu_ln_g=sgu_ln_g, sgu_ln_b=sgu_ln_b, w_spatial=w_spatial, b_spatial=b_spatial,
             w_proj_attn=w_proj_attn, w_proj_sgu=w_proj_sgu, w_out=w_out, norm2_g=norm2_g, w_ffn_gate=w_ffn_gate,
             w_ffn_up=w_ffn_up, w_ffn_down=w_ffn_down, final_g=final_g)
    m = dict(norm1_g=m_norm1_g, w_in=m_w_in, sgu_ln_g=m_sgu_ln_g, sgu_ln_b=m_sgu_ln_b, w_spatial=m_w_spatial,
             b_spatial=m_b_spatial, w_proj_attn=m_w_proj_attn, w_proj_sgu=m_w_proj_sgu, w_out=m_w_out, norm2_g=m_norm2_g,
             w_ffn_gate=m_w_ffn_gate, w_ffn_up=m_w_ffn_up, w_ffn_down=m_w_ffn_down, final_g=m_final_g)
    v = dict(norm1_g=v_norm1_g, w_in=v_w_in, sgu_ln_g=v_sgu_ln_g, sgu_ln_b=v_sgu_ln_b, w_spatial=v_w_spatial,
             b_spatial=v_b_spatial, w_proj_attn=v_w_proj_attn, w_proj_sgu=v_w_proj_sgu, w_out=v_w_out, norm2_g=v_norm2_g,
             w_ffn_gate=v_w_ffn_gate, w_ffn_up=v_w_ffn_up, w_ffn_down=v_w_ffn_down, final_g=v_final_g)
    t = x.shape[1]

    shards = {n: _ew(f"cast_{n}", lambda a: (a,), [w[n][0]], [BF16])[0] for n in BIG}
    late = COMM_GROUPS[1]
    k_in, n_in = shards["w_in"].shape
    *first, token = _split_start("gather_start_0", _gather_half_copies, 3, [shards["w_in"]], [(3, k_in // 2, n_in)])
    pending = {}

    def first_weight(after):
        srcs, filled = _split_wait("gather_wait_0", _gather_half_copies, *first, after)
        gath_in, late_shards = lax.optimization_barrier(
            (_gather_finish("gather_finish_0", srcs[0], filled[0]), [shards[n] for n in late]))
        *pending["late"], _ = _split_start(
            "gather_start_1", _gather_copies, 3, late_shards, [(N_CHIPS,) + s.shape for s in late_shards])
        return _cols_from_chips(gath_in)

    def late_weights(after):
        srcs, filled = _split_wait("gather_wait_1", _gather_copies, *pending["late"], after)
        me = 2 * lax.axis_index("x") + lax.axis_index("y")
        gath = {n: lax.dynamic_update_slice(f, s[None], (me, 0, 0)) for n, f, s in zip(late, filled, srcs)}
        return (_cols_from_chips(gath["w_proj_attn"]), _cols_from_chips(gath["w_proj_sgu"]),
                gath["w_out"].reshape(D_MODEL, D_MODEL), gath["w_ffn_gate"], gath["w_ffn_up"], gath["w_ffn_down"])

    exchanges = {}

    def on_grads(i, partials):
        if "w_out" in partials:
            partials["w_out"] = partials["w_out"].reshape(N_CHIPS, D_MODEL // N_CHIPS, D_MODEL)
        parts = [partials[n] for n in COMM_GROUPS[i]]
        *exchanges[i], started = _split_start(
            f"rs_exchange_start_{i}", _exchange_copies, N_PEERS, parts, [(N_PEERS,) + _piece_shape(p.shape) for p in parts])
        return started

    dx, _, small = _local_step(
        x[0], positions.reshape(t, 1), loss_target[0], norm1_g + token, sgu_ln_g, sgu_ln_b, w_spatial[0], b_spatial[0],
        norm2_g, final_g.reshape(1, D_MODEL), first_weight, late_weights, on_grads=on_grads)
    *small_started, small_token = _small_start(small)

    grads = {}
    for i in (1, 0):
        parts, filled = _split_wait(f"rs_exchange_wait_{i}", _exchange_copies, *exchanges[i], small_token)
        grads.update(zip(COMM_GROUPS[i], _device_sum(f"rs_device_sum_{i}", parts, filled)))

    delta, new_m, new_v, updated = {}, {}, {}, []
    for n in BIG:
        shp = w[n].shape
        flip = jnp.transpose if shp[-1] % LANES else (lambda a: a)
        outs = _adamw(f"adamw_{n}", flip(grads[n]), flip(w[n][0]), flip(m[n][0]), flip(v[n][0]))
        grads[n], delta[n], new_m[n], new_v[n] = (flip(a).reshape(shp) for a in outs)
        updated.append(outs[-1])

    g_s, d_s, m_s, v_s = _small_finish(small_started, updated, w, m, v)
    loss = g_s["loss"][0, 0]
    for n in SMALL_PARAMS:
        shp = w[n].shape
        grads[n], delta[n], new_m[n], new_v[n] = (a[n].reshape(shp) for a in (g_s, d_s, m_s, v_s))

    return (loss, dx.reshape(x.shape), *[grads[n] for n in WEIGHTS], *[delta[n] for n in WEIGHTS],
            *[new_m[n] for n in WEIGHTS], *[new_v[n] for n in WEIGHTS])
```

```python
import functools

import numpy as np
import jax
import jax.numpy as jnp
from jax import lax
from jax.experimental import pallas as pl
from jax.experimental.pallas import tpu as pltpu

F32, BF16 = jnp.float32, jnp.bfloat16
MESH = pl.DeviceIdType.MESH

D_MODEL = 1024
HEAD_DIM = 64
ATTN_W = 512
DILATIONS = (1, 4, 16)
BLK = 128
ATTN_BLOCKS_PER_STEP = 4
ROPE_DIM = 16
ROPE_THETA = 500000.0
SGU_W = 512
SGU_CHUNK = 128
SGU_GROUPS = 8
D_FF = 2816
N_CHIPS = 4
FF_SHARD = D_FF // N_CHIPS
IN_COLS = 7680
EPS = 1e-6
NEG = -1e30
LANES = 128
VMEM_LIMIT = 52 * 1024 * 1024

ADAM_LR, ADAM_B1, ADAM_B2, ADAM_EPS, ADAM_WD, ADAM_STEP = 0.001, 0.9, 0.999, 1e-08, 0.01, 10

QKV_BLOCKS = 9


def _w_in_block(part, g):
    return part * len(DILATIONS) + g


def _cparams(ngrid):
    return pltpu.CompilerParams(dimension_semantics=("arbitrary",) * ngrid, vmem_limit_bytes=VMEM_LIMIT)


def _full(shape):
    return pl.BlockSpec(shape, lambda *_: (0,) * len(shape))


def _resident(shape):
    return pl.BlockSpec(shape, lambda *_: (0,) * len(shape), pipeline_mode=pl.Buffered(1))


NT = ((1,), (1,))
TN = ((0,), (0,))


def _rope(v, cos_t, sin_t):
    half = ROPE_DIM // 2
    first = (lax.broadcasted_iota(jnp.int32, cos_t.shape, 1) % HEAD_DIM) < half
    outs = []
    for cs in range(v.shape[1] // LANES):
        x = v[:, cs * LANES:(cs + 1) * LANES]
        partner = jnp.where(first, pltpu.roll(x, LANES - half, axis=1), pltpu.roll(x, half, axis=1))
        outs.append(x * cos_t + partner * sin_t)
    return outs[0] if len(outs) == 1 else jnp.concatenate(outs, axis=1)


def _spread_heads(v2, upper):
    other = pltpu.roll(v2, HEAD_DIM, axis=1)
    h0 = jnp.where(upper, other, v2)
    h1 = jnp.where(upper, v2, other)
    return jnp.concatenate([jnp.concatenate([h0, h0], axis=1), jnp.concatenate([h1, h1], axis=1)], axis=0)


def _sigmoid(v):
    return 0.5 * jnp.tanh(0.5 * v) + 0.5


def _rms_stats(v):
    r = lax.rsqrt(jnp.mean(v * v, axis=-1, keepdims=True) + EPS)
    return v * r, r


def _rms_bwd(dy, xhat, r, g):
    dxh = dy * g
    return r * (dxh - xhat * jnp.mean(dxh * xhat, axis=-1, keepdims=True))


def _head_sum_matrix():
    idx = np.arange(ATTN_W) // HEAD_DIM
    return jnp.asarray((idx[:, None] == idx[None, :]).astype(np.float32), dtype=BF16)


def _group_sum(v, e):
    hi = v.astype(BF16)
    lo = (v - hi.astype(F32)).astype(BF16)
    return jnp.dot(hi, e, preferred_element_type=F32) + jnp.dot(lo, e, preferred_element_type=F32)


TILE = 512


def _to_slabs(slab_ref, v):
    for cs in range(slab_ref.shape[0]):
        slab_ref[cs] = v[:, cs * LANES:(cs + 1) * LANES]


def _from_slabs(slab_ref):
    return jnp.concatenate([slab_ref[cs] for cs in range(slab_ref.shape[0])], axis=1)


def _class_rows(slab_ref, r, dil):
    n = slab_ref.shape[1] // dil
    return jnp.concatenate([slab_ref.at[cs][pl.ds(r, n, stride=dil), :] for cs in range(slab_ref.shape[0])], axis=1)


def _put_class_rows(slab_ref, r, dil, v):
    n = slab_ref.shape[1] // dil
    for cs in range(slab_ref.shape[0]):
        slab_ref.at[cs][pl.ds(r, n, stride=dil), :] = v[:, cs * LANES:(cs + 1) * LANES]


def _natural_from_group(slab_ref, grp_ref):
    dil = grp_ref.shape[0]
    for r in range(dil):
        _put_class_rows(slab_ref, r, dil, grp_ref[r].astype(F32))
    return _from_slabs(slab_ref)


def _group_from_natural(slab_ref, grp_ref, v):
    dil = grp_ref.shape[0]
    _to_slabs(slab_ref, v)
    for r in range(dil):
        grp_ref[r] = _class_rows(slab_ref, r, dil).astype(grp_ref.dtype)


def _group_spec(dil, tile, width):
    return pl.BlockSpec((dil, tile // dil, width), lambda i, *_: (0, i, 0))


def _slabs(tile, width):
    return pltpu.VMEM((width // LANES, tile, LANES), F32)


def _rope_consts():
    lane = np.arange(LANES) % HEAD_DIM
    fi = lane % (ROPE_DIM // 2)
    invf = np.where(lane < ROPE_DIM, ROPE_THETA ** (-(2.0 * fi) / ROPE_DIM), 0.0)
    sgn = np.where(lane < ROPE_DIM // 2, -1.0, np.where(lane < ROPE_DIM, 1.0, 0.0))
    return (jnp.asarray(invf.astype(np.float32)).reshape(1, LANES), jnp.asarray(sgn.astype(np.float32)).reshape(1, LANES))


def _rope_tables(pos_col):
    t = pos_col.shape[0]
    tile = min(t, TILE)
    invf, sgn = _rope_consts()

    def body(p_ref, f_ref, s_ref, c0, s0, c1, s1, c2, s2, slab_c, slab_s):
        ang = p_ref[...].astype(F32) * f_ref[...]
        cos, sin = jnp.cos(ang), jnp.sin(ang) * s_ref[...]
        c0[...] = cos
        s0[...] = sin
        _group_from_natural(slab_c, c1, cos)
        _group_from_natural(slab_s, s1, sin)
        for r in range(DILATIONS[2]):
            c2[r] = _class_rows(slab_c, r, DILATIONS[2])
            s2[r] = _class_rows(slab_s, r, DILATIONS[2])

    nat = pl.BlockSpec((tile, LANES), lambda i: (i, 0))
    specs, shapes = [nat, nat], [(t, LANES)] * 2
    for d in DILATIONS[1:]:
        specs += [_group_spec(d, tile, LANES)] * 2
        shapes += [(d, t // d, LANES)] * 2
    outs = pl.pallas_call(
        body, grid=(t // tile,),
        in_specs=[pl.BlockSpec((tile, 1), lambda i: (i, 0)), _full((1, LANES)), _full((1, LANES))],
        out_specs=specs, out_shape=[jax.ShapeDtypeStruct(s, F32) for s in shapes],
        scratch_shapes=[_slabs(tile, LANES)] * 2,
        compiler_params=_cparams(1), name="rope_tables")(pos_col, invf, sgn)
    return [(outs[2 * g].reshape(t, LANES), outs[2 * g + 1].reshape(t, LANES)) for g in range(len(DILATIONS))]


def _norm_fwd(x, g):
    t = x.shape[0]
    tile = min(t, TILE)

    def body(x_ref, g_ref, h0_ref, h1_ref, h2_ref, slab):
        xhat, _ = _rms_stats(x_ref[...])
        hn = xhat * g_ref[...]
        h0_ref[...] = hn.astype(BF16)
        _group_from_natural(slab, h1_ref, hn)
        for r in range(DILATIONS[2]):
            h2_ref[r] = _class_rows(slab, r, DILATIONS[2]).astype(BF16)

    nat = pl.BlockSpec((tile, D_MODEL), lambda i: (i, 0))
    return pl.pallas_call(
        body, grid=(t // tile,),
        in_specs=[nat, _full((1, D_MODEL))],
        out_specs=[nat] + [_group_spec(d, tile, D_MODEL) for d in DILATIONS[1:]],
        out_shape=[jax.ShapeDtypeStruct((t, D_MODEL), BF16)]
        + [jax.ShapeDtypeStruct((d, t // d, D_MODEL), BF16) for d in DILATIONS[1:]],
        scratch_shapes=[_slabs(tile, D_MODEL)],
        compiler_params=_cparams(1), name="norm1_fwd")(x, g)


GU_COLS = 3072
GROUP_COLS = 1536
GU_HALF = GU_COLS // 2


def _w_in_spec(width, block):
    return pl.BlockSpec((D_MODEL, width), lambda i: (0, block), pipeline_mode=pl.Buffered(1))


def _gu_w_specs():
    first = QKV_BLOCKS * ATTN_W // GU_HALF
    return [_w_in_spec(GU_HALF, first), _w_in_spec(GU_HALF, first + 1)]


def _group_w_specs(g):
    return [_w_in_spec(ATTN_W, _w_in_block(part, g)) for part in range(3)]


def _in_proj(hs, w_in, tables):
    t = hs[0].shape[0]
    tm = min(t, 1024)

    def body_gu(h_ref, w0_ref, w1_ref, o_ref):
        h = h_ref[...]
        o_ref[:, 0:GU_HALF] = jnp.dot(h, w0_ref[...], preferred_element_type=F32).astype(BF16)
        o_ref[:, GU_HALF:] = jnp.dot(h, w1_ref[...], preferred_element_type=F32).astype(BF16)

    gu = _token_call("in_proj_gates_uv", body_gu, t, tm,
                     [(hs[0], _rows_spec(tm, D_MODEL))] + [(w_in, s) for s in _gu_w_specs()],
                     [((t, GU_COLS), BF16, _rows_spec(tm, GU_COLS))])[0]

    qkvs = []
    for g in range(len(DILATIONS)):

        def body_qkv(h_ref, wq_ref, wk_ref, wv_ref, cos_ref, sin_ref, o_ref):
            h = h_ref[...]
            cos_w, sin_w = cos_ref[...], sin_ref[...]
            q = jnp.dot(h, wq_ref[...], preferred_element_type=F32)
            o_ref[:, 0:ATTN_W] = (_rope(q, cos_w, sin_w) * HEAD_DIM ** -0.5).astype(BF16)
            k = jnp.dot(h, wk_ref[...], preferred_element_type=F32)
            o_ref[:, ATTN_W:2 * ATTN_W] = _rope(k, cos_w, sin_w).astype(BF16)
            o_ref[:, 2 * ATTN_W:] = jnp.dot(h, wv_ref[...], preferred_element_type=F32).astype(BF16)

        cos_t, sin_t = tables[g]
        qkvs.append(_token_call(
            f"in_proj_qkv_g{g}", body_qkv, t, tm,
            [(hs[g].reshape(t, D_MODEL), _rows_spec(tm, D_MODEL))] + [(w_in, s) for s in _group_w_specs(g)]
            + [(cos_t, _rows_spec(tm, LANES)), (sin_t, _rows_spec(tm, LANES))],
            [((t, GROUP_COLS), BF16, _rows_spec(tm, GROUP_COLS))])[0])
    return gu, qkvs


def _attn_masks(n):
    row = lax.broadcasted_iota(jnp.int32, (2 * BLK, 2 * BLK), 0) % BLK
    col = lax.broadcasted_iota(jnp.int32, (2 * BLK, 2 * BLK), 1)
    diff = BLK + row - col
    valid = (diff >= 0) & (diff <= BLK) & ((col >= BLK) | (n > 0))
    upper = lax.broadcasted_iota(jnp.int32, (BLK, LANES), 1) >= HEAD_DIM
    return valid, upper


def _stack_heads(v2, upper):
    zero = jnp.zeros_like(v2)
    return jnp.concatenate([jnp.where(upper, zero, v2), jnp.where(upper, v2, zero)], axis=0)


def _unstack_heads(v, upper):
    return jnp.where(upper, v[BLK:], v[:BLK])


def _attn_fwd(qkv, g, dil):
    t = qkv.shape[0]
    length = t // dil
    nb = length // BLK
    per_step = min(nb, ATTN_BLOCKS_PER_STEP)
    view = qkv.reshape(dil, length, GROUP_COLS)

    def body(q_ref, kc_ref, kp_ref, vc_ref, vp_ref, o_ref, l_ref, kwin, vwin):
        n = pl.program_id(1)
        kwin[0:BLK] = kp_ref[...]
        kwin[BLK:] = kc_ref[...]
        vwin[0:BLK] = vp_ref[...]
        vwin[BLK:] = vc_ref[...]

        def block(b, carry):
            valid, upper = _attn_masks(n * per_step + b)
            rows = pl.ds(pl.multiple_of(b * BLK, BLK), BLK)
            window = pl.ds(pl.multiple_of(b * BLK, BLK), 2 * BLK)
            slabs = [slice(p * LANES, (p + 1) * LANES) for p in range(ATTN_W // LANES)]
            ss = [lax.dot_general(_stack_heads(q_ref[rows, sl], upper), kwin[window, sl], (NT, ((), ())),
                                  preferred_element_type=F32) for sl in slabs]
            soft = []
            for s in ss:
                s = jnp.where(valid, s, NEG)
                m = jnp.max(s, axis=1, keepdims=True)
                pe = jnp.exp(s - m)
                soft.append((m, pe, jnp.sum(pe, axis=1, keepdims=True)))
            for sl, (m, pe, den) in zip(slabs, soft):
                o = jnp.dot(pe.astype(BF16), vwin[window, sl], preferred_element_type=F32) / den
                lse = jnp.broadcast_to(m + jnp.log(den), (2 * BLK, LANES))
                o_ref[rows, sl] = _unstack_heads(o, upper).astype(BF16)
                l_ref[rows, sl] = _unstack_heads(lse, upper)
            return carry

        lax.fori_loop(0, per_step, block, 0)

    rows = per_step * BLK
    cur = lambda part: pl.BlockSpec((None, rows, ATTN_W), lambda r, n: (r, n, part))
    prev = lambda part: pl.BlockSpec((None, BLK, ATTN_W), lambda r, n: (r, jnp.maximum(n * per_step - 1, 0), part))
    out_spec = pl.BlockSpec((None, rows, ATTN_W), lambda r, n: (r, n, 0))
    return pl.pallas_call(
        body, grid=(dil, nb // per_step),
        in_specs=[cur(0), cur(1), prev(1), cur(2), prev(2)],
        out_specs=[out_spec, out_spec],
        out_shape=[jax.ShapeDtypeStruct((dil, length, ATTN_W), BF16), jax.ShapeDtypeStruct((dil, length, ATTN_W), F32)],
        scratch_shapes=[pltpu.VMEM((rows + BLK, ATTN_W), BF16)] * 2,
        compiler_params=_cparams(2), name=f"attn_fwd_g{g}")(view, view, view, view, view)


def _alphas(l0, l1, l2):
    m = jnp.maximum(jnp.maximum(l0, l1), l2)
    e0, e1, e2 = jnp.exp(l0 - m), jnp.exp(l1 - m), jnp.exp(l2 - m)
    inv = 1.0 / (e0 + e1 + e2)
    return e0 * inv, e1 * inv, e2 * inv


def _natural_group_values(o_refs, l_refs, slabs):
    os_ = [o_refs[0][0].astype(F32)] + [_natural_from_group(slabs[2 * g - 2], o_refs[g]) for g in (1, 2)]
    ls_ = [l_refs[0][0]] + [_natural_from_group(slabs[2 * g - 1], l_refs[g]) for g in (1, 2)]
    return os_, ls_


def _combine_fwd(os_, ls_):
    t = os_[0].shape[1]
    tile = min(t, TILE)

    def body(o0, o1, o2, l0, l1, l2, a_ref, *slabs):
        ov, lv = _natural_group_values((o0, o1, o2), (l0, l1, l2), slabs)
        a0, a1, a2 = _alphas(*lv)
        a_ref[...] = (a0 * ov[0] + a1 * ov[1] + a2 * ov[2]).astype(BF16)

    specs = [_group_spec(d, tile, ATTN_W) for d in DILATIONS]
    return pl.pallas_call(
        body, grid=(t // tile,), in_specs=specs * 2, out_specs=pl.BlockSpec((tile, ATTN_W), lambda i: (i, 0)),
        out_shape=jax.ShapeDtypeStruct((t, ATTN_W), BF16),
        scratch_shapes=[_slabs(tile, ATTN_W)] * 4,
        compiler_params=_cparams(1), name="combine_fwd")(*os_, *ls_)


def _combine_bwd(dattn, os_, ls_):
    t = dattn.shape[0]
    tile = min(t, TILE)
    e = _head_sum_matrix()

    def body(d_ref, o0, o1, o2, l0, l1, l2, e_ref, do0, do1, do2, c0, c1, c2, *slabs):
        ov, lv = _natural_group_values((o0, o1, o2), (l0, l1, l2), slabs)
        alphas = _alphas(*lv)
        d = d_ref[...]
        attn = alphas[0] * ov[0] + alphas[1] * ov[1] + alphas[2] * ov[2]
        s = _group_sum(d * attn, e_ref[...])
        do0[0] = (alphas[0] * d).astype(BF16)
        c0[0] = -alphas[0] * s
        for g, do_ref, c_ref in ((1, do1, c1), (2, do2, c2)):
            _group_from_natural(slabs[2 * g - 2], do_ref, alphas[g] * d)
            _group_from_natural(slabs[2 * g - 1], c_ref, -alphas[g] * s)

    specs = [_group_spec(d, tile, ATTN_W) for d in DILATIONS]
    shapes = [(d, t // d, ATTN_W) for d in DILATIONS]
    outs = pl.pallas_call(
        body, grid=(t // tile,),
        in_specs=[pl.BlockSpec((tile, ATTN_W), lambda i: (i, 0))] + specs * 2 + [_full((ATTN_W, ATTN_W))],
        out_specs=specs * 2,
        out_shape=[jax.ShapeDtypeStruct(s, BF16) for s in shapes] + [jax.ShapeDtypeStruct(s, F32) for s in shapes],
        scratch_shapes=[_slabs(tile, ATTN_W)] * 4,
        compiler_params=_cparams(1), name="combine_bwd")(dattn, *os_, *ls_, e)
    return outs[:3], outs[3:]


def _attn_bwd(qkv, do, cc, lse, cos_t, sin_t, g, dil):
    t = qkv.shape[0]
    length = t // dil
    nb = length // BLK
    per_step = min(nb, ATTN_BLOCKS_PER_STEP)
    nsteps = nb // per_step
    rows_per_step = per_step * BLK
    qkv_v = qkv.reshape(dil, length, GROUP_COLS)
    cos_v, sin_v = (a.reshape(dil, length, LANES) for a in (cos_t, sin_t))
    scale = HEAD_DIM ** -0.5
    dq_cols, dk_cols, dv_cols = (slice(i * ATTN_W, (i + 1) * ATTN_W) for i in range(3))

    def body(q_ref, kc_ref, kp_ref, vc_ref, vp_ref, do_ref, c_ref, l_ref, cosc, sinc, cosp, sinp,
             out_ref, acc, kwin, vwin, cwin, swin):
        n = pl.program_id(1)

        def one_block(b):
            valid, upper = _attn_masks(n * per_step + b)
            start = b * BLK if isinstance(b, int) else pl.multiple_of(b * BLK, BLK)
            rows, before, window = pl.ds(start, BLK), pl.ds(start, BLK), pl.ds(start, 2 * BLK)
            own = pl.ds(start + BLK, BLK)
            dq_parts, dkp_parts, dkc_parts, dvp_parts, dvc_parts = [], [], [], [], []
            npairs = ATTN_W // LANES
            slabs = [slice(p * LANES, (p + 1) * LANES) for p in range(npairs)]
            qss = [_stack_heads(q_ref[rows, sl], upper) for sl in slabs]
            doss = [_stack_heads(do_ref[rows, sl], upper) for sl in slabs]
            ss = [lax.dot_general(qss[p], kwin[window, slabs[p]], (NT, ((), ())), preferred_element_type=F32) for p in range(npairs)]
            dpvs = [lax.dot_general(doss[p], vwin[window, slabs[p]], (NT, ((), ())), preferred_element_type=F32)
                    for p in range(npairs)]
            pes = [jnp.exp(jnp.where(valid, ss[p], NEG) - _spread_heads(l_ref[rows, slabs[p]], upper)) for p in range(npairs)]
            dss = [(pes[p] * (dpvs[p] + _spread_heads(c_ref[rows, slabs[p]], upper))).astype(BF16) for p in range(npairs)]
            for p in range(npairs):
                qs, dos, ds = qss[p], doss[p], dss[p]
                dq2 = _unstack_heads(jnp.dot(ds, kwin[window, slabs[p]], preferred_element_type=F32), upper)
                dk2 = lax.dot_general(ds, qs, (TN, ((), ())), preferred_element_type=F32)
                dv2 = lax.dot_general(pes[p].astype(BF16), dos, (TN, ((), ())), preferred_element_type=F32)
                dq_parts.append(dq2)
                dkp_parts.append(dk2[:BLK])
                dkc_parts.append(dk2[BLK:])
                dvp_parts.append(dv2[:BLK])
                dvc_parts.append(dv2[BLK:])
            dq = _rope(jnp.concatenate(dq_parts, axis=1) * scale, cwin[own, :], swin[own, :])
            dkc = _rope(jnp.concatenate(dkc_parts, axis=1), cwin[own, :], swin[own, :])
            dkp = _rope(jnp.concatenate(dkp_parts, axis=1), cwin[before, :], swin[before, :])
            return dq, dkp, dkc, jnp.concatenate(dvp_parts, axis=1), jnp.concatenate(dvc_parts, axis=1)

        @pl.when(n < nsteps)
        def _():
            kwin[0:BLK] = kp_ref[...]
            kwin[BLK:] = kc_ref[...]
            vwin[0:BLK] = vp_ref[...]
            vwin[BLK:] = vc_ref[...]
            cwin[0:BLK] = cosp[...]
            cwin[BLK:] = cosc[...]
            swin[0:BLK] = -sinp[...]
            swin[BLK:] = -sinc[...]
            dq, dkp, dkc, dvp, dvc = one_block(0)
            last = slice(rows_per_step - BLK, rows_per_step)

            @pl.when(n > 0)
            def _():
                if per_step > 1:
                    out_ref[0:rows_per_step - BLK, :] = acc[0:rows_per_step - BLK, :].astype(BF16)
                out_ref[last, dq_cols] = acc[last, dq_cols].astype(BF16)
                out_ref[last, dk_cols] = (acc[last, dk_cols] + dkp).astype(BF16)
                out_ref[last, dv_cols] = (acc[last, dv_cols] + dvp).astype(BF16)

            acc[0:BLK, dq_cols] = dq
            acc[0:BLK, dk_cols] = dkc
            acc[0:BLK, dv_cols] = dvc

            def later(b, carry):
                dq, dkp, dkc, dvp, dvc = one_block(b)
                start = pl.multiple_of(b * BLK, BLK)
                before, rows = pl.ds(start - BLK, BLK), pl.ds(start, BLK)
                acc[before, dk_cols] += dkp
                acc[before, dv_cols] += dvp
                acc[rows, dq_cols] = dq
                acc[rows, dk_cols] = dkc
                acc[rows, dv_cols] = dvc
                return carry

            lax.fori_loop(1, per_step, later, 0)

        @pl.when(n == flush_at)
        def _():
            out_ref[...] = acc[...].astype(BF16)

    flush_at = nsteps - 1 if nsteps == 1 else nsteps
    out_lag = 0 if nsteps == 1 else 1
    nc = lambda n: jnp.minimum(n, nsteps - 1)
    npv = lambda n: jnp.maximum(jnp.minimum(n, nsteps - 1) * per_step - 1, 0)
    cur = lambda part: pl.BlockSpec((None, rows_per_step, ATTN_W), lambda r, n: (r, nc(n), part))
    prev = lambda part: pl.BlockSpec((None, BLK, ATTN_W), lambda r, n: (r, npv(n), part))
    row = pl.BlockSpec((None, rows_per_step, ATTN_W), lambda r, n: (r, nc(n), 0))
    tab_c = pl.BlockSpec((None, rows_per_step, LANES), lambda r, n: (r, nc(n), 0))
    tab_p = pl.BlockSpec((None, BLK, LANES), lambda r, n: (r, npv(n), 0))
    out_spec = pl.BlockSpec((None, rows_per_step, GROUP_COLS), lambda r, n: (r, jnp.maximum(n - out_lag, 0), 0))
    out = pl.pallas_call(
        body, grid=(dil, nsteps + out_lag),
        in_specs=[cur(0), cur(1), prev(1), cur(2), prev(2), row, row, row, tab_c, tab_c, tab_p, tab_p],
        out_specs=out_spec,
        out_shape=jax.ShapeDtypeStruct((dil, length, GROUP_COLS), BF16),
        scratch_shapes=[pltpu.VMEM((rows_per_step, GROUP_COLS), F32)]
        + [pltpu.VMEM((rows_per_step + BLK, ATTN_W), BF16)] * 2 + [pltpu.VMEM((rows_per_step + BLK, LANES), F32)] * 2,
        compiler_params=_cparams(2), name=f"attn_bwd_g{g}")(
            qkv_v, qkv_v, qkv_v, qkv_v, qkv_v, do, cc, lse, cos_v, sin_v, cos_v, sin_v)
    return out.reshape(t, GROUP_COLS)


SQRT_HALF = 0.7071067811865476
INV_SQRT_2PI = 0.3989422804014327


def _sgu_core(uv, g, b, w_ref, bias):
    cdf = 0.5 * (1.0 + lax.erf(uv * SQRT_HALF))
    z = uv * cdf
    u, v = z[:, :SGU_W], z[:, SGU_W:]
    mu = jnp.mean(v, axis=1, keepdims=True)
    xc = v - mu
    rs = lax.rsqrt(jnp.mean(xc * xc, axis=1, keepdims=True) + EPS)
    xhat = xc * rs
    vn = xhat * g + b
    row = lax.broadcasted_iota(jnp.int32, (SGU_CHUNK, SGU_CHUNK), 0)
    col = lax.broadcasted_iota(jnp.int32, (SGU_CHUNK, SGU_CHUNK), 1)
    tril = row >= col
    upper = lax.broadcasted_iota(jnp.int32, (SGU_CHUNK, LANES), 1) >= SGU_W // SGU_GROUPS
    ws, vlo, vhi, mixed = [], [], [], []
    for pr in range(SGU_W // LANES):
        sl = slice(pr * LANES, (pr + 1) * LANES)
        w0 = jnp.where(tril, w_ref[2 * pr], 0.0).astype(BF16)
        w1 = jnp.where(tril, w_ref[2 * pr + 1], 0.0).astype(BF16)
        vn2 = vn[:, sl]
        lo = jnp.where(upper, 0.0, vn2).astype(BF16)
        hi = jnp.where(upper, vn2, 0.0).astype(BF16)
        mixed.append(jnp.dot(w0, lo, preferred_element_type=F32) + jnp.dot(w1, hi, preferred_element_type=F32)
                     + bias[:, sl])
        ws.append((w0, w1))
        vlo.append(lo)
        vhi.append(hi)
    return cdf, u, xhat, rs, jnp.concatenate(mixed, axis=1), ws, vlo, vhi, tril, upper


SGU_STEP = 4 * SGU_CHUNK


def _for_chunks(step_rows, fn):
    def one(ci, carry):
        fn(pl.ds(pl.multiple_of(ci * SGU_CHUNK, SGU_CHUNK), SGU_CHUNK))
        return carry

    lax.fori_loop(0, step_rows // SGU_CHUNK, one, 0)


def _sgu_fwd(gu, ln_g, ln_b, w_s, bias_exp):
    t = gu.shape[0]
    step = min(t, SGU_STEP)

    def body(uv_ref, g_ref, b_ref, w_ref, bias_ref, o_ref):
        def chunk(rows):
            _, u, _, _, mixed, *_ = _sgu_core(uv_ref[rows, :].astype(F32), g_ref[...], b_ref[...], w_ref, bias_ref[...])
            o_ref[rows, :] = (u * mixed).astype(BF16)

        _for_chunks(step, chunk)

    return pl.pallas_call(
        body, grid=(t // step,),
        in_specs=[pl.BlockSpec((step, 2 * SGU_W), lambda n: (n, 0)), _full((1, SGU_W)), _full((1, SGU_W)),
                  _full((SGU_GROUPS, SGU_CHUNK, SGU_CHUNK)), _full((SGU_CHUNK, SGU_W))],
        out_specs=pl.BlockSpec((step, SGU_W), lambda n: (n, 0)),
        out_shape=jax.ShapeDtypeStruct((t, SGU_W), BF16),
        compiler_params=_cparams(1), name="sgu_fwd")(gu, ln_g, ln_b, w_s, bias_exp)


def _sgu_bwd(dproj, gu, dsgu, ln_g, ln_b, w_s, bias_exp):
    t = gu.shape[0]
    step = min(t, SGU_STEP)
    nsteps = t // step
    e = _head_sum_matrix()

    def body(dp_in, uv_ref, ds_ref, g_ref, b_ref, w_ref, bias_ref, e_ref, out_ref, dw_ref, dbias_ref, dg_ref, db_ref):
        n = pl.program_id(0)

        @pl.when(n == 0)
        def _():
            dw_ref[...] = jnp.zeros(dw_ref.shape, F32)
            dbias_ref[...] = jnp.zeros(dbias_ref.shape, F32)
            dg_ref[...] = jnp.zeros(dg_ref.shape, F32)
            db_ref[...] = jnp.zeros(db_ref.shape, F32)

        _for_chunks(step, functools.partial(chunk, uv_ref, ds_ref, g_ref, b_ref, w_ref, bias_ref, out_ref, dw_ref, dbias_ref,
                                            dg_ref, db_ref))

        @pl.when(n == nsteps - 1)
        def _():
            dbias_ref[...] = _group_sum(dbias_ref[...], e_ref[...])

    def chunk(uv_ref, ds_ref, g_ref, b_ref, w_ref, bias_ref, out_ref, dw_ref, dbias_ref, dg_ref, db_ref, rows):
        uv = uv_ref[rows, :].astype(F32)
        g = g_ref[...]
        cdf, u, xhat, rs, mixed, ws, vlo, vhi, tril, upper = _sgu_core(uv, g, b_ref[...], w_ref, bias_ref[...])
        dsg = ds_ref[rows, :]
        du = dsg * mixed
        dmixed = dsg * u
        dbias_ref[...] += dmixed
        dvn = []
        for pr in range(SGU_W // LANES):
            sl = slice(pr * LANES, (pr + 1) * LANES)
            dm2 = dmixed[:, sl]
            dlo = jnp.where(upper, 0.0, dm2).astype(BF16)
            dhi = jnp.where(upper, dm2, 0.0).astype(BF16)
            w0, w1 = ws[pr]
            dvn.append(lax.dot_general(w0, dlo, (TN, ((), ())), preferred_element_type=F32)
                       + lax.dot_general(w1, dhi, (TN, ((), ())), preferred_element_type=F32))
            dw0 = lax.dot_general(dlo, vlo[pr], (NT, ((), ())), preferred_element_type=F32)
            dw1 = lax.dot_general(dhi, vhi[pr], (NT, ((), ())), preferred_element_type=F32)
            dw_ref[2 * pr] += jnp.where(tril, dw0, 0.0)
            dw_ref[2 * pr + 1] += jnp.where(tril, dw1, 0.0)
        dvn = jnp.concatenate(dvn, axis=1)
        dg_ref[...] += jnp.sum(dvn * xhat, axis=0, keepdims=True)
        db_ref[...] += jnp.sum(dvn, axis=0, keepdims=True)
        dxh = dvn * g
        dv = rs * (dxh - jnp.mean(dxh, axis=1, keepdims=True) - xhat * jnp.mean(dxh * xhat, axis=1, keepdims=True))
        dz = jnp.concatenate([du, dv], axis=1)
        dgelu = cdf + uv * (INV_SQRT_2PI * jnp.exp(-0.5 * uv * uv))
        out_ref[rows, :] = (dz * dgelu).astype(BF16)

    outs = pl.pallas_call(
        body, grid=(nsteps,),
        in_specs=[pl.BlockSpec(memory_space=pl.ANY), pl.BlockSpec((step, 2 * SGU_W), lambda n: (n, 0)),
                  pl.BlockSpec((step, SGU_W), lambda n: (n, 0)), _full((1, SGU_W)), _full((1, SGU_W)),
                  _full((SGU_GROUPS, SGU_CHUNK, SGU_CHUNK)), _full((SGU_CHUNK, SGU_W)), _full((ATTN_W, ATTN_W))],
        out_specs=[pl.BlockSpec((step, 2 * SGU_W), lambda n: (n, 0)), _full((SGU_GROUPS, SGU_CHUNK, SGU_CHUNK)),
                   _full((SGU_CHUNK, SGU_W)), _full((1, SGU_W)), _full((1, SGU_W))],
        out_shape=[jax.ShapeDtypeStruct(dproj.shape, BF16), jax.ShapeDtypeStruct((SGU_GROUPS, SGU_CHUNK, SGU_CHUNK), F32),
                   jax.ShapeDtypeStruct((SGU_CHUNK, SGU_W), F32), jax.ShapeDtypeStruct((1, SGU_W), F32),
                   jax.ShapeDtypeStruct((1, SGU_W), F32)],
        input_output_aliases={0: 0},
        compiler_params=_cparams(1), name="sgu_bwd")(dproj, gu, dsgu, ln_g, ln_b, w_s, bias_exp, e)
    return outs


def _merge_fwd(attn, sgu, gu, x, w_pa, w_ps, w_out, g2):
    t = x.shape[0]
    tm = min(t, 512)

    def body(a_ref, s_ref, ga_ref, gb_ref, x_ref, wpa, wps, wo, g_ref, pa_ref, ps_ref, m_ref, x1_ref, h2_ref):
        pa = jnp.dot(a_ref[...], wpa[...], preferred_element_type=F32)
        ps = jnp.dot(s_ref[...], wps[...], preferred_element_type=F32)
        merged = (_sigmoid(ga_ref[...].astype(F32)) * pa + _sigmoid(gb_ref[...].astype(F32)) * ps).astype(BF16)
        x1 = x_ref[...] + jnp.dot(merged, wo[...], preferred_element_type=F32)
        xhat, _ = _rms_stats(x1)
        pa_ref[...] = pa.astype(BF16)
        ps_ref[...] = ps.astype(BF16)
        m_ref[...] = merged
        x1_ref[...] = x1
        h2_ref[...] = (xhat * g_ref[...]).astype(BF16)

    half = pl.BlockSpec((tm, ATTN_W), lambda i: (i, 0))
    full = pl.BlockSpec((tm, D_MODEL), lambda i: (i, 0))
    return pl.pallas_call(
        body, grid=(t // tm,),
        in_specs=[half, half, pl.BlockSpec((tm, D_MODEL), lambda i: (i, 1)), pl.BlockSpec((tm, D_MODEL), lambda i: (i, 2)),
                  full, _resident((ATTN_W, D_MODEL)), _resident((SGU_W, D_MODEL)), _resident((D_MODEL, D_MODEL)),
                  _full((1, D_MODEL))],
        out_specs=[full] * 5,
        out_shape=[jax.ShapeDtypeStruct((t, D_MODEL), BF16), jax.ShapeDtypeStruct((t, D_MODEL), BF16),
                   jax.ShapeDtypeStruct((t, D_MODEL), BF16), jax.ShapeDtypeStruct((t, D_MODEL), F32),
                   jax.ShapeDtypeStruct((t, D_MODEL), BF16)],
        compiler_params=_cparams(1), name="merge_fwd")(attn, sgu, gu, gu, x, w_pa, w_ps, w_out, g2)


def _merge_bwd(dx1b, gu, pa, ps, w_pa, w_ps, w_out):
    t = dx1b.shape[0]
    tm = min(t, 512)

    def body(d_ref, ga_ref, gb_ref, pa_ref, ps_ref, wpa, wps, wo, out_ref, dpa_ref, dps_ref, da_ref, dsg_ref):
        dm = lax.dot_general(d_ref[...], wo[...], (NT, ((), ())), preferred_element_type=F32)
        sa, sb = _sigmoid(ga_ref[...].astype(F32)), _sigmoid(gb_ref[...].astype(F32))
        dpa = (dm * sa).astype(BF16)
        dps = (dm * sb).astype(BF16)
        out_ref[:, 0:D_MODEL] = jnp.zeros((tm, D_MODEL), BF16)
        out_ref[:, D_MODEL:2 * D_MODEL] = (dm * pa_ref[...].astype(F32) * sa * (1.0 - sa)).astype(BF16)
        out_ref[:, 2 * D_MODEL:GU_COLS] = (dm * ps_ref[...].astype(F32) * sb * (1.0 - sb)).astype(BF16)
        dpa_ref[...] = dpa
        dps_ref[...] = dps
        da_ref[...] = lax.dot_general(dpa, wpa[...], (NT, ((), ())), preferred_element_type=F32)
        dsg_ref[...] = lax.dot_general(dps, wps[...], (NT, ((), ())), preferred_element_type=F32)

    half = pl.BlockSpec((tm, ATTN_W), lambda i: (i, 0))
    full = pl.BlockSpec((tm, D_MODEL), lambda i: (i, 0))
    return pl.pallas_call(
        body, grid=(t // tm,),
        in_specs=[full, pl.BlockSpec((tm, D_MODEL), lambda i: (i, 1)),
                  pl.BlockSpec((tm, D_MODEL), lambda i: (i, 2)), full, full,
                  _resident((ATTN_W, D_MODEL)), _resident((SGU_W, D_MODEL)), _resident((D_MODEL, D_MODEL))],
        out_specs=[pl.BlockSpec((tm, GU_COLS), lambda i: (i, 0)), full, full, half, half],
        out_shape=[jax.ShapeDtypeStruct((t, GU_COLS), BF16), jax.ShapeDtypeStruct((t, D_MODEL), BF16),
                   jax.ShapeDtypeStruct((t, D_MODEL), BF16), jax.ShapeDtypeStruct((t, ATTN_W), F32),
                   jax.ShapeDtypeStruct((t, SGU_W), F32)],
        compiler_params=_cparams(1), name="merge_bwd")(dx1b, gu, gu, pa, ps, w_pa, w_ps, w_out)


def _token_call(name, body, t, tm, ins, outs, reds=(), scratch=()):
    return pl.pallas_call(
        body, grid=(t // tm,), in_specs=[s for _, s in ins],
        out_specs=[o[2] for o in outs] + [_full(r) for r in reds],
        out_shape=[jax.ShapeDtypeStruct(o[0], o[1]) for o in outs] + [jax.ShapeDtypeStruct(r, F32) for r in reds],
        scratch_shapes=list(scratch), compiler_params=_cparams(1), name=name)(*[a for a, _ in ins])


def _rows_spec(tm, width):
    return pl.BlockSpec((tm, width), lambda i: (i, 0))


def _chips_spec(tm):
    return pl.BlockSpec((N_CHIPS, tm, FF_SHARD), lambda i: (0, i, 0))


def _zero_at_start(*refs):
    @pl.when(pl.program_id(0) == 0)
    def _():
        for r in refs:
            r[...] = jnp.zeros(r.shape, r.dtype)


def _ffn_fwd(h2, w_g, w_u):
    t = h2.shape[0]
    tm = min(t, 512)

    def body(h_ref, wg_ref, wu_ref, fa_ref, fb_ref, ff_ref):
        h = h_ref[...]
        for s in range(N_CHIPS):
            a = jnp.dot(h, wg_ref[s], preferred_element_type=F32)
            b = jnp.dot(h, wu_ref[s], preferred_element_type=F32)
            sg = _sigmoid(a)
            silu = a * sg
            fa_ref[s] = (b * (sg * (1.0 + a * (1.0 - sg)))).astype(BF16)
            fb_ref[s] = silu.astype(BF16)
            ff_ref[s] = (silu * b).astype(BF16)

    shp = (N_CHIPS, t, FF_SHARD)
    w_spec = _resident((N_CHIPS, D_MODEL, FF_SHARD))
    return _token_call("ffn_fwd", body, t, tm, [(h2, _rows_spec(tm, D_MODEL)), (w_g, w_spec), (w_u, w_spec)],
                       [(shp, BF16, _chips_spec(tm))] * 3)


def _ffn_down_loss(ff, w_d, x1, tgt, gf):
    t = x1.shape[0]
    tm = min(t, 512)

    def body(ff_ref, wd_ref, x1_ref, tgt_ref, g_ref, dx2_ref, dx2b_ref, loss_ref, dgf_ref):
        _zero_at_start(loss_ref, dgf_ref)
        acc = jnp.dot(ff_ref[0], wd_ref[0], preferred_element_type=F32)
        for s in range(1, N_CHIPS):
            acc = acc + jnp.dot(ff_ref[s], wd_ref[s], preferred_element_type=F32)
        x2 = x1_ref[...] + acc
        g = g_ref[...]
        xhat, rr = _rms_stats(x2)
        diff = xhat * g - tgt_ref[...]
        rows = jnp.sum(diff * diff, axis=1, keepdims=True)
        loss_ref[...] += jnp.broadcast_to(jnp.sum(rows, axis=0, keepdims=True) * (0.5 / D_MODEL), (1, LANES))
        dy = diff * (1.0 / D_MODEL)
        dgf_ref[...] += jnp.sum(dy * xhat, axis=0, keepdims=True)
        dx2 = _rms_bwd(dy, xhat, rr, g)
        dx2_ref[...] = dx2
        dx2b_ref[...] = dx2.astype(BF16)

    row = _rows_spec(tm, D_MODEL)
    return _token_call("ffn_down_loss", body, t, tm,
                       [(ff, _chips_spec(tm)), (w_d, _resident((N_CHIPS, FF_SHARD, D_MODEL))), (x1, row), (tgt, row),
                        (gf, _full((1, D_MODEL)))],
                       [((t, D_MODEL), F32, row), ((t, D_MODEL), BF16, row)], reds=[(1, LANES), (1, D_MODEL)])


def _ffn_bwd_act(dx2b, w_d, fa, fb):
    t = dx2b.shape[0]
    tm = min(t, 512)

    def body(d_ref, wd_ref, fa_ref, fb_ref, da_ref, db_ref):
        d = d_ref[...]
        for s in range(N_CHIPS):
            dff = lax.dot_general(d, wd_ref[s], (NT, ((), ())), preferred_element_type=F32)
            da_ref[s] = (dff * fa_ref[s].astype(F32)).astype(BF16)
            db_ref[s] = (dff * fb_ref[s].astype(F32)).astype(BF16)

    shp = (N_CHIPS, t, FF_SHARD)
    return _token_call("ffn_bwd_act", body, t, tm,
                       [(dx2b, _rows_spec(tm, D_MODEL)), (w_d, _resident((N_CHIPS, FF_SHARD, D_MODEL))),
                        (fa, _chips_spec(tm)), (fb, _chips_spec(tm))],
                       [(shp, BF16, _chips_spec(tm))] * 2)


def _ffn_bwd_in(da, db, w_g, w_u, x1, dx2, g2):
    t = x1.shape[0]
    tm = min(t, 512)

    def body(da_ref, db_ref, wg_ref, wu_ref, x1_ref, dx2_ref, g_ref, dx1_ref, dx1b_ref, dg_ref):
        _zero_at_start(dg_ref)
        acc = None
        for s in range(N_CHIPS):
            part = (lax.dot_general(da_ref[s], wg_ref[s], (NT, ((), ())), preferred_element_type=F32)
                    + lax.dot_general(db_ref[s], wu_ref[s], (NT, ((), ())), preferred_element_type=F32))
            acc = part if acc is None else acc + part
        xhat, rr = _rms_stats(x1_ref[...])
        dg_ref[...] += jnp.sum(acc * xhat, axis=0, keepdims=True)
        dx1 = dx2_ref[...] + _rms_bwd(acc, xhat, rr, g_ref[...])
        dx1_ref[...] = dx1
        dx1b_ref[...] = dx1.astype(BF16)

    row = _rows_spec(tm, D_MODEL)
    w_spec = _resident((N_CHIPS, D_MODEL, FF_SHARD))
    return _token_call("ffn_bwd_in", body, t, tm,
                       [(da, _chips_spec(tm)), (db, _chips_spec(tm)), (w_g, w_spec), (w_u, w_spec), (x1, row), (dx2, row),
                        (g2, _full((1, D_MODEL)))],
                       [((t, D_MODEL), F32, row), ((t, D_MODEL), BF16, row)], reds=[(1, D_MODEL)])


def _group_dh(d, w_refs):
    dh = None
    for part, w_ref in enumerate(w_refs):
        term = lax.dot_general(d[:, part * ATTN_W:(part + 1) * ATTN_W], w_ref[...], (NT, ((), ())),
                               preferred_element_type=F32)
        dh = term if dh is None else dh + term
    return dh


def _in_proj_bwd(dgu, dqkvs, w_in, x, dx1, g1):
    t = x.shape[0]
    tile = min(t, TILE)
    ngroups = len(DILATIONS)

    def body(*refs):
        dgu_ref, dq_refs = refs[0], refs[1:1 + ngroups]
        w0_ref, w1_ref = refs[1 + ngroups:3 + ngroups]
        wg_refs = [refs[3 + ngroups + 3 * g:6 + ngroups + 3 * g] for g in range(ngroups)]
        x_ref, dx1_ref, g_ref, dx_ref, dg_ref = refs[3 + 4 * ngroups:5 + 4 * ngroups + 3]
        slabs = refs[5 + 4 * ngroups + 3:]
        _zero_at_start(dg_ref)
        for g in range(1, ngroups):
            dil = DILATIONS[g]
            part = _group_dh(dq_refs[g][...].reshape(tile, GROUP_COLS), wg_refs[g])
            for r in range(dil):
                _put_class_rows(slabs[g - 1], r, dil, part[r * (tile // dil):(r + 1) * (tile // dil)])
        dh = lax.dot_general(dgu_ref[:, 0:GU_HALF], w0_ref[...], (NT, ((), ())), preferred_element_type=F32)
        dh = dh + lax.dot_general(dgu_ref[:, GU_HALF:], w1_ref[...], (NT, ((), ())), preferred_element_type=F32)
        dh = dh + _group_dh(dq_refs[0][0], wg_refs[0])
        for slab in slabs:
            dh = dh + _from_slabs(slab)
        xhat, rr = _rms_stats(x_ref[...])
        dg_ref[...] += jnp.sum(dh * xhat, axis=0, keepdims=True)
        dx_ref[...] = dx1_ref[...] + _rms_bwd(dh, xhat, rr, g_ref[...])

    row = _rows_spec(tile, D_MODEL)
    group_ins = [(dqkvs[g].reshape(d, t // d, GROUP_COLS), _group_spec(d, tile, GROUP_COLS)) for g, d in enumerate(DILATIONS)]
    w_specs = _gu_w_specs() + [s for g in range(ngroups) for s in _group_w_specs(g)]
    return _token_call(
        "in_proj_bwd", body, t, tile,
        [(dgu, _rows_spec(tile, GU_COLS))] + group_ins + [(w_in, s) for s in w_specs]
        + [(x, row), (dx1, row), (g1, _full((1, D_MODEL)))],
        [((t, D_MODEL), F32, row)], reds=[(1, D_MODEL)], scratch=[_slabs(tile, D_MODEL)] * (ngroups - 1))


WGRAD_TK = 2048


def _wgrad_mm(name, grid, a, a_spec, b, b_spec, acc_shape, out_shape, out_spec, dst=None):
    nk = grid[-1]

    def body(*refs):
        a_ref, b_ref, o_ref, acc_ref = refs[0], refs[1], refs[-2], refs[-1]
        k = pl.program_id(len(grid) - 1)
        part = lax.dot_general(a_ref[...], b_ref[...], (TN, ((), ())), preferred_element_type=F32)

        @pl.when(k == 0)
        def _():
            acc_ref[...] = part

        @pl.when(k > 0)
        def _():
            acc_ref[...] += part

        @pl.when(k == nk - 1)
        def _():
            o_ref[...] = acc_ref[...].astype(BF16)

    filled = [] if dst is None else [dst]
    return pl.pallas_call(
        body, grid=grid, in_specs=[a_spec, b_spec] + [pl.BlockSpec(memory_space=pl.ANY)] * len(filled),
        out_specs=out_spec, out_shape=jax.ShapeDtypeStruct(out_shape, BF16), scratch_shapes=[pltpu.VMEM(acc_shape, F32)],
        input_output_aliases={2: 0} if filled else {}, compiler_params=_cparams(len(grid)), name=name)(a, b, *filled)


def _wgrad_2d(name, a, b, tm, tn):
    t, k1 = a.shape
    n = b.shape[1]
    tk = min(t, WGRAD_TK)
    return _wgrad_mm(name, (k1 // tm, n // tn, t // tk), a, pl.BlockSpec((tk, tm), lambda i, j, k: (k, i)),
                     b, pl.BlockSpec((tk, tn), lambda i, j, k: (k, j)), (tm, tn), (k1, n),
                     pl.BlockSpec((tm, tn), lambda i, j, k: (i, j)))


def _wgrad_in(hs, dgu, dqkvs):
    t = dgu.shape[0]
    tk = min(t, WGRAD_TK)
    gu_block = QKV_BLOCKS * ATTN_W // GU_HALF
    parts = [(hs[0], dgu, GU_HALF, lambda j: j + gu_block)]
    parts += [(hs[g].reshape(t, D_MODEL), dqkvs[g], ATTN_W, lambda j, g=g: _w_in_block(j, g)) for g in range(3)]
    dst = None
    for n, (a, b, tn, block_of) in enumerate(parts):
        dst = _wgrad_mm(f"wgrad_in_{n}", (1, b.shape[1] // tn, t // tk),
                        a, pl.BlockSpec((tk, D_MODEL), lambda i, j, k: (k, 0)), b, pl.BlockSpec((tk, tn), lambda i, j, k: (k, j)),
                        (D_MODEL, tn), (D_MODEL, IN_COLS),
                        pl.BlockSpec((D_MODEL, tn), lambda i, j, k, block_of=block_of: (0, block_of(j))), dst=dst)
    return dst


def _wgrad_ff_in(name, h2, da):
    t = h2.shape[0]
    tk = min(t, WGRAD_TK)
    return _wgrad_mm(name, (N_CHIPS, 1, t // tk), h2, pl.BlockSpec((tk, D_MODEL), lambda i, j, k: (k, 0)),
                     da, pl.BlockSpec((None, tk, FF_SHARD), lambda i, j, k: (i, k, 0)), (D_MODEL, FF_SHARD),
                     (N_CHIPS, D_MODEL, FF_SHARD), pl.BlockSpec((None, D_MODEL, FF_SHARD), lambda i, j, k: (i, 0, 0)))


def _wgrad_ff_down(ff, dx2b):
    t = dx2b.shape[0]
    tk = min(t, WGRAD_TK)
    return _wgrad_mm("wgrad_ffn_down", (N_CHIPS, 1, t // tk), ff, pl.BlockSpec((None, tk, FF_SHARD), lambda i, j, k: (i, k, 0)),
                     dx2b, pl.BlockSpec((tk, D_MODEL), lambda i, j, k: (k, 0)), (FF_SHARD, D_MODEL),
                     (N_CHIPS, FF_SHARD, D_MODEL), pl.BlockSpec((None, FF_SHARD, D_MODEL), lambda i, j, k: (i, 0, 0)))


def _local_step(x, pos_col, tgt, g1, ln_g, ln_b, w_s, b_s, g2, gf, first_weight, late_weights, on_grads=None):
    tables = _rope_tables(pos_col)
    bias_exp = jnp.repeat(jnp.transpose(b_s), SGU_W // SGU_GROUPS, axis=1)

    hs = _norm_fwd(x, g1)
    w_p = first_weight([hs[0], bias_exp] + [table for pair in tables for table in pair])
    gu, qkvs = _in_proj(hs, w_p, tables)
    os_, ls_ = [], []
    for g, dil in enumerate(DILATIONS):
        o, lse = _attn_fwd(qkvs[g], g, dil)
        os_.append(o)
        ls_.append(lse)
    attn = _combine_fwd(os_, ls_)
    sgu = _sgu_fwd(gu, ln_g, ln_b, w_s, bias_exp)
    w_pa, w_ps, w_out, w_g, w_u, w_d = late_weights(attn)
    pa, ps, merged, x1, h2 = _merge_fwd(attn, sgu, gu, x, w_pa, w_ps, w_out, g2)
    fa, fb, ff = _ffn_fwd(h2, w_g, w_u)
    dx2, dx2b, loss, dgf = _ffn_down_loss(ff, w_d, x1, tgt, gf)

    da, db = _ffn_bwd_act(dx2b, w_d, fa, fb)
    dw_d = _wgrad_ff_down(ff, dx2b)
    dx1, dx1b, dg2 = _ffn_bwd_in(da, db, w_g, w_u, x1, dx2, g2)
    dw_g = _wgrad_ff_in("wgrad_ffn_gate", h2, da)
    dw_u = _wgrad_ff_in("wgrad_ffn_up", h2, db)

    dgu, dpa, dps, dattn, dsgu = _merge_bwd(dx1b, gu, pa, ps, w_pa, w_ps, w_out)
    dw_out = _wgrad_2d("wgrad_out", merged, dx1b, D_MODEL, D_MODEL)
    dw_pa = _wgrad_2d("wgrad_proj_attn", attn, dpa, ATTN_W, D_MODEL)
    dw_ps = _wgrad_2d("wgrad_proj_sgu", sgu, dps, SGU_W, D_MODEL)
    if on_grads is not None:
        ln_g = ln_g + on_grads(1, dict(w_proj_attn=dw_pa, w_proj_sgu=dw_ps, w_out=dw_out, w_ffn_gate=dw_g, w_ffn_up=dw_u,
                                       w_ffn_down=dw_d))[:, :SGU_W]
    dgu, dw_s, dbias, dln_g, dln_b = _sgu_bwd(dgu, gu, dsgu, ln_g, ln_b, w_s, bias_exp)
    dos, ccs = _combine_bwd(dattn, os_, ls_)
    dqkvs = [_attn_bwd(qkvs[g], dos[g], ccs[g], ls_[g], *tables[g], g, dil) for g, dil in enumerate(DILATIONS)]
    dw_p = _wgrad_in(hs, dgu, dqkvs)
    if on_grads is not None:
        g1 = g1 + on_grads(0, dict(w_in=dw_p))
    dx, dg1 = _in_proj_bwd(dgu, dqkvs, w_p, x, dx1, g1)

    db_s = jnp.transpose(dbias[:, ::SGU_W // SGU_GROUPS])
    small = dict(loss=loss, norm1_g=dg1, sgu_ln_g=dln_g, sgu_ln_b=dln_b, w_spatial=dw_s, b_spatial=db_s,
                 norm2_g=dg2, final_g=dgf)
    big = dict(w_in=dw_p, w_proj_attn=dw_pa, w_proj_sgu=dw_ps, w_out=dw_out, w_ffn_gate=dw_g, w_ffn_up=dw_u,
               w_ffn_down=dw_d)
    return dx, big, small


def _ew(name, fn, ins, out_dtypes):
    shp = ins[0].shape
    rows, cols = shp
    tr = next((cand for cand in (256, 352, 128) if rows % cand == 0 and rows > cand), rows)

    def body(*refs):
        res = fn(*[r[...] for r in refs[:len(ins)]])
        for o_ref, v in zip(refs[len(ins):], res):
            o_ref[...] = v.astype(o_ref.dtype)

    spec = pl.BlockSpec((tr, cols), lambda i: (i, 0))
    return pl.pallas_call(
        body, grid=(rows // tr,), in_specs=[spec] * len(ins), out_specs=[spec] * len(out_dtypes),
        out_shape=[jax.ShapeDtypeStruct(shp, d) for d in out_dtypes],
        compiler_params=_cparams(1), name=name)(*ins)


def _adamw_math(g, w, m, v):
    m = ADAM_B1 * m + (1.0 - ADAM_B1) * g
    v = ADAM_B2 * v + (1.0 - ADAM_B2) * (g * g)
    m_hat = m / (1.0 - ADAM_B1 ** ADAM_STEP)
    v_hat = v / (1.0 - ADAM_B2 ** ADAM_STEP)
    delta = -ADAM_LR * (m_hat / (jnp.sqrt(v_hat) + ADAM_EPS) + ADAM_WD * w)
    return delta, m, v


def _adamw(name, g, w, m, v):
    return _ew(name, lambda g_, w_, m_, v_: (g_,) + _adamw_math(g_, w_, m_, v_), [g, w, m, v], [F32] * 4)


VMEM_SPEC = pl.BlockSpec(memory_space=pltpu.VMEM)


def _for_row_chunks(rows, fn):
    ck = next(c for c in (64, 32, 16) if rows % c == 0)

    def step(i, carry):
        fn(pl.multiple_of(i * ck, ck), ck)
        return carry

    lax.fori_loop(0, rows // ck, step, 0)


def _place():
    x, y, c = lax.axis_index("x"), lax.axis_index("y"), lax.axis_index("c")
    chips = [(1 - x, y), (x, 1 - y), (1 - x, 1 - y)]
    return x, y, c, 2 * x + y, chips


def _rows(ref, start, size):
    if len(ref.shape) == 2:
        return ref.at[pl.ds(start, size), :]
    return ref.at[:, pl.ds(start, size), :]


def _gather_finish(name, shard, landed):
    k_rows, n = shard.shape
    kh = k_rows // 2

    def body(shard_ref, land_ref, out_ref, loc, send, recv):
        x, y, c, me, chips = _place()
        sibling = (x, y, 1 - c)

        def window(core, chip):
            return out_ref.at[pl.ds(core * kh, kh), pl.ds(pl.multiple_of(chip * n, LANES), n)]

        copies = [pltpu.make_async_copy(shard_ref, out_ref.at[:, pl.ds(pl.multiple_of(me * n, LANES), n)], loc.at[3])]
        passed = []
        for j, chip in enumerate(chips):
            mine = window(c, 2 * chip[0] + chip[1])
            copies.append(pltpu.make_async_copy(land_ref.at[j], mine, loc.at[j]))
            passed.append(pltpu.make_async_remote_copy(src_ref=land_ref.at[j], dst_ref=mine, send_sem=send.at[j],
                                                       recv_sem=recv.at[j], device_id=sibling, device_id_type=MESH))
        for cp in copies + passed:
            cp.start()
        for j, chip in enumerate(chips):
            pltpu.make_async_remote_copy(src_ref=land_ref.at[j], dst_ref=window(1 - c, 2 * chip[0] + chip[1]), send_sem=send.at[j],
                                         recv_sem=recv.at[j], device_id=sibling, device_id_type=MESH).wait_recv()
        for cp in copies:
            cp.wait()
        for cp in passed:
            cp.wait_send()

    return pl.pallas_call(
        body, in_specs=[VMEM_SPEC] * 2, out_specs=pl.BlockSpec(memory_space=pl.ANY),
        out_shape=jax.ShapeDtypeStruct((k_rows, N_CHIPS * n), shard.dtype),
        scratch_shapes=[pltpu.SemaphoreType.DMA((4,)), pltpu.SemaphoreType.DMA((3,)), pltpu.SemaphoreType.DMA((3,))],
        compiler_params=pltpu.CompilerParams(vmem_limit_bytes=VMEM_LIMIT), name=name)(shard, landed)


HBM_SPEC = pl.BlockSpec(memory_space=pltpu.HBM)
SEM_SPEC = pl.BlockSpec(memory_space=pltpu.SEMAPHORE)
DATAFLOW = pltpu.SideEffectType.DATAFLOW_SIDE_EFFECTING
TOKEN_SHAPE = (1, D_MODEL)
N_PEERS = 7
SUM_SPLIT = 4
SUM_SPLIT_ELEMS = 512 * 1024


def _peers():
    x, y, c = lax.axis_index("x"), lax.axis_index("y"), lax.axis_index("c")
    flip = lambda v, f: 1 - v if f else v
    return [(flip(x, k & 4), flip(y, k & 2), flip(c, k & 1)) for k in range(1, N_PEERS + 1)]


def _piece_shape(shape):
    return (shape[-2] // 2, shape[2] if len(shape) == 3 else shape[1] // N_CHIPS)


def _device_piece(ref, chip, core):
    kh, n4 = _piece_shape(ref.shape)
    if len(ref.shape) == 3:
        return ref.at[chip, pl.ds(core * kh, kh), :]
    return ref.at[pl.ds(core * kh, kh), pl.ds(chip * n4, n4)]


def _exchange_copies(partials, lands, send, recv):
    return [pltpu.make_async_remote_copy(
        src_ref=_device_piece(partials[t], 2 * px + py, pc), dst_ref=lands[t].at[k], send_sem=send.at[t * N_PEERS + k],
        recv_sem=recv.at[t * N_PEERS + k], device_id=(px, py, pc), device_id_type=MESH)
        for t in range(len(partials)) for k, (px, py, pc) in enumerate(_peers())]


def _broadcast_copies(srcs, lands, send, recv):
    return [pltpu.make_async_remote_copy(
        src_ref=srcs[t], dst_ref=lands[t].at[k], send_sem=send.at[t * N_PEERS + k], recv_sem=recv.at[t * N_PEERS + k],
        device_id=peer, device_id_type=MESH)
        for t in range(len(srcs)) for k, peer in enumerate(_peers())]


def _gather_copies(shards, lands, send, recv):
    x, y, c, me, chips = _place()
    return [pltpu.make_async_remote_copy(
        src_ref=shards[t], dst_ref=lands[t].at[me], send_sem=send.at[t * 3 + j], recv_sem=recv.at[t * 3 + j],
        device_id=(*chip, c), device_id_type=MESH)
        for t in range(len(shards)) for j, chip in enumerate(chips)]


def _gather_half_copies(shards, lands, send, recv):
    x, y, c, me, chips = _place()
    return [pltpu.make_async_remote_copy(
        src_ref=_rows(shards[t], c * (shards[t].shape[0] // 2), shards[t].shape[0] // 2), dst_ref=lands[t].at[j],
        send_sem=send.at[t * 3 + j], recv_sem=recv.at[t * 3 + j], device_id=(*chip, c), device_id_type=MESH)
        for t in range(len(shards)) for j, chip in enumerate(chips)]


def _split_start(name, copies, per_tensor, srcs, land_shapes):
    nt = len(srcs)
    lands = [lax.empty(s, a.dtype) for s, a in zip(land_shapes, srcs)]
    nsem = nt * per_tensor

    def body(*refs):
        send, recv = refs[2 * nt], refs[2 * nt + 1]
        for cp in copies(refs[:nt], refs[nt:2 * nt], send, recv):
            cp.start()
        refs[-1][...] = jnp.zeros(TOKEN_SHAPE, F32)

    hbm = lambda a: pltpu.with_memory_space_constraint(a, pltpu.HBM)
    outs = pl.pallas_call(
        body, name=name,
        out_shape=[pltpu.SemaphoreType.DMA((nsem,)), pltpu.SemaphoreType.DMA((nsem,))]
        + [pltpu.HBM(s.shape, s.dtype) for s in srcs] + [pltpu.HBM(l.shape, l.dtype) for l in lands]
        + [jax.ShapeDtypeStruct(TOKEN_SHAPE, F32)],
        in_specs=[HBM_SPEC] * (2 * nt), out_specs=[SEM_SPEC, SEM_SPEC] + [HBM_SPEC] * (2 * nt) + [VMEM_SPEC],
        input_output_aliases={i: 2 + i for i in range(2 * nt)},
        compiler_params=pltpu.CompilerParams(has_side_effects=DATAFLOW))(*[hbm(a) for a in list(srcs) + lands])
    return outs[0], outs[1], outs[2:2 + nt], outs[2 + nt:2 + 2 * nt], outs[-1]


def _split_wait(name, copies, send, recv, srcs, lands, after):
    nt = len(srcs)
    after = list(after) if isinstance(after, (list, tuple)) else [after]

    def body(*refs):
        for cp in copies(refs[:nt], refs[nt:2 * nt], refs[2 * nt], refs[2 * nt + 1]):
            cp.wait_send()
            cp.wait_recv()

    outs = pl.pallas_call(
        body, name=name,
        out_shape=[pltpu.HBM(s.shape, s.dtype) for s in srcs] + [pltpu.HBM(l.shape, l.dtype) for l in lands],
        in_specs=[HBM_SPEC] * (2 * nt) + [SEM_SPEC, SEM_SPEC] + [pl.BlockSpec(memory_space=pl.ANY)] * len(after),
        out_specs=[HBM_SPEC] * (2 * nt), input_output_aliases={i: i for i in range(2 * nt)},
        compiler_params=pltpu.CompilerParams(has_side_effects=DATAFLOW))(*srcs, *lands, send, recv, *after)
    return outs[:nt], outs[nt:]


def _device_sum(name, partials, lands):
    nt = len(partials)
    pieces = [_piece_shape(p.shape) for p in partials]
    units = []
    for t, (kh, n4) in enumerate(pieces):
        split = SUM_SPLIT if kh * n4 >= SUM_SPLIT_ELEMS else 1
        units += [(t, j * (kh // split), kh // split) for j in range(split)]
    nu = len(units)

    def body(*refs):
        ins, slots, outs = refs[:nt], refs[nt:2 * nt], refs[2 * nt:3 * nt]
        owns, landed, sums = refs[3 * nt:4 * nt], refs[4 * nt:5 * nt], refs[5 * nt:6 * nt]
        loc, send, recv = refs[6 * nt:]
        x, y, c, me, chips = _place()
        sibling = (x, y, 1 - c)
        loads = []
        for u, (t, r0, rows) in enumerate(units):
            loads.append((
                pltpu.make_async_copy(_rows(_device_piece(ins[t], me, c), r0, rows), _rows(owns[t], r0, rows), loc.at[0, u]),
                pltpu.make_async_copy(_rows(slots[t], r0, rows), _rows(landed[t], r0, rows), loc.at[1, u])))
            for cp in loads[-1]:
                cp.start()
        stores = []
        for u, (t, r0, rows) in enumerate(units):
            for cp in loads[u]:
                cp.wait()

            def add(q0, ck, own=owns[t], slot=landed[t], dst=sums[t], r0=r0):
                at = pl.ds(pl.multiple_of(r0 + q0, ck), ck)
                acc = own[at, :].astype(F32)
                for k in range(N_PEERS):
                    acc = acc + slot[k, at, :].astype(F32)
                dst[at, :] = acc

            _for_row_chunks(rows, add)
            mine = _rows(outs[t], c * pieces[t][0] + r0, rows)
            stores.append((
                pltpu.make_async_copy(_rows(sums[t], r0, rows), mine, loc.at[2, u]),
                pltpu.make_async_remote_copy(src_ref=_rows(sums[t], r0, rows), dst_ref=mine, send_sem=send.at[u],
                                             recv_sem=recv.at[u], device_id=sibling, device_id_type=MESH)))
            for cp in stores[-1]:
                cp.start()
        for u, (t, r0, rows) in enumerate(units):
            pltpu.make_async_remote_copy(
                src_ref=_rows(sums[t], r0, rows), dst_ref=_rows(outs[t], (1 - c) * pieces[t][0] + r0, rows),
                send_sem=send.at[u], recv_sem=recv.at[u], device_id=sibling, device_id_type=MESH).wait_recv()
            stores[u][0].wait()
            stores[u][1].wait_send()

    any_spec = pl.BlockSpec(memory_space=pl.ANY)
    return pl.pallas_call(
        body, in_specs=[any_spec] * (2 * nt), out_specs=[any_spec] * nt,
        out_shape=[jax.ShapeDtypeStruct((2 * kh, n4), F32) for kh, n4 in pieces],
        scratch_shapes=[pltpu.VMEM(p, BF16) for p in pieces] + [pltpu.VMEM((N_PEERS,) + p, BF16) for p in pieces]
        + [pltpu.VMEM(p, F32) for p in pieces]
        + [pltpu.SemaphoreType.DMA((3, nu)), pltpu.SemaphoreType.DMA((nu,)), pltpu.SemaphoreType.DMA((nu,))],
        compiler_params=pltpu.CompilerParams(vmem_limit_bytes=VMEM_LIMIT), name=name)(*partials, *lands)


VEC_SHAPE = (8, D_MODEL + LANES)
VEC_SLOTS = dict(norm1_g=(slice(0, 1), slice(0, D_MODEL)), norm2_g=(slice(1, 2), slice(0, D_MODEL)),
                 final_g=(slice(2, 3), slice(0, D_MODEL)), sgu_ln_g=(slice(3, 4), slice(0, SGU_W)),
                 sgu_ln_b=(slice(3, 4), slice(SGU_W, 2 * SGU_W)), b_spatial=(slice(0, 8), slice(D_MODEL, D_MODEL + LANES)),
                 loss=(slice(4, 5), slice(0, LANES)))
VEC_PARAMS = ("norm1_g", "norm2_g", "final_g", "sgu_ln_g", "sgu_ln_b", "b_spatial")
SMALL_PARAMS = VEC_PARAMS + ("w_spatial",)
W_SPATIAL_2D = (SGU_GROUPS * SGU_CHUNK, SGU_CHUNK)


SMALL_GRADS = VEC_PARAMS + ("loss", "w_spatial")


def _small_shape(name):
    if name == "w_spatial":
        return W_SPATIAL_2D
    rows, cols = VEC_SLOTS[name]
    return (rows.stop - rows.start, cols.stop - cols.start)


def _pack_small(dst, parts):
    dst[...] = jnp.zeros(VEC_SHAPE, F32)
    for n, ref in parts.items():
        if n in VEC_SLOTS:
            dst[VEC_SLOTS[n]] = ref[...]


def _small_start(partials):
    names = VEC_PARAMS + ("loss",)

    def body(*refs):
        _pack_small(refs[-1], dict(zip(names, refs[:-1])))

    vec = pl.pallas_call(
        body, in_specs=[VMEM_SPEC] * len(names), out_specs=VMEM_SPEC, out_shape=jax.ShapeDtypeStruct(VEC_SHAPE, F32),
        name="small_params_pack")(*[partials[n].reshape(_small_shape(n)) for n in names])
    srcs = [vec, partials["w_spatial"].reshape(W_SPATIAL_2D)]
    return _split_start("small_params_start", _broadcast_copies, N_PEERS, srcs, [(N_PEERS,) + s.shape for s in srcs])


def _small_finish(started, after, w, m, v):
    own, landed = _split_wait("small_params_wait", _broadcast_copies, *started, after)
    ng, npar = len(SMALL_GRADS), len(SMALL_PARAMS)

    def update_body(*refs):
        vec_own, ws_own, vec_slots, ws_slots = refs[:4]
        w_in, m_in, v_in = (dict(zip(SMALL_PARAMS, refs[4 + k * npar:4 + (k + 1) * npar])) for k in range(3))
        o0 = 4 + 3 * npar
        g_out = dict(zip(SMALL_GRADS, refs[o0:o0 + ng]))
        d_out, m_out, v_out = (dict(zip(SMALL_PARAMS, refs[o0 + ng + k * npar:o0 + ng + (k + 1) * npar])) for k in range(3))
        vg, vw, vm, vv = refs[o0 + ng + 3 * npar:]
        me = 4 * lax.axis_index("x") + 2 * lax.axis_index("y") + lax.axis_index("c")

        def device_sum(mine, slots, read):
            acc = None
            for i in range(N_PEERS + 1):
                k = me ^ i
                part = jnp.where(k == 0, read(mine), read(slots.at[jnp.maximum(k, 1) - 1]))
                acc = part if acc is None else acc + part
            return acc

        vg[...] = device_sum(vec_own, vec_slots, lambda ref: ref[...])
        _pack_small(vw, w_in)
        _pack_small(vm, m_in)
        _pack_small(vv, v_in)
        d_vec, m_vec, v_vec = _adamw_math(vg[...], vw[...], vm[...], vv[...])
        vw[...] = d_vec
        vm[...] = m_vec
        vv[...] = v_vec
        for n in VEC_PARAMS + ("loss",):
            g_out[n][...] = vg[VEC_SLOTS[n]]
        for n in VEC_PARAMS:
            d_out[n][...] = vw[VEC_SLOTS[n]]
            m_out[n][...] = vm[VEC_SLOTS[n]]
            v_out[n][...] = vv[VEC_SLOTS[n]]

        def spatial(r0, ck):
            rows = pl.ds(r0, ck)
            g = device_sum(ws_own, ws_slots, lambda ref: ref[rows, :])
            d_, m_, v_ = _adamw_math(g, w_in["w_spatial"][rows, :], m_in["w_spatial"][rows, :], v_in["w_spatial"][rows, :])
            g_out["w_spatial"][rows, :] = g
            d_out["w_spatial"][rows, :] = d_
            m_out["w_spatial"][rows, :] = m_
            v_out["w_spatial"][rows, :] = v_

        _for_row_chunks(W_SPATIAL_2D[0], spatial)

    ins = list(own) + list(landed)
    for src in (w, m, v):
        ins += [src[n].reshape(_small_shape(n)) for n in SMALL_PARAMS]
    out_shapes = [jax.ShapeDtypeStruct(_small_shape(n), F32) for n in SMALL_GRADS + SMALL_PARAMS * 3]
    outs = pl.pallas_call(
        update_body, in_specs=[VMEM_SPEC] * len(ins), out_specs=[VMEM_SPEC] * len(out_shapes), out_shape=out_shapes,
        scratch_shapes=[pltpu.VMEM(VEC_SHAPE, F32)] * 4, name="small_params_update")(*ins)
    grads = dict(zip(SMALL_GRADS, outs[:ng]))
    rest = [dict(zip(SMALL_PARAMS, outs[ng + k * npar:ng + (k + 1) * npar])) for k in range(3)]
    return grads, rest[0], rest[1], rest[2]


BIG = ("w_in", "w_proj_attn", "w_proj_sgu", "w_out", "w_ffn_gate", "w_ffn_up", "w_ffn_down")
COMM_GROUPS = (("w_in",), ("w_proj_attn", "w_proj_sgu", "w_out", "w_ffn_gate", "w_ffn_up", "w_ffn_down"))
WEIGHTS = ("norm1_g", "w_in", "sgu_ln_g", "sgu_ln_b", "w_spatial", "b_spatial", "w_proj_attn", "w_proj_sgu", "w_out",
           "norm2_g", "w_ffn_gate", "w_ffn_up", "w_ffn_down", "final_g")


def _cols_from_chips(g):
    return jnp.transpose(g, (1, 0, 2)).reshape(g.shape[1], N_CHIPS * g.shape[2])


def kernel(x, positions, norm1_g, w_in, sgu_ln_g, sgu_ln_b, w_spatial, b_spatial, w_proj_attn, w_proj_sgu, w_out, norm2_g, w_ffn_gate, w_ffn_up, w_ffn_down, final_g, loss_target, m_norm1_g, m_w_in, m_sgu_ln_g, m_sgu_ln_b, m_w_spatial, m_b_spatial, m_w_proj_attn, m_w_proj_sgu, m_w_out, m_norm2_g, m_w_ffn_gate, m_w_ffn_up, m_w_ffn_down, m_final_g, v_norm1_g, v_w_in, v_sgu_ln_g, v_sgu_ln_b, v_w_spatial, v_b_spatial, v_w_proj_attn, v_w_proj_sgu, v_w_out, v_norm2_g, v_w_ffn_gate, v_w_ffn_up, v_w_ffn_down, v_final_g):
    w = dict(norm1_g=norm1_g, w_in=w_in, sgu_ln_g=sgu_ln_g, sgu_ln_b=sgu_ln_b, w_spatial=w_spatial, b_spatial=b_spatial,
             w_proj_attn=w_proj_attn, w_proj_sgu=w_proj_sgu, w_out=w_out, norm2_g=norm2_g, w_ffn_gate=w_ffn_gate,
             w_ffn_up=w_ffn_up, w_ffn_down=w_ffn_down, final_g=final_g)
    m = dict(norm1_g=m_norm1_g, w_in=m_w_in, sgu_ln_g=m_sgu_ln_g, sgu_ln_b=m_sgu_ln_b, w_spatial=m_w_spatial,
             b_spatial=m_b_spatial, w_proj_attn=m_w_proj_attn, w_proj_sgu=m_w_proj_sgu, w_out=m_w_out, norm2_g=m_norm2_g,
             w_ffn_gate=m_w_ffn_gate, w_ffn_up=m_w_ffn_up, w_ffn_down=m_w_ffn_down, final_g=m_final_g)
    v = dict(norm1_g=v_norm1_g, w_in=v_w_in, sgu_ln_g=v_sgu_ln_g, sgu_ln_b=v_sgu_ln_b, w_spatial=v_w_spatial,
             b_spatial=v_b_spatial, w_proj_attn=v_w_proj_attn, w_proj_sgu=v_w_proj_sgu, w_out=v_w_out, norm2_g=v_norm2_g,
             w_ffn_gate=v_w_ffn_gate, w_ffn_up=v_w_ffn_up, w_ffn_down=v_w_ffn_down, final_g=v_final_g)
    t = x.shape[1]

    shards = {n: _ew(f"cast_{n}", lambda a: (a,), [w[n][0]], [BF16])[0] for n in BIG}
    late = COMM_GROUPS[1]
    k_in, n_in = shards["w_in"].shape
    *first, token = _split_start("gather_start_0", _gather_half_copies, 3, [shards["w_in"]], [(3, k_in // 2, n_in)])
    pending = {}

    def first_weight(after):
        srcs, filled = _split_wait("gather_wait_0", _gather_half_copies, *first, after)
        gath_in, late_shards = lax.optimization_barrier(
            (_gather_finish("gather_finish_0", srcs[0], filled[0]), [shards[n] for n in late]))
        *pending["late"], _ = _split_start(
            "gather_start_1", _gather_copies, 3, late_shards, [(N_CHIPS,) + s.shape for s in late_shards])
        return gath_in

    def late_weights(after):
        srcs, filled = _split_wait("gather_wait_1", _gather_copies, *pending["late"], after)
        me = 2 * lax.axis_index("x") + lax.axis_index("y")
        gath = {n: lax.dynamic_update_slice(f, s[None], (me, 0, 0)) for n, f, s in zip(late, filled, srcs)}
        return (_cols_from_chips(gath["w_proj_attn"]), _cols_from_chips(gath["w_proj_sgu"]),
                gath["w_out"].reshape(D_MODEL, D_MODEL), gath["w_ffn_gate"], gath["w_ffn_up"], gath["w_ffn_down"])

    exchanges = {}

    def on_grads(i, partials):
        if "w_out" in partials:
            partials["w_out"] = partials["w_out"].reshape(N_CHIPS, D_MODEL // N_CHIPS, D_MODEL)
        parts = [partials[n] for n in COMM_GROUPS[i]]
        *exchanges[i], started = _split_start(
            f"rs_exchange_start_{i}", _exchange_copies, N_PEERS, parts, [(N_PEERS,) + _piece_shape(p.shape) for p in parts])
        return started

    dx, _, small = _local_step(
        x[0], positions.reshape(t, 1), loss_target[0], norm1_g + token, sgu_ln_g, sgu_ln_b, w_spatial[0], b_spatial[0],
        norm2_g, final_g.reshape(1, D_MODEL), first_weight, late_weights, on_grads=on_grads)
    *small_started, small_token = _small_start(small)

    grads = {}
    for i in (1, 0):
        parts, filled = _split_wait(f"rs_exchange_wait_{i}", _exchange_copies, *exchanges[i], small_token)
        grads.update(zip(COMM_GROUPS[i], _device_sum(f"rs_device_sum_{i}", parts, filled)))

    delta, new_m, new_v, updated = {}, {}, {}, []
    for n in BIG:
        shp = w[n].shape
        flip = jnp.transpose if shp[-1] % LANES else (lambda a: a)
        outs = _adamw(f"adamw_{n}", flip(grads[n]), flip(w[n][0]), flip(m[n][0]), flip(v[n][0]))
        grads[n], delta[n], new_m[n], new_v[n] = (flip(a).reshape(shp) for a in outs)
        updated.append(outs[-1])

    g_s, d_s, m_s, v_s = _small_finish(small_started, updated, w, m, v)
    loss = g_s["loss"][0, 0]
    for n in SMALL_PARAMS:
        shp = w[n].shape
        grads[n], delta[n], new_m[n], new_v[n] = (a[n].reshape(shp) for a in (g_s, d_s, m_s, v_s))

    return (loss, dx.reshape(x.shape), *[grads[n] for n in WEIGHTS], *[delta[n] for n in WEIGHTS],
            *[new_m[n] for n in WEIGHTS], *[new_v[n] for n in WEIGHTS])
```

```python
import functools

import numpy as np
import jax
import jax.numpy as jnp
from jax import lax
from jax.experimental import pallas as pl
from jax.experimental.pallas import tpu as pltpu

F32, BF16 = jnp.float32, jnp.bfloat16
MESH = pl.DeviceIdType.MESH

D_MODEL = 1024
HEAD_DIM = 64
ATTN_W = 512
DILATIONS = (1, 4, 16)
BLK = 128
ATTN_BLOCKS_PER_STEP = 4
ROPE_DIM = 16
ROPE_THETA = 500000.0
SGU_W = 512
SGU_CHUNK = 128
SGU_GROUPS = 8
D_FF = 2816
N_CHIPS = 4
FF_SHARD = D_FF // N_CHIPS
IN_COLS = 7680
EPS = 1e-6
NEG = -1e30
LANES = 128
VMEM_LIMIT = 52 * 1024 * 1024

ADAM_LR, ADAM_B1, ADAM_B2, ADAM_EPS, ADAM_WD, ADAM_STEP = 0.001, 0.9, 0.999, 1e-08, 0.01, 10

QKV_BLOCKS = 9


def _w_in_block(part, g):
    return part * len(DILATIONS) + g


def _cparams(ngrid):
    return pltpu.CompilerParams(dimension_semantics=("arbitrary",) * ngrid, vmem_limit_bytes=VMEM_LIMIT)


def _full(shape):
    return pl.BlockSpec(shape, lambda *_: (0,) * len(shape))


def _resident(shape):
    return pl.BlockSpec(shape, lambda *_: (0,) * len(shape), pipeline_mode=pl.Buffered(1))


NT = ((1,), (1,))
TN = ((0,), (0,))


def _rope(v, cos_t, sin_t):
    half = ROPE_DIM // 2
    first = (lax.broadcasted_iota(jnp.int32, cos_t.shape, 1) % HEAD_DIM) < half
    outs = []
    for cs in range(v.shape[1] // LANES):
        x = v[:, cs * LANES:(cs + 1) * LANES]
        partner = jnp.where(first, pltpu.roll(x, LANES - half, axis=1), pltpu.roll(x, half, axis=1))
        outs.append(x * cos_t + partner * sin_t)
    return outs[0] if len(outs) == 1 else jnp.concatenate(outs, axis=1)


def _spread_heads(v2, upper):
    other = pltpu.roll(v2, HEAD_DIM, axis=1)
    h0 = jnp.where(upper, other, v2)
    h1 = jnp.where(upper, v2, other)
    return jnp.concatenate([jnp.concatenate([h0, h0], axis=1), jnp.concatenate([h1, h1], axis=1)], axis=0)


def _sigmoid(v):
    return 0.5 * jnp.tanh(0.5 * v) + 0.5


def _rms_stats(v):
    r = lax.rsqrt(jnp.mean(v * v, axis=-1, keepdims=True) + EPS)
    return v * r, r


def _rms_bwd(dy, xhat, r, g):
    dxh = dy * g
    return r * (dxh - xhat * jnp.mean(dxh * xhat, axis=-1, keepdims=True))


def _head_sum_matrix():
    idx = np.arange(ATTN_W) // HEAD_DIM
    return jnp.asarray((idx[:, None] == idx[None, :]).astype(np.float32), dtype=BF16)


def _group_sum(v, e):
    hi = v.astype(BF16)
    lo = (v - hi.astype(F32)).astype(BF16)
    return jnp.dot(hi, e, preferred_element_type=F32) + jnp.dot(lo, e, preferred_element_type=F32)


TILE = 512


def _to_slabs(slab_ref, v):
    for cs in range(slab_ref.shape[0]):
        slab_ref[cs] = v[:, cs * LANES:(cs + 1) * LANES]


def _from_slabs(slab_ref):
    return jnp.concatenate([slab_ref[cs] for cs in range(slab_ref.shape[0])], axis=1)


def _class_rows(slab_ref, r, dil):
    n = slab_ref.shape[1] // dil
    return jnp.concatenate([slab_ref.at[cs][pl.ds(r, n, stride=dil), :] for cs in range(slab_ref.shape[0])], axis=1)


def _put_class_rows(slab_ref, r, dil, v):
    n = slab_ref.shape[1] // dil
    for cs in range(slab_ref.shape[0]):
        slab_ref.at[cs][pl.ds(r, n, stride=dil), :] = v[:, cs * LANES:(cs + 1) * LANES]


def _natural_from_group(slab_ref, grp_ref):
    dil = grp_ref.shape[0]
    for r in range(dil):
        _put_class_rows(slab_ref, r, dil, grp_ref[r].astype(F32))
    return _from_slabs(slab_ref)


def _group_from_natural(slab_ref, grp_ref, v):
    dil = grp_ref.shape[0]
    _to_slabs(slab_ref, v)
    for r in range(dil):
        grp_ref[r] = _class_rows(slab_ref, r, dil).astype(grp_ref.dtype)


def _group_spec(dil, tile, width):
    return pl.BlockSpec((dil, tile // dil, width), lambda i, *_: (0, i, 0))


def _slabs(tile, width):
    return pltpu.VMEM((width // LANES, tile, LANES), F32)


def _rope_consts():
    lane = np.arange(LANES) % HEAD_DIM
    fi = lane % (ROPE_DIM // 2)
    invf = np.where(lane < ROPE_DIM, ROPE_THETA ** (-(2.0 * fi) / ROPE_DIM), 0.0)
    sgn = np.where(lane < ROPE_DIM // 2, -1.0, np.where(lane < ROPE_DIM, 1.0, 0.0))
    return (jnp.asarray(invf.astype(np.float32)).reshape(1, LANES), jnp.asarray(sgn.astype(np.float32)).reshape(1, LANES))


def _rope_tables(pos_col):
    t = pos_col.shape[0]
    tile = min(t, TILE)
    invf, sgn = _rope_consts()

    def body(p_ref, f_ref, s_ref, c0, s0, c1, s1, c2, s2, slab_c, slab_s):
        ang = p_ref[...].astype(F32) * f_ref[...]
        cos, sin = jnp.cos(ang), jnp.sin(ang) * s_ref[...]
        c0[...] = cos
        s0[...] = sin
        _group_from_natural(slab_c, c1, cos)
        _group_from_natural(slab_s, s1, sin)
        for r in range(DILATIONS[2]):
            c2[r] = _class_rows(slab_c, r, DILATIONS[2])
            s2[r] = _class_rows(slab_s, r, DILATIONS[2])

    nat = pl.BlockSpec((tile, LANES), lambda i: (i, 0))
    specs, shapes = [nat, nat], [(t, LANES)] * 2
    for d in DILATIONS[1:]:
        specs += [_group_spec(d, tile, LANES)] * 2
        shapes += [(d, t // d, LANES)] * 2
    outs = pl.pallas_call(
        body, grid=(t // tile,),
        in_specs=[pl.BlockSpec((tile, 1), lambda i: (i, 0)), _full((1, LANES)), _full((1, LANES))],
        out_specs=specs, out_shape=[jax.ShapeDtypeStruct(s, F32) for s in shapes],
        scratch_shapes=[_slabs(tile, LANES)] * 2,
        compiler_params=_cparams(1), name="rope_tables")(pos_col, invf, sgn)
    return [(outs[2 * g].reshape(t, LANES), outs[2 * g + 1].reshape(t, LANES)) for g in range(len(DILATIONS))]


def _norm_fwd(x, g):
    t = x.shape[0]
    tile = min(t, TILE)

    def body(x_ref, g_ref, h0_ref, h1_ref, h2_ref, slab):
        xhat, _ = _rms_stats(x_ref[...])
        hn = xhat * g_ref[...]
        h0_ref[...] = hn.astype(BF16)
        _group_from_natural(slab, h1_ref, hn)
        for r in range(DILATIONS[2]):
            h2_ref[r] = _class_rows(slab, r, DILATIONS[2]).astype(BF16)

    nat = pl.BlockSpec((tile, D_MODEL), lambda i: (i, 0))
    return pl.pallas_call(
        body, grid=(t // tile,),
        in_specs=[nat, _full((1, D_MODEL))],
        out_specs=[nat] + [_group_spec(d, tile, D_MODEL) for d in DILATIONS[1:]],
        out_shape=[jax.ShapeDtypeStruct((t, D_MODEL), BF16)]
        + [jax.ShapeDtypeStruct((d, t // d, D_MODEL), BF16) for d in DILATIONS[1:]],
        scratch_shapes=[_slabs(tile, D_MODEL)],
        compiler_params=_cparams(1), name="norm1_fwd")(x, g)


GU_COLS = 3072
GROUP_COLS = 1536
GU_HALF = GU_COLS // 2


def _w_in_spec(width, block):
    return pl.BlockSpec((D_MODEL, width), lambda i: (0, block), pipeline_mode=pl.Buffered(1))


def _gu_w_specs():
    first = QKV_BLOCKS * ATTN_W // GU_HALF
    return [_w_in_spec(GU_HALF, first), _w_in_spec(GU_HALF, first + 1)]


def _group_w_specs(g):
    return [_w_in_spec(ATTN_W, _w_in_block(part, g)) for part in range(3)]


def _in_proj(hs, w_in, tables):
    t = hs[0].shape[0]
    tm = min(t, 1024)

    def body_gu(h_ref, w0_ref, w1_ref, o_ref):
        h = h_ref[...]
        o_ref[:, 0:GU_HALF] = jnp.dot(h, w0_ref[...], preferred_element_type=F32).astype(BF16)
        o_ref[:, GU_HALF:] = jnp.dot(h, w1_ref[...], preferred_element_type=F32).astype(BF16)

    gu = _token_call("in_proj_gates_uv", body_gu, t, tm,
                     [(hs[0], _rows_spec(tm, D_MODEL))] + [(w_in, s) for s in _gu_w_specs()],
                     [((t, GU_COLS), BF16, _rows_spec(tm, GU_COLS))])[0]

    qkvs = []
    for g in range(len(DILATIONS)):

        def body_qkv(h_ref, wq_ref, wk_ref, wv_ref, cos_ref, sin_ref, o_ref):
            h = h_ref[...]
            cos_w, sin_w = cos_ref[...], sin_ref[...]
            q = jnp.dot(h, wq_ref[...], preferred_element_type=F32)
            o_ref[:, 0:ATTN_W] = (_rope(q, cos_w, sin_w) * HEAD_DIM ** -0.5).astype(BF16)
            k = jnp.dot(h, wk_ref[...], preferred_element_type=F32)
            o_ref[:, ATTN_W:2 * ATTN_W] = _rope(k, cos_w, sin_w).astype(BF16)
            o_ref[:, 2 * ATTN_W:] = jnp.dot(h, wv_ref[...], preferred_element_type=F32).astype(BF16)

        cos_t, sin_t = tables[g]
        qkvs.append(_token_call(
            f"in_proj_qkv_g{g}", body_qkv, t, tm,
            [(hs[g].reshape(t, D_MODEL), _rows_spec(tm, D_MODEL))] + [(w_in, s) for s in _group_w_specs(g)]
            + [(cos_t, _rows_spec(tm, LANES)), (sin_t, _rows_spec(tm, LANES))],
            [((t, GROUP_COLS), BF16, _rows_spec(tm, GROUP_COLS))])[0])
    return gu, qkvs


def _attn_masks(n):
    row = lax.broadcasted_iota(jnp.int32, (2 * BLK, 2 * BLK), 0) % BLK
    col = lax.broadcasted_iota(jnp.int32, (2 * BLK, 2 * BLK), 1)
    diff = BLK + row - col
    valid = (diff >= 0) & (diff <= BLK) & ((col >= BLK) | (n > 0))
    upper = lax.broadcasted_iota(jnp.int32, (BLK, LANES), 1) >= HEAD_DIM
    return valid, upper


def _stack_heads(v2, upper):
    zero = jnp.zeros_like(v2)
    return jnp.concatenate([jnp.where(upper, zero, v2), jnp.where(upper, v2, zero)], axis=0)


def _unstack_heads(v, upper):
    return jnp.where(upper, v[BLK:], v[:BLK])


def _attn_fwd(qkv, g, dil):
    t = qkv.shape[0]
    length = t // dil
    nb = length // BLK
    per_step = min(nb, ATTN_BLOCKS_PER_STEP)
    view = qkv.reshape(dil, length, GROUP_COLS)

    def body(q_ref, kc_ref, kp_ref, vc_ref, vp_ref, o_ref, l_ref, kwin, vwin):
        n = pl.program_id(1)
        kwin[0:BLK] = kp_ref[...]
        kwin[BLK:] = kc_ref[...]
        vwin[0:BLK] = vp_ref[...]
        vwin[BLK:] = vc_ref[...]

        def block(b, carry):
            valid, upper = _attn_masks(n * per_step + b)
            rows = pl.ds(pl.multiple_of(b * BLK, BLK), BLK)
            window = pl.ds(pl.multiple_of(b * BLK, BLK), 2 * BLK)
            slabs = [slice(p * LANES, (p + 1) * LANES) for p in range(ATTN_W // LANES)]
            ss = [lax.dot_general(_stack_heads(q_ref[rows, sl], upper), kwin[window, sl], (NT, ((), ())),
                                  preferred_element_type=F32) for sl in slabs]
            soft = []
            for s in ss:
                s = jnp.where(valid, s, NEG)
                m = jnp.max(s, axis=1, keepdims=True)
                pe = jnp.exp(s - m)
                soft.append((m, pe, jnp.sum(pe, axis=1, keepdims=True)))
            for sl, (m, pe, den) in zip(slabs, soft):
                o = jnp.dot(pe.astype(BF16), vwin[window, sl], preferred_element_type=F32) / den
                lse = jnp.broadcast_to(m + jnp.log(den), (2 * BLK, LANES))
                o_ref[rows, sl] = _unstack_heads(o, upper).astype(BF16)
                l_ref[rows, sl] = _unstack_heads(lse, upper)
            return carry

        lax.fori_loop(0, per_step, block, 0)

    rows = per_step * BLK
    cur = lambda part: pl.BlockSpec((None, rows, ATTN_W), lambda r, n: (r, n, part))
    prev = lambda part: pl.BlockSpec((None, BLK, ATTN_W), lambda r, n: (r, jnp.maximum(n * per_step - 1, 0), part))
    out_spec = pl.BlockSpec((None, rows, ATTN_W), lambda r, n: (r, n, 0))
    return pl.pallas_call(
        body, grid=(dil, nb // per_step),
        in_specs=[cur(0), cur(1), prev(1), cur(2), prev(2)],
        out_specs=[out_spec, out_spec],
        out_shape=[jax.ShapeDtypeStruct((dil, length, ATTN_W), BF16), jax.ShapeDtypeStruct((dil, length, ATTN_W), F32)],
        scratch_shapes=[pltpu.VMEM((rows + BLK, ATTN_W), BF16)] * 2,
        compiler_params=_cparams(2), name=f"attn_fwd_g{g}")(view, view, view, view, view)


def _alphas(l0, l1, l2):
    m = jnp.maximum(jnp.maximum(l0, l1), l2)
    e0, e1, e2 = jnp.exp(l0 - m), jnp.exp(l1 - m), jnp.exp(l2 - m)
    inv = 1.0 / (e0 + e1 + e2)
    return e0 * inv, e1 * inv, e2 * inv


def _natural_group_values(o_refs, l_refs, slabs):
    os_ = [o_refs[0][0].astype(F32)] + [_natural_from_group(slabs[2 * g - 2], o_refs[g]) for g in (1, 2)]
    ls_ = [l_refs[0][0]] + [_natural_from_group(slabs[2 * g - 1], l_refs[g]) for g in (1, 2)]
    return os_, ls_


def _combine_fwd(os_, ls_):
    t = os_[0].shape[1]
    tile = min(t, TILE)

    def body(o0, o1, o2, l0, l1, l2, a_ref, *slabs):
        ov, lv = _natural_group_values((o0, o1, o2), (l0, l1, l2), slabs)
        a0, a1, a2 = _alphas(*lv)
        a_ref[...] = (a0 * ov[0] + a1 * ov[1] + a2 * ov[2]).astype(BF16)

    specs = [_group_spec(d, tile, ATTN_W) for d in DILATIONS]
    return pl.pallas_call(
        body, grid=(t // tile,), in_specs=specs * 2, out_specs=pl.BlockSpec((tile, ATTN_W), lambda i: (i, 0)),
        out_shape=jax.ShapeDtypeStruct((t, ATTN_W), BF16),
        scratch_shapes=[_slabs(tile, ATTN_W)] * 4,
        compiler_params=_cparams(1), name="combine_fwd")(*os_, *ls_)


def _combine_bwd(dattn, os_, ls_):
    t = dattn.shape[0]
    tile = min(t, TILE)
    e = _head_sum_matrix()

    def body(d_ref, o0, o1, o2, l0, l1, l2, e_ref, do0, do1, do2, c0, c1, c2, *slabs):
        ov, lv = _natural_group_values((o0, o1, o2), (l0, l1, l2), slabs)
        alphas = _alphas(*lv)
        d = d_ref[...]
        attn = alphas[0] * ov[0] + alphas[1] * ov[1] + alphas[2] * ov[2]
        s = _group_sum(d * attn, e_ref[...])
        do0[0] = (alphas[0] * d).astype(BF16)
        c0[0] = -alphas[0] * s
        for g, do_ref, c_ref in ((1, do1, c1), (2, do2, c2)):
            _group_from_natural(slabs[2 * g - 2], do_ref, alphas[g] * d)
            _group_from_natural(slabs[2 * g - 1], c_ref, -alphas[g] * s)

    specs = [_group_spec(d, tile, ATTN_W) for d in DILATIONS]
    shapes = [(d, t // d, ATTN_W) for d in DILATIONS]
    outs = pl.pallas_call(
        body, grid=(t // tile,),
        in_specs=[pl.BlockSpec((tile, ATTN_W), lambda i: (i, 0))] + specs * 2 + [_full((ATTN_W, ATTN_W))],
        out_specs=specs * 2,
        out_shape=[jax.ShapeDtypeStruct(s, BF16) for s in shapes] + [jax.ShapeDtypeStruct(s, F32) for s in shapes],
        scratch_shapes=[_slabs(tile, ATTN_W)] * 4,
        compiler_params=_cparams(1), name="combine_bwd")(dattn, *os_, *ls_, e)
    return outs[:3], outs[3:]


def _attn_bwd(qkv, do, cc, lse, cos_t, sin_t, g, dil):
    t = qkv.shape[0]
    length = t // dil
    nb = length // BLK
    per_step = min(nb, ATTN_BLOCKS_PER_STEP)
    nsteps = nb // per_step
    rows_per_step = per_step * BLK
    qkv_v = qkv.reshape(dil, length, GROUP_COLS)
    cos_v, sin_v = (a.reshape(dil, length, LANES) for a in (cos_t, sin_t))
    scale = HEAD_DIM ** -0.5
    dq_cols, dk_cols, dv_cols = (slice(i * ATTN_W, (i + 1) * ATTN_W) for i in range(3))

    def body(q_ref, kc_ref, kp_ref, vc_ref, vp_ref, do_ref, c_ref, l_ref, cosc, sinc, cosp, sinp,
             out_ref, acc, kwin, vwin, cwin, swin):
        n = pl.program_id(1)

        def one_block(b):
            valid, upper = _attn_masks(n * per_step + b)
            start = b * BLK if isinstance(b, int) else pl.multiple_of(b * BLK, BLK)
            rows, before, window = pl.ds(start, BLK), pl.ds(start, BLK), pl.ds(start, 2 * BLK)
            own = pl.ds(start + BLK, BLK)
            dq_parts, dkp_parts, dkc_parts, dvp_parts, dvc_parts = [], [], [], [], []
            npairs = ATTN_W // LANES
            slabs = [slice(p * LANES, (p + 1) * LANES) for p in range(npairs)]
            qss = [_stack_heads(q_ref[rows, sl], upper) for sl in slabs]
            doss = [_stack_heads(do_ref[rows, sl], upper) for sl in slabs]
            ss = [lax.dot_general(qss[p], kwin[window, slabs[p]], (NT, ((), ())), preferred_element_type=F32) for p in range(npairs)]
            dpvs = [lax.dot_general(doss[p], vwin[window, slabs[p]], (NT, ((), ())), preferred_element_type=F32)
                    for p in range(npairs)]
            pes = [jnp.exp(jnp.where(valid, ss[p], NEG) - _spread_heads(l_ref[rows, slabs[p]], upper)) for p in range(npairs)]
            dss = [(pes[p] * (dpvs[p] + _spread_heads(c_ref[rows, slabs[p]], upper))).astype(BF16) for p in range(npairs)]
            for p in range(npairs):
                qs, dos, ds = qss[p], doss[p], dss[p]
                dq2 = _unstack_heads(jnp.dot(ds, kwin[window, slabs[p]], preferred_element_type=F32), upper)
                dk2 = lax.dot_general(ds, qs, (TN, ((), ())), preferred_element_type=F32)
                dv2 = lax.dot_general(pes[p].astype(BF16), dos, (TN, ((), ())), preferred_element_type=F32)
                dq_parts.append(dq2)
                dkp_parts.append(dk2[:BLK])
                dkc_parts.append(dk2[BLK:])
                dvp_parts.append(dv2[:BLK])
                dvc_parts.append(dv2[BLK:])
            dq = _rope(jnp.concatenate(dq_parts, axis=1) * scale, cwin[own, :], swin[own, :])
            dkc = _rope(jnp.concatenate(dkc_parts, axis=1), cwin[own, :], swin[own, :])
            dkp = _rope(jnp.concatenate(dkp_parts, axis=1), cwin[before, :], swin[before, :])
            return dq, dkp, dkc, jnp.concatenate(dvp_parts, axis=1), jnp.concatenate(dvc_parts, axis=1)

        @pl.when(n < nsteps)
        def _():
            kwin[0:BLK] = kp_ref[...]
            kwin[BLK:] = kc_ref[...]
            vwin[0:BLK] = vp_ref[...]
            vwin[BLK:] = vc_ref[...]
            cwin[0:BLK] = cosp[...]
            cwin[BLK:] = cosc[...]
            swin[0:BLK] = -sinp[...]
            swin[BLK:] = -sinc[...]
            dq, dkp, dkc, dvp, dvc = one_block(0)
            last = slice(rows_per_step - BLK, rows_per_step)

            @pl.when(n > 0)
            def _():
                if per_step > 1:
                    out_ref[0:rows_per_step - BLK, :] = acc[0:rows_per_step - BLK, :].astype(BF16)
                out_ref[last, dq_cols] = acc[last, dq_cols].astype(BF16)
                out_ref[last, dk_cols] = (acc[last, dk_cols] + dkp).astype(BF16)
                out_ref[last, dv_cols] = (acc[last, dv_cols] + dvp).astype(BF16)

            acc[0:BLK, dq_cols] = dq
            acc[0:BLK, dk_cols] = dkc
            acc[0:BLK, dv_cols] = dvc

            def later(b, carry):
                dq, dkp, dkc, dvp, dvc = one_block(b)
                start = pl.multiple_of(b * BLK, BLK)
                before, rows = pl.ds(start - BLK, BLK), pl.ds(start, BLK)
                acc[before, dk_cols] += dkp
                acc[before, dv_cols] += dvp
                acc[rows, dq_cols] = dq
                acc[rows, dk_cols] = dkc
                acc[rows, dv_cols] = dvc
                return carry

            lax.fori_loop(1, per_step, later, 0)

        @pl.when(n == flush_at)
        def _():
            out_ref[...] = acc[...].astype(BF16)

    flush_at = nsteps - 1 if nsteps == 1 else nsteps
    out_lag = 0 if nsteps == 1 else 1
    nc = lambda n: jnp.minimum(n, nsteps - 1)
    npv = lambda n: jnp.maximum(jnp.minimum(n, nsteps - 1) * per_step - 1, 0)
    cur = lambda part: pl.BlockSpec((None, rows_per_step, ATTN_W), lambda r, n: (r, nc(n), part))
    prev = lambda part: pl.BlockSpec((None, BLK, ATTN_W), lambda r, n: (r, npv(n), part))
    row = pl.BlockSpec((None, rows_per_step, ATTN_W), lambda r, n: (r, nc(n), 0))
    tab_c = pl.BlockSpec((None, rows_per_step, LANES), lambda r, n: (r, nc(n), 0))
    tab_p = pl.BlockSpec((None, BLK, LANES), lambda r, n: (r, npv(n), 0))
    out_spec = pl.BlockSpec((None, rows_per_step, GROUP_COLS), lambda r, n: (r, jnp.maximum(n - out_lag, 0), 0))
    out = pl.pallas_call(
        body, grid=(dil, nsteps + out_lag),
        in_specs=[cur(0), cur(1), prev(1), cur(2), prev(2), row, row, row, tab_c, tab_c, tab_p, tab_p],
        out_specs=out_spec,
        out_shape=jax.ShapeDtypeStruct((dil, length, GROUP_COLS), BF16),
        scratch_shapes=[pltpu.VMEM((rows_per_step, GROUP_COLS), F32)]
        + [pltpu.VMEM((rows_per_step + BLK, ATTN_W), BF16)] * 2 + [pltpu.VMEM((rows_per_step + BLK, LANES), F32)] * 2,
        compiler_params=_cparams(2), name=f"attn_bwd_g{g}")(
            qkv_v, qkv_v, qkv_v, qkv_v, qkv_v, do, cc, lse, cos_v, sin_v, cos_v, sin_v)
    return out.reshape(t, GROUP_COLS)


SQRT_HALF = 0.7071067811865476
INV_SQRT_2PI = 0.3989422804014327


def _sgu_core(uv, g, b, w_ref, bias):
    cdf = 0.5 * (1.0 + lax.erf(uv * SQRT_HALF))
    z = uv * cdf
    u, v = z[:, :SGU_W], z[:, SGU_W:]
    mu = jnp.mean(v, axis=1, keepdims=True)
    xc = v - mu
    rs = lax.rsqrt(jnp.mean(xc * xc, axis=1, keepdims=True) + EPS)
    xhat = xc * rs
    vn = xhat * g + b
    row = lax.broadcasted_iota(jnp.int32, (SGU_CHUNK, SGU_CHUNK), 0)
    col = lax.broadcasted_iota(jnp.int32, (SGU_CHUNK, SGU_CHUNK), 1)
    tril = row >= col
    upper = lax.broadcasted_iota(jnp.int32, (SGU_CHUNK, LANES), 1) >= SGU_W // SGU_GROUPS
    ws, vlo, vhi, mixed = [], [], [], []
    for pr in range(SGU_W // LANES):
        sl = slice(pr * LANES, (pr + 1) * LANES)
        w0 = jnp.where(tril, w_ref[2 * pr], 0.0).astype(BF16)
        w1 = jnp.where(tril, w_ref[2 * pr + 1], 0.0).astype(BF16)
        vn2 = vn[:, sl]
        lo = jnp.where(upper, 0.0, vn2).astype(BF16)
        hi = jnp.where(upper, vn2, 0.0).astype(BF16)
        mixed.append(jnp.dot(w0, lo, preferred_element_type=F32) + jnp.dot(w1, hi, preferred_element_type=F32)
                     + bias[:, sl])
        ws.append((w0, w1))
        vlo.append(lo)
        vhi.append(hi)
    return cdf, u, xhat, rs, jnp.concatenate(mixed, axis=1), ws, vlo, vhi, tril, upper


SGU_STEP = 4 * SGU_CHUNK


def _for_chunks(step_rows, fn):
    def one(ci, carry):
        fn(pl.ds(pl.multiple_of(ci * SGU_CHUNK, SGU_CHUNK), SGU_CHUNK))
        return carry

    lax.fori_loop(0, step_rows // SGU_CHUNK, one, 0)


def _sgu_fwd(gu, ln_g, ln_b, w_s, bias_exp):
    t = gu.shape[0]
    step = min(t, SGU_STEP)

    def body(uv_ref, g_ref, b_ref, w_ref, bias_ref, o_ref):
        def chunk(rows):
            _, u, _, _, mixed, *_ = _sgu_core(uv_ref[rows, :].astype(F32), g_ref[...], b_ref[...], w_ref, bias_ref[...])
            o_ref[rows, :] = (u * mixed).astype(BF16)

        _for_chunks(step, chunk)

    return pl.pallas_call(
        body, grid=(t // step,),
        in_specs=[pl.BlockSpec((step, 2 * SGU_W), lambda n: (n, 0)), _full((1, SGU_W)), _full((1, SGU_W)),
                  _full((SGU_GROUPS, SGU_CHUNK, SGU_CHUNK)), _full((SGU_CHUNK, SGU_W))],
        out_specs=pl.BlockSpec((step, SGU_W), lambda n: (n, 0)),
        out_shape=jax.ShapeDtypeStruct((t, SGU_W), BF16),
        compiler_params=_cparams(1), name="sgu_fwd")(gu, ln_g, ln_b, w_s, bias_exp)


def _sgu_bwd(dproj, gu, dsgu, ln_g, ln_b, w_s, bias_exp):
    t = gu.shape[0]
    step = min(t, SGU_STEP)
    nsteps = t // step
    e = _head_sum_matrix()

    def body(dp_in, uv_ref, ds_ref, g_ref, b_ref, w_ref, bias_ref, e_ref, out_ref, dw_ref, dbias_ref, dg_ref, db_ref):
        n = pl.program_id(0)

        @pl.when(n == 0)
        def _():
            dw_ref[...] = jnp.zeros(dw_ref.shape, F32)
            dbias_ref[...] = jnp.zeros(dbias_ref.shape, F32)
            dg_ref[...] = jnp.zeros(dg_ref.shape, F32)
            db_ref[...] = jnp.zeros(db_ref.shape, F32)

        _for_chunks(step, functools.partial(chunk, uv_ref, ds_ref, g_ref, b_ref, w_ref, bias_ref, out_ref, dw_ref, dbias_ref,
                                            dg_ref, db_ref))

        @pl.when(n == nsteps - 1)
        def _():
            dbias_ref[...] = _group_sum(dbias_ref[...], e_ref[...])

    def chunk(uv_ref, ds_ref, g_ref, b_ref, w_ref, bias_ref, out_ref, dw_ref, dbias_ref, dg_ref, db_ref, rows):
        uv = uv_ref[rows, :].astype(F32)
        g = g_ref[...]
        cdf, u, xhat, rs, mixed, ws, vlo, vhi, tril, upper = _sgu_core(uv, g, b_ref[...], w_ref, bias_ref[...])
        dsg = ds_ref[rows, :]
        du = dsg * mixed
        dmixed = dsg * u
        dbias_ref[...] += dmixed
        dvn = []
        for pr in range(SGU_W // LANES):
            sl = slice(pr * LANES, (pr + 1) * LANES)
            dm2 = dmixed[:, sl]
            dlo = jnp.where(upper, 0.0, dm2).astype(BF16)
            dhi = jnp.where(upper, dm2, 0.0).astype(BF16)
            w0, w1 = ws[pr]
            dvn.append(lax.dot_general(w0, dlo, (TN, ((), ())), preferred_element_type=F32)
                       + lax.dot_general(w1, dhi, (TN, ((), ())), preferred_element_type=F32))
            dw0 = lax.dot_general(dlo, vlo[pr], (NT, ((), ())), preferred_element_type=F32)
            dw1 = lax.dot_general(dhi, vhi[pr], (NT, ((), ())), preferred_element_type=F32)
            dw_ref[2 * pr] += jnp.where(tril, dw0, 0.0)
            dw_ref[2 * pr + 1] += jnp.where(tril, dw1, 0.0)
        dvn = jnp.concatenate(dvn, axis=1)
        dg_ref[...] += jnp.sum(dvn * xhat, axis=0, keepdims=True)
        db_ref[...] += jnp.sum(dvn, axis=0, keepdims=True)
        dxh = dvn * g
        dv = rs * (dxh - jnp.mean(dxh, axis=1, keepdims=True) - xhat * jnp.mean(dxh * xhat, axis=1, keepdims=True))
        dz = jnp.concatenate([du, dv], axis=1)
        dgelu = cdf + uv * (INV_SQRT_2PI * jnp.exp(-0.5 * uv * uv))
        out_ref[rows, :] = (dz * dgelu).astype(BF16)

    outs = pl.pallas_call(
        body, grid=(nsteps,),
        in_specs=[pl.BlockSpec(memory_space=pl.ANY), pl.BlockSpec((step, 2 * SGU_W), lambda n: (n, 0)),
                  pl.BlockSpec((step, SGU_W), lambda n: (n, 0)), _full((1, SGU_W)), _full((1, SGU_W)),
                  _full((SGU_GROUPS, SGU_CHUNK, SGU_CHUNK)), _full((SGU_CHUNK, SGU_W)), _full((ATTN_W, ATTN_W))],
        out_specs=[pl.BlockSpec((step, 2 * SGU_W), lambda n: (n, 0)), _full((SGU_GROUPS, SGU_CHUNK, SGU_CHUNK)),
                   _full((SGU_CHUNK, SGU_W)), _full((1, SGU_W)), _full((1, SGU_W))],
        out_shape=[jax.ShapeDtypeStruct(dproj.shape, BF16), jax.ShapeDtypeStruct((SGU_GROUPS, SGU_CHUNK, SGU_CHUNK), F32),
                   jax.ShapeDtypeStruct((SGU_CHUNK, SGU_W), F32), jax.ShapeDtypeStruct((1, SGU_W), F32),
                   jax.ShapeDtypeStruct((1, SGU_W), F32)],
        input_output_aliases={0: 0},
        compiler_params=_cparams(1), name="sgu_bwd")(dproj, gu, dsgu, ln_g, ln_b, w_s, bias_exp, e)
    return outs


def _merge_fwd(attn, sgu, gu, x, w_pa, w_ps, w_out, g2):
    t = x.shape[0]
    tm = min(t, 512)

    def body(a_ref, s_ref, ga_ref, gb_ref, x_ref, wpa, wps, wo, g_ref, pa_ref, ps_ref, m_ref, x1_ref, h2_ref):
        pa = jnp.dot(a_ref[...], wpa[...], preferred_element_type=F32)
        ps = jnp.dot(s_ref[...], wps[...], preferred_element_type=F32)
        merged = (_sigmoid(ga_ref[...].astype(F32)) * pa + _sigmoid(gb_ref[...].astype(F32)) * ps).astype(BF16)
        x1 = x_ref[...] + jnp.dot(merged, wo[...], preferred_element_type=F32)
        xhat, _ = _rms_stats(x1)
        pa_ref[...] = pa.astype(BF16)
        ps_ref[...] = ps.astype(BF16)
        m_ref[...] = merged
        x1_ref[...] = x1
        h2_ref[...] = (xhat * g_ref[...]).astype(BF16)

    half = pl.BlockSpec((tm, ATTN_W), lambda i: (i, 0))
    full = pl.BlockSpec((tm, D_MODEL), lambda i: (i, 0))
    return pl.pallas_call(
        body, grid=(t // tm,),
        in_specs=[half, half, pl.BlockSpec((tm, D_MODEL), lambda i: (i, 1)), pl.BlockSpec((tm, D_MODEL), lambda i: (i, 2)),
                  full, _resident((ATTN_W, D_MODEL)), _resident((SGU_W, D_MODEL)), _resident((D_MODEL, D_MODEL)),
                  _full((1, D_MODEL))],
        out_specs=[full] * 5,
        out_shape=[jax.ShapeDtypeStruct((t, D_MODEL), BF16), jax.ShapeDtypeStruct((t, D_MODEL), BF16),
                   jax.ShapeDtypeStruct((t, D_MODEL), BF16), jax.ShapeDtypeStruct((t, D_MODEL), F32),
                   jax.ShapeDtypeStruct((t, D_MODEL), BF16)],
        compiler_params=_cparams(1), name="merge_fwd")(attn, sgu, gu, gu, x, w_pa, w_ps, w_out, g2)


def _merge_bwd(dx1b, gu, pa, ps, w_pa, w_ps, w_out):
    t = dx1b.shape[0]
    tm = min(t, 512)

    def body(d_ref, ga_ref, gb_ref, pa_ref, ps_ref, wpa, wps, wo, out_ref, dpa_ref, dps_ref, da_ref, dsg_ref):
        dm = lax.dot_general(d_ref[...], wo[...], (NT, ((), ())), preferred_element_type=F32)
        sa, sb = _sigmoid(ga_ref[...].astype(F32)), _sigmoid(gb_ref[...].astype(F32))
        dpa = (dm * sa).astype(BF16)
        dps = (dm * sb).astype(BF16)
        out_ref[:, 0:D_MODEL] = jnp.zeros((tm, D_MODEL), BF16)
        out_ref[:, D_MODEL:2 * D_MODEL] = (dm * pa_ref[...].astype(F32) * sa * (1.0 - sa)).astype(BF16)
        out_ref[:, 2 * D_MODEL:GU_COLS] = (dm * ps_ref[...].astype(F32) * sb * (1.0 - sb)).astype(BF16)
        dpa_ref[...] = dpa
        dps_ref[...] = dps
        da_ref[...] = lax.dot_general(dpa, wpa[...], (NT, ((), ())), preferred_element_type=F32)
        dsg_ref[...] = lax.dot_general(dps, wps[...], (NT, ((), ())), preferred_element_type=F32)

    half = pl.BlockSpec((tm, ATTN_W), lambda i: (i, 0))
    full = pl.BlockSpec((tm, D_MODEL), lambda i: (i, 0))
    return pl.pallas_call(
        body, grid=(t // tm,),
        in_specs=[full, pl.BlockSpec((tm, D_MODEL), lambda i: (i, 1)),
                  pl.BlockSpec((tm, D_MODEL), lambda i: (i, 2)), full, full,
                  _resident((ATTN_W, D_MODEL)), _resident((SGU_W, D_MODEL)), _resident((D_MODEL, D_MODEL))],
        out_specs=[pl.BlockSpec((tm, GU_COLS), lambda i: (i, 0)), full, full, half, half],
        out_shape=[jax.ShapeDtypeStruct((t, GU_COLS), BF16), jax.ShapeDtypeStruct((t, D_MODEL), BF16),
                   jax.ShapeDtypeStruct((t, D_MODEL), BF16), jax.ShapeDtypeStruct((t, ATTN_W), F32),
                   jax.ShapeDtypeStruct((t, SGU_W), F32)],
        compiler_params=_cparams(1), name="merge_bwd")(dx1b, gu, gu, pa, ps, w_pa, w_ps, w_out)


def _token_call(name, body, t, tm, ins, outs, reds=(), scratch=()):
    return pl.pallas_call(
        body, grid=(t // tm,), in_specs=[s for _, s in ins],
        out_specs=[o[2] for o in outs] + [_full(r) for r in reds],
        out_shape=[jax.ShapeDtypeStruct(o[0], o[1]) for o in outs] + [jax.ShapeDtypeStruct(r, F32) for r in reds],
        scratch_shapes=list(scratch), compiler_params=_cparams(1), name=name)(*[a for a, _ in ins])


def _rows_spec(tm, width):
    return pl.BlockSpec((tm, width), lambda i: (i, 0))


def _chips_spec(tm):
    return pl.BlockSpec((N_CHIPS, tm, FF_SHARD), lambda i: (0, i, 0))


def _zero_at_start(*refs):
    @pl.when(pl.program_id(0) == 0)
    def _():
        for r in refs:
            r[...] = jnp.zeros(r.shape, r.dtype)


def _ffn_fwd(h2, w_g, w_u):
    t = h2.shape[0]
    tm = min(t, 512)

    def body(h_ref, wg_ref, wu_ref, fa_ref, fb_ref, ff_ref):
        h = h_ref[...]
        for s in range(N_CHIPS):
            a = jnp.dot(h, wg_ref[s], preferred_element_type=F32)
            b = jnp.dot(h, wu_ref[s], preferred_element_type=F32)
            sg = _sigmoid(a)
            silu = a * sg
            fa_ref[s] = (b * (sg * (1.0 + a * (1.0 - sg)))).astype(BF16)
            fb_ref[s] = silu.astype(BF16)
            ff_ref[s] = (silu * b).astype(BF16)

    shp = (N_CHIPS, t, FF_SHARD)
    w_spec = _resident((N_CHIPS, D_MODEL, FF_SHARD))
    return _token_call("ffn_fwd", body, t, tm, [(h2, _rows_spec(tm, D_MODEL)), (w_g, w_spec), (w_u, w_spec)],
                       [(shp, BF16, _chips_spec(tm))] * 3)


def _ffn_down_loss(ff, w_d, x1, tgt, gf):
    t = x1.shape[0]
    tm = min(t, 512)

    def body(ff_ref, wd_ref, x1_ref, tgt_ref, g_ref, dx2_ref, dx2b_ref, loss_ref, dgf_ref):
        _zero_at_start(loss_ref, dgf_ref)
        acc = jnp.dot(ff_ref[0], wd_ref[0], preferred_element_type=F32)
        for s in range(1, N_CHIPS):
            acc = acc + jnp.dot(ff_ref[s], wd_ref[s], preferred_element_type=F32)
        x2 = x1_ref[...] + acc
        g = g_ref[...]
        xhat, rr = _rms_stats(x2)
        diff = xhat * g - tgt_ref[...]
        rows = jnp.sum(diff * diff, axis=1, keepdims=True)
        loss_ref[...] += jnp.broadcast_to(jnp.sum(rows, axis=0, keepdims=True) * (0.5 / D_MODEL), (1, LANES))
        dy = diff * (1.0 / D_MODEL)
        dgf_ref[...] += jnp.sum(dy * xhat, axis=0, keepdims=True)
        dx2 = _rms_bwd(dy, xhat, rr, g)
        dx2_ref[...] = dx2
        dx2b_ref[...] = dx2.astype(BF16)

    row = _rows_spec(tm, D_MODEL)
    return _token_call("ffn_down_loss", body, t, tm,
                       [(ff, _chips_spec(tm)), (w_d, _resident((N_CHIPS, FF_SHARD, D_MODEL))), (x1, row), (tgt, row),
                        (gf, _full((1, D_MODEL)))],
                       [((t, D_MODEL), F32, row), ((t, D_MODEL), BF16, row)], reds=[(1, LANES), (1, D_MODEL)])


def _ffn_bwd_act(dx2b, w_d, fa, fb):
    t = dx2b.shape[0]
    tm = min(t, 512)

    def body(d_ref, wd_ref, fa_ref, fb_ref, da_ref, db_ref):
        d = d_ref[...]
        for s in range(N_CHIPS):
            dff = lax.dot_general(d, wd_ref[s], (NT, ((), ())), preferred_element_type=F32)
            da_ref[s] = (dff * fa_ref[s].astype(F32)).astype(BF16)
            db_ref[s] = (dff * fb_ref[s].astype(F32)).astype(BF16)

    shp = (N_CHIPS, t, FF_SHARD)
    return _token_call("ffn_bwd_act", body, t, tm,
                       [(dx2b, _rows_spec(tm, D_MODEL)), (w_d, _resident((N_CHIPS, FF_SHARD, D_MODEL))),
                        (fa, _chips_spec(tm)), (fb, _chips_spec(tm))],
                       [(shp, BF16, _chips_spec(tm))] * 2)


def _ffn_bwd_in(da, db, w_g, w_u, x1, dx2, g2):
    t = x1.shape[0]
    tm = min(t, 512)

    def body(da_ref, db_ref, wg_ref, wu_ref, x1_ref, dx2_ref, g_ref, dx1_ref, dx1b_ref, dg_ref):
        _zero_at_start(dg_ref)
        acc = None
        for s in range(N_CHIPS):
            part = (lax.dot_general(da_ref[s], wg_ref[s], (NT, ((), ())), preferred_element_type=F32)
                    + lax.dot_general(db_ref[s], wu_ref[s], (NT, ((), ())), preferred_element_type=F32))
            acc = part if acc is None else acc + part
        xhat, rr = _rms_stats(x1_ref[...])
        dg_ref[...] += jnp.sum(acc * xhat, axis=0, keepdims=True)
        dx1 = dx2_ref[...] + _rms_bwd(acc, xhat, rr, g_ref[...])
        dx1_ref[...] = dx1
        dx1b_ref[...] = dx1.astype(BF16)

    row = _rows_spec(tm, D_MODEL)
    w_spec = _resident((N_CHIPS, D_MODEL, FF_SHARD))
    return _token_call("ffn_bwd_in", body, t, tm,
                       [(da, _chips_spec(tm)), (db, _chips_spec(tm)), (w_g, w_spec), (w_u, w_spec), (x1, row), (dx2, row),
                        (g2, _full((1, D_MODEL)))],
                       [((t, D_MODEL), F32, row), ((t, D_MODEL), BF16, row)], reds=[(1, D_MODEL)])


def _group_dh(d, w_refs):
    dh = None
    for part, w_ref in enumerate(w_refs):
        term = lax.dot_general(d[:, part * ATTN_W:(part + 1) * ATTN_W], w_ref[...], (NT, ((), ())),
                               preferred_element_type=F32)
        dh = term if dh is None else dh + term
    return dh


def _in_proj_bwd(dgu, dqkvs, w_in, x, dx1, g1):
    t = x.shape[0]
    tile = min(t, TILE)
    ngroups = len(DILATIONS)

    def body(*refs):
        dgu_ref, dq_refs = refs[0], refs[1:1 + ngroups]
        w0_ref, w1_ref = refs[1 + ngroups:3 + ngroups]
        wg_refs = [refs[3 + ngroups + 3 * g:6 + ngroups + 3 * g] for g in range(ngroups)]
        x_ref, dx1_ref, g_ref, dx_ref, dg_ref = refs[3 + 4 * ngroups:5 + 4 * ngroups + 3]
        slabs = refs[5 + 4 * ngroups + 3:]
        _zero_at_start(dg_ref)
        for g in range(1, ngroups):
            dil = DILATIONS[g]
            part = _group_dh(dq_refs[g][...].reshape(tile, GROUP_COLS), wg_refs[g])
            for r in range(dil):
                _put_class_rows(slabs[g - 1], r, dil, part[r * (tile // dil):(r + 1) * (tile // dil)])
        dh = lax.dot_general(dgu_ref[:, 0:GU_HALF], w0_ref[...], (NT, ((), ())), preferred_element_type=F32)
        dh = dh + lax.dot_general(dgu_ref[:, GU_HALF:], w1_ref[...], (NT, ((), ())), preferred_element_type=F32)
        dh = dh + _group_dh(dq_refs[0][0], wg_refs[0])
        for slab in slabs:
            dh = dh + _from_slabs(slab)
        xhat, rr = _rms_stats(x_ref[...])
        dg_ref[...] += jnp.sum(dh * xhat, axis=0, keepdims=True)
        dx_ref[...] = dx1_ref[...] + _rms_bwd(dh, xhat, rr, g_ref[...])

    row = _rows_spec(tile, D_MODEL)
    group_ins = [(dqkvs[g].reshape(d, t // d, GROUP_COLS), _group_spec(d, tile, GROUP_COLS)) for g, d in enumerate(DILATIONS)]
    w_specs = _gu_w_specs() + [s for g in range(ngroups) for s in _group_w_specs(g)]
    return _token_call(
        "in_proj_bwd", body, t, tile,
        [(dgu, _rows_spec(tile, GU_COLS))] + group_ins + [(w_in, s) for s in w_specs]
        + [(x, row), (dx1, row), (g1, _full((1, D_MODEL)))],
        [((t, D_MODEL), F32, row)], reds=[(1, D_MODEL)], scratch=[_slabs(tile, D_MODEL)] * (ngroups - 1))


WGRAD_TK = 2048


def _wgrad_mm(name, grid, a, a_spec, b, b_spec, acc_shape, out_shape, out_spec, dst=None):
    nk = grid[-1]

    def body(*refs):
        a_ref, b_ref, o_ref, acc_ref = refs[0], refs[1], refs[-2], refs[-1]
        k = pl.program_id(len(grid) - 1)
        part = lax.dot_general(a_ref[...], b_ref[...], (TN, ((), ())), preferred_element_type=F32)

        @pl.when(k == 0)
        def _():
            acc_ref[...] = part

        @pl.when(k > 0)
        def _():
            acc_ref[...] += part

        @pl.when(k == nk - 1)
        def _():
            o_ref[...] = acc_ref[...].astype(BF16)

    filled = [] if dst is None else [dst]
    return pl.pallas_call(
        body, grid=grid, in_specs=[a_spec, b_spec] + [pl.BlockSpec(memory_space=pl.ANY)] * len(filled),
        out_specs=out_spec, out_shape=jax.ShapeDtypeStruct(out_shape, BF16), scratch_shapes=[pltpu.VMEM(acc_shape, F32)],
        input_output_aliases={2: 0} if filled else {}, compiler_params=_cparams(len(grid)), name=name)(a, b, *filled)


def _wgrad_2d(name, a, b, tm, tn):
    t, k1 = a.shape
    n = b.shape[1]
    tk = min(t, WGRAD_TK)
    return _wgrad_mm(name, (k1 // tm, n // tn, t // tk), a, pl.BlockSpec((tk, tm), lambda i, j, k: (k, i)),
                     b, pl.BlockSpec((tk, tn), lambda i, j, k: (k, j)), (tm, tn), (k1, n),
                     pl.BlockSpec((tm, tn), lambda i, j, k: (i, j)))


def _wgrad_in(hs, dgu, dqkvs):
    t = dgu.shape[0]
    tk = min(t, WGRAD_TK)
    gu_block = QKV_BLOCKS * ATTN_W // GU_HALF
    parts = [(hs[0], dgu, GU_HALF, lambda j: j + gu_block)]
    parts += [(hs[g].reshape(t, D_MODEL), dqkvs[g], ATTN_W, lambda j, g=g: _w_in_block(j, g)) for g in range(3)]
    dst = None
    for n, (a, b, tn, block_of) in enumerate(parts):
        dst = _wgrad_mm(f"wgrad_in_{n}", (1, b.shape[1] // tn, t // tk),
                        a, pl.BlockSpec((tk, D_MODEL), lambda i, j, k: (k, 0)), b, pl.BlockSpec((tk, tn), lambda i, j, k: (k, j)),
                        (D_MODEL, tn), (D_MODEL, IN_COLS),
                        pl.BlockSpec((D_MODEL, tn), lambda i, j, k, block_of=block_of: (0, block_of(j))), dst=dst)
    return dst


def _wgrad_ff_in(name, h2, da):
    t = h2.shape[0]
    tk = min(t, WGRAD_TK)
    return _wgrad_mm(name, (N_CHIPS, 1, t // tk), h2, pl.BlockSpec((tk, D_MODEL), lambda i, j, k: (k, 0)),
                     da, pl.BlockSpec((None, tk, FF_SHARD), lambda i, j, k: (i, k, 0)), (D_MODEL, FF_SHARD),
                     (N_CHIPS, D_MODEL, FF_SHARD), pl.BlockSpec((None, D_MODEL, FF_SHARD), lambda i, j, k: (i, 0, 0)))


def _wgrad_ff_down(ff, dx2b):
    t = dx2b.shape[0]
    tk = min(t, WGRAD_TK)
    return _wgrad_mm("wgrad_ffn_down", (N_CHIPS, 1, t // tk), ff, pl.BlockSpec((None, tk, FF_SHARD), lambda i, j, k: (i, k, 0)),
                     dx2b, pl.BlockSpec((tk, D_MODEL), lambda i, j, k: (k, 0)), (FF_SHARD, D_MODEL),
                     (N_CHIPS, FF_SHARD, D_MODEL), pl.BlockSpec((None, FF_SHARD, D_MODEL), lambda i, j, k: (i, 0, 0)))


def _local_step(x, pos_col, tgt, g1, ln_g, ln_b, w_s, b_s, g2, gf, first_weight, late_weights, on_grads=None):
    tables = _rope_tables(pos_col)
    bias_exp = jnp.repeat(jnp.transpose(b_s), SGU_W // SGU_GROUPS, axis=1)

    hs = _norm_fwd(x, g1)
    w_p = first_weight([hs[0], bias_exp] + [table for pair in tables for table in pair])
    gu, qkvs = _in_proj(hs, w_p, tables)
    os_, ls_ = [], []
    for g, dil in enumerate(DILATIONS):
        o, lse = _attn_fwd(qkvs[g], g, dil)
        os_.append(o)
        ls_.append(lse)
    attn = _combine_fwd(os_, ls_)
    sgu = _sgu_fwd(gu, ln_g, ln_b, w_s, bias_exp)
    w_pa, w_ps, w_out, w_g, w_u, w_d = late_weights(attn)
    pa, ps, merged, x1, h2 = _merge_fwd(attn, sgu, gu, x, w_pa, w_ps, w_out, g2)
    fa, fb, ff = _ffn_fwd(h2, w_g, w_u)
    dx2, dx2b, loss, dgf = _ffn_down_loss(ff, w_d, x1, tgt, gf)

    da, db = _ffn_bwd_act(dx2b, w_d, fa, fb)
    dw_d = _wgrad_ff_down(ff, dx2b)
    dx1, dx1b, dg2 = _ffn_bwd_in(da, db, w_g, w_u, x1, dx2, g2)
    dw_g = _wgrad_ff_in("wgrad_ffn_gate", h2, da)
    dw_u = _wgrad_ff_in("wgrad_ffn_up", h2, db)

    dgu, dpa, dps, dattn, dsgu = _merge_bwd(dx1b, gu, pa, ps, w_pa, w_ps, w_out)
    dw_out = _wgrad_2d("wgrad_out", merged, dx1b, D_MODEL, D_MODEL)
    dw_pa = _wgrad_2d("wgrad_proj_attn", attn, dpa, ATTN_W, D_MODEL)
    dw_ps = _wgrad_2d("wgrad_proj_sgu", sgu, dps, SGU_W, D_MODEL)
    if on_grads is not None:
        ln_g = ln_g + on_grads(1, dict(w_proj_attn=dw_pa, w_proj_sgu=dw_ps, w_out=dw_out, w_ffn_gate=dw_g, w_ffn_up=dw_u,
                                       w_ffn_down=dw_d))[:, :SGU_W]
    dgu, dw_s, dbias, dln_g, dln_b = _sgu_bwd(dgu, gu, dsgu, ln_g, ln_b, w_s, bias_exp)
    dos, ccs = _combine_bwd(dattn, os_, ls_)
    dqkvs = [_attn_bwd(qkvs[g], dos[g], ccs[g], ls_[g], *tables[g], g, dil) for g, dil in enumerate(DILATIONS)]
    dw_p = _wgrad_in(hs, dgu, dqkvs)
    if on_grads is not None:
        g1 = g1 + on_grads(0, dict(w_in=dw_p))
    dx, dg1 = _in_proj_bwd(dgu, dqkvs, w_p, x, dx1, g1)

    db_s = jnp.transpose(dbias[:, ::SGU_W // SGU_GROUPS])
    small = dict(loss=loss, norm1_g=dg1, sgu_ln_g=dln_g, sgu_ln_b=dln_b, w_spatial=dw_s, b_spatial=db_s,
                 norm2_g=dg2, final_g=dgf)
    big = dict(w_in=dw_p, w_proj_attn=dw_pa, w_proj_sgu=dw_ps, w_out=dw_out, w_ffn_gate=dw_g, w_ffn_up=dw_u,
               w_ffn_down=dw_d)
    return dx, big, small


def _ew(name, fn, ins, out_dtypes, after=()):
    shp = ins[0].shape
    rows, cols = shp
    tr = next((cand for cand in (256, 352, 128) if rows % cand == 0 and rows > cand), rows)

    def body(*refs):
        res = fn(*[r[...] for r in refs[:len(ins)]])
        for o_ref, v in zip(refs[len(ins) + len(after):], res):
            o_ref[...] = v.astype(o_ref.dtype)

    spec = pl.BlockSpec((tr, cols), lambda i: (i, 0))
    return pl.pallas_call(
        body, grid=(rows // tr,), in_specs=[spec] * len(ins) + [pl.BlockSpec(memory_space=pl.ANY)] * len(after),
        out_specs=[spec] * len(out_dtypes), out_shape=[jax.ShapeDtypeStruct(shp, d) for d in out_dtypes],
        compiler_params=_cparams(1), name=name)(*ins, *after)


def _adamw_math(g, w, m, v):
    m = ADAM_B1 * m + (1.0 - ADAM_B1) * g
    v = ADAM_B2 * v + (1.0 - ADAM_B2) * (g * g)
    m_hat = m / (1.0 - ADAM_B1 ** ADAM_STEP)
    v_hat = v / (1.0 - ADAM_B2 ** ADAM_STEP)
    delta = -ADAM_LR * (m_hat / (jnp.sqrt(v_hat) + ADAM_EPS) + ADAM_WD * w)
    return delta, m, v


def _adamw(name, g, w, m, v):
    return _ew(name, lambda g_, w_, m_, v_: (g_,) + _adamw_math(g_, w_, m_, v_), [g, w, m, v], [F32] * 4)


VMEM_SPEC = pl.BlockSpec(memory_space=pltpu.VMEM)


def _for_row_chunks(rows, fn):
    ck = next(c for c in (64, 32, 16) if rows % c == 0)

    def step(i, carry):
        fn(pl.multiple_of(i * ck, ck), ck)
        return carry

    lax.fori_loop(0, rows // ck, step, 0)


def _place():
    x, y, c = lax.axis_index("x"), lax.axis_index("y"), lax.axis_index("c")
    chips = [(1 - x, y), (x, 1 - y), (1 - x, 1 - y)]
    return x, y, c, 2 * x + y, chips


def _rows(ref, start, size):
    if len(ref.shape) == 2:
        return ref.at[pl.ds(start, size), :]
    return ref.at[:, pl.ds(start, size), :]


def _gather_finish(name, shard, landed):
    k_rows, n = shard.shape
    kh = k_rows // 2

    def body(shard_ref, land_ref, out_ref, loc, send, recv):
        x, y, c, me, chips = _place()
        sibling = (x, y, 1 - c)

        def window(core, chip):
            return out_ref.at[pl.ds(core * kh, kh), pl.ds(pl.multiple_of(chip * n, LANES), n)]

        copies = [pltpu.make_async_copy(shard_ref, out_ref.at[:, pl.ds(pl.multiple_of(me * n, LANES), n)], loc.at[3])]
        passed = []
        for j, chip in enumerate(chips):
            mine = window(c, 2 * chip[0] + chip[1])
            copies.append(pltpu.make_async_copy(land_ref.at[j], mine, loc.at[j]))
            passed.append(pltpu.make_async_remote_copy(src_ref=land_ref.at[j], dst_ref=mine, send_sem=send.at[j],
                                                       recv_sem=recv.at[j], device_id=sibling, device_id_type=MESH))
        for cp in copies + passed:
            cp.start()
        for j, chip in enumerate(chips):
            pltpu.make_async_remote_copy(src_ref=land_ref.at[j], dst_ref=window(1 - c, 2 * chip[0] + chip[1]), send_sem=send.at[j],
                                         recv_sem=recv.at[j], device_id=sibling, device_id_type=MESH).wait_recv()
        for cp in copies:
            cp.wait()
        for cp in passed:
            cp.wait_send()

    return pl.pallas_call(
        body, in_specs=[VMEM_SPEC] * 2, out_specs=pl.BlockSpec(memory_space=pl.ANY),
        out_shape=jax.ShapeDtypeStruct((k_rows, N_CHIPS * n), shard.dtype),
        scratch_shapes=[pltpu.SemaphoreType.DMA((4,)), pltpu.SemaphoreType.DMA((3,)), pltpu.SemaphoreType.DMA((3,))],
        compiler_params=pltpu.CompilerParams(vmem_limit_bytes=VMEM_LIMIT), name=name)(shard, landed)


HBM_SPEC = pl.BlockSpec(memory_space=pltpu.HBM)
SEM_SPEC = pl.BlockSpec(memory_space=pltpu.SEMAPHORE)
DATAFLOW = pltpu.SideEffectType.DATAFLOW_SIDE_EFFECTING
TOKEN_SHAPE = (1, D_MODEL)
N_PEERS = 7
SUM_SPLIT = 4
SUM_SPLIT_ELEMS = 512 * 1024


def _peers():
    x, y, c = lax.axis_index("x"), lax.axis_index("y"), lax.axis_index("c")
    flip = lambda v, f: 1 - v if f else v
    return [(flip(x, k & 4), flip(y, k & 2), flip(c, k & 1)) for k in range(1, N_PEERS + 1)]


def _piece_shape(shape):
    return (shape[-2] // 2, shape[2] if len(shape) == 3 else shape[1] // N_CHIPS)


def _device_piece(ref, chip, core):
    kh, n4 = _piece_shape(ref.shape)
    if len(ref.shape) == 3:
        return ref.at[chip, pl.ds(core * kh, kh), :]
    return ref.at[pl.ds(core * kh, kh), pl.ds(chip * n4, n4)]


def _exchange_copies(partials, lands, send, recv):
    return [pltpu.make_async_remote_copy(
        src_ref=_device_piece(partials[t], 2 * px + py, pc), dst_ref=lands[t].at[k], send_sem=send.at[t * N_PEERS + k],
        recv_sem=recv.at[t * N_PEERS + k], device_id=(px, py, pc), device_id_type=MESH)
        for t in range(len(partials)) for k, (px, py, pc) in enumerate(_peers())]


def _broadcast_copies(srcs, lands, send, recv):
    return [pltpu.make_async_remote_copy(
        src_ref=srcs[t], dst_ref=lands[t].at[k], send_sem=send.at[t * N_PEERS + k], recv_sem=recv.at[t * N_PEERS + k],
        device_id=peer, device_id_type=MESH)
        for t in range(len(srcs)) for k, peer in enumerate(_peers())]


def _gather_copies(shards, lands, send, recv):
    x, y, c, me, chips = _place()
    return [pltpu.make_async_remote_copy(
        src_ref=shards[t], dst_ref=lands[t].at[me], send_sem=send.at[t * 3 + j], recv_sem=recv.at[t * 3 + j],
        device_id=(*chip, c), device_id_type=MESH)
        for t in range(len(shards)) for j, chip in enumerate(chips)]


def _gather_half_copies(shards, lands, send, recv):
    x, y, c, me, chips = _place()
    return [pltpu.make_async_remote_copy(
        src_ref=_rows(shards[t], c * (shards[t].shape[0] // 2), shards[t].shape[0] // 2), dst_ref=lands[t].at[j],
        send_sem=send.at[t * 3 + j], recv_sem=recv.at[t * 3 + j], device_id=(*chip, c), device_id_type=MESH)
        for t in range(len(shards)) for j, chip in enumerate(chips)]


def _split_start(name, copies, per_tensor, srcs, land_shapes):
    nt = len(srcs)
    lands = [lax.empty(s, a.dtype) for s, a in zip(land_shapes, srcs)]
    nsem = nt * per_tensor

    def body(*refs):
        send, recv = refs[2 * nt], refs[2 * nt + 1]
        for cp in copies(refs[:nt], refs[nt:2 * nt], send, recv):
            cp.start()
        refs[-1][...] = jnp.zeros(TOKEN_SHAPE, F32)

    hbm = lambda a: pltpu.with_memory_space_constraint(a, pltpu.HBM)
    outs = pl.pallas_call(
        body, name=name,
        out_shape=[pltpu.SemaphoreType.DMA((nsem,)), pltpu.SemaphoreType.DMA((nsem,))]
        + [pltpu.HBM(s.shape, s.dtype) for s in srcs] + [pltpu.HBM(l.shape, l.dtype) for l in lands]
        + [jax.ShapeDtypeStruct(TOKEN_SHAPE, F32)],
        in_specs=[HBM_SPEC] * (2 * nt), out_specs=[SEM_SPEC, SEM_SPEC] + [HBM_SPEC] * (2 * nt) + [VMEM_SPEC],
        input_output_aliases={i: 2 + i for i in range(2 * nt)},
        compiler_params=pltpu.CompilerParams(has_side_effects=DATAFLOW))(*[hbm(a) for a in list(srcs) + lands])
    return outs[0], outs[1], outs[2:2 + nt], outs[2 + nt:2 + 2 * nt], outs[-1]


def _split_wait(name, copies, send, recv, srcs, lands, after):
    nt = len(srcs)
    after = list(after) if isinstance(after, (list, tuple)) else [after]

    def body(*refs):
        for cp in copies(refs[:nt], refs[nt:2 * nt], refs[2 * nt], refs[2 * nt + 1]):
            cp.wait_send()
            cp.wait_recv()

    outs = pl.pallas_call(
        body, name=name,
        out_shape=[pltpu.HBM(s.shape, s.dtype) for s in srcs] + [pltpu.HBM(l.shape, l.dtype) for l in lands],
        in_specs=[HBM_SPEC] * (2 * nt) + [SEM_SPEC, SEM_SPEC] + [pl.BlockSpec(memory_space=pl.ANY)] * len(after),
        out_specs=[HBM_SPEC] * (2 * nt), input_output_aliases={i: i for i in range(2 * nt)},
        compiler_params=pltpu.CompilerParams(has_side_effects=DATAFLOW))(*srcs, *lands, send, recv, *after)
    return outs[:nt], outs[nt:]


def _device_sum(name, partials, lands):
    nt = len(partials)
    pieces = [_piece_shape(p.shape) for p in partials]
    units = []
    for t, (kh, n4) in enumerate(pieces):
        split = SUM_SPLIT if kh * n4 >= SUM_SPLIT_ELEMS else 1
        units += [(t, j * (kh // split), kh // split) for j in range(split)]
    nu = len(units)

    def body(*refs):
        ins, slots, outs = refs[:nt], refs[nt:2 * nt], refs[2 * nt:3 * nt]
        owns, landed, sums = refs[3 * nt:4 * nt], refs[4 * nt:5 * nt], refs[5 * nt:6 * nt]
        loc, send, recv = refs[6 * nt:]
        x, y, c, me, chips = _place()
        sibling = (x, y, 1 - c)
        loads = []
        for u, (t, r0, rows) in enumerate(units):
            loads.append((
                pltpu.make_async_copy(_rows(_device_piece(ins[t], me, c), r0, rows), _rows(owns[t], r0, rows), loc.at[0, u]),
                pltpu.make_async_copy(_rows(slots[t], r0, rows), _rows(landed[t], r0, rows), loc.at[1, u])))
            for cp in loads[-1]:
                cp.start()
        stores = []
        for u, (t, r0, rows) in enumerate(units):
            for cp in loads[u]:
                cp.wait()

            def add(q0, ck, own=owns[t], slot=landed[t], dst=sums[t], r0=r0):
                at = pl.ds(pl.multiple_of(r0 + q0, ck), ck)
                acc = own[at, :].astype(F32)
                for k in range(N_PEERS):
                    acc = acc + slot[k, at, :].astype(F32)
                dst[at, :] = acc

            _for_row_chunks(rows, add)
            mine = _rows(outs[t], c * pieces[t][0] + r0, rows)
            stores.append((
                pltpu.make_async_copy(_rows(sums[t], r0, rows), mine, loc.at[2, u]),
                pltpu.make_async_remote_copy(src_ref=_rows(sums[t], r0, rows), dst_ref=mine, send_sem=send.at[u],
                                             recv_sem=recv.at[u], device_id=sibling, device_id_type=MESH)))
            for cp in stores[-1]:
                cp.start()
        for u, (t, r0, rows) in enumerate(units):
            pltpu.make_async_remote_copy(
                src_ref=_rows(sums[t], r0, rows), dst_ref=_rows(outs[t], (1 - c) * pieces[t][0] + r0, rows),
                send_sem=send.at[u], recv_sem=recv.at[u], device_id=sibling, device_id_type=MESH).wait_recv()
            stores[u][0].wait()
            stores[u][1].wait_send()

    any_spec = pl.BlockSpec(memory_space=pl.ANY)
    return pl.pallas_call(
        body, in_specs=[any_spec] * (2 * nt), out_specs=[any_spec] * nt,
        out_shape=[jax.ShapeDtypeStruct((2 * kh, n4), F32) for kh, n4 in pieces],
        scratch_shapes=[pltpu.VMEM(p, BF16) for p in pieces] + [pltpu.VMEM((N_PEERS,) + p, BF16) for p in pieces]
        + [pltpu.VMEM(p, F32) for p in pieces]
        + [pltpu.SemaphoreType.DMA((3, nu)), pltpu.SemaphoreType.DMA((nu,)), pltpu.SemaphoreType.DMA((nu,))],
        compiler_params=pltpu.CompilerParams(vmem_limit_bytes=VMEM_LIMIT), name=name)(*partials, *lands)


VEC_SHAPE = (8, D_MODEL + LANES)
VEC_SLOTS = dict(norm1_g=(slice(0, 1), slice(0, D_MODEL)), norm2_g=(slice(1, 2), slice(0, D_MODEL)),
                 final_g=(slice(2, 3), slice(0, D_MODEL)), sgu_ln_g=(slice(3, 4), slice(0, SGU_W)),
                 sgu_ln_b=(slice(3, 4), slice(SGU_W, 2 * SGU_W)), b_spatial=(slice(0, 8), slice(D_MODEL, D_MODEL + LANES)),
                 loss=(slice(4, 5), slice(0, LANES)))
VEC_PARAMS = ("norm1_g", "norm2_g", "final_g", "sgu_ln_g", "sgu_ln_b", "b_spatial")
SMALL_PARAMS = VEC_PARAMS + ("w_spatial",)
W_SPATIAL_2D = (SGU_GROUPS * SGU_CHUNK, SGU_CHUNK)


SMALL_GRADS = VEC_PARAMS + ("loss", "w_spatial")


def _small_shape(name):
    if name == "w_spatial":
        return W_SPATIAL_2D
    rows, cols = VEC_SLOTS[name]
    return (rows.stop - rows.start, cols.stop - cols.start)


def _pack_small(dst, parts):
    dst[...] = jnp.zeros(VEC_SHAPE, F32)
    for n, ref in parts.items():
        if n in VEC_SLOTS:
            dst[VEC_SLOTS[n]] = ref[...]


def _small_start(partials):
    names = VEC_PARAMS + ("loss",)

    def body(*refs):
        _pack_small(refs[-1], dict(zip(names, refs[:-1])))

    vec = pl.pallas_call(
        body, in_specs=[VMEM_SPEC] * len(names), out_specs=VMEM_SPEC, out_shape=jax.ShapeDtypeStruct(VEC_SHAPE, F32),
        name="small_params_pack")(*[partials[n].reshape(_small_shape(n)) for n in names])
    srcs = [vec, partials["w_spatial"].reshape(W_SPATIAL_2D)]
    return _split_start("small_params_start", _broadcast_copies, N_PEERS, srcs, [(N_PEERS,) + s.shape for s in srcs])


def _small_finish(started, after, w, m, v):
    own, landed = _split_wait("small_params_wait", _broadcast_copies, *started, after)
    ng, npar = len(SMALL_GRADS), len(SMALL_PARAMS)

    def update_body(*refs):
        vec_own, ws_own, vec_slots, ws_slots = refs[:4]
        w_in, m_in, v_in = (dict(zip(SMALL_PARAMS, refs[4 + k * npar:4 + (k + 1) * npar])) for k in range(3))
        o0 = 4 + 3 * npar
        g_out = dict(zip(SMALL_GRADS, refs[o0:o0 + ng]))
        d_out, m_out, v_out = (dict(zip(SMALL_PARAMS, refs[o0 + ng + k * npar:o0 + ng + (k + 1) * npar])) for k in range(3))
        vg, vw, vm, vv = refs[o0 + ng + 3 * npar:]
        me = 4 * lax.axis_index("x") + 2 * lax.axis_index("y") + lax.axis_index("c")

        def device_sum(mine, slots, read):
            acc = None
            for i in range(N_PEERS + 1):
                k = me ^ i
                part = jnp.where(k == 0, read(mine), read(slots.at[jnp.maximum(k, 1) - 1]))
                acc = part if acc is None else acc + part
            return acc

        vg[...] = device_sum(vec_own, vec_slots, lambda ref: ref[...])
        _pack_small(vw, w_in)
        _pack_small(vm, m_in)
        _pack_small(vv, v_in)
        d_vec, m_vec, v_vec = _adamw_math(vg[...], vw[...], vm[...], vv[...])
        vw[...] = d_vec
        vm[...] = m_vec
        vv[...] = v_vec
        for n in VEC_PARAMS + ("loss",):
            g_out[n][...] = vg[VEC_SLOTS[n]]
        for n in VEC_PARAMS:
            d_out[n][...] = vw[VEC_SLOTS[n]]
            m_out[n][...] = vm[VEC_SLOTS[n]]
            v_out[n][...] = vv[VEC_SLOTS[n]]

        def spatial(r0, ck):
            rows = pl.ds(r0, ck)
            g = device_sum(ws_own, ws_slots, lambda ref: ref[rows, :])
            d_, m_, v_ = _adamw_math(g, w_in["w_spatial"][rows, :], m_in["w_spatial"][rows, :], v_in["w_spatial"][rows, :])
            g_out["w_spatial"][rows, :] = g
            d_out["w_spatial"][rows, :] = d_
            m_out["w_spatial"][rows, :] = m_
            v_out["w_spatial"][rows, :] = v_

        _for_row_chunks(W_SPATIAL_2D[0], spatial)

    ins = list(own) + list(landed)
    for src in (w, m, v):
        ins += [src[n].reshape(_small_shape(n)) for n in SMALL_PARAMS]
    out_shapes = [jax.ShapeDtypeStruct(_small_shape(n), F32) for n in SMALL_GRADS + SMALL_PARAMS * 3]
    outs = pl.pallas_call(
        update_body, in_specs=[VMEM_SPEC] * len(ins), out_specs=[VMEM_SPEC] * len(out_shapes), out_shape=out_shapes,
        scratch_shapes=[pltpu.VMEM(VEC_SHAPE, F32)] * 4, name="small_params_update")(*ins)
    grads = dict(zip(SMALL_GRADS, outs[:ng]))
    rest = [dict(zip(SMALL_PARAMS, outs[ng + k * npar:ng + (k + 1) * npar])) for k in range(3)]
    return grads, rest[0], rest[1], rest[2]


BIG = ("w_in", "w_proj_attn", "w_proj_sgu", "w_out", "w_ffn_gate", "w_ffn_up", "w_ffn_down")
COMM_GROUPS = (("w_in",), ("w_proj_attn", "w_proj_sgu", "w_out", "w_ffn_gate", "w_ffn_up", "w_ffn_down"))
WEIGHTS = ("norm1_g", "w_in", "sgu_ln_g", "sgu_ln_b", "w_spatial", "b_spatial", "w_proj_attn", "w_proj_sgu", "w_out",
           "norm2_g", "w_ffn_gate", "w_ffn_up", "w_ffn_down", "final_g")


def _cols_from_chips(g):
    return jnp.transpose(g, (1, 0, 2)).reshape(g.shape[1], N_CHIPS * g.shape[2])


def kernel(x, positions, norm1_g, w_in, sgu_ln_g, sgu_ln_b, w_spatial, b_spatial, w_proj_attn, w_proj_sgu, w_out, norm2_g, w_ffn_gate, w_ffn_up, w_ffn_down, final_g, loss_target, m_norm1_g, m_w_in, m_sgu_ln_g, m_sgu_ln_b, m_w_spatial, m_b_spatial, m_w_proj_attn, m_w_proj_sgu, m_w_out, m_norm2_g, m_w_ffn_gate, m_w_ffn_up, m_w_ffn_down, m_final_g, v_norm1_g, v_w_in, v_sgu_ln_g, v_sgu_ln_b, v_w_spatial, v_b_spatial, v_w_proj_attn, v_w_proj_sgu, v_w_out, v_norm2_g, v_w_ffn_gate, v_w_ffn_up, v_w_ffn_down, v_final_g):
    w = dict(norm1_g=norm1_g, w_in=w_in, sgu_ln_g=sgu_ln_g, sgu_ln_b=sgu_ln_b, w_spatial=w_spatial, b_spatial=b_spatial,
             w_proj_attn=w_proj_attn, w_proj_sgu=w_proj_sgu, w_out=w_out, norm2_g=norm2_g, w_ffn_gate=w_ffn_gate,
             w_ffn_up=w_ffn_up, w_ffn_down=w_ffn_down, final_g=final_g)
    m = dict(norm1_g=m_norm1_g, w_in=m_w_in, sgu_ln_g=m_sgu_ln_g, sgu_ln_b=m_sgu_ln_b, w_spatial=m_w_spatial,
             b_spatial=m_b_spatial, w_proj_attn=m_w_proj_attn, w_proj_sgu=m_w_proj_sgu, w_out=m_w_out, norm2_g=m_norm2_g,
             w_ffn_gate=m_w_ffn_gate, w_ffn_up=m_w_ffn_up, w_ffn_down=m_w_ffn_down, final_g=m_final_g)
    v = dict(norm1_g=v_norm1_g, w_in=v_w_in, sgu_ln_g=v_sgu_ln_g, sgu_ln_b=v_sgu_ln_b, w_spatial=v_w_spatial,
             b_spatial=v_b_spatial, w_proj_attn=v_w_proj_attn, w_proj_sgu=v_w_proj_sgu, w_out=v_w_out, norm2_g=v_norm2_g,
             w_ffn_gate=v_w_ffn_gate, w_ffn_up=v_w_ffn_up, w_ffn_down=v_w_ffn_down, final_g=v_final_g)
    t = x.shape[1]

    cast = lambda n, after: _ew(f"cast_{n}", lambda a: (a,), [w[n][0]], [BF16], after)[0]
    shards = {"w_in": cast("w_in", [])}
    late = COMM_GROUPS[1]
    k_in, n_in = shards["w_in"].shape
    *first, token = _split_start("gather_start_0", _gather_half_copies, 3, [shards["w_in"]], [(3, k_in // 2, n_in)])
    shards.update({n: cast(n, [token]) for n in late})
    pending = {}

    def first_weight(after):
        srcs, filled = _split_wait("gather_wait_0", _gather_half_copies, *first, list(after) + [shards[n] for n in late])
        gath_in, late_shards = lax.optimization_barrier(
            (_gather_finish("gather_finish_0", srcs[0], filled[0]), [shards[n] for n in late]))
        *pending["late"], _ = _split_start(
            "gather_start_1", _gather_copies, 3, late_shards, [(N_CHIPS,) + s.shape for s in late_shards])
        return gath_in

    def late_weights(after):
        srcs, filled = _split_wait("gather_wait_1", _gather_copies, *pending["late"], after)
        me = 2 * lax.axis_index("x") + lax.axis_index("y")
        gath = {n: lax.dynamic_update_slice(f, s[None], (me, 0, 0)) for n, f, s in zip(late, filled, srcs)}
        return (_cols_from_chips(gath["w_proj_attn"]), _cols_from_chips(gath["w_proj_sgu"]),
                gath["w_out"].reshape(D_MODEL, D_MODEL), gath["w_ffn_gate"], gath["w_ffn_up"], gath["w_ffn_down"])

    exchanges = {}

    def on_grads(i, partials):
        if "w_out" in partials:
            partials["w_out"] = partials["w_out"].reshape(N_CHIPS, D_MODEL // N_CHIPS, D_MODEL)
        parts = [partials[n] for n in COMM_GROUPS[i]]
        *exchanges[i], started = _split_start(
            f"rs_exchange_start_{i}", _exchange_copies, N_PEERS, parts, [(N_PEERS,) + _piece_shape(p.shape) for p in parts])
        return started

    dx, _, small = _local_step(
        x[0], positions.reshape(t, 1), loss_target[0], norm1_g + token, sgu_ln_g, sgu_ln_b, w_spatial[0], b_spatial[0],
        norm2_g, final_g.reshape(1, D_MODEL), first_weight, late_weights, on_grads=on_grads)
    *small_started, small_token = _small_start(small)

    grads = {}
    for i in (1, 0):
        parts, filled = _split_wait(f"rs_exchange_wait_{i}", _exchange_copies, *exchanges[i], small_token)
        grads.update(zip(COMM_GROUPS[i], _device_sum(f"rs_device_sum_{i}", parts, filled)))

    delta, new_m, new_v, updated = {}, {}, {}, []
    for n in BIG:
        shp = w[n].shape
        flip = jnp.transpose if shp[-1] % LANES else (lambda a: a)
        outs = _adamw(f"adamw_{n}", flip(grads[n]), flip(w[n][0]), flip(m[n][0]), flip(v[n][0]))
        grads[n], delta[n], new_m[n], new_v[n] = (flip(a).reshape(shp) for a in outs)
        updated.append(outs[-1])

    g_s, d_s, m_s, v_s = _small_finish(small_started, updated, w, m, v)
    loss = g_s["loss"][0, 0]
    for n in SMALL_PARAMS:
        shp = w[n].shape
        grads[n], delta[n], new_m[n], new_v[n] = (a[n].reshape(shp) for a in (g_s, d_s, m_s, v_s))

    return (loss, dx.reshape(x.shape), *[grads[n] for n in WEIGHTS], *[delta[n] for n in WEIGHTS],
            *[new_m[n] for n in WEIGHTS], *[new_v[n] for n in WEIGHTS])
```

```python
import functools

import numpy as np
import jax
import jax.numpy as jnp
from jax import lax
from jax.experimental import pallas as pl
from jax.experimental.pallas import tpu as pltpu

F32, BF16 = jnp.float32, jnp.bfloat16
MESH = pl.DeviceIdType.MESH

D_MODEL = 1024
HEAD_DIM = 64
ATTN_W = 512
DILATIONS = (1, 4, 16)
BLK = 128
ATTN_BLOCKS_PER_STEP = 4
ROPE_DIM = 16
ROPE_THETA = 500000.0
SGU_W = 512
SGU_CHUNK = 128
SGU_GROUPS = 8
D_FF = 2816
N_CHIPS = 4
FF_SHARD = D_FF // N_CHIPS
IN_COLS = 7680
EPS = 1e-6
NEG = -1e30
LANES = 128
VMEM_LIMIT = 52 * 1024 * 1024

ADAM_LR, ADAM_B1, ADAM_B2, ADAM_EPS, ADAM_WD, ADAM_STEP = 0.001, 0.9, 0.999, 1e-08, 0.01, 10

QKV_BLOCKS = 9


def _w_in_block(part, g):
    return part * len(DILATIONS) + g


def _cparams(ngrid):
    return pltpu.CompilerParams(dimension_semantics=("arbitrary",) * ngrid, vmem_limit_bytes=VMEM_LIMIT)


def _full(shape):
    return pl.BlockSpec(shape, lambda *_: (0,) * len(shape))


def _resident(shape):
    return pl.BlockSpec(shape, lambda *_: (0,) * len(shape), pipeline_mode=pl.Buffered(1))


NT = ((1,), (1,))
TN = ((0,), (0,))


def _rope(v, cos_t, sin_t):
    half = ROPE_DIM // 2
    first = (lax.broadcasted_iota(jnp.int32, cos_t.shape, 1) % HEAD_DIM) < half
    outs = []
    for cs in range(v.shape[1] // LANES):
        x = v[:, cs * LANES:(cs + 1) * LANES]
        partner = jnp.where(first, pltpu.roll(x, LANES - half, axis=1), pltpu.roll(x, half, axis=1))
        outs.append(x * cos_t + partner * sin_t)
    return outs[0] if len(outs) == 1 else jnp.concatenate(outs, axis=1)


def _spread_heads(v2, upper):
    other = pltpu.roll(v2, HEAD_DIM, axis=1)
    h0 = jnp.where(upper, other, v2)
    h1 = jnp.where(upper, v2, other)
    return jnp.concatenate([jnp.concatenate([h0, h0], axis=1), jnp.concatenate([h1, h1], axis=1)], axis=0)


def _sigmoid(v):
    return 0.5 * jnp.tanh(0.5 * v) + 0.5


def _rms_stats(v):
    r = lax.rsqrt(jnp.mean(v * v, axis=-1, keepdims=True) + EPS)
    return v * r, r


def _rms_bwd(dy, xhat, r, g):
    dxh = dy * g
    return r * (dxh - xhat * jnp.mean(dxh * xhat, axis=-1, keepdims=True))


def _head_sum_matrix():
    idx = np.arange(ATTN_W) // HEAD_DIM
    return jnp.asarray((idx[:, None] == idx[None, :]).astype(np.float32), dtype=BF16)


def _group_sum(v, e):
    hi = v.astype(BF16)
    lo = (v - hi.astype(F32)).astype(BF16)
    return jnp.dot(hi, e, preferred_element_type=F32) + jnp.dot(lo, e, preferred_element_type=F32)


TILE = 512


def _to_slabs(slab_ref, v):
    for cs in range(slab_ref.shape[0]):
        slab_ref[cs] = v[:, cs * LANES:(cs + 1) * LANES]


def _from_slabs(slab_ref):
    return jnp.concatenate([slab_ref[cs] for cs in range(slab_ref.shape[0])], axis=1)


def _class_rows(slab_ref, r, dil):
    n = slab_ref.shape[1] // dil
    return jnp.concatenate([slab_ref.at[cs][pl.ds(r, n, stride=dil), :] for cs in range(slab_ref.shape[0])], axis=1)


def _put_class_rows(slab_ref, r, dil, v):
    n = slab_ref.shape[1] // dil
    for cs in range(slab_ref.shape[0]):
        slab_ref.at[cs][pl.ds(r, n, stride=dil), :] = v[:, cs * LANES:(cs + 1) * LANES]


def _natural_from_group(slab_ref, grp_ref):
    dil = grp_ref.shape[0]
    for r in range(dil):
        _put_class_rows(slab_ref, r, dil, grp_ref[r].astype(F32))
    return _from_slabs(slab_ref)


def _group_from_natural(slab_ref, grp_ref, v):
    dil = grp_ref.shape[0]
    _to_slabs(slab_ref, v)
    for r in range(dil):
        grp_ref[r] = _class_rows(slab_ref, r, dil).astype(grp_ref.dtype)


def _group_spec(dil, tile, width):
    return pl.BlockSpec((dil, tile // dil, width), lambda i, *_: (0, i, 0))


def _slabs(tile, width):
    return pltpu.VMEM((width // LANES, tile, LANES), F32)


def _rope_consts():
    lane = np.arange(LANES) % HEAD_DIM
    fi = lane % (ROPE_DIM // 2)
    invf = np.where(lane < ROPE_DIM, ROPE_THETA ** (-(2.0 * fi) / ROPE_DIM), 0.0)
    sgn = np.where(lane < ROPE_DIM // 2, -1.0, np.where(lane < ROPE_DIM, 1.0, 0.0))
    return (jnp.asarray(invf.astype(np.float32)).reshape(1, LANES), jnp.asarray(sgn.astype(np.float32)).reshape(1, LANES))


def _rope_tables(pos_col):
    t = pos_col.shape[0]
    tile = min(t, TILE)
    invf, sgn = _rope_consts()

    def body(p_ref, f_ref, s_ref, c0, s0, c1, s1, c2, s2, slab_c, slab_s):
        ang = p_ref[...].astype(F32) * f_ref[...]
        cos, sin = jnp.cos(ang), jnp.sin(ang) * s_ref[...]
        c0[...] = cos
        s0[...] = sin
        _group_from_natural(slab_c, c1, cos)
        _group_from_natural(slab_s, s1, sin)
        for r in range(DILATIONS[2]):
            c2[r] = _class_rows(slab_c, r, DILATIONS[2])
            s2[r] = _class_rows(slab_s, r, DILATIONS[2])

    nat = pl.BlockSpec((tile, LANES), lambda i: (i, 0))
    specs, shapes = [nat, nat], [(t, LANES)] * 2
    for d in DILATIONS[1:]:
        specs += [_group_spec(d, tile, LANES)] * 2
        shapes += [(d, t // d, LANES)] * 2
    outs = pl.pallas_call(
        body, grid=(t // tile,),
        in_specs=[pl.BlockSpec((tile, 1), lambda i: (i, 0)), _full((1, LANES)), _full((1, LANES))],
        out_specs=specs, out_shape=[jax.ShapeDtypeStruct(s, F32) for s in shapes],
        scratch_shapes=[_slabs(tile, LANES)] * 2,
        compiler_params=_cparams(1), name="rope_tables")(pos_col, invf, sgn)
    return [(outs[2 * g].reshape(t, LANES), outs[2 * g + 1].reshape(t, LANES)) for g in range(len(DILATIONS))]


def _norm_fwd(x, g):
    t = x.shape[0]
    tile = min(t, TILE)

    def body(x_ref, g_ref, h0_ref, h1_ref, h2_ref, slab):
        xhat, _ = _rms_stats(x_ref[...])
        hn = xhat * g_ref[...]
        h0_ref[...] = hn.astype(BF16)
        _group_from_natural(slab, h1_ref, hn)
        for r in range(DILATIONS[2]):
            h2_ref[r] = _class_rows(slab, r, DILATIONS[2]).astype(BF16)

    nat = pl.BlockSpec((tile, D_MODEL), lambda i: (i, 0))
    return pl.pallas_call(
        body, grid=(t // tile,),
        in_specs=[nat, _full((1, D_MODEL))],
        out_specs=[nat] + [_group_spec(d, tile, D_MODEL) for d in DILATIONS[1:]],
        out_shape=[jax.ShapeDtypeStruct((t, D_MODEL), BF16)]
        + [jax.ShapeDtypeStruct((d, t // d, D_MODEL), BF16) for d in DILATIONS[1:]],
        scratch_shapes=[_slabs(tile, D_MODEL)],
        compiler_params=_cparams(1), name="norm1_fwd")(x, g)


GU_COLS = 3072
GROUP_COLS = 1536
GU_HALF = GU_COLS // 2


def _w_in_spec(width, block):
    return pl.BlockSpec((D_MODEL, width), lambda i: (0, block), pipeline_mode=pl.Buffered(1))


def _gu_w_specs():
    first = QKV_BLOCKS * ATTN_W // GU_HALF
    return [_w_in_spec(GU_HALF, first), _w_in_spec(GU_HALF, first + 1)]


def _group_w_specs(g):
    return [_w_in_spec(ATTN_W, _w_in_block(part, g)) for part in range(3)]


def _in_proj(hs, w_in, tables):
    t = hs[0].shape[0]
    tm = min(t, 1024)

    def body_gu(h_ref, w0_ref, w1_ref, o_ref):
        h = h_ref[...]
        o_ref[:, 0:GU_HALF] = jnp.dot(h, w0_ref[...], preferred_element_type=F32).astype(BF16)
        o_ref[:, GU_HALF:] = jnp.dot(h, w1_ref[...], preferred_element_type=F32).astype(BF16)

    gu = _token_call("in_proj_gates_uv", body_gu, t, tm,
                     [(hs[0], _rows_spec(tm, D_MODEL))] + [(w_in, s) for s in _gu_w_specs()],
                     [((t, GU_COLS), BF16, _rows_spec(tm, GU_COLS))])[0]

    qkvs = []
    for g in range(len(DILATIONS)):

        def body_qkv(h_ref, wq_ref, wk_ref, wv_ref, cos_ref, sin_ref, o_ref):
            h = h_ref[...]
            cos_w, sin_w = cos_ref[...], sin_ref[...]
            q = jnp.dot(h, wq_ref[...], preferred_element_type=F32)
            o_ref[:, 0:ATTN_W] = (_rope(q, cos_w, sin_w) * HEAD_DIM ** -0.5).astype(BF16)
            k = jnp.dot(h, wk_ref[...], preferred_element_type=F32)
            o_ref[:, ATTN_W:2 * ATTN_W] = _rope(k, cos_w, sin_w).astype(BF16)
            o_ref[:, 2 * ATTN_W:] = jnp.dot(h, wv_ref[...], preferred_element_type=F32).astype(BF16)

        cos_t, sin_t = tables[g]
        qkvs.append(_token_call(
            f"in_proj_qkv_g{g}", body_qkv, t, tm,
            [(hs[g].reshape(t, D_MODEL), _rows_spec(tm, D_MODEL))] + [(w_in, s) for s in _group_w_specs(g)]
            + [(cos_t, _rows_spec(tm, LANES)), (sin_t, _rows_spec(tm, LANES))],
            [((t, GROUP_COLS), BF16, _rows_spec(tm, GROUP_COLS))])[0])
    return gu, qkvs


def _attn_masks(n):
    row = lax.broadcasted_iota(jnp.int32, (2 * BLK, 2 * BLK), 0) % BLK
    col = lax.broadcasted_iota(jnp.int32, (2 * BLK, 2 * BLK), 1)
    diff = BLK + row - col
    valid = (diff >= 0) & (diff <= BLK) & ((col >= BLK) | (n > 0))
    upper = lax.broadcasted_iota(jnp.int32, (BLK, LANES), 1) >= HEAD_DIM
    return valid, upper


def _stack_heads(v2, upper):
    zero = jnp.zeros_like(v2)
    return jnp.concatenate([jnp.where(upper, zero, v2), jnp.where(upper, v2, zero)], axis=0)


def _unstack_heads(v, upper):
    return jnp.where(upper, v[BLK:], v[:BLK])


def _attn_fwd(qkv, g, dil):
    t = qkv.shape[0]
    length = t // dil
    nb = length // BLK
    per_step = min(nb, ATTN_BLOCKS_PER_STEP)
    view = qkv.reshape(dil, length, GROUP_COLS)

    def body(q_ref, kc_ref, kp_ref, vc_ref, vp_ref, o_ref, l_ref, kwin, vwin):
        n = pl.program_id(1)
        kwin[0:BLK] = kp_ref[...]
        kwin[BLK:] = kc_ref[...]
        vwin[0:BLK] = vp_ref[...]
        vwin[BLK:] = vc_ref[...]

        def block(b, carry):
            valid, upper = _attn_masks(n * per_step + b)
            rows = pl.ds(pl.multiple_of(b * BLK, BLK), BLK)
            window = pl.ds(pl.multiple_of(b * BLK, BLK), 2 * BLK)
            slabs = [slice(p * LANES, (p + 1) * LANES) for p in range(ATTN_W // LANES)]
            ss = [lax.dot_general(_stack_heads(q_ref[rows, sl], upper), kwin[window, sl], (NT, ((), ())),
                                  preferred_element_type=F32) for sl in slabs]
            soft = []
            for s in ss:
                s = jnp.where(valid, s, NEG)
                m = jnp.max(s, axis=1, keepdims=True)
                pe = jnp.exp(s - m)
                soft.append((m, pe, jnp.sum(pe, axis=1, keepdims=True)))
            for sl, (m, pe, den) in zip(slabs, soft):
                o = jnp.dot(pe.astype(BF16), vwin[window, sl], preferred_element_type=F32) / den
                lse = jnp.broadcast_to(m + jnp.log(den), (2 * BLK, LANES))
                o_ref[rows, sl] = _unstack_heads(o, upper).astype(BF16)
                l_ref[rows, sl] = _unstack_heads(lse, upper)
            return carry

        lax.fori_loop(0, per_step, block, 0)

    rows = per_step * BLK
    cur = lambda part: pl.BlockSpec((None, rows, ATTN_W), lambda r, n: (r, n, part))
    prev = lambda part: pl.BlockSpec((None, BLK, ATTN_W), lambda r, n: (r, jnp.maximum(n * per_step - 1, 0), part))
    out_spec = pl.BlockSpec((None, rows, ATTN_W), lambda r, n: (r, n, 0))
    return pl.pallas_call(
        body, grid=(dil, nb // per_step),
        in_specs=[cur(0), cur(1), prev(1), cur(2), prev(2)],
        out_specs=[out_spec, out_spec],
        out_shape=[jax.ShapeDtypeStruct((dil, length, ATTN_W), BF16), jax.ShapeDtypeStruct((dil, length, ATTN_W), F32)],
        scratch_shapes=[pltpu.VMEM((rows + BLK, ATTN_W), BF16)] * 2,
        compiler_params=_cparams(2), name=f"attn_fwd_g{g}")(view, view, view, view, view)


def _alphas(l0, l1, l2):
    m = jnp.maximum(jnp.maximum(l0, l1), l2)
    e0, e1, e2 = jnp.exp(l0 - m), jnp.exp(l1 - m), jnp.exp(l2 - m)
    inv = 1.0 / (e0 + e1 + e2)
    return e0 * inv, e1 * inv, e2 * inv


def _natural_group_values(o_refs, l_refs, slabs):
    os_ = [o_refs[0][0].astype(F32)] + [_natural_from_group(slabs[2 * g - 2], o_refs[g]) for g in (1, 2)]
    ls_ = [l_refs[0][0]] + [_natural_from_group(slabs[2 * g - 1], l_refs[g]) for g in (1, 2)]
    return os_, ls_


def _combine_fwd(os_, ls_):
    t = os_[0].shape[1]
    tile = min(t, TILE)

    def body(o0, o1, o2, l0, l1, l2, a_ref, *slabs):
        ov, lv = _natural_group_values((o0, o1, o2), (l0, l1, l2), slabs)
        a0, a1, a2 = _alphas(*lv)
        a_ref[...] = (a0 * ov[0] + a1 * ov[1] + a2 * ov[2]).astype(BF16)

    specs = [_group_spec(d, tile, ATTN_W) for d in DILATIONS]
    return pl.pallas_call(
        body, grid=(t // tile,), in_specs=specs * 2, out_specs=pl.BlockSpec((tile, ATTN_W), lambda i: (i, 0)),
        out_shape=jax.ShapeDtypeStruct((t, ATTN_W), BF16),
        scratch_shapes=[_slabs(tile, ATTN_W)] * 4,
        compiler_params=_cparams(1), name="combine_fwd")(*os_, *ls_)


def _combine_bwd(dattn, os_, ls_):
    t = dattn.shape[0]
    tile = min(t, TILE)
    e = _head_sum_matrix()

    def body(d_ref, o0, o1, o2, l0, l1, l2, e_ref, do0, do1, do2, c0, c1, c2, *slabs):
        ov, lv = _natural_group_values((o0, o1, o2), (l0, l1, l2), slabs)
        alphas = _alphas(*lv)
        d = d_ref[...]
        attn = alphas[0] * ov[0] + alphas[1] * ov[1] + alphas[2] * ov[2]
        s = _group_sum(d * attn, e_ref[...])
        do0[0] = (alphas[0] * d).astype(BF16)
        c0[0] = -alphas[0] * s
        for g, do_ref, c_ref in ((1, do1, c1), (2, do2, c2)):
            _group_from_natural(slabs[2 * g - 2], do_ref, alphas[g] * d)
            _group_from_natural(slabs[2 * g - 1], c_ref, -alphas[g] * s)

    specs = [_group_spec(d, tile, ATTN_W) for d in DILATIONS]
    shapes = [(d, t // d, ATTN_W) for d in DILATIONS]
    outs = pl.pallas_call(
        body, grid=(t // tile,),
        in_specs=[pl.BlockSpec((tile, ATTN_W), lambda i: (i, 0))] + specs * 2 + [_full((ATTN_W, ATTN_W))],
        out_specs=specs * 2,
        out_shape=[jax.ShapeDtypeStruct(s, BF16) for s in shapes] + [jax.ShapeDtypeStruct(s, F32) for s in shapes],
        scratch_shapes=[_slabs(tile, ATTN_W)] * 4,
        compiler_params=_cparams(1), name="combine_bwd")(dattn, *os_, *ls_, e)
    return outs[:3], outs[3:]


def _attn_bwd(qkv, do, cc, lse, cos_t, sin_t, g, dil):
    t = qkv.shape[0]
    length = t // dil
    nb = length // BLK
    per_step = min(nb, ATTN_BLOCKS_PER_STEP)
    nsteps = nb // per_step
    rows_per_step = per_step * BLK
    qkv_v = qkv.reshape(dil, length, GROUP_COLS)
    cos_v, sin_v = (a.reshape(dil, length, LANES) for a in (cos_t, sin_t))
    scale = HEAD_DIM ** -0.5
    dq_cols, dk_cols, dv_cols = (slice(i * ATTN_W, (i + 1) * ATTN_W) for i in range(3))

    def body(q_ref, kc_ref, kp_ref, vc_ref, vp_ref, do_ref, c_ref, l_ref, cosc, sinc, cosp, sinp,
             out_ref, acc, kwin, vwin, cwin, swin):
        n = pl.program_id(1)

        def one_block(b):
            valid, upper = _attn_masks(n * per_step + b)
            start = b * BLK if isinstance(b, int) else pl.multiple_of(b * BLK, BLK)
            rows, before, window = pl.ds(start, BLK), pl.ds(start, BLK), pl.ds(start, 2 * BLK)
            own = pl.ds(start + BLK, BLK)
            dq_parts, dkp_parts, dkc_parts, dvp_parts, dvc_parts = [], [], [], [], []
            npairs = ATTN_W // LANES
            slabs = [slice(p * LANES, (p + 1) * LANES) for p in range(npairs)]
            qss = [_stack_heads(q_ref[rows, sl], upper) for sl in slabs]
            doss = [_stack_heads(do_ref[rows, sl], upper) for sl in slabs]
            ss = [lax.dot_general(qss[p], kwin[window, slabs[p]], (NT, ((), ())), preferred_element_type=F32) for p in range(npairs)]
            dpvs = [lax.dot_general(doss[p], vwin[window, slabs[p]], (NT, ((), ())), preferred_element_type=F32)
                    for p in range(npairs)]
            pes = [jnp.exp(jnp.where(valid, ss[p], NEG) - _spread_heads(l_ref[rows, slabs[p]], upper)) for p in range(npairs)]
            dss = [(pes[p] * (dpvs[p] + _spread_heads(c_ref[rows, slabs[p]], upper))).astype(BF16) for p in range(npairs)]
            for p in range(npairs):
                qs, dos, ds = qss[p], doss[p], dss[p]
                dq2 = _unstack_heads(jnp.dot(ds, kwin[window, slabs[p]], preferred_element_type=F32), upper)
                dk2 = lax.dot_general(ds, qs, (TN, ((), ())), preferred_element_type=F32)
                dv2 = lax.dot_general(pes[p].astype(BF16), dos, (TN, ((), ())), preferred_element_type=F32)
                dq_parts.append(dq2)
                dkp_parts.append(dk2[:BLK])
                dkc_parts.append(dk2[BLK:])
                dvp_parts.append(dv2[:BLK])
                dvc_parts.append(dv2[BLK:])
            dq = _rope(jnp.concatenate(dq_parts, axis=1) * scale, cwin[own, :], swin[own, :])
            dkc = _rope(jnp.concatenate(dkc_parts, axis=1), cwin[own, :], swin[own, :])
            dkp = _rope(jnp.concatenate(dkp_parts, axis=1), cwin[before, :], swin[before, :])
            return dq, dkp, dkc, jnp.concatenate(dvp_parts, axis=1), jnp.concatenate(dvc_parts, axis=1)

        @pl.when(n < nsteps)
        def _():
            kwin[0:BLK] = kp_ref[...]
            kwin[BLK:] = kc_ref[...]
            vwin[0:BLK] = vp_ref[...]
            vwin[BLK:] = vc_ref[...]
            cwin[0:BLK] = cosp[...]
            cwin[BLK:] = cosc[...]
            swin[0:BLK] = -sinp[...]
            swin[BLK:] = -sinc[...]
            dq, dkp, dkc, dvp, dvc = one_block(0)
            last = slice(rows_per_step - BLK, rows_per_step)

            @pl.when(n > 0)
            def _():
                if per_step > 1:
                    out_ref[0:rows_per_step - BLK, :] = acc[0:rows_per_step - BLK, :].astype(BF16)
                out_ref[last, dq_cols] = acc[last, dq_cols].astype(BF16)
                out_ref[last, dk_cols] = (acc[last, dk_cols] + dkp).astype(BF16)
                out_ref[last, dv_cols] = (acc[last, dv_cols] + dvp).astype(BF16)

            acc[0:BLK, dq_cols] = dq
            acc[0:BLK, dk_cols] = dkc
            acc[0:BLK, dv_cols] = dvc

            def later(b, carry):
                dq, dkp, dkc, dvp, dvc = one_block(b)
                start = pl.multiple_of(b * BLK, BLK)
                before, rows = pl.ds(start - BLK, BLK), pl.ds(start, BLK)
                acc[before, dk_cols] += dkp
                acc[before, dv_cols] += dvp
                acc[rows, dq_cols] = dq
                acc[rows, dk_cols] = dkc
                acc[rows, dv_cols] = dvc
                return carry

            lax.fori_loop(1, per_step, later, 0)

        @pl.when(n == flush_at)
        def _():
            out_ref[...] = acc[...].astype(BF16)

    flush_at = nsteps - 1 if nsteps == 1 else nsteps
    out_lag = 0 if nsteps == 1 else 1
    nc = lambda n: jnp.minimum(n, nsteps - 1)
    npv = lambda n: jnp.maximum(jnp.minimum(n, nsteps - 1) * per_step - 1, 0)
    cur = lambda part: pl.BlockSpec((None, rows_per_step, ATTN_W), lambda r, n: (r, nc(n), part))
    prev = lambda part: pl.BlockSpec((None, BLK, ATTN_W), lambda r, n: (r, npv(n), part))
    row = pl.BlockSpec((None, rows_per_step, ATTN_W), lambda r, n: (r, nc(n), 0))
    tab_c = pl.BlockSpec((None, rows_per_step, LANES), lambda r, n: (r, nc(n), 0))
    tab_p = pl.BlockSpec((None, BLK, LANES), lambda r, n: (r, npv(n), 0))
    out_spec = pl.BlockSpec((None, rows_per_step, GROUP_COLS), lambda r, n: (r, jnp.maximum(n - out_lag, 0), 0))
    out = pl.pallas_call(
        body, grid=(dil, nsteps + out_lag),
        in_specs=[cur(0), cur(1), prev(1), cur(2), prev(2), row, row, row, tab_c, tab_c, tab_p, tab_p],
        out_specs=out_spec,
        out_shape=jax.ShapeDtypeStruct((dil, length, GROUP_COLS), BF16),
        scratch_shapes=[pltpu.VMEM((rows_per_step, GROUP_COLS), F32)]
        + [pltpu.VMEM((rows_per_step + BLK, ATTN_W), BF16)] * 2 + [pltpu.VMEM((rows_per_step + BLK, LANES), F32)] * 2,
        compiler_params=_cparams(2), name=f"attn_bwd_g{g}")(
            qkv_v, qkv_v, qkv_v, qkv_v, qkv_v, do, cc, lse, cos_v, sin_v, cos_v, sin_v)
    return out.reshape(t, GROUP_COLS)


SQRT_HALF = 0.7071067811865476
INV_SQRT_2PI = 0.3989422804014327


def _sgu_core(uv, g, b, w_ref, bias):
    cdf = 0.5 * (1.0 + lax.erf(uv * SQRT_HALF))
    z = uv * cdf
    u, v = z[:, :SGU_W], z[:, SGU_W:]
    mu = jnp.mean(v, axis=1, keepdims=True)
    xc = v - mu
    rs = lax.rsqrt(jnp.mean(xc * xc, axis=1, keepdims=True) + EPS)
    xhat = xc * rs
    vn = xhat * g + b
    row = lax.broadcasted_iota(jnp.int32, (SGU_CHUNK, SGU_CHUNK), 0)
    col = lax.broadcasted_iota(jnp.int32, (SGU_CHUNK, SGU_CHUNK), 1)
    tril = row >= col
    upper = lax.broadcasted_iota(jnp.int32, (SGU_CHUNK, LANES), 1) >= SGU_W // SGU_GROUPS
    ws, vlo, vhi, mixed = [], [], [], []
    for pr in range(SGU_W // LANES):
        sl = slice(pr * LANES, (pr + 1) * LANES)
        w0 = jnp.where(tril, w_ref[2 * pr], 0.0).astype(BF16)
        w1 = jnp.where(tril, w_ref[2 * pr + 1], 0.0).astype(BF16)
        vn2 = vn[:, sl]
        lo = jnp.where(upper, 0.0, vn2).astype(BF16)
        hi = jnp.where(upper, vn2, 0.0).astype(BF16)
        mixed.append(jnp.dot(w0, lo, preferred_element_type=F32) + jnp.dot(w1, hi, preferred_element_type=F32)
                     + bias[:, sl])
        ws.append((w0, w1))
        vlo.append(lo)
        vhi.append(hi)
    return cdf, u, xhat, rs, jnp.concatenate(mixed, axis=1), ws, vlo, vhi, tril, upper


SGU_STEP = 4 * SGU_CHUNK


def _for_chunks(step_rows, fn):
    def one(ci, carry):
        fn(pl.ds(pl.multiple_of(ci * SGU_CHUNK, SGU_CHUNK), SGU_CHUNK))
        return carry

    lax.fori_loop(0, step_rows // SGU_CHUNK, one, 0)


def _sgu_fwd(gu, ln_g, ln_b, w_s, bias_exp):
    t = gu.shape[0]
    step = min(t, SGU_STEP)

    def body(uv_ref, g_ref, b_ref, w_ref, bias_ref, o_ref):
        def chunk(rows):
            _, u, _, _, mixed, *_ = _sgu_core(uv_ref[rows, :].astype(F32), g_ref[...], b_ref[...], w_ref, bias_ref[...])
            o_ref[rows, :] = (u * mixed).astype(BF16)

        _for_chunks(step, chunk)

    return pl.pallas_call(
        body, grid=(t // step,),
        in_specs=[pl.BlockSpec((step, 2 * SGU_W), lambda n: (n, 0)), _full((1, SGU_W)), _full((1, SGU_W)),
                  _full((SGU_GROUPS, SGU_CHUNK, SGU_CHUNK)), _full((SGU_CHUNK, SGU_W))],
        out_specs=pl.BlockSpec((step, SGU_W), lambda n: (n, 0)),
        out_shape=jax.ShapeDtypeStruct((t, SGU_W), BF16),
        compiler_params=_cparams(1), name="sgu_fwd")(gu, ln_g, ln_b, w_s, bias_exp)


def _sgu_bwd(dproj, gu, dsgu, ln_g, ln_b, w_s, bias_exp):
    t = gu.shape[0]
    step = min(t, SGU_STEP)
    nsteps = t // step
    e = _head_sum_matrix()

    def body(dp_in, uv_ref, ds_ref, g_ref, b_ref, w_ref, bias_ref, e_ref, out_ref, dw_ref, dbias_ref, dg_ref, db_ref):
        n = pl.program_id(0)

        @pl.when(n == 0)
        def _():
            dw_ref[...] = jnp.zeros(dw_ref.shape, F32)
            dbias_ref[...] = jnp.zeros(dbias_ref.shape, F32)
            dg_ref[...] = jnp.zeros(dg_ref.shape, F32)
            db_ref[...] = jnp.zeros(db_ref.shape, F32)

        _for_chunks(step, functools.partial(chunk, uv_ref, ds_ref, g_ref, b_ref, w_ref, bias_ref, out_ref, dw_ref, dbias_ref,
                                            dg_ref, db_ref))

        @pl.when(n == nsteps - 1)
        def _():
            dbias_ref[...] = _group_sum(dbias_ref[...], e_ref[...])

    def chunk(uv_ref, ds_ref, g_ref, b_ref, w_ref, bias_ref, out_ref, dw_ref, dbias_ref, dg_ref, db_ref, rows):
        uv = uv_ref[rows, :].astype(F32)
        g = g_ref[...]
        cdf, u, xhat, rs, mixed, ws, vlo, vhi, tril, upper = _sgu_core(uv, g, b_ref[...], w_ref, bias_ref[...])
        dsg = ds_ref[rows, :]
        du = dsg * mixed
        dmixed = dsg * u
        dbias_ref[...] += dmixed
        dvn = []
        for pr in range(SGU_W // LANES):
            sl = slice(pr * LANES, (pr + 1) * LANES)
            dm2 = dmixed[:, sl]
            dlo = jnp.where(upper, 0.0, dm2).astype(BF16)
            dhi = jnp.where(upper, dm2, 0.0).astype(BF16)
            w0, w1 = ws[pr]
            dvn.append(lax.dot_general(w0, dlo, (TN, ((), ())), preferred_element_type=F32)
                       + lax.dot_general(w1, dhi, (TN, ((), ())), preferred_element_type=F32))
            dw0 = lax.dot_general(dlo, vlo[pr], (NT, ((), ())), preferred_element_type=F32)
            dw1 = lax.dot_general(dhi, vhi[pr], (NT, ((), ())), preferred_element_type=F32)
            dw_ref[2 * pr] += jnp.where(tril, dw0, 0.0)
            dw_ref[2 * pr + 1] += jnp.where(tril, dw1, 0.0)
        dvn = jnp.concatenate(dvn, axis=1)
        dg_ref[...] += jnp.sum(dvn * xhat, axis=0, keepdims=True)
        db_ref[...] += jnp.sum(dvn, axis=0, keepdims=True)
        dxh = dvn * g
        dv = rs * (dxh - jnp.mean(dxh, axis=1, keepdims=True) - xhat * jnp.mean(dxh * xhat, axis=1, keepdims=True))
        dz = jnp.concatenate([du, dv], axis=1)
        dgelu = cdf + uv * (INV_SQRT_2PI * jnp.exp(-0.5 * uv * uv))
        out_ref[rows, :] = (dz * dgelu).astype(BF16)

    outs = pl.pallas_call(
        body, grid=(nsteps,),
        in_specs=[pl.BlockSpec(memory_space=pl.ANY), pl.BlockSpec((step, 2 * SGU_W), lambda n: (n, 0)),
                  pl.BlockSpec((step, SGU_W), lambda n: (n, 0)), _full((1, SGU_W)), _full((1, SGU_W)),
                  _full((SGU_GROUPS, SGU_CHUNK, SGU_CHUNK)), _full((SGU_CHUNK, SGU_W)), _full((ATTN_W, ATTN_W))],
        out_specs=[pl.BlockSpec((step, 2 * SGU_W), lambda n: (n, 0)), _full((SGU_GROUPS, SGU_CHUNK, SGU_CHUNK)),
                   _full((SGU_CHUNK, SGU_W)), _full((1, SGU_W)), _full((1, SGU_W))],
        out_shape=[jax.ShapeDtypeStruct(dproj.shape, BF16), jax.ShapeDtypeStruct((SGU_GROUPS, SGU_CHUNK, SGU_CHUNK), F32),
                   jax.ShapeDtypeStruct((SGU_CHUNK, SGU_W), F32), jax.ShapeDtypeStruct((1, SGU_W), F32),
                   jax.ShapeDtypeStruct((1, SGU_W), F32)],
        input_output_aliases={0: 0},
        compiler_params=_cparams(1), name="sgu_bwd")(dproj, gu, dsgu, ln_g, ln_b, w_s, bias_exp, e)
    return outs


def _merge_fwd(attn, sgu, gu, x, w_pa, w_ps, w_out, g2):
    t = x.shape[0]
    tm = min(t, 512)

    def body(a_ref, s_ref, ga_ref, gb_ref, x_ref, wpa, wps, wo, g_ref, pa_ref, ps_ref, m_ref, x1_ref, h2_ref):
        pa = jnp.dot(a_ref[...], wpa[...], preferred_element_type=F32)
        ps = jnp.dot(s_ref[...], wps[...], preferred_element_type=F32)
        merged = (_sigmoid(ga_ref[...].astype(F32)) * pa + _sigmoid(gb_ref[...].astype(F32)) * ps).astype(BF16)
        x1 = x_ref[...] + jnp.dot(merged, wo[...], preferred_element_type=F32)
        xhat, _ = _rms_stats(x1)
        pa_ref[...] = pa.astype(BF16)
        ps_ref[...] = ps.astype(BF16)
        m_ref[...] = merged
        x1_ref[...] = x1
        h2_ref[...] = (xhat * g_ref[...]).astype(BF16)

    half = pl.BlockSpec((tm, ATTN_W), lambda i: (i, 0))
    full = pl.BlockSpec((tm, D_MODEL), lambda i: (i, 0))
    return pl.pallas_call(
        body, grid=(t // tm,),
        in_specs=[half, half, pl.BlockSpec((tm, D_MODEL), lambda i: (i, 1)), pl.BlockSpec((tm, D_MODEL), lambda i: (i, 2)),
                  full, _resident((ATTN_W, D_MODEL)), _resident((SGU_W, D_MODEL)), _resident((D_MODEL, D_MODEL)),
                  _full((1, D_MODEL))],
        out_specs=[full] * 5,
        out_shape=[jax.ShapeDtypeStruct((t, D_MODEL), BF16), jax.ShapeDtypeStruct((t, D_MODEL), BF16),
                   jax.ShapeDtypeStruct((t, D_MODEL), BF16), jax.ShapeDtypeStruct((t, D_MODEL), F32),
                   jax.ShapeDtypeStruct((t, D_MODEL), BF16)],
        compiler_params=_cparams(1), name="merge_fwd")(attn, sgu, gu, gu, x, w_pa, w_ps, w_out, g2)


def _merge_bwd(dx1b, gu, pa, ps, w_pa, w_ps, w_out):
    t = dx1b.shape[0]
    tm = min(t, 512)

    def body(d_ref, ga_ref, gb_ref, pa_ref, ps_ref, wpa, wps, wo, out_ref, dpa_ref, dps_ref, da_ref, dsg_ref):
        dm = lax.dot_general(d_ref[...], wo[...], (NT, ((), ())), preferred_element_type=F32)
        sa, sb = _sigmoid(ga_ref[...].astype(F32)), _sigmoid(gb_ref[...].astype(F32))
        dpa = (dm * sa).astype(BF16)
        dps = (dm * sb).astype(BF16)
        out_ref[:, 0:D_MODEL] = jnp.zeros((tm, D_MODEL), BF16)
        out_ref[:, D_MODEL:2 * D_MODEL] = (dm * pa_ref[...].astype(F32) * sa * (1.0 - sa)).astype(BF16)
        out_ref[:, 2 * D_MODEL:GU_COLS] = (dm * ps_ref[...].astype(F32) * sb * (1.0 - sb)).astype(BF16)
        dpa_ref[...] = dpa
        dps_ref[...] = dps
        da_ref[...] = lax.dot_general(dpa, wpa[...], (NT, ((), ())), preferred_element_type=F32)
        dsg_ref[...] = lax.dot_general(dps, wps[...], (NT, ((), ())), preferred_element_type=F32)

    half = pl.BlockSpec((tm, ATTN_W), lambda i: (i, 0))
    full = pl.BlockSpec((tm, D_MODEL), lambda i: (i, 0))
    return pl.pallas_call(
        body, grid=(t // tm,),
        in_specs=[full, pl.BlockSpec((tm, D_MODEL), lambda i: (i, 1)),
                  pl.BlockSpec((tm, D_MODEL), lambda i: (i, 2)), full, full,
                  _resident((ATTN_W, D_MODEL)), _resident((SGU_W, D_MODEL)), _resident((D_MODEL, D_MODEL))],
        out_specs=[pl.BlockSpec((tm, GU_COLS), lambda i: (i, 0)), full, full, half, half],
        out_shape=[jax.ShapeDtypeStruct((t, GU_COLS), BF16), jax.ShapeDtypeStruct((t, D_MODEL), BF16),
                   jax.ShapeDtypeStruct((t, D_MODEL), BF16), jax.ShapeDtypeStruct((t, ATTN_W), F32),
                   jax.ShapeDtypeStruct((t, SGU_W), F32)],
        compiler_params=_cparams(1), name="merge_bwd")(dx1b, gu, gu, pa, ps, w_pa, w_ps, w_out)


def _token_call(name, body, t, tm, ins, outs, reds=(), scratch=()):
    return pl.pallas_call(
        body, grid=(t // tm,), in_specs=[s for _, s in ins],
        out_specs=[o[2] for o in outs] + [_full(r) for r in reds],
        out_shape=[jax.ShapeDtypeStruct(o[0], o[1]) for o in outs] + [jax.ShapeDtypeStruct(r, F32) for r in reds],
        scratch_shapes=list(scratch), compiler_params=_cparams(1), name=name)(*[a for a, _ in ins])


def _rows_spec(tm, width):
    return pl.BlockSpec((tm, width), lambda i: (i, 0))


def _chips_spec(tm):
    return pl.BlockSpec((N_CHIPS, tm, FF_SHARD), lambda i: (0, i, 0))


def _zero_at_start(*refs):
    @pl.when(pl.program_id(0) == 0)
    def _():
        for r in refs:
            r[...] = jnp.zeros(r.shape, r.dtype)


def _ffn_fwd(h2, w_g, w_u):
    t = h2.shape[0]
    tm = min(t, 512)

    def body(h_ref, wg_ref, wu_ref, fa_ref, fb_ref, ff_ref):
        h = h_ref[...]
        for s in range(N_CHIPS):
            a = jnp.dot(h, wg_ref[s], preferred_element_type=F32)
            b = jnp.dot(h, wu_ref[s], preferred_element_type=F32)
            sg = _sigmoid(a)
            silu = a * sg
            fa_ref[s] = (b * (sg * (1.0 + a * (1.0 - sg)))).astype(BF16)
            fb_ref[s] = silu.astype(BF16)
            ff_ref[s] = (silu * b).astype(BF16)

    shp = (N_CHIPS, t, FF_SHARD)
    w_spec = _resident((N_CHIPS, D_MODEL, FF_SHARD))
    return _token_call("ffn_fwd", body, t, tm, [(h2, _rows_spec(tm, D_MODEL)), (w_g, w_spec), (w_u, w_spec)],
                       [(shp, BF16, _chips_spec(tm))] * 3)


def _ffn_down_loss(ff, w_d, x1, tgt, gf):
    t = x1.shape[0]
    tm = min(t, 512)

    def body(ff_ref, wd_ref, x1_ref, tgt_ref, g_ref, dx2_ref, dx2b_ref, loss_ref, dgf_ref):
        _zero_at_start(loss_ref, dgf_ref)
        acc = jnp.dot(ff_ref[0], wd_ref[0], preferred_element_type=F32)
        for s in range(1, N_CHIPS):
            acc = acc + jnp.dot(ff_ref[s], wd_ref[s], preferred_element_type=F32)
        x2 = x1_ref[...] + acc
        g = g_ref[...]
        xhat, rr = _rms_stats(x2)
        diff = xhat * g - tgt_ref[...]
        rows = jnp.sum(diff * diff, axis=1, keepdims=True)
        loss_ref[...] += jnp.broadcast_to(jnp.sum(rows, axis=0, keepdims=True) * (0.5 / D_MODEL), (1, LANES))
        dy = diff * (1.0 / D_MODEL)
        dgf_ref[...] += jnp.sum(dy * xhat, axis=0, keepdims=True)
        dx2 = _rms_bwd(dy, xhat, rr, g)
        dx2_ref[...] = dx2
        dx2b_ref[...] = dx2.astype(BF16)

    row = _rows_spec(tm, D_MODEL)
    return _token_call("ffn_down_loss", body, t, tm,
                       [(ff, _chips_spec(tm)), (w_d, _resident((N_CHIPS, FF_SHARD, D_MODEL))), (x1, row), (tgt, row),
                        (gf, _full((1, D_MODEL)))],
                       [((t, D_MODEL), F32, row), ((t, D_MODEL), BF16, row)], reds=[(1, LANES), (1, D_MODEL)])


def _ffn_bwd_act(dx2b, w_d, fa, fb):
    t = dx2b.shape[0]
    tm = min(t, 512)

    def body(d_ref, wd_ref, fa_ref, fb_ref, da_ref, db_ref):
        d = d_ref[...]
        for s in range(N_CHIPS):
            dff = lax.dot_general(d, wd_ref[s], (NT, ((), ())), preferred_element_type=F32)
            da_ref[s] = (dff * fa_ref[s].astype(F32)).astype(BF16)
            db_ref[s] = (dff * fb_ref[s].astype(F32)).astype(BF16)

    shp = (N_CHIPS, t, FF_SHARD)
    return _token_call("ffn_bwd_act", body, t, tm,
                       [(dx2b, _rows_spec(tm, D_MODEL)), (w_d, _resident((N_CHIPS, FF_SHARD, D_MODEL))),
                        (fa, _chips_spec(tm)), (fb, _chips_spec(tm))],
                       [(shp, BF16, _chips_spec(tm))] * 2)


def _ffn_bwd_in(da, db, w_g, w_u, x1, dx2, g2):
    t = x1.shape[0]
    tm = min(t, 512)

    def body(da_ref, db_ref, wg_ref, wu_ref, x1_ref, dx2_ref, g_ref, dx1_ref, dx1b_ref, dg_ref):
        _zero_at_start(dg_ref)
        acc = None
        for s in range(N_CHIPS):
            part = (lax.dot_general(da_ref[s], wg_ref[s], (NT, ((), ())), preferred_element_type=F32)
                    + lax.dot_general(db_ref[s], wu_ref[s], (NT, ((), ())), preferred_element_type=F32))
            acc = part if acc is None else acc + part
        xhat, rr = _rms_stats(x1_ref[...])
        dg_ref[...] += jnp.sum(acc * xhat, axis=0, keepdims=True)
        dx1 = dx2_ref[...] + _rms_bwd(acc, xhat, rr, g_ref[...])
        dx1_ref[...] = dx1
        dx1b_ref[...] = dx1.astype(BF16)

    row = _rows_spec(tm, D_MODEL)
    w_spec = _resident((N_CHIPS, D_MODEL, FF_SHARD))
    return _token_call("ffn_bwd_in", body, t, tm,
                       [(da, _chips_spec(tm)), (db, _chips_spec(tm)), (w_g, w_spec), (w_u, w_spec), (x1, row), (dx2, row),
                        (g2, _full((1, D_MODEL)))],
                       [((t, D_MODEL), F32, row), ((t, D_MODEL), BF16, row)], reds=[(1, D_MODEL)])


def _group_dh(d, w_refs):
    dh = None
    for part, w_ref in enumerate(w_refs):
        term = lax.dot_general(d[:, part * ATTN_W:(part + 1) * ATTN_W], w_ref[...], (NT, ((), ())),
                               preferred_element_type=F32)
        dh = term if dh is None else dh + term
    return dh


def _in_proj_bwd(dgu, dqkvs, w_in, x, dx1, g1):
    t = x.shape[0]
    tile = min(t, TILE)
    ngroups = len(DILATIONS)

    def body(*refs):
        dgu_ref, dq_refs = refs[0], refs[1:1 + ngroups]
        w0_ref, w1_ref = refs[1 + ngroups:3 + ngroups]
        wg_refs = [refs[3 + ngroups + 3 * g:6 + ngroups + 3 * g] for g in range(ngroups)]
        x_ref, dx1_ref, g_ref, dx_ref, dg_ref = refs[3 + 4 * ngroups:5 + 4 * ngroups + 3]
        slabs = refs[5 + 4 * ngroups + 3:]
        _zero_at_start(dg_ref)
        for g in range(1, ngroups):
            dil = DILATIONS[g]
            part = _group_dh(dq_refs[g][...].reshape(tile, GROUP_COLS), wg_refs[g])
            for r in range(dil):
                _put_class_rows(slabs[g - 1], r, dil, part[r * (tile // dil):(r + 1) * (tile // dil)])
        dh = lax.dot_general(dgu_ref[:, 0:GU_HALF], w0_ref[...], (NT, ((), ())), preferred_element_type=F32)
        dh = dh + lax.dot_general(dgu_ref[:, GU_HALF:], w1_ref[...], (NT, ((), ())), preferred_element_type=F32)
        dh = dh + _group_dh(dq_refs[0][0], wg_refs[0])
        for slab in slabs:
            dh = dh + _from_slabs(slab)
        xhat, rr = _rms_stats(x_ref[...])
        dg_ref[...] += jnp.sum(dh * xhat, axis=0, keepdims=True)
        dx_ref[...] = dx1_ref[...] + _rms_bwd(dh, xhat, rr, g_ref[...])

    row = _rows_spec(tile, D_MODEL)
    group_ins = [(dqkvs[g].reshape(d, t // d, GROUP_COLS), _group_spec(d, tile, GROUP_COLS)) for g, d in enumerate(DILATIONS)]
    w_specs = _gu_w_specs() + [s for g in range(ngroups) for s in _group_w_specs(g)]
    return _token_call(
        "in_proj_bwd", body, t, tile,
        [(dgu, _rows_spec(tile, GU_COLS))] + group_ins + [(w_in, s) for s in w_specs]
        + [(x, row), (dx1, row), (g1, _full((1, D_MODEL)))],
        [((t, D_MODEL), F32, row)], reds=[(1, D_MODEL)], scratch=[_slabs(tile, D_MODEL)] * (ngroups - 1))


WGRAD_TK = 2048


def _wgrad_mm(name, grid, a, a_spec, b, b_spec, acc_shape, out_shape, out_spec, dst=None):
    nk = grid[-1]

    def body(*refs):
        a_ref, b_ref, o_ref, acc_ref = refs[0], refs[1], refs[-2], refs[-1]
        k = pl.program_id(len(grid) - 1)
        part = lax.dot_general(a_ref[...], b_ref[...], (TN, ((), ())), preferred_element_type=F32)

        @pl.when(k == 0)
        def _():
            acc_ref[...] = part

        @pl.when(k > 0)
        def _():
            acc_ref[...] += part

        @pl.when(k == nk - 1)
        def _():
            o_ref[...] = acc_ref[...].astype(BF16)

    filled = [] if dst is None else [dst]
    return pl.pallas_call(
        body, grid=grid, in_specs=[a_spec, b_spec] + [pl.BlockSpec(memory_space=pl.ANY)] * len(filled),
        out_specs=out_spec, out_shape=jax.ShapeDtypeStruct(out_shape, BF16), scratch_shapes=[pltpu.VMEM(acc_shape, F32)],
        input_output_aliases={2: 0} if filled else {}, compiler_params=_cparams(len(grid)), name=name)(a, b, *filled)


def _wgrad_2d(name, a, b, tm, tn):
    t, k1 = a.shape
    n = b.shape[1]
    tk = min(t, WGRAD_TK)
    return _wgrad_mm(name, (k1 // tm, n // tn, t // tk), a, pl.BlockSpec((tk, tm), lambda i, j, k: (k, i)),
                     b, pl.BlockSpec((tk, tn), lambda i, j, k: (k, j)), (tm, tn), (k1, n),
                     pl.BlockSpec((tm, tn), lambda i, j, k: (i, j)))


def _wgrad_in(hs, dgu, dqkvs):
    t = dgu.shape[0]
    tk = min(t, WGRAD_TK)
    gu_block = QKV_BLOCKS * ATTN_W // GU_HALF
    parts = [(hs[0], dgu, GU_HALF, lambda j: j + gu_block)]
    parts += [(hs[g].reshape(t, D_MODEL), dqkvs[g], ATTN_W, lambda j, g=g: _w_in_block(j, g)) for g in range(3)]
    dst = None
    for n, (a, b, tn, block_of) in enumerate(parts):
        dst = _wgrad_mm(f"wgrad_in_{n}", (1, b.shape[1] // tn, t // tk),
                        a, pl.BlockSpec((tk, D_MODEL), lambda i, j, k: (k, 0)), b, pl.BlockSpec((tk, tn), lambda i, j, k: (k, j)),
                        (D_MODEL, tn), (D_MODEL, IN_COLS),
                        pl.BlockSpec((D_MODEL, tn), lambda i, j, k, block_of=block_of: (0, block_of(j))), dst=dst)
    return dst


def _wgrad_ff_in(name, h2, da):
    t = h2.shape[0]
    tk = min(t, WGRAD_TK)
    return _wgrad_mm(name, (N_CHIPS, 1, t // tk), h2, pl.BlockSpec((tk, D_MODEL), lambda i, j, k: (k, 0)),
                     da, pl.BlockSpec((None, tk, FF_SHARD), lambda i, j, k: (i, k, 0)), (D_MODEL, FF_SHARD),
                     (N_CHIPS, D_MODEL, FF_SHARD), pl.BlockSpec((None, D_MODEL, FF_SHARD), lambda i, j, k: (i, 0, 0)))


def _wgrad_ff_down(ff, dx2b):
    t = dx2b.shape[0]
    tk = min(t, WGRAD_TK)
    return _wgrad_mm("wgrad_ffn_down", (N_CHIPS, 1, t // tk), ff, pl.BlockSpec((None, tk, FF_SHARD), lambda i, j, k: (i, k, 0)),
                     dx2b, pl.BlockSpec((tk, D_MODEL), lambda i, j, k: (k, 0)), (FF_SHARD, D_MODEL),
                     (N_CHIPS, FF_SHARD, D_MODEL), pl.BlockSpec((None, FF_SHARD, D_MODEL), lambda i, j, k: (i, 0, 0)))


def _local_step(x, pos_col, tgt, g1, ln_g, ln_b, w_s, b_s, g2, gf, first_weight, late_weights, on_grads=None):
    tables = _rope_tables(pos_col)
    bias_exp = jnp.repeat(jnp.transpose(b_s), SGU_W // SGU_GROUPS, axis=1)

    hs = _norm_fwd(x, g1)
    w_p = first_weight([hs[0], bias_exp] + [table for pair in tables for table in pair])
    gu, qkvs = _in_proj(hs, w_p, tables)
    os_, ls_ = [], []
    for g, dil in enumerate(DILATIONS):
        o, lse = _attn_fwd(qkvs[g], g, dil)
        os_.append(o)
        ls_.append(lse)
    attn = _combine_fwd(os_, ls_)
    sgu = _sgu_fwd(gu, ln_g, ln_b, w_s, bias_exp)
    w_pa, w_ps, w_out, w_g, w_u, w_d = late_weights(attn)
    pa, ps, merged, x1, h2 = _merge_fwd(attn, sgu, gu, x, w_pa, w_ps, w_out, g2)
    fa, fb, ff = _ffn_fwd(h2, w_g, w_u)
    dx2, dx2b, loss, dgf = _ffn_down_loss(ff, w_d, x1, tgt, gf)

    da, db = _ffn_bwd_act(dx2b, w_d, fa, fb)
    dw_d = _wgrad_ff_down(ff, dx2b)
    dx1, dx1b, dg2 = _ffn_bwd_in(da, db, w_g, w_u, x1, dx2, g2)
    dw_g = _wgrad_ff_in("wgrad_ffn_gate", h2, da)
    dw_u = _wgrad_ff_in("wgrad_ffn_up", h2, db)

    dgu, dpa, dps, dattn, dsgu = _merge_bwd(dx1b, gu, pa, ps, w_pa, w_ps, w_out)
    dw_out = _wgrad_2d("wgrad_out", merged, dx1b, D_MODEL, D_MODEL)
    dw_pa = _wgrad_2d("wgrad_proj_attn", attn, dpa, ATTN_W, D_MODEL)
    dw_ps = _wgrad_2d("wgrad_proj_sgu", sgu, dps, SGU_W, D_MODEL)
    if on_grads is not None:
        ln_g = ln_g + on_grads(1, dict(w_proj_attn=dw_pa, w_proj_sgu=dw_ps, w_out=dw_out, w_ffn_gate=dw_g, w_ffn_up=dw_u,
                                       w_ffn_down=dw_d))[:, :SGU_W]
    dgu, dw_s, dbias, dln_g, dln_b = _sgu_bwd(dgu, gu, dsgu, ln_g, ln_b, w_s, bias_exp)
    dos, ccs = _combine_bwd(dattn, os_, ls_)
    dqkvs = [_attn_bwd(qkvs[g], dos[g], ccs[g], ls_[g], *tables[g], g, dil) for g, dil in enumerate(DILATIONS)]
    dw_p = _wgrad_in(hs, dgu, dqkvs)
    if on_grads is not None:
        g1 = g1 + on_grads(0, dict(w_in=dw_p))
    dx, dg1 = _in_proj_bwd(dgu, dqkvs, w_p, x, dx1, g1)

    db_s = jnp.transpose(dbias[:, ::SGU_W // SGU_GROUPS])
    small = dict(loss=loss, norm1_g=dg1, sgu_ln_g=dln_g, sgu_ln_b=dln_b, w_spatial=dw_s, b_spatial=db_s,
                 norm2_g=dg2, final_g=dgf)
    big = dict(w_in=dw_p, w_proj_attn=dw_pa, w_proj_sgu=dw_ps, w_out=dw_out, w_ffn_gate=dw_g, w_ffn_up=dw_u,
               w_ffn_down=dw_d)
    return dx, big, small


def _ew(name, fn, ins, out_dtypes, after=()):
    shp = ins[0].shape
    rows, cols = shp
    tr = next((cand for cand in (256, 352, 128) if rows % cand == 0 and rows > cand), rows)

    def body(*refs):
        res = fn(*[r[...] for r in refs[:len(ins)]])
        for o_ref, v in zip(refs[len(ins) + len(after):], res):
            o_ref[...] = v.astype(o_ref.dtype)

    spec = pl.BlockSpec((tr, cols), lambda i: (i, 0))
    return pl.pallas_call(
        body, grid=(rows // tr,), in_specs=[spec] * len(ins) + [pl.BlockSpec(memory_space=pl.ANY)] * len(after),
        out_specs=[spec] * len(out_dtypes), out_shape=[jax.ShapeDtypeStruct(shp, d) for d in out_dtypes],
        compiler_params=_cparams(1), name=name)(*ins, *after)


def _adamw_math(g, w, m, v):
    m = ADAM_B1 * m + (1.0 - ADAM_B1) * g
    v = ADAM_B2 * v + (1.0 - ADAM_B2) * (g * g)
    m_hat = m / (1.0 - ADAM_B1 ** ADAM_STEP)
    v_hat = v / (1.0 - ADAM_B2 ** ADAM_STEP)
    delta = -ADAM_LR * (m_hat / (jnp.sqrt(v_hat) + ADAM_EPS) + ADAM_WD * w)
    return delta, m, v


def _adamw(name, g, w, m, v):
    return _ew(name, lambda g_, w_, m_, v_: (g_,) + _adamw_math(g_, w_, m_, v_), [g, w, m, v], [F32] * 4)


VMEM_SPEC = pl.BlockSpec(memory_space=pltpu.VMEM)


def _for_row_chunks(rows, fn):
    ck = next(c for c in (64, 32, 16) if rows % c == 0)

    def step(i, carry):
        fn(pl.multiple_of(i * ck, ck), ck)
        return carry

    lax.fori_loop(0, rows // ck, step, 0)


def _place():
    x, y, c = lax.axis_index("x"), lax.axis_index("y"), lax.axis_index("c")
    chips = [(1 - x, y), (x, 1 - y), (1 - x, 1 - y)]
    return x, y, c, 2 * x + y, chips


def _rows(ref, start, size):
    if len(ref.shape) == 2:
        return ref.at[pl.ds(start, size), :]
    return ref.at[:, pl.ds(start, size), :]


def _gather_finish(name, shard, landed):
    k_rows, n = shard.shape
    kh = k_rows // 2

    def body(shard_ref, land_ref, out_ref, loc, send, recv):
        x, y, c, me, chips = _place()
        sibling = (x, y, 1 - c)

        def window(core, chip):
            return out_ref.at[pl.ds(core * kh, kh), pl.ds(pl.multiple_of(chip * n, LANES), n)]

        copies = [pltpu.make_async_copy(shard_ref, out_ref.at[:, pl.ds(pl.multiple_of(me * n, LANES), n)], loc.at[3])]
        passed = []
        for j, chip in enumerate(chips):
            mine = window(c, 2 * chip[0] + chip[1])
            copies.append(pltpu.make_async_copy(land_ref.at[j], mine, loc.at[j]))
            passed.append(pltpu.make_async_remote_copy(src_ref=land_ref.at[j], dst_ref=mine, send_sem=send.at[j],
                                                       recv_sem=recv.at[j], device_id=sibling, device_id_type=MESH))
        for cp in copies + passed:
            cp.start()
        for j, chip in enumerate(chips):
            pltpu.make_async_remote_copy(src_ref=land_ref.at[j], dst_ref=window(1 - c, 2 * chip[0] + chip[1]), send_sem=send.at[j],
                                         recv_sem=recv.at[j], device_id=sibling, device_id_type=MESH).wait_recv()
        for cp in copies:
            cp.wait()
        for cp in passed:
            cp.wait_send()

    return pl.pallas_call(
        body, in_specs=[VMEM_SPEC] * 2, out_specs=pl.BlockSpec(memory_space=pl.ANY),
        out_shape=jax.ShapeDtypeStruct((k_rows, N_CHIPS * n), shard.dtype),
        scratch_shapes=[pltpu.SemaphoreType.DMA((4,)), pltpu.SemaphoreType.DMA((3,)), pltpu.SemaphoreType.DMA((3,))],
        compiler_params=pltpu.CompilerParams(vmem_limit_bytes=VMEM_LIMIT), name=name)(shard, landed)


HBM_SPEC = pl.BlockSpec(memory_space=pltpu.HBM)
SEM_SPEC = pl.BlockSpec(memory_space=pltpu.SEMAPHORE)
DATAFLOW = pltpu.SideEffectType.DATAFLOW_SIDE_EFFECTING
TOKEN_SHAPE = (1, D_MODEL)
N_PEERS = 7
SUM_SPLIT = 4
SUM_SPLIT_ELEMS = 512 * 1024


def _peers():
    x, y, c = lax.axis_index("x"), lax.axis_index("y"), lax.axis_index("c")
    flip = lambda v, f: 1 - v if f else v
    return [(flip(x, k & 4), flip(y, k & 2), flip(c, k & 1)) for k in range(1, N_PEERS + 1)]


def _piece_shape(shape):
    return (shape[-2] // 2, shape[2] if len(shape) == 3 else shape[1] // N_CHIPS)


def _device_piece(ref, chip, core):
    kh, n4 = _piece_shape(ref.shape)
    if len(ref.shape) == 3:
        return ref.at[chip, pl.ds(core * kh, kh), :]
    return ref.at[pl.ds(core * kh, kh), pl.ds(chip * n4, n4)]


def _exchange_copies(partials, lands, send, recv):
    return [pltpu.make_async_remote_copy(
        src_ref=_device_piece(partials[t], 2 * px + py, pc), dst_ref=lands[t].at[k], send_sem=send.at[t * N_PEERS + k],
        recv_sem=recv.at[t * N_PEERS + k], device_id=(px, py, pc), device_id_type=MESH)
        for t in range(len(partials)) for k, (px, py, pc) in enumerate(_peers())]


def _broadcast_copies(srcs, lands, send, recv):
    return [pltpu.make_async_remote_copy(
        src_ref=srcs[t], dst_ref=lands[t].at[k], send_sem=send.at[t * N_PEERS + k], recv_sem=recv.at[t * N_PEERS + k],
        device_id=peer, device_id_type=MESH)
        for t in range(len(srcs)) for k, peer in enumerate(_peers())]


class _LocalCopy:
    def __init__(self, src_ref, dst_ref, sem):
        self.copy = pltpu.make_async_copy(src_ref, dst_ref, sem)

    def start(self):
        self.copy.start()

    def wait_send(self):
        self.copy.wait()

    def wait_recv(self):
        pass


def _gather_copies(shards, lands, send, recv):
    x, y, c, me, chips = _place()
    copies = []
    for t in range(len(shards)):
        copies += [pltpu.make_async_remote_copy(
            src_ref=shards[t], dst_ref=lands[t].at[me], send_sem=send.at[t * 4 + j], recv_sem=recv.at[t * 4 + j],
            device_id=(*chip, c), device_id_type=MESH) for j, chip in enumerate(chips)]
        copies.append(_LocalCopy(shards[t], lands[t].at[me], send.at[t * 4 + 3]))
    return copies


def _gather_half_copies(shards, lands, send, recv):
    x, y, c, me, chips = _place()
    return [pltpu.make_async_remote_copy(
        src_ref=_rows(shards[t], c * (shards[t].shape[0] // 2), shards[t].shape[0] // 2), dst_ref=lands[t].at[j],
        send_sem=send.at[t * 3 + j], recv_sem=recv.at[t * 3 + j], device_id=(*chip, c), device_id_type=MESH)
        for t in range(len(shards)) for j, chip in enumerate(chips)]


def _split_start(name, copies, per_tensor, srcs, land_shapes):
    nt = len(srcs)
    lands = [lax.empty(s, a.dtype) for s, a in zip(land_shapes, srcs)]
    nsem = nt * per_tensor

    def body(*refs):
        send, recv = refs[2 * nt], refs[2 * nt + 1]
        for cp in copies(refs[:nt], refs[nt:2 * nt], send, recv):
            cp.start()
        refs[-1][...] = jnp.zeros(TOKEN_SHAPE, F32)

    hbm = lambda a: pltpu.with_memory_space_constraint(a, pltpu.HBM)
    outs = pl.pallas_call(
        body, name=name,
        out_shape=[pltpu.SemaphoreType.DMA((nsem,)), pltpu.SemaphoreType.DMA((nsem,))]
        + [pltpu.HBM(s.shape, s.dtype) for s in srcs] + [pltpu.HBM(l.shape, l.dtype) for l in lands]
        + [jax.ShapeDtypeStruct(TOKEN_SHAPE, F32)],
        in_specs=[HBM_SPEC] * (2 * nt), out_specs=[SEM_SPEC, SEM_SPEC] + [HBM_SPEC] * (2 * nt) + [VMEM_SPEC],
        input_output_aliases={i: 2 + i for i in range(2 * nt)},
        compiler_params=pltpu.CompilerParams(has_side_effects=DATAFLOW))(*[hbm(a) for a in list(srcs) + lands])
    return outs[0], outs[1], outs[2:2 + nt], outs[2 + nt:2 + 2 * nt], outs[-1]


def _split_wait(name, copies, send, recv, srcs, lands, after):
    nt = len(srcs)
    after = list(after) if isinstance(after, (list, tuple)) else [after]

    def body(*refs):
        for cp in copies(refs[:nt], refs[nt:2 * nt], refs[2 * nt], refs[2 * nt + 1]):
            cp.wait_send()
            cp.wait_recv()

    outs = pl.pallas_call(
        body, name=name,
        out_shape=[pltpu.HBM(s.shape, s.dtype) for s in srcs] + [pltpu.HBM(l.shape, l.dtype) for l in lands],
        in_specs=[HBM_SPEC] * (2 * nt) + [SEM_SPEC, SEM_SPEC] + [pl.BlockSpec(memory_space=pl.ANY)] * len(after),
        out_specs=[HBM_SPEC] * (2 * nt), input_output_aliases={i: i for i in range(2 * nt)},
        compiler_params=pltpu.CompilerParams(has_side_effects=DATAFLOW))(*srcs, *lands, send, recv, *after)
    return outs[:nt], outs[nt:]


def _device_sum(name, partials, lands):
    nt = len(partials)
    pieces = [_piece_shape(p.shape) for p in partials]
    units = []
    for t, (kh, n4) in enumerate(pieces):
        split = SUM_SPLIT if kh * n4 >= SUM_SPLIT_ELEMS else 1
        units += [(t, j * (kh // split), kh // split) for j in range(split)]
    nu = len(units)

    def body(*refs):
        ins, slots, outs = refs[:nt], refs[nt:2 * nt], refs[2 * nt:3 * nt]
        owns, landed, sums = refs[3 * nt:4 * nt], refs[4 * nt:5 * nt], refs[5 * nt:6 * nt]
        loc, send, recv = refs[6 * nt:]
        x, y, c, me, chips = _place()
        sibling = (x, y, 1 - c)
        loads = []
        for u, (t, r0, rows) in enumerate(units):
            loads.append((
                pltpu.make_async_copy(_rows(_device_piece(ins[t], me, c), r0, rows), _rows(owns[t], r0, rows), loc.at[0, u]),
                pltpu.make_async_copy(_rows(slots[t], r0, rows), _rows(landed[t], r0, rows), loc.at[1, u])))
            for cp in loads[-1]:
                cp.start()
        stores = []
        for u, (t, r0, rows) in enumerate(units):
            for cp in loads[u]:
                cp.wait()

            def add(q0, ck, own=owns[t], slot=landed[t], dst=sums[t], r0=r0):
                at = pl.ds(pl.multiple_of(r0 + q0, ck), ck)
                acc = own[at, :].astype(F32)
                for k in range(N_PEERS):
                    acc = acc + slot[k, at, :].astype(F32)
                dst[at, :] = acc

            _for_row_chunks(rows, add)
            mine = _rows(outs[t], c * pieces[t][0] + r0, rows)
            stores.append((
                pltpu.make_async_copy(_rows(sums[t], r0, rows), mine, loc.at[2, u]),
                pltpu.make_async_remote_copy(src_ref=_rows(sums[t], r0, rows), dst_ref=mine, send_sem=send.at[u],
                                             recv_sem=recv.at[u], device_id=sibling, device_id_type=MESH)))
            for cp in stores[-1]:
                cp.start()
        for u, (t, r0, rows) in enumerate(units):
            pltpu.make_async_remote_copy(
                src_ref=_rows(sums[t], r0, rows), dst_ref=_rows(outs[t], (1 - c) * pieces[t][0] + r0, rows),
                send_sem=send.at[u], recv_sem=recv.at[u], device_id=sibling, device_id_type=MESH).wait_recv()
            stores[u][0].wait()
            stores[u][1].wait_send()

    any_spec = pl.BlockSpec(memory_space=pl.ANY)
    return pl.pallas_call(
        body, in_specs=[any_spec] * (2 * nt), out_specs=[any_spec] * nt,
        out_shape=[jax.ShapeDtypeStruct((2 * kh, n4), F32) for kh, n4 in pieces],
        scratch_shapes=[pltpu.VMEM(p, BF16) for p in pieces] + [pltpu.VMEM((N_PEERS,) + p, BF16) for p in pieces]
        + [pltpu.VMEM(p, F32) for p in pieces]
        + [pltpu.SemaphoreType.DMA((3, nu)), pltpu.SemaphoreType.DMA((nu,)), pltpu.SemaphoreType.DMA((nu,))],
        compiler_params=pltpu.CompilerParams(vmem_limit_bytes=VMEM_LIMIT), name=name)(*partials, *lands)


VEC_SHAPE = (8, D_MODEL + LANES)
VEC_SLOTS = dict(norm1_g=(slice(0, 1), slice(0, D_MODEL)), norm2_g=(slice(1, 2), slice(0, D_MODEL)),
                 final_g=(slice(2, 3), slice(0, D_MODEL)), sgu_ln_g=(slice(3, 4), slice(0, SGU_W)),
                 sgu_ln_b=(slice(3, 4), slice(SGU_W, 2 * SGU_W)), b_spatial=(slice(0, 8), slice(D_MODEL, D_MODEL + LANES)),
                 loss=(slice(4, 5), slice(0, LANES)))
VEC_PARAMS = ("norm1_g", "norm2_g", "final_g", "sgu_ln_g", "sgu_ln_b", "b_spatial")
SMALL_PARAMS = VEC_PARAMS + ("w_spatial",)
W_SPATIAL_2D = (SGU_GROUPS * SGU_CHUNK, SGU_CHUNK)


SMALL_GRADS = VEC_PARAMS + ("loss", "w_spatial")


def _small_shape(name):
    if name == "w_spatial":
        return W_SPATIAL_2D
    rows, cols = VEC_SLOTS[name]
    return (rows.stop - rows.start, cols.stop - cols.start)


def _pack_small(dst, parts):
    dst[...] = jnp.zeros(VEC_SHAPE, F32)
    for n, ref in parts.items():
        if n in VEC_SLOTS:
            dst[VEC_SLOTS[n]] = ref[...]


def _small_start(partials):
    names = VEC_PARAMS + ("loss",)

    def body(*refs):
        _pack_small(refs[-1], dict(zip(names, refs[:-1])))

    vec = pl.pallas_call(
        body, in_specs=[VMEM_SPEC] * len(names), out_specs=VMEM_SPEC, out_shape=jax.ShapeDtypeStruct(VEC_SHAPE, F32),
        name="small_params_pack")(*[partials[n].reshape(_small_shape(n)) for n in names])
    srcs = [vec, partials["w_spatial"].reshape(W_SPATIAL_2D)]
    return _split_start("small_params_start", _broadcast_copies, N_PEERS, srcs, [(N_PEERS,) + s.shape for s in srcs])


def _small_finish(started, after, w, m, v):
    own, landed = _split_wait("small_params_wait", _broadcast_copies, *started, after)
    ng, npar = len(SMALL_GRADS), len(SMALL_PARAMS)

    def update_body(*refs):
        vec_own, ws_own, vec_slots, ws_slots = refs[:4]
        w_in, m_in, v_in = (dict(zip(SMALL_PARAMS, refs[4 + k * npar:4 + (k + 1) * npar])) for k in range(3))
        o0 = 4 + 3 * npar
        g_out = dict(zip(SMALL_GRADS, refs[o0:o0 + ng]))
        d_out, m_out, v_out = (dict(zip(SMALL_PARAMS, refs[o0 + ng + k * npar:o0 + ng + (k + 1) * npar])) for k in range(3))
        vg, vw, vm, vv = refs[o0 + ng + 3 * npar:]
        me = 4 * lax.axis_index("x") + 2 * lax.axis_index("y") + lax.axis_index("c")

        def device_sum(mine, slots, read):
            acc = None
            for i in range(N_PEERS + 1):
                k = me ^ i
                part = jnp.where(k == 0, read(mine), read(slots.at[jnp.maximum(k, 1) - 1]))
                acc = part if acc is None else acc + part
            return acc

        vg[...] = device_sum(vec_own, vec_slots, lambda ref: ref[...])
        _pack_small(vw, w_in)
        _pack_small(vm, m_in)
        _pack_small(vv, v_in)
        d_vec, m_vec, v_vec = _adamw_math(vg[...], vw[...], vm[...], vv[...])
        vw[...] = d_vec
        vm[...] = m_vec
        vv[...] = v_vec
        for n in VEC_PARAMS + ("loss",):
            g_out[n][...] = vg[VEC_SLOTS[n]]
        for n in VEC_PARAMS:
            d_out[n][...] = vw[VEC_SLOTS[n]]
            m_out[n][...] = vm[VEC_SLOTS[n]]
            v_out[n][...] = vv[VEC_SLOTS[n]]

        def spatial(r0, ck):
            rows = pl.ds(r0, ck)
            g = device_sum(ws_own, ws_slots, lambda ref: ref[rows, :])
            d_, m_, v_ = _adamw_math(g, w_in["w_spatial"][rows, :], m_in["w_spatial"][rows, :], v_in["w_spatial"][rows, :])
            g_out["w_spatial"][rows, :] = g
            d_out["w_spatial"][rows, :] = d_
            m_out["w_spatial"][rows, :] = m_
            v_out["w_spatial"][rows, :] = v_

        _for_row_chunks(W_SPATIAL_2D[0], spatial)

    ins = list(own) + list(landed)
    for src in (w, m, v):
        ins += [src[n].reshape(_small_shape(n)) for n in SMALL_PARAMS]
    out_shapes = [jax.ShapeDtypeStruct(_small_shape(n), F32) for n in SMALL_GRADS + SMALL_PARAMS * 3]
    outs = pl.pallas_call(
        update_body, in_specs=[VMEM_SPEC] * len(ins), out_specs=[VMEM_SPEC] * len(out_shapes), out_shape=out_shapes,
        scratch_shapes=[pltpu.VMEM(VEC_SHAPE, F32)] * 4, name="small_params_update")(*ins)
    grads = dict(zip(SMALL_GRADS, outs[:ng]))
    rest = [dict(zip(SMALL_PARAMS, outs[ng + k * npar:ng + (k + 1) * npar])) for k in range(3)]
    return grads, rest[0], rest[1], rest[2]


BIG = ("w_in", "w_proj_attn", "w_proj_sgu", "w_out", "w_ffn_gate", "w_ffn_up", "w_ffn_down")
COMM_GROUPS = (("w_in",), ("w_proj_attn", "w_proj_sgu", "w_out", "w_ffn_gate", "w_ffn_up", "w_ffn_down"))
WEIGHTS = ("norm1_g", "w_in", "sgu_ln_g", "sgu_ln_b", "w_spatial", "b_spatial", "w_proj_attn", "w_proj_sgu", "w_out",
           "norm2_g", "w_ffn_gate", "w_ffn_up", "w_ffn_down", "final_g")


def _cols_from_chips(g):
    return jnp.transpose(g, (1, 0, 2)).reshape(g.shape[1], N_CHIPS * g.shape[2])


def kernel(x, positions, norm1_g, w_in, sgu_ln_g, sgu_ln_b, w_spatial, b_spatial, w_proj_attn, w_proj_sgu, w_out, norm2_g, w_ffn_gate, w_ffn_up, w_ffn_down, final_g, loss_target, m_norm1_g, m_w_in, m_sgu_ln_g, m_sgu_ln_b, m_w_spatial, m_b_spatial, m_w_proj_attn, m_w_proj_sgu, m_w_out, m_norm2_g, m_w_ffn_gate, m_w_ffn_up, m_w_ffn_down, m_final_g, v_norm1_g, v_w_in, v_sgu_ln_g, v_sgu_ln_b, v_w_spatial, v_b_spatial, v_w_proj_attn, v_w_proj_sgu, v_w_out, v_norm2_g, v_w_ffn_gate, v_w_ffn_up, v_w_ffn_down, v_final_g):
    w = dict(norm1_g=norm1_g, w_in=w_in, sgu_ln_g=sgu_ln_g, sgu_ln_b=sgu_ln_b, w_spatial=w_spatial, b_spatial=b_spatial,
             w_proj_attn=w_proj_attn, w_proj_sgu=w_proj_sgu, w_out=w_out, norm2_g=norm2_g, w_ffn_gate=w_ffn_gate,
             w_ffn_up=w_ffn_up, w_ffn_down=w_ffn_down, final_g=final_g)
    m = dict(norm1_g=m_norm1_g, w_in=m_w_in, sgu_ln_g=m_sgu_ln_g, sgu_ln_b=m_sgu_ln_b, w_spatial=m_w_spatial,
             b_spatial=m_b_spatial, w_proj_attn=m_w_proj_attn, w_proj_sgu=m_w_proj_sgu, w_out=m_w_out, norm2_g=m_norm2_g,
             w_ffn_gate=m_w_ffn_gate, w_ffn_up=m_w_ffn_up, w_ffn_down=m_w_ffn_down, final_g=m_final_g)
    v = dict(norm1_g=v_norm1_g, w_in=v_w_in, sgu_ln_g=v_sgu_ln_g, sgu_ln_b=v_sgu_ln_b, w_spatial=v_w_spatial,
             b_spatial=v_b_spatial, w_proj_attn=v_w_proj_attn, w_proj_sgu=v_w_proj_sgu, w_out=v_w_out, norm2_g=v_norm2_g,
             w_ffn_gate=v_w_ffn_gate, w_ffn_up=v_w_ffn_up, w_ffn_down=v_w_ffn_down, final_g=v_final_g)
    t = x.shape[1]

    cast = lambda n, after: _ew(f"cast_{n}", lambda a: (a,), [w[n][0]], [BF16], after)[0]
    shards = {"w_in": cast("w_in", [])}
    late = COMM_GROUPS[1]
    k_in, n_in = shards["w_in"].shape
    *first, token = _split_start("gather_start_0", _gather_half_copies, 3, [shards["w_in"]], [(3, k_in // 2, n_in)])
    shards.update({n: cast(n, [token]) for n in late})
    pending = {}

    def first_weight(after):
        srcs, filled = _split_wait("gather_wait_0", _gather_half_copies, *first, list(after) + [shards[n] for n in late])
        gath_in, late_shards = lax.optimization_barrier(
            (_gather_finish("gather_finish_0", srcs[0], filled[0]), [shards[n] for n in late]))
        *pending["late"], _ = _split_start(
            "gather_start_1", _gather_copies, 4, late_shards, [(N_CHIPS,) + s.shape for s in late_shards])
        return gath_in

    def late_weights(after):
        _, filled = _split_wait("gather_wait_1", _gather_copies, *pending["late"], after)
        gath = dict(zip(late, filled))
        return (_cols_from_chips(gath["w_proj_attn"]), _cols_from_chips(gath["w_proj_sgu"]),
                gath["w_out"].reshape(D_MODEL, D_MODEL), gath["w_ffn_gate"], gath["w_ffn_up"], gath["w_ffn_down"])

    exchanges = {}

    def on_grads(i, partials):
        if "w_out" in partials:
            partials["w_out"] = partials["w_out"].reshape(N_CHIPS, D_MODEL // N_CHIPS, D_MODEL)
        parts = [partials[n] for n in COMM_GROUPS[i]]
        *exchanges[i], started = _split_start(
            f"rs_exchange_start_{i}", _exchange_copies, N_PEERS, parts, [(N_PEERS,) + _piece_shape(p.shape) for p in parts])
        return started

    dx, _, small = _local_step(
        x[0], positions.reshape(t, 1), loss_target[0], norm1_g + token, sgu_ln_g, sgu_ln_b,
        w_spatial[0], b_spatial[0] + token[:1, :LANES],
        norm2_g, final_g.reshape(1, D_MODEL), first_weight, late_weights, on_grads=on_grads)
    *small_started, small_token = _small_start(small)

    grads = {}
    for i in (1, 0):
        parts, filled = _split_wait(f"rs_exchange_wait_{i}", _exchange_copies, *exchanges[i], small_token)
        grads.update(zip(COMM_GROUPS[i], _device_sum(f"rs_device_sum_{i}", parts, filled)))

    delta, new_m, new_v, updated = {}, {}, {}, []
    for n in BIG:
        shp = w[n].shape
        flip = jnp.transpose if shp[-1] % LANES else (lambda a: a)
        outs = _adamw(f"adamw_{n}", flip(grads[n]), flip(w[n][0]), flip(m[n][0]), flip(v[n][0]))
        grads[n], delta[n], new_m[n], new_v[n] = (flip(a).reshape(shp) for a in outs)
        updated.append(outs[-1])

    g_s, d_s, m_s, v_s = _small_finish(small_started, updated, w, m, v)
    loss = g_s["loss"][0, 0]
    for n in SMALL_PARAMS:
        shp = w[n].shape
        grads[n], delta[n], new_m[n], new_v[n] = (a[n].reshape(shp) for a in (g_s, d_s, m_s, v_s))

    return (loss, dx.reshape(x.shape), *[grads[n] for n in WEIGHTS], *[delta[n] for n in WEIGHTS],
            *[new_m[n] for n in WEIGHTS], *[new_v[n] for n in WEIGHTS])
```

```python
import functools

import numpy as np
import jax
import jax.numpy as jnp
from jax import lax
from jax.experimental import pallas as pl
from jax.experimental.pallas import tpu as pltpu

F32, BF16 = jnp.float32, jnp.bfloat16
MESH = pl.DeviceIdType.MESH

D_MODEL = 1024
HEAD_DIM = 64
ATTN_W = 512
DILATIONS = (1, 4, 16)
BLK = 128
ATTN_BLOCKS_PER_STEP = 4
ROPE_DIM = 16
ROPE_THETA = 500000.0
SGU_W = 512
SGU_CHUNK = 128
SGU_GROUPS = 8
D_FF = 2816
N_CHIPS = 4
FF_SHARD = D_FF // N_CHIPS
IN_COLS = 7680
EPS = 1e-6
NEG = -1e30
LANES = 128
VMEM_LIMIT = 52 * 1024 * 1024

ADAM_LR, ADAM_B1, ADAM_B2, ADAM_EPS, ADAM_WD, ADAM_STEP = 0.001, 0.9, 0.999, 1e-08, 0.01, 10

QKV_BLOCKS = 9


def _w_in_block(part, g):
    return part * len(DILATIONS) + g


def _cparams(ngrid):
    return pltpu.CompilerParams(dimension_semantics=("arbitrary",) * ngrid, vmem_limit_bytes=VMEM_LIMIT)


def _full(shape):
    return pl.BlockSpec(shape, lambda *_: (0,) * len(shape))


def _resident(shape):
    return pl.BlockSpec(shape, lambda *_: (0,) * len(shape), pipeline_mode=pl.Buffered(1))


NT = ((1,), (1,))
TN = ((0,), (0,))


def _rope(v, cos_t, sin_t):
    half = ROPE_DIM // 2
    first = (lax.broadcasted_iota(jnp.int32, cos_t.shape, 1) % HEAD_DIM) < half
    outs = []
    for cs in range(v.shape[1] // LANES):
        x = v[:, cs * LANES:(cs + 1) * LANES]
        partner = jnp.where(first, pltpu.roll(x, LANES - half, axis=1), pltpu.roll(x, half, axis=1))
        outs.append(x * cos_t + partner * sin_t)
    return outs[0] if len(outs) == 1 else jnp.concatenate(outs, axis=1)


def _spread_heads(v2, upper):
    other = pltpu.roll(v2, HEAD_DIM, axis=1)
    h0 = jnp.where(upper, other, v2)
    h1 = jnp.where(upper, v2, other)
    return jnp.concatenate([jnp.concatenate([h0, h0], axis=1), jnp.concatenate([h1, h1], axis=1)], axis=0)


def _sigmoid(v):
    return 0.5 * jnp.tanh(0.5 * v) + 0.5


def _rms_stats(v):
    r = lax.rsqrt(jnp.mean(v * v, axis=-1, keepdims=True) + EPS)
    return v * r, r


def _rms_bwd(dy, xhat, r, g):
    dxh = dy * g
    return r * (dxh - xhat * jnp.mean(dxh * xhat, axis=-1, keepdims=True))


def _head_sum_matrix():
    idx = np.arange(ATTN_W) // HEAD_DIM
    return jnp.asarray((idx[:, None] == idx[None, :]).astype(np.float32), dtype=BF16)


def _group_sum(v, e):
    hi = v.astype(BF16)
    lo = (v - hi.astype(F32)).astype(BF16)
    return jnp.dot(hi, e, preferred_element_type=F32) + jnp.dot(lo, e, preferred_element_type=F32)


TILE = 512


def _to_slabs(slab_ref, v):
    for cs in range(slab_ref.shape[0]):
        slab_ref[cs] = v[:, cs * LANES:(cs + 1) * LANES]


def _from_slabs(slab_ref):
    return jnp.concatenate([slab_ref[cs] for cs in range(slab_ref.shape[0])], axis=1)


def _class_rows(slab_ref, r, dil):
    n = slab_ref.shape[1] // dil
    return jnp.concatenate([slab_ref.at[cs][pl.ds(r, n, stride=dil), :] for cs in range(slab_ref.shape[0])], axis=1)


def _put_class_rows(slab_ref, r, dil, v):
    n = slab_ref.shape[1] // dil
    for cs in range(slab_ref.shape[0]):
        slab_ref.at[cs][pl.ds(r, n, stride=dil), :] = v[:, cs * LANES:(cs + 1) * LANES]


def _natural_from_group(slab_ref, grp_ref):
    dil = grp_ref.shape[0]
    for r in range(dil):
        _put_class_rows(slab_ref, r, dil, grp_ref[r].astype(F32))
    return _from_slabs(slab_ref)


def _group_from_natural(slab_ref, grp_ref, v):
    dil = grp_ref.shape[0]
    _to_slabs(slab_ref, v)
    for r in range(dil):
        grp_ref[r] = _class_rows(slab_ref, r, dil).astype(grp_ref.dtype)


def _group_spec(dil, tile, width):
    return pl.BlockSpec((dil, tile // dil, width), lambda i, *_: (0, i, 0))


def _slabs(tile, width):
    return pltpu.VMEM((width // LANES, tile, LANES), F32)


def _rope_consts():
    lane = np.arange(LANES) % HEAD_DIM
    fi = lane % (ROPE_DIM // 2)
    invf = np.where(lane < ROPE_DIM, ROPE_THETA ** (-(2.0 * fi) / ROPE_DIM), 0.0)
    sgn = np.where(lane < ROPE_DIM // 2, -1.0, np.where(lane < ROPE_DIM, 1.0, 0.0))
    return (jnp.asarray(invf.astype(np.float32)).reshape(1, LANES), jnp.asarray(sgn.astype(np.float32)).reshape(1, LANES))


def _rope_tables(pos_col):
    t = pos_col.shape[0]
    tile = min(t, TILE)
    invf, sgn = _rope_consts()

    def body(p_ref, f_ref, s_ref, c0, s0, c1, s1, c2, s2, slab_c, slab_s):
        ang = p_ref[...].astype(F32) * f_ref[...]
        cos, sin = jnp.cos(ang), jnp.sin(ang) * s_ref[...]
        c0[...] = cos
        s0[...] = sin
        _group_from_natural(slab_c, c1, cos)
        _group_from_natural(slab_s, s1, sin)
        for r in range(DILATIONS[2]):
            c2[r] = _class_rows(slab_c, r, DILATIONS[2])
            s2[r] = _class_rows(slab_s, r, DILATIONS[2])

    nat = pl.BlockSpec((tile, LANES), lambda i: (i, 0))
    specs, shapes = [nat, nat], [(t, LANES)] * 2
    for d in DILATIONS[1:]:
        specs += [_group_spec(d, tile, LANES)] * 2
        shapes += [(d, t // d, LANES)] * 2
    outs = pl.pallas_call(
        body, grid=(t // tile,),
        in_specs=[pl.BlockSpec((tile, 1), lambda i: (i, 0)), _full((1, LANES)), _full((1, LANES))],
        out_specs=specs, out_shape=[jax.ShapeDtypeStruct(s, F32) for s in shapes],
        scratch_shapes=[_slabs(tile, LANES)] * 2,
        compiler_params=_cparams(1), name="rope_tables")(pos_col, invf, sgn)
    return [(outs[2 * g].reshape(t, LANES), outs[2 * g + 1].reshape(t, LANES)) for g in range(len(DILATIONS))]


def _norm_fwd(x, g):
    t = x.shape[0]
    tile = min(t, TILE)

    def body(x_ref, g_ref, h0_ref, h1_ref, h2_ref, slab):
        xhat, _ = _rms_stats(x_ref[...])
        hn = xhat * g_ref[...]
        h0_ref[...] = hn.astype(BF16)
        _group_from_natural(slab, h1_ref, hn)
        for r in range(DILATIONS[2]):
            h2_ref[r] = _class_rows(slab, r, DILATIONS[2]).astype(BF16)

    nat = pl.BlockSpec((tile, D_MODEL), lambda i: (i, 0))
    return pl.pallas_call(
        body, grid=(t // tile,),
        in_specs=[nat, _full((1, D_MODEL))],
        out_specs=[nat] + [_group_spec(d, tile, D_MODEL) for d in DILATIONS[1:]],
        out_shape=[jax.ShapeDtypeStruct((t, D_MODEL), BF16)]
        + [jax.ShapeDtypeStruct((d, t // d, D_MODEL), BF16) for d in DILATIONS[1:]],
        scratch_shapes=[_slabs(tile, D_MODEL)],
        compiler_params=_cparams(1), name="norm1_fwd")(x, g)


GU_COLS = 3072
GROUP_COLS = 1536
GU_HALF = GU_COLS // 2


def _w_in_spec(width, block):
    return pl.BlockSpec((D_MODEL, width), lambda i: (0, block), pipeline_mode=pl.Buffered(1))


def _gu_w_specs():
    first = QKV_BLOCKS * ATTN_W // GU_HALF
    return [_w_in_spec(GU_HALF, first), _w_in_spec(GU_HALF, first + 1)]


def _group_w_specs(g):
    return [_w_in_spec(ATTN_W, _w_in_block(part, g)) for part in range(3)]


def _in_proj(hs, w_in, tables):
    t = hs[0].shape[0]
    tm = min(t, 1024)

    def body_gu(h_ref, w0_ref, w1_ref, o_ref):
        h = h_ref[...]
        o_ref[:, 0:GU_HALF] = jnp.dot(h, w0_ref[...], preferred_element_type=F32).astype(BF16)
        o_ref[:, GU_HALF:] = jnp.dot(h, w1_ref[...], preferred_element_type=F32).astype(BF16)

    gu = _token_call("in_proj_gates_uv", body_gu, t, tm,
                     [(hs[0], _rows_spec(tm, D_MODEL))] + [(w_in, s) for s in _gu_w_specs()],
                     [((t, GU_COLS), BF16, _rows_spec(tm, GU_COLS))])[0]

    qkvs = []
    for g in range(len(DILATIONS)):

        def body_qkv(h_ref, wq_ref, wk_ref, wv_ref, cos_ref, sin_ref, o_ref):
            h = h_ref[...]
            cos_w, sin_w = cos_ref[...], sin_ref[...]
            q = jnp.dot(h, wq_ref[...], preferred_element_type=F32)
            o_ref[:, 0:ATTN_W] = (_rope(q, cos_w, sin_w) * HEAD_DIM ** -0.5).astype(BF16)
            k = jnp.dot(h, wk_ref[...], preferred_element_type=F32)
            o_ref[:, ATTN_W:2 * ATTN_W] = _rope(k, cos_w, sin_w).astype(BF16)
            o_ref[:, 2 * ATTN_W:] = jnp.dot(h, wv_ref[...], preferred_element_type=F32).astype(BF16)

        cos_t, sin_t = tables[g]
        qkvs.append(_token_call(
            f"in_proj_qkv_g{g}", body_qkv, t, tm,
            [(hs[g].reshape(t, D_MODEL), _rows_spec(tm, D_MODEL))] + [(w_in, s) for s in _group_w_specs(g)]
            + [(cos_t, _rows_spec(tm, LANES)), (sin_t, _rows_spec(tm, LANES))],
            [((t, GROUP_COLS), BF16, _rows_spec(tm, GROUP_COLS))])[0])
    return gu, qkvs


def _attn_masks(n):
    row = lax.broadcasted_iota(jnp.int32, (2 * BLK, 2 * BLK), 0) % BLK
    col = lax.broadcasted_iota(jnp.int32, (2 * BLK, 2 * BLK), 1)
    diff = BLK + row - col
    valid = (diff >= 0) & (diff <= BLK) & ((col >= BLK) | (n > 0))
    upper = lax.broadcasted_iota(jnp.int32, (BLK, LANES), 1) >= HEAD_DIM
    return valid, upper


def _stack_heads(v2, upper):
    zero = jnp.zeros_like(v2)
    return jnp.concatenate([jnp.where(upper, zero, v2), jnp.where(upper, v2, zero)], axis=0)


def _unstack_heads(v, upper):
    return jnp.where(upper, v[BLK:], v[:BLK])


def _attn_fwd(qkv, g, dil):
    t = qkv.shape[0]
    length = t // dil
    nb = length // BLK
    per_step = min(nb, ATTN_BLOCKS_PER_STEP)
    view = qkv.reshape(dil, length, GROUP_COLS)

    def body(q_ref, kc_ref, kp_ref, vc_ref, vp_ref, o_ref, l_ref, kwin, vwin):
        n = pl.program_id(1)
        kwin[0:BLK] = kp_ref[...]
        kwin[BLK:] = kc_ref[...]
        vwin[0:BLK] = vp_ref[...]
        vwin[BLK:] = vc_ref[...]

        def block(b, carry):
            valid, upper = _attn_masks(n * per_step + b)
            rows = pl.ds(pl.multiple_of(b * BLK, BLK), BLK)
            window = pl.ds(pl.multiple_of(b * BLK, BLK), 2 * BLK)
            slabs = [slice(p * LANES, (p + 1) * LANES) for p in range(ATTN_W // LANES)]
            ss = [lax.dot_general(_stack_heads(q_ref[rows, sl], upper), kwin[window, sl], (NT, ((), ())),
                                  preferred_element_type=F32) for sl in slabs]
            soft = []
            for s in ss:
                s = jnp.where(valid, s, NEG)
                m = jnp.max(s, axis=1, keepdims=True)
                pe = jnp.exp(s - m)
                soft.append((m, pe, jnp.sum(pe, axis=1, keepdims=True)))
            for sl, (m, pe, den) in zip(slabs, soft):
                o = jnp.dot(pe.astype(BF16), vwin[window, sl], preferred_element_type=F32) / den
                lse = jnp.broadcast_to(m + jnp.log(den), (2 * BLK, LANES))
                o_ref[rows, sl] = _unstack_heads(o, upper).astype(BF16)
                l_ref[rows, sl] = _unstack_heads(lse, upper)
            return carry

        lax.fori_loop(0, per_step, block, 0)

    rows = per_step * BLK
    cur = lambda part: pl.BlockSpec((None, rows, ATTN_W), lambda r, n: (r, n, part))
    prev = lambda part: pl.BlockSpec((None, BLK, ATTN_W), lambda r, n: (r, jnp.maximum(n * per_step - 1, 0), part))
    out_spec = pl.BlockSpec((None, rows, ATTN_W), lambda r, n: (r, n, 0))
    return pl.pallas_call(
        body, grid=(dil, nb // per_step),
        in_specs=[cur(0), cur(1), prev(1), cur(2), prev(2)],
        out_specs=[out_spec, out_spec],
        out_shape=[jax.ShapeDtypeStruct((dil, length, ATTN_W), BF16), jax.ShapeDtypeStruct((dil, length, ATTN_W), F32)],
        scratch_shapes=[pltpu.VMEM((rows + BLK, ATTN_W), BF16)] * 2,
        compiler_params=_cparams(2), name=f"attn_fwd_g{g}")(view, view, view, view, view)


def _alphas(l0, l1, l2):
    m = jnp.maximum(jnp.maximum(l0, l1), l2)
    e0, e1, e2 = jnp.exp(l0 - m), jnp.exp(l1 - m), jnp.exp(l2 - m)
    inv = 1.0 / (e0 + e1 + e2)
    return e0 * inv, e1 * inv, e2 * inv


def _natural_group_values(o_refs, l_refs, slabs):
    os_ = [o_refs[0][0].astype(F32)] + [_natural_from_group(slabs[2 * g - 2], o_refs[g]) for g in (1, 2)]
    ls_ = [l_refs[0][0]] + [_natural_from_group(slabs[2 * g - 1], l_refs[g]) for g in (1, 2)]
    return os_, ls_


def _combine_fwd(os_, ls_):
    t = os_[0].shape[1]
    tile = min(t, TILE)

    def body(o0, o1, o2, l0, l1, l2, a_ref, *slabs):
        ov, lv = _natural_group_values((o0, o1, o2), (l0, l1, l2), slabs)
        a0, a1, a2 = _alphas(*lv)
        a_ref[...] = (a0 * ov[0] + a1 * ov[1] + a2 * ov[2]).astype(BF16)

    specs = [_group_spec(d, tile, ATTN_W) for d in DILATIONS]
    return pl.pallas_call(
        body, grid=(t // tile,), in_specs=specs * 2, out_specs=pl.BlockSpec((tile, ATTN_W), lambda i: (i, 0)),
        out_shape=jax.ShapeDtypeStruct((t, ATTN_W), BF16),
        scratch_shapes=[_slabs(tile, ATTN_W)] * 4,
        compiler_params=_cparams(1), name="combine_fwd")(*os_, *ls_)


def _combine_bwd(dattn, os_, ls_):
    t = dattn.shape[0]
    tile = min(t, TILE)
    e = _head_sum_matrix()

    def body(d_ref, o0, o1, o2, l0, l1, l2, e_ref, do0, do1, do2, c0, c1, c2, *slabs):
        ov, lv = _natural_group_values((o0, o1, o2), (l0, l1, l2), slabs)
        alphas = _alphas(*lv)
        d = d_ref[...]
        attn = alphas[0] * ov[0] + alphas[1] * ov[1] + alphas[2] * ov[2]
        s = _group_sum(d * attn, e_ref[...])
        do0[0] = (alphas[0] * d).astype(BF16)
        c0[0] = -alphas[0] * s
        for g, do_ref, c_ref in ((1, do1, c1), (2, do2, c2)):
            _group_from_natural(slabs[2 * g - 2], do_ref, alphas[g] * d)
            _group_from_natural(slabs[2 * g - 1], c_ref, -alphas[g] * s)

    specs = [_group_spec(d, tile, ATTN_W) for d in DILATIONS]
    shapes = [(d, t // d, ATTN_W) for d in DILATIONS]
    outs = pl.pallas_call(
        body, grid=(t // tile,),
        in_specs=[pl.BlockSpec((tile, ATTN_W), lambda i: (i, 0))] + specs * 2 + [_full((ATTN_W, ATTN_W))],
        out_specs=specs * 2,
        out_shape=[jax.ShapeDtypeStruct(s, BF16) for s in shapes] + [jax.ShapeDtypeStruct(s, F32) for s in shapes],
        scratch_shapes=[_slabs(tile, ATTN_W)] * 4,
        compiler_params=_cparams(1), name="combine_bwd")(dattn, *os_, *ls_, e)
    return outs[:3], outs[3:]


def _attn_bwd(qkv, do, cc, lse, cos_t, sin_t, g, dil):
    t = qkv.shape[0]
    length = t // dil
    nb = length // BLK
    per_step = min(nb, ATTN_BLOCKS_PER_STEP)
    nsteps = nb // per_step
    rows_per_step = per_step * BLK
    qkv_v = qkv.reshape(dil, length, GROUP_COLS)
    cos_v, sin_v = (a.reshape(dil, length, LANES) for a in (cos_t, sin_t))
    scale = HEAD_DIM ** -0.5
    dq_cols, dk_cols, dv_cols = (slice(i * ATTN_W, (i + 1) * ATTN_W) for i in range(3))

    def body(q_ref, kc_ref, kp_ref, vc_ref, vp_ref, do_ref, c_ref, l_ref, cosc, sinc, cosp, sinp,
             out_ref, acc, kwin, vwin, cwin, swin):
        n = pl.program_id(1)

        def one_block(b):
            valid, upper = _attn_masks(n * per_step + b)
            start = b * BLK if isinstance(b, int) else pl.multiple_of(b * BLK, BLK)
            rows, before, window = pl.ds(start, BLK), pl.ds(start, BLK), pl.ds(start, 2 * BLK)
            own = pl.ds(start + BLK, BLK)
            dq_parts, dkp_parts, dkc_parts, dvp_parts, dvc_parts = [], [], [], [], []
            npairs = ATTN_W // LANES
            slabs = [slice(p * LANES, (p + 1) * LANES) for p in range(npairs)]
            qss = [_stack_heads(q_ref[rows, sl], upper) for sl in slabs]
            doss = [_stack_heads(do_ref[rows, sl], upper) for sl in slabs]
            ss = [lax.dot_general(qss[p], kwin[window, slabs[p]], (NT, ((), ())), preferred_element_type=F32) for p in range(npairs)]
            dpvs = [lax.dot_general(doss[p], vwin[window, slabs[p]], (NT, ((), ())), preferred_element_type=F32)
                    for p in range(npairs)]
            pes = [jnp.exp(jnp.where(valid, ss[p], NEG) - _spread_heads(l_ref[rows, slabs[p]], upper)) for p in range(npairs)]
            dss = [(pes[p] * (dpvs[p] + _spread_heads(c_ref[rows, slabs[p]], upper))).astype(BF16) for p in range(npairs)]
            for p in range(npairs):
                qs, dos, ds = qss[p], doss[p], dss[p]
                dq2 = _unstack_heads(jnp.dot(ds, kwin[window, slabs[p]], preferred_element_type=F32), upper)
                dk2 = lax.dot_general(ds, qs, (TN, ((), ())), preferred_element_type=F32)
                dv2 = lax.dot_general(pes[p].astype(BF16), dos, (TN, ((), ())), preferred_element_type=F32)
                dq_parts.append(dq2)
                dkp_parts.append(dk2[:BLK])
                dkc_parts.append(dk2[BLK:])
                dvp_parts.append(dv2[:BLK])
                dvc_parts.append(dv2[BLK:])
            dq = _rope(jnp.concatenate(dq_parts, axis=1) * scale, cwin[own, :], swin[own, :])
            dkc = _rope(jnp.concatenate(dkc_parts, axis=1), cwin[own, :], swin[own, :])
            dkp = _rope(jnp.concatenate(dkp_parts, axis=1), cwin[before, :], swin[before, :])
            return dq, dkp, dkc, jnp.concatenate(dvp_parts, axis=1), jnp.concatenate(dvc_parts, axis=1)

        @pl.when(n < nsteps)
        def _():
            kwin[0:BLK] = kp_ref[...]
            kwin[BLK:] = kc_ref[...]
            vwin[0:BLK] = vp_ref[...]
            vwin[BLK:] = vc_ref[...]
            cwin[0:BLK] = cosp[...]
            cwin[BLK:] = cosc[...]
            swin[0:BLK] = -sinp[...]
            swin[BLK:] = -sinc[...]
            dq, dkp, dkc, dvp, dvc = one_block(0)
            last = slice(rows_per_step - BLK, rows_per_step)

            @pl.when(n > 0)
            def _():
                if per_step > 1:
                    out_ref[0:rows_per_step - BLK, :] = acc[0:rows_per_step - BLK, :].astype(BF16)
                out_ref[last, dq_cols] = acc[last, dq_cols].astype(BF16)
                out_ref[last, dk_cols] = (acc[last, dk_cols] + dkp).astype(BF16)
                out_ref[last, dv_cols] = (acc[last, dv_cols] + dvp).astype(BF16)

            acc[0:BLK, dq_cols] = dq
            acc[0:BLK, dk_cols] = dkc
            acc[0:BLK, dv_cols] = dvc

            def later(b, carry):
                dq, dkp, dkc, dvp, dvc = one_block(b)
                start = pl.multiple_of(b * BLK, BLK)
                before, rows = pl.ds(start - BLK, BLK), pl.ds(start, BLK)
                acc[before, dk_cols] += dkp
                acc[before, dv_cols] += dvp
                acc[rows, dq_cols] = dq
                acc[rows, dk_cols] = dkc
                acc[rows, dv_cols] = dvc
                return carry

            lax.fori_loop(1, per_step, later, 0)

        @pl.when(n == flush_at)
        def _():
            out_ref[...] = acc[...].astype(BF16)

    flush_at = nsteps - 1 if nsteps == 1 else nsteps
    out_lag = 0 if nsteps == 1 else 1
    nc = lambda n: jnp.minimum(n, nsteps - 1)
    npv = lambda n: jnp.maximum(jnp.minimum(n, nsteps - 1) * per_step - 1, 0)
    cur = lambda part: pl.BlockSpec((None, rows_per_step, ATTN_W), lambda r, n: (r, nc(n), part))
    prev = lambda part: pl.BlockSpec((None, BLK, ATTN_W), lambda r, n: (r, npv(n), part))
    row = pl.BlockSpec((None, rows_per_step, ATTN_W), lambda r, n: (r, nc(n), 0))
    tab_c = pl.BlockSpec((None, rows_per_step, LANES), lambda r, n: (r, nc(n), 0))
    tab_p = pl.BlockSpec((None, BLK, LANES), lambda r, n: (r, npv(n), 0))
    out_spec = pl.BlockSpec((None, rows_per_step, GROUP_COLS), lambda r, n: (r, jnp.maximum(n - out_lag, 0), 0))
    out = pl.pallas_call(
        body, grid=(dil, nsteps + out_lag),
        in_specs=[cur(0), cur(1), prev(1), cur(2), prev(2), row, row, row, tab_c, tab_c, tab_p, tab_p],
        out_specs=out_spec,
        out_shape=jax.ShapeDtypeStruct((dil, length, GROUP_COLS), BF16),
        scratch_shapes=[pltpu.VMEM((rows_per_step, GROUP_COLS), F32)]
        + [pltpu.VMEM((rows_per_step + BLK, ATTN_W), BF16)] * 2 + [pltpu.VMEM((rows_per_step + BLK, LANES), F32)] * 2,
        compiler_params=_cparams(2), name=f"attn_bwd_g{g}")(
            qkv_v, qkv_v, qkv_v, qkv_v, qkv_v, do, cc, lse, cos_v, sin_v, cos_v, sin_v)
    return out.reshape(t, GROUP_COLS)


SQRT_HALF = 0.7071067811865476
INV_SQRT_2PI = 0.3989422804014327


def _sgu_core(uv, g, b, w_ref, bias):
    cdf = 0.5 * (1.0 + lax.erf(uv * SQRT_HALF))
    z = uv * cdf
    u, v = z[:, :SGU_W], z[:, SGU_W:]
    mu = jnp.mean(v, axis=1, keepdims=True)
    xc = v - mu
    rs = lax.rsqrt(jnp.mean(xc * xc, axis=1, keepdims=True) + EPS)
    xhat = xc * rs
    vn = xhat * g + b
    row = lax.broadcasted_iota(jnp.int32, (SGU_CHUNK, SGU_CHUNK), 0)
    col = lax.broadcasted_iota(jnp.int32, (SGU_CHUNK, SGU_CHUNK), 1)
    tril = row >= col
    upper = lax.broadcasted_iota(jnp.int32, (SGU_CHUNK, LANES), 1) >= SGU_W // SGU_GROUPS
    ws, vlo, vhi, mixed = [], [], [], []
    for pr in range(SGU_W // LANES):
        sl = slice(pr * LANES, (pr + 1) * LANES)
        w0 = jnp.where(tril, w_ref[2 * pr], 0.0).astype(BF16)
        w1 = jnp.where(tril, w_ref[2 * pr + 1], 0.0).astype(BF16)
        vn2 = vn[:, sl]
        lo = jnp.where(upper, 0.0, vn2).astype(BF16)
        hi = jnp.where(upper, vn2, 0.0).astype(BF16)
        mixed.append(jnp.dot(w0, lo, preferred_element_type=F32) + jnp.dot(w1, hi, preferred_element_type=F32)
                     + bias[:, sl])
        ws.append((w0, w1))
        vlo.append(lo)
        vhi.append(hi)
    return cdf, u, xhat, rs, jnp.concatenate(mixed, axis=1), ws, vlo, vhi, tril, upper


SGU_STEP = 4 * SGU_CHUNK


def _for_chunks(step_rows, fn):
    def one(ci, carry):
        fn(pl.ds(pl.multiple_of(ci * SGU_CHUNK, SGU_CHUNK), SGU_CHUNK))
        return carry

    lax.fori_loop(0, step_rows // SGU_CHUNK, one, 0)


def _sgu_fwd(gu, ln_g, ln_b, w_s, bias_exp):
    t = gu.shape[0]
    step = min(t, SGU_STEP)

    def body(uv_ref, g_ref, b_ref, w_ref, bias_ref, o_ref):
        def chunk(rows):
            _, u, _, _, mixed, *_ = _sgu_core(uv_ref[rows, :].astype(F32), g_ref[...], b_ref[...], w_ref, bias_ref[...])
            o_ref[rows, :] = (u * mixed).astype(BF16)

        _for_chunks(step, chunk)

    return pl.pallas_call(
        body, grid=(t // step,),
        in_specs=[pl.BlockSpec((step, 2 * SGU_W), lambda n: (n, 0)), _full((1, SGU_W)), _full((1, SGU_W)),
                  _full((SGU_GROUPS, SGU_CHUNK, SGU_CHUNK)), _full((SGU_CHUNK, SGU_W))],
        out_specs=pl.BlockSpec((step, SGU_W), lambda n: (n, 0)),
        out_shape=jax.ShapeDtypeStruct((t, SGU_W), BF16),
        compiler_params=_cparams(1), name="sgu_fwd")(gu, ln_g, ln_b, w_s, bias_exp)


def _sgu_bwd(dproj, gu, dsgu, ln_g, ln_b, w_s, bias_exp):
    t = gu.shape[0]
    step = min(t, SGU_STEP)
    nsteps = t // step
    e = _head_sum_matrix()

    def body(dp_in, uv_ref, ds_ref, g_ref, b_ref, w_ref, bias_ref, e_ref, out_ref, dw_ref, dbias_ref, dg_ref, db_ref):
        n = pl.program_id(0)

        @pl.when(n == 0)
        def _():
            dw_ref[...] = jnp.zeros(dw_ref.shape, F32)
            dbias_ref[...] = jnp.zeros(dbias_ref.shape, F32)
            dg_ref[...] = jnp.zeros(dg_ref.shape, F32)
            db_ref[...] = jnp.zeros(db_ref.shape, F32)

        _for_chunks(step, functools.partial(chunk, uv_ref, ds_ref, g_ref, b_ref, w_ref, bias_ref, out_ref, dw_ref, dbias_ref,
                                            dg_ref, db_ref))

        @pl.when(n == nsteps - 1)
        def _():
            dbias_ref[...] = _group_sum(dbias_ref[...], e_ref[...])

    def chunk(uv_ref, ds_ref, g_ref, b_ref, w_ref, bias_ref, out_ref, dw_ref, dbias_ref, dg_ref, db_ref, rows):
        uv = uv_ref[rows, :].astype(F32)
        g = g_ref[...]
        cdf, u, xhat, rs, mixed, ws, vlo, vhi, tril, upper = _sgu_core(uv, g, b_ref[...], w_ref, bias_ref[...])
        dsg = ds_ref[rows, :]
        du = dsg * mixed
        dmixed = dsg * u
        dbias_ref[...] += dmixed
        dvn = []
        for pr in range(SGU_W // LANES):
            sl = slice(pr * LANES, (pr + 1) * LANES)
            dm2 = dmixed[:, sl]
            dlo = jnp.where(upper, 0.0, dm2).astype(BF16)
            dhi = jnp.where(upper, dm2, 0.0).astype(BF16)
            w0, w1 = ws[pr]
            dvn.append(lax.dot_general(w0, dlo, (TN, ((), ())), preferred_element_type=F32)
                       + lax.dot_general(w1, dhi, (TN, ((), ())), preferred_element_type=F32))
            dw0 = lax.dot_general(dlo, vlo[pr], (NT, ((), ())), preferred_element_type=F32)
            dw1 = lax.dot_general(dhi, vhi[pr], (NT, ((), ())), preferred_element_type=F32)
            dw_ref[2 * pr] += jnp.where(tril, dw0, 0.0)
            dw_ref[2 * pr + 1] += jnp.where(tril, dw1, 0.0)
        dvn = jnp.concatenate(dvn, axis=1)
        dg_ref[...] += jnp.sum(dvn * xhat, axis=0, keepdims=True)
        db_ref[...] += jnp.sum(dvn, axis=0, keepdims=True)
        dxh = dvn * g
        dv = rs * (dxh - jnp.mean(dxh, axis=1, keepdims=True) - xhat * jnp.mean(dxh * xhat, axis=1, keepdims=True))
        dz = jnp.concatenate([du, dv], axis=1)
        dgelu = cdf + uv * (INV_SQRT_2PI * jnp.exp(-0.5 * uv * uv))
        out_ref[rows, :] = (dz * dgelu).astype(BF16)

    outs = pl.pallas_call(
        body, grid=(nsteps,),
        in_specs=[pl.BlockSpec(memory_space=pl.ANY), pl.BlockSpec((step, 2 * SGU_W), lambda n: (n, 0)),
                  pl.BlockSpec((step, SGU_W), lambda n: (n, 0)), _full((1, SGU_W)), _full((1, SGU_W)),
                  _full((SGU_GROUPS, SGU_CHUNK, SGU_CHUNK)), _full((SGU_CHUNK, SGU_W)), _full((ATTN_W, ATTN_W))],
        out_specs=[pl.BlockSpec((step, 2 * SGU_W), lambda n: (n, 0)), _full((SGU_GROUPS, SGU_CHUNK, SGU_CHUNK)),
                   _full((SGU_CHUNK, SGU_W)), _full((1, SGU_W)), _full((1, SGU_W))],
        out_shape=[jax.ShapeDtypeStruct(dproj.shape, BF16), jax.ShapeDtypeStruct((SGU_GROUPS, SGU_CHUNK, SGU_CHUNK), F32),
                   jax.ShapeDtypeStruct((SGU_CHUNK, SGU_W), F32), jax.ShapeDtypeStruct((1, SGU_W), F32),
                   jax.ShapeDtypeStruct((1, SGU_W), F32)],
        input_output_aliases={0: 0},
        compiler_params=_cparams(1), name="sgu_bwd")(dproj, gu, dsgu, ln_g, ln_b, w_s, bias_exp, e)
    return outs


def _merge_fwd(attn, sgu, gu, x, w_pa, w_ps, w_out, g2):
    t = x.shape[0]
    tm = min(t, 512)

    def body(a_ref, s_ref, ga_ref, gb_ref, x_ref, wpa, wps, wo, g_ref, pa_ref, ps_ref, m_ref, x1_ref, h2_ref):
        pa = jnp.dot(a_ref[...], wpa[...], preferred_element_type=F32)
        ps = jnp.dot(s_ref[...], wps[...], preferred_element_type=F32)
        merged = (_sigmoid(ga_ref[...].astype(F32)) * pa + _sigmoid(gb_ref[...].astype(F32)) * ps).astype(BF16)
        x1 = x_ref[...] + jnp.dot(merged, wo[...], preferred_element_type=F32)
        xhat, _ = _rms_stats(x1)
        pa_ref[...] = pa.astype(BF16)
        ps_ref[...] = ps.astype(BF16)
        m_ref[...] = merged
        x1_ref[...] = x1
        h2_ref[...] = (xhat * g_ref[...]).astype(BF16)

    half = pl.BlockSpec((tm, ATTN_W), lambda i: (i, 0))
    full = pl.BlockSpec((tm, D_MODEL), lambda i: (i, 0))
    return pl.pallas_call(
        body, grid=(t // tm,),
        in_specs=[half, half, pl.BlockSpec((tm, D_MODEL), lambda i: (i, 1)), pl.BlockSpec((tm, D_MODEL), lambda i: (i, 2)),
                  full, _resident((ATTN_W, D_MODEL)), _resident((SGU_W, D_MODEL)), _resident((D_MODEL, D_MODEL)),
                  _full((1, D_MODEL))],
        out_specs=[full] * 5,
        out_shape=[jax.ShapeDtypeStruct((t, D_MODEL), BF16), jax.ShapeDtypeStruct((t, D_MODEL), BF16),
                   jax.ShapeDtypeStruct((t, D_MODEL), BF16), jax.ShapeDtypeStruct((t, D_MODEL), F32),
                   jax.ShapeDtypeStruct((t, D_MODEL), BF16)],
        compiler_params=_cparams(1), name="merge_fwd")(attn, sgu, gu, gu, x, w_pa, w_ps, w_out, g2)


def _merge_bwd(dx1b, gu, pa, ps, w_pa, w_ps, w_out):
    t = dx1b.shape[0]
    tm = min(t, 512)

    def body(d_ref, ga_ref, gb_ref, pa_ref, ps_ref, wpa, wps, wo, out_ref, dpa_ref, dps_ref, da_ref, dsg_ref):
        dm = lax.dot_general(d_ref[...], wo[...], (NT, ((), ())), preferred_element_type=F32)
        sa, sb = _sigmoid(ga_ref[...].astype(F32)), _sigmoid(gb_ref[...].astype(F32))
        dpa = (dm * sa).astype(BF16)
        dps = (dm * sb).astype(BF16)
        out_ref[:, 0:D_MODEL] = jnp.zeros((tm, D_MODEL), BF16)
        out_ref[:, D_MODEL:2 * D_MODEL] = (dm * pa_ref[...].astype(F32) * sa * (1.0 - sa)).astype(BF16)
        out_ref[:, 2 * D_MODEL:GU_COLS] = (dm * ps_ref[...].astype(F32) * sb * (1.0 - sb)).astype(BF16)
        dpa_ref[...] = dpa
        dps_ref[...] = dps
        da_ref[...] = lax.dot_general(dpa, wpa[...], (NT, ((), ())), preferred_element_type=F32)
        dsg_ref[...] = lax.dot_general(dps, wps[...], (NT, ((), ())), preferred_element_type=F32)

    half = pl.BlockSpec((tm, ATTN_W), lambda i: (i, 0))
    full = pl.BlockSpec((tm, D_MODEL), lambda i: (i, 0))
    return pl.pallas_call(
        body, grid=(t // tm,),
        in_specs=[full, pl.BlockSpec((tm, D_MODEL), lambda i: (i, 1)),
                  pl.BlockSpec((tm, D_MODEL), lambda i: (i, 2)), full, full,
                  _resident((ATTN_W, D_MODEL)), _resident((SGU_W, D_MODEL)), _resident((D_MODEL, D_MODEL))],
        out_specs=[pl.BlockSpec((tm, GU_COLS), lambda i: (i, 0)), full, full, half, half],
        out_shape=[jax.ShapeDtypeStruct((t, GU_COLS), BF16), jax.ShapeDtypeStruct((t, D_MODEL), BF16),
                   jax.ShapeDtypeStruct((t, D_MODEL), BF16), jax.ShapeDtypeStruct((t, ATTN_W), F32),
                   jax.ShapeDtypeStruct((t, SGU_W), F32)],
        compiler_params=_cparams(1), name="merge_bwd")(dx1b, gu, gu, pa, ps, w_pa, w_ps, w_out)


def _token_call(name, body, t, tm, ins, outs, reds=(), scratch=()):
    return pl.pallas_call(
        body, grid=(t // tm,), in_specs=[s for _, s in ins],
        out_specs=[o[2] for o in outs] + [_full(r) for r in reds],
        out_shape=[jax.ShapeDtypeStruct(o[0], o[1]) for o in outs] + [jax.ShapeDtypeStruct(r, F32) for r in reds],
        scratch_shapes=list(scratch), compiler_params=_cparams(1), name=name)(*[a for a, _ in ins])


def _rows_spec(tm, width):
    return pl.BlockSpec((tm, width), lambda i: (i, 0))


def _chips_spec(tm):
    return pl.BlockSpec((N_CHIPS, tm, FF_SHARD), lambda i: (0, i, 0))


def _zero_at_start(*refs):
    @pl.when(pl.program_id(0) == 0)
    def _():
        for r in refs:
            r[...] = jnp.zeros(r.shape, r.dtype)


def _ffn_fwd(h2, w_g, w_u):
    t = h2.shape[0]
    tm = min(t, 512)

    def body(h_ref, wg_ref, wu_ref, fa_ref, fb_ref, ff_ref):
        h = h_ref[...]
        for s in range(N_CHIPS):
            a = jnp.dot(h, wg_ref[s], preferred_element_type=F32)
            b = jnp.dot(h, wu_ref[s], preferred_element_type=F32)
            sg = _sigmoid(a)
            silu = a * sg
            fa_ref[s] = (b * (sg * (1.0 + a * (1.0 - sg)))).astype(BF16)
            fb_ref[s] = silu.astype(BF16)
            ff_ref[s] = (silu * b).astype(BF16)

    shp = (N_CHIPS, t, FF_SHARD)
    w_spec = _resident((N_CHIPS, D_MODEL, FF_SHARD))
    return _token_call("ffn_fwd", body, t, tm, [(h2, _rows_spec(tm, D_MODEL)), (w_g, w_spec), (w_u, w_spec)],
                       [(shp, BF16, _chips_spec(tm))] * 3)


def _ffn_down_loss(ff, w_d, x1, tgt, gf):
    t = x1.shape[0]
    tm = min(t, 512)

    def body(ff_ref, wd_ref, x1_ref, tgt_ref, g_ref, dx2_ref, dx2b_ref, loss_ref, dgf_ref):
        _zero_at_start(loss_ref, dgf_ref)
        acc = jnp.dot(ff_ref[0], wd_ref[0], preferred_element_type=F32)
        for s in range(1, N_CHIPS):
            acc = acc + jnp.dot(ff_ref[s], wd_ref[s], preferred_element_type=F32)
        x2 = x1_ref[...] + acc
        g = g_ref[...]
        xhat, rr = _rms_stats(x2)
        diff = xhat * g - tgt_ref[...]
        rows = jnp.sum(diff * diff, axis=1, keepdims=True)
        loss_ref[...] += jnp.broadcast_to(jnp.sum(rows, axis=0, keepdims=True) * (0.5 / D_MODEL), (1, LANES))
        dy = diff * (1.0 / D_MODEL)
        dgf_ref[...] += jnp.sum(dy * xhat, axis=0, keepdims=True)
        dx2 = _rms_bwd(dy, xhat, rr, g)
        dx2_ref[...] = dx2
        dx2b_ref[...] = dx2.astype(BF16)

    row = _rows_spec(tm, D_MODEL)
    return _token_call("ffn_down_loss", body, t, tm,
                       [(ff, _chips_spec(tm)), (w_d, _resident((N_CHIPS, FF_SHARD, D_MODEL))), (x1, row), (tgt, row),
                        (gf, _full((1, D_MODEL)))],
                       [((t, D_MODEL), F32, row), ((t, D_MODEL), BF16, row)], reds=[(1, LANES), (1, D_MODEL)])


def _ffn_bwd_act(dx2b, w_d, fa, fb):
    t = dx2b.shape[0]
    tm = min(t, 512)

    def body(d_ref, wd_ref, fa_ref, fb_ref, da_ref, db_ref):
        d = d_ref[...]
        for s in range(N_CHIPS):
            dff = lax.dot_general(d, wd_ref[s], (NT, ((), ())), preferred_element_type=F32)
            da_ref[s] = (dff * fa_ref[s].astype(F32)).astype(BF16)
            db_ref[s] = (dff * fb_ref[s].astype(F32)).astype(BF16)

    shp = (N_CHIPS, t, FF_SHARD)
    return _token_call("ffn_bwd_act", body, t, tm,
                       [(dx2b, _rows_spec(tm, D_MODEL)), (w_d, _resident((N_CHIPS, FF_SHARD, D_MODEL))),
                        (fa, _chips_spec(tm)), (fb, _chips_spec(tm))],
                       [(shp, BF16, _chips_spec(tm))] * 2)


def _ffn_bwd_in(da, db, w_g, w_u, x1, dx2, g2):
    t = x1.shape[0]
    tm = min(t, 512)

    def body(da_ref, db_ref, wg_ref, wu_ref, x1_ref, dx2_ref, g_ref, dx1_ref, dx1b_ref, dg_ref):
        _zero_at_start(dg_ref)
        acc = None
        for s in range(N_CHIPS):
            part = (lax.dot_general(da_ref[s], wg_ref[s], (NT, ((), ())), preferred_element_type=F32)
                    + lax.dot_general(db_ref[s], wu_ref[s], (NT, ((), ())), preferred_element_type=F32))
            acc = part if acc is None else acc + part
        xhat, rr = _rms_stats(x1_ref[...])
        dg_ref[...] += jnp.sum(acc * xhat, axis=0, keepdims=True)
        dx1 = dx2_ref[...] + _rms_bwd(acc, xhat, rr, g_ref[...])
        dx1_ref[...] = dx1
        dx1b_ref[...] = dx1.astype(BF16)

    row = _rows_spec(tm, D_MODEL)
    w_spec = _resident((N_CHIPS, D_MODEL, FF_SHARD))
    return _token_call("ffn_bwd_in", body, t, tm,
                       [(da, _chips_spec(tm)), (db, _chips_spec(tm)), (w_g, w_spec), (w_u, w_spec), (x1, row), (dx2, row),
                        (g2, _full((1, D_MODEL)))],
                       [((t, D_MODEL), F32, row), ((t, D_MODEL), BF16, row)], reds=[(1, D_MODEL)])


def _group_dh(d, w_refs):
    dh = None
    for part, w_ref in enumerate(w_refs):
        term = lax.dot_general(d[:, part * ATTN_W:(part + 1) * ATTN_W], w_ref[...], (NT, ((), ())),
                               preferred_element_type=F32)
        dh = term if dh is None else dh + term
    return dh


def _in_proj_bwd(dgu, dqkvs, w_in, x, dx1, g1):
    t = x.shape[0]
    tile = min(t, TILE)
    ngroups = len(DILATIONS)

    def body(*refs):
        dgu_ref, dq_refs = refs[0], refs[1:1 + ngroups]
        w0_ref, w1_ref = refs[1 + ngroups:3 + ngroups]
        wg_refs = [refs[3 + ngroups + 3 * g:6 + ngroups + 3 * g] for g in range(ngroups)]
        x_ref, dx1_ref, g_ref, dx_ref, dg_ref = refs[3 + 4 * ngroups:5 + 4 * ngroups + 3]
        slabs = refs[5 + 4 * ngroups + 3:]
        _zero_at_start(dg_ref)
        for g in range(1, ngroups):
            dil = DILATIONS[g]
            part = _group_dh(dq_refs[g][...].reshape(tile, GROUP_COLS), wg_refs[g])
            for r in range(dil):
                _put_class_rows(slabs[g - 1], r, dil, part[r * (tile // dil):(r + 1) * (tile // dil)])
        dh = lax.dot_general(dgu_ref[:, 0:GU_HALF], w0_ref[...], (NT, ((), ())), preferred_element_type=F32)
        dh = dh + lax.dot_general(dgu_ref[:, GU_HALF:], w1_ref[...], (NT, ((), ())), preferred_element_type=F32)
        dh = dh + _group_dh(dq_refs[0][0], wg_refs[0])
        for slab in slabs:
            dh = dh + _from_slabs(slab)
        xhat, rr = _rms_stats(x_ref[...])
        dg_ref[...] += jnp.sum(dh * xhat, axis=0, keepdims=True)
        dx_ref[...] = dx1_ref[...] + _rms_bwd(dh, xhat, rr, g_ref[...])

    row = _rows_spec(tile, D_MODEL)
    group_ins = [(dqkvs[g].reshape(d, t // d, GROUP_COLS), _group_spec(d, tile, GROUP_COLS)) for g, d in enumerate(DILATIONS)]
    w_specs = _gu_w_specs() + [s for g in range(ngroups) for s in _group_w_specs(g)]
    return _token_call(
        "in_proj_bwd", body, t, tile,
        [(dgu, _rows_spec(tile, GU_COLS))] + group_ins + [(w_in, s) for s in w_specs]
        + [(x, row), (dx1, row), (g1, _full((1, D_MODEL)))],
        [((t, D_MODEL), F32, row)], reds=[(1, D_MODEL)], scratch=[_slabs(tile, D_MODEL)] * (ngroups - 1))


WGRAD_TK = 2048


def _wgrad_mm(name, grid, a, a_spec, b, b_spec, acc_shape, out_shape, out_spec, dst=None):
    nk = grid[-1]

    def body(*refs):
        a_ref, b_ref, o_ref, acc_ref = refs[0], refs[1], refs[-2], refs[-1]
        k = pl.program_id(len(grid) - 1)
        part = lax.dot_general(a_ref[...], b_ref[...], (TN, ((), ())), preferred_element_type=F32)

        @pl.when(k == 0)
        def _():
            acc_ref[...] = part

        @pl.when(k > 0)
        def _():
            acc_ref[...] += part

        @pl.when(k == nk - 1)
        def _():
            o_ref[...] = acc_ref[...].astype(BF16)

    filled = [] if dst is None else [dst]
    return pl.pallas_call(
        body, grid=grid, in_specs=[a_spec, b_spec] + [pl.BlockSpec(memory_space=pl.ANY)] * len(filled),
        out_specs=out_spec, out_shape=jax.ShapeDtypeStruct(out_shape, BF16), scratch_shapes=[pltpu.VMEM(acc_shape, F32)],
        input_output_aliases={2: 0} if filled else {}, compiler_params=_cparams(len(grid)), name=name)(a, b, *filled)


def _wgrad_2d(name, a, b, tm, tn):
    t, k1 = a.shape
    n = b.shape[1]
    tk = min(t, WGRAD_TK)
    return _wgrad_mm(name, (k1 // tm, n // tn, t // tk), a, pl.BlockSpec((tk, tm), lambda i, j, k: (k, i)),
                     b, pl.BlockSpec((tk, tn), lambda i, j, k: (k, j)), (tm, tn), (k1, n),
                     pl.BlockSpec((tm, tn), lambda i, j, k: (i, j)))


def _wgrad_in(hs, dgu, dqkvs):
    t = dgu.shape[0]
    tk = min(t, WGRAD_TK)
    gu_block = QKV_BLOCKS * ATTN_W // GU_HALF
    parts = [(hs[0], dgu, GU_HALF, lambda j: j + gu_block)]
    parts += [(hs[g].reshape(t, D_MODEL), dqkvs[g], ATTN_W, lambda j, g=g: _w_in_block(j, g)) for g in range(3)]
    dst = None
    for n, (a, b, tn, block_of) in enumerate(parts):
        dst = _wgrad_mm(f"wgrad_in_{n}", (1, b.shape[1] // tn, t // tk),
                        a, pl.BlockSpec((tk, D_MODEL), lambda i, j, k: (k, 0)), b, pl.BlockSpec((tk, tn), lambda i, j, k: (k, j)),
                        (D_MODEL, tn), (D_MODEL, IN_COLS),
                        pl.BlockSpec((D_MODEL, tn), lambda i, j, k, block_of=block_of: (0, block_of(j))), dst=dst)
    return dst


def _wgrad_ff_in(name, h2, da):
    t = h2.shape[0]
    tk = min(t, WGRAD_TK)
    return _wgrad_mm(name, (N_CHIPS, 1, t // tk), h2, pl.BlockSpec((tk, D_MODEL), lambda i, j, k: (k, 0)),
                     da, pl.BlockSpec((None, tk, FF_SHARD), lambda i, j, k: (i, k, 0)), (D_MODEL, FF_SHARD),
                     (N_CHIPS, D_MODEL, FF_SHARD), pl.BlockSpec((None, D_MODEL, FF_SHARD), lambda i, j, k: (i, 0, 0)))


def _wgrad_ff_down(ff, dx2b):
    t = dx2b.shape[0]
    tk = min(t, WGRAD_TK)
    return _wgrad_mm("wgrad_ffn_down", (N_CHIPS, 1, t // tk), ff, pl.BlockSpec((None, tk, FF_SHARD), lambda i, j, k: (i, k, 0)),
                     dx2b, pl.BlockSpec((tk, D_MODEL), lambda i, j, k: (k, 0)), (FF_SHARD, D_MODEL),
                     (N_CHIPS, FF_SHARD, D_MODEL), pl.BlockSpec((None, FF_SHARD, D_MODEL), lambda i, j, k: (i, 0, 0)))


def _local_step(x, pos_col, tgt, g1, ln_g, ln_b, w_s, b_s, g2, gf, first_weight, late_weights, on_grads=None):
    tables = _rope_tables(pos_col)
    bias_exp = jnp.repeat(jnp.transpose(b_s), SGU_W // SGU_GROUPS, axis=1)

    hs = _norm_fwd(x, g1)
    w_p = first_weight([hs[0], bias_exp] + [table for pair in tables for table in pair])
    gu, qkvs = _in_proj(hs, w_p, tables)
    os_, ls_ = [], []
    for g, dil in enumerate(DILATIONS):
        o, lse = _attn_fwd(qkvs[g], g, dil)
        os_.append(o)
        ls_.append(lse)
    attn = _combine_fwd(os_, ls_)
    sgu = _sgu_fwd(gu, ln_g, ln_b, w_s, bias_exp)
    w_pa, w_ps, w_out, w_g, w_u, w_d = late_weights(attn)
    pa, ps, merged, x1, h2 = _merge_fwd(attn, sgu, gu, x, w_pa, w_ps, w_out, g2)
    fa, fb, ff = _ffn_fwd(h2, w_g, w_u)
    dx2, dx2b, loss, dgf = _ffn_down_loss(ff, w_d, x1, tgt, gf)

    da, db = _ffn_bwd_act(dx2b, w_d, fa, fb)
    dw_d = _wgrad_ff_down(ff, dx2b)
    dx1, dx1b, dg2 = _ffn_bwd_in(da, db, w_g, w_u, x1, dx2, g2)
    dw_g = _wgrad_ff_in("wgrad_ffn_gate", h2, da)
    dw_u = _wgrad_ff_in("wgrad_ffn_up", h2, db)

    dgu, dpa, dps, dattn, dsgu = _merge_bwd(dx1b, gu, pa, ps, w_pa, w_ps, w_out)
    dw_out = _wgrad_2d("wgrad_out", merged, dx1b, D_MODEL, D_MODEL)
    dw_pa = _wgrad_2d("wgrad_proj_attn", attn, dpa, ATTN_W, D_MODEL)
    dw_ps = _wgrad_2d("wgrad_proj_sgu", sgu, dps, SGU_W, D_MODEL)
    if on_grads is not None:
        ln_g = ln_g + on_grads(1, dict(w_proj_attn=dw_pa, w_proj_sgu=dw_ps, w_out=dw_out, w_ffn_gate=dw_g, w_ffn_up=dw_u,
                                       w_ffn_down=dw_d))[:, :SGU_W]
    dgu, dw_s, dbias, dln_g, dln_b = _sgu_bwd(dgu, gu, dsgu, ln_g, ln_b, w_s, bias_exp)
    dos, ccs = _combine_bwd(dattn, os_, ls_)
    dqkvs = [_attn_bwd(qkvs[g], dos[g], ccs[g], ls_[g], *tables[g], g, dil) for g, dil in enumerate(DILATIONS)]
    dw_p = _wgrad_in(hs, dgu, dqkvs)
    if on_grads is not None:
        g1 = g1 + on_grads(0, dict(w_in=dw_p))
    dx, dg1 = _in_proj_bwd(dgu, dqkvs, w_p, x, dx1, g1)

    db_s = jnp.transpose(dbias[:, ::SGU_W // SGU_GROUPS])
    small = dict(loss=loss, norm1_g=dg1, sgu_ln_g=dln_g, sgu_ln_b=dln_b, w_spatial=dw_s, b_spatial=db_s,
                 norm2_g=dg2, final_g=dgf)
    big = dict(w_in=dw_p, w_proj_attn=dw_pa, w_proj_sgu=dw_ps, w_out=dw_out, w_ffn_gate=dw_g, w_ffn_up=dw_u,
               w_ffn_down=dw_d)
    return dx, big, small


def _ew(name, fn, ins, out_dtypes, after=()):
    shp = ins[0].shape
    rows, cols = shp
    tr = next((cand for cand in (256, 352, 128) if rows % cand == 0 and rows > cand), rows)

    def body(*refs):
        res = fn(*[r[...] for r in refs[:len(ins)]])
        for o_ref, v in zip(refs[len(ins) + len(after):], res):
            o_ref[...] = v.astype(o_ref.dtype)

    spec = pl.BlockSpec((tr, cols), lambda i: (i, 0))
    return pl.pallas_call(
        body, grid=(rows // tr,), in_specs=[spec] * len(ins) + [pl.BlockSpec(memory_space=pl.ANY)] * len(after),
        out_specs=[spec] * len(out_dtypes), out_shape=[jax.ShapeDtypeStruct(shp, d) for d in out_dtypes],
        compiler_params=_cparams(1), name=name)(*ins, *after)


def _adamw_math(g, w, m, v):
    m = ADAM_B1 * m + (1.0 - ADAM_B1) * g
    v = ADAM_B2 * v + (1.0 - ADAM_B2) * (g * g)
    m_hat = m / (1.0 - ADAM_B1 ** ADAM_STEP)
    v_hat = v / (1.0 - ADAM_B2 ** ADAM_STEP)
    delta = -ADAM_LR * (m_hat / (jnp.sqrt(v_hat) + ADAM_EPS) + ADAM_WD * w)
    return delta, m, v


def _adamw(name, g, w, m, v):
    return _ew(name, lambda g_, w_, m_, v_: (g_,) + _adamw_math(g_, w_, m_, v_), [g, w, m, v], [F32] * 4)


VMEM_SPEC = pl.BlockSpec(memory_space=pltpu.VMEM)


def _for_row_chunks(rows, fn):
    ck = next(c for c in (64, 32, 16) if rows % c == 0)

    def step(i, carry):
        fn(pl.multiple_of(i * ck, ck), ck)
        return carry

    lax.fori_loop(0, rows // ck, step, 0)


def _place():
    x, y, c = lax.axis_index("x"), lax.axis_index("y"), lax.axis_index("c")
    chips = [(1 - x, y), (x, 1 - y), (1 - x, 1 - y)]
    return x, y, c, 2 * x + y, chips


def _rows(ref, start, size):
    if len(ref.shape) == 2:
        return ref.at[pl.ds(start, size), :]
    return ref.at[:, pl.ds(start, size), :]


def _gather_finish(name, shard, landed):
    k_rows, n = shard.shape
    kh = k_rows // 2

    def body(shard_hbm, land_hbm, out_ref, shard_ref, land_ref, loc, send, recv):
        x, y, c, me, chips = _place()
        sibling = (x, y, 1 - c)

        def window(core, chip):
            return out_ref.at[pl.ds(core * kh, kh), pl.ds(pl.multiple_of(chip * n, LANES), n)]

        loads = [pltpu.make_async_copy(land_hbm.at[j], land_ref.at[j], loc.at[0, j]) for j in range(3)]
        loads.append(pltpu.make_async_copy(shard_hbm, shard_ref, loc.at[0, 3]))
        for cp in loads:
            cp.start()
        copies, passed = [], []
        for j, chip in enumerate(chips):
            mine = window(c, 2 * chip[0] + chip[1])
            loads[j].wait()
            copies.append(pltpu.make_async_copy(land_ref.at[j], mine, loc.at[1, j]))
            passed.append(pltpu.make_async_remote_copy(src_ref=land_ref.at[j], dst_ref=mine, send_sem=send.at[j],
                                                       recv_sem=recv.at[j], device_id=sibling, device_id_type=MESH))
            copies[-1].start()
            passed[-1].start()
        loads[3].wait()
        copies.append(pltpu.make_async_copy(shard_ref, out_ref.at[:, pl.ds(pl.multiple_of(me * n, LANES), n)], loc.at[1, 3]))
        copies[-1].start()
        for j, chip in enumerate(chips):
            pltpu.make_async_remote_copy(src_ref=land_ref.at[j], dst_ref=window(1 - c, 2 * chip[0] + chip[1]), send_sem=send.at[j],
                                         recv_sem=recv.at[j], device_id=sibling, device_id_type=MESH).wait_recv()
        for cp in copies:
            cp.wait()
        for cp in passed:
            cp.wait_send()

    any_spec = pl.BlockSpec(memory_space=pl.ANY)
    return pl.pallas_call(
        body, in_specs=[any_spec] * 2, out_specs=any_spec,
        out_shape=jax.ShapeDtypeStruct((k_rows, N_CHIPS * n), shard.dtype),
        scratch_shapes=[pltpu.VMEM(shard.shape, shard.dtype), pltpu.VMEM(landed.shape, landed.dtype),
                        pltpu.SemaphoreType.DMA((2, 4)), pltpu.SemaphoreType.DMA((3,)), pltpu.SemaphoreType.DMA((3,))],
        compiler_params=pltpu.CompilerParams(vmem_limit_bytes=VMEM_LIMIT), name=name)(shard, landed)


HBM_SPEC = pl.BlockSpec(memory_space=pltpu.HBM)
SEM_SPEC = pl.BlockSpec(memory_space=pltpu.SEMAPHORE)
DATAFLOW = pltpu.SideEffectType.DATAFLOW_SIDE_EFFECTING
TOKEN_SHAPE = (1, D_MODEL)
N_PEERS = 7
SUM_SPLIT = 4
SUM_SPLIT_ELEMS = 512 * 1024


def _peers():
    x, y, c = lax.axis_index("x"), lax.axis_index("y"), lax.axis_index("c")
    flip = lambda v, f: 1 - v if f else v
    return [(flip(x, k & 4), flip(y, k & 2), flip(c, k & 1)) for k in range(1, N_PEERS + 1)]


def _piece_shape(shape):
    return (shape[-2] // 2, shape[2] if len(shape) == 3 else shape[1] // N_CHIPS)


def _device_piece(ref, chip, core):
    kh, n4 = _piece_shape(ref.shape)
    if len(ref.shape) == 3:
        return ref.at[chip, pl.ds(core * kh, kh), :]
    return ref.at[pl.ds(core * kh, kh), pl.ds(chip * n4, n4)]


def _exchange_copies(partials, lands, send, recv):
    return [pltpu.make_async_remote_copy(
        src_ref=_device_piece(partials[t], 2 * px + py, pc), dst_ref=lands[t].at[k], send_sem=send.at[t * N_PEERS + k],
        recv_sem=recv.at[t * N_PEERS + k], device_id=(px, py, pc), device_id_type=MESH)
        for t in range(len(partials)) for k, (px, py, pc) in enumerate(_peers())]


def _broadcast_copies(srcs, lands, send, recv):
    return [pltpu.make_async_remote_copy(
        src_ref=srcs[t], dst_ref=lands[t].at[k], send_sem=send.at[t * N_PEERS + k], recv_sem=recv.at[t * N_PEERS + k],
        device_id=peer, device_id_type=MESH)
        for t in range(len(srcs)) for k, peer in enumerate(_peers())]


class _LocalCopy:
    def __init__(self, src_ref, dst_ref, sem):
        self.copy = pltpu.make_async_copy(src_ref, dst_ref, sem)

    def start(self):
        self.copy.start()

    def wait_send(self):
        self.copy.wait()

    def wait_recv(self):
        pass


def _gather_copies(shards, lands, send, recv):
    x, y, c, me, chips = _place()
    copies = []
    for t in range(len(shards)):
        n = shards[t].shape[1]
        place = lands[t].at[me] if len(lands[t].shape) == 3 else lands[t].at[:, pl.ds(pl.multiple_of(me * n, LANES), n)]
        copies += [pltpu.make_async_remote_copy(
            src_ref=shards[t], dst_ref=place, send_sem=send.at[t * 4 + j], recv_sem=recv.at[t * 4 + j],
            device_id=(*chip, c), device_id_type=MESH) for j, chip in enumerate(chips)]
        copies.append(_LocalCopy(shards[t], place, send.at[t * 4 + 3]))
    return copies


def _gather_half_copies(shards, lands, send, recv):
    x, y, c, me, chips = _place()
    return [pltpu.make_async_remote_copy(
        src_ref=_rows(shards[t], c * (shards[t].shape[0] // 2), shards[t].shape[0] // 2), dst_ref=lands[t].at[j],
        send_sem=send.at[t * 3 + j], recv_sem=recv.at[t * 3 + j], device_id=(*chip, c), device_id_type=MESH)
        for t in range(len(shards)) for j, chip in enumerate(chips)]


def _split_start(name, copies, per_tensor, srcs, land_shapes):
    nt = len(srcs)
    lands = [lax.empty(s, a.dtype) for s, a in zip(land_shapes, srcs)]
    nsem = nt * per_tensor

    def body(*refs):
        send, recv = refs[2 * nt], refs[2 * nt + 1]
        for cp in copies(refs[:nt], refs[nt:2 * nt], send, recv):
            cp.start()
        refs[-1][...] = jnp.zeros(TOKEN_SHAPE, F32)

    hbm = lambda a: pltpu.with_memory_space_constraint(a, pltpu.HBM)
    outs = pl.pallas_call(
        body, name=name,
        out_shape=[pltpu.SemaphoreType.DMA((nsem,)), pltpu.SemaphoreType.DMA((nsem,))]
        + [pltpu.HBM(s.shape, s.dtype) for s in srcs] + [pltpu.HBM(l.shape, l.dtype) for l in lands]
        + [jax.ShapeDtypeStruct(TOKEN_SHAPE, F32)],
        in_specs=[HBM_SPEC] * (2 * nt), out_specs=[SEM_SPEC, SEM_SPEC] + [HBM_SPEC] * (2 * nt) + [VMEM_SPEC],
        input_output_aliases={i: 2 + i for i in range(2 * nt)},
        compiler_params=pltpu.CompilerParams(has_side_effects=DATAFLOW))(*[hbm(a) for a in list(srcs) + lands])
    return outs[0], outs[1], outs[2:2 + nt], outs[2 + nt:2 + 2 * nt], outs[-1]


def _split_wait(name, copies, send, recv, srcs, lands, after):
    nt = len(srcs)
    after = list(after) if isinstance(after, (list, tuple)) else [after]

    def body(*refs):
        for cp in copies(refs[:nt], refs[nt:2 * nt], refs[2 * nt], refs[2 * nt + 1]):
            cp.wait_send()
            cp.wait_recv()

    outs = pl.pallas_call(
        body, name=name,
        out_shape=[pltpu.HBM(s.shape, s.dtype) for s in srcs] + [pltpu.HBM(l.shape, l.dtype) for l in lands],
        in_specs=[HBM_SPEC] * (2 * nt) + [SEM_SPEC, SEM_SPEC] + [pl.BlockSpec(memory_space=pl.ANY)] * len(after),
        out_specs=[HBM_SPEC] * (2 * nt), input_output_aliases={i: i for i in range(2 * nt)},
        compiler_params=pltpu.CompilerParams(has_side_effects=DATAFLOW))(*srcs, *lands, send, recv, *after)
    return outs[:nt], outs[nt:]


def _device_sum(name, partials, lands):
    nt = len(partials)
    pieces = [_piece_shape(p.shape) for p in partials]
    units = []
    for t, (kh, n4) in enumerate(pieces):
        split = SUM_SPLIT if kh * n4 >= SUM_SPLIT_ELEMS else 1
        units += [(t, j * (kh // split), kh // split) for j in range(split)]
    nu = len(units)

    def body(*refs):
        ins, slots, outs = refs[:nt], refs[nt:2 * nt], refs[2 * nt:3 * nt]
        owns, landed, sums = refs[3 * nt:4 * nt], refs[4 * nt:5 * nt], refs[5 * nt:6 * nt]
        loc, send, recv = refs[6 * nt:]
        x, y, c, me, chips = _place()
        sibling = (x, y, 1 - c)
        loads = []
        for u, (t, r0, rows) in enumerate(units):
            loads.append((
                pltpu.make_async_copy(_rows(_device_piece(ins[t], me, c), r0, rows), _rows(owns[t], r0, rows), loc.at[0, u]),
                pltpu.make_async_copy(_rows(slots[t], r0, rows), _rows(landed[t], r0, rows), loc.at[1, u])))
            for cp in loads[-1]:
                cp.start()
        stores = []
        for u, (t, r0, rows) in enumerate(units):
            for cp in loads[u]:
                cp.wait()

            def add(q0, ck, own=owns[t], slot=landed[t], dst=sums[t], r0=r0):
                at = pl.ds(pl.multiple_of(r0 + q0, ck), ck)
                acc = own[at, :].astype(F32)
                for k in range(N_PEERS):
                    acc = acc + slot[k, at, :].astype(F32)
                dst[at, :] = acc

            _for_row_chunks(rows, add)
            mine = _rows(outs[t], c * pieces[t][0] + r0, rows)
            stores.append((
                pltpu.make_async_copy(_rows(sums[t], r0, rows), mine, loc.at[2, u]),
                pltpu.make_async_remote_copy(src_ref=_rows(sums[t], r0, rows), dst_ref=mine, send_sem=send.at[u],
                                             recv_sem=recv.at[u], device_id=sibling, device_id_type=MESH)))
            for cp in stores[-1]:
                cp.start()
        for u, (t, r0, rows) in enumerate(units):
            pltpu.make_async_remote_copy(
                src_ref=_rows(sums[t], r0, rows), dst_ref=_rows(outs[t], (1 - c) * pieces[t][0] + r0, rows),
                send_sem=send.at[u], recv_sem=recv.at[u], device_id=sibling, device_id_type=MESH).wait_recv()
            stores[u][0].wait()
            stores[u][1].wait_send()

    any_spec = pl.BlockSpec(memory_space=pl.ANY)
    return pl.pallas_call(
        body, in_specs=[any_spec] * (2 * nt), out_specs=[any_spec] * nt,
        out_shape=[jax.ShapeDtypeStruct((2 * kh, n4), F32) for kh, n4 in pieces],
        scratch_shapes=[pltpu.VMEM(p, BF16) for p in pieces] + [pltpu.VMEM((N_PEERS,) + p, BF16) for p in pieces]
        + [pltpu.VMEM(p, F32) for p in pieces]
        + [pltpu.SemaphoreType.DMA((3, nu)), pltpu.SemaphoreType.DMA((nu,)), pltpu.SemaphoreType.DMA((nu,))],
        compiler_params=pltpu.CompilerParams(vmem_limit_bytes=VMEM_LIMIT), name=name)(*partials, *lands)


VEC_SHAPE = (8, D_MODEL + LANES)
VEC_SLOTS = dict(norm1_g=(slice(0, 1), slice(0, D_MODEL)), norm2_g=(slice(1, 2), slice(0, D_MODEL)),
                 final_g=(slice(2, 3), slice(0, D_MODEL)), sgu_ln_g=(slice(3, 4), slice(0, SGU_W)),
                 sgu_ln_b=(slice(3, 4), slice(SGU_W, 2 * SGU_W)), b_spatial=(slice(0, 8), slice(D_MODEL, D_MODEL + LANES)),
                 loss=(slice(4, 5), slice(0, LANES)))
VEC_PARAMS = ("norm1_g", "norm2_g", "final_g", "sgu_ln_g", "sgu_ln_b", "b_spatial")
SMALL_PARAMS = VEC_PARAMS + ("w_spatial",)
W_SPATIAL_2D = (SGU_GROUPS * SGU_CHUNK, SGU_CHUNK)


SMALL_GRADS = VEC_PARAMS + ("loss", "w_spatial")


def _small_shape(name):
    if name == "w_spatial":
        return W_SPATIAL_2D
    rows, cols = VEC_SLOTS[name]
    return (rows.stop - rows.start, cols.stop - cols.start)


def _pack_small(dst, parts):
    dst[...] = jnp.zeros(VEC_SHAPE, F32)
    for n, ref in parts.items():
        if n in VEC_SLOTS:
            dst[VEC_SLOTS[n]] = ref[...]


def _small_start(partials):
    names = VEC_PARAMS + ("loss",)

    def body(*refs):
        _pack_small(refs[-1], dict(zip(names, refs[:-1])))

    vec = pl.pallas_call(
        body, in_specs=[VMEM_SPEC] * len(names), out_specs=VMEM_SPEC, out_shape=jax.ShapeDtypeStruct(VEC_SHAPE, F32),
        name="small_params_pack")(*[partials[n].reshape(_small_shape(n)) for n in names])
    srcs = [vec, partials["w_spatial"].reshape(W_SPATIAL_2D)]
    return _split_start("small_params_start", _broadcast_copies, N_PEERS, srcs, [(N_PEERS,) + s.shape for s in srcs])


def _small_finish(started, after, w, m, v):
    own, landed = _split_wait("small_params_wait", _broadcast_copies, *started, after)
    ng, npar = len(SMALL_GRADS), len(SMALL_PARAMS)

    def update_body(*refs):
        vec_own, ws_own, vec_slots, ws_slots = refs[:4]
        w_in, m_in, v_in = (dict(zip(SMALL_PARAMS, refs[4 + k * npar:4 + (k + 1) * npar])) for k in range(3))
        o0 = 4 + 3 * npar
        g_out = dict(zip(SMALL_GRADS, refs[o0:o0 + ng]))
        d_out, m_out, v_out = (dict(zip(SMALL_PARAMS, refs[o0 + ng + k * npar:o0 + ng + (k + 1) * npar])) for k in range(3))
        vg, vw, vm, vv = refs[o0 + ng + 3 * npar:]
        me = 4 * lax.axis_index("x") + 2 * lax.axis_index("y") + lax.axis_index("c")

        def device_sum(mine, slots, read):
            acc = None
            for i in range(N_PEERS + 1):
                k = me ^ i
                part = jnp.where(k == 0, read(mine), read(slots.at[jnp.maximum(k, 1) - 1]))
                acc = part if acc is None else acc + part
            return acc

        vg[...] = device_sum(vec_own, vec_slots, lambda ref: ref[...])
        _pack_small(vw, w_in)
        _pack_small(vm, m_in)
        _pack_small(vv, v_in)
        d_vec, m_vec, v_vec = _adamw_math(vg[...], vw[...], vm[...], vv[...])
        vw[...] = d_vec
        vm[...] = m_vec
        vv[...] = v_vec
        for n in VEC_PARAMS + ("loss",):
            g_out[n][...] = vg[VEC_SLOTS[n]]
        for n in VEC_PARAMS:
            d_out[n][...] = vw[VEC_SLOTS[n]]
            m_out[n][...] = vm[VEC_SLOTS[n]]
            v_out[n][...] = vv[VEC_SLOTS[n]]

        def spatial(r0, ck):
            rows = pl.ds(r0, ck)
            g = device_sum(ws_own, ws_slots, lambda ref: ref[rows, :])
            d_, m_, v_ = _adamw_math(g, w_in["w_spatial"][rows, :], m_in["w_spatial"][rows, :], v_in["w_spatial"][rows, :])
            g_out["w_spatial"][rows, :] = g
            d_out["w_spatial"][rows, :] = d_
            m_out["w_spatial"][rows, :] = m_
            v_out["w_spatial"][rows, :] = v_

        _for_row_chunks(W_SPATIAL_2D[0], spatial)

    ins = list(own) + list(landed)
    for src in (w, m, v):
        ins += [src[n].reshape(_small_shape(n)) for n in SMALL_PARAMS]
    out_shapes = [jax.ShapeDtypeStruct(_small_shape(n), F32) for n in SMALL_GRADS + SMALL_PARAMS * 3]
    outs = pl.pallas_call(
        update_body, in_specs=[VMEM_SPEC] * len(ins), out_specs=[VMEM_SPEC] * len(out_shapes), out_shape=out_shapes,
        scratch_shapes=[pltpu.VMEM(VEC_SHAPE, F32)] * 4, name="small_params_update")(*ins)
    grads = dict(zip(SMALL_GRADS, outs[:ng]))
    rest = [dict(zip(SMALL_PARAMS, outs[ng + k * npar:ng + (k + 1) * npar])) for k in range(3)]
    return grads, rest[0], rest[1], rest[2]


BIG = ("w_in", "w_proj_attn", "w_proj_sgu", "w_out", "w_ffn_gate", "w_ffn_up", "w_ffn_down")
COMM_GROUPS = (("w_in",), ("w_proj_attn", "w_proj_sgu", "w_out", "w_ffn_gate", "w_ffn_up", "w_ffn_down"))
WEIGHTS = ("norm1_g", "w_in", "sgu_ln_g", "sgu_ln_b", "w_spatial", "b_spatial", "w_proj_attn", "w_proj_sgu", "w_out",
           "norm2_g", "w_ffn_gate", "w_ffn_up", "w_ffn_down", "final_g")


def kernel(x, positions, norm1_g, w_in, sgu_ln_g, sgu_ln_b, w_spatial, b_spatial, w_proj_attn, w_proj_sgu, w_out, norm2_g, w_ffn_gate, w_ffn_up, w_ffn_down, final_g, loss_target, m_norm1_g, m_w_in, m_sgu_ln_g, m_sgu_ln_b, m_w_spatial, m_b_spatial, m_w_proj_attn, m_w_proj_sgu, m_w_out, m_norm2_g, m_w_ffn_gate, m_w_ffn_up, m_w_ffn_down, m_final_g, v_norm1_g, v_w_in, v_sgu_ln_g, v_sgu_ln_b, v_w_spatial, v_b_spatial, v_w_proj_attn, v_w_proj_sgu, v_w_out, v_norm2_g, v_w_ffn_gate, v_w_ffn_up, v_w_ffn_down, v_final_g):
    w = dict(norm1_g=norm1_g, w_in=w_in, sgu_ln_g=sgu_ln_g, sgu_ln_b=sgu_ln_b, w_spatial=w_spatial, b_spatial=b_spatial,
             w_proj_attn=w_proj_attn, w_proj_sgu=w_proj_sgu, w_out=w_out, norm2_g=norm2_g, w_ffn_gate=w_ffn_gate,
             w_ffn_up=w_ffn_up, w_ffn_down=w_ffn_down, final_g=final_g)
    m = dict(norm1_g=m_norm1_g, w_in=m_w_in, sgu_ln_g=m_sgu_ln_g, sgu_ln_b=m_sgu_ln_b, w_spatial=m_w_spatial,
             b_spatial=m_b_spatial, w_proj_attn=m_w_proj_attn, w_proj_sgu=m_w_proj_sgu, w_out=m_w_out, norm2_g=m_norm2_g,
             w_ffn_gate=m_w_ffn_gate, w_ffn_up=m_w_ffn_up, w_ffn_down=m_w_ffn_down, final_g=m_final_g)
    v = dict(norm1_g=v_norm1_g, w_in=v_w_in, sgu_ln_g=v_sgu_ln_g, sgu_ln_b=v_sgu_ln_b, w_spatial=v_w_spatial,
             b_spatial=v_b_spatial, w_proj_attn=v_w_proj_attn, w_proj_sgu=v_w_proj_sgu, w_out=v_w_out, norm2_g=v_norm2_g,
             w_ffn_gate=v_w_ffn_gate, w_ffn_up=v_w_ffn_up, w_ffn_down=v_w_ffn_down, final_g=v_final_g)
    t = x.shape[1]

    cast = lambda n, after: _ew(f"cast_{n}", lambda a: (a,), [w[n][0]], [BF16], after)[0]
    shards = {"w_in": cast("w_in", [])}
    late = COMM_GROUPS[1]
    k_in, n_in = shards["w_in"].shape
    *first, token = _split_start("gather_start_0", _gather_half_copies, 3, [shards["w_in"]], [(3, k_in // 2, n_in)])
    shards.update({n: cast(n, [token]) for n in late})
    pending = {}

    def first_weight(after):
        srcs, filled = _split_wait("gather_wait_0", _gather_half_copies, *first, list(after) + [shards[n] for n in late])
        gath_in, late_shards = lax.optimization_barrier(
            (_gather_finish("gather_finish_0", srcs[0], filled[0]), [shards[n] for n in late]))
        land_shapes = [(s.shape[0], N_CHIPS * s.shape[1]) if n.startswith("w_proj") else (N_CHIPS,) + s.shape
                       for n, s in zip(late, late_shards)]
        *pending["late"], _ = _split_start("gather_start_1", _gather_copies, 4, late_shards, land_shapes)
        return gath_in

    def late_weights(after):
        _, filled = _split_wait("gather_wait_1", _gather_copies, *pending["late"], after)
        gath = dict(zip(late, filled))
        return (gath["w_proj_attn"], gath["w_proj_sgu"],
                gath["w_out"].reshape(D_MODEL, D_MODEL), gath["w_ffn_gate"], gath["w_ffn_up"], gath["w_ffn_down"])

    exchanges = {}

    def on_grads(i, partials):
        if "w_out" in partials:
            partials["w_out"] = partials["w_out"].reshape(N_CHIPS, D_MODEL // N_CHIPS, D_MODEL)
        parts = [partials[n] for n in COMM_GROUPS[i]]
        *exchanges[i], started = _split_start(
            f"rs_exchange_start_{i}", _exchange_copies, N_PEERS, parts, [(N_PEERS,) + _piece_shape(p.shape) for p in parts])
        return started

    dx, _, small = _local_step(
        x[0], positions.reshape(t, 1), loss_target[0], norm1_g + token, sgu_ln_g, sgu_ln_b,
        w_spatial[0], b_spatial[0] + token[:1, :LANES],
        norm2_g, final_g.reshape(1, D_MODEL), first_weight, late_weights, on_grads=on_grads)
    *small_started, small_token = _small_start(small)

    grads = {}
    for i in (1, 0):
        parts, filled = _split_wait(f"rs_exchange_wait_{i}", _exchange_copies, *exchanges[i], small_token)
        grads.update(zip(COMM_GROUPS[i], _device_sum(f"rs_device_sum_{i}", parts, filled)))

    delta, new_m, new_v, updated = {}, {}, {}, []
    for n in BIG:
        shp = w[n].shape
        flip = jnp.transpose if shp[-1] % LANES else (lambda a: a)
        outs = _adamw(f"adamw_{n}", flip(grads[n]), flip(w[n][0]), flip(m[n][0]), flip(v[n][0]))
        grads[n], delta[n], new_m[n], new_v[n] = (flip(a).reshape(shp) for a in outs)
        updated.append(outs[-1])

    g_s, d_s, m_s, v_s = _small_finish(small_started, updated, w, m, v)
    loss = g_s["loss"][0, 0]
    for n in SMALL_PARAMS:
        shp = w[n].shape
        grads[n], delta[n], new_m[n], new_v[n] = (a[n].reshape(shp) for a in (g_s, d_s, m_s, v_s))

    return (loss, dx.reshape(x.shape), *[grads[n] for n in WEIGHTS], *[delta[n] for n in WEIGHTS],
            *[new_m[n] for n in WEIGHTS], *[new_v[n] for n in WEIGHTS])
```

```python
import functools

import numpy as np
import jax
import jax.numpy as jnp
from jax import lax
from jax.experimental import pallas as pl
from jax.experimental.pallas import tpu as pltpu

F32, BF16 = jnp.float32, jnp.bfloat16
MESH = pl.DeviceIdType.MESH

D_MODEL = 1024
HEAD_DIM = 64
ATTN_W = 512
DILATIONS = (1, 4, 16)
BLK = 128
ATTN_BLOCKS_PER_STEP = 4
ROPE_DIM = 16
ROPE_THETA = 500000.0
SGU_W = 512
SGU_CHUNK = 128
SGU_GROUPS = 8
D_FF = 2816
N_CHIPS = 4
FF_SHARD = D_FF // N_CHIPS
IN_COLS = 7680
EPS = 1e-6
NEG = -1e30
LANES = 128
VMEM_LIMIT = 52 * 1024 * 1024

ADAM_LR, ADAM_B1, ADAM_B2, ADAM_EPS, ADAM_WD, ADAM_STEP = 0.001, 0.9, 0.999, 1e-08, 0.01, 10

QKV_BLOCKS = 9


def _w_in_block(part, g):
    return part * len(DILATIONS) + g


def _cparams(ngrid):
    return pltpu.CompilerParams(dimension_semantics=("arbitrary",) * ngrid, vmem_limit_bytes=VMEM_LIMIT)


def _full(shape):
    return pl.BlockSpec(shape, lambda *_: (0,) * len(shape))


def _resident(shape):
    return pl.BlockSpec(shape, lambda *_: (0,) * len(shape), pipeline_mode=pl.Buffered(1))


NT = ((1,), (1,))
TN = ((0,), (0,))


def _rope(v, cos_t, sin_t):
    half = ROPE_DIM // 2
    first = (lax.broadcasted_iota(jnp.int32, cos_t.shape, 1) % HEAD_DIM) < half
    outs = []
    for cs in range(v.shape[1] // LANES):
        x = v[:, cs * LANES:(cs + 1) * LANES]
        partner = jnp.where(first, pltpu.roll(x, LANES - half, axis=1), pltpu.roll(x, half, axis=1))
        outs.append(x * cos_t + partner * sin_t)
    return outs[0] if len(outs) == 1 else jnp.concatenate(outs, axis=1)


def _spread_heads(v2, upper):
    other = pltpu.roll(v2, HEAD_DIM, axis=1)
    h0 = jnp.where(upper, other, v2)
    h1 = jnp.where(upper, v2, other)
    return jnp.concatenate([jnp.concatenate([h0, h0], axis=1), jnp.concatenate([h1, h1], axis=1)], axis=0)


def _sigmoid(v):
    return 0.5 * jnp.tanh(0.5 * v) + 0.5


def _rms_stats(v):
    r = lax.rsqrt(jnp.mean(v * v, axis=-1, keepdims=True) + EPS)
    return v * r, r


def _rms_bwd(dy, xhat, r, g):
    dxh = dy * g
    return r * (dxh - xhat * jnp.mean(dxh * xhat, axis=-1, keepdims=True))


def _head_sum_matrix():
    idx = np.arange(ATTN_W) // HEAD_DIM
    return jnp.asarray((idx[:, None] == idx[None, :]).astype(np.float32), dtype=BF16)


def _group_sum(v, e):
    hi = v.astype(BF16)
    lo = (v - hi.astype(F32)).astype(BF16)
    return jnp.dot(hi, e, preferred_element_type=F32) + jnp.dot(lo, e, preferred_element_type=F32)


TILE = 512


def _to_slabs(slab_ref, v):
    for cs in range(slab_ref.shape[0]):
        slab_ref[cs] = v[:, cs * LANES:(cs + 1) * LANES]


def _from_slabs(slab_ref):
    return jnp.concatenate([slab_ref[cs] for cs in range(slab_ref.shape[0])], axis=1)


def _class_rows(slab_ref, r, dil):
    n = slab_ref.shape[1] // dil
    return jnp.concatenate([slab_ref.at[cs][pl.ds(r, n, stride=dil), :] for cs in range(slab_ref.shape[0])], axis=1)


def _put_class_rows(slab_ref, r, dil, v):
    n = slab_ref.shape[1] // dil
    for cs in range(slab_ref.shape[0]):
        slab_ref.at[cs][pl.ds(r, n, stride=dil), :] = v[:, cs * LANES:(cs + 1) * LANES]


def _natural_from_group(slab_ref, grp_ref):
    dil = grp_ref.shape[0]
    for r in range(dil):
        _put_class_rows(slab_ref, r, dil, grp_ref[r].astype(F32))
    return _from_slabs(slab_ref)


def _group_from_natural(slab_ref, grp_ref, v):
    dil = grp_ref.shape[0]
    _to_slabs(slab_ref, v)
    for r in range(dil):
        grp_ref[r] = _class_rows(slab_ref, r, dil).astype(grp_ref.dtype)


def _group_spec(dil, tile, width):
    return pl.BlockSpec((dil, tile // dil, width), lambda i, *_: (0, i, 0))


def _slabs(tile, width):
    return pltpu.VMEM((width // LANES, tile, LANES), F32)


def _rope_consts():
    lane = np.arange(LANES) % HEAD_DIM
    fi = lane % (ROPE_DIM // 2)
    invf = np.where(lane < ROPE_DIM, ROPE_THETA ** (-(2.0 * fi) / ROPE_DIM), 0.0)
    sgn = np.where(lane < ROPE_DIM // 2, -1.0, np.where(lane < ROPE_DIM, 1.0, 0.0))
    return (jnp.asarray(invf.astype(np.float32)).reshape(1, LANES), jnp.asarray(sgn.astype(np.float32)).reshape(1, LANES))


def _rope_tables(pos_col):
    t = pos_col.shape[0]
    tile = min(t, TILE)
    invf, sgn = _rope_consts()

    def body(p_ref, f_ref, s_ref, c0, s0, c1, s1, c2, s2, slab_c, slab_s):
        ang = p_ref[...].astype(F32) * f_ref[...]
        cos, sin = jnp.cos(ang), jnp.sin(ang) * s_ref[...]
        c0[...] = cos
        s0[...] = sin
        _group_from_natural(slab_c, c1, cos)
        _group_from_natural(slab_s, s1, sin)
        for r in range(DILATIONS[2]):
            c2[r] = _class_rows(slab_c, r, DILATIONS[2])
            s2[r] = _class_rows(slab_s, r, DILATIONS[2])

    nat = pl.BlockSpec((tile, LANES), lambda i: (i, 0))
    specs, shapes = [nat, nat], [(t, LANES)] * 2
    for d in DILATIONS[1:]:
        specs += [_group_spec(d, tile, LANES)] * 2
        shapes += [(d, t // d, LANES)] * 2
    outs = pl.pallas_call(
        body, grid=(t // tile,),
        in_specs=[pl.BlockSpec((tile, 1), lambda i: (i, 0)), _full((1, LANES)), _full((1, LANES))],
        out_specs=specs, out_shape=[jax.ShapeDtypeStruct(s, F32) for s in shapes],
        scratch_shapes=[_slabs(tile, LANES)] * 2,
        compiler_params=_cparams(1), name="rope_tables")(pos_col, invf, sgn)
    return [(outs[2 * g].reshape(t, LANES), outs[2 * g + 1].reshape(t, LANES)) for g in range(len(DILATIONS))]


def _norm_fwd(x, g):
    t = x.shape[0]
    tile = min(t, TILE)

    def body(x_ref, g_ref, h0_ref, h1_ref, h2_ref, slab):
        xhat, _ = _rms_stats(x_ref[...])
        hn = xhat * g_ref[...]
        h0_ref[...] = hn.astype(BF16)
        _group_from_natural(slab, h1_ref, hn)
        for r in range(DILATIONS[2]):
            h2_ref[r] = _class_rows(slab, r, DILATIONS[2]).astype(BF16)

    nat = pl.BlockSpec((tile, D_MODEL), lambda i: (i, 0))
    return pl.pallas_call(
        body, grid=(t // tile,),
        in_specs=[nat, _full((1, D_MODEL))],
        out_specs=[nat] + [_group_spec(d, tile, D_MODEL) for d in DILATIONS[1:]],
        out_shape=[jax.ShapeDtypeStruct((t, D_MODEL), BF16)]
        + [jax.ShapeDtypeStruct((d, t // d, D_MODEL), BF16) for d in DILATIONS[1:]],
        scratch_shapes=[_slabs(tile, D_MODEL)],
        compiler_params=_cparams(1), name="norm1_fwd")(x, g)


GU_COLS = 3072
GROUP_COLS = 1536
GU_HALF = GU_COLS // 2


def _w_in_spec(width, block):
    return pl.BlockSpec((D_MODEL, width), lambda i: (0, block), pipeline_mode=pl.Buffered(1))


def _gu_w_specs():
    first = QKV_BLOCKS * ATTN_W // GU_HALF
    return [_w_in_spec(GU_HALF, first), _w_in_spec(GU_HALF, first + 1)]


def _group_w_specs(g):
    return [_w_in_spec(ATTN_W, _w_in_block(part, g)) for part in range(3)]


def _in_proj(hs, w_in, tables):
    t = hs[0].shape[0]
    tm = min(t, 1024)

    def body_gu(h_ref, w0_ref, w1_ref, o_ref):
        h = h_ref[...]
        o_ref[:, 0:GU_HALF] = jnp.dot(h, w0_ref[...], preferred_element_type=F32).astype(BF16)
        o_ref[:, GU_HALF:] = jnp.dot(h, w1_ref[...], preferred_element_type=F32).astype(BF16)

    gu = _token_call("in_proj_gates_uv", body_gu, t, tm,
                     [(hs[0], _rows_spec(tm, D_MODEL))] + [(w_in, s) for s in _gu_w_specs()],
                     [((t, GU_COLS), BF16, _rows_spec(tm, GU_COLS))])[0]

    qkvs = []
    for g in range(len(DILATIONS)):

        def body_qkv(h_ref, wq_ref, wk_ref, wv_ref, cos_ref, sin_ref, o_ref):
            h = h_ref[...]
            cos_w, sin_w = cos_ref[...], sin_ref[...]
            q = jnp.dot(h, wq_ref[...], preferred_element_type=F32)
            o_ref[:, 0:ATTN_W] = (_rope(q, cos_w, sin_w) * HEAD_DIM ** -0.5).astype(BF16)
            k = jnp.dot(h, wk_ref[...], preferred_element_type=F32)
            o_ref[:, ATTN_W:2 * ATTN_W] = _rope(k, cos_w, sin_w).astype(BF16)
            o_ref[:, 2 * ATTN_W:] = jnp.dot(h, wv_ref[...], preferred_element_type=F32).astype(BF16)

        cos_t, sin_t = tables[g]
        qkvs.append(_token_call(
            f"in_proj_qkv_g{g}", body_qkv, t, tm,
            [(hs[g].reshape(t, D_MODEL), _rows_spec(tm, D_MODEL))] + [(w_in, s) for s in _group_w_specs(g)]
            + [(cos_t, _rows_spec(tm, LANES)), (sin_t, _rows_spec(tm, LANES))],
            [((t, GROUP_COLS), BF16, _rows_spec(tm, GROUP_COLS))])[0])
    return gu, qkvs


def _attn_masks(n):
    row = lax.broadcasted_iota(jnp.int32, (2 * BLK, 2 * BLK), 0) % BLK
    col = lax.broadcasted_iota(jnp.int32, (2 * BLK, 2 * BLK), 1)
    diff = BLK + row - col
    valid = (diff >= 0) & (diff <= BLK) & ((col >= BLK) | (n > 0))
    upper = lax.broadcasted_iota(jnp.int32, (BLK, LANES), 1) >= HEAD_DIM
    return valid, upper


def _stack_heads(v2, upper):
    zero = jnp.zeros_like(v2)
    return jnp.concatenate([jnp.where(upper, zero, v2), jnp.where(upper, v2, zero)], axis=0)


def _unstack_heads(v, upper):
    return jnp.where(upper, v[BLK:], v[:BLK])


def _attn_fwd(qkv, g, dil):
    t = qkv.shape[0]
    length = t // dil
    nb = length // BLK
    per_step = min(nb, ATTN_BLOCKS_PER_STEP)
    view = qkv.reshape(dil, length, GROUP_COLS)

    def body(q_ref, kc_ref, kp_ref, vc_ref, vp_ref, o_ref, l_ref, kwin, vwin):
        n = pl.program_id(1)
        kwin[0:BLK] = kp_ref[...]
        kwin[BLK:] = kc_ref[...]
        vwin[0:BLK] = vp_ref[...]
        vwin[BLK:] = vc_ref[...]

        def block(b, carry):
            valid, upper = _attn_masks(n * per_step + b)
            rows = pl.ds(pl.multiple_of(b * BLK, BLK), BLK)
            window = pl.ds(pl.multiple_of(b * BLK, BLK), 2 * BLK)
            slabs = [slice(p * LANES, (p + 1) * LANES) for p in range(ATTN_W // LANES)]
            ss = [lax.dot_general(_stack_heads(q_ref[rows, sl], upper), kwin[window, sl], (NT, ((), ())),
                                  preferred_element_type=F32) for sl in slabs]
            soft = []
            for s in ss:
                s = jnp.where(valid, s, NEG)
                m = jnp.max(s, axis=1, keepdims=True)
                pe = jnp.exp(s - m)
                soft.append((m, pe, jnp.sum(pe, axis=1, keepdims=True)))
            for sl, (m, pe, den) in zip(slabs, soft):
                o = jnp.dot(pe.astype(BF16), vwin[window, sl], preferred_element_type=F32) / den
                lse = jnp.broadcast_to(m + jnp.log(den), (2 * BLK, LANES))
                o_ref[rows, sl] = _unstack_heads(o, upper).astype(BF16)
                l_ref[rows, sl] = _unstack_heads(lse, upper)
            return carry

        lax.fori_loop(0, per_step, block, 0)

    rows = per_step * BLK
    cur = lambda part: pl.BlockSpec((None, rows, ATTN_W), lambda r, n: (r, n, part))
    prev = lambda part: pl.BlockSpec((None, BLK, ATTN_W), lambda r, n: (r, jnp.maximum(n * per_step - 1, 0), part))
    out_spec = pl.BlockSpec((None, rows, ATTN_W), lambda r, n: (r, n, 0))
    return pl.pallas_call(
        body, grid=(dil, nb // per_step),
        in_specs=[cur(0), cur(1), prev(1), cur(2), prev(2)],
        out_specs=[out_spec, out_spec],
        out_shape=[jax.ShapeDtypeStruct((dil, length, ATTN_W), BF16), jax.ShapeDtypeStruct((dil, length, ATTN_W), F32)],
        scratch_shapes=[pltpu.VMEM((rows + BLK, ATTN_W), BF16)] * 2,
        compiler_params=_cparams(2), name=f"attn_fwd_g{g}")(view, view, view, view, view)


def _alphas(l0, l1, l2):
    m = jnp.maximum(jnp.maximum(l0, l1), l2)
    e0, e1, e2 = jnp.exp(l0 - m), jnp.exp(l1 - m), jnp.exp(l2 - m)
    inv = 1.0 / (e0 + e1 + e2)
    return e0 * inv, e1 * inv, e2 * inv


def _natural_group_values(o_refs, l_refs, slabs):
    os_ = [o_refs[0][0].astype(F32)] + [_natural_from_group(slabs[2 * g - 2], o_refs[g]) for g in (1, 2)]
    ls_ = [l_refs[0][0]] + [_natural_from_group(slabs[2 * g - 1], l_refs[g]) for g in (1, 2)]
    return os_, ls_


def _combine_fwd(os_, ls_):
    t = os_[0].shape[1]
    tile = min(t, TILE)

    def body(o0, o1, o2, l0, l1, l2, a_ref, *slabs):
        ov, lv = _natural_group_values((o0, o1, o2), (l0, l1, l2), slabs)
        a0, a1, a2 = _alphas(*lv)
        a_ref[...] = (a0 * ov[0] + a1 * ov[1] + a2 * ov[2]).astype(BF16)

    specs = [_group_spec(d, tile, ATTN_W) for d in DILATIONS]
    return pl.pallas_call(
        body, grid=(t // tile,), in_specs=specs * 2, out_specs=pl.BlockSpec((tile, ATTN_W), lambda i: (i, 0)),
        out_shape=jax.ShapeDtypeStruct((t, ATTN_W), BF16),
        scratch_shapes=[_slabs(tile, ATTN_W)] * 4,
        compiler_params=_cparams(1), name="combine_fwd")(*os_, *ls_)


def _combine_bwd(dattn, os_, ls_):
    t = dattn.shape[0]
    tile = min(t, TILE)
    e = _head_sum_matrix()

    def body(d_ref, o0, o1, o2, l0, l1, l2, e_ref, do0, do1, do2, c0, c1, c2, *slabs):
        ov, lv = _natural_group_values((o0, o1, o2), (l0, l1, l2), slabs)
        alphas = _alphas(*lv)
        d = d_ref[...]
        attn = alphas[0] * ov[0] + alphas[1] * ov[1] + alphas[2] * ov[2]
        s = _group_sum(d * attn, e_ref[...])
        do0[0] = (alphas[0] * d).astype(BF16)
        c0[0] = -alphas[0] * s
        for g, do_ref, c_ref in ((1, do1, c1), (2, do2, c2)):
            _group_from_natural(slabs[2 * g - 2], do_ref, alphas[g] * d)
            _group_from_natural(slabs[2 * g - 1], c_ref, -alphas[g] * s)

    specs = [_group_spec(d, tile, ATTN_W) for d in DILATIONS]
    shapes = [(d, t // d, ATTN_W) for d in DILATIONS]
    outs = pl.pallas_call(
        body, grid=(t // tile,),
        in_specs=[pl.BlockSpec((tile, ATTN_W), lambda i: (i, 0))] + specs * 2 + [_full((ATTN_W, ATTN_W))],
        out_specs=specs * 2,
        out_shape=[jax.ShapeDtypeStruct(s, BF16) for s in shapes] + [jax.ShapeDtypeStruct(s, F32) for s in shapes],
        scratch_shapes=[_slabs(tile, ATTN_W)] * 4,
        compiler_params=_cparams(1), name="combine_bwd")(dattn, *os_, *ls_, e)
    return outs[:3], outs[3:]


def _attn_bwd(qkv, do, cc, lse, cos_t, sin_t, g, dil):
    t = qkv.shape[0]
    length = t // dil
    nb = length // BLK
    per_step = min(nb, ATTN_BLOCKS_PER_STEP)
    nsteps = nb // per_step
    rows_per_step = per_step * BLK
    qkv_v = qkv.reshape(dil, length, GROUP_COLS)
    cos_v, sin_v = (a.reshape(dil, length, LANES) for a in (cos_t, sin_t))
    scale = HEAD_DIM ** -0.5
    dq_cols, dk_cols, dv_cols = (slice(i * ATTN_W, (i + 1) * ATTN_W) for i in range(3))

    def body(q_ref, kc_ref, kp_ref, vc_ref, vp_ref, do_ref, c_ref, l_ref, cosc, sinc, cosp, sinp,
             out_ref, acc, kwin, vwin, cwin, swin):
        n = pl.program_id(1)

        def one_block(b):
            valid, upper = _attn_masks(n * per_step + b)
            start = b * BLK if isinstance(b, int) else pl.multiple_of(b * BLK, BLK)
            rows, before, window = pl.ds(start, BLK), pl.ds(start, BLK), pl.ds(start, 2 * BLK)
            own = pl.ds(start + BLK, BLK)
            dq_parts, dkp_parts, dkc_parts, dvp_parts, dvc_parts = [], [], [], [], []
            npairs = ATTN_W // LANES
            slabs = [slice(p * LANES, (p + 1) * LANES) for p in range(npairs)]
            qss = [_stack_heads(q_ref[rows, sl], upper) for sl in slabs]
            doss = [_stack_heads(do_ref[rows, sl], upper) for sl in slabs]
            ss = [lax.dot_general(qss[p], kwin[window, slabs[p]], (NT, ((), ())), preferred_element_type=F32) for p in range(npairs)]
            dpvs = [lax.dot_general(doss[p], vwin[window, slabs[p]], (NT, ((), ())), preferred_element_type=F32)
                    for p in range(npairs)]
            pes = [jnp.exp(jnp.where(valid, ss[p], NEG) - _spread_heads(l_ref[rows, slabs[p]], upper)) for p in range(npairs)]
            dss = [(pes[p] * (dpvs[p] + _spread_heads(c_ref[rows, slabs[p]], upper))).astype(BF16) for p in range(npairs)]
            for p in range(npairs):
                qs, dos, ds = qss[p], doss[p], dss[p]
                dq2 = _unstack_heads(jnp.dot(ds, kwin[window, slabs[p]], preferred_element_type=F32), upper)
                dk2 = lax.dot_general(ds, qs, (TN, ((), ())), preferred_element_type=F32)
                dv2 = lax.dot_general(pes[p].astype(BF16), dos, (TN, ((), ())), preferred_element_type=F32)
                dq_parts.append(dq2)
                dkp_parts.append(dk2[:BLK])
                dkc_parts.append(dk2[BLK:])
                dvp_parts.append(dv2[:BLK])
                dvc_parts.append(dv2[BLK:])
            dq = _rope(jnp.concatenate(dq_parts, axis=1) * scale, cwin[own, :], swin[own, :])
            dkc = _rope(jnp.concatenate(dkc_parts, axis=1), cwin[own, :], swin[own, :])
            dkp = _rope(jnp.concatenate(dkp_parts, axis=1), cwin[before, :], swin[before, :])
            return dq, dkp, dkc, jnp.concatenate(dvp_parts, axis=1), jnp.concatenate(dvc_parts, axis=1)

        @pl.when(n < nsteps)
        def _():
            kwin[0:BLK] = kp_ref[...]
            kwin[BLK:] = kc_ref[...]
            vwin[0:BLK] = vp_ref[...]
            vwin[BLK:] = vc_ref[...]
            cwin[0:BLK] = cosp[...]
            cwin[BLK:] = cosc[...]
            swin[0:BLK] = -sinp[...]
            swin[BLK:] = -sinc[...]
            dq, dkp, dkc, dvp, dvc = one_block(0)
            last = slice(rows_per_step - BLK, rows_per_step)

            @pl.when(n > 0)
            def _():
                if per_step > 1:
                    out_ref[0:rows_per_step - BLK, :] = acc[0:rows_per_step - BLK, :].astype(BF16)
                out_ref[last, dq_cols] = acc[last, dq_cols].astype(BF16)
                out_ref[last, dk_cols] = (acc[last, dk_cols] + dkp).astype(BF16)
                out_ref[last, dv_cols] = (acc[last, dv_cols] + dvp).astype(BF16)

            acc[0:BLK, dq_cols] = dq
            acc[0:BLK, dk_cols] = dkc
            acc[0:BLK, dv_cols] = dvc

            def later(b, carry):
                dq, dkp, dkc, dvp, dvc = one_block(b)
                start = pl.multiple_of(b * BLK, BLK)
                before, rows = pl.ds(start - BLK, BLK), pl.ds(start, BLK)
                acc[before, dk_cols] += dkp
                acc[before, dv_cols] += dvp
                acc[rows, dq_cols] = dq
                acc[rows, dk_cols] = dkc
                acc[rows, dv_cols] = dvc
                return carry

            lax.fori_loop(1, per_step, later, 0)

        @pl.when(n == flush_at)
        def _():
            out_ref[...] = acc[...].astype(BF16)

    flush_at = nsteps - 1 if nsteps == 1 else nsteps
    out_lag = 0 if nsteps == 1 else 1
    nc = lambda n: jnp.minimum(n, nsteps - 1)
    npv = lambda n: jnp.maximum(jnp.minimum(n, nsteps - 1) * per_step - 1, 0)
    cur = lambda part: pl.BlockSpec((None, rows_per_step, ATTN_W), lambda r, n: (r, nc(n), part))
    prev = lambda part: pl.BlockSpec((None, BLK, ATTN_W), lambda r, n: (r, npv(n), part))
    row = pl.BlockSpec((None, rows_per_step, ATTN_W), lambda r, n: (r, nc(n), 0))
    tab_c = pl.BlockSpec((None, rows_per_step, LANES), lambda r, n: (r, nc(n), 0))
    tab_p = pl.BlockSpec((None, BLK, LANES), lambda r, n: (r, npv(n), 0))
    out_spec = pl.BlockSpec((None, rows_per_step, GROUP_COLS), lambda r, n: (r, jnp.maximum(n - out_lag, 0), 0))
    out = pl.pallas_call(
        body, grid=(dil, nsteps + out_lag),
        in_specs=[cur(0), cur(1), prev(1), cur(2), prev(2), row, row, row, tab_c, tab_c, tab_p, tab_p],
        out_specs=out_spec,
        out_shape=jax.ShapeDtypeStruct((dil, length, GROUP_COLS), BF16),
        scratch_shapes=[pltpu.VMEM((rows_per_step, GROUP_COLS), F32)]
        + [pltpu.VMEM((rows_per_step + BLK, ATTN_W), BF16)] * 2 + [pltpu.VMEM((rows_per_step + BLK, LANES), F32)] * 2,
        compiler_params=_cparams(2), name=f"attn_bwd_g{g}")(
            qkv_v, qkv_v, qkv_v, qkv_v, qkv_v, do, cc, lse, cos_v, sin_v, cos_v, sin_v)
    return out.reshape(t, GROUP_COLS)


SQRT_HALF = 0.7071067811865476
INV_SQRT_2PI = 0.3989422804014327


def _sgu_core(uv, g, b, w_ref, bias):
    cdf = 0.5 * (1.0 + lax.erf(uv * SQRT_HALF))
    z = uv * cdf
    u, v = z[:, :SGU_W], z[:, SGU_W:]
    mu = jnp.mean(v, axis=1, keepdims=True)
    xc = v - mu
    rs = lax.rsqrt(jnp.mean(xc * xc, axis=1, keepdims=True) + EPS)
    xhat = xc * rs
    vn = xhat * g + b
    row = lax.broadcasted_iota(jnp.int32, (SGU_CHUNK, SGU_CHUNK), 0)
    col = lax.broadcasted_iota(jnp.int32, (SGU_CHUNK, SGU_CHUNK), 1)
    tril = row >= col
    upper = lax.broadcasted_iota(jnp.int32, (SGU_CHUNK, LANES), 1) >= SGU_W // SGU_GROUPS
    ws, vlo, vhi, mixed = [], [], [], []
    for pr in range(SGU_W // LANES):
        sl = slice(pr * LANES, (pr + 1) * LANES)
        w0 = jnp.where(tril, w_ref[2 * pr], 0.0).astype(BF16)
        w1 = jnp.where(tril, w_ref[2 * pr + 1], 0.0).astype(BF16)
        vn2 = vn[:, sl]
        lo = jnp.where(upper, 0.0, vn2).astype(BF16)
        hi = jnp.where(upper, vn2, 0.0).astype(BF16)
        mixed.append(jnp.dot(w0, lo, preferred_element_type=F32) + jnp.dot(w1, hi, preferred_element_type=F32)
                     + bias[:, sl])
        ws.append((w0, w1))
        vlo.append(lo)
        vhi.append(hi)
    return cdf, u, xhat, rs, jnp.concatenate(mixed, axis=1), ws, vlo, vhi, tril, upper


SGU_STEP = 4 * SGU_CHUNK


def _for_chunks(step_rows, fn):
    def one(ci, carry):
        fn(pl.ds(pl.multiple_of(ci * SGU_CHUNK, SGU_CHUNK), SGU_CHUNK))
        return carry

    lax.fori_loop(0, step_rows // SGU_CHUNK, one, 0)


def _sgu_fwd(gu, ln_g, ln_b, w_s, bias_exp):
    t = gu.shape[0]
    step = min(t, SGU_STEP)

    def body(uv_ref, g_ref, b_ref, w_ref, bias_ref, o_ref):
        def chunk(rows):
            _, u, _, _, mixed, *_ = _sgu_core(uv_ref[rows, :].astype(F32), g_ref[...], b_ref[...], w_ref, bias_ref[...])
            o_ref[rows, :] = (u * mixed).astype(BF16)

        _for_chunks(step, chunk)

    return pl.pallas_call(
        body, grid=(t // step,),
        in_specs=[pl.BlockSpec((step, 2 * SGU_W), lambda n: (n, 0)), _full((1, SGU_W)), _full((1, SGU_W)),
                  _full((SGU_GROUPS, SGU_CHUNK, SGU_CHUNK)), _full((SGU_CHUNK, SGU_W))],
        out_specs=pl.BlockSpec((step, SGU_W), lambda n: (n, 0)),
        out_shape=jax.ShapeDtypeStruct((t, SGU_W), BF16),
        compiler_params=_cparams(1), name="sgu_fwd")(gu, ln_g, ln_b, w_s, bias_exp)


def _sgu_bwd(dproj, gu, dsgu, ln_g, ln_b, w_s, bias_exp):
    t = gu.shape[0]
    step = min(t, SGU_STEP)
    nsteps = t // step
    e = _head_sum_matrix()

    def body(dp_in, uv_ref, ds_ref, g_ref, b_ref, w_ref, bias_ref, e_ref, out_ref, dw_ref, dbias_ref, dg_ref, db_ref):
        n = pl.program_id(0)

        @pl.when(n == 0)
        def _():
            dw_ref[...] = jnp.zeros(dw_ref.shape, F32)
            dbias_ref[...] = jnp.zeros(dbias_ref.shape, F32)
            dg_ref[...] = jnp.zeros(dg_ref.shape, F32)
            db_ref[...] = jnp.zeros(db_ref.shape, F32)

        _for_chunks(step, functools.partial(chunk, uv_ref, ds_ref, g_ref, b_ref, w_ref, bias_ref, out_ref, dw_ref, dbias_ref,
                                            dg_ref, db_ref))

        @pl.when(n == nsteps - 1)
        def _():
            dbias_ref[...] = _group_sum(dbias_ref[...], e_ref[...])

    def chunk(uv_ref, ds_ref, g_ref, b_ref, w_ref, bias_ref, out_ref, dw_ref, dbias_ref, dg_ref, db_ref, rows):
        uv = uv_ref[rows, :].astype(F32)
        g = g_ref[...]
        cdf, u, xhat, rs, mixed, ws, vlo, vhi, tril, upper = _sgu_core(uv, g, b_ref[...], w_ref, bias_ref[...])
        dsg = ds_ref[rows, :]
        du = dsg * mixed
        dmixed = dsg * u
        dbias_ref[...] += dmixed
        dvn = []
        for pr in range(SGU_W // LANES):
            sl = slice(pr * LANES, (pr + 1) * LANES)
            dm2 = dmixed[:, sl]
            dlo = jnp.where(upper, 0.0, dm2).astype(BF16)
            dhi = jnp.where(upper, dm2, 0.0).astype(BF16)
            w0, w1 = ws[pr]
            dvn.append(lax.dot_general(w0, dlo, (TN, ((), ())), preferred_element_type=F32)
                       + lax.dot_general(w1, dhi, (TN, ((), ())), preferred_element_type=F32))
            dw0 = lax.dot_general(dlo, vlo[pr], (NT, ((), ())), preferred_element_type=F32)
            dw1 = lax.dot_general(dhi, vhi[pr], (NT, ((), ())), preferred_element_type=F32)
            dw_ref[2 * pr] += jnp.where(tril, dw0, 0.0)
            dw_ref[2 * pr + 1] += jnp.where(tril, dw1, 0.0)
        dvn = jnp.concatenate(dvn, axis=1)
        dg_ref[...] += jnp.sum(dvn * xhat, axis=0, keepdims=True)
        db_ref[...] += jnp.sum(dvn, axis=0, keepdims=True)
        dxh = dvn * g
        dv = rs * (dxh - jnp.mean(dxh, axis=1, keepdims=True) - xhat * jnp.mean(dxh * xhat, axis=1, keepdims=True))
        dz = jnp.concatenate([du, dv], axis=1)
        dgelu = cdf + uv * (INV_SQRT_2PI * jnp.exp(-0.5 * uv * uv))
        out_ref[rows, :] = (dz * dgelu).astype(BF16)

    outs = pl.pallas_call(
        body, grid=(nsteps,),
        in_specs=[pl.BlockSpec(memory_space=pl.ANY), pl.BlockSpec((step, 2 * SGU_W), lambda n: (n, 0)),
                  pl.BlockSpec((step, SGU_W), lambda n: (n, 0)), _full((1, SGU_W)), _full((1, SGU_W)),
                  _full((SGU_GROUPS, SGU_CHUNK, SGU_CHUNK)), _full((SGU_CHUNK, SGU_W)), _full((ATTN_W, ATTN_W))],
        out_specs=[pl.BlockSpec((step, 2 * SGU_W), lambda n: (n, 0)), _full((SGU_GROUPS, SGU_CHUNK, SGU_CHUNK)),
                   _full((SGU_CHUNK, SGU_W)), _full((1, SGU_W)), _full((1, SGU_W))],
        out_shape=[jax.ShapeDtypeStruct(dproj.shape, BF16), jax.ShapeDtypeStruct((SGU_GROUPS, SGU_CHUNK, SGU_CHUNK), F32),
                   jax.ShapeDtypeStruct((SGU_CHUNK, SGU_W), F32), jax.ShapeDtypeStruct((1, SGU_W), F32),
                   jax.ShapeDtypeStruct((1, SGU_W), F32)],
        input_output_aliases={0: 0},
        compiler_params=_cparams(1), name="sgu_bwd")(dproj, gu, dsgu, ln_g, ln_b, w_s, bias_exp, e)
    return outs


def _merge_fwd(attn, sgu, gu, x, w_pa, w_ps, w_out, g2):
    t = x.shape[0]
    tm = min(t, 512)

    def body(a_ref, s_ref, ga_ref, gb_ref, x_ref, wpa, wps, wo, g_ref, pa_ref, ps_ref, m_ref, x1_ref, h2_ref):
        pa = jnp.dot(a_ref[...], wpa[...], preferred_element_type=F32)
        ps = jnp.dot(s_ref[...], wps[...], preferred_element_type=F32)
        merged = (_sigmoid(ga_ref[...].astype(F32)) * pa + _sigmoid(gb_ref[...].astype(F32)) * ps).astype(BF16)
        x1 = x_ref[...] + jnp.dot(merged, wo[...], preferred_element_type=F32)
        xhat, _ = _rms_stats(x1)
        pa_ref[...] = pa.astype(BF16)
        ps_ref[...] = ps.astype(BF16)
        m_ref[...] = merged
        x1_ref[...] = x1
        h2_ref[...] = (xhat * g_ref[...]).astype(BF16)

    half = pl.BlockSpec((tm, ATTN_W), lambda i: (i, 0))
    full = pl.BlockSpec((tm, D_MODEL), lambda i: (i, 0))
    return pl.pallas_call(
        body, grid=(t // tm,),
        in_specs=[half, half, pl.BlockSpec((tm, D_MODEL), lambda i: (i, 1)), pl.BlockSpec((tm, D_MODEL), lambda i: (i, 2)),
                  full, _resident((ATTN_W, D_MODEL)), _resident((SGU_W, D_MODEL)), _resident((D_MODEL, D_MODEL)),
                  _full((1, D_MODEL))],
        out_specs=[full] * 5,
        out_shape=[jax.ShapeDtypeStruct((t, D_MODEL), BF16), jax.ShapeDtypeStruct((t, D_MODEL), BF16),
                   jax.ShapeDtypeStruct((t, D_MODEL), BF16), jax.ShapeDtypeStruct((t, D_MODEL), F32),
                   jax.ShapeDtypeStruct((t, D_MODEL), BF16)],
        compiler_params=_cparams(1), name="merge_fwd")(attn, sgu, gu, gu, x, w_pa, w_ps, w_out, g2)


def _merge_bwd(dx1b, gu, pa, ps, w_pa, w_ps, w_out):
    t = dx1b.shape[0]
    tm = min(t, 512)

    def body(d_ref, ga_ref, gb_ref, pa_ref, ps_ref, wpa, wps, wo, out_ref, dpa_ref, dps_ref, da_ref, dsg_ref):
        dm = lax.dot_general(d_ref[...], wo[...], (NT, ((), ())), preferred_element_type=F32)
        sa, sb = _sigmoid(ga_ref[...].astype(F32)), _sigmoid(gb_ref[...].astype(F32))
        dpa = (dm * sa).astype(BF16)
        dps = (dm * sb).astype(BF16)
        out_ref[:, 0:D_MODEL] = jnp.zeros((tm, D_MODEL), BF16)
        out_ref[:, D_MODEL:2 * D_MODEL] = (dm * pa_ref[...].astype(F32) * sa * (1.0 - sa)).astype(BF16)
        out_ref[:, 2 * D_MODEL:GU_COLS] = (dm * ps_ref[...].astype(F32) * sb * (1.0 - sb)).astype(BF16)
        dpa_ref[...] = dpa
        dps_ref[...] = dps
        da_ref[...] = lax.dot_general(dpa, wpa[...], (NT, ((), ())), preferred_element_type=F32)
        dsg_ref[...] = lax.dot_general(dps, wps[...], (NT, ((), ())), preferred_element_type=F32)

    half = pl.BlockSpec((tm, ATTN_W), lambda i: (i, 0))
    full = pl.BlockSpec((tm, D_MODEL), lambda i: (i, 0))
    return pl.pallas_call(
        body, grid=(t // tm,),
        in_specs=[full, pl.BlockSpec((tm, D_MODEL), lambda i: (i, 1)),
                  pl.BlockSpec((tm, D_MODEL), lambda i: (i, 2)), full, full,
                  _resident((ATTN_W, D_MODEL)), _resident((SGU_W, D_MODEL)), _resident((D_MODEL, D_MODEL))],
        out_specs=[pl.BlockSpec((tm, GU_COLS), lambda i: (i, 0)), full, full, half, half],
        out_shape=[jax.ShapeDtypeStruct((t, GU_COLS), BF16), jax.ShapeDtypeStruct((t, D_MODEL), BF16),
                   jax.ShapeDtypeStruct((t, D_MODEL), BF16), jax.ShapeDtypeStruct((t, ATTN_W), F32),
                   jax.ShapeDtypeStruct((t, SGU_W), F32)],
        compiler_params=_cparams(1), name="merge_bwd")(dx1b, gu, gu, pa, ps, w_pa, w_ps, w_out)


def _token_call(name, body, t, tm, ins, outs, reds=(), scratch=()):
    return pl.pallas_call(
        body, grid=(t // tm,), in_specs=[s for _, s in ins],
        out_specs=[o[2] for o in outs] + [_full(r) for r in reds],
        out_shape=[jax.ShapeDtypeStruct(o[0], o[1]) for o in outs] + [jax.ShapeDtypeStruct(r, F32) for r in reds],
        scratch_shapes=list(scratch), compiler_params=_cparams(1), name=name)(*[a for a, _ in ins])


def _rows_spec(tm, width):
    return pl.BlockSpec((tm, width), lambda i: (i, 0))


def _chips_spec(tm):
    return pl.BlockSpec((N_CHIPS, tm, FF_SHARD), lambda i: (0, i, 0))


def _zero_at_start(*refs):
    @pl.when(pl.program_id(0) == 0)
    def _():
        for r in refs:
            r[...] = jnp.zeros(r.shape, r.dtype)


def _ffn_fwd(h2, w_g, w_u):
    t = h2.shape[0]
    tm = min(t, 512)

    def body(h_ref, wg_ref, wu_ref, fa_ref, fb_ref, ff_ref):
        h = h_ref[...]
        for s in range(N_CHIPS):
            a = jnp.dot(h, wg_ref[s], preferred_element_type=F32)
            b = jnp.dot(h, wu_ref[s], preferred_element_type=F32)
            sg = _sigmoid(a)
            silu = a * sg
            fa_ref[s] = (b * (sg * (1.0 + a * (1.0 - sg)))).astype(BF16)
            fb_ref[s] = silu.astype(BF16)
            ff_ref[s] = (silu * b).astype(BF16)

    shp = (N_CHIPS, t, FF_SHARD)
    w_spec = _resident((N_CHIPS, D_MODEL, FF_SHARD))
    return _token_call("ffn_fwd", body, t, tm, [(h2, _rows_spec(tm, D_MODEL)), (w_g, w_spec), (w_u, w_spec)],
                       [(shp, BF16, _chips_spec(tm))] * 3)


def _ffn_down_loss(ff, w_d, x1, tgt, gf):
    t = x1.shape[0]
    tm = min(t, 512)

    def body(ff_ref, wd_ref, x1_ref, tgt_ref, g_ref, dx2_ref, dx2b_ref, loss_ref, dgf_ref):
        _zero_at_start(loss_ref, dgf_ref)
        acc = jnp.dot(ff_ref[0], wd_ref[0], preferred_element_type=F32)
        for s in range(1, N_CHIPS):
            acc = acc + jnp.dot(ff_ref[s], wd_ref[s], preferred_element_type=F32)
        x2 = x1_ref[...] + acc
        g = g_ref[...]
        xhat, rr = _rms_stats(x2)
        diff = xhat * g - tgt_ref[...]
        rows = jnp.sum(diff * diff, axis=1, keepdims=True)
        loss_ref[...] += jnp.broadcast_to(jnp.sum(rows, axis=0, keepdims=True) * (0.5 / D_MODEL), (1, LANES))
        dy = diff * (1.0 / D_MODEL)
        dgf_ref[...] += jnp.sum(dy * xhat, axis=0, keepdims=True)
        dx2 = _rms_bwd(dy, xhat, rr, g)
        dx2_ref[...] = dx2
        dx2b_ref[...] = dx2.astype(BF16)

    row = _rows_spec(tm, D_MODEL)
    return _token_call("ffn_down_loss", body, t, tm,
                       [(ff, _chips_spec(tm)), (w_d, _resident((N_CHIPS, FF_SHARD, D_MODEL))), (x1, row), (tgt, row),
                        (gf, _full((1, D_MODEL)))],
                       [((t, D_MODEL), F32, row), ((t, D_MODEL), BF16, row)], reds=[(1, LANES), (1, D_MODEL)])


def _ffn_bwd_act(dx2b, w_d, fa, fb):
    t = dx2b.shape[0]
    tm = min(t, 512)

    def body(d_ref, wd_ref, fa_ref, fb_ref, da_ref, db_ref):
        d = d_ref[...]
        for s in range(N_CHIPS):
            dff = lax.dot_general(d, wd_ref[s], (NT, ((), ())), preferred_element_type=F32)
            da_ref[s] = (dff * fa_ref[s].astype(F32)).astype(BF16)
            db_ref[s] = (dff * fb_ref[s].astype(F32)).astype(BF16)

    shp = (N_CHIPS, t, FF_SHARD)
    return _token_call("ffn_bwd_act", body, t, tm,
                       [(dx2b, _rows_spec(tm, D_MODEL)), (w_d, _resident((N_CHIPS, FF_SHARD, D_MODEL))),
                        (fa, _chips_spec(tm)), (fb, _chips_spec(tm))],
                       [(shp, BF16, _chips_spec(tm))] * 2)


def _ffn_bwd_in(da, db, w_g, w_u, x1, dx2, g2):
    t = x1.shape[0]
    tm = min(t, 512)

    def body(da_ref, db_ref, wg_ref, wu_ref, x1_ref, dx2_ref, g_ref, dx1_ref, dx1b_ref, dg_ref):
        _zero_at_start(dg_ref)
        acc = None
        for s in range(N_CHIPS):
            part = (lax.dot_general(da_ref[s], wg_ref[s], (NT, ((), ())), preferred_element_type=F32)
                    + lax.dot_general(db_ref[s], wu_ref[s], (NT, ((), ())), preferred_element_type=F32))
            acc = part if acc is None else acc + part
        xhat, rr = _rms_stats(x1_ref[...])
        dg_ref[...] += jnp.sum(acc * xhat, axis=0, keepdims=True)
        dx1 = dx2_ref[...] + _rms_bwd(acc, xhat, rr, g_ref[...])
        dx1_ref[...] = dx1
        dx1b_ref[...] = dx1.astype(BF16)

    row = _rows_spec(tm, D_MODEL)
    w_spec = _resident((N_CHIPS, D_MODEL, FF_SHARD))
    return _token_call("ffn_bwd_in", body, t, tm,
                       [(da, _chips_spec(tm)), (db, _chips_spec(tm)), (w_g, w_spec), (w_u, w_spec), (x1, row), (dx2, row),
                        (g2, _full((1, D_MODEL)))],
                       [((t, D_MODEL), F32, row), ((t, D_MODEL), BF16, row)], reds=[(1, D_MODEL)])


def _group_dh(d, w_refs):
    dh = None
    for part, w_ref in enumerate(w_refs):
        term = lax.dot_general(d[:, part * ATTN_W:(part + 1) * ATTN_W], w_ref[...], (NT, ((), ())),
                               preferred_element_type=F32)
        dh = term if dh is None else dh + term
    return dh


def _in_proj_bwd(dgu, dqkvs, w_in, x, dx1, g1):
    t = x.shape[0]
    tile = min(t, TILE)
    ngroups = len(DILATIONS)

    def body(*refs):
        dgu_ref, dq_refs = refs[0], refs[1:1 + ngroups]
        w0_ref, w1_ref = refs[1 + ngroups:3 + ngroups]
        wg_refs = [refs[3 + ngroups + 3 * g:6 + ngroups + 3 * g] for g in range(ngroups)]
        x_ref, dx1_ref, g_ref, dx_ref, dg_ref = refs[3 + 4 * ngroups:5 + 4 * ngroups + 3]
        slabs = refs[5 + 4 * ngroups + 3:]
        _zero_at_start(dg_ref)
        for g in range(1, ngroups):
            dil = DILATIONS[g]
            part = _group_dh(dq_refs[g][...].reshape(tile, GROUP_COLS), wg_refs[g])
            for r in range(dil):
                _put_class_rows(slabs[g - 1], r, dil, part[r * (tile // dil):(r + 1) * (tile // dil)])
        dh = lax.dot_general(dgu_ref[:, 0:GU_HALF], w0_ref[...], (NT, ((), ())), preferred_element_type=F32)
        dh = dh + lax.dot_general(dgu_ref[:, GU_HALF:], w1_ref[...], (NT, ((), ())), preferred_element_type=F32)
        dh = dh + _group_dh(dq_refs[0][0], wg_refs[0])
        for slab in slabs:
            dh = dh + _from_slabs(slab)
        xhat, rr = _rms_stats(x_ref[...])
        dg_ref[...] += jnp.sum(dh * xhat, axis=0, keepdims=True)
        dx_ref[...] = dx1_ref[...] + _rms_bwd(dh, xhat, rr, g_ref[...])

    row = _rows_spec(tile, D_MODEL)
    group_ins = [(dqkvs[g].reshape(d, t // d, GROUP_COLS), _group_spec(d, tile, GROUP_COLS)) for g, d in enumerate(DILATIONS)]
    w_specs = _gu_w_specs() + [s for g in range(ngroups) for s in _group_w_specs(g)]
    return _token_call(
        "in_proj_bwd", body, t, tile,
        [(dgu, _rows_spec(tile, GU_COLS))] + group_ins + [(w_in, s) for s in w_specs]
        + [(x, row), (dx1, row), (g1, _full((1, D_MODEL)))],
        [((t, D_MODEL), F32, row)], reds=[(1, D_MODEL)], scratch=[_slabs(tile, D_MODEL)] * (ngroups - 1))


WGRAD_TK = 2048


def _wgrad_mm(name, grid, a, a_spec, b, b_spec, acc_shape, out_shape, out_spec, dst=None):
    nk = grid[-1]

    def body(*refs):
        a_ref, b_ref, o_ref, acc_ref = refs[0], refs[1], refs[-2], refs[-1]
        k = pl.program_id(len(grid) - 1)
        part = lax.dot_general(a_ref[...], b_ref[...], (TN, ((), ())), preferred_element_type=F32)

        @pl.when(k == 0)
        def _():
            acc_ref[...] = part

        @pl.when(k > 0)
        def _():
            acc_ref[...] += part

        @pl.when(k == nk - 1)
        def _():
            o_ref[...] = acc_ref[...].astype(BF16)

    filled = [] if dst is None else [dst]
    return pl.pallas_call(
        body, grid=grid, in_specs=[a_spec, b_spec] + [pl.BlockSpec(memory_space=pl.ANY)] * len(filled),
        out_specs=out_spec, out_shape=jax.ShapeDtypeStruct(out_shape, BF16), scratch_shapes=[pltpu.VMEM(acc_shape, F32)],
        input_output_aliases={2: 0} if filled else {}, compiler_params=_cparams(len(grid)), name=name)(a, b, *filled)


def _wgrad_2d(name, a, b, tm, tn):
    t, k1 = a.shape
    n = b.shape[1]
    tk = min(t, WGRAD_TK)
    return _wgrad_mm(name, (k1 // tm, n // tn, t // tk), a, pl.BlockSpec((tk, tm), lambda i, j, k: (k, i)),
                     b, pl.BlockSpec((tk, tn), lambda i, j, k: (k, j)), (tm, tn), (k1, n),
                     pl.BlockSpec((tm, tn), lambda i, j, k: (i, j)))


def _wgrad_in(hs, dgu, dqkvs):
    t = dgu.shape[0]
    tk = min(t, WGRAD_TK)
    gu_block = QKV_BLOCKS * ATTN_W // GU_HALF
    parts = [(hs[0], dgu, GU_HALF, lambda j: j + gu_block)]
    parts += [(hs[g].reshape(t, D_MODEL), dqkvs[g], ATTN_W, lambda j, g=g: _w_in_block(j, g)) for g in range(3)]
    dst = None
    for n, (a, b, tn, block_of) in enumerate(parts):
        dst = _wgrad_mm(f"wgrad_in_{n}", (1, b.shape[1] // tn, t // tk),
                        a, pl.BlockSpec((tk, D_MODEL), lambda i, j, k: (k, 0)), b, pl.BlockSpec((tk, tn), lambda i, j, k: (k, j)),
                        (D_MODEL, tn), (D_MODEL, IN_COLS),
                        pl.BlockSpec((D_MODEL, tn), lambda i, j, k, block_of=block_of: (0, block_of(j))), dst=dst)
    return dst


def _wgrad_ff_in(name, h2, da):
    t = h2.shape[0]
    tk = min(t, WGRAD_TK)
    return _wgrad_mm(name, (N_CHIPS, 1, t // tk), h2, pl.BlockSpec((tk, D_MODEL), lambda i, j, k: (k, 0)),
                     da, pl.BlockSpec((None, tk, FF_SHARD), lambda i, j, k: (i, k, 0)), (D_MODEL, FF_SHARD),
                     (N_CHIPS, D_MODEL, FF_SHARD), pl.BlockSpec((None, D_MODEL, FF_SHARD), lambda i, j, k: (i, 0, 0)))


def _wgrad_ff_down(ff, dx2b):
    t = dx2b.shape[0]
    tk = min(t, WGRAD_TK)
    return _wgrad_mm("wgrad_ffn_down", (N_CHIPS, 1, t // tk), ff, pl.BlockSpec((None, tk, FF_SHARD), lambda i, j, k: (i, k, 0)),
                     dx2b, pl.BlockSpec((tk, D_MODEL), lambda i, j, k: (k, 0)), (FF_SHARD, D_MODEL),
                     (N_CHIPS, FF_SHARD, D_MODEL), pl.BlockSpec((None, FF_SHARD, D_MODEL), lambda i, j, k: (i, 0, 0)))


def _local_step(x, pos_col, tgt, g1, ln_g, ln_b, w_s, b_s, g2, gf, first_weight, late_weights, on_grads=None):
    tables = _rope_tables(pos_col)
    bias_exp = jnp.repeat(jnp.transpose(b_s), SGU_W // SGU_GROUPS, axis=1)

    hs = _norm_fwd(x, g1)
    w_p = first_weight([hs[0], bias_exp] + [table for pair in tables for table in pair])
    gu, qkvs = _in_proj(hs, w_p, tables)
    os_, ls_ = [], []
    for g, dil in enumerate(DILATIONS):
        o, lse = _attn_fwd(qkvs[g], g, dil)
        os_.append(o)
        ls_.append(lse)
    attn = _combine_fwd(os_, ls_)
    sgu = _sgu_fwd(gu, ln_g, ln_b, w_s, bias_exp)
    w_pa, w_ps, w_out, w_g, w_u, w_d = late_weights(attn)
    pa, ps, merged, x1, h2 = _merge_fwd(attn, sgu, gu, x, w_pa, w_ps, w_out, g2)
    fa, fb, ff = _ffn_fwd(h2, w_g, w_u)
    dx2, dx2b, loss, dgf = _ffn_down_loss(ff, w_d, x1, tgt, gf)

    da, db = _ffn_bwd_act(dx2b, w_d, fa, fb)
    dw_d = _wgrad_ff_down(ff, dx2b)
    dx1, dx1b, dg2 = _ffn_bwd_in(da, db, w_g, w_u, x1, dx2, g2)
    dw_g = _wgrad_ff_in("wgrad_ffn_gate", h2, da)
    dw_u = _wgrad_ff_in("wgrad_ffn_up", h2, db)

    dgu, dpa, dps, dattn, dsgu = _merge_bwd(dx1b, gu, pa, ps, w_pa, w_ps, w_out)
    dw_out = _wgrad_2d("wgrad_out", merged, dx1b, D_MODEL, D_MODEL)
    dw_pa = _wgrad_2d("wgrad_proj_attn", attn, dpa, ATTN_W, D_MODEL)
    dw_ps = _wgrad_2d("wgrad_proj_sgu", sgu, dps, SGU_W, D_MODEL)
    if on_grads is not None:
        ln_g = ln_g + on_grads(1, dict(w_proj_attn=dw_pa, w_proj_sgu=dw_ps, w_out=dw_out, w_ffn_gate=dw_g, w_ffn_up=dw_u,
                                       w_ffn_down=dw_d))[:, :SGU_W]
    dgu, dw_s, dbias, dln_g, dln_b = _sgu_bwd(dgu, gu, dsgu, ln_g, ln_b, w_s, bias_exp)
    dos, ccs = _combine_bwd(dattn, os_, ls_)
    dqkvs = [_attn_bwd(qkvs[g], dos[g], ccs[g], ls_[g], *tables[g], g, dil) for g, dil in enumerate(DILATIONS)]
    dw_p = _wgrad_in(hs, dgu, dqkvs)
    if on_grads is not None:
        g1 = g1 + on_grads(0, dict(w_in=dw_p))
    dx, dg1 = _in_proj_bwd(dgu, dqkvs, w_p, x, dx1, g1)

    db_s = jnp.transpose(dbias[:, ::SGU_W // SGU_GROUPS])
    small = dict(loss=loss, norm1_g=dg1, sgu_ln_g=dln_g, sgu_ln_b=dln_b, w_spatial=dw_s, b_spatial=db_s,
                 norm2_g=dg2, final_g=dgf)
    big = dict(w_in=dw_p, w_proj_attn=dw_pa, w_proj_sgu=dw_ps, w_out=dw_out, w_ffn_gate=dw_g, w_ffn_up=dw_u,
               w_ffn_down=dw_d)
    return dx, big, small


def _ew(name, fn, ins, out_dtypes, after=()):
    shp = ins[0].shape
    rows, cols = shp
    tr = next((cand for cand in (256, 352, 128) if rows % cand == 0 and rows > cand), rows)

    def body(*refs):
        res = fn(*[r[...] for r in refs[:len(ins)]])
        for o_ref, v in zip(refs[len(ins) + len(after):], res):
            o_ref[...] = v.astype(o_ref.dtype)

    spec = pl.BlockSpec((tr, cols), lambda i: (i, 0))
    return pl.pallas_call(
        body, grid=(rows // tr,), in_specs=[spec] * len(ins) + [pl.BlockSpec(memory_space=pl.ANY)] * len(after),
        out_specs=[spec] * len(out_dtypes), out_shape=[jax.ShapeDtypeStruct(shp, d) for d in out_dtypes],
        compiler_params=_cparams(1), name=name)(*ins, *after)


def _adamw_math(g, w, m, v):
    m = ADAM_B1 * m + (1.0 - ADAM_B1) * g
    v = ADAM_B2 * v + (1.0 - ADAM_B2) * (g * g)
    m_hat = m / (1.0 - ADAM_B1 ** ADAM_STEP)
    v_hat = v / (1.0 - ADAM_B2 ** ADAM_STEP)
    delta = -ADAM_LR * (m_hat / (jnp.sqrt(v_hat) + ADAM_EPS) + ADAM_WD * w)
    return delta, m, v


def _adamw(name, g, w, m, v):
    return _ew(name, lambda g_, w_, m_, v_: (g_,) + _adamw_math(g_, w_, m_, v_), [g, w, m, v], [F32] * 4)


VMEM_SPEC = pl.BlockSpec(memory_space=pltpu.VMEM)


def _for_row_chunks(rows, fn):
    ck = next(c for c in (64, 32, 16) if rows % c == 0)

    def step(i, carry):
        fn(pl.multiple_of(i * ck, ck), ck)
        return carry

    lax.fori_loop(0, rows // ck, step, 0)


def _place():
    x, y, c = lax.axis_index("x"), lax.axis_index("y"), lax.axis_index("c")
    chips = [(1 - x, y), (x, 1 - y), (1 - x, 1 - y)]
    return x, y, c, 2 * x + y, chips


def _rows(ref, start, size):
    if len(ref.shape) == 2:
        return ref.at[pl.ds(start, size), :]
    return ref.at[:, pl.ds(start, size), :]


def _gather_finish(name, shard, landed):
    k_rows, n = shard.shape
    kh = k_rows // 2

    def body(shard_hbm, land_hbm, out_ref, shard_ref, land_ref, loc, send, recv):
        x, y, c, me, chips = _place()
        sibling = (x, y, 1 - c)

        def window(core, chip):
            return out_ref.at[pl.ds(core * kh, kh), pl.ds(pl.multiple_of(chip * n, LANES), n)]

        loads = [pltpu.make_async_copy(land_hbm.at[j], land_ref.at[j], loc.at[0, j]) for j in range(3)]
        loads.append(pltpu.make_async_copy(shard_hbm, shard_ref, loc.at[0, 3]))
        for cp in loads:
            cp.start()
        copies, passed = [], []
        for j, chip in enumerate(chips):
            mine = window(c, 2 * chip[0] + chip[1])
            loads[j].wait()
            copies.append(pltpu.make_async_copy(land_ref.at[j], mine, loc.at[1, j]))
            passed.append(pltpu.make_async_remote_copy(src_ref=land_ref.at[j], dst_ref=mine, send_sem=send.at[j],
                                                       recv_sem=recv.at[j], device_id=sibling, device_id_type=MESH))
            copies[-1].start()
            passed[-1].start()
        loads[3].wait()
        copies.append(pltpu.make_async_copy(shard_ref, out_ref.at[:, pl.ds(pl.multiple_of(me * n, LANES), n)], loc.at[1, 3]))
        copies[-1].start()
        for j, chip in enumerate(chips):
            pltpu.make_async_remote_copy(src_ref=land_ref.at[j], dst_ref=window(1 - c, 2 * chip[0] + chip[1]), send_sem=send.at[j],
                                         recv_sem=recv.at[j], device_id=sibling, device_id_type=MESH).wait_recv()
        for cp in copies:
            cp.wait()
        for cp in passed:
            cp.wait_send()

    any_spec = pl.BlockSpec(memory_space=pl.ANY)
    return pl.pallas_call(
        body, in_specs=[any_spec] * 2, out_specs=any_spec,
        out_shape=jax.ShapeDtypeStruct((k_rows, N_CHIPS * n), shard.dtype),
        scratch_shapes=[pltpu.VMEM(shard.shape, shard.dtype), pltpu.VMEM(landed.shape, landed.dtype),
                        pltpu.SemaphoreType.DMA((2, 4)), pltpu.SemaphoreType.DMA((3,)), pltpu.SemaphoreType.DMA((3,))],
        compiler_params=pltpu.CompilerParams(vmem_limit_bytes=VMEM_LIMIT), name=name)(shard, landed)


HBM_SPEC = pl.BlockSpec(memory_space=pltpu.HBM)
SEM_SPEC = pl.BlockSpec(memory_space=pltpu.SEMAPHORE)
DATAFLOW = pltpu.SideEffectType.DATAFLOW_SIDE_EFFECTING
TOKEN_SHAPE = (1, D_MODEL)
N_PEERS = 7
SUM_SPLIT = 4
SUM_SPLIT_ELEMS = 512 * 1024


def _peers():
    x, y, c = lax.axis_index("x"), lax.axis_index("y"), lax.axis_index("c")
    flip = lambda v, f: 1 - v if f else v
    return [(flip(x, k & 4), flip(y, k & 2), flip(c, k & 1)) for k in range(1, N_PEERS + 1)]


def _piece_shape(shape):
    return (shape[-2] // 2, shape[2] if len(shape) == 3 else shape[1] // N_CHIPS)


def _device_piece(ref, chip, core):
    kh, n4 = _piece_shape(ref.shape)
    if len(ref.shape) == 3:
        return ref.at[chip, pl.ds(core * kh, kh), :]
    return ref.at[pl.ds(core * kh, kh), pl.ds(chip * n4, n4)]


def _exchange_copies(partials, lands, send, recv):
    return [pltpu.make_async_remote_copy(
        src_ref=_device_piece(partials[t], 2 * px + py, pc), dst_ref=lands[t].at[k], send_sem=send.at[t * N_PEERS + k],
        recv_sem=recv.at[t * N_PEERS + k], device_id=(px, py, pc), device_id_type=MESH)
        for t in range(len(partials)) for k, (px, py, pc) in enumerate(_peers())]


def _broadcast_copies(srcs, lands, send, recv):
    return [pltpu.make_async_remote_copy(
        src_ref=srcs[t], dst_ref=lands[t].at[k], send_sem=send.at[t * N_PEERS + k], recv_sem=recv.at[t * N_PEERS + k],
        device_id=peer, device_id_type=MESH)
        for t in range(len(srcs)) for k, peer in enumerate(_peers())]


class _LocalCopy:
    def __init__(self, src_ref, dst_ref, sem):
        self.copy = pltpu.make_async_copy(src_ref, dst_ref, sem)

    def start(self):
        self.copy.start()

    def wait_send(self):
        self.copy.wait()

    def wait_recv(self):
        pass


def _gather_copies(shards, lands, send, recv):
    x, y, c, me, chips = _place()
    copies = []
    for t in range(len(shards)):
        n = shards[t].shape[1]
        place = lands[t].at[me] if len(lands[t].shape) == 3 else lands[t].at[:, pl.ds(pl.multiple_of(me * n, LANES), n)]
        copies += [pltpu.make_async_remote_copy(
            src_ref=shards[t], dst_ref=place, send_sem=send.at[t * 4 + j], recv_sem=recv.at[t * 4 + j],
            device_id=(*chip, c), device_id_type=MESH) for j, chip in enumerate(chips)]
        copies.append(_LocalCopy(shards[t], place, send.at[t * 4 + 3]))
    return copies


def _gather_half_copies(shards, lands, send, recv):
    x, y, c, me, chips = _place()
    return [pltpu.make_async_remote_copy(
        src_ref=_rows(shards[t], c * (shards[t].shape[0] // 2), shards[t].shape[0] // 2), dst_ref=lands[t].at[j],
        send_sem=send.at[t * 3 + j], recv_sem=recv.at[t * 3 + j], device_id=(*chip, c), device_id_type=MESH)
        for t in range(len(shards)) for j, chip in enumerate(chips)]


def _split_start(name, copies, per_tensor, srcs, land_shapes):
    nt = len(srcs)
    lands = [lax.empty(s, a.dtype) for s, a in zip(land_shapes, srcs)]
    nsem = nt * per_tensor

    def body(*refs):
        send, recv = refs[2 * nt], refs[2 * nt + 1]
        for cp in copies(refs[:nt], refs[nt:2 * nt], send, recv):
            cp.start()
        refs[-1][...] = jnp.zeros(TOKEN_SHAPE, F32)

    hbm = lambda a: pltpu.with_memory_space_constraint(a, pltpu.HBM)
    outs = pl.pallas_call(
        body, name=name,
        out_shape=[pltpu.SemaphoreType.DMA((nsem,)), pltpu.SemaphoreType.DMA((nsem,))]
        + [pltpu.HBM(s.shape, s.dtype) for s in srcs] + [pltpu.HBM(l.shape, l.dtype) for l in lands]
        + [jax.ShapeDtypeStruct(TOKEN_SHAPE, F32)],
        in_specs=[HBM_SPEC] * (2 * nt), out_specs=[SEM_SPEC, SEM_SPEC] + [HBM_SPEC] * (2 * nt) + [VMEM_SPEC],
        input_output_aliases={i: 2 + i for i in range(2 * nt)},
        compiler_params=pltpu.CompilerParams(has_side_effects=DATAFLOW))(*[hbm(a) for a in list(srcs) + lands])
    return outs[0], outs[1], outs[2:2 + nt], outs[2 + nt:2 + 2 * nt], outs[-1]


def _split_wait(name, copies, send, recv, srcs, lands, after):
    nt = len(srcs)
    after = list(after) if isinstance(after, (list, tuple)) else [after]

    def body(*refs):
        for cp in copies(refs[:nt], refs[nt:2 * nt], refs[2 * nt], refs[2 * nt + 1]):
            cp.wait_send()
            cp.wait_recv()

    outs = pl.pallas_call(
        body, name=name,
        out_shape=[pltpu.HBM(s.shape, s.dtype) for s in srcs] + [pltpu.HBM(l.shape, l.dtype) for l in lands],
        in_specs=[HBM_SPEC] * (2 * nt) + [SEM_SPEC, SEM_SPEC] + [pl.BlockSpec(memory_space=pl.ANY)] * len(after),
        out_specs=[HBM_SPEC] * (2 * nt), input_output_aliases={i: i for i in range(2 * nt)},
        compiler_params=pltpu.CompilerParams(has_side_effects=DATAFLOW))(*srcs, *lands, send, recv, *after)
    return outs[:nt], outs[nt:]


def _device_sum(name, partials, lands):
    nt = len(partials)
    pieces = [_piece_shape(p.shape) for p in partials]
    units = []
    for t, (kh, n4) in enumerate(pieces):
        split = SUM_SPLIT if kh * n4 >= SUM_SPLIT_ELEMS else 1
        units += [(t, j * (kh // split), kh // split) for j in range(split)]
    nu = len(units)

    def body(*refs):
        ins, slots, outs = refs[:nt], refs[nt:2 * nt], refs[2 * nt:3 * nt]
        owns, landed, sums = refs[3 * nt:4 * nt], refs[4 * nt:5 * nt], refs[5 * nt:6 * nt]
        loc, send, recv = refs[6 * nt:]
        x, y, c, me, chips = _place()
        sibling = (x, y, 1 - c)
        loads = []
        for u, (t, r0, rows) in enumerate(units):
            loads.append((
                pltpu.make_async_copy(_rows(_device_piece(ins[t], me, c), r0, rows), _rows(owns[t], r0, rows), loc.at[0, u]),
                pltpu.make_async_copy(_rows(slots[t], r0, rows), _rows(landed[t], r0, rows), loc.at[1, u])))
            for cp in loads[-1]:
                cp.start()
        stores = []
        for u, (t, r0, rows) in enumerate(units):
            for cp in loads[u]:
                cp.wait()

            def add(q0, ck, own=owns[t], slot=landed[t], dst=sums[t], r0=r0):
                at = pl.ds(pl.multiple_of(r0 + q0, ck), ck)
                acc = own[at, :].astype(F32)
                for k in range(N_PEERS):
                    acc = acc + slot[k, at, :].astype(F32)
                dst[at, :] = acc

            _for_row_chunks(rows, add)
            mine = _rows(outs[t], c * pieces[t][0] + r0, rows)
            stores.append((
                pltpu.make_async_copy(_rows(sums[t], r0, rows), mine, loc.at[2, u]),
                pltpu.make_async_remote_copy(src_ref=_rows(sums[t], r0, rows), dst_ref=mine, send_sem=send.at[u],
                                             recv_sem=recv.at[u], device_id=sibling, device_id_type=MESH)))
            for cp in stores[-1]:
                cp.start()
        for u, (t, r0, rows) in enumerate(units):
            pltpu.make_async_remote_copy(
                src_ref=_rows(sums[t], r0, rows), dst_ref=_rows(outs[t], (1 - c) * pieces[t][0] + r0, rows),
                send_sem=send.at[u], recv_sem=recv.at[u], device_id=sibling, device_id_type=MESH).wait_recv()
            stores[u][0].wait()
            stores[u][1].wait_send()

    any_spec = pl.BlockSpec(memory_space=pl.ANY)
    return pl.pallas_call(
        body, in_specs=[any_spec] * (2 * nt), out_specs=[any_spec] * nt,
        out_shape=[jax.ShapeDtypeStruct((2 * kh, n4), F32) for kh, n4 in pieces],
        scratch_shapes=[pltpu.VMEM(p, BF16) for p in pieces] + [pltpu.VMEM((N_PEERS,) + p, BF16) for p in pieces]
        + [pltpu.VMEM(p, F32) for p in pieces]
        + [pltpu.SemaphoreType.DMA((3, nu)), pltpu.SemaphoreType.DMA((nu,)), pltpu.SemaphoreType.DMA((nu,))],
        compiler_params=pltpu.CompilerParams(vmem_limit_bytes=VMEM_LIMIT), name=name)(*partials, *lands)


VEC_SHAPE = (8, D_MODEL + LANES)
VEC_SLOTS = dict(norm1_g=(slice(0, 1), slice(0, D_MODEL)), norm2_g=(slice(1, 2), slice(0, D_MODEL)),
                 final_g=(slice(2, 3), slice(0, D_MODEL)), sgu_ln_g=(slice(3, 4), slice(0, SGU_W)),
                 sgu_ln_b=(slice(3, 4), slice(SGU_W, 2 * SGU_W)), b_spatial=(slice(0, 8), slice(D_MODEL, D_MODEL + LANES)),
                 loss=(slice(4, 5), slice(0, LANES)))
VEC_PARAMS = ("norm1_g", "norm2_g", "final_g", "sgu_ln_g", "sgu_ln_b", "b_spatial")
SMALL_PARAMS = VEC_PARAMS + ("w_spatial",)
W_SPATIAL_2D = (SGU_GROUPS * SGU_CHUNK, SGU_CHUNK)


SMALL_GRADS = VEC_PARAMS + ("loss", "w_spatial")


def _small_shape(name):
    if name == "w_spatial":
        return W_SPATIAL_2D
    rows, cols = VEC_SLOTS[name]
    return (rows.stop - rows.start, cols.stop - cols.start)


def _pack_small(dst, parts):
    dst[...] = jnp.zeros(VEC_SHAPE, F32)
    for n, ref in parts.items():
        if n in VEC_SLOTS:
            dst[VEC_SLOTS[n]] = ref[...]


def _small_start(partials):
    names = VEC_PARAMS + ("loss",)

    def body(*refs):
        _pack_small(refs[-1], dict(zip(names, refs[:-1])))

    vec = pl.pallas_call(
        body, in_specs=[VMEM_SPEC] * len(names), out_specs=VMEM_SPEC, out_shape=jax.ShapeDtypeStruct(VEC_SHAPE, F32),
        name="small_params_pack")(*[partials[n].reshape(_small_shape(n)) for n in names])
    srcs = [vec, partials["w_spatial"].reshape(W_SPATIAL_2D)]
    return _split_start("small_params_start", _broadcast_copies, N_PEERS, srcs, [(N_PEERS,) + s.shape for s in srcs])


def _small_finish(started, after, w, m, v):
    own, landed = _split_wait("small_params_wait", _broadcast_copies, *started, after)
    ng, npar = len(SMALL_GRADS), len(SMALL_PARAMS)

    def update_body(*refs):
        vec_own, ws_own, vec_slots, ws_slots = refs[:4]
        w_in, m_in, v_in = (dict(zip(SMALL_PARAMS, refs[4 + k * npar:4 + (k + 1) * npar])) for k in range(3))
        o0 = 4 + 3 * npar
        g_out = dict(zip(SMALL_GRADS, refs[o0:o0 + ng]))
        d_out, m_out, v_out = (dict(zip(SMALL_PARAMS, refs[o0 + ng + k * npar:o0 + ng + (k + 1) * npar])) for k in range(3))
        vg, vw, vm, vv = refs[o0 + ng + 3 * npar:]
        me = 4 * lax.axis_index("x") + 2 * lax.axis_index("y") + lax.axis_index("c")

        def device_sum(mine, slots, read):
            acc = None
            for i in range(N_PEERS + 1):
                k = me ^ i
                part = jnp.where(k == 0, read(mine), read(slots.at[jnp.maximum(k, 1) - 1]))
                acc = part if acc is None else acc + part
            return acc

        vg[...] = device_sum(vec_own, vec_slots, lambda ref: ref[...])
        _pack_small(vw, w_in)
        _pack_small(vm, m_in)
        _pack_small(vv, v_in)
        d_vec, m_vec, v_vec = _adamw_math(vg[...], vw[...], vm[...], vv[...])
        vw[...] = d_vec
        vm[...] = m_vec
        vv[...] = v_vec
        for n in VEC_PARAMS + ("loss",):
            g_out[n][...] = vg[VEC_SLOTS[n]]
        for n in VEC_PARAMS:
            d_out[n][...] = vw[VEC_SLOTS[n]]
            m_out[n][...] = vm[VEC_SLOTS[n]]
            v_out[n][...] = vv[VEC_SLOTS[n]]

        def spatial(r0, ck):
            rows = pl.ds(r0, ck)
            g = device_sum(ws_own, ws_slots, lambda ref: ref[rows, :])
            d_, m_, v_ = _adamw_math(g, w_in["w_spatial"][rows, :], m_in["w_spatial"][rows, :], v_in["w_spatial"][rows, :])
            g_out["w_spatial"][rows, :] = g
            d_out["w_spatial"][rows, :] = d_
            m_out["w_spatial"][rows, :] = m_
            v_out["w_spatial"][rows, :] = v_

        _for_row_chunks(W_SPATIAL_2D[0], spatial)

    ins = list(own) + list(landed)
    for src in (w, m, v):
        ins += [src[n].reshape(_small_shape(n)) for n in SMALL_PARAMS]
    out_shapes = [jax.ShapeDtypeStruct(_small_shape(n), F32) for n in SMALL_GRADS + SMALL_PARAMS * 3]
    outs = pl.pallas_call(
        update_body, in_specs=[VMEM_SPEC] * len(ins), out_specs=[VMEM_SPEC] * len(out_shapes), out_shape=out_shapes,
        scratch_shapes=[pltpu.VMEM(VEC_SHAPE, F32)] * 4, name="small_params_update")(*ins)
    grads = dict(zip(SMALL_GRADS, outs[:ng]))
    rest = [dict(zip(SMALL_PARAMS, outs[ng + k * npar:ng + (k + 1) * npar])) for k in range(3)]
    return grads, rest[0], rest[1], rest[2]


BIG = ("w_in", "w_proj_attn", "w_proj_sgu", "w_out", "w_ffn_gate", "w_ffn_up", "w_ffn_down")
COMM_GROUPS = (("w_in",), ("w_proj_attn", "w_proj_sgu", "w_out", "w_ffn_gate", "w_ffn_up", "w_ffn_down"))
WEIGHTS = ("norm1_g", "w_in", "sgu_ln_g", "sgu_ln_b", "w_spatial", "b_spatial", "w_proj_attn", "w_proj_sgu", "w_out",
           "norm2_g", "w_ffn_gate", "w_ffn_up", "w_ffn_down", "final_g")


def kernel(x, positions, norm1_g, w_in, sgu_ln_g, sgu_ln_b, w_spatial, b_spatial, w_proj_attn, w_proj_sgu, w_out, norm2_g, w_ffn_gate, w_ffn_up, w_ffn_down, final_g, loss_target, m_norm1_g, m_w_in, m_sgu_ln_g, m_sgu_ln_b, m_w_spatial, m_b_spatial, m_w_proj_attn, m_w_proj_sgu, m_w_out, m_norm2_g, m_w_ffn_gate, m_w_ffn_up, m_w_ffn_down, m_final_g, v_norm1_g, v_w_in, v_sgu_ln_g, v_sgu_ln_b, v_w_spatial, v_b_spatial, v_w_proj_attn, v_w_proj_sgu, v_w_out, v_norm2_g, v_w_ffn_gate, v_w_ffn_up, v_w_ffn_down, v_final_g):
    w = dict(norm1_g=norm1_g, w_in=w_in, sgu_ln_g=sgu_ln_g, sgu_ln_b=sgu_ln_b, w_spatial=w_spatial, b_spatial=b_spatial,
             w_proj_attn=w_proj_attn, w_proj_sgu=w_proj_sgu, w_out=w_out, norm2_g=norm2_g, w_ffn_gate=w_ffn_gate,
             w_ffn_up=w_ffn_up, w_ffn_down=w_ffn_down, final_g=final_g)
    m = dict(norm1_g=m_norm1_g, w_in=m_w_in, sgu_ln_g=m_sgu_ln_g, sgu_ln_b=m_sgu_ln_b, w_spatial=m_w_spatial,
             b_spatial=m_b_spatial, w_proj_attn=m_w_proj_attn, w_proj_sgu=m_w_proj_sgu, w_out=m_w_out, norm2_g=m_norm2_g,
             w_ffn_gate=m_w_ffn_gate, w_ffn_up=m_w_ffn_up, w_ffn_down=m_w_ffn_down, final_g=m_final_g)
    v = dict(norm1_g=v_norm1_g, w_in=v_w_in, sgu_ln_g=v_sgu_ln_g, sgu_ln_b=v_sgu_ln_b, w_spatial=v_w_spatial,
             b_spatial=v_b_spatial, w_proj_attn=v_w_proj_attn, w_proj_sgu=v_w_proj_sgu, w_out=v_w_out, norm2_g=v_norm2_g,
             w_ffn_gate=v_w_ffn_gate, w_ffn_up=v_w_ffn_up, w_ffn_down=v_w_ffn_down, final_g=v_final_g)
    t = x.shape[1]

    def cast(n, after):
        flip = jnp.transpose if w[n].shape[-1] % LANES else (lambda a: a)
        return flip(_ew(f"cast_{n}", lambda a: (a,), [flip(w[n][0])], [BF16], after)[0])

    shards = {"w_in": cast("w_in", [])}
    late = COMM_GROUPS[1]
    k_in, n_in = shards["w_in"].shape
    *first, token = _split_start("gather_start_0", _gather_half_copies, 3, [shards["w_in"]], [(3, k_in // 2, n_in)])
    shards.update({n: cast(n, [token]) for n in late})
    pending = {}

    def first_weight(after):
        srcs, filled = _split_wait("gather_wait_0", _gather_half_copies, *first, list(after) + [shards[n] for n in late])
        gath_in, late_shards = lax.optimization_barrier(
            (_gather_finish("gather_finish_0", srcs[0], filled[0]), [shards[n] for n in late]))
        land_shapes = [(s.shape[0], N_CHIPS * s.shape[1]) if n.startswith("w_proj") else (N_CHIPS,) + s.shape
                       for n, s in zip(late, late_shards)]
        *pending["late"], _ = _split_start("gather_start_1", _gather_copies, 4, late_shards, land_shapes)
        return gath_in

    def late_weights(after):
        _, filled = _split_wait("gather_wait_1", _gather_copies, *pending["late"], after)
        gath = dict(zip(late, filled))
        return (gath["w_proj_attn"], gath["w_proj_sgu"],
                gath["w_out"].reshape(D_MODEL, D_MODEL), gath["w_ffn_gate"], gath["w_ffn_up"], gath["w_ffn_down"])

    exchanges = {}

    def on_grads(i, partials):
        if "w_out" in partials:
            partials["w_out"] = partials["w_out"].reshape(N_CHIPS, D_MODEL // N_CHIPS, D_MODEL)
        parts = [partials[n] for n in COMM_GROUPS[i]]
        *exchanges[i], started = _split_start(
            f"rs_exchange_start_{i}", _exchange_copies, N_PEERS, parts, [(N_PEERS,) + _piece_shape(p.shape) for p in parts])
        return started

    dx, _, small = _local_step(
        x[0], positions.reshape(t, 1), loss_target[0], norm1_g + token, sgu_ln_g, sgu_ln_b,
        w_spatial[0], b_spatial[0] + token[:1, :LANES],
        norm2_g, final_g.reshape(1, D_MODEL), first_weight, late_weights, on_grads=on_grads)
    *small_started, small_token = _small_start(small)

    grads = {}
    for i in (1, 0):
        parts, filled = _split_wait(f"rs_exchange_wait_{i}", _exchange_copies, *exchanges[i], small_token)
        grads.update(zip(COMM_GROUPS[i], _device_sum(f"rs_device_sum_{i}", parts, filled)))

    delta, new_m, new_v, updated = {}, {}, {}, []
    for n in BIG:
        shp = w[n].shape
        flip = jnp.transpose if shp[-1] % LANES else (lambda a: a)
        outs = _adamw(f"adamw_{n}", flip(grads[n]), flip(w[n][0]), flip(m[n][0]), flip(v[n][0]))
        grads[n], delta[n], new_m[n], new_v[n] = (flip(a).reshape(shp) for a in outs)
        updated.append(outs[-1])

    g_s, d_s, m_s, v_s = _small_finish(small_started, updated, w, m, v)
    loss = g_s["loss"][0, 0]
    for n in SMALL_PARAMS:
        shp = w[n].shape
        grads[n], delta[n], new_m[n], new_v[n] = (a[n].reshape(shp) for a in (g_s, d_s, m_s, v_s))

    return (loss, dx.reshape(x.shape), *[grads[n] for n in WEIGHTS], *[delta[n] for n in WEIGHTS],
            *[new_m[n] for n in WEIGHTS], *[new_v[n] for n in WEIGHTS])
```

```python
import functools

import numpy as np
import jax
import jax.numpy as jnp
from jax import lax
from jax.experimental import pallas as pl
from jax.experimental.pallas import tpu as pltpu

F32, BF16 = jnp.float32, jnp.bfloat16
MESH = pl.DeviceIdType.MESH

D_MODEL = 1024
HEAD_DIM = 64
ATTN_W = 512
DILATIONS = (1, 4, 16)
BLK = 128
ATTN_BLOCKS_PER_STEP = 4
ROPE_DIM = 16
ROPE_THETA = 500000.0
SGU_W = 512
SGU_CHUNK = 128
SGU_GROUPS = 8
D_FF = 2816
N_CHIPS = 4
FF_SHARD = D_FF // N_CHIPS
IN_COLS = 7680
EPS = 1e-6
NEG = -1e30
LANES = 128
VMEM_LIMIT = 52 * 1024 * 1024

ADAM_LR, ADAM_B1, ADAM_B2, ADAM_EPS, ADAM_WD, ADAM_STEP = 0.001, 0.9, 0.999, 1e-08, 0.01, 10

QKV_BLOCKS = 9


def _w_in_block(part, g):
    return part * len(DILATIONS) + g


def _cparams(ngrid):
    return pltpu.CompilerParams(dimension_semantics=("arbitrary",) * ngrid, vmem_limit_bytes=VMEM_LIMIT)


def _full(shape):
    return pl.BlockSpec(shape, lambda *_: (0,) * len(shape))


def _resident(shape):
    return pl.BlockSpec(shape, lambda *_: (0,) * len(shape), pipeline_mode=pl.Buffered(1))


NT = ((1,), (1,))
TN = ((0,), (0,))


def _rope(v, cos_t, sin_t):
    half = ROPE_DIM // 2
    first = (lax.broadcasted_iota(jnp.int32, cos_t.shape, 1) % HEAD_DIM) < half
    outs = []
    for cs in range(v.shape[1] // LANES):
        x = v[:, cs * LANES:(cs + 1) * LANES]
        partner = jnp.where(first, pltpu.roll(x, LANES - half, axis=1), pltpu.roll(x, half, axis=1))
        outs.append(x * cos_t + partner * sin_t)
    return outs[0] if len(outs) == 1 else jnp.concatenate(outs, axis=1)


def _spread_heads(v2, upper):
    other = pltpu.roll(v2, HEAD_DIM, axis=1)
    h0 = jnp.where(upper, other, v2)
    h1 = jnp.where(upper, v2, other)
    return jnp.concatenate([jnp.concatenate([h0, h0], axis=1), jnp.concatenate([h1, h1], axis=1)], axis=0)


def _sigmoid(v):
    return 0.5 * jnp.tanh(0.5 * v) + 0.5


def _rms_stats(v):
    r = lax.rsqrt(jnp.mean(v * v, axis=-1, keepdims=True) + EPS)
    return v * r, r


def _rms_bwd(dy, xhat, r, g):
    dxh = dy * g
    return r * (dxh - xhat * jnp.mean(dxh * xhat, axis=-1, keepdims=True))


def _head_sum_matrix():
    idx = np.arange(ATTN_W) // HEAD_DIM
    return jnp.asarray((idx[:, None] == idx[None, :]).astype(np.float32), dtype=BF16)


def _group_sum(v, e):
    hi = v.astype(BF16)
    lo = (v - hi.astype(F32)).astype(BF16)
    return jnp.dot(hi, e, preferred_element_type=F32) + jnp.dot(lo, e, preferred_element_type=F32)


TILE = 512


def _to_slabs(slab_ref, v):
    for cs in range(slab_ref.shape[0]):
        slab_ref[cs] = v[:, cs * LANES:(cs + 1) * LANES]


def _from_slabs(slab_ref):
    return jnp.concatenate([slab_ref[cs] for cs in range(slab_ref.shape[0])], axis=1)


def _class_rows(slab_ref, r, dil):
    n = slab_ref.shape[1] // dil
    return jnp.concatenate([slab_ref.at[cs][pl.ds(r, n, stride=dil), :] for cs in range(slab_ref.shape[0])], axis=1)


def _put_class_rows(slab_ref, r, dil, v):
    n = slab_ref.shape[1] // dil
    for cs in range(slab_ref.shape[0]):
        slab_ref.at[cs][pl.ds(r, n, stride=dil), :] = v[:, cs * LANES:(cs + 1) * LANES]


def _natural_from_group(slab_ref, grp_ref):
    dil = grp_ref.shape[0]
    for r in range(dil):
        _put_class_rows(slab_ref, r, dil, grp_ref[r].astype(F32))
    return _from_slabs(slab_ref)


def _group_from_natural(slab_ref, grp_ref, v):
    dil = grp_ref.shape[0]
    _to_slabs(slab_ref, v)
    for r in range(dil):
        grp_ref[r] = _class_rows(slab_ref, r, dil).astype(grp_ref.dtype)


def _group_spec(dil, tile, width):
    return pl.BlockSpec((dil, tile // dil, width), lambda i, *_: (0, i, 0))


def _slabs(tile, width):
    return pltpu.VMEM((width // LANES, tile, LANES), F32)


def _rope_consts():
    lane = np.arange(LANES) % HEAD_DIM
    fi = lane % (ROPE_DIM // 2)
    invf = np.where(lane < ROPE_DIM, ROPE_THETA ** (-(2.0 * fi) / ROPE_DIM), 0.0)
    sgn = np.where(lane < ROPE_DIM // 2, -1.0, np.where(lane < ROPE_DIM, 1.0, 0.0))
    return (jnp.asarray(invf.astype(np.float32)).reshape(1, LANES), jnp.asarray(sgn.astype(np.float32)).reshape(1, LANES))


def _rope_tables(pos_col):
    t = pos_col.shape[0]
    tile = min(t, TILE)
    invf, sgn = _rope_consts()

    def body(p_ref, f_ref, s_ref, c0, s0, c1, s1, c2, s2, slab_c, slab_s):
        ang = p_ref[...].astype(F32) * f_ref[...]
        cos, sin = jnp.cos(ang), jnp.sin(ang) * s_ref[...]
        c0[...] = cos
        s0[...] = sin
        _group_from_natural(slab_c, c1, cos)
        _group_from_natural(slab_s, s1, sin)
        for r in range(DILATIONS[2]):
            c2[r] = _class_rows(slab_c, r, DILATIONS[2])
            s2[r] = _class_rows(slab_s, r, DILATIONS[2])

    nat = pl.BlockSpec((tile, LANES), lambda i: (i, 0))
    specs, shapes = [nat, nat], [(t, LANES)] * 2
    for d in DILATIONS[1:]:
        specs += [_group_spec(d, tile, LANES)] * 2
        shapes += [(d, t // d, LANES)] * 2
    outs = pl.pallas_call(
        body, grid=(t // tile,),
        in_specs=[pl.BlockSpec((tile, 1), lambda i: (i, 0)), _full((1, LANES)), _full((1, LANES))],
        out_specs=specs, out_shape=[jax.ShapeDtypeStruct(s, F32) for s in shapes],
        scratch_shapes=[_slabs(tile, LANES)] * 2,
        compiler_params=_cparams(1), name="rope_tables")(pos_col, invf, sgn)
    return [(outs[2 * g].reshape(t, LANES), outs[2 * g + 1].reshape(t, LANES)) for g in range(len(DILATIONS))]


def _norm_fwd(x, g):
    t = x.shape[0]
    tile = min(t, TILE)

    def body(x_ref, g_ref, h0_ref, h1_ref, h2_ref, slab):
        xhat, _ = _rms_stats(x_ref[...])
        hn = xhat * g_ref[...]
        h0_ref[...] = hn.astype(BF16)
        _group_from_natural(slab, h1_ref, hn)
        for r in range(DILATIONS[2]):
            h2_ref[r] = _class_rows(slab, r, DILATIONS[2]).astype(BF16)

    nat = pl.BlockSpec((tile, D_MODEL), lambda i: (i, 0))
    return pl.pallas_call(
        body, grid=(t // tile,),
        in_specs=[nat, _full((1, D_MODEL))],
        out_specs=[nat] + [_group_spec(d, tile, D_MODEL) for d in DILATIONS[1:]],
        out_shape=[jax.ShapeDtypeStruct((t, D_MODEL), BF16)]
        + [jax.ShapeDtypeStruct((d, t // d, D_MODEL), BF16) for d in DILATIONS[1:]],
        scratch_shapes=[_slabs(tile, D_MODEL)],
        compiler_params=_cparams(1), name="norm1_fwd")(x, g)


GU_COLS = 3072
GROUP_COLS = 1536
GU_HALF = GU_COLS // 2


def _w_in_spec(width, block):
    return pl.BlockSpec((D_MODEL, width), lambda i: (0, block), pipeline_mode=pl.Buffered(1))


def _gu_w_specs():
    first = QKV_BLOCKS * ATTN_W // GU_HALF
    return [_w_in_spec(GU_HALF, first), _w_in_spec(GU_HALF, first + 1)]


def _group_w_specs(g):
    return [_w_in_spec(ATTN_W, _w_in_block(part, g)) for part in range(3)]


def _in_proj(hs, w_in, tables):
    t = hs[0].shape[0]
    tm = min(t, 1024)

    def body_gu(h_ref, w0_ref, w1_ref, o_ref):
        h = h_ref[...]
        o_ref[:, 0:GU_HALF] = jnp.dot(h, w0_ref[...], preferred_element_type=F32).astype(BF16)
        o_ref[:, GU_HALF:] = jnp.dot(h, w1_ref[...], preferred_element_type=F32).astype(BF16)

    gu = _token_call("in_proj_gates_uv", body_gu, t, tm,
                     [(hs[0], _rows_spec(tm, D_MODEL))] + [(w_in, s) for s in _gu_w_specs()],
                     [((t, GU_COLS), BF16, _rows_spec(tm, GU_COLS))])[0]

    qkvs = []
    for g in range(len(DILATIONS)):

        def body_qkv(h_ref, wq_ref, wk_ref, wv_ref, cos_ref, sin_ref, o_ref):
            h = h_ref[...]
            cos_w, sin_w = cos_ref[...], sin_ref[...]
            q = jnp.dot(h, wq_ref[...], preferred_element_type=F32)
            o_ref[:, 0:ATTN_W] = (_rope(q, cos_w, sin_w) * HEAD_DIM ** -0.5).astype(BF16)
            k = jnp.dot(h, wk_ref[...], preferred_element_type=F32)
            o_ref[:, ATTN_W:2 * ATTN_W] = _rope(k, cos_w, sin_w).astype(BF16)
            o_ref[:, 2 * ATTN_W:] = jnp.dot(h, wv_ref[...], preferred_element_type=F32).astype(BF16)

        cos_t, sin_t = tables[g]
        qkvs.append(_token_call(
            f"in_proj_qkv_g{g}", body_qkv, t, tm,
            [(hs[g].reshape(t, D_MODEL), _rows_spec(tm, D_MODEL))] + [(w_in, s) for s in _group_w_specs(g)]
            + [(cos_t, _rows_spec(tm, LANES)), (sin_t, _rows_spec(tm, LANES))],
            [((t, GROUP_COLS), BF16, _rows_spec(tm, GROUP_COLS))])[0])
    return gu, qkvs


def _attn_masks(n):
    row = lax.broadcasted_iota(jnp.int32, (2 * BLK, 2 * BLK), 0) % BLK
    col = lax.broadcasted_iota(jnp.int32, (2 * BLK, 2 * BLK), 1)
    diff = BLK + row - col
    valid = (diff >= 0) & (diff <= BLK) & ((col >= BLK) | (n > 0))
    upper = lax.broadcasted_iota(jnp.int32, (BLK, LANES), 1) >= HEAD_DIM
    return valid, upper


def _stack_heads(v2, upper):
    zero = jnp.zeros_like(v2)
    return jnp.concatenate([jnp.where(upper, zero, v2), jnp.where(upper, v2, zero)], axis=0)


def _unstack_heads(v, upper):
    return jnp.where(upper, v[BLK:], v[:BLK])


def _attn_fwd(qkv, g, dil):
    t = qkv.shape[0]
    length = t // dil
    nb = length // BLK
    per_step = min(nb, ATTN_BLOCKS_PER_STEP)
    view = qkv.reshape(dil, length, GROUP_COLS)

    def body(q_ref, kc_ref, kp_ref, vc_ref, vp_ref, o_ref, l_ref, kwin, vwin):
        n = pl.program_id(1)
        kwin[0:BLK] = kp_ref[...]
        kwin[BLK:] = kc_ref[...]
        vwin[0:BLK] = vp_ref[...]
        vwin[BLK:] = vc_ref[...]

        def block(b, carry):
            valid, upper = _attn_masks(n * per_step + b)
            rows = pl.ds(pl.multiple_of(b * BLK, BLK), BLK)
            window = pl.ds(pl.multiple_of(b * BLK, BLK), 2 * BLK)
            slabs = [slice(p * LANES, (p + 1) * LANES) for p in range(ATTN_W // LANES)]
            ss = [lax.dot_general(_stack_heads(q_ref[rows, sl], upper), kwin[window, sl], (NT, ((), ())),
                                  preferred_element_type=F32) for sl in slabs]
            soft = []
            for s in ss:
                s = jnp.where(valid, s, NEG)
                m = jnp.max(s, axis=1, keepdims=True)
                pe = jnp.exp(s - m)
                soft.append((m, pe, jnp.sum(pe, axis=1, keepdims=True)))
            for sl, (m, pe, den) in zip(slabs, soft):
                o = jnp.dot(pe.astype(BF16), vwin[window, sl], preferred_element_type=F32) / den
                lse = jnp.broadcast_to(m + jnp.log(den), (2 * BLK, LANES))
                o_ref[rows, sl] = _unstack_heads(o, upper).astype(BF16)
                l_ref[rows, sl] = _unstack_heads(lse, upper)
            return carry

        lax.fori_loop(0, per_step, block, 0)

    rows = per_step * BLK
    cur = lambda part: pl.BlockSpec((None, rows, ATTN_W), lambda r, n: (r, n, part))
    prev = lambda part: pl.BlockSpec((None, BLK, ATTN_W), lambda r, n: (r, jnp.maximum(n * per_step - 1, 0), part))
    out_spec = pl.BlockSpec((None, rows, ATTN_W), lambda r, n: (r, n, 0))
    return pl.pallas_call(
        body, grid=(dil, nb // per_step),
        in_specs=[cur(0), cur(1), prev(1), cur(2), prev(2)],
        out_specs=[out_spec, out_spec],
        out_shape=[jax.ShapeDtypeStruct((dil, length, ATTN_W), BF16), jax.ShapeDtypeStruct((dil, length, ATTN_W), F32)],
        scratch_shapes=[pltpu.VMEM((rows + BLK, ATTN_W), BF16)] * 2,
        compiler_params=_cparams(2), name=f"attn_fwd_g{g}")(view, view, view, view, view)


def _alphas(l0, l1, l2):
    m = jnp.maximum(jnp.maximum(l0, l1), l2)
    e0, e1, e2 = jnp.exp(l0 - m), jnp.exp(l1 - m), jnp.exp(l2 - m)
    inv = 1.0 / (e0 + e1 + e2)
    return e0 * inv, e1 * inv, e2 * inv


def _natural_group_values(o_refs, l_refs, slabs):
    os_ = [o_refs[0][0].astype(F32)] + [_natural_from_group(slabs[2 * g - 2], o_refs[g]) for g in (1, 2)]
    ls_ = [l_refs[0][0]] + [_natural_from_group(slabs[2 * g - 1], l_refs[g]) for g in (1, 2)]
    return os_, ls_


def _combine_fwd(os_, ls_):
    t = os_[0].shape[1]
    tile = min(t, TILE)

    def body(o0, o1, o2, l0, l1, l2, a_ref, *slabs):
        ov, lv = _natural_group_values((o0, o1, o2), (l0, l1, l2), slabs)
        a0, a1, a2 = _alphas(*lv)
        a_ref[...] = (a0 * ov[0] + a1 * ov[1] + a2 * ov[2]).astype(BF16)

    specs = [_group_spec(d, tile, ATTN_W) for d in DILATIONS]
    return pl.pallas_call(
        body, grid=(t // tile,), in_specs=specs * 2, out_specs=pl.BlockSpec((tile, ATTN_W), lambda i: (i, 0)),
        out_shape=jax.ShapeDtypeStruct((t, ATTN_W), BF16),
        scratch_shapes=[_slabs(tile, ATTN_W)] * 4,
        compiler_params=_cparams(1), name="combine_fwd")(*os_, *ls_)


def _combine_bwd(dattn, os_, ls_):
    t = dattn.shape[0]
    tile = min(t, TILE)
    e = _head_sum_matrix()

    def body(d_ref, o0, o1, o2, l0, l1, l2, e_ref, do0, do1, do2, c0, c1, c2, *slabs):
        ov, lv = _natural_group_values((o0, o1, o2), (l0, l1, l2), slabs)
        alphas = _alphas(*lv)
        d = d_ref[...]
        attn = alphas[0] * ov[0] + alphas[1] * ov[1] + alphas[2] * ov[2]
        s = _group_sum(d * attn, e_ref[...])
        do0[0] = (alphas[0] * d).astype(BF16)
        c0[0] = -alphas[0] * s
        for g, do_ref, c_ref in ((1, do1, c1), (2, do2, c2)):
            _group_from_natural(slabs[2 * g - 2], do_ref, alphas[g] * d)
            _group_from_natural(slabs[2 * g - 1], c_ref, -alphas[g] * s)

    specs = [_group_spec(d, tile, ATTN_W) for d in DILATIONS]
    shapes = [(d, t // d, ATTN_W) for d in DILATIONS]
    outs = pl.pallas_call(
        body, grid=(t // tile,),
        in_specs=[pl.BlockSpec((tile, ATTN_W), lambda i: (i, 0))] + specs * 2 + [_full((ATTN_W, ATTN_W))],
        out_specs=specs * 2,
        out_shape=[jax.ShapeDtypeStruct(s, BF16) for s in shapes] + [jax.ShapeDtypeStruct(s, F32) for s in shapes],
        scratch_shapes=[_slabs(tile, ATTN_W)] * 4,
        compiler_params=_cparams(1), name="combine_bwd")(dattn, *os_, *ls_, e)
    return outs[:3], outs[3:]


def _attn_bwd(qkv, do, cc, lse, cos_t, sin_t, g, dil):
    t = qkv.shape[0]
    length = t // dil
    nb = length // BLK
    per_step = min(nb, ATTN_BLOCKS_PER_STEP)
    nsteps = nb // per_step
    rows_per_step = per_step * BLK
    qkv_v = qkv.reshape(dil, length, GROUP_COLS)
    cos_v, sin_v = (a.reshape(dil, length, LANES) for a in (cos_t, sin_t))
    scale = HEAD_DIM ** -0.5
    dq_cols, dk_cols, dv_cols = (slice(i * ATTN_W, (i + 1) * ATTN_W) for i in range(3))

    def body(q_ref, kc_ref, kp_ref, vc_ref, vp_ref, do_ref, c_ref, l_ref, cosc, sinc, cosp, sinp,
             out_ref, acc, kwin, vwin, cwin, swin):
        n = pl.program_id(1)

        def one_block(b):
            valid, upper = _attn_masks(n * per_step + b)
            start = b * BLK if isinstance(b, int) else pl.multiple_of(b * BLK, BLK)
            rows, before, window = pl.ds(start, BLK), pl.ds(start, BLK), pl.ds(start, 2 * BLK)
            own = pl.ds(start + BLK, BLK)
            dq_parts, dkp_parts, dkc_parts, dvp_parts, dvc_parts = [], [], [], [], []
            npairs = ATTN_W // LANES
            slabs = [slice(p * LANES, (p + 1) * LANES) for p in range(npairs)]
            qss = [_stack_heads(q_ref[rows, sl], upper) for sl in slabs]
            doss = [_stack_heads(do_ref[rows, sl], upper) for sl in slabs]
            ss = [lax.dot_general(qss[p], kwin[window, slabs[p]], (NT, ((), ())), preferred_element_type=F32) for p in range(npairs)]
            dpvs = [lax.dot_general(doss[p], vwin[window, slabs[p]], (NT, ((), ())), preferred_element_type=F32)
                    for p in range(npairs)]
            pes = [jnp.exp(jnp.where(valid, ss[p], NEG) - _spread_heads(l_ref[rows, slabs[p]], upper)) for p in range(npairs)]
            dss = [(pes[p] * (dpvs[p] + _spread_heads(c_ref[rows, slabs[p]], upper))).astype(BF16) for p in range(npairs)]
            for p in range(npairs):
                qs, dos, ds = qss[p], doss[p], dss[p]
                dq2 = _unstack_heads(jnp.dot(ds, kwin[window, slabs[p]], preferred_element_type=F32), upper)
                dk2 = lax.dot_general(ds, qs, (TN, ((), ())), preferred_element_type=F32)
                dv2 = lax.dot_general(pes[p].astype(BF16), dos, (TN, ((), ())), preferred_element_type=F32)
                dq_parts.append(dq2)
                dkp_parts.append(dk2[:BLK])
                dkc_parts.append(dk2[BLK:])
                dvp_parts.append(dv2[:BLK])
                dvc_parts.append(dv2[BLK:])
            dq = _rope(jnp.concatenate(dq_parts, axis=1) * scale, cwin[own, :], swin[own, :])
            dkc = _rope(jnp.concatenate(dkc_parts, axis=1), cwin[own, :], swin[own, :])
            dkp = _rope(jnp.concatenate(dkp_parts, axis=1), cwin[before, :], swin[before, :])
            return dq, dkp, dkc, jnp.concatenate(dvp_parts, axis=1), jnp.concatenate(dvc_parts, axis=1)

        @pl.when(n < nsteps)
        def _():
            kwin[0:BLK] = kp_ref[...]
            kwin[BLK:] = kc_ref[...]
            vwin[0:BLK] = vp_ref[...]
            vwin[BLK:] = vc_ref[...]
            cwin[0:BLK] = cosp[...]
            cwin[BLK:] = cosc[...]
            swin[0:BLK] = -sinp[...]
            swin[BLK:] = -sinc[...]
            dq, dkp, dkc, dvp, dvc = one_block(0)
            last = slice(rows_per_step - BLK, rows_per_step)

            @pl.when(n > 0)
            def _():
                if per_step > 1:
                    out_ref[0:rows_per_step - BLK, :] = acc[0:rows_per_step - BLK, :].astype(BF16)
                out_ref[last, dq_cols] = acc[last, dq_cols].astype(BF16)
                out_ref[last, dk_cols] = (acc[last, dk_cols] + dkp).astype(BF16)
                out_ref[last, dv_cols] = (acc[last, dv_cols] + dvp).astype(BF16)

            acc[0:BLK, dq_cols] = dq
            acc[0:BLK, dk_cols] = dkc
            acc[0:BLK, dv_cols] = dvc

            def later(b, carry):
                dq, dkp, dkc, dvp, dvc = one_block(b)
                start = pl.multiple_of(b * BLK, BLK)
                before, rows = pl.ds(start - BLK, BLK), pl.ds(start, BLK)
                acc[before, dk_cols] += dkp
                acc[before, dv_cols] += dvp
                acc[rows, dq_cols] = dq
                acc[rows, dk_cols] = dkc
                acc[rows, dv_cols] = dvc
                return carry

            lax.fori_loop(1, per_step, later, 0)

        @pl.when(n == flush_at)
        def _():
            out_ref[...] = acc[...].astype(BF16)

    flush_at = nsteps - 1 if nsteps == 1 else nsteps
    out_lag = 0 if nsteps == 1 else 1
    nc = lambda n: jnp.minimum(n, nsteps - 1)
    npv = lambda n: jnp.maximum(jnp.minimum(n, nsteps - 1) * per_step - 1, 0)
    cur = lambda part: pl.BlockSpec((None, rows_per_step, ATTN_W), lambda r, n: (r, nc(n), part))
    prev = lambda part: pl.BlockSpec((None, BLK, ATTN_W), lambda r, n: (r, npv(n), part))
    row = pl.BlockSpec((None, rows_per_step, ATTN_W), lambda r, n: (r, nc(n), 0))
    tab_c = pl.BlockSpec((None, rows_per_step, LANES), lambda r, n: (r, nc(n), 0))
    tab_p = pl.BlockSpec((None, BLK, LANES), lambda r, n: (r, npv(n), 0))
    out_spec = pl.BlockSpec((None, rows_per_step, GROUP_COLS), lambda r, n: (r, jnp.maximum(n - out_lag, 0), 0))
    out = pl.pallas_call(
        body, grid=(dil, nsteps + out_lag),
        in_specs=[cur(0), cur(1), prev(1), cur(2), prev(2), row, row, row, tab_c, tab_c, tab_p, tab_p],
        out_specs=out_spec,
        out_shape=jax.ShapeDtypeStruct((dil, length, GROUP_COLS), BF16),
        scratch_shapes=[pltpu.VMEM((rows_per_step, GROUP_COLS), F32)]
        + [pltpu.VMEM((rows_per_step + BLK, ATTN_W), BF16)] * 2 + [pltpu.VMEM((rows_per_step + BLK, LANES), F32)] * 2,
        compiler_params=_cparams(2), name=f"attn_bwd_g{g}")(
            qkv_v, qkv_v, qkv_v, qkv_v, qkv_v, do, cc, lse, cos_v, sin_v, cos_v, sin_v)
    return out.reshape(t, GROUP_COLS)


SQRT_HALF = 0.7071067811865476
INV_SQRT_2PI = 0.3989422804014327


def _sgu_core(uv, g, b, w_ref, bias):
    cdf = 0.5 * (1.0 + lax.erf(uv * SQRT_HALF))
    z = uv * cdf
    u, v = z[:, :SGU_W], z[:, SGU_W:]
    mu = jnp.mean(v, axis=1, keepdims=True)
    xc = v - mu
    rs = lax.rsqrt(jnp.mean(xc * xc, axis=1, keepdims=True) + EPS)
    xhat = xc * rs
    vn = xhat * g + b
    row = lax.broadcasted_iota(jnp.int32, (SGU_CHUNK, SGU_CHUNK), 0)
    col = lax.broadcasted_iota(jnp.int32, (SGU_CHUNK, SGU_CHUNK), 1)
    tril = row >= col
    upper = lax.broadcasted_iota(jnp.int32, (SGU_CHUNK, LANES), 1) >= SGU_W // SGU_GROUPS
    ws, vlo, vhi, mixed = [], [], [], []
    for pr in range(SGU_W // LANES):
        sl = slice(pr * LANES, (pr + 1) * LANES)
        w0 = jnp.where(tril, w_ref[2 * pr], 0.0).astype(BF16)
        w1 = jnp.where(tril, w_ref[2 * pr + 1], 0.0).astype(BF16)
        vn2 = vn[:, sl]
        lo = jnp.where(upper, 0.0, vn2).astype(BF16)
        hi = jnp.where(upper, vn2, 0.0).astype(BF16)
        mixed.append(jnp.dot(w0, lo, preferred_element_type=F32) + jnp.dot(w1, hi, preferred_element_type=F32)
                     + bias[:, sl])
        ws.append((w0, w1))
        vlo.append(lo)
        vhi.append(hi)
    return cdf, u, xhat, rs, jnp.concatenate(mixed, axis=1), ws, vlo, vhi, tril, upper


SGU_STEP = 4 * SGU_CHUNK


def _for_chunks(step_rows, fn):
    def one(ci, carry):
        fn(pl.ds(pl.multiple_of(ci * SGU_CHUNK, SGU_CHUNK), SGU_CHUNK))
        return carry

    lax.fori_loop(0, step_rows // SGU_CHUNK, one, 0)


def _sgu_fwd(gu, ln_g, ln_b, w_s, bias_exp):
    t = gu.shape[0]
    step = min(t, SGU_STEP)

    def body(uv_ref, g_ref, b_ref, w_ref, bias_ref, o_ref):
        def chunk(rows):
            _, u, _, _, mixed, *_ = _sgu_core(uv_ref[rows, :].astype(F32), g_ref[...], b_ref[...], w_ref, bias_ref[...])
            o_ref[rows, :] = (u * mixed).astype(BF16)

        _for_chunks(step, chunk)

    return pl.pallas_call(
        body, grid=(t // step,),
        in_specs=[pl.BlockSpec((step, 2 * SGU_W), lambda n: (n, 0)), _full((1, SGU_W)), _full((1, SGU_W)),
                  _full((SGU_GROUPS, SGU_CHUNK, SGU_CHUNK)), _full((SGU_CHUNK, SGU_W))],
        out_specs=pl.BlockSpec((step, SGU_W), lambda n: (n, 0)),
        out_shape=jax.ShapeDtypeStruct((t, SGU_W), BF16),
        compiler_params=_cparams(1), name="sgu_fwd")(gu, ln_g, ln_b, w_s, bias_exp)


def _sgu_bwd(dproj, gu, dsgu, ln_g, ln_b, w_s, bias_exp):
    t = gu.shape[0]
    step = min(t, SGU_STEP)
    nsteps = t // step
    e = _head_sum_matrix()

    def body(dp_in, uv_ref, ds_ref, g_ref, b_ref, w_ref, bias_ref, e_ref, out_ref, dw_ref, dbias_ref, dg_ref, db_ref):
        n = pl.program_id(0)

        @pl.when(n == 0)
        def _():
            dw_ref[...] = jnp.zeros(dw_ref.shape, F32)
            dbias_ref[...] = jnp.zeros(dbias_ref.shape, F32)
            dg_ref[...] = jnp.zeros(dg_ref.shape, F32)
            db_ref[...] = jnp.zeros(db_ref.shape, F32)

        _for_chunks(step, functools.partial(chunk, uv_ref, ds_ref, g_ref, b_ref, w_ref, bias_ref, out_ref, dw_ref, dbias_ref,
                                            dg_ref, db_ref))

        @pl.when(n == nsteps - 1)
        def _():
            dbias_ref[...] = _group_sum(dbias_ref[...], e_ref[...])

    def chunk(uv_ref, ds_ref, g_ref, b_ref, w_ref, bias_ref, out_ref, dw_ref, dbias_ref, dg_ref, db_ref, rows):
        uv = uv_ref[rows, :].astype(F32)
        g = g_ref[...]
        cdf, u, xhat, rs, mixed, ws, vlo, vhi, tril, upper = _sgu_core(uv, g, b_ref[...], w_ref, bias_ref[...])
        dsg = ds_ref[rows, :]
        du = dsg * mixed
        dmixed = dsg * u
        dbias_ref[...] += dmixed
        dvn = []
        for pr in range(SGU_W // LANES):
            sl = slice(pr * LANES, (pr + 1) * LANES)
            dm2 = dmixed[:, sl]
            dlo = jnp.where(upper, 0.0, dm2).astype(BF16)
            dhi = jnp.where(upper, dm2, 0.0).astype(BF16)
            w0, w1 = ws[pr]
            dvn.append(lax.dot_general(w0, dlo, (TN, ((), ())), preferred_element_type=F32)
                       + lax.dot_general(w1, dhi, (TN, ((), ())), preferred_element_type=F32))
            dw0 = lax.dot_general(dlo, vlo[pr], (NT, ((), ())), preferred_element_type=F32)
            dw1 = lax.dot_general(dhi, vhi[pr], (NT, ((), ())), preferred_element_type=F32)
            dw_ref[2 * pr] += jnp.where(tril, dw0, 0.0)
            dw_ref[2 * pr + 1] += jnp.where(tril, dw1, 0.0)
        dvn = jnp.concatenate(dvn, axis=1)
        dg_ref[...] += jnp.sum(dvn * xhat, axis=0, keepdims=True)
        db_ref[...] += jnp.sum(dvn, axis=0, keepdims=True)
        dxh = dvn * g
        dv = rs * (dxh - jnp.mean(dxh, axis=1, keepdims=True) - xhat * jnp.mean(dxh * xhat, axis=1, keepdims=True))
        dz = jnp.concatenate([du, dv], axis=1)
        dgelu = cdf + uv * (INV_SQRT_2PI * jnp.exp(-0.5 * uv * uv))
        out_ref[rows, :] = (dz * dgelu).astype(BF16)

    outs = pl.pallas_call(
        body, grid=(nsteps,),
        in_specs=[pl.BlockSpec(memory_space=pl.ANY), pl.BlockSpec((step, 2 * SGU_W), lambda n: (n, 0)),
                  pl.BlockSpec((step, SGU_W), lambda n: (n, 0)), _full((1, SGU_W)), _full((1, SGU_W)),
                  _full((SGU_GROUPS, SGU_CHUNK, SGU_CHUNK)), _full((SGU_CHUNK, SGU_W)), _full((ATTN_W, ATTN_W))],
        out_specs=[pl.BlockSpec((step, 2 * SGU_W), lambda n: (n, 0)), _full((SGU_GROUPS, SGU_CHUNK, SGU_CHUNK)),
                   _full((SGU_CHUNK, SGU_W)), _full((1, SGU_W)), _full((1, SGU_W))],
        out_shape=[jax.ShapeDtypeStruct(dproj.shape, BF16), jax.ShapeDtypeStruct((SGU_GROUPS, SGU_CHUNK, SGU_CHUNK), F32),
                   jax.ShapeDtypeStruct((SGU_CHUNK, SGU_W), F32), jax.ShapeDtypeStruct((1, SGU_W), F32),
                   jax.ShapeDtypeStruct((1, SGU_W), F32)],
        input_output_aliases={0: 0},
        compiler_params=_cparams(1), name="sgu_bwd")(dproj, gu, dsgu, ln_g, ln_b, w_s, bias_exp, e)
    return outs


def _merge_fwd(attn, sgu, gu, x, w_pa, w_ps, w_out, g2):
    t = x.shape[0]
    tm = min(t, 512)

    def body(a_ref, s_ref, ga_ref, gb_ref, x_ref, wpa, wps, wo, g_ref, pa_ref, ps_ref, m_ref, x1_ref, h2_ref):
        pa = jnp.dot(a_ref[...], wpa[...], preferred_element_type=F32)
        ps = jnp.dot(s_ref[...], wps[...], preferred_element_type=F32)
        merged = (_sigmoid(ga_ref[...].astype(F32)) * pa + _sigmoid(gb_ref[...].astype(F32)) * ps).astype(BF16)
        x1 = x_ref[...] + jnp.dot(merged, wo[...], preferred_element_type=F32)
        xhat, _ = _rms_stats(x1)
        pa_ref[...] = pa.astype(BF16)
        ps_ref[...] = ps.astype(BF16)
        m_ref[...] = merged
        x1_ref[...] = x1
        h2_ref[...] = (xhat * g_ref[...]).astype(BF16)

    half = pl.BlockSpec((tm, ATTN_W), lambda i: (i, 0))
    full = pl.BlockSpec((tm, D_MODEL), lambda i: (i, 0))
    return pl.pallas_call(
        body, grid=(t // tm,),
        in_specs=[half, half, pl.BlockSpec((tm, D_MODEL), lambda i: (i, 1)), pl.BlockSpec((tm, D_MODEL), lambda i: (i, 2)),
                  full, _resident((ATTN_W, D_MODEL)), _resident((SGU_W, D_MODEL)), _resident((D_MODEL, D_MODEL)),
                  _full((1, D_MODEL))],
        out_specs=[full] * 5,
        out_shape=[jax.ShapeDtypeStruct((t, D_MODEL), BF16), jax.ShapeDtypeStruct((t, D_MODEL), BF16),
                   jax.ShapeDtypeStruct((t, D_MODEL), BF16), jax.ShapeDtypeStruct((t, D_MODEL), F32),
                   jax.ShapeDtypeStruct((t, D_MODEL), BF16)],
        compiler_params=_cparams(1), name="merge_fwd")(attn, sgu, gu, gu, x, w_pa, w_ps, w_out, g2)


def _merge_bwd(dx1b, gu, pa, ps, w_pa, w_ps, w_out):
    t = dx1b.shape[0]
    tm = min(t, 512)

    def body(d_ref, ga_ref, gb_ref, pa_ref, ps_ref, wpa, wps, wo, out_ref, dpa_ref, dps_ref, da_ref, dsg_ref):
        dm = lax.dot_general(d_ref[...], wo[...], (NT, ((), ())), preferred_element_type=F32)
        sa, sb = _sigmoid(ga_ref[...].astype(F32)), _sigmoid(gb_ref[...].astype(F32))
        dpa = (dm * sa).astype(BF16)
        dps = (dm * sb).astype(BF16)
        out_ref[:, 0:D_MODEL] = jnp.zeros((tm, D_MODEL), BF16)
        out_ref[:, D_MODEL:2 * D_MODEL] = (dm * pa_ref[...].astype(F32) * sa * (1.0 - sa)).astype(BF16)
        out_ref[:, 2 * D_MODEL:GU_COLS] = (dm * ps_ref[...].astype(F32) * sb * (1.0 - sb)).astype(BF16)
        dpa_ref[...] = dpa
        dps_ref[...] = dps
        da_ref[...] = lax.dot_general(dpa, wpa[...], (NT, ((), ())), preferred_element_type=F32)
        dsg_ref[...] = lax.dot_general(dps, wps[...], (NT, ((), ())), preferred_element_type=F32)

    half = pl.BlockSpec((tm, ATTN_W), lambda i: (i, 0))
    full = pl.BlockSpec((tm, D_MODEL), lambda i: (i, 0))
    return pl.pallas_call(
        body, grid=(t // tm,),
        in_specs=[full, pl.BlockSpec((tm, D_MODEL), lambda i: (i, 1)),
                  pl.BlockSpec((tm, D_MODEL), lambda i: (i, 2)), full, full,
                  _resident((ATTN_W, D_MODEL)), _resident((SGU_W, D_MODEL)), _resident((D_MODEL, D_MODEL))],
        out_specs=[pl.BlockSpec((tm, GU_COLS), lambda i: (i, 0)), full, full, half, half],
        out_shape=[jax.ShapeDtypeStruct((t, GU_COLS), BF16), jax.ShapeDtypeStruct((t, D_MODEL), BF16),
                   jax.ShapeDtypeStruct((t, D_MODEL), BF16), jax.ShapeDtypeStruct((t, ATTN_W), F32),
                   jax.ShapeDtypeStruct((t, SGU_W), F32)],
        compiler_params=_cparams(1), name="merge_bwd")(dx1b, gu, gu, pa, ps, w_pa, w_ps, w_out)


def _token_call(name, body, t, tm, ins, outs, reds=(), scratch=()):
    return pl.pallas_call(
        body, grid=(t // tm,), in_specs=[s for _, s in ins],
        out_specs=[o[2] for o in outs] + [_full(r) for r in reds],
        out_shape=[jax.ShapeDtypeStruct(o[0], o[1]) for o in outs] + [jax.ShapeDtypeStruct(r, F32) for r in reds],
        scratch_shapes=list(scratch), compiler_params=_cparams(1), name=name)(*[a for a, _ in ins])


def _rows_spec(tm, width):
    return pl.BlockSpec((tm, width), lambda i: (i, 0))


def _chips_spec(tm):
    return pl.BlockSpec((N_CHIPS, tm, FF_SHARD), lambda i: (0, i, 0))


def _zero_at_start(*refs):
    @pl.when(pl.program_id(0) == 0)
    def _():
        for r in refs:
            r[...] = jnp.zeros(r.shape, r.dtype)


def _ffn_fwd(h2, w_g, w_u):
    t = h2.shape[0]
    tm = min(t, 512)

    def body(h_ref, wg_ref, wu_ref, fa_ref, fb_ref, ff_ref):
        h = h_ref[...]
        for s in range(N_CHIPS):
            a = jnp.dot(h, wg_ref[s], preferred_element_type=F32)
            b = jnp.dot(h, wu_ref[s], preferred_element_type=F32)
            sg = _sigmoid(a)
            silu = a * sg
            fa_ref[s] = (b * (sg * (1.0 + a * (1.0 - sg)))).astype(BF16)
            fb_ref[s] = silu.astype(BF16)
            ff_ref[s] = (silu * b).astype(BF16)

    shp = (N_CHIPS, t, FF_SHARD)
    w_spec = _resident((N_CHIPS, D_MODEL, FF_SHARD))
    return _token_call("ffn_fwd", body, t, tm, [(h2, _rows_spec(tm, D_MODEL)), (w_g, w_spec), (w_u, w_spec)],
                       [(shp, BF16, _chips_spec(tm))] * 3)


def _ffn_down_loss(ff, w_d, x1, tgt, gf):
    t = x1.shape[0]
    tm = min(t, 512)

    def body(ff_ref, wd_ref, x1_ref, tgt_ref, g_ref, dx2_ref, dx2b_ref, loss_ref, dgf_ref):
        _zero_at_start(loss_ref, dgf_ref)
        acc = jnp.dot(ff_ref[0], wd_ref[0], preferred_element_type=F32)
        for s in range(1, N_CHIPS):
            acc = acc + jnp.dot(ff_ref[s], wd_ref[s], preferred_element_type=F32)
        x2 = x1_ref[...] + acc
        g = g_ref[...]
        xhat, rr = _rms_stats(x2)
        diff = xhat * g - tgt_ref[...]
        rows = jnp.sum(diff * diff, axis=1, keepdims=True)
        loss_ref[...] += jnp.broadcast_to(jnp.sum(rows, axis=0, keepdims=True) * (0.5 / D_MODEL), (1, LANES))
        dy = diff * (1.0 / D_MODEL)
        dgf_ref[...] += jnp.sum(dy * xhat, axis=0, keepdims=True)
        dx2 = _rms_bwd(dy, xhat, rr, g)
        dx2_ref[...] = dx2
        dx2b_ref[...] = dx2.astype(BF16)

    row = _rows_spec(tm, D_MODEL)
    return _token_call("ffn_down_loss", body, t, tm,
                       [(ff, _chips_spec(tm)), (w_d, _resident((N_CHIPS, FF_SHARD, D_MODEL))), (x1, row), (tgt, row),
                        (gf, _full((1, D_MODEL)))],
                       [((t, D_MODEL), F32, row), ((t, D_MODEL), BF16, row)], reds=[(1, LANES), (1, D_MODEL)])


def _ffn_bwd_act(dx2b, w_d, fa, fb):
    t = dx2b.shape[0]
    tm = min(t, 512)

    def body(d_ref, wd_ref, fa_ref, fb_ref, da_ref, db_ref):
        d = d_ref[...]
        for s in range(N_CHIPS):
            dff = lax.dot_general(d, wd_ref[s], (NT, ((), ())), preferred_element_type=F32)
            da_ref[s] = (dff * fa_ref[s].astype(F32)).astype(BF16)
            db_ref[s] = (dff * fb_ref[s].astype(F32)).astype(BF16)

    shp = (N_CHIPS, t, FF_SHARD)
    return _token_call("ffn_bwd_act", body, t, tm,
                       [(dx2b, _rows_spec(tm, D_MODEL)), (w_d, _resident((N_CHIPS, FF_SHARD, D_MODEL))),
                        (fa, _chips_spec(tm)), (fb, _chips_spec(tm))],
                       [(shp, BF16, _chips_spec(tm))] * 2)


def _ffn_bwd_in(da, db, w_g, w_u, x1, dx2, g2):
    t = x1.shape[0]
    tm = min(t, 512)

    def body(da_ref, db_ref, wg_ref, wu_ref, x1_ref, dx2_ref, g_ref, dx1_ref, dx1b_ref, dg_ref):
        _zero_at_start(dg_ref)
        acc = None
        for s in range(N_CHIPS):
            part = (lax.dot_general(da_ref[s], wg_ref[s], (NT, ((), ())), preferred_element_type=F32)
                    + lax.dot_general(db_ref[s], wu_ref[s], (NT, ((), ())), preferred_element_type=F32))
            acc = part if acc is None else acc + part
        xhat, rr = _rms_stats(x1_ref[...])
        dg_ref[...] += jnp.sum(acc * xhat, axis=0, keepdims=True)
        dx1 = dx2_ref[...] + _rms_bwd(acc, xhat, rr, g_ref[...])
        dx1_ref[...] = dx1
        dx1b_ref[...] = dx1.astype(BF16)

    row = _rows_spec(tm, D_MODEL)
    w_spec = _resident((N_CHIPS, D_MODEL, FF_SHARD))
    return _token_call("ffn_bwd_in", body, t, tm,
                       [(da, _chips_spec(tm)), (db, _chips_spec(tm)), (w_g, w_spec), (w_u, w_spec), (x1, row), (dx2, row),
                        (g2, _full((1, D_MODEL)))],
                       [((t, D_MODEL), F32, row), ((t, D_MODEL), BF16, row)], reds=[(1, D_MODEL)])


def _group_dh(d, w_refs):
    dh = None
    for part, w_ref in enumerate(w_refs):
        term = lax.dot_general(d[:, part * ATTN_W:(part + 1) * ATTN_W], w_ref[...], (NT, ((), ())),
                               preferred_element_type=F32)
        dh = term if dh is None else dh + term
    return dh


def _in_proj_bwd(dgu, dqkvs, w_in, x, dx1, g1):
    t = x.shape[0]
    tile = min(t, TILE)
    ngroups = len(DILATIONS)

    def body(*refs):
        dgu_ref, dq_refs = refs[0], refs[1:1 + ngroups]
        w0_ref, w1_ref = refs[1 + ngroups:3 + ngroups]
        wg_refs = [refs[3 + ngroups + 3 * g:6 + ngroups + 3 * g] for g in range(ngroups)]
        x_ref, dx1_ref, g_ref, dx_ref, dg_ref = refs[3 + 4 * ngroups:5 + 4 * ngroups + 3]
        slabs = refs[5 + 4 * ngroups + 3:]
        _zero_at_start(dg_ref)
        for g in range(1, ngroups):
            dil = DILATIONS[g]
            part = _group_dh(dq_refs[g][...].reshape(tile, GROUP_COLS), wg_refs[g])
            for r in range(dil):
                _put_class_rows(slabs[g - 1], r, dil, part[r * (tile // dil):(r + 1) * (tile // dil)])
        dh = lax.dot_general(dgu_ref[:, 0:GU_HALF], w0_ref[...], (NT, ((), ())), preferred_element_type=F32)
        dh = dh + lax.dot_general(dgu_ref[:, GU_HALF:], w1_ref[...], (NT, ((), ())), preferred_element_type=F32)
        dh = dh + _group_dh(dq_refs[0][0], wg_refs[0])
        for slab in slabs:
            dh = dh + _from_slabs(slab)
        xhat, rr = _rms_stats(x_ref[...])
        dg_ref[...] += jnp.sum(dh * xhat, axis=0, keepdims=True)
        dx_ref[...] = dx1_ref[...] + _rms_bwd(dh, xhat, rr, g_ref[...])

    row = _rows_spec(tile, D_MODEL)
    group_ins = [(dqkvs[g].reshape(d, t // d, GROUP_COLS), _group_spec(d, tile, GROUP_COLS)) for g, d in enumerate(DILATIONS)]
    w_specs = _gu_w_specs() + [s for g in range(ngroups) for s in _group_w_specs(g)]
    return _token_call(
        "in_proj_bwd", body, t, tile,
        [(dgu, _rows_spec(tile, GU_COLS))] + group_ins + [(w_in, s) for s in w_specs]
        + [(x, row), (dx1, row), (g1, _full((1, D_MODEL)))],
        [((t, D_MODEL), F32, row)], reds=[(1, D_MODEL)], scratch=[_slabs(tile, D_MODEL)] * (ngroups - 1))


WGRAD_TK = 2048


def _wgrad_mm(name, grid, a, a_spec, b, b_spec, acc_shape, out_shape, out_spec, dst=None):
    nk = grid[-1]

    def body(*refs):
        a_ref, b_ref, o_ref, acc_ref = refs[0], refs[1], refs[-2], refs[-1]
        k = pl.program_id(len(grid) - 1)
        part = lax.dot_general(a_ref[...], b_ref[...], (TN, ((), ())), preferred_element_type=F32)

        @pl.when(k == 0)
        def _():
            acc_ref[...] = part

        @pl.when(k > 0)
        def _():
            acc_ref[...] += part

        @pl.when(k == nk - 1)
        def _():
            o_ref[...] = acc_ref[...].astype(BF16)

    filled = [] if dst is None else [dst]
    return pl.pallas_call(
        body, grid=grid, in_specs=[a_spec, b_spec] + [pl.BlockSpec(memory_space=pl.ANY)] * len(filled),
        out_specs=out_spec, out_shape=jax.ShapeDtypeStruct(out_shape, BF16), scratch_shapes=[pltpu.VMEM(acc_shape, F32)],
        input_output_aliases={2: 0} if filled else {}, compiler_params=_cparams(len(grid)), name=name)(a, b, *filled)


def _wgrad_2d(name, a, b, tm, tn):
    t, k1 = a.shape
    n = b.shape[1]
    tk = min(t, WGRAD_TK)
    return _wgrad_mm(name, (k1 // tm, n // tn, t // tk), a, pl.BlockSpec((tk, tm), lambda i, j, k: (k, i)),
                     b, pl.BlockSpec((tk, tn), lambda i, j, k: (k, j)), (tm, tn), (k1, n),
                     pl.BlockSpec((tm, tn), lambda i, j, k: (i, j)))


def _wgrad_in(hs, dgu, dqkvs):
    t = dgu.shape[0]
    tk = min(t, WGRAD_TK)
    gu_block = QKV_BLOCKS * ATTN_W // GU_HALF
    parts = [(hs[0], dgu, GU_HALF, lambda j: j + gu_block)]
    parts += [(hs[g].reshape(t, D_MODEL), dqkvs[g], ATTN_W, lambda j, g=g: _w_in_block(j, g)) for g in range(3)]
    dst = None
    for n, (a, b, tn, block_of) in enumerate(parts):
        dst = _wgrad_mm(f"wgrad_in_{n}", (1, b.shape[1] // tn, t // tk),
                        a, pl.BlockSpec((tk, D_MODEL), lambda i, j, k: (k, 0)), b, pl.BlockSpec((tk, tn), lambda i, j, k: (k, j)),
                        (D_MODEL, tn), (D_MODEL, IN_COLS),
                        pl.BlockSpec((D_MODEL, tn), lambda i, j, k, block_of=block_of: (0, block_of(j))), dst=dst)
    return dst


def _wgrad_ff_in(name, h2, da):
    t = h2.shape[0]
    tk = min(t, WGRAD_TK)
    return _wgrad_mm(name, (N_CHIPS, 1, t // tk), h2, pl.BlockSpec((tk, D_MODEL), lambda i, j, k: (k, 0)),
                     da, pl.BlockSpec((None, tk, FF_SHARD), lambda i, j, k: (i, k, 0)), (D_MODEL, FF_SHARD),
                     (N_CHIPS, D_MODEL, FF_SHARD), pl.BlockSpec((None, D_MODEL, FF_SHARD), lambda i, j, k: (i, 0, 0)))


def _wgrad_ff_down(ff, dx2b):
    t = dx2b.shape[0]
    tk = min(t, WGRAD_TK)
    return _wgrad_mm("wgrad_ffn_down", (N_CHIPS, 1, t // tk), ff, pl.BlockSpec((None, tk, FF_SHARD), lambda i, j, k: (i, k, 0)),
                     dx2b, pl.BlockSpec((tk, D_MODEL), lambda i, j, k: (k, 0)), (FF_SHARD, D_MODEL),
                     (N_CHIPS, FF_SHARD, D_MODEL), pl.BlockSpec((None, FF_SHARD, D_MODEL), lambda i, j, k: (i, 0, 0)))


def _local_step(x, pos_col, tgt, g1, ln_g, ln_b, w_s, b_s, g2, gf, first_weight, late_weights, on_grads=None):
    tables = _rope_tables(pos_col)
    bias_exp = jnp.repeat(jnp.transpose(b_s), SGU_W // SGU_GROUPS, axis=1)

    hs = _norm_fwd(x, g1)
    w_p = first_weight([hs[0], bias_exp] + [table for pair in tables for table in pair])
    gu, qkvs = _in_proj(hs, w_p, tables)
    os_, ls_ = [], []
    for g, dil in enumerate(DILATIONS):
        o, lse = _attn_fwd(qkvs[g], g, dil)
        os_.append(o)
        ls_.append(lse)
    attn = _combine_fwd(os_, ls_)
    sgu = _sgu_fwd(gu, ln_g, ln_b, w_s, bias_exp)
    w_pa, w_ps, w_out, w_g, w_u, w_d = late_weights(attn)
    pa, ps, merged, x1, h2 = _merge_fwd(attn, sgu, gu, x, w_pa, w_ps, w_out, g2)
    fa, fb, ff = _ffn_fwd(h2, w_g, w_u)
    dx2, dx2b, loss, dgf = _ffn_down_loss(ff, w_d, x1, tgt, gf)

    da, db = _ffn_bwd_act(dx2b, w_d, fa, fb)
    dw_d = _wgrad_ff_down(ff, dx2b)
    dx1, dx1b, dg2 = _ffn_bwd_in(da, db, w_g, w_u, x1, dx2, g2)
    dw_g = _wgrad_ff_in("wgrad_ffn_gate", h2, da)
    dw_u = _wgrad_ff_in("wgrad_ffn_up", h2, db)

    dgu, dpa, dps, dattn, dsgu = _merge_bwd(dx1b, gu, pa, ps, w_pa, w_ps, w_out)
    dw_out = _wgrad_2d("wgrad_out", merged, dx1b, D_MODEL, D_MODEL)
    dw_pa = _wgrad_2d("wgrad_proj_attn", attn, dpa, ATTN_W, D_MODEL)
    dw_ps = _wgrad_2d("wgrad_proj_sgu", sgu, dps, SGU_W, D_MODEL)
    if on_grads is not None:
        ln_g = ln_g + on_grads(1, dict(w_proj_attn=dw_pa, w_proj_sgu=dw_ps, w_out=dw_out, w_ffn_gate=dw_g, w_ffn_up=dw_u,
                                       w_ffn_down=dw_d))[:, :SGU_W]
    dgu, dw_s, dbias, dln_g, dln_b = _sgu_bwd(dgu, gu, dsgu, ln_g, ln_b, w_s, bias_exp)
    dos, ccs = _combine_bwd(dattn, os_, ls_)
    dqkvs = [_attn_bwd(qkvs[g], dos[g], ccs[g], ls_[g], *tables[g], g, dil) for g, dil in enumerate(DILATIONS)]
    dw_p = _wgrad_in(hs, dgu, dqkvs)
    if on_grads is not None:
        g1 = g1 + on_grads(0, dict(w_in=dw_p))
    dx, dg1 = _in_proj_bwd(dgu, dqkvs, w_p, x, dx1, g1)

    db_s = jnp.transpose(dbias[:, ::SGU_W // SGU_GROUPS])
    small = dict(loss=loss, norm1_g=dg1, sgu_ln_g=dln_g, sgu_ln_b=dln_b, w_spatial=dw_s, b_spatial=db_s,
                 norm2_g=dg2, final_g=dgf)
    big = dict(w_in=dw_p, w_proj_attn=dw_pa, w_proj_sgu=dw_ps, w_out=dw_out, w_ffn_gate=dw_g, w_ffn_up=dw_u,
               w_ffn_down=dw_d)
    return dx, big, small


def _ew(name, fn, ins, out_dtypes, after=()):
    shp = ins[0].shape
    rows, cols = shp
    tr = next((cand for cand in (128, 176) if rows % cand == 0 and rows > cand), rows)

    def body(*refs):
        res = fn(*[r[...] for r in refs[:len(ins)]])
        for o_ref, v in zip(refs[len(ins) + len(after):], res):
            o_ref[...] = v.astype(o_ref.dtype)

    spec = pl.BlockSpec((tr, cols), lambda i: (i, 0))
    return pl.pallas_call(
        body, grid=(rows // tr,), in_specs=[spec] * len(ins) + [pl.BlockSpec(memory_space=pl.ANY)] * len(after),
        out_specs=[spec] * len(out_dtypes), out_shape=[jax.ShapeDtypeStruct(shp, d) for d in out_dtypes],
        compiler_params=_cparams(1), name=name)(*ins, *after)


def _adamw_math(g, w, m, v):
    m = ADAM_B1 * m + (1.0 - ADAM_B1) * g
    v = ADAM_B2 * v + (1.0 - ADAM_B2) * (g * g)
    m_hat = m / (1.0 - ADAM_B1 ** ADAM_STEP)
    v_hat = v / (1.0 - ADAM_B2 ** ADAM_STEP)
    delta = -ADAM_LR * (m_hat / (jnp.sqrt(v_hat) + ADAM_EPS) + ADAM_WD * w)
    return delta, m, v


def _adamw(name, g, w, m, v):
    return _ew(name, lambda g_, w_, m_, v_: (g_,) + _adamw_math(g_, w_, m_, v_), [g, w, m, v], [F32] * 4)


VMEM_SPEC = pl.BlockSpec(memory_space=pltpu.VMEM)


def _for_row_chunks(rows, fn):
    ck = next(c for c in (64, 32, 16) if rows % c == 0)

    def step(i, carry):
        fn(pl.multiple_of(i * ck, ck), ck)
        return carry

    lax.fori_loop(0, rows // ck, step, 0)


def _place():
    x, y, c = lax.axis_index("x"), lax.axis_index("y"), lax.axis_index("c")
    chips = [(1 - x, y), (x, 1 - y), (1 - x, 1 - y)]
    return x, y, c, 2 * x + y, chips


def _rows(ref, start, size):
    if len(ref.shape) == 2:
        return ref.at[pl.ds(start, size), :]
    return ref.at[:, pl.ds(start, size), :]


def _gather_finish(name, shard, landed):
    k_rows, n = shard.shape
    kh = k_rows // 2

    def body(shard_hbm, land_hbm, out_ref, shard_ref, land_ref, loc, send, recv):
        x, y, c, me, chips = _place()
        sibling = (x, y, 1 - c)

        def window(core, chip):
            return out_ref.at[pl.ds(core * kh, kh), pl.ds(pl.multiple_of(chip * n, LANES), n)]

        loads = [pltpu.make_async_copy(land_hbm.at[j], land_ref.at[j], loc.at[0, j]) for j in range(3)]
        loads.append(pltpu.make_async_copy(shard_hbm, shard_ref, loc.at[0, 3]))
        for cp in loads:
            cp.start()
        copies, passed = [], []
        for j, chip in enumerate(chips):
            mine = window(c, 2 * chip[0] + chip[1])
            loads[j].wait()
            copies.append(pltpu.make_async_copy(land_ref.at[j], mine, loc.at[1, j]))
            passed.append(pltpu.make_async_remote_copy(src_ref=land_ref.at[j], dst_ref=mine, send_sem=send.at[j],
                                                       recv_sem=recv.at[j], device_id=sibling, device_id_type=MESH))
            copies[-1].start()
            passed[-1].start()
        loads[3].wait()
        copies.append(pltpu.make_async_copy(shard_ref, out_ref.at[:, pl.ds(pl.multiple_of(me * n, LANES), n)], loc.at[1, 3]))
        copies[-1].start()
        for j, chip in enumerate(chips):
            pltpu.make_async_remote_copy(src_ref=land_ref.at[j], dst_ref=window(1 - c, 2 * chip[0] + chip[1]), send_sem=send.at[j],
                                         recv_sem=recv.at[j], device_id=sibling, device_id_type=MESH).wait_recv()
        for cp in copies:
            cp.wait()
        for cp in passed:
            cp.wait_send()

    any_spec = pl.BlockSpec(memory_space=pl.ANY)
    return pl.pallas_call(
        body, in_specs=[any_spec] * 2, out_specs=any_spec,
        out_shape=jax.ShapeDtypeStruct((k_rows, N_CHIPS * n), shard.dtype),
        scratch_shapes=[pltpu.VMEM(shard.shape, shard.dtype), pltpu.VMEM(landed.shape, landed.dtype),
                        pltpu.SemaphoreType.DMA((2, 4)), pltpu.SemaphoreType.DMA((3,)), pltpu.SemaphoreType.DMA((3,))],
        compiler_params=pltpu.CompilerParams(vmem_limit_bytes=VMEM_LIMIT), name=name)(shard, landed)


HBM_SPEC = pl.BlockSpec(memory_space=pltpu.HBM)
SEM_SPEC = pl.BlockSpec(memory_space=pltpu.SEMAPHORE)
DATAFLOW = pltpu.SideEffectType.DATAFLOW_SIDE_EFFECTING
TOKEN_SHAPE = (1, D_MODEL)
N_PEERS = 7
SUM_SPLIT = 4
SUM_SPLIT_ELEMS = 512 * 1024


def _peers():
    x, y, c = lax.axis_index("x"), lax.axis_index("y"), lax.axis_index("c")
    flip = lambda v, f: 1 - v if f else v
    return [(flip(x, k & 4), flip(y, k & 2), flip(c, k & 1)) for k in range(1, N_PEERS + 1)]


def _piece_shape(shape):
    return (shape[-2] // 2, shape[2] if len(shape) == 3 else shape[1] // N_CHIPS)


def _device_piece(ref, chip, core):
    kh, n4 = _piece_shape(ref.shape)
    if len(ref.shape) == 3:
        return ref.at[chip, pl.ds(core * kh, kh), :]
    return ref.at[pl.ds(core * kh, kh), pl.ds(chip * n4, n4)]


def _exchange_copies(partials, lands, send, recv):
    return [pltpu.make_async_remote_copy(
        src_ref=_device_piece(partials[t], 2 * px + py, pc), dst_ref=lands[t].at[k], send_sem=send.at[t * N_PEERS + k],
        recv_sem=recv.at[t * N_PEERS + k], device_id=(px, py, pc), device_id_type=MESH)
        for t in range(len(partials)) for k, (px, py, pc) in enumerate(_peers())]


def _broadcast_copies(srcs, lands, send, recv):
    return [pltpu.make_async_remote_copy(
        src_ref=srcs[t], dst_ref=lands[t].at[k], send_sem=send.at[t * N_PEERS + k], recv_sem=recv.at[t * N_PEERS + k],
        device_id=peer, device_id_type=MESH)
        for t in range(len(srcs)) for k, peer in enumerate(_peers())]


class _LocalCopy:
    def __init__(self, src_ref, dst_ref, sem):
        self.copy = pltpu.make_async_copy(src_ref, dst_ref, sem)

    def start(self):
        self.copy.start()

    def wait_send(self):
        self.copy.wait()

    def wait_recv(self):
        pass


def _gather_copies(shards, lands, send, recv):
    x, y, c, me, chips = _place()
    copies = []
    for t in range(len(shards)):
        n = shards[t].shape[1]
        place = lands[t].at[me] if len(lands[t].shape) == 3 else lands[t].at[:, pl.ds(pl.multiple_of(me * n, LANES), n)]
        copies += [pltpu.make_async_remote_copy(
            src_ref=shards[t], dst_ref=place, send_sem=send.at[t * 4 + j], recv_sem=recv.at[t * 4 + j],
            device_id=(*chip, c), device_id_type=MESH) for j, chip in enumerate(chips)]
        copies.append(_LocalCopy(shards[t], place, send.at[t * 4 + 3]))
    return copies


def _gather_half_copies(shards, lands, send, recv):
    x, y, c, me, chips = _place()
    return [pltpu.make_async_remote_copy(
        src_ref=_rows(shards[t], c * (shards[t].shape[0] // 2), shards[t].shape[0] // 2), dst_ref=lands[t].at[j],
        send_sem=send.at[t * 3 + j], recv_sem=recv.at[t * 3 + j], device_id=(*chip, c), device_id_type=MESH)
        for t in range(len(shards)) for j, chip in enumerate(chips)]


def _split_start(name, copies, per_tensor, srcs, land_shapes):
    nt = len(srcs)
    lands = [lax.empty(s, a.dtype) for s, a in zip(land_shapes, srcs)]
    nsem = nt * per_tensor

    def body(*refs):
        send, recv = refs[2 * nt], refs[2 * nt + 1]
        for cp in copies(refs[:nt], refs[nt:2 * nt], send, recv):
            cp.start()
        refs[-1][...] = jnp.zeros(TOKEN_SHAPE, F32)

    hbm = lambda a: pltpu.with_memory_space_constraint(a, pltpu.HBM)
    outs = pl.pallas_call(
        body, name=name,
        out_shape=[pltpu.SemaphoreType.DMA((nsem,)), pltpu.SemaphoreType.DMA((nsem,))]
        + [pltpu.HBM(s.shape, s.dtype) for s in srcs] + [pltpu.HBM(l.shape, l.dtype) for l in lands]
        + [jax.ShapeDtypeStruct(TOKEN_SHAPE, F32)],
        in_specs=[HBM_SPEC] * (2 * nt), out_specs=[SEM_SPEC, SEM_SPEC] + [HBM_SPEC] * (2 * nt) + [VMEM_SPEC],
        input_output_aliases={i: 2 + i for i in range(2 * nt)},
        compiler_params=pltpu.CompilerParams(has_side_effects=DATAFLOW))(*[hbm(a) for a in list(srcs) + lands])
    return outs[0], outs[1], outs[2:2 + nt], outs[2 + nt:2 + 2 * nt], outs[-1]


def _split_wait(name, copies, send, recv, srcs, lands, after):
    nt = len(srcs)
    after = list(after) if isinstance(after, (list, tuple)) else [after]

    def body(*refs):
        for cp in copies(refs[:nt], refs[nt:2 * nt], refs[2 * nt], refs[2 * nt + 1]):
            cp.wait_send()
            cp.wait_recv()

    outs = pl.pallas_call(
        body, name=name,
        out_shape=[pltpu.HBM(s.shape, s.dtype) for s in srcs] + [pltpu.HBM(l.shape, l.dtype) for l in lands],
        in_specs=[HBM_SPEC] * (2 * nt) + [SEM_SPEC, SEM_SPEC] + [pl.BlockSpec(memory_space=pl.ANY)] * len(after),
        out_specs=[HBM_SPEC] * (2 * nt), input_output_aliases={i: i for i in range(2 * nt)},
        compiler_params=pltpu.CompilerParams(has_side_effects=DATAFLOW))(*srcs, *lands, send, recv, *after)
    return outs[:nt], outs[nt:]


def _device_sum(name, partials, lands):
    nt = len(partials)
    pieces = [_piece_shape(p.shape) for p in partials]
    units = []
    for t, (kh, n4) in enumerate(pieces):
        split = SUM_SPLIT if kh * n4 >= SUM_SPLIT_ELEMS else 1
        units += [(t, j * (kh // split), kh // split) for j in range(split)]
    nu = len(units)

    def body(*refs):
        ins, slots, outs = refs[:nt], refs[nt:2 * nt], refs[2 * nt:3 * nt]
        owns, landed, sums = refs[3 * nt:4 * nt], refs[4 * nt:5 * nt], refs[5 * nt:6 * nt]
        loc, send, recv = refs[6 * nt:]
        x, y, c, me, chips = _place()
        sibling = (x, y, 1 - c)
        loads = []
        for u, (t, r0, rows) in enumerate(units):
            loads.append((
                pltpu.make_async_copy(_rows(_device_piece(ins[t], me, c), r0, rows), _rows(owns[t], r0, rows), loc.at[0, u]),
                pltpu.make_async_copy(_rows(slots[t], r0, rows), _rows(landed[t], r0, rows), loc.at[1, u])))
            for cp in loads[-1]:
                cp.start()
        stores = []
        for u, (t, r0, rows) in enumerate(units):
            for cp in loads[u]:
                cp.wait()

            def add(q0, ck, own=owns[t], slot=landed[t], dst=sums[t], r0=r0):
                at = pl.ds(pl.multiple_of(r0 + q0, ck), ck)
                acc = own[at, :].astype(F32)
                for k in range(N_PEERS):
                    acc = acc + slot[k, at, :].astype(F32)
                dst[at, :] = acc

            _for_row_chunks(rows, add)
            mine = _rows(outs[t], c * pieces[t][0] + r0, rows)
            stores.append((
                pltpu.make_async_copy(_rows(sums[t], r0, rows), mine, loc.at[2, u]),
                pltpu.make_async_remote_copy(src_ref=_rows(sums[t], r0, rows), dst_ref=mine, send_sem=send.at[u],
                                             recv_sem=recv.at[u], device_id=sibling, device_id_type=MESH)))
            for cp in stores[-1]:
                cp.start()
        for u, (t, r0, rows) in enumerate(units):
            pltpu.make_async_remote_copy(
                src_ref=_rows(sums[t], r0, rows), dst_ref=_rows(outs[t], (1 - c) * pieces[t][0] + r0, rows),
                send_sem=send.at[u], recv_sem=recv.at[u], device_id=sibling, device_id_type=MESH).wait_recv()
            stores[u][0].wait()
            stores[u][1].wait_send()

    any_spec = pl.BlockSpec(memory_space=pl.ANY)
    return pl.pallas_call(
        body, in_specs=[any_spec] * (2 * nt), out_specs=[any_spec] * nt,
        out_shape=[jax.ShapeDtypeStruct((2 * kh, n4), F32) for kh, n4 in pieces],
        scratch_shapes=[pltpu.VMEM(p, BF16) for p in pieces] + [pltpu.VMEM((N_PEERS,) + p, BF16) for p in pieces]
        + [pltpu.VMEM(p, F32) for p in pieces]
        + [pltpu.SemaphoreType.DMA((3, nu)), pltpu.SemaphoreType.DMA((nu,)), pltpu.SemaphoreType.DMA((nu,))],
        compiler_params=pltpu.CompilerParams(vmem_limit_bytes=VMEM_LIMIT), name=name)(*partials, *lands)


VEC_SHAPE = (8, D_MODEL + LANES)
VEC_SLOTS = dict(norm1_g=(slice(0, 1), slice(0, D_MODEL)), norm2_g=(slice(1, 2), slice(0, D_MODEL)),
                 final_g=(slice(2, 3), slice(0, D_MODEL)), sgu_ln_g=(slice(3, 4), slice(0, SGU_W)),
                 sgu_ln_b=(slice(3, 4), slice(SGU_W, 2 * SGU_W)), b_spatial=(slice(0, 8), slice(D_MODEL, D_MODEL + LANES)),
                 loss=(slice(4, 5), slice(0, LANES)))
VEC_PARAMS = ("norm1_g", "norm2_g", "final_g", "sgu_ln_g", "sgu_ln_b", "b_spatial")
SMALL_PARAMS = VEC_PARAMS + ("w_spatial",)
W_SPATIAL_2D = (SGU_GROUPS * SGU_CHUNK, SGU_CHUNK)


SMALL_GRADS = VEC_PARAMS + ("loss", "w_spatial")


def _small_shape(name):
    if name == "w_spatial":
        return W_SPATIAL_2D
    rows, cols = VEC_SLOTS[name]
    return (rows.stop - rows.start, cols.stop - cols.start)


def _pack_small(dst, parts):
    dst[...] = jnp.zeros(VEC_SHAPE, F32)
    for n, ref in parts.items():
        if n in VEC_SLOTS:
            dst[VEC_SLOTS[n]] = ref[...]


def _small_start(partials):
    names = VEC_PARAMS + ("loss",)

    def body(*refs):
        _pack_small(refs[-1], dict(zip(names, refs[:-1])))

    vec = pl.pallas_call(
        body, in_specs=[VMEM_SPEC] * len(names), out_specs=VMEM_SPEC, out_shape=jax.ShapeDtypeStruct(VEC_SHAPE, F32),
        name="small_params_pack")(*[partials[n].reshape(_small_shape(n)) for n in names])
    srcs = [vec, partials["w_spatial"].reshape(W_SPATIAL_2D)]
    return _split_start("small_params_start", _broadcast_copies, N_PEERS, srcs, [(N_PEERS,) + s.shape for s in srcs])


def _small_finish(started, after, w, m, v):
    own, landed = _split_wait("small_params_wait", _broadcast_copies, *started, after)
    ng, npar = len(SMALL_GRADS), len(SMALL_PARAMS)

    def update_body(*refs):
        vec_own, ws_own, vec_slots, ws_slots = refs[:4]
        w_in, m_in, v_in = (dict(zip(SMALL_PARAMS, refs[4 + k * npar:4 + (k + 1) * npar])) for k in range(3))
        o0 = 4 + 3 * npar
        g_out = dict(zip(SMALL_GRADS, refs[o0:o0 + ng]))
        d_out, m_out, v_out = (dict(zip(SMALL_PARAMS, refs[o0 + ng + k * npar:o0 + ng + (k + 1) * npar])) for k in range(3))
        vg, vw, vm, vv = refs[o0 + ng + 3 * npar:]
        me = 4 * lax.axis_index("x") + 2 * lax.axis_index("y") + lax.axis_index("c")

        def device_sum(mine, slots, read):
            acc = None
            for i in range(N_PEERS + 1):
                k = me ^ i
                part = jnp.where(k == 0, read(mine), read(slots.at[jnp.maximum(k, 1) - 1]))
                acc = part if acc is None else acc + part
            return acc

        vg[...] = device_sum(vec_own, vec_slots, lambda ref: ref[...])
        _pack_small(vw, w_in)
        _pack_small(vm, m_in)
        _pack_small(vv, v_in)
        d_vec, m_vec, v_vec = _adamw_math(vg[...], vw[...], vm[...], vv[...])
        vw[...] = d_vec
        vm[...] = m_vec
        vv[...] = v_vec
        for n in VEC_PARAMS + ("loss",):
            g_out[n][...] = vg[VEC_SLOTS[n]]
        for n in VEC_PARAMS:
            d_out[n][...] = vw[VEC_SLOTS[n]]
            m_out[n][...] = vm[VEC_SLOTS[n]]
            v_out[n][...] = vv[VEC_SLOTS[n]]

        def spatial(r0, ck):
            rows = pl.ds(r0, ck)
            g = device_sum(ws_own, ws_slots, lambda ref: ref[rows, :])
            d_, m_, v_ = _adamw_math(g, w_in["w_spatial"][rows, :], m_in["w_spatial"][rows, :], v_in["w_spatial"][rows, :])
            g_out["w_spatial"][rows, :] = g
            d_out["w_spatial"][rows, :] = d_
            m_out["w_spatial"][rows, :] = m_
            v_out["w_spatial"][rows, :] = v_

        _for_row_chunks(W_SPATIAL_2D[0], spatial)

    ins = list(own) + list(landed)
    for src in (w, m, v):
        ins += [src[n].reshape(_small_shape(n)) for n in SMALL_PARAMS]
    out_shapes = [jax.ShapeDtypeStruct(_small_shape(n), F32) for n in SMALL_GRADS + SMALL_PARAMS * 3]
    outs = pl.pallas_call(
        update_body, in_specs=[VMEM_SPEC] * len(ins), out_specs=[VMEM_SPEC] * len(out_shapes), out_shape=out_shapes,
        scratch_shapes=[pltpu.VMEM(VEC_SHAPE, F32)] * 4, name="small_params_update")(*ins)
    grads = dict(zip(SMALL_GRADS, outs[:ng]))
    rest = [dict(zip(SMALL_PARAMS, outs[ng + k * npar:ng + (k + 1) * npar])) for k in range(3)]
    return grads, rest[0], rest[1], rest[2]


BIG = ("w_in", "w_proj_attn", "w_proj_sgu", "w_out", "w_ffn_gate", "w_ffn_up", "w_ffn_down")
COMM_GROUPS = (("w_in",), ("w_proj_attn", "w_proj_sgu", "w_out", "w_ffn_gate", "w_ffn_up", "w_ffn_down"))
WEIGHTS = ("norm1_g", "w_in", "sgu_ln_g", "sgu_ln_b", "w_spatial", "b_spatial", "w_proj_attn", "w_proj_sgu", "w_out",
           "norm2_g", "w_ffn_gate", "w_ffn_up", "w_ffn_down", "final_g")


def kernel(x, positions, norm1_g, w_in, sgu_ln_g, sgu_ln_b, w_spatial, b_spatial, w_proj_attn, w_proj_sgu, w_out, norm2_g, w_ffn_gate, w_ffn_up, w_ffn_down, final_g, loss_target, m_norm1_g, m_w_in, m_sgu_ln_g, m_sgu_ln_b, m_w_spatial, m_b_spatial, m_w_proj_attn, m_w_proj_sgu, m_w_out, m_norm2_g, m_w_ffn_gate, m_w_ffn_up, m_w_ffn_down, m_final_g, v_norm1_g, v_w_in, v_sgu_ln_g, v_sgu_ln_b, v_w_spatial, v_b_spatial, v_w_proj_attn, v_w_proj_sgu, v_w_out, v_norm2_g, v_w_ffn_gate, v_w_ffn_up, v_w_ffn_down, v_final_g):
    w = dict(norm1_g=norm1_g, w_in=w_in, sgu_ln_g=sgu_ln_g, sgu_ln_b=sgu_ln_b, w_spatial=w_spatial, b_spatial=b_spatial,
             w_proj_attn=w_proj_attn, w_proj_sgu=w_proj_sgu, w_out=w_out, norm2_g=norm2_g, w_ffn_gate=w_ffn_gate,
             w_ffn_up=w_ffn_up, w_ffn_down=w_ffn_down, final_g=final_g)
    m = dict(norm1_g=m_norm1_g, w_in=m_w_in, sgu_ln_g=m_sgu_ln_g, sgu_ln_b=m_sgu_ln_b, w_spatial=m_w_spatial,
             b_spatial=m_b_spatial, w_proj_attn=m_w_proj_attn, w_proj_sgu=m_w_proj_sgu, w_out=m_w_out, norm2_g=m_norm2_g,
             w_ffn_gate=m_w_ffn_gate, w_ffn_up=m_w_ffn_up, w_ffn_down=m_w_ffn_down, final_g=m_final_g)
    v = dict(norm1_g=v_norm1_g, w_in=v_w_in, sgu_ln_g=v_sgu_ln_g, sgu_ln_b=v_sgu_ln_b, w_spatial=v_w_spatial,
             b_spatial=v_b_spatial, w_proj_attn=v_w_proj_attn, w_proj_sgu=v_w_proj_sgu, w_out=v_w_out, norm2_g=v_norm2_g,
             w_ffn_gate=v_w_ffn_gate, w_ffn_up=v_w_ffn_up, w_ffn_down=v_w_ffn_down, final_g=v_final_g)
    t = x.shape[1]

    def cast(n, after):
        flip = jnp.transpose if w[n].shape[-1] % LANES else (lambda a: a)
        return flip(_ew(f"cast_{n}", lambda a: (a,), [flip(w[n][0])], [BF16], after)[0])

    shards = {"w_in": cast("w_in", [])}
    late = COMM_GROUPS[1]
    k_in, n_in = shards["w_in"].shape
    *first, token = _split_start("gather_start_0", _gather_half_copies, 3, [shards["w_in"]], [(3, k_in // 2, n_in)])
    shards.update({n: cast(n, [token]) for n in late})
    pending = {}

    def first_weight(after):
        srcs, filled = _split_wait("gather_wait_0", _gather_half_copies, *first, list(after) + [shards[n] for n in late])
        gath_in, late_shards = lax.optimization_barrier(
            (_gather_finish("gather_finish_0", srcs[0], filled[0]), [shards[n] for n in late]))
        land_shapes = [(s.shape[0], N_CHIPS * s.shape[1]) if n.startswith("w_proj") else (N_CHIPS,) + s.shape
                       for n, s in zip(late, late_shards)]
        *pending["late"], _ = _split_start("gather_start_1", _gather_copies, 4, late_shards, land_shapes)
        return gath_in

    def late_weights(after):
        _, filled = _split_wait("gather_wait_1", _gather_copies, *pending["late"], after)
        gath = dict(zip(late, filled))
        return (gath["w_proj_attn"], gath["w_proj_sgu"],
                gath["w_out"].reshape(D_MODEL, D_MODEL), gath["w_ffn_gate"], gath["w_ffn_up"], gath["w_ffn_down"])

    exchanges = {}

    def on_grads(i, partials):
        if "w_out" in partials:
            partials["w_out"] = partials["w_out"].reshape(N_CHIPS, D_MODEL // N_CHIPS, D_MODEL)
        parts = [partials[n] for n in COMM_GROUPS[i]]
        *exchanges[i], started = _split_start(
            f"rs_exchange_start_{i}", _exchange_copies, N_PEERS, parts, [(N_PEERS,) + _piece_shape(p.shape) for p in parts])
        return started

    dx, _, small = _local_step(
        x[0], positions.reshape(t, 1), loss_target[0], norm1_g + token, sgu_ln_g, sgu_ln_b,
        w_spatial[0], b_spatial[0] + token[:1, :LANES],
        norm2_g, final_g.reshape(1, D_MODEL), first_weight, late_weights, on_grads=on_grads)
    *small_started, small_token = _small_start(small)

    grads = {}
    for i in (1, 0):
        parts, filled = _split_wait(f"rs_exchange_wait_{i}", _exchange_copies, *exchanges[i], small_token)
        grads.update(zip(COMM_GROUPS[i], _device_sum(f"rs_device_sum_{i}", parts, filled)))

    delta, new_m, new_v, updated = {}, {}, {}, []
    for n in BIG:
        shp = w[n].shape
        flip = jnp.transpose if shp[-1] % LANES else (lambda a: a)
        outs = _adamw(f"adamw_{n}", flip(grads[n]), flip(w[n][0]), flip(m[n][0]), flip(v[n][0]))
        grads[n], delta[n], new_m[n], new_v[n] = (flip(a).reshape(shp) for a in outs)
        updated.append(outs[-1])

    g_s, d_s, m_s, v_s = _small_finish(small_started, updated, w, m, v)
    loss = g_s["loss"][0, 0]
    for n in SMALL_PARAMS:
        shp = w[n].shape
        grads[n], delta[n], new_m[n], new_v[n] = (a[n].reshape(shp) for a in (g_s, d_s, m_s, v_s))

    return (loss, dx.reshape(x.shape), *[grads[n] for n in WEIGHTS], *[delta[n] for n in WEIGHTS],
            *[new_m[n] for n in WEIGHTS], *[new_v[n] for n in WEIGHTS])
```

```python
import functools

import numpy as np
import jax
import jax.numpy as jnp
from jax import lax
from jax.experimental import pallas as pl
from jax.experimental.pallas import tpu as pltpu

F32, BF16 = jnp.float32, jnp.bfloat16
MESH = pl.DeviceIdType.MESH

D_MODEL = 1024
HEAD_DIM = 64
ATTN_W = 512
DILATIONS = (1, 4, 16)
BLK = 128
ATTN_BLOCKS_PER_STEP = 4
ROPE_DIM = 16
ROPE_THETA = 500000.0
SGU_W = 512
SGU_CHUNK = 128
SGU_GROUPS = 8
D_FF = 2816
N_CHIPS = 4
FF_SHARD = D_FF // N_CHIPS
IN_COLS = 7680
EPS = 1e-6
NEG = -1e30
LANES = 128
VMEM_LIMIT = 52 * 1024 * 1024

ADAM_LR, ADAM_B1, ADAM_B2, ADAM_EPS, ADAM_WD, ADAM_STEP = 0.001, 0.9, 0.999, 1e-08, 0.01, 10

QKV_BLOCKS = 9


def _w_in_block(part, g):
    return part * len(DILATIONS) + g


def _cparams(ngrid):
    return pltpu.CompilerParams(dimension_semantics=("arbitrary",) * ngrid, vmem_limit_bytes=VMEM_LIMIT)


def _full(shape):
    return pl.BlockSpec(shape, lambda *_: (0,) * len(shape))


def _resident(shape):
    return pl.BlockSpec(shape, lambda *_: (0,) * len(shape), pipeline_mode=pl.Buffered(1))


NT = ((1,), (1,))
TN = ((0,), (0,))


def _rope(v, cos_t, sin_t):
    half = ROPE_DIM // 2
    first = (lax.broadcasted_iota(jnp.int32, cos_t.shape, 1) % HEAD_DIM) < half
    outs = []
    for cs in range(v.shape[1] // LANES):
        x = v[:, cs * LANES:(cs + 1) * LANES]
        partner = jnp.where(first, pltpu.roll(x, LANES - half, axis=1), pltpu.roll(x, half, axis=1))
        outs.append(x * cos_t + partner * sin_t)
    return outs[0] if len(outs) == 1 else jnp.concatenate(outs, axis=1)


def _spread_heads(v2, upper):
    other = pltpu.roll(v2, HEAD_DIM, axis=1)
    h0 = jnp.where(upper, other, v2)
    h1 = jnp.where(upper, v2, other)
    return jnp.concatenate([jnp.concatenate([h0, h0], axis=1), jnp.concatenate([h1, h1], axis=1)], axis=0)


def _sigmoid(v):
    return 0.5 * jnp.tanh(0.5 * v) + 0.5


def _rms_stats(v):
    r = lax.rsqrt(jnp.mean(v * v, axis=-1, keepdims=True) + EPS)
    return v * r, r


def _rms_bwd(dy, xhat, r, g):
    dxh = dy * g
    return r * (dxh - xhat * jnp.mean(dxh * xhat, axis=-1, keepdims=True))


def _head_sum_matrix():
    idx = np.arange(ATTN_W) // HEAD_DIM
    return jnp.asarray((idx[:, None] == idx[None, :]).astype(np.float32), dtype=BF16)


def _group_sum(v, e):
    hi = v.astype(BF16)
    lo = (v - hi.astype(F32)).astype(BF16)
    return jnp.dot(hi, e, preferred_element_type=F32) + jnp.dot(lo, e, preferred_element_type=F32)


TILE = 512


def _to_slabs(slab_ref, v):
    for cs in range(slab_ref.shape[0]):
        slab_ref[cs] = v[:, cs * LANES:(cs + 1) * LANES]


def _from_slabs(slab_ref):
    return jnp.concatenate([slab_ref[cs] for cs in range(slab_ref.shape[0])], axis=1)


def _class_rows(slab_ref, r, dil):
    n = slab_ref.shape[1] // dil
    return jnp.concatenate([slab_ref.at[cs][pl.ds(r, n, stride=dil), :] for cs in range(slab_ref.shape[0])], axis=1)


def _put_class_rows(slab_ref, r, dil, v):
    n = slab_ref.shape[1] // dil
    for cs in range(slab_ref.shape[0]):
        slab_ref.at[cs][pl.ds(r, n, stride=dil), :] = v[:, cs * LANES:(cs + 1) * LANES]


def _natural_from_group(slab_ref, grp_ref):
    dil = grp_ref.shape[0]
    for r in range(dil):
        _put_class_rows(slab_ref, r, dil, grp_ref[r].astype(F32))
    return _from_slabs(slab_ref)


def _group_from_natural(slab_ref, grp_ref, v):
    dil = grp_ref.shape[0]
    _to_slabs(slab_ref, v)
    for r in range(dil):
        grp_ref[r] = _class_rows(slab_ref, r, dil).astype(grp_ref.dtype)


def _group_spec(dil, tile, width):
    return pl.BlockSpec((dil, tile // dil, width), lambda i, *_: (0, i, 0))


def _slabs(tile, width):
    return pltpu.VMEM((width // LANES, tile, LANES), F32)


def _rope_consts():
    lane = np.arange(LANES) % HEAD_DIM
    fi = lane % (ROPE_DIM // 2)
    invf = np.where(lane < ROPE_DIM, ROPE_THETA ** (-(2.0 * fi) / ROPE_DIM), 0.0)
    sgn = np.where(lane < ROPE_DIM // 2, -1.0, np.where(lane < ROPE_DIM, 1.0, 0.0))
    return (jnp.asarray(invf.astype(np.float32)).reshape(1, LANES), jnp.asarray(sgn.astype(np.float32)).reshape(1, LANES))


def _rope_tables(pos_col):
    t = pos_col.shape[0]
    tile = min(t, TILE)
    invf, sgn = _rope_consts()

    def body(p_ref, f_ref, s_ref, c0, s0, c1, s1, c2, s2, slab_c, slab_s):
        ang = p_ref[...].astype(F32) * f_ref[...]
        cos, sin = jnp.cos(ang), jnp.sin(ang) * s_ref[...]
        c0[...] = cos
        s0[...] = sin
        _group_from_natural(slab_c, c1, cos)
        _group_from_natural(slab_s, s1, sin)
        for r in range(DILATIONS[2]):
            c2[r] = _class_rows(slab_c, r, DILATIONS[2])
            s2[r] = _class_rows(slab_s, r, DILATIONS[2])

    nat = pl.BlockSpec((tile, LANES), lambda i: (i, 0))
    specs, shapes = [nat, nat], [(t, LANES)] * 2
    for d in DILATIONS[1:]:
        specs += [_group_spec(d, tile, LANES)] * 2
        shapes += [(d, t // d, LANES)] * 2
    outs = pl.pallas_call(
        body, grid=(t // tile,),
        in_specs=[pl.BlockSpec((tile, 1), lambda i: (i, 0)), _full((1, LANES)), _full((1, LANES))],
        out_specs=specs, out_shape=[jax.ShapeDtypeStruct(s, F32) for s in shapes],
        scratch_shapes=[_slabs(tile, LANES)] * 2,
        compiler_params=_cparams(1), name="rope_tables")(pos_col, invf, sgn)
    return [(outs[2 * g].reshape(t, LANES), outs[2 * g + 1].reshape(t, LANES)) for g in range(len(DILATIONS))]


def _norm_fwd(x, g):
    t = x.shape[0]
    tile = min(t, TILE)

    def body(x_ref, g_ref, h0_ref, h1_ref, h2_ref, slab):
        xhat, _ = _rms_stats(x_ref[...])
        hn = xhat * g_ref[...]
        h0_ref[...] = hn.astype(BF16)
        _group_from_natural(slab, h1_ref, hn)
        for r in range(DILATIONS[2]):
            h2_ref[r] = _class_rows(slab, r, DILATIONS[2]).astype(BF16)

    nat = pl.BlockSpec((tile, D_MODEL), lambda i: (i, 0))
    return pl.pallas_call(
        body, grid=(t // tile,),
        in_specs=[nat, _full((1, D_MODEL))],
        out_specs=[nat] + [_group_spec(d, tile, D_MODEL) for d in DILATIONS[1:]],
        out_shape=[jax.ShapeDtypeStruct((t, D_MODEL), BF16)]
        + [jax.ShapeDtypeStruct((d, t // d, D_MODEL), BF16) for d in DILATIONS[1:]],
        scratch_shapes=[_slabs(tile, D_MODEL)],
        compiler_params=_cparams(1), name="norm1_fwd")(x, g)


GU_COLS = 3072
GROUP_COLS = 1536
GU_HALF = GU_COLS // 2


def _w_in_spec(width, block):
    return pl.BlockSpec((D_MODEL, width), lambda i: (0, block), pipeline_mode=pl.Buffered(1))


def _gu_w_specs():
    first = QKV_BLOCKS * ATTN_W // GU_HALF
    return [_w_in_spec(GU_HALF, first), _w_in_spec(GU_HALF, first + 1)]


def _group_w_specs(g):
    return [_w_in_spec(ATTN_W, _w_in_block(part, g)) for part in range(3)]


def _in_proj(hs, w_in, tables):
    t = hs[0].shape[0]
    tm = min(t, 1024)

    def body_gu(h_ref, w0_ref, w1_ref, o_ref):
        h = h_ref[...]
        o_ref[:, 0:GU_HALF] = jnp.dot(h, w0_ref[...], preferred_element_type=F32).astype(BF16)
        o_ref[:, GU_HALF:] = jnp.dot(h, w1_ref[...], preferred_element_type=F32).astype(BF16)

    gu = _token_call("in_proj_gates_uv", body_gu, t, tm,
                     [(hs[0], _rows_spec(tm, D_MODEL))] + [(w_in, s) for s in _gu_w_specs()],
                     [((t, GU_COLS), BF16, _rows_spec(tm, GU_COLS))])[0]

    qkvs = []
    for g in range(len(DILATIONS)):

        def body_qkv(h_ref, wq_ref, wk_ref, wv_ref, cos_ref, sin_ref, o_ref):
            h = h_ref[...]
            cos_w, sin_w = cos_ref[...], sin_ref[...]
            q = jnp.dot(h, wq_ref[...], preferred_element_type=F32)
            o_ref[:, 0:ATTN_W] = (_rope(q, cos_w, sin_w) * HEAD_DIM ** -0.5).astype(BF16)
            k = jnp.dot(h, wk_ref[...], preferred_element_type=F32)
            o_ref[:, ATTN_W:2 * ATTN_W] = _rope(k, cos_w, sin_w).astype(BF16)
            o_ref[:, 2 * ATTN_W:] = jnp.dot(h, wv_ref[...], preferred_element_type=F32).astype(BF16)

        cos_t, sin_t = tables[g]
        qkvs.append(_token_call(
            f"in_proj_qkv_g{g}", body_qkv, t, tm,
            [(hs[g].reshape(t, D_MODEL), _rows_spec(tm, D_MODEL))] + [(w_in, s) for s in _group_w_specs(g)]
            + [(cos_t, _rows_spec(tm, LANES)), (sin_t, _rows_spec(tm, LANES))],
            [((t, GROUP_COLS), BF16, _rows_spec(tm, GROUP_COLS))])[0])
    return gu, qkvs


def _attn_masks(n):
    row = lax.broadcasted_iota(jnp.int32, (2 * BLK, 2 * BLK), 0) % BLK
    col = lax.broadcasted_iota(jnp.int32, (2 * BLK, 2 * BLK), 1)
    diff = BLK + row - col
    valid = (diff >= 0) & (diff <= BLK) & ((col >= BLK) | (n > 0))
    upper = lax.broadcasted_iota(jnp.int32, (BLK, LANES), 1) >= HEAD_DIM
    return valid, upper


def _stack_heads(v2, upper):
    zero = jnp.zeros_like(v2)
    return jnp.concatenate([jnp.where(upper, zero, v2), jnp.where(upper, v2, zero)], axis=0)


def _unstack_heads(v, upper):
    return jnp.where(upper, v[BLK:], v[:BLK])


def _attn_fwd(qkv, g, dil):
    t = qkv.shape[0]
    length = t // dil
    nb = length // BLK
    per_step = min(nb, ATTN_BLOCKS_PER_STEP)
    view = qkv.reshape(dil, length, GROUP_COLS)

    def body(q_ref, kc_ref, kp_ref, vc_ref, vp_ref, o_ref, l_ref, kwin, vwin):
        n = pl.program_id(1)
        kwin[0:BLK] = kp_ref[...]
        kwin[BLK:] = kc_ref[...]
        vwin[0:BLK] = vp_ref[...]
        vwin[BLK:] = vc_ref[...]

        def block(b, carry):
            valid, upper = _attn_masks(n * per_step + b)
            rows = pl.ds(pl.multiple_of(b * BLK, BLK), BLK)
            window = pl.ds(pl.multiple_of(b * BLK, BLK), 2 * BLK)
            slabs = [slice(p * LANES, (p + 1) * LANES) for p in range(ATTN_W // LANES)]
            ss = [lax.dot_general(_stack_heads(q_ref[rows, sl], upper), kwin[window, sl], (NT, ((), ())),
                                  preferred_element_type=F32) for sl in slabs]
            soft = []
            for s in ss:
                s = jnp.where(valid, s, NEG)
                m = jnp.max(s, axis=1, keepdims=True)
                pe = jnp.exp(s - m)
                soft.append((m, pe, jnp.sum(pe, axis=1, keepdims=True)))
            for sl, (m, pe, den) in zip(slabs, soft):
                o = jnp.dot(pe.astype(BF16), vwin[window, sl], preferred_element_type=F32) / den
                lse = jnp.broadcast_to(m + jnp.log(den), (2 * BLK, LANES))
                o_ref[rows, sl] = _unstack_heads(o, upper).astype(BF16)
                l_ref[rows, sl] = _unstack_heads(lse, upper)
            return carry

        lax.fori_loop(0, per_step, block, 0)

    rows = per_step * BLK
    cur = lambda part: pl.BlockSpec((None, rows, ATTN_W), lambda r, n: (r, n, part))
    prev = lambda part: pl.BlockSpec((None, BLK, ATTN_W), lambda r, n: (r, jnp.maximum(n * per_step - 1, 0), part))
    out_spec = pl.BlockSpec((None, rows, ATTN_W), lambda r, n: (r, n, 0))
    return pl.pallas_call(
        body, grid=(dil, nb // per_step),
        in_specs=[cur(0), cur(1), prev(1), cur(2), prev(2)],
        out_specs=[out_spec, out_spec],
        out_shape=[jax.ShapeDtypeStruct((dil, length, ATTN_W), BF16), jax.ShapeDtypeStruct((dil, length, ATTN_W), F32)],
        scratch_shapes=[pltpu.VMEM((rows + BLK, ATTN_W), BF16)] * 2,
        compiler_params=_cparams(2), name=f"attn_fwd_g{g}")(view, view, view, view, view)


def _alphas(l0, l1, l2):
    m = jnp.maximum(jnp.maximum(l0, l1), l2)
    e0, e1, e2 = jnp.exp(l0 - m), jnp.exp(l1 - m), jnp.exp(l2 - m)
    inv = 1.0 / (e0 + e1 + e2)
    return e0 * inv, e1 * inv, e2 * inv


def _natural_group_values(o_refs, l_refs, slabs):
    os_ = [o_refs[0][0].astype(F32)] + [_natural_from_group(slabs[2 * g - 2], o_refs[g]) for g in (1, 2)]
    ls_ = [l_refs[0][0]] + [_natural_from_group(slabs[2 * g - 1], l_refs[g]) for g in (1, 2)]
    return os_, ls_


def _combine_fwd(os_, ls_):
    t = os_[0].shape[1]
    tile = min(t, TILE)

    def body(o0, o1, o2, l0, l1, l2, a_ref, *slabs):
        ov, lv = _natural_group_values((o0, o1, o2), (l0, l1, l2), slabs)
        a0, a1, a2 = _alphas(*lv)
        a_ref[...] = (a0 * ov[0] + a1 * ov[1] + a2 * ov[2]).astype(BF16)

    specs = [_group_spec(d, tile, ATTN_W) for d in DILATIONS]
    return pl.pallas_call(
        body, grid=(t // tile,), in_specs=specs * 2, out_specs=pl.BlockSpec((tile, ATTN_W), lambda i: (i, 0)),
        out_shape=jax.ShapeDtypeStruct((t, ATTN_W), BF16),
        scratch_shapes=[_slabs(tile, ATTN_W)] * 4,
        compiler_params=_cparams(1), name="combine_fwd")(*os_, *ls_)


def _combine_bwd(dattn, os_, ls_):
    t = dattn.shape[0]
    tile = min(t, TILE)
    e = _head_sum_matrix()

    def body(d_ref, o0, o1, o2, l0, l1, l2, e_ref, do0, do1, do2, c0, c1, c2, *slabs):
        ov, lv = _natural_group_values((o0, o1, o2), (l0, l1, l2), slabs)
        alphas = _alphas(*lv)
        d = d_ref[...]
        attn = alphas[0] * ov[0] + alphas[1] * ov[1] + alphas[2] * ov[2]
        s = _group_sum(d * attn, e_ref[...])
        do0[0] = (alphas[0] * d).astype(BF16)
        c0[0] = -alphas[0] * s
        for g, do_ref, c_ref in ((1, do1, c1), (2, do2, c2)):
            _group_from_natural(slabs[2 * g - 2], do_ref, alphas[g] * d)
            _group_from_natural(slabs[2 * g - 1], c_ref, -alphas[g] * s)

    specs = [_group_spec(d, tile, ATTN_W) for d in DILATIONS]
    shapes = [(d, t // d, ATTN_W) for d in DILATIONS]
    outs = pl.pallas_call(
        body, grid=(t // tile,),
        in_specs=[pl.BlockSpec((tile, ATTN_W), lambda i: (i, 0))] + specs * 2 + [_full((ATTN_W, ATTN_W))],
        out_specs=specs * 2,
        out_shape=[jax.ShapeDtypeStruct(s, BF16) for s in shapes] + [jax.ShapeDtypeStruct(s, F32) for s in shapes],
        scratch_shapes=[_slabs(tile, ATTN_W)] * 4,
        compiler_params=_cparams(1), name="combine_bwd")(dattn, *os_, *ls_, e)
    return outs[:3], outs[3:]


def _attn_bwd(qkv, do, cc, lse, cos_t, sin_t, g, dil):
    t = qkv.shape[0]
    length = t // dil
    nb = length // BLK
    per_step = min(nb, ATTN_BLOCKS_PER_STEP)
    nsteps = nb // per_step
    rows_per_step = per_step * BLK
    qkv_v = qkv.reshape(dil, length, GROUP_COLS)
    cos_v, sin_v = (a.reshape(dil, length, LANES) for a in (cos_t, sin_t))
    scale = HEAD_DIM ** -0.5
    dq_cols, dk_cols, dv_cols = (slice(i * ATTN_W, (i + 1) * ATTN_W) for i in range(3))

    def body(q_ref, kc_ref, kp_ref, vc_ref, vp_ref, do_ref, c_ref, l_ref, cosc, sinc, cosp, sinp,
             out_ref, acc, kwin, vwin, cwin, swin):
        n = pl.program_id(1)

        def one_block(b):
            valid, upper = _attn_masks(n * per_step + b)
            start = b * BLK if isinstance(b, int) else pl.multiple_of(b * BLK, BLK)
            rows, before, window = pl.ds(start, BLK), pl.ds(start, BLK), pl.ds(start, 2 * BLK)
            own = pl.ds(start + BLK, BLK)
            dq_parts, dkp_parts, dkc_parts, dvp_parts, dvc_parts = [], [], [], [], []
            npairs = ATTN_W // LANES
            slabs = [slice(p * LANES, (p + 1) * LANES) for p in range(npairs)]
            qss = [_stack_heads(q_ref[rows, sl], upper) for sl in slabs]
            doss = [_stack_heads(do_ref[rows, sl], upper) for sl in slabs]
            ss = [lax.dot_general(qss[p], kwin[window, slabs[p]], (NT, ((), ())), preferred_element_type=F32) for p in range(npairs)]
            dpvs = [lax.dot_general(doss[p], vwin[window, slabs[p]], (NT, ((), ())), preferred_element_type=F32)
                    for p in range(npairs)]
            pes = [jnp.exp(jnp.where(valid, ss[p], NEG) - _spread_heads(l_ref[rows, slabs[p]], upper)) for p in range(npairs)]
            dss = [(pes[p] * (dpvs[p] + _spread_heads(c_ref[rows, slabs[p]], upper))).astype(BF16) for p in range(npairs)]
            for p in range(npairs):
                qs, dos, ds = qss[p], doss[p], dss[p]
                dq2 = _unstack_heads(jnp.dot(ds, kwin[window, slabs[p]], preferred_element_type=F32), upper)
                dk2 = lax.dot_general(ds, qs, (TN, ((), ())), preferred_element_type=F32)
                dv2 = lax.dot_general(pes[p].astype(BF16), dos, (TN, ((), ())), preferred_element_type=F32)
                dq_parts.append(dq2)
                dkp_parts.append(dk2[:BLK])
                dkc_parts.append(dk2[BLK:])
                dvp_parts.append(dv2[:BLK])
                dvc_parts.append(dv2[BLK:])
            dq = _rope(jnp.concatenate(dq_parts, axis=1) * scale, cwin[own, :], swin[own, :])
            dkc = _rope(jnp.concatenate(dkc_parts, axis=1), cwin[own, :], swin[own, :])
            dkp = _rope(jnp.concatenate(dkp_parts, axis=1), cwin[before, :], swin[before, :])
            return dq, dkp, dkc, jnp.concatenate(dvp_parts, axis=1), jnp.concatenate(dvc_parts, axis=1)

        @pl.when(n < nsteps)
        def _():
            kwin[0:BLK] = kp_ref[...]
            kwin[BLK:] = kc_ref[...]
            vwin[0:BLK] = vp_ref[...]
            vwin[BLK:] = vc_ref[...]
            cwin[0:BLK] = cosp[...]
            cwin[BLK:] = cosc[...]
            swin[0:BLK] = -sinp[...]
            swin[BLK:] = -sinc[...]
            dq, dkp, dkc, dvp, dvc = one_block(0)
            last = slice(rows_per_step - BLK, rows_per_step)

            @pl.when(n > 0)
            def _():
                if per_step > 1:
                    out_ref[0:rows_per_step - BLK, :] = acc[0:rows_per_step - BLK, :].astype(BF16)
                out_ref[last, dq_cols] = acc[last, dq_cols].astype(BF16)
                out_ref[last, dk_cols] = (acc[last, dk_cols] + dkp).astype(BF16)
                out_ref[last, dv_cols] = (acc[last, dv_cols] + dvp).astype(BF16)

            acc[0:BLK, dq_cols] = dq
            acc[0:BLK, dk_cols] = dkc
            acc[0:BLK, dv_cols] = dvc

            def later(b, carry):
                dq, dkp, dkc, dvp, dvc = one_block(b)
                start = pl.multiple_of(b * BLK, BLK)
                before, rows = pl.ds(start - BLK, BLK), pl.ds(start, BLK)
                acc[before, dk_cols] += dkp
                acc[before, dv_cols] += dvp
                acc[rows, dq_cols] = dq
                acc[rows, dk_cols] = dkc
                acc[rows, dv_cols] = dvc
                return carry

            lax.fori_loop(1, per_step, later, 0)

        @pl.when(n == flush_at)
        def _():
            out_ref[...] = acc[...].astype(BF16)

    flush_at = nsteps - 1 if nsteps == 1 else nsteps
    out_lag = 0 if nsteps == 1 else 1
    nc = lambda n: jnp.minimum(n, nsteps - 1)
    npv = lambda n: jnp.maximum(jnp.minimum(n, nsteps - 1) * per_step - 1, 0)
    cur = lambda part: pl.BlockSpec((None, rows_per_step, ATTN_W), lambda r, n: (r, nc(n), part))
    prev = lambda part: pl.BlockSpec((None, BLK, ATTN_W), lambda r, n: (r, npv(n), part))
    row = pl.BlockSpec((None, rows_per_step, ATTN_W), lambda r, n: (r, nc(n), 0))
    tab_c = pl.BlockSpec((None, rows_per_step, LANES), lambda r, n: (r, nc(n), 0))
    tab_p = pl.BlockSpec((None, BLK, LANES), lambda r, n: (r, npv(n), 0))
    out_spec = pl.BlockSpec((None, rows_per_step, GROUP_COLS), lambda r, n: (r, jnp.maximum(n - out_lag, 0), 0))
    out = pl.pallas_call(
        body, grid=(dil, nsteps + out_lag),
        in_specs=[cur(0), cur(1), prev(1), cur(2), prev(2), row, row, row, tab_c, tab_c, tab_p, tab_p],
        out_specs=out_spec,
        out_shape=jax.ShapeDtypeStruct((dil, length, GROUP_COLS), BF16),
        scratch_shapes=[pltpu.VMEM((rows_per_step, GROUP_COLS), F32)]
        + [pltpu.VMEM((rows_per_step + BLK, ATTN_W), BF16)] * 2 + [pltpu.VMEM((rows_per_step + BLK, LANES), F32)] * 2,
        compiler_params=_cparams(2), name=f"attn_bwd_g{g}")(
            qkv_v, qkv_v, qkv_v, qkv_v, qkv_v, do, cc, lse, cos_v, sin_v, cos_v, sin_v)
    return out.reshape(t, GROUP_COLS)


SQRT_HALF = 0.7071067811865476
INV_SQRT_2PI = 0.3989422804014327


def _sgu_core(uv, g, b, w_ref, bias):
    cdf = 0.5 * (1.0 + lax.erf(uv * SQRT_HALF))
    z = uv * cdf
    u, v = z[:, :SGU_W], z[:, SGU_W:]
    mu = jnp.mean(v, axis=1, keepdims=True)
    xc = v - mu
    rs = lax.rsqrt(jnp.mean(xc * xc, axis=1, keepdims=True) + EPS)
    xhat = xc * rs
    vn = xhat * g + b
    row = lax.broadcasted_iota(jnp.int32, (SGU_CHUNK, SGU_CHUNK), 0)
    col = lax.broadcasted_iota(jnp.int32, (SGU_CHUNK, SGU_CHUNK), 1)
    tril = row >= col
    upper = lax.broadcasted_iota(jnp.int32, (SGU_CHUNK, LANES), 1) >= SGU_W // SGU_GROUPS
    ws, vlo, vhi, mixed = [], [], [], []
    for pr in range(SGU_W // LANES):
        sl = slice(pr * LANES, (pr + 1) * LANES)
        w0 = jnp.where(tril, w_ref[2 * pr], 0.0).astype(BF16)
        w1 = jnp.where(tril, w_ref[2 * pr + 1], 0.0).astype(BF16)
        vn2 = vn[:, sl]
        lo = jnp.where(upper, 0.0, vn2).astype(BF16)
        hi = jnp.where(upper, vn2, 0.0).astype(BF16)
        mixed.append(jnp.dot(w0, lo, preferred_element_type=F32) + jnp.dot(w1, hi, preferred_element_type=F32)
                     + bias[:, sl])
        ws.append((w0, w1))
        vlo.append(lo)
        vhi.append(hi)
    return cdf, u, xhat, rs, jnp.concatenate(mixed, axis=1), ws, vlo, vhi, tril, upper


SGU_STEP = 4 * SGU_CHUNK


def _for_chunks(step_rows, fn):
    def one(ci, carry):
        fn(pl.ds(pl.multiple_of(ci * SGU_CHUNK, SGU_CHUNK), SGU_CHUNK))
        return carry

    lax.fori_loop(0, step_rows // SGU_CHUNK, one, 0)


def _sgu_fwd(gu, ln_g, ln_b, w_s, bias_exp):
    t = gu.shape[0]
    step = min(t, SGU_STEP)

    def body(uv_ref, g_ref, b_ref, w_ref, bias_ref, o_ref):
        def chunk(rows):
            _, u, _, _, mixed, *_ = _sgu_core(uv_ref[rows, :].astype(F32), g_ref[...], b_ref[...], w_ref, bias_ref[...])
            o_ref[rows, :] = (u * mixed).astype(BF16)

        _for_chunks(step, chunk)

    return pl.pallas_call(
        body, grid=(t // step,),
        in_specs=[pl.BlockSpec((step, 2 * SGU_W), lambda n: (n, 0)), _full((1, SGU_W)), _full((1, SGU_W)),
                  _full((SGU_GROUPS, SGU_CHUNK, SGU_CHUNK)), _full((SGU_CHUNK, SGU_W))],
        out_specs=pl.BlockSpec((step, SGU_W), lambda n: (n, 0)),
        out_shape=jax.ShapeDtypeStruct((t, SGU_W), BF16),
        compiler_params=_cparams(1), name="sgu_fwd")(gu, ln_g, ln_b, w_s, bias_exp)


def _sgu_bwd(dproj, gu, dsgu, ln_g, ln_b, w_s, bias_exp):
    t = gu.shape[0]
    step = min(t, SGU_STEP)
    nsteps = t // step
    e = _head_sum_matrix()

    def body(dp_in, uv_ref, ds_ref, g_ref, b_ref, w_ref, bias_ref, e_ref, out_ref, dw_ref, dbias_ref, dg_ref, db_ref):
        n = pl.program_id(0)

        @pl.when(n == 0)
        def _():
            dw_ref[...] = jnp.zeros(dw_ref.shape, F32)
            dbias_ref[...] = jnp.zeros(dbias_ref.shape, F32)
            dg_ref[...] = jnp.zeros(dg_ref.shape, F32)
            db_ref[...] = jnp.zeros(db_ref.shape, F32)

        _for_chunks(step, functools.partial(chunk, uv_ref, ds_ref, g_ref, b_ref, w_ref, bias_ref, out_ref, dw_ref, dbias_ref,
                                            dg_ref, db_ref))

        @pl.when(n == nsteps - 1)
        def _():
            dbias_ref[...] = _group_sum(dbias_ref[...], e_ref[...])

    def chunk(uv_ref, ds_ref, g_ref, b_ref, w_ref, bias_ref, out_ref, dw_ref, dbias_ref, dg_ref, db_ref, rows):
        uv = uv_ref[rows, :].astype(F32)
        g = g_ref[...]
        cdf, u, xhat, rs, mixed, ws, vlo, vhi, tril, upper = _sgu_core(uv, g, b_ref[...], w_ref, bias_ref[...])
        dsg = ds_ref[rows, :]
        du = dsg * mixed
        dmixed = dsg * u
        dbias_ref[...] += dmixed
        dvn = []
        for pr in range(SGU_W // LANES):
            sl = slice(pr * LANES, (pr + 1) * LANES)
            dm2 = dmixed[:, sl]
            dlo = jnp.where(upper, 0.0, dm2).astype(BF16)
            dhi = jnp.where(upper, dm2, 0.0).astype(BF16)
            w0, w1 = ws[pr]
            dvn.append(lax.dot_general(w0, dlo, (TN, ((), ())), preferred_element_type=F32)
                       + lax.dot_general(w1, dhi, (TN, ((), ())), preferred_element_type=F32))
            dw0 = lax.dot_general(dlo, vlo[pr], (NT, ((), ())), preferred_element_type=F32)
            dw1 = lax.dot_general(dhi, vhi[pr], (NT, ((), ())), preferred_element_type=F32)
            dw_ref[2 * pr] += jnp.where(tril, dw0, 0.0)
            dw_ref[2 * pr + 1] += jnp.where(tril, dw1, 0.0)
        dvn = jnp.concatenate(dvn, axis=1)
        dg_ref[...] += jnp.sum(dvn * xhat, axis=0, keepdims=True)
        db_ref[...] += jnp.sum(dvn, axis=0, keepdims=True)
        dxh = dvn * g
        dv = rs * (dxh - jnp.mean(dxh, axis=1, keepdims=True) - xhat * jnp.mean(dxh * xhat, axis=1, keepdims=True))
        dz = jnp.concatenate([du, dv], axis=1)
        dgelu = cdf + uv * (INV_SQRT_2PI * jnp.exp(-0.5 * uv * uv))
        out_ref[rows, :] = (dz * dgelu).astype(BF16)

    outs = pl.pallas_call(
        body, grid=(nsteps,),
        in_specs=[pl.BlockSpec(memory_space=pl.ANY), pl.BlockSpec((step, 2 * SGU_W), lambda n: (n, 0)),
                  pl.BlockSpec((step, SGU_W), lambda n: (n, 0)), _full((1, SGU_W)), _full((1, SGU_W)),
                  _full((SGU_GROUPS, SGU_CHUNK, SGU_CHUNK)), _full((SGU_CHUNK, SGU_W)), _full((ATTN_W, ATTN_W))],
        out_specs=[pl.BlockSpec((step, 2 * SGU_W), lambda n: (n, 0)), _full((SGU_GROUPS, SGU_CHUNK, SGU_CHUNK)),
                   _full((SGU_CHUNK, SGU_W)), _full((1, SGU_W)), _full((1, SGU_W))],
        out_shape=[jax.ShapeDtypeStruct(dproj.shape, BF16), jax.ShapeDtypeStruct((SGU_GROUPS, SGU_CHUNK, SGU_CHUNK), F32),
                   jax.ShapeDtypeStruct((SGU_CHUNK, SGU_W), F32), jax.ShapeDtypeStruct((1, SGU_W), F32),
                   jax.ShapeDtypeStruct((1, SGU_W), F32)],
        input_output_aliases={0: 0},
        compiler_params=_cparams(1), name="sgu_bwd")(dproj, gu, dsgu, ln_g, ln_b, w_s, bias_exp, e)
    return outs


def _merge_fwd(attn, sgu, gu, x, w_pa, w_ps, w_out, g2):
    t = x.shape[0]
    tm = min(t, 512)

    def body(a_ref, s_ref, ga_ref, gb_ref, x_ref, wpa, wps, wo, g_ref, pa_ref, ps_ref, m_ref, x1_ref, h2_ref):
        pa = jnp.dot(a_ref[...], wpa[...], preferred_element_type=F32)
        ps = jnp.dot(s_ref[...], wps[...], preferred_element_type=F32)
        merged = (_sigmoid(ga_ref[...].astype(F32)) * pa + _sigmoid(gb_ref[...].astype(F32)) * ps).astype(BF16)
        x1 = x_ref[...] + jnp.dot(merged, wo[...], preferred_element_type=F32)
        xhat, _ = _rms_stats(x1)
        pa_ref[...] = pa.astype(BF16)
        ps_ref[...] = ps.astype(BF16)
        m_ref[...] = merged
        x1_ref[...] = x1
        h2_ref[...] = (xhat * g_ref[...]).astype(BF16)

    half = pl.BlockSpec((tm, ATTN_W), lambda i: (i, 0))
    full = pl.BlockSpec((tm, D_MODEL), lambda i: (i, 0))
    return pl.pallas_call(
        body, grid=(t // tm,),
        in_specs=[half, half, pl.BlockSpec((tm, D_MODEL), lambda i: (i, 1)), pl.BlockSpec((tm, D_MODEL), lambda i: (i, 2)),
                  full, _resident((ATTN_W, D_MODEL)), _resident((SGU_W, D_MODEL)), _resident((D_MODEL, D_MODEL)),
                  _full((1, D_MODEL))],
        out_specs=[full] * 5,
        out_shape=[jax.ShapeDtypeStruct((t, D_MODEL), BF16), jax.ShapeDtypeStruct((t, D_MODEL), BF16),
                   jax.ShapeDtypeStruct((t, D_MODEL), BF16), jax.ShapeDtypeStruct((t, D_MODEL), F32),
                   jax.ShapeDtypeStruct((t, D_MODEL), BF16)],
        compiler_params=_cparams(1), name="merge_fwd")(attn, sgu, gu, gu, x, w_pa, w_ps, w_out, g2)


def _merge_bwd(dx1b, gu, pa, ps, w_pa, w_ps, w_out):
    t = dx1b.shape[0]
    tm = min(t, 512)

    def body(d_ref, ga_ref, gb_ref, pa_ref, ps_ref, wpa, wps, wo, out_ref, dpa_ref, dps_ref, da_ref, dsg_ref):
        dm = lax.dot_general(d_ref[...], wo[...], (NT, ((), ())), preferred_element_type=F32)
        sa, sb = _sigmoid(ga_ref[...].astype(F32)), _sigmoid(gb_ref[...].astype(F32))
        dpa = (dm * sa).astype(BF16)
        dps = (dm * sb).astype(BF16)
        out_ref[:, 0:D_MODEL] = jnp.zeros((tm, D_MODEL), BF16)
        out_ref[:, D_MODEL:2 * D_MODEL] = (dm * pa_ref[...].astype(F32) * sa * (1.0 - sa)).astype(BF16)
        out_ref[:, 2 * D_MODEL:GU_COLS] = (dm * ps_ref[...].astype(F32) * sb * (1.0 - sb)).astype(BF16)
        dpa_ref[...] = dpa
        dps_ref[...] = dps
        da_ref[...] = lax.dot_general(dpa, wpa[...], (NT, ((), ())), preferred_element_type=F32)
        dsg_ref[...] = lax.dot_general(dps, wps[...], (NT, ((), ())), preferred_element_type=F32)

    half = pl.BlockSpec((tm, ATTN_W), lambda i: (i, 0))
    full = pl.BlockSpec((tm, D_MODEL), lambda i: (i, 0))
    return pl.pallas_call(
        body, grid=(t // tm,),
        in_specs=[full, pl.BlockSpec((tm, D_MODEL), lambda i: (i, 1)),
                  pl.BlockSpec((tm, D_MODEL), lambda i: (i, 2)), full, full,
                  _resident((ATTN_W, D_MODEL)), _resident((SGU_W, D_MODEL)), _resident((D_MODEL, D_MODEL))],
        out_specs=[pl.BlockSpec((tm, GU_COLS), lambda i: (i, 0)), full, full, half, half],
        out_shape=[jax.ShapeDtypeStruct((t, GU_COLS), BF16), jax.ShapeDtypeStruct((t, D_MODEL), BF16),
                   jax.ShapeDtypeStruct((t, D_MODEL), BF16), jax.ShapeDtypeStruct((t, ATTN_W), F32),
                   jax.ShapeDtypeStruct((t, SGU_W), F32)],
        compiler_params=_cparams(1), name="merge_bwd")(dx1b, gu, gu, pa, ps, w_pa, w_ps, w_out)


def _token_call(name, body, t, tm, ins, outs, reds=(), scratch=()):
    return pl.pallas_call(
        body, grid=(t // tm,), in_specs=[s for _, s in ins],
        out_specs=[o[2] for o in outs] + [_full(r) for r in reds],
        out_shape=[jax.ShapeDtypeStruct(o[0], o[1]) for o in outs] + [jax.ShapeDtypeStruct(r, F32) for r in reds],
        scratch_shapes=list(scratch), compiler_params=_cparams(1), name=name)(*[a for a, _ in ins])


def _rows_spec(tm, width):
    return pl.BlockSpec((tm, width), lambda i: (i, 0))


def _chips_spec(tm):
    return pl.BlockSpec((N_CHIPS, tm, FF_SHARD), lambda i: (0, i, 0))


def _zero_at_start(*refs):
    @pl.when(pl.program_id(0) == 0)
    def _():
        for r in refs:
            r[...] = jnp.zeros(r.shape, r.dtype)


def _ffn_fwd(h2, w_g, w_u):
    t = h2.shape[0]
    tm = min(t, 512)

    def body(h_ref, wg_ref, wu_ref, fa_ref, fb_ref, ff_ref):
        h = h_ref[...]
        for s in range(N_CHIPS):
            a = jnp.dot(h, wg_ref[s], preferred_element_type=F32)
            b = jnp.dot(h, wu_ref[s], preferred_element_type=F32)
            sg = _sigmoid(a)
            silu = a * sg
            fa_ref[s] = (b * (sg * (1.0 + a * (1.0 - sg)))).astype(BF16)
            fb_ref[s] = silu.astype(BF16)
            ff_ref[s] = (silu * b).astype(BF16)

    shp = (N_CHIPS, t, FF_SHARD)
    w_spec = _resident((N_CHIPS, D_MODEL, FF_SHARD))
    return _token_call("ffn_fwd", body, t, tm, [(h2, _rows_spec(tm, D_MODEL)), (w_g, w_spec), (w_u, w_spec)],
                       [(shp, BF16, _chips_spec(tm))] * 3)


def _ffn_down_loss(ff, w_d, x1, tgt, gf):
    t = x1.shape[0]
    tm = min(t, 512)

    def body(ff_ref, wd_ref, x1_ref, tgt_ref, g_ref, dx2_ref, dx2b_ref, loss_ref, dgf_ref):
        _zero_at_start(loss_ref, dgf_ref)
        acc = jnp.dot(ff_ref[0], wd_ref[0], preferred_element_type=F32)
        for s in range(1, N_CHIPS):
            acc = acc + jnp.dot(ff_ref[s], wd_ref[s], preferred_element_type=F32)
        x2 = x1_ref[...] + acc
        g = g_ref[...]
        xhat, rr = _rms_stats(x2)
        diff = xhat * g - tgt_ref[...]
        rows = jnp.sum(diff * diff, axis=1, keepdims=True)
        loss_ref[...] += jnp.broadcast_to(jnp.sum(rows, axis=0, keepdims=True) * (0.5 / D_MODEL), (1, LANES))
        dy = diff * (1.0 / D_MODEL)
        dgf_ref[...] += jnp.sum(dy * xhat, axis=0, keepdims=True)
        dx2 = _rms_bwd(dy, xhat, rr, g)
        dx2_ref[...] = dx2
        dx2b_ref[...] = dx2.astype(BF16)

    row = _rows_spec(tm, D_MODEL)
    return _token_call("ffn_down_loss", body, t, tm,
                       [(ff, _chips_spec(tm)), (w_d, _resident((N_CHIPS, FF_SHARD, D_MODEL))), (x1, row), (tgt, row),
                        (gf, _full((1, D_MODEL)))],
                       [((t, D_MODEL), F32, row), ((t, D_MODEL), BF16, row)], reds=[(1, LANES), (1, D_MODEL)])


def _ffn_bwd_act(dx2b, w_d, fa, fb):
    t = dx2b.shape[0]
    tm = min(t, 512)

    def body(d_ref, wd_ref, fa_ref, fb_ref, da_ref, db_ref):
        d = d_ref[...]
        for s in range(N_CHIPS):
            dff = lax.dot_general(d, wd_ref[s], (NT, ((), ())), preferred_element_type=F32)
            da_ref[s] = (dff * fa_ref[s].astype(F32)).astype(BF16)
            db_ref[s] = (dff * fb_ref[s].astype(F32)).astype(BF16)

    shp = (N_CHIPS, t, FF_SHARD)
    return _token_call("ffn_bwd_act", body, t, tm,
                       [(dx2b, _rows_spec(tm, D_MODEL)), (w_d, _resident((N_CHIPS, FF_SHARD, D_MODEL))),
                        (fa, _chips_spec(tm)), (fb, _chips_spec(tm))],
                       [(shp, BF16, _chips_spec(tm))] * 2)


def _ffn_bwd_in(da, db, w_g, w_u, x1, dx2, g2):
    t = x1.shape[0]
    tm = min(t, 512)

    def body(da_ref, db_ref, wg_ref, wu_ref, x1_ref, dx2_ref, g_ref, dx1_ref, dx1b_ref, dg_ref):
        _zero_at_start(dg_ref)
        acc = None
        for s in range(N_CHIPS):
            part = (lax.dot_general(da_ref[s], wg_ref[s], (NT, ((), ())), preferred_element_type=F32)
                    + lax.dot_general(db_ref[s], wu_ref[s], (NT, ((), ())), preferred_element_type=F32))
            acc = part if acc is None else acc + part
        xhat, rr = _rms_stats(x1_ref[...])
        dg_ref[...] += jnp.sum(acc * xhat, axis=0, keepdims=True)
        dx1 = dx2_ref[...] + _rms_bwd(acc, xhat, rr, g_ref[...])
        dx1_ref[...] = dx1
        dx1b_ref[...] = dx1.astype(BF16)

    row = _rows_spec(tm, D_MODEL)
    w_spec = _resident((N_CHIPS, D_MODEL, FF_SHARD))
    return _token_call("ffn_bwd_in", body, t, tm,
                       [(da, _chips_spec(tm)), (db, _chips_spec(tm)), (w_g, w_spec), (w_u, w_spec), (x1, row), (dx2, row),
                        (g2, _full((1, D_MODEL)))],
                       [((t, D_MODEL), F32, row), ((t, D_MODEL), BF16, row)], reds=[(1, D_MODEL)])


def _group_dh(d, w_refs):
    dh = None
    for part, w_ref in enumerate(w_refs):
        term = lax.dot_general(d[:, part * ATTN_W:(part + 1) * ATTN_W], w_ref[...], (NT, ((), ())),
                               preferred_element_type=F32)
        dh = term if dh is None else dh + term
    return dh


def _in_proj_bwd(dgu, dqkvs, w_in, x, dx1, g1):
    t = x.shape[0]
    tile = min(t, TILE)
    ngroups = len(DILATIONS)

    def body(*refs):
        dgu_ref, dq_refs = refs[0], refs[1:1 + ngroups]
        w0_ref, w1_ref = refs[1 + ngroups:3 + ngroups]
        wg_refs = [refs[3 + ngroups + 3 * g:6 + ngroups + 3 * g] for g in range(ngroups)]
        x_ref, dx1_ref, g_ref, dx_ref, dg_ref = refs[3 + 4 * ngroups:5 + 4 * ngroups + 3]
        slabs = refs[5 + 4 * ngroups + 3:]
        _zero_at_start(dg_ref)
        for g in range(1, ngroups):
            dil = DILATIONS[g]
            part = _group_dh(dq_refs[g][...].reshape(tile, GROUP_COLS), wg_refs[g])
            for r in range(dil):
                _put_class_rows(slabs[g - 1], r, dil, part[r * (tile // dil):(r + 1) * (tile // dil)])
        dh = lax.dot_general(dgu_ref[:, 0:GU_HALF], w0_ref[...], (NT, ((), ())), preferred_element_type=F32)
        dh = dh + lax.dot_general(dgu_ref[:, GU_HALF:], w1_ref[...], (NT, ((), ())), preferred_element_type=F32)
        dh = dh + _group_dh(dq_refs[0][0], wg_refs[0])
        for slab in slabs:
            dh = dh + _from_slabs(slab)
        xhat, rr = _rms_stats(x_ref[...])
        dg_ref[...] += jnp.sum(dh * xhat, axis=0, keepdims=True)
        dx_ref[...] = dx1_ref[...] + _rms_bwd(dh, xhat, rr, g_ref[...])

    row = _rows_spec(tile, D_MODEL)
    group_ins = [(dqkvs[g].reshape(d, t // d, GROUP_COLS), _group_spec(d, tile, GROUP_COLS)) for g, d in enumerate(DILATIONS)]
    w_specs = _gu_w_specs() + [s for g in range(ngroups) for s in _group_w_specs(g)]
    return _token_call(
        "in_proj_bwd", body, t, tile,
        [(dgu, _rows_spec(tile, GU_COLS))] + group_ins + [(w_in, s) for s in w_specs]
        + [(x, row), (dx1, row), (g1, _full((1, D_MODEL)))],
        [((t, D_MODEL), F32, row)], reds=[(1, D_MODEL)], scratch=[_slabs(tile, D_MODEL)] * (ngroups - 1))


WGRAD_TK = 2048


def _wgrad_mm(name, grid, a, a_spec, b, b_spec, acc_shape, out_shape, out_spec, dst=None):
    nk = grid[-1]

    def body(*refs):
        a_ref, b_ref, o_ref, acc_ref = refs[0], refs[1], refs[-2], refs[-1]
        k = pl.program_id(len(grid) - 1)
        part = lax.dot_general(a_ref[...], b_ref[...], (TN, ((), ())), preferred_element_type=F32)

        @pl.when(k == 0)
        def _():
            acc_ref[...] = part

        @pl.when(k > 0)
        def _():
            acc_ref[...] += part

        @pl.when(k == nk - 1)
        def _():
            o_ref[...] = acc_ref[...].astype(BF16)

    filled = [] if dst is None else [dst]
    return pl.pallas_call(
        body, grid=grid, in_specs=[a_spec, b_spec] + [pl.BlockSpec(memory_space=pl.ANY)] * len(filled),
        out_specs=out_spec, out_shape=jax.ShapeDtypeStruct(out_shape, BF16), scratch_shapes=[pltpu.VMEM(acc_shape, F32)],
        input_output_aliases={2: 0} if filled else {}, compiler_params=_cparams(len(grid)), name=name)(a, b, *filled)


def _wgrad_2d(name, a, b, tm, tn):
    t, k1 = a.shape
    n = b.shape[1]
    tk = min(t, WGRAD_TK)
    return _wgrad_mm(name, (k1 // tm, n // tn, t // tk), a, pl.BlockSpec((tk, tm), lambda i, j, k: (k, i)),
                     b, pl.BlockSpec((tk, tn), lambda i, j, k: (k, j)), (tm, tn), (k1, n),
                     pl.BlockSpec((tm, tn), lambda i, j, k: (i, j)))


def _wgrad_in(hs, dgu, dqkvs):
    t = dgu.shape[0]
    tk = min(t, WGRAD_TK)
    gu_block = QKV_BLOCKS * ATTN_W // GU_HALF
    parts = [(hs[0], dgu, GU_HALF, lambda j: j + gu_block)]
    parts += [(hs[g].reshape(t, D_MODEL), dqkvs[g], ATTN_W, lambda j, g=g: _w_in_block(j, g)) for g in range(3)]
    dst = None
    for n, (a, b, tn, block_of) in enumerate(parts):
        dst = _wgrad_mm(f"wgrad_in_{n}", (1, b.shape[1] // tn, t // tk),
                        a, pl.BlockSpec((tk, D_MODEL), lambda i, j, k: (k, 0)), b, pl.BlockSpec((tk, tn), lambda i, j, k: (k, j)),
                        (D_MODEL, tn), (D_MODEL, IN_COLS),
                        pl.BlockSpec((D_MODEL, tn), lambda i, j, k, block_of=block_of: (0, block_of(j))), dst=dst)
    return dst


def _wgrad_ff_in(name, h2, da):
    t = h2.shape[0]
    tk = min(t, WGRAD_TK)
    return _wgrad_mm(name, (N_CHIPS, 1, t // tk), h2, pl.BlockSpec((tk, D_MODEL), lambda i, j, k: (k, 0)),
                     da, pl.BlockSpec((None, tk, FF_SHARD), lambda i, j, k: (i, k, 0)), (D_MODEL, FF_SHARD),
                     (N_CHIPS, D_MODEL, FF_SHARD), pl.BlockSpec((None, D_MODEL, FF_SHARD), lambda i, j, k: (i, 0, 0)))


def _wgrad_ff_down(ff, dx2b):
    t = dx2b.shape[0]
    tk = min(t, WGRAD_TK)
    return _wgrad_mm("wgrad_ffn_down", (N_CHIPS, 1, t // tk), ff, pl.BlockSpec((None, tk, FF_SHARD), lambda i, j, k: (i, k, 0)),
                     dx2b, pl.BlockSpec((tk, D_MODEL), lambda i, j, k: (k, 0)), (FF_SHARD, D_MODEL),
                     (N_CHIPS, FF_SHARD, D_MODEL), pl.BlockSpec((None, FF_SHARD, D_MODEL), lambda i, j, k: (i, 0, 0)))


def _local_step(x, pos_col, tgt, g1, ln_g, ln_b, w_s, b_s, g2, gf, first_weight, late_weights, on_grads=None):
    tables = _rope_tables(pos_col)
    bias_exp = jnp.repeat(jnp.transpose(b_s), SGU_W // SGU_GROUPS, axis=1)

    hs = _norm_fwd(x, g1)
    w_p = first_weight([hs[0], bias_exp] + [table for pair in tables for table in pair])
    gu, qkvs = _in_proj(hs, w_p, tables)
    os_, ls_ = [], []
    for g, dil in enumerate(DILATIONS):
        o, lse = _attn_fwd(qkvs[g], g, dil)
        os_.append(o)
        ls_.append(lse)
    attn = _combine_fwd(os_, ls_)
    sgu = _sgu_fwd(gu, ln_g, ln_b, w_s, bias_exp)
    w_pa, w_ps, w_out, w_g, w_u, w_d = late_weights(attn)
    pa, ps, merged, x1, h2 = _merge_fwd(attn, sgu, gu, x, w_pa, w_ps, w_out, g2)
    fa, fb, ff = _ffn_fwd(h2, w_g, w_u)
    dx2, dx2b, loss, dgf = _ffn_down_loss(ff, w_d, x1, tgt, gf)

    da, db = _ffn_bwd_act(dx2b, w_d, fa, fb)
    dw_d = _wgrad_ff_down(ff, dx2b)
    dx1, dx1b, dg2 = _ffn_bwd_in(da, db, w_g, w_u, x1, dx2, g2)
    dw_g = _wgrad_ff_in("wgrad_ffn_gate", h2, da)
    dw_u = _wgrad_ff_in("wgrad_ffn_up", h2, db)

    dgu, dpa, dps, dattn, dsgu = _merge_bwd(dx1b, gu, pa, ps, w_pa, w_ps, w_out)
    dw_out = _wgrad_2d("wgrad_out", merged, dx1b, D_MODEL, D_MODEL)
    dw_pa = _wgrad_2d("wgrad_proj_attn", attn, dpa, ATTN_W, D_MODEL)
    dw_ps = _wgrad_2d("wgrad_proj_sgu", sgu, dps, SGU_W, D_MODEL)
    if on_grads is not None:
        ln_g = ln_g + on_grads(1, dict(w_proj_attn=dw_pa, w_proj_sgu=dw_ps, w_out=dw_out, w_ffn_gate=dw_g, w_ffn_up=dw_u,
                                       w_ffn_down=dw_d))[:, :SGU_W]
    dgu, dw_s, dbias, dln_g, dln_b = _sgu_bwd(dgu, gu, dsgu, ln_g, ln_b, w_s, bias_exp)
    dos, ccs = _combine_bwd(dattn, os_, ls_)
    dqkvs = [_attn_bwd(qkvs[g], dos[g], ccs[g], ls_[g], *tables[g], g, dil) for g, dil in enumerate(DILATIONS)]
    dw_p = _wgrad_in(hs, dgu, dqkvs)
    if on_grads is not None:
        g1 = g1 + on_grads(0, dict(w_in=dw_p))
    dx, dg1 = _in_proj_bwd(dgu, dqkvs, w_p, x, dx1, g1)

    db_s = jnp.transpose(dbias[:, ::SGU_W // SGU_GROUPS])
    small = dict(loss=loss, norm1_g=dg1, sgu_ln_g=dln_g, sgu_ln_b=dln_b, w_spatial=dw_s, b_spatial=db_s,
                 norm2_g=dg2, final_g=dgf)
    big = dict(w_in=dw_p, w_proj_attn=dw_pa, w_proj_sgu=dw_ps, w_out=dw_out, w_ffn_gate=dw_g, w_ffn_up=dw_u,
               w_ffn_down=dw_d)
    return dx, big, small


def _ew(name, fn, ins, out_dtypes, after=()):
    shp = ins[0].shape
    rows, cols = shp
    tr = next((cand for cand in (256, 352, 128) if rows % cand == 0 and rows > cand), rows)

    def body(*refs):
        res = fn(*[r[...] for r in refs[:len(ins)]])
        for o_ref, v in zip(refs[len(ins) + len(after):], res):
            o_ref[...] = v.astype(o_ref.dtype)

    spec = pl.BlockSpec((tr, cols), lambda i: (i, 0))
    return pl.pallas_call(
        body, grid=(rows // tr,), in_specs=[spec] * len(ins) + [pl.BlockSpec(memory_space=pl.ANY)] * len(after),
        out_specs=[spec] * len(out_dtypes), out_shape=[jax.ShapeDtypeStruct(shp, d) for d in out_dtypes],
        compiler_params=_cparams(1), name=name)(*ins, *after)


def _adamw_math(g, w, m, v):
    m = ADAM_B1 * m + (1.0 - ADAM_B1) * g
    v = ADAM_B2 * v + (1.0 - ADAM_B2) * (g * g)
    m_hat = m / (1.0 - ADAM_B1 ** ADAM_STEP)
    v_hat = v / (1.0 - ADAM_B2 ** ADAM_STEP)
    delta = -ADAM_LR * (m_hat / (jnp.sqrt(v_hat) + ADAM_EPS) + ADAM_WD * w)
    return delta, m, v


def _adamw(name, g, w, m, v):
    return _ew(name, lambda g_, w_, m_, v_: (g_,) + _adamw_math(g_, w_, m_, v_), [g, w, m, v], [F32] * 4)


VMEM_SPEC = pl.BlockSpec(memory_space=pltpu.VMEM)


def _for_row_chunks(rows, fn):
    ck = next(c for c in (64, 32, 16) if rows % c == 0)

    def step(i, carry):
        fn(pl.multiple_of(i * ck, ck), ck)
        return carry

    lax.fori_loop(0, rows // ck, step, 0)


def _place():
    x, y, c = lax.axis_index("x"), lax.axis_index("y"), lax.axis_index("c")
    chips = [(1 - x, y), (x, 1 - y), (1 - x, 1 - y)]
    return x, y, c, 2 * x + y, chips


def _rows(ref, start, size):
    if len(ref.shape) == 2:
        return ref.at[pl.ds(start, size), :]
    return ref.at[:, pl.ds(start, size), :]


def _gather_finish(name, shard, landed):
    k_rows, n = shard.shape
    kh = k_rows // 2

    def body(shard_hbm, land_hbm, out_ref, shard_ref, land_ref, loc, send, recv):
        x, y, c, me, chips = _place()
        sibling = (x, y, 1 - c)

        def window(core, chip):
            return out_ref.at[pl.ds(core * kh, kh), pl.ds(pl.multiple_of(chip * n, LANES), n)]

        loads = [pltpu.make_async_copy(land_hbm.at[j], land_ref.at[j], loc.at[0, j]) for j in range(3)]
        loads.append(pltpu.make_async_copy(shard_hbm, shard_ref, loc.at[0, 3]))
        for cp in loads:
            cp.start()
        copies, passed = [], []
        for j, chip in enumerate(chips):
            mine = window(c, 2 * chip[0] + chip[1])
            loads[j].wait()
            copies.append(pltpu.make_async_copy(land_ref.at[j], mine, loc.at[1, j]))
            passed.append(pltpu.make_async_remote_copy(src_ref=land_ref.at[j], dst_ref=mine, send_sem=send.at[j],
                                                       recv_sem=recv.at[j], device_id=sibling, device_id_type=MESH))
            copies[-1].start()
            passed[-1].start()
        loads[3].wait()
        copies.append(pltpu.make_async_copy(shard_ref, out_ref.at[:, pl.ds(pl.multiple_of(me * n, LANES), n)], loc.at[1, 3]))
        copies[-1].start()
        for j, chip in enumerate(chips):
            pltpu.make_async_remote_copy(src_ref=land_ref.at[j], dst_ref=window(1 - c, 2 * chip[0] + chip[1]), send_sem=send.at[j],
                                         recv_sem=recv.at[j], device_id=sibling, device_id_type=MESH).wait_recv()
        for cp in copies:
            cp.wait()
        for cp in passed:
            cp.wait_send()

    any_spec = pl.BlockSpec(memory_space=pl.ANY)
    return pl.pallas_call(
        body, in_specs=[any_spec] * 2, out_specs=any_spec,
        out_shape=jax.ShapeDtypeStruct((k_rows, N_CHIPS * n), shard.dtype),
        scratch_shapes=[pltpu.VMEM(shard.shape, shard.dtype), pltpu.VMEM(landed.shape, landed.dtype),
                        pltpu.SemaphoreType.DMA((2, 4)), pltpu.SemaphoreType.DMA((3,)), pltpu.SemaphoreType.DMA((3,))],
        compiler_params=pltpu.CompilerParams(vmem_limit_bytes=VMEM_LIMIT), name=name)(shard, landed)


HBM_SPEC = pl.BlockSpec(memory_space=pltpu.HBM)
SEM_SPEC = pl.BlockSpec(memory_space=pltpu.SEMAPHORE)
DATAFLOW = pltpu.SideEffectType.DATAFLOW_SIDE_EFFECTING
TOKEN_SHAPE = (1, D_MODEL)
N_PEERS = 7
SUM_SPLIT = 4
SUM_SPLIT_ELEMS = 512 * 1024


def _peers():
    x, y, c = lax.axis_index("x"), lax.axis_index("y"), lax.axis_index("c")
    flip = lambda v, f: 1 - v if f else v
    return [(flip(x, k & 4), flip(y, k & 2), flip(c, k & 1)) for k in range(1, N_PEERS + 1)]


def _piece_shape(shape):
    return (shape[-2] // 2, shape[2] if len(shape) == 3 else shape[1] // N_CHIPS)


def _device_piece(ref, chip, core):
    kh, n4 = _piece_shape(ref.shape)
    if len(ref.shape) == 3:
        return ref.at[chip, pl.ds(core * kh, kh), :]
    return ref.at[pl.ds(core * kh, kh), pl.ds(chip * n4, n4)]


def _exchange_copies(partials, lands, send, recv):
    return [pltpu.make_async_remote_copy(
        src_ref=_device_piece(partials[t], 2 * px + py, pc), dst_ref=lands[t].at[k], send_sem=send.at[t * N_PEERS + k],
        recv_sem=recv.at[t * N_PEERS + k], device_id=(px, py, pc), device_id_type=MESH)
        for t in range(len(partials)) for k, (px, py, pc) in enumerate(_peers())]


def _broadcast_copies(srcs, lands, send, recv):
    return [pltpu.make_async_remote_copy(
        src_ref=srcs[t], dst_ref=lands[t].at[k], send_sem=send.at[t * N_PEERS + k], recv_sem=recv.at[t * N_PEERS + k],
        device_id=peer, device_id_type=MESH)
        for t in range(len(srcs)) for k, peer in enumerate(_peers())]


class _LocalCopy:
    def __init__(self, src_ref, dst_ref, sem):
        self.copy = pltpu.make_async_copy(src_ref, dst_ref, sem)

    def start(self):
        self.copy.start()

    def wait_send(self):
        self.copy.wait()

    def wait_recv(self):
        pass


def _gather_copies(shards, lands, send, recv):
    x, y, c, me, chips = _place()
    copies = []
    for t in range(len(shards)):
        n = shards[t].shape[1]
        place = lands[t].at[me] if len(lands[t].shape) == 3 else lands[t].at[:, pl.ds(pl.multiple_of(me * n, LANES), n)]
        copies += [pltpu.make_async_remote_copy(
            src_ref=shards[t], dst_ref=place, send_sem=send.at[t * 4 + j], recv_sem=recv.at[t * 4 + j],
            device_id=(*chip, c), device_id_type=MESH) for j, chip in enumerate(chips)]
        copies.append(_LocalCopy(shards[t], place, send.at[t * 4 + 3]))
    return copies


def _gather_half_copies(shards, lands, send, recv):
    x, y, c, me, chips = _place()
    return [pltpu.make_async_remote_copy(
        src_ref=_rows(shards[t], c * (shards[t].shape[0] // 2), shards[t].shape[0] // 2), dst_ref=lands[t].at[j],
        send_sem=send.at[t * 3 + j], recv_sem=recv.at[t * 3 + j], device_id=(*chip, c), device_id_type=MESH)
        for t in range(len(shards)) for j, chip in enumerate(chips)]


def _split_start(name, copies, per_tensor, srcs, land_shapes):
    nt = len(srcs)
    lands = [lax.empty(s, a.dtype) for s, a in zip(land_shapes, srcs)]
    nsem = nt * per_tensor

    def body(*refs):
        send, recv = refs[2 * nt], refs[2 * nt + 1]
        for cp in copies(refs[:nt], refs[nt:2 * nt], send, recv):
            cp.start()
        refs[-1][...] = jnp.zeros(TOKEN_SHAPE, F32)

    hbm = lambda a: pltpu.with_memory_space_constraint(a, pltpu.HBM)
    outs = pl.pallas_call(
        body, name=name,
        out_shape=[pltpu.SemaphoreType.DMA((nsem,)), pltpu.SemaphoreType.DMA((nsem,))]
        + [pltpu.HBM(s.shape, s.dtype) for s in srcs] + [pltpu.HBM(l.shape, l.dtype) for l in lands]
        + [jax.ShapeDtypeStruct(TOKEN_SHAPE, F32)],
        in_specs=[HBM_SPEC] * (2 * nt), out_specs=[SEM_SPEC, SEM_SPEC] + [HBM_SPEC] * (2 * nt) + [VMEM_SPEC],
        input_output_aliases={i: 2 + i for i in range(2 * nt)},
        compiler_params=pltpu.CompilerParams(has_side_effects=DATAFLOW))(*[hbm(a) for a in list(srcs) + lands])
    return outs[0], outs[1], outs[2:2 + nt], outs[2 + nt:2 + 2 * nt], outs[-1]


def _split_wait(name, copies, send, recv, srcs, lands, after):
    nt = len(srcs)
    after = list(after) if isinstance(after, (list, tuple)) else [after]

    def body(*refs):
        for cp in copies(refs[:nt], refs[nt:2 * nt], refs[2 * nt], refs[2 * nt + 1]):
            cp.wait_send()
            cp.wait_recv()

    outs = pl.pallas_call(
        body, name=name,
        out_shape=[pltpu.HBM(s.shape, s.dtype) for s in srcs] + [pltpu.HBM(l.shape, l.dtype) for l in lands],
        in_specs=[HBM_SPEC] * (2 * nt) + [SEM_SPEC, SEM_SPEC] + [pl.BlockSpec(memory_space=pl.ANY)] * len(after),
        out_specs=[HBM_SPEC] * (2 * nt), input_output_aliases={i: i for i in range(2 * nt)},
        compiler_params=pltpu.CompilerParams(has_side_effects=DATAFLOW))(*srcs, *lands, send, recv, *after)
    return outs[:nt], outs[nt:]


def _device_sum(name, partials, lands):
    nt = len(partials)
    pieces = [_piece_shape(p.shape) for p in partials]
    units = []
    for t, (kh, n4) in enumerate(pieces):
        split = SUM_SPLIT if kh * n4 >= SUM_SPLIT_ELEMS else 1
        units += [(t, j * (kh // split), kh // split) for j in range(split)]
    nu = len(units)

    def body(*refs):
        ins, slots, outs = refs[:nt], refs[nt:2 * nt], refs[2 * nt:3 * nt]
        owns, landed, sums = refs[3 * nt:4 * nt], refs[4 * nt:5 * nt], refs[5 * nt:6 * nt]
        loc, send, recv = refs[6 * nt:]
        x, y, c, me, chips = _place()
        sibling = (x, y, 1 - c)
        loads = []
        for u, (t, r0, rows) in enumerate(units):
            loads.append((
                pltpu.make_async_copy(_rows(_device_piece(ins[t], me, c), r0, rows), _rows(owns[t], r0, rows), loc.at[0, u]),
                pltpu.make_async_copy(_rows(slots[t], r0, rows), _rows(landed[t], r0, rows), loc.at[1, u])))
        for u in range(min(2, nu)):
            for cp in loads[u]:
                cp.start()
        stores = []
        for u, (t, r0, rows) in enumerate(units):
            for cp in loads[u]:
                cp.wait()
            if u + 2 < nu:
                for cp in loads[u + 2]:
                    cp.start()

            def add(q0, ck, own=owns[t], slot=landed[t], dst=sums[t], r0=r0):
                at = pl.ds(pl.multiple_of(r0 + q0, ck), ck)
                acc = own[at, :].astype(F32)
                for k in range(N_PEERS):
                    acc = acc + slot[k, at, :].astype(F32)
                dst[at, :] = acc

            _for_row_chunks(rows, add)
            mine = _rows(outs[t], c * pieces[t][0] + r0, rows)
            stores.append((
                pltpu.make_async_copy(_rows(sums[t], r0, rows), mine, loc.at[2, u]),
                pltpu.make_async_remote_copy(src_ref=_rows(sums[t], r0, rows), dst_ref=mine, send_sem=send.at[u],
                                             recv_sem=recv.at[u], device_id=sibling, device_id_type=MESH)))
            for cp in stores[-1]:
                cp.start()
        for u, (t, r0, rows) in enumerate(units):
            pltpu.make_async_remote_copy(
                src_ref=_rows(sums[t], r0, rows), dst_ref=_rows(outs[t], (1 - c) * pieces[t][0] + r0, rows),
                send_sem=send.at[u], recv_sem=recv.at[u], device_id=sibling, device_id_type=MESH).wait_recv()
            stores[u][0].wait()
            stores[u][1].wait_send()

    any_spec = pl.BlockSpec(memory_space=pl.ANY)
    return pl.pallas_call(
        body, in_specs=[any_spec] * (2 * nt), out_specs=[any_spec] * nt,
        out_shape=[jax.ShapeDtypeStruct((2 * kh, n4), F32) for kh, n4 in pieces],
        scratch_shapes=[pltpu.VMEM(p, BF16) for p in pieces] + [pltpu.VMEM((N_PEERS,) + p, BF16) for p in pieces]
        + [pltpu.VMEM(p, F32) for p in pieces]
        + [pltpu.SemaphoreType.DMA((3, nu)), pltpu.SemaphoreType.DMA((nu,)), pltpu.SemaphoreType.DMA((nu,))],
        compiler_params=pltpu.CompilerParams(vmem_limit_bytes=VMEM_LIMIT), name=name)(*partials, *lands)


VEC_SHAPE = (8, D_MODEL + LANES)
VEC_SLOTS = dict(norm1_g=(slice(0, 1), slice(0, D_MODEL)), norm2_g=(slice(1, 2), slice(0, D_MODEL)),
                 final_g=(slice(2, 3), slice(0, D_MODEL)), sgu_ln_g=(slice(3, 4), slice(0, SGU_W)),
                 sgu_ln_b=(slice(3, 4), slice(SGU_W, 2 * SGU_W)), b_spatial=(slice(0, 8), slice(D_MODEL, D_MODEL + LANES)),
                 loss=(slice(4, 5), slice(0, LANES)))
VEC_PARAMS = ("norm1_g", "norm2_g", "final_g", "sgu_ln_g", "sgu_ln_b", "b_spatial")
SMALL_PARAMS = VEC_PARAMS + ("w_spatial",)
W_SPATIAL_2D = (SGU_GROUPS * SGU_CHUNK, SGU_CHUNK)


SMALL_GRADS = VEC_PARAMS + ("loss", "w_spatial")


def _small_shape(name):
    if name == "w_spatial":
        return W_SPATIAL_2D
    rows, cols = VEC_SLOTS[name]
    return (rows.stop - rows.start, cols.stop - cols.start)


def _pack_small(dst, parts):
    dst[...] = jnp.zeros(VEC_SHAPE, F32)
    for n, ref in parts.items():
        if n in VEC_SLOTS:
            dst[VEC_SLOTS[n]] = ref[...]


def _small_start(partials):
    names = VEC_PARAMS + ("loss",)

    def body(*refs):
        _pack_small(refs[-1], dict(zip(names, refs[:-1])))

    vec = pl.pallas_call(
        body, in_specs=[VMEM_SPEC] * len(names), out_specs=VMEM_SPEC, out_shape=jax.ShapeDtypeStruct(VEC_SHAPE, F32),
        name="small_params_pack")(*[partials[n].reshape(_small_shape(n)) for n in names])
    srcs = [vec, partials["w_spatial"].reshape(W_SPATIAL_2D)]
    return _split_start("small_params_start", _broadcast_copies, N_PEERS, srcs, [(N_PEERS,) + s.shape for s in srcs])


def _small_finish(started, after, w, m, v):
    own, landed = _split_wait("small_params_wait", _broadcast_copies, *started, after)
    ng, npar = len(SMALL_GRADS), len(SMALL_PARAMS)

    def update_body(*refs):
        vec_own, ws_own, vec_slots, ws_slots = refs[:4]
        w_in, m_in, v_in = (dict(zip(SMALL_PARAMS, refs[4 + k * npar:4 + (k + 1) * npar])) for k in range(3))
        o0 = 4 + 3 * npar
        g_out = dict(zip(SMALL_GRADS, refs[o0:o0 + ng]))
        d_out, m_out, v_out = (dict(zip(SMALL_PARAMS, refs[o0 + ng + k * npar:o0 + ng + (k + 1) * npar])) for k in range(3))
        vg, vw, vm, vv = refs[o0 + ng + 3 * npar:]
        me = 4 * lax.axis_index("x") + 2 * lax.axis_index("y") + lax.axis_index("c")

        def device_sum(mine, slots, read):
            acc = None
            for i in range(N_PEERS + 1):
                k = me ^ i
                part = jnp.where(k == 0, read(mine), read(slots.at[jnp.maximum(k, 1) - 1]))
                acc = part if acc is None else acc + part
            return acc

        vg[...] = device_sum(vec_own, vec_slots, lambda ref: ref[...])
        _pack_small(vw, w_in)
        _pack_small(vm, m_in)
        _pack_small(vv, v_in)
        d_vec, m_vec, v_vec = _adamw_math(vg[...], vw[...], vm[...], vv[...])
        vw[...] = d_vec
        vm[...] = m_vec
        vv[...] = v_vec
        for n in VEC_PARAMS + ("loss",):
            g_out[n][...] = vg[VEC_SLOTS[n]]
        for n in VEC_PARAMS:
            d_out[n][...] = vw[VEC_SLOTS[n]]
            m_out[n][...] = vm[VEC_SLOTS[n]]
            v_out[n][...] = vv[VEC_SLOTS[n]]

        def spatial(r0, ck):
            rows = pl.ds(r0, ck)
            g = device_sum(ws_own, ws_slots, lambda ref: ref[rows, :])
            d_, m_, v_ = _adamw_math(g, w_in["w_spatial"][rows, :], m_in["w_spatial"][rows, :], v_in["w_spatial"][rows, :])
            g_out["w_spatial"][rows, :] = g
            d_out["w_spatial"][rows, :] = d_
            m_out["w_spatial"][rows, :] = m_
            v_out["w_spatial"][rows, :] = v_

        _for_row_chunks(W_SPATIAL_2D[0], spatial)

    ins = list(own) + list(landed)
    for src in (w, m, v):
        ins += [src[n].reshape(_small_shape(n)) for n in SMALL_PARAMS]
    out_shapes = [jax.ShapeDtypeStruct(_small_shape(n), F32) for n in SMALL_GRADS + SMALL_PARAMS * 3]
    outs = pl.pallas_call(
        update_body, in_specs=[VMEM_SPEC] * len(ins), out_specs=[VMEM_SPEC] * len(out_shapes), out_shape=out_shapes,
        scratch_shapes=[pltpu.VMEM(VEC_SHAPE, F32)] * 4, name="small_params_update")(*ins)
    grads = dict(zip(SMALL_GRADS, outs[:ng]))
    rest = [dict(zip(SMALL_PARAMS, outs[ng + k * npar:ng + (k + 1) * npar])) for k in range(3)]
    return grads, rest[0], rest[1], rest[2]


BIG = ("w_in", "w_proj_attn", "w_proj_sgu", "w_out", "w_ffn_gate", "w_ffn_up", "w_ffn_down")
COMM_GROUPS = (("w_in",), ("w_proj_attn", "w_proj_sgu", "w_out", "w_ffn_gate", "w_ffn_up", "w_ffn_down"))
WEIGHTS = ("norm1_g", "w_in", "sgu_ln_g", "sgu_ln_b", "w_spatial", "b_spatial", "w_proj_attn", "w_proj_sgu", "w_out",
           "norm2_g", "w_ffn_gate", "w_ffn_up", "w_ffn_down", "final_g")


def kernel(x, positions, norm1_g, w_in, sgu_ln_g, sgu_ln_b, w_spatial, b_spatial, w_proj_attn, w_proj_sgu, w_out, norm2_g, w_ffn_gate, w_ffn_up, w_ffn_down, final_g, loss_target, m_norm1_g, m_w_in, m_sgu_ln_g, m_sgu_ln_b, m_w_spatial, m_b_spatial, m_w_proj_attn, m_w_proj_sgu, m_w_out, m_norm2_g, m_w_ffn_gate, m_w_ffn_up, m_w_ffn_down, m_final_g, v_norm1_g, v_w_in, v_sgu_ln_g, v_sgu_ln_b, v_w_spatial, v_b_spatial, v_w_proj_attn, v_w_proj_sgu, v_w_out, v_norm2_g, v_w_ffn_gate, v_w_ffn_up, v_w_ffn_down, v_final_g):
    w = dict(norm1_g=norm1_g, w_in=w_in, sgu_ln_g=sgu_ln_g, sgu_ln_b=sgu_ln_b, w_spatial=w_spatial, b_spatial=b_spatial,
             w_proj_attn=w_proj_attn, w_proj_sgu=w_proj_sgu, w_out=w_out, norm2_g=norm2_g, w_ffn_gate=w_ffn_gate,
             w_ffn_up=w_ffn_up, w_ffn_down=w_ffn_down, final_g=final_g)
    m = dict(norm1_g=m_norm1_g, w_in=m_w_in, sgu_ln_g=m_sgu_ln_g, sgu_ln_b=m_sgu_ln_b, w_spatial=m_w_spatial,
             b_spatial=m_b_spatial, w_proj_attn=m_w_proj_attn, w_proj_sgu=m_w_proj_sgu, w_out=m_w_out, norm2_g=m_norm2_g,
             w_ffn_gate=m_w_ffn_gate, w_ffn_up=m_w_ffn_up, w_ffn_down=m_w_ffn_down, final_g=m_final_g)
    v = dict(norm1_g=v_norm1_g, w_in=v_w_in, sgu_ln_g=v_sgu_ln_g, sgu_ln_b=v_sgu_ln_b, w_spatial=v_w_spatial,
             b_spatial=v_b_spatial, w_proj_attn=v_w_proj_attn, w_proj_sgu=v_w_proj_sgu, w_out=v_w_out, norm2_g=v_norm2_g,
             w_ffn_gate=v_w_ffn_gate, w_ffn_up=v_w_ffn_up, w_ffn_down=v_w_ffn_down, final_g=v_final_g)
    t = x.shape[1]

    def cast(n, after):
        flip = jnp.transpose if w[n].shape[-1] % LANES else (lambda a: a)
        return flip(_ew(f"cast_{n}", lambda a: (a,), [flip(w[n][0])], [BF16], after)[0])

    shards = {"w_in": cast("w_in", [])}
    late = COMM_GROUPS[1]
    k_in, n_in = shards["w_in"].shape
    *first, token = _split_start("gather_start_0", _gather_half_copies, 3, [shards["w_in"]], [(3, k_in // 2, n_in)])
    shards.update({n: cast(n, [token]) for n in late})
    pending = {}

    def first_weight(after):
        srcs, filled = _split_wait("gather_wait_0", _gather_half_copies, *first, list(after) + [shards[n] for n in late])
        gath_in, late_shards = lax.optimization_barrier(
            (_gather_finish("gather_finish_0", srcs[0], filled[0]), [shards[n] for n in late]))
        land_shapes = [(s.shape[0], N_CHIPS * s.shape[1]) if n.startswith("w_proj") else (N_CHIPS,) + s.shape
                       for n, s in zip(late, late_shards)]
        *pending["late"], _ = _split_start("gather_start_1", _gather_copies, 4, late_shards, land_shapes)
        return gath_in

    def late_weights(after):
        _, filled = _split_wait("gather_wait_1", _gather_copies, *pending["late"], after)
        gath = dict(zip(late, filled))
        return (gath["w_proj_attn"], gath["w_proj_sgu"],
                gath["w_out"].reshape(D_MODEL, D_MODEL), gath["w_ffn_gate"], gath["w_ffn_up"], gath["w_ffn_down"])

    exchanges = {}

    def on_grads(i, partials):
        if "w_out" in partials:
            partials["w_out"] = partials["w_out"].reshape(N_CHIPS, D_MODEL // N_CHIPS, D_MODEL)
        parts = [partials[n] for n in COMM_GROUPS[i]]
        *exchanges[i], started = _split_start(
            f"rs_exchange_start_{i}", _exchange_copies, N_PEERS, parts, [(N_PEERS,) + _piece_shape(p.shape) for p in parts])
        return started

    dx, _, small = _local_step(
        x[0], positions.reshape(t, 1), loss_target[0], norm1_g + token, sgu_ln_g, sgu_ln_b,
        w_spatial[0], b_spatial[0] + token[:1, :LANES],
        norm2_g, final_g.reshape(1, D_MODEL), first_weight, late_weights, on_grads=on_grads)
    *small_started, small_token = _small_start(small)

    grads = {}
    for i in (1, 0):
        parts, filled = _split_wait(f"rs_exchange_wait_{i}", _exchange_copies, *exchanges[i], small_token)
        grads.update(zip(COMM_GROUPS[i], _device_sum(f"rs_device_sum_{i}", parts, filled)))

    delta, new_m, new_v, updated = {}, {}, {}, []
    for n in BIG:
        shp = w[n].shape
        flip = jnp.transpose if shp[-1] % LANES else (lambda a: a)
        outs = _adamw(f"adamw_{n}", flip(grads[n]), flip(w[n][0]), flip(m[n][0]), flip(v[n][0]))
        grads[n], delta[n], new_m[n], new_v[n] = (flip(a).reshape(shp) for a in outs)
        updated.append(outs[-1])

    g_s, d_s, m_s, v_s = _small_finish(small_started, updated, w, m, v)
    loss = g_s["loss"][0, 0]
    for n in SMALL_PARAMS:
        shp = w[n].shape
        grads[n], delta[n], new_m[n], new_v[n] = (a[n].reshape(shp) for a in (g_s, d_s, m_s, v_s))

    return (loss, dx.reshape(x.shape), *[grads[n] for n in WEIGHTS], *[delta[n] for n in WEIGHTS],
            *[new_m[n] for n in WEIGHTS], *[new_v[n] for n in WEIGHTS])
```

```python
import functools

import numpy as np
import jax
import jax.numpy as jnp
from jax import lax
from jax.experimental import pallas as pl
from jax.experimental.pallas import tpu as pltpu

F32, BF16 = jnp.float32, jnp.bfloat16
MESH = pl.DeviceIdType.MESH

D_MODEL = 1024
HEAD_DIM = 64
ATTN_W = 512
DILATIONS = (1, 4, 16)
BLK = 128
ATTN_BLOCKS_PER_STEP = 4
ROPE_DIM = 16
ROPE_THETA = 500000.0
SGU_W = 512
SGU_CHUNK = 128
SGU_GROUPS = 8
D_FF = 2816
N_CHIPS = 4
FF_SHARD = D_FF // N_CHIPS
IN_COLS = 7680
EPS = 1e-6
NEG = -1e30
LANES = 128
VMEM_LIMIT = 52 * 1024 * 1024

ADAM_LR, ADAM_B1, ADAM_B2, ADAM_EPS, ADAM_WD, ADAM_STEP = 0.001, 0.9, 0.999, 1e-08, 0.01, 10

QKV_BLOCKS = 9


def _w_in_block(part, g):
    return part * len(DILATIONS) + g


def _cparams(ngrid):
    return pltpu.CompilerParams(dimension_semantics=("arbitrary",) * ngrid, vmem_limit_bytes=VMEM_LIMIT)


def _full(shape):
    return pl.BlockSpec(shape, lambda *_: (0,) * len(shape))


def _resident(shape):
    return pl.BlockSpec(shape, lambda *_: (0,) * len(shape), pipeline_mode=pl.Buffered(1))


NT = ((1,), (1,))
TN = ((0,), (0,))


def _rope(v, cos_t, sin_t):
    half = ROPE_DIM // 2
    first = (lax.broadcasted_iota(jnp.int32, cos_t.shape, 1) % HEAD_DIM) < half
    outs = []
    for cs in range(v.shape[1] // LANES):
        x = v[:, cs * LANES:(cs + 1) * LANES]
        partner = jnp.where(first, pltpu.roll(x, LANES - half, axis=1), pltpu.roll(x, half, axis=1))
        outs.append(x * cos_t + partner * sin_t)
    return outs[0] if len(outs) == 1 else jnp.concatenate(outs, axis=1)


def _spread_heads(v2, upper):
    other = pltpu.roll(v2, HEAD_DIM, axis=1)
    h0 = jnp.where(upper, other, v2)
    h1 = jnp.where(upper, v2, other)
    return jnp.concatenate([jnp.concatenate([h0, h0], axis=1), jnp.concatenate([h1, h1], axis=1)], axis=0)


def _sigmoid(v):
    return 0.5 * jnp.tanh(0.5 * v) + 0.5


def _rms_stats(v):
    r = lax.rsqrt(jnp.mean(v * v, axis=-1, keepdims=True) + EPS)
    return v * r, r


def _rms_bwd(dy, xhat, r, g):
    dxh = dy * g
    return r * (dxh - xhat * jnp.mean(dxh * xhat, axis=-1, keepdims=True))


def _head_sum_matrix():
    idx = np.arange(ATTN_W) // HEAD_DIM
    return jnp.asarray((idx[:, None] == idx[None, :]).astype(np.float32), dtype=BF16)


def _group_sum(v, e):
    hi = v.astype(BF16)
    lo = (v - hi.astype(F32)).astype(BF16)
    return jnp.dot(hi, e, preferred_element_type=F32) + jnp.dot(lo, e, preferred_element_type=F32)


TILE = 512


def _to_slabs(slab_ref, v):
    for cs in range(slab_ref.shape[0]):
        slab_ref[cs] = v[:, cs * LANES:(cs + 1) * LANES]


def _from_slabs(slab_ref):
    return jnp.concatenate([slab_ref[cs] for cs in range(slab_ref.shape[0])], axis=1)


def _class_rows(slab_ref, r, dil):
    n = slab_ref.shape[1] // dil
    return jnp.concatenate([slab_ref.at[cs][pl.ds(r, n, stride=dil), :] for cs in range(slab_ref.shape[0])], axis=1)


def _put_class_rows(slab_ref, r, dil, v):
    n = slab_ref.shape[1] // dil
    for cs in range(slab_ref.shape[0]):
        slab_ref.at[cs][pl.ds(r, n, stride=dil), :] = v[:, cs * LANES:(cs + 1) * LANES]


def _natural_from_group(slab_ref, grp_ref):
    dil = grp_ref.shape[0]
    for r in range(dil):
        _put_class_rows(slab_ref, r, dil, grp_ref[r].astype(F32))
    return _from_slabs(slab_ref)


def _group_from_natural(slab_ref, grp_ref, v):
    dil = grp_ref.shape[0]
    _to_slabs(slab_ref, v)
    for r in range(dil):
        grp_ref[r] = _class_rows(slab_ref, r, dil).astype(grp_ref.dtype)


def _group_spec(dil, tile, width):
    return pl.BlockSpec((dil, tile // dil, width), lambda i, *_: (0, i, 0))


def _slabs(tile, width):
    return pltpu.VMEM((width // LANES, tile, LANES), F32)


def _rope_consts():
    lane = np.arange(LANES) % HEAD_DIM
    fi = lane % (ROPE_DIM // 2)
    invf = np.where(lane < ROPE_DIM, ROPE_THETA ** (-(2.0 * fi) / ROPE_DIM), 0.0)
    sgn = np.where(lane < ROPE_DIM // 2, -1.0, np.where(lane < ROPE_DIM, 1.0, 0.0))
    return (jnp.asarray(invf.astype(np.float32)).reshape(1, LANES), jnp.asarray(sgn.astype(np.float32)).reshape(1, LANES))


def _rope_tables(pos_col):
    t = pos_col.shape[0]
    tile = min(t, TILE)
    invf, sgn = _rope_consts()

    def body(p_ref, f_ref, s_ref, c0, s0, c1, s1, c2, s2, slab_c, slab_s):
        ang = p_ref[...].astype(F32) * f_ref[...]
        cos, sin = jnp.cos(ang), jnp.sin(ang) * s_ref[...]
        c0[...] = cos
        s0[...] = sin
        _group_from_natural(slab_c, c1, cos)
        _group_from_natural(slab_s, s1, sin)
        for r in range(DILATIONS[2]):
            c2[r] = _class_rows(slab_c, r, DILATIONS[2])
            s2[r] = _class_rows(slab_s, r, DILATIONS[2])

    nat = pl.BlockSpec((tile, LANES), lambda i: (i, 0))
    specs, shapes = [nat, nat], [(t, LANES)] * 2
    for d in DILATIONS[1:]:
        specs += [_group_spec(d, tile, LANES)] * 2
        shapes += [(d, t // d, LANES)] * 2
    outs = pl.pallas_call(
        body, grid=(t // tile,),
        in_specs=[pl.BlockSpec((tile, 1), lambda i: (i, 0)), _full((1, LANES)), _full((1, LANES))],
        out_specs=specs, out_shape=[jax.ShapeDtypeStruct(s, F32) for s in shapes],
        scratch_shapes=[_slabs(tile, LANES)] * 2,
        compiler_params=_cparams(1), name="rope_tables")(pos_col, invf, sgn)
    return [(outs[2 * g].reshape(t, LANES), outs[2 * g + 1].reshape(t, LANES)) for g in range(len(DILATIONS))]


def _norm_fwd(x, g):
    t = x.shape[0]
    tile = min(t, TILE)

    def body(x_ref, g_ref, h0_ref, h1_ref, h2_ref, slab):
        xhat, _ = _rms_stats(x_ref[...])
        hn = xhat * g_ref[...]
        h0_ref[...] = hn.astype(BF16)
        _group_from_natural(slab, h1_ref, hn)
        for r in range(DILATIONS[2]):
            h2_ref[r] = _class_rows(slab, r, DILATIONS[2]).astype(BF16)

    nat = pl.BlockSpec((tile, D_MODEL), lambda i: (i, 0))
    return pl.pallas_call(
        body, grid=(t // tile,),
        in_specs=[nat, _full((1, D_MODEL))],
        out_specs=[nat] + [_group_spec(d, tile, D_MODEL) for d in DILATIONS[1:]],
        out_shape=[jax.ShapeDtypeStruct((t, D_MODEL), BF16)]
        + [jax.ShapeDtypeStruct((d, t // d, D_MODEL), BF16) for d in DILATIONS[1:]],
        scratch_shapes=[_slabs(tile, D_MODEL)],
        compiler_params=_cparams(1), name="norm1_fwd")(x, g)


GU_COLS = 3072
GROUP_COLS = 1536
GU_HALF = GU_COLS // 2


def _w_in_spec(width, block):
    return pl.BlockSpec((D_MODEL, width), lambda i: (0, block), pipeline_mode=pl.Buffered(1))


def _gu_w_specs():
    first = QKV_BLOCKS * ATTN_W // GU_HALF
    return [_w_in_spec(GU_HALF, first), _w_in_spec(GU_HALF, first + 1)]


def _group_w_specs(g):
    return [_w_in_spec(ATTN_W, _w_in_block(part, g)) for part in range(3)]


def _in_proj(hs, w_in, tables):
    t = hs[0].shape[0]
    tm = min(t, 1024)

    def body_gu(h_ref, w0_ref, w1_ref, o_ref):
        h = h_ref[...]
        o_ref[:, 0:GU_HALF] = jnp.dot(h, w0_ref[...], preferred_element_type=F32).astype(BF16)
        o_ref[:, GU_HALF:] = jnp.dot(h, w1_ref[...], preferred_element_type=F32).astype(BF16)

    gu = _token_call("in_proj_gates_uv", body_gu, t, tm,
                     [(hs[0], _rows_spec(tm, D_MODEL))] + [(w_in, s) for s in _gu_w_specs()],
                     [((t, GU_COLS), BF16, _rows_spec(tm, GU_COLS))])[0]

    qkvs = []
    for g in range(len(DILATIONS)):

        def body_qkv(h_ref, wq_ref, wk_ref, wv_ref, cos_ref, sin_ref, o_ref):
            h = h_ref[...]
            cos_w, sin_w = cos_ref[...], sin_ref[...]
            q = jnp.dot(h, wq_ref[...], preferred_element_type=F32)
            o_ref[:, 0:ATTN_W] = (_rope(q, cos_w, sin_w) * HEAD_DIM ** -0.5).astype(BF16)
            k = jnp.dot(h, wk_ref[...], preferred_element_type=F32)
            o_ref[:, ATTN_W:2 * ATTN_W] = _rope(k, cos_w, sin_w).astype(BF16)
            o_ref[:, 2 * ATTN_W:] = jnp.dot(h, wv_ref[...], preferred_element_type=F32).astype(BF16)

        cos_t, sin_t = tables[g]
        qkvs.append(_token_call(
            f"in_proj_qkv_g{g}", body_qkv, t, tm,
            [(hs[g].reshape(t, D_MODEL), _rows_spec(tm, D_MODEL))] + [(w_in, s) for s in _group_w_specs(g)]
            + [(cos_t, _rows_spec(tm, LANES)), (sin_t, _rows_spec(tm, LANES))],
            [((t, GROUP_COLS), BF16, _rows_spec(tm, GROUP_COLS))])[0])
    return gu, qkvs


def _attn_masks(n):
    row = lax.broadcasted_iota(jnp.int32, (2 * BLK, 2 * BLK), 0) % BLK
    col = lax.broadcasted_iota(jnp.int32, (2 * BLK, 2 * BLK), 1)
    diff = BLK + row - col
    valid = (diff >= 0) & (diff <= BLK) & ((col >= BLK) | (n > 0))
    upper = lax.broadcasted_iota(jnp.int32, (BLK, LANES), 1) >= HEAD_DIM
    return valid, upper


def _stack_heads(v2, upper):
    zero = jnp.zeros_like(v2)
    return jnp.concatenate([jnp.where(upper, zero, v2), jnp.where(upper, v2, zero)], axis=0)


def _unstack_heads(v, upper):
    return jnp.where(upper, v[BLK:], v[:BLK])


def _attn_fwd(qkv, g, dil):
    t = qkv.shape[0]
    length = t // dil
    nb = length // BLK
    per_step = min(nb, ATTN_BLOCKS_PER_STEP)
    view = qkv.reshape(dil, length, GROUP_COLS)

    def body(q_ref, kc_ref, kp_ref, vc_ref, vp_ref, o_ref, l_ref, kwin, vwin):
        n = pl.program_id(1)
        kwin[0:BLK] = kp_ref[...]
        kwin[BLK:] = kc_ref[...]
        vwin[0:BLK] = vp_ref[...]
        vwin[BLK:] = vc_ref[...]

        def block(b, carry):
            valid, upper = _attn_masks(n * per_step + b)
            rows = pl.ds(pl.multiple_of(b * BLK, BLK), BLK)
            window = pl.ds(pl.multiple_of(b * BLK, BLK), 2 * BLK)
            slabs = [slice(p * LANES, (p + 1) * LANES) for p in range(ATTN_W // LANES)]
            ss = [lax.dot_general(_stack_heads(q_ref[rows, sl], upper), kwin[window, sl], (NT, ((), ())),
                                  preferred_element_type=F32) for sl in slabs]
            soft = []
            for s in ss:
                s = jnp.where(valid, s, NEG)
                m = jnp.max(s, axis=1, keepdims=True)
                pe = jnp.exp(s - m)
                soft.append((m, pe, jnp.sum(pe, axis=1, keepdims=True)))
            for sl, (m, pe, den) in zip(slabs, soft):
                o = jnp.dot(pe.astype(BF16), vwin[window, sl], preferred_element_type=F32) / den
                lse = jnp.broadcast_to(m + jnp.log(den), (2 * BLK, LANES))
                o_ref[rows, sl] = _unstack_heads(o, upper).astype(BF16)
                l_ref[rows, sl] = _unstack_heads(lse, upper)
            return carry

        lax.fori_loop(0, per_step, block, 0)

    rows = per_step * BLK
    cur = lambda part: pl.BlockSpec((None, rows, ATTN_W), lambda r, n: (r, n, part))
    prev = lambda part: pl.BlockSpec((None, BLK, ATTN_W), lambda r, n: (r, jnp.maximum(n * per_step - 1, 0), part))
    out_spec = pl.BlockSpec((None, rows, ATTN_W), lambda r, n: (r, n, 0))
    return pl.pallas_call(
        body, grid=(dil, nb // per_step),
        in_specs=[cur(0), cur(1), prev(1), cur(2), prev(2)],
        out_specs=[out_spec, out_spec],
        out_shape=[jax.ShapeDtypeStruct((dil, length, ATTN_W), BF16), jax.ShapeDtypeStruct((dil, length, ATTN_W), F32)],
        scratch_shapes=[pltpu.VMEM((rows + BLK, ATTN_W), BF16)] * 2,
        compiler_params=_cparams(2), name=f"attn_fwd_g{g}")(view, view, view, view, view)


def _alphas(l0, l1, l2):
    m = jnp.maximum(jnp.maximum(l0, l1), l2)
    e0, e1, e2 = jnp.exp(l0 - m), jnp.exp(l1 - m), jnp.exp(l2 - m)
    inv = 1.0 / (e0 + e1 + e2)
    return e0 * inv, e1 * inv, e2 * inv


def _natural_group_values(o_refs, l_refs, slabs):
    os_ = [o_refs[0][0].astype(F32)] + [_natural_from_group(slabs[2 * g - 2], o_refs[g]) for g in (1, 2)]
    ls_ = [l_refs[0][0]] + [_natural_from_group(slabs[2 * g - 1], l_refs[g]) for g in (1, 2)]
    return os_, ls_


def _combine_fwd(os_, ls_):
    t = os_[0].shape[1]
    tile = min(t, TILE)

    def body(o0, o1, o2, l0, l1, l2, a_ref, *slabs):
        ov, lv = _natural_group_values((o0, o1, o2), (l0, l1, l2), slabs)
        a0, a1, a2 = _alphas(*lv)
        a_ref[...] = (a0 * ov[0] + a1 * ov[1] + a2 * ov[2]).astype(BF16)

    specs = [_group_spec(d, tile, ATTN_W) for d in DILATIONS]
    return pl.pallas_call(
        body, grid=(t // tile,), in_specs=specs * 2, out_specs=pl.BlockSpec((tile, ATTN_W), lambda i: (i, 0)),
        out_shape=jax.ShapeDtypeStruct((t, ATTN_W), BF16),
        scratch_shapes=[_slabs(tile, ATTN_W)] * 4,
        compiler_params=_cparams(1), name="combine_fwd")(*os_, *ls_)


def _combine_bwd(dattn, os_, ls_):
    t = dattn.shape[0]
    tile = min(t, TILE)
    e = _head_sum_matrix()

    def body(d_ref, o0, o1, o2, l0, l1, l2, e_ref, do0, do1, do2, c0, c1, c2, *slabs):
        ov, lv = _natural_group_values((o0, o1, o2), (l0, l1, l2), slabs)
        alphas = _alphas(*lv)
        d = d_ref[...]
        attn = alphas[0] * ov[0] + alphas[1] * ov[1] + alphas[2] * ov[2]
        s = _group_sum(d * attn, e_ref[...])
        do0[0] = (alphas[0] * d).astype(BF16)
        c0[0] = -alphas[0] * s
        for g, do_ref, c_ref in ((1, do1, c1), (2, do2, c2)):
            _group_from_natural(slabs[2 * g - 2], do_ref, alphas[g] * d)
            _group_from_natural(slabs[2 * g - 1], c_ref, -alphas[g] * s)

    specs = [_group_spec(d, tile, ATTN_W) for d in DILATIONS]
    shapes = [(d, t // d, ATTN_W) for d in DILATIONS]
    outs = pl.pallas_call(
        body, grid=(t // tile,),
        in_specs=[pl.BlockSpec((tile, ATTN_W), lambda i: (i, 0))] + specs * 2 + [_full((ATTN_W, ATTN_W))],
        out_specs=specs * 2,
        out_shape=[jax.ShapeDtypeStruct(s, BF16) for s in shapes] + [jax.ShapeDtypeStruct(s, F32) for s in shapes],
        scratch_shapes=[_slabs(tile, ATTN_W)] * 4,
        compiler_params=_cparams(1), name="combine_bwd")(dattn, *os_, *ls_, e)
    return outs[:3], outs[3:]


def _attn_bwd(qkv, do, cc, lse, cos_t, sin_t, g, dil):
    t = qkv.shape[0]
    length = t // dil
    nb = length // BLK
    per_step = min(nb, ATTN_BLOCKS_PER_STEP)
    nsteps = nb // per_step
    rows_per_step = per_step * BLK
    qkv_v = qkv.reshape(dil, length, GROUP_COLS)
    cos_v, sin_v = (a.reshape(dil, length, LANES) for a in (cos_t, sin_t))
    scale = HEAD_DIM ** -0.5
    dq_cols, dk_cols, dv_cols = (slice(i * ATTN_W, (i + 1) * ATTN_W) for i in range(3))

    def body(q_ref, kc_ref, kp_ref, vc_ref, vp_ref, do_ref, c_ref, l_ref, cosc, sinc, cosp, sinp,
             out_ref, acc, kwin, vwin, cwin, swin):
        n = pl.program_id(1)

        def one_block(b):
            valid, upper = _attn_masks(n * per_step + b)
            start = b * BLK if isinstance(b, int) else pl.multiple_of(b * BLK, BLK)
            rows, before, window = pl.ds(start, BLK), pl.ds(start, BLK), pl.ds(start, 2 * BLK)
            own = pl.ds(start + BLK, BLK)
            dq_parts, dkp_parts, dkc_parts, dvp_parts, dvc_parts = [], [], [], [], []
            npairs = ATTN_W // LANES
            slabs = [slice(p * LANES, (p + 1) * LANES) for p in range(npairs)]
            qss = [_stack_heads(q_ref[rows, sl], upper) for sl in slabs]
            doss = [_stack_heads(do_ref[rows, sl], upper) for sl in slabs]
            ss = [lax.dot_general(qss[p], kwin[window, slabs[p]], (NT, ((), ())), preferred_element_type=F32) for p in range(npairs)]
            dpvs = [lax.dot_general(doss[p], vwin[window, slabs[p]], (NT, ((), ())), preferred_element_type=F32)
                    for p in range(npairs)]
            pes = [jnp.exp(jnp.where(valid, ss[p], NEG) - _spread_heads(l_ref[rows, slabs[p]], upper)) for p in range(npairs)]
            dss = [(pes[p] * (dpvs[p] + _spread_heads(c_ref[rows, slabs[p]], upper))).astype(BF16) for p in range(npairs)]
            for p in range(npairs):
                qs, dos, ds = qss[p], doss[p], dss[p]
                dq2 = _unstack_heads(jnp.dot(ds, kwin[window, slabs[p]], preferred_element_type=F32), upper)
                dk2 = lax.dot_general(ds, qs, (TN, ((), ())), preferred_element_type=F32)
                dv2 = lax.dot_general(pes[p].astype(BF16), dos, (TN, ((), ())), preferred_element_type=F32)
                dq_parts.append(dq2)
                dkp_parts.append(dk2[:BLK])
                dkc_parts.append(dk2[BLK:])
                dvp_parts.append(dv2[:BLK])
                dvc_parts.append(dv2[BLK:])
            dq = _rope(jnp.concatenate(dq_parts, axis=1) * scale, cwin[own, :], swin[own, :])
            dkc = _rope(jnp.concatenate(dkc_parts, axis=1), cwin[own, :], swin[own, :])
            dkp = _rope(jnp.concatenate(dkp_parts, axis=1), cwin[before, :], swin[before, :])
            return dq, dkp, dkc, jnp.concatenate(dvp_parts, axis=1), jnp.concatenate(dvc_parts, axis=1)

        @pl.when(n < nsteps)
        def _():
            kwin[0:BLK] = kp_ref[...]
            kwin[BLK:] = kc_ref[...]
            vwin[0:BLK] = vp_ref[...]
            vwin[BLK:] = vc_ref[...]
            cwin[0:BLK] = cosp[...]
            cwin[BLK:] = cosc[...]
            swin[0:BLK] = -sinp[...]
            swin[BLK:] = -sinc[...]
            dq, dkp, dkc, dvp, dvc = one_block(0)
            last = slice(rows_per_step - BLK, rows_per_step)

            @pl.when(n > 0)
            def _():
                if per_step > 1:
                    out_ref[0:rows_per_step - BLK, :] = acc[0:rows_per_step - BLK, :].astype(BF16)
                out_ref[last, dq_cols] = acc[last, dq_cols].astype(BF16)
                out_ref[last, dk_cols] = (acc[last, dk_cols] + dkp).astype(BF16)
                out_ref[last, dv_cols] = (acc[last, dv_cols] + dvp).astype(BF16)

            acc[0:BLK, dq_cols] = dq
            acc[0:BLK, dk_cols] = dkc
            acc[0:BLK, dv_cols] = dvc

            def later(b, carry):
                dq, dkp, dkc, dvp, dvc = one_block(b)
                start = pl.multiple_of(b * BLK, BLK)
                before, rows = pl.ds(start - BLK, BLK), pl.ds(start, BLK)
                acc[before, dk_cols] += dkp
                acc[before, dv_cols] += dvp
                acc[rows, dq_cols] = dq
                acc[rows, dk_cols] = dkc
                acc[rows, dv_cols] = dvc
                return carry

            lax.fori_loop(1, per_step, later, 0)

        @pl.when(n == flush_at)
        def _():
            out_ref[...] = acc[...].astype(BF16)

    flush_at = nsteps - 1 if nsteps == 1 else nsteps
    out_lag = 0 if nsteps == 1 else 1
    nc = lambda n: jnp.minimum(n, nsteps - 1)
    npv = lambda n: jnp.maximum(jnp.minimum(n, nsteps - 1) * per_step - 1, 0)
    cur = lambda part: pl.BlockSpec((None, rows_per_step, ATTN_W), lambda r, n: (r, nc(n), part))
    prev = lambda part: pl.BlockSpec((None, BLK, ATTN_W), lambda r, n: (r, npv(n), part))
    row = pl.BlockSpec((None, rows_per_step, ATTN_W), lambda r, n: (r, nc(n), 0))
    tab_c = pl.BlockSpec((None, rows_per_step, LANES), lambda r, n: (r, nc(n), 0))
    tab_p = pl.BlockSpec((None, BLK, LANES), lambda r, n: (r, npv(n), 0))
    out_spec = pl.BlockSpec((None, rows_per_step, GROUP_COLS), lambda r, n: (r, jnp.maximum(n - out_lag, 0), 0))
    out = pl.pallas_call(
        body, grid=(dil, nsteps + out_lag),
        in_specs=[cur(0), cur(1), prev(1), cur(2), prev(2), row, row, row, tab_c, tab_c, tab_p, tab_p],
        out_specs=out_spec,
        out_shape=jax.ShapeDtypeStruct((dil, length, GROUP_COLS), BF16),
        scratch_shapes=[pltpu.VMEM((rows_per_step, GROUP_COLS), F32)]
        + [pltpu.VMEM((rows_per_step + BLK, ATTN_W), BF16)] * 2 + [pltpu.VMEM((rows_per_step + BLK, LANES), F32)] * 2,
        compiler_params=_cparams(2), name=f"attn_bwd_g{g}")(
            qkv_v, qkv_v, qkv_v, qkv_v, qkv_v, do, cc, lse, cos_v, sin_v, cos_v, sin_v)
    return out.reshape(t, GROUP_COLS)


SQRT_HALF = 0.7071067811865476
INV_SQRT_2PI = 0.3989422804014327


def _sgu_core(uv, g, b, w_ref, bias):
    cdf = 0.5 * (1.0 + lax.erf(uv * SQRT_HALF))
    z = uv * cdf
    u, v = z[:, :SGU_W], z[:, SGU_W:]
    mu = jnp.mean(v, axis=1, keepdims=True)
    xc = v - mu
    rs = lax.rsqrt(jnp.mean(xc * xc, axis=1, keepdims=True) + EPS)
    xhat = xc * rs
    vn = xhat * g + b
    row = lax.broadcasted_iota(jnp.int32, (SGU_CHUNK, SGU_CHUNK), 0)
    col = lax.broadcasted_iota(jnp.int32, (SGU_CHUNK, SGU_CHUNK), 1)
    tril = row >= col
    upper = lax.broadcasted_iota(jnp.int32, (SGU_CHUNK, LANES), 1) >= SGU_W // SGU_GROUPS
    ws, vlo, vhi, mixed = [], [], [], []
    for pr in range(SGU_W // LANES):
        sl = slice(pr * LANES, (pr + 1) * LANES)
        w0 = jnp.where(tril, w_ref[2 * pr], 0.0).astype(BF16)
        w1 = jnp.where(tril, w_ref[2 * pr + 1], 0.0).astype(BF16)
        vn2 = vn[:, sl]
        lo = jnp.where(upper, 0.0, vn2).astype(BF16)
        hi = jnp.where(upper, vn2, 0.0).astype(BF16)
        mixed.append(jnp.dot(w0, lo, preferred_element_type=F32) + jnp.dot(w1, hi, preferred_element_type=F32)
                     + bias[:, sl])
        ws.append((w0, w1))
        vlo.append(lo)
        vhi.append(hi)
    return cdf, u, xhat, rs, jnp.concatenate(mixed, axis=1), ws, vlo, vhi, tril, upper


SGU_STEP = 4 * SGU_CHUNK


def _for_chunks(step_rows, fn):
    def one(ci, carry):
        fn(pl.ds(pl.multiple_of(ci * SGU_CHUNK, SGU_CHUNK), SGU_CHUNK))
        return carry

    lax.fori_loop(0, step_rows // SGU_CHUNK, one, 0)


def _sgu_fwd(gu, ln_g, ln_b, w_s, bias_exp):
    t = gu.shape[0]
    step = min(t, SGU_STEP)

    def body(uv_ref, g_ref, b_ref, w_ref, bias_ref, o_ref):
        def chunk(rows):
            _, u, _, _, mixed, *_ = _sgu_core(uv_ref[rows, :].astype(F32), g_ref[...], b_ref[...], w_ref, bias_ref[...])
            o_ref[rows, :] = (u * mixed).astype(BF16)

        _for_chunks(step, chunk)

    return pl.pallas_call(
        body, grid=(t // step,),
        in_specs=[pl.BlockSpec((step, 2 * SGU_W), lambda n: (n, 0)), _full((1, SGU_W)), _full((1, SGU_W)),
                  _full((SGU_GROUPS, SGU_CHUNK, SGU_CHUNK)), _full((SGU_CHUNK, SGU_W))],
        out_specs=pl.BlockSpec((step, SGU_W), lambda n: (n, 0)),
        out_shape=jax.ShapeDtypeStruct((t, SGU_W), BF16),
        compiler_params=_cparams(1), name="sgu_fwd")(gu, ln_g, ln_b, w_s, bias_exp)


def _sgu_bwd(dproj, gu, dsgu, ln_g, ln_b, w_s, bias_exp):
    t = gu.shape[0]
    step = min(t, SGU_STEP)
    nsteps = t // step
    e = _head_sum_matrix()

    def body(dp_in, uv_ref, ds_ref, g_ref, b_ref, w_ref, bias_ref, e_ref, out_ref, dw_ref, dbias_ref, dg_ref, db_ref):
        n = pl.program_id(0)

        @pl.when(n == 0)
        def _():
            dw_ref[...] = jnp.zeros(dw_ref.shape, F32)
            dbias_ref[...] = jnp.zeros(dbias_ref.shape, F32)
            dg_ref[...] = jnp.zeros(dg_ref.shape, F32)
            db_ref[...] = jnp.zeros(db_ref.shape, F32)

        _for_chunks(step, functools.partial(chunk, uv_ref, ds_ref, g_ref, b_ref, w_ref, bias_ref, out_ref, dw_ref, dbias_ref,
                                            dg_ref, db_ref))

        @pl.when(n == nsteps - 1)
        def _():
            dbias_ref[...] = _group_sum(dbias_ref[...], e_ref[...])

    def chunk(uv_ref, ds_ref, g_ref, b_ref, w_ref, bias_ref, out_ref, dw_ref, dbias_ref, dg_ref, db_ref, rows):
        uv = uv_ref[rows, :].astype(F32)
        g = g_ref[...]
        cdf, u, xhat, rs, mixed, ws, vlo, vhi, tril, upper = _sgu_core(uv, g, b_ref[...], w_ref, bias_ref[...])
        dsg = ds_ref[rows, :]
        du = dsg * mixed
        dmixed = dsg * u
        dbias_ref[...] += dmixed
        dvn = []
        for pr in range(SGU_W // LANES):
            sl = slice(pr * LANES, (pr + 1) * LANES)
            dm2 = dmixed[:, sl]
            dlo = jnp.where(upper, 0.0, dm2).astype(BF16)
            dhi = jnp.where(upper, dm2, 0.0).astype(BF16)
            w0, w1 = ws[pr]
            dvn.append(lax.dot_general(w0, dlo, (TN, ((), ())), preferred_element_type=F32)
                       + lax.dot_general(w1, dhi, (TN, ((), ())), preferred_element_type=F32))
            dw0 = lax.dot_general(dlo, vlo[pr], (NT, ((), ())), preferred_element_type=F32)
            dw1 = lax.dot_general(dhi, vhi[pr], (NT, ((), ())), preferred_element_type=F32)
            dw_ref[2 * pr] += jnp.where(tril, dw0, 0.0)
            dw_ref[2 * pr + 1] += jnp.where(tril, dw1, 0.0)
        dvn = jnp.concatenate(dvn, axis=1)
        dg_ref[...] += jnp.sum(dvn * xhat, axis=0, keepdims=True)
        db_ref[...] += jnp.sum(dvn, axis=0, keepdims=True)
        dxh = dvn * g
        dv = rs * (dxh - jnp.mean(dxh, axis=1, keepdims=True) - xhat * jnp.mean(dxh * xhat, axis=1, keepdims=True))
        dz = jnp.concatenate([du, dv], axis=1)
        dgelu = cdf + uv * (INV_SQRT_2PI * jnp.exp(-0.5 * uv * uv))
        out_ref[rows, :] = (dz * dgelu).astype(BF16)

    outs = pl.pallas_call(
        body, grid=(nsteps,),
        in_specs=[pl.BlockSpec(memory_space=pl.ANY), pl.BlockSpec((step, 2 * SGU_W), lambda n: (n, 0)),
                  pl.BlockSpec((step, SGU_W), lambda n: (n, 0)), _full((1, SGU_W)), _full((1, SGU_W)),
                  _full((SGU_GROUPS, SGU_CHUNK, SGU_CHUNK)), _full((SGU_CHUNK, SGU_W)), _full((ATTN_W, ATTN_W))],
        out_specs=[pl.BlockSpec((step, 2 * SGU_W), lambda n: (n, 0)), _full((SGU_GROUPS, SGU_CHUNK, SGU_CHUNK)),
                   _full((SGU_CHUNK, SGU_W)), _full((1, SGU_W)), _full((1, SGU_W))],
        out_shape=[jax.ShapeDtypeStruct(dproj.shape, BF16), jax.ShapeDtypeStruct((SGU_GROUPS, SGU_CHUNK, SGU_CHUNK), F32),
                   jax.ShapeDtypeStruct((SGU_CHUNK, SGU_W), F32), jax.ShapeDtypeStruct((1, SGU_W), F32),
                   jax.ShapeDtypeStruct((1, SGU_W), F32)],
        input_output_aliases={0: 0},
        compiler_params=_cparams(1), name="sgu_bwd")(dproj, gu, dsgu, ln_g, ln_b, w_s, bias_exp, e)
    return outs


def _merge_fwd(attn, sgu, gu, x, w_pa, w_ps, w_out, g2):
    t = x.shape[0]
    tm = min(t, 512)

    def body(a_ref, s_ref, ga_ref, gb_ref, x_ref, wpa, wps, wo, g_ref, pa_ref, ps_ref, m_ref, x1_ref, h2_ref):
        pa = jnp.dot(a_ref[...], wpa[...], preferred_element_type=F32)
        ps = jnp.dot(s_ref[...], wps[...], preferred_element_type=F32)
        merged = (_sigmoid(ga_ref[...].astype(F32)) * pa + _sigmoid(gb_ref[...].astype(F32)) * ps).astype(BF16)
        x1 = x_ref[...] + jnp.dot(merged, wo[...], preferred_element_type=F32)
        xhat, _ = _rms_stats(x1)
        pa_ref[...] = pa.astype(BF16)
        ps_ref[...] = ps.astype(BF16)
        m_ref[...] = merged
        x1_ref[...] = x1
        h2_ref[...] = (xhat * g_ref[...]).astype(BF16)

    half = pl.BlockSpec((tm, ATTN_W), lambda i: (i, 0))
    full = pl.BlockSpec((tm, D_MODEL), lambda i: (i, 0))
    return pl.pallas_call(
        body, grid=(t // tm,),
        in_specs=[half, half, pl.BlockSpec((tm, D_MODEL), lambda i: (i, 1)), pl.BlockSpec((tm, D_MODEL), lambda i: (i, 2)),
                  full, _resident((ATTN_W, D_MODEL)), _resident((SGU_W, D_MODEL)), _resident((D_MODEL, D_MODEL)),
                  _full((1, D_MODEL))],
        out_specs=[full] * 5,
        out_shape=[jax.ShapeDtypeStruct((t, D_MODEL), BF16), jax.ShapeDtypeStruct((t, D_MODEL), BF16),
                   jax.ShapeDtypeStruct((t, D_MODEL), BF16), jax.ShapeDtypeStruct((t, D_MODEL), F32),
                   jax.ShapeDtypeStruct((t, D_MODEL), BF16)],
        compiler_params=_cparams(1), name="merge_fwd")(attn, sgu, gu, gu, x, w_pa, w_ps, w_out, g2)


def _merge_bwd(dx1b, gu, pa, ps, w_pa, w_ps, w_out):
    t = dx1b.shape[0]
    tm = min(t, 512)

    def body(d_ref, ga_ref, gb_ref, pa_ref, ps_ref, wpa, wps, wo, out_ref, dpa_ref, dps_ref, da_ref, dsg_ref):
        dm = lax.dot_general(d_ref[...], wo[...], (NT, ((), ())), preferred_element_type=F32)
        sa, sb = _sigmoid(ga_ref[...].astype(F32)), _sigmoid(gb_ref[...].astype(F32))
        dpa = (dm * sa).astype(BF16)
        dps = (dm * sb).astype(BF16)
        out_ref[:, 0:D_MODEL] = jnp.zeros((tm, D_MODEL), BF16)
        out_ref[:, D_MODEL:2 * D_MODEL] = (dm * pa_ref[...].astype(F32) * sa * (1.0 - sa)).astype(BF16)
        out_ref[:, 2 * D_MODEL:GU_COLS] = (dm * ps_ref[...].astype(F32) * sb * (1.0 - sb)).astype(BF16)
        dpa_ref[...] = dpa
        dps_ref[...] = dps
        da_ref[...] = lax.dot_general(dpa, wpa[...], (NT, ((), ())), preferred_element_type=F32)
        dsg_ref[...] = lax.dot_general(dps, wps[...], (NT, ((), ())), preferred_element_type=F32)

    half = pl.BlockSpec((tm, ATTN_W), lambda i: (i, 0))
    full = pl.BlockSpec((tm, D_MODEL), lambda i: (i, 0))
    return pl.pallas_call(
        body, grid=(t // tm,),
        in_specs=[full, pl.BlockSpec((tm, D_MODEL), lambda i: (i, 1)),
                  pl.BlockSpec((tm, D_MODEL), lambda i: (i, 2)), full, full,
                  _resident((ATTN_W, D_MODEL)), _resident((SGU_W, D_MODEL)), _resident((D_MODEL, D_MODEL))],
        out_specs=[pl.BlockSpec((tm, GU_COLS), lambda i: (i, 0)), full, full, half, half],
        out_shape=[jax.ShapeDtypeStruct((t, GU_COLS), BF16), jax.ShapeDtypeStruct((t, D_MODEL), BF16),
                   jax.ShapeDtypeStruct((t, D_MODEL), BF16), jax.ShapeDtypeStruct((t, ATTN_W), F32),
                   jax.ShapeDtypeStruct((t, SGU_W), F32)],
        compiler_params=_cparams(1), name="merge_bwd")(dx1b, gu, gu, pa, ps, w_pa, w_ps, w_out)


def _token_call(name, body, t, tm, ins, outs, reds=(), scratch=()):
    return pl.pallas_call(
        body, grid=(t // tm,), in_specs=[s for _, s in ins],
        out_specs=[o[2] for o in outs] + [_full(r) for r in reds],
        out_shape=[jax.ShapeDtypeStruct(o[0], o[1]) for o in outs] + [jax.ShapeDtypeStruct(r, F32) for r in reds],
        scratch_shapes=list(scratch), compiler_params=_cparams(1), name=name)(*[a for a, _ in ins])


def _rows_spec(tm, width):
    return pl.BlockSpec((tm, width), lambda i: (i, 0))


def _chips_spec(tm):
    return pl.BlockSpec((N_CHIPS, tm, FF_SHARD), lambda i: (0, i, 0))


def _zero_at_start(*refs):
    @pl.when(pl.program_id(0) == 0)
    def _():
        for r in refs:
            r[...] = jnp.zeros(r.shape, r.dtype)


def _ffn_fwd(h2, w_g, w_u):
    t = h2.shape[0]
    tm = min(t, 512)

    def body(h_ref, wg_ref, wu_ref, fa_ref, fb_ref, ff_ref):
        h = h_ref[...]
        for s in range(N_CHIPS):
            a = jnp.dot(h, wg_ref[s], preferred_element_type=F32)
            b = jnp.dot(h, wu_ref[s], preferred_element_type=F32)
            sg = _sigmoid(a)
            silu = a * sg
            fa_ref[s] = (b * (sg * (1.0 + a * (1.0 - sg)))).astype(BF16)
            fb_ref[s] = silu.astype(BF16)
            ff_ref[s] = (silu * b).astype(BF16)

    shp = (N_CHIPS, t, FF_SHARD)
    w_spec = _resident((N_CHIPS, D_MODEL, FF_SHARD))
    return _token_call("ffn_fwd", body, t, tm, [(h2, _rows_spec(tm, D_MODEL)), (w_g, w_spec), (w_u, w_spec)],
                       [(shp, BF16, _chips_spec(tm))] * 3)


def _ffn_down_loss(ff, w_d, x1, tgt, gf):
    t = x1.shape[0]
    tm = min(t, 512)

    def body(ff_ref, wd_ref, x1_ref, tgt_ref, g_ref, dx2_ref, dx2b_ref, loss_ref, dgf_ref):
        _zero_at_start(loss_ref, dgf_ref)
        acc = jnp.dot(ff_ref[0], wd_ref[0], preferred_element_type=F32)
        for s in range(1, N_CHIPS):
            acc = acc + jnp.dot(ff_ref[s], wd_ref[s], preferred_element_type=F32)
        x2 = x1_ref[...] + acc
        g = g_ref[...]
        xhat, rr = _rms_stats(x2)
        diff = xhat * g - tgt_ref[...]
        rows = jnp.sum(diff * diff, axis=1, keepdims=True)
        loss_ref[...] += jnp.broadcast_to(jnp.sum(rows, axis=0, keepdims=True) * (0.5 / D_MODEL), (1, LANES))
        dy = diff * (1.0 / D_MODEL)
        dgf_ref[...] += jnp.sum(dy * xhat, axis=0, keepdims=True)
        dx2 = _rms_bwd(dy, xhat, rr, g)
        dx2_ref[...] = dx2
        dx2b_ref[...] = dx2.astype(BF16)

    row = _rows_spec(tm, D_MODEL)
    return _token_call("ffn_down_loss", body, t, tm,
                       [(ff, _chips_spec(tm)), (w_d, _resident((N_CHIPS, FF_SHARD, D_MODEL))), (x1, row), (tgt, row),
                        (gf, _full((1, D_MODEL)))],
                       [((t, D_MODEL), F32, row), ((t, D_MODEL), BF16, row)], reds=[(1, LANES), (1, D_MODEL)])


def _ffn_bwd_act(dx2b, w_d, fa, fb):
    t = dx2b.shape[0]
    tm = min(t, 512)

    def body(d_ref, wd_ref, fa_ref, fb_ref, da_ref, db_ref):
        d = d_ref[...]
        for s in range(N_CHIPS):
            dff = lax.dot_general(d, wd_ref[s], (NT, ((), ())), preferred_element_type=F32)
            da_ref[s] = (dff * fa_ref[s].astype(F32)).astype(BF16)
            db_ref[s] = (dff * fb_ref[s].astype(F32)).astype(BF16)

    shp = (N_CHIPS, t, FF_SHARD)
    return _token_call("ffn_bwd_act", body, t, tm,
                       [(dx2b, _rows_spec(tm, D_MODEL)), (w_d, _resident((N_CHIPS, FF_SHARD, D_MODEL))),
                        (fa, _chips_spec(tm)), (fb, _chips_spec(tm))],
                       [(shp, BF16, _chips_spec(tm))] * 2)


def _ffn_bwd_in(da, db, w_g, w_u, x1, dx2, g2):
    t = x1.shape[0]
    tm = min(t, 512)

    def body(da_ref, db_ref, wg_ref, wu_ref, x1_ref, dx2_ref, g_ref, dx1_ref, dx1b_ref, dg_ref):
        _zero_at_start(dg_ref)
        acc = None
        for s in range(N_CHIPS):
            part = (lax.dot_general(da_ref[s], wg_ref[s], (NT, ((), ())), preferred_element_type=F32)
                    + lax.dot_general(db_ref[s], wu_ref[s], (NT, ((), ())), preferred_element_type=F32))
            acc = part if acc is None else acc + part
        xhat, rr = _rms_stats(x1_ref[...])
        dg_ref[...] += jnp.sum(acc * xhat, axis=0, keepdims=True)
        dx1 = dx2_ref[...] + _rms_bwd(acc, xhat, rr, g_ref[...])
        dx1_ref[...] = dx1
        dx1b_ref[...] = dx1.astype(BF16)

    row = _rows_spec(tm, D_MODEL)
    w_spec = _resident((N_CHIPS, D_MODEL, FF_SHARD))
    return _token_call("ffn_bwd_in", body, t, tm,
                       [(da, _chips_spec(tm)), (db, _chips_spec(tm)), (w_g, w_spec), (w_u, w_spec), (x1, row), (dx2, row),
                        (g2, _full((1, D_MODEL)))],
                       [((t, D_MODEL), F32, row), ((t, D_MODEL), BF16, row)], reds=[(1, D_MODEL)])


def _group_dh(d, w_refs):
    dh = None
    for part, w_ref in enumerate(w_refs):
        term = lax.dot_general(d[:, part * ATTN_W:(part + 1) * ATTN_W], w_ref[...], (NT, ((), ())),
                               preferred_element_type=F32)
        dh = term if dh is None else dh + term
    return dh


def _in_proj_bwd(dgu, dqkvs, w_in, x, dx1, g1):
    t = x.shape[0]
    tile = min(t, TILE)
    ngroups = len(DILATIONS)

    def body(*refs):
        dgu_ref, dq_refs = refs[0], refs[1:1 + ngroups]
        w0_ref, w1_ref = refs[1 + ngroups:3 + ngroups]
        wg_refs = [refs[3 + ngroups + 3 * g:6 + ngroups + 3 * g] for g in range(ngroups)]
        x_ref, dx1_ref, g_ref, dx_ref, dg_ref = refs[3 + 4 * ngroups:5 + 4 * ngroups + 3]
        slabs = refs[5 + 4 * ngroups + 3:]
        _zero_at_start(dg_ref)
        for g in range(1, ngroups):
            dil = DILATIONS[g]
            part = _group_dh(dq_refs[g][...].reshape(tile, GROUP_COLS), wg_refs[g])
            for r in range(dil):
                _put_class_rows(slabs[g - 1], r, dil, part[r * (tile // dil):(r + 1) * (tile // dil)])
        dh = lax.dot_general(dgu_ref[:, 0:GU_HALF], w0_ref[...], (NT, ((), ())), preferred_element_type=F32)
        dh = dh + lax.dot_general(dgu_ref[:, GU_HALF:], w1_ref[...], (NT, ((), ())), preferred_element_type=F32)
        dh = dh + _group_dh(dq_refs[0][0], wg_refs[0])
        for slab in slabs:
            dh = dh + _from_slabs(slab)
        xhat, rr = _rms_stats(x_ref[...])
        dg_ref[...] += jnp.sum(dh * xhat, axis=0, keepdims=True)
        dx_ref[...] = dx1_ref[...] + _rms_bwd(dh, xhat, rr, g_ref[...])

    row = _rows_spec(tile, D_MODEL)
    group_ins = [(dqkvs[g].reshape(d, t // d, GROUP_COLS), _group_spec(d, tile, GROUP_COLS)) for g, d in enumerate(DILATIONS)]
    w_specs = _gu_w_specs() + [s for g in range(ngroups) for s in _group_w_specs(g)]
    return _token_call(
        "in_proj_bwd", body, t, tile,
        [(dgu, _rows_spec(tile, GU_COLS))] + group_ins + [(w_in, s) for s in w_specs]
        + [(x, row), (dx1, row), (g1, _full((1, D_MODEL)))],
        [((t, D_MODEL), F32, row)], reds=[(1, D_MODEL)], scratch=[_slabs(tile, D_MODEL)] * (ngroups - 1))


WGRAD_TK = 2048


def _wgrad_mm(name, grid, a, a_spec, b, b_spec, acc_shape, out_shape, out_spec, dst=None):
    nk = grid[-1]

    def body(*refs):
        a_ref, b_ref, o_ref, acc_ref = refs[0], refs[1], refs[-2], refs[-1]
        k = pl.program_id(len(grid) - 1)
        part = lax.dot_general(a_ref[...], b_ref[...], (TN, ((), ())), preferred_element_type=F32)

        @pl.when(k == 0)
        def _():
            acc_ref[...] = part

        @pl.when(k > 0)
        def _():
            acc_ref[...] += part

        @pl.when(k == nk - 1)
        def _():
            o_ref[...] = acc_ref[...].astype(BF16)

    filled = [] if dst is None else [dst]
    return pl.pallas_call(
        body, grid=grid, in_specs=[a_spec, b_spec] + [pl.BlockSpec(memory_space=pl.ANY)] * len(filled),
        out_specs=out_spec, out_shape=jax.ShapeDtypeStruct(out_shape, BF16), scratch_shapes=[pltpu.VMEM(acc_shape, F32)],
        input_output_aliases={2: 0} if filled else {}, compiler_params=_cparams(len(grid)), name=name)(a, b, *filled)


def _wgrad_2d(name, a, b, tm, tn):
    t, k1 = a.shape
    n = b.shape[1]
    tk = min(t, WGRAD_TK)
    return _wgrad_mm(name, (k1 // tm, n // tn, t // tk), a, pl.BlockSpec((tk, tm), lambda i, j, k: (k, i)),
                     b, pl.BlockSpec((tk, tn), lambda i, j, k: (k, j)), (tm, tn), (k1, n),
                     pl.BlockSpec((tm, tn), lambda i, j, k: (i, j)))


def _wgrad_in(hs, dgu, dqkvs):
    t = dgu.shape[0]
    tk = min(t, WGRAD_TK)
    gu_block = QKV_BLOCKS * ATTN_W // GU_HALF
    parts = [(hs[0], dgu, GU_HALF, lambda j: j + gu_block)]
    parts += [(hs[g].reshape(t, D_MODEL), dqkvs[g], ATTN_W, lambda j, g=g: _w_in_block(j, g)) for g in range(3)]
    dst = None
    for n, (a, b, tn, block_of) in enumerate(parts):
        dst = _wgrad_mm(f"wgrad_in_{n}", (1, b.shape[1] // tn, t // tk),
                        a, pl.BlockSpec((tk, D_MODEL), lambda i, j, k: (k, 0)), b, pl.BlockSpec((tk, tn), lambda i, j, k: (k, j)),
                        (D_MODEL, tn), (D_MODEL, IN_COLS),
                        pl.BlockSpec((D_MODEL, tn), lambda i, j, k, block_of=block_of: (0, block_of(j))), dst=dst)
    return dst


def _wgrad_ff_in(name, h2, da):
    t = h2.shape[0]
    tk = min(t, WGRAD_TK)
    return _wgrad_mm(name, (N_CHIPS, 1, t // tk), h2, pl.BlockSpec((tk, D_MODEL), lambda i, j, k: (k, 0)),
                     da, pl.BlockSpec((None, tk, FF_SHARD), lambda i, j, k: (i, k, 0)), (D_MODEL, FF_SHARD),
                     (N_CHIPS, D_MODEL, FF_SHARD), pl.BlockSpec((None, D_MODEL, FF_SHARD), lambda i, j, k: (i, 0, 0)))


def _wgrad_ff_down(ff, dx2b):
    t = dx2b.shape[0]
    tk = min(t, WGRAD_TK)
    return _wgrad_mm("wgrad_ffn_down", (N_CHIPS, 1, t // tk), ff, pl.BlockSpec((None, tk, FF_SHARD), lambda i, j, k: (i, k, 0)),
                     dx2b, pl.BlockSpec((tk, D_MODEL), lambda i, j, k: (k, 0)), (FF_SHARD, D_MODEL),
                     (N_CHIPS, FF_SHARD, D_MODEL), pl.BlockSpec((None, FF_SHARD, D_MODEL), lambda i, j, k: (i, 0, 0)))


def _local_step(x, pos_col, tgt, g1, ln_g, ln_b, w_s, b_s, g2, gf, first_weight, late_weights, on_grads=None):
    tables = _rope_tables(pos_col)
    bias_exp = jnp.repeat(jnp.transpose(b_s), SGU_W // SGU_GROUPS, axis=1)

    hs = _norm_fwd(x, g1)
    w_p = first_weight([hs[0], bias_exp] + [table for pair in tables for table in pair])
    gu, qkvs = _in_proj(hs, w_p, tables)
    os_, ls_ = [], []
    for g, dil in enumerate(DILATIONS):
        o, lse = _attn_fwd(qkvs[g], g, dil)
        os_.append(o)
        ls_.append(lse)
    attn = _combine_fwd(os_, ls_)
    sgu = _sgu_fwd(gu, ln_g, ln_b, w_s, bias_exp)
    w_pa, w_ps, w_out, w_g, w_u, w_d = late_weights(attn)
    pa, ps, merged, x1, h2 = _merge_fwd(attn, sgu, gu, x, w_pa, w_ps, w_out, g2)
    fa, fb, ff = _ffn_fwd(h2, w_g, w_u)
    dx2, dx2b, loss, dgf = _ffn_down_loss(ff, w_d, x1, tgt, gf)

    da, db = _ffn_bwd_act(dx2b, w_d, fa, fb)
    dw_d = _wgrad_ff_down(ff, dx2b)
    dx1, dx1b, dg2 = _ffn_bwd_in(da, db, w_g, w_u, x1, dx2, g2)
    dw_g = _wgrad_ff_in("wgrad_ffn_gate", h2, da)
    dw_u = _wgrad_ff_in("wgrad_ffn_up", h2, db)

    dgu, dpa, dps, dattn, dsgu = _merge_bwd(dx1b, gu, pa, ps, w_pa, w_ps, w_out)
    dw_out = _wgrad_2d("wgrad_out", merged, dx1b, D_MODEL, D_MODEL)
    dw_pa = _wgrad_2d("wgrad_proj_attn", attn, dpa, ATTN_W, D_MODEL)
    dw_ps = _wgrad_2d("wgrad_proj_sgu", sgu, dps, SGU_W, D_MODEL)
    if on_grads is not None:
        ln_g = ln_g + on_grads(1, dict(w_proj_attn=dw_pa, w_proj_sgu=dw_ps, w_out=dw_out, w_ffn_gate=dw_g, w_ffn_up=dw_u,
                                       w_ffn_down=dw_d))[:, :SGU_W]
    dgu, dw_s, dbias, dln_g, dln_b = _sgu_bwd(dgu, gu, dsgu, ln_g, ln_b, w_s, bias_exp)
    dos, ccs = _combine_bwd(dattn, os_, ls_)
    dqkvs = [_attn_bwd(qkvs[g], dos[g], ccs[g], ls_[g], *tables[g], g, dil) for g, dil in enumerate(DILATIONS)]
    dw_p = _wgrad_in(hs, dgu, dqkvs)
    if on_grads is not None:
        g1 = g1 + on_grads(0, dict(w_in=dw_p))
    dx, dg1 = _in_proj_bwd(dgu, dqkvs, w_p, x, dx1, g1)

    db_s = jnp.transpose(dbias[:, ::SGU_W // SGU_GROUPS])
    small = dict(loss=loss, norm1_g=dg1, sgu_ln_g=dln_g, sgu_ln_b=dln_b, w_spatial=dw_s, b_spatial=db_s,
                 norm2_g=dg2, final_g=dgf)
    big = dict(w_in=dw_p, w_proj_attn=dw_pa, w_proj_sgu=dw_ps, w_out=dw_out, w_ffn_gate=dw_g, w_ffn_up=dw_u,
               w_ffn_down=dw_d)
    return dx, big, small


def _ew(name, fn, ins, out_dtypes, after=()):
    shp = ins[0].shape
    rows, cols = shp
    tr = next((cand for cand in (256, 352, 128) if rows % cand == 0 and rows > cand), rows)

    def body(*refs):
        res = fn(*[r[...] for r in refs[:len(ins)]])
        for o_ref, v in zip(refs[len(ins) + len(after):], res):
            o_ref[...] = v.astype(o_ref.dtype)

    spec = pl.BlockSpec((tr, cols), lambda i: (i, 0))
    return pl.pallas_call(
        body, grid=(rows // tr,), in_specs=[spec] * len(ins) + [pl.BlockSpec(memory_space=pl.ANY)] * len(after),
        out_specs=[spec] * len(out_dtypes), out_shape=[jax.ShapeDtypeStruct(shp, d) for d in out_dtypes],
        compiler_params=_cparams(1), name=name)(*ins, *after)


def _adamw_math(g, w, m, v):
    m = ADAM_B1 * m + (1.0 - ADAM_B1) * g
    v = ADAM_B2 * v + (1.0 - ADAM_B2) * (g * g)
    m_hat = m / (1.0 - ADAM_B1 ** ADAM_STEP)
    v_hat = v / (1.0 - ADAM_B2 ** ADAM_STEP)
    delta = -ADAM_LR * (m_hat / (jnp.sqrt(v_hat) + ADAM_EPS) + ADAM_WD * w)
    return delta, m, v


ADAMW_UNIT_ROWS = (128, 176)
ADAMW_CHUNK_ROWS = 8


def _adamw_all(items):
    n = len(items)
    units, sets = [], {}
    for i, arrs in enumerate(items):
        rows, cols = arrs[0].shape
        tr = next(c for c in ADAMW_UNIT_ROWS if rows % c == 0)
        sets.setdefault((tr, cols), len(sets))
        units += [(i, r0, (tr, cols)) for r0 in range(0, rows, tr)]
    nu = len(units)
    slot_of, seen = [], {}
    for _, _, key in units:
        slot_of.append(seen.get(key, 0) % 2)
        seen[key] = seen.get(key, 0) + 1

    def body(*refs):
        ins = [refs[4 * i:4 * i + 4] for i in range(n)]
        outs = [refs[4 * n + 4 * i:4 * n + 4 * i + 4] for i in range(n)]
        bufs = refs[8 * n:8 * n + len(sets)]
        sem = refs[-1]

        def loads(u):
            i, r0, key = units[u]
            return [pltpu.make_async_copy(ins[i][a].at[pl.ds(r0, key[0]), :], bufs[sets[key]].at[slot_of[u], a], sem.at[u, a])
                    for a in range(4)]

        def stores(u):
            i, r0, key = units[u]
            return [pltpu.make_async_copy(bufs[sets[key]].at[slot_of[u], 4 + a], outs[i][a].at[pl.ds(r0, key[0]), :],
                                          sem.at[u, 4 + a]) for a in range(4)]

        for u in range(min(2, nu)):
            for cp in loads(u):
                cp.start()
        leaving = {}
        for u, (i, r0, key) in enumerate(units):
            buf, s = bufs[sets[key]], slot_of[u]
            for cp in loads(u):
                cp.wait()
            for cp in leaving.pop((key, s), []):
                cp.wait()

            def chunk(j, carry, buf=buf, s=s):
                at = pl.ds(pl.multiple_of(j * ADAMW_CHUNK_ROWS, ADAMW_CHUNK_ROWS), ADAMW_CHUNK_ROWS)
                g = buf[s, 0, at, :]
                d_, m_, v_ = _adamw_math(g, buf[s, 1, at, :], buf[s, 2, at, :], buf[s, 3, at, :])
                buf[s, 4, at, :] = g
                buf[s, 5, at, :] = d_
                buf[s, 6, at, :] = m_
                buf[s, 7, at, :] = v_
                return carry

            lax.fori_loop(0, key[0] // ADAMW_CHUNK_ROWS, chunk, 0)
            leaving[(key, s)] = stores(u)
            for cp in leaving[(key, s)]:
                cp.start()
            if u + 2 < nu:
                for cp in loads(u + 2):
                    cp.start()
        for cps in leaving.values():
            for cp in cps:
                cp.wait()

    any_spec = pl.BlockSpec(memory_space=pl.ANY)
    flat = [a for arrs in items for a in arrs]
    outs = pl.pallas_call(
        body, in_specs=[any_spec] * (4 * n), out_specs=[any_spec] * (4 * n),
        out_shape=[jax.ShapeDtypeStruct(a.shape, F32) for a in flat],
        scratch_shapes=[pltpu.VMEM((2, 8) + key, F32) for key in sets] + [pltpu.SemaphoreType.DMA((nu, 8))],
        compiler_params=pltpu.CompilerParams(vmem_limit_bytes=VMEM_LIMIT), name="adamw_big")(*flat)
    return [outs[4 * i:4 * i + 4] for i in range(n)]


VMEM_SPEC = pl.BlockSpec(memory_space=pltpu.VMEM)


def _for_row_chunks(rows, fn):
    ck = next(c for c in (64, 32, 16) if rows % c == 0)

    def step(i, carry):
        fn(pl.multiple_of(i * ck, ck), ck)
        return carry

    lax.fori_loop(0, rows // ck, step, 0)


def _place():
    x, y, c = lax.axis_index("x"), lax.axis_index("y"), lax.axis_index("c")
    chips = [(1 - x, y), (x, 1 - y), (1 - x, 1 - y)]
    return x, y, c, 2 * x + y, chips


def _rows(ref, start, size):
    if len(ref.shape) == 2:
        return ref.at[pl.ds(start, size), :]
    return ref.at[:, pl.ds(start, size), :]


def _gather_finish(name, shard, landed):
    k_rows, n = shard.shape
    kh = k_rows // 2

    def body(shard_hbm, land_hbm, out_ref, shard_ref, land_ref, loc, send, recv):
        x, y, c, me, chips = _place()
        sibling = (x, y, 1 - c)

        def window(core, chip):
            return out_ref.at[pl.ds(core * kh, kh), pl.ds(pl.multiple_of(chip * n, LANES), n)]

        loads = [pltpu.make_async_copy(land_hbm.at[j], land_ref.at[j], loc.at[0, j]) for j in range(3)]
        loads.append(pltpu.make_async_copy(shard_hbm, shard_ref, loc.at[0, 3]))
        for cp in loads:
            cp.start()
        copies, passed = [], []
        for j, chip in enumerate(chips):
            mine = window(c, 2 * chip[0] + chip[1])
            loads[j].wait()
            copies.append(pltpu.make_async_copy(land_ref.at[j], mine, loc.at[1, j]))
            passed.append(pltpu.make_async_remote_copy(src_ref=land_ref.at[j], dst_ref=mine, send_sem=send.at[j],
                                                       recv_sem=recv.at[j], device_id=sibling, device_id_type=MESH))
            copies[-1].start()
            passed[-1].start()
        loads[3].wait()
        copies.append(pltpu.make_async_copy(shard_ref, out_ref.at[:, pl.ds(pl.multiple_of(me * n, LANES), n)], loc.at[1, 3]))
        copies[-1].start()
        for j, chip in enumerate(chips):
            pltpu.make_async_remote_copy(src_ref=land_ref.at[j], dst_ref=window(1 - c, 2 * chip[0] + chip[1]), send_sem=send.at[j],
                                         recv_sem=recv.at[j], device_id=sibling, device_id_type=MESH).wait_recv()
        for cp in copies:
            cp.wait()
        for cp in passed:
            cp.wait_send()

    any_spec = pl.BlockSpec(memory_space=pl.ANY)
    return pl.pallas_call(
        body, in_specs=[any_spec] * 2, out_specs=any_spec,
        out_shape=jax.ShapeDtypeStruct((k_rows, N_CHIPS * n), shard.dtype),
        scratch_shapes=[pltpu.VMEM(shard.shape, shard.dtype), pltpu.VMEM(landed.shape, landed.dtype),
                        pltpu.SemaphoreType.DMA((2, 4)), pltpu.SemaphoreType.DMA((3,)), pltpu.SemaphoreType.DMA((3,))],
        compiler_params=pltpu.CompilerParams(vmem_limit_bytes=VMEM_LIMIT), name=name)(shard, landed)


HBM_SPEC = pl.BlockSpec(memory_space=pltpu.HBM)
SEM_SPEC = pl.BlockSpec(memory_space=pltpu.SEMAPHORE)
DATAFLOW = pltpu.SideEffectType.DATAFLOW_SIDE_EFFECTING
TOKEN_SHAPE = (1, D_MODEL)
N_PEERS = 7
SUM_SPLIT = 4
SUM_SPLIT_ELEMS = 512 * 1024


def _peers():
    x, y, c = lax.axis_index("x"), lax.axis_index("y"), lax.axis_index("c")
    flip = lambda v, f: 1 - v if f else v
    return [(flip(x, k & 4), flip(y, k & 2), flip(c, k & 1)) for k in range(1, N_PEERS + 1)]


def _piece_shape(shape):
    return (shape[-2] // 2, shape[2] if len(shape) == 3 else shape[1] // N_CHIPS)


def _device_piece(ref, chip, core):
    kh, n4 = _piece_shape(ref.shape)
    if len(ref.shape) == 3:
        return ref.at[chip, pl.ds(core * kh, kh), :]
    return ref.at[pl.ds(core * kh, kh), pl.ds(chip * n4, n4)]


def _exchange_copies(partials, lands, send, recv):
    return [pltpu.make_async_remote_copy(
        src_ref=_device_piece(partials[t], 2 * px + py, pc), dst_ref=lands[t].at[k], send_sem=send.at[t * N_PEERS + k],
        recv_sem=recv.at[t * N_PEERS + k], device_id=(px, py, pc), device_id_type=MESH)
        for t in range(len(partials)) for k, (px, py, pc) in enumerate(_peers())]


def _broadcast_copies(srcs, lands, send, recv):
    return [pltpu.make_async_remote_copy(
        src_ref=srcs[t], dst_ref=lands[t].at[k], send_sem=send.at[t * N_PEERS + k], recv_sem=recv.at[t * N_PEERS + k],
        device_id=peer, device_id_type=MESH)
        for t in range(len(srcs)) for k, peer in enumerate(_peers())]


class _LocalCopy:
    def __init__(self, src_ref, dst_ref, sem):
        self.copy = pltpu.make_async_copy(src_ref, dst_ref, sem)

    def start(self):
        self.copy.start()

    def wait_send(self):
        self.copy.wait()

    def wait_recv(self):
        pass


def _gather_copies(shards, lands, send, recv):
    x, y, c, me, chips = _place()
    copies = []
    for t in range(len(shards)):
        n = shards[t].shape[1]
        place = lands[t].at[me] if len(lands[t].shape) == 3 else lands[t].at[:, pl.ds(pl.multiple_of(me * n, LANES), n)]
        copies += [pltpu.make_async_remote_copy(
            src_ref=shards[t], dst_ref=place, send_sem=send.at[t * 4 + j], recv_sem=recv.at[t * 4 + j],
            device_id=(*chip, c), device_id_type=MESH) for j, chip in enumerate(chips)]
        copies.append(_LocalCopy(shards[t], place, send.at[t * 4 + 3]))
    return copies


def _gather_half_copies(shards, lands, send, recv):
    x, y, c, me, chips = _place()
    return [pltpu.make_async_remote_copy(
        src_ref=_rows(shards[t], c * (shards[t].shape[0] // 2), shards[t].shape[0] // 2), dst_ref=lands[t].at[j],
        send_sem=send.at[t * 3 + j], recv_sem=recv.at[t * 3 + j], device_id=(*chip, c), device_id_type=MESH)
        for t in range(len(shards)) for j, chip in enumerate(chips)]


def _split_start(name, copies, per_tensor, srcs, land_shapes):
    nt = len(srcs)
    lands = [lax.empty(s, a.dtype) for s, a in zip(land_shapes, srcs)]
    nsem = nt * per_tensor

    def body(*refs):
        send, recv = refs[2 * nt], refs[2 * nt + 1]
        for cp in copies(refs[:nt], refs[nt:2 * nt], send, recv):
            cp.start()
        refs[-1][...] = jnp.zeros(TOKEN_SHAPE, F32)

    hbm = lambda a: pltpu.with_memory_space_constraint(a, pltpu.HBM)
    outs = pl.pallas_call(
        body, name=name,
        out_shape=[pltpu.SemaphoreType.DMA((nsem,)), pltpu.SemaphoreType.DMA((nsem,))]
        + [pltpu.HBM(s.shape, s.dtype) for s in srcs] + [pltpu.HBM(l.shape, l.dtype) for l in lands]
        + [jax.ShapeDtypeStruct(TOKEN_SHAPE, F32)],
        in_specs=[HBM_SPEC] * (2 * nt), out_specs=[SEM_SPEC, SEM_SPEC] + [HBM_SPEC] * (2 * nt) + [VMEM_SPEC],
        input_output_aliases={i: 2 + i for i in range(2 * nt)},
        compiler_params=pltpu.CompilerParams(has_side_effects=DATAFLOW))(*[hbm(a) for a in list(srcs) + lands])
    return outs[0], outs[1], outs[2:2 + nt], outs[2 + nt:2 + 2 * nt], outs[-1]


def _split_wait(name, copies, send, recv, srcs, lands, after):
    nt = len(srcs)
    after = list(after) if isinstance(after, (list, tuple)) else [after]

    def body(*refs):
        for cp in copies(refs[:nt], refs[nt:2 * nt], refs[2 * nt], refs[2 * nt + 1]):
            cp.wait_send()
            cp.wait_recv()

    outs = pl.pallas_call(
        body, name=name,
        out_shape=[pltpu.HBM(s.shape, s.dtype) for s in srcs] + [pltpu.HBM(l.shape, l.dtype) for l in lands],
        in_specs=[HBM_SPEC] * (2 * nt) + [SEM_SPEC, SEM_SPEC] + [pl.BlockSpec(memory_space=pl.ANY)] * len(after),
        out_specs=[HBM_SPEC] * (2 * nt), input_output_aliases={i: i for i in range(2 * nt)},
        compiler_params=pltpu.CompilerParams(has_side_effects=DATAFLOW))(*srcs, *lands, send, recv, *after)
    return outs[:nt], outs[nt:]


def _device_sum(name, partials, lands):
    nt = len(partials)
    pieces = [_piece_shape(p.shape) for p in partials]
    units = []
    for t, (kh, n4) in enumerate(pieces):
        split = SUM_SPLIT if kh * n4 >= SUM_SPLIT_ELEMS else 1
        units += [(t, j * (kh // split), kh // split) for j in range(split)]
    nu = len(units)

    def body(*refs):
        ins, slots, outs = refs[:nt], refs[nt:2 * nt], refs[2 * nt:3 * nt]
        owns, landed, sums = refs[3 * nt:4 * nt], refs[4 * nt:5 * nt], refs[5 * nt:6 * nt]
        loc, send, recv = refs[6 * nt:]
        x, y, c, me, chips = _place()
        sibling = (x, y, 1 - c)
        loads = []
        for u, (t, r0, rows) in enumerate(units):
            loads.append((
                pltpu.make_async_copy(_rows(_device_piece(ins[t], me, c), r0, rows), _rows(owns[t], r0, rows), loc.at[0, u]),
                pltpu.make_async_copy(_rows(slots[t], r0, rows), _rows(landed[t], r0, rows), loc.at[1, u])))
            for cp in loads[-1]:
                cp.start()
        stores = []
        for u, (t, r0, rows) in enumerate(units):
            for cp in loads[u]:
                cp.wait()

            def add(q0, ck, own=owns[t], slot=landed[t], dst=sums[t], r0=r0):
                at = pl.ds(pl.multiple_of(r0 + q0, ck), ck)
                acc = own[at, :].astype(F32)
                for k in range(N_PEERS):
                    acc = acc + slot[k, at, :].astype(F32)
                dst[at, :] = acc

            _for_row_chunks(rows, add)
            mine = _rows(outs[t], c * pieces[t][0] + r0, rows)
            stores.append((
                pltpu.make_async_copy(_rows(sums[t], r0, rows), mine, loc.at[2, u]),
                pltpu.make_async_remote_copy(src_ref=_rows(sums[t], r0, rows), dst_ref=mine, send_sem=send.at[u],
                                             recv_sem=recv.at[u], device_id=sibling, device_id_type=MESH)))
            for cp in stores[-1]:
                cp.start()
        for u, (t, r0, rows) in enumerate(units):
            pltpu.make_async_remote_copy(
                src_ref=_rows(sums[t], r0, rows), dst_ref=_rows(outs[t], (1 - c) * pieces[t][0] + r0, rows),
                send_sem=send.at[u], recv_sem=recv.at[u], device_id=sibling, device_id_type=MESH).wait_recv()
            stores[u][0].wait()
            stores[u][1].wait_send()

    any_spec = pl.BlockSpec(memory_space=pl.ANY)
    return pl.pallas_call(
        body, in_specs=[any_spec] * (2 * nt), out_specs=[any_spec] * nt,
        out_shape=[jax.ShapeDtypeStruct((2 * kh, n4), F32) for kh, n4 in pieces],
        scratch_shapes=[pltpu.VMEM(p, BF16) for p in pieces] + [pltpu.VMEM((N_PEERS,) + p, BF16) for p in pieces]
        + [pltpu.VMEM(p, F32) for p in pieces]
        + [pltpu.SemaphoreType.DMA((3, nu)), pltpu.SemaphoreType.DMA((nu,)), pltpu.SemaphoreType.DMA((nu,))],
        compiler_params=pltpu.CompilerParams(vmem_limit_bytes=VMEM_LIMIT), name=name)(*partials, *lands)


VEC_SHAPE = (8, D_MODEL + LANES)
VEC_SLOTS = dict(norm1_g=(slice(0, 1), slice(0, D_MODEL)), norm2_g=(slice(1, 2), slice(0, D_MODEL)),
                 final_g=(slice(2, 3), slice(0, D_MODEL)), sgu_ln_g=(slice(3, 4), slice(0, SGU_W)),
                 sgu_ln_b=(slice(3, 4), slice(SGU_W, 2 * SGU_W)), b_spatial=(slice(0, 8), slice(D_MODEL, D_MODEL + LANES)),
                 loss=(slice(4, 5), slice(0, LANES)))
VEC_PARAMS = ("norm1_g", "norm2_g", "final_g", "sgu_ln_g", "sgu_ln_b", "b_spatial")
SMALL_PARAMS = VEC_PARAMS + ("w_spatial",)
W_SPATIAL_2D = (SGU_GROUPS * SGU_CHUNK, SGU_CHUNK)


SMALL_GRADS = VEC_PARAMS + ("loss", "w_spatial")


def _small_shape(name):
    if name == "w_spatial":
        return W_SPATIAL_2D
    rows, cols = VEC_SLOTS[name]
    return (rows.stop - rows.start, cols.stop - cols.start)


def _pack_small(dst, parts):
    dst[...] = jnp.zeros(VEC_SHAPE, F32)
    for n, ref in parts.items():
        if n in VEC_SLOTS:
            dst[VEC_SLOTS[n]] = ref[...]


def _small_start(partials):
    names = VEC_PARAMS + ("loss",)

    def body(*refs):
        _pack_small(refs[-1], dict(zip(names, refs[:-1])))

    vec = pl.pallas_call(
        body, in_specs=[VMEM_SPEC] * len(names), out_specs=VMEM_SPEC, out_shape=jax.ShapeDtypeStruct(VEC_SHAPE, F32),
        name="small_params_pack")(*[partials[n].reshape(_small_shape(n)) for n in names])
    srcs = [vec, partials["w_spatial"].reshape(W_SPATIAL_2D)]
    return _split_start("small_params_start", _broadcast_copies, N_PEERS, srcs, [(N_PEERS,) + s.shape for s in srcs])


def _small_finish(started, after, w, m, v):
    own, landed = _split_wait("small_params_wait", _broadcast_copies, *started, after)
    ng, npar = len(SMALL_GRADS), len(SMALL_PARAMS)

    def update_body(*refs):
        vec_own, ws_own, vec_slots, ws_slots = refs[:4]
        w_in, m_in, v_in = (dict(zip(SMALL_PARAMS, refs[4 + k * npar:4 + (k + 1) * npar])) for k in range(3))
        o0 = 4 + 3 * npar
        g_out = dict(zip(SMALL_GRADS, refs[o0:o0 + ng]))
        d_out, m_out, v_out = (dict(zip(SMALL_PARAMS, refs[o0 + ng + k * npar:o0 + ng + (k + 1) * npar])) for k in range(3))
        vg, vw, vm, vv = refs[o0 + ng + 3 * npar:]
        me = 4 * lax.axis_index("x") + 2 * lax.axis_index("y") + lax.axis_index("c")

        def device_sum(mine, slots, read):
            acc = None
            for i in range(N_PEERS + 1):
                k = me ^ i
                part = jnp.where(k == 0, read(mine), read(slots.at[jnp.maximum(k, 1) - 1]))
                acc = part if acc is None else acc + part
            return acc

        vg[...] = device_sum(vec_own, vec_slots, lambda ref: ref[...])
        _pack_small(vw, w_in)
        _pack_small(vm, m_in)
        _pack_small(vv, v_in)
        d_vec, m_vec, v_vec = _adamw_math(vg[...], vw[...], vm[...], vv[...])
        vw[...] = d_vec
        vm[...] = m_vec
        vv[...] = v_vec
        for n in VEC_PARAMS + ("loss",):
            g_out[n][...] = vg[VEC_SLOTS[n]]
        for n in VEC_PARAMS:
            d_out[n][...] = vw[VEC_SLOTS[n]]
            m_out[n][...] = vm[VEC_SLOTS[n]]
            v_out[n][...] = vv[VEC_SLOTS[n]]

        def spatial(r0, ck):
            rows = pl.ds(r0, ck)
            g = device_sum(ws_own, ws_slots, lambda ref: ref[rows, :])
            d_, m_, v_ = _adamw_math(g, w_in["w_spatial"][rows, :], m_in["w_spatial"][rows, :], v_in["w_spatial"][rows, :])
            g_out["w_spatial"][rows, :] = g
            d_out["w_spatial"][rows, :] = d_
            m_out["w_spatial"][rows, :] = m_
            v_out["w_spatial"][rows, :] = v_

        _for_row_chunks(W_SPATIAL_2D[0], spatial)

    ins = list(own) + list(landed)
    for src in (w, m, v):
        ins += [src[n].reshape(_small_shape(n)) for n in SMALL_PARAMS]
    out_shapes = [jax.ShapeDtypeStruct(_small_shape(n), F32) for n in SMALL_GRADS + SMALL_PARAMS * 3]
    outs = pl.pallas_call(
        update_body, in_specs=[VMEM_SPEC] * len(ins), out_specs=[VMEM_SPEC] * len(out_shapes), out_shape=out_shapes,
        scratch_shapes=[pltpu.VMEM(VEC_SHAPE, F32)] * 4, name="small_params_update")(*ins)
    grads = dict(zip(SMALL_GRADS, outs[:ng]))
    rest = [dict(zip(SMALL_PARAMS, outs[ng + k * npar:ng + (k + 1) * npar])) for k in range(3)]
    return grads, rest[0], rest[1], rest[2]


BIG = ("w_in", "w_proj_attn", "w_proj_sgu", "w_out", "w_ffn_gate", "w_ffn_up", "w_ffn_down")
COMM_GROUPS = (("w_in",), ("w_proj_attn", "w_proj_sgu", "w_out", "w_ffn_gate", "w_ffn_up", "w_ffn_down"))
WEIGHTS = ("norm1_g", "w_in", "sgu_ln_g", "sgu_ln_b", "w_spatial", "b_spatial", "w_proj_attn", "w_proj_sgu", "w_out",
           "norm2_g", "w_ffn_gate", "w_ffn_up", "w_ffn_down", "final_g")


def kernel(x, positions, norm1_g, w_in, sgu_ln_g, sgu_ln_b, w_spatial, b_spatial, w_proj_attn, w_proj_sgu, w_out, norm2_g, w_ffn_gate, w_ffn_up, w_ffn_down, final_g, loss_target, m_norm1_g, m_w_in, m_sgu_ln_g, m_sgu_ln_b, m_w_spatial, m_b_spatial, m_w_proj_attn, m_w_proj_sgu, m_w_out, m_norm2_g, m_w_ffn_gate, m_w_ffn_up, m_w_ffn_down, m_final_g, v_norm1_g, v_w_in, v_sgu_ln_g, v_sgu_ln_b, v_w_spatial, v_b_spatial, v_w_proj_attn, v_w_proj_sgu, v_w_out, v_norm2_g, v_w_ffn_gate, v_w_ffn_up, v_w_ffn_down, v_final_g):
    w = dict(norm1_g=norm1_g, w_in=w_in, sgu_ln_g=sgu_ln_g, sgu_ln_b=sgu_ln_b, w_spatial=w_spatial, b_spatial=b_spatial,
             w_proj_attn=w_proj_attn, w_proj_sgu=w_proj_sgu, w_out=w_out, norm2_g=norm2_g, w_ffn_gate=w_ffn_gate,
             w_ffn_up=w_ffn_up, w_ffn_down=w_ffn_down, final_g=final_g)
    m = dict(norm1_g=m_norm1_g, w_in=m_w_in, sgu_ln_g=m_sgu_ln_g, sgu_ln_b=m_sgu_ln_b, w_spatial=m_w_spatial,
             b_spatial=m_b_spatial, w_proj_attn=m_w_proj_attn, w_proj_sgu=m_w_proj_sgu, w_out=m_w_out, norm2_g=m_norm2_g,
             w_ffn_gate=m_w_ffn_gate, w_ffn_up=m_w_ffn_up, w_ffn_down=m_w_ffn_down, final_g=m_final_g)
    v = dict(norm1_g=v_norm1_g, w_in=v_w_in, sgu_ln_g=v_sgu_ln_g, sgu_ln_b=v_sgu_ln_b, w_spatial=v_w_spatial,
             b_spatial=v_b_spatial, w_proj_attn=v_w_proj_attn, w_proj_sgu=v_w_proj_sgu, w_out=v_w_out, norm2_g=v_norm2_g,
             w_ffn_gate=v_w_ffn_gate, w_ffn_up=v_w_ffn_up, w_ffn_down=v_w_ffn_down, final_g=v_final_g)
    t = x.shape[1]

    def cast(n, after):
        flip = jnp.transpose if w[n].shape[-1] % LANES else (lambda a: a)
        return flip(_ew(f"cast_{n}", lambda a: (a,), [flip(w[n][0])], [BF16], after)[0])

    shards = {"w_in": cast("w_in", [])}
    late = COMM_GROUPS[1]
    k_in, n_in = shards["w_in"].shape
    *first, token = _split_start("gather_start_0", _gather_half_copies, 3, [shards["w_in"]], [(3, k_in // 2, n_in)])
    shards.update({n: cast(n, [token]) for n in late})
    pending = {}

    def first_weight(after):
        srcs, filled = _split_wait("gather_wait_0", _gather_half_copies, *first, list(after) + [shards[n] for n in late])
        gath_in, late_shards = lax.optimization_barrier(
            (_gather_finish("gather_finish_0", srcs[0], filled[0]), [shards[n] for n in late]))
        land_shapes = [(s.shape[0], N_CHIPS * s.shape[1]) if n.startswith("w_proj") else (N_CHIPS,) + s.shape
                       for n, s in zip(late, late_shards)]
        *pending["late"], _ = _split_start("gather_start_1", _gather_copies, 4, late_shards, land_shapes)
        return gath_in

    def late_weights(after):
        _, filled = _split_wait("gather_wait_1", _gather_copies, *pending["late"], after)
        gath = dict(zip(late, filled))
        return (gath["w_proj_attn"], gath["w_proj_sgu"],
                gath["w_out"].reshape(D_MODEL, D_MODEL), gath["w_ffn_gate"], gath["w_ffn_up"], gath["w_ffn_down"])

    exchanges = {}

    def on_grads(i, partials):
        if "w_out" in partials:
            partials["w_out"] = partials["w_out"].reshape(N_CHIPS, D_MODEL // N_CHIPS, D_MODEL)
        parts = [partials[n] for n in COMM_GROUPS[i]]
        *exchanges[i], started = _split_start(
            f"rs_exchange_start_{i}", _exchange_copies, N_PEERS, parts, [(N_PEERS,) + _piece_shape(p.shape) for p in parts])
        return started

    dx, _, small = _local_step(
        x[0], positions.reshape(t, 1), loss_target[0], norm1_g + token, sgu_ln_g, sgu_ln_b,
        w_spatial[0], b_spatial[0] + token[:1, :LANES],
        norm2_g, final_g.reshape(1, D_MODEL), first_weight, late_weights, on_grads=on_grads)
    *small_started, small_token = _small_start(small)

    grads = {}
    for i in (1, 0):
        parts, filled = _split_wait(f"rs_exchange_wait_{i}", _exchange_copies, *exchanges[i], small_token)
        grads.update(zip(COMM_GROUPS[i], _device_sum(f"rs_device_sum_{i}", parts, filled)))

    delta, new_m, new_v, updated = {}, {}, {}, []
    flips = {n: jnp.transpose if w[n].shape[-1] % LANES else (lambda a: a) for n in BIG}
    results = _adamw_all([tuple(flips[n](a) for a in (grads[n], w[n][0], m[n][0], v[n][0])) for n in BIG])
    for n, outs in zip(BIG, results):
        grads[n], delta[n], new_m[n], new_v[n] = (flips[n](a).reshape(w[n].shape) for a in outs)
        updated.append(outs[-1])

    g_s, d_s, m_s, v_s = _small_finish(small_started, updated, w, m, v)
    loss = g_s["loss"][0, 0]
    for n in SMALL_PARAMS:
        shp = w[n].shape
        grads[n], delta[n], new_m[n], new_v[n] = (a[n].reshape(shp) for a in (g_s, d_s, m_s, v_s))

    return (loss, dx.reshape(x.shape), *[grads[n] for n in WEIGHTS], *[delta[n] for n in WEIGHTS],
            *[new_m[n] for n in WEIGHTS], *[new_v[n] for n in WEIGHTS])
```

```python
import functools

import numpy as np
import jax
import jax.numpy as jnp
from jax import lax
from jax.experimental import pallas as pl
from jax.experimental.pallas import tpu as pltpu

F32, BF16 = jnp.float32, jnp.bfloat16
MESH = pl.DeviceIdType.MESH

D_MODEL = 1024
HEAD_DIM = 64
ATTN_W = 512
DILATIONS = (1, 4, 16)
BLK = 128
ATTN_BLOCKS_PER_STEP = 4
ROPE_DIM = 16
ROPE_THETA = 500000.0
SGU_W = 512
SGU_CHUNK = 128
SGU_GROUPS = 8
D_FF = 2816
N_CHIPS = 4
FF_SHARD = D_FF // N_CHIPS
IN_COLS = 7680
EPS = 1e-6
NEG = -1e30
LANES = 128
VMEM_LIMIT = 52 * 1024 * 1024

ADAM_LR, ADAM_B1, ADAM_B2, ADAM_EPS, ADAM_WD, ADAM_STEP = 0.001, 0.9, 0.999, 1e-08, 0.01, 10

QKV_BLOCKS = 9


def _w_in_block(part, g):
    return part * len(DILATIONS) + g


def _cparams(ngrid):
    return pltpu.CompilerParams(dimension_semantics=("arbitrary",) * ngrid, vmem_limit_bytes=VMEM_LIMIT)


def _full(shape):
    return pl.BlockSpec(shape, lambda *_: (0,) * len(shape))


def _resident(shape):
    return pl.BlockSpec(shape, lambda *_: (0,) * len(shape), pipeline_mode=pl.Buffered(1))


NT = ((1,), (1,))
TN = ((0,), (0,))


def _rope(v, cos_t, sin_t):
    half = ROPE_DIM // 2
    first = (lax.broadcasted_iota(jnp.int32, cos_t.shape, 1) % HEAD_DIM) < half
    outs = []
    for cs in range(v.shape[1] // LANES):
        x = v[:, cs * LANES:(cs + 1) * LANES]
        partner = jnp.where(first, pltpu.roll(x, LANES - half, axis=1), pltpu.roll(x, half, axis=1))
        outs.append(x * cos_t + partner * sin_t)
    return outs[0] if len(outs) == 1 else jnp.concatenate(outs, axis=1)


def _spread_heads(v2, upper):
    other = pltpu.roll(v2, HEAD_DIM, axis=1)
    h0 = jnp.where(upper, other, v2)
    h1 = jnp.where(upper, v2, other)
    return jnp.concatenate([jnp.concatenate([h0, h0], axis=1), jnp.concatenate([h1, h1], axis=1)], axis=0)


def _sigmoid(v):
    return 0.5 * jnp.tanh(0.5 * v) + 0.5


def _rms_stats(v):
    r = lax.rsqrt(jnp.mean(v * v, axis=-1, keepdims=True) + EPS)
    return v * r, r


def _rms_bwd(dy, xhat, r, g):
    dxh = dy * g
    return r * (dxh - xhat * jnp.mean(dxh * xhat, axis=-1, keepdims=True))


def _head_sum_matrix():
    idx = np.arange(ATTN_W) // HEAD_DIM
    return jnp.asarray((idx[:, None] == idx[None, :]).astype(np.float32), dtype=BF16)


def _group_sum(v, e):
    hi = v.astype(BF16)
    lo = (v - hi.astype(F32)).astype(BF16)
    return jnp.dot(hi, e, preferred_element_type=F32) + jnp.dot(lo, e, preferred_element_type=F32)


TILE = 512


def _to_slabs(slab_ref, v):
    for cs in range(slab_ref.shape[0]):
        slab_ref[cs] = v[:, cs * LANES:(cs + 1) * LANES]


def _from_slabs(slab_ref):
    return jnp.concatenate([slab_ref[cs] for cs in range(slab_ref.shape[0])], axis=1)


def _class_rows(slab_ref, r, dil):
    n = slab_ref.shape[1] // dil
    return jnp.concatenate([slab_ref.at[cs][pl.ds(r, n, stride=dil), :] for cs in range(slab_ref.shape[0])], axis=1)


def _put_class_rows(slab_ref, r, dil, v):
    n = slab_ref.shape[1] // dil
    for cs in range(slab_ref.shape[0]):
        slab_ref.at[cs][pl.ds(r, n, stride=dil), :] = v[:, cs * LANES:(cs + 1) * LANES]


def _natural_from_group(slab_ref, grp_ref):
    dil = grp_ref.shape[0]
    for r in range(dil):
        _put_class_rows(slab_ref, r, dil, grp_ref[r].astype(F32))
    return _from_slabs(slab_ref)


def _group_from_natural(slab_ref, grp_ref, v):
    dil = grp_ref.shape[0]
    _to_slabs(slab_ref, v)
    for r in range(dil):
        grp_ref[r] = _class_rows(slab_ref, r, dil).astype(grp_ref.dtype)


def _group_spec(dil, tile, width):
    return pl.BlockSpec((dil, tile // dil, width), lambda i, *_: (0, i, 0))


def _slabs(tile, width):
    return pltpu.VMEM((width // LANES, tile, LANES), F32)


def _rope_consts():
    lane = np.arange(LANES) % HEAD_DIM
    fi = lane % (ROPE_DIM // 2)
    invf = np.where(lane < ROPE_DIM, ROPE_THETA ** (-(2.0 * fi) / ROPE_DIM), 0.0)
    sgn = np.where(lane < ROPE_DIM // 2, -1.0, np.where(lane < ROPE_DIM, 1.0, 0.0))
    return (jnp.asarray(invf.astype(np.float32)).reshape(1, LANES), jnp.asarray(sgn.astype(np.float32)).reshape(1, LANES))


def _rope_tables(pos_col):
    t = pos_col.shape[0]
    tile = min(t, TILE)
    invf, sgn = _rope_consts()

    def body(p_ref, f_ref, s_ref, c0, s0, c1, s1, c2, s2, slab_c, slab_s):
        ang = p_ref[...].astype(F32) * f_ref[...]
        cos, sin = jnp.cos(ang), jnp.sin(ang) * s_ref[...]
        c0[...] = cos
        s0[...] = sin
        _group_from_natural(slab_c, c1, cos)
        _group_from_natural(slab_s, s1, sin)
        for r in range(DILATIONS[2]):
            c2[r] = _class_rows(slab_c, r, DILATIONS[2])
            s2[r] = _class_rows(slab_s, r, DILATIONS[2])

    nat = pl.BlockSpec((tile, LANES), lambda i: (i, 0))
    specs, shapes = [nat, nat], [(t, LANES)] * 2
    for d in DILATIONS[1:]:
        specs += [_group_spec(d, tile, LANES)] * 2
        shapes += [(d, t // d, LANES)] * 2
    outs = pl.pallas_call(
        body, grid=(t // tile,),
        in_specs=[pl.BlockSpec((tile, 1), lambda i: (i, 0)), _full((1, LANES)), _full((1, LANES))],
        out_specs=specs, out_shape=[jax.ShapeDtypeStruct(s, F32) for s in shapes],
        scratch_shapes=[_slabs(tile, LANES)] * 2,
        compiler_params=_cparams(1), name="rope_tables")(pos_col, invf, sgn)
    return [(outs[2 * g].reshape(t, LANES), outs[2 * g + 1].reshape(t, LANES)) for g in range(len(DILATIONS))]


def _norm_fwd(x, g):
    t = x.shape[0]
    tile = min(t, TILE)

    def body(x_ref, g_ref, h0_ref, h1_ref, h2_ref, slab):
        xhat, _ = _rms_stats(x_ref[...])
        hn = xhat * g_ref[...]
        h0_ref[...] = hn.astype(BF16)
        _group_from_natural(slab, h1_ref, hn)
        for r in range(DILATIONS[2]):
            h2_ref[r] = _class_rows(slab, r, DILATIONS[2]).astype(BF16)

    nat = pl.BlockSpec((tile, D_MODEL), lambda i: (i, 0))
    return pl.pallas_call(
        body, grid=(t // tile,),
        in_specs=[nat, _full((1, D_MODEL))],
        out_specs=[nat] + [_group_spec(d, tile, D_MODEL) for d in DILATIONS[1:]],
        out_shape=[jax.ShapeDtypeStruct((t, D_MODEL), BF16)]
        + [jax.ShapeDtypeStruct((d, t // d, D_MODEL), BF16) for d in DILATIONS[1:]],
        scratch_shapes=[_slabs(tile, D_MODEL)],
        compiler_params=_cparams(1), name="norm1_fwd")(x, g)


GU_COLS = 3072
GROUP_COLS = 1536
GU_HALF = GU_COLS // 2


def _w_in_spec(width, block):
    return pl.BlockSpec((D_MODEL, width), lambda i: (0, block), pipeline_mode=pl.Buffered(1))


def _gu_w_specs():
    first = QKV_BLOCKS * ATTN_W // GU_HALF
    return [_w_in_spec(GU_HALF, first), _w_in_spec(GU_HALF, first + 1)]


def _group_w_specs(g):
    return [_w_in_spec(ATTN_W, _w_in_block(part, g)) for part in range(3)]


def _in_proj(hs, w_in, tables):
    t = hs[0].shape[0]
    tm = min(t, 1024)

    def body_gu(h_ref, w0_ref, w1_ref, o_ref):
        h = h_ref[...]
        o_ref[:, 0:GU_HALF] = jnp.dot(h, w0_ref[...], preferred_element_type=F32).astype(BF16)
        o_ref[:, GU_HALF:] = jnp.dot(h, w1_ref[...], preferred_element_type=F32).astype(BF16)

    gu = _token_call("in_proj_gates_uv", body_gu, t, tm,
                     [(hs[0], _rows_spec(tm, D_MODEL))] + [(w_in, s) for s in _gu_w_specs()],
                     [((t, GU_COLS), BF16, _rows_spec(tm, GU_COLS))])[0]

    qkvs = []
    for g in range(len(DILATIONS)):

        def body_qkv(h_ref, wq_ref, wk_ref, wv_ref, cos_ref, sin_ref, o_ref):
            h = h_ref[...]
            cos_w, sin_w = cos_ref[...], sin_ref[...]
            q = jnp.dot(h, wq_ref[...], preferred_element_type=F32)
            o_ref[:, 0:ATTN_W] = (_rope(q, cos_w, sin_w) * HEAD_DIM ** -0.5).astype(BF16)
            k = jnp.dot(h, wk_ref[...], preferred_element_type=F32)
            o_ref[:, ATTN_W:2 * ATTN_W] = _rope(k, cos_w, sin_w).astype(BF16)
            o_ref[:, 2 * ATTN_W:] = jnp.dot(h, wv_ref[...], preferred_element_type=F32).astype(BF16)

        cos_t, sin_t = tables[g]
        qkvs.append(_token_call(
            f"in_proj_qkv_g{g}", body_qkv, t, tm,
            [(hs[g].reshape(t, D_MODEL), _rows_spec(tm, D_MODEL))] + [(w_in, s) for s in _group_w_specs(g)]
            + [(cos_t, _rows_spec(tm, LANES)), (sin_t, _rows_spec(tm, LANES))],
            [((t, GROUP_COLS), BF16, _rows_spec(tm, GROUP_COLS))])[0])
    return gu, qkvs


def _attn_masks(n):
    row = lax.broadcasted_iota(jnp.int32, (2 * BLK, 2 * BLK), 0) % BLK
    col = lax.broadcasted_iota(jnp.int32, (2 * BLK, 2 * BLK), 1)
    diff = BLK + row - col
    valid = (diff >= 0) & (diff <= BLK) & ((col >= BLK) | (n > 0))
    upper = lax.broadcasted_iota(jnp.int32, (BLK, LANES), 1) >= HEAD_DIM
    return valid, upper


def _stack_heads(v2, upper):
    zero = jnp.zeros_like(v2)
    return jnp.concatenate([jnp.where(upper, zero, v2), jnp.where(upper, v2, zero)], axis=0)


def _unstack_heads(v, upper):
    return jnp.where(upper, v[BLK:], v[:BLK])


def _attn_fwd(qkv, g, dil):
    t = qkv.shape[0]
    length = t // dil
    nb = length // BLK
    per_step = min(nb, ATTN_BLOCKS_PER_STEP)
    view = qkv.reshape(dil, length, GROUP_COLS)

    def body(q_ref, kc_ref, kp_ref, vc_ref, vp_ref, o_ref, l_ref, kwin, vwin):
        n = pl.program_id(1)
        kwin[0:BLK] = kp_ref[...]
        kwin[BLK:] = kc_ref[...]
        vwin[0:BLK] = vp_ref[...]
        vwin[BLK:] = vc_ref[...]

        def block(b, carry):
            valid, upper = _attn_masks(n * per_step + b)
            rows = pl.ds(pl.multiple_of(b * BLK, BLK), BLK)
            window = pl.ds(pl.multiple_of(b * BLK, BLK), 2 * BLK)
            slabs = [slice(p * LANES, (p + 1) * LANES) for p in range(ATTN_W // LANES)]
            ss = [lax.dot_general(_stack_heads(q_ref[rows, sl], upper), kwin[window, sl], (NT, ((), ())),
                                  preferred_element_type=F32) for sl in slabs]
            soft = []
            for s in ss:
                s = jnp.where(valid, s, NEG)
                m = jnp.max(s, axis=1, keepdims=True)
                pe = jnp.exp(s - m)
                soft.append((m, pe, jnp.sum(pe, axis=1, keepdims=True)))
            for sl, (m, pe, den) in zip(slabs, soft):
                o = jnp.dot(pe.astype(BF16), vwin[window, sl], preferred_element_type=F32) / den
                lse = jnp.broadcast_to(m + jnp.log(den), (2 * BLK, LANES))
                o_ref[rows, sl] = _unstack_heads(o, upper).astype(BF16)
                l_ref[rows, sl] = _unstack_heads(lse, upper)
            return carry

        lax.fori_loop(0, per_step, block, 0)

    rows = per_step * BLK
    cur = lambda part: pl.BlockSpec((None, rows, ATTN_W), lambda r, n: (r, n, part))
    prev = lambda part: pl.BlockSpec((None, BLK, ATTN_W), lambda r, n: (r, jnp.maximum(n * per_step - 1, 0), part))
    out_spec = pl.BlockSpec((None, rows, ATTN_W), lambda r, n: (r, n, 0))
    return pl.pallas_call(
        body, grid=(dil, nb // per_step),
        in_specs=[cur(0), cur(1), prev(1), cur(2), prev(2)],
        out_specs=[out_spec, out_spec],
        out_shape=[jax.ShapeDtypeStruct((dil, length, ATTN_W), BF16), jax.ShapeDtypeStruct((dil, length, ATTN_W), F32)],
        scratch_shapes=[pltpu.VMEM((rows + BLK, ATTN_W), BF16)] * 2,
        compiler_params=_cparams(2), name=f"attn_fwd_g{g}")(view, view, view, view, view)


def _alphas(l0, l1, l2):
    m = jnp.maximum(jnp.maximum(l0, l1), l2)
    e0, e1, e2 = jnp.exp(l0 - m), jnp.exp(l1 - m), jnp.exp(l2 - m)
    inv = 1.0 / (e0 + e1 + e2)
    return e0 * inv, e1 * inv, e2 * inv


def _natural_group_values(o_refs, l_refs, slabs):
    os_ = [o_refs[0][0].astype(F32)] + [_natural_from_group(slabs[2 * g - 2], o_refs[g]) for g in (1, 2)]
    ls_ = [l_refs[0][0]] + [_natural_from_group(slabs[2 * g - 1], l_refs[g]) for g in (1, 2)]
    return os_, ls_


def _combine_fwd(os_, ls_):
    t = os_[0].shape[1]
    tile = min(t, TILE)

    def body(o0, o1, o2, l0, l1, l2, a_ref, *slabs):
        ov, lv = _natural_group_values((o0, o1, o2), (l0, l1, l2), slabs)
        a0, a1, a2 = _alphas(*lv)
        a_ref[...] = (a0 * ov[0] + a1 * ov[1] + a2 * ov[2]).astype(BF16)

    specs = [_group_spec(d, tile, ATTN_W) for d in DILATIONS]
    return pl.pallas_call(
        body, grid=(t // tile,), in_specs=specs * 2, out_specs=pl.BlockSpec((tile, ATTN_W), lambda i: (i, 0)),
        out_shape=jax.ShapeDtypeStruct((t, ATTN_W), BF16),
        scratch_shapes=[_slabs(tile, ATTN_W)] * 4,
        compiler_params=_cparams(1), name="combine_fwd")(*os_, *ls_)


def _combine_bwd(dattn, os_, ls_):
    t = dattn.shape[0]
    tile = min(t, TILE)
    e = _head_sum_matrix()

    def body(d_ref, o0, o1, o2, l0, l1, l2, e_ref, do0, do1, do2, c0, c1, c2, *slabs):
        ov, lv = _natural_group_values((o0, o1, o2), (l0, l1, l2), slabs)
        alphas = _alphas(*lv)
        d = d_ref[...]
        attn = alphas[0] * ov[0] + alphas[1] * ov[1] + alphas[2] * ov[2]
        s = _group_sum(d * attn, e_ref[...])
        do0[0] = (alphas[0] * d).astype(BF16)
        c0[0] = -alphas[0] * s
        for g, do_ref, c_ref in ((1, do1, c1), (2, do2, c2)):
            _group_from_natural(slabs[2 * g - 2], do_ref, alphas[g] * d)
            _group_from_natural(slabs[2 * g - 1], c_ref, -alphas[g] * s)

    specs = [_group_spec(d, tile, ATTN_W) for d in DILATIONS]
    shapes = [(d, t // d, ATTN_W) for d in DILATIONS]
    outs = pl.pallas_call(
        body, grid=(t // tile,),
        in_specs=[pl.BlockSpec((tile, ATTN_W), lambda i: (i, 0))] + specs * 2 + [_full((ATTN_W, ATTN_W))],
        out_specs=specs * 2,
        out_shape=[jax.ShapeDtypeStruct(s, BF16) for s in shapes] + [jax.ShapeDtypeStruct(s, F32) for s in shapes],
        scratch_shapes=[_slabs(tile, ATTN_W)] * 4,
        compiler_params=_cparams(1), name="combine_bwd")(dattn, *os_, *ls_, e)
    return outs[:3], outs[3:]


def _attn_bwd(qkv, do, cc, lse, cos_t, sin_t, g, dil):
    t = qkv.shape[0]
    length = t // dil
    nb = length // BLK
    per_step = min(nb, ATTN_BLOCKS_PER_STEP)
    nsteps = nb // per_step
    rows_per_step = per_step * BLK
    qkv_v = qkv.reshape(dil, length, GROUP_COLS)
    cos_v, sin_v = (a.reshape(dil, length, LANES) for a in (cos_t, sin_t))
    scale = HEAD_DIM ** -0.5
    dq_cols, dk_cols, dv_cols = (slice(i * ATTN_W, (i + 1) * ATTN_W) for i in range(3))

    def body(q_ref, kc_ref, kp_ref, vc_ref, vp_ref, do_ref, c_ref, l_ref, cosc, sinc, cosp, sinp,
             out_ref, acc, kwin, vwin, cwin, swin):
        n = pl.program_id(1)

        def one_block(b):
            valid, upper = _attn_masks(n * per_step + b)
            start = b * BLK if isinstance(b, int) else pl.multiple_of(b * BLK, BLK)
            rows, before, window = pl.ds(start, BLK), pl.ds(start, BLK), pl.ds(start, 2 * BLK)
            own = pl.ds(start + BLK, BLK)
            dq_parts, dkp_parts, dkc_parts, dvp_parts, dvc_parts = [], [], [], [], []
            npairs = ATTN_W // LANES
            slabs = [slice(p * LANES, (p + 1) * LANES) for p in range(npairs)]
            qss = [_stack_heads(q_ref[rows, sl], upper) for sl in slabs]
            doss = [_stack_heads(do_ref[rows, sl], upper) for sl in slabs]
            ss = [lax.dot_general(qss[p], kwin[window, slabs[p]], (NT, ((), ())), preferred_element_type=F32) for p in range(npairs)]
            dpvs = [lax.dot_general(doss[p], vwin[window, slabs[p]], (NT, ((), ())), preferred_element_type=F32)
                    for p in range(npairs)]
            pes = [jnp.exp(jnp.where(valid, ss[p], NEG) - _spread_heads(l_ref[rows, slabs[p]], upper)) for p in range(npairs)]
            dss = [(pes[p] * (dpvs[p] + _spread_heads(c_ref[rows, slabs[p]], upper))).astype(BF16) for p in range(npairs)]
            for p in range(npairs):
                qs, dos, ds = qss[p], doss[p], dss[p]
                dq2 = _unstack_heads(jnp.dot(ds, kwin[window, slabs[p]], preferred_element_type=F32), upper)
                dk2 = lax.dot_general(ds, qs, (TN, ((), ())), preferred_element_type=F32)
                dv2 = lax.dot_general(pes[p].astype(BF16), dos, (TN, ((), ())), preferred_element_type=F32)
                dq_parts.append(dq2)
                dkp_parts.append(dk2[:BLK])
                dkc_parts.append(dk2[BLK:])
                dvp_parts.append(dv2[:BLK])
                dvc_parts.append(dv2[BLK:])
            dq = _rope(jnp.concatenate(dq_parts, axis=1) * scale, cwin[own, :], swin[own, :])
            dkc = _rope(jnp.concatenate(dkc_parts, axis=1), cwin[own, :], swin[own, :])
            dkp = _rope(jnp.concatenate(dkp_parts, axis=1), cwin[before, :], swin[before, :])
            return dq, dkp, dkc, jnp.concatenate(dvp_parts, axis=1), jnp.concatenate(dvc_parts, axis=1)

        @pl.when(n < nsteps)
        def _():
            kwin[0:BLK] = kp_ref[...]
            kwin[BLK:] = kc_ref[...]
            vwin[0:BLK] = vp_ref[...]
            vwin[BLK:] = vc_ref[...]
            cwin[0:BLK] = cosp[...]
            cwin[BLK:] = cosc[...]
            swin[0:BLK] = -sinp[...]
            swin[BLK:] = -sinc[...]
            dq, dkp, dkc, dvp, dvc = one_block(0)
            last = slice(rows_per_step - BLK, rows_per_step)

            @pl.when(n > 0)
            def _():
                if per_step > 1:
                    out_ref[0:rows_per_step - BLK, :] = acc[0:rows_per_step - BLK, :].astype(BF16)
                out_ref[last, dq_cols] = acc[last, dq_cols].astype(BF16)
                out_ref[last, dk_cols] = (acc[last, dk_cols] + dkp).astype(BF16)
                out_ref[last, dv_cols] = (acc[last, dv_cols] + dvp).astype(BF16)

            acc[0:BLK, dq_cols] = dq
            acc[0:BLK, dk_cols] = dkc
            acc[0:BLK, dv_cols] = dvc

            def later(b, carry):
                dq, dkp, dkc, dvp, dvc = one_block(b)
                start = pl.multiple_of(b * BLK, BLK)
                before, rows = pl.ds(start - BLK, BLK), pl.ds(start, BLK)
                acc[before, dk_cols] += dkp
                acc[before, dv_cols] += dvp
                acc[rows, dq_cols] = dq
                acc[rows, dk_cols] = dkc
                acc[rows, dv_cols] = dvc
                return carry

            lax.fori_loop(1, per_step, later, 0)

        @pl.when(n == flush_at)
        def _():
            out_ref[...] = acc[...].astype(BF16)

    flush_at = nsteps - 1 if nsteps == 1 else nsteps
    out_lag = 0 if nsteps == 1 else 1
    nc = lambda n: jnp.minimum(n, nsteps - 1)
    npv = lambda n: jnp.maximum(jnp.minimum(n, nsteps - 1) * per_step - 1, 0)
    cur = lambda part: pl.BlockSpec((None, rows_per_step, ATTN_W), lambda r, n: (r, nc(n), part))
    prev = lambda part: pl.BlockSpec((None, BLK, ATTN_W), lambda r, n: (r, npv(n), part))
    row = pl.BlockSpec((None, rows_per_step, ATTN_W), lambda r, n: (r, nc(n), 0))
    tab_c = pl.BlockSpec((None, rows_per_step, LANES), lambda r, n: (r, nc(n), 0))
    tab_p = pl.BlockSpec((None, BLK, LANES), lambda r, n: (r, npv(n), 0))
    out_spec = pl.BlockSpec((None, rows_per_step, GROUP_COLS), lambda r, n: (r, jnp.maximum(n - out_lag, 0), 0))
    out = pl.pallas_call(
        body, grid=(dil, nsteps + out_lag),
        in_specs=[cur(0), cur(1), prev(1), cur(2), prev(2), row, row, row, tab_c, tab_c, tab_p, tab_p],
        out_specs=out_spec,
        out_shape=jax.ShapeDtypeStruct((dil, length, GROUP_COLS), BF16),
        scratch_shapes=[pltpu.VMEM((rows_per_step, GROUP_COLS), F32)]
        + [pltpu.VMEM((rows_per_step + BLK, ATTN_W), BF16)] * 2 + [pltpu.VMEM((rows_per_step + BLK, LANES), F32)] * 2,
        compiler_params=_cparams(2), name=f"attn_bwd_g{g}")(
            qkv_v, qkv_v, qkv_v, qkv_v, qkv_v, do, cc, lse, cos_v, sin_v, cos_v, sin_v)
    return out.reshape(t, GROUP_COLS)


SQRT_HALF = 0.7071067811865476
INV_SQRT_2PI = 0.3989422804014327


def _sgu_core(uv, g, b, w_ref, bias):
    cdf = 0.5 * (1.0 + lax.erf(uv * SQRT_HALF))
    z = uv * cdf
    u, v = z[:, :SGU_W], z[:, SGU_W:]
    mu = jnp.mean(v, axis=1, keepdims=True)
    xc = v - mu
    rs = lax.rsqrt(jnp.mean(xc * xc, axis=1, keepdims=True) + EPS)
    xhat = xc * rs
    vn = xhat * g + b
    row = lax.broadcasted_iota(jnp.int32, (SGU_CHUNK, SGU_CHUNK), 0)
    col = lax.broadcasted_iota(jnp.int32, (SGU_CHUNK, SGU_CHUNK), 1)
    tril = row >= col
    upper = lax.broadcasted_iota(jnp.int32, (SGU_CHUNK, LANES), 1) >= SGU_W // SGU_GROUPS
    ws, vlo, vhi, mixed = [], [], [], []
    for pr in range(SGU_W // LANES):
        sl = slice(pr * LANES, (pr + 1) * LANES)
        w0 = jnp.where(tril, w_ref[2 * pr], 0.0).astype(BF16)
        w1 = jnp.where(tril, w_ref[2 * pr + 1], 0.0).astype(BF16)
        vn2 = vn[:, sl]
        lo = jnp.where(upper, 0.0, vn2).astype(BF16)
        hi = jnp.where(upper, vn2, 0.0).astype(BF16)
        mixed.append(jnp.dot(w0, lo, preferred_element_type=F32) + jnp.dot(w1, hi, preferred_element_type=F32)
                     + bias[:, sl])
        ws.append((w0, w1))
        vlo.append(lo)
        vhi.append(hi)
    return cdf, u, xhat, rs, jnp.concatenate(mixed, axis=1), ws, vlo, vhi, tril, upper


SGU_STEP = 4 * SGU_CHUNK


def _for_chunks(step_rows, fn):
    def one(ci, carry):
        fn(pl.ds(pl.multiple_of(ci * SGU_CHUNK, SGU_CHUNK), SGU_CHUNK))
        return carry

    lax.fori_loop(0, step_rows // SGU_CHUNK, one, 0)


def _sgu_fwd(gu, ln_g, ln_b, w_s, bias_exp):
    t = gu.shape[0]
    step = min(t, SGU_STEP)

    def body(uv_ref, g_ref, b_ref, w_ref, bias_ref, o_ref):
        def chunk(rows):
            _, u, _, _, mixed, *_ = _sgu_core(uv_ref[rows, :].astype(F32), g_ref[...], b_ref[...], w_ref, bias_ref[...])
            o_ref[rows, :] = (u * mixed).astype(BF16)

        _for_chunks(step, chunk)

    return pl.pallas_call(
        body, grid=(t // step,),
        in_specs=[pl.BlockSpec((step, 2 * SGU_W), lambda n: (n, 0)), _full((1, SGU_W)), _full((1, SGU_W)),
                  _full((SGU_GROUPS, SGU_CHUNK, SGU_CHUNK)), _full((SGU_CHUNK, SGU_W))],
        out_specs=pl.BlockSpec((step, SGU_W), lambda n: (n, 0)),
        out_shape=jax.ShapeDtypeStruct((t, SGU_W), BF16),
        compiler_params=_cparams(1), name="sgu_fwd")(gu, ln_g, ln_b, w_s, bias_exp)


def _sgu_bwd(dproj, gu, dsgu, ln_g, ln_b, w_s, bias_exp):
    t = gu.shape[0]
    step = min(t, SGU_STEP)
    nsteps = t // step
    e = _head_sum_matrix()

    def body(dp_in, uv_ref, ds_ref, g_ref, b_ref, w_ref, bias_ref, e_ref, out_ref, dw_ref, dbias_ref, dg_ref, db_ref):
        n = pl.program_id(0)

        @pl.when(n == 0)
        def _():
            dw_ref[...] = jnp.zeros(dw_ref.shape, F32)
            dbias_ref[...] = jnp.zeros(dbias_ref.shape, F32)
            dg_ref[...] = jnp.zeros(dg_ref.shape, F32)
            db_ref[...] = jnp.zeros(db_ref.shape, F32)

        _for_chunks(step, functools.partial(chunk, uv_ref, ds_ref, g_ref, b_ref, w_ref, bias_ref, out_ref, dw_ref, dbias_ref,
                                            dg_ref, db_ref))

        @pl.when(n == nsteps - 1)
        def _():
            dbias_ref[...] = _group_sum(dbias_ref[...], e_ref[...])

    def chunk(uv_ref, ds_ref, g_ref, b_ref, w_ref, bias_ref, out_ref, dw_ref, dbias_ref, dg_ref, db_ref, rows):
        uv = uv_ref[rows, :].astype(F32)
        g = g_ref[...]
        cdf, u, xhat, rs, mixed, ws, vlo, vhi, tril, upper = _sgu_core(uv, g, b_ref[...], w_ref, bias_ref[...])
        dsg = ds_ref[rows, :]
        du = dsg * mixed
        dmixed = dsg * u
        dbias_ref[...] += dmixed
        dvn = []
        for pr in range(SGU_W // LANES):
            sl = slice(pr * LANES, (pr + 1) * LANES)
            dm2 = dmixed[:, sl]
            dlo = jnp.where(upper, 0.0, dm2).astype(BF16)
            dhi = jnp.where(upper, dm2, 0.0).astype(BF16)
            w0, w1 = ws[pr]
            dvn.append(lax.dot_general(w0, dlo, (TN, ((), ())), preferred_element_type=F32)
                       + lax.dot_general(w1, dhi, (TN, ((), ())), preferred_element_type=F32))
            dw0 = lax.dot_general(dlo, vlo[pr], (NT, ((), ())), preferred_element_type=F32)
            dw1 = lax.dot_general(dhi, vhi[pr], (NT, ((), ())), preferred_element_type=F32)
            dw_ref[2 * pr] += jnp.where(tril, dw0, 0.0)
            dw_ref[2 * pr + 1] += jnp.where(tril, dw1, 0.0)
        dvn = jnp.concatenate(dvn, axis=1)
        dg_ref[...] += jnp.sum(dvn * xhat, axis=0, keepdims=True)
        db_ref[...] += jnp.sum(dvn, axis=0, keepdims=True)
        dxh = dvn * g
        dv = rs * (dxh - jnp.mean(dxh, axis=1, keepdims=True) - xhat * jnp.mean(dxh * xhat, axis=1, keepdims=True))
        dz = jnp.concatenate([du, dv], axis=1)
        dgelu = cdf + uv * (INV_SQRT_2PI * jnp.exp(-0.5 * uv * uv))
        out_ref[rows, :] = (dz * dgelu).astype(BF16)

    outs = pl.pallas_call(
        body, grid=(nsteps,),
        in_specs=[pl.BlockSpec(memory_space=pl.ANY), pl.BlockSpec((step, 2 * SGU_W), lambda n: (n, 0)),
                  pl.BlockSpec((step, SGU_W), lambda n: (n, 0)), _full((1, SGU_W)), _full((1, SGU_W)),
                  _full((SGU_GROUPS, SGU_CHUNK, SGU_CHUNK)), _full((SGU_CHUNK, SGU_W)), _full((ATTN_W, ATTN_W))],
        out_specs=[pl.BlockSpec((step, 2 * SGU_W), lambda n: (n, 0)), _full((SGU_GROUPS, SGU_CHUNK, SGU_CHUNK)),
                   _full((SGU_CHUNK, SGU_W)), _full((1, SGU_W)), _full((1, SGU_W))],
        out_shape=[jax.ShapeDtypeStruct(dproj.shape, BF16), jax.ShapeDtypeStruct((SGU_GROUPS, SGU_CHUNK, SGU_CHUNK), F32),
                   jax.ShapeDtypeStruct((SGU_CHUNK, SGU_W), F32), jax.ShapeDtypeStruct((1, SGU_W), F32),
                   jax.ShapeDtypeStruct((1, SGU_W), F32)],
        input_output_aliases={0: 0},
        compiler_params=_cparams(1), name="sgu_bwd")(dproj, gu, dsgu, ln_g, ln_b, w_s, bias_exp, e)
    return outs


def _merge_fwd(attn, sgu, gu, x, w_pa, w_ps, w_out, g2):
    t = x.shape[0]
    tm = min(t, 512)

    def body(a_ref, s_ref, ga_ref, gb_ref, x_ref, wpa, wps, wo, g_ref, pa_ref, ps_ref, m_ref, x1_ref, h2_ref):
        pa = jnp.dot(a_ref[...], wpa[...], preferred_element_type=F32)
        ps = jnp.dot(s_ref[...], wps[...], preferred_element_type=F32)
        merged = (_sigmoid(ga_ref[...].astype(F32)) * pa + _sigmoid(gb_ref[...].astype(F32)) * ps).astype(BF16)
        x1 = x_ref[...] + jnp.dot(merged, wo[...], preferred_element_type=F32)
        xhat, _ = _rms_stats(x1)
        pa_ref[...] = pa.astype(BF16)
        ps_ref[...] = ps.astype(BF16)
        m_ref[...] = merged
        x1_ref[...] = x1
        h2_ref[...] = (xhat * g_ref[...]).astype(BF16)

    half = pl.BlockSpec((tm, ATTN_W), lambda i: (i, 0))
    full = pl.BlockSpec((tm, D_MODEL), lambda i: (i, 0))
    return pl.pallas_call(
        body, grid=(t // tm,),
        in_specs=[half, half, pl.BlockSpec((tm, D_MODEL), lambda i: (i, 1)), pl.BlockSpec((tm, D_MODEL), lambda i: (i, 2)),
                  full, _resident((ATTN_W, D_MODEL)), _resident((SGU_W, D_MODEL)), _resident((D_MODEL, D_MODEL)),
                  _full((1, D_MODEL))],
        out_specs=[full] * 5,
        out_shape=[jax.ShapeDtypeStruct((t, D_MODEL), BF16), jax.ShapeDtypeStruct((t, D_MODEL), BF16),
                   jax.ShapeDtypeStruct((t, D_MODEL), BF16), jax.ShapeDtypeStruct((t, D_MODEL), F32),
                   jax.ShapeDtypeStruct((t, D_MODEL), BF16)],
        compiler_params=_cparams(1), name="merge_fwd")(attn, sgu, gu, gu, x, w_pa, w_ps, w_out, g2)


def _merge_bwd(dx1b, gu, pa, ps, w_pa, w_ps, w_out):
    t = dx1b.shape[0]
    tm = min(t, 512)

    def body(d_ref, ga_ref, gb_ref, pa_ref, ps_ref, wpa, wps, wo, out_ref, dpa_ref, dps_ref, da_ref, dsg_ref):
        dm = lax.dot_general(d_ref[...], wo[...], (NT, ((), ())), preferred_element_type=F32)
        sa, sb = _sigmoid(ga_ref[...].astype(F32)), _sigmoid(gb_ref[...].astype(F32))
        dpa = (dm * sa).astype(BF16)
        dps = (dm * sb).astype(BF16)
        out_ref[:, 0:D_MODEL] = jnp.zeros((tm, D_MODEL), BF16)
        out_ref[:, D_MODEL:2 * D_MODEL] = (dm * pa_ref[...].astype(F32) * sa * (1.0 - sa)).astype(BF16)
        out_ref[:, 2 * D_MODEL:GU_COLS] = (dm * ps_ref[...].astype(F32) * sb * (1.0 - sb)).astype(BF16)
        dpa_ref[...] = dpa
        dps_ref[...] = dps
        da_ref[...] = lax.dot_general(dpa, wpa[...], (NT, ((), ())), preferred_element_type=F32)
        dsg_ref[...] = lax.dot_general(dps, wps[...], (NT, ((), ())), preferred_element_type=F32)

    half = pl.BlockSpec((tm, ATTN_W), lambda i: (i, 0))
    full = pl.BlockSpec((tm, D_MODEL), lambda i: (i, 0))
    return pl.pallas_call(
        body, grid=(t // tm,),
        in_specs=[full, pl.BlockSpec((tm, D_MODEL), lambda i: (i, 1)),
                  pl.BlockSpec((tm, D_MODEL), lambda i: (i, 2)), full, full,
                  _resident((ATTN_W, D_MODEL)), _resident((SGU_W, D_MODEL)), _resident((D_MODEL, D_MODEL))],
        out_specs=[pl.BlockSpec((tm, GU_COLS), lambda i: (i, 0)), full, full, half, half],
        out_shape=[jax.ShapeDtypeStruct((t, GU_COLS), BF16), jax.ShapeDtypeStruct((t, D_MODEL), BF16),
                   jax.ShapeDtypeStruct((t, D_MODEL), BF16), jax.ShapeDtypeStruct((t, ATTN_W), F32),
                   jax.ShapeDtypeStruct((t, SGU_W), F32)],
        compiler_params=_cparams(1), name="merge_bwd")(dx1b, gu, gu, pa, ps, w_pa, w_ps, w_out)


def _token_call(name, body, t, tm, ins, outs, reds=(), scratch=()):
    return pl.pallas_call(
        body, grid=(t // tm,), in_specs=[s for _, s in ins],
        out_specs=[o[2] for o in outs] + [_full(r) for r in reds],
        out_shape=[jax.ShapeDtypeStruct(o[0], o[1]) for o in outs] + [jax.ShapeDtypeStruct(r, F32) for r in reds],
        scratch_shapes=list(scratch), compiler_params=_cparams(1), name=name)(*[a for a, _ in ins])


def _rows_spec(tm, width):
    return pl.BlockSpec((tm, width), lambda i: (i, 0))


def _chips_spec(tm):
    return pl.BlockSpec((N_CHIPS, tm, FF_SHARD), lambda i: (0, i, 0))


def _zero_at_start(*refs):
    @pl.when(pl.program_id(0) == 0)
    def _():
        for r in refs:
            r[...] = jnp.zeros(r.shape, r.dtype)


def _ffn_fwd(h2, w_g, w_u):
    t = h2.shape[0]
    tm = min(t, 512)

    def body(h_ref, wg_ref, wu_ref, fa_ref, fb_ref, ff_ref):
        h = h_ref[...]
        for s in range(N_CHIPS):
            a = jnp.dot(h, wg_ref[s], preferred_element_type=F32)
            b = jnp.dot(h, wu_ref[s], preferred_element_type=F32)
            sg = _sigmoid(a)
            silu = a * sg
            fa_ref[s] = (b * (sg * (1.0 + a * (1.0 - sg)))).astype(BF16)
            fb_ref[s] = silu.astype(BF16)
            ff_ref[s] = (silu * b).astype(BF16)

    shp = (N_CHIPS, t, FF_SHARD)
    w_spec = _resident((N_CHIPS, D_MODEL, FF_SHARD))
    return _token_call("ffn_fwd", body, t, tm, [(h2, _rows_spec(tm, D_MODEL)), (w_g, w_spec), (w_u, w_spec)],
                       [(shp, BF16, _chips_spec(tm))] * 3)


def _ffn_down_loss(ff, w_d, x1, tgt, gf):
    t = x1.shape[0]
    tm = min(t, 512)

    def body(ff_ref, wd_ref, x1_ref, tgt_ref, g_ref, dx2_ref, dx2b_ref, loss_ref, dgf_ref):
        _zero_at_start(loss_ref, dgf_ref)
        acc = jnp.dot(ff_ref[0], wd_ref[0], preferred_element_type=F32)
        for s in range(1, N_CHIPS):
            acc = acc + jnp.dot(ff_ref[s], wd_ref[s], preferred_element_type=F32)
        x2 = x1_ref[...] + acc
        g = g_ref[...]
        xhat, rr = _rms_stats(x2)
        diff = xhat * g - tgt_ref[...]
        rows = jnp.sum(diff * diff, axis=1, keepdims=True)
        loss_ref[...] += jnp.broadcast_to(jnp.sum(rows, axis=0, keepdims=True) * (0.5 / D_MODEL), (1, LANES))
        dy = diff * (1.0 / D_MODEL)
        dgf_ref[...] += jnp.sum(dy * xhat, axis=0, keepdims=True)
        dx2 = _rms_bwd(dy, xhat, rr, g)
        dx2_ref[...] = dx2
        dx2b_ref[...] = dx2.astype(BF16)

    row = _rows_spec(tm, D_MODEL)
    return _token_call("ffn_down_loss", body, t, tm,
                       [(ff, _chips_spec(tm)), (w_d, _resident((N_CHIPS, FF_SHARD, D_MODEL))), (x1, row), (tgt, row),
                        (gf, _full((1, D_MODEL)))],
                       [((t, D_MODEL), F32, row), ((t, D_MODEL), BF16, row)], reds=[(1, LANES), (1, D_MODEL)])


def _ffn_bwd_act(dx2b, w_d, fa, fb):
    t = dx2b.shape[0]
    tm = min(t, 512)

    def body(d_ref, wd_ref, fa_ref, fb_ref, da_ref, db_ref):
        d = d_ref[...]
        for s in range(N_CHIPS):
            dff = lax.dot_general(d, wd_ref[s], (NT, ((), ())), preferred_element_type=F32)
            da_ref[s] = (dff * fa_ref[s].astype(F32)).astype(BF16)
            db_ref[s] = (dff * fb_ref[s].astype(F32)).astype(BF16)

    shp = (N_CHIPS, t, FF_SHARD)
    return _token_call("ffn_bwd_act", body, t, tm,
                       [(dx2b, _rows_spec(tm, D_MODEL)), (w_d, _resident((N_CHIPS, FF_SHARD, D_MODEL))),
                        (fa, _chips_spec(tm)), (fb, _chips_spec(tm))],
                       [(shp, BF16, _chips_spec(tm))] * 2)


def _ffn_bwd_in(da, db, w_g, w_u, x1, dx2, g2):
    t = x1.shape[0]
    tm = min(t, 512)

    def body(da_ref, db_ref, wg_ref, wu_ref, x1_ref, dx2_ref, g_ref, dx1_ref, dx1b_ref, dg_ref):
        _zero_at_start(dg_ref)
        acc = None
        for s in range(N_CHIPS):
            part = (lax.dot_general(da_ref[s], wg_ref[s], (NT, ((), ())), preferred_element_type=F32)
                    + lax.dot_general(db_ref[s], wu_ref[s], (NT, ((), ())), preferred_element_type=F32))
            acc = part if acc is None else acc + part
        xhat, rr = _rms_stats(x1_ref[...])
        dg_ref[...] += jnp.sum(acc * xhat, axis=0, keepdims=True)
        dx1 = dx2_ref[...] + _rms_bwd(acc, xhat, rr, g_ref[...])
        dx1_ref[...] = dx1
        dx1b_ref[...] = dx1.astype(BF16)

    row = _rows_spec(tm, D_MODEL)
    w_spec = _resident((N_CHIPS, D_MODEL, FF_SHARD))
    return _token_call("ffn_bwd_in", body, t, tm,
                       [(da, _chips_spec(tm)), (db, _chips_spec(tm)), (w_g, w_spec), (w_u, w_spec), (x1, row), (dx2, row),
                        (g2, _full((1, D_MODEL)))],
                       [((t, D_MODEL), F32, row), ((t, D_MODEL), BF16, row)], reds=[(1, D_MODEL)])


def _group_dh(d, w_refs):
    dh = None
    for part, w_ref in enumerate(w_refs):
        term = lax.dot_general(d[:, part * ATTN_W:(part + 1) * ATTN_W], w_ref[...], (NT, ((), ())),
                               preferred_element_type=F32)
        dh = term if dh is None else dh + term
    return dh


def _in_proj_bwd(dgu, dqkvs, w_in, x, dx1, g1):
    t = x.shape[0]
    tile = min(t, TILE)
    ngroups = len(DILATIONS)

    def body(*refs):
        dgu_ref, dq_refs = refs[0], refs[1:1 + ngroups]
        w0_ref, w1_ref = refs[1 + ngroups:3 + ngroups]
        wg_refs = [refs[3 + ngroups + 3 * g:6 + ngroups + 3 * g] for g in range(ngroups)]
        x_ref, dx1_ref, g_ref, dx_ref, dg_ref = refs[3 + 4 * ngroups:5 + 4 * ngroups + 3]
        slabs = refs[5 + 4 * ngroups + 3:]
        _zero_at_start(dg_ref)
        for g in range(1, ngroups):
            dil = DILATIONS[g]
            part = _group_dh(dq_refs[g][...].reshape(tile, GROUP_COLS), wg_refs[g])
            for r in range(dil):
                _put_class_rows(slabs[g - 1], r, dil, part[r * (tile // dil):(r + 1) * (tile // dil)])
        dh = lax.dot_general(dgu_ref[:, 0:GU_HALF], w0_ref[...], (NT, ((), ())), preferred_element_type=F32)
        dh = dh + lax.dot_general(dgu_ref[:, GU_HALF:], w1_ref[...], (NT, ((), ())), preferred_element_type=F32)
        dh = dh + _group_dh(dq_refs[0][0], wg_refs[0])
        for slab in slabs:
            dh = dh + _from_slabs(slab)
        xhat, rr = _rms_stats(x_ref[...])
        dg_ref[...] += jnp.sum(dh * xhat, axis=0, keepdims=True)
        dx_ref[...] = dx1_ref[...] + _rms_bwd(dh, xhat, rr, g_ref[...])

    row = _rows_spec(tile, D_MODEL)
    group_ins = [(dqkvs[g].reshape(d, t // d, GROUP_COLS), _group_spec(d, tile, GROUP_COLS)) for g, d in enumerate(DILATIONS)]
    w_specs = _gu_w_specs() + [s for g in range(ngroups) for s in _group_w_specs(g)]
    return _token_call(
        "in_proj_bwd", body, t, tile,
        [(dgu, _rows_spec(tile, GU_COLS))] + group_ins + [(w_in, s) for s in w_specs]
        + [(x, row), (dx1, row), (g1, _full((1, D_MODEL)))],
        [((t, D_MODEL), F32, row)], reds=[(1, D_MODEL)], scratch=[_slabs(tile, D_MODEL)] * (ngroups - 1))


WGRAD_TK = 2048


def _wgrad_mm(name, grid, a, a_spec, b, b_spec, acc_shape, out_shape, out_spec, dst=None):
    nk = grid[-1]

    def body(*refs):
        a_ref, b_ref, o_ref, acc_ref = refs[0], refs[1], refs[-2], refs[-1]
        k = pl.program_id(len(grid) - 1)
        part = lax.dot_general(a_ref[...], b_ref[...], (TN, ((), ())), preferred_element_type=F32)

        @pl.when(k == 0)
        def _():
            acc_ref[...] = part

        @pl.when(k > 0)
        def _():
            acc_ref[...] += part

        @pl.when(k == nk - 1)
        def _():
            o_ref[...] = acc_ref[...].astype(BF16)

    filled = [] if dst is None else [dst]
    return pl.pallas_call(
        body, grid=grid, in_specs=[a_spec, b_spec] + [pl.BlockSpec(memory_space=pl.ANY)] * len(filled),
        out_specs=out_spec, out_shape=jax.ShapeDtypeStruct(out_shape, BF16), scratch_shapes=[pltpu.VMEM(acc_shape, F32)],
        input_output_aliases={2: 0} if filled else {}, compiler_params=_cparams(len(grid)), name=name)(a, b, *filled)


def _wgrad_2d(name, a, b, tm, tn):
    t, k1 = a.shape
    n = b.shape[1]
    tk = min(t, WGRAD_TK)
    return _wgrad_mm(name, (k1 // tm, n // tn, t // tk), a, pl.BlockSpec((tk, tm), lambda i, j, k: (k, i)),
                     b, pl.BlockSpec((tk, tn), lambda i, j, k: (k, j)), (tm, tn), (k1, n),
                     pl.BlockSpec((tm, tn), lambda i, j, k: (i, j)))


def _wgrad_in(hs, dgu, dqkvs):
    t = dgu.shape[0]
    tk = min(t, WGRAD_TK)
    gu_block = QKV_BLOCKS * ATTN_W // GU_HALF
    parts = [(hs[0], dgu, GU_HALF, lambda j: j + gu_block)]
    parts += [(hs[g].reshape(t, D_MODEL), dqkvs[g], ATTN_W, lambda j, g=g: _w_in_block(j, g)) for g in range(3)]
    dst = None
    for n, (a, b, tn, block_of) in enumerate(parts):
        dst = _wgrad_mm(f"wgrad_in_{n}", (1, b.shape[1] // tn, t // tk),
                        a, pl.BlockSpec((tk, D_MODEL), lambda i, j, k: (k, 0)), b, pl.BlockSpec((tk, tn), lambda i, j, k: (k, j)),
                        (D_MODEL, tn), (D_MODEL, IN_COLS),
                        pl.BlockSpec((D_MODEL, tn), lambda i, j, k, block_of=block_of: (0, block_of(j))), dst=dst)
    return dst


def _wgrad_ff_in(name, h2, da):
    t = h2.shape[0]
    tk = min(t, WGRAD_TK)
    return _wgrad_mm(name, (N_CHIPS, 1, t // tk), h2, pl.BlockSpec((tk, D_MODEL), lambda i, j, k: (k, 0)),
                     da, pl.BlockSpec((None, tk, FF_SHARD), lambda i, j, k: (i, k, 0)), (D_MODEL, FF_SHARD),
                     (N_CHIPS, D_MODEL, FF_SHARD), pl.BlockSpec((None, D_MODEL, FF_SHARD), lambda i, j, k: (i, 0, 0)))


def _wgrad_ff_down(ff, dx2b):
    t = dx2b.shape[0]
    tk = min(t, WGRAD_TK)
    return _wgrad_mm("wgrad_ffn_down", (N_CHIPS, 1, t // tk), ff, pl.BlockSpec((None, tk, FF_SHARD), lambda i, j, k: (i, k, 0)),
                     dx2b, pl.BlockSpec((tk, D_MODEL), lambda i, j, k: (k, 0)), (FF_SHARD, D_MODEL),
                     (N_CHIPS, FF_SHARD, D_MODEL), pl.BlockSpec((None, FF_SHARD, D_MODEL), lambda i, j, k: (i, 0, 0)))


def _local_step(x, pos_col, tgt, g1, ln_g, ln_b, w_s, b_s, g2, gf, first_weight, late_weights, on_grads=None):
    tables = _rope_tables(pos_col)
    bias_exp = jnp.repeat(jnp.transpose(b_s), SGU_W // SGU_GROUPS, axis=1)

    hs = _norm_fwd(x, g1)
    w_p = first_weight([hs[0], bias_exp] + [table for pair in tables for table in pair])
    gu, qkvs = _in_proj(hs, w_p, tables)
    os_, ls_ = [], []
    for g, dil in enumerate(DILATIONS):
        o, lse = _attn_fwd(qkvs[g], g, dil)
        os_.append(o)
        ls_.append(lse)
    attn = _combine_fwd(os_, ls_)
    sgu = _sgu_fwd(gu, ln_g, ln_b, w_s, bias_exp)
    w_pa, w_ps, w_out, w_g, w_u, w_d = late_weights(attn)
    pa, ps, merged, x1, h2 = _merge_fwd(attn, sgu, gu, x, w_pa, w_ps, w_out, g2)
    fa, fb, ff = _ffn_fwd(h2, w_g, w_u)
    dx2, dx2b, loss, dgf = _ffn_down_loss(ff, w_d, x1, tgt, gf)

    da, db = _ffn_bwd_act(dx2b, w_d, fa, fb)
    dw_d = _wgrad_ff_down(ff, dx2b)
    dx1, dx1b, dg2 = _ffn_bwd_in(da, db, w_g, w_u, x1, dx2, g2)
    dw_g = _wgrad_ff_in("wgrad_ffn_gate", h2, da)
    dw_u = _wgrad_ff_in("wgrad_ffn_up", h2, db)

    dgu, dpa, dps, dattn, dsgu = _merge_bwd(dx1b, gu, pa, ps, w_pa, w_ps, w_out)
    dw_out = _wgrad_2d("wgrad_out", merged, dx1b, D_MODEL, D_MODEL)
    dw_pa = _wgrad_2d("wgrad_proj_attn", attn, dpa, ATTN_W, D_MODEL)
    dw_ps = _wgrad_2d("wgrad_proj_sgu", sgu, dps, SGU_W, D_MODEL)
    if on_grads is not None:
        ln_g = ln_g + on_grads(1, dict(w_proj_attn=dw_pa, w_proj_sgu=dw_ps, w_out=dw_out, w_ffn_gate=dw_g, w_ffn_up=dw_u,
                                       w_ffn_down=dw_d))[:, :SGU_W]
    dgu, dw_s, dbias, dln_g, dln_b = _sgu_bwd(dgu, gu, dsgu, ln_g, ln_b, w_s, bias_exp)
    dos, ccs = _combine_bwd(dattn, os_, ls_)
    dqkvs = [_attn_bwd(qkvs[g], dos[g], ccs[g], ls_[g], *tables[g], g, dil) for g, dil in enumerate(DILATIONS)]
    dw_p = _wgrad_in(hs, dgu, dqkvs)
    if on_grads is not None:
        g1 = g1 + on_grads(0, dict(w_in=dw_p))
    dx, dg1 = _in_proj_bwd(dgu, dqkvs, w_p, x, dx1, g1)

    db_s = jnp.transpose(dbias[:, ::SGU_W // SGU_GROUPS])
    small = dict(loss=loss, norm1_g=dg1, sgu_ln_g=dln_g, sgu_ln_b=dln_b, w_spatial=dw_s, b_spatial=db_s,
                 norm2_g=dg2, final_g=dgf)
    big = dict(w_in=dw_p, w_proj_attn=dw_pa, w_proj_sgu=dw_ps, w_out=dw_out, w_ffn_gate=dw_g, w_ffn_up=dw_u,
               w_ffn_down=dw_d)
    return dx, big, small


def _ew(name, fn, ins, out_dtypes, after=()):
    shp = ins[0].shape
    rows, cols = shp
    tr = next((cand for cand in (256, 352, 128) if rows % cand == 0 and rows > cand), rows)

    def body(*refs):
        res = fn(*[r[...] for r in refs[:len(ins)]])
        for o_ref, v in zip(refs[len(ins) + len(after):], res):
            o_ref[...] = v.astype(o_ref.dtype)

    spec = pl.BlockSpec((tr, cols), lambda i: (i, 0))
    return pl.pallas_call(
        body, grid=(rows // tr,), in_specs=[spec] * len(ins) + [pl.BlockSpec(memory_space=pl.ANY)] * len(after),
        out_specs=[spec] * len(out_dtypes), out_shape=[jax.ShapeDtypeStruct(shp, d) for d in out_dtypes],
        compiler_params=_cparams(1), name=name)(*ins, *after)


def _adamw_math(g, w, m, v):
    m = ADAM_B1 * m + (1.0 - ADAM_B1) * g
    v = ADAM_B2 * v + (1.0 - ADAM_B2) * (g * g)
    m_hat = m / (1.0 - ADAM_B1 ** ADAM_STEP)
    v_hat = v / (1.0 - ADAM_B2 ** ADAM_STEP)
    delta = -ADAM_LR * (m_hat / (jnp.sqrt(v_hat) + ADAM_EPS) + ADAM_WD * w)
    return delta, m, v


def _adamw(name, g, w, m, v):
    return _ew(name, lambda g_, w_, m_, v_: (g_,) + _adamw_math(g_, w_, m_, v_), [g, w, m, v], [F32] * 4)


VMEM_SPEC = pl.BlockSpec(memory_space=pltpu.VMEM)


def _for_row_chunks(rows, fn):
    ck = next(c for c in (64, 32, 16) if rows % c == 0)

    def step(i, carry):
        fn(pl.multiple_of(i * ck, ck), ck)
        return carry

    lax.fori_loop(0, rows // ck, step, 0)


def _place():
    x, y, c = lax.axis_index("x"), lax.axis_index("y"), lax.axis_index("c")
    chips = [(1 - x, y), (x, 1 - y), (1 - x, 1 - y)]
    return x, y, c, 2 * x + y, chips


def _rows(ref, start, size):
    if len(ref.shape) == 2:
        return ref.at[pl.ds(start, size), :]
    return ref.at[:, pl.ds(start, size), :]


def _gather_finish(name, shard, landed):
    k_rows, n = shard.shape
    kh = k_rows // 2

    def body(shard_hbm, land_hbm, out_ref, shard_ref, land_ref, loc, send, recv):
        x, y, c, me, chips = _place()
        sibling = (x, y, 1 - c)

        def window(core, chip):
            return out_ref.at[pl.ds(core * kh, kh), pl.ds(pl.multiple_of(chip * n, LANES), n)]

        loads = [pltpu.make_async_copy(land_hbm.at[j], land_ref.at[j], loc.at[0, j]) for j in range(3)]
        loads.append(pltpu.make_async_copy(shard_hbm, shard_ref, loc.at[0, 3]))
        for cp in loads:
            cp.start(priority=1)
        copies, passed = [], []
        for j, chip in enumerate(chips):
            mine = window(c, 2 * chip[0] + chip[1])
            loads[j].wait()
            copies.append(pltpu.make_async_copy(land_ref.at[j], mine, loc.at[1, j]))
            passed.append(pltpu.make_async_remote_copy(src_ref=land_ref.at[j], dst_ref=mine, send_sem=send.at[j],
                                                       recv_sem=recv.at[j], device_id=sibling, device_id_type=MESH))
            copies[-1].start()
            passed[-1].start()
        loads[3].wait()
        copies.append(pltpu.make_async_copy(shard_ref, out_ref.at[:, pl.ds(pl.multiple_of(me * n, LANES), n)], loc.at[1, 3]))
        copies[-1].start()
        for j, chip in enumerate(chips):
            pltpu.make_async_remote_copy(src_ref=land_ref.at[j], dst_ref=window(1 - c, 2 * chip[0] + chip[1]), send_sem=send.at[j],
                                         recv_sem=recv.at[j], device_id=sibling, device_id_type=MESH).wait_recv()
        for cp in copies:
            cp.wait()
        for cp in passed:
            cp.wait_send()

    any_spec = pl.BlockSpec(memory_space=pl.ANY)
    return pl.pallas_call(
        body, in_specs=[any_spec] * 2, out_specs=any_spec,
        out_shape=jax.ShapeDtypeStruct((k_rows, N_CHIPS * n), shard.dtype),
        scratch_shapes=[pltpu.VMEM(shard.shape, shard.dtype), pltpu.VMEM(landed.shape, landed.dtype),
                        pltpu.SemaphoreType.DMA((2, 4)), pltpu.SemaphoreType.DMA((3,)), pltpu.SemaphoreType.DMA((3,))],
        compiler_params=pltpu.CompilerParams(vmem_limit_bytes=VMEM_LIMIT), name=name)(shard, landed)


HBM_SPEC = pl.BlockSpec(memory_space=pltpu.HBM)
SEM_SPEC = pl.BlockSpec(memory_space=pltpu.SEMAPHORE)
DATAFLOW = pltpu.SideEffectType.DATAFLOW_SIDE_EFFECTING
TOKEN_SHAPE = (1, D_MODEL)
N_PEERS = 7
SUM_SPLIT = 4
SUM_SPLIT_ELEMS = 512 * 1024


def _peers():
    x, y, c = lax.axis_index("x"), lax.axis_index("y"), lax.axis_index("c")
    flip = lambda v, f: 1 - v if f else v
    return [(flip(x, k & 4), flip(y, k & 2), flip(c, k & 1)) for k in range(1, N_PEERS + 1)]


def _piece_shape(shape):
    return (shape[-2] // 2, shape[2] if len(shape) == 3 else shape[1] // N_CHIPS)


def _device_piece(ref, chip, core):
    kh, n4 = _piece_shape(ref.shape)
    if len(ref.shape) == 3:
        return ref.at[chip, pl.ds(core * kh, kh), :]
    return ref.at[pl.ds(core * kh, kh), pl.ds(chip * n4, n4)]


def _exchange_copies(partials, lands, send, recv):
    return [pltpu.make_async_remote_copy(
        src_ref=_device_piece(partials[t], 2 * px + py, pc), dst_ref=lands[t].at[k], send_sem=send.at[t * N_PEERS + k],
        recv_sem=recv.at[t * N_PEERS + k], device_id=(px, py, pc), device_id_type=MESH)
        for t in range(len(partials)) for k, (px, py, pc) in enumerate(_peers())]


def _broadcast_copies(srcs, lands, send, recv):
    return [pltpu.make_async_remote_copy(
        src_ref=srcs[t], dst_ref=lands[t].at[k], send_sem=send.at[t * N_PEERS + k], recv_sem=recv.at[t * N_PEERS + k],
        device_id=peer, device_id_type=MESH)
        for t in range(len(srcs)) for k, peer in enumerate(_peers())]


class _LocalCopy:
    def __init__(self, src_ref, dst_ref, sem):
        self.copy = pltpu.make_async_copy(src_ref, dst_ref, sem)

    def start(self):
        self.copy.start()

    def wait_send(self):
        self.copy.wait()

    def wait_recv(self):
        pass


def _gather_copies(shards, lands, send, recv):
    x, y, c, me, chips = _place()
    copies = []
    for t in range(len(shards)):
        n = shards[t].shape[1]
        place = lands[t].at[me] if len(lands[t].shape) == 3 else lands[t].at[:, pl.ds(pl.multiple_of(me * n, LANES), n)]
        copies += [pltpu.make_async_remote_copy(
            src_ref=shards[t], dst_ref=place, send_sem=send.at[t * 4 + j], recv_sem=recv.at[t * 4 + j],
            device_id=(*chip, c), device_id_type=MESH) for j, chip in enumerate(chips)]
        copies.append(_LocalCopy(shards[t], place, send.at[t * 4 + 3]))
    return copies


def _gather_half_copies(shards, lands, send, recv):
    x, y, c, me, chips = _place()
    return [pltpu.make_async_remote_copy(
        src_ref=_rows(shards[t], c * (shards[t].shape[0] // 2), shards[t].shape[0] // 2), dst_ref=lands[t].at[j],
        send_sem=send.at[t * 3 + j], recv_sem=recv.at[t * 3 + j], device_id=(*chip, c), device_id_type=MESH)
        for t in range(len(shards)) for j, chip in enumerate(chips)]


def _split_start(name, copies, per_tensor, srcs, land_shapes):
    nt = len(srcs)
    lands = [lax.empty(s, a.dtype) for s, a in zip(land_shapes, srcs)]
    nsem = nt * per_tensor

    def body(*refs):
        send, recv = refs[2 * nt], refs[2 * nt + 1]
        for cp in copies(refs[:nt], refs[nt:2 * nt], send, recv):
            cp.start()
        refs[-1][...] = jnp.zeros(TOKEN_SHAPE, F32)

    hbm = lambda a: pltpu.with_memory_space_constraint(a, pltpu.HBM)
    outs = pl.pallas_call(
        body, name=name,
        out_shape=[pltpu.SemaphoreType.DMA((nsem,)), pltpu.SemaphoreType.DMA((nsem,))]
        + [pltpu.HBM(s.shape, s.dtype) for s in srcs] + [pltpu.HBM(l.shape, l.dtype) for l in lands]
        + [jax.ShapeDtypeStruct(TOKEN_SHAPE, F32)],
        in_specs=[HBM_SPEC] * (2 * nt), out_specs=[SEM_SPEC, SEM_SPEC] + [HBM_SPEC] * (2 * nt) + [VMEM_SPEC],
        input_output_aliases={i: 2 + i for i in range(2 * nt)},
        compiler_params=pltpu.CompilerParams(has_side_effects=DATAFLOW))(*[hbm(a) for a in list(srcs) + lands])
    return outs[0], outs[1], outs[2:2 + nt], outs[2 + nt:2 + 2 * nt], outs[-1]


def _split_wait(name, copies, send, recv, srcs, lands, after):
    nt = len(srcs)
    after = list(after) if isinstance(after, (list, tuple)) else [after]

    def body(*refs):
        for cp in copies(refs[:nt], refs[nt:2 * nt], refs[2 * nt], refs[2 * nt + 1]):
            cp.wait_send()
            cp.wait_recv()

    outs = pl.pallas_call(
        body, name=name,
        out_shape=[pltpu.HBM(s.shape, s.dtype) for s in srcs] + [pltpu.HBM(l.shape, l.dtype) for l in lands],
        in_specs=[HBM_SPEC] * (2 * nt) + [SEM_SPEC, SEM_SPEC] + [pl.BlockSpec(memory_space=pl.ANY)] * len(after),
        out_specs=[HBM_SPEC] * (2 * nt), input_output_aliases={i: i for i in range(2 * nt)},
        compiler_params=pltpu.CompilerParams(has_side_effects=DATAFLOW))(*srcs, *lands, send, recv, *after)
    return outs[:nt], outs[nt:]


def _device_sum(name, partials, lands):
    nt = len(partials)
    pieces = [_piece_shape(p.shape) for p in partials]
    units = []
    for t, (kh, n4) in enumerate(pieces):
        split = SUM_SPLIT if kh * n4 >= SUM_SPLIT_ELEMS else 1
        units += [(t, j * (kh // split), kh // split) for j in range(split)]
    nu = len(units)

    def body(*refs):
        ins, slots, outs = refs[:nt], refs[nt:2 * nt], refs[2 * nt:3 * nt]
        owns, landed, sums = refs[3 * nt:4 * nt], refs[4 * nt:5 * nt], refs[5 * nt:6 * nt]
        loc, send, recv = refs[6 * nt:]
        x, y, c, me, chips = _place()
        sibling = (x, y, 1 - c)
        loads = []
        for u, (t, r0, rows) in enumerate(units):
            loads.append((
                pltpu.make_async_copy(_rows(_device_piece(ins[t], me, c), r0, rows), _rows(owns[t], r0, rows), loc.at[0, u]),
                pltpu.make_async_copy(_rows(slots[t], r0, rows), _rows(landed[t], r0, rows), loc.at[1, u])))
            for cp in loads[-1]:
                cp.start(priority=1)
        stores = []
        for u, (t, r0, rows) in enumerate(units):
            for cp in loads[u]:
                cp.wait()

            def add(q0, ck, own=owns[t], slot=landed[t], dst=sums[t], r0=r0):
                at = pl.ds(pl.multiple_of(r0 + q0, ck), ck)
                acc = own[at, :].astype(F32)
                for k in range(N_PEERS):
                    acc = acc + slot[k, at, :].astype(F32)
                dst[at, :] = acc

            _for_row_chunks(rows, add)
            mine = _rows(outs[t], c * pieces[t][0] + r0, rows)
            stores.append((
                pltpu.make_async_copy(_rows(sums[t], r0, rows), mine, loc.at[2, u]),
                pltpu.make_async_remote_copy(src_ref=_rows(sums[t], r0, rows), dst_ref=mine, send_sem=send.at[u],
                                             recv_sem=recv.at[u], device_id=sibling, device_id_type=MESH)))
            for cp in stores[-1]:
                cp.start()
        for u, (t, r0, rows) in enumerate(units):
            pltpu.make_async_remote_copy(
                src_ref=_rows(sums[t], r0, rows), dst_ref=_rows(outs[t], (1 - c) * pieces[t][0] + r0, rows),
                send_sem=send.at[u], recv_sem=recv.at[u], device_id=sibling, device_id_type=MESH).wait_recv()
            stores[u][0].wait()
            stores[u][1].wait_send()

    any_spec = pl.BlockSpec(memory_space=pl.ANY)
    return pl.pallas_call(
        body, in_specs=[any_spec] * (2 * nt), out_specs=[any_spec] * nt,
        out_shape=[jax.ShapeDtypeStruct((2 * kh, n4), F32) for kh, n4 in pieces],
        scratch_shapes=[pltpu.VMEM(p, BF16) for p in pieces] + [pltpu.VMEM((N_PEERS,) + p, BF16) for p in pieces]
        + [pltpu.VMEM(p, F32) for p in pieces]
        + [pltpu.SemaphoreType.DMA((3, nu)), pltpu.SemaphoreType.DMA((nu,)), pltpu.SemaphoreType.DMA((nu,))],
        compiler_params=pltpu.CompilerParams(vmem_limit_bytes=VMEM_LIMIT), name=name)(*partials, *lands)


VEC_SHAPE = (8, D_MODEL + LANES)
VEC_SLOTS = dict(norm1_g=(slice(0, 1), slice(0, D_MODEL)), norm2_g=(slice(1, 2), slice(0, D_MODEL)),
                 final_g=(slice(2, 3), slice(0, D_MODEL)), sgu_ln_g=(slice(3, 4), slice(0, SGU_W)),
                 sgu_ln_b=(slice(3, 4), slice(SGU_W, 2 * SGU_W)), b_spatial=(slice(0, 8), slice(D_MODEL, D_MODEL + LANES)),
                 loss=(slice(4, 5), slice(0, LANES)))
VEC_PARAMS = ("norm1_g", "norm2_g", "final_g", "sgu_ln_g", "sgu_ln_b", "b_spatial")
SMALL_PARAMS = VEC_PARAMS + ("w_spatial",)
W_SPATIAL_2D = (SGU_GROUPS * SGU_CHUNK, SGU_CHUNK)


SMALL_GRADS = VEC_PARAMS + ("loss", "w_spatial")


def _small_shape(name):
    if name == "w_spatial":
        return W_SPATIAL_2D
    rows, cols = VEC_SLOTS[name]
    return (rows.stop - rows.start, cols.stop - cols.start)


def _pack_small(dst, parts):
    dst[...] = jnp.zeros(VEC_SHAPE, F32)
    for n, ref in parts.items():
        if n in VEC_SLOTS:
            dst[VEC_SLOTS[n]] = ref[...]


def _small_start(partials):
    names = VEC_PARAMS + ("loss",)

    def body(*refs):
        _pack_small(refs[-1], dict(zip(names, refs[:-1])))

    vec = pl.pallas_call(
        body, in_specs=[VMEM_SPEC] * len(names), out_specs=VMEM_SPEC, out_shape=jax.ShapeDtypeStruct(VEC_SHAPE, F32),
        name="small_params_pack")(*[partials[n].reshape(_small_shape(n)) for n in names])
    srcs = [vec, partials["w_spatial"].reshape(W_SPATIAL_2D)]
    return _split_start("small_params_start", _broadcast_copies, N_PEERS, srcs, [(N_PEERS,) + s.shape for s in srcs])


def _small_finish(started, after, w, m, v):
    own, landed = _split_wait("small_params_wait", _broadcast_copies, *started, after)
    ng, npar = len(SMALL_GRADS), len(SMALL_PARAMS)

    def update_body(*refs):
        vec_own, ws_own, vec_slots, ws_slots = refs[:4]
        w_in, m_in, v_in = (dict(zip(SMALL_PARAMS, refs[4 + k * npar:4 + (k + 1) * npar])) for k in range(3))
        o0 = 4 + 3 * npar
        g_out = dict(zip(SMALL_GRADS, refs[o0:o0 + ng]))
        d_out, m_out, v_out = (dict(zip(SMALL_PARAMS, refs[o0 + ng + k * npar:o0 + ng + (k + 1) * npar])) for k in range(3))
        vg, vw, vm, vv = refs[o0 + ng + 3 * npar:]
        me = 4 * lax.axis_index("x") + 2 * lax.axis_index("y") + lax.axis_index("c")

        def device_sum(mine, slots, read):
            acc = None
            for i in range(N_PEERS + 1):
                k = me ^ i
                part = jnp.where(k == 0, read(mine), read(slots.at[jnp.maximum(k, 1) - 1]))
                acc = part if acc is None else acc + part
            return acc

        vg[...] = device_sum(vec_own, vec_slots, lambda ref: ref[...])
        _pack_small(vw, w_in)
        _pack_small(vm, m_in)
        _pack_small(vv, v_in)
        d_vec, m_vec, v_vec = _adamw_math(vg[...], vw[...], vm[...], vv[...])
        vw[...] = d_vec
        vm[...] = m_vec
        vv[...] = v_vec
        for n in VEC_PARAMS + ("loss",):
            g_out[n][...] = vg[VEC_SLOTS[n]]
        for n in VEC_PARAMS:
            d_out[n][...] = vw[VEC_SLOTS[n]]
            m_out[n][...] = vm[VEC_SLOTS[n]]
            v_out[n][...] = vv[VEC_SLOTS[n]]

        def spatial(r0, ck):
            rows = pl.ds(r0, ck)
            g = device_sum(ws_own, ws_slots, lambda ref: ref[rows, :])
            d_, m_, v_ = _adamw_math(g, w_in["w_spatial"][rows, :], m_in["w_spatial"][rows, :], v_in["w_spatial"][rows, :])
            g_out["w_spatial"][rows, :] = g
            d_out["w_spatial"][rows, :] = d_
            m_out["w_spatial"][rows, :] = m_
            v_out["w_spatial"][rows, :] = v_

        _for_row_chunks(W_SPATIAL_2D[0], spatial)

    ins = list(own) + list(landed)
    for src in (w, m, v):
        ins += [src[n].reshape(_small_shape(n)) for n in SMALL_PARAMS]
    out_shapes = [jax.ShapeDtypeStruct(_small_shape(n), F32) for n in SMALL_GRADS + SMALL_PARAMS * 3]
    outs = pl.pallas_call(
        update_body, in_specs=[VMEM_SPEC] * len(ins), out_specs=[VMEM_SPEC] * len(out_shapes), out_shape=out_shapes,
        scratch_shapes=[pltpu.VMEM(VEC_SHAPE, F32)] * 4, name="small_params_update")(*ins)
    grads = dict(zip(SMALL_GRADS, outs[:ng]))
    rest = [dict(zip(SMALL_PARAMS, outs[ng + k * npar:ng + (k + 1) * npar])) for k in range(3)]
    return grads, rest[0], rest[1], rest[2]


BIG = ("w_in", "w_proj_attn", "w_proj_sgu", "w_out", "w_ffn_gate", "w_ffn_up", "w_ffn_down")
COMM_GROUPS = (("w_in",), ("w_proj_attn", "w_proj_sgu", "w_out", "w_ffn_gate", "w_ffn_up", "w_ffn_down"))
WEIGHTS = ("norm1_g", "w_in", "sgu_ln_g", "sgu_ln_b", "w_spatial", "b_spatial", "w_proj_attn", "w_proj_sgu", "w_out",
           "norm2_g", "w_ffn_gate", "w_ffn_up", "w_ffn_down", "final_g")


def kernel(x, positions, norm1_g, w_in, sgu_ln_g, sgu_ln_b, w_spatial, b_spatial, w_proj_attn, w_proj_sgu, w_out, norm2_g, w_ffn_gate, w_ffn_up, w_ffn_down, final_g, loss_target, m_norm1_g, m_w_in, m_sgu_ln_g, m_sgu_ln_b, m_w_spatial, m_b_spatial, m_w_proj_attn, m_w_proj_sgu, m_w_out, m_norm2_g, m_w_ffn_gate, m_w_ffn_up, m_w_ffn_down, m_final_g, v_norm1_g, v_w_in, v_sgu_ln_g, v_sgu_ln_b, v_w_spatial, v_b_spatial, v_w_proj_attn, v_w_proj_sgu, v_w_out, v_norm2_g, v_w_ffn_gate, v_w_ffn_up, v_w_ffn_down, v_final_g):
    w = dict(norm1_g=norm1_g, w_in=w_in, sgu_ln_g=sgu_ln_g, sgu_ln_b=sgu_ln_b, w_spatial=w_spatial, b_spatial=b_spatial,
             w_proj_attn=w_proj_attn, w_proj_sgu=w_proj_sgu, w_out=w_out, norm2_g=norm2_g, w_ffn_gate=w_ffn_gate,
             w_ffn_up=w_ffn_up, w_ffn_down=w_ffn_down, final_g=final_g)
    m = dict(norm1_g=m_norm1_g, w_in=m_w_in, sgu_ln_g=m_sgu_ln_g, sgu_ln_b=m_sgu_ln_b, w_spatial=m_w_spatial,
             b_spatial=m_b_spatial, w_proj_attn=m_w_proj_attn, w_proj_sgu=m_w_proj_sgu, w_out=m_w_out, norm2_g=m_norm2_g,
             w_ffn_gate=m_w_ffn_gate, w_ffn_up=m_w_ffn_up, w_ffn_down=m_w_ffn_down, final_g=m_final_g)
    v = dict(norm1_g=v_norm1_g, w_in=v_w_in, sgu_ln_g=v_sgu_ln_g, sgu_ln_b=v_sgu_ln_b, w_spatial=v_w_spatial,
             b_spatial=v_b_spatial, w_proj_attn=v_w_proj_attn, w_proj_sgu=v_w_proj_sgu, w_out=v_w_out, norm2_g=v_norm2_g,
             w_ffn_gate=v_w_ffn_gate, w_ffn_up=v_w_ffn_up, w_ffn_down=v_w_ffn_down, final_g=v_final_g)
    t = x.shape[1]

    def cast(n, after):
        flip = jnp.transpose if w[n].shape[-1] % LANES else (lambda a: a)
        return flip(_ew(f"cast_{n}", lambda a: (a,), [flip(w[n][0])], [BF16], after)[0])

    shards = {"w_in": cast("w_in", [])}
    late = COMM_GROUPS[1]
    k_in, n_in = shards["w_in"].shape
    *first, token = _split_start("gather_start_0", _gather_half_copies, 3, [shards["w_in"]], [(3, k_in // 2, n_in)])
    shards.update({n: cast(n, [token]) for n in late})
    pending = {}

    def first_weight(after):
        srcs, filled = _split_wait("gather_wait_0", _gather_half_copies, *first, list(after) + [shards[n] for n in late])
        gath_in, late_shards = lax.optimization_barrier(
            (_gather_finish("gather_finish_0", srcs[0], filled[0]), [shards[n] for n in late]))
        land_shapes = [(s.shape[0], N_CHIPS * s.shape[1]) if n.startswith("w_proj") else (N_CHIPS,) + s.shape
                       for n, s in zip(late, late_shards)]
        *pending["late"], _ = _split_start("gather_start_1", _gather_copies, 4, late_shards, land_shapes)
        return gath_in

    def late_weights(after):
        _, filled = _split_wait("gather_wait_1", _gather_copies, *pending["late"], after)
        gath = dict(zip(late, filled))
        return (gath["w_proj_attn"], gath["w_proj_sgu"],
                gath["w_out"].reshape(D_MODEL, D_MODEL), gath["w_ffn_gate"], gath["w_ffn_up"], gath["w_ffn_down"])

    exchanges = {}

    def on_grads(i, partials):
        if "w_out" in partials:
            partials["w_out"] = partials["w_out"].reshape(N_CHIPS, D_MODEL // N_CHIPS, D_MODEL)
        parts = [partials[n] for n in COMM_GROUPS[i]]
        *exchanges[i], started = _split_start(
            f"rs_exchange_start_{i}", _exchange_copies, N_PEERS, parts, [(N_PEERS,) + _piece_shape(p.shape) for p in parts])
        return started

    dx, _, small = _local_step(
        x[0], positions.reshape(t, 1), loss_target[0], norm1_g + token, sgu_ln_g, sgu_ln_b,
        w_spatial[0], b_spatial[0] + token[:1, :LANES],
        norm2_g, final_g.reshape(1, D_MODEL), first_weight, late_weights, on_grads=on_grads)
    *small_started, small_token = _small_start(small)

    grads = {}
    for i in (1, 0):
        parts, filled = _split_wait(f"rs_exchange_wait_{i}", _exchange_copies, *exchanges[i], small_token)
        grads.update(zip(COMM_GROUPS[i], _device_sum(f"rs_device_sum_{i}", parts, filled)))

    delta, new_m, new_v, updated = {}, {}, {}, []
    for n in BIG:
        shp = w[n].shape
        flip = jnp.transpose if shp[-1] % LANES else (lambda a: a)
        outs = _adamw(f"adamw_{n}", flip(grads[n]), flip(w[n][0]), flip(m[n][0]), flip(v[n][0]))
        grads[n], delta[n], new_m[n], new_v[n] = (flip(a).reshape(shp) for a in outs)
        updated.append(outs[-1])

    g_s, d_s, m_s, v_s = _small_finish(small_started, updated, w, m, v)
    loss = g_s["loss"][0, 0]
    for n in SMALL_PARAMS:
        shp = w[n].shape
        grads[n], delta[n], new_m[n], new_v[n] = (a[n].reshape(shp) for a in (g_s, d_s, m_s, v_s))

    return (loss, dx.reshape(x.shape), *[grads[n] for n in WEIGHTS], *[delta[n] for n in WEIGHTS],
            *[new_m[n] for n in WEIGHTS], *[new_v[n] for n in WEIGHTS])
```

```python
import functools

import numpy as np
import jax
import jax.numpy as jnp
from jax import lax
from jax.experimental import pallas as pl
from jax.experimental.pallas import tpu as pltpu

F32, BF16 = jnp.float32, jnp.bfloat16
MESH = pl.DeviceIdType.MESH

D_MODEL = 1024
HEAD_DIM = 64
ATTN_W = 512
DILATIONS = (1, 4, 16)
BLK = 128
ATTN_BLOCKS_PER_STEP = 4
ROPE_DIM = 16
ROPE_THETA = 500000.0
SGU_W = 512
SGU_CHUNK = 128
SGU_GROUPS = 8
D_FF = 2816
N_CHIPS = 4
FF_SHARD = D_FF // N_CHIPS
IN_COLS = 7680
EPS = 1e-6
NEG = -1e30
LANES = 128
VMEM_LIMIT = 52 * 1024 * 1024

ADAM_LR, ADAM_B1, ADAM_B2, ADAM_EPS, ADAM_WD, ADAM_STEP = 0.001, 0.9, 0.999, 1e-08, 0.01, 10

QKV_BLOCKS = 9


def _w_in_block(part, g):
    return part * len(DILATIONS) + g


def _cparams(ngrid):
    return pltpu.CompilerParams(dimension_semantics=("arbitrary",) * ngrid, vmem_limit_bytes=VMEM_LIMIT)


def _full(shape):
    return pl.BlockSpec(shape, lambda *_: (0,) * len(shape))


def _resident(shape):
    return pl.BlockSpec(shape, lambda *_: (0,) * len(shape), pipeline_mode=pl.Buffered(1))


NT = ((1,), (1,))
TN = ((0,), (0,))


def _rope(v, cos_t, sin_t):
    half = ROPE_DIM // 2
    first = (lax.broadcasted_iota(jnp.int32, cos_t.shape, 1) % HEAD_DIM) < half
    outs = []
    for cs in range(v.shape[1] // LANES):
        x = v[:, cs * LANES:(cs + 1) * LANES]
        partner = jnp.where(first, pltpu.roll(x, LANES - half, axis=1), pltpu.roll(x, half, axis=1))
        outs.append(x * cos_t + partner * sin_t)
    return outs[0] if len(outs) == 1 else jnp.concatenate(outs, axis=1)


def _spread_heads(v2, upper):
    other = pltpu.roll(v2, HEAD_DIM, axis=1)
    h0 = jnp.where(upper, other, v2)
    h1 = jnp.where(upper, v2, other)
    return jnp.concatenate([jnp.concatenate([h0, h0], axis=1), jnp.concatenate([h1, h1], axis=1)], axis=0)


def _sigmoid(v):
    return 0.5 * jnp.tanh(0.5 * v) + 0.5


def _rms_stats(v):
    r = lax.rsqrt(jnp.mean(v * v, axis=-1, keepdims=True) + EPS)
    return v * r, r


def _rms_bwd(dy, xhat, r, g):
    dxh = dy * g
    return r * (dxh - xhat * jnp.mean(dxh * xhat, axis=-1, keepdims=True))


def _head_sum_matrix():
    idx = np.arange(ATTN_W) // HEAD_DIM
    return jnp.asarray((idx[:, None] == idx[None, :]).astype(np.float32), dtype=BF16)


def _group_sum(v, e):
    hi = v.astype(BF16)
    lo = (v - hi.astype(F32)).astype(BF16)
    return jnp.dot(hi, e, preferred_element_type=F32) + jnp.dot(lo, e, preferred_element_type=F32)


TILE = 512


def _to_slabs(slab_ref, v):
    for cs in range(slab_ref.shape[0]):
        slab_ref[cs] = v[:, cs * LANES:(cs + 1) * LANES]


def _from_slabs(slab_ref):
    return jnp.concatenate([slab_ref[cs] for cs in range(slab_ref.shape[0])], axis=1)


def _class_rows(slab_ref, r, dil):
    n = slab_ref.shape[1] // dil
    return jnp.concatenate([slab_ref.at[cs][pl.ds(r, n, stride=dil), :] for cs in range(slab_ref.shape[0])], axis=1)


def _put_class_rows(slab_ref, r, dil, v):
    n = slab_ref.shape[1] // dil
    for cs in range(slab_ref.shape[0]):
        slab_ref.at[cs][pl.ds(r, n, stride=dil), :] = v[:, cs * LANES:(cs + 1) * LANES]


def _natural_from_group(slab_ref, grp_ref):
    dil = grp_ref.shape[0]
    for r in range(dil):
        _put_class_rows(slab_ref, r, dil, grp_ref[r].astype(F32))
    return _from_slabs(slab_ref)


def _group_from_natural(slab_ref, grp_ref, v):
    dil = grp_ref.shape[0]
    _to_slabs(slab_ref, v)
    for r in range(dil):
        grp_ref[r] = _class_rows(slab_ref, r, dil).astype(grp_ref.dtype)


def _group_spec(dil, tile, width):
    return pl.BlockSpec((dil, tile // dil, width), lambda i, *_: (0, i, 0))


def _slabs(tile, width):
    return pltpu.VMEM((width // LANES, tile, LANES), F32)


def _rope_consts():
    lane = np.arange(LANES) % HEAD_DIM
    fi = lane % (ROPE_DIM // 2)
    invf = np.where(lane < ROPE_DIM, ROPE_THETA ** (-(2.0 * fi) / ROPE_DIM), 0.0)
    sgn = np.where(lane < ROPE_DIM // 2, -1.0, np.where(lane < ROPE_DIM, 1.0, 0.0))
    return (jnp.asarray(invf.astype(np.float32)).reshape(1, LANES), jnp.asarray(sgn.astype(np.float32)).reshape(1, LANES))


def _rope_tables(pos_col):
    t = pos_col.shape[0]
    tile = min(t, TILE)
    invf, sgn = _rope_consts()

    def body(p_ref, f_ref, s_ref, c0, s0, c1, s1, c2, s2, slab_c, slab_s):
        ang = p_ref[...].astype(F32) * f_ref[...]
        cos, sin = jnp.cos(ang), jnp.sin(ang) * s_ref[...]
        c0[...] = cos
        s0[...] = sin
        _group_from_natural(slab_c, c1, cos)
        _group_from_natural(slab_s, s1, sin)
        for r in range(DILATIONS[2]):
            c2[r] = _class_rows(slab_c, r, DILATIONS[2])
            s2[r] = _class_rows(slab_s, r, DILATIONS[2])

    nat = pl.BlockSpec((tile, LANES), lambda i: (i, 0))
    specs, shapes = [nat, nat], [(t, LANES)] * 2
    for d in DILATIONS[1:]:
        specs += [_group_spec(d, tile, LANES)] * 2
        shapes += [(d, t // d, LANES)] * 2
    outs = pl.pallas_call(
        body, grid=(t // tile,),
        in_specs=[pl.BlockSpec((tile, 1), lambda i: (i, 0)), _full((1, LANES)), _full((1, LANES))],
        out_specs=specs, out_shape=[jax.ShapeDtypeStruct(s, F32) for s in shapes],
        scratch_shapes=[_slabs(tile, LANES)] * 2,
        compiler_params=_cparams(1), name="rope_tables")(pos_col, invf, sgn)
    return [(outs[2 * g].reshape(t, LANES), outs[2 * g + 1].reshape(t, LANES)) for g in range(len(DILATIONS))]


def _norm_fwd(x, g):
    t = x.shape[0]
    tile = min(t, TILE)

    def body(x_ref, g_ref, h0_ref, h1_ref, h2_ref, slab):
        xhat, _ = _rms_stats(x_ref[...])
        hn = xhat * g_ref[...]
        h0_ref[...] = hn.astype(BF16)
        _group_from_natural(slab, h1_ref, hn)
        for r in range(DILATIONS[2]):
            h2_ref[r] = _class_rows(slab, r, DILATIONS[2]).astype(BF16)

    nat = pl.BlockSpec((tile, D_MODEL), lambda i: (i, 0))
    return pl.pallas_call(
        body, grid=(t // tile,),
        in_specs=[nat, _full((1, D_MODEL))],
        out_specs=[nat] + [_group_spec(d, tile, D_MODEL) for d in DILATIONS[1:]],
        out_shape=[jax.ShapeDtypeStruct((t, D_MODEL), BF16)]
        + [jax.ShapeDtypeStruct((d, t // d, D_MODEL), BF16) for d in DILATIONS[1:]],
        scratch_shapes=[_slabs(tile, D_MODEL)],
        compiler_params=_cparams(1), name="norm1_fwd")(x, g)


GU_COLS = 3072
GROUP_COLS = 1536
GU_HALF = GU_COLS // 2


def _w_in_spec(width, block):
    return pl.BlockSpec((D_MODEL, width), lambda i: (0, block), pipeline_mode=pl.Buffered(1))


def _gu_w_specs():
    first = QKV_BLOCKS * ATTN_W // GU_HALF
    return [_w_in_spec(GU_HALF, first), _w_in_spec(GU_HALF, first + 1)]


def _group_w_specs(g):
    return [_w_in_spec(ATTN_W, _w_in_block(part, g)) for part in range(3)]


def _in_proj(hs, w_in, tables):
    t = hs[0].shape[0]
    tm = min(t, 1024)

    def body_gu(h_ref, w0_ref, w1_ref, o_ref):
        h = h_ref[...]
        o_ref[:, 0:GU_HALF] = jnp.dot(h, w0_ref[...], preferred_element_type=F32).astype(BF16)
        o_ref[:, GU_HALF:] = jnp.dot(h, w1_ref[...], preferred_element_type=F32).astype(BF16)

    gu = _token_call("in_proj_gates_uv", body_gu, t, tm,
                     [(hs[0], _rows_spec(tm, D_MODEL))] + [(w_in, s) for s in _gu_w_specs()],
                     [((t, GU_COLS), BF16, _rows_spec(tm, GU_COLS))])[0]

    qkvs = []
    for g in range(len(DILATIONS)):

        def body_qkv(h_ref, wq_ref, wk_ref, wv_ref, cos_ref, sin_ref, o_ref):
            h = h_ref[...]
            cos_w, sin_w = cos_ref[...], sin_ref[...]
            q = jnp.dot(h, wq_ref[...], preferred_element_type=F32)
            o_ref[:, 0:ATTN_W] = (_rope(q, cos_w, sin_w) * HEAD_DIM ** -0.5).astype(BF16)
            k = jnp.dot(h, wk_ref[...], preferred_element_type=F32)
            o_ref[:, ATTN_W:2 * ATTN_W] = _rope(k, cos_w, sin_w).astype(BF16)
            o_ref[:, 2 * ATTN_W:] = jnp.dot(h, wv_ref[...], preferred_element_type=F32).astype(BF16)

        cos_t, sin_t = tables[g]
        qkvs.append(_token_call(
            f"in_proj_qkv_g{g}", body_qkv, t, tm,
            [(hs[g].reshape(t, D_MODEL), _rows_spec(tm, D_MODEL))] + [(w_in, s) for s in _group_w_specs(g)]
            + [(cos_t, _rows_spec(tm, LANES)), (sin_t, _rows_spec(tm, LANES))],
            [((t, GROUP_COLS), BF16, _rows_spec(tm, GROUP_COLS))])[0])
    return gu, qkvs


def _attn_masks(n):
    row = lax.broadcasted_iota(jnp.int32, (2 * BLK, 2 * BLK), 0) % BLK
    col = lax.broadcasted_iota(jnp.int32, (2 * BLK, 2 * BLK), 1)
    diff = BLK + row - col
    valid = (diff >= 0) & (diff <= BLK) & ((col >= BLK) | (n > 0))
    upper = lax.broadcasted_iota(jnp.int32, (BLK, LANES), 1) >= HEAD_DIM
    return valid, upper


def _stack_heads(v2, upper):
    zero = jnp.zeros_like(v2)
    return jnp.concatenate([jnp.where(upper, zero, v2), jnp.where(upper, v2, zero)], axis=0)


def _unstack_heads(v, upper):
    return jnp.where(upper, v[BLK:], v[:BLK])


def _attn_fwd(qkv, g, dil):
    t = qkv.shape[0]
    length = t // dil
    nb = length // BLK
    per_step = min(nb, ATTN_BLOCKS_PER_STEP)
    view = qkv.reshape(dil, length, GROUP_COLS)

    def body(q_ref, kc_ref, kp_ref, vc_ref, vp_ref, o_ref, l_ref, kwin, vwin):
        n = pl.program_id(1)
        kwin[0:BLK] = kp_ref[...]
        kwin[BLK:] = kc_ref[...]
        vwin[0:BLK] = vp_ref[...]
        vwin[BLK:] = vc_ref[...]

        def block(b, carry):
            valid, upper = _attn_masks(n * per_step + b)
            rows = pl.ds(pl.multiple_of(b * BLK, BLK), BLK)
            window = pl.ds(pl.multiple_of(b * BLK, BLK), 2 * BLK)
            slabs = [slice(p * LANES, (p + 1) * LANES) for p in range(ATTN_W // LANES)]
            ss = [lax.dot_general(_stack_heads(q_ref[rows, sl], upper), kwin[window, sl], (NT, ((), ())),
                                  preferred_element_type=F32) for sl in slabs]
            soft = []
            for s in ss:
                s = jnp.where(valid, s, NEG)
                m = jnp.max(s, axis=1, keepdims=True)
                pe = jnp.exp(s - m)
                soft.append((m, pe, jnp.sum(pe, axis=1, keepdims=True)))
            for sl, (m, pe, den) in zip(slabs, soft):
                o = jnp.dot(pe.astype(BF16), vwin[window, sl], preferred_element_type=F32) / den
                lse = jnp.broadcast_to(m + jnp.log(den), (2 * BLK, LANES))
                o_ref[rows, sl] = _unstack_heads(o, upper).astype(BF16)
                l_ref[rows, sl] = _unstack_heads(lse, upper)
            return carry

        lax.fori_loop(0, per_step, block, 0)

    rows = per_step * BLK
    cur = lambda part: pl.BlockSpec((None, rows, ATTN_W), lambda r, n: (r, n, part))
    prev = lambda part: pl.BlockSpec((None, BLK, ATTN_W), lambda r, n: (r, jnp.maximum(n * per_step - 1, 0), part))
    out_spec = pl.BlockSpec((None, rows, ATTN_W), lambda r, n: (r, n, 0))
    return pl.pallas_call(
        body, grid=(dil, nb // per_step),
        in_specs=[cur(0), cur(1), prev(1), cur(2), prev(2)],
        out_specs=[out_spec, out_spec],
        out_shape=[jax.ShapeDtypeStruct((dil, length, ATTN_W), BF16), jax.ShapeDtypeStruct((dil, length, ATTN_W), F32)],
        scratch_shapes=[pltpu.VMEM((rows + BLK, ATTN_W), BF16)] * 2,
        compiler_params=_cparams(2), name=f"attn_fwd_g{g}")(view, view, view, view, view)


def _alphas(l0, l1, l2):
    m = jnp.maximum(jnp.maximum(l0, l1), l2)
    e0, e1, e2 = jnp.exp(l0 - m), jnp.exp(l1 - m), jnp.exp(l2 - m)
    inv = 1.0 / (e0 + e1 + e2)
    return e0 * inv, e1 * inv, e2 * inv


def _natural_group_values(o_refs, l_refs, slabs):
    os_ = [o_refs[0][0].astype(F32)] + [_natural_from_group(slabs[2 * g - 2], o_refs[g]) for g in (1, 2)]
    ls_ = [l_refs[0][0]] + [_natural_from_group(slabs[2 * g - 1], l_refs[g]) for g in (1, 2)]
    return os_, ls_


def _combine_fwd(os_, ls_):
    t = os_[0].shape[1]
    tile = min(t, TILE)

    def body(o0, o1, o2, l0, l1, l2, a_ref, *slabs):
        ov, lv = _natural_group_values((o0, o1, o2), (l0, l1, l2), slabs)
        a0, a1, a2 = _alphas(*lv)
        a_ref[...] = (a0 * ov[0] + a1 * ov[1] + a2 * ov[2]).astype(BF16)

    specs = [_group_spec(d, tile, ATTN_W) for d in DILATIONS]
    return pl.pallas_call(
        body, grid=(t // tile,), in_specs=specs * 2, out_specs=pl.BlockSpec((tile, ATTN_W), lambda i: (i, 0)),
        out_shape=jax.ShapeDtypeStruct((t, ATTN_W), BF16),
        scratch_shapes=[_slabs(tile, ATTN_W)] * 4,
        compiler_params=_cparams(1), name="combine_fwd")(*os_, *ls_)


def _combine_bwd(dattn, os_, ls_):
    t = dattn.shape[0]
    tile = min(t, TILE)
    e = _head_sum_matrix()

    def body(d_ref, o0, o1, o2, l0, l1, l2, e_ref, do0, do1, do2, c0, c1, c2, *slabs):
        ov, lv = _natural_group_values((o0, o1, o2), (l0, l1, l2), slabs)
        alphas = _alphas(*lv)
        d = d_ref[...]
        attn = alphas[0] * ov[0] + alphas[1] * ov[1] + alphas[2] * ov[2]
        s = _group_sum(d * attn, e_ref[...])
        do0[0] = (alphas[0] * d).astype(BF16)
        c0[0] = -alphas[0] * s
        for g, do_ref, c_ref in ((1, do1, c1), (2, do2, c2)):
            _group_from_natural(slabs[2 * g - 2], do_ref, alphas[g] * d)
            _group_from_natural(slabs[2 * g - 1], c_ref, -alphas[g] * s)

    specs = [_group_spec(d, tile, ATTN_W) for d in DILATIONS]
    shapes = [(d, t // d, ATTN_W) for d in DILATIONS]
    outs = pl.pallas_call(
        body, grid=(t // tile,),
        in_specs=[pl.BlockSpec((tile, ATTN_W), lambda i: (i, 0))] + specs * 2 + [_full((ATTN_W, ATTN_W))],
        out_specs=specs * 2,
        out_shape=[jax.ShapeDtypeStruct(s, BF16) for s in shapes] + [jax.ShapeDtypeStruct(s, F32) for s in shapes],
        scratch_shapes=[_slabs(tile, ATTN_W)] * 4,
        compiler_params=_cparams(1), name="combine_bwd")(dattn, *os_, *ls_, e)
    return outs[:3], outs[3:]


def _attn_bwd(qkv, do, cc, lse, cos_t, sin_t, g, dil):
    t = qkv.shape[0]
    length = t // dil
    nb = length // BLK
    per_step = min(nb, ATTN_BLOCKS_PER_STEP)
    nsteps = nb // per_step
    rows_per_step = per_step * BLK
    qkv_v = qkv.reshape(dil, length, GROUP_COLS)
    cos_v, sin_v = (a.reshape(dil, length, LANES) for a in (cos_t, sin_t))
    scale = HEAD_DIM ** -0.5
    dq_cols, dk_cols, dv_cols = (slice(i * ATTN_W, (i + 1) * ATTN_W) for i in range(3))

    def body(q_ref, kc_ref, kp_ref, vc_ref, vp_ref, do_ref, c_ref, l_ref, cosc, sinc, cosp, sinp,
             out_ref, acc, kwin, vwin, cwin, swin):
        n = pl.program_id(1)

        def one_block(b):
            valid, upper = _attn_masks(n * per_step + b)
            start = b * BLK if isinstance(b, int) else pl.multiple_of(b * BLK, BLK)
            rows, before, window = pl.ds(start, BLK), pl.ds(start, BLK), pl.ds(start, 2 * BLK)
            own = pl.ds(start + BLK, BLK)
            dq_parts, dkp_parts, dkc_parts, dvp_parts, dvc_parts = [], [], [], [], []
            npairs = ATTN_W // LANES
            slabs = [slice(p * LANES, (p + 1) * LANES) for p in range(npairs)]
            qss = [_stack_heads(q_ref[rows, sl], upper) for sl in slabs]
            doss = [_stack_heads(do_ref[rows, sl], upper) for sl in slabs]
            ss = [lax.dot_general(qss[p], kwin[window, slabs[p]], (NT, ((), ())), preferred_element_type=F32) for p in range(npairs)]
            dpvs = [lax.dot_general(doss[p], vwin[window, slabs[p]], (NT, ((), ())), preferred_element_type=F32)
                    for p in range(npairs)]
            pes = [jnp.exp(jnp.where(valid, ss[p], NEG) - _spread_heads(l_ref[rows, slabs[p]], upper)) for p in range(npairs)]
            dss = [(pes[p] * (dpvs[p] + _spread_heads(c_ref[rows, slabs[p]], upper))).astype(BF16) for p in range(npairs)]
            for p in range(npairs):
                qs, dos, ds = qss[p], doss[p], dss[p]
                dq2 = _unstack_heads(jnp.dot(ds, kwin[window, slabs[p]], preferred_element_type=F32), upper)
                dk2 = lax.dot_general(ds, qs, (TN, ((), ())), preferred_element_type=F32)
                dv2 = lax.dot_general(pes[p].astype(BF16), dos, (TN, ((), ())), preferred_element_type=F32)
                dq_parts.append(dq2)
                dkp_parts.append(dk2[:BLK])
                dkc_parts.append(dk2[BLK:])
                dvp_parts.append(dv2[:BLK])
                dvc_parts.append(dv2[BLK:])
            dq = _rope(jnp.concatenate(dq_parts, axis=1) * scale, cwin[own, :], swin[own, :])
            dkc = _rope(jnp.concatenate(dkc_parts, axis=1), cwin[own, :], swin[own, :])
            dkp = _rope(jnp.concatenate(dkp_parts, axis=1), cwin[before, :], swin[before, :])
            return dq, dkp, dkc, jnp.concatenate(dvp_parts, axis=1), jnp.concatenate(dvc_parts, axis=1)

        @pl.when(n < nsteps)
        def _():
            kwin[0:BLK] = kp_ref[...]
            kwin[BLK:] = kc_ref[...]
            vwin[0:BLK] = vp_ref[...]
            vwin[BLK:] = vc_ref[...]
            cwin[0:BLK] = cosp[...]
            cwin[BLK:] = cosc[...]
            swin[0:BLK] = -sinp[...]
            swin[BLK:] = -sinc[...]
            dq, dkp, dkc, dvp, dvc = one_block(0)
            last = slice(rows_per_step - BLK, rows_per_step)

            @pl.when(n > 0)
            def _():
                if per_step > 1:
                    out_ref[0:rows_per_step - BLK, :] = acc[0:rows_per_step - BLK, :].astype(BF16)
                out_ref[last, dq_cols] = acc[last, dq_cols].astype(BF16)
                out_ref[last, dk_cols] = (acc[last, dk_cols] + dkp).astype(BF16)
                out_ref[last, dv_cols] = (acc[last, dv_cols] + dvp).astype(BF16)

            acc[0:BLK, dq_cols] = dq
            acc[0:BLK, dk_cols] = dkc
            acc[0:BLK, dv_cols] = dvc

            def later(b, carry):
                dq, dkp, dkc, dvp, dvc = one_block(b)
                start = pl.multiple_of(b * BLK, BLK)
                before, rows = pl.ds(start - BLK, BLK), pl.ds(start, BLK)
                acc[before, dk_cols] += dkp
                acc[before, dv_cols] += dvp
                acc[rows, dq_cols] = dq
                acc[rows, dk_cols] = dkc
                acc[rows, dv_cols] = dvc
                return carry

            lax.fori_loop(1, per_step, later, 0)

        @pl.when(n == flush_at)
        def _():
            out_ref[...] = acc[...].astype(BF16)

    flush_at = nsteps - 1 if nsteps == 1 else nsteps
    out_lag = 0 if nsteps == 1 else 1
    nc = lambda n: jnp.minimum(n, nsteps - 1)
    npv = lambda n: jnp.maximum(jnp.minimum(n, nsteps - 1) * per_step - 1, 0)
    cur = lambda part: pl.BlockSpec((None, rows_per_step, ATTN_W), lambda r, n: (r, nc(n), part))
    prev = lambda part: pl.BlockSpec((None, BLK, ATTN_W), lambda r, n: (r, npv(n), part))
    row = pl.BlockSpec((None, rows_per_step, ATTN_W), lambda r, n: (r, nc(n), 0))
    tab_c = pl.BlockSpec((None, rows_per_step, LANES), lambda r, n: (r, nc(n), 0))
    tab_p = pl.BlockSpec((None, BLK, LANES), lambda r, n: (r, npv(n), 0))
    out_spec = pl.BlockSpec((None, rows_per_step, GROUP_COLS), lambda r, n: (r, jnp.maximum(n - out_lag, 0), 0))
    out = pl.pallas_call(
        body, grid=(dil, nsteps + out_lag),
        in_specs=[cur(0), cur(1), prev(1), cur(2), prev(2), row, row, row, tab_c, tab_c, tab_p, tab_p],
        out_specs=out_spec,
        out_shape=jax.ShapeDtypeStruct((dil, length, GROUP_COLS), BF16),
        scratch_shapes=[pltpu.VMEM((rows_per_step, GROUP_COLS), F32)]
        + [pltpu.VMEM((rows_per_step + BLK, ATTN_W), BF16)] * 2 + [pltpu.VMEM((rows_per_step + BLK, LANES), F32)] * 2,
        compiler_params=_cparams(2), name=f"attn_bwd_g{g}")(
            qkv_v, qkv_v, qkv_v, qkv_v, qkv_v, do, cc, lse, cos_v, sin_v, cos_v, sin_v)
    return out.reshape(t, GROUP_COLS)


SQRT_HALF = 0.7071067811865476
INV_SQRT_2PI = 0.3989422804014327


def _sgu_core(uv, g, b, w_ref, bias):
    cdf = 0.5 * (1.0 + lax.erf(uv * SQRT_HALF))
    z = uv * cdf
    u, v = z[:, :SGU_W], z[:, SGU_W:]
    mu = jnp.mean(v, axis=1, keepdims=True)
    xc = v - mu
    rs = lax.rsqrt(jnp.mean(xc * xc, axis=1, keepdims=True) + EPS)
    xhat = xc * rs
    vn = xhat * g + b
    row = lax.broadcasted_iota(jnp.int32, (SGU_CHUNK, SGU_CHUNK), 0)
    col = lax.broadcasted_iota(jnp.int32, (SGU_CHUNK, SGU_CHUNK), 1)
    tril = row >= col
    upper = lax.broadcasted_iota(jnp.int32, (SGU_CHUNK, LANES), 1) >= SGU_W // SGU_GROUPS
    ws, vlo, vhi, mixed = [], [], [], []
    for pr in range(SGU_W // LANES):
        sl = slice(pr * LANES, (pr + 1) * LANES)
        w0 = jnp.where(tril, w_ref[2 * pr], 0.0).astype(BF16)
        w1 = jnp.where(tril, w_ref[2 * pr + 1], 0.0).astype(BF16)
        vn2 = vn[:, sl]
        lo = jnp.where(upper, 0.0, vn2).astype(BF16)
        hi = jnp.where(upper, vn2, 0.0).astype(BF16)
        mixed.append(jnp.dot(w0, lo, preferred_element_type=F32) + jnp.dot(w1, hi, preferred_element_type=F32)
                     + bias[:, sl])
        ws.append((w0, w1))
        vlo.append(lo)
        vhi.append(hi)
    return cdf, u, xhat, rs, jnp.concatenate(mixed, axis=1), ws, vlo, vhi, tril, upper


SGU_STEP = 4 * SGU_CHUNK


def _for_chunks(step_rows, fn):
    def one(ci, carry):
        fn(pl.ds(pl.multiple_of(ci * SGU_CHUNK, SGU_CHUNK), SGU_CHUNK))
        return carry

    lax.fori_loop(0, step_rows // SGU_CHUNK, one, 0)


def _sgu_fwd(gu, ln_g, ln_b, w_s, bias_exp):
    t = gu.shape[0]
    step = min(t, SGU_STEP)

    def body(uv_ref, g_ref, b_ref, w_ref, bias_ref, o_ref):
        def chunk(rows):
            _, u, _, _, mixed, *_ = _sgu_core(uv_ref[rows, :].astype(F32), g_ref[...], b_ref[...], w_ref, bias_ref[...])
            o_ref[rows, :] = (u * mixed).astype(BF16)

        _for_chunks(step, chunk)

    return pl.pallas_call(
        body, grid=(t // step,),
        in_specs=[pl.BlockSpec((step, 2 * SGU_W), lambda n: (n, 0)), _full((1, SGU_W)), _full((1, SGU_W)),
                  _full((SGU_GROUPS, SGU_CHUNK, SGU_CHUNK)), _full((SGU_CHUNK, SGU_W))],
        out_specs=pl.BlockSpec((step, SGU_W), lambda n: (n, 0)),
        out_shape=jax.ShapeDtypeStruct((t, SGU_W), BF16),
        compiler_params=_cparams(1), name="sgu_fwd")(gu, ln_g, ln_b, w_s, bias_exp)


def _sgu_bwd(dproj, gu, dsgu, ln_g, ln_b, w_s, bias_exp):
    t = gu.shape[0]
    step = min(t, SGU_STEP)
    nsteps = t // step
    e = _head_sum_matrix()

    def body(dp_in, uv_ref, ds_ref, g_ref, b_ref, w_ref, bias_ref, e_ref, out_ref, dw_ref, dbias_ref, dg_ref, db_ref):
        n = pl.program_id(0)

        @pl.when(n == 0)
        def _():
            dw_ref[...] = jnp.zeros(dw_ref.shape, F32)
            dbias_ref[...] = jnp.zeros(dbias_ref.shape, F32)
            dg_ref[...] = jnp.zeros(dg_ref.shape, F32)
            db_ref[...] = jnp.zeros(db_ref.shape, F32)

        _for_chunks(step, functools.partial(chunk, uv_ref, ds_ref, g_ref, b_ref, w_ref, bias_ref, out_ref, dw_ref, dbias_ref,
                                            dg_ref, db_ref))

        @pl.when(n == nsteps - 1)
        def _():
            dbias_ref[...] = _group_sum(dbias_ref[...], e_ref[...])

    def chunk(uv_ref, ds_ref, g_ref, b_ref, w_ref, bias_ref, out_ref, dw_ref, dbias_ref, dg_ref, db_ref, rows):
        uv = uv_ref[rows, :].astype(F32)
        g = g_ref[...]
        cdf, u, xhat, rs, mixed, ws, vlo, vhi, tril, upper = _sgu_core(uv, g, b_ref[...], w_ref, bias_ref[...])
        dsg = ds_ref[rows, :]
        du = dsg * mixed
        dmixed = dsg * u
        dbias_ref[...] += dmixed
        dvn = []
        for pr in range(SGU_W // LANES):
            sl = slice(pr * LANES, (pr + 1) * LANES)
            dm2 = dmixed[:, sl]
            dlo = jnp.where(upper, 0.0, dm2).astype(BF16)
            dhi = jnp.where(upper, dm2, 0.0).astype(BF16)
            w0, w1 = ws[pr]
            dvn.append(lax.dot_general(w0, dlo, (TN, ((), ())), preferred_element_type=F32)
                       + lax.dot_general(w1, dhi, (TN, ((), ())), preferred_element_type=F32))
            dw0 = lax.dot_general(dlo, vlo[pr], (NT, ((), ())), preferred_element_type=F32)
            dw1 = lax.dot_general(dhi, vhi[pr], (NT, ((), ())), preferred_element_type=F32)
            dw_ref[2 * pr] += jnp.where(tril, dw0, 0.0)
            dw_ref[2 * pr + 1] += jnp.where(tril, dw1, 0.0)
        dvn = jnp.concatenate(dvn, axis=1)
        dg_ref[...] += jnp.sum(dvn * xhat, axis=0, keepdims=True)
        db_ref[...] += jnp.sum(dvn, axis=0, keepdims=True)
        dxh = dvn * g
        dv = rs * (dxh - jnp.mean(dxh, axis=1, keepdims=True) - xhat * jnp.mean(dxh * xhat, axis=1, keepdims=True))
        dz = jnp.concatenate([du, dv], axis=1)
        dgelu = cdf + uv * (INV_SQRT_2PI * jnp.exp(-0.5 * uv * uv))
        out_ref[rows, :] = (dz * dgelu).astype(BF16)

    outs = pl.pallas_call(
        body, grid=(nsteps,),
        in_specs=[pl.BlockSpec(memory_space=pl.ANY), pl.BlockSpec((step, 2 * SGU_W), lambda n: (n, 0)),
                  pl.BlockSpec((step, SGU_W), lambda n: (n, 0)), _full((1, SGU_W)), _full((1, SGU_W)),
                  _full((SGU_GROUPS, SGU_CHUNK, SGU_CHUNK)), _full((SGU_CHUNK, SGU_W)), _full((ATTN_W, ATTN_W))],
        out_specs=[pl.BlockSpec((step, 2 * SGU_W), lambda n: (n, 0)), _full((SGU_GROUPS, SGU_CHUNK, SGU_CHUNK)),
                   _full((SGU_CHUNK, SGU_W)), _full((1, SGU_W)), _full((1, SGU_W))],
        out_shape=[jax.ShapeDtypeStruct(dproj.shape, BF16), jax.ShapeDtypeStruct((SGU_GROUPS, SGU_CHUNK, SGU_CHUNK), F32),
                   jax.ShapeDtypeStruct((SGU_CHUNK, SGU_W), F32), jax.ShapeDtypeStruct((1, SGU_W), F32),
                   jax.ShapeDtypeStruct((1, SGU_W), F32)],
        input_output_aliases={0: 0},
        compiler_params=_cparams(1), name="sgu_bwd")(dproj, gu, dsgu, ln_g, ln_b, w_s, bias_exp, e)
    return outs


def _merge_fwd(attn, sgu, gu, x, w_pa, w_ps, w_out, g2):
    t = x.shape[0]
    tm = min(t, 512)

    def body(a_ref, s_ref, ga_ref, gb_ref, x_ref, wpa, wps, wo, g_ref, pa_ref, ps_ref, m_ref, x1_ref, h2_ref):
        pa = jnp.dot(a_ref[...], wpa[...], preferred_element_type=F32)
        ps = jnp.dot(s_ref[...], wps[...], preferred_element_type=F32)
        merged = (_sigmoid(ga_ref[...].astype(F32)) * pa + _sigmoid(gb_ref[...].astype(F32)) * ps).astype(BF16)
        x1 = x_ref[...] + jnp.dot(merged, wo[...], preferred_element_type=F32)
        xhat, _ = _rms_stats(x1)
        pa_ref[...] = pa.astype(BF16)
        ps_ref[...] = ps.astype(BF16)
        m_ref[...] = merged
        x1_ref[...] = x1
        h2_ref[...] = (xhat * g_ref[...]).astype(BF16)

    half = pl.BlockSpec((tm, ATTN_W), lambda i: (i, 0))
    full = pl.BlockSpec((tm, D_MODEL), lambda i: (i, 0))
    return pl.pallas_call(
        body, grid=(t // tm,),
        in_specs=[half, half, pl.BlockSpec((tm, D_MODEL), lambda i: (i, 1)), pl.BlockSpec((tm, D_MODEL), lambda i: (i, 2)),
                  full, _resident((ATTN_W, D_MODEL)), _resident((SGU_W, D_MODEL)), _resident((D_MODEL, D_MODEL)),
                  _full((1, D_MODEL))],
        out_specs=[full] * 5,
        out_shape=[jax.ShapeDtypeStruct((t, D_MODEL), BF16), jax.ShapeDtypeStruct((t, D_MODEL), BF16),
                   jax.ShapeDtypeStruct((t, D_MODEL), BF16), jax.ShapeDtypeStruct((t, D_MODEL), F32),
                   jax.ShapeDtypeStruct((t, D_MODEL), BF16)],
        compiler_params=_cparams(1), name="merge_fwd")(attn, sgu, gu, gu, x, w_pa, w_ps, w_out, g2)


def _merge_bwd(dx1b, gu, pa, ps, w_pa, w_ps, w_out):
    t = dx1b.shape[0]
    tm = min(t, 512)

    def body(d_ref, ga_ref, gb_ref, pa_ref, ps_ref, wpa, wps, wo, out_ref, dpa_ref, dps_ref, da_ref, dsg_ref):
        dm = lax.dot_general(d_ref[...], wo[...], (NT, ((), ())), preferred_element_type=F32)
        sa, sb = _sigmoid(ga_ref[...].astype(F32)), _sigmoid(gb_ref[...].astype(F32))
        dpa = (dm * sa).astype(BF16)
        dps = (dm * sb).astype(BF16)
        out_ref[:, 0:D_MODEL] = jnp.zeros((tm, D_MODEL), BF16)
        out_ref[:, D_MODEL:2 * D_MODEL] = (dm * pa_ref[...].astype(F32) * sa * (1.0 - sa)).astype(BF16)
        out_ref[:, 2 * D_MODEL:GU_COLS] = (dm * ps_ref[...].astype(F32) * sb * (1.0 - sb)).astype(BF16)
        dpa_ref[...] = dpa
        dps_ref[...] = dps
        da_ref[...] = lax.dot_general(dpa, wpa[...], (NT, ((), ())), preferred_element_type=F32)
        dsg_ref[...] = lax.dot_general(dps, wps[...], (NT, ((), ())), preferred_element_type=F32)

    half = pl.BlockSpec((tm, ATTN_W), lambda i: (i, 0))
    full = pl.BlockSpec((tm, D_MODEL), lambda i: (i, 0))
    return pl.pallas_call(
        body, grid=(t // tm,),
        in_specs=[full, pl.BlockSpec((tm, D_MODEL), lambda i: (i, 1)),
                  pl.BlockSpec((tm, D_MODEL), lambda i: (i, 2)), full, full,
                  _resident((ATTN_W, D_MODEL)), _resident((SGU_W, D_MODEL)), _resident((D_MODEL, D_MODEL))],
        out_specs=[pl.BlockSpec((tm, GU_COLS), lambda i: (i, 0)), full, full, half, half],
        out_shape=[jax.ShapeDtypeStruct((t, GU_COLS), BF16), jax.ShapeDtypeStruct((t, D_MODEL), BF16),
                   jax.ShapeDtypeStruct((t, D_MODEL), BF16), jax.ShapeDtypeStruct((t, ATTN_W), F32),
                   jax.ShapeDtypeStruct((t, SGU_W), F32)],
        compiler_params=_cparams(1), name="merge_bwd")(dx1b, gu, gu, pa, ps, w_pa, w_ps, w_out)


def _token_call(name, body, t, tm, ins, outs, reds=(), scratch=()):
    return pl.pallas_call(
        body, grid=(t // tm,), in_specs=[s for _, s in ins],
        out_specs=[o[2] for o in outs] + [_full(r) for r in reds],
        out_shape=[jax.ShapeDtypeStruct(o[0], o[1]) for o in outs] + [jax.ShapeDtypeStruct(r, F32) for r in reds],
        scratch_shapes=list(scratch), compiler_params=_cparams(1), name=name)(*[a for a, _ in ins])


def _rows_spec(tm, width):
    return pl.BlockSpec((tm, width), lambda i: (i, 0))


def _chips_spec(tm):
    return pl.BlockSpec((N_CHIPS, tm, FF_SHARD), lambda i: (0, i, 0))


def _zero_at_start(*refs):
    @pl.when(pl.program_id(0) == 0)
    def _():
        for r in refs:
            r[...] = jnp.zeros(r.shape, r.dtype)


def _ffn_fwd(h2, w_g, w_u):
    t = h2.shape[0]
    tm = min(t, 512)

    def body(h_ref, wg_ref, wu_ref, fa_ref, fb_ref, ff_ref):
        h = h_ref[...]
        for s in range(N_CHIPS):
            a = jnp.dot(h, wg_ref[s], preferred_element_type=F32)
            b = jnp.dot(h, wu_ref[s], preferred_element_type=F32)
            sg = _sigmoid(a)
            silu = a * sg
            fa_ref[s] = (b * (sg * (1.0 + a * (1.0 - sg)))).astype(BF16)
            fb_ref[s] = silu.astype(BF16)
            ff_ref[s] = (silu * b).astype(BF16)

    shp = (N_CHIPS, t, FF_SHARD)
    w_spec = _resident((N_CHIPS, D_MODEL, FF_SHARD))
    return _token_call("ffn_fwd", body, t, tm, [(h2, _rows_spec(tm, D_MODEL)), (w_g, w_spec), (w_u, w_spec)],
                       [(shp, BF16, _chips_spec(tm))] * 3)


def _ffn_down_loss(ff, w_d, x1, tgt, gf):
    t = x1.shape[0]
    tm = min(t, 512)

    def body(ff_ref, wd_ref, x1_ref, tgt_ref, g_ref, dx2_ref, dx2b_ref, loss_ref, dgf_ref):
        _zero_at_start(loss_ref, dgf_ref)
        acc = jnp.dot(ff_ref[0], wd_ref[0], preferred_element_type=F32)
        for s in range(1, N_CHIPS):
            acc = acc + jnp.dot(ff_ref[s], wd_ref[s], preferred_element_type=F32)
        x2 = x1_ref[...] + acc
        g = g_ref[...]
        xhat, rr = _rms_stats(x2)
        diff = xhat * g - tgt_ref[...]
        rows = jnp.sum(diff * diff, axis=1, keepdims=True)
        loss_ref[...] += jnp.broadcast_to(jnp.sum(rows, axis=0, keepdims=True) * (0.5 / D_MODEL), (1, LANES))
        dy = diff * (1.0 / D_MODEL)
        dgf_ref[...] += jnp.sum(dy * xhat, axis=0, keepdims=True)
        dx2 = _rms_bwd(dy, xhat, rr, g)
        dx2_ref[...] = dx2
        dx2b_ref[...] = dx2.astype(BF16)

    row = _rows_spec(tm, D_MODEL)
    return _token_call("ffn_down_loss", body, t, tm,
                       [(ff, _chips_spec(tm)), (w_d, _resident((N_CHIPS, FF_SHARD, D_MODEL))), (x1, row), (tgt, row),
                        (gf, _full((1, D_MODEL)))],
                       [((t, D_MODEL), F32, row), ((t, D_MODEL), BF16, row)], reds=[(1, LANES), (1, D_MODEL)])


def _ffn_bwd_act(dx2b, w_d, fa, fb):
    t = dx2b.shape[0]
    tm = min(t, 512)

    def body(d_ref, wd_ref, fa_ref, fb_ref, da_ref, db_ref):
        d = d_ref[...]
        for s in range(N_CHIPS):
            dff = lax.dot_general(d, wd_ref[s], (NT, ((), ())), preferred_element_type=F32)
            da_ref[s] = (dff * fa_ref[s].astype(F32)).astype(BF16)
            db_ref[s] = (dff * fb_ref[s].astype(F32)).astype(BF16)

    shp = (N_CHIPS, t, FF_SHARD)
    return _token_call("ffn_bwd_act", body, t, tm,
                       [(dx2b, _rows_spec(tm, D_MODEL)), (w_d, _resident((N_CHIPS, FF_SHARD, D_MODEL))),
                        (fa, _chips_spec(tm)), (fb, _chips_spec(tm))],
                       [(shp, BF16, _chips_spec(tm))] * 2)


def _ffn_bwd_in(da, db, w_g, w_u, x1, dx2, g2):
    t = x1.shape[0]
    tm = min(t, 512)

    def body(da_ref, db_ref, wg_ref, wu_ref, x1_ref, dx2_ref, g_ref, dx1_ref, dx1b_ref, dg_ref):
        _zero_at_start(dg_ref)
        acc = None
        for s in range(N_CHIPS):
            part = (lax.dot_general(da_ref[s], wg_ref[s], (NT, ((), ())), preferred_element_type=F32)
                    + lax.dot_general(db_ref[s], wu_ref[s], (NT, ((), ())), preferred_element_type=F32))
            acc = part if acc is None else acc + part
        xhat, rr = _rms_stats(x1_ref[...])
        dg_ref[...] += jnp.sum(acc * xhat, axis=0, keepdims=True)
        dx1 = dx2_ref[...] + _rms_bwd(acc, xhat, rr, g_ref[...])
        dx1_ref[...] = dx1
        dx1b_ref[...] = dx1.astype(BF16)

    row = _rows_spec(tm, D_MODEL)
    w_spec = _resident((N_CHIPS, D_MODEL, FF_SHARD))
    return _token_call("ffn_bwd_in", body, t, tm,
                       [(da, _chips_spec(tm)), (db, _chips_spec(tm)), (w_g, w_spec), (w_u, w_spec), (x1, row), (dx2, row),
                        (g2, _full((1, D_MODEL)))],
                       [((t, D_MODEL), F32, row), ((t, D_MODEL), BF16, row)], reds=[(1, D_MODEL)])


def _group_dh(d, w_refs):
    dh = None
    for part, w_ref in enumerate(w_refs):
        term = lax.dot_general(d[:, part * ATTN_W:(part + 1) * ATTN_W], w_ref[...], (NT, ((), ())),
                               preferred_element_type=F32)
        dh = term if dh is None else dh + term
    return dh


def _in_proj_bwd(dgu, dqkvs, w_in, x, dx1, g1):
    t = x.shape[0]
    tile = min(t, TILE)
    ngroups = len(DILATIONS)

    def body(*refs):
        dgu_ref, dq_refs = refs[0], refs[1:1 + ngroups]
        w0_ref, w1_ref = refs[1 + ngroups:3 + ngroups]
        wg_refs = [refs[3 + ngroups + 3 * g:6 + ngroups + 3 * g] for g in range(ngroups)]
        x_ref, dx1_ref, g_ref, dx_ref, dg_ref = refs[3 + 4 * ngroups:5 + 4 * ngroups + 3]
        slabs = refs[5 + 4 * ngroups + 3:]
        _zero_at_start(dg_ref)
        for g in range(1, ngroups):
            dil = DILATIONS[g]
            part = _group_dh(dq_refs[g][...].reshape(tile, GROUP_COLS), wg_refs[g])
            for r in range(dil):
                _put_class_rows(slabs[g - 1], r, dil, part[r * (tile // dil):(r + 1) * (tile // dil)])
        dh = lax.dot_general(dgu_ref[:, 0:GU_HALF], w0_ref[...], (NT, ((), ())), preferred_element_type=F32)
        dh = dh + lax.dot_general(dgu_ref[:, GU_HALF:], w1_ref[...], (NT, ((), ())), preferred_element_type=F32)
        dh = dh + _group_dh(dq_refs[0][0], wg_refs[0])
        for slab in slabs:
            dh = dh + _from_slabs(slab)
        xhat, rr = _rms_stats(x_ref[...])
        dg_ref[...] += jnp.sum(dh * xhat, axis=0, keepdims=True)
        dx_ref[...] = dx1_ref[...] + _rms_bwd(dh, xhat, rr, g_ref[...])

    row = _rows_spec(tile, D_MODEL)
    group_ins = [(dqkvs[g].reshape(d, t // d, GROUP_COLS), _group_spec(d, tile, GROUP_COLS)) for g, d in enumerate(DILATIONS)]
    w_specs = _gu_w_specs() + [s for g in range(ngroups) for s in _group_w_specs(g)]
    return _token_call(
        "in_proj_bwd", body, t, tile,
        [(dgu, _rows_spec(tile, GU_COLS))] + group_ins + [(w_in, s) for s in w_specs]
        + [(x, row), (dx1, row), (g1, _full((1, D_MODEL)))],
        [((t, D_MODEL), F32, row)], reds=[(1, D_MODEL)], scratch=[_slabs(tile, D_MODEL)] * (ngroups - 1))


WGRAD_TK = 2048


def _wgrad_mm(name, grid, a, a_spec, b, b_spec, acc_shape, out_shape, out_spec, dst=None):
    nk = grid[-1]

    def body(*refs):
        a_ref, b_ref, o_ref, acc_ref = refs[0], refs[1], refs[-2], refs[-1]
        k = pl.program_id(len(grid) - 1)
        part = lax.dot_general(a_ref[...], b_ref[...], (TN, ((), ())), preferred_element_type=F32)

        @pl.when(k == 0)
        def _():
            acc_ref[...] = part

        @pl.when(k > 0)
        def _():
            acc_ref[...] += part

        @pl.when(k == nk - 1)
        def _():
            o_ref[...] = acc_ref[...].astype(BF16)

    filled = [] if dst is None else [dst]
    return pl.pallas_call(
        body, grid=grid, in_specs=[a_spec, b_spec] + [pl.BlockSpec(memory_space=pl.ANY)] * len(filled),
        out_specs=out_spec, out_shape=jax.ShapeDtypeStruct(out_shape, BF16), scratch_shapes=[pltpu.VMEM(acc_shape, F32)],
        input_output_aliases={2: 0} if filled else {}, compiler_params=_cparams(len(grid)), name=name)(a, b, *filled)


def _wgrad_2d(name, a, b, tm, tn):
    t, k1 = a.shape
    n = b.shape[1]
    tk = min(t, WGRAD_TK)
    return _wgrad_mm(name, (k1 // tm, n // tn, t // tk), a, pl.BlockSpec((tk, tm), lambda i, j, k: (k, i)),
                     b, pl.BlockSpec((tk, tn), lambda i, j, k: (k, j)), (tm, tn), (k1, n),
                     pl.BlockSpec((tm, tn), lambda i, j, k: (i, j)))


def _wgrad_in(hs, dgu, dqkvs):
    t = dgu.shape[0]
    tk = min(t, WGRAD_TK)
    gu_block = QKV_BLOCKS * ATTN_W // GU_HALF
    parts = [(hs[0], dgu, GU_HALF, lambda j: j + gu_block)]
    parts += [(hs[g].reshape(t, D_MODEL), dqkvs[g], ATTN_W, lambda j, g=g: _w_in_block(j, g)) for g in range(3)]
    dst = None
    for n, (a, b, tn, block_of) in enumerate(parts):
        dst = _wgrad_mm(f"wgrad_in_{n}", (1, b.shape[1] // tn, t // tk),
                        a, pl.BlockSpec((tk, D_MODEL), lambda i, j, k: (k, 0)), b, pl.BlockSpec((tk, tn), lambda i, j, k: (k, j)),
                        (D_MODEL, tn), (D_MODEL, IN_COLS),
                        pl.BlockSpec((D_MODEL, tn), lambda i, j, k, block_of=block_of: (0, block_of(j))), dst=dst)
    return dst


def _wgrad_ff_in(name, h2, da):
    t = h2.shape[0]
    tk = min(t, WGRAD_TK)
    return _wgrad_mm(name, (N_CHIPS, 1, t // tk), h2, pl.BlockSpec((tk, D_MODEL), lambda i, j, k: (k, 0)),
                     da, pl.BlockSpec((None, tk, FF_SHARD), lambda i, j, k: (i, k, 0)), (D_MODEL, FF_SHARD),
                     (N_CHIPS, D_MODEL, FF_SHARD), pl.BlockSpec((None, D_MODEL, FF_SHARD), lambda i, j, k: (i, 0, 0)))


def _wgrad_ff_down(ff, dx2b):
    t = dx2b.shape[0]
    tk = min(t, WGRAD_TK)
    return _wgrad_mm("wgrad_ffn_down", (N_CHIPS, 1, t // tk), ff, pl.BlockSpec((None, tk, FF_SHARD), lambda i, j, k: (i, k, 0)),
                     dx2b, pl.BlockSpec((tk, D_MODEL), lambda i, j, k: (k, 0)), (FF_SHARD, D_MODEL),
                     (N_CHIPS, FF_SHARD, D_MODEL), pl.BlockSpec((None, FF_SHARD, D_MODEL), lambda i, j, k: (i, 0, 0)))


def _local_step(x, pos_col, tgt, g1, ln_g, ln_b, w_s, b_s, g2, gf, first_weight, late_weights, on_grads=None):
    tables = _rope_tables(pos_col)
    bias_exp = jnp.repeat(jnp.transpose(b_s), SGU_W // SGU_GROUPS, axis=1)

    hs = _norm_fwd(x, g1)
    w_p = first_weight([hs[0], bias_exp] + [table for pair in tables for table in pair])
    gu, qkvs = _in_proj(hs, w_p, tables)
    os_, ls_ = [], []
    for g, dil in enumerate(DILATIONS):
        o, lse = _attn_fwd(qkvs[g], g, dil)
        os_.append(o)
        ls_.append(lse)
    attn = _combine_fwd(os_, ls_)
    sgu = _sgu_fwd(gu, ln_g, ln_b, w_s, bias_exp)
    w_pa, w_ps, w_out, w_g, w_u, w_d = late_weights(attn)
    pa, ps, merged, x1, h2 = _merge_fwd(attn, sgu, gu, x, w_pa, w_ps, w_out, g2)
    fa, fb, ff = _ffn_fwd(h2, w_g, w_u)
    dx2, dx2b, loss, dgf = _ffn_down_loss(ff, w_d, x1, tgt, gf)

    da, db = _ffn_bwd_act(dx2b, w_d, fa, fb)
    dw_d = _wgrad_ff_down(ff, dx2b)
    dx1, dx1b, dg2 = _ffn_bwd_in(da, db, w_g, w_u, x1, dx2, g2)
    dw_g = _wgrad_ff_in("wgrad_ffn_gate", h2, da)
    dw_u = _wgrad_ff_in("wgrad_ffn_up", h2, db)

    dgu, dpa, dps, dattn, dsgu = _merge_bwd(dx1b, gu, pa, ps, w_pa, w_ps, w_out)
    dw_out = _wgrad_2d("wgrad_out", merged, dx1b, D_MODEL, D_MODEL)
    dw_pa = _wgrad_2d("wgrad_proj_attn", attn, dpa, ATTN_W, D_MODEL)
    dw_ps = _wgrad_2d("wgrad_proj_sgu", sgu, dps, SGU_W, D_MODEL)
    if on_grads is not None:
        ln_g = ln_g + on_grads(1, dict(w_proj_attn=dw_pa, w_proj_sgu=dw_ps, w_out=dw_out, w_ffn_gate=dw_g, w_ffn_up=dw_u,
                                       w_ffn_down=dw_d))[:, :SGU_W]
    dgu, dw_s, dbias, dln_g, dln_b = _sgu_bwd(dgu, gu, dsgu, ln_g, ln_b, w_s, bias_exp)
    dos, ccs = _combine_bwd(dattn, os_, ls_)
    dqkvs = [_attn_bwd(qkvs[g], dos[g], ccs[g], ls_[g], *tables[g], g, dil) for g, dil in enumerate(DILATIONS)]
    dw_p = _wgrad_in(hs, dgu, dqkvs)
    if on_grads is not None:
        g1 = g1 + on_grads(0, dict(w_in=dw_p))
    dx, dg1 = _in_proj_bwd(dgu, dqkvs, w_p, x, dx1, g1)

    db_s = jnp.transpose(dbias[:, ::SGU_W // SGU_GROUPS])
    small = dict(loss=loss, norm1_g=dg1, sgu_ln_g=dln_g, sgu_ln_b=dln_b, w_spatial=dw_s, b_spatial=db_s,
                 norm2_g=dg2, final_g=dgf)
    big = dict(w_in=dw_p, w_proj_attn=dw_pa, w_proj_sgu=dw_ps, w_out=dw_out, w_ffn_gate=dw_g, w_ffn_up=dw_u,
               w_ffn_down=dw_d)
    return dx, big, small


def _ew(name, fn, ins, out_dtypes, after=()):
    shp = ins[0].shape
    rows, cols = shp
    tr = next((cand for cand in (256, 352, 128) if rows % cand == 0 and rows > cand), rows)

    def body(*refs):
        res = fn(*[r[...] for r in refs[:len(ins)]])
        for o_ref, v in zip(refs[len(ins) + len(after):], res):
            o_ref[...] = v.astype(o_ref.dtype)

    spec = pl.BlockSpec((tr, cols), lambda i: (i, 0))
    return pl.pallas_call(
        body, grid=(rows // tr,), in_specs=[spec] * len(ins) + [pl.BlockSpec(memory_space=pl.ANY)] * len(after),
        out_specs=[spec] * len(out_dtypes), out_shape=[jax.ShapeDtypeStruct(shp, d) for d in out_dtypes],
        compiler_params=_cparams(1), name=name)(*ins, *after)


def _adamw_math(g, w, m, v):
    m = ADAM_B1 * m + (1.0 - ADAM_B1) * g
    v = ADAM_B2 * v + (1.0 - ADAM_B2) * (g * g)
    m_hat = m / (1.0 - ADAM_B1 ** ADAM_STEP)
    v_hat = v / (1.0 - ADAM_B2 ** ADAM_STEP)
    delta = -ADAM_LR * (m_hat / (jnp.sqrt(v_hat) + ADAM_EPS) + ADAM_WD * w)
    return delta, m, v


def _adamw(name, g, w, m, v):
    return _ew(name, lambda g_, w_, m_, v_: (g_,) + _adamw_math(g_, w_, m_, v_), [g, w, m, v], [F32] * 4)


VMEM_SPEC = pl.BlockSpec(memory_space=pltpu.VMEM)


def _for_row_chunks(rows, fn):
    ck = next(c for c in (64, 32, 16) if rows % c == 0)

    def step(i, carry):
        fn(pl.multiple_of(i * ck, ck), ck)
        return carry

    lax.fori_loop(0, rows // ck, step, 0)


def _place():
    x, y, c = lax.axis_index("x"), lax.axis_index("y"), lax.axis_index("c")
    chips = [(1 - x, y), (x, 1 - y), (1 - x, 1 - y)]
    return x, y, c, 2 * x + y, chips


def _rows(ref, start, size):
    if len(ref.shape) == 2:
        return ref.at[pl.ds(start, size), :]
    return ref.at[:, pl.ds(start, size), :]


def _gather_finish(name, shard, landed):
    k_rows, n = shard.shape
    kh = k_rows // 2

    def body(shard_hbm, land_hbm, out_ref, shard_ref, land_ref, loc, send, recv):
        x, y, c, me, chips = _place()
        sibling = (x, y, 1 - c)

        def window(core, chip):
            return out_ref.at[pl.ds(core * kh, kh), pl.ds(pl.multiple_of(chip * n, LANES), n)]

        loads = [pltpu.make_async_copy(land_hbm.at[j], land_ref.at[j], loc.at[0, j]) for j in range(3)]
        loads.append(pltpu.make_async_copy(shard_hbm, shard_ref, loc.at[0, 3]))
        for cp in loads[:3]:
            cp.start()
        copies, passed = [], []
        for j, chip in enumerate(chips):
            mine = window(c, 2 * chip[0] + chip[1])
            loads[j].wait()
            copies.append(pltpu.make_async_copy(land_ref.at[j], mine, loc.at[1, j]))
            passed.append(pltpu.make_async_remote_copy(src_ref=land_ref.at[j], dst_ref=mine, send_sem=send.at[j],
                                                       recv_sem=recv.at[j], device_id=sibling, device_id_type=MESH))
            passed[-1].start()
            copies[-1].start()
            if j == 1:
                loads[3].start()
        loads[3].wait()
        copies.append(pltpu.make_async_copy(shard_ref, out_ref.at[:, pl.ds(pl.multiple_of(me * n, LANES), n)], loc.at[1, 3]))
        copies[-1].start()
        for j, chip in enumerate(chips):
            pltpu.make_async_remote_copy(src_ref=land_ref.at[j], dst_ref=window(1 - c, 2 * chip[0] + chip[1]), send_sem=send.at[j],
                                         recv_sem=recv.at[j], device_id=sibling, device_id_type=MESH).wait_recv()
        for cp in copies:
            cp.wait()
        for cp in passed:
            cp.wait_send()

    any_spec = pl.BlockSpec(memory_space=pl.ANY)
    return pl.pallas_call(
        body, in_specs=[any_spec] * 2, out_specs=any_spec,
        out_shape=jax.ShapeDtypeStruct((k_rows, N_CHIPS * n), shard.dtype),
        scratch_shapes=[pltpu.VMEM(shard.shape, shard.dtype), pltpu.VMEM(landed.shape, landed.dtype),
                        pltpu.SemaphoreType.DMA((2, 4)), pltpu.SemaphoreType.DMA((3,)), pltpu.SemaphoreType.DMA((3,))],
        compiler_params=pltpu.CompilerParams(vmem_limit_bytes=VMEM_LIMIT), name=name)(shard, landed)


HBM_SPEC = pl.BlockSpec(memory_space=pltpu.HBM)
SEM_SPEC = pl.BlockSpec(memory_space=pltpu.SEMAPHORE)
DATAFLOW = pltpu.SideEffectType.DATAFLOW_SIDE_EFFECTING
TOKEN_SHAPE = (1, D_MODEL)
N_PEERS = 7
SUM_SPLIT = 4
SUM_SPLIT_ELEMS = 512 * 1024


def _peers():
    x, y, c = lax.axis_index("x"), lax.axis_index("y"), lax.axis_index("c")
    flip = lambda v, f: 1 - v if f else v
    return [(flip(x, k & 4), flip(y, k & 2), flip(c, k & 1)) for k in range(1, N_PEERS + 1)]


def _piece_shape(shape):
    return (shape[-2] // 2, shape[2] if len(shape) == 3 else shape[1] // N_CHIPS)


def _device_piece(ref, chip, core):
    kh, n4 = _piece_shape(ref.shape)
    if len(ref.shape) == 3:
        return ref.at[chip, pl.ds(core * kh, kh), :]
    return ref.at[pl.ds(core * kh, kh), pl.ds(chip * n4, n4)]


def _exchange_copies(partials, lands, send, recv):
    return [pltpu.make_async_remote_copy(
        src_ref=_device_piece(partials[t], 2 * px + py, pc), dst_ref=lands[t].at[k], send_sem=send.at[t * N_PEERS + k],
        recv_sem=recv.at[t * N_PEERS + k], device_id=(px, py, pc), device_id_type=MESH)
        for t in range(len(partials)) for k, (px, py, pc) in enumerate(_peers())]


def _broadcast_copies(srcs, lands, send, recv):
    return [pltpu.make_async_remote_copy(
        src_ref=srcs[t], dst_ref=lands[t].at[k], send_sem=send.at[t * N_PEERS + k], recv_sem=recv.at[t * N_PEERS + k],
        device_id=peer, device_id_type=MESH)
        for t in range(len(srcs)) for k, peer in enumerate(_peers())]


class _LocalCopy:
    def __init__(self, src_ref, dst_ref, sem):
        self.copy = pltpu.make_async_copy(src_ref, dst_ref, sem)

    def start(self):
        self.copy.start()

    def wait_send(self):
        self.copy.wait()

    def wait_recv(self):
        pass


def _gather_copies(shards, lands, send, recv):
    x, y, c, me, chips = _place()
    copies = []
    for t in range(len(shards)):
        n = shards[t].shape[1]
        place = lands[t].at[me] if len(lands[t].shape) == 3 else lands[t].at[:, pl.ds(pl.multiple_of(me * n, LANES), n)]
        copies += [pltpu.make_async_remote_copy(
            src_ref=shards[t], dst_ref=place, send_sem=send.at[t * 4 + j], recv_sem=recv.at[t * 4 + j],
            device_id=(*chip, c), device_id_type=MESH) for j, chip in enumerate(chips)]
        copies.append(_LocalCopy(shards[t], place, send.at[t * 4 + 3]))
    return copies


def _gather_half_copies(shards, lands, send, recv):
    x, y, c, me, chips = _place()
    return [pltpu.make_async_remote_copy(
        src_ref=_rows(shards[t], c * (shards[t].shape[0] // 2), shards[t].shape[0] // 2), dst_ref=lands[t].at[j],
        send_sem=send.at[t * 3 + j], recv_sem=recv.at[t * 3 + j], device_id=(*chip, c), device_id_type=MESH)
        for t in range(len(shards)) for j, chip in enumerate(chips)]


def _split_start(name, copies, per_tensor, srcs, land_shapes):
    nt = len(srcs)
    lands = [lax.empty(s, a.dtype) for s, a in zip(land_shapes, srcs)]
    nsem = nt * per_tensor

    def body(*refs):
        send, recv = refs[2 * nt], refs[2 * nt + 1]
        for cp in copies(refs[:nt], refs[nt:2 * nt], send, recv):
            cp.start()
        refs[-1][...] = jnp.zeros(TOKEN_SHAPE, F32)

    hbm = lambda a: pltpu.with_memory_space_constraint(a, pltpu.HBM)
    outs = pl.pallas_call(
        body, name=name,
        out_shape=[pltpu.SemaphoreType.DMA((nsem,)), pltpu.SemaphoreType.DMA((nsem,))]
        + [pltpu.HBM(s.shape, s.dtype) for s in srcs] + [pltpu.HBM(l.shape, l.dtype) for l in lands]
        + [jax.ShapeDtypeStruct(TOKEN_SHAPE, F32)],
        in_specs=[HBM_SPEC] * (2 * nt), out_specs=[SEM_SPEC, SEM_SPEC] + [HBM_SPEC] * (2 * nt) + [VMEM_SPEC],
        input_output_aliases={i: 2 + i for i in range(2 * nt)},
        compiler_params=pltpu.CompilerParams(has_side_effects=DATAFLOW))(*[hbm(a) for a in list(srcs) + lands])
    return outs[0], outs[1], outs[2:2 + nt], outs[2 + nt:2 + 2 * nt], outs[-1]


def _split_wait(name, copies, send, recv, srcs, lands, after):
    nt = len(srcs)
    after = list(after) if isinstance(after, (list, tuple)) else [after]

    def body(*refs):
        for cp in copies(refs[:nt], refs[nt:2 * nt], refs[2 * nt], refs[2 * nt + 1]):
            cp.wait_send()
            cp.wait_recv()

    outs = pl.pallas_call(
        body, name=name,
        out_shape=[pltpu.HBM(s.shape, s.dtype) for s in srcs] + [pltpu.HBM(l.shape, l.dtype) for l in lands],
        in_specs=[HBM_SPEC] * (2 * nt) + [SEM_SPEC, SEM_SPEC] + [pl.BlockSpec(memory_space=pl.ANY)] * len(after),
        out_specs=[HBM_SPEC] * (2 * nt), input_output_aliases={i: i for i in range(2 * nt)},
        compiler_params=pltpu.CompilerParams(has_side_effects=DATAFLOW))(*srcs, *lands, send, recv, *after)
    return outs[:nt], outs[nt:]


def _device_sum(name, partials, lands):
    nt = len(partials)
    pieces = [_piece_shape(p.shape) for p in partials]
    units = []
    for t, (kh, n4) in enumerate(pieces):
        split = SUM_SPLIT if kh * n4 >= SUM_SPLIT_ELEMS else 1
        units += [(t, j * (kh // split), kh // split) for j in range(split)]
    nu = len(units)

    def body(*refs):
        ins, slots, outs = refs[:nt], refs[nt:2 * nt], refs[2 * nt:3 * nt]
        owns, landed, sums = refs[3 * nt:4 * nt], refs[4 * nt:5 * nt], refs[5 * nt:6 * nt]
        loc, send, recv = refs[6 * nt:]
        x, y, c, me, chips = _place()
        sibling = (x, y, 1 - c)
        loads = []
        for u, (t, r0, rows) in enumerate(units):
            loads.append((
                pltpu.make_async_copy(_rows(_device_piece(ins[t], me, c), r0, rows), _rows(owns[t], r0, rows), loc.at[0, u]),
                pltpu.make_async_copy(_rows(slots[t], r0, rows), _rows(landed[t], r0, rows), loc.at[1, u])))
            for cp in loads[-1]:
                cp.start()
        stores = []
        for u, (t, r0, rows) in enumerate(units):
            for cp in loads[u]:
                cp.wait()

            def add(q0, ck, own=owns[t], slot=landed[t], dst=sums[t], r0=r0):
                at = pl.ds(pl.multiple_of(r0 + q0, ck), ck)
                acc = own[at, :].astype(F32)
                for k in range(N_PEERS):
                    acc = acc + slot[k, at, :].astype(F32)
                dst[at, :] = acc

            _for_row_chunks(rows, add)
            mine = _rows(outs[t], c * pieces[t][0] + r0, rows)
            stores.append((
                pltpu.make_async_copy(_rows(sums[t], r0, rows), mine, loc.at[2, u]),
                pltpu.make_async_remote_copy(src_ref=_rows(sums[t], r0, rows), dst_ref=mine, send_sem=send.at[u],
                                             recv_sem=recv.at[u], device_id=sibling, device_id_type=MESH)))
            for cp in stores[-1]:
                cp.start()
        for u, (t, r0, rows) in enumerate(units):
            pltpu.make_async_remote_copy(
                src_ref=_rows(sums[t], r0, rows), dst_ref=_rows(outs[t], (1 - c) * pieces[t][0] + r0, rows),
                send_sem=send.at[u], recv_sem=recv.at[u], device_id=sibling, device_id_type=MESH).wait_recv()
            stores[u][0].wait()
            stores[u][1].wait_send()

    any_spec = pl.BlockSpec(memory_space=pl.ANY)
    return pl.pallas_call(
        body, in_specs=[any_spec] * (2 * nt), out_specs=[any_spec] * nt,
        out_shape=[jax.ShapeDtypeStruct((2 * kh, n4), F32) for kh, n4 in pieces],
        scratch_shapes=[pltpu.VMEM(p, BF16) for p in pieces] + [pltpu.VMEM((N_PEERS,) + p, BF16) for p in pieces]
        + [pltpu.VMEM(p, F32) for p in pieces]
        + [pltpu.SemaphoreType.DMA((3, nu)), pltpu.SemaphoreType.DMA((nu,)), pltpu.SemaphoreType.DMA((nu,))],
        compiler_params=pltpu.CompilerParams(vmem_limit_bytes=VMEM_LIMIT), name=name)(*partials, *lands)


VEC_SHAPE = (8, D_MODEL + LANES)
VEC_SLOTS = dict(norm1_g=(slice(0, 1), slice(0, D_MODEL)), norm2_g=(slice(1, 2), slice(0, D_MODEL)),
                 final_g=(slice(2, 3), slice(0, D_MODEL)), sgu_ln_g=(slice(3, 4), slice(0, SGU_W)),
                 sgu_ln_b=(slice(3, 4), slice(SGU_W, 2 * SGU_W)), b_spatial=(slice(0, 8), slice(D_MODEL, D_MODEL + LANES)),
                 loss=(slice(4, 5), slice(0, LANES)))
VEC_PARAMS = ("norm1_g", "norm2_g", "final_g", "sgu_ln_g", "sgu_ln_b", "b_spatial")
SMALL_PARAMS = VEC_PARAMS + ("w_spatial",)
W_SPATIAL_2D = (SGU_GROUPS * SGU_CHUNK, SGU_CHUNK)


SMALL_GRADS = VEC_PARAMS + ("loss", "w_spatial")


def _small_shape(name):
    if name == "w_spatial":
        return W_SPATIAL_2D
    rows, cols = VEC_SLOTS[name]
    return (rows.stop - rows.start, cols.stop - cols.start)


def _pack_small(dst, parts):
    dst[...] = jnp.zeros(VEC_SHAPE, F32)
    for n, ref in parts.items():
        if n in VEC_SLOTS:
            dst[VEC_SLOTS[n]] = ref[...]


def _small_start(partials):
    names = VEC_PARAMS + ("loss",)

    def body(*refs):
        _pack_small(refs[-1], dict(zip(names, refs[:-1])))

    vec = pl.pallas_call(
        body, in_specs=[VMEM_SPEC] * len(names), out_specs=VMEM_SPEC, out_shape=jax.ShapeDtypeStruct(VEC_SHAPE, F32),
        name="small_params_pack")(*[partials[n].reshape(_small_shape(n)) for n in names])
    srcs = [vec, partials["w_spatial"].reshape(W_SPATIAL_2D)]
    return _split_start("small_params_start", _broadcast_copies, N_PEERS, srcs, [(N_PEERS,) + s.shape for s in srcs])


def _small_finish(started, after, w, m, v):
    own, landed = _split_wait("small_params_wait", _broadcast_copies, *started, after)
    ng, npar = len(SMALL_GRADS), len(SMALL_PARAMS)

    def update_body(*refs):
        vec_own, ws_own, vec_slots, ws_slots = refs[:4]
        w_in, m_in, v_in = (dict(zip(SMALL_PARAMS, refs[4 + k * npar:4 + (k + 1) * npar])) for k in range(3))
        o0 = 4 + 3 * npar
        g_out = dict(zip(SMALL_GRADS, refs[o0:o0 + ng]))
        d_out, m_out, v_out = (dict(zip(SMALL_PARAMS, refs[o0 + ng + k * npar:o0 + ng + (k + 1) * npar])) for k in range(3))
        vg, vw, vm, vv = refs[o0 + ng + 3 * npar:]
        me = 4 * lax.axis_index("x") + 2 * lax.axis_index("y") + lax.axis_index("c")

        def device_sum(mine, slots, read):
            acc = None
            for i in range(N_PEERS + 1):
                k = me ^ i
                part = jnp.where(k == 0, read(mine), read(slots.at[jnp.maximum(k, 1) - 1]))
                acc = part if acc is None else acc + part
            return acc

        vg[...] = device_sum(vec_own, vec_slots, lambda ref: ref[...])
        _pack_small(vw, w_in)
        _pack_small(vm, m_in)
        _pack_small(vv, v_in)
        d_vec, m_vec, v_vec = _adamw_math(vg[...], vw[...], vm[...], vv[...])
        vw[...] = d_vec
        vm[...] = m_vec
        vv[...] = v_vec
        for n in VEC_PARAMS + ("loss",):
            g_out[n][...] = vg[VEC_SLOTS[n]]
        for n in VEC_PARAMS:
            d_out[n][...] = vw[VEC_SLOTS[n]]
            m_out[n][...] = vm[VEC_SLOTS[n]]
            v_out[n][...] = vv[VEC_SLOTS[n]]

        def spatial(r0, ck):
            rows = pl.ds(r0, ck)
            g = device_sum(ws_own, ws_slots, lambda ref: ref[rows, :])
            d_, m_, v_ = _adamw_math(g, w_in["w_spatial"][rows, :], m_in["w_spatial"][rows, :], v_in["w_spatial"][rows, :])
            g_out["w_spatial"][rows, :] = g
            d_out["w_spatial"][rows, :] = d_
            m_out["w_spatial"][rows, :] = m_
            v_out["w_spatial"][rows, :] = v_

        _for_row_chunks(W_SPATIAL_2D[0], spatial)

    ins = list(own) + list(landed)
    for src in (w, m, v):
        ins += [src[n].reshape(_small_shape(n)) for n in SMALL_PARAMS]
    out_shapes = [jax.ShapeDtypeStruct(_small_shape(n), F32) for n in SMALL_GRADS + SMALL_PARAMS * 3]
    outs = pl.pallas_call(
        update_body, in_specs=[VMEM_SPEC] * len(ins), out_specs=[VMEM_SPEC] * len(out_shapes), out_shape=out_shapes,
        scratch_shapes=[pltpu.VMEM(VEC_SHAPE, F32)] * 4, name="small_params_update")(*ins)
    grads = dict(zip(SMALL_GRADS, outs[:ng]))
    rest = [dict(zip(SMALL_PARAMS, outs[ng + k * npar:ng + (k + 1) * npar])) for k in range(3)]
    return grads, rest[0], rest[1], rest[2]


BIG = ("w_in", "w_proj_attn", "w_proj_sgu", "w_out", "w_ffn_gate", "w_ffn_up", "w_ffn_down")
COMM_GROUPS = (("w_in",), ("w_proj_attn", "w_proj_sgu", "w_out", "w_ffn_gate", "w_ffn_up", "w_ffn_down"))
WEIGHTS = ("norm1_g", "w_in", "sgu_ln_g", "sgu_ln_b", "w_spatial", "b_spatial", "w_proj_attn", "w_proj_sgu", "w_out",
           "norm2_g", "w_ffn_gate", "w_ffn_up", "w_ffn_down", "final_g")


def kernel(x, positions, norm1_g, w_in, sgu_ln_g, sgu_ln_b, w_spatial, b_spatial, w_proj_attn, w_proj_sgu, w_out, norm2_g, w_ffn_gate, w_ffn_up, w_ffn_down, final_g, loss_target, m_norm1_g, m_w_in, m_sgu_ln_g, m_sgu_ln_b, m_w_spatial, m_b_spatial, m_w_proj_attn, m_w_proj_sgu, m_w_out, m_norm2_g, m_w_ffn_gate, m_w_ffn_up, m_w_ffn_down, m_final_g, v_norm1_g, v_w_in, v_sgu_ln_g, v_sgu_ln_b, v_w_spatial, v_b_spatial, v_w_proj_attn, v_w_proj_sgu, v_w_out, v_norm2_g, v_w_ffn_gate, v_w_ffn_up, v_w_ffn_down, v_final_g):
    w = dict(norm1_g=norm1_g, w_in=w_in, sgu_ln_g=sgu_ln_g, sgu_ln_b=sgu_ln_b, w_spatial=w_spatial, b_spatial=b_spatial,
             w_proj_attn=w_proj_attn, w_proj_sgu=w_proj_sgu, w_out=w_out, norm2_g=norm2_g, w_ffn_gate=w_ffn_gate,
             w_ffn_up=w_ffn_up, w_ffn_down=w_ffn_down, final_g=final_g)
    m = dict(norm1_g=m_norm1_g, w_in=m_w_in, sgu_ln_g=m_sgu_ln_g, sgu_ln_b=m_sgu_ln_b, w_spatial=m_w_spatial,
             b_spatial=m_b_spatial, w_proj_attn=m_w_proj_attn, w_proj_sgu=m_w_proj_sgu, w_out=m_w_out, norm2_g=m_norm2_g,
             w_ffn_gate=m_w_ffn_gate, w_ffn_up=m_w_ffn_up, w_ffn_down=m_w_ffn_down, final_g=m_final_g)
    v = dict(norm1_g=v_norm1_g, w_in=v_w_in, sgu_ln_g=v_sgu_ln_g, sgu_ln_b=v_sgu_ln_b, w_spatial=v_w_spatial,
             b_spatial=v_b_spatial, w_proj_attn=v_w_proj_attn, w_proj_sgu=v_w_proj_sgu, w_out=v_w_out, norm2_g=v_norm2_g,
             w_ffn_gate=v_w_ffn_gate, w_ffn_up=v_w_ffn_up, w_ffn_down=v_w_ffn_down, final_g=v_final_g)
    t = x.shape[1]

    def cast(n, after):
        flip = jnp.transpose if w[n].shape[-1] % LANES else (lambda a: a)
        return flip(_ew(f"cast_{n}", lambda a: (a,), [flip(w[n][0])], [BF16], after)[0])

    shards = {"w_in": cast("w_in", [])}
    late = COMM_GROUPS[1]
    k_in, n_in = shards["w_in"].shape
    *first, token = _split_start("gather_start_0", _gather_half_copies, 3, [shards["w_in"]], [(3, k_in // 2, n_in)])
    shards.update({n: cast(n, [token]) for n in late})
    pending = {}

    def first_weight(after):
        srcs, filled = _split_wait("gather_wait_0", _gather_half_copies, *first, list(after) + [shards[n] for n in late])
        gath_in, late_shards = lax.optimization_barrier(
            (_gather_finish("gather_finish_0", srcs[0], filled[0]), [shards[n] for n in late]))
        land_shapes = [(s.shape[0], N_CHIPS * s.shape[1]) if n.startswith("w_proj") else (N_CHIPS,) + s.shape
                       for n, s in zip(late, late_shards)]
        *pending["late"], _ = _split_start("gather_start_1", _gather_copies, 4, late_shards, land_shapes)
        return gath_in

    def late_weights(after):
        _, filled = _split_wait("gather_wait_1", _gather_copies, *pending["late"], after)
        gath = dict(zip(late, filled))
        return (gath["w_proj_attn"], gath["w_proj_sgu"],
                gath["w_out"].reshape(D_MODEL, D_MODEL), gath["w_ffn_gate"], gath["w_ffn_up"], gath["w_ffn_down"])

    exchanges = {}

    def on_grads(i, partials):
        if "w_out" in partials:
            partials["w_out"] = partials["w_out"].reshape(N_CHIPS, D_MODEL // N_CHIPS, D_MODEL)
        parts = [partials[n] for n in COMM_GROUPS[i]]
        *exchanges[i], started = _split_start(
            f"rs_exchange_start_{i}", _exchange_copies, N_PEERS, parts, [(N_PEERS,) + _piece_shape(p.shape) for p in parts])
        return started

    dx, _, small = _local_step(
        x[0], positions.reshape(t, 1), loss_target[0], norm1_g + token, sgu_ln_g, sgu_ln_b,
        w_spatial[0], b_spatial[0] + token[:1, :LANES],
        norm2_g, final_g.reshape(1, D_MODEL), first_weight, late_weights, on_grads=on_grads)
    *small_started, small_token = _small_start(small)

    grads = {}
    for i in (1, 0):
        parts, filled = _split_wait(f"rs_exchange_wait_{i}", _exchange_copies, *exchanges[i], small_token)
        grads.update(zip(COMM_GROUPS[i], _device_sum(f"rs_device_sum_{i}", parts, filled)))

    delta, new_m, new_v, updated = {}, {}, {}, []
    for n in BIG:
        shp = w[n].shape
        flip = jnp.transpose if shp[-1] % LANES else (lambda a: a)
        outs = _adamw(f"adamw_{n}", flip(grads[n]), flip(w[n][0]), flip(m[n][0]), flip(v[n][0]))
        grads[n], delta[n], new_m[n], new_v[n] = (flip(a).reshape(shp) for a in outs)
        updated.append(outs[-1])

    g_s, d_s, m_s, v_s = _small_finish(small_started, updated, w, m, v)
    loss = g_s["loss"][0, 0]
    for n in SMALL_PARAMS:
        shp = w[n].shape
        grads[n], delta[n], new_m[n], new_v[n] = (a[n].reshape(shp) for a in (g_s, d_s, m_s, v_s))

    return (loss, dx.reshape(x.shape), *[grads[n] for n in WEIGHTS], *[delta[n] for n in WEIGHTS],
            *[new_m[n] for n in WEIGHTS], *[new_v[n] for n in WEIGHTS])
```
